```python
import jax, jax.numpy as jnp
from jax import lax
import numpy as np

D_MODEL = 1024
BATCH = 16
SEQ = 2048
DEPTH = 1

LRU_WIDTH = D_MODEL
LRU_BLOCKS = 16
LRU_BW = LRU_WIDTH // LRU_BLOCKS
CONV_WIDTH = 4
LRU_C = 8.0
RET_HEADS = 4
RET_DK = 256
RET_DV = 256
RET_QK_WIDTH = RET_HEADS * RET_DK
RET_WIDTH = RET_HEADS * RET_DV
CHUNK = 128
ROPE_THETA = 10000.0
EPS = 1e-6
IN_SIZES = (LRU_WIDTH, LRU_WIDTH, RET_QK_WIDTH, RET_QK_WIDTH, RET_WIDTH, RET_WIDTH, D_MODEL, D_MODEL)
IN_COLS = sum(IN_SIZES)
SPLIT_POINTS = tuple(int(c) for c in np.cumsum(IN_SIZES)[:-1])

kernel_name = "hybrid_rglru_retention_gated_block"


def rmsnorm(x, g):
    xf = x.astype(jnp.float32)
    y = xf * lax.rsqrt(jnp.mean(xf * xf, axis=-1, keepdims=True) + EPS)
    return (y * g.astype(jnp.float32)).astype(x.dtype)


def causal_depthwise_conv(x, w, b):
    S = x.shape[1]
    xp = jnp.pad(x, ((0, 0), (CONV_WIDTH - 1, 0), (0, 0)))
    y = b
    for k in range(CONV_WIDTH):
        y = y + xp[:, k:k + S, :] * w[k]
    return y


def block_diag_linear(x, w, b):
    B, S, W = x.shape
    xb = x.reshape(B, S, LRU_BLOCKS, LRU_BW)
    return jnp.einsum('bsnk,nkj->bsnj', xb, w).reshape(B, S, W) + b


def rg_lru(x, wx, bx, wa, ba, lam):
    B, S, W = x.shape
    i_t = jax.nn.sigmoid(block_diag_linear(x, wx, bx))
    r_t = jax.nn.sigmoid(block_diag_linear(x, wa, ba))
    log_a = -LRU_C * r_t.astype(jnp.float32) * jax.nn.softplus(-lam.astype(jnp.float32))
    a = jnp.exp(log_a)
    mult = jnp.sqrt(-jnp.expm1(2.0 * log_a))
    u = mult * (i_t * x).astype(jnp.float32)

    def step(h, au):
        a_t, u_t = au
        h = a_t * h + u_t
        return h, h

    _, hs = lax.scan(step, jnp.zeros((B, W), jnp.float32),
                     (jnp.swapaxes(a, 0, 1), jnp.swapaxes(u, 0, 1)))
    return jnp.swapaxes(hs, 0, 1).astype(x.dtype)


def rotary(x):
    S, D = x.shape[1], x.shape[3]
    half = D // 2
    freqs = ROPE_THETA ** (-jnp.arange(half, dtype=jnp.float32) / half)
    ang = jnp.arange(S, dtype=jnp.float32)[:, None] * freqs[None, :]
    cos = jnp.cos(ang)[None, :, None, :]
    sin = jnp.sin(ang)[None, :, None, :]
    x1, x2 = x[..., :half], x[..., half:]
    return jnp.concatenate([x1 * cos - x2 * sin, x1 * sin + x2 * cos], axis=-1)


def retention_chunkwise(q, k, v):
    B, S, H, DK = q.shape
    DV = v.shape[-1]
    NC = S // CHUNK
    log_g = jnp.log1p(-(2.0 ** (-5.0 - jnp.arange(H, dtype=jnp.float32))))
    idx = jnp.arange(CHUNK, dtype=jnp.float32)
    diff = idx[:, None] - idx[None, :]
    inner_decay = jnp.where(diff >= 0, jnp.exp(jnp.maximum(diff, 0.0)[None] * log_g[:, None, None]), 0.0)
    cross_decay = jnp.exp((idx[:, None] + 1.0) * log_g[None, :])[None, :, :, None]
    state_decay = jnp.exp((CHUNK - 1.0 - idx[:, None]) * log_g[None, :])[None, :, :, None]
    chunk_decay = jnp.exp(CHUNK * log_g)[None, :, None, None]

    def to_chunks(t):
        return jnp.swapaxes(t.reshape(B, NC, CHUNK, H, t.shape[-1]), 0, 1)

    def step(R, qkv):
        qc, kc, vc = qkv
        scores = jnp.einsum('bihd,bjhd->bhij', qc, kc) * inner_decay
        inner = jnp.einsum('bhij,bjhe->bihe', scores, vc)
        cross = jnp.einsum('bihd,bhde->bihe', qc, R) * cross_decay
        R_new = chunk_decay * R + jnp.einsum('bjhd,bjhe->bhde', kc, vc * state_decay)
        return R_new, inner + cross

    R0 = jnp.zeros((B, H, DK, DV), jnp.float32)
    _, out = lax.scan(step, R0, (to_chunks(q), to_chunks(k), to_chunks(v)))
    return jnp.swapaxes(out, 0, 1).reshape(B, S, H, DV)


def head_groupnorm(y, g):
    mu = jnp.mean(y, axis=-1, keepdims=True)
    yc = y - mu
    var = jnp.mean(yc * yc, axis=-1, keepdims=True)
    return yc * lax.rsqrt(var + EPS) * g.astype(jnp.float32)


def _fwd_setup_inputs(seed: int = 0) -> dict:
    key = jax.random.key(seed)
    ks = jax.random.split(key, 16)
    L = DEPTH
    f32 = jnp.float32
    nrm = lambda k, shape, fan: jax.random.normal(k, shape, f32) * (fan ** -0.5)
    a0 = jax.random.uniform(ks[10], (L, LRU_WIDTH), f32, 0.9, 0.999)
    return {
        "x": jax.random.normal(ks[0], (BATCH, SEQ, D_MODEL), f32),
        "norm_in": 1.0 + 0.02 * jax.random.normal(ks[1], (L, D_MODEL), f32),
        "w_in": nrm(ks[2], (L, D_MODEL, IN_COLS), D_MODEL),
        "conv_w": nrm(ks[3], (L, CONV_WIDTH, LRU_WIDTH), CONV_WIDTH),
        "conv_b": 0.02 * jax.random.normal(ks[4], (L, LRU_WIDTH), f32),
        "gate_x_w": nrm(ks[5], (L, LRU_BLOCKS, LRU_BW, LRU_BW), LRU_BW),
        "gate_x_b": 0.02 * jax.random.normal(ks[6], (L, LRU_WIDTH), f32),
        "gate_a_w": nrm(ks[7], (L, LRU_BLOCKS, LRU_BW, LRU_BW), LRU_BW),
        "gate_a_b": 0.02 * jax.random.normal(ks[8], (L, LRU_WIDTH), f32),
        "lru_lambda": jnp.log(a0 / (1.0 - a0)),
        "gn_gain": 1.0 + 0.02 * jax.random.normal(ks[9], (L, RET_HEADS, RET_DV), f32),
        "w_proj_a": nrm(ks[11], (L, LRU_WIDTH, D_MODEL), LRU_WIDTH),
        "w_proj_b": nrm(ks[12], (L, RET_WIDTH, D_MODEL), RET_WIDTH),
        "w_out": nrm(ks[13], (L, D_MODEL, D_MODEL), D_MODEL),
        "norm_final": 1.0 + 0.02 * jax.random.normal(ks[14], (D_MODEL,), f32),
    }


def _fwd_reference(x, norm_in, w_in, conv_w, conv_b, gate_x_w, gate_x_b, gate_a_w, gate_a_b,
              lru_lambda, gn_gain, w_proj_a, w_proj_b, w_out, norm_final):
    B, S, _ = x.shape
    for l in range(DEPTH):
        h = rmsnorm(x, norm_in[l])
        proj = jnp.einsum('bsd,dc->bsc', h, w_in[l])
        xa, ga, q, k, v, gb, ma, mb = jnp.split(proj, SPLIT_POINTS, axis=-1)

        xa = causal_depthwise_conv(xa, conv_w[l], conv_b[l])
        ya = rg_lru(xa, gate_x_w[l], gate_x_b[l], gate_a_w[l], gate_a_b[l], lru_lambda[l])
        ya = jax.nn.silu(ga) * ya
        out_a = jnp.einsum('bsw,wd->bsd', ya, w_proj_a[l])

        qh = rotary(q.reshape(B, S, RET_HEADS, RET_DK).astype(jnp.float32))
        kh = rotary(k.reshape(B, S, RET_HEADS, RET_DK).astype(jnp.float32)) * (RET_DK ** -0.5)
        vh = v.reshape(B, S, RET_HEADS, RET_DV).astype(jnp.float32)
        ret = head_groupnorm(retention_chunkwise(qh, kh, vh), gn_gain[l])
        yb = jax.nn.silu(gb) * ret.reshape(B, S, RET_WIDTH).astype(x.dtype)
        out_b = jnp.einsum('bsw,wd->bsd', yb, w_proj_b[l])

        merged = jax.nn.sigmoid(ma) * out_a + jax.nn.sigmoid(mb) * out_b
        x = x + jnp.einsum('bsd,de->bse', merged, w_out[l])
    return rmsnorm(x, norm_final)


import jax as _jax
import jax.numpy as _jnp

TWIN_FORMAT = 'train_step'
FWD_PARAMS = ['x', 'norm_in', 'w_in', 'conv_w', 'conv_b', 'gate_x_w', 'gate_x_b', 'gate_a_w', 'gate_a_b', 'lru_lambda', 'gn_gain', 'w_proj_a', 'w_proj_b', 'w_out', 'norm_final']
TWIN_WEIGHTS = ['norm_in', 'w_in', 'conv_w', 'conv_b', 'gate_x_w', 'gate_x_b', 'gate_a_w', 'gate_a_b', 'lru_lambda', 'gn_gain', 'w_proj_a', 'w_proj_b', 'w_out', 'norm_final']
TWIN_DIFF_INPUT = 'x'
TWIN_INPUTS = ['x', 'norm_in', 'w_in', 'conv_w', 'conv_b', 'gate_x_w', 'gate_x_b', 'gate_a_w', 'gate_a_b', 'lru_lambda', 'gn_gain', 'w_proj_a', 'w_proj_b', 'w_out', 'norm_final', 'loss_target', 'm_norm_in', 'm_w_in', 'm_conv_w', 'm_conv_b', 'm_gate_x_w', 'm_gate_x_b', 'm_gate_a_w', 'm_gate_a_b', 'm_lru_lambda', 'm_gn_gain', 'm_w_proj_a', 'm_w_proj_b', 'm_w_out', 'm_norm_final', 'v_norm_in', 'v_w_in', 'v_conv_w', 'v_conv_b', 'v_gate_x_w', 'v_gate_x_b', 'v_gate_a_w', 'v_gate_a_b', 'v_lru_lambda', 'v_gn_gain', 'v_w_proj_a', 'v_w_proj_b', 'v_w_out', 'v_norm_final']
TWIN_OUTPUTS = ['loss', 'grad_x', 'grad_norm_in', 'grad_w_in', 'grad_conv_w', 'grad_conv_b', 'grad_gate_x_w', 'grad_gate_x_b', 'grad_gate_a_w', 'grad_gate_a_b', 'grad_lru_lambda', 'grad_gn_gain', 'grad_w_proj_a', 'grad_w_proj_b', 'grad_w_out', 'grad_norm_final', 'delta_norm_in', 'delta_w_in', 'delta_conv_w', 'delta_conv_b', 'delta_gate_x_w', 'delta_gate_x_b', 'delta_gate_a_w', 'delta_gate_a_b', 'delta_lru_lambda', 'delta_gn_gain', 'delta_w_proj_a', 'delta_w_proj_b', 'delta_w_out', 'delta_norm_final', 'new_m_norm_in', 'new_m_w_in', 'new_m_conv_w', 'new_m_conv_b', 'new_m_gate_x_w', 'new_m_gate_x_b', 'new_m_gate_a_w', 'new_m_gate_a_b', 'new_m_lru_lambda', 'new_m_gn_gain', 'new_m_w_proj_a', 'new_m_w_proj_b', 'new_m_w_out', 'new_m_norm_final', 'new_v_norm_in', 'new_v_w_in', 'new_v_conv_w', 'new_v_conv_b', 'new_v_gate_x_w', 'new_v_gate_x_b', 'new_v_gate_a_w', 'new_v_gate_a_b', 'new_v_lru_lambda', 'new_v_gn_gain', 'new_v_w_proj_a', 'new_v_w_proj_b', 'new_v_w_out', 'new_v_norm_final']
TWIN_LEAF_KINDS = {'loss': 'loss', 'grad_x': 'grad_x', 'grad_norm_in': 'grad_w', 'grad_w_in': 'grad_w', 'grad_conv_w': 'grad_w', 'grad_conv_b': 'grad_w', 'grad_gate_x_w': 'grad_w', 'grad_gate_x_b': 'grad_w', 'grad_gate_a_w': 'grad_w', 'grad_gate_a_b': 'grad_w', 'grad_lru_lambda': 'grad_w', 'grad_gn_gain': 'grad_w', 'grad_w_proj_a': 'grad_w', 'grad_w_proj_b': 'grad_w', 'grad_w_out': 'grad_w', 'grad_norm_final': 'grad_w', 'delta_norm_in': 'delta_w', 'delta_w_in': 'delta_w', 'delta_conv_w': 'delta_w', 'delta_conv_b': 'delta_w', 'delta_gate_x_w': 'delta_w', 'delta_gate_x_b': 'delta_w', 'delta_gate_a_w': 'delta_w', 'delta_gate_a_b': 'delta_w', 'delta_lru_lambda': 'delta_w', 'delta_gn_gain': 'delta_w', 'delta_w_proj_a': 'delta_w', 'delta_w_proj_b': 'delta_w', 'delta_w_out': 'delta_w', 'delta_norm_final': 'delta_w', 'new_m_norm_in': 'new_m', 'new_m_w_in': 'new_m', 'new_m_conv_w': 'new_m', 'new_m_conv_b': 'new_m', 'new_m_gate_x_w': 'new_m', 'new_m_gate_x_b': 'new_m', 'new_m_gate_a_w': 'new_m', 'new_m_gate_a_b': 'new_m', 'new_m_lru_lambda': 'new_m', 'new_m_gn_gain': 'new_m', 'new_m_w_proj_a': 'new_m', 'new_m_w_proj_b': 'new_m', 'new_m_w_out': 'new_m', 'new_m_norm_final': 'new_m', 'new_v_norm_in': 'new_v', 'new_v_w_in': 'new_v', 'new_v_conv_w': 'new_v', 'new_v_conv_b': 'new_v', 'new_v_gate_x_w': 'new_v', 'new_v_gate_x_b': 'new_v', 'new_v_gate_a_w': 'new_v', 'new_v_gate_a_b': 'new_v', 'new_v_lru_lambda': 'new_v', 'new_v_gn_gain': 'new_v', 'new_v_w_proj_a': 'new_v', 'new_v_w_proj_b': 'new_v', 'new_v_w_out': 'new_v', 'new_v_norm_final': 'new_v'}


def _forward(args):
    return _fwd_reference(*[args[k] for k in FWD_PARAMS])


def _output_shape():
    out = _jax.eval_shape(lambda: _forward(_fwd_setup_inputs(0)))
    return out.shape, out.dtype

N_MICROBATCH = 1
ADAM_LR = 0.001
ADAM_B1 = 0.9
ADAM_B2 = 0.999
ADAM_EPS = 1e-08
ADAM_WD = 0.01
ADAM_STEP = 10
PER_EXAMPLE_BATCH_AXIS = {'x': 0, 'loss_target': 0}
SHARED_INPUTS = []
_WEIGHT_DTYPES = {'norm_in': _jnp.float32, 'w_in': _jnp.float32, 'conv_w': _jnp.float32, 'conv_b': _jnp.float32, 'gate_x_w': _jnp.float32, 'gate_x_b': _jnp.float32, 'gate_a_w': _jnp.float32, 'gate_a_b': _jnp.float32, 'lru_lambda': _jnp.float32, 'gn_gain': _jnp.float32, 'w_proj_a': _jnp.float32, 'w_proj_b': _jnp.float32, 'w_out': _jnp.float32, 'norm_final': _jnp.float32}
MOMENT_SCALE = {'norm_in': 1.239018e-01, 'w_in': 4.234899e-02, 'conv_w': 3.171335e-02, 'conv_b': 1.407985e-01, 'gate_x_w': 1.557286e-02, 'gate_x_b': 1.136721e-02, 'gate_a_w': 8.840006e-03, 'gate_a_b': 7.477382e-03, 'lru_lambda': 1.416075e-02, 'gn_gain': 5.368489e-02, 'w_proj_a': 3.073906e-02, 'w_proj_b': 5.390613e-02, 'w_out': 6.167857e-02, 'norm_final': 3.202828e+01}


def _to_microbatches(a, axis):
    t = _jnp.moveaxis(a, axis, 0)
    t = t.reshape((N_MICROBATCH, t.shape[0] // N_MICROBATCH) + t.shape[1:])
    return _jnp.moveaxis(t, 1, axis + 1)


def setup_inputs(seed: int = 0) -> dict:
    inp = _fwd_setup_inputs(seed)
    key = _jax.random.fold_in(_jax.random.key(seed), 7919)
    shape, _ = _output_shape()
    out = dict(inp)
    out["loss_target"] = _jax.random.normal(_jax.random.fold_in(key, 0), shape, _jnp.float32)
    for i, name in enumerate(TWIN_WEIGHTS):
        w = inp[name].astype(_jnp.float32)
        if MOMENT_SCALE is None:
            s = _jnp.sqrt(_jnp.mean(_jnp.square(w)) + 1e-30)
        else:
            s = MOMENT_SCALE[name]
        km, kv = _jax.random.split(_jax.random.fold_in(key, i + 1))
        out[name] = w
        out["m_" + name] = s * _jax.random.normal(km, w.shape, _jnp.float32)
        out["v_" + name] = (s * s) * _jax.random.uniform(kv, w.shape, _jnp.float32, 0.5, 1.5)
    if N_MICROBATCH > 1:
        for name, axis in PER_EXAMPLE_BATCH_AXIS.items():
            out[name] = _to_microbatches(out[name], axis)
    return {'x': out['x'], 'norm_in': out['norm_in'], 'w_in': out['w_in'], 'conv_w': out['conv_w'], 'conv_b': out['conv_b'], 'gate_x_w': out['gate_x_w'], 'gate_x_b': out['gate_x_b'], 'gate_a_w': out['gate_a_w'], 'gate_a_b': out['gate_a_b'], 'lru_lambda': out['lru_lambda'], 'gn_gain': out['gn_gain'], 'w_proj_a': out['w_proj_a'], 'w_proj_b': out['w_proj_b'], 'w_out': out['w_out'], 'norm_final': out['norm_final'], 'loss_target': out['loss_target'], 'm_norm_in': out['m_norm_in'], 'm_w_in': out['m_w_in'], 'm_conv_w': out['m_conv_w'], 'm_conv_b': out['m_conv_b'], 'm_gate_x_w': out['m_gate_x_w'], 'm_gate_x_b': out['m_gate_x_b'], 'm_gate_a_w': out['m_gate_a_w'], 'm_gate_a_b': out['m_gate_a_b'], 'm_lru_lambda': out['m_lru_lambda'], 'm_gn_gain': out['m_gn_gain'], 'm_w_proj_a': out['m_w_proj_a'], 'm_w_proj_b': out['m_w_proj_b'], 'm_w_out': out['m_w_out'], 'm_norm_final': out['m_norm_final'], 'v_norm_in': out['v_norm_in'], 'v_w_in': out['v_w_in'], 'v_conv_w': out['v_conv_w'], 'v_conv_b': out['v_conv_b'], 'v_gate_x_w': out['v_gate_x_w'], 'v_gate_x_b': out['v_gate_x_b'], 'v_gate_a_w': out['v_gate_a_w'], 'v_gate_a_b': out['v_gate_a_b'], 'v_lru_lambda': out['v_lru_lambda'], 'v_gn_gain': out['v_gn_gain'], 'v_w_proj_a': out['v_w_proj_a'], 'v_w_proj_b': out['v_w_proj_b'], 'v_w_out': out['v_w_out'], 'v_norm_final': out['v_norm_final']}


def _loss(weights, diff, rest, loss_target):
    with _jax.named_scope("forward"):
        args = {**rest, TWIN_DIFF_INPUT: diff, **{k: w.astype(_WEIGHT_DTYPES[k]) for k, w in weights.items()}}
        y = _forward(args)
    with _jax.named_scope("loss_head"):
        err = _jnp.square(y.astype(_jnp.float32) - loss_target)
        return 0.5 * _jnp.sum(_jnp.mean(err, axis=-1)) if err.ndim else 0.5 * err


def _adamw(w, g, m, v):
    m = ADAM_B1 * m + (1.0 - ADAM_B1) * g
    v = ADAM_B2 * v + (1.0 - ADAM_B2) * _jnp.square(g)
    m_hat = m / (1.0 - ADAM_B1 ** ADAM_STEP)
    v_hat = v / (1.0 - ADAM_B2 ** ADAM_STEP)
    delta = -ADAM_LR * (m_hat / (_jnp.sqrt(v_hat) + ADAM_EPS) + ADAM_WD * w)
    return delta, m, v


def reference(x, norm_in, w_in, conv_w, conv_b, gate_x_w, gate_x_b, gate_a_w, gate_a_b, lru_lambda, gn_gain, w_proj_a, w_proj_b, w_out, norm_final, loss_target, m_norm_in, m_w_in, m_conv_w, m_conv_b, m_gate_x_w, m_gate_x_b, m_gate_a_w, m_gate_a_b, m_lru_lambda, m_gn_gain, m_w_proj_a, m_w_proj_b, m_w_out, m_norm_final, v_norm_in, v_w_in, v_conv_w, v_conv_b, v_gate_x_w, v_gate_x_b, v_gate_a_w, v_gate_a_b, v_lru_lambda, v_gn_gain, v_w_proj_a, v_w_proj_b, v_w_out, v_norm_final):
    given = dict(x=x, norm_in=norm_in, w_in=w_in, conv_w=conv_w, conv_b=conv_b, gate_x_w=gate_x_w, gate_x_b=gate_x_b, gate_a_w=gate_a_w, gate_a_b=gate_a_b, lru_lambda=lru_lambda, gn_gain=gn_gain, w_proj_a=w_proj_a, w_proj_b=w_proj_b, w_out=w_out, norm_final=norm_final, loss_target=loss_target, m_norm_in=m_norm_in, m_w_in=m_w_in, m_conv_w=m_conv_w, m_conv_b=m_conv_b, m_gate_x_w=m_gate_x_w, m_gate_x_b=m_gate_x_b, m_gate_a_w=m_gate_a_w, m_gate_a_b=m_gate_a_b, m_lru_lambda=m_lru_lambda, m_gn_gain=m_gn_gain, m_w_proj_a=m_w_proj_a, m_w_proj_b=m_w_proj_b, m_w_out=m_w_out, m_norm_final=m_norm_final, v_norm_in=v_norm_in, v_w_in=v_w_in, v_conv_w=v_conv_w, v_conv_b=v_conv_b, v_gate_x_w=v_gate_x_w, v_gate_x_b=v_gate_x_b, v_gate_a_w=v_gate_a_w, v_gate_a_b=v_gate_a_b, v_lru_lambda=v_lru_lambda, v_gn_gain=v_gn_gain, v_w_proj_a=v_w_proj_a, v_w_proj_b=v_w_proj_b, v_w_out=v_w_out, v_norm_final=v_norm_final)
    weights = {n: given[n] for n in TWIN_WEIGHTS}
    shared = {n: given[n] for n in SHARED_INPUTS}
    per_example = {n: given[n] for n in ['x']}
    grad_fn = _jax.value_and_grad(_loss, argnums=(0, 1))

    def one_microbatch(ex, loss_target):
        ex = dict(ex)
        diff = ex.pop(TWIN_DIFF_INPUT)
        return grad_fn(weights, diff, {**shared, **ex}, loss_target)

    if N_MICROBATCH == 1:
        loss, (grad_w, grad_x) = one_microbatch(per_example, given["loss_target"])
    else:
        def body(carry, xs):
            loss_sum, grad_sum = carry
            l_k, (gw_k, gx_k) = one_microbatch(xs[0], xs[1])
            with _jax.named_scope("update"):
                return (loss_sum + l_k, _jax.tree.map(_jnp.add, grad_sum, gw_k)), gx_k

        init = (_jnp.zeros((), _jnp.float32), _jax.tree.map(_jnp.zeros_like, weights))
        (loss, grad_w), grad_x = _jax.lax.scan(body, init, (per_example, given["loss_target"]))
    with _jax.named_scope("update"):
        delta_w, new_m, new_v = {}, {}, {}
        for n in TWIN_WEIGHTS:
            delta_w[n], new_m[n], new_v[n] = _adamw(weights[n], grad_w[n], given["m_" + n], given["v_" + n])
    return (loss, grad_x, *[grad_w[n] for n in TWIN_WEIGHTS], *[delta_w[n] for n in TWIN_WEIGHTS],
            *[new_m[n] for n in TWIN_WEIGHTS], *[new_v[n] for n in TWIN_WEIGHTS])
```

```python
import functools

import jax
import jax.numpy as jnp
from jax import lax
from jax.experimental import pallas as pl
from jax.experimental.pallas import tpu as pltpu

F32 = jnp.float32
BF16 = jnp.bfloat16
MESH = pl.DeviceIdType.MESH
AXES = ("x", "y", "c")

D = 1024
S = 2048
NSEG = 8
NDEV = 8
HEADS = 4
DK = 256
CH = 256
NCH = S // CH
CB = 256
NCB = D // CB
RC = 128
EPS = 1e-6
LRU_C = 8.0
WROWS = D + 3 * (D // NDEV)
SMALL_ROWS = 144
VMEM_LIMIT = 56 * 1024 * 1024

ADAM_LR = 0.001
ADAM_B1 = 0.9
ADAM_B2 = 0.999
ADAM_EPS = 1e-08
ADAM_WD = 0.01
ADAM_STEP = 10


def _params(sem=None):
    return pltpu.CompilerParams(dimension_semantics=sem, vmem_limit_bytes=VMEM_LIMIT)


def _dot(a, b):
    return jnp.dot(a, b, preferred_element_type=F32)


def _dot_nt(a, b):
    return lax.dot_general(a, b, (((1,), (1,)), ((), ())), preferred_element_type=F32)


def _dot_tn(a, b):
    return lax.dot_general(a, b, (((0,), (0,)), ((), ())), preferred_element_type=F32)


def _sigmoid(x):
    return jax.nn.sigmoid(x)


def _expm1_nonpos(x):
    poly = x * (1.0 + x * (0.5 + x * (1.0 / 6.0 + x * (1.0 / 24.0))))
    return jnp.where(x > -0.05, poly, jnp.exp(x) - 1.0)


def _softplus(x):
    return jnp.maximum(x, 0.0) + jnp.log(1.0 + jnp.exp(-jnp.abs(x)))


def _rows(c, n):
    return pl.ds(pl.multiple_of(c * n, n), n)


def _window_before(ref, c, n):
    r0 = c * n
    prev = ref[pl.ds(pl.multiple_of(jnp.maximum(r0 - 8, 0), 8), 8), :]
    prev = jnp.where(c > 0, prev, 0.0)
    return jnp.concatenate([prev, ref[_rows(c, n), :]], axis=0)


def _shift_down(win, s, n):
    if s == 0:
        return win[8:, :]
    return pltpu.roll(win, s, 0)[8:, :]


def _shift_up(win, s, n):
    if s == 0:
        return win[:n, :]
    return pltpu.roll(win, n + 8 - s, 0)[:n, :]


def _inproj(x2d, g_in, wg):
    t = x2d.shape[0]
    tm = 1024

    def body(x_ref, g_ref, w_ref, proj_ref, h_ref):
        @pl.when(pl.program_id(1) == 0)
        def _():
            x = x_ref[...]
            r = lax.rsqrt(jnp.mean(x * x, axis=-1, keepdims=True) + EPS)
            h_ref[...] = (x * r * g_ref[...]).astype(BF16)

        proj_ref[...] = _dot(h_ref[...], w_ref[...])

    return pl.pallas_call(
        body, name="inproj", grid=(t // tm, NSEG),
        in_specs=[pl.BlockSpec((tm, D), lambda i, j: (i, 0)),
                  pl.BlockSpec((1, D), lambda i, j: (0, 0)),
                  pl.BlockSpec((None, D, D), lambda i, j: (j, 0, 0))],
        out_specs=[pl.BlockSpec((None, tm, D), lambda i, j: (j, i, 0)),
                   pl.BlockSpec((tm, D), lambda i, j: (i, 0))],
        out_shape=[jax.ShapeDtypeStruct((NSEG, t, D), F32), jax.ShapeDtypeStruct((t, D), BF16)],
        compiler_params=_params(("arbitrary", "arbitrary")),
    )(x2d, g_in, wg)


def _tile_scan(a, u):
    row = lax.broadcasted_iota(jnp.int32, a.shape, 0)
    for d in (1, 2, 4):
        m = row >= d
        a_sh = pltpu.roll(a, d, 0)
        u_sh = pltpu.roll(u, d, 0)
        u = jnp.where(m, a * u_sh + u, u)
        a = jnp.where(m, a * a_sh, a)
    return a, u


def _tile_scan_rev(a, w):
    row = lax.broadcasted_iota(jnp.int32, a.shape, 0)
    for d in (1, 2, 4):
        m = row < 8 - d
        a_sh = pltpu.roll(a, 8 - d, 0)
        w_sh = pltpu.roll(w, 8 - d, 0)
        w = jnp.where(m, a * w_sh + w, w)
        a = jnp.where(m, a * a_sh, a)
    return a, w


def _lru_gates(xa_ref, c, cw_ref, cb_ref, wbd_ref, bx_ref, ba_ref, sp):
    win = _window_before(xa_ref, c, RC)
    xc = cb_ref[...] + cw_ref[3:4, :] * _shift_down(win, 0, RC)
    for s in (1, 2, 3):
        xc = xc + cw_ref[3 - s:4 - s, :] * _shift_down(win, s, RC)
    z = _dot(xc.astype(BF16), wbd_ref[...])
    gi = _sigmoid(z[:, :CB] + bx_ref[...])
    gr = _sigmoid(z[:, CB:] + ba_ref[...])
    log_a = -LRU_C * gr * sp
    return win, xc, gi, gr, log_a


def _lru_fwd(proj, conv_w, conv_b, wbd, bx, ba, lam, nb):
    t = nb * S

    def body(xa_ref, ga_ref, cw_ref, cb_ref, wbd_ref, bx_ref, ba_ref, lam_ref, ya_ref, hs_ref, a_s, u_s):
        sp = _softplus(-lam_ref[...])

        def gates(c, carry):
            _, xc, gi, _, log_a = _lru_gates(xa_ref, c, cw_ref, cb_ref, wbd_ref, bx_ref, ba_ref, sp)
            a_s[_rows(c, RC), :] = jnp.exp(log_a)
            u_s[_rows(c, RC), :] = jnp.sqrt(-_expm1_nonpos(2.0 * log_a)) * (gi * xc)
            return carry

        lax.fori_loop(0, S // RC, gates, 0)

        def scan(k, h):
            a_cum, u_cum = _tile_scan(a_s[_rows(k, 8), :], u_s[_rows(k, 8), :])
            h_tile = u_cum + a_cum * h
            hs_ref[_rows(k, 8), :] = h_tile
            return h_tile[7:8, :]

        lax.fori_loop(0, S // 8, scan, jnp.zeros((1, CB), F32), unroll=4)

        def gate_out(c, carry):
            ga = ga_ref[_rows(c, RC), :]
            ya_ref[_rows(c, RC), :] = (ga * _sigmoid(ga) * hs_ref[_rows(c, RC), :]).astype(BF16)
            return carry

        lax.fori_loop(0, S // RC, gate_out, 0)

    vec = pl.BlockSpec((1, CB), lambda b, cb: (0, cb))
    return pl.pallas_call(
        body, name="lru_fwd", grid=(nb, NCB),
        in_specs=[pl.BlockSpec((None, S, CB), lambda b, cb: (0, b, cb)),
                  pl.BlockSpec((None, S, CB), lambda b, cb: (1, b, cb)),
                  pl.BlockSpec((4, CB), lambda b, cb: (0, cb)),
                  vec,
                  pl.BlockSpec((None, CB, 2 * CB), lambda b, cb: (cb, 0, 0)),
                  vec, vec, vec],
        out_specs=[pl.BlockSpec((S, CB), lambda b, cb: (b, cb)),
                   pl.BlockSpec((S, CB), lambda b, cb: (b, cb))],
        out_shape=[jax.ShapeDtypeStruct((t, D), BF16), jax.ShapeDtypeStruct((t, D), F32)],
        scratch_shapes=[pltpu.VMEM((S, CB), F32), pltpu.VMEM((S, CB), F32)],
        compiler_params=_params(("arbitrary", "arbitrary")),
    )(proj, proj, conv_w, conv_b, wbd, bx, ba, lam)


def _lru_bwd(proj, hs, dya, conv_w, conv_b, wbd, bx, ba, lam, nb):
    t = nb * S

    def body(xa_ref, ga_ref, hs_ref, dya_ref, cw_ref, cb_ref, wbd_ref, bx_ref, ba_ref, lam_ref,
             dp_ref, dwbd_ref, vec_ref, a_s, xc_s, gi_s, gr_s, dh_s, dxc_s, acc_s):
        b = pl.program_id(1)
        lam_v = lam_ref[...]
        sp = _softplus(-lam_v)
        acc_s[...] = jnp.zeros_like(acc_s)

        @pl.when(b == 0)
        def _():
            dwbd_ref[...] = jnp.zeros_like(dwbd_ref)
            vec_ref[...] = jnp.zeros_like(vec_ref)

        def gates(c, carry):
            _, xc, gi, gr, log_a = _lru_gates(xa_ref, c, cw_ref, cb_ref, wbd_ref, bx_ref, ba_ref, sp)
            rows = _rows(c, RC)
            a_s[rows, :] = jnp.exp(log_a)
            xc_s[rows, :] = xc
            gi_s[rows, :] = gi
            gr_s[rows, :] = gr
            ga = ga_ref[rows, :]
            sg = _sigmoid(ga)
            dya_c = dya_ref[rows, :]
            dh_s[rows, :] = dya_c * (ga * sg)
            dp_ref[1, rows, :] = (dya_c * hs_ref[rows, :] * (sg * (1.0 + ga * (1.0 - sg)))).astype(BF16)
            return carry

        lax.fori_loop(0, S // RC, gates, 0)

        def scan(i, g_in):
            k = S // 8 - 1 - i
            a = a_s[_rows(k, 8), :]
            dl = dh_s[_rows(k, 8), :]
            a_cum, g_loc = _tile_scan_rev(a, a * dl)
            g = g_loc + a_cum * g_in
            row = lax.broadcasted_iota(jnp.int32, a.shape, 0)
            dh_s[_rows(k, 8), :] = dl + jnp.where(row < 7, pltpu.roll(g, 7, 0), g_in)
            return g[0:1, :]

        lax.fori_loop(0, S // 8, scan, jnp.zeros((1, CB), F32), unroll=4)

        dxc_s[pl.ds(S, 8), :] = jnp.zeros((8, CB), F32)

        def grads(c, carry):
            rows = _rows(c, RC)
            dh = dh_s[rows, :]
            h_prev = _shift_down(_window_before(hs_ref, c, RC), 1, RC)
            xc, gi, gr, a = xc_s[rows, :], gi_s[rows, :], gr_s[rows, :], a_s[rows, :]
            mult = jnp.sqrt(-_expm1_nonpos(-2.0 * LRU_C * gr * sp))
            dmult = dh * gi * xc
            d_log_a = dh * h_prev * a - dmult * (a * a) / mult
            dzi = dh * mult * xc * gi * (1.0 - gi)
            dzr = d_log_a * (-LRU_C * sp) * gr * (1.0 - gr)
            dz = jnp.concatenate([dzi, dzr], axis=1).astype(BF16)
            dxc_s[rows, :] = dh * mult * gi + _dot_nt(dz, wbd_ref[...])
            dwbd_ref[...] += _dot_tn(xc.astype(BF16), dz)
            acc_s[1:2, :] += jnp.sum(dzi, axis=0, keepdims=True)
            acc_s[2:3, :] += jnp.sum(dzr, axis=0, keepdims=True)
            acc_s[3:4, :] += jnp.sum(d_log_a * (-LRU_C * gr), axis=0, keepdims=True)
            return carry

        lax.fori_loop(0, S // RC, grads, 0)

        def conv_bwd(c, carry):
            rows = _rows(c, RC)
            dwin = dxc_s[pl.ds(pl.multiple_of(c * RC, RC), RC + 8), :]
            dxc = dwin[:RC, :]
            xwin = _window_before(xa_ref, c, RC)
            dxa = cw_ref[3:4, :] * dxc
            acc_s[0:1, :] += jnp.sum(dxc, axis=0, keepdims=True)
            acc_s[7:8, :] += jnp.sum(dxc * _shift_down(xwin, 0, RC), axis=0, keepdims=True)
            for s in (1, 2, 3):
                dxa = dxa + cw_ref[3 - s:4 - s, :] * _shift_up(dwin, s, RC)
                acc_s[7 - s:8 - s, :] += jnp.sum(dxc * _shift_down(xwin, s, RC), axis=0, keepdims=True)
            dp_ref[0, rows, :] = dxa.astype(BF16)
            return carry

        lax.fori_loop(0, S // RC, conv_bwd, 0)

        row = lax.broadcasted_iota(jnp.int32, acc_s.shape, 0)
        vec_ref[...] += jnp.where(row == 3, acc_s[...] * (-_sigmoid(-lam_v)), acc_s[...])

    vec = pl.BlockSpec((1, CB), lambda cb, b: (0, cb))
    blk = pl.BlockSpec((S, CB), lambda cb, b: (b, cb))
    return pl.pallas_call(
        body, name="lru_bwd", grid=(NCB, nb),
        in_specs=[pl.BlockSpec((None, S, CB), lambda cb, b: (0, b, cb)),
                  pl.BlockSpec((None, S, CB), lambda cb, b: (1, b, cb)),
                  blk, blk,
                  pl.BlockSpec((4, CB), lambda cb, b: (0, cb)),
                  vec,
                  pl.BlockSpec((None, CB, 2 * CB), lambda cb, b: (cb, 0, 0)),
                  vec, vec, vec],
        out_specs=[pl.BlockSpec((2, S, CB), lambda cb, b: (0, b, cb)),
                   pl.BlockSpec((None, CB, 2 * CB), lambda cb, b: (cb, 0, 0)),
                   pl.BlockSpec((8, CB), lambda cb, b: (0, cb))],
        out_shape=[jax.ShapeDtypeStruct((2, t, D), BF16),
                   jax.ShapeDtypeStruct((NCB, CB, 2 * CB), F32),
                   jax.ShapeDtypeStruct((8, D), F32)],
        scratch_shapes=[pltpu.VMEM((S, CB), F32), pltpu.VMEM((S, CB), F32), pltpu.VMEM((S, CB), F32),
                        pltpu.VMEM((S, CB), F32), pltpu.VMEM((S, CB), F32), pltpu.VMEM((S + 8, CB), F32),
                        pltpu.VMEM((8, CB), F32)],
        compiler_params=_params(("arbitrary", "arbitrary")),
    )(proj, proj, hs, dya, conv_w, conv_b, wbd, bx, ba, lam)


def _retention_tables():
    log_g = jnp.log1p(-(2.0 ** (-5.0 - jnp.arange(HEADS, dtype=F32))))
    idx = jnp.arange(CH, dtype=F32)
    diff = idx[:, None] - idx[None, :]
    inner = jnp.where(diff >= 0, jnp.exp(jnp.maximum(diff, 0.0)[None] * log_g[:, None, None]), 0.0)
    cross = jnp.exp((idx[None, :] + 1.0) * log_g[:, None])
    state = jnp.exp((CH - 1.0 - idx[None, :]) * log_g[:, None])
    cross = jnp.broadcast_to(cross[:, :, None], (HEADS, CH, DK))
    state = jnp.broadcast_to(state[:, :, None], (HEADS, CH, DK))
    half = DK // 2
    freqs = 10000.0 ** (-jnp.arange(half, dtype=F32) / half)
    ang = jnp.arange(S, dtype=F32)[:, None] * freqs[None, :]
    return inner, cross, state, jnp.cos(ang), jnp.sin(ang)


def _rotate(x, cos, sin):
    half = DK // 2
    x1, x2 = x[:, :half], x[:, half:]
    return jnp.concatenate([x1 * cos - x2 * sin, x1 * sin + x2 * cos], axis=1)


def _rotate_back(d, cos, sin):
    half = DK // 2
    d1, d2 = d[:, :half], d[:, half:]
    return jnp.concatenate([d1 * cos + d2 * sin, d2 * cos - d1 * sin], axis=1)


def _ret_fwd(proj, gain, tables, nb):
    t = nb * S
    inner_t, cross_t, state_t, cos_t, sin_t = tables

    def body(q_ref, k_ref, v_ref, gb_ref, gain_ref, dm_ref, cd_ref, sd_ref, cos_ref, sin_ref,
             yb_ref, qr_ref, kr_ref, o_ref, rs_ref, r_s):
        r_s[...] = jnp.zeros_like(r_s)
        chunk_decay = cd_ref[CH - 1:CH, :]

        def chunk(c, carry):
            rows = _rows(c, CH)
            cos, sin = cos_ref[rows, :], sin_ref[rows, :]
            qr = _rotate(q_ref[rows, :], cos, sin).astype(BF16)
            kr = (_rotate(k_ref[rows, :], cos, sin) * (DK ** -0.5)).astype(BF16)
            v = v_ref[rows, :]
            qr_ref[rows, :] = qr
            kr_ref[rows, :] = kr
            r = r_s[...]
            rb = r.astype(BF16)
            rs_ref[c] = rb
            p = (_dot_nt(qr, kr) * dm_ref[...]).astype(BF16)
            o = _dot(p, v.astype(BF16)) + _dot(qr, rb) * cd_ref[...]
            r_s[...] = chunk_decay * r + _dot_tn(kr, (v * sd_ref[...]).astype(BF16))
            o_ref[rows, :] = o
            oc = o - jnp.mean(o, axis=-1, keepdims=True)
            rstd = lax.rsqrt(jnp.mean(oc * oc, axis=-1, keepdims=True) + EPS)
            gb = gb_ref[rows, :]
            yb_ref[rows, :] = (gb * _sigmoid(gb) * (oc * rstd * gain_ref[...])).astype(BF16)
            return carry

        lax.fori_loop(0, NCH, chunk, 0)

    seg = lambda s: pl.BlockSpec((None, S, DK), lambda b, h: (s, b, h))
    tab = pl.BlockSpec((None, CH, DK), lambda b, h: (h, 0, 0))
    rot = pl.BlockSpec((S, DK // 2), lambda b, h: (0, 0))
    blk = pl.BlockSpec((S, DK), lambda b, h: (b, h))
    return pl.pallas_call(
        body, name="ret_fwd", grid=(nb, HEADS),
        in_specs=[seg(2), seg(3), seg(4), seg(5),
                  pl.BlockSpec((None, 1, DK), lambda b, h: (h, 0, 0)),
                  tab, tab, tab, rot, rot],
        out_specs=[blk, blk, blk, blk,
                   pl.BlockSpec((None, None, NCH, DK, DK), lambda b, h: (b, h, 0, 0, 0))],
        out_shape=[jax.ShapeDtypeStruct((t, D), BF16), jax.ShapeDtypeStruct((t, D), BF16),
                   jax.ShapeDtypeStruct((t, D), BF16), jax.ShapeDtypeStruct((t, D), F32),
                   jax.ShapeDtypeStruct((nb, HEADS, NCH, DK, DK), BF16)],
        scratch_shapes=[pltpu.VMEM((DK, DK), F32)],
        compiler_params=_params(("arbitrary", "arbitrary")),
    )(proj, proj, proj, proj, gain, inner_t, cross_t, state_t, cos_t, sin_t)


def _ret_bwd(proj, qr, kr, o, rs, dyb, gain, tables, nb):
    t = nb * S
    inner_t, cross_t, state_t, cos_t, sin_t = tables

    def body(qr_ref, kr_ref, v_ref, gb_ref, o_ref, dyb_ref, rs_ref, gain_ref, dm_ref, cd_ref, sd_ref,
             cos_ref, sin_ref, dp_ref, dgain_ref, dr_s):
        dr_s[...] = jnp.zeros_like(dr_s)
        chunk_decay = cd_ref[CH - 1:CH, :]

        @pl.when(pl.program_id(1) == 0)
        def _():
            dgain_ref[...] = jnp.zeros_like(dgain_ref)

        def chunk(i, carry):
            c = NCH - 1 - i
            rows = _rows(c, CH)
            gain_v = gain_ref[...]
            o_c = o_ref[rows, :]
            oc = o_c - jnp.mean(o_c, axis=-1, keepdims=True)
            rstd = lax.rsqrt(jnp.mean(oc * oc, axis=-1, keepdims=True) + EPS)
            yn = oc * rstd
            gb = gb_ref[rows, :]
            sg = _sigmoid(gb)
            dyb_c = dyb_ref[rows, :]
            dgn = dyb_c * (gb * sg)
            dp_ref[3, rows, :] = (dyb_c * (yn * gain_v) * (sg * (1.0 + gb * (1.0 - sg)))).astype(BF16)
            dgain_ref[...] += jnp.sum(dgn * yn, axis=0, keepdims=True)
            dyn = dgn * gain_v
            do = rstd * (dyn - jnp.mean(dyn, axis=-1, keepdims=True)
                         - yn * jnp.mean(dyn * yn, axis=-1, keepdims=True))
            dob = do.astype(BF16)
            dox = (do * cd_ref[...]).astype(BF16)

            q_c, k_c = qr_ref[rows, :], kr_ref[rows, :]
            v = v_ref[rows, :]
            vb = v.astype(BF16)
            vs = (v * sd_ref[...]).astype(BF16)
            rb = rs_ref[c]
            d_r = dr_s[...]
            drb = d_r.astype(BF16)
            dm = dm_ref[...]
            p = (_dot_nt(q_c, k_c) * dm).astype(BF16)
            dpm = (_dot_nt(dob, vb) * dm).astype(BF16)
            dq = _dot(dpm, k_c) + _dot_nt(dox, rb)
            dk = _dot_tn(dpm, q_c) + _dot_nt(vs, drb)
            dv = _dot_tn(p, dob) + _dot(k_c, drb) * sd_ref[...]
            dr_s[...] = chunk_decay * d_r + _dot_tn(q_c, dox)

            cos, sin = cos_ref[rows, :], sin_ref[rows, :]
            dp_ref[0, rows, :] = _rotate_back(dq, cos, sin).astype(BF16)
            dp_ref[1, rows, :] = (_rotate_back(dk, cos, sin) * (DK ** -0.5)).astype(BF16)
            dp_ref[2, rows, :] = dv.astype(BF16)
            return carry

        lax.fori_loop(0, NCH, chunk, 0)

    seg = lambda s: pl.BlockSpec((None, S, DK), lambda h, b: (s, b, h))
    tab = pl.BlockSpec((None, CH, DK), lambda h, b: (h, 0, 0))
    rot = pl.BlockSpec((S, DK // 2), lambda h, b: (0, 0))
    blk = pl.BlockSpec((S, DK), lambda h, b: (b, h))
    one = pl.BlockSpec((None, 1, DK), lambda h, b: (h, 0, 0))
    return pl.pallas_call(
        body, name="ret_bwd", grid=(HEADS, nb),
        in_specs=[blk, blk, seg(4), seg(5), blk, blk,
                  pl.BlockSpec((None, None, NCH, DK, DK), lambda h, b: (b, h, 0, 0, 0)),
                  one, tab, tab, tab, rot, rot],
        out_specs=[pl.BlockSpec((4, S, DK), lambda h, b: (0, b, h)), one],
        out_shape=[jax.ShapeDtypeStruct((4, t, D), BF16), jax.ShapeDtypeStruct((HEADS, 1, DK), F32)],
        scratch_shapes=[pltpu.VMEM((DK, DK), F32)],
        compiler_params=_params(("arbitrary", "arbitrary")),
    )(qr, kr, proj, proj, o, dyb, rs, gain, inner_t, cross_t, state_t, cos_t, sin_t)


def _wblock(k):
    return pl.BlockSpec((NDEV, D // NDEV, D), lambda i: (0, D // (D // NDEV) + k, 0))


def _tail(ya, yb, proj, x2d, tgt, wg, g_fin):
    t = x2d.shape[0]
    tm = 256

    def body(ya_ref, yb_ref, ma_ref, mb_ref, x_ref, t_ref, wa_ref, wb_ref, wo_ref, g_ref,
             dx2_ref, dya_ref, dyb_ref, dm_ref, mg_ref, doa_ref, dob_ref, gfin_ref, loss_ref):
        i = pl.program_id(0)

        @pl.when(i == 0)
        def _():
            gfin_ref[...] = jnp.zeros_like(gfin_ref)
            loss_ref[...] = jnp.zeros_like(loss_ref)

        wa = wa_ref[...].reshape(D, D)
        wb = wb_ref[...].reshape(D, D)
        wo = wo_ref[...].reshape(D, D)
        out_a = _dot(ya_ref[...], wa)
        out_b = _dot(yb_ref[...], wb)
        sa = _sigmoid(ma_ref[...])
        sb = _sigmoid(mb_ref[...])
        merged = (sa * out_a + sb * out_b).astype(BF16)
        mg_ref[...] = merged
        x2 = x_ref[...] + _dot(merged, wo)
        r2 = lax.rsqrt(jnp.mean(x2 * x2, axis=-1, keepdims=True) + EPS)
        xh = x2 * r2
        g = g_ref[...]
        err = xh * g - t_ref[...]
        loss_ref[...] += jnp.sum(err * err, axis=0, keepdims=True) * (0.5 / D)
        dy = err * (1.0 / D)
        gfin_ref[...] += jnp.sum(dy * xh, axis=0, keepdims=True)
        dxh = dy * g
        dx2 = r2 * (dxh - xh * jnp.mean(dxh * xh, axis=-1, keepdims=True))
        dx2_ref[...] = dx2
        dmerged = _dot_nt(dx2.astype(BF16), wo)
        doa = (sa * dmerged).astype(BF16)
        dob = (sb * dmerged).astype(BF16)
        doa_ref[...] = doa
        dob_ref[...] = dob
        dm_ref[0] = (dmerged * out_a * sa * (1.0 - sa)).astype(BF16)
        dm_ref[1] = (dmerged * out_b * sb * (1.0 - sb)).astype(BF16)
        dya_ref[...] = _dot_nt(doa, wa)
        dyb_ref[...] = _dot_nt(dob, wb)

    row = lambda: pl.BlockSpec((tm, D), lambda i: (i, 0))
    seg = lambda s: pl.BlockSpec((None, tm, D), lambda i: (s, i, 0))
    vec = pl.BlockSpec((1, D), lambda i: (0, 0))
    return pl.pallas_call(
        body, name="tail", grid=(t // tm,),
        in_specs=[row(), row(), seg(6), seg(7), row(), row(), _wblock(0), _wblock(1), _wblock(2), vec],
        out_specs=[row(), row(), row(), pl.BlockSpec((2, tm, D), lambda i: (0, i, 0)),
                   row(), row(), row(), vec, vec],
        out_shape=[jax.ShapeDtypeStruct((t, D), F32), jax.ShapeDtypeStruct((t, D), F32),
                   jax.ShapeDtypeStruct((t, D), F32), jax.ShapeDtypeStruct((2, t, D), BF16),
                   jax.ShapeDtypeStruct((t, D), BF16), jax.ShapeDtypeStruct((t, D), BF16),
                   jax.ShapeDtypeStruct((t, D), BF16), jax.ShapeDtypeStruct((1, D), F32),
                   jax.ShapeDtypeStruct((1, D), F32)],
        compiler_params=_params(("arbitrary",)),
    )(ya, yb, proj, proj, x2d, tgt, wg, wg, wg, g_fin)


def _tail_wgrad(ya, yb, merged, doa, dob, dx2):
    t = ya.shape[0]
    tm = 512

    def body(ya_ref, yb_ref, mg_ref, doa_ref, dob_ref, dx2_ref, ga_ref, gb_ref, go_ref):
        @pl.when(pl.program_id(0) == 0)
        def _():
            ga_ref[...] = jnp.zeros_like(ga_ref)
            gb_ref[...] = jnp.zeros_like(gb_ref)
            go_ref[...] = jnp.zeros_like(go_ref)

        ga_ref[...] += _dot_tn(ya_ref[...], doa_ref[...])
        gb_ref[...] += _dot_tn(yb_ref[...], dob_ref[...])
        go_ref[...] += _dot_tn(mg_ref[...], dx2_ref[...].astype(BF16))

    row = lambda: pl.BlockSpec((tm, D), lambda i: (i, 0))
    full = lambda: pl.BlockSpec((D, D), lambda i: (0, 0))
    return pl.pallas_call(
        body, name="tail_wgrad", grid=(t // tm,),
        in_specs=[row() for _ in range(6)], out_specs=[full(), full(), full()],
        out_shape=[jax.ShapeDtypeStruct((D, D), F32)] * 3,
        compiler_params=_params(("arbitrary",)),
    )(ya, yb, merged, doa, dob, dx2)


def _dproj_specs(tm, j_of, i_of):
    last = lambda j, i, lo, n: (jnp.clip(j - lo, 0, n - 1), i, 0)
    return [pl.BlockSpec((None, tm, D), lambda a, b: last(j_of(a, b), i_of(a, b), 0, 2)),
            pl.BlockSpec((None, tm, D), lambda a, b: last(j_of(a, b), i_of(a, b), 2, 4)),
            pl.BlockSpec((None, tm, D), lambda a, b: last(j_of(a, b), i_of(a, b), 6, 2))]


def _dproj_pick(j, da_ref, db_ref, dc_ref, use):
    @pl.when(j < 2)
    def _():
        use(da_ref[...])

    @pl.when(jnp.logical_and(j >= 2, j < 6))
    def _():
        use(db_ref[...])

    @pl.when(j >= 6)
    def _():
        use(dc_ref[...])


def _inproj_wgrad(h, dpa, dpb, dpc):
    t = h.shape[0]
    tm = 1024

    def body(h_ref, da_ref, db_ref, dc_ref, gw_ref):
        @pl.when(pl.program_id(1) == 0)
        def _():
            gw_ref[...] = jnp.zeros_like(gw_ref)

        def use(d):
            gw_ref[...] += _dot_tn(h_ref[...], d)

        _dproj_pick(pl.program_id(0), da_ref, db_ref, dc_ref, use)

    return pl.pallas_call(
        body, name="inproj_wgrad", grid=(NSEG, t // tm),
        in_specs=[pl.BlockSpec((tm, D), lambda j, i: (i, 0))] + _dproj_specs(tm, lambda j, i: j, lambda j, i: i),
        out_specs=pl.BlockSpec((None, D, D), lambda j, i: (j, 0, 0)),
        out_shape=jax.ShapeDtypeStruct((NSEG, D, D), F32),
        compiler_params=_params(("arbitrary", "arbitrary")),
    )(h, dpa, dpb, dpc)


def _inproj_dgrad(dpa, dpb, dpc, wg, x2d, dx2, g_in):
    t = x2d.shape[0]
    tm = 512

    def body(da_ref, db_ref, dc_ref, w_ref, x_ref, dx2_ref, g_ref, gx_ref, gg_ref, acc_s):
        i, j = pl.program_id(0), pl.program_id(1)

        @pl.when(jnp.logical_and(i == 0, j == 0))
        def _():
            gg_ref[...] = jnp.zeros_like(gg_ref)

        @pl.when(j == 0)
        def _():
            acc_s[...] = jnp.zeros_like(acc_s)

        def use(d):
            acc_s[...] += _dot_nt(d, w_ref[...])

        _dproj_pick(j, da_ref, db_ref, dc_ref, use)

        @pl.when(j == NSEG - 1)
        def _():
            x = x_ref[...]
            r = lax.rsqrt(jnp.mean(x * x, axis=-1, keepdims=True) + EPS)
            xh = x * r
            dh = acc_s[...]
            gg_ref[...] += jnp.sum(dh * xh, axis=0, keepdims=True)
            dxh = dh * g_ref[...]
            gx_ref[...] = dx2_ref[...] + r * (dxh - xh * jnp.mean(dxh * xh, axis=-1, keepdims=True))

    row = lambda: pl.BlockSpec((tm, D), lambda i, j: (i, 0))
    vec = pl.BlockSpec((1, D), lambda i, j: (0, 0))
    return pl.pallas_call(
        body, name="inproj_dgrad", grid=(t // tm, NSEG),
        in_specs=_dproj_specs(tm, lambda i, j: j, lambda i, j: i)
        + [pl.BlockSpec((None, D, D), lambda i, j: (j, 0, 0)), row(), row(), vec],
        out_specs=[row(), vec],
        out_shape=[jax.ShapeDtypeStruct((t, D), F32), jax.ShapeDtypeStruct((1, D), F32)],
        scratch_shapes=[pltpu.VMEM((tm, D), F32)],
        compiler_params=_params(("arbitrary", "arbitrary")),
    )(dpa, dpb, dpc, wg, x2d, dx2, g_in)


def _adamw(name, parts, w, m, v):
    n, rows, cols = parts.shape
    tr = rows if rows <= 256 else 256

    def body(p_ref, w_ref, m_ref, v_ref, g_ref, d_ref, nm_ref, nv_ref):
        g = p_ref[0]
        for k in range(1, n):
            g = g + p_ref[k]
        m_new = ADAM_B1 * m_ref[...] + (1.0 - ADAM_B1) * g
        v_new = ADAM_B2 * v_ref[...] + (1.0 - ADAM_B2) * (g * g)
        m_hat = m_new / (1.0 - ADAM_B1 ** ADAM_STEP)
        v_hat = v_new / (1.0 - ADAM_B2 ** ADAM_STEP)
        g_ref[...] = g
        d_ref[...] = -ADAM_LR * (m_hat / (jnp.sqrt(v_hat) + ADAM_EPS) + ADAM_WD * w_ref[...])
        nm_ref[...] = m_new
        nv_ref[...] = v_new

    blk = lambda: pl.BlockSpec((tr, cols), lambda i: (i, 0))
    return pl.pallas_call(
        body, name=name, grid=(rows // tr,),
        in_specs=[pl.BlockSpec((n, tr, cols), lambda i: (0, i, 0)), blk(), blk(), blk()],
        out_specs=[blk(), blk(), blk(), blk()],
        out_shape=[jax.ShapeDtypeStruct((rows, cols), F32)] * 4,
        compiler_params=_params(("arbitrary",)),
    )(parts, w, m, v)


ANY = pl.BlockSpec(memory_space=pl.ANY)


def _place():
    return lax.axis_index("x"), lax.axis_index("y"), lax.axis_index("c")


def _allgather(name, arrs):
    n = len(arrs)

    def body(*refs):
        ins, outs = refs[:n], refs[n:2 * n]
        send_sems, recv_sems, local_sems = refs[2 * n:]
        x, y, c = _place()
        me, sibling = (x, y, c), (x, y, 1 - c)
        chips = [(1 - x, y), (x, 1 - y), (1 - x, 1 - y)]

        def copy(a, k, block, to, src=None):
            px, py, pc = block
            dst = outs[a].at[4 * px + 2 * py + pc]
            return pltpu.make_async_remote_copy(
                src_ref=dst if src is None else src, dst_ref=dst,
                send_sem=send_sems.at[a, k], recv_sem=recv_sems.at[a, k], device_id=to, device_id_type=MESH)

        mine = [pltpu.make_async_copy(ins[a], outs[a].at[4 * x + 2 * y + c], local_sems.at[a]) for a in range(n)]
        for cp in mine:
            cp.start()
        first = []
        for a in range(n):
            first.append(copy(a, 0, me, sibling, src=ins[a]))
            first += [copy(a, 1 + j, me, (*chip, c), src=ins[a]) for j, chip in enumerate(chips)]
        for cp in first:
            cp.start()
        passed = []
        for j, chip in enumerate(chips):
            for a in range(n):
                copy(a, 1 + j, (*chip, c), me).wait_recv()
                fwd = copy(a, 4 + j, (*chip, c), sibling)
                fwd.start()
                passed.append(fwd)
        for a in range(n):
            copy(a, 0, sibling, me).wait_recv()
            for j, chip in enumerate(chips):
                copy(a, 4 + j, (*chip, 1 - c), me).wait_recv()
        for cp in first + passed:
            cp.wait_send()
        for cp in mine:
            cp.wait()

    return pl.pallas_call(
        body, name=name,
        in_specs=[ANY] * n, out_specs=[ANY] * n,
        out_shape=[jax.ShapeDtypeStruct((NDEV,) + a.shape, a.dtype) for a in arrs],
        scratch_shapes=[pltpu.SemaphoreType.DMA((n, 7)), pltpu.SemaphoreType.DMA((n, 7)),
                        pltpu.SemaphoreType.DMA((n,))],
    )(*arrs)


def _sibling_exchange(arrs):
    n = len(arrs)

    def body(*refs):
        ins, outs = refs[:n], refs[n:2 * n]
        send_sems, recv_sems = refs[2 * n:]
        x, y, c = _place()
        copies = []
        for a in range(n):
            for q in range(4):
                copies.append(pltpu.make_async_remote_copy(
                    src_ref=ins[a].at[2 * q + 1 - c], dst_ref=outs[a].at[q],
                    send_sem=send_sems.at[a, q], recv_sem=recv_sems.at[a, q],
                    device_id=(x, y, 1 - c), device_id_type=MESH))
        for cp in copies:
            cp.start()
        for cp in copies:
            cp.wait()

    return pl.pallas_call(
        body, name="grad_sibling_exchange",
        in_specs=[ANY] * n, out_specs=[ANY] * n,
        out_shape=[jax.ShapeDtypeStruct((4,) + a.shape[1:], a.dtype) for a in arrs],
        scratch_shapes=[pltpu.SemaphoreType.DMA((n, 4)), pltpu.SemaphoreType.DMA((n, 4))],
    )(*arrs)


def _chip_sum(name, own, got, core):
    _, rows, cols = own.shape
    tr = rows if rows <= 512 else 512

    def body(core_ref, own_ref, got_ref, out_ref):
        out_ref[...] = own_ref[...] + got_ref[...]

    return pl.pallas_call(
        body, name=name,
        grid_spec=pltpu.PrefetchScalarGridSpec(
            num_scalar_prefetch=1, grid=(4, rows // tr),
            in_specs=[pl.BlockSpec((None, tr, cols), lambda q, i, core_ref: (2 * q + core_ref[0], i, 0)),
                      pl.BlockSpec((None, tr, cols), lambda q, i, core_ref: (q, i, 0))],
            out_specs=pl.BlockSpec((None, tr, cols), lambda q, i, core_ref: (q, i, 0))),
        out_shape=jax.ShapeDtypeStruct((4, rows, cols), F32),
        compiler_params=_params(("arbitrary", "arbitrary")),
    )(core, own, got)


def _chip_exchange(arrs):
    n = len(arrs)

    def body(*refs):
        ins, outs = refs[:n], refs[n:2 * n]
        send_sems, recv_sems, local_sems = refs[2 * n:]
        x, y, c = _place()
        my_chip = 2 * x + y
        chips = [(1 - x, y), (x, 1 - y), (1 - x, 1 - y)]
        mine = [pltpu.make_async_copy(ins[a].at[my_chip], outs[a].at[my_chip], local_sems.at[a]) for a in range(n)]
        for cp in mine:
            cp.start()
        copies = []
        for a in range(n):
            for j, (px, py) in enumerate(chips):
                copies.append(pltpu.make_async_remote_copy(
                    src_ref=ins[a].at[2 * px + py], dst_ref=outs[a].at[my_chip],
                    send_sem=send_sems.at[a, j], recv_sem=recv_sems.at[a, j],
                    device_id=(px, py, c), device_id_type=MESH))
        for cp in copies:
            cp.start()
        for a in range(n):
            for j, (px, py) in enumerate(chips):
                pltpu.make_async_remote_copy(
                    src_ref=ins[a].at[my_chip], dst_ref=outs[a].at[2 * px + py],
                    send_sem=send_sems.at[a, j], recv_sem=recv_sems.at[a, j],
                    device_id=(px, py, c), device_id_type=MESH).wait_recv()
        for cp in copies:
            cp.wait_send()
        for cp in mine:
            cp.wait()

    return pl.pallas_call(
        body, name="grad_chip_exchange",
        in_specs=[ANY] * n, out_specs=[ANY] * n,
        out_shape=[jax.ShapeDtypeStruct(a.shape, a.dtype) for a in arrs],
        scratch_shapes=[pltpu.SemaphoreType.DMA((n, 3)), pltpu.SemaphoreType.DMA((n, 3)),
                        pltpu.SemaphoreType.DMA((n,))],
    )(*arrs)


def _block_diag(w):
    w4 = w.reshape(NCB, 4, 64, 64)
    eye = jnp.eye(4, dtype=w.dtype)
    return (w4[:, :, :, None, :] * eye[None, :, None, :, None]).reshape(NCB, CB, CB)


def _block_diag_back(g):
    g5 = g.reshape(NCB, 4, 64, 4, 64)
    return jnp.stack([g5[:, m, :, m, :] for m in range(4)], axis=1).reshape(16, 64, 64)


def _local_grads(x, tgt, wg, norm_in, conv_w, conv_b, gate_x_w, gate_x_b, gate_a_w, gate_a_b, lru_lambda,
                 gn_gain, norm_final):
    nb = x.shape[0]
    t = nb * S
    x2d = x.reshape(t, D)
    tgt2d = tgt.reshape(t, D)
    wbd = jnp.concatenate([_block_diag(gate_x_w), _block_diag(gate_a_w)], axis=-1).astype(BF16)
    gain3 = gn_gain.reshape(HEADS, 1, DK)
    tables = _retention_tables()

    proj, h = _inproj(x2d, norm_in, wg)
    ya, hs = _lru_fwd(proj, conv_w, conv_b, wbd, gate_x_b, gate_a_b, lru_lambda, nb)
    yb, qr, kr, o, rs = _ret_fwd(proj, gain3, tables, nb)
    dx2, dya, dyb, dpc, merged, doa, dob, g_fin, loss_vec = _tail(ya, yb, proj, x2d, tgt2d, wg, norm_final)
    g_pa, g_pb, g_out = _tail_wgrad(ya, yb, merged, doa, dob, dx2)
    dpa, g_wbd, g_vec = _lru_bwd(proj, hs, dya, conv_w, conv_b, wbd, gate_x_b, gate_a_b, lru_lambda, nb)
    dpb, g_gain = _ret_bwd(proj, qr, kr, o, rs, dyb, gain3, tables, nb)
    g_in = _inproj_wgrad(h, dpa, dpb, dpc)
    grad_x, g_norm_in = _inproj_dgrad(dpa, dpb, dpc, wg, x2d, dx2, norm_in)

    small = jnp.concatenate([
        g_norm_in, g_vec[0:1], g_vec[1:2], g_vec[2:3], g_vec[3:4], g_fin,
        g_vec[4:8],
        g_gain.reshape(1, D),
        jnp.zeros((5, D), F32),
        _block_diag_back(g_wbd[:, :, :CB]).reshape(64, D),
        _block_diag_back(g_wbd[:, :, CB:]).reshape(64, D),
    ], axis=0)
    return jnp.sum(loss_vec), grad_x.reshape(nb, S, D), g_in, g_pa, g_pb, g_out, small


def kernel(x, norm_in, w_in, conv_w, conv_b, gate_x_w, gate_x_b, gate_a_w, gate_a_b, lru_lambda, gn_gain, w_proj_a, w_proj_b, w_out, norm_final, loss_target, m_norm_in, m_w_in, m_conv_w, m_conv_b, m_gate_x_w, m_gate_x_b, m_gate_a_w, m_gate_a_b, m_lru_lambda, m_gn_gain, m_w_proj_a, m_w_proj_b, m_w_out, m_norm_final, v_norm_in, v_w_in, v_conv_w, v_conv_b, v_gate_x_w, v_gate_x_b, v_gate_a_w, v_gate_a_b, v_lru_lambda, v_gn_gain, v_w_proj_a, v_w_proj_b, v_w_out, v_norm_final):
    xi, yi, ci = _place()
    me = 4 * xi + 2 * yi + ci
    nshard = D // NDEV

    slab = jnp.concatenate([w_in[0], w_proj_a[0], w_proj_b[0], w_out[0]], axis=0).astype(BF16)
    tiny = jnp.concatenate([conv_w[0], jnp.pad(gn_gain[0], ((0, 0), (0, nshard - DK // NDEV)))], axis=0)
    wg, tiny_g = _allgather("weight_allgather", [slab, tiny])
    conv_w_full = tiny_g[:, 0:4, :].transpose(1, 0, 2).reshape(4, D)
    gain_full = tiny_g[:, 4:8, :DK // NDEV].transpose(1, 0, 2).reshape(HEADS, DK)

    loss_part, grad_x, g_in, g_pa, g_pb, g_out, small = _local_grads(
        x, loss_target, wg, norm_in, conv_w_full, conv_b, gate_x_w[0], gate_x_b, gate_a_w[0], gate_a_b,
        lru_lambda, gain_full, norm_final.reshape(1, D))
    loss = lax.psum(loss_part, AXES)

    own = [g_in, g_pa.reshape(NDEV, nshard, D), g_pb.reshape(NDEV, nshard, D), g_out.reshape(NDEV, nshard, D)]
    got = _sibling_exchange(own)
    core = ci.astype(jnp.int32).reshape(1)
    sums = [_chip_sum("chip_sum_%d" % k, own[k], got[k], core) for k in range(4)]
    parts = _chip_exchange(sums)

    (small_all,) = _allgather("small_grad_allgather", [small])

    big = [("w_in", w_in, m_w_in, v_w_in), ("w_proj_a", w_proj_a, m_w_proj_a, v_w_proj_a),
           ("w_proj_b", w_proj_b, m_w_proj_b, v_w_proj_b), ("w_out", w_out, m_w_out, v_w_out)]
    res = {}
    for k, (nm, w, m, v) in enumerate(big):
        out = _adamw("adamw_" + nm, parts[k], w[0], m[0], v[0])
        res[nm] = [o[None] for o in out]

    def pack(p):
        return jnp.concatenate([
            p["norm_in"], p["conv_b"], p["gate_x_b"], p["gate_a_b"], p["lru_lambda"], p["norm_final"].reshape(1, D),
            jnp.zeros((10, D), F32), p["gate_x_w"].reshape(64, D), p["gate_a_w"].reshape(64, D)], axis=0)

    names = ["norm_in", "conv_b", "gate_x_b", "gate_a_b", "lru_lambda", "norm_final", "gate_x_w", "gate_a_w"]
    ws = dict(norm_in=norm_in, conv_b=conv_b, gate_x_b=gate_x_b, gate_a_b=gate_a_b, lru_lambda=lru_lambda,
              norm_final=norm_final, gate_x_w=gate_x_w, gate_a_w=gate_a_w)
    ms = dict(norm_in=m_norm_in, conv_b=m_conv_b, gate_x_b=m_gate_x_b, gate_a_b=m_gate_a_b, lru_lambda=m_lru_lambda,
              norm_final=m_norm_final, gate_x_w=m_gate_x_w, gate_a_w=m_gate_a_w)
    vs = dict(norm_in=v_norm_in, conv_b=v_conv_b, gate_x_b=v_gate_x_b, gate_a_b=v_gate_a_b, lru_lambda=v_lru_lambda,
              norm_final=v_norm_final, gate_x_w=v_gate_x_w, gate_a_w=v_gate_a_w)
    packed = _adamw("adamw_small", small_all, pack(ws), pack(ms), pack(vs))
    for nm in names:
        shape = ws[nm].shape
        if nm in ("gate_x_w", "gate_a_w"):
            lo = 16 if nm == "gate_x_w" else 80
            res[nm] = [o[lo:lo + 64].reshape(shape) for o in packed]
        else:
            r = names.index(nm)
            res[nm] = [o[r:r + 1].reshape(shape) for o in packed]

    g_small = packed[0]
    g_conv = lax.dynamic_slice(g_small[6:10], (0, me * nshard), (4, nshard))
    g_gain = lax.dynamic_slice(g_small[10:11].reshape(HEADS, DK), (0, me * (DK // NDEV)), (HEADS, DK // NDEV))
    pad = lambda a: jnp.pad(a, ((0, 0), (0, nshard - DK // NDEV)))
    shard = _adamw("adamw_shard",
                   jnp.concatenate([g_conv, pad(g_gain)], axis=0)[None],
                   jnp.concatenate([conv_w[0], pad(gn_gain[0])], axis=0),
                   jnp.concatenate([m_conv_w[0], pad(m_gn_gain[0])], axis=0),
                   jnp.concatenate([v_conv_w[0], pad(v_gn_gain[0])], axis=0))
    res["conv_w"] = [o[0:4][None] for o in shard]
    res["gn_gain"] = [o[4:8, :DK // NDEV][None] for o in shard]

    order = ["norm_in", "w_in", "conv_w", "conv_b", "gate_x_w", "gate_x_b", "gate_a_w", "gate_a_b", "lru_lambda",
             "gn_gain", "w_proj_a", "w_proj_b", "w_out", "norm_final"]
    outs = [loss, grad_x]
    for k in range(4):
        outs += [res[nm][k] for nm in order]
    return tuple(outs)
```

```python
import functools

import jax
import jax.numpy as jnp
from jax import lax
from jax.experimental import pallas as pl
from jax.experimental.pallas import tpu as pltpu

F32 = jnp.float32
BF16 = jnp.bfloat16
MESH = pl.DeviceIdType.MESH
AXES = ("x", "y", "c")

D = 1024
S = 2048
NSEG = 8
NDEV = 8
HEADS = 4
DK = 256
CH = 256
NCH = S // CH
CB = 256
NCB = D // CB
RC = 128
EPS = 1e-6
LRU_C = 8.0
WROWS = D + 3 * (D // NDEV)
SMALL_ROWS = 144
VMEM_LIMIT = 56 * 1024 * 1024

ADAM_LR = 0.001
ADAM_B1 = 0.9
ADAM_B2 = 0.999
ADAM_EPS = 1e-08
ADAM_WD = 0.01
ADAM_STEP = 10


def _params(sem=None):
    return pltpu.CompilerParams(dimension_semantics=sem, vmem_limit_bytes=VMEM_LIMIT)


def _dot(a, b):
    return jnp.dot(a, b, preferred_element_type=F32)


def _dot_nt(a, b):
    return lax.dot_general(a, b, (((1,), (1,)), ((), ())), preferred_element_type=F32)


def _dot_tn(a, b):
    return lax.dot_general(a, b, (((0,), (0,)), ((), ())), preferred_element_type=F32)


def _sigmoid(x):
    return jax.nn.sigmoid(x)


def _expm1_nonpos(x):
    poly = x * (1.0 + x * (0.5 + x * (1.0 / 6.0 + x * (1.0 / 24.0))))
    return jnp.where(x > -0.05, poly, jnp.exp(x) - 1.0)


def _softplus(x):
    return jnp.maximum(x, 0.0) + jnp.log(1.0 + jnp.exp(-jnp.abs(x)))


def _rows(c, n):
    return pl.ds(pl.multiple_of(c * n, n), n)


def _window_before(ref, c, n):
    r0 = c * n
    prev = ref[pl.ds(pl.multiple_of(jnp.maximum(r0 - 8, 0), 8), 8), :]
    prev = jnp.where(c > 0, prev, 0.0)
    return jnp.concatenate([prev, ref[_rows(c, n), :]], axis=0)


def _shift_down(win, s, n):
    if s == 0:
        return win[8:, :]
    return pltpu.roll(win, s, 0)[8:, :]


def _shift_up(win, s, n):
    if s == 0:
        return win[:n, :]
    return pltpu.roll(win, n + 8 - s, 0)[:n, :]


def _inproj(x2d, g_in, wg):
    t = x2d.shape[0]
    tm = 1024

    def body(x_ref, g_ref, w_ref, proj_ref, h_ref):
        @pl.when(pl.program_id(1) == 0)
        def _():
            x = x_ref[...]
            r = lax.rsqrt(jnp.mean(x * x, axis=-1, keepdims=True) + EPS)
            h_ref[...] = (x * r * g_ref[...]).astype(BF16)

        proj_ref[...] = _dot(h_ref[...], w_ref[...])

    return pl.pallas_call(
        body, name="inproj", grid=(t // tm, NSEG),
        in_specs=[pl.BlockSpec((tm, D), lambda i, j: (i, 0)),
                  pl.BlockSpec((1, D), lambda i, j: (0, 0)),
                  pl.BlockSpec((None, D, D), lambda i, j: (j, 0, 0))],
        out_specs=[pl.BlockSpec((None, tm, D), lambda i, j: (j, i, 0)),
                   pl.BlockSpec((tm, D), lambda i, j: (i, 0))],
        out_shape=[jax.ShapeDtypeStruct((NSEG, t, D), F32), jax.ShapeDtypeStruct((t, D), BF16)],
        compiler_params=_params(("arbitrary", "arbitrary")),
    )(x2d, g_in, wg)


def _tile_scan(a, u):
    row = lax.broadcasted_iota(jnp.int32, a.shape, 0)
    for d in (1, 2, 4):
        m = row >= d
        a_sh = pltpu.roll(a, d, 0)
        u_sh = pltpu.roll(u, d, 0)
        u = jnp.where(m, a * u_sh + u, u)
        a = jnp.where(m, a * a_sh, a)
    return a, u


def _tile_scan_rev(a, w):
    row = lax.broadcasted_iota(jnp.int32, a.shape, 0)
    for d in (1, 2, 4):
        m = row < 8 - d
        a_sh = pltpu.roll(a, 8 - d, 0)
        w_sh = pltpu.roll(w, 8 - d, 0)
        w = jnp.where(m, a * w_sh + w, w)
        a = jnp.where(m, a * a_sh, a)
    return a, w


def _lru_gates(xa_ref, c, cw_ref, cb_ref, wbd_ref, bx_ref, ba_ref, sp):
    win = _window_before(xa_ref, c, RC)
    xc = cb_ref[...] + cw_ref[3:4, :] * _shift_down(win, 0, RC)
    for s in (1, 2, 3):
        xc = xc + cw_ref[3 - s:4 - s, :] * _shift_down(win, s, RC)
    z = _dot(xc.astype(BF16), wbd_ref[...])
    gi = _sigmoid(z[:, :CB] + bx_ref[...])
    gr = _sigmoid(z[:, CB:] + ba_ref[...])
    log_a = -LRU_C * gr * sp
    return win, xc, gi, gr, log_a


def _lru_fwd(proj, conv_w, conv_b, wbd, bx, ba, lam, nb):
    t = nb * S

    def body(xa_ref, ga_ref, cw_ref, cb_ref, wbd_ref, bx_ref, ba_ref, lam_ref, ya_ref, hs_ref, a_s, u_s):
        sp = _softplus(-lam_ref[...])

        def gates(c, carry):
            _, xc, gi, _, log_a = _lru_gates(xa_ref, c, cw_ref, cb_ref, wbd_ref, bx_ref, ba_ref, sp)
            a_s[_rows(c, RC), :] = jnp.exp(log_a)
            u_s[_rows(c, RC), :] = jnp.sqrt(-_expm1_nonpos(2.0 * log_a)) * (gi * xc)
            return carry

        lax.fori_loop(0, S // RC, gates, 0)

        def scan(k, h):
            a_cum, u_cum = _tile_scan(a_s[_rows(k, 8), :], u_s[_rows(k, 8), :])
            h_tile = u_cum + a_cum * h
            hs_ref[_rows(k, 8), :] = h_tile
            return h_tile[7:8, :]

        lax.fori_loop(0, S // 8, scan, jnp.zeros((1, CB), F32), unroll=4)

        def gate_out(c, carry):
            ga = ga_ref[_rows(c, RC), :]
            ya_ref[_rows(c, RC), :] = (ga * _sigmoid(ga) * hs_ref[_rows(c, RC), :]).astype(BF16)
            return carry

        lax.fori_loop(0, S // RC, gate_out, 0)

    vec = pl.BlockSpec((1, CB), lambda b, cb: (0, cb))
    return pl.pallas_call(
        body, name="lru_fwd", grid=(nb, NCB),
        in_specs=[pl.BlockSpec((None, S, CB), lambda b, cb: (0, b, cb)),
                  pl.BlockSpec((None, S, CB), lambda b, cb: (1, b, cb)),
                  pl.BlockSpec((4, CB), lambda b, cb: (0, cb)),
                  vec,
                  pl.BlockSpec((None, CB, 2 * CB), lambda b, cb: (cb, 0, 0)),
                  vec, vec, vec],
        out_specs=[pl.BlockSpec((S, CB), lambda b, cb: (b, cb)),
                   pl.BlockSpec((S, CB), lambda b, cb: (b, cb))],
        out_shape=[jax.ShapeDtypeStruct((t, D), BF16), jax.ShapeDtypeStruct((t, D), F32)],
        scratch_shapes=[pltpu.VMEM((S, CB), F32), pltpu.VMEM((S, CB), F32)],
        compiler_params=_params(("arbitrary", "arbitrary")),
    )(proj, proj, conv_w, conv_b, wbd, bx, ba, lam)


def _lru_bwd(proj, hs, dya, conv_w, conv_b, wbd, bx, ba, lam, nb):
    t = nb * S

    def body(xa_ref, ga_ref, hs_ref, dya_ref, cw_ref, cb_ref, wbd_ref, bx_ref, ba_ref, lam_ref,
             dp_ref, dwbd_ref, vec_ref, a_s, xc_s, gi_s, gr_s, dh_s, dxc_s, acc_s):
        b = pl.program_id(1)
        lam_v = lam_ref[...]
        sp = _softplus(-lam_v)
        acc_s[...] = jnp.zeros_like(acc_s)

        @pl.when(b == 0)
        def _():
            dwbd_ref[...] = jnp.zeros_like(dwbd_ref)
            vec_ref[...] = jnp.zeros_like(vec_ref)

        def gates(c, carry):
            _, xc, gi, gr, log_a = _lru_gates(xa_ref, c, cw_ref, cb_ref, wbd_ref, bx_ref, ba_ref, sp)
            rows = _rows(c, RC)
            a_s[rows, :] = jnp.exp(log_a)
            xc_s[rows, :] = xc
            gi_s[rows, :] = gi
            gr_s[rows, :] = gr
            ga = ga_ref[rows, :]
            sg = _sigmoid(ga)
            dya_c = dya_ref[rows, :]
            dh_s[rows, :] = dya_c * (ga * sg)
            dp_ref[1, rows, :] = (dya_c * hs_ref[rows, :] * (sg * (1.0 + ga * (1.0 - sg)))).astype(BF16)
            return carry

        lax.fori_loop(0, S // RC, gates, 0)

        def scan(i, g_in):
            k = S // 8 - 1 - i
            a = a_s[_rows(k, 8), :]
            dl = dh_s[_rows(k, 8), :]
            a_cum, g_loc = _tile_scan_rev(a, a * dl)
            g = g_loc + a_cum * g_in
            row = lax.broadcasted_iota(jnp.int32, a.shape, 0)
            dh_s[_rows(k, 8), :] = dl + jnp.where(row < 7, pltpu.roll(g, 7, 0), g_in)
            return g[0:1, :]

        lax.fori_loop(0, S // 8, scan, jnp.zeros((1, CB), F32), unroll=4)

        dxc_s[pl.ds(S, 8), :] = jnp.zeros((8, CB), F32)

        def grads(c, carry):
            rows = _rows(c, RC)
            dh = dh_s[rows, :]
            h_prev = _shift_down(_window_before(hs_ref, c, RC), 1, RC)
            xc, gi, gr, a = xc_s[rows, :], gi_s[rows, :], gr_s[rows, :], a_s[rows, :]
            mult = jnp.sqrt(-_expm1_nonpos(-2.0 * LRU_C * gr * sp))
            dmult = dh * gi * xc
            d_log_a = dh * h_prev * a - dmult * (a * a) / mult
            dzi = dh * mult * xc * gi * (1.0 - gi)
            dzr = d_log_a * (-LRU_C * sp) * gr * (1.0 - gr)
            dz = jnp.concatenate([dzi, dzr], axis=1).astype(BF16)
            dxc_s[rows, :] = dh * mult * gi + _dot_nt(dz, wbd_ref[...])
            dwbd_ref[...] += _dot_tn(xc.astype(BF16), dz)
            acc_s[1:2, :] += jnp.sum(dzi, axis=0, keepdims=True)
            acc_s[2:3, :] += jnp.sum(dzr, axis=0, keepdims=True)
            acc_s[3:4, :] += jnp.sum(d_log_a * (-LRU_C * gr), axis=0, keepdims=True)
            return carry

        lax.fori_loop(0, S // RC, grads, 0)

        def conv_bwd(c, carry):
            rows = _rows(c, RC)
            dwin = dxc_s[pl.ds(pl.multiple_of(c * RC, RC), RC + 8), :]
            dxc = dwin[:RC, :]
            xwin = _window_before(xa_ref, c, RC)
            dxa = cw_ref[3:4, :] * dxc
            acc_s[0:1, :] += jnp.sum(dxc, axis=0, keepdims=True)
            acc_s[7:8, :] += jnp.sum(dxc * _shift_down(xwin, 0, RC), axis=0, keepdims=True)
            for s in (1, 2, 3):
                dxa = dxa + cw_ref[3 - s:4 - s, :] * _shift_up(dwin, s, RC)
                acc_s[7 - s:8 - s, :] += jnp.sum(dxc * _shift_down(xwin, s, RC), axis=0, keepdims=True)
            dp_ref[0, rows, :] = dxa.astype(BF16)
            return carry

        lax.fori_loop(0, S // RC, conv_bwd, 0)

        row = lax.broadcasted_iota(jnp.int32, acc_s.shape, 0)
        vec_ref[...] += jnp.where(row == 3, acc_s[...] * (-_sigmoid(-lam_v)), acc_s[...])

    vec = pl.BlockSpec((1, CB), lambda cb, b: (0, cb))
    blk = pl.BlockSpec((S, CB), lambda cb, b: (b, cb))
    return pl.pallas_call(
        body, name="lru_bwd", grid=(NCB, nb),
        in_specs=[pl.BlockSpec((None, S, CB), lambda cb, b: (0, b, cb)),
                  pl.BlockSpec((None, S, CB), lambda cb, b: (1, b, cb)),
                  blk, blk,
                  pl.BlockSpec((4, CB), lambda cb, b: (0, cb)),
                  vec,
                  pl.BlockSpec((None, CB, 2 * CB), lambda cb, b: (cb, 0, 0)),
                  vec, vec, vec],
        out_specs=[pl.BlockSpec((2, S, CB), lambda cb, b: (0, b, cb)),
                   pl.BlockSpec((None, CB, 2 * CB), lambda cb, b: (cb, 0, 0)),
                   pl.BlockSpec((8, CB), lambda cb, b: (0, cb))],
        out_shape=[jax.ShapeDtypeStruct((2, t, D), BF16),
                   jax.ShapeDtypeStruct((NCB, CB, 2 * CB), F32),
                   jax.ShapeDtypeStruct((8, D), F32)],
        scratch_shapes=[pltpu.VMEM((S, CB), F32), pltpu.VMEM((S, CB), F32), pltpu.VMEM((S, CB), F32),
                        pltpu.VMEM((S, CB), F32), pltpu.VMEM((S, CB), F32), pltpu.VMEM((S + 8, CB), F32),
                        pltpu.VMEM((8, CB), F32)],
        compiler_params=_params(("arbitrary", "arbitrary")),
    )(proj, proj, hs, dya, conv_w, conv_b, wbd, bx, ba, lam)


def _retention_tables():
    log_g = jnp.log1p(-(2.0 ** (-5.0 - jnp.arange(HEADS, dtype=F32))))
    idx = jnp.arange(CH, dtype=F32)
    diff = idx[:, None] - idx[None, :]
    inner = jnp.where(diff >= 0, jnp.exp(jnp.maximum(diff, 0.0)[None] * log_g[:, None, None]), 0.0)
    cross = jnp.exp((idx[None, :] + 1.0) * log_g[:, None])
    state = jnp.exp((CH - 1.0 - idx[None, :]) * log_g[:, None])
    cross = jnp.broadcast_to(cross[:, :, None], (HEADS, CH, DK))
    state = jnp.broadcast_to(state[:, :, None], (HEADS, CH, DK))
    half = DK // 2
    freqs = 10000.0 ** (-jnp.arange(half, dtype=F32) / half)
    ang = jnp.arange(S, dtype=F32)[:, None] * freqs[None, :]
    return inner, cross, state, jnp.cos(ang), jnp.sin(ang)


def _rotate(x, cos, sin):
    half = DK // 2
    x1, x2 = x[:, :half], x[:, half:]
    return jnp.concatenate([x1 * cos - x2 * sin, x1 * sin + x2 * cos], axis=1)


def _rotate_back(d, cos, sin):
    half = DK // 2
    d1, d2 = d[:, :half], d[:, half:]
    return jnp.concatenate([d1 * cos + d2 * sin, d2 * cos - d1 * sin], axis=1)


def _ret_fwd(proj, gain, tables, nb):
    t = nb * S
    inner_t, cross_t, state_t, cos_t, sin_t = tables

    def body(q_ref, k_ref, v_ref, gb_ref, gain_ref, dm_ref, cd_ref, sd_ref, cos_ref, sin_ref,
             yb_ref, qr_ref, kr_ref, o_ref, rs_ref, r_s):
        r_s[...] = jnp.zeros_like(r_s)
        chunk_decay = cd_ref[CH - 1:CH, :]

        def chunk(c, carry):
            rows = _rows(c, CH)
            cos, sin = cos_ref[rows, :], sin_ref[rows, :]
            qr = _rotate(q_ref[rows, :], cos, sin).astype(BF16)
            kr = (_rotate(k_ref[rows, :], cos, sin) * (DK ** -0.5)).astype(BF16)
            v = v_ref[rows, :]
            qr_ref[rows, :] = qr
            kr_ref[rows, :] = kr
            r = r_s[...]
            rb = r.astype(BF16)
            rs_ref[c] = rb
            p = (_dot_nt(qr, kr) * dm_ref[...]).astype(BF16)
            o = _dot(p, v.astype(BF16)) + _dot(qr, rb) * cd_ref[...]
            r_s[...] = chunk_decay * r + _dot_tn(kr, (v * sd_ref[...]).astype(BF16))
            o_ref[rows, :] = o
            oc = o - jnp.mean(o, axis=-1, keepdims=True)
            rstd = lax.rsqrt(jnp.mean(oc * oc, axis=-1, keepdims=True) + EPS)
            gb = gb_ref[rows, :]
            yb_ref[rows, :] = (gb * _sigmoid(gb) * (oc * rstd * gain_ref[...])).astype(BF16)
            return carry

        lax.fori_loop(0, NCH, chunk, 0)

    seg = lambda s: pl.BlockSpec((None, S, DK), lambda b, h: (s, b, h))
    tab = pl.BlockSpec((None, CH, DK), lambda b, h: (h, 0, 0))
    rot = pl.BlockSpec((S, DK // 2), lambda b, h: (0, 0))
    blk = pl.BlockSpec((S, DK), lambda b, h: (b, h))
    return pl.pallas_call(
        body, name="ret_fwd", grid=(nb, HEADS),
        in_specs=[seg(2), seg(3), seg(4), seg(5),
                  pl.BlockSpec((None, 1, DK), lambda b, h: (h, 0, 0)),
                  tab, tab, tab, rot, rot],
        out_specs=[blk, blk, blk, blk,
                   pl.BlockSpec((None, None, NCH, DK, DK), lambda b, h: (b, h, 0, 0, 0))],
        out_shape=[jax.ShapeDtypeStruct((t, D), BF16), jax.ShapeDtypeStruct((t, D), BF16),
                   jax.ShapeDtypeStruct((t, D), BF16), jax.ShapeDtypeStruct((t, D), F32),
                   jax.ShapeDtypeStruct((nb, HEADS, NCH, DK, DK), BF16)],
        scratch_shapes=[pltpu.VMEM((DK, DK), F32)],
        compiler_params=_params(("arbitrary", "arbitrary")),
    )(proj, proj, proj, proj, gain, inner_t, cross_t, state_t, cos_t, sin_t)


def _ret_bwd(proj, qr, kr, o, rs, dyb, gain, tables, nb):
    t = nb * S
    inner_t, cross_t, state_t, cos_t, sin_t = tables

    def body(qr_ref, kr_ref, v_ref, gb_ref, o_ref, dyb_ref, rs_ref, gain_ref, dm_ref, cd_ref, sd_ref,
             cos_ref, sin_ref, dp_ref, dgain_ref, dr_s):
        dr_s[...] = jnp.zeros_like(dr_s)
        chunk_decay = cd_ref[CH - 1:CH, :]

        @pl.when(pl.program_id(1) == 0)
        def _():
            dgain_ref[...] = jnp.zeros_like(dgain_ref)

        def chunk(i, carry):
            c = NCH - 1 - i
            rows = _rows(c, CH)
            gain_v = gain_ref[...]
            o_c = o_ref[rows, :]
            oc = o_c - jnp.mean(o_c, axis=-1, keepdims=True)
            rstd = lax.rsqrt(jnp.mean(oc * oc, axis=-1, keepdims=True) + EPS)
            yn = oc * rstd
            gb = gb_ref[rows, :]
            sg = _sigmoid(gb)
            dyb_c = dyb_ref[rows, :]
            dgn = dyb_c * (gb * sg)
            dp_ref[3, rows, :] = (dyb_c * (yn * gain_v) * (sg * (1.0 + gb * (1.0 - sg)))).astype(BF16)
            dgain_ref[...] += jnp.sum(dgn * yn, axis=0, keepdims=True)
            dyn = dgn * gain_v
            do = rstd * (dyn - jnp.mean(dyn, axis=-1, keepdims=True)
                         - yn * jnp.mean(dyn * yn, axis=-1, keepdims=True))
            dob = do.astype(BF16)
            dox = (do * cd_ref[...]).astype(BF16)

            q_c, k_c = qr_ref[rows, :], kr_ref[rows, :]
            v = v_ref[rows, :]
            vb = v.astype(BF16)
            vs = (v * sd_ref[...]).astype(BF16)
            rb = rs_ref[c]
            d_r = dr_s[...]
            drb = d_r.astype(BF16)
            dm = dm_ref[...]
            p = (_dot_nt(q_c, k_c) * dm).astype(BF16)
            dpm = (_dot_nt(dob, vb) * dm).astype(BF16)
            dq = _dot(dpm, k_c) + _dot_nt(dox, rb)
            dk = _dot_tn(dpm, q_c) + _dot_nt(vs, drb)
            dv = _dot_tn(p, dob) + _dot(k_c, drb) * sd_ref[...]
            dr_s[...] = chunk_decay * d_r + _dot_tn(q_c, dox)

            cos, sin = cos_ref[rows, :], sin_ref[rows, :]
            dp_ref[0, rows, :] = _rotate_back(dq, cos, sin).astype(BF16)
            dp_ref[1, rows, :] = (_rotate_back(dk, cos, sin) * (DK ** -0.5)).astype(BF16)
            dp_ref[2, rows, :] = dv.astype(BF16)
            return carry

        lax.fori_loop(0, NCH, chunk, 0)

    seg = lambda s: pl.BlockSpec((None, S, DK), lambda h, b: (s, b, h))
    tab = pl.BlockSpec((None, CH, DK), lambda h, b: (h, 0, 0))
    rot = pl.BlockSpec((S, DK // 2), lambda h, b: (0, 0))
    blk = pl.BlockSpec((S, DK), lambda h, b: (b, h))
    one = pl.BlockSpec((None, 1, DK), lambda h, b: (h, 0, 0))
    return pl.pallas_call(
        body, name="ret_bwd", grid=(HEADS, nb),
        in_specs=[blk, blk, seg(4), seg(5), blk, blk,
                  pl.BlockSpec((None, None, NCH, DK, DK), lambda h, b: (b, h, 0, 0, 0)),
                  one, tab, tab, tab, rot, rot],
        out_specs=[pl.BlockSpec((4, S, DK), lambda h, b: (0, b, h)), one],
        out_shape=[jax.ShapeDtypeStruct((4, t, D), BF16), jax.ShapeDtypeStruct((HEADS, 1, DK), F32)],
        scratch_shapes=[pltpu.VMEM((DK, DK), F32)],
        compiler_params=_params(("arbitrary", "arbitrary")),
    )(qr, kr, proj, proj, o, dyb, rs, gain, inner_t, cross_t, state_t, cos_t, sin_t)


def _wblock(k):
    return pl.BlockSpec((NDEV, D // NDEV, D), lambda i: (0, D // (D // NDEV) + k, 0))


def _tail(ya, yb, proj, x2d, tgt, wg, g_fin):
    t = x2d.shape[0]
    tm = 256

    def body(ya_ref, yb_ref, ma_ref, mb_ref, x_ref, t_ref, wa_ref, wb_ref, wo_ref, g_ref,
             dx2_ref, dya_ref, dyb_ref, dm_ref, mg_ref, doa_ref, dob_ref, gfin_ref, loss_ref):
        i = pl.program_id(0)

        @pl.when(i == 0)
        def _():
            gfin_ref[...] = jnp.zeros_like(gfin_ref)
            loss_ref[...] = jnp.zeros_like(loss_ref)

        wa = wa_ref[...].reshape(D, D)
        wb = wb_ref[...].reshape(D, D)
        wo = wo_ref[...].reshape(D, D)
        out_a = _dot(ya_ref[...], wa)
        out_b = _dot(yb_ref[...], wb)
        sa = _sigmoid(ma_ref[...])
        sb = _sigmoid(mb_ref[...])
        merged = (sa * out_a + sb * out_b).astype(BF16)
        mg_ref[...] = merged
        x2 = x_ref[...] + _dot(merged, wo)
        r2 = lax.rsqrt(jnp.mean(x2 * x2, axis=-1, keepdims=True) + EPS)
        xh = x2 * r2
        g = g_ref[...]
        err = xh * g - t_ref[...]
        loss_ref[...] += jnp.sum(err * err, axis=0, keepdims=True) * (0.5 / D)
        dy = err * (1.0 / D)
        gfin_ref[...] += jnp.sum(dy * xh, axis=0, keepdims=True)
        dxh = dy * g
        dx2 = r2 * (dxh - xh * jnp.mean(dxh * xh, axis=-1, keepdims=True))
        dx2_ref[...] = dx2
        dmerged = _dot_nt(dx2.astype(BF16), wo)
        doa = (sa * dmerged).astype(BF16)
        dob = (sb * dmerged).astype(BF16)
        doa_ref[...] = doa
        dob_ref[...] = dob
        dm_ref[0] = (dmerged * out_a * sa * (1.0 - sa)).astype(BF16)
        dm_ref[1] = (dmerged * out_b * sb * (1.0 - sb)).astype(BF16)
        dya_ref[...] = _dot_nt(doa, wa)
        dyb_ref[...] = _dot_nt(dob, wb)

    row = lambda: pl.BlockSpec((tm, D), lambda i: (i, 0))
    seg = lambda s: pl.BlockSpec((None, tm, D), lambda i: (s, i, 0))
    vec = pl.BlockSpec((1, D), lambda i: (0, 0))
    return pl.pallas_call(
        body, name="tail", grid=(t // tm,),
        in_specs=[row(), row(), seg(6), seg(7), row(), row(), _wblock(0), _wblock(1), _wblock(2), vec],
        out_specs=[row(), row(), row(), pl.BlockSpec((2, tm, D), lambda i: (0, i, 0)),
                   row(), row(), row(), vec, vec],
        out_shape=[jax.ShapeDtypeStruct((t, D), F32), jax.ShapeDtypeStruct((t, D), F32),
                   jax.ShapeDtypeStruct((t, D), F32), jax.ShapeDtypeStruct((2, t, D), BF16),
                   jax.ShapeDtypeStruct((t, D), BF16), jax.ShapeDtypeStruct((t, D), BF16),
                   jax.ShapeDtypeStruct((t, D), BF16), jax.ShapeDtypeStruct((1, D), F32),
                   jax.ShapeDtypeStruct((1, D), F32)],
        compiler_params=_params(("arbitrary",)),
    )(ya, yb, proj, proj, x2d, tgt, wg, wg, wg, g_fin)


def _tail_wgrad(ya, yb, merged, doa, dob, dx2):
    t = ya.shape[0]
    tm = 512

    def body(ya_ref, yb_ref, mg_ref, doa_ref, dob_ref, dx2_ref, ga_ref, gb_ref, go_ref):
        @pl.when(pl.program_id(0) == 0)
        def _():
            ga_ref[...] = jnp.zeros_like(ga_ref)
            gb_ref[...] = jnp.zeros_like(gb_ref)
            go_ref[...] = jnp.zeros_like(go_ref)

        ga_ref[...] += _dot_tn(ya_ref[...], doa_ref[...])
        gb_ref[...] += _dot_tn(yb_ref[...], dob_ref[...])
        go_ref[...] += _dot_tn(mg_ref[...], dx2_ref[...].astype(BF16))

    row = lambda: pl.BlockSpec((tm, D), lambda i: (i, 0))
    full = lambda: pl.BlockSpec((D, D), lambda i: (0, 0))
    return pl.pallas_call(
        body, name="tail_wgrad", grid=(t // tm,),
        in_specs=[row() for _ in range(6)], out_specs=[full(), full(), full()],
        out_shape=[jax.ShapeDtypeStruct((D, D), F32)] * 3,
        compiler_params=_params(("arbitrary",)),
    )(ya, yb, merged, doa, dob, dx2)


def _dproj_specs(tm, j_of, i_of):
    last = lambda j, i, lo, n: (jnp.clip(j - lo, 0, n - 1), i, 0)
    return [pl.BlockSpec((None, tm, D), lambda a, b: last(j_of(a, b), i_of(a, b), 0, 2)),
            pl.BlockSpec((None, tm, D), lambda a, b: last(j_of(a, b), i_of(a, b), 2, 4)),
            pl.BlockSpec((None, tm, D), lambda a, b: last(j_of(a, b), i_of(a, b), 6, 2))]


def _dproj_pick(j, da_ref, db_ref, dc_ref, use):
    @pl.when(j < 2)
    def _():
        use(da_ref[...])

    @pl.when(jnp.logical_and(j >= 2, j < 6))
    def _():
        use(db_ref[...])

    @pl.when(j >= 6)
    def _():
        use(dc_ref[...])


def _inproj_wgrad(h, dpa, dpb, dpc):
    t = h.shape[0]
    tm = 1024

    def body(h_ref, da_ref, db_ref, dc_ref, gw_ref):
        @pl.when(pl.program_id(1) == 0)
        def _():
            gw_ref[...] = jnp.zeros_like(gw_ref)

        def use(d):
            gw_ref[...] += _dot_tn(h_ref[...], d)

        _dproj_pick(pl.program_id(0), da_ref, db_ref, dc_ref, use)

    return pl.pallas_call(
        body, name="inproj_wgrad", grid=(NSEG, t // tm),
        in_specs=[pl.BlockSpec((tm, D), lambda j, i: (i, 0))] + _dproj_specs(tm, lambda j, i: j, lambda j, i: i),
        out_specs=pl.BlockSpec((None, D, D), lambda j, i: (j, 0, 0)),
        out_shape=jax.ShapeDtypeStruct((NSEG, D, D), F32),
        compiler_params=_params(("arbitrary", "arbitrary")),
    )(h, dpa, dpb, dpc)


def _inproj_dgrad(dpa, dpb, dpc, wg, x2d, dx2, g_in):
    t = x2d.shape[0]
    tm = 512

    def body(da_ref, db_ref, dc_ref, w_ref, x_ref, dx2_ref, g_ref, gx_ref, gg_ref, acc_s):
        i, j = pl.program_id(0), pl.program_id(1)

        @pl.when(jnp.logical_and(i == 0, j == 0))
        def _():
            gg_ref[...] = jnp.zeros_like(gg_ref)

        @pl.when(j == 0)
        def _():
            acc_s[...] = jnp.zeros_like(acc_s)

        def use(d):
            acc_s[...] += _dot_nt(d, w_ref[...])

        _dproj_pick(j, da_ref, db_ref, dc_ref, use)

        @pl.when(j == NSEG - 1)
        def _():
            x = x_ref[...]
            r = lax.rsqrt(jnp.mean(x * x, axis=-1, keepdims=True) + EPS)
            xh = x * r
            dh = acc_s[...]
            gg_ref[...] += jnp.sum(dh * xh, axis=0, keepdims=True)
            dxh = dh * g_ref[...]
            gx_ref[...] = dx2_ref[...] + r * (dxh - xh * jnp.mean(dxh * xh, axis=-1, keepdims=True))

    row = lambda: pl.BlockSpec((tm, D), lambda i, j: (i, 0))
    vec = pl.BlockSpec((1, D), lambda i, j: (0, 0))
    return pl.pallas_call(
        body, name="inproj_dgrad", grid=(t // tm, NSEG),
        in_specs=_dproj_specs(tm, lambda i, j: j, lambda i, j: i)
        + [pl.BlockSpec((None, D, D), lambda i, j: (j, 0, 0)), row(), row(), vec],
        out_specs=[row(), vec],
        out_shape=[jax.ShapeDtypeStruct((t, D), F32), jax.ShapeDtypeStruct((1, D), F32)],
        scratch_shapes=[pltpu.VMEM((tm, D), F32)],
        compiler_params=_params(("arbitrary", "arbitrary")),
    )(dpa, dpb, dpc, wg, x2d, dx2, g_in)


def _adamw(name, parts, w, m, v):
    n, rows, cols = parts.shape
    tr = rows if rows <= 256 else 256

    def body(p_ref, w_ref, m_ref, v_ref, g_ref, d_ref, nm_ref, nv_ref):
        g = p_ref[0].astype(F32)
        for k in range(1, n):
            g = g + p_ref[k].astype(F32)
        m_new = ADAM_B1 * m_ref[...] + (1.0 - ADAM_B1) * g
        v_new = ADAM_B2 * v_ref[...] + (1.0 - ADAM_B2) * (g * g)
        m_hat = m_new / (1.0 - ADAM_B1 ** ADAM_STEP)
        v_hat = v_new / (1.0 - ADAM_B2 ** ADAM_STEP)
        g_ref[...] = g
        d_ref[...] = -ADAM_LR * (m_hat / (jnp.sqrt(v_hat) + ADAM_EPS) + ADAM_WD * w_ref[...])
        nm_ref[...] = m_new
        nv_ref[...] = v_new

    blk = lambda: pl.BlockSpec((tr, cols), lambda i: (i, 0))
    return pl.pallas_call(
        body, name=name, grid=(rows // tr,),
        in_specs=[pl.BlockSpec((n, tr, cols), lambda i: (0, i, 0)), blk(), blk(), blk()],
        out_specs=[blk(), blk(), blk(), blk()],
        out_shape=[jax.ShapeDtypeStruct((rows, cols), F32)] * 4,
        compiler_params=_params(("arbitrary",)),
    )(parts, w, m, v)


ANY = pl.BlockSpec(memory_space=pl.ANY)


def _place():
    return lax.axis_index("x"), lax.axis_index("y"), lax.axis_index("c")


def _allgather(name, arrs):
    n = len(arrs)

    def body(*refs):
        ins, outs = refs[:n], refs[n:2 * n]
        send_sems, recv_sems, local_sems = refs[2 * n:]
        x, y, c = _place()
        me, sibling = (x, y, c), (x, y, 1 - c)
        chips = [(1 - x, y), (x, 1 - y), (1 - x, 1 - y)]

        def copy(a, k, block, to, src=None):
            px, py, pc = block
            dst = outs[a].at[4 * px + 2 * py + pc]
            return pltpu.make_async_remote_copy(
                src_ref=dst if src is None else src, dst_ref=dst,
                send_sem=send_sems.at[a, k], recv_sem=recv_sems.at[a, k], device_id=to, device_id_type=MESH)

        mine = [pltpu.make_async_copy(ins[a], outs[a].at[4 * x + 2 * y + c], local_sems.at[a]) for a in range(n)]
        for cp in mine:
            cp.start()
        first = []
        for a in range(n):
            first.append(copy(a, 0, me, sibling, src=ins[a]))
            first += [copy(a, 1 + j, me, (*chip, c), src=ins[a]) for j, chip in enumerate(chips)]
        for cp in first:
            cp.start()
        passed = []
        for j, chip in enumerate(chips):
            for a in range(n):
                copy(a, 1 + j, (*chip, c), me).wait_recv()
                fwd = copy(a, 4 + j, (*chip, c), sibling)
                fwd.start()
                passed.append(fwd)
        for a in range(n):
            copy(a, 0, sibling, me).wait_recv()
            for j, chip in enumerate(chips):
                copy(a, 4 + j, (*chip, 1 - c), me).wait_recv()
        for cp in first + passed:
            cp.wait_send()
        for cp in mine:
            cp.wait()

    return pl.pallas_call(
        body, name=name,
        in_specs=[ANY] * n, out_specs=[ANY] * n,
        out_shape=[jax.ShapeDtypeStruct((NDEV,) + a.shape, a.dtype) for a in arrs],
        scratch_shapes=[pltpu.SemaphoreType.DMA((n, 7)), pltpu.SemaphoreType.DMA((n, 7)),
                        pltpu.SemaphoreType.DMA((n,))],
    )(*arrs)


def _sibling_exchange(arrs):
    n = len(arrs)

    def body(*refs):
        ins, outs = refs[:n], refs[n:2 * n]
        send_sems, recv_sems = refs[2 * n:]
        x, y, c = _place()
        copies = []
        for a in range(n):
            for q in range(4):
                copies.append(pltpu.make_async_remote_copy(
                    src_ref=ins[a].at[2 * q + 1 - c], dst_ref=outs[a].at[q],
                    send_sem=send_sems.at[a, q], recv_sem=recv_sems.at[a, q],
                    device_id=(x, y, 1 - c), device_id_type=MESH))
        for cp in copies:
            cp.start()
        for cp in copies:
            cp.wait()

    return pl.pallas_call(
        body, name="grad_sibling_exchange",
        in_specs=[ANY] * n, out_specs=[ANY] * n,
        out_shape=[jax.ShapeDtypeStruct((4,) + a.shape[1:], a.dtype) for a in arrs],
        scratch_shapes=[pltpu.SemaphoreType.DMA((n, 4)), pltpu.SemaphoreType.DMA((n, 4))],
    )(*arrs)


def _chip_sum(name, own, got, core):
    _, rows, cols = own.shape
    tr = rows if rows <= 512 else 512

    def body(core_ref, own_ref, got_ref, out_ref):
        out_ref[...] = (own_ref[...] + got_ref[...]).astype(BF16)

    return pl.pallas_call(
        body, name=name,
        grid_spec=pltpu.PrefetchScalarGridSpec(
            num_scalar_prefetch=1, grid=(4, rows // tr),
            in_specs=[pl.BlockSpec((None, tr, cols), lambda q, i, core_ref: (2 * q + core_ref[0], i, 0)),
                      pl.BlockSpec((None, tr, cols), lambda q, i, core_ref: (q, i, 0))],
            out_specs=pl.BlockSpec((None, tr, cols), lambda q, i, core_ref: (q, i, 0))),
        out_shape=jax.ShapeDtypeStruct((4, rows, cols), BF16),
        compiler_params=_params(("arbitrary", "arbitrary")),
    )(core, own, got)


def _chip_exchange(arrs):
    n = len(arrs)

    def body(*refs):
        ins, outs = refs[:n], refs[n:2 * n]
        send_sems, recv_sems, local_sems = refs[2 * n:]
        x, y, c = _place()
        my_chip = 2 * x + y
        chips = [(1 - x, y), (x, 1 - y), (1 - x, 1 - y)]
        mine = [pltpu.make_async_copy(ins[a].at[my_chip], outs[a].at[my_chip], local_sems.at[a]) for a in range(n)]
        for cp in mine:
            cp.start()
        copies = []
        for a in range(n):
            for j, (px, py) in enumerate(chips):
                copies.append(pltpu.make_async_remote_copy(
                    src_ref=ins[a].at[2 * px + py], dst_ref=outs[a].at[my_chip],
                    send_sem=send_sems.at[a, j], recv_sem=recv_sems.at[a, j],
                    device_id=(px, py, c), device_id_type=MESH))
        for cp in copies:
            cp.start()
        for a in range(n):
            for j, (px, py) in enumerate(chips):
                pltpu.make_async_remote_copy(
                    src_ref=ins[a].at[my_chip], dst_ref=outs[a].at[2 * px + py],
                    send_sem=send_sems.at[a, j], recv_sem=recv_sems.at[a, j],
                    device_id=(px, py, c), device_id_type=MESH).wait_recv()
        for cp in copies:
            cp.wait_send()
        for cp in mine:
            cp.wait()

    return pl.pallas_call(
        body, name="grad_chip_exchange",
        in_specs=[ANY] * n, out_specs=[ANY] * n,
        out_shape=[jax.ShapeDtypeStruct(a.shape, a.dtype) for a in arrs],
        scratch_shapes=[pltpu.SemaphoreType.DMA((n, 3)), pltpu.SemaphoreType.DMA((n, 3)),
                        pltpu.SemaphoreType.DMA((n,))],
    )(*arrs)


def _block_diag(w):
    w4 = w.reshape(NCB, 4, 64, 64)
    eye = jnp.eye(4, dtype=w.dtype)
    return (w4[:, :, :, None, :] * eye[None, :, None, :, None]).reshape(NCB, CB, CB)


def _block_diag_back(g):
    g5 = g.reshape(NCB, 4, 64, 4, 64)
    return jnp.stack([g5[:, m, :, m, :] for m in range(4)], axis=1).reshape(16, 64, 64)


def _local_grads(x, tgt, wg, norm_in, conv_w, conv_b, gate_x_w, gate_x_b, gate_a_w, gate_a_b, lru_lambda,
                 gn_gain, norm_final):
    nb = x.shape[0]
    t = nb * S
    x2d = x.reshape(t, D)
    tgt2d = tgt.reshape(t, D)
    wbd = jnp.concatenate([_block_diag(gate_x_w), _block_diag(gate_a_w)], axis=-1).astype(BF16)
    gain3 = gn_gain.reshape(HEADS, 1, DK)
    tables = _retention_tables()

    proj, h = _inproj(x2d, norm_in, wg)
    ya, hs = _lru_fwd(proj, conv_w, conv_b, wbd, gate_x_b, gate_a_b, lru_lambda, nb)
    yb, qr, kr, o, rs = _ret_fwd(proj, gain3, tables, nb)
    dx2, dya, dyb, dpc, merged, doa, dob, g_fin, loss_vec = _tail(ya, yb, proj, x2d, tgt2d, wg, norm_final)
    g_pa, g_pb, g_out = _tail_wgrad(ya, yb, merged, doa, dob, dx2)
    dpa, g_wbd, g_vec = _lru_bwd(proj, hs, dya, conv_w, conv_b, wbd, gate_x_b, gate_a_b, lru_lambda, nb)
    dpb, g_gain = _ret_bwd(proj, qr, kr, o, rs, dyb, gain3, tables, nb)
    g_in = _inproj_wgrad(h, dpa, dpb, dpc)
    grad_x, g_norm_in = _inproj_dgrad(dpa, dpb, dpc, wg, x2d, dx2, norm_in)

    small = jnp.concatenate([
        g_norm_in, g_vec[0:1], g_vec[1:2], g_vec[2:3], g_vec[3:4], g_fin,
        g_vec[4:8],
        g_gain.reshape(1, D),
        jnp.zeros((5, D), F32),
        _block_diag_back(g_wbd[:, :, :CB]).reshape(64, D),
        _block_diag_back(g_wbd[:, :, CB:]).reshape(64, D),
    ], axis=0)
    return jnp.sum(loss_vec), grad_x.reshape(nb, S, D), g_in, g_pa, g_pb, g_out, small


def kernel(x, norm_in, w_in, conv_w, conv_b, gate_x_w, gate_x_b, gate_a_w, gate_a_b, lru_lambda, gn_gain, w_proj_a, w_proj_b, w_out, norm_final, loss_target, m_norm_in, m_w_in, m_conv_w, m_conv_b, m_gate_x_w, m_gate_x_b, m_gate_a_w, m_gate_a_b, m_lru_lambda, m_gn_gain, m_w_proj_a, m_w_proj_b, m_w_out, m_norm_final, v_norm_in, v_w_in, v_conv_w, v_conv_b, v_gate_x_w, v_gate_x_b, v_gate_a_w, v_gate_a_b, v_lru_lambda, v_gn_gain, v_w_proj_a, v_w_proj_b, v_w_out, v_norm_final):
    xi, yi, ci = _place()
    me = 4 * xi + 2 * yi + ci
    nshard = D // NDEV

    slab = jnp.concatenate([w_in[0], w_proj_a[0], w_proj_b[0], w_out[0]], axis=0).astype(BF16)
    tiny = jnp.concatenate([conv_w[0], jnp.pad(gn_gain[0], ((0, 0), (0, nshard - DK // NDEV)))], axis=0)
    wg, tiny_g = _allgather("weight_allgather", [slab, tiny])
    conv_w_full = tiny_g[:, 0:4, :].transpose(1, 0, 2).reshape(4, D)
    gain_full = tiny_g[:, 4:8, :DK // NDEV].transpose(1, 0, 2).reshape(HEADS, DK)

    loss_part, grad_x, g_in, g_pa, g_pb, g_out, small = _local_grads(
        x, loss_target, wg, norm_in, conv_w_full, conv_b, gate_x_w[0], gate_x_b, gate_a_w[0], gate_a_b,
        lru_lambda, gain_full, norm_final.reshape(1, D))
    loss = lax.psum(loss_part, AXES)

    own = [g_in, g_pa.reshape(NDEV, nshard, D), g_pb.reshape(NDEV, nshard, D), g_out.reshape(NDEV, nshard, D)]
    got = _sibling_exchange(own)
    core = ci.astype(jnp.int32).reshape(1)
    sums = [_chip_sum("chip_sum_%d" % k, own[k], got[k], core) for k in range(4)]
    parts = _chip_exchange(sums)

    (small_all,) = _allgather("small_grad_allgather", [small])

    big = [("w_in", w_in, m_w_in, v_w_in), ("w_proj_a", w_proj_a, m_w_proj_a, v_w_proj_a),
           ("w_proj_b", w_proj_b, m_w_proj_b, v_w_proj_b), ("w_out", w_out, m_w_out, v_w_out)]
    res = {}
    for k, (nm, w, m, v) in enumerate(big):
        out = _adamw("adamw_" + nm, parts[k], w[0], m[0], v[0])
        res[nm] = [o[None] for o in out]

    def pack(p):
        return jnp.concatenate([
            p["norm_in"], p["conv_b"], p["gate_x_b"], p["gate_a_b"], p["lru_lambda"], p["norm_final"].reshape(1, D),
            jnp.zeros((10, D), F32), p["gate_x_w"].reshape(64, D), p["gate_a_w"].reshape(64, D)], axis=0)

    names = ["norm_in", "conv_b", "gate_x_b", "gate_a_b", "lru_lambda", "norm_final", "gate_x_w", "gate_a_w"]
    ws = dict(norm_in=norm_in, conv_b=conv_b, gate_x_b=gate_x_b, gate_a_b=gate_a_b, lru_lambda=lru_lambda,
              norm_final=norm_final, gate_x_w=gate_x_w, gate_a_w=gate_a_w)
    ms = dict(norm_in=m_norm_in, conv_b=m_conv_b, gate_x_b=m_gate_x_b, gate_a_b=m_gate_a_b, lru_lambda=m_lru_lambda,
              norm_final=m_norm_final, gate_x_w=m_gate_x_w, gate_a_w=m_gate_a_w)
    vs = dict(norm_in=v_norm_in, conv_b=v_conv_b, gate_x_b=v_gate_x_b, gate_a_b=v_gate_a_b, lru_lambda=v_lru_lambda,
              norm_final=v_norm_final, gate_x_w=v_gate_x_w, gate_a_w=v_gate_a_w)
    packed = _adamw("adamw_small", small_all, pack(ws), pack(ms), pack(vs))
    for nm in names:
        shape = ws[nm].shape
        if nm in ("gate_x_w", "gate_a_w"):
            lo = 16 if nm == "gate_x_w" else 80
            res[nm] = [o[lo:lo + 64].reshape(shape) for o in packed]
        else:
            r = names.index(nm)
            res[nm] = [o[r:r + 1].reshape(shape) for o in packed]

    g_small = packed[0]
    g_conv = lax.dynamic_slice(g_small[6:10], (0, me * nshard), (4, nshard))
    g_gain = lax.dynamic_slice(g_small[10:11].reshape(HEADS, DK), (0, me * (DK // NDEV)), (HEADS, DK // NDEV))
    pad = lambda a: jnp.pad(a, ((0, 0), (0, nshard - DK // NDEV)))
    shard = _adamw("adamw_shard",
                   jnp.concatenate([g_conv, pad(g_gain)], axis=0)[None],
                   jnp.concatenate([conv_w[0], pad(gn_gain[0])], axis=0),
                   jnp.concatenate([m_conv_w[0], pad(m_gn_gain[0])], axis=0),
                   jnp.concatenate([v_conv_w[0], pad(v_gn_gain[0])], axis=0))
    res["conv_w"] = [o[0:4][None] for o in shard]
    res["gn_gain"] = [o[4:8, :DK // NDEV][None] for o in shard]

    order = ["norm_in", "w_in", "conv_w", "conv_b", "gate_x_w", "gate_x_b", "gate_a_w", "gate_a_b", "lru_lambda",
             "gn_gain", "w_proj_a", "w_proj_b", "w_out", "norm_final"]
    outs = [loss, grad_x]
    for k in range(4):
        outs += [res[nm][k] for nm in order]
    return tuple(outs)
```

```python
import functools

import jax
import jax.numpy as jnp
from jax import lax
from jax.experimental import pallas as pl
from jax.experimental.pallas import tpu as pltpu

F32 = jnp.float32
BF16 = jnp.bfloat16
MESH = pl.DeviceIdType.MESH
AXES = ("x", "y", "c")

D = 1024
S = 2048
NSEG = 8
NDEV = 8
HEADS = 4
DK = 256
CH = 256
NCH = S // CH
CB = 256
NCB = D // CB
RC = 128
EPS = 1e-6
LRU_C = 8.0
WROWS = D + 3 * (D // NDEV)
SMALL_ROWS = 144
VMEM_LIMIT = 56 * 1024 * 1024

ADAM_LR = 0.001
ADAM_B1 = 0.9
ADAM_B2 = 0.999
ADAM_EPS = 1e-08
ADAM_WD = 0.01
ADAM_STEP = 10


def _params(sem=None):
    return pltpu.CompilerParams(dimension_semantics=sem, vmem_limit_bytes=VMEM_LIMIT)


def _dot(a, b):
    return jnp.dot(a, b, preferred_element_type=F32)


def _dot_nt(a, b):
    return lax.dot_general(a, b, (((1,), (1,)), ((), ())), preferred_element_type=F32)


def _dot_tn(a, b):
    return lax.dot_general(a, b, (((0,), (0,)), ((), ())), preferred_element_type=F32)


def _sigmoid(x):
    return jax.nn.sigmoid(x)


def _expm1_nonpos(x):
    poly = x * (1.0 + x * (0.5 + x * (1.0 / 6.0 + x * (1.0 / 24.0))))
    return jnp.where(x > -0.05, poly, jnp.exp(x) - 1.0)


def _softplus(x):
    return jnp.maximum(x, 0.0) + jnp.log(1.0 + jnp.exp(-jnp.abs(x)))


def _rows(c, n):
    return pl.ds(pl.multiple_of(c * n, n), n)


def _window_before(ref, c, n):
    r0 = c * n
    prev = ref[pl.ds(pl.multiple_of(jnp.maximum(r0 - 8, 0), 8), 8), :]
    prev = jnp.where(c > 0, prev, 0.0)
    return jnp.concatenate([prev, ref[_rows(c, n), :]], axis=0)


def _shift_down(win, s, n):
    if s == 0:
        return win[8:, :]
    return pltpu.roll(win, s, 0)[8:, :]


def _shift_up(win, s, n):
    if s == 0:
        return win[:n, :]
    return pltpu.roll(win, n + 8 - s, 0)[:n, :]


def _inproj(x2d, g_in, wg):
    t = x2d.shape[0]
    tm = 1024

    def body(x_ref, g_ref, w_ref, proj_ref, h_ref):
        @pl.when(pl.program_id(1) == 0)
        def _():
            x = x_ref[...]
            r = lax.rsqrt(jnp.mean(x * x, axis=-1, keepdims=True) + EPS)
            h_ref[...] = (x * r * g_ref[...]).astype(BF16)

        proj_ref[...] = _dot(h_ref[...], w_ref[...])

    return pl.pallas_call(
        body, name="inproj", grid=(t // tm, NSEG),
        in_specs=[pl.BlockSpec((tm, D), lambda i, j: (i, 0)),
                  pl.BlockSpec((1, D), lambda i, j: (0, 0)),
                  pl.BlockSpec((None, D, D), lambda i, j: (j, 0, 0))],
        out_specs=[pl.BlockSpec((None, tm, D), lambda i, j: (j, i, 0)),
                   pl.BlockSpec((tm, D), lambda i, j: (i, 0))],
        out_shape=[jax.ShapeDtypeStruct((NSEG, t, D), F32), jax.ShapeDtypeStruct((t, D), BF16)],
        compiler_params=_params(("arbitrary", "arbitrary")),
    )(x2d, g_in, wg)


def _tile_scan(a, u):
    row = lax.broadcasted_iota(jnp.int32, a.shape, 0)
    for d in (1, 2, 4):
        m = row >= d
        a_sh = pltpu.roll(a, d, 0)
        u_sh = pltpu.roll(u, d, 0)
        u = jnp.where(m, a * u_sh + u, u)
        a = jnp.where(m, a * a_sh, a)
    return a, u


def _tile_scan_rev(a, w):
    row = lax.broadcasted_iota(jnp.int32, a.shape, 0)
    for d in (1, 2, 4):
        m = row < 8 - d
        a_sh = pltpu.roll(a, 8 - d, 0)
        w_sh = pltpu.roll(w, 8 - d, 0)
        w = jnp.where(m, a * w_sh + w, w)
        a = jnp.where(m, a * a_sh, a)
    return a, w


def _lru_gates(xa_ref, c, cw_ref, cb_ref, wbd_ref, bx_ref, ba_ref, sp):
    win = _window_before(xa_ref, c, RC)
    xc = cb_ref[...] + cw_ref[3:4, :] * _shift_down(win, 0, RC)
    for s in (1, 2, 3):
        xc = xc + cw_ref[3 - s:4 - s, :] * _shift_down(win, s, RC)
    z = _dot(xc.astype(BF16), wbd_ref[...])
    gi = _sigmoid(z[:, :CB] + bx_ref[...])
    gr = _sigmoid(z[:, CB:] + ba_ref[...])
    log_a = -LRU_C * gr * sp
    return win, xc, gi, gr, log_a


def _lru_fwd(proj, conv_w, conv_b, wbd, bx, ba, lam, nb):
    t = nb * S

    def body(xa_ref, ga_ref, cw_ref, cb_ref, wbd_ref, bx_ref, ba_ref, lam_ref, ya_ref, hs_ref, a_s, u_s):
        sp = _softplus(-lam_ref[...])

        def gates(c, carry):
            _, xc, gi, _, log_a = _lru_gates(xa_ref, c, cw_ref, cb_ref, wbd_ref, bx_ref, ba_ref, sp)
            a_s[_rows(c, RC), :] = jnp.exp(log_a)
            u_s[_rows(c, RC), :] = jnp.sqrt(-_expm1_nonpos(2.0 * log_a)) * (gi * xc)
            return carry

        lax.fori_loop(0, S // RC, gates, 0)

        def scan(k, h):
            a_cum, u_cum = _tile_scan(a_s[_rows(k, 8), :], u_s[_rows(k, 8), :])
            h_tile = u_cum + a_cum * h
            hs_ref[_rows(k, 8), :] = h_tile
            return h_tile[7:8, :]

        lax.fori_loop(0, S // 8, scan, jnp.zeros((1, CB), F32), unroll=4)

        def gate_out(c, carry):
            ga = ga_ref[_rows(c, RC), :]
            ya_ref[_rows(c, RC), :] = (ga * _sigmoid(ga) * hs_ref[_rows(c, RC), :]).astype(BF16)
            return carry

        lax.fori_loop(0, S // RC, gate_out, 0)

    vec = pl.BlockSpec((1, CB), lambda b, cb: (0, cb))
    return pl.pallas_call(
        body, name="lru_fwd", grid=(nb, NCB),
        in_specs=[pl.BlockSpec((None, S, CB), lambda b, cb: (0, b, cb)),
                  pl.BlockSpec((None, S, CB), lambda b, cb: (1, b, cb)),
                  pl.BlockSpec((4, CB), lambda b, cb: (0, cb)),
                  vec,
                  pl.BlockSpec((None, CB, 2 * CB), lambda b, cb: (cb, 0, 0)),
                  vec, vec, vec],
        out_specs=[pl.BlockSpec((S, CB), lambda b, cb: (b, cb)),
                   pl.BlockSpec((S, CB), lambda b, cb: (b, cb))],
        out_shape=[jax.ShapeDtypeStruct((t, D), BF16), jax.ShapeDtypeStruct((t, D), F32)],
        scratch_shapes=[pltpu.VMEM((S, CB), F32), pltpu.VMEM((S, CB), F32)],
        compiler_params=_params(("arbitrary", "arbitrary")),
    )(proj, proj, conv_w, conv_b, wbd, bx, ba, lam)


def _lru_bwd(proj, hs, dya, conv_w, conv_b, wbd, bx, ba, lam, nb):
    t = nb * S

    def body(xa_ref, ga_ref, hs_ref, dya_ref, cw_ref, cb_ref, wbd_ref, bx_ref, ba_ref, lam_ref,
             dp_ref, dwbd_ref, vec_ref, a_s, xc_s, gi_s, gr_s, dh_s, dxc_s, acc_s):
        b = pl.program_id(1)
        lam_v = lam_ref[...]
        sp = _softplus(-lam_v)
        acc_s[...] = jnp.zeros_like(acc_s)

        @pl.when(b == 0)
        def _():
            dwbd_ref[...] = jnp.zeros_like(dwbd_ref)
            vec_ref[...] = jnp.zeros_like(vec_ref)

        def gates(c, carry):
            _, xc, gi, gr, log_a = _lru_gates(xa_ref, c, cw_ref, cb_ref, wbd_ref, bx_ref, ba_ref, sp)
            rows = _rows(c, RC)
            a_s[rows, :] = jnp.exp(log_a)
            xc_s[rows, :] = xc
            gi_s[rows, :] = gi
            gr_s[rows, :] = gr
            ga = ga_ref[rows, :]
            sg = _sigmoid(ga)
            dya_c = dya_ref[rows, :]
            dh_s[rows, :] = dya_c * (ga * sg)
            dp_ref[1, rows, :] = (dya_c * hs_ref[rows, :] * (sg * (1.0 + ga * (1.0 - sg)))).astype(BF16)
            return carry

        lax.fori_loop(0, S // RC, gates, 0)

        def scan(i, g_in):
            k = S // 8 - 1 - i
            a = a_s[_rows(k, 8), :]
            dl = dh_s[_rows(k, 8), :]
            a_cum, g_loc = _tile_scan_rev(a, a * dl)
            g = g_loc + a_cum * g_in
            row = lax.broadcasted_iota(jnp.int32, a.shape, 0)
            dh_s[_rows(k, 8), :] = dl + jnp.where(row < 7, pltpu.roll(g, 7, 0), g_in)
            return g[0:1, :]

        lax.fori_loop(0, S // 8, scan, jnp.zeros((1, CB), F32), unroll=4)

        dxc_s[pl.ds(S, 8), :] = jnp.zeros((8, CB), F32)

        def grads(c, carry):
            rows = _rows(c, RC)
            dh = dh_s[rows, :]
            h_prev = _shift_down(_window_before(hs_ref, c, RC), 1, RC)
            xc, gi, gr, a = xc_s[rows, :], gi_s[rows, :], gr_s[rows, :], a_s[rows, :]
            mult = jnp.sqrt(-_expm1_nonpos(-2.0 * LRU_C * gr * sp))
            dmult = dh * gi * xc
            d_log_a = dh * h_prev * a - dmult * (a * a) / mult
            dzi = dh * mult * xc * gi * (1.0 - gi)
            dzr = d_log_a * (-LRU_C * sp) * gr * (1.0 - gr)
            dz = jnp.concatenate([dzi, dzr], axis=1).astype(BF16)
            dxc_s[rows, :] = dh * mult * gi + _dot_nt(dz, wbd_ref[...])
            dwbd_ref[...] += _dot_tn(xc.astype(BF16), dz)
            acc_s[1:2, :] += jnp.sum(dzi, axis=0, keepdims=True)
            acc_s[2:3, :] += jnp.sum(dzr, axis=0, keepdims=True)
            acc_s[3:4, :] += jnp.sum(d_log_a * (-LRU_C * gr), axis=0, keepdims=True)
            return carry

        lax.fori_loop(0, S // RC, grads, 0)

        def conv_bwd(c, carry):
            rows = _rows(c, RC)
            dwin = dxc_s[pl.ds(pl.multiple_of(c * RC, RC), RC + 8), :]
            dxc = dwin[:RC, :]
            xwin = _window_before(xa_ref, c, RC)
            dxa = cw_ref[3:4, :] * dxc
            acc_s[0:1, :] += jnp.sum(dxc, axis=0, keepdims=True)
            acc_s[7:8, :] += jnp.sum(dxc * _shift_down(xwin, 0, RC), axis=0, keepdims=True)
            for s in (1, 2, 3):
                dxa = dxa + cw_ref[3 - s:4 - s, :] * _shift_up(dwin, s, RC)
                acc_s[7 - s:8 - s, :] += jnp.sum(dxc * _shift_down(xwin, s, RC), axis=0, keepdims=True)
            dp_ref[0, rows, :] = dxa.astype(BF16)
            return carry

        lax.fori_loop(0, S // RC, conv_bwd, 0)

        row = lax.broadcasted_iota(jnp.int32, acc_s.shape, 0)
        vec_ref[...] += jnp.where(row == 3, acc_s[...] * (-_sigmoid(-lam_v)), acc_s[...])

    vec = pl.BlockSpec((1, CB), lambda cb, b: (0, cb))
    blk = pl.BlockSpec((S, CB), lambda cb, b: (b, cb))
    return pl.pallas_call(
        body, name="lru_bwd", grid=(NCB, nb),
        in_specs=[pl.BlockSpec((None, S, CB), lambda cb, b: (0, b, cb)),
                  pl.BlockSpec((None, S, CB), lambda cb, b: (1, b, cb)),
                  blk, blk,
                  pl.BlockSpec((4, CB), lambda cb, b: (0, cb)),
                  vec,
                  pl.BlockSpec((None, CB, 2 * CB), lambda cb, b: (cb, 0, 0)),
                  vec, vec, vec],
        out_specs=[pl.BlockSpec((2, S, CB), lambda cb, b: (0, b, cb)),
                   pl.BlockSpec((None, CB, 2 * CB), lambda cb, b: (cb, 0, 0)),
                   pl.BlockSpec((8, CB), lambda cb, b: (0, cb))],
        out_shape=[jax.ShapeDtypeStruct((2, t, D), BF16),
                   jax.ShapeDtypeStruct((NCB, CB, 2 * CB), F32),
                   jax.ShapeDtypeStruct((8, D), F32)],
        scratch_shapes=[pltpu.VMEM((S, CB), F32), pltpu.VMEM((S, CB), F32), pltpu.VMEM((S, CB), F32),
                        pltpu.VMEM((S, CB), F32), pltpu.VMEM((S, CB), F32), pltpu.VMEM((S + 8, CB), F32),
                        pltpu.VMEM((8, CB), F32)],
        compiler_params=_params(("arbitrary", "arbitrary")),
    )(proj, proj, hs, dya, conv_w, conv_b, wbd, bx, ba, lam)


def _retention_tables():
    log_g = jnp.log1p(-(2.0 ** (-5.0 - jnp.arange(HEADS, dtype=F32))))
    idx = jnp.arange(CH, dtype=F32)
    diff = idx[:, None] - idx[None, :]
    inner = jnp.where(diff >= 0, jnp.exp(jnp.maximum(diff, 0.0)[None] * log_g[:, None, None]), 0.0)
    cross = jnp.exp((idx[None, :] + 1.0) * log_g[:, None])
    state = jnp.exp((CH - 1.0 - idx[None, :]) * log_g[:, None])
    cross = jnp.broadcast_to(cross[:, :, None], (HEADS, CH, DK))
    state = jnp.broadcast_to(state[:, :, None], (HEADS, CH, DK))
    half = DK // 2
    freqs = 10000.0 ** (-jnp.arange(half, dtype=F32) / half)
    ang = jnp.arange(S, dtype=F32)[:, None] * freqs[None, :]
    return inner, cross, state, jnp.cos(ang), jnp.sin(ang)


def _rotate(x, cos, sin):
    half = DK // 2
    x1, x2 = x[:, :half], x[:, half:]
    return jnp.concatenate([x1 * cos - x2 * sin, x1 * sin + x2 * cos], axis=1)


def _rotate_back(d, cos, sin):
    half = DK // 2
    d1, d2 = d[:, :half], d[:, half:]
    return jnp.concatenate([d1 * cos + d2 * sin, d2 * cos - d1 * sin], axis=1)


def _ret_fwd(proj, gain, tables, nb):
    t = nb * S
    inner_t, cross_t, state_t, cos_t, sin_t = tables

    def body(q_ref, k_ref, v_ref, gb_ref, gain_ref, dm_ref, cd_ref, sd_ref, cos_ref, sin_ref,
             yb_ref, qr_ref, kr_ref, o_ref, rs_ref, r_s):
        r_s[...] = jnp.zeros_like(r_s)
        chunk_decay = cd_ref[CH - 1:CH, :]

        def chunk(c, carry):
            rows = _rows(c, CH)
            cos, sin = cos_ref[rows, :], sin_ref[rows, :]
            qr = _rotate(q_ref[rows, :], cos, sin).astype(BF16)
            kr = (_rotate(k_ref[rows, :], cos, sin) * (DK ** -0.5)).astype(BF16)
            v = v_ref[rows, :]
            qr_ref[rows, :] = qr
            kr_ref[rows, :] = kr
            r = r_s[...]
            rb = r.astype(BF16)
            rs_ref[c] = rb
            p = (_dot_nt(qr, kr) * dm_ref[...]).astype(BF16)
            o = _dot(p, v.astype(BF16)) + _dot(qr, rb) * cd_ref[...]
            r_s[...] = chunk_decay * r + _dot_tn(kr, (v * sd_ref[...]).astype(BF16))
            o_ref[rows, :] = o
            oc = o - jnp.mean(o, axis=-1, keepdims=True)
            rstd = lax.rsqrt(jnp.mean(oc * oc, axis=-1, keepdims=True) + EPS)
            gb = gb_ref[rows, :]
            yb_ref[rows, :] = (gb * _sigmoid(gb) * (oc * rstd * gain_ref[...])).astype(BF16)
            return carry

        lax.fori_loop(0, NCH, chunk, 0)

    seg = lambda s: pl.BlockSpec((None, S, DK), lambda b, h: (s, b, h))
    tab = pl.BlockSpec((None, CH, DK), lambda b, h: (h, 0, 0))
    rot = pl.BlockSpec((S, DK // 2), lambda b, h: (0, 0))
    blk = pl.BlockSpec((S, DK), lambda b, h: (b, h))
    return pl.pallas_call(
        body, name="ret_fwd", grid=(nb, HEADS),
        in_specs=[seg(2), seg(3), seg(4), seg(5),
                  pl.BlockSpec((None, 1, DK), lambda b, h: (h, 0, 0)),
                  tab, tab, tab, rot, rot],
        out_specs=[blk, blk, blk, blk,
                   pl.BlockSpec((None, None, NCH, DK, DK), lambda b, h: (b, h, 0, 0, 0))],
        out_shape=[jax.ShapeDtypeStruct((t, D), BF16), jax.ShapeDtypeStruct((t, D), BF16),
                   jax.ShapeDtypeStruct((t, D), BF16), jax.ShapeDtypeStruct((t, D), F32),
                   jax.ShapeDtypeStruct((nb, HEADS, NCH, DK, DK), BF16)],
        scratch_shapes=[pltpu.VMEM((DK, DK), F32)],
        compiler_params=_params(("arbitrary", "arbitrary")),
    )(proj, proj, proj, proj, gain, inner_t, cross_t, state_t, cos_t, sin_t)


def _ret_bwd(proj, qr, kr, o, rs, dyb, gain, tables, nb):
    t = nb * S
    inner_t, cross_t, state_t, cos_t, sin_t = tables

    def body(qr_ref, kr_ref, v_ref, gb_ref, o_ref, dyb_ref, rs_ref, gain_ref, dm_ref, cd_ref, sd_ref,
             cos_ref, sin_ref, dp_ref, dgain_ref, dr_s):
        dr_s[...] = jnp.zeros_like(dr_s)
        chunk_decay = cd_ref[CH - 1:CH, :]

        @pl.when(pl.program_id(1) == 0)
        def _():
            dgain_ref[...] = jnp.zeros_like(dgain_ref)

        def chunk(i, carry):
            c = NCH - 1 - i
            rows = _rows(c, CH)
            gain_v = gain_ref[...]
            o_c = o_ref[rows, :]
            oc = o_c - jnp.mean(o_c, axis=-1, keepdims=True)
            rstd = lax.rsqrt(jnp.mean(oc * oc, axis=-1, keepdims=True) + EPS)
            yn = oc * rstd
            gb = gb_ref[rows, :]
            sg = _sigmoid(gb)
            dyb_c = dyb_ref[rows, :]
            dgn = dyb_c * (gb * sg)
            dp_ref[3, rows, :] = (dyb_c * (yn * gain_v) * (sg * (1.0 + gb * (1.0 - sg)))).astype(BF16)
            dgain_ref[...] += jnp.sum(dgn * yn, axis=0, keepdims=True)
            dyn = dgn * gain_v
            do = rstd * (dyn - jnp.mean(dyn, axis=-1, keepdims=True)
                         - yn * jnp.mean(dyn * yn, axis=-1, keepdims=True))
            dob = do.astype(BF16)
            dox = (do * cd_ref[...]).astype(BF16)

            q_c, k_c = qr_ref[rows, :], kr_ref[rows, :]
            v = v_ref[rows, :]
            vb = v.astype(BF16)
            vs = (v * sd_ref[...]).astype(BF16)
            rb = rs_ref[c]
            d_r = dr_s[...]
            drb = d_r.astype(BF16)
            dm = dm_ref[...]
            p = (_dot_nt(q_c, k_c) * dm).astype(BF16)
            dpm = (_dot_nt(dob, vb) * dm).astype(BF16)
            dq = _dot(dpm, k_c) + _dot_nt(dox, rb)
            dk = _dot_tn(dpm, q_c) + _dot_nt(vs, drb)
            dv = _dot_tn(p, dob) + _dot(k_c, drb) * sd_ref[...]
            dr_s[...] = chunk_decay * d_r + _dot_tn(q_c, dox)

            cos, sin = cos_ref[rows, :], sin_ref[rows, :]
            dp_ref[0, rows, :] = _rotate_back(dq, cos, sin).astype(BF16)
            dp_ref[1, rows, :] = (_rotate_back(dk, cos, sin) * (DK ** -0.5)).astype(BF16)
            dp_ref[2, rows, :] = dv.astype(BF16)
            return carry

        lax.fori_loop(0, NCH, chunk, 0)

    seg = lambda s: pl.BlockSpec((None, S, DK), lambda h, b: (s, b, h))
    tab = pl.BlockSpec((None, CH, DK), lambda h, b: (h, 0, 0))
    rot = pl.BlockSpec((S, DK // 2), lambda h, b: (0, 0))
    blk = pl.BlockSpec((S, DK), lambda h, b: (b, h))
    one = pl.BlockSpec((None, 1, DK), lambda h, b: (h, 0, 0))
    return pl.pallas_call(
        body, name="ret_bwd", grid=(HEADS, nb),
        in_specs=[blk, blk, seg(4), seg(5), blk, blk,
                  pl.BlockSpec((None, None, NCH, DK, DK), lambda h, b: (b, h, 0, 0, 0)),
                  one, tab, tab, tab, rot, rot],
        out_specs=[pl.BlockSpec((4, S, DK), lambda h, b: (0, b, h)), one],
        out_shape=[jax.ShapeDtypeStruct((4, t, D), BF16), jax.ShapeDtypeStruct((HEADS, 1, DK), F32)],
        scratch_shapes=[pltpu.VMEM((DK, DK), F32)],
        compiler_params=_params(("arbitrary", "arbitrary")),
    )(qr, kr, proj, proj, o, dyb, rs, gain, inner_t, cross_t, state_t, cos_t, sin_t)


def _wblock(k):
    return pl.BlockSpec((NDEV, D // NDEV, D), lambda i: (0, D // (D // NDEV) + k, 0))


def _tail(ya, yb, proj, x2d, tgt, wg, g_fin):
    t = x2d.shape[0]
    tm = 256

    def body(ya_ref, yb_ref, ma_ref, mb_ref, x_ref, t_ref, wa_ref, wb_ref, wo_ref, g_ref,
             dx2_ref, dya_ref, dyb_ref, dm_ref, mg_ref, doa_ref, dob_ref, gfin_ref, loss_ref):
        i = pl.program_id(0)

        @pl.when(i == 0)
        def _():
            gfin_ref[...] = jnp.zeros_like(gfin_ref)
            loss_ref[...] = jnp.zeros_like(loss_ref)

        wa = wa_ref[...].reshape(D, D)
        wb = wb_ref[...].reshape(D, D)
        wo = wo_ref[...].reshape(D, D)
        out_a = _dot(ya_ref[...], wa)
        out_b = _dot(yb_ref[...], wb)
        sa = _sigmoid(ma_ref[...])
        sb = _sigmoid(mb_ref[...])
        merged = (sa * out_a + sb * out_b).astype(BF16)
        mg_ref[...] = merged
        x2 = x_ref[...] + _dot(merged, wo)
        r2 = lax.rsqrt(jnp.mean(x2 * x2, axis=-1, keepdims=True) + EPS)
        xh = x2 * r2
        g = g_ref[...]
        err = xh * g - t_ref[...]
        loss_ref[...] += jnp.sum(err * err, axis=0, keepdims=True) * (0.5 / D)
        dy = err * (1.0 / D)
        gfin_ref[...] += jnp.sum(dy * xh, axis=0, keepdims=True)
        dxh = dy * g
        dx2 = r2 * (dxh - xh * jnp.mean(dxh * xh, axis=-1, keepdims=True))
        dx2_ref[...] = dx2
        dmerged = _dot_nt(dx2.astype(BF16), wo)
        doa = (sa * dmerged).astype(BF16)
        dob = (sb * dmerged).astype(BF16)
        doa_ref[...] = doa
        dob_ref[...] = dob
        dm_ref[0] = (dmerged * out_a * sa * (1.0 - sa)).astype(BF16)
        dm_ref[1] = (dmerged * out_b * sb * (1.0 - sb)).astype(BF16)
        dya_ref[...] = _dot_nt(doa, wa)
        dyb_ref[...] = _dot_nt(dob, wb)

    row = lambda: pl.BlockSpec((tm, D), lambda i: (i, 0))
    seg = lambda s: pl.BlockSpec((None, tm, D), lambda i: (s, i, 0))
    vec = pl.BlockSpec((1, D), lambda i: (0, 0))
    return pl.pallas_call(
        body, name="tail", grid=(t // tm,),
        in_specs=[row(), row(), seg(6), seg(7), row(), row(), _wblock(0), _wblock(1), _wblock(2), vec],
        out_specs=[row(), row(), row(), pl.BlockSpec((2, tm, D), lambda i: (0, i, 0)),
                   row(), row(), row(), vec, vec],
        out_shape=[jax.ShapeDtypeStruct((t, D), F32), jax.ShapeDtypeStruct((t, D), F32),
                   jax.ShapeDtypeStruct((t, D), F32), jax.ShapeDtypeStruct((2, t, D), BF16),
                   jax.ShapeDtypeStruct((t, D), BF16), jax.ShapeDtypeStruct((t, D), BF16),
                   jax.ShapeDtypeStruct((t, D), BF16), jax.ShapeDtypeStruct((1, D), F32),
                   jax.ShapeDtypeStruct((1, D), F32)],
        compiler_params=_params(("arbitrary",)),
    )(ya, yb, proj, proj, x2d, tgt, wg, wg, wg, g_fin)


def _tail_wgrad(ya, yb, merged, doa, dob, dx2):
    t = ya.shape[0]
    tm = 512

    def body(ya_ref, yb_ref, mg_ref, doa_ref, dob_ref, dx2_ref, ga_ref, gb_ref, go_ref):
        @pl.when(pl.program_id(0) == 0)
        def _():
            ga_ref[...] = jnp.zeros_like(ga_ref)
            gb_ref[...] = jnp.zeros_like(gb_ref)
            go_ref[...] = jnp.zeros_like(go_ref)

        ga_ref[...] += _dot_tn(ya_ref[...], doa_ref[...])
        gb_ref[...] += _dot_tn(yb_ref[...], dob_ref[...])
        go_ref[...] += _dot_tn(mg_ref[...], dx2_ref[...].astype(BF16))

    row = lambda: pl.BlockSpec((tm, D), lambda i: (i, 0))
    full = lambda: pl.BlockSpec((D, D), lambda i: (0, 0))
    return pl.pallas_call(
        body, name="tail_wgrad", grid=(t // tm,),
        in_specs=[row() for _ in range(6)], out_specs=[full(), full(), full()],
        out_shape=[jax.ShapeDtypeStruct((D, D), F32)] * 3,
        compiler_params=_params(("arbitrary",)),
    )(ya, yb, merged, doa, dob, dx2)


def _dproj_specs(tm, j_of, i_of):
    last = lambda j, i, lo, n: (jnp.clip(j - lo, 0, n - 1), i, 0)
    return [pl.BlockSpec((None, tm, D), lambda a, b: last(j_of(a, b), i_of(a, b), 0, 2)),
            pl.BlockSpec((None, tm, D), lambda a, b: last(j_of(a, b), i_of(a, b), 2, 4)),
            pl.BlockSpec((None, tm, D), lambda a, b: last(j_of(a, b), i_of(a, b), 6, 2))]


def _dproj_specs_ordered(tm):
    def spec(lo, n):
        def index(k, i, order_ref):
            seg = order_ref[k]
            mine = jnp.logical_and(seg >= lo, seg < lo + n)
            return jnp.where(mine, seg - lo, 0), jnp.where(mine, i, 0), 0
        return pl.BlockSpec((None, tm, D), index)
    return [spec(0, 2), spec(2, 4), spec(6, 2)]


def _dproj_pick(j, da_ref, db_ref, dc_ref, use):
    @pl.when(j < 2)
    def _():
        use(da_ref[...])

    @pl.when(jnp.logical_and(j >= 2, j < 6))
    def _():
        use(db_ref[...])

    @pl.when(j >= 6)
    def _():
        use(dc_ref[...])


def _rs_schedule(q, c):
    steps = []
    for s in range(3):
        d_a = lax.rem(q + 1 + s, 4)
        d_b = lax.rem(q + 1 + (s + 1) % 3, 4)
        steps.append((jnp.where(c == 0, d_a, d_b), jnp.where(c == 0, d_b, d_a)))
    steps.append((q, q))
    return steps


def _rs_order(q, c):
    order = []
    for keep, give in _rs_schedule(q, c):
        order += [2 * give + 1 - c, 2 * keep + c]
    return jnp.stack(order).astype(jnp.int32)


def _inproj_wgrad_rs(h, dpa, dpb, dpc, order):
    t = h.shape[0]
    tm = 1024
    nt = t // tm

    def body(order_ref, h_ref, da_ref, db_ref, dc_ref, parts_ref, acc, sib, outb,
             give_send, give_recv, sum_send, sum_recv, own_sem):
        k, i = pl.program_id(0), pl.program_id(1)
        x, y, c = _place()
        schedule = _rs_schedule(2 * x + y, c)

        def use(d):
            @pl.when(i == 0)
            def _():
                acc[k % 2] = _dot_tn(h_ref[...], d)

            @pl.when(i > 0)
            def _():
                acc[k % 2] += _dot_tn(h_ref[...], d)

        _dproj_pick(order_ref[k], da_ref, db_ref, dc_ref, use)

        def give_copy(s):
            return pltpu.make_async_remote_copy(
                src_ref=acc.at[0], dst_ref=sib.at[s % 2], send_sem=give_send.at[s], recv_sem=give_recv.at[s],
                device_id=(x, y, 1 - c), device_id_type=MESH)

        def sum_copy(s):
            keep = schedule[s][0]
            return pltpu.make_async_remote_copy(
                src_ref=outb.at[s], dst_ref=parts_ref.at[s], send_sem=sum_send.at[s], recv_sem=sum_recv.at[s],
                device_id=(keep // 2, lax.rem(keep, 2), c), device_id_type=MESH)

        own_copy = pltpu.make_async_copy(outb.at[3], parts_ref.at[3], own_sem)

        for s in range(4):
            @pl.when(jnp.logical_and(k == 2 * s, i == nt - 1))
            def _():
                give_copy(s).start()

            @pl.when(jnp.logical_and(k == 2 * s + 1, i == nt - 1))
            def _():
                give_copy(s).wait_recv()
                outb[s] = (acc[1] + sib[s % 2]).astype(BF16)
                give_copy(s).wait_send()
                if s < 3:
                    sum_copy(s).start()
                else:
                    own_copy.start()

        @pl.when(jnp.logical_and(k == NSEG - 1, i == nt - 1))
        def _():
            for s in range(3):
                sum_copy(s).wait_recv()
            for s in range(3):
                sum_copy(s).wait_send()
            own_copy.wait()

    return pl.pallas_call(
        body, name="inproj_wgrad_rs",
        grid_spec=pltpu.PrefetchScalarGridSpec(
            num_scalar_prefetch=1, grid=(NSEG, nt),
            in_specs=[pl.BlockSpec((tm, D), lambda k, i, order_ref: (i, 0))] + _dproj_specs_ordered(tm),
            out_specs=ANY,
            scratch_shapes=[pltpu.VMEM((2, D, D), F32), pltpu.VMEM((2, D, D), F32), pltpu.VMEM((4, D, D), BF16),
                            pltpu.SemaphoreType.DMA((4,)), pltpu.SemaphoreType.DMA((4,)),
                            pltpu.SemaphoreType.DMA((3,)), pltpu.SemaphoreType.DMA((3,)),
                            pltpu.SemaphoreType.DMA]),
        out_shape=jax.ShapeDtypeStruct((4, D, D), BF16),
        compiler_params=_params(("arbitrary", "arbitrary")),
    )(order, h, dpa, dpb, dpc)


def _inproj_dgrad(dpa, dpb, dpc, wg, x2d, dx2, g_in):
    t = x2d.shape[0]
    tm = 512

    def body(da_ref, db_ref, dc_ref, w_ref, x_ref, dx2_ref, g_ref, gx_ref, gg_ref, acc_s):
        i, j = pl.program_id(0), pl.program_id(1)

        @pl.when(jnp.logical_and(i == 0, j == 0))
        def _():
            gg_ref[...] = jnp.zeros_like(gg_ref)

        @pl.when(j == 0)
        def _():
            acc_s[...] = jnp.zeros_like(acc_s)

        def use(d):
            acc_s[...] += _dot_nt(d, w_ref[...])

        _dproj_pick(j, da_ref, db_ref, dc_ref, use)

        @pl.when(j == NSEG - 1)
        def _():
            x = x_ref[...]
            r = lax.rsqrt(jnp.mean(x * x, axis=-1, keepdims=True) + EPS)
            xh = x * r
            dh = acc_s[...]
            gg_ref[...] += jnp.sum(dh * xh, axis=0, keepdims=True)
            dxh = dh * g_ref[...]
            gx_ref[...] = dx2_ref[...] + r * (dxh - xh * jnp.mean(dxh * xh, axis=-1, keepdims=True))

    row = lambda: pl.BlockSpec((tm, D), lambda i, j: (i, 0))
    vec = pl.BlockSpec((1, D), lambda i, j: (0, 0))
    return pl.pallas_call(
        body, name="inproj_dgrad", grid=(t // tm, NSEG),
        in_specs=_dproj_specs(tm, lambda i, j: j, lambda i, j: i)
        + [pl.BlockSpec((None, D, D), lambda i, j: (j, 0, 0)), row(), row(), vec],
        out_specs=[row(), vec],
        out_shape=[jax.ShapeDtypeStruct((t, D), F32), jax.ShapeDtypeStruct((1, D), F32)],
        scratch_shapes=[pltpu.VMEM((tm, D), F32)],
        compiler_params=_params(("arbitrary", "arbitrary")),
    )(dpa, dpb, dpc, wg, x2d, dx2, g_in)


def _adamw(name, parts, w, m, v):
    n, rows, cols = parts.shape
    tr = rows if rows <= 256 else 256

    def body(p_ref, w_ref, m_ref, v_ref, g_ref, d_ref, nm_ref, nv_ref):
        g = p_ref[0].astype(F32)
        for k in range(1, n):
            g = g + p_ref[k].astype(F32)
        m_new = ADAM_B1 * m_ref[...] + (1.0 - ADAM_B1) * g
        v_new = ADAM_B2 * v_ref[...] + (1.0 - ADAM_B2) * (g * g)
        m_hat = m_new / (1.0 - ADAM_B1 ** ADAM_STEP)
        v_hat = v_new / (1.0 - ADAM_B2 ** ADAM_STEP)
        g_ref[...] = g
        d_ref[...] = -ADAM_LR * (m_hat / (jnp.sqrt(v_hat) + ADAM_EPS) + ADAM_WD * w_ref[...])
        nm_ref[...] = m_new
        nv_ref[...] = v_new

    blk = lambda: pl.BlockSpec((tr, cols), lambda i: (i, 0))
    return pl.pallas_call(
        body, name=name, grid=(rows // tr,),
        in_specs=[pl.BlockSpec((n, tr, cols), lambda i: (0, i, 0)), blk(), blk(), blk()],
        out_specs=[blk(), blk(), blk(), blk()],
        out_shape=[jax.ShapeDtypeStruct((rows, cols), F32)] * 4,
        compiler_params=_params(("arbitrary",)),
    )(parts, w, m, v)


ANY = pl.BlockSpec(memory_space=pl.ANY)


def _place():
    return lax.axis_index("x"), lax.axis_index("y"), lax.axis_index("c")


def _allgather(name, arrs):
    n = len(arrs)

    def body(*refs):
        ins, outs = refs[:n], refs[n:2 * n]
        send_sems, recv_sems, local_sems = refs[2 * n:]
        x, y, c = _place()
        me, sibling = (x, y, c), (x, y, 1 - c)
        chips = [(1 - x, y), (x, 1 - y), (1 - x, 1 - y)]

        def copy(a, k, block, to, src=None):
            px, py, pc = block
            dst = outs[a].at[4 * px + 2 * py + pc]
            return pltpu.make_async_remote_copy(
                src_ref=dst if src is None else src, dst_ref=dst,
                send_sem=send_sems.at[a, k], recv_sem=recv_sems.at[a, k], device_id=to, device_id_type=MESH)

        mine = [pltpu.make_async_copy(ins[a], outs[a].at[4 * x + 2 * y + c], local_sems.at[a]) for a in range(n)]
        for cp in mine:
            cp.start()
        first = []
        for a in range(n):
            first.append(copy(a, 0, me, sibling, src=ins[a]))
            first += [copy(a, 1 + j, me, (*chip, c), src=ins[a]) for j, chip in enumerate(chips)]
        for cp in first:
            cp.start()
        passed = []
        for j, chip in enumerate(chips):
            for a in range(n):
                copy(a, 1 + j, (*chip, c), me).wait_recv()
                fwd = copy(a, 4 + j, (*chip, c), sibling)
                fwd.start()
                passed.append(fwd)
        for a in range(n):
            copy(a, 0, sibling, me).wait_recv()
            for j, chip in enumerate(chips):
                copy(a, 4 + j, (*chip, 1 - c), me).wait_recv()
        for cp in first + passed:
            cp.wait_send()
        for cp in mine:
            cp.wait()

    return pl.pallas_call(
        body, name=name,
        in_specs=[ANY] * n, out_specs=[ANY] * n,
        out_shape=[jax.ShapeDtypeStruct((NDEV,) + a.shape, a.dtype) for a in arrs],
        scratch_shapes=[pltpu.SemaphoreType.DMA((n, 7)), pltpu.SemaphoreType.DMA((n, 7)),
                        pltpu.SemaphoreType.DMA((n,))],
    )(*arrs)


def _sibling_exchange(arrs):
    n = len(arrs)

    def body(*refs):
        ins, outs = refs[:n], refs[n:2 * n]
        send_sems, recv_sems = refs[2 * n:]
        x, y, c = _place()
        copies = []
        for a in range(n):
            for q in range(4):
                copies.append(pltpu.make_async_remote_copy(
                    src_ref=ins[a].at[2 * q + 1 - c], dst_ref=outs[a].at[q],
                    send_sem=send_sems.at[a, q], recv_sem=recv_sems.at[a, q],
                    device_id=(x, y, 1 - c), device_id_type=MESH))
        for cp in copies:
            cp.start()
        for cp in copies:
            cp.wait()

    return pl.pallas_call(
        body, name="grad_sibling_exchange",
        in_specs=[ANY] * n, out_specs=[ANY] * n,
        out_shape=[jax.ShapeDtypeStruct((4,) + a.shape[1:], a.dtype) for a in arrs],
        scratch_shapes=[pltpu.SemaphoreType.DMA((n, 4)), pltpu.SemaphoreType.DMA((n, 4))],
    )(*arrs)


def _chip_sum(name, own, got, core):
    _, rows, cols = own.shape
    tr = rows if rows <= 512 else 512

    def body(core_ref, own_ref, got_ref, out_ref):
        out_ref[...] = (own_ref[...] + got_ref[...]).astype(BF16)

    return pl.pallas_call(
        body, name=name,
        grid_spec=pltpu.PrefetchScalarGridSpec(
            num_scalar_prefetch=1, grid=(4, rows // tr),
            in_specs=[pl.BlockSpec((None, tr, cols), lambda q, i, core_ref: (2 * q + core_ref[0], i, 0)),
                      pl.BlockSpec((None, tr, cols), lambda q, i, core_ref: (q, i, 0))],
            out_specs=pl.BlockSpec((None, tr, cols), lambda q, i, core_ref: (q, i, 0))),
        out_shape=jax.ShapeDtypeStruct((4, rows, cols), BF16),
        compiler_params=_params(("arbitrary", "arbitrary")),
    )(core, own, got)


def _chip_exchange(arrs):
    n = len(arrs)

    def body(*refs):
        ins, outs = refs[:n], refs[n:2 * n]
        send_sems, recv_sems, local_sems = refs[2 * n:]
        x, y, c = _place()
        my_chip = 2 * x + y
        chips = [(1 - x, y), (x, 1 - y), (1 - x, 1 - y)]
        mine = [pltpu.make_async_copy(ins[a].at[my_chip], outs[a].at[my_chip], local_sems.at[a]) for a in range(n)]
        for cp in mine:
            cp.start()
        copies = []
        for a in range(n):
            for j, (px, py) in enumerate(chips):
                copies.append(pltpu.make_async_remote_copy(
                    src_ref=ins[a].at[2 * px + py], dst_ref=outs[a].at[my_chip],
                    send_sem=send_sems.at[a, j], recv_sem=recv_sems.at[a, j],
                    device_id=(px, py, c), device_id_type=MESH))
        for cp in copies:
            cp.start()
        for a in range(n):
            for j, (px, py) in enumerate(chips):
                pltpu.make_async_remote_copy(
                    src_ref=ins[a].at[my_chip], dst_ref=outs[a].at[2 * px + py],
                    send_sem=send_sems.at[a, j], recv_sem=recv_sems.at[a, j],
                    device_id=(px, py, c), device_id_type=MESH).wait_recv()
        for cp in copies:
            cp.wait_send()
        for cp in mine:
            cp.wait()

    return pl.pallas_call(
        body, name="grad_chip_exchange",
        in_specs=[ANY] * n, out_specs=[ANY] * n,
        out_shape=[jax.ShapeDtypeStruct(a.shape, a.dtype) for a in arrs],
        scratch_shapes=[pltpu.SemaphoreType.DMA((n, 3)), pltpu.SemaphoreType.DMA((n, 3)),
                        pltpu.SemaphoreType.DMA((n,))],
    )(*arrs)


def _block_diag(w):
    w4 = w.reshape(NCB, 4, 64, 64)
    eye = jnp.eye(4, dtype=w.dtype)
    return (w4[:, :, :, None, :] * eye[None, :, None, :, None]).reshape(NCB, CB, CB)


def _block_diag_back(g):
    g5 = g.reshape(NCB, 4, 64, 4, 64)
    return jnp.stack([g5[:, m, :, m, :] for m in range(4)], axis=1).reshape(16, 64, 64)


def _local_grads(x, tgt, wg, norm_in, conv_w, conv_b, gate_x_w, gate_x_b, gate_a_w, gate_a_b, lru_lambda,
                 gn_gain, norm_final, order, core):
    nb = x.shape[0]
    t = nb * S
    nshard = D // NDEV
    x2d = x.reshape(t, D)
    tgt2d = tgt.reshape(t, D)
    wbd = jnp.concatenate([_block_diag(gate_x_w), _block_diag(gate_a_w)], axis=-1).astype(BF16)
    gain3 = gn_gain.reshape(HEADS, 1, DK)
    tables = _retention_tables()

    proj, h = _inproj(x2d, norm_in, wg)
    ya, hs = _lru_fwd(proj, conv_w, conv_b, wbd, gate_x_b, gate_a_b, lru_lambda, nb)
    yb, qr, kr, o, rs = _ret_fwd(proj, gain3, tables, nb)
    dx2, dya, dyb, dpc, merged, doa, dob, g_fin, loss_vec = _tail(ya, yb, proj, x2d, tgt2d, wg, norm_final)
    g_pa, g_pb, g_out = _tail_wgrad(ya, yb, merged, doa, dob, dx2)

    own = [g.reshape(NDEV, nshard, D) for g in (g_pa, g_pb, g_out)]
    got = _sibling_exchange(own)
    sums = [_chip_sum("chip_sum_%d" % k, own[k], got[k], core) for k in range(3)]
    parts = _chip_exchange(sums)

    dpa, g_wbd, g_vec = _lru_bwd(proj, hs, dya, conv_w, conv_b, wbd, gate_x_b, gate_a_b, lru_lambda, nb)
    dpb, g_gain = _ret_bwd(proj, qr, kr, o, rs, dyb, gain3, tables, nb)
    parts_in = _inproj_wgrad_rs(h, dpa, dpb, dpc, order)
    grad_x, g_norm_in = _inproj_dgrad(dpa, dpb, dpc, wg, x2d, dx2, norm_in)

    small = jnp.concatenate([
        g_norm_in, g_vec[0:1], g_vec[1:2], g_vec[2:3], g_vec[3:4], g_fin,
        g_vec[4:8],
        g_gain.reshape(1, D),
        jnp.zeros((5, D), F32),
        _block_diag_back(g_wbd[:, :, :CB]).reshape(64, D),
        _block_diag_back(g_wbd[:, :, CB:]).reshape(64, D),
    ], axis=0)
    return jnp.sum(loss_vec), grad_x.reshape(nb, S, D), [parts_in] + list(parts), small


def kernel(x, norm_in, w_in, conv_w, conv_b, gate_x_w, gate_x_b, gate_a_w, gate_a_b, lru_lambda, gn_gain, w_proj_a, w_proj_b, w_out, norm_final, loss_target, m_norm_in, m_w_in, m_conv_w, m_conv_b, m_gate_x_w, m_gate_x_b, m_gate_a_w, m_gate_a_b, m_lru_lambda, m_gn_gain, m_w_proj_a, m_w_proj_b, m_w_out, m_norm_final, v_norm_in, v_w_in, v_conv_w, v_conv_b, v_gate_x_w, v_gate_x_b, v_gate_a_w, v_gate_a_b, v_lru_lambda, v_gn_gain, v_w_proj_a, v_w_proj_b, v_w_out, v_norm_final):
    xi, yi, ci = _place()
    me = 4 * xi + 2 * yi + ci
    nshard = D // NDEV

    slab = jnp.concatenate([w_in[0], w_proj_a[0], w_proj_b[0], w_out[0]], axis=0).astype(BF16)
    tiny = jnp.concatenate([conv_w[0], jnp.pad(gn_gain[0], ((0, 0), (0, nshard - DK // NDEV)))], axis=0)
    wg, tiny_g = _allgather("weight_allgather", [slab, tiny])
    conv_w_full = tiny_g[:, 0:4, :].transpose(1, 0, 2).reshape(4, D)
    gain_full = tiny_g[:, 4:8, :DK // NDEV].transpose(1, 0, 2).reshape(HEADS, DK)

    loss_part, grad_x, parts, small = _local_grads(
        x, loss_target, wg, norm_in, conv_w_full, conv_b, gate_x_w[0], gate_x_b, gate_a_w[0], gate_a_b,
        lru_lambda, gain_full, norm_final.reshape(1, D),
        _rs_order(2 * xi + yi, ci), ci.astype(jnp.int32).reshape(1))
    loss = lax.psum(loss_part, AXES)

    (small_all,) = _allgather("small_grad_allgather", [small])

    big = [("w_in", w_in, m_w_in, v_w_in), ("w_proj_a", w_proj_a, m_w_proj_a, v_w_proj_a),
           ("w_proj_b", w_proj_b, m_w_proj_b, v_w_proj_b), ("w_out", w_out, m_w_out, v_w_out)]
    res = {}
    for k, (nm, w, m, v) in enumerate(big):
        out = _adamw("adamw_" + nm, parts[k], w[0], m[0], v[0])
        res[nm] = [o[None] for o in out]

    def pack(p):
        return jnp.concatenate([
            p["norm_in"], p["conv_b"], p["gate_x_b"], p["gate_a_b"], p["lru_lambda"], p["norm_final"].reshape(1, D),
            jnp.zeros((10, D), F32), p["gate_x_w"].reshape(64, D), p["gate_a_w"].reshape(64, D)], axis=0)

    names = ["norm_in", "conv_b", "gate_x_b", "gate_a_b", "lru_lambda", "norm_final", "gate_x_w", "gate_a_w"]
    ws = dict(norm_in=norm_in, conv_b=conv_b, gate_x_b=gate_x_b, gate_a_b=gate_a_b, lru_lambda=lru_lambda,
              norm_final=norm_final, gate_x_w=gate_x_w, gate_a_w=gate_a_w)
    ms = dict(norm_in=m_norm_in, conv_b=m_conv_b, gate_x_b=m_gate_x_b, gate_a_b=m_gate_a_b, lru_lambda=m_lru_lambda,
              norm_final=m_norm_final, gate_x_w=m_gate_x_w, gate_a_w=m_gate_a_w)
    vs = dict(norm_in=v_norm_in, conv_b=v_conv_b, gate_x_b=v_gate_x_b, gate_a_b=v_gate_a_b, lru_lambda=v_lru_lambda,
              norm_final=v_norm_final, gate_x_w=v_gate_x_w, gate_a_w=v_gate_a_w)
    packed = _adamw("adamw_small", small_all, pack(ws), pack(ms), pack(vs))
    for nm in names:
        shape = ws[nm].shape
        if nm in ("gate_x_w", "gate_a_w"):
            lo = 16 if nm == "gate_x_w" else 80
            res[nm] = [o[lo:lo + 64].reshape(shape) for o in packed]
        else:
            r = names.index(nm)
            res[nm] = [o[r:r + 1].reshape(shape) for o in packed]

    g_small = packed[0]
    g_conv = lax.dynamic_slice(g_small[6:10], (0, me * nshard), (4, nshard))
    g_gain = lax.dynamic_slice(g_small[10:11].reshape(HEADS, DK), (0, me * (DK // NDEV)), (HEADS, DK // NDEV))
    pad = lambda a: jnp.pad(a, ((0, 0), (0, nshard - DK // NDEV)))
    shard = _adamw("adamw_shard",
                   jnp.concatenate([g_conv, pad(g_gain)], axis=0)[None],
                   jnp.concatenate([conv_w[0], pad(gn_gain[0])], axis=0),
                   jnp.concatenate([m_conv_w[0], pad(m_gn_gain[0])], axis=0),
                   jnp.concatenate([v_conv_w[0], pad(v_gn_gain[0])], axis=0))
    res["conv_w"] = [o[0:4][None] for o in shard]
    res["gn_gain"] = [o[4:8, :DK // NDEV][None] for o in shard]

    order = ["norm_in", "w_in", "conv_w", "conv_b", "gate_x_w", "gate_x_b", "gate_a_w", "gate_a_b", "lru_lambda",
             "gn_gain", "w_proj_a", "w_proj_b", "w_out", "norm_final"]
    outs = [loss, grad_x]
    for k in range(4):
        outs += [res[nm][k] for nm in order]
    return tuple(outs)
```

```python
import functools

import jax
import jax.numpy as jnp
from jax import lax
from jax.experimental import pallas as pl
from jax.experimental.pallas import tpu as pltpu

F32 = jnp.float32
BF16 = jnp.bfloat16
MESH = pl.DeviceIdType.MESH
AXES = ("x", "y", "c")

D = 1024
S = 2048
NSEG = 8
NDEV = 8
HEADS = 4
DK = 256
CH = 256
NCH = S // CH
CB = 256
NCB = D // CB
RC = 128
EPS = 1e-6
LRU_C = 8.0
SMALL_ROWS = 144
VMEM_LIMIT = 56 * 1024 * 1024

ADAM_LR = 0.001
ADAM_B1 = 0.9
ADAM_B2 = 0.999
ADAM_EPS = 1e-08
ADAM_WD = 0.01
ADAM_STEP = 10


def _params(sem=None):
    return pltpu.CompilerParams(dimension_semantics=sem, vmem_limit_bytes=VMEM_LIMIT)


def _dot(a, b):
    return jnp.dot(a, b, preferred_element_type=F32)


def _dot_nt(a, b):
    return lax.dot_general(a, b, (((1,), (1,)), ((), ())), preferred_element_type=F32)


def _dot_tn(a, b):
    return lax.dot_general(a, b, (((0,), (0,)), ((), ())), preferred_element_type=F32)


def _sigmoid(x):
    return jax.nn.sigmoid(x)


def _expm1_nonpos(x):
    poly = x * (1.0 + x * (0.5 + x * (1.0 / 6.0 + x * (1.0 / 24.0))))
    return jnp.where(x > -0.05, poly, jnp.exp(x) - 1.0)


def _softplus(x):
    return jnp.maximum(x, 0.0) + jnp.log(1.0 + jnp.exp(-jnp.abs(x)))


def _rows(c, n):
    return pl.ds(pl.multiple_of(c * n, n), n)


def _window_before(ref, c, n):
    r0 = c * n
    prev = ref[pl.ds(pl.multiple_of(jnp.maximum(r0 - 8, 0), 8), 8), :]
    prev = jnp.where(c > 0, prev, 0.0)
    return jnp.concatenate([prev, ref[_rows(c, n), :]], axis=0)


def _shift_down(win, s, n):
    if s == 0:
        return win[8:, :]
    return pltpu.roll(win, s, 0)[8:, :]


def _shift_up(win, s, n):
    if s == 0:
        return win[:n, :]
    return pltpu.roll(win, n + 8 - s, 0)[:n, :]


def _gather_order(x, y, c):
    chips = [(1 - x, y), (x, 1 - y), (1 - x, 1 - y)]
    order = [4 * x + 2 * y + c, 4 * x + 2 * y + 1 - c]
    order += [4 * px + 2 * py + c for px, py in chips] + [4 * px + 2 * py + 1 - c for px, py in chips]
    return jnp.stack(order).astype(jnp.int32)


def _inproj_gather(x2d, g_in, w_own, wp_own, tiny_own, order):
    t = x2d.shape[0]
    tm = 1024
    nt = t // tm

    def body(order_ref, x_ref, g_ref, w_own_ref, wp_own_ref, tiny_own_ref,
             proj_ref, h_ref, wg_ref, wpg_ref, tinyg_ref,
             w_all, h_all, send_sems, recv_sems, own_sems, out_sems):
        k, i = pl.program_id(0), pl.program_id(1)
        x, y, c = _place()
        me, sibling = (x, y, c), (x, y, 1 - c)
        chips = [(1 - x, y), (x, 1 - y), (1 - x, 1 - y)]
        srcs = [w_own_ref, wp_own_ref, tiny_own_ref]
        dsts = [w_all, wpg_ref, tinyg_ref]

        def copy(a, n, block, to, own_src=False):
            px, py, pc = block
            dst = dsts[a].at[4 * px + 2 * py + pc]
            return pltpu.make_async_remote_copy(
                src_ref=srcs[a] if own_src else dst, dst_ref=dst,
                send_sem=send_sems.at[a, n], recv_sem=recv_sems.at[a, n], device_id=to, device_id_type=MESH)

        def own_copy(a):
            return pltpu.make_async_copy(srcs[a], dsts[a].at[4 * x + 2 * y + c], own_sems.at[a])

        def keep_copy(n):
            return pltpu.make_async_copy(w_all.at[order_ref[n]], wg_ref.at[order_ref[n]], out_sems.at[n])

        def first_copies(a):
            return [copy(a, 0, me, sibling, True)] + [copy(a, 1 + j, me, (*chip, c), True) for j, chip in enumerate(chips)]

        def at_slot(n):
            return jnp.logical_and(k == n, i == 0)

        @pl.when(at_slot(0))
        def _():
            for a in range(3):
                own_copy(a).start()
            for a in range(3):
                for cp in first_copies(a):
                    cp.start()
            own_copy(0).wait()
            keep_copy(0).start()

        @pl.when(at_slot(1))
        def _():
            copy(0, 0, sibling, me).wait_recv()
            keep_copy(1).start()

        for j, chip in enumerate(chips):
            @pl.when(at_slot(2 + j))
            def _():
                copy(0, 1 + j, (*chip, c), me).wait_recv()
                copy(0, 4 + j, (*chip, c), sibling).start()
                keep_copy(2 + j).start()

            @pl.when(at_slot(5 + j))
            def _():
                copy(0, 4 + j, (*chip, 1 - c), me).wait_recv()
                keep_copy(5 + j).start()

        rows = pl.ds(pl.multiple_of(i * tm, tm), tm)

        @pl.when(k == 0)
        def _():
            xv = x_ref[...]
            r = lax.rsqrt(jnp.mean(xv * xv, axis=-1, keepdims=True) + EPS)
            hv = (xv * r * g_ref[...]).astype(BF16)
            h_ref[...] = hv
            h_all[rows, :] = hv

        proj_ref[...] = _dot(h_all[rows, :], w_all[order_ref[k]])

        @pl.when(jnp.logical_and(k == NSEG - 1, i == nt - 1))
        def _():
            for a in (1, 2):
                for j, chip in enumerate(chips):
                    copy(a, 1 + j, (*chip, c), me).wait_recv()
                    copy(a, 4 + j, (*chip, c), sibling).start()
            for a in (1, 2):
                copy(a, 0, sibling, me).wait_recv()
                for j, chip in enumerate(chips):
                    copy(a, 4 + j, (*chip, 1 - c), me).wait_recv()
            for a in range(3):
                for cp in first_copies(a):
                    cp.wait_send()
                for j, chip in enumerate(chips):
                    copy(a, 4 + j, (*chip, c), sibling).wait_send()
            for a in (1, 2):
                own_copy(a).wait()
            for n in range(NSEG):
                keep_copy(n).wait()

    hold = lambda k, i, order_ref: (jnp.where(k == 0, i, nt - 1), 0)
    return pl.pallas_call(
        body, name="inproj_gather",
        grid_spec=pltpu.PrefetchScalarGridSpec(
            num_scalar_prefetch=1, grid=(NSEG, nt),
            in_specs=[pl.BlockSpec((tm, D), hold),
                      pl.BlockSpec((1, D), lambda k, i, order_ref: (0, 0)),
                      ANY, ANY, ANY],
            out_specs=[pl.BlockSpec((None, tm, D), lambda k, i, order_ref: (order_ref[k], i, 0)),
                       pl.BlockSpec((tm, D), hold),
                       ANY, ANY, ANY],
            scratch_shapes=[pltpu.VMEM((NDEV, D, D), BF16), pltpu.VMEM((t, D), BF16),
                            pltpu.SemaphoreType.DMA((3, 7)), pltpu.SemaphoreType.DMA((3, 7)),
                            pltpu.SemaphoreType.DMA((3,)), pltpu.SemaphoreType.DMA((NSEG,))]),
        out_shape=[jax.ShapeDtypeStruct((NSEG, t, D), F32), jax.ShapeDtypeStruct((t, D), BF16),
                   jax.ShapeDtypeStruct((NDEV,) + w_own.shape, BF16),
                   jax.ShapeDtypeStruct((NDEV,) + wp_own.shape, BF16),
                   jax.ShapeDtypeStruct((NDEV,) + tiny_own.shape, F32)],
        compiler_params=_params(("arbitrary", "arbitrary")),
    )(order, x2d, g_in, w_own, wp_own, tiny_own)


def _tile_scan(a, u):
    row = lax.broadcasted_iota(jnp.int32, a.shape, 0)
    for d in (1, 2, 4):
        m = row >= d
        a_sh = pltpu.roll(a, d, 0)
        u_sh = pltpu.roll(u, d, 0)
        u = jnp.where(m, a * u_sh + u, u)
        a = jnp.where(m, a * a_sh, a)
    return a, u


def _tile_scan_rev(a, w):
    row = lax.broadcasted_iota(jnp.int32, a.shape, 0)
    for d in (1, 2, 4):
        m = row < 8 - d
        a_sh = pltpu.roll(a, 8 - d, 0)
        w_sh = pltpu.roll(w, 8 - d, 0)
        w = jnp.where(m, a * w_sh + w, w)
        a = jnp.where(m, a * a_sh, a)
    return a, w


def _lru_gates(xa_ref, c, cw_ref, cb_ref, wbd_ref, bx_ref, ba_ref, sp):
    win = _window_before(xa_ref, c, RC)
    xc = cb_ref[...] + cw_ref[3:4, :] * _shift_down(win, 0, RC)
    for s in (1, 2, 3):
        xc = xc + cw_ref[3 - s:4 - s, :] * _shift_down(win, s, RC)
    z = _dot(xc.astype(BF16), wbd_ref[...])
    gi = _sigmoid(z[:, :CB] + bx_ref[...])
    gr = _sigmoid(z[:, CB:] + ba_ref[...])
    log_a = -LRU_C * gr * sp
    return win, xc, gi, gr, log_a


def _lru_fwd(proj, conv_w, conv_b, wbd, bx, ba, lam, nb):
    t = nb * S

    def body(xa_ref, ga_ref, cw_ref, cb_ref, wbd_ref, bx_ref, ba_ref, lam_ref, ya_ref, hs_ref, a_s, u_s):
        sp = _softplus(-lam_ref[...])

        def gates(c, carry):
            _, xc, gi, _, log_a = _lru_gates(xa_ref, c, cw_ref, cb_ref, wbd_ref, bx_ref, ba_ref, sp)
            a_s[_rows(c, RC), :] = jnp.exp(log_a)
            u_s[_rows(c, RC), :] = jnp.sqrt(-_expm1_nonpos(2.0 * log_a)) * (gi * xc)
            return carry

        lax.fori_loop(0, S // RC, gates, 0)

        def scan(k, h):
            a_cum, u_cum = _tile_scan(a_s[_rows(k, 8), :], u_s[_rows(k, 8), :])
            h_tile = u_cum + a_cum * h
            hs_ref[_rows(k, 8), :] = h_tile
            return h_tile[7:8, :]

        lax.fori_loop(0, S // 8, scan, jnp.zeros((1, CB), F32), unroll=4)

        def gate_out(c, carry):
            ga = ga_ref[_rows(c, RC), :]
            ya_ref[_rows(c, RC), :] = (ga * _sigmoid(ga) * hs_ref[_rows(c, RC), :]).astype(BF16)
            return carry

        lax.fori_loop(0, S // RC, gate_out, 0)

    vec = pl.BlockSpec((1, CB), lambda b, cb: (0, cb))
    return pl.pallas_call(
        body, name="lru_fwd", grid=(nb, NCB),
        in_specs=[pl.BlockSpec((None, S, CB), lambda b, cb: (0, b, cb)),
                  pl.BlockSpec((None, S, CB), lambda b, cb: (1, b, cb)),
                  pl.BlockSpec((4, CB), lambda b, cb: (0, cb)),
                  vec,
                  pl.BlockSpec((None, CB, 2 * CB), lambda b, cb: (cb, 0, 0)),
                  vec, vec, vec],
        out_specs=[pl.BlockSpec((S, CB), lambda b, cb: (b, cb)),
                   pl.BlockSpec((S, CB), lambda b, cb: (b, cb))],
        out_shape=[jax.ShapeDtypeStruct((t, D), BF16), jax.ShapeDtypeStruct((t, D), F32)],
        scratch_shapes=[pltpu.VMEM((S, CB), F32), pltpu.VMEM((S, CB), F32)],
        compiler_params=_params(("arbitrary", "arbitrary")),
    )(proj, proj, conv_w, conv_b, wbd, bx, ba, lam)


def _lru_bwd(proj, hs, dya, conv_w, conv_b, wbd, bx, ba, lam, nb):
    t = nb * S

    def body(xa_ref, ga_ref, hs_ref, dya_ref, cw_ref, cb_ref, wbd_ref, bx_ref, ba_ref, lam_ref,
             dp_ref, dwbd_ref, vec_ref, a_s, xc_s, gi_s, gr_s, dh_s, dxc_s, acc_s):
        b = pl.program_id(1)
        lam_v = lam_ref[...]
        sp = _softplus(-lam_v)
        acc_s[...] = jnp.zeros_like(acc_s)

        @pl.when(b == 0)
        def _():
            dwbd_ref[...] = jnp.zeros_like(dwbd_ref)
            vec_ref[...] = jnp.zeros_like(vec_ref)

        def gates(c, carry):
            _, xc, gi, gr, log_a = _lru_gates(xa_ref, c, cw_ref, cb_ref, wbd_ref, bx_ref, ba_ref, sp)
            rows = _rows(c, RC)
            a_s[rows, :] = jnp.exp(log_a)
            xc_s[rows, :] = xc
            gi_s[rows, :] = gi
            gr_s[rows, :] = gr
            ga = ga_ref[rows, :]
            sg = _sigmoid(ga)
            dya_c = dya_ref[rows, :]
            dh_s[rows, :] = dya_c * (ga * sg)
            dp_ref[1, rows, :] = (dya_c * hs_ref[rows, :] * (sg * (1.0 + ga * (1.0 - sg)))).astype(BF16)
            return carry

        lax.fori_loop(0, S // RC, gates, 0)

        def scan(i, g_in):
            k = S // 8 - 1 - i
            a = a_s[_rows(k, 8), :]
            dl = dh_s[_rows(k, 8), :]
            a_cum, g_loc = _tile_scan_rev(a, a * dl)
            g = g_loc + a_cum * g_in
            row = lax.broadcasted_iota(jnp.int32, a.shape, 0)
            dh_s[_rows(k, 8), :] = dl + jnp.where(row < 7, pltpu.roll(g, 7, 0), g_in)
            return g[0:1, :]

        lax.fori_loop(0, S // 8, scan, jnp.zeros((1, CB), F32), unroll=4)

        dxc_s[pl.ds(S, 8), :] = jnp.zeros((8, CB), F32)

        def grads(c, carry):
            rows = _rows(c, RC)
            dh = dh_s[rows, :]
            h_prev = _shift_down(_window_before(hs_ref, c, RC), 1, RC)
            xc, gi, gr, a = xc_s[rows, :], gi_s[rows, :], gr_s[rows, :], a_s[rows, :]
            mult = jnp.sqrt(-_expm1_nonpos(-2.0 * LRU_C * gr * sp))
            dmult = dh * gi * xc
            d_log_a = dh * h_prev * a - dmult * (a * a) / mult
            dzi = dh * mult * xc * gi * (1.0 - gi)
            dzr = d_log_a * (-LRU_C * sp) * gr * (1.0 - gr)
            dz = jnp.concatenate([dzi, dzr], axis=1).astype(BF16)
            dxc_s[rows, :] = dh * mult * gi + _dot_nt(dz, wbd_ref[...])
            dwbd_ref[...] += _dot_tn(xc.astype(BF16), dz)
            acc_s[1:2, :] += jnp.sum(dzi, axis=0, keepdims=True)
            acc_s[2:3, :] += jnp.sum(dzr, axis=0, keepdims=True)
            acc_s[3:4, :] += jnp.sum(d_log_a * (-LRU_C * gr), axis=0, keepdims=True)
            return carry

        lax.fori_loop(0, S // RC, grads, 0)

        def conv_bwd(c, carry):
            rows = _rows(c, RC)
            dwin = dxc_s[pl.ds(pl.multiple_of(c * RC, RC), RC + 8), :]
            dxc = dwin[:RC, :]
            xwin = _window_before(xa_ref, c, RC)
            dxa = cw_ref[3:4, :] * dxc
            acc_s[0:1, :] += jnp.sum(dxc, axis=0, keepdims=True)
            acc_s[7:8, :] += jnp.sum(dxc * _shift_down(xwin, 0, RC), axis=0, keepdims=True)
            for s in (1, 2, 3):
                dxa = dxa + cw_ref[3 - s:4 - s, :] * _shift_up(dwin, s, RC)
                acc_s[7 - s:8 - s, :] += jnp.sum(dxc * _shift_down(xwin, s, RC), axis=0, keepdims=True)
            dp_ref[0, rows, :] = dxa.astype(BF16)
            return carry

        lax.fori_loop(0, S // RC, conv_bwd, 0)

        row = lax.broadcasted_iota(jnp.int32, acc_s.shape, 0)
        vec_ref[...] += jnp.where(row == 3, acc_s[...] * (-_sigmoid(-lam_v)), acc_s[...])

    vec = pl.BlockSpec((1, CB), lambda cb, b: (0, cb))
    blk = pl.BlockSpec((S, CB), lambda cb, b: (b, cb))
    return pl.pallas_call(
        body, name="lru_bwd", grid=(NCB, nb),
        in_specs=[pl.BlockSpec((None, S, CB), lambda cb, b: (0, b, cb)),
                  pl.BlockSpec((None, S, CB), lambda cb, b: (1, b, cb)),
                  blk, blk,
                  pl.BlockSpec((4, CB), lambda cb, b: (0, cb)),
                  vec,
                  pl.BlockSpec((None, CB, 2 * CB), lambda cb, b: (cb, 0, 0)),
                  vec, vec, vec],
        out_specs=[pl.BlockSpec((2, S, CB), lambda cb, b: (0, b, cb)),
                   pl.BlockSpec((None, CB, 2 * CB), lambda cb, b: (cb, 0, 0)),
                   pl.BlockSpec((8, CB), lambda cb, b: (0, cb))],
        out_shape=[jax.ShapeDtypeStruct((2, t, D), BF16),
                   jax.ShapeDtypeStruct((NCB, CB, 2 * CB), F32),
                   jax.ShapeDtypeStruct((8, D), F32)],
        scratch_shapes=[pltpu.VMEM((S, CB), F32), pltpu.VMEM((S, CB), F32), pltpu.VMEM((S, CB), F32),
                        pltpu.VMEM((S, CB), F32), pltpu.VMEM((S, CB), F32), pltpu.VMEM((S + 8, CB), F32),
                        pltpu.VMEM((8, CB), F32)],
        compiler_params=_params(("arbitrary", "arbitrary")),
    )(proj, proj, hs, dya, conv_w, conv_b, wbd, bx, ba, lam)


def _retention_tables():
    log_g = jnp.log1p(-(2.0 ** (-5.0 - jnp.arange(HEADS, dtype=F32))))
    idx = jnp.arange(CH, dtype=F32)
    diff = idx[:, None] - idx[None, :]
    inner = jnp.where(diff >= 0, jnp.exp(jnp.maximum(diff, 0.0)[None] * log_g[:, None, None]), 0.0)
    cross = jnp.exp((idx[None, :] + 1.0) * log_g[:, None])
    state = jnp.exp((CH - 1.0 - idx[None, :]) * log_g[:, None])
    cross = jnp.broadcast_to(cross[:, :, None], (HEADS, CH, DK))
    state = jnp.broadcast_to(state[:, :, None], (HEADS, CH, DK))
    half = DK // 2
    freqs = 10000.0 ** (-jnp.arange(half, dtype=F32) / half)
    ang = jnp.arange(S, dtype=F32)[:, None] * freqs[None, :]
    return inner, cross, state, jnp.cos(ang), jnp.sin(ang)


def _rotate(x, cos, sin):
    half = DK // 2
    x1, x2 = x[:, :half], x[:, half:]
    return jnp.concatenate([x1 * cos - x2 * sin, x1 * sin + x2 * cos], axis=1)


def _rotate_back(d, cos, sin):
    half = DK // 2
    d1, d2 = d[:, :half], d[:, half:]
    return jnp.concatenate([d1 * cos + d2 * sin, d2 * cos - d1 * sin], axis=1)


def _ret_fwd(proj, gain, tables, nb):
    t = nb * S
    inner_t, cross_t, state_t, cos_t, sin_t = tables

    def body(q_ref, k_ref, v_ref, gb_ref, gain_ref, dm_ref, cd_ref, sd_ref, cos_ref, sin_ref,
             yb_ref, qr_ref, kr_ref, o_ref, rs_ref, r_s):
        r_s[...] = jnp.zeros_like(r_s)
        chunk_decay = cd_ref[CH - 1:CH, :]

        def chunk(c, carry):
            rows = _rows(c, CH)
            cos, sin = cos_ref[rows, :], sin_ref[rows, :]
            qr = _rotate(q_ref[rows, :], cos, sin).astype(BF16)
            kr = (_rotate(k_ref[rows, :], cos, sin) * (DK ** -0.5)).astype(BF16)
            v = v_ref[rows, :]
            qr_ref[rows, :] = qr
            kr_ref[rows, :] = kr
            r = r_s[...]
            rb = r.astype(BF16)
            rs_ref[c] = rb
            p = (_dot_nt(qr, kr) * dm_ref[...]).astype(BF16)
            o = _dot(p, v.astype(BF16)) + _dot(qr, rb) * cd_ref[...]
            r_s[...] = chunk_decay * r + _dot_tn(kr, (v * sd_ref[...]).astype(BF16))
            o_ref[rows, :] = o
            oc = o - jnp.mean(o, axis=-1, keepdims=True)
            rstd = lax.rsqrt(jnp.mean(oc * oc, axis=-1, keepdims=True) + EPS)
            gb = gb_ref[rows, :]
            yb_ref[rows, :] = (gb * _sigmoid(gb) * (oc * rstd * gain_ref[...])).astype(BF16)
            return carry

        lax.fori_loop(0, NCH, chunk, 0)

    seg = lambda s: pl.BlockSpec((None, S, DK), lambda b, h: (s, b, h))
    tab = pl.BlockSpec((None, CH, DK), lambda b, h: (h, 0, 0))
    rot = pl.BlockSpec((S, DK // 2), lambda b, h: (0, 0))
    blk = pl.BlockSpec((S, DK), lambda b, h: (b, h))
    return pl.pallas_call(
        body, name="ret_fwd", grid=(nb, HEADS),
        in_specs=[seg(2), seg(3), seg(4), seg(5),
                  pl.BlockSpec((None, 1, DK), lambda b, h: (h, 0, 0)),
                  tab, tab, tab, rot, rot],
        out_specs=[blk, blk, blk, blk,
                   pl.BlockSpec((None, None, NCH, DK, DK), lambda b, h: (b, h, 0, 0, 0))],
        out_shape=[jax.ShapeDtypeStruct((t, D), BF16), jax.ShapeDtypeStruct((t, D), BF16),
                   jax.ShapeDtypeStruct((t, D), BF16), jax.ShapeDtypeStruct((t, D), F32),
                   jax.ShapeDtypeStruct((nb, HEADS, NCH, DK, DK), BF16)],
        scratch_shapes=[pltpu.VMEM((DK, DK), F32)],
        compiler_params=_params(("arbitrary", "arbitrary")),
    )(proj, proj, proj, proj, gain, inner_t, cross_t, state_t, cos_t, sin_t)


def _ret_bwd(proj, qr, kr, o, rs, dyb, gain, tables, nb):
    t = nb * S
    inner_t, cross_t, state_t, cos_t, sin_t = tables

    def body(qr_ref, kr_ref, v_ref, gb_ref, o_ref, dyb_ref, rs_ref, gain_ref, dm_ref, cd_ref, sd_ref,
             cos_ref, sin_ref, dp_ref, dgain_ref, dr_s):
        dr_s[...] = jnp.zeros_like(dr_s)
        chunk_decay = cd_ref[CH - 1:CH, :]

        @pl.when(pl.program_id(1) == 0)
        def _():
            dgain_ref[...] = jnp.zeros_like(dgain_ref)

        def chunk(i, carry):
            c = NCH - 1 - i
            rows = _rows(c, CH)
            gain_v = gain_ref[...]
            o_c = o_ref[rows, :]
            oc = o_c - jnp.mean(o_c, axis=-1, keepdims=True)
            rstd = lax.rsqrt(jnp.mean(oc * oc, axis=-1, keepdims=True) + EPS)
            yn = oc * rstd
            gb = gb_ref[rows, :]
            sg = _sigmoid(gb)
            dyb_c = dyb_ref[rows, :]
            dgn = dyb_c * (gb * sg)
            dp_ref[3, rows, :] = (dyb_c * (yn * gain_v) * (sg * (1.0 + gb * (1.0 - sg)))).astype(BF16)
            dgain_ref[...] += jnp.sum(dgn * yn, axis=0, keepdims=True)
            dyn = dgn * gain_v
            do = rstd * (dyn - jnp.mean(dyn, axis=-1, keepdims=True)
                         - yn * jnp.mean(dyn * yn, axis=-1, keepdims=True))
            dob = do.astype(BF16)
            dox = (do * cd_ref[...]).astype(BF16)

            q_c, k_c = qr_ref[rows, :], kr_ref[rows, :]
            v = v_ref[rows, :]
            vb = v.astype(BF16)
            vs = (v * sd_ref[...]).astype(BF16)
            rb = rs_ref[c]
            d_r = dr_s[...]
            drb = d_r.astype(BF16)
            dm = dm_ref[...]
            p = (_dot_nt(q_c, k_c) * dm).astype(BF16)
            dpm = (_dot_nt(dob, vb) * dm).astype(BF16)
            dq = _dot(dpm, k_c) + _dot_nt(dox, rb)
            dk = _dot_tn(dpm, q_c) + _dot_nt(vs, drb)
            dv = _dot_tn(p, dob) + _dot(k_c, drb) * sd_ref[...]
            dr_s[...] = chunk_decay * d_r + _dot_tn(q_c, dox)

            cos, sin = cos_ref[rows, :], sin_ref[rows, :]
            dp_ref[0, rows, :] = _rotate_back(dq, cos, sin).astype(BF16)
            dp_ref[1, rows, :] = (_rotate_back(dk, cos, sin) * (DK ** -0.5)).astype(BF16)
            dp_ref[2, rows, :] = dv.astype(BF16)
            return carry

        lax.fori_loop(0, NCH, chunk, 0)

    seg = lambda s: pl.BlockSpec((None, S, DK), lambda h, b: (s, b, h))
    tab = pl.BlockSpec((None, CH, DK), lambda h, b: (h, 0, 0))
    rot = pl.BlockSpec((S, DK // 2), lambda h, b: (0, 0))
    blk = pl.BlockSpec((S, DK), lambda h, b: (b, h))
    one = pl.BlockSpec((None, 1, DK), lambda h, b: (h, 0, 0))
    return pl.pallas_call(
        body, name="ret_bwd", grid=(HEADS, nb),
        in_specs=[blk, blk, seg(4), seg(5), blk, blk,
                  pl.BlockSpec((None, None, NCH, DK, DK), lambda h, b: (b, h, 0, 0, 0)),
                  one, tab, tab, tab, rot, rot],
        out_specs=[pl.BlockSpec((4, S, DK), lambda h, b: (0, b, h)), one],
        out_shape=[jax.ShapeDtypeStruct((4, t, D), BF16), jax.ShapeDtypeStruct((HEADS, 1, DK), F32)],
        scratch_shapes=[pltpu.VMEM((DK, DK), F32)],
        compiler_params=_params(("arbitrary", "arbitrary")),
    )(qr, kr, proj, proj, o, dyb, rs, gain, inner_t, cross_t, state_t, cos_t, sin_t)


def _wblock(k):
    return pl.BlockSpec((NDEV, D // NDEV, D), lambda i: (0, k, 0))


def _tail(ya, yb, proj, x2d, tgt, wg, g_fin):
    t = x2d.shape[0]
    tm = 256

    def body(ya_ref, yb_ref, ma_ref, mb_ref, x_ref, t_ref, wa_ref, wb_ref, wo_ref, g_ref,
             dx2_ref, dya_ref, dyb_ref, dm_ref, mg_ref, doa_ref, dob_ref, gfin_ref, loss_ref):
        i = pl.program_id(0)

        @pl.when(i == 0)
        def _():
            gfin_ref[...] = jnp.zeros_like(gfin_ref)
            loss_ref[...] = jnp.zeros_like(loss_ref)

        wa = wa_ref[...].reshape(D, D)
        wb = wb_ref[...].reshape(D, D)
        wo = wo_ref[...].reshape(D, D)
        out_a = _dot(ya_ref[...], wa)
        out_b = _dot(yb_ref[...], wb)
        sa = _sigmoid(ma_ref[...])
        sb = _sigmoid(mb_ref[...])
        merged = (sa * out_a + sb * out_b).astype(BF16)
        mg_ref[...] = merged
        x2 = x_ref[...] + _dot(merged, wo)
        r2 = lax.rsqrt(jnp.mean(x2 * x2, axis=-1, keepdims=True) + EPS)
        xh = x2 * r2
        g = g_ref[...]
        err = xh * g - t_ref[...]
        loss_ref[...] += jnp.sum(err * err, axis=0, keepdims=True) * (0.5 / D)
        dy = err * (1.0 / D)
        gfin_ref[...] += jnp.sum(dy * xh, axis=0, keepdims=True)
        dxh = dy * g
        dx2 = r2 * (dxh - xh * jnp.mean(dxh * xh, axis=-1, keepdims=True))
        dx2_ref[...] = dx2
        dmerged = _dot_nt(dx2.astype(BF16), wo)
        doa = (sa * dmerged).astype(BF16)
        dob = (sb * dmerged).astype(BF16)
        doa_ref[...] = doa
        dob_ref[...] = dob
        dm_ref[0] = (dmerged * out_a * sa * (1.0 - sa)).astype(BF16)
        dm_ref[1] = (dmerged * out_b * sb * (1.0 - sb)).astype(BF16)
        dya_ref[...] = _dot_nt(doa, wa)
        dyb_ref[...] = _dot_nt(dob, wb)

    row = lambda: pl.BlockSpec((tm, D), lambda i: (i, 0))
    seg = lambda s: pl.BlockSpec((None, tm, D), lambda i: (s, i, 0))
    vec = pl.BlockSpec((1, D), lambda i: (0, 0))
    return pl.pallas_call(
        body, name="tail", grid=(t // tm,),
        in_specs=[row(), row(), seg(6), seg(7), row(), row(), _wblock(0), _wblock(1), _wblock(2), vec],
        out_specs=[row(), row(), row(), pl.BlockSpec((2, tm, D), lambda i: (0, i, 0)),
                   row(), row(), row(), vec, vec],
        out_shape=[jax.ShapeDtypeStruct((t, D), F32), jax.ShapeDtypeStruct((t, D), F32),
                   jax.ShapeDtypeStruct((t, D), F32), jax.ShapeDtypeStruct((2, t, D), BF16),
                   jax.ShapeDtypeStruct((t, D), BF16), jax.ShapeDtypeStruct((t, D), BF16),
                   jax.ShapeDtypeStruct((t, D), BF16), jax.ShapeDtypeStruct((1, D), F32),
                   jax.ShapeDtypeStruct((1, D), F32)],
        compiler_params=_params(("arbitrary",)),
    )(ya, yb, proj, proj, x2d, tgt, wg, wg, wg, g_fin)


def _tail_wgrad(ya, yb, merged, doa, dob, dx2):
    t = ya.shape[0]
    tm = 512

    def body(ya_ref, yb_ref, mg_ref, doa_ref, dob_ref, dx2_ref, ga_ref, gb_ref, go_ref):
        @pl.when(pl.program_id(0) == 0)
        def _():
            ga_ref[...] = jnp.zeros_like(ga_ref)
            gb_ref[...] = jnp.zeros_like(gb_ref)
            go_ref[...] = jnp.zeros_like(go_ref)

        ga_ref[...] += _dot_tn(ya_ref[...], doa_ref[...])
        gb_ref[...] += _dot_tn(yb_ref[...], dob_ref[...])
        go_ref[...] += _dot_tn(mg_ref[...], dx2_ref[...].astype(BF16))

    row = lambda: pl.BlockSpec((tm, D), lambda i: (i, 0))
    full = lambda: pl.BlockSpec((D, D), lambda i: (0, 0))
    return pl.pallas_call(
        body, name="tail_wgrad", grid=(t // tm,),
        in_specs=[row() for _ in range(6)], out_specs=[full(), full(), full()],
        out_shape=[jax.ShapeDtypeStruct((D, D), F32)] * 3,
        compiler_params=_params(("arbitrary",)),
    )(ya, yb, merged, doa, dob, dx2)


def _dproj_specs(tm, j_of, i_of):
    last = lambda j, i, lo, n: (jnp.clip(j - lo, 0, n - 1), i, 0)
    return [pl.BlockSpec((None, tm, D), lambda a, b: last(j_of(a, b), i_of(a, b), 0, 2)),
            pl.BlockSpec((None, tm, D), lambda a, b: last(j_of(a, b), i_of(a, b), 2, 4)),
            pl.BlockSpec((None, tm, D), lambda a, b: last(j_of(a, b), i_of(a, b), 6, 2))]


def _dproj_specs_ordered(tm):
    def spec(lo, n):
        def index(k, i, order_ref):
            seg = order_ref[k]
            mine = jnp.logical_and(seg >= lo, seg < lo + n)
            return jnp.where(mine, seg - lo, 0), jnp.where(mine, i, 0), 0
        return pl.BlockSpec((None, tm, D), index)
    return [spec(0, 2), spec(2, 4), spec(6, 2)]


def _dproj_pick(j, da_ref, db_ref, dc_ref, use):
    @pl.when(j < 2)
    def _():
        use(da_ref[...])

    @pl.when(jnp.logical_and(j >= 2, j < 6))
    def _():
        use(db_ref[...])

    @pl.when(j >= 6)
    def _():
        use(dc_ref[...])


def _rs_schedule(q, c):
    steps = []
    for s in range(3):
        d_a = lax.rem(q + 1 + s, 4)
        d_b = lax.rem(q + 1 + (s + 1) % 3, 4)
        steps.append((jnp.where(c == 0, d_a, d_b), jnp.where(c == 0, d_b, d_a)))
    steps.append((q, q))
    return steps


def _rs_order(q, c):
    order = []
    for keep, give in _rs_schedule(q, c):
        order += [2 * give + 1 - c, 2 * keep + c]
    return jnp.stack(order).astype(jnp.int32)


def _inproj_wgrad_rs(h, dpa, dpb, dpc, order):
    t = h.shape[0]
    tm = 1024
    nt = t // tm

    def body(order_ref, h_ref, da_ref, db_ref, dc_ref, parts_ref, acc, sib, outb,
             give_send, give_recv, sum_send, sum_recv, own_sem):
        k, i = pl.program_id(0), pl.program_id(1)
        x, y, c = _place()
        schedule = _rs_schedule(2 * x + y, c)

        def use(d):
            @pl.when(i == 0)
            def _():
                acc[k % 2] = _dot_tn(h_ref[...], d)

            @pl.when(i > 0)
            def _():
                acc[k % 2] += _dot_tn(h_ref[...], d)

        _dproj_pick(order_ref[k], da_ref, db_ref, dc_ref, use)

        def give_copy(s):
            return pltpu.make_async_remote_copy(
                src_ref=acc.at[0], dst_ref=sib.at[s % 2], send_sem=give_send.at[s], recv_sem=give_recv.at[s],
                device_id=(x, y, 1 - c), device_id_type=MESH)

        def sum_copy(s):
            keep = schedule[s][0]
            return pltpu.make_async_remote_copy(
                src_ref=outb.at[s], dst_ref=parts_ref.at[s], send_sem=sum_send.at[s], recv_sem=sum_recv.at[s],
                device_id=(keep // 2, lax.rem(keep, 2), c), device_id_type=MESH)

        own_copy = pltpu.make_async_copy(outb.at[3], parts_ref.at[3], own_sem)

        for s in range(4):
            @pl.when(jnp.logical_and(k == 2 * s, i == nt - 1))
            def _():
                give_copy(s).start()

            @pl.when(jnp.logical_and(k == 2 * s + 1, i == nt - 1))
            def _():
                give_copy(s).wait_recv()
                outb[s] = (acc[1] + sib[s % 2]).astype(BF16)
                give_copy(s).wait_send()
                if s < 3:
                    sum_copy(s).start()
                else:
                    own_copy.start()

        @pl.when(jnp.logical_and(k == NSEG - 1, i == nt - 1))
        def _():
            for s in range(3):
                sum_copy(s).wait_recv()
            for s in range(3):
                sum_copy(s).wait_send()
            own_copy.wait()

    return pl.pallas_call(
        body, name="inproj_wgrad_rs",
        grid_spec=pltpu.PrefetchScalarGridSpec(
            num_scalar_prefetch=1, grid=(NSEG, nt),
            in_specs=[pl.BlockSpec((tm, D), lambda k, i, order_ref: (i, 0))] + _dproj_specs_ordered(tm),
            out_specs=ANY,
            scratch_shapes=[pltpu.VMEM((2, D, D), F32), pltpu.VMEM((2, D, D), F32), pltpu.VMEM((4, D, D), BF16),
                            pltpu.SemaphoreType.DMA((4,)), pltpu.SemaphoreType.DMA((4,)),
                            pltpu.SemaphoreType.DMA((3,)), pltpu.SemaphoreType.DMA((3,)),
                            pltpu.SemaphoreType.DMA]),
        out_shape=jax.ShapeDtypeStruct((4, D, D), BF16),
        compiler_params=_params(("arbitrary", "arbitrary")),
    )(order, h, dpa, dpb, dpc)


def _inproj_dgrad(dpa, dpb, dpc, wg, x2d, dx2, g_in):
    t = x2d.shape[0]
    tm = 512

    def body(da_ref, db_ref, dc_ref, w_ref, x_ref, dx2_ref, g_ref, gx_ref, gg_ref, acc_s):
        i, j = pl.program_id(0), pl.program_id(1)

        @pl.when(jnp.logical_and(i == 0, j == 0))
        def _():
            gg_ref[...] = jnp.zeros_like(gg_ref)

        @pl.when(j == 0)
        def _():
            acc_s[...] = jnp.zeros_like(acc_s)

        def use(d):
            acc_s[...] += _dot_nt(d, w_ref[...])

        _dproj_pick(j, da_ref, db_ref, dc_ref, use)

        @pl.when(j == NSEG - 1)
        def _():
            x = x_ref[...]
            r = lax.rsqrt(jnp.mean(x * x, axis=-1, keepdims=True) + EPS)
            xh = x * r
            dh = acc_s[...]
            gg_ref[...] += jnp.sum(dh * xh, axis=0, keepdims=True)
            dxh = dh * g_ref[...]
            gx_ref[...] = dx2_ref[...] + r * (dxh - xh * jnp.mean(dxh * xh, axis=-1, keepdims=True))

    row = lambda: pl.BlockSpec((tm, D), lambda i, j: (i, 0))
    vec = pl.BlockSpec((1, D), lambda i, j: (0, 0))
    return pl.pallas_call(
        body, name="inproj_dgrad", grid=(t // tm, NSEG),
        in_specs=_dproj_specs(tm, lambda i, j: j, lambda i, j: i)
        + [pl.BlockSpec((None, D, D), lambda i, j: (j, 0, 0)), row(), row(), vec],
        out_specs=[row(), vec],
        out_shape=[jax.ShapeDtypeStruct((t, D), F32), jax.ShapeDtypeStruct((1, D), F32)],
        scratch_shapes=[pltpu.VMEM((tm, D), F32)],
        compiler_params=_params(("arbitrary", "arbitrary")),
    )(dpa, dpb, dpc, wg, x2d, dx2, g_in)


def _adamw(name, parts, w, m, v):
    n, rows, cols = parts.shape
    tr = rows if rows <= 256 else 256

    def body(p_ref, w_ref, m_ref, v_ref, g_ref, d_ref, nm_ref, nv_ref):
        g = p_ref[0].astype(F32)
        for k in range(1, n):
            g = g + p_ref[k].astype(F32)
        m_new = ADAM_B1 * m_ref[...] + (1.0 - ADAM_B1) * g
        v_new = ADAM_B2 * v_ref[...] + (1.0 - ADAM_B2) * (g * g)
        m_hat = m_new / (1.0 - ADAM_B1 ** ADAM_STEP)
        v_hat = v_new / (1.0 - ADAM_B2 ** ADAM_STEP)
        g_ref[...] = g
        d_ref[...] = -ADAM_LR * (m_hat / (jnp.sqrt(v_hat) + ADAM_EPS) + ADAM_WD * w_ref[...])
        nm_ref[...] = m_new
        nv_ref[...] = v_new

    blk = lambda: pl.BlockSpec((tr, cols), lambda i: (i, 0))
    return pl.pallas_call(
        body, name=name, grid=(rows // tr,),
        in_specs=[pl.BlockSpec((n, tr, cols), lambda i: (0, i, 0)), blk(), blk(), blk()],
        out_specs=[blk(), blk(), blk(), blk()],
        out_shape=[jax.ShapeDtypeStruct((rows, cols), F32)] * 4,
        compiler_params=_params(("arbitrary",)),
    )(parts, w, m, v)


ANY = pl.BlockSpec(memory_space=pl.ANY)


def _place():
    return lax.axis_index("x"), lax.axis_index("y"), lax.axis_index("c")


def _allgather(name, arrs):
    n = len(arrs)

    def body(*refs):
        ins, outs = refs[:n], refs[n:2 * n]
        send_sems, recv_sems, local_sems = refs[2 * n:]
        x, y, c = _place()
        me, sibling = (x, y, c), (x, y, 1 - c)
        chips = [(1 - x, y), (x, 1 - y), (1 - x, 1 - y)]

        def copy(a, k, block, to, src=None):
            px, py, pc = block
            dst = outs[a].at[4 * px + 2 * py + pc]
            return pltpu.make_async_remote_copy(
                src_ref=dst if src is None else src, dst_ref=dst,
                send_sem=send_sems.at[a, k], recv_sem=recv_sems.at[a, k], device_id=to, device_id_type=MESH)

        mine = [pltpu.make_async_copy(ins[a], outs[a].at[4 * x + 2 * y + c], local_sems.at[a]) for a in range(n)]
        for cp in mine:
            cp.start()
        first = []
        for a in range(n):
            first.append(copy(a, 0, me, sibling, src=ins[a]))
            first += [copy(a, 1 + j, me, (*chip, c), src=ins[a]) for j, chip in enumerate(chips)]
        for cp in first:
            cp.start()
        passed = []
        for j, chip in enumerate(chips):
            for a in range(n):
                copy(a, 1 + j, (*chip, c), me).wait_recv()
                fwd = copy(a, 4 + j, (*chip, c), sibling)
                fwd.start()
                passed.append(fwd)
        for a in range(n):
            copy(a, 0, sibling, me).wait_recv()
            for j, chip in enumerate(chips):
                copy(a, 4 + j, (*chip, 1 - c), me).wait_recv()
        for cp in first + passed:
            cp.wait_send()
        for cp in mine:
            cp.wait()

    return pl.pallas_call(
        body, name=name,
        in_specs=[ANY] * n, out_specs=[ANY] * n,
        out_shape=[jax.ShapeDtypeStruct((NDEV,) + a.shape, a.dtype) for a in arrs],
        scratch_shapes=[pltpu.SemaphoreType.DMA((n, 7)), pltpu.SemaphoreType.DMA((n, 7)),
                        pltpu.SemaphoreType.DMA((n,))],
    )(*arrs)


def _sibling_exchange(arrs):
    n = len(arrs)

    def body(*refs):
        ins, outs = refs[:n], refs[n:2 * n]
        send_sems, recv_sems = refs[2 * n:]
        x, y, c = _place()
        copies = []
        for a in range(n):
            for q in range(4):
                copies.append(pltpu.make_async_remote_copy(
                    src_ref=ins[a].at[2 * q + 1 - c], dst_ref=outs[a].at[q],
                    send_sem=send_sems.at[a, q], recv_sem=recv_sems.at[a, q],
                    device_id=(x, y, 1 - c), device_id_type=MESH))
        for cp in copies:
            cp.start()
        for cp in copies:
            cp.wait()

    return pl.pallas_call(
        body, name="grad_sibling_exchange",
        in_specs=[ANY] * n, out_specs=[ANY] * n,
        out_shape=[jax.ShapeDtypeStruct((4,) + a.shape[1:], a.dtype) for a in arrs],
        scratch_shapes=[pltpu.SemaphoreType.DMA((n, 4)), pltpu.SemaphoreType.DMA((n, 4))],
    )(*arrs)


def _chip_sum(name, own, got, core):
    _, rows, cols = own.shape
    tr = rows if rows <= 512 else 512

    def body(core_ref, own_ref, got_ref, out_ref):
        out_ref[...] = (own_ref[...] + got_ref[...]).astype(BF16)

    return pl.pallas_call(
        body, name=name,
        grid_spec=pltpu.PrefetchScalarGridSpec(
            num_scalar_prefetch=1, grid=(4, rows // tr),
            in_specs=[pl.BlockSpec((None, tr, cols), lambda q, i, core_ref: (2 * q + core_ref[0], i, 0)),
                      pl.BlockSpec((None, tr, cols), lambda q, i, core_ref: (q, i, 0))],
            out_specs=pl.BlockSpec((None, tr, cols), lambda q, i, core_ref: (q, i, 0))),
        out_shape=jax.ShapeDtypeStruct((4, rows, cols), BF16),
        compiler_params=_params(("arbitrary", "arbitrary")),
    )(core, own, got)


def _chip_exchange(arrs):
    n = len(arrs)

    def body(*refs):
        ins, outs = refs[:n], refs[n:2 * n]
        send_sems, recv_sems, local_sems = refs[2 * n:]
        x, y, c = _place()
        my_chip = 2 * x + y
        chips = [(1 - x, y), (x, 1 - y), (1 - x, 1 - y)]
        mine = [pltpu.make_async_copy(ins[a].at[my_chip], outs[a].at[my_chip], local_sems.at[a]) for a in range(n)]
        for cp in mine:
            cp.start()
        copies = []
        for a in range(n):
            for j, (px, py) in enumerate(chips):
                copies.append(pltpu.make_async_remote_copy(
                    src_ref=ins[a].at[2 * px + py], dst_ref=outs[a].at[my_chip],
                    send_sem=send_sems.at[a, j], recv_sem=recv_sems.at[a, j],
                    device_id=(px, py, c), device_id_type=MESH))
        for cp in copies:
            cp.start()
        for a in range(n):
            for j, (px, py) in enumerate(chips):
                pltpu.make_async_remote_copy(
                    src_ref=ins[a].at[my_chip], dst_ref=outs[a].at[2 * px + py],
                    send_sem=send_sems.at[a, j], recv_sem=recv_sems.at[a, j],
                    device_id=(px, py, c), device_id_type=MESH).wait_recv()
        for cp in copies:
            cp.wait_send()
        for cp in mine:
            cp.wait()

    return pl.pallas_call(
        body, name="grad_chip_exchange",
        in_specs=[ANY] * n, out_specs=[ANY] * n,
        out_shape=[jax.ShapeDtypeStruct(a.shape, a.dtype) for a in arrs],
        scratch_shapes=[pltpu.SemaphoreType.DMA((n, 3)), pltpu.SemaphoreType.DMA((n, 3)),
                        pltpu.SemaphoreType.DMA((n,))],
    )(*arrs)


def _block_diag(w):
    w4 = w.reshape(NCB, 4, 64, 64)
    eye = jnp.eye(4, dtype=w.dtype)
    return (w4[:, :, :, None, :] * eye[None, :, None, :, None]).reshape(NCB, CB, CB)


def _block_diag_back(g):
    g5 = g.reshape(NCB, 4, 64, 4, 64)
    return jnp.stack([g5[:, m, :, m, :] for m in range(4)], axis=1).reshape(16, 64, 64)


def kernel(x, norm_in, w_in, conv_w, conv_b, gate_x_w, gate_x_b, gate_a_w, gate_a_b, lru_lambda, gn_gain, w_proj_a, w_proj_b, w_out, norm_final, loss_target, m_norm_in, m_w_in, m_conv_w, m_conv_b, m_gate_x_w, m_gate_x_b, m_gate_a_w, m_gate_a_b, m_lru_lambda, m_gn_gain, m_w_proj_a, m_w_proj_b, m_w_out, m_norm_final, v_norm_in, v_w_in, v_conv_w, v_conv_b, v_gate_x_w, v_gate_x_b, v_gate_a_w, v_gate_a_b, v_lru_lambda, v_gn_gain, v_w_proj_a, v_w_proj_b, v_w_out, v_norm_final):
    xi, yi, ci = _place()
    me = 4 * xi + 2 * yi + ci
    core = ci.astype(jnp.int32).reshape(1)
    nshard = D // NDEV
    nb = x.shape[0]
    t = nb * S
    x2d = x.reshape(t, D)
    tgt2d = loss_target.reshape(t, D)
    g_final = norm_final.reshape(1, D)
    wbd = jnp.concatenate([_block_diag(gate_x_w[0]), _block_diag(gate_a_w[0])], axis=-1).astype(BF16)
    tables = _retention_tables()

    wp_own = jnp.concatenate([w_proj_a[0], w_proj_b[0], w_out[0]], axis=0).astype(BF16)
    tiny = jnp.concatenate([conv_w[0], jnp.pad(gn_gain[0], ((0, 0), (0, nshard - DK // NDEV)))], axis=0)
    proj, h, wg, wpg, tiny_g = _inproj_gather(x2d, norm_in, w_in[0].astype(BF16), wp_own, tiny,
                                              _gather_order(xi, yi, ci))
    conv_w_full = tiny_g[:, 0:4, :].transpose(1, 0, 2).reshape(4, D)
    gain3 = tiny_g[:, 4:8, :DK // NDEV].transpose(1, 0, 2).reshape(HEADS, 1, DK)

    ya, hs = _lru_fwd(proj, conv_w_full, conv_b, wbd, gate_x_b, gate_a_b, lru_lambda, nb)
    yb, qr, kr, o, rs = _ret_fwd(proj, gain3, tables, nb)
    dx2, dya, dyb, dpc, merged, doa, dob, g_fin, loss_vec = _tail(ya, yb, proj, x2d, tgt2d, wpg, g_final)
    g_pa, g_pb, g_out = _tail_wgrad(ya, yb, merged, doa, dob, dx2)
    loss = lax.psum(jnp.sum(loss_vec), AXES)

    own = [g.reshape(NDEV, nshard, D) for g in (g_pa, g_pb, g_out)]
    got = _sibling_exchange(own)
    sums = [_chip_sum("chip_sum_%d" % k, own[k], got[k], core) for k in range(3)]
    parts = _chip_exchange(sums)

    dpa, g_wbd, g_vec = _lru_bwd(proj, hs, dya, conv_w_full, conv_b, wbd, gate_x_b, gate_a_b, lru_lambda, nb)
    dpb, g_gain = _ret_bwd(proj, qr, kr, o, rs, dyb, gain3, tables, nb)
    parts_in = _inproj_wgrad_rs(h, dpa, dpb, dpc, _rs_order(2 * xi + yi, ci))
    grad_x, g_norm_in = _inproj_dgrad(dpa, dpb, dpc, wg, x2d, dx2, norm_in)
    grad_x = grad_x.reshape(nb, S, D)
    parts = [parts_in] + list(parts)

    small = jnp.concatenate([
        g_norm_in, g_vec[0:1], g_vec[1:2], g_vec[2:3], g_vec[3:4], g_fin,
        g_vec[4:8],
        g_gain.reshape(1, D),
        jnp.zeros((5, D), F32),
        _block_diag_back(g_wbd[:, :, :CB]).reshape(64, D),
        _block_diag_back(g_wbd[:, :, CB:]).reshape(64, D),
    ], axis=0)

    (small_all,) = _allgather("small_grad_allgather", [small])

    big = [("w_in", w_in, m_w_in, v_w_in), ("w_proj_a", w_proj_a, m_w_proj_a, v_w_proj_a),
           ("w_proj_b", w_proj_b, m_w_proj_b, v_w_proj_b), ("w_out", w_out, m_w_out, v_w_out)]
    res = {}
    for k, (nm, w, m, v) in enumerate(big):
        out = _adamw("adamw_" + nm, parts[k], w[0], m[0], v[0])
        res[nm] = [o[None] for o in out]

    def pack(p):
        return jnp.concatenate([
            p["norm_in"], p["conv_b"], p["gate_x_b"], p["gate_a_b"], p["lru_lambda"], p["norm_final"].reshape(1, D),
            jnp.zeros((10, D), F32), p["gate_x_w"].reshape(64, D), p["gate_a_w"].reshape(64, D)], axis=0)

    names = ["norm_in", "conv_b", "gate_x_b", "gate_a_b", "lru_lambda", "norm_final", "gate_x_w", "gate_a_w"]
    ws = dict(norm_in=norm_in, conv_b=conv_b, gate_x_b=gate_x_b, gate_a_b=gate_a_b, lru_lambda=lru_lambda,
              norm_final=norm_final, gate_x_w=gate_x_w, gate_a_w=gate_a_w)
    ms = dict(norm_in=m_norm_in, conv_b=m_conv_b, gate_x_b=m_gate_x_b, gate_a_b=m_gate_a_b, lru_lambda=m_lru_lambda,
              norm_final=m_norm_final, gate_x_w=m_gate_x_w, gate_a_w=m_gate_a_w)
    vs = dict(norm_in=v_norm_in, conv_b=v_conv_b, gate_x_b=v_gate_x_b, gate_a_b=v_gate_a_b, lru_lambda=v_lru_lambda,
              norm_final=v_norm_final, gate_x_w=v_gate_x_w, gate_a_w=v_gate_a_w)
    packed = _adamw("adamw_small", small_all, pack(ws), pack(ms), pack(vs))
    for nm in names:
        shape = ws[nm].shape
        if nm in ("gate_x_w", "gate_a_w"):
            lo = 16 if nm == "gate_x_w" else 80
            res[nm] = [o[lo:lo + 64].reshape(shape) for o in packed]
        else:
            r = names.index(nm)
            res[nm] = [o[r:r + 1].reshape(shape) for o in packed]

    g_small = packed[0]
    g_conv = lax.dynamic_slice(g_small[6:10], (0, me * nshard), (4, nshard))
    g_gain = lax.dynamic_slice(g_small[10:11].reshape(HEADS, DK), (0, me * (DK // NDEV)), (HEADS, DK // NDEV))
    pad = lambda a: jnp.pad(a, ((0, 0), (0, nshard - DK // NDEV)))
    shard = _adamw("adamw_shard",
                   jnp.concatenate([g_conv, pad(g_gain)], axis=0)[None],
                   jnp.concatenate([conv_w[0], pad(gn_gain[0])], axis=0),
                   jnp.concatenate([m_conv_w[0], pad(m_gn_gain[0])], axis=0),
                   jnp.concatenate([v_conv_w[0], pad(v_gn_gain[0])], axis=0))
    res["conv_w"] = [o[0:4][None] for o in shard]
    res["gn_gain"] = [o[4:8, :DK // NDEV][None] for o in shard]

    order = ["norm_in", "w_in", "conv_w", "conv_b", "gate_x_w", "gate_x_b", "gate_a_w", "gate_a_b", "lru_lambda",
             "gn_gain", "w_proj_a", "w_proj_b", "w_out", "norm_final"]
    outs = [loss, grad_x]
    for k in range(4):
        outs += [res[nm][k] for nm in order]
    return tuple(outs)
```

```python
import functools

import jax
import jax.numpy as jnp
from jax import lax
from jax.experimental import pallas as pl
from jax.experimental.pallas import tpu as pltpu

F32 = jnp.float32
BF16 = jnp.bfloat16
MESH = pl.DeviceIdType.MESH
AXES = ("x", "y", "c")

D = 1024
S = 2048
NSEG = 8
NDEV = 8
HEADS = 4
DK = 256
CH = 256
NCH = S // CH
CB = 256
NCB = D // CB
RC = 128
SCAN_GROUP = 8
EPS = 1e-6
LRU_C = 8.0
SMALL_ROWS = 144
VMEM_LIMIT = 56 * 1024 * 1024

ADAM_LR = 0.001
ADAM_B1 = 0.9
ADAM_B2 = 0.999
ADAM_EPS = 1e-08
ADAM_WD = 0.01
ADAM_STEP = 10


def _params(sem=None):
    return pltpu.CompilerParams(dimension_semantics=sem, vmem_limit_bytes=VMEM_LIMIT)


def _dot(a, b):
    return jnp.dot(a, b, preferred_element_type=F32)


def _dot_nt(a, b):
    return lax.dot_general(a, b, (((1,), (1,)), ((), ())), preferred_element_type=F32)


def _dot_tn(a, b):
    return lax.dot_general(a, b, (((0,), (0,)), ((), ())), preferred_element_type=F32)


def _sigmoid(x):
    return jax.nn.sigmoid(x)


def _expm1_nonpos(x):
    poly = x * (1.0 + x * (0.5 + x * (1.0 / 6.0 + x * (1.0 / 24.0))))
    return jnp.where(x > -0.05, poly, jnp.exp(x) - 1.0)


def _softplus(x):
    return jnp.maximum(x, 0.0) + jnp.log(1.0 + jnp.exp(-jnp.abs(x)))


def _rows(c, n):
    return pl.ds(pl.multiple_of(c * n, n), n)


def _window_before(ref, c, n):
    r0 = c * n
    prev = ref[pl.ds(pl.multiple_of(jnp.maximum(r0 - 8, 0), 8), 8), :]
    prev = jnp.where(c > 0, prev, 0.0)
    return jnp.concatenate([prev, ref[_rows(c, n), :]], axis=0)


def _shift_down(win, s, n):
    if s == 0:
        return win[8:, :]
    return pltpu.roll(win, s, 0)[8:, :]


def _shift_up(win, s, n):
    if s == 0:
        return win[:n, :]
    return pltpu.roll(win, n + 8 - s, 0)[:n, :]


def _gather_order(x, y, c):
    chips = [(1 - x, y), (x, 1 - y), (1 - x, 1 - y)]
    order = [4 * x + 2 * y + c, 4 * x + 2 * y + 1 - c]
    for px, py in chips:
        order += [4 * px + 2 * py + c, 4 * px + 2 * py + 1 - c]
    return jnp.stack(order).astype(jnp.int32)


def _inproj_gather(x2d, g_in, w_own, wp_own, tiny_own, order):
    t = x2d.shape[0]
    tm = 1024
    nt = t // tm

    def body(order_ref, x_ref, g_ref, w_own_ref, wp_own_ref, tiny_own_ref,
             proj_ref, h_ref, wg_ref, wpg_ref, tinyg_ref,
             w_all, h_all, send_sems, recv_sems, own_sems, out_sems):
        k, i = pl.program_id(0), pl.program_id(1)
        x, y, c = _place()
        me, sibling = (x, y, c), (x, y, 1 - c)
        chips = [(1 - x, y), (x, 1 - y), (1 - x, 1 - y)]
        srcs = [w_own_ref, wp_own_ref, tiny_own_ref]
        dsts = [w_all, wpg_ref, tinyg_ref]

        def copy(a, n, block, to, own_src=False):
            px, py, pc = block
            dst = dsts[a].at[4 * px + 2 * py + pc]
            return pltpu.make_async_remote_copy(
                src_ref=srcs[a] if own_src else dst, dst_ref=dst,
                send_sem=send_sems.at[a, n], recv_sem=recv_sems.at[a, n], device_id=to, device_id_type=MESH)

        def own_copy(a):
            return pltpu.make_async_copy(srcs[a], dsts[a].at[4 * x + 2 * y + c], own_sems.at[a])

        def keep_copy(n):
            return pltpu.make_async_copy(w_all.at[order_ref[n]], wg_ref.at[order_ref[n]], out_sems.at[n])

        def first_copies(a):
            return [copy(a, 0, me, sibling, True)] + [copy(a, 1 + j, me, (*chip, c), True) for j, chip in enumerate(chips)]

        def at_slot(n):
            return jnp.logical_and(k == n, i == 0)

        @pl.when(at_slot(0))
        def _():
            for a in range(3):
                own_copy(a).start()
            for a in range(3):
                for cp in first_copies(a):
                    cp.start()
            own_copy(0).wait()
            keep_copy(0).start()

        @pl.when(at_slot(1))
        def _():
            copy(0, 0, sibling, me).wait_recv()
            keep_copy(1).start()

        for j, chip in enumerate(chips):
            @pl.when(at_slot(2 + 2 * j))
            def _():
                copy(0, 1 + j, (*chip, c), me).wait_recv()
                copy(0, 4 + j, (*chip, c), sibling).start()
                keep_copy(2 + 2 * j).start()

            @pl.when(at_slot(3 + 2 * j))
            def _():
                copy(0, 4 + j, (*chip, 1 - c), me).wait_recv()
                keep_copy(3 + 2 * j).start()

        rows = pl.ds(pl.multiple_of(i * tm, tm), tm)

        @pl.when(k == 0)
        def _():
            xv = x_ref[...]
            r = lax.rsqrt(jnp.mean(xv * xv, axis=-1, keepdims=True) + EPS)
            hv = (xv * r * g_ref[...]).astype(BF16)
            h_ref[...] = hv
            h_all[rows, :] = hv

        proj_ref[...] = _dot(h_all[rows, :], w_all[order_ref[k]])

        @pl.when(jnp.logical_and(k == NSEG - 1, i == nt - 1))
        def _():
            for a in (1, 2):
                for j, chip in enumerate(chips):
                    copy(a, 1 + j, (*chip, c), me).wait_recv()
                    copy(a, 4 + j, (*chip, c), sibling).start()
            for a in (1, 2):
                copy(a, 0, sibling, me).wait_recv()
                for j, chip in enumerate(chips):
                    copy(a, 4 + j, (*chip, 1 - c), me).wait_recv()
            for a in range(3):
                for cp in first_copies(a):
                    cp.wait_send()
                for j, chip in enumerate(chips):
                    copy(a, 4 + j, (*chip, c), sibling).wait_send()
            for a in (1, 2):
                own_copy(a).wait()
            for n in range(NSEG):
                keep_copy(n).wait()

    hold = lambda k, i, order_ref: (jnp.where(k == 0, i, nt - 1), 0)
    return pl.pallas_call(
        body, name="inproj_gather",
        grid_spec=pltpu.PrefetchScalarGridSpec(
            num_scalar_prefetch=1, grid=(NSEG, nt),
            in_specs=[pl.BlockSpec((tm, D), hold),
                      pl.BlockSpec((1, D), lambda k, i, order_ref: (0, 0)),
                      ANY, ANY, ANY],
            out_specs=[pl.BlockSpec((None, tm, D), lambda k, i, order_ref: (order_ref[k], i, 0)),
                       pl.BlockSpec((tm, D), hold),
                       ANY, ANY, ANY],
            scratch_shapes=[pltpu.VMEM((NDEV, D, D), BF16), pltpu.VMEM((t, D), BF16),
                            pltpu.SemaphoreType.DMA((3, 7)), pltpu.SemaphoreType.DMA((3, 7)),
                            pltpu.SemaphoreType.DMA((3,)), pltpu.SemaphoreType.DMA((NSEG,))]),
        out_shape=[jax.ShapeDtypeStruct((NSEG, t, D), F32), jax.ShapeDtypeStruct((t, D), BF16),
                   jax.ShapeDtypeStruct((NDEV,) + w_own.shape, BF16),
                   jax.ShapeDtypeStruct((NDEV,) + wp_own.shape, BF16),
                   jax.ShapeDtypeStruct((NDEV,) + tiny_own.shape, F32)],
        compiler_params=_params(("arbitrary", "arbitrary")),
    )(order, x2d, g_in, w_own, wp_own, tiny_own)


def _tile_scan(a, u):
    row = lax.broadcasted_iota(jnp.int32, a.shape, 0)
    for d in (1, 2, 4):
        m = row >= d
        a_sh = pltpu.roll(a, d, 0)
        u_sh = pltpu.roll(u, d, 0)
        u = jnp.where(m, a * u_sh + u, u)
        a = jnp.where(m, a * a_sh, a)
    return a, u


def _tile_scan_rev(a, w):
    row = lax.broadcasted_iota(jnp.int32, a.shape, 0)
    for d in (1, 2, 4):
        m = row < 8 - d
        a_sh = pltpu.roll(a, 8 - d, 0)
        w_sh = pltpu.roll(w, 8 - d, 0)
        w = jnp.where(m, a * w_sh + w, w)
        a = jnp.where(m, a * a_sh, a)
    return a, w


def _lru_gates(xa_ref, c, cw_ref, cb_ref, wbd_ref, bx_ref, ba_ref, sp):
    win = _window_before(xa_ref, c, RC)
    xc = cb_ref[...] + cw_ref[3:4, :] * _shift_down(win, 0, RC)
    for s in (1, 2, 3):
        xc = xc + cw_ref[3 - s:4 - s, :] * _shift_down(win, s, RC)
    z = _dot(xc.astype(BF16), wbd_ref[...])
    gi = _sigmoid(z[:, :CB] + bx_ref[...])
    gr = _sigmoid(z[:, CB:] + ba_ref[...])
    log_a = -LRU_C * gr * sp
    return win, xc, gi, gr, log_a


def _lru_fwd(proj, conv_w, conv_b, wbd, bx, ba, lam, nb):
    t = nb * S

    def body(xa_ref, ga_ref, cw_ref, cb_ref, wbd_ref, bx_ref, ba_ref, lam_ref, ya_ref, hs_ref, a_s, u_s):
        sp = _softplus(-lam_ref[...])

        def gates(c, carry):
            _, xc, gi, _, log_a = _lru_gates(xa_ref, c, cw_ref, cb_ref, wbd_ref, bx_ref, ba_ref, sp)
            a_s[_rows(c, RC), :] = jnp.exp(log_a)
            u_s[_rows(c, RC), :] = jnp.sqrt(-_expm1_nonpos(2.0 * log_a)) * (gi * xc)
            return carry

        lax.fori_loop(0, S // RC, gates, 0)

        def scan(g, h):
            for k in range(SCAN_GROUP):
                rows = pl.ds(pl.multiple_of(g * (8 * SCAN_GROUP), 8 * SCAN_GROUP) + 8 * k, 8)
                a_cum, u_cum = _tile_scan(a_s[rows, :], u_s[rows, :])
                hs_ref[rows, :] = u_cum + a_cum * h
                h = u_cum[7:8, :] + a_cum[7:8, :] * h
            return h

        lax.fori_loop(0, S // (8 * SCAN_GROUP), scan, jnp.zeros((1, CB), F32))

        def gate_out(c, carry):
            ga = ga_ref[_rows(c, RC), :]
            ya_ref[_rows(c, RC), :] = (ga * _sigmoid(ga) * hs_ref[_rows(c, RC), :]).astype(BF16)
            return carry

        lax.fori_loop(0, S // RC, gate_out, 0)

    vec = pl.BlockSpec((1, CB), lambda b, cb: (0, cb))
    return pl.pallas_call(
        body, name="lru_fwd", grid=(nb, NCB),
        in_specs=[pl.BlockSpec((None, S, CB), lambda b, cb: (0, b, cb)),
                  pl.BlockSpec((None, S, CB), lambda b, cb: (1, b, cb)),
                  pl.BlockSpec((4, CB), lambda b, cb: (0, cb)),
                  vec,
                  pl.BlockSpec((None, CB, 2 * CB), lambda b, cb: (cb, 0, 0)),
                  vec, vec, vec],
        out_specs=[pl.BlockSpec((S, CB), lambda b, cb: (b, cb)),
                   pl.BlockSpec((S, CB), lambda b, cb: (b, cb))],
        out_shape=[jax.ShapeDtypeStruct((t, D), BF16), jax.ShapeDtypeStruct((t, D), F32)],
        scratch_shapes=[pltpu.VMEM((S, CB), F32), pltpu.VMEM((S, CB), F32)],
        compiler_params=_params(("arbitrary", "arbitrary")),
    )(proj, proj, conv_w, conv_b, wbd, bx, ba, lam)


def _lru_bwd(proj, hs, dya, conv_w, conv_b, wbd, bx, ba, lam, nb):
    t = nb * S

    def body(xa_ref, ga_ref, hs_ref, dya_ref, cw_ref, cb_ref, wbd_ref, bx_ref, ba_ref, lam_ref,
             dp_ref, dwbd_ref, vec_ref, a_s, xc_s, gi_s, gr_s, dh_s, dxc_s, acc_s):
        b = pl.program_id(1)
        lam_v = lam_ref[...]
        sp = _softplus(-lam_v)
        acc_s[...] = jnp.zeros_like(acc_s)

        @pl.when(b == 0)
        def _():
            dwbd_ref[...] = jnp.zeros_like(dwbd_ref)
            vec_ref[...] = jnp.zeros_like(vec_ref)

        def gates(c, carry):
            _, xc, gi, gr, log_a = _lru_gates(xa_ref, c, cw_ref, cb_ref, wbd_ref, bx_ref, ba_ref, sp)
            rows = _rows(c, RC)
            a_s[rows, :] = jnp.exp(log_a)
            xc_s[rows, :] = xc
            gi_s[rows, :] = gi
            gr_s[rows, :] = gr
            ga = ga_ref[rows, :]
            sg = _sigmoid(ga)
            dya_c = dya_ref[rows, :]
            dh_s[rows, :] = dya_c * (ga * sg)
            dp_ref[1, rows, :] = (dya_c * hs_ref[rows, :] * (sg * (1.0 + ga * (1.0 - sg)))).astype(BF16)
            return carry

        lax.fori_loop(0, S // RC, gates, 0)

        def scan(i, g_in):
            base = pl.multiple_of((S // (8 * SCAN_GROUP) - 1 - i) * (8 * SCAN_GROUP), 8 * SCAN_GROUP)
            row = lax.broadcasted_iota(jnp.int32, (8, CB), 0)
            for k in reversed(range(SCAN_GROUP)):
                rows = pl.ds(base + 8 * k, 8)
                a = a_s[rows, :]
                dl = dh_s[rows, :]
                a_cum, g_loc = _tile_scan_rev(a, a * dl)
                g = g_loc + a_cum * g_in
                dh_s[rows, :] = dl + jnp.where(row < 7, pltpu.roll(g, 7, 0), g_in)
                g_in = g_loc[0:1, :] + a_cum[0:1, :] * g_in
            return g_in

        lax.fori_loop(0, S // (8 * SCAN_GROUP), scan, jnp.zeros((1, CB), F32))

        dxc_s[pl.ds(S, 8), :] = jnp.zeros((8, CB), F32)

        def grads(c, carry):
            rows = _rows(c, RC)
            dh = dh_s[rows, :]
            h_prev = _shift_down(_window_before(hs_ref, c, RC), 1, RC)
            xc, gi, gr, a = xc_s[rows, :], gi_s[rows, :], gr_s[rows, :], a_s[rows, :]
            mult = jnp.sqrt(-_expm1_nonpos(-2.0 * LRU_C * gr * sp))
            dmult = dh * gi * xc
            d_log_a = dh * h_prev * a - dmult * (a * a) / mult
            dzi = dh * mult * xc * gi * (1.0 - gi)
            dzr = d_log_a * (-LRU_C * sp) * gr * (1.0 - gr)
            dz = jnp.concatenate([dzi, dzr], axis=1).astype(BF16)
            dxc_s[rows, :] = dh * mult * gi + _dot_nt(dz, wbd_ref[...])
            dwbd_ref[...] += _dot_tn(xc.astype(BF16), dz)
            acc_s[1:2, :] += jnp.sum(dzi, axis=0, keepdims=True)
            acc_s[2:3, :] += jnp.sum(dzr, axis=0, keepdims=True)
            acc_s[3:4, :] += jnp.sum(d_log_a * (-LRU_C * gr), axis=0, keepdims=True)
            return carry

        lax.fori_loop(0, S // RC, grads, 0)

        def conv_bwd(c, carry):
            rows = _rows(c, RC)
            dwin = dxc_s[pl.ds(pl.multiple_of(c * RC, RC), RC + 8), :]
            dxc = dwin[:RC, :]
            xwin = _window_before(xa_ref, c, RC)
            dxa = cw_ref[3:4, :] * dxc
            acc_s[0:1, :] += jnp.sum(dxc, axis=0, keepdims=True)
            acc_s[7:8, :] += jnp.sum(dxc * _shift_down(xwin, 0, RC), axis=0, keepdims=True)
            for s in (1, 2, 3):
                dxa = dxa + cw_ref[3 - s:4 - s, :] * _shift_up(dwin, s, RC)
                acc_s[7 - s:8 - s, :] += jnp.sum(dxc * _shift_down(xwin, s, RC), axis=0, keepdims=True)
            dp_ref[0, rows, :] = dxa.astype(BF16)
            return carry

        lax.fori_loop(0, S // RC, conv_bwd, 0)

        row = lax.broadcasted_iota(jnp.int32, acc_s.shape, 0)
        vec_ref[...] += jnp.where(row == 3, acc_s[...] * (-_sigmoid(-lam_v)), acc_s[...])

    vec = pl.BlockSpec((1, CB), lambda cb, b: (0, cb))
    blk = pl.BlockSpec((S, CB), lambda cb, b: (b, cb))
    return pl.pallas_call(
        body, name="lru_bwd", grid=(NCB, nb),
        in_specs=[pl.BlockSpec((None, S, CB), lambda cb, b: (0, b, cb)),
                  pl.BlockSpec((None, S, CB), lambda cb, b: (1, b, cb)),
                  blk, blk,
                  pl.BlockSpec((4, CB), lambda cb, b: (0, cb)),
                  vec,
                  pl.BlockSpec((None, CB, 2 * CB), lambda cb, b: (cb, 0, 0)),
                  vec, vec, vec],
        out_specs=[pl.BlockSpec((2, S, CB), lambda cb, b: (0, b, cb)),
                   pl.BlockSpec((None, CB, 2 * CB), lambda cb, b: (cb, 0, 0)),
                   pl.BlockSpec((8, CB), lambda cb, b: (0, cb))],
        out_shape=[jax.ShapeDtypeStruct((2, t, D), BF16),
                   jax.ShapeDtypeStruct((NCB, CB, 2 * CB), F32),
                   jax.ShapeDtypeStruct((8, D), F32)],
        scratch_shapes=[pltpu.VMEM((S, CB), F32), pltpu.VMEM((S, CB), F32), pltpu.VMEM((S, CB), F32),
                        pltpu.VMEM((S, CB), F32), pltpu.VMEM((S, CB), F32), pltpu.VMEM((S + 8, CB), F32),
                        pltpu.VMEM((8, CB), F32)],
        compiler_params=_params(("arbitrary", "arbitrary")),
    )(proj, proj, hs, dya, conv_w, conv_b, wbd, bx, ba, lam)


def _retention_tables():
    log_g = jnp.log1p(-(2.0 ** (-5.0 - jnp.arange(HEADS, dtype=F32))))
    idx = jnp.arange(CH, dtype=F32)
    diff = idx[:, None] - idx[None, :]
    inner = jnp.where(diff >= 0, jnp.exp(jnp.maximum(diff, 0.0)[None] * log_g[:, None, None]), 0.0)
    cross = jnp.exp((idx[None, :] + 1.0) * log_g[:, None])
    state = jnp.exp((CH - 1.0 - idx[None, :]) * log_g[:, None])
    cross = jnp.broadcast_to(cross[:, :, None], (HEADS, CH, DK))
    state = jnp.broadcast_to(state[:, :, None], (HEADS, CH, DK))
    half = DK // 2
    freqs = 10000.0 ** (-jnp.arange(half, dtype=F32) / half)
    ang = jnp.arange(S, dtype=F32)[:, None] * freqs[None, :]
    return inner, cross, state, jnp.cos(ang), jnp.sin(ang)


def _rotate(x, cos, sin):
    half = DK // 2
    x1, x2 = x[:, :half], x[:, half:]
    return jnp.concatenate([x1 * cos - x2 * sin, x1 * sin + x2 * cos], axis=1)


def _rotate_back(d, cos, sin):
    half = DK // 2
    d1, d2 = d[:, :half], d[:, half:]
    return jnp.concatenate([d1 * cos + d2 * sin, d2 * cos - d1 * sin], axis=1)


def _ret_fwd(proj, gain, tables, nb):
    t = nb * S
    inner_t, cross_t, state_t, cos_t, sin_t = tables

    def body(q_ref, k_ref, v_ref, gb_ref, gain_ref, dm_ref, cd_ref, sd_ref, cos_ref, sin_ref,
             yb_ref, qr_ref, kr_ref, o_ref, rs_ref, r_s):
        r_s[...] = jnp.zeros_like(r_s)
        chunk_decay = cd_ref[CH - 1:CH, :]

        def chunk(c, carry):
            rows = _rows(c, CH)
            cos, sin = cos_ref[rows, :], sin_ref[rows, :]
            qr = _rotate(q_ref[rows, :], cos, sin).astype(BF16)
            kr = (_rotate(k_ref[rows, :], cos, sin) * (DK ** -0.5)).astype(BF16)
            v = v_ref[rows, :]
            qr_ref[rows, :] = qr
            kr_ref[rows, :] = kr
            r = r_s[...]
            rb = r.astype(BF16)
            rs_ref[c] = rb
            p = (_dot_nt(qr, kr) * dm_ref[...]).astype(BF16)
            o = _dot(p, v.astype(BF16)) + _dot(qr, rb) * cd_ref[...]
            r_s[...] = chunk_decay * r + _dot_tn(kr, (v * sd_ref[...]).astype(BF16))
            o_ref[rows, :] = o
            oc = o - jnp.mean(o, axis=-1, keepdims=True)
            rstd = lax.rsqrt(jnp.mean(oc * oc, axis=-1, keepdims=True) + EPS)
            gb = gb_ref[rows, :]
            yb_ref[rows, :] = (gb * _sigmoid(gb) * (oc * rstd * gain_ref[...])).astype(BF16)
            return carry

        lax.fori_loop(0, NCH, chunk, 0)

    seg = lambda s: pl.BlockSpec((None, S, DK), lambda b, h: (s, b, h))
    tab = pl.BlockSpec((None, CH, DK), lambda b, h: (h, 0, 0))
    rot = pl.BlockSpec((S, DK // 2), lambda b, h: (0, 0))
    blk = pl.BlockSpec((S, DK), lambda b, h: (b, h))
    return pl.pallas_call(
        body, name="ret_fwd", grid=(nb, HEADS),
        in_specs=[seg(2), seg(3), seg(4), seg(5),
                  pl.BlockSpec((None, 1, DK), lambda b, h: (h, 0, 0)),
                  tab, tab, tab, rot, rot],
        out_specs=[blk, blk, blk, blk,
                   pl.BlockSpec((None, None, NCH, DK, DK), lambda b, h: (b, h, 0, 0, 0))],
        out_shape=[jax.ShapeDtypeStruct((t, D), BF16), jax.ShapeDtypeStruct((t, D), BF16),
                   jax.ShapeDtypeStruct((t, D), BF16), jax.ShapeDtypeStruct((t, D), F32),
                   jax.ShapeDtypeStruct((nb, HEADS, NCH, DK, DK), BF16)],
        scratch_shapes=[pltpu.VMEM((DK, DK), F32)],
        compiler_params=_params(("arbitrary", "arbitrary")),
    )(proj, proj, proj, proj, gain, inner_t, cross_t, state_t, cos_t, sin_t)


def _ret_bwd(proj, qr, kr, o, rs, dyb, gain, tables, nb):
    t = nb * S
    inner_t, cross_t, state_t, cos_t, sin_t = tables

    def body(qr_ref, kr_ref, v_ref, gb_ref, o_ref, dyb_ref, rs_ref, gain_ref, dm_ref, cd_ref, sd_ref,
             cos_ref, sin_ref, dp_ref, dgain_ref, dr_s):
        dr_s[...] = jnp.zeros_like(dr_s)
        chunk_decay = cd_ref[CH - 1:CH, :]

        @pl.when(pl.program_id(1) == 0)
        def _():
            dgain_ref[...] = jnp.zeros_like(dgain_ref)

        def chunk(i, carry):
            c = NCH - 1 - i
            rows = _rows(c, CH)
            gain_v = gain_ref[...]
            o_c = o_ref[rows, :]
            oc = o_c - jnp.mean(o_c, axis=-1, keepdims=True)
            rstd = lax.rsqrt(jnp.mean(oc * oc, axis=-1, keepdims=True) + EPS)
            yn = oc * rstd
            gb = gb_ref[rows, :]
            sg = _sigmoid(gb)
            dyb_c = dyb_ref[rows, :]
            dgn = dyb_c * (gb * sg)
            dp_ref[3, rows, :] = (dyb_c * (yn * gain_v) * (sg * (1.0 + gb * (1.0 - sg)))).astype(BF16)
            dgain_ref[...] += jnp.sum(dgn * yn, axis=0, keepdims=True)
            dyn = dgn * gain_v
            do = rstd * (dyn - jnp.mean(dyn, axis=-1, keepdims=True)
                         - yn * jnp.mean(dyn * yn, axis=-1, keepdims=True))
            dob = do.astype(BF16)
            dox = (do * cd_ref[...]).astype(BF16)

            q_c, k_c = qr_ref[rows, :], kr_ref[rows, :]
            v = v_ref[rows, :]
            vb = v.astype(BF16)
            vs = (v * sd_ref[...]).astype(BF16)
            rb = rs_ref[c]
            d_r = dr_s[...]
            drb = d_r.astype(BF16)
            dm = dm_ref[...]
            p = (_dot_nt(q_c, k_c) * dm).astype(BF16)
            dpm = (_dot_nt(dob, vb) * dm).astype(BF16)
            dq = _dot(dpm, k_c) + _dot_nt(dox, rb)
            dk = _dot_tn(dpm, q_c) + _dot_nt(vs, drb)
            dv = _dot_tn(p, dob) + _dot(k_c, drb) * sd_ref[...]
            dr_s[...] = chunk_decay * d_r + _dot_tn(q_c, dox)

            cos, sin = cos_ref[rows, :], sin_ref[rows, :]
            dp_ref[0, rows, :] = _rotate_back(dq, cos, sin).astype(BF16)
            dp_ref[1, rows, :] = (_rotate_back(dk, cos, sin) * (DK ** -0.5)).astype(BF16)
            dp_ref[2, rows, :] = dv.astype(BF16)
            return carry

        lax.fori_loop(0, NCH, chunk, 0)

    seg = lambda s: pl.BlockSpec((None, S, DK), lambda h, b: (s, b, h))
    tab = pl.BlockSpec((None, CH, DK), lambda h, b: (h, 0, 0))
    rot = pl.BlockSpec((S, DK // 2), lambda h, b: (0, 0))
    blk = pl.BlockSpec((S, DK), lambda h, b: (b, h))
    one = pl.BlockSpec((None, 1, DK), lambda h, b: (h, 0, 0))
    return pl.pallas_call(
        body, name="ret_bwd", grid=(HEADS, nb),
        in_specs=[blk, blk, seg(4), seg(5), blk, blk,
                  pl.BlockSpec((None, None, NCH, DK, DK), lambda h, b: (b, h, 0, 0, 0)),
                  one, tab, tab, tab, rot, rot],
        out_specs=[pl.BlockSpec((4, S, DK), lambda h, b: (0, b, h)), one],
        out_shape=[jax.ShapeDtypeStruct((4, t, D), BF16), jax.ShapeDtypeStruct((HEADS, 1, DK), F32)],
        scratch_shapes=[pltpu.VMEM((DK, DK), F32)],
        compiler_params=_params(("arbitrary", "arbitrary")),
    )(qr, kr, proj, proj, o, dyb, rs, gain, inner_t, cross_t, state_t, cos_t, sin_t)


def _wblock(k):
    return pl.BlockSpec((NDEV, D // NDEV, D), lambda i: (0, k, 0))


def _tail(ya, yb, proj, x2d, tgt, wg, g_fin):
    t = x2d.shape[0]
    tm = 256

    def body(ya_ref, yb_ref, ma_ref, mb_ref, x_ref, t_ref, wa_ref, wb_ref, wo_ref, g_ref,
             dx2_ref, dya_ref, dyb_ref, dm_ref, mg_ref, doa_ref, dob_ref, gfin_ref, loss_ref):
        i = pl.program_id(0)

        @pl.when(i == 0)
        def _():
            gfin_ref[...] = jnp.zeros_like(gfin_ref)
            loss_ref[...] = jnp.zeros_like(loss_ref)

        wa = wa_ref[...].reshape(D, D)
        wb = wb_ref[...].reshape(D, D)
        wo = wo_ref[...].reshape(D, D)
        out_a = _dot(ya_ref[...], wa)
        out_b = _dot(yb_ref[...], wb)
        sa = _sigmoid(ma_ref[...])
        sb = _sigmoid(mb_ref[...])
        merged = (sa * out_a + sb * out_b).astype(BF16)
        mg_ref[...] = merged
        x2 = x_ref[...] + _dot(merged, wo)
        r2 = lax.rsqrt(jnp.mean(x2 * x2, axis=-1, keepdims=True) + EPS)
        xh = x2 * r2
        g = g_ref[...]
        err = xh * g - t_ref[...]
        loss_ref[...] += jnp.sum(err * err, axis=0, keepdims=True) * (0.5 / D)
        dy = err * (1.0 / D)
        gfin_ref[...] += jnp.sum(dy * xh, axis=0, keepdims=True)
        dxh = dy * g
        dx2 = r2 * (dxh - xh * jnp.mean(dxh * xh, axis=-1, keepdims=True))
        dx2_ref[...] = dx2
        dmerged = _dot_nt(dx2.astype(BF16), wo)
        doa = (sa * dmerged).astype(BF16)
        dob = (sb * dmerged).astype(BF16)
        doa_ref[...] = doa
        dob_ref[...] = dob
        dm_ref[0] = (dmerged * out_a * sa * (1.0 - sa)).astype(BF16)
        dm_ref[1] = (dmerged * out_b * sb * (1.0 - sb)).astype(BF16)
        dya_ref[...] = _dot_nt(doa, wa)
        dyb_ref[...] = _dot_nt(dob, wb)

    row = lambda: pl.BlockSpec((tm, D), lambda i: (i, 0))
    seg = lambda s: pl.BlockSpec((None, tm, D), lambda i: (s, i, 0))
    vec = pl.BlockSpec((1, D), lambda i: (0, 0))
    return pl.pallas_call(
        body, name="tail", grid=(t // tm,),
        in_specs=[row(), row(), seg(6), seg(7), row(), row(), _wblock(0), _wblock(1), _wblock(2), vec],
        out_specs=[row(), row(), row(), pl.BlockSpec((2, tm, D), lambda i: (0, i, 0)),
                   row(), row(), row(), vec, vec],
        out_shape=[jax.ShapeDtypeStruct((t, D), F32), jax.ShapeDtypeStruct((t, D), F32),
                   jax.ShapeDtypeStruct((t, D), F32), jax.ShapeDtypeStruct((2, t, D), BF16),
                   jax.ShapeDtypeStruct((t, D), BF16), jax.ShapeDtypeStruct((t, D), BF16),
                   jax.ShapeDtypeStruct((t, D), BF16), jax.ShapeDtypeStruct((1, D), F32),
                   jax.ShapeDtypeStruct((1, D), F32)],
        compiler_params=_params(("arbitrary",)),
    )(ya, yb, proj, proj, x2d, tgt, wg, wg, wg, g_fin)


def _tail_wgrad(ya, yb, merged, doa, dob, dx2):
    t = ya.shape[0]
    tm = 512

    def body(ya_ref, yb_ref, mg_ref, doa_ref, dob_ref, dx2_ref, ga_ref, gb_ref, go_ref):
        @pl.when(pl.program_id(0) == 0)
        def _():
            ga_ref[...] = jnp.zeros_like(ga_ref)
            gb_ref[...] = jnp.zeros_like(gb_ref)
            go_ref[...] = jnp.zeros_like(go_ref)

        ga_ref[...] += _dot_tn(ya_ref[...], doa_ref[...])
        gb_ref[...] += _dot_tn(yb_ref[...], dob_ref[...])
        go_ref[...] += _dot_tn(mg_ref[...], dx2_ref[...].astype(BF16))

    row = lambda: pl.BlockSpec((tm, D), lambda i: (i, 0))
    full = lambda: pl.BlockSpec((D, D), lambda i: (0, 0))
    return pl.pallas_call(
        body, name="tail_wgrad", grid=(t // tm,),
        in_specs=[row() for _ in range(6)], out_specs=[full(), full(), full()],
        out_shape=[jax.ShapeDtypeStruct((D, D), F32)] * 3,
        compiler_params=_params(("arbitrary",)),
    )(ya, yb, merged, doa, dob, dx2)


def _dproj_specs(tm, j_of, i_of):
    last = lambda j, i, lo, n: (jnp.clip(j - lo, 0, n - 1), i, 0)
    return [pl.BlockSpec((None, tm, D), lambda a, b: last(j_of(a, b), i_of(a, b), 0, 2)),
            pl.BlockSpec((None, tm, D), lambda a, b: last(j_of(a, b), i_of(a, b), 2, 4)),
            pl.BlockSpec((None, tm, D), lambda a, b: last(j_of(a, b), i_of(a, b), 6, 2))]


def _dproj_specs_ordered(tm):
    def spec(lo, n):
        def index(k, i, order_ref):
            seg = order_ref[k]
            mine = jnp.logical_and(seg >= lo, seg < lo + n)
            return jnp.where(mine, seg - lo, 0), jnp.where(mine, i, 0), 0
        return pl.BlockSpec((None, tm, D), index)
    return [spec(0, 2), spec(2, 4), spec(6, 2)]


def _dproj_pick(j, da_ref, db_ref, dc_ref, use):
    @pl.when(j < 2)
    def _():
        use(da_ref[...])

    @pl.when(jnp.logical_and(j >= 2, j < 6))
    def _():
        use(db_ref[...])

    @pl.when(j >= 6)
    def _():
        use(dc_ref[...])


def _rs_schedule(q, c):
    steps = []
    for s in range(3):
        d_a = lax.rem(q + 1 + s, 4)
        d_b = lax.rem(q + 1 + (s + 1) % 3, 4)
        steps.append((jnp.where(c == 0, d_a, d_b), jnp.where(c == 0, d_b, d_a)))
    steps.append((q, q))
    return steps


def _rs_order(q, c):
    order = []
    for keep, give in _rs_schedule(q, c):
        order += [2 * give + 1 - c, 2 * keep + c]
    return jnp.stack(order).astype(jnp.int32)


def _inproj_wgrad_rs(h, dpa, dpb, dpc, order):
    t = h.shape[0]
    tm = 1024
    nt = t // tm

    def body(order_ref, h_ref, da_ref, db_ref, dc_ref, parts_ref, acc, sib, outb,
             give_send, give_recv, sum_send, sum_recv, own_sem):
        k, i = pl.program_id(0), pl.program_id(1)
        x, y, c = _place()
        schedule = _rs_schedule(2 * x + y, c)

        def use(d):
            @pl.when(i == 0)
            def _():
                acc[k % 2] = _dot_tn(h_ref[...], d)

            @pl.when(i > 0)
            def _():
                acc[k % 2] += _dot_tn(h_ref[...], d)

        _dproj_pick(order_ref[k], da_ref, db_ref, dc_ref, use)

        def give_copy(s):
            return pltpu.make_async_remote_copy(
                src_ref=acc.at[0], dst_ref=sib.at[s % 2], send_sem=give_send.at[s], recv_sem=give_recv.at[s],
                device_id=(x, y, 1 - c), device_id_type=MESH)

        def sum_copy(s):
            keep = schedule[s][0]
            return pltpu.make_async_remote_copy(
                src_ref=outb.at[s], dst_ref=parts_ref.at[s], send_sem=sum_send.at[s], recv_sem=sum_recv.at[s],
                device_id=(keep // 2, lax.rem(keep, 2), c), device_id_type=MESH)

        own_copy = pltpu.make_async_copy(outb.at[3], parts_ref.at[3], own_sem)

        for s in range(4):
            @pl.when(jnp.logical_and(k == 2 * s, i == nt - 1))
            def _():
                give_copy(s).start()

            @pl.when(jnp.logical_and(k == 2 * s + 1, i == nt - 1))
            def _():
                give_copy(s).wait_recv()
                outb[s] = (acc[1] + sib[s % 2]).astype(BF16)
                give_copy(s).wait_send()
                if s < 3:
                    sum_copy(s).start()
                else:
                    own_copy.start()

        @pl.when(jnp.logical_and(k == NSEG - 1, i == nt - 1))
        def _():
            for s in range(3):
                sum_copy(s).wait_recv()
            for s in range(3):
                sum_copy(s).wait_send()
            own_copy.wait()

    return pl.pallas_call(
        body, name="inproj_wgrad_rs",
        grid_spec=pltpu.PrefetchScalarGridSpec(
            num_scalar_prefetch=1, grid=(NSEG, nt),
            in_specs=[pl.BlockSpec((tm, D), lambda k, i, order_ref: (i, 0))] + _dproj_specs_ordered(tm),
            out_specs=ANY,
            scratch_shapes=[pltpu.VMEM((2, D, D), F32), pltpu.VMEM((2, D, D), F32), pltpu.VMEM((4, D, D), BF16),
                            pltpu.SemaphoreType.DMA((4,)), pltpu.SemaphoreType.DMA((4,)),
                            pltpu.SemaphoreType.DMA((3,)), pltpu.SemaphoreType.DMA((3,)),
                            pltpu.SemaphoreType.DMA]),
        out_shape=jax.ShapeDtypeStruct((4, D, D), BF16),
        compiler_params=_params(("arbitrary", "arbitrary")),
    )(order, h, dpa, dpb, dpc)


def _inproj_dgrad(dpa, dpb, dpc, wg, x2d, dx2, g_in):
    t = x2d.shape[0]
    tm = 512

    def body(da_ref, db_ref, dc_ref, w_ref, x_ref, dx2_ref, g_ref, gx_ref, gg_ref, acc_s):
        i, j = pl.program_id(0), pl.program_id(1)

        @pl.when(jnp.logical_and(i == 0, j == 0))
        def _():
            gg_ref[...] = jnp.zeros_like(gg_ref)

        @pl.when(j == 0)
        def _():
            acc_s[...] = jnp.zeros_like(acc_s)

        def use(d):
            acc_s[...] += _dot_nt(d, w_ref[...])

        _dproj_pick(j, da_ref, db_ref, dc_ref, use)

        @pl.when(j == NSEG - 1)
        def _():
            x = x_ref[...]
            r = lax.rsqrt(jnp.mean(x * x, axis=-1, keepdims=True) + EPS)
            xh = x * r
            dh = acc_s[...]
            gg_ref[...] += jnp.sum(dh * xh, axis=0, keepdims=True)
            dxh = dh * g_ref[...]
            gx_ref[...] = dx2_ref[...] + r * (dxh - xh * jnp.mean(dxh * xh, axis=-1, keepdims=True))

    row = lambda: pl.BlockSpec((tm, D), lambda i, j: (i, 0))
    vec = pl.BlockSpec((1, D), lambda i, j: (0, 0))
    return pl.pallas_call(
        body, name="inproj_dgrad", grid=(t // tm, NSEG),
        in_specs=_dproj_specs(tm, lambda i, j: j, lambda i, j: i)
        + [pl.BlockSpec((None, D, D), lambda i, j: (j, 0, 0)), row(), row(), vec],
        out_specs=[row(), vec],
        out_shape=[jax.ShapeDtypeStruct((t, D), F32), jax.ShapeDtypeStruct((1, D), F32)],
        scratch_shapes=[pltpu.VMEM((tm, D), F32)],
        compiler_params=_params(("arbitrary", "arbitrary")),
    )(dpa, dpb, dpc, wg, x2d, dx2, g_in)


def _adamw(name, parts, w, m, v):
    n, rows, cols = parts.shape
    tr = rows if rows <= 256 else 256

    def body(p_ref, w_ref, m_ref, v_ref, g_ref, d_ref, nm_ref, nv_ref):
        g = p_ref[0].astype(F32)
        for k in range(1, n):
            g = g + p_ref[k].astype(F32)
        m_new = ADAM_B1 * m_ref[...] + (1.0 - ADAM_B1) * g
        v_new = ADAM_B2 * v_ref[...] + (1.0 - ADAM_B2) * (g * g)
        m_hat = m_new / (1.0 - ADAM_B1 ** ADAM_STEP)
        v_hat = v_new / (1.0 - ADAM_B2 ** ADAM_STEP)
        g_ref[...] = g
        d_ref[...] = -ADAM_LR * (m_hat / (jnp.sqrt(v_hat) + ADAM_EPS) + ADAM_WD * w_ref[...])
        nm_ref[...] = m_new
        nv_ref[...] = v_new

    blk = lambda: pl.BlockSpec((tr, cols), lambda i: (i, 0))
    return pl.pallas_call(
        body, name=name, grid=(rows // tr,),
        in_specs=[pl.BlockSpec((n, tr, cols), lambda i: (0, i, 0)), blk(), blk(), blk()],
        out_specs=[blk(), blk(), blk(), blk()],
        out_shape=[jax.ShapeDtypeStruct((rows, cols), F32)] * 4,
        compiler_params=_params(("arbitrary",)),
    )(parts, w, m, v)


ANY = pl.BlockSpec(memory_space=pl.ANY)


def _place():
    return lax.axis_index("x"), lax.axis_index("y"), lax.axis_index("c")


def _allgather(name, arrs):
    n = len(arrs)

    def body(*refs):
        ins, outs = refs[:n], refs[n:2 * n]
        send_sems, recv_sems, local_sems = refs[2 * n:]
        x, y, c = _place()
        me, sibling = (x, y, c), (x, y, 1 - c)
        chips = [(1 - x, y), (x, 1 - y), (1 - x, 1 - y)]

        def copy(a, k, block, to, src=None):
            px, py, pc = block
            dst = outs[a].at[4 * px + 2 * py + pc]
            return pltpu.make_async_remote_copy(
                src_ref=dst if src is None else src, dst_ref=dst,
                send_sem=send_sems.at[a, k], recv_sem=recv_sems.at[a, k], device_id=to, device_id_type=MESH)

        mine = [pltpu.make_async_copy(ins[a], outs[a].at[4 * x + 2 * y + c], local_sems.at[a]) for a in range(n)]
        for cp in mine:
            cp.start()
        first = []
        for a in range(n):
            first.append(copy(a, 0, me, sibling, src=ins[a]))
            first += [copy(a, 1 + j, me, (*chip, c), src=ins[a]) for j, chip in enumerate(chips)]
        for cp in first:
            cp.start()
        passed = []
        for j, chip in enumerate(chips):
            for a in range(n):
                copy(a, 1 + j, (*chip, c), me).wait_recv()
                fwd = copy(a, 4 + j, (*chip, c), sibling)
                fwd.start()
                passed.append(fwd)
        for a in range(n):
            copy(a, 0, sibling, me).wait_recv()
            for j, chip in enumerate(chips):
                copy(a, 4 + j, (*chip, 1 - c), me).wait_recv()
        for cp in first + passed:
            cp.wait_send()
        for cp in mine:
            cp.wait()

    return pl.pallas_call(
        body, name=name,
        in_specs=[ANY] * n, out_specs=[ANY] * n,
        out_shape=[jax.ShapeDtypeStruct((NDEV,) + a.shape, a.dtype) for a in arrs],
        scratch_shapes=[pltpu.SemaphoreType.DMA((n, 7)), pltpu.SemaphoreType.DMA((n, 7)),
                        pltpu.SemaphoreType.DMA((n,))],
    )(*arrs)


def _sibling_exchange(arrs):
    n = len(arrs)

    def body(*refs):
        ins, outs = refs[:n], refs[n:2 * n]
        send_sems, recv_sems = refs[2 * n:]
        x, y, c = _place()
        copies = []
        for a in range(n):
            for q in range(4):
                copies.append(pltpu.make_async_remote_copy(
                    src_ref=ins[a].at[2 * q + 1 - c], dst_ref=outs[a].at[q],
                    send_sem=send_sems.at[a, q], recv_sem=recv_sems.at[a, q],
                    device_id=(x, y, 1 - c), device_id_type=MESH))
        for cp in copies:
            cp.start()
        for cp in copies:
            cp.wait()

    return pl.pallas_call(
        body, name="grad_sibling_exchange",
        in_specs=[ANY] * n, out_specs=[ANY] * n,
        out_shape=[jax.ShapeDtypeStruct((4,) + a.shape[1:], a.dtype) for a in arrs],
        scratch_shapes=[pltpu.SemaphoreType.DMA((n, 4)), pltpu.SemaphoreType.DMA((n, 4))],
    )(*arrs)


def _chip_sum(name, own, got, core):
    _, rows, cols = own.shape
    tr = rows if rows <= 512 else 512

    def body(core_ref, own_ref, got_ref, out_ref):
        out_ref[...] = (own_ref[...] + got_ref[...]).astype(BF16)

    return pl.pallas_call(
        body, name=name,
        grid_spec=pltpu.PrefetchScalarGridSpec(
            num_scalar_prefetch=1, grid=(4, rows // tr),
            in_specs=[pl.BlockSpec((None, tr, cols), lambda q, i, core_ref: (2 * q + core_ref[0], i, 0)),
                      pl.BlockSpec((None, tr, cols), lambda q, i, core_ref: (q, i, 0))],
            out_specs=pl.BlockSpec((None, tr, cols), lambda q, i, core_ref: (q, i, 0))),
        out_shape=jax.ShapeDtypeStruct((4, rows, cols), BF16),
        compiler_params=_params(("arbitrary", "arbitrary")),
    )(core, own, got)


def _chip_exchange(arrs):
    n = len(arrs)

    def body(*refs):
        ins, outs = refs[:n], refs[n:2 * n]
        send_sems, recv_sems, local_sems = refs[2 * n:]
        x, y, c = _place()
        my_chip = 2 * x + y
        chips = [(1 - x, y), (x, 1 - y), (1 - x, 1 - y)]
        mine = [pltpu.make_async_copy(ins[a].at[my_chip], outs[a].at[my_chip], local_sems.at[a]) for a in range(n)]
        for cp in mine:
            cp.start()
        copies = []
        for a in range(n):
            for j, (px, py) in enumerate(chips):
                copies.append(pltpu.make_async_remote_copy(
                    src_ref=ins[a].at[2 * px + py], dst_ref=outs[a].at[my_chip],
                    send_sem=send_sems.at[a, j], recv_sem=recv_sems.at[a, j],
                    device_id=(px, py, c), device_id_type=MESH))
        for cp in copies:
            cp.start()
        for a in range(n):
            for j, (px, py) in enumerate(chips):
                pltpu.make_async_remote_copy(
                    src_ref=ins[a].at[my_chip], dst_ref=outs[a].at[2 * px + py],
                    send_sem=send_sems.at[a, j], recv_sem=recv_sems.at[a, j],
                    device_id=(px, py, c), device_id_type=MESH).wait_recv()
        for cp in copies:
            cp.wait_send()
        for cp in mine:
            cp.wait()

    return pl.pallas_call(
        body, name="grad_chip_exchange",
        in_specs=[ANY] * n, out_specs=[ANY] * n,
        out_shape=[jax.ShapeDtypeStruct(a.shape, a.dtype) for a in arrs],
        scratch_shapes=[pltpu.SemaphoreType.DMA((n, 3)), pltpu.SemaphoreType.DMA((n, 3)),
                        pltpu.SemaphoreType.DMA((n,))],
    )(*arrs)


def _block_diag(w):
    w4 = w.reshape(NCB, 4, 64, 64)
    eye = jnp.eye(4, dtype=w.dtype)
    return (w4[:, :, :, None, :] * eye[None, :, None, :, None]).reshape(NCB, CB, CB)


def _block_diag_back(g):
    g5 = g.reshape(NCB, 4, 64, 4, 64)
    return jnp.stack([g5[:, m, :, m, :] for m in range(4)], axis=1).reshape(16, 64, 64)


def kernel(x, norm_in, w_in, conv_w, conv_b, gate_x_w, gate_x_b, gate_a_w, gate_a_b, lru_lambda, gn_gain, w_proj_a, w_proj_b, w_out, norm_final, loss_target, m_norm_in, m_w_in, m_conv_w, m_conv_b, m_gate_x_w, m_gate_x_b, m_gate_a_w, m_gate_a_b, m_lru_lambda, m_gn_gain, m_w_proj_a, m_w_proj_b, m_w_out, m_norm_final, v_norm_in, v_w_in, v_conv_w, v_conv_b, v_gate_x_w, v_gate_x_b, v_gate_a_w, v_gate_a_b, v_lru_lambda, v_gn_gain, v_w_proj_a, v_w_proj_b, v_w_out, v_norm_final):
    xi, yi, ci = _place()
    me = 4 * xi + 2 * yi + ci
    core = ci.astype(jnp.int32).reshape(1)
    nshard = D // NDEV
    nb = x.shape[0]
    t = nb * S
    x2d = x.reshape(t, D)
    tgt2d = loss_target.reshape(t, D)
    g_final = norm_final.reshape(1, D)
    wbd = jnp.concatenate([_block_diag(gate_x_w[0]), _block_diag(gate_a_w[0])], axis=-1).astype(BF16)
    tables = _retention_tables()

    wp_own = jnp.concatenate([w_proj_a[0], w_proj_b[0], w_out[0]], axis=0).astype(BF16)
    tiny = jnp.concatenate([conv_w[0], jnp.pad(gn_gain[0], ((0, 0), (0, nshard - DK // NDEV)))], axis=0)
    proj, h, wg, wpg, tiny_g = _inproj_gather(x2d, norm_in, w_in[0].astype(BF16), wp_own, tiny,
                                              _gather_order(xi, yi, ci))
    conv_w_full = tiny_g[:, 0:4, :].transpose(1, 0, 2).reshape(4, D)
    gain3 = tiny_g[:, 4:8, :DK // NDEV].transpose(1, 0, 2).reshape(HEADS, 1, DK)

    ya, hs = _lru_fwd(proj, conv_w_full, conv_b, wbd, gate_x_b, gate_a_b, lru_lambda, nb)
    yb, qr, kr, o, rs = _ret_fwd(proj, gain3, tables, nb)
    dx2, dya, dyb, dpc, merged, doa, dob, g_fin, loss_vec = _tail(ya, yb, proj, x2d, tgt2d, wpg, g_final)
    g_pa, g_pb, g_out = _tail_wgrad(ya, yb, merged, doa, dob, dx2)
    loss = lax.psum(jnp.sum(loss_vec), AXES)

    own = [g.reshape(NDEV, nshard, D) for g in (g_pa, g_pb, g_out)]
    got = _sibling_exchange(own)
    sums = [_chip_sum("chip_sum_%d" % k, own[k], got[k], core) for k in range(3)]
    parts = _chip_exchange(sums)

    dpa, g_wbd, g_vec = _lru_bwd(proj, hs, dya, conv_w_full, conv_b, wbd, gate_x_b, gate_a_b, lru_lambda, nb)
    dpb, g_gain = _ret_bwd(proj, qr, kr, o, rs, dyb, gain3, tables, nb)
    parts_in = _inproj_wgrad_rs(h, dpa, dpb, dpc, _rs_order(2 * xi + yi, ci))
    grad_x, g_norm_in = _inproj_dgrad(dpa, dpb, dpc, wg, x2d, dx2, norm_in)
    grad_x = grad_x.reshape(nb, S, D)
    parts = [parts_in] + list(parts)

    small = jnp.concatenate([
        g_norm_in, g_vec[0:1], g_vec[1:2], g_vec[2:3], g_vec[3:4], g_fin,
        g_vec[4:8],
        g_gain.reshape(1, D),
        jnp.zeros((5, D), F32),
        _block_diag_back(g_wbd[:, :, :CB]).reshape(64, D),
        _block_diag_back(g_wbd[:, :, CB:]).reshape(64, D),
    ], axis=0)

    (small_all,) = _allgather("small_grad_allgather", [small])

    big = [("w_in", w_in, m_w_in, v_w_in), ("w_proj_a", w_proj_a, m_w_proj_a, v_w_proj_a),
           ("w_proj_b", w_proj_b, m_w_proj_b, v_w_proj_b), ("w_out", w_out, m_w_out, v_w_out)]
    res = {}
    for k, (nm, w, m, v) in enumerate(big):
        out = _adamw("adamw_" + nm, parts[k], w[0], m[0], v[0])
        res[nm] = [o[None] for o in out]

    def pack(p):
        return jnp.concatenate([
            p["norm_in"], p["conv_b"], p["gate_x_b"], p["gate_a_b"], p["lru_lambda"], p["norm_final"].reshape(1, D),
            jnp.zeros((10, D), F32), p["gate_x_w"].reshape(64, D), p["gate_a_w"].reshape(64, D)], axis=0)

    names = ["norm_in", "conv_b", "gate_x_b", "gate_a_b", "lru_lambda", "norm_final", "gate_x_w", "gate_a_w"]
    ws = dict(norm_in=norm_in, conv_b=conv_b, gate_x_b=gate_x_b, gate_a_b=gate_a_b, lru_lambda=lru_lambda,
              norm_final=norm_final, gate_x_w=gate_x_w, gate_a_w=gate_a_w)
    ms = dict(norm_in=m_norm_in, conv_b=m_conv_b, gate_x_b=m_gate_x_b, gate_a_b=m_gate_a_b, lru_lambda=m_lru_lambda,
              norm_final=m_norm_final, gate_x_w=m_gate_x_w, gate_a_w=m_gate_a_w)
    vs = dict(norm_in=v_norm_in, conv_b=v_conv_b, gate_x_b=v_gate_x_b, gate_a_b=v_gate_a_b, lru_lambda=v_lru_lambda,
              norm_final=v_norm_final, gate_x_w=v_gate_x_w, gate_a_w=v_gate_a_w)
    packed = _adamw("adamw_small", small_all, pack(ws), pack(ms), pack(vs))
    for nm in names:
        shape = ws[nm].shape
        if nm in ("gate_x_w", "gate_a_w"):
            lo = 16 if nm == "gate_x_w" else 80
            res[nm] = [o[lo:lo + 64].reshape(shape) for o in packed]
        else:
            r = names.index(nm)
            res[nm] = [o[r:r + 1].reshape(shape) for o in packed]

    g_small = packed[0]
    g_conv = lax.dynamic_slice(g_small[6:10], (0, me * nshard), (4, nshard))
    g_gain = lax.dynamic_slice(g_small[10:11].reshape(HEADS, DK), (0, me * (DK // NDEV)), (HEADS, DK // NDEV))
    pad = lambda a: jnp.pad(a, ((0, 0), (0, nshard - DK // NDEV)))
    shard = _adamw("adamw_shard",
                   jnp.concatenate([g_conv, pad(g_gain)], axis=0)[None],
                   jnp.concatenate([conv_w[0], pad(gn_gain[0])], axis=0),
                   jnp.concatenate([m_conv_w[0], pad(m_gn_gain[0])], axis=0),
                   jnp.concatenate([v_conv_w[0], pad(v_gn_gain[0])], axis=0))
    res["conv_w"] = [o[0:4][None] for o in shard]
    res["gn_gain"] = [o[4:8, :DK // NDEV][None] for o in shard]

    order = ["norm_in", "w_in", "conv_w", "conv_b", "gate_x_w", "gate_x_b", "gate_a_w", "gate_a_b", "lru_lambda",
             "gn_gain", "w_proj_a", "w_proj_b", "w_out", "norm_final"]
    outs = [loss, grad_x]
    for k in range(4):
        outs += [res[nm][k] for nm in order]
    return tuple(outs)
```

```python
import functools

import jax
import jax.numpy as jnp
from jax import lax
from jax.experimental import pallas as pl
from jax.experimental.pallas import tpu as pltpu

F32 = jnp.float32
BF16 = jnp.bfloat16
MESH = pl.DeviceIdType.MESH
AXES = ("x", "y", "c")

D = 1024
S = 2048
NSEG = 8
NDEV = 8
HEADS = 4
DK = 256
CH = 256
NCH = S // CH
CB = 256
NCB = D // CB
RC = 128
SCAN_GROUP = 8
EPS = 1e-6
LRU_C = 8.0
SMALL_ROWS = 144
VMEM_LIMIT = 56 * 1024 * 1024

ADAM_LR = 0.001
ADAM_B1 = 0.9
ADAM_B2 = 0.999
ADAM_EPS = 1e-08
ADAM_WD = 0.01
ADAM_STEP = 10


def _params(sem=None):
    return pltpu.CompilerParams(dimension_semantics=sem, vmem_limit_bytes=VMEM_LIMIT)


def _dot(a, b):
    return jnp.dot(a, b, preferred_element_type=F32)


def _dot_nt(a, b):
    return lax.dot_general(a, b, (((1,), (1,)), ((), ())), preferred_element_type=F32)


def _dot_tn(a, b):
    return lax.dot_general(a, b, (((0,), (0,)), ((), ())), preferred_element_type=F32)


def _sigmoid(x):
    return jax.nn.sigmoid(x)


def _expm1_nonpos(x):
    poly = x * (1.0 + x * (0.5 + x * (1.0 / 6.0 + x * (1.0 / 24.0))))
    return jnp.where(x > -0.05, poly, jnp.exp(x) - 1.0)


def _softplus(x):
    return jnp.maximum(x, 0.0) + jnp.log(1.0 + jnp.exp(-jnp.abs(x)))


def _rows(c, n):
    return pl.ds(pl.multiple_of(c * n, n), n)


def _window_before(ref, c, n):
    r0 = c * n
    prev = ref[pl.ds(pl.multiple_of(jnp.maximum(r0 - 8, 0), 8), 8), :]
    prev = jnp.where(c > 0, prev, 0.0)
    return jnp.concatenate([prev, ref[_rows(c, n), :]], axis=0)


def _shift_down(win, s, n):
    if s == 0:
        return win[8:, :]
    return pltpu.roll(win, s, 0)[8:, :]


def _shift_up(win, s, n):
    if s == 0:
        return win[:n, :]
    return pltpu.roll(win, n + 8 - s, 0)[:n, :]


def _gather_order(x, y, c):
    chips = [(1 - x, y), (x, 1 - y), (1 - x, 1 - y)]
    order = [4 * x + 2 * y + c, 4 * x + 2 * y + 1 - c]
    for px, py in chips:
        order += [4 * px + 2 * py + c, 4 * px + 2 * py + 1 - c]
    return jnp.stack(order).astype(jnp.int32)


def _inproj_gather(x2d, g_in, w_own, wp_own, tiny_own, order):
    t = x2d.shape[0]
    tm = 1024
    nt = t // tm

    def body(order_ref, x_ref, g_ref, w_own_ref, wp_own_ref, tiny_own_ref,
             proj_ref, h_ref, wg_ref, wpg_ref, tinyg_ref,
             w_all, h_all, send_sems, recv_sems, own_sems, out_sems):
        k, i = pl.program_id(0), pl.program_id(1)
        x, y, c = _place()
        me, sibling = (x, y, c), (x, y, 1 - c)
        chips = [(1 - x, y), (x, 1 - y), (1 - x, 1 - y)]
        srcs = [w_own_ref, wp_own_ref, tiny_own_ref]
        dsts = [w_all, wpg_ref, tinyg_ref]

        def copy(a, n, block, to, own_src=False):
            px, py, pc = block
            dst = dsts[a].at[4 * px + 2 * py + pc]
            return pltpu.make_async_remote_copy(
                src_ref=srcs[a] if own_src else dst, dst_ref=dst,
                send_sem=send_sems.at[a, n], recv_sem=recv_sems.at[a, n], device_id=to, device_id_type=MESH)

        def own_copy(a):
            return pltpu.make_async_copy(srcs[a], dsts[a].at[4 * x + 2 * y + c], own_sems.at[a])

        def keep_copy(n):
            return pltpu.make_async_copy(w_all.at[order_ref[n]], wg_ref.at[order_ref[n]], out_sems.at[n])

        def first_copies(a):
            return [copy(a, 0, me, sibling, True)] + [copy(a, 1 + j, me, (*chip, c), True) for j, chip in enumerate(chips)]

        def at_slot(n):
            return jnp.logical_and(k == n, i == 0)

        @pl.when(at_slot(0))
        def _():
            for a in range(3):
                own_copy(a).start()
            for a in range(3):
                for cp in first_copies(a):
                    cp.start()
            own_copy(0).wait()
            keep_copy(0).start()

        @pl.when(at_slot(1))
        def _():
            copy(0, 0, sibling, me).wait_recv()
            keep_copy(1).start()

        for j, chip in enumerate(chips):
            @pl.when(at_slot(2 + 2 * j))
            def _():
                copy(0, 1 + j, (*chip, c), me).wait_recv()
                copy(0, 4 + j, (*chip, c), sibling).start()
                keep_copy(2 + 2 * j).start()

            @pl.when(at_slot(3 + 2 * j))
            def _():
                copy(0, 4 + j, (*chip, 1 - c), me).wait_recv()
                keep_copy(3 + 2 * j).start()

        rows = pl.ds(pl.multiple_of(i * tm, tm), tm)

        @pl.when(k == 0)
        def _():
            xv = x_ref[...]
            r = lax.rsqrt(jnp.mean(xv * xv, axis=-1, keepdims=True) + EPS)
            hv = (xv * r * g_ref[...]).astype(BF16)
            h_ref[...] = hv
            h_all[rows, :] = hv

        proj_ref[...] = _dot(h_all[rows, :], w_all[order_ref[k]])

        @pl.when(jnp.logical_and(k == NSEG - 1, i == nt - 1))
        def _():
            for a in (1, 2):
                for j, chip in enumerate(chips):
                    copy(a, 1 + j, (*chip, c), me).wait_recv()
                    copy(a, 4 + j, (*chip, c), sibling).start()
            for a in (1, 2):
                copy(a, 0, sibling, me).wait_recv()
                for j, chip in enumerate(chips):
                    copy(a, 4 + j, (*chip, 1 - c), me).wait_recv()
            for a in range(3):
                for cp in first_copies(a):
                    cp.wait_send()
                for j, chip in enumerate(chips):
                    copy(a, 4 + j, (*chip, c), sibling).wait_send()
            for a in (1, 2):
                own_copy(a).wait()
            for n in range(NSEG):
                keep_copy(n).wait()

    hold = lambda k, i, order_ref: (jnp.where(k == 0, i, nt - 1), 0)
    return pl.pallas_call(
        body, name="inproj_gather",
        grid_spec=pltpu.PrefetchScalarGridSpec(
            num_scalar_prefetch=1, grid=(NSEG, nt),
            in_specs=[pl.BlockSpec((tm, D), hold),
                      pl.BlockSpec((1, D), lambda k, i, order_ref: (0, 0)),
                      ANY, ANY, ANY],
            out_specs=[pl.BlockSpec((None, tm, D), lambda k, i, order_ref: (order_ref[k], i, 0)),
                       pl.BlockSpec((tm, D), hold),
                       ANY, ANY, ANY],
            scratch_shapes=[pltpu.VMEM((NDEV, D, D), BF16), pltpu.VMEM((t, D), BF16),
                            pltpu.SemaphoreType.DMA((3, 7)), pltpu.SemaphoreType.DMA((3, 7)),
                            pltpu.SemaphoreType.DMA((3,)), pltpu.SemaphoreType.DMA((NSEG,))]),
        out_shape=[jax.ShapeDtypeStruct((NSEG, t, D), F32), jax.ShapeDtypeStruct((t, D), BF16),
                   jax.ShapeDtypeStruct((NDEV,) + w_own.shape, BF16),
                   jax.ShapeDtypeStruct((NDEV,) + wp_own.shape, BF16),
                   jax.ShapeDtypeStruct((NDEV,) + tiny_own.shape, F32)],
        compiler_params=_params(("arbitrary", "arbitrary")),
    )(order, x2d, g_in, w_own, wp_own, tiny_own)


def _tile_scan(a, u):
    row = lax.broadcasted_iota(jnp.int32, a.shape, 0)
    for d in (1, 2, 4):
        m = row >= d
        a_sh = pltpu.roll(a, d, 0)
        u_sh = pltpu.roll(u, d, 0)
        u = jnp.where(m, a * u_sh + u, u)
        a = jnp.where(m, a * a_sh, a)
    return a, u


def _tile_scan_rev(a, w):
    row = lax.broadcasted_iota(jnp.int32, a.shape, 0)
    for d in (1, 2, 4):
        m = row < 8 - d
        a_sh = pltpu.roll(a, 8 - d, 0)
        w_sh = pltpu.roll(w, 8 - d, 0)
        w = jnp.where(m, a * w_sh + w, w)
        a = jnp.where(m, a * a_sh, a)
    return a, w


def _lru_gates(xa_ref, c, cw_ref, cb_ref, wbd_ref, bx_ref, ba_ref, sp):
    win = _window_before(xa_ref, c, RC)
    xc = cb_ref[...] + cw_ref[3:4, :] * _shift_down(win, 0, RC)
    for s in (1, 2, 3):
        xc = xc + cw_ref[3 - s:4 - s, :] * _shift_down(win, s, RC)
    z = _dot(xc.astype(BF16), wbd_ref[...])
    gi = _sigmoid(z[:, :CB] + bx_ref[...])
    gr = _sigmoid(z[:, CB:] + ba_ref[...])
    log_a = -LRU_C * gr * sp
    return win, xc, gi, gr, log_a


def _lru_fwd(proj, conv_w, conv_b, wbd, bx, ba, lam, nb):
    t = nb * S

    def body(xa_ref, ga_ref, cw_ref, cb_ref, wbd_ref, bx_ref, ba_ref, lam_ref, ya_ref, hs_ref, a_s, u_s):
        sp = _softplus(-lam_ref[...])

        def gates(c, carry):
            _, xc, gi, _, log_a = _lru_gates(xa_ref, c, cw_ref, cb_ref, wbd_ref, bx_ref, ba_ref, sp)
            a_s[_rows(c, RC), :] = jnp.exp(log_a)
            u_s[_rows(c, RC), :] = jnp.sqrt(-_expm1_nonpos(2.0 * log_a)) * (gi * xc)
            return carry

        lax.fori_loop(0, S // RC, gates, 0)

        def scan(g, h):
            for k in range(SCAN_GROUP):
                rows = pl.ds(pl.multiple_of(g * (8 * SCAN_GROUP), 8 * SCAN_GROUP) + 8 * k, 8)
                a_cum, u_cum = _tile_scan(a_s[rows, :], u_s[rows, :])
                hs_ref[rows, :] = u_cum + a_cum * h
                h = u_cum[7:8, :] + a_cum[7:8, :] * h
            return h

        lax.fori_loop(0, S // (8 * SCAN_GROUP), scan, jnp.zeros((1, CB), F32))

        def gate_out(c, carry):
            ga = ga_ref[_rows(c, RC), :]
            ya_ref[_rows(c, RC), :] = (ga * _sigmoid(ga) * hs_ref[_rows(c, RC), :]).astype(BF16)
            return carry

        lax.fori_loop(0, S // RC, gate_out, 0)

    vec = pl.BlockSpec((1, CB), lambda b, cb: (0, cb))
    return pl.pallas_call(
        body, name="lru_fwd", grid=(nb, NCB),
        in_specs=[pl.BlockSpec((None, S, CB), lambda b, cb: (0, b, cb)),
                  pl.BlockSpec((None, S, CB), lambda b, cb: (1, b, cb)),
                  pl.BlockSpec((4, CB), lambda b, cb: (0, cb)),
                  vec,
                  pl.BlockSpec((None, CB, 2 * CB), lambda b, cb: (cb, 0, 0)),
                  vec, vec, vec],
        out_specs=[pl.BlockSpec((S, CB), lambda b, cb: (b, cb)),
                   pl.BlockSpec((S, CB), lambda b, cb: (b, cb))],
        out_shape=[jax.ShapeDtypeStruct((t, D), BF16), jax.ShapeDtypeStruct((t, D), F32)],
        scratch_shapes=[pltpu.VMEM((S, CB), F32), pltpu.VMEM((S, CB), F32)],
        compiler_params=_params(("arbitrary", "arbitrary")),
    )(proj, proj, conv_w, conv_b, wbd, bx, ba, lam)


def _lru_bwd(proj, hs, dya, conv_w, conv_b, wbd, bx, ba, lam, nb, give):
    t = nb * S
    ng = len(give)

    def body(xa_ref, ga_ref, hs_ref, dya_ref, cw_ref, cb_ref, wbd_ref, bx_ref, ba_ref, lam_ref, *rest):
        give_refs, rest = rest[:ng], rest[ng:]
        dp_ref, dwbd_ref, vec_ref = rest[:3]
        got_refs, rest = rest[3:3 + ng], rest[3 + ng:]
        a_s, xc_s, gi_s, gr_s, dl_s, dh_s, dxc_s, acc_s, send_sems, recv_sems = rest
        b = pl.program_id(1)
        exchange = _sibling_copies(give_refs, got_refs, send_sems, recv_sems)

        @pl.when(jnp.logical_and(pl.program_id(0) == 0, b == 0))
        def _():
            for cp in exchange:
                cp.start()

        lam_v = lam_ref[...]
        sp = _softplus(-lam_v)
        acc_s[...] = jnp.zeros_like(acc_s)

        @pl.when(b == 0)
        def _():
            dwbd_ref[...] = jnp.zeros_like(dwbd_ref)
            vec_ref[...] = jnp.zeros_like(vec_ref)

        def gates(c, carry):
            _, xc, gi, gr, log_a = _lru_gates(xa_ref, c, cw_ref, cb_ref, wbd_ref, bx_ref, ba_ref, sp)
            rows = _rows(c, RC)
            a_s[rows, :] = jnp.exp(log_a)
            xc_s[rows, :] = xc
            gi_s[rows, :] = gi
            gr_s[rows, :] = gr
            ga = ga_ref[rows, :]
            sg = _sigmoid(ga)
            dya_c = dya_ref[rows, :]
            dl_s[rows, :] = dya_c * (ga * sg)
            dp_ref[1, rows, :] = (dya_c * hs_ref[rows, :] * (sg * (1.0 + ga * (1.0 - sg)))).astype(BF16)
            return carry

        lax.fori_loop(0, S // RC, gates, 0)

        def scan(i, g_in):
            base = pl.multiple_of((S // (8 * SCAN_GROUP) - 1 - i) * (8 * SCAN_GROUP), 8 * SCAN_GROUP)
            row = lax.broadcasted_iota(jnp.int32, (8, CB), 0)
            for k in reversed(range(SCAN_GROUP)):
                rows = pl.ds(base + 8 * k, 8)
                a = a_s[rows, :]
                dl = dl_s[rows, :]
                a_cum, g_loc = _tile_scan_rev(a, a * dl)
                g = g_loc + a_cum * g_in
                dh_s[rows, :] = dl + jnp.where(row < 7, pltpu.roll(g, 7, 0), g_in)
                g_in = g_loc[0:1, :] + a_cum[0:1, :] * g_in
            return g_in

        lax.fori_loop(0, S // (8 * SCAN_GROUP), scan, jnp.zeros((1, CB), F32))

        dxc_s[pl.ds(S, 8), :] = jnp.zeros((8, CB), F32)

        def grads(c, carry):
            rows = _rows(c, RC)
            dh = dh_s[rows, :]
            h_prev = _shift_down(_window_before(hs_ref, c, RC), 1, RC)
            xc, gi, gr, a = xc_s[rows, :], gi_s[rows, :], gr_s[rows, :], a_s[rows, :]
            mult = jnp.sqrt(-_expm1_nonpos(-2.0 * LRU_C * gr * sp))
            dmult = dh * gi * xc
            d_log_a = dh * h_prev * a - dmult * (a * a) / mult
            dzi = dh * mult * xc * gi * (1.0 - gi)
            dzr = d_log_a * (-LRU_C * sp) * gr * (1.0 - gr)
            dz = jnp.concatenate([dzi, dzr], axis=1).astype(BF16)
            dxc_s[rows, :] = dh * mult * gi + _dot_nt(dz, wbd_ref[...])
            dwbd_ref[...] += _dot_tn(xc.astype(BF16), dz)
            acc_s[1:2, :] += jnp.sum(dzi, axis=0, keepdims=True)
            acc_s[2:3, :] += jnp.sum(dzr, axis=0, keepdims=True)
            acc_s[3:4, :] += jnp.sum(d_log_a * (-LRU_C * gr), axis=0, keepdims=True)
            return carry

        lax.fori_loop(0, S // RC, grads, 0)

        def conv_bwd(c, carry):
            rows = _rows(c, RC)
            dwin = dxc_s[pl.ds(pl.multiple_of(c * RC, RC), RC + 8), :]
            dxc = dwin[:RC, :]
            xwin = _window_before(xa_ref, c, RC)
            dxa = cw_ref[3:4, :] * dxc
            acc_s[0:1, :] += jnp.sum(dxc, axis=0, keepdims=True)
            acc_s[7:8, :] += jnp.sum(dxc * _shift_down(xwin, 0, RC), axis=0, keepdims=True)
            for s in (1, 2, 3):
                dxa = dxa + cw_ref[3 - s:4 - s, :] * _shift_up(dwin, s, RC)
                acc_s[7 - s:8 - s, :] += jnp.sum(dxc * _shift_down(xwin, s, RC), axis=0, keepdims=True)
            dp_ref[0, rows, :] = dxa.astype(BF16)
            return carry

        lax.fori_loop(0, S // RC, conv_bwd, 0)

        row = lax.broadcasted_iota(jnp.int32, acc_s.shape, 0)
        vec_ref[...] += jnp.where(row == 3, acc_s[...] * (-_sigmoid(-lam_v)), acc_s[...])

        @pl.when(jnp.logical_and(pl.program_id(0) == NCB - 1, b == nb - 1))
        def _():
            for cp in exchange:
                cp.wait()

    vec = pl.BlockSpec((1, CB), lambda cb, b: (0, cb))
    blk = pl.BlockSpec((S, CB), lambda cb, b: (b, cb))
    return pl.pallas_call(
        body, name="lru_bwd", grid=(NCB, nb),
        in_specs=[pl.BlockSpec((None, S, CB), lambda cb, b: (0, b, cb)),
                  pl.BlockSpec((None, S, CB), lambda cb, b: (1, b, cb)),
                  blk, blk,
                  pl.BlockSpec((4, CB), lambda cb, b: (0, cb)),
                  vec,
                  pl.BlockSpec((None, CB, 2 * CB), lambda cb, b: (cb, 0, 0)),
                  vec, vec, vec] + [ANY] * ng,
        out_specs=[pl.BlockSpec((2, S, CB), lambda cb, b: (0, b, cb)),
                   pl.BlockSpec((None, CB, 2 * CB), lambda cb, b: (cb, 0, 0)),
                   pl.BlockSpec((8, CB), lambda cb, b: (0, cb))] + [ANY] * ng,
        out_shape=[jax.ShapeDtypeStruct((2, t, D), BF16),
                   jax.ShapeDtypeStruct((NCB, CB, 2 * CB), F32),
                   jax.ShapeDtypeStruct((8, D), F32)]
        + [jax.ShapeDtypeStruct((4,) + g.shape[1:], g.dtype) for g in give],
        scratch_shapes=[pltpu.VMEM((S, CB), F32), pltpu.VMEM((S, CB), F32), pltpu.VMEM((S, CB), F32),
                        pltpu.VMEM((S, CB), F32), pltpu.VMEM((S, CB), F32), pltpu.VMEM((S, CB), F32),
                        pltpu.VMEM((S + 8, CB), F32), pltpu.VMEM((8, CB), F32),
                        pltpu.SemaphoreType.DMA((ng, 4)), pltpu.SemaphoreType.DMA((ng, 4))],
        compiler_params=_params(("arbitrary", "arbitrary")),
    )(proj, proj, hs, dya, conv_w, conv_b, wbd, bx, ba, lam, *give)


def _retention_tables():
    log_g = jnp.log1p(-(2.0 ** (-5.0 - jnp.arange(HEADS, dtype=F32))))
    idx = jnp.arange(CH, dtype=F32)
    diff = idx[:, None] - idx[None, :]
    inner = jnp.where(diff >= 0, jnp.exp(jnp.maximum(diff, 0.0)[None] * log_g[:, None, None]), 0.0)
    cross = jnp.exp((idx[None, :] + 1.0) * log_g[:, None])
    state = jnp.exp((CH - 1.0 - idx[None, :]) * log_g[:, None])
    cross = jnp.broadcast_to(cross[:, :, None], (HEADS, CH, DK))
    state = jnp.broadcast_to(state[:, :, None], (HEADS, CH, DK))
    half = DK // 2
    freqs = 10000.0 ** (-jnp.arange(half, dtype=F32) / half)
    ang = jnp.arange(S, dtype=F32)[:, None] * freqs[None, :]
    return inner, cross, state, jnp.cos(ang), jnp.sin(ang)


def _rotate(x, cos, sin):
    half = DK // 2
    x1, x2 = x[:, :half], x[:, half:]
    return jnp.concatenate([x1 * cos - x2 * sin, x1 * sin + x2 * cos], axis=1)


def _rotate_back(d, cos, sin):
    half = DK // 2
    d1, d2 = d[:, :half], d[:, half:]
    return jnp.concatenate([d1 * cos + d2 * sin, d2 * cos - d1 * sin], axis=1)


def _ret_fwd(proj, gain, tables, nb):
    t = nb * S
    inner_t, cross_t, state_t, cos_t, sin_t = tables

    def body(q_ref, k_ref, v_ref, gb_ref, gain_ref, dm_ref, cd_ref, sd_ref, cos_ref, sin_ref,
             yb_ref, qr_ref, kr_ref, o_ref, rs_ref, r_s):
        r_s[...] = jnp.zeros_like(r_s)
        chunk_decay = cd_ref[CH - 1:CH, :]

        def chunk(c, carry):
            rows = _rows(c, CH)
            cos, sin = cos_ref[rows, :], sin_ref[rows, :]
            qr = _rotate(q_ref[rows, :], cos, sin).astype(BF16)
            kr = (_rotate(k_ref[rows, :], cos, sin) * (DK ** -0.5)).astype(BF16)
            v = v_ref[rows, :]
            qr_ref[rows, :] = qr
            kr_ref[rows, :] = kr
            r = r_s[...]
            rb = r.astype(BF16)
            rs_ref[c] = rb
            p = (_dot_nt(qr, kr) * dm_ref[...]).astype(BF16)
            o = _dot(p, v.astype(BF16)) + _dot(qr, rb) * cd_ref[...]
            r_s[...] = chunk_decay * r + _dot_tn(kr, (v * sd_ref[...]).astype(BF16))
            o_ref[rows, :] = o
            oc = o - jnp.mean(o, axis=-1, keepdims=True)
            rstd = lax.rsqrt(jnp.mean(oc * oc, axis=-1, keepdims=True) + EPS)
            gb = gb_ref[rows, :]
            yb_ref[rows, :] = (gb * _sigmoid(gb) * (oc * rstd * gain_ref[...])).astype(BF16)
            return carry

        lax.fori_loop(0, NCH, chunk, 0)

    seg = lambda s: pl.BlockSpec((None, S, DK), lambda b, h: (s, b, h))
    tab = pl.BlockSpec((None, CH, DK), lambda b, h: (h, 0, 0))
    rot = pl.BlockSpec((S, DK // 2), lambda b, h: (0, 0))
    blk = pl.BlockSpec((S, DK), lambda b, h: (b, h))
    return pl.pallas_call(
        body, name="ret_fwd", grid=(nb, HEADS),
        in_specs=[seg(2), seg(3), seg(4), seg(5),
                  pl.BlockSpec((None, 1, DK), lambda b, h: (h, 0, 0)),
                  tab, tab, tab, rot, rot],
        out_specs=[blk, blk, blk, blk,
                   pl.BlockSpec((None, None, NCH, DK, DK), lambda b, h: (b, h, 0, 0, 0))],
        out_shape=[jax.ShapeDtypeStruct((t, D), BF16), jax.ShapeDtypeStruct((t, D), BF16),
                   jax.ShapeDtypeStruct((t, D), BF16), jax.ShapeDtypeStruct((t, D), F32),
                   jax.ShapeDtypeStruct((nb, HEADS, NCH, DK, DK), BF16)],
        scratch_shapes=[pltpu.VMEM((DK, DK), F32)],
        compiler_params=_params(("arbitrary", "arbitrary")),
    )(proj, proj, proj, proj, gain, inner_t, cross_t, state_t, cos_t, sin_t)


def _ret_bwd(proj, qr, kr, o, rs, dyb, gain, tables, nb, sums):
    t = nb * S
    ns = len(sums)
    inner_t, cross_t, state_t, cos_t, sin_t = tables

    def body(qr_ref, kr_ref, v_ref, gb_ref, o_ref, dyb_ref, rs_ref, gain_ref, dm_ref, cd_ref, sd_ref,
             cos_ref, sin_ref, *rest):
        sum_refs, rest = rest[:ns], rest[ns:]
        dp_ref, dgain_ref = rest[:2]
        part_refs, rest = rest[2:2 + ns], rest[2 + ns:]
        dr_s, send_sems, recv_sems, local_sems = rest
        mine, sends, recvs = _chip_copies(sum_refs, part_refs, send_sems, recv_sems, local_sems)

        @pl.when(jnp.logical_and(pl.program_id(0) == 0, pl.program_id(1) == 0))
        def _():
            for cp in mine + sends:
                cp.start()

        dr_s[...] = jnp.zeros_like(dr_s)
        chunk_decay = cd_ref[CH - 1:CH, :]

        @pl.when(pl.program_id(1) == 0)
        def _():
            dgain_ref[...] = jnp.zeros_like(dgain_ref)

        def chunk(i, carry):
            c = NCH - 1 - i
            rows = _rows(c, CH)
            gain_v = gain_ref[...]
            o_c = o_ref[rows, :]
            oc = o_c - jnp.mean(o_c, axis=-1, keepdims=True)
            rstd = lax.rsqrt(jnp.mean(oc * oc, axis=-1, keepdims=True) + EPS)
            yn = oc * rstd
            gb = gb_ref[rows, :]
            sg = _sigmoid(gb)
            dyb_c = dyb_ref[rows, :]
            dgn = dyb_c * (gb * sg)
            dp_ref[3, rows, :] = (dyb_c * (yn * gain_v) * (sg * (1.0 + gb * (1.0 - sg)))).astype(BF16)
            dgain_ref[...] += jnp.sum(dgn * yn, axis=0, keepdims=True)
            dyn = dgn * gain_v
            do = rstd * (dyn - jnp.mean(dyn, axis=-1, keepdims=True)
                         - yn * jnp.mean(dyn * yn, axis=-1, keepdims=True))
            dob = do.astype(BF16)
            dox = (do * cd_ref[...]).astype(BF16)

            q_c, k_c = qr_ref[rows, :], kr_ref[rows, :]
            v = v_ref[rows, :]
            vb = v.astype(BF16)
            vs = (v * sd_ref[...]).astype(BF16)
            rb = rs_ref[c]
            d_r = dr_s[...]
            drb = d_r.astype(BF16)
            dm = dm_ref[...]
            p = (_dot_nt(q_c, k_c) * dm).astype(BF16)
            dpm = (_dot_nt(dob, vb) * dm).astype(BF16)
            dq = _dot(dpm, k_c) + _dot_nt(dox, rb)
            dk = _dot_tn(dpm, q_c) + _dot_nt(vs, drb)
            dv = _dot_tn(p, dob) + _dot(k_c, drb) * sd_ref[...]
            dr_s[...] = chunk_decay * d_r + _dot_tn(q_c, dox)

            cos, sin = cos_ref[rows, :], sin_ref[rows, :]
            dp_ref[0, rows, :] = _rotate_back(dq, cos, sin).astype(BF16)
            dp_ref[1, rows, :] = (_rotate_back(dk, cos, sin) * (DK ** -0.5)).astype(BF16)
            dp_ref[2, rows, :] = dv.astype(BF16)
            return carry

        lax.fori_loop(0, NCH, chunk, 0)

        @pl.when(jnp.logical_and(pl.program_id(0) == HEADS - 1, pl.program_id(1) == nb - 1))
        def _():
            for cp in recvs:
                cp.wait_recv()
            for cp in sends:
                cp.wait_send()
            for cp in mine:
                cp.wait()

    seg = lambda s: pl.BlockSpec((None, S, DK), lambda h, b: (s, b, h))
    tab = pl.BlockSpec((None, CH, DK), lambda h, b: (h, 0, 0))
    rot = pl.BlockSpec((S, DK // 2), lambda h, b: (0, 0))
    blk = pl.BlockSpec((S, DK), lambda h, b: (b, h))
    one = pl.BlockSpec((None, 1, DK), lambda h, b: (h, 0, 0))
    return pl.pallas_call(
        body, name="ret_bwd", grid=(HEADS, nb),
        in_specs=[blk, blk, seg(4), seg(5), blk, blk,
                  pl.BlockSpec((None, None, NCH, DK, DK), lambda h, b: (b, h, 0, 0, 0)),
                  one, tab, tab, tab, rot, rot] + [ANY] * ns,
        out_specs=[pl.BlockSpec((4, S, DK), lambda h, b: (0, b, h)), one] + [ANY] * ns,
        out_shape=[jax.ShapeDtypeStruct((4, t, D), BF16), jax.ShapeDtypeStruct((HEADS, 1, DK), F32)]
        + [jax.ShapeDtypeStruct(a.shape, a.dtype) for a in sums],
        scratch_shapes=[pltpu.VMEM((DK, DK), F32), pltpu.SemaphoreType.DMA((ns, 3)), pltpu.SemaphoreType.DMA((ns, 3)),
                        pltpu.SemaphoreType.DMA((ns,))],
        compiler_params=_params(("arbitrary", "arbitrary")),
    )(qr, kr, proj, proj, o, dyb, rs, gain, inner_t, cross_t, state_t, cos_t, sin_t, *sums)


def _wblock(k):
    return pl.BlockSpec((NDEV, D // NDEV, D), lambda i: (0, k, 0))


def _tail(ya, yb, proj, x2d, tgt, wg, g_fin):
    t = x2d.shape[0]
    tm = 256

    def body(ya_ref, yb_ref, ma_ref, mb_ref, x_ref, t_ref, wa_ref, wb_ref, wo_ref, g_ref,
             dx2_ref, dya_ref, dyb_ref, dm_ref, mg_ref, doa_ref, dob_ref, gfin_ref, loss_ref):
        i = pl.program_id(0)

        @pl.when(i == 0)
        def _():
            gfin_ref[...] = jnp.zeros_like(gfin_ref)
            loss_ref[...] = jnp.zeros_like(loss_ref)

        wa = wa_ref[...].reshape(D, D)
        wb = wb_ref[...].reshape(D, D)
        wo = wo_ref[...].reshape(D, D)
        out_a = _dot(ya_ref[...], wa)
        out_b = _dot(yb_ref[...], wb)
        sa = _sigmoid(ma_ref[...])
        sb = _sigmoid(mb_ref[...])
        merged = (sa * out_a + sb * out_b).astype(BF16)
        mg_ref[...] = merged
        x2 = x_ref[...] + _dot(merged, wo)
        r2 = lax.rsqrt(jnp.mean(x2 * x2, axis=-1, keepdims=True) + EPS)
        xh = x2 * r2
        g = g_ref[...]
        err = xh * g - t_ref[...]
        loss_ref[...] += jnp.sum(err * err, axis=0, keepdims=True) * (0.5 / D)
        dy = err * (1.0 / D)
        gfin_ref[...] += jnp.sum(dy * xh, axis=0, keepdims=True)
        dxh = dy * g
        dx2 = r2 * (dxh - xh * jnp.mean(dxh * xh, axis=-1, keepdims=True))
        dx2_ref[...] = dx2
        dmerged = _dot_nt(dx2.astype(BF16), wo)
        doa = (sa * dmerged).astype(BF16)
        dob = (sb * dmerged).astype(BF16)
        doa_ref[...] = doa
        dob_ref[...] = dob
        dm_ref[0] = (dmerged * out_a * sa * (1.0 - sa)).astype(BF16)
        dm_ref[1] = (dmerged * out_b * sb * (1.0 - sb)).astype(BF16)
        dya_ref[...] = _dot_nt(doa, wa)
        dyb_ref[...] = _dot_nt(dob, wb)

    row = lambda: pl.BlockSpec((tm, D), lambda i: (i, 0))
    seg = lambda s: pl.BlockSpec((None, tm, D), lambda i: (s, i, 0))
    vec = pl.BlockSpec((1, D), lambda i: (0, 0))
    return pl.pallas_call(
        body, name="tail", grid=(t // tm,),
        in_specs=[row(), row(), seg(6), seg(7), row(), row(), _wblock(0), _wblock(1), _wblock(2), vec],
        out_specs=[row(), row(), row(), pl.BlockSpec((2, tm, D), lambda i: (0, i, 0)),
                   row(), row(), row(), vec, vec],
        out_shape=[jax.ShapeDtypeStruct((t, D), F32), jax.ShapeDtypeStruct((t, D), F32),
                   jax.ShapeDtypeStruct((t, D), F32), jax.ShapeDtypeStruct((2, t, D), BF16),
                   jax.ShapeDtypeStruct((t, D), BF16), jax.ShapeDtypeStruct((t, D), BF16),
                   jax.ShapeDtypeStruct((t, D), BF16), jax.ShapeDtypeStruct((1, D), F32),
                   jax.ShapeDtypeStruct((1, D), F32)],
        compiler_params=_params(("arbitrary",)),
    )(ya, yb, proj, proj, x2d, tgt, wg, wg, wg, g_fin)


def _tail_wgrad(ya, yb, merged, doa, dob, dx2):
    t = ya.shape[0]
    tm = 512

    def body(ya_ref, yb_ref, mg_ref, doa_ref, dob_ref, dx2_ref, ga_ref, gb_ref, go_ref):
        @pl.when(pl.program_id(0) == 0)
        def _():
            ga_ref[...] = jnp.zeros_like(ga_ref)
            gb_ref[...] = jnp.zeros_like(gb_ref)
            go_ref[...] = jnp.zeros_like(go_ref)

        ga_ref[...] += _dot_tn(ya_ref[...], doa_ref[...])
        gb_ref[...] += _dot_tn(yb_ref[...], dob_ref[...])
        go_ref[...] += _dot_tn(mg_ref[...], dx2_ref[...].astype(BF16))

    row = lambda: pl.BlockSpec((tm, D), lambda i: (i, 0))
    full = lambda: pl.BlockSpec((D, D), lambda i: (0, 0))
    return pl.pallas_call(
        body, name="tail_wgrad", grid=(t // tm,),
        in_specs=[row() for _ in range(6)], out_specs=[full(), full(), full()],
        out_shape=[jax.ShapeDtypeStruct((D, D), F32)] * 3,
        compiler_params=_params(("arbitrary",)),
    )(ya, yb, merged, doa, dob, dx2)


def _dproj_specs(tm, j_of, i_of):
    last = lambda j, i, lo, n: (jnp.clip(j - lo, 0, n - 1), i, 0)
    return [pl.BlockSpec((None, tm, D), lambda a, b: last(j_of(a, b), i_of(a, b), 0, 2)),
            pl.BlockSpec((None, tm, D), lambda a, b: last(j_of(a, b), i_of(a, b), 2, 4)),
            pl.BlockSpec((None, tm, D), lambda a, b: last(j_of(a, b), i_of(a, b), 6, 2))]


def _dproj_specs_ordered(tm):
    def spec(lo, n):
        def index(k, i, order_ref):
            seg = order_ref[k]
            mine = jnp.logical_and(seg >= lo, seg < lo + n)
            return jnp.where(mine, seg - lo, 0), jnp.where(mine, i, 0), 0
        return pl.BlockSpec((None, tm, D), index)
    return [spec(0, 2), spec(2, 4), spec(6, 2)]


def _dproj_pick(j, da_ref, db_ref, dc_ref, use):
    @pl.when(j < 2)
    def _():
        use(da_ref[...])

    @pl.when(jnp.logical_and(j >= 2, j < 6))
    def _():
        use(db_ref[...])

    @pl.when(j >= 6)
    def _():
        use(dc_ref[...])


def _rs_schedule(q, c):
    steps = []
    for s in range(3):
        d_a = lax.rem(q + 1 + s, 4)
        d_b = lax.rem(q + 1 + (s + 1) % 3, 4)
        steps.append((jnp.where(c == 0, d_a, d_b), jnp.where(c == 0, d_b, d_a)))
    steps.append((q, q))
    return steps


def _rs_order(q, c):
    order = []
    for keep, give in _rs_schedule(q, c):
        order += [2 * give + 1 - c, 2 * keep + c]
    return jnp.stack(order).astype(jnp.int32)


def _inproj_wgrad_rs(h, dpa, dpb, dpc, order):
    t = h.shape[0]
    tm = 1024
    nt = t // tm

    def body(order_ref, h_ref, da_ref, db_ref, dc_ref, parts_ref, acc, sib, outb,
             give_send, give_recv, sum_send, sum_recv, own_sem):
        k, i = pl.program_id(0), pl.program_id(1)
        x, y, c = _place()
        schedule = _rs_schedule(2 * x + y, c)

        def use(d):
            @pl.when(i == 0)
            def _():
                acc[k % 2] = _dot_tn(h_ref[...], d)

            @pl.when(i > 0)
            def _():
                acc[k % 2] += _dot_tn(h_ref[...], d)

        _dproj_pick(order_ref[k], da_ref, db_ref, dc_ref, use)

        def give_copy(s):
            return pltpu.make_async_remote_copy(
                src_ref=acc.at[0], dst_ref=sib.at[s % 2], send_sem=give_send.at[s], recv_sem=give_recv.at[s],
                device_id=(x, y, 1 - c), device_id_type=MESH)

        def sum_copy(s):
            keep = schedule[s][0]
            return pltpu.make_async_remote_copy(
                src_ref=outb.at[s], dst_ref=parts_ref.at[s], send_sem=sum_send.at[s], recv_sem=sum_recv.at[s],
                device_id=(keep // 2, lax.rem(keep, 2), c), device_id_type=MESH)

        own_copy = pltpu.make_async_copy(outb.at[3], parts_ref.at[3], own_sem)

        for s in range(4):
            @pl.when(jnp.logical_and(k == 2 * s, i == nt - 1))
            def _():
                give_copy(s).start()

            @pl.when(jnp.logical_and(k == 2 * s + 1, i == nt - 1))
            def _():
                give_copy(s).wait_recv()
                outb[s] = (acc[1] + sib[s % 2]).astype(BF16)
                give_copy(s).wait_send()
                if s < 3:
                    sum_copy(s).start()
                else:
                    own_copy.start()

        @pl.when(jnp.logical_and(k == NSEG - 1, i == nt - 1))
        def _():
            for s in range(3):
                sum_copy(s).wait_recv()
            for s in range(3):
                sum_copy(s).wait_send()
            own_copy.wait()

    return pl.pallas_call(
        body, name="inproj_wgrad_rs",
        grid_spec=pltpu.PrefetchScalarGridSpec(
            num_scalar_prefetch=1, grid=(NSEG, nt),
            in_specs=[pl.BlockSpec((tm, D), lambda k, i, order_ref: (i, 0))] + _dproj_specs_ordered(tm),
            out_specs=ANY,
            scratch_shapes=[pltpu.VMEM((2, D, D), F32), pltpu.VMEM((2, D, D), F32), pltpu.VMEM((4, D, D), BF16),
                            pltpu.SemaphoreType.DMA((4,)), pltpu.SemaphoreType.DMA((4,)),
                            pltpu.SemaphoreType.DMA((3,)), pltpu.SemaphoreType.DMA((3,)),
                            pltpu.SemaphoreType.DMA]),
        out_shape=jax.ShapeDtypeStruct((4, D, D), BF16),
        compiler_params=_params(("arbitrary", "arbitrary")),
    )(order, h, dpa, dpb, dpc)


def _inproj_dgrad(dpa, dpb, dpc, wg, x2d, dx2, g_in):
    t = x2d.shape[0]
    tm = 512

    def body(da_ref, db_ref, dc_ref, w_ref, x_ref, dx2_ref, g_ref, gx_ref, gg_ref, acc_s):
        i, j = pl.program_id(0), pl.program_id(1)

        @pl.when(jnp.logical_and(i == 0, j == 0))
        def _():
            gg_ref[...] = jnp.zeros_like(gg_ref)

        @pl.when(j == 0)
        def _():
            acc_s[...] = jnp.zeros_like(acc_s)

        def use(d):
            acc_s[...] += _dot_nt(d, w_ref[...])

        _dproj_pick(j, da_ref, db_ref, dc_ref, use)

        @pl.when(j == NSEG - 1)
        def _():
            x = x_ref[...]
            r = lax.rsqrt(jnp.mean(x * x, axis=-1, keepdims=True) + EPS)
            xh = x * r
            dh = acc_s[...]
            gg_ref[...] += jnp.sum(dh * xh, axis=0, keepdims=True)
            dxh = dh * g_ref[...]
            gx_ref[...] = dx2_ref[...] + r * (dxh - xh * jnp.mean(dxh * xh, axis=-1, keepdims=True))

    row = lambda: pl.BlockSpec((tm, D), lambda i, j: (i, 0))
    vec = pl.BlockSpec((1, D), lambda i, j: (0, 0))
    return pl.pallas_call(
        body, name="inproj_dgrad", grid=(t // tm, NSEG),
        in_specs=_dproj_specs(tm, lambda i, j: j, lambda i, j: i)
        + [pl.BlockSpec((None, D, D), lambda i, j: (j, 0, 0)), row(), row(), vec],
        out_specs=[row(), vec],
        out_shape=[jax.ShapeDtypeStruct((t, D), F32), jax.ShapeDtypeStruct((1, D), F32)],
        scratch_shapes=[pltpu.VMEM((tm, D), F32)],
        compiler_params=_params(("arbitrary", "arbitrary")),
    )(dpa, dpb, dpc, wg, x2d, dx2, g_in)


def _adamw(name, parts, w, m, v):
    n, rows, cols = parts.shape
    tr = rows if rows <= 256 else 256

    def body(p_ref, w_ref, m_ref, v_ref, g_ref, d_ref, nm_ref, nv_ref):
        g = p_ref[0].astype(F32)
        for k in range(1, n):
            g = g + p_ref[k].astype(F32)
        m_new = ADAM_B1 * m_ref[...] + (1.0 - ADAM_B1) * g
        v_new = ADAM_B2 * v_ref[...] + (1.0 - ADAM_B2) * (g * g)
        m_hat = m_new / (1.0 - ADAM_B1 ** ADAM_STEP)
        v_hat = v_new / (1.0 - ADAM_B2 ** ADAM_STEP)
        g_ref[...] = g
        d_ref[...] = -ADAM_LR * (m_hat / (jnp.sqrt(v_hat) + ADAM_EPS) + ADAM_WD * w_ref[...])
        nm_ref[...] = m_new
        nv_ref[...] = v_new

    blk = lambda: pl.BlockSpec((tr, cols), lambda i: (i, 0))
    return pl.pallas_call(
        body, name=name, grid=(rows // tr,),
        in_specs=[pl.BlockSpec((n, tr, cols), lambda i: (0, i, 0)), blk(), blk(), blk()],
        out_specs=[blk(), blk(), blk(), blk()],
        out_shape=[jax.ShapeDtypeStruct((rows, cols), F32)] * 4,
        compiler_params=_params(("arbitrary",)),
    )(parts, w, m, v)


ANY = pl.BlockSpec(memory_space=pl.ANY)


def _place():
    return lax.axis_index("x"), lax.axis_index("y"), lax.axis_index("c")


def _allgather(name, arrs):
    n = len(arrs)

    def body(*refs):
        ins, outs = refs[:n], refs[n:2 * n]
        send_sems, recv_sems, local_sems = refs[2 * n:]
        x, y, c = _place()
        me, sibling = (x, y, c), (x, y, 1 - c)
        chips = [(1 - x, y), (x, 1 - y), (1 - x, 1 - y)]

        def copy(a, k, block, to, src=None):
            px, py, pc = block
            dst = outs[a].at[4 * px + 2 * py + pc]
            return pltpu.make_async_remote_copy(
                src_ref=dst if src is None else src, dst_ref=dst,
                send_sem=send_sems.at[a, k], recv_sem=recv_sems.at[a, k], device_id=to, device_id_type=MESH)

        mine = [pltpu.make_async_copy(ins[a], outs[a].at[4 * x + 2 * y + c], local_sems.at[a]) for a in range(n)]
        for cp in mine:
            cp.start()
        first = []
        for a in range(n):
            first.append(copy(a, 0, me, sibling, src=ins[a]))
            first += [copy(a, 1 + j, me, (*chip, c), src=ins[a]) for j, chip in enumerate(chips)]
        for cp in first:
            cp.start()
        passed = []
        for j, chip in enumerate(chips):
            for a in range(n):
                copy(a, 1 + j, (*chip, c), me).wait_recv()
                fwd = copy(a, 4 + j, (*chip, c), sibling)
                fwd.start()
                passed.append(fwd)
        for a in range(n):
            copy(a, 0, sibling, me).wait_recv()
            for j, chip in enumerate(chips):
                copy(a, 4 + j, (*chip, 1 - c), me).wait_recv()
        for cp in first + passed:
            cp.wait_send()
        for cp in mine:
            cp.wait()

    return pl.pallas_call(
        body, name=name,
        in_specs=[ANY] * n, out_specs=[ANY] * n,
        out_shape=[jax.ShapeDtypeStruct((NDEV,) + a.shape, a.dtype) for a in arrs],
        scratch_shapes=[pltpu.SemaphoreType.DMA((n, 7)), pltpu.SemaphoreType.DMA((n, 7)),
                        pltpu.SemaphoreType.DMA((n,))],
    )(*arrs)


def _sibling_copies(ins, outs, send_sems, recv_sems):
    x, y, c = _place()
    return [pltpu.make_async_remote_copy(
        src_ref=ins[a].at[2 * q + 1 - c], dst_ref=outs[a].at[q],
        send_sem=send_sems.at[a, q], recv_sem=recv_sems.at[a, q],
        device_id=(x, y, 1 - c), device_id_type=MESH) for a in range(len(ins)) for q in range(4)]


def _chip_copies(ins, outs, send_sems, recv_sems, local_sems):
    x, y, c = _place()
    my_chip = 2 * x + y
    chips = [(1 - x, y), (x, 1 - y), (1 - x, 1 - y)]
    n = len(ins)
    mine = [pltpu.make_async_copy(ins[a].at[my_chip], outs[a].at[my_chip], local_sems.at[a]) for a in range(n)]
    sends = [pltpu.make_async_remote_copy(
        src_ref=ins[a].at[2 * px + py], dst_ref=outs[a].at[my_chip],
        send_sem=send_sems.at[a, j], recv_sem=recv_sems.at[a, j],
        device_id=(px, py, c), device_id_type=MESH) for a in range(n) for j, (px, py) in enumerate(chips)]
    recvs = [pltpu.make_async_remote_copy(
        src_ref=ins[a].at[my_chip], dst_ref=outs[a].at[2 * px + py],
        send_sem=send_sems.at[a, j], recv_sem=recv_sems.at[a, j],
        device_id=(px, py, c), device_id_type=MESH) for a in range(n) for j, (px, py) in enumerate(chips)]
    return mine, sends, recvs


def _chip_sum(owns, gots, core):
    n = len(owns)
    _, rows, cols = owns[0].shape

    def body(core_ref, *refs):
        for a in range(n):
            refs[2 * n + a][...] = (refs[a][...] + refs[n + a][...]).astype(BF16)

    own_spec = pl.BlockSpec((None, rows, cols), lambda q, core_ref: (2 * q + core_ref[0], 0, 0))
    slab = pl.BlockSpec((None, rows, cols), lambda q, core_ref: (q, 0, 0))
    return pl.pallas_call(
        body, name="chip_sum",
        grid_spec=pltpu.PrefetchScalarGridSpec(
            num_scalar_prefetch=1, grid=(4,),
            in_specs=[own_spec] * n + [slab] * n, out_specs=[slab] * n),
        out_shape=[jax.ShapeDtypeStruct((4, rows, cols), BF16)] * n,
        compiler_params=_params(("arbitrary",)),
    )(core, *owns, *gots)


def _block_diag(w):
    w4 = w.reshape(NCB, 4, 64, 64)
    eye = jnp.eye(4, dtype=w.dtype)
    return (w4[:, :, :, None, :] * eye[None, :, None, :, None]).reshape(NCB, CB, CB)


def _block_diag_back(g):
    g5 = g.reshape(NCB, 4, 64, 4, 64)
    return jnp.stack([g5[:, m, :, m, :] for m in range(4)], axis=1).reshape(16, 64, 64)


def kernel(x, norm_in, w_in, conv_w, conv_b, gate_x_w, gate_x_b, gate_a_w, gate_a_b, lru_lambda, gn_gain, w_proj_a, w_proj_b, w_out, norm_final, loss_target, m_norm_in, m_w_in, m_conv_w, m_conv_b, m_gate_x_w, m_gate_x_b, m_gate_a_w, m_gate_a_b, m_lru_lambda, m_gn_gain, m_w_proj_a, m_w_proj_b, m_w_out, m_norm_final, v_norm_in, v_w_in, v_conv_w, v_conv_b, v_gate_x_w, v_gate_x_b, v_gate_a_w, v_gate_a_b, v_lru_lambda, v_gn_gain, v_w_proj_a, v_w_proj_b, v_w_out, v_norm_final):
    xi, yi, ci = _place()
    me = 4 * xi + 2 * yi + ci
    core = ci.astype(jnp.int32).reshape(1)
    nshard = D // NDEV
    nb = x.shape[0]
    t = nb * S
    x2d = x.reshape(t, D)
    tgt2d = loss_target.reshape(t, D)
    g_final = norm_final.reshape(1, D)
    wbd = jnp.concatenate([_block_diag(gate_x_w[0]), _block_diag(gate_a_w[0])], axis=-1).astype(BF16)
    tables = _retention_tables()

    wp_own = jnp.concatenate([w_proj_a[0], w_proj_b[0], w_out[0]], axis=0).astype(BF16)
    tiny = jnp.concatenate([conv_w[0], jnp.pad(gn_gain[0], ((0, 0), (0, nshard - DK // NDEV)))], axis=0)
    proj, h, wg, wpg, tiny_g = _inproj_gather(x2d, norm_in, w_in[0].astype(BF16), wp_own, tiny,
                                              _gather_order(xi, yi, ci))
    conv_w_full = tiny_g[:, 0:4, :].transpose(1, 0, 2).reshape(4, D)
    gain3 = tiny_g[:, 4:8, :DK // NDEV].transpose(1, 0, 2).reshape(HEADS, 1, DK)

    ya, hs = _lru_fwd(proj, conv_w_full, conv_b, wbd, gate_x_b, gate_a_b, lru_lambda, nb)
    yb, qr, kr, o, rs = _ret_fwd(proj, gain3, tables, nb)
    dx2, dya, dyb, dpc, merged, doa, dob, g_fin, loss_vec = _tail(ya, yb, proj, x2d, tgt2d, wpg, g_final)
    g_pa, g_pb, g_out = _tail_wgrad(ya, yb, merged, doa, dob, dx2)
    loss = lax.psum(jnp.sum(loss_vec), AXES)

    own = [g.reshape(NDEV, nshard, D) for g in (g_pa, g_pb, g_out)]
    dpa, g_wbd, g_vec, *got = _lru_bwd(proj, hs, dya, conv_w_full, conv_b, wbd, gate_x_b, gate_a_b, lru_lambda, nb, own)
    sums = _chip_sum(own, got, core)
    dpb, g_gain, *parts = _ret_bwd(proj, qr, kr, o, rs, dyb, gain3, tables, nb, sums)

    parts_in = _inproj_wgrad_rs(h, dpa, dpb, dpc, _rs_order(2 * xi + yi, ci))
    grad_x, g_norm_in = _inproj_dgrad(dpa, dpb, dpc, wg, x2d, dx2, norm_in)
    grad_x = grad_x.reshape(nb, S, D)
    parts = [parts_in] + list(parts)

    small = jnp.concatenate([
        g_norm_in, g_vec[0:1], g_vec[1:2], g_vec[2:3], g_vec[3:4], g_fin,
        g_vec[4:8],
        g_gain.reshape(1, D),
        jnp.zeros((5, D), F32),
        _block_diag_back(g_wbd[:, :, :CB]).reshape(64, D),
        _block_diag_back(g_wbd[:, :, CB:]).reshape(64, D),
    ], axis=0)

    (small_all,) = _allgather("small_grad_allgather", [small])

    big = [("w_in", w_in, m_w_in, v_w_in), ("w_proj_a", w_proj_a, m_w_proj_a, v_w_proj_a),
           ("w_proj_b", w_proj_b, m_w_proj_b, v_w_proj_b), ("w_out", w_out, m_w_out, v_w_out)]
    res = {}
    for k, (nm, w, m, v) in enumerate(big):
        out = _adamw("adamw_" + nm, parts[k], w[0], m[0], v[0])
        res[nm] = [o[None] for o in out]

    def pack(p):
        return jnp.concatenate([
            p["norm_in"], p["conv_b"], p["gate_x_b"], p["gate_a_b"], p["lru_lambda"], p["norm_final"].reshape(1, D),
            jnp.zeros((10, D), F32), p["gate_x_w"].reshape(64, D), p["gate_a_w"].reshape(64, D)], axis=0)

    names = ["norm_in", "conv_b", "gate_x_b", "gate_a_b", "lru_lambda", "norm_final", "gate_x_w", "gate_a_w"]
    ws = dict(norm_in=norm_in, conv_b=conv_b, gate_x_b=gate_x_b, gate_a_b=gate_a_b, lru_lambda=lru_lambda,
              norm_final=norm_final, gate_x_w=gate_x_w, gate_a_w=gate_a_w)
    ms = dict(norm_in=m_norm_in, conv_b=m_conv_b, gate_x_b=m_gate_x_b, gate_a_b=m_gate_a_b, lru_lambda=m_lru_lambda,
              norm_final=m_norm_final, gate_x_w=m_gate_x_w, gate_a_w=m_gate_a_w)
    vs = dict(norm_in=v_norm_in, conv_b=v_conv_b, gate_x_b=v_gate_x_b, gate_a_b=v_gate_a_b, lru_lambda=v_lru_lambda,
              norm_final=v_norm_final, gate_x_w=v_gate_x_w, gate_a_w=v_gate_a_w)
    packed = _adamw("adamw_small", small_all, pack(ws), pack(ms), pack(vs))
    for nm in names:
        shape = ws[nm].shape
        if nm in ("gate_x_w", "gate_a_w"):
            lo = 16 if nm == "gate_x_w" else 80
            res[nm] = [o[lo:lo + 64].reshape(shape) for o in packed]
        else:
            r = names.index(nm)
            res[nm] = [o[r:r + 1].reshape(shape) for o in packed]

    g_small = packed[0]
    g_conv = lax.dynamic_slice(g_small[6:10], (0, me * nshard), (4, nshard))
    g_gain = lax.dynamic_slice(g_small[10:11].reshape(HEADS, DK), (0, me * (DK // NDEV)), (HEADS, DK // NDEV))
    pad = lambda a: jnp.pad(a, ((0, 0), (0, nshard - DK // NDEV)))
    shard = _adamw("adamw_shard",
                   jnp.concatenate([g_conv, pad(g_gain)], axis=0)[None],
                   jnp.concatenate([conv_w[0], pad(gn_gain[0])], axis=0),
                   jnp.concatenate([m_conv_w[0], pad(m_gn_gain[0])], axis=0),
                   jnp.concatenate([v_conv_w[0], pad(v_gn_gain[0])], axis=0))
    res["conv_w"] = [o[0:4][None] for o in shard]
    res["gn_gain"] = [o[4:8, :DK // NDEV][None] for o in shard]

    order = ["norm_in", "w_in", "conv_w", "conv_b", "gate_x_w", "gate_x_b", "gate_a_w", "gate_a_b", "lru_lambda",
             "gn_gain", "w_proj_a", "w_proj_b", "w_out", "norm_final"]
    outs = [loss, grad_x]
    for k in range(4):
        outs += [res[nm][k] for nm in order]
    return tuple(outs)
```

```python
import jax
import jax.numpy as jnp
from jax import lax
from jax.experimental import pallas as pl
from jax.experimental.pallas import tpu as pltpu

F32 = jnp.float32
BF16 = jnp.bfloat16
MESH = pl.DeviceIdType.MESH

D = 1024
S = 2048
NSEG = 8
NDEV = 8
HEADS = 4
DK = 256
CH = 256
NCH = S // CH
CB = 256
NCB = D // CB
RC = 128
SCAN_GROUP = 8
EPS = 1e-6
LRU_C = 8.0
VMEM_LIMIT = 56 * 1024 * 1024

ADAM_LR = 0.001
ADAM_B1 = 0.9
ADAM_B2 = 0.999
ADAM_EPS = 1e-08
ADAM_WD = 0.01
ADAM_STEP = 10


def _params(sem=None):
    return pltpu.CompilerParams(dimension_semantics=sem, vmem_limit_bytes=VMEM_LIMIT)


def _dot(a, b):
    return jnp.dot(a, b, preferred_element_type=F32)


def _dot_nt(a, b):
    return lax.dot_general(a, b, (((1,), (1,)), ((), ())), preferred_element_type=F32)


def _dot_tn(a, b):
    return lax.dot_general(a, b, (((0,), (0,)), ((), ())), preferred_element_type=F32)


def _sigmoid(x):
    return jax.nn.sigmoid(x)


def _expm1_nonpos(x):
    poly = x * (1.0 + x * (0.5 + x * (1.0 / 6.0 + x * (1.0 / 24.0))))
    return jnp.where(x > -0.05, poly, jnp.exp(x) - 1.0)


def _softplus(x):
    return jnp.maximum(x, 0.0) + jnp.log(1.0 + jnp.exp(-jnp.abs(x)))


def _rows(c, n):
    return pl.ds(pl.multiple_of(c * n, n), n)


def _window_before(ref, c, n):
    r0 = c * n
    prev = ref[pl.ds(pl.multiple_of(jnp.maximum(r0 - 8, 0), 8), 8), :]
    prev = jnp.where(c > 0, prev, 0.0)
    return jnp.concatenate([prev, ref[_rows(c, n), :]], axis=0)


def _shift_down(win, s, n):
    if s == 0:
        return win[8:, :]
    return pltpu.roll(win, s, 0)[8:, :]


def _shift_up(win, s, n):
    if s == 0:
        return win[:n, :]
    return pltpu.roll(win, n + 8 - s, 0)[:n, :]


def _gather_order(x, y, c):
    chips = [(1 - x, y), (x, 1 - y), (1 - x, 1 - y)]
    order = [4 * x + 2 * y + c, 4 * x + 2 * y + 1 - c]
    for px, py in chips:
        order += [4 * px + 2 * py + c, 4 * px + 2 * py + 1 - c]
    return jnp.stack(order).astype(jnp.int32)


def _inproj_gather(x2d, g_in, w_own, wp_own, tiny_own, order):
    t = x2d.shape[0]
    tm = 1024
    nt = t // tm

    def body(order_ref, x_ref, g_ref, w_own_ref, wp_own_ref, tiny_own_ref,
             proj_ref, h_ref, wg_ref, wpg_ref, tinyg_ref,
             w_all, h_all, send_sems, recv_sems, own_sems, out_sems):
        k, i = pl.program_id(0), pl.program_id(1)
        x, y, c = _place()
        me, sibling = (x, y, c), (x, y, 1 - c)
        chips = [(1 - x, y), (x, 1 - y), (1 - x, 1 - y)]
        srcs = [w_own_ref, wp_own_ref, tiny_own_ref]
        dsts = [w_all, wpg_ref, tinyg_ref]

        def copy(a, n, block, to, own_src=False):
            px, py, pc = block
            dst = dsts[a].at[4 * px + 2 * py + pc]
            return pltpu.make_async_remote_copy(
                src_ref=srcs[a] if own_src else dst, dst_ref=dst,
                send_sem=send_sems.at[a, n], recv_sem=recv_sems.at[a, n], device_id=to, device_id_type=MESH)

        def own_copy(a):
            return pltpu.make_async_copy(srcs[a], dsts[a].at[4 * x + 2 * y + c], own_sems.at[a])

        def keep_copy(n):
            return pltpu.make_async_copy(w_all.at[order_ref[n]], wg_ref.at[order_ref[n]], out_sems.at[n])

        def first_copies(a):
            return [copy(a, 0, me, sibling, True)] + [copy(a, 1 + j, me, (*chip, c), True) for j, chip in enumerate(chips)]

        def at_slot(n):
            return jnp.logical_and(k == n, i == 0)

        @pl.when(at_slot(0))
        def _():
            for a in range(3):
                own_copy(a).start()
            for a in range(3):
                for cp in first_copies(a):
                    cp.start()
            own_copy(0).wait()
            keep_copy(0).start()

        @pl.when(at_slot(1))
        def _():
            copy(0, 0, sibling, me).wait_recv()
            keep_copy(1).start()

        for j, chip in enumerate(chips):
            @pl.when(at_slot(2 + 2 * j))
            def _():
                copy(0, 1 + j, (*chip, c), me).wait_recv()
                copy(0, 4 + j, (*chip, c), sibling).start()
                keep_copy(2 + 2 * j).start()

            @pl.when(at_slot(3 + 2 * j))
            def _():
                copy(0, 4 + j, (*chip, 1 - c), me).wait_recv()
                keep_copy(3 + 2 * j).start()

        rows = pl.ds(pl.multiple_of(i * tm, tm), tm)

        @pl.when(k == 0)
        def _():
            xv = x_ref[...]
            r = lax.rsqrt(jnp.mean(xv * xv, axis=-1, keepdims=True) + EPS)
            hv = (xv * r * g_ref[...]).astype(BF16)
            h_ref[...] = hv
            h_all[rows, :] = hv

        proj_ref[...] = _dot(h_all[rows, :], w_all[order_ref[k]])

        @pl.when(jnp.logical_and(k == NSEG - 1, i == nt - 1))
        def _():
            for a in (1, 2):
                for j, chip in enumerate(chips):
                    copy(a, 1 + j, (*chip, c), me).wait_recv()
                    copy(a, 4 + j, (*chip, c), sibling).start()
            for a in (1, 2):
                copy(a, 0, sibling, me).wait_recv()
                for j, chip in enumerate(chips):
                    copy(a, 4 + j, (*chip, 1 - c), me).wait_recv()
            for a in range(3):
                for cp in first_copies(a):
                    cp.wait_send()
                for j, chip in enumerate(chips):
                    copy(a, 4 + j, (*chip, c), sibling).wait_send()
            for a in (1, 2):
                own_copy(a).wait()
            for n in range(NSEG):
                keep_copy(n).wait()

    hold = lambda k, i, order_ref: (jnp.where(k == 0, i, nt - 1), 0)
    return pl.pallas_call(
        body, name="inproj_gather",
        grid_spec=pltpu.PrefetchScalarGridSpec(
            num_scalar_prefetch=1, grid=(NSEG, nt),
            in_specs=[pl.BlockSpec((tm, D), hold),
                      pl.BlockSpec((1, D), lambda k, i, order_ref: (0, 0)),
                      ANY, ANY, ANY],
            out_specs=[pl.BlockSpec((None, tm, D), lambda k, i, order_ref: (order_ref[k], i, 0)),
                       pl.BlockSpec((tm, D), hold),
                       ANY, ANY, ANY],
            scratch_shapes=[pltpu.VMEM((NDEV, D, D), BF16), pltpu.VMEM((t, D), BF16),
                            pltpu.SemaphoreType.DMA((3, 7)), pltpu.SemaphoreType.DMA((3, 7)),
                            pltpu.SemaphoreType.DMA((3,)), pltpu.SemaphoreType.DMA((NSEG,))]),
        out_shape=[jax.ShapeDtypeStruct((NSEG, t, D), F32), jax.ShapeDtypeStruct((t, D), BF16),
                   jax.ShapeDtypeStruct((NDEV,) + w_own.shape, BF16),
                   jax.ShapeDtypeStruct((NDEV,) + wp_own.shape, BF16),
                   jax.ShapeDtypeStruct((NDEV,) + tiny_own.shape, F32)],
        compiler_params=_params(("arbitrary", "arbitrary")),
    )(order, x2d, g_in, w_own, wp_own, tiny_own)


def _tile_scan(a, u):
    row = lax.broadcasted_iota(jnp.int32, a.shape, 0)
    for d in (1, 2, 4):
        m = row >= d
        a_sh = pltpu.roll(a, d, 0)
        u_sh = pltpu.roll(u, d, 0)
        u = jnp.where(m, a * u_sh + u, u)
        a = jnp.where(m, a * a_sh, a)
    return a, u


def _tile_scan_rev(a, w):
    row = lax.broadcasted_iota(jnp.int32, a.shape, 0)
    for d in (1, 2, 4):
        m = row < 8 - d
        a_sh = pltpu.roll(a, 8 - d, 0)
        w_sh = pltpu.roll(w, 8 - d, 0)
        w = jnp.where(m, a * w_sh + w, w)
        a = jnp.where(m, a * a_sh, a)
    return a, w


def _lru_gates(xa_ref, c, cw_ref, cb_ref, wbd_ref, bx_ref, ba_ref, sp):
    win = _window_before(xa_ref, c, RC)
    xc = cb_ref[...] + cw_ref[3:4, :] * _shift_down(win, 0, RC)
    for s in (1, 2, 3):
        xc = xc + cw_ref[3 - s:4 - s, :] * _shift_down(win, s, RC)
    z = _dot(xc.astype(BF16), wbd_ref[...])
    gi = _sigmoid(z[:, :CB] + bx_ref[...])
    gr = _sigmoid(z[:, CB:] + ba_ref[...])
    log_a = -LRU_C * gr * sp
    return win, xc, gi, gr, log_a


def _lru_fwd(proj, conv_w, conv_b, wbd, bx, ba, lam, nb):
    t = nb * S

    def body(xa_ref, ga_ref, cw_ref, cb_ref, wbd_ref, bx_ref, ba_ref, lam_ref, ya_ref, hs_ref, a_s, u_s):
        sp = _softplus(-lam_ref[...])

        def gates(c, carry):
            _, xc, gi, _, log_a = _lru_gates(xa_ref, c, cw_ref, cb_ref, wbd_ref, bx_ref, ba_ref, sp)
            a_s[_rows(c, RC), :] = jnp.exp(log_a)
            u_s[_rows(c, RC), :] = jnp.sqrt(-_expm1_nonpos(2.0 * log_a)) * (gi * xc)
            return carry

        lax.fori_loop(0, S // RC, gates, 0)

        def scan(g, h):
            for k in range(SCAN_GROUP):
                rows = pl.ds(pl.multiple_of(g * (8 * SCAN_GROUP), 8 * SCAN_GROUP) + 8 * k, 8)
                a_cum, u_cum = _tile_scan(a_s[rows, :], u_s[rows, :])
                hs_ref[rows, :] = u_cum + a_cum * h
                h = u_cum[7:8, :] + a_cum[7:8, :] * h
            return h

        lax.fori_loop(0, S // (8 * SCAN_GROUP), scan, jnp.zeros((1, CB), F32))

        def gate_out(c, carry):
            ga = ga_ref[_rows(c, RC), :]
            ya_ref[_rows(c, RC), :] = (ga * _sigmoid(ga) * hs_ref[_rows(c, RC), :]).astype(BF16)
            return carry

        lax.fori_loop(0, S // RC, gate_out, 0)

    vec = pl.BlockSpec((1, CB), lambda b, cb: (0, cb))
    return pl.pallas_call(
        body, name="lru_fwd", grid=(nb, NCB),
        in_specs=[pl.BlockSpec((None, S, CB), lambda b, cb: (0, b, cb)),
                  pl.BlockSpec((None, S, CB), lambda b, cb: (1, b, cb)),
                  pl.BlockSpec((4, CB), lambda b, cb: (0, cb)),
                  vec,
                  pl.BlockSpec((None, CB, 2 * CB), lambda b, cb: (cb, 0, 0)),
                  vec, vec, vec],
        out_specs=[pl.BlockSpec((S, CB), lambda b, cb: (b, cb)),
                   pl.BlockSpec((S, CB), lambda b, cb: (b, cb))],
        out_shape=[jax.ShapeDtypeStruct((t, D), BF16), jax.ShapeDtypeStruct((t, D), F32)],
        scratch_shapes=[pltpu.VMEM((S, CB), F32), pltpu.VMEM((S, CB), F32)],
        compiler_params=_params(("arbitrary", "arbitrary")),
    )(proj, proj, conv_w, conv_b, wbd, bx, ba, lam)


def _lru_bwd(proj, hs, dya, conv_w, conv_b, wbd, bx, ba, lam, nb, give):
    t = nb * S
    ng = len(give)

    def body(xa_ref, ga_ref, hs_ref, dya_ref, cw_ref, cb_ref, wbd_ref, bx_ref, ba_ref, lam_ref, *rest):
        give_refs, rest = rest[:ng], rest[ng:]
        dp_ref, dwbd_ref, vec_ref = rest[:3]
        got_refs, rest = rest[3:3 + ng], rest[3 + ng:]
        a_s, xc_s, gi_s, gr_s, dl_s, dh_s, dxc_s, acc_s, send_sems, recv_sems = rest
        b = pl.program_id(1)
        exchange = _sibling_copies(give_refs, got_refs, send_sems, recv_sems)

        @pl.when(jnp.logical_and(pl.program_id(0) == 0, b == 0))
        def _():
            for cp in exchange:
                cp.start()

        lam_v = lam_ref[...]
        sp = _softplus(-lam_v)
        acc_s[...] = jnp.zeros_like(acc_s)

        @pl.when(b == 0)
        def _():
            dwbd_ref[...] = jnp.zeros_like(dwbd_ref)
            vec_ref[...] = jnp.zeros_like(vec_ref)

        def gates(c, carry):
            _, xc, gi, gr, log_a = _lru_gates(xa_ref, c, cw_ref, cb_ref, wbd_ref, bx_ref, ba_ref, sp)
            rows = _rows(c, RC)
            a_s[rows, :] = jnp.exp(log_a)
            xc_s[rows, :] = xc
            gi_s[rows, :] = gi
            gr_s[rows, :] = gr
            ga = ga_ref[rows, :]
            sg = _sigmoid(ga)
            dya_c = dya_ref[rows, :]
            dl_s[rows, :] = dya_c * (ga * sg)
            dp_ref[1, rows, :] = (dya_c * hs_ref[rows, :] * (sg * (1.0 + ga * (1.0 - sg)))).astype(BF16)
            return carry

        lax.fori_loop(0, S // RC, gates, 0)

        def scan(i, g_in):
            base = pl.multiple_of((S // (8 * SCAN_GROUP) - 1 - i) * (8 * SCAN_GROUP), 8 * SCAN_GROUP)
            row = lax.broadcasted_iota(jnp.int32, (8, CB), 0)
            for k in reversed(range(SCAN_GROUP)):
                rows = pl.ds(base + 8 * k, 8)
                a = a_s[rows, :]
                dl = dl_s[rows, :]
                a_cum, g_loc = _tile_scan_rev(a, a * dl)
                g = g_loc + a_cum * g_in
                dh_s[rows, :] = dl + jnp.where(row < 7, pltpu.roll(g, 7, 0), g_in)
                g_in = g_loc[0:1, :] + a_cum[0:1, :] * g_in
            return g_in

        lax.fori_loop(0, S // (8 * SCAN_GROUP), scan, jnp.zeros((1, CB), F32))

        dxc_s[pl.ds(S, 8), :] = jnp.zeros((8, CB), F32)

        def grads(c, carry):
            rows = _rows(c, RC)
            dh = dh_s[rows, :]
            h_prev = _shift_down(_window_before(hs_ref, c, RC), 1, RC)
            xc, gi, gr, a = xc_s[rows, :], gi_s[rows, :], gr_s[rows, :], a_s[rows, :]
            mult = jnp.sqrt(-_expm1_nonpos(-2.0 * LRU_C * gr * sp))
            dmult = dh * gi * xc
            d_log_a = dh * h_prev * a - dmult * (a * a) / mult
            dzi = dh * mult * xc * gi * (1.0 - gi)
            dzr = d_log_a * (-LRU_C * sp) * gr * (1.0 - gr)
            dz = jnp.concatenate([dzi, dzr], axis=1).astype(BF16)
            dxc_s[rows, :] = dh * mult * gi + _dot_nt(dz, wbd_ref[...])
            dwbd_ref[...] += _dot_tn(xc.astype(BF16), dz)
            acc_s[1:2, :] += jnp.sum(dzi, axis=0, keepdims=True)
            acc_s[2:3, :] += jnp.sum(dzr, axis=0, keepdims=True)
            acc_s[3:4, :] += jnp.sum(d_log_a * (-LRU_C * gr), axis=0, keepdims=True)
            return carry

        lax.fori_loop(0, S // RC, grads, 0)

        def conv_bwd(c, carry):
            rows = _rows(c, RC)
            dwin = dxc_s[pl.ds(pl.multiple_of(c * RC, RC), RC + 8), :]
            dxc = dwin[:RC, :]
            xwin = _window_before(xa_ref, c, RC)
            dxa = cw_ref[3:4, :] * dxc
            acc_s[0:1, :] += jnp.sum(dxc, axis=0, keepdims=True)
            acc_s[7:8, :] += jnp.sum(dxc * _shift_down(xwin, 0, RC), axis=0, keepdims=True)
            for s in (1, 2, 3):
                dxa = dxa + cw_ref[3 - s:4 - s, :] * _shift_up(dwin, s, RC)
                acc_s[7 - s:8 - s, :] += jnp.sum(dxc * _shift_down(xwin, s, RC), axis=0, keepdims=True)
            dp_ref[0, rows, :] = dxa.astype(BF16)
            return carry

        lax.fori_loop(0, S // RC, conv_bwd, 0)

        row = lax.broadcasted_iota(jnp.int32, acc_s.shape, 0)
        vec_ref[...] += jnp.where(row == 3, acc_s[...] * (-_sigmoid(-lam_v)), acc_s[...])

        @pl.when(jnp.logical_and(pl.program_id(0) == NCB - 1, b == nb - 1))
        def _():
            for cp in exchange:
                cp.wait()

    vec = pl.BlockSpec((1, CB), lambda cb, b: (0, cb))
    blk = pl.BlockSpec((S, CB), lambda cb, b: (b, cb))
    return pl.pallas_call(
        body, name="lru_bwd", grid=(NCB, nb),
        in_specs=[pl.BlockSpec((None, S, CB), lambda cb, b: (0, b, cb)),
                  pl.BlockSpec((None, S, CB), lambda cb, b: (1, b, cb)),
                  blk, blk,
                  pl.BlockSpec((4, CB), lambda cb, b: (0, cb)),
                  vec,
                  pl.BlockSpec((None, CB, 2 * CB), lambda cb, b: (cb, 0, 0)),
                  vec, vec, vec] + [ANY] * ng,
        out_specs=[pl.BlockSpec((2, S, CB), lambda cb, b: (0, b, cb)),
                   pl.BlockSpec((None, CB, 2 * CB), lambda cb, b: (cb, 0, 0)),
                   pl.BlockSpec((8, CB), lambda cb, b: (0, cb))] + [ANY] * ng,
        out_shape=[jax.ShapeDtypeStruct((2, t, D), BF16),
                   jax.ShapeDtypeStruct((NCB, CB, 2 * CB), F32),
                   jax.ShapeDtypeStruct((8, D), F32)]
        + [jax.ShapeDtypeStruct((4,) + g.shape[1:], g.dtype) for g in give],
        scratch_shapes=[pltpu.VMEM((S, CB), F32), pltpu.VMEM((S, CB), F32), pltpu.VMEM((S, CB), F32),
                        pltpu.VMEM((S, CB), F32), pltpu.VMEM((S, CB), F32), pltpu.VMEM((S, CB), F32),
                        pltpu.VMEM((S + 8, CB), F32), pltpu.VMEM((8, CB), F32),
                        pltpu.SemaphoreType.DMA((ng, 4)), pltpu.SemaphoreType.DMA((ng, 4))],
        compiler_params=_params(("arbitrary", "arbitrary")),
    )(proj, proj, hs, dya, conv_w, conv_b, wbd, bx, ba, lam, *give)


def _retention_tables():
    log_g = jnp.log1p(-(2.0 ** (-5.0 - jnp.arange(HEADS, dtype=F32))))
    idx = jnp.arange(CH, dtype=F32)
    diff = idx[:, None] - idx[None, :]
    inner = jnp.where(diff >= 0, jnp.exp(jnp.maximum(diff, 0.0)[None] * log_g[:, None, None]), 0.0)
    cross = jnp.exp((idx[None, :] + 1.0) * log_g[:, None])
    state = jnp.exp((CH - 1.0 - idx[None, :]) * log_g[:, None])
    cross = jnp.broadcast_to(cross[:, :, None], (HEADS, CH, DK))
    state = jnp.broadcast_to(state[:, :, None], (HEADS, CH, DK))
    half = DK // 2
    freqs = 10000.0 ** (-jnp.arange(half, dtype=F32) / half)
    ang = jnp.arange(S, dtype=F32)[:, None] * freqs[None, :]
    return inner, cross, state, jnp.cos(ang), jnp.sin(ang)


def _rotate(x, cos, sin):
    half = DK // 2
    x1, x2 = x[:, :half], x[:, half:]
    return jnp.concatenate([x1 * cos - x2 * sin, x1 * sin + x2 * cos], axis=1)


def _rotate_back(d, cos, sin):
    half = DK // 2
    d1, d2 = d[:, :half], d[:, half:]
    return jnp.concatenate([d1 * cos + d2 * sin, d2 * cos - d1 * sin], axis=1)


def _ret_fwd(proj, gain, tables, nb):
    t = nb * S
    inner_t, cross_t, state_t, cos_t, sin_t = tables

    def body(q_ref, k_ref, v_ref, gb_ref, gain_ref, dm_ref, cd_ref, sd_ref, cos_ref, sin_ref,
             yb_ref, qr_ref, kr_ref, o_ref, rs_ref, r_s):
        r_s[...] = jnp.zeros_like(r_s)
        chunk_decay = cd_ref[CH - 1:CH, :]

        def chunk(c, carry):
            rows = _rows(c, CH)
            cos, sin = cos_ref[rows, :], sin_ref[rows, :]
            qr = _rotate(q_ref[rows, :], cos, sin).astype(BF16)
            kr = (_rotate(k_ref[rows, :], cos, sin) * (DK ** -0.5)).astype(BF16)
            v = v_ref[rows, :]
            qr_ref[rows, :] = qr
            kr_ref[rows, :] = kr
            r = r_s[...]
            rb = r.astype(BF16)
            rs_ref[c] = rb
            p = (_dot_nt(qr, kr) * dm_ref[...]).astype(BF16)
            o = _dot(p, v.astype(BF16)) + _dot(qr, rb) * cd_ref[...]
            r_s[...] = chunk_decay * r + _dot_tn(kr, (v * sd_ref[...]).astype(BF16))
            o_ref[rows, :] = o
            oc = o - jnp.mean(o, axis=-1, keepdims=True)
            rstd = lax.rsqrt(jnp.mean(oc * oc, axis=-1, keepdims=True) + EPS)
            gb = gb_ref[rows, :]
            yb_ref[rows, :] = (gb * _sigmoid(gb) * (oc * rstd * gain_ref[...])).astype(BF16)
            return carry

        lax.fori_loop(0, NCH, chunk, 0)

    seg = lambda s: pl.BlockSpec((None, S, DK), lambda b, h: (s, b, h))
    tab = pl.BlockSpec((None, CH, DK), lambda b, h: (h, 0, 0))
    rot = pl.BlockSpec((S, DK // 2), lambda b, h: (0, 0))
    blk = pl.BlockSpec((S, DK), lambda b, h: (b, h))
    return pl.pallas_call(
        body, name="ret_fwd", grid=(nb, HEADS),
        in_specs=[seg(2), seg(3), seg(4), seg(5),
                  pl.BlockSpec((None, 1, DK), lambda b, h: (h, 0, 0)),
                  tab, tab, tab, rot, rot],
        out_specs=[blk, blk, blk, blk,
                   pl.BlockSpec((None, None, NCH, DK, DK), lambda b, h: (b, h, 0, 0, 0))],
        out_shape=[jax.ShapeDtypeStruct((t, D), BF16), jax.ShapeDtypeStruct((t, D), BF16),
                   jax.ShapeDtypeStruct((t, D), BF16), jax.ShapeDtypeStruct((t, D), F32),
                   jax.ShapeDtypeStruct((nb, HEADS, NCH, DK, DK), BF16)],
        scratch_shapes=[pltpu.VMEM((DK, DK), F32)],
        compiler_params=_params(("arbitrary", "arbitrary")),
    )(proj, proj, proj, proj, gain, inner_t, cross_t, state_t, cos_t, sin_t)


def _ret_bwd(proj, qr, kr, o, rs, dyb, gain, tables, nb, sums):
    t = nb * S
    ns = len(sums)
    inner_t, cross_t, state_t, cos_t, sin_t = tables

    def body(qr_ref, kr_ref, v_ref, gb_ref, o_ref, dyb_ref, rs_ref, gain_ref, dm_ref, cd_ref, sd_ref,
             cos_ref, sin_ref, *rest):
        sum_refs, rest = rest[:ns], rest[ns:]
        dp_ref, dgain_ref = rest[:2]
        part_refs, rest = rest[2:2 + ns], rest[2 + ns:]
        dr_s, send_sems, recv_sems, local_sems = rest
        mine, sends, recvs = _chip_copies(sum_refs, part_refs, send_sems, recv_sems, local_sems)

        @pl.when(jnp.logical_and(pl.program_id(0) == 0, pl.program_id(1) == 0))
        def _():
            for cp in mine + sends:
                cp.start()

        dr_s[...] = jnp.zeros_like(dr_s)
        chunk_decay = cd_ref[CH - 1:CH, :]

        @pl.when(pl.program_id(1) == 0)
        def _():
            dgain_ref[...] = jnp.zeros_like(dgain_ref)

        def chunk(i, carry):
            c = NCH - 1 - i
            rows = _rows(c, CH)
            gain_v = gain_ref[...]
            o_c = o_ref[rows, :]
            oc = o_c - jnp.mean(o_c, axis=-1, keepdims=True)
            rstd = lax.rsqrt(jnp.mean(oc * oc, axis=-1, keepdims=True) + EPS)
            yn = oc * rstd
            gb = gb_ref[rows, :]
            sg = _sigmoid(gb)
            dyb_c = dyb_ref[rows, :]
            dgn = dyb_c * (gb * sg)
            dp_ref[3, rows, :] = (dyb_c * (yn * gain_v) * (sg * (1.0 + gb * (1.0 - sg)))).astype(BF16)
            dgain_ref[...] += jnp.sum(dgn * yn, axis=0, keepdims=True)
            dyn = dgn * gain_v
            do = rstd * (dyn - jnp.mean(dyn, axis=-1, keepdims=True)
                         - yn * jnp.mean(dyn * yn, axis=-1, keepdims=True))
            dob = do.astype(BF16)
            dox = (do * cd_ref[...]).astype(BF16)

            q_c, k_c = qr_ref[rows, :], kr_ref[rows, :]
            v = v_ref[rows, :]
            vb = v.astype(BF16)
            vs = (v * sd_ref[...]).astype(BF16)
            rb = rs_ref[c]
            d_r = dr_s[...]
            drb = d_r.astype(BF16)
            dm = dm_ref[...]
            p = (_dot_nt(q_c, k_c) * dm).astype(BF16)
            dpm = (_dot_nt(dob, vb) * dm).astype(BF16)
            dq = _dot(dpm, k_c) + _dot_nt(dox, rb)
            dk = _dot_tn(dpm, q_c) + _dot_nt(vs, drb)
            dv = _dot_tn(p, dob) + _dot(k_c, drb) * sd_ref[...]
            dr_s[...] = chunk_decay * d_r + _dot_tn(q_c, dox)

            cos, sin = cos_ref[rows, :], sin_ref[rows, :]
            dp_ref[0, rows, :] = _rotate_back(dq, cos, sin).astype(BF16)
            dp_ref[1, rows, :] = (_rotate_back(dk, cos, sin) * (DK ** -0.5)).astype(BF16)
            dp_ref[2, rows, :] = dv.astype(BF16)
            return carry

        lax.fori_loop(0, NCH, chunk, 0)

        @pl.when(jnp.logical_and(pl.program_id(0) == HEADS - 1, pl.program_id(1) == nb - 1))
        def _():
            for cp in recvs:
                cp.wait_recv()
            for cp in sends:
                cp.wait_send()
            for cp in mine:
                cp.wait()

    seg = lambda s: pl.BlockSpec((None, S, DK), lambda h, b: (s, b, h))
    tab = pl.BlockSpec((None, CH, DK), lambda h, b: (h, 0, 0))
    rot = pl.BlockSpec((S, DK // 2), lambda h, b: (0, 0))
    blk = pl.BlockSpec((S, DK), lambda h, b: (b, h))
    one = pl.BlockSpec((None, 1, DK), lambda h, b: (h, 0, 0))
    return pl.pallas_call(
        body, name="ret_bwd", grid=(HEADS, nb),
        in_specs=[blk, blk, seg(4), seg(5), blk, blk,
                  pl.BlockSpec((None, None, NCH, DK, DK), lambda h, b: (b, h, 0, 0, 0)),
                  one, tab, tab, tab, rot, rot] + [ANY] * ns,
        out_specs=[pl.BlockSpec((4, S, DK), lambda h, b: (0, b, h)), one] + [ANY] * ns,
        out_shape=[jax.ShapeDtypeStruct((4, t, D), BF16), jax.ShapeDtypeStruct((HEADS, 1, DK), F32)]
        + [jax.ShapeDtypeStruct(a.shape, a.dtype) for a in sums],
        scratch_shapes=[pltpu.VMEM((DK, DK), F32), pltpu.SemaphoreType.DMA((ns, 3)), pltpu.SemaphoreType.DMA((ns, 3)),
                        pltpu.SemaphoreType.DMA((ns,))],
        compiler_params=_params(("arbitrary", "arbitrary")),
    )(qr, kr, proj, proj, o, dyb, rs, gain, inner_t, cross_t, state_t, cos_t, sin_t, *sums)


def _wblock(k):
    return pl.BlockSpec((NDEV, D // NDEV, D), lambda i: (0, k, 0))


def _tail(ya, yb, proj, x2d, tgt, wg, g_fin):
    t = x2d.shape[0]
    tm = 256

    def body(ya_ref, yb_ref, ma_ref, mb_ref, x_ref, t_ref, wa_ref, wb_ref, wo_ref, g_ref,
             dx2_ref, dya_ref, dyb_ref, dm_ref, mg_ref, doa_ref, dob_ref, gfin_ref, loss_ref):
        i = pl.program_id(0)

        @pl.when(i == 0)
        def _():
            gfin_ref[...] = jnp.zeros_like(gfin_ref)
            loss_ref[...] = jnp.zeros_like(loss_ref)

        wa = wa_ref[...].reshape(D, D)
        wb = wb_ref[...].reshape(D, D)
        wo = wo_ref[...].reshape(D, D)
        out_a = _dot(ya_ref[...], wa)
        out_b = _dot(yb_ref[...], wb)
        sa = _sigmoid(ma_ref[...])
        sb = _sigmoid(mb_ref[...])
        merged = (sa * out_a + sb * out_b).astype(BF16)
        mg_ref[...] = merged
        x2 = x_ref[...] + _dot(merged, wo)
        r2 = lax.rsqrt(jnp.mean(x2 * x2, axis=-1, keepdims=True) + EPS)
        xh = x2 * r2
        g = g_ref[...]
        err = xh * g - t_ref[...]
        loss_ref[...] += jnp.sum(err * err, axis=0, keepdims=True) * (0.5 / D)
        dy = err * (1.0 / D)
        gfin_ref[...] += jnp.sum(dy * xh, axis=0, keepdims=True)
        dxh = dy * g
        dx2 = r2 * (dxh - xh * jnp.mean(dxh * xh, axis=-1, keepdims=True))
        dx2_ref[...] = dx2
        dmerged = _dot_nt(dx2.astype(BF16), wo)
        doa = (sa * dmerged).astype(BF16)
        dob = (sb * dmerged).astype(BF16)
        doa_ref[...] = doa
        dob_ref[...] = dob
        dm_ref[0] = (dmerged * out_a * sa * (1.0 - sa)).astype(BF16)
        dm_ref[1] = (dmerged * out_b * sb * (1.0 - sb)).astype(BF16)
        dya_ref[...] = _dot_nt(doa, wa)
        dyb_ref[...] = _dot_nt(dob, wb)

    row = lambda: pl.BlockSpec((tm, D), lambda i: (i, 0))
    seg = lambda s: pl.BlockSpec((None, tm, D), lambda i: (s, i, 0))
    vec = pl.BlockSpec((1, D), lambda i: (0, 0))
    return pl.pallas_call(
        body, name="tail", grid=(t // tm,),
        in_specs=[row(), row(), seg(6), seg(7), row(), row(), _wblock(0), _wblock(1), _wblock(2), vec],
        out_specs=[row(), row(), row(), pl.BlockSpec((2, tm, D), lambda i: (0, i, 0)),
                   row(), row(), row(), vec, vec],
        out_shape=[jax.ShapeDtypeStruct((t, D), F32), jax.ShapeDtypeStruct((t, D), F32),
                   jax.ShapeDtypeStruct((t, D), F32), jax.ShapeDtypeStruct((2, t, D), BF16),
                   jax.ShapeDtypeStruct((t, D), BF16), jax.ShapeDtypeStruct((t, D), BF16),
                   jax.ShapeDtypeStruct((t, D), BF16), jax.ShapeDtypeStruct((1, D), F32),
                   jax.ShapeDtypeStruct((1, D), F32)],
        compiler_params=_params(("arbitrary",)),
    )(ya, yb, proj, proj, x2d, tgt, wg, wg, wg, g_fin)


def _tail_wgrad(ya, yb, merged, doa, dob, dx2):
    t = ya.shape[0]
    tm = 512

    def body(ya_ref, yb_ref, mg_ref, doa_ref, dob_ref, dx2_ref, ga_ref, gb_ref, go_ref):
        @pl.when(pl.program_id(0) == 0)
        def _():
            ga_ref[...] = jnp.zeros_like(ga_ref)
            gb_ref[...] = jnp.zeros_like(gb_ref)
            go_ref[...] = jnp.zeros_like(go_ref)

        ga_ref[...] += _dot_tn(ya_ref[...], doa_ref[...])
        gb_ref[...] += _dot_tn(yb_ref[...], dob_ref[...])
        go_ref[...] += _dot_tn(mg_ref[...], dx2_ref[...].astype(BF16))

    row = lambda: pl.BlockSpec((tm, D), lambda i: (i, 0))
    full = lambda: pl.BlockSpec((D, D), lambda i: (0, 0))
    return pl.pallas_call(
        body, name="tail_wgrad", grid=(t // tm,),
        in_specs=[row() for _ in range(6)], out_specs=[full(), full(), full()],
        out_shape=[jax.ShapeDtypeStruct((D, D), F32)] * 3,
        compiler_params=_params(("arbitrary",)),
    )(ya, yb, merged, doa, dob, dx2)


def _dproj_specs(tm, j_of, i_of):
    last = lambda j, i, lo, n: (jnp.clip(j - lo, 0, n - 1), i, 0)
    return [pl.BlockSpec((None, tm, D), lambda a, b: last(j_of(a, b), i_of(a, b), 0, 2)),
            pl.BlockSpec((None, tm, D), lambda a, b: last(j_of(a, b), i_of(a, b), 2, 4)),
            pl.BlockSpec((None, tm, D), lambda a, b: last(j_of(a, b), i_of(a, b), 6, 2))]


def _dproj_specs_ordered(tm):
    def spec(lo, n):
        def index(k, i, order_ref):
            seg = order_ref[k]
            mine = jnp.logical_and(seg >= lo, seg < lo + n)
            return jnp.where(mine, seg - lo, 0), jnp.where(mine, i, 0), 0
        return pl.BlockSpec((None, tm, D), index)
    return [spec(0, 2), spec(2, 4), spec(6, 2)]


def _dproj_pick(j, da_ref, db_ref, dc_ref, use):
    @pl.when(j < 2)
    def _():
        use(da_ref[...])

    @pl.when(jnp.logical_and(j >= 2, j < 6))
    def _():
        use(db_ref[...])

    @pl.when(j >= 6)
    def _():
        use(dc_ref[...])


def _rs_schedule(q, c):
    steps = []
    for s in range(3):
        d_a = lax.rem(q + 1 + s, 4)
        d_b = lax.rem(q + 1 + (s + 1) % 3, 4)
        steps.append((jnp.where(c == 0, d_a, d_b), jnp.where(c == 0, d_b, d_a)))
    steps.append((q, q))
    return steps


def _rs_order(q, c):
    order = []
    for keep, give in _rs_schedule(q, c):
        order += [2 * give + 1 - c, 2 * keep + c]
    return jnp.stack(order).astype(jnp.int32)


def _inproj_wgrad_rs(h, dpa, dpb, dpc, order):
    t = h.shape[0]
    tm = 1024
    nt = t // tm

    def body(order_ref, h_ref, da_ref, db_ref, dc_ref, parts_ref, acc, sib, outb,
             give_send, give_recv, sum_send, sum_recv, own_sem):
        k, i = pl.program_id(0), pl.program_id(1)
        x, y, c = _place()
        schedule = _rs_schedule(2 * x + y, c)

        def use(d):
            @pl.when(i == 0)
            def _():
                acc[k % 2] = _dot_tn(h_ref[...], d)

            @pl.when(i > 0)
            def _():
                acc[k % 2] += _dot_tn(h_ref[...], d)

        _dproj_pick(order_ref[k], da_ref, db_ref, dc_ref, use)

        def give_copy(s):
            return pltpu.make_async_remote_copy(
                src_ref=acc.at[0], dst_ref=sib.at[s % 2], send_sem=give_send.at[s], recv_sem=give_recv.at[s],
                device_id=(x, y, 1 - c), device_id_type=MESH)

        def sum_copy(s):
            keep = schedule[s][0]
            return pltpu.make_async_remote_copy(
                src_ref=outb.at[s], dst_ref=parts_ref.at[s], send_sem=sum_send.at[s], recv_sem=sum_recv.at[s],
                device_id=(keep // 2, lax.rem(keep, 2), c), device_id_type=MESH)

        own_copy = pltpu.make_async_copy(outb.at[3], parts_ref.at[3], own_sem)

        for s in range(4):
            @pl.when(jnp.logical_and(k == 2 * s, i == nt - 1))
            def _():
                give_copy(s).start()

            @pl.when(jnp.logical_and(k == 2 * s + 1, i == nt - 1))
            def _():
                give_copy(s).wait_recv()
                outb[s] = (acc[1] + sib[s % 2]).astype(BF16)
                give_copy(s).wait_send()
                if s < 3:
                    sum_copy(s).start()
                else:
                    own_copy.start()

        @pl.when(jnp.logical_and(k == NSEG - 1, i == nt - 1))
        def _():
            for s in range(3):
                sum_copy(s).wait_recv()
            for s in range(3):
                sum_copy(s).wait_send()
            own_copy.wait()

    return pl.pallas_call(
        body, name="inproj_wgrad_rs",
        grid_spec=pltpu.PrefetchScalarGridSpec(
            num_scalar_prefetch=1, grid=(NSEG, nt),
            in_specs=[pl.BlockSpec((tm, D), lambda k, i, order_ref: (i, 0))] + _dproj_specs_ordered(tm),
            out_specs=ANY,
            scratch_shapes=[pltpu.VMEM((2, D, D), F32), pltpu.VMEM((2, D, D), F32), pltpu.VMEM((4, D, D), BF16),
                            pltpu.SemaphoreType.DMA((4,)), pltpu.SemaphoreType.DMA((4,)),
                            pltpu.SemaphoreType.DMA((3,)), pltpu.SemaphoreType.DMA((3,)),
                            pltpu.SemaphoreType.DMA]),
        out_shape=jax.ShapeDtypeStruct((4, D, D), BF16),
        compiler_params=_params(("arbitrary", "arbitrary")),
    )(order, h, dpa, dpb, dpc)


def _inproj_dgrad(dpa, dpb, dpc, wg, x2d, dx2, g_in, smalls):
    t = x2d.shape[0]
    tm = 512
    ni = t // tm
    nsm = len(smalls)

    def body(da_ref, db_ref, dc_ref, w_ref, x_ref, dx2_ref, g_ref, *rest):
        small_refs, rest = rest[:nsm], rest[nsm:]
        gx_ref, gg_ref = rest[:2]
        all_refs, rest = rest[2:2 + nsm], rest[2 + nsm:]
        acc_s, rows_s, send_sems, recv_sems, own_sems, row_send, row_recv = rest
        i, j = pl.program_id(0), pl.program_id(1)
        own, first, arrive, forward, others = _gather_copies(small_refs, all_refs, send_sems, recv_sems, own_sems)

        @pl.when(jnp.logical_and(i == 0, j == 0))
        def _():
            gg_ref[...] = jnp.zeros_like(gg_ref)
            for cp in own + first:
                cp.start()

        @pl.when(jnp.logical_and(i == ni // 2, j == 0))
        def _():
            for came, on in zip(arrive, forward):
                came.wait_recv()
                on.start()

        @pl.when(j == 0)
        def _():
            acc_s[...] = jnp.zeros_like(acc_s)

        def use(d):
            acc_s[...] += _dot_nt(d, w_ref[...])

        _dproj_pick(j, da_ref, db_ref, dc_ref, use)

        @pl.when(j == NSEG - 1)
        def _():
            x = x_ref[...]
            r = lax.rsqrt(jnp.mean(x * x, axis=-1, keepdims=True) + EPS)
            xh = x * r
            dh = acc_s[...]
            gg_ref[...] += jnp.sum(dh * xh, axis=0, keepdims=True)
            dxh = dh * g_ref[...]
            gx_ref[...] = dx2_ref[...] + r * (dxh - xh * jnp.mean(dxh * xh, axis=-1, keepdims=True))

        @pl.when(jnp.logical_and(i == ni - 1, j == NSEG - 1))
        def _():
            for cp in others:
                cp.wait_recv()
            for cp in first + forward:
                cp.wait_send()
            for cp in own:
                cp.wait()
            x, y, c = _place()
            me = 4 * x + 2 * y + c
            rows_s[me] = gg_ref[...]
            flips = [(fx, fy, fc) for fx in (0, 1) for fy in (0, 1) for fc in (0, 1)][1:]
            sends, recvs = [], []
            for n, (fx, fy, fc) in enumerate(flips):
                px, py, pc = (1 - x if fx else x), (1 - y if fy else y), (1 - c if fc else c)
                sends.append(pltpu.make_async_remote_copy(
                    src_ref=rows_s.at[me], dst_ref=rows_s.at[me], send_sem=row_send.at[n], recv_sem=row_recv.at[n],
                    device_id=(px, py, pc), device_id_type=MESH))
                recvs.append(pltpu.make_async_remote_copy(
                    src_ref=rows_s.at[me], dst_ref=rows_s.at[4 * px + 2 * py + pc], send_sem=row_send.at[n],
                    recv_sem=row_recv.at[n], device_id=(px, py, pc), device_id_type=MESH))
            for cp in sends:
                cp.start()
            for cp in recvs:
                cp.wait_recv()
            for cp in sends:
                cp.wait_send()
            total = rows_s[0]
            for p in range(1, NDEV):
                total = total + rows_s[p]
            gg_ref[...] = total

    row = lambda: pl.BlockSpec((tm, D), lambda i, j: (i, 0))
    vec = pl.BlockSpec((1, D), lambda i, j: (0, 0))
    return pl.pallas_call(
        body, name="inproj_dgrad", grid=(ni, NSEG),
        in_specs=_dproj_specs(tm, lambda i, j: j, lambda i, j: i)
        + [pl.BlockSpec((None, D, D), lambda i, j: (j, 0, 0)), row(), row(), vec] + [ANY] * nsm,
        out_specs=[row(), vec] + [ANY] * nsm,
        out_shape=[jax.ShapeDtypeStruct((t, D), F32), jax.ShapeDtypeStruct((1, D), F32)]
        + [jax.ShapeDtypeStruct((NDEV,) + a.shape, a.dtype) for a in smalls],
        scratch_shapes=[pltpu.VMEM((tm, D), F32), pltpu.VMEM((NDEV, 1, D), F32),
                        pltpu.SemaphoreType.DMA((nsm, 7)), pltpu.SemaphoreType.DMA((nsm, 7)),
                        pltpu.SemaphoreType.DMA((nsm,)),
                        pltpu.SemaphoreType.DMA((NDEV - 1,)), pltpu.SemaphoreType.DMA((NDEV - 1,))],
        compiler_params=_params(("arbitrary", "arbitrary")),
    )(dpa, dpb, dpc, wg, x2d, dx2, g_in, *smalls)


def _adam_update(g, w, m, v):
    m_new = ADAM_B1 * m + (1.0 - ADAM_B1) * g
    v_new = ADAM_B2 * v + (1.0 - ADAM_B2) * (g * g)
    m_hat = m_new / (1.0 - ADAM_B1 ** ADAM_STEP)
    v_hat = v_new / (1.0 - ADAM_B2 ** ADAM_STEP)
    return -ADAM_LR * (m_hat / (jnp.sqrt(v_hat) + ADAM_EPS) + ADAM_WD * w), m_new, v_new


def _sum_in_order(ref):
    total = ref[0].astype(F32)
    for k in range(1, ref.shape[0]):
        total = total + ref[k].astype(F32)
    return total


def _adamw_small(me, vec_all, gx_all, ga_all, g_norm_in, groups):
    flat = [a for grp in groups for a in grp]
    ng = len(groups)
    nshard = D // NDEV

    def body(me_ref, vec_ref, shard_ref, gx_ref, ga_ref, gn_ref, *refs):
        ins, outs = refs[:3 * ng], refs[3 * ng:]
        vec = _sum_in_order(vec_ref)
        shard = _sum_in_order(shard_ref)
        grads = [gn_ref[...]] + [vec[r:r + 1, :] for r in range(1, 6)]
        grads += [shard[0:4, :], shard[4:8, 0:DK // NDEV], _sum_in_order(gx_ref), _sum_in_order(ga_ref)]
        for n, g in enumerate(grads):
            delta, m_new, v_new = _adam_update(g, ins[3 * n][...], ins[3 * n + 1][...], ins[3 * n + 2][...])
            outs[4 * n][...] = g
            outs[4 * n + 1][...] = delta
            outs[4 * n + 2][...] = m_new
            outs[4 * n + 3][...] = v_new
        outs[4 * ng][...] = jnp.sum(vec[6:7, :], axis=1, keepdims=True)

    full = lambda a: pl.BlockSpec(a.shape, lambda i, me_ref, nd=len(a.shape): (0,) * nd)
    out_shape = [jax.ShapeDtypeStruct(w.shape, F32) for w, _, _ in groups for _ in range(4)]
    out_shape.append(jax.ShapeDtypeStruct((1, 1), F32))
    outs = pl.pallas_call(
        body, name="adamw_small",
        grid_spec=pltpu.PrefetchScalarGridSpec(
            num_scalar_prefetch=1, grid=(1,),
            in_specs=[full(vec_all),
                      pl.BlockSpec((NDEV, 8, nshard), lambda i, me_ref: (0, 1, me_ref[0])),
                      full(gx_all), full(ga_all), full(g_norm_in)] + [full(a) for a in flat],
            out_specs=[full(s) for s in out_shape]),
        out_shape=out_shape,
        compiler_params=_params(("arbitrary",)),
    )(me, vec_all, vec_all, gx_all, ga_all, g_norm_in, *flat)
    return [outs[4 * n:4 * n + 4] for n in range(ng)], outs[4 * ng]


def _adamw(name, parts, w, m, v):
    n, rows, cols = parts.shape
    tr = rows if rows <= 256 else 256

    def body(p_ref, w_ref, m_ref, v_ref, g_ref, d_ref, nm_ref, nv_ref):
        g = _sum_in_order(p_ref)
        delta, m_new, v_new = _adam_update(g, w_ref[...], m_ref[...], v_ref[...])
        g_ref[...] = g
        d_ref[...] = delta
        nm_ref[...] = m_new
        nv_ref[...] = v_new

    blk = lambda: pl.BlockSpec((tr, cols), lambda i: (i, 0))
    return pl.pallas_call(
        body, name=name, grid=(rows // tr,),
        in_specs=[pl.BlockSpec((n, tr, cols), lambda i: (0, i, 0)), blk(), blk(), blk()],
        out_specs=[blk(), blk(), blk(), blk()],
        out_shape=[jax.ShapeDtypeStruct((rows, cols), F32)] * 4,
        compiler_params=_params(("arbitrary",)),
    )(parts, w, m, v)


ANY = pl.BlockSpec(memory_space=pl.ANY)


def _place():
    return lax.axis_index("x"), lax.axis_index("y"), lax.axis_index("c")


def _gather_copies(ins, outs, send_sems, recv_sems, own_sems):
    x, y, c = _place()
    me, sibling = (x, y, c), (x, y, 1 - c)
    chips = [(1 - x, y), (x, 1 - y), (1 - x, 1 - y)]
    n = len(ins)

    def copy(a, k, block, to, src=None):
        px, py, pc = block
        dst = outs[a].at[4 * px + 2 * py + pc]
        return pltpu.make_async_remote_copy(
            src_ref=dst if src is None else src, dst_ref=dst,
            send_sem=send_sems.at[a, k], recv_sem=recv_sems.at[a, k], device_id=to, device_id_type=MESH)

    own = [pltpu.make_async_copy(ins[a], outs[a].at[4 * x + 2 * y + c], own_sems.at[a]) for a in range(n)]
    first = []
    for a in range(n):
        first.append(copy(a, 0, me, sibling, src=ins[a]))
        first += [copy(a, 1 + j, me, (*chip, c), src=ins[a]) for j, chip in enumerate(chips)]
    arrive = [copy(a, 1 + j, (*chip, c), me) for j, chip in enumerate(chips) for a in range(n)]
    forward = [copy(a, 4 + j, (*chip, c), sibling) for j, chip in enumerate(chips) for a in range(n)]
    rest = [copy(a, 0, sibling, me) for a in range(n)]
    rest += [copy(a, 4 + j, (*chip, 1 - c), me) for a in range(n) for j, chip in enumerate(chips)]
    return own, first, arrive, forward, rest


def _sibling_copies(ins, outs, send_sems, recv_sems):
    x, y, c = _place()
    return [pltpu.make_async_remote_copy(
        src_ref=ins[a].at[2 * q + 1 - c], dst_ref=outs[a].at[q],
        send_sem=send_sems.at[a, q], recv_sem=recv_sems.at[a, q],
        device_id=(x, y, 1 - c), device_id_type=MESH) for a in range(len(ins)) for q in range(4)]


def _chip_copies(ins, outs, send_sems, recv_sems, local_sems):
    x, y, c = _place()
    my_chip = 2 * x + y
    chips = [(1 - x, y), (x, 1 - y), (1 - x, 1 - y)]
    n = len(ins)
    mine = [pltpu.make_async_copy(ins[a].at[my_chip], outs[a].at[my_chip], local_sems.at[a]) for a in range(n)]
    sends = [pltpu.make_async_remote_copy(
        src_ref=ins[a].at[2 * px + py], dst_ref=outs[a].at[my_chip],
        send_sem=send_sems.at[a, j], recv_sem=recv_sems.at[a, j],
        device_id=(px, py, c), device_id_type=MESH) for a in range(n) for j, (px, py) in enumerate(chips)]
    recvs = [pltpu.make_async_remote_copy(
        src_ref=ins[a].at[my_chip], dst_ref=outs[a].at[2 * px + py],
        send_sem=send_sems.at[a, j], recv_sem=recv_sems.at[a, j],
        device_id=(px, py, c), device_id_type=MESH) for a in range(n) for j, (px, py) in enumerate(chips)]
    return mine, sends, recvs


def _chip_sum(owns, gots, core):
    n = len(owns)
    _, rows, cols = owns[0].shape

    def body(core_ref, *refs):
        for a in range(n):
            refs[2 * n + a][...] = (refs[a][...] + refs[n + a][...]).astype(BF16)

    own_spec = pl.BlockSpec((None, rows, cols), lambda q, core_ref: (2 * q + core_ref[0], 0, 0))
    slab = pl.BlockSpec((None, rows, cols), lambda q, core_ref: (q, 0, 0))
    return pl.pallas_call(
        body, name="chip_sum",
        grid_spec=pltpu.PrefetchScalarGridSpec(
            num_scalar_prefetch=1, grid=(4,),
            in_specs=[own_spec] * n + [slab] * n, out_specs=[slab] * n),
        out_shape=[jax.ShapeDtypeStruct((4, rows, cols), BF16)] * n,
        compiler_params=_params(("arbitrary",)),
    )(core, *owns, *gots)


def _block_diag(w):
    w4 = w.reshape(NCB, 4, 64, 64)
    eye = jnp.eye(4, dtype=w.dtype)
    return (w4[:, :, :, None, :] * eye[None, :, None, :, None]).reshape(NCB, CB, CB)


def _block_diag_back(g):
    g5 = g.reshape(NCB, 4, 64, 4, 64)
    return jnp.stack([g5[:, m, :, m, :] for m in range(4)], axis=1).reshape(16, 64, 64)


def kernel(x, norm_in, w_in, conv_w, conv_b, gate_x_w, gate_x_b, gate_a_w, gate_a_b, lru_lambda, gn_gain, w_proj_a, w_proj_b, w_out, norm_final, loss_target, m_norm_in, m_w_in, m_conv_w, m_conv_b, m_gate_x_w, m_gate_x_b, m_gate_a_w, m_gate_a_b, m_lru_lambda, m_gn_gain, m_w_proj_a, m_w_proj_b, m_w_out, m_norm_final, v_norm_in, v_w_in, v_conv_w, v_conv_b, v_gate_x_w, v_gate_x_b, v_gate_a_w, v_gate_a_b, v_lru_lambda, v_gn_gain, v_w_proj_a, v_w_proj_b, v_w_out, v_norm_final):
    xi, yi, ci = _place()
    me = 4 * xi + 2 * yi + ci
    core = ci.astype(jnp.int32).reshape(1)
    nshard = D // NDEV
    nb = x.shape[0]
    t = nb * S
    x2d = x.reshape(t, D)
    tgt2d = loss_target.reshape(t, D)
    g_final = norm_final.reshape(1, D)
    wbd = jnp.concatenate([_block_diag(gate_x_w[0]), _block_diag(gate_a_w[0])], axis=-1).astype(BF16)
    tables = _retention_tables()

    wp_own = jnp.concatenate([w_proj_a[0], w_proj_b[0], w_out[0]], axis=0).astype(BF16)
    tiny = jnp.concatenate([conv_w[0], jnp.pad(gn_gain[0], ((0, 0), (0, nshard - DK // NDEV)))], axis=0)
    proj, h, wg, wpg, tiny_g = _inproj_gather(x2d, norm_in, w_in[0].astype(BF16), wp_own, tiny,
                                              _gather_order(xi, yi, ci))
    conv_w_full = tiny_g[:, 0:4, :].transpose(1, 0, 2).reshape(4, D)
    gain3 = tiny_g[:, 4:8, :DK // NDEV].transpose(1, 0, 2).reshape(HEADS, 1, DK)

    ya, hs = _lru_fwd(proj, conv_w_full, conv_b, wbd, gate_x_b, gate_a_b, lru_lambda, nb)
    yb, qr, kr, o, rs = _ret_fwd(proj, gain3, tables, nb)
    dx2, dya, dyb, dpc, merged, doa, dob, g_fin, loss_vec = _tail(ya, yb, proj, x2d, tgt2d, wpg, g_final)
    g_pa, g_pb, g_out = _tail_wgrad(ya, yb, merged, doa, dob, dx2)

    own = [g.reshape(NDEV, nshard, D) for g in (g_pa, g_pb, g_out)]
    dpa, g_wbd, g_vec, *got = _lru_bwd(proj, hs, dya, conv_w_full, conv_b, wbd, gate_x_b, gate_a_b, lru_lambda, nb, own)
    sums = _chip_sum(own, got, core)
    dpb, g_gain, *parts = _ret_bwd(proj, qr, kr, o, rs, dyb, gain3, tables, nb, sums)

    parts_in = _inproj_wgrad_rs(h, dpa, dpb, dpc, _rs_order(2 * xi + yi, ci))
    parts = [parts_in] + list(parts)

    gain_rows = jnp.pad(g_gain.reshape(HEADS, NDEV, DK // NDEV), ((0, 0), (0, 0), (0, nshard - DK // NDEV)))
    zero_row = jnp.zeros((1, D), F32)
    vec = jnp.concatenate([zero_row, g_vec[0:4], g_fin, loss_vec, zero_row, g_vec[4:8], gain_rows.reshape(HEADS, D)],
                          axis=0)
    g_gx = _block_diag_back(g_wbd[:, :, :CB]).reshape(D, 64)
    g_ga = _block_diag_back(g_wbd[:, :, CB:]).reshape(D, 64)
    grad_x, g_norm_in, vec_all, gx_all, ga_all = _inproj_dgrad(dpa, dpb, dpc, wg, x2d, dx2, norm_in, [vec, g_gx, g_ga])
    grad_x = grad_x.reshape(nb, S, D)

    big = [("w_in", w_in, m_w_in, v_w_in), ("w_proj_a", w_proj_a, m_w_proj_a, v_w_proj_a),
           ("w_proj_b", w_proj_b, m_w_proj_b, v_w_proj_b), ("w_out", w_out, m_w_out, v_w_out)]
    res = {}
    for k, (nm, w, m, v) in enumerate(big):
        out = _adamw("adamw_" + nm, parts[k], w[0], m[0], v[0])
        res[nm] = [o[None] for o in out]

    row = lambda a: a.reshape(1, D)
    gate = lambda a: a.reshape(D, 64)
    groups = [("norm_in", norm_in, m_norm_in, v_norm_in, row), ("conv_b", conv_b, m_conv_b, v_conv_b, row),
              ("gate_x_b", gate_x_b, m_gate_x_b, v_gate_x_b, row), ("gate_a_b", gate_a_b, m_gate_a_b, v_gate_a_b, row),
              ("lru_lambda", lru_lambda, m_lru_lambda, v_lru_lambda, row),
              ("norm_final", norm_final, m_norm_final, v_norm_final, row),
              ("conv_w", conv_w, m_conv_w, v_conv_w, lambda a: a[0]), ("gn_gain", gn_gain, m_gn_gain, v_gn_gain, lambda a: a[0]),
              ("gate_x_w", gate_x_w, m_gate_x_w, v_gate_x_w, gate), ("gate_a_w", gate_a_w, m_gate_a_w, v_gate_a_w, gate)]
    small_out, loss = _adamw_small(me.astype(jnp.int32).reshape(1), vec_all, gx_all, ga_all, g_norm_in,
                                   [tuple(view(a) for a in (w, m, v)) for _, w, m, v, view in groups])
    for (nm, w, _, _, _), out in zip(groups, small_out):
        res[nm] = [o.reshape(w.shape) for o in out]
    loss = loss.reshape(())

    order = ["norm_in", "w_in", "conv_w", "conv_b", "gate_x_w", "gate_x_b", "gate_a_w", "gate_a_b", "lru_lambda",
             "gn_gain", "w_proj_a", "w_proj_b", "w_out", "norm_final"]
    outs = [loss, grad_x]
    for k in range(4):
        outs += [res[nm][k] for nm in order]
    return tuple(outs)
```

```python
import jax
import jax.numpy as jnp
from jax import lax
from jax.experimental import pallas as pl
from jax.experimental.pallas import tpu as pltpu

F32 = jnp.float32
BF16 = jnp.bfloat16
MESH = pl.DeviceIdType.MESH

D = 1024
S = 2048
NSEG = 8
NDEV = 8
HEADS = 4
DK = 256
CH = 256
NCH = S // CH
CB = 256
NCB = D // CB
RC = 128
SCAN_GROUP = 8
EPS = 1e-6
LRU_C = 8.0
VMEM_LIMIT = 56 * 1024 * 1024

ADAM_LR = 0.001
ADAM_B1 = 0.9
ADAM_B2 = 0.999
ADAM_EPS = 1e-08
ADAM_WD = 0.01
ADAM_STEP = 10


def _params(sem=None):
    return pltpu.CompilerParams(dimension_semantics=sem, vmem_limit_bytes=VMEM_LIMIT)


def _dot(a, b):
    return jnp.dot(a, b, preferred_element_type=F32)


def _dot_nt(a, b):
    return lax.dot_general(a, b, (((1,), (1,)), ((), ())), preferred_element_type=F32)


def _dot_tn(a, b):
    return lax.dot_general(a, b, (((0,), (0,)), ((), ())), preferred_element_type=F32)


def _sigmoid(x):
    return jax.nn.sigmoid(x)


def _expm1_nonpos(x):
    poly = x * (1.0 + x * (0.5 + x * (1.0 / 6.0 + x * (1.0 / 24.0))))
    return jnp.where(x > -0.05, poly, jnp.exp(x) - 1.0)


def _softplus(x):
    return jnp.maximum(x, 0.0) + jnp.log(1.0 + jnp.exp(-jnp.abs(x)))


def _rows(c, n):
    return pl.ds(pl.multiple_of(c * n, n), n)


def _window_before(ref, c, n):
    r0 = c * n
    prev = ref[pl.ds(pl.multiple_of(jnp.maximum(r0 - 8, 0), 8), 8), :]
    prev = jnp.where(c > 0, prev, 0.0)
    return jnp.concatenate([prev, ref[_rows(c, n), :]], axis=0)


def _shift_down(win, s, n):
    if s == 0:
        return win[8:, :]
    return pltpu.roll(win, s, 0)[8:, :]


def _shift_up(win, s, n):
    if s == 0:
        return win[:n, :]
    return pltpu.roll(win, n + 8 - s, 0)[:n, :]


def _gather_order(x, y, c):
    chips = [(1 - x, y), (x, 1 - y), (1 - x, 1 - y)]
    order = [4 * x + 2 * y + c, 4 * x + 2 * y + 1 - c]
    for px, py in chips:
        order += [4 * px + 2 * py + c, 4 * px + 2 * py + 1 - c]
    return jnp.stack(order).astype(jnp.int32)


def _inproj_gather(x2d, g_in, w_own, wp_own, tiny_own, order):
    t = x2d.shape[0]
    tm = 1024
    nt = t // tm

    def body(order_ref, x_ref, g_ref, w_own_ref, wp_own_ref, tiny_own_ref,
             proj_ref, h_ref, wg_ref, wpg_ref, tinyg_ref,
             w_all, h_all, send_sems, recv_sems, own_sems, out_sems):
        k, i = pl.program_id(0), pl.program_id(1)
        x, y, c = _place()
        me, sibling = (x, y, c), (x, y, 1 - c)
        chips = [(1 - x, y), (x, 1 - y), (1 - x, 1 - y)]
        srcs = [w_own_ref, wp_own_ref, tiny_own_ref]
        dsts = [w_all, wpg_ref, tinyg_ref]

        def copy(a, n, block, to, own_src=False):
            px, py, pc = block
            dst = dsts[a].at[4 * px + 2 * py + pc]
            return pltpu.make_async_remote_copy(
                src_ref=srcs[a] if own_src else dst, dst_ref=dst,
                send_sem=send_sems.at[a, n], recv_sem=recv_sems.at[a, n], device_id=to, device_id_type=MESH)

        def own_copy(a):
            return pltpu.make_async_copy(srcs[a], dsts[a].at[4 * x + 2 * y + c], own_sems.at[a])

        def keep_copy(n):
            return pltpu.make_async_copy(w_all.at[order_ref[n]], wg_ref.at[order_ref[n]], out_sems.at[n])

        def first_copies(a):
            return [copy(a, 0, me, sibling, True)] + [copy(a, 1 + j, me, (*chip, c), True) for j, chip in enumerate(chips)]

        def at_slot(n):
            return jnp.logical_and(k == n, i == 0)

        @pl.when(at_slot(0))
        def _():
            for a in range(3):
                own_copy(a).start()
            for a in range(3):
                for cp in first_copies(a):
                    cp.start()
            own_copy(0).wait()
            keep_copy(0).start()

        @pl.when(at_slot(1))
        def _():
            copy(0, 0, sibling, me).wait_recv()
            keep_copy(1).start()

        for j, chip in enumerate(chips):
            @pl.when(at_slot(2 + 2 * j))
            def _():
                copy(0, 1 + j, (*chip, c), me).wait_recv()
                copy(0, 4 + j, (*chip, c), sibling).start()
                keep_copy(2 + 2 * j).start()

            @pl.when(at_slot(3 + 2 * j))
            def _():
                copy(0, 4 + j, (*chip, 1 - c), me).wait_recv()
                keep_copy(3 + 2 * j).start()

        rows = pl.ds(pl.multiple_of(i * tm, tm), tm)

        @pl.when(k == 0)
        def _():
            xv = x_ref[...]
            r = lax.rsqrt(jnp.mean(xv * xv, axis=-1, keepdims=True) + EPS)
            hv = (xv * r * g_ref[...]).astype(BF16)
            h_ref[...] = hv
            h_all[rows, :] = hv

        proj_ref[...] = _dot(h_all[rows, :], w_all[order_ref[k]])

        @pl.when(jnp.logical_and(k == NSEG - 1, i == nt - 1))
        def _():
            for a in (1, 2):
                for j, chip in enumerate(chips):
                    copy(a, 1 + j, (*chip, c), me).wait_recv()
                    copy(a, 4 + j, (*chip, c), sibling).start()
            for a in (1, 2):
                copy(a, 0, sibling, me).wait_recv()
                for j, chip in enumerate(chips):
                    copy(a, 4 + j, (*chip, 1 - c), me).wait_recv()
            for a in range(3):
                for cp in first_copies(a):
                    cp.wait_send()
                for j, chip in enumerate(chips):
                    copy(a, 4 + j, (*chip, c), sibling).wait_send()
            for a in (1, 2):
                own_copy(a).wait()
            for n in range(NSEG):
                keep_copy(n).wait()

    hold = lambda k, i, order_ref: (jnp.where(k == 0, i, nt - 1), 0)
    return pl.pallas_call(
        body, name="inproj_gather",
        grid_spec=pltpu.PrefetchScalarGridSpec(
            num_scalar_prefetch=1, grid=(NSEG, nt),
            in_specs=[pl.BlockSpec((tm, D), hold),
                      pl.BlockSpec((1, D), lambda k, i, order_ref: (0, 0)),
                      ANY, ANY, ANY],
            out_specs=[pl.BlockSpec((None, tm, D), lambda k, i, order_ref: (order_ref[k], i, 0)),
                       pl.BlockSpec((tm, D), hold),
                       ANY, ANY, ANY],
            scratch_shapes=[pltpu.VMEM((NDEV, D, D), BF16), pltpu.VMEM((t, D), BF16),
                            pltpu.SemaphoreType.DMA((3, 7)), pltpu.SemaphoreType.DMA((3, 7)),
                            pltpu.SemaphoreType.DMA((3,)), pltpu.SemaphoreType.DMA((NSEG,))]),
        out_shape=[jax.ShapeDtypeStruct((NSEG, t, D), F32), jax.ShapeDtypeStruct((t, D), BF16),
                   jax.ShapeDtypeStruct((NDEV,) + w_own.shape, BF16),
                   jax.ShapeDtypeStruct((NDEV,) + wp_own.shape, BF16),
                   jax.ShapeDtypeStruct((NDEV,) + tiny_own.shape, F32)],
        compiler_params=_params(("arbitrary", "arbitrary")),
    )(order, x2d, g_in, w_own, wp_own, tiny_own)


def _tile_scan(a, u):
    row = lax.broadcasted_iota(jnp.int32, a.shape, 0)
    for d in (1, 2, 4):
        m = row >= d
        a_sh = pltpu.roll(a, d, 0)
        u_sh = pltpu.roll(u, d, 0)
        u = jnp.where(m, a * u_sh + u, u)
        a = jnp.where(m, a * a_sh, a)
    return a, u


def _tile_scan_rev(a, w):
    row = lax.broadcasted_iota(jnp.int32, a.shape, 0)
    for d in (1, 2, 4):
        m = row < 8 - d
        a_sh = pltpu.roll(a, 8 - d, 0)
        w_sh = pltpu.roll(w, 8 - d, 0)
        w = jnp.where(m, a * w_sh + w, w)
        a = jnp.where(m, a * a_sh, a)
    return a, w


def _lru_gates(xa_ref, c, cw_ref, cb_ref, wbd_ref, bx_ref, ba_ref, sp):
    win = _window_before(xa_ref, c, RC)
    xc = cb_ref[...] + cw_ref[3:4, :] * _shift_down(win, 0, RC)
    for s in (1, 2, 3):
        xc = xc + cw_ref[3 - s:4 - s, :] * _shift_down(win, s, RC)
    z = _dot(xc.astype(BF16), wbd_ref[...])
    gi = _sigmoid(z[:, :CB] + bx_ref[...])
    gr = _sigmoid(z[:, CB:] + ba_ref[...])
    log_a = -LRU_C * gr * sp
    return win, xc, gi, gr, log_a


def _lru_fwd(proj, conv_w, conv_b, wbd, bx, ba, lam, nb):
    t = nb * S

    def body(xa_ref, ga_ref, cw_ref, cb_ref, wbd_ref, bx_ref, ba_ref, lam_ref, ya_ref, hs_ref, a_s, u_s):
        sp = _softplus(-lam_ref[...])

        def gates(c, carry):
            _, xc, gi, _, log_a = _lru_gates(xa_ref, c, cw_ref, cb_ref, wbd_ref, bx_ref, ba_ref, sp)
            a_s[_rows(c, RC), :] = jnp.exp(log_a)
            u_s[_rows(c, RC), :] = jnp.sqrt(-_expm1_nonpos(2.0 * log_a)) * (gi * xc)
            return carry

        lax.fori_loop(0, S // RC, gates, 0)

        def scan(g, h):
            for k in range(SCAN_GROUP):
                rows = pl.ds(pl.multiple_of(g * (8 * SCAN_GROUP), 8 * SCAN_GROUP) + 8 * k, 8)
                a_cum, u_cum = _tile_scan(a_s[rows, :], u_s[rows, :])
                hs_ref[rows, :] = u_cum + a_cum * h
                h = u_cum[7:8, :] + a_cum[7:8, :] * h
            return h

        lax.fori_loop(0, S // (8 * SCAN_GROUP), scan, jnp.zeros((1, CB), F32))

        def gate_out(c, carry):
            ga = ga_ref[_rows(c, RC), :]
            ya_ref[_rows(c, RC), :] = (ga * _sigmoid(ga) * hs_ref[_rows(c, RC), :]).astype(BF16)
            return carry

        lax.fori_loop(0, S // RC, gate_out, 0)

    vec = pl.BlockSpec((1, CB), lambda b, cb: (0, cb))
    return pl.pallas_call(
        body, name="lru_fwd", grid=(nb, NCB),
        in_specs=[pl.BlockSpec((None, S, CB), lambda b, cb: (0, b, cb)),
                  pl.BlockSpec((None, S, CB), lambda b, cb: (1, b, cb)),
                  pl.BlockSpec((4, CB), lambda b, cb: (0, cb)),
                  vec,
                  pl.BlockSpec((None, CB, 2 * CB), lambda b, cb: (cb, 0, 0)),
                  vec, vec, vec],
        out_specs=[pl.BlockSpec((S, CB), lambda b, cb: (b, cb)),
                   pl.BlockSpec((S, CB), lambda b, cb: (b, cb))],
        out_shape=[jax.ShapeDtypeStruct((t, D), BF16), jax.ShapeDtypeStruct((t, D), F32)],
        scratch_shapes=[pltpu.VMEM((S, CB), F32), pltpu.VMEM((S, CB), F32)],
        compiler_params=_params(("arbitrary", "arbitrary")),
    )(proj, proj, conv_w, conv_b, wbd, bx, ba, lam)


def _lru_bwd(proj, hs, dya, conv_w, conv_b, wbd, bx, ba, lam, nb, give):
    t = nb * S
    ng = len(give)

    def body(xa_ref, ga_ref, hs_ref, dya_ref, cw_ref, cb_ref, wbd_ref, bx_ref, ba_ref, lam_ref, *rest):
        give_refs, rest = rest[:ng], rest[ng:]
        dp_ref, dwbd_ref, vec_ref = rest[:3]
        got_refs, rest = rest[3:3 + ng], rest[3 + ng:]
        a_s, xc_s, gi_s, gr_s, dl_s, dh_s, dxc_s, acc_s, send_sems, recv_sems = rest
        b = pl.program_id(1)
        exchange = _sibling_copies(give_refs, got_refs, send_sems, recv_sems)

        @pl.when(jnp.logical_and(pl.program_id(0) == 0, b == 0))
        def _():
            for cp in exchange:
                cp.start()

        lam_v = lam_ref[...]
        sp = _softplus(-lam_v)
        acc_s[...] = jnp.zeros_like(acc_s)

        @pl.when(b == 0)
        def _():
            dwbd_ref[...] = jnp.zeros_like(dwbd_ref)
            vec_ref[...] = jnp.zeros_like(vec_ref)

        def gates(c, carry):
            _, xc, gi, gr, log_a = _lru_gates(xa_ref, c, cw_ref, cb_ref, wbd_ref, bx_ref, ba_ref, sp)
            rows = _rows(c, RC)
            a_s[rows, :] = jnp.exp(log_a)
            xc_s[rows, :] = xc
            gi_s[rows, :] = gi
            gr_s[rows, :] = gr
            ga = ga_ref[rows, :]
            sg = _sigmoid(ga)
            dya_c = dya_ref[rows, :]
            dl_s[rows, :] = dya_c * (ga * sg)
            dp_ref[1, rows, :] = (dya_c * hs_ref[rows, :] * (sg * (1.0 + ga * (1.0 - sg)))).astype(BF16)
            return carry

        lax.fori_loop(0, S // RC, gates, 0)

        def scan(i, g_in):
            base = pl.multiple_of((S // (8 * SCAN_GROUP) - 1 - i) * (8 * SCAN_GROUP), 8 * SCAN_GROUP)
            row = lax.broadcasted_iota(jnp.int32, (8, CB), 0)
            for k in reversed(range(SCAN_GROUP)):
                rows = pl.ds(base + 8 * k, 8)
                a = a_s[rows, :]
                dl = dl_s[rows, :]
                a_cum, g_loc = _tile_scan_rev(a, a * dl)
                g = g_loc + a_cum * g_in
                dh_s[rows, :] = dl + jnp.where(row < 7, pltpu.roll(g, 7, 0), g_in)
                g_in = g_loc[0:1, :] + a_cum[0:1, :] * g_in
            return g_in

        lax.fori_loop(0, S // (8 * SCAN_GROUP), scan, jnp.zeros((1, CB), F32))

        dxc_s[pl.ds(S, 8), :] = jnp.zeros((8, CB), F32)

        def grads(c, carry):
            rows = _rows(c, RC)
            dh = dh_s[rows, :]
            h_prev = _shift_down(_window_before(hs_ref, c, RC), 1, RC)
            xc, gi, gr, a = xc_s[rows, :], gi_s[rows, :], gr_s[rows, :], a_s[rows, :]
            mult = jnp.sqrt(-_expm1_nonpos(-2.0 * LRU_C * gr * sp))
            dmult = dh * gi * xc
            d_log_a = dh * h_prev * a - dmult * (a * a) / mult
            dzi = dh * mult * xc * gi * (1.0 - gi)
            dzr = d_log_a * (-LRU_C * sp) * gr * (1.0 - gr)
            dz = jnp.concatenate([dzi, dzr], axis=1).astype(BF16)
            dxc_s[rows, :] = dh * mult * gi + _dot_nt(dz, wbd_ref[...])
            dwbd_ref[...] += _dot_tn(xc.astype(BF16), dz)
            acc_s[1:2, :] += jnp.sum(dzi, axis=0, keepdims=True)
            acc_s[2:3, :] += jnp.sum(dzr, axis=0, keepdims=True)
            acc_s[3:4, :] += jnp.sum(d_log_a * (-LRU_C * gr), axis=0, keepdims=True)
            return carry

        lax.fori_loop(0, S // RC, grads, 0)

        def conv_bwd(c, carry):
            rows = _rows(c, RC)
            dwin = dxc_s[pl.ds(pl.multiple_of(c * RC, RC), RC + 8), :]
            dxc = dwin[:RC, :]
            xwin = _window_before(xa_ref, c, RC)
            dxa = cw_ref[3:4, :] * dxc
            acc_s[0:1, :] += jnp.sum(dxc, axis=0, keepdims=True)
            acc_s[7:8, :] += jnp.sum(dxc * _shift_down(xwin, 0, RC), axis=0, keepdims=True)
            for s in (1, 2, 3):
                dxa = dxa + cw_ref[3 - s:4 - s, :] * _shift_up(dwin, s, RC)
                acc_s[7 - s:8 - s, :] += jnp.sum(dxc * _shift_down(xwin, s, RC), axis=0, keepdims=True)
            dp_ref[0, rows, :] = dxa.astype(BF16)
            return carry

        lax.fori_loop(0, S // RC, conv_bwd, 0)

        row = lax.broadcasted_iota(jnp.int32, acc_s.shape, 0)
        vec_ref[...] += jnp.where(row == 3, acc_s[...] * (-_sigmoid(-lam_v)), acc_s[...])

        @pl.when(jnp.logical_and(pl.program_id(0) == NCB - 1, b == nb - 1))
        def _():
            for cp in exchange:
                cp.wait()

    vec = pl.BlockSpec((1, CB), lambda cb, b: (0, cb))
    blk = pl.BlockSpec((S, CB), lambda cb, b: (b, cb))
    return pl.pallas_call(
        body, name="lru_bwd", grid=(NCB, nb),
        in_specs=[pl.BlockSpec((None, S, CB), lambda cb, b: (0, b, cb)),
                  pl.BlockSpec((None, S, CB), lambda cb, b: (1, b, cb)),
                  blk, blk,
                  pl.BlockSpec((4, CB), lambda cb, b: (0, cb)),
                  vec,
                  pl.BlockSpec((None, CB, 2 * CB), lambda cb, b: (cb, 0, 0)),
                  vec, vec, vec] + [ANY] * ng,
        out_specs=[pl.BlockSpec((2, S, CB), lambda cb, b: (0, b, cb)),
                   pl.BlockSpec((None, CB, 2 * CB), lambda cb, b: (cb, 0, 0)),
                   pl.BlockSpec((8, CB), lambda cb, b: (0, cb))] + [ANY] * ng,
        out_shape=[jax.ShapeDtypeStruct((2, t, D), BF16),
                   jax.ShapeDtypeStruct((NCB, CB, 2 * CB), F32),
                   jax.ShapeDtypeStruct((8, D), F32)]
        + [jax.ShapeDtypeStruct((4,) + g.shape[1:], g.dtype) for g in give],
        scratch_shapes=[pltpu.VMEM((S, CB), F32), pltpu.VMEM((S, CB), F32), pltpu.VMEM((S, CB), F32),
                        pltpu.VMEM((S, CB), F32), pltpu.VMEM((S, CB), F32), pltpu.VMEM((S, CB), F32),
                        pltpu.VMEM((S + 8, CB), F32), pltpu.VMEM((8, CB), F32),
                        pltpu.SemaphoreType.DMA((ng, 4)), pltpu.SemaphoreType.DMA((ng, 4))],
        compiler_params=_params(("arbitrary", "arbitrary")),
    )(proj, proj, hs, dya, conv_w, conv_b, wbd, bx, ba, lam, *give)


def _retention_tables():
    log_g = jnp.log1p(-(2.0 ** (-5.0 - jnp.arange(HEADS, dtype=F32))))
    idx = jnp.arange(CH, dtype=F32)
    diff = idx[:, None] - idx[None, :]
    inner = jnp.where(diff >= 0, jnp.exp(jnp.maximum(diff, 0.0)[None] * log_g[:, None, None]), 0.0)
    cross = jnp.exp((idx[None, :] + 1.0) * log_g[:, None])
    state = jnp.exp((CH - 1.0 - idx[None, :]) * log_g[:, None])
    cross = jnp.broadcast_to(cross[:, :, None], (HEADS, CH, DK))
    state = jnp.broadcast_to(state[:, :, None], (HEADS, CH, DK))
    half = DK // 2
    freqs = 10000.0 ** (-jnp.arange(half, dtype=F32) / half)
    ang = jnp.arange(S, dtype=F32)[:, None] * freqs[None, :]
    return inner, cross, state, jnp.cos(ang), jnp.sin(ang)


def _rotate(x, cos, sin):
    half = DK // 2
    x1, x2 = x[:, :half], x[:, half:]
    return jnp.concatenate([x1 * cos - x2 * sin, x1 * sin + x2 * cos], axis=1)


def _rotate_back(d, cos, sin):
    half = DK // 2
    d1, d2 = d[:, :half], d[:, half:]
    return jnp.concatenate([d1 * cos + d2 * sin, d2 * cos - d1 * sin], axis=1)


def _ret_fwd(proj, gain, tables, nb):
    t = nb * S
    inner_t, cross_t, state_t, cos_t, sin_t = tables

    def body(q_ref, k_ref, v_ref, gb_ref, gain_ref, dm_ref, cd_ref, sd_ref, cos_ref, sin_ref,
             yb_ref, qr_ref, kr_ref, o_ref, rs_ref, r_s):
        r_s[...] = jnp.zeros_like(r_s)
        chunk_decay = cd_ref[CH - 1:CH, :]

        def chunk(c, carry):
            rows = _rows(c, CH)
            cos, sin = cos_ref[rows, :], sin_ref[rows, :]
            qr = _rotate(q_ref[rows, :], cos, sin).astype(BF16)
            kr = (_rotate(k_ref[rows, :], cos, sin) * (DK ** -0.5)).astype(BF16)
            v = v_ref[rows, :]
            qr_ref[rows, :] = qr
            kr_ref[rows, :] = kr
            r = r_s[...]
            rb = r.astype(BF16)
            rs_ref[c] = rb
            p = (_dot_nt(qr, kr) * dm_ref[...]).astype(BF16)
            o = _dot(p, v.astype(BF16)) + _dot(qr, rb) * cd_ref[...]
            r_s[...] = chunk_decay * r + _dot_tn(kr, (v * sd_ref[...]).astype(BF16))
            o_ref[rows, :] = o
            oc = o - jnp.mean(o, axis=-1, keepdims=True)
            rstd = lax.rsqrt(jnp.mean(oc * oc, axis=-1, keepdims=True) + EPS)
            gb = gb_ref[rows, :]
            yb_ref[rows, :] = (gb * _sigmoid(gb) * (oc * rstd * gain_ref[...])).astype(BF16)
            return carry

        lax.fori_loop(0, NCH, chunk, 0)

    seg = lambda s: pl.BlockSpec((None, S, DK), lambda b, h: (s, b, h))
    tab = pl.BlockSpec((None, CH, DK), lambda b, h: (h, 0, 0))
    rot = pl.BlockSpec((S, DK // 2), lambda b, h: (0, 0))
    blk = pl.BlockSpec((S, DK), lambda b, h: (b, h))
    return pl.pallas_call(
        body, name="ret_fwd", grid=(nb, HEADS),
        in_specs=[seg(2), seg(3), seg(4), seg(5),
                  pl.BlockSpec((None, 1, DK), lambda b, h: (h, 0, 0)),
                  tab, tab, tab, rot, rot],
        out_specs=[blk, blk, blk, blk,
                   pl.BlockSpec((None, None, NCH, DK, DK), lambda b, h: (b, h, 0, 0, 0))],
        out_shape=[jax.ShapeDtypeStruct((t, D), BF16), jax.ShapeDtypeStruct((t, D), BF16),
                   jax.ShapeDtypeStruct((t, D), BF16), jax.ShapeDtypeStruct((t, D), F32),
                   jax.ShapeDtypeStruct((nb, HEADS, NCH, DK, DK), BF16)],
        scratch_shapes=[pltpu.VMEM((DK, DK), F32)],
        compiler_params=_params(("arbitrary", "arbitrary")),
    )(proj, proj, proj, proj, gain, inner_t, cross_t, state_t, cos_t, sin_t)


def _ret_bwd(proj, qr, kr, o, rs, dyb, gain, tables, nb, sums):
    t = nb * S
    ns = len(sums)
    inner_t, cross_t, state_t, cos_t, sin_t = tables

    def body(qr_ref, kr_ref, v_ref, gb_ref, o_ref, dyb_ref, rs_ref, gain_ref, dm_ref, cd_ref, sd_ref,
             cos_ref, sin_ref, *rest):
        sum_refs, rest = rest[:ns], rest[ns:]
        dp_ref, dgain_ref = rest[:2]
        part_refs, rest = rest[2:2 + ns], rest[2 + ns:]
        dr_s, send_sems, recv_sems, local_sems = rest
        mine, sends, recvs = _chip_copies(sum_refs, part_refs, send_sems, recv_sems, local_sems)

        @pl.when(jnp.logical_and(pl.program_id(0) == 0, pl.program_id(1) == 0))
        def _():
            for cp in mine + sends:
                cp.start()

        dr_s[...] = jnp.zeros_like(dr_s)
        chunk_decay = cd_ref[CH - 1:CH, :]

        @pl.when(pl.program_id(1) == 0)
        def _():
            dgain_ref[...] = jnp.zeros_like(dgain_ref)

        def chunk(i, carry):
            c = NCH - 1 - i
            rows = _rows(c, CH)
            gain_v = gain_ref[...]
            o_c = o_ref[rows, :]
            oc = o_c - jnp.mean(o_c, axis=-1, keepdims=True)
            rstd = lax.rsqrt(jnp.mean(oc * oc, axis=-1, keepdims=True) + EPS)
            yn = oc * rstd
            gb = gb_ref[rows, :]
            sg = _sigmoid(gb)
            dyb_c = dyb_ref[rows, :]
            dgn = dyb_c * (gb * sg)
            dp_ref[3, rows, :] = (dyb_c * (yn * gain_v) * (sg * (1.0 + gb * (1.0 - sg)))).astype(BF16)
            dgain_ref[...] += jnp.sum(dgn * yn, axis=0, keepdims=True)
            dyn = dgn * gain_v
            do = rstd * (dyn - jnp.mean(dyn, axis=-1, keepdims=True)
                         - yn * jnp.mean(dyn * yn, axis=-1, keepdims=True))
            dob = do.astype(BF16)
            dox = (do * cd_ref[...]).astype(BF16)

            q_c, k_c = qr_ref[rows, :], kr_ref[rows, :]
            v = v_ref[rows, :]
            vb = v.astype(BF16)
            vs = (v * sd_ref[...]).astype(BF16)
            rb = rs_ref[c]
            d_r = dr_s[...]
            drb = d_r.astype(BF16)
            dm = dm_ref[...]
            p = (_dot_nt(q_c, k_c) * dm).astype(BF16)
            dpm = (_dot_nt(dob, vb) * dm).astype(BF16)
            dq = _dot(dpm, k_c) + _dot_nt(dox, rb)
            dk = _dot_tn(dpm, q_c) + _dot_nt(vs, drb)
            dv = _dot_tn(p, dob) + _dot(k_c, drb) * sd_ref[...]
            dr_s[...] = chunk_decay * d_r + _dot_tn(q_c, dox)

            cos, sin = cos_ref[rows, :], sin_ref[rows, :]
            dp_ref[0, rows, :] = _rotate_back(dq, cos, sin).astype(BF16)
            dp_ref[1, rows, :] = (_rotate_back(dk, cos, sin) * (DK ** -0.5)).astype(BF16)
            dp_ref[2, rows, :] = dv.astype(BF16)
            return carry

        lax.fori_loop(0, NCH, chunk, 0)

        @pl.when(jnp.logical_and(pl.program_id(0) == HEADS - 1, pl.program_id(1) == nb - 1))
        def _():
            for cp in recvs:
                cp.wait_recv()
            for cp in sends:
                cp.wait_send()
            for cp in mine:
                cp.wait()

    seg = lambda s: pl.BlockSpec((None, S, DK), lambda h, b: (s, b, h))
    tab = pl.BlockSpec((None, CH, DK), lambda h, b: (h, 0, 0))
    rot = pl.BlockSpec((S, DK // 2), lambda h, b: (0, 0))
    blk = pl.BlockSpec((S, DK), lambda h, b: (b, h))
    one = pl.BlockSpec((None, 1, DK), lambda h, b: (h, 0, 0))
    return pl.pallas_call(
        body, name="ret_bwd", grid=(HEADS, nb),
        in_specs=[blk, blk, seg(4), seg(5), blk, blk,
                  pl.BlockSpec((None, None, NCH, DK, DK), lambda h, b: (b, h, 0, 0, 0)),
                  one, tab, tab, tab, rot, rot] + [ANY] * ns,
        out_specs=[pl.BlockSpec((4, S, DK), lambda h, b: (0, b, h)), one] + [ANY] * ns,
        out_shape=[jax.ShapeDtypeStruct((4, t, D), BF16), jax.ShapeDtypeStruct((HEADS, 1, DK), F32)]
        + [jax.ShapeDtypeStruct(a.shape, a.dtype) for a in sums],
        scratch_shapes=[pltpu.VMEM((DK, DK), F32), pltpu.SemaphoreType.DMA((ns, 3)), pltpu.SemaphoreType.DMA((ns, 3)),
                        pltpu.SemaphoreType.DMA((ns,))],
        compiler_params=_params(("arbitrary", "arbitrary")),
    )(qr, kr, proj, proj, o, dyb, rs, gain, inner_t, cross_t, state_t, cos_t, sin_t, *sums)


def _wblock(k):
    return pl.BlockSpec((NDEV, D // NDEV, D), lambda i: (0, k, 0))


def _tail(ya, yb, proj, x2d, tgt, wg, g_fin):
    t = x2d.shape[0]
    tm = 256

    def body(ya_ref, yb_ref, ma_ref, mb_ref, x_ref, t_ref, wa_ref, wb_ref, wo_ref, g_ref,
             dx2_ref, dya_ref, dyb_ref, dm_ref, mg_ref, doa_ref, dob_ref, gfin_ref, loss_ref):
        i = pl.program_id(0)

        @pl.when(i == 0)
        def _():
            gfin_ref[...] = jnp.zeros_like(gfin_ref)
            loss_ref[...] = jnp.zeros_like(loss_ref)

        wa = wa_ref[...].reshape(D, D)
        wb = wb_ref[...].reshape(D, D)
        wo = wo_ref[...].reshape(D, D)
        out_a = _dot(ya_ref[...], wa)
        out_b = _dot(yb_ref[...], wb)
        sa = _sigmoid(ma_ref[...])
        sb = _sigmoid(mb_ref[...])
        merged = (sa * out_a + sb * out_b).astype(BF16)
        mg_ref[...] = merged
        x2 = x_ref[...] + _dot(merged, wo)
        r2 = lax.rsqrt(jnp.mean(x2 * x2, axis=-1, keepdims=True) + EPS)
        xh = x2 * r2
        g = g_ref[...]
        err = xh * g - t_ref[...]
        loss_ref[...] += jnp.sum(err * err, axis=0, keepdims=True) * (0.5 / D)
        dy = err * (1.0 / D)
        gfin_ref[...] += jnp.sum(dy * xh, axis=0, keepdims=True)
        dxh = dy * g
        dx2 = r2 * (dxh - xh * jnp.mean(dxh * xh, axis=-1, keepdims=True))
        dx2_ref[...] = dx2
        dmerged = _dot_nt(dx2.astype(BF16), wo)
        doa = (sa * dmerged).astype(BF16)
        dob = (sb * dmerged).astype(BF16)
        doa_ref[...] = doa
        dob_ref[...] = dob
        dm_ref[0] = (dmerged * out_a * sa * (1.0 - sa)).astype(BF16)
        dm_ref[1] = (dmerged * out_b * sb * (1.0 - sb)).astype(BF16)
        dya_ref[...] = _dot_nt(doa, wa)
        dyb_ref[...] = _dot_nt(dob, wb)

    row = lambda: pl.BlockSpec((tm, D), lambda i: (i, 0))
    seg = lambda s: pl.BlockSpec((None, tm, D), lambda i: (s, i, 0))
    vec = pl.BlockSpec((1, D), lambda i: (0, 0))
    return pl.pallas_call(
        body, name="tail", grid=(t // tm,),
        in_specs=[row(), row(), seg(6), seg(7), row(), row(), _wblock(0), _wblock(1), _wblock(2), vec],
        out_specs=[row(), row(), row(), pl.BlockSpec((2, tm, D), lambda i: (0, i, 0)),
                   row(), row(), row(), vec, vec],
        out_shape=[jax.ShapeDtypeStruct((t, D), F32), jax.ShapeDtypeStruct((t, D), F32),
                   jax.ShapeDtypeStruct((t, D), F32), jax.ShapeDtypeStruct((2, t, D), BF16),
                   jax.ShapeDtypeStruct((t, D), BF16), jax.ShapeDtypeStruct((t, D), BF16),
                   jax.ShapeDtypeStruct((t, D), BF16), jax.ShapeDtypeStruct((1, D), F32),
                   jax.ShapeDtypeStruct((1, D), F32)],
        compiler_params=_params(("arbitrary",)),
    )(ya, yb, proj, proj, x2d, tgt, wg, wg, wg, g_fin)


def _tail_wgrad(ya, yb, merged, doa, dob, dx2):
    t = ya.shape[0]
    tm = 512

    def body(ya_ref, yb_ref, mg_ref, doa_ref, dob_ref, dx2_ref, ga_ref, gb_ref, go_ref):
        @pl.when(pl.program_id(0) == 0)
        def _():
            ga_ref[...] = jnp.zeros_like(ga_ref)
            gb_ref[...] = jnp.zeros_like(gb_ref)
            go_ref[...] = jnp.zeros_like(go_ref)

        ga_ref[...] += _dot_tn(ya_ref[...], doa_ref[...])
        gb_ref[...] += _dot_tn(yb_ref[...], dob_ref[...])
        go_ref[...] += _dot_tn(mg_ref[...], dx2_ref[...].astype(BF16))

    row = lambda: pl.BlockSpec((tm, D), lambda i: (i, 0))
    full = lambda: pl.BlockSpec((D, D), lambda i: (0, 0))
    return pl.pallas_call(
        body, name="tail_wgrad", grid=(t // tm,),
        in_specs=[row() for _ in range(6)], out_specs=[full(), full(), full()],
        out_shape=[jax.ShapeDtypeStruct((D, D), F32)] * 3,
        compiler_params=_params(("arbitrary",)),
    )(ya, yb, merged, doa, dob, dx2)


def _dproj_specs(tm, j_of, i_of):
    last = lambda j, i, lo, n: (jnp.clip(j - lo, 0, n - 1), i, 0)
    return [pl.BlockSpec((None, tm, D), lambda a, b: last(j_of(a, b), i_of(a, b), 0, 2)),
            pl.BlockSpec((None, tm, D), lambda a, b: last(j_of(a, b), i_of(a, b), 2, 4)),
            pl.BlockSpec((None, tm, D), lambda a, b: last(j_of(a, b), i_of(a, b), 6, 2))]


def _dproj_specs_ordered(tm):
    def spec(lo, n):
        def index(k, i, order_ref):
            seg = order_ref[k]
            mine = jnp.logical_and(seg >= lo, seg < lo + n)
            return jnp.where(mine, seg - lo, 0), jnp.where(mine, i, 0), 0
        return pl.BlockSpec((None, tm, D), index)
    return [spec(0, 2), spec(2, 4), spec(6, 2)]


def _dproj_pick(j, da_ref, db_ref, dc_ref, use):
    @pl.when(j < 2)
    def _():
        use(da_ref[...])

    @pl.when(jnp.logical_and(j >= 2, j < 6))
    def _():
        use(db_ref[...])

    @pl.when(j >= 6)
    def _():
        use(dc_ref[...])


def _rs_schedule(q, c):
    steps = []
    for s in range(3):
        d_a = lax.rem(q + 1 + s, 4)
        d_b = lax.rem(q + 1 + (s + 1) % 3, 4)
        steps.append((jnp.where(c == 0, d_a, d_b), jnp.where(c == 0, d_b, d_a)))
    steps.append((q, q))
    return steps


def _rs_order(q, c):
    order = []
    for keep, give in _rs_schedule(q, c):
        order += [2 * give + 1 - c, 2 * keep + c]
    return jnp.stack(order).astype(jnp.int32)


def _inproj_wgrad_rs(h, dpa, dpb, dpc, order, smalls):
    t = h.shape[0]
    tm = 1024
    nt = t // tm
    nsm = len(smalls)

    def body(order_ref, h_ref, da_ref, db_ref, dc_ref, *rest):
        small_refs, parts_ref, rest = rest[:nsm], rest[nsm], rest[nsm + 1:]
        all_refs, rest = rest[:nsm], rest[nsm:]
        (acc, sib, outb, give_send, give_recv, sum_send, sum_recv, own_sem,
         small_send, small_recv, small_own) = rest
        k, i = pl.program_id(0), pl.program_id(1)
        x, y, c = _place()
        schedule = _rs_schedule(2 * x + y, c)
        own, first, arrive, forward, others = _gather_copies(small_refs, all_refs, small_send, small_recv, small_own)

        @pl.when(jnp.logical_and(k == 0, i == 0))
        def _():
            for cp in own + first:
                cp.start()

        @pl.when(jnp.logical_and(k == 2, i == 0))
        def _():
            for came, on in zip(arrive, forward):
                came.wait_recv()
                on.start()

        def use(d):
            @pl.when(i == 0)
            def _():
                acc[k % 2] = _dot_tn(h_ref[...], d)

            @pl.when(i > 0)
            def _():
                acc[k % 2] += _dot_tn(h_ref[...], d)

        _dproj_pick(order_ref[k], da_ref, db_ref, dc_ref, use)

        def give_copy(s):
            return pltpu.make_async_remote_copy(
                src_ref=acc.at[0], dst_ref=sib.at[s % 2], send_sem=give_send.at[s], recv_sem=give_recv.at[s],
                device_id=(x, y, 1 - c), device_id_type=MESH)

        def sum_copy(s):
            keep = schedule[s][0]
            return pltpu.make_async_remote_copy(
                src_ref=outb.at[s], dst_ref=parts_ref.at[s], send_sem=sum_send.at[s], recv_sem=sum_recv.at[s],
                device_id=(keep // 2, lax.rem(keep, 2), c), device_id_type=MESH)

        own_copy = pltpu.make_async_copy(outb.at[3], parts_ref.at[3], own_sem)

        for s in range(4):
            @pl.when(jnp.logical_and(k == 2 * s, i == nt - 1))
            def _():
                give_copy(s).start()

            @pl.when(jnp.logical_and(k == 2 * s + 1, i == nt - 1))
            def _():
                give_copy(s).wait_recv()
                outb[s] = (acc[1] + sib[s % 2]).astype(BF16)
                give_copy(s).wait_send()
                if s < 3:
                    sum_copy(s).start()
                else:
                    own_copy.start()

        @pl.when(jnp.logical_and(k == NSEG - 1, i == nt - 1))
        def _():
            for s in range(3):
                sum_copy(s).wait_recv()
            for s in range(3):
                sum_copy(s).wait_send()
            own_copy.wait()
            for cp in others:
                cp.wait_recv()
            for cp in first + forward:
                cp.wait_send()
            for cp in own:
                cp.wait()

    return pl.pallas_call(
        body, name="inproj_wgrad_rs",
        grid_spec=pltpu.PrefetchScalarGridSpec(
            num_scalar_prefetch=1, grid=(NSEG, nt),
            in_specs=[pl.BlockSpec((tm, D), lambda k, i, order_ref: (i, 0))] + _dproj_specs_ordered(tm) + [ANY] * nsm,
            out_specs=[ANY] * (1 + nsm),
            scratch_shapes=[pltpu.VMEM((2, D, D), F32), pltpu.VMEM((2, D, D), F32), pltpu.VMEM((4, D, D), BF16),
                            pltpu.SemaphoreType.DMA((4,)), pltpu.SemaphoreType.DMA((4,)),
                            pltpu.SemaphoreType.DMA((3,)), pltpu.SemaphoreType.DMA((3,)),
                            pltpu.SemaphoreType.DMA,
                            pltpu.SemaphoreType.DMA((nsm, 7)), pltpu.SemaphoreType.DMA((nsm, 7)),
                            pltpu.SemaphoreType.DMA((nsm,))]),
        out_shape=[jax.ShapeDtypeStruct((4, D, D), BF16)]
        + [jax.ShapeDtypeStruct((NDEV,) + a.shape, a.dtype) for a in smalls],
        compiler_params=_params(("arbitrary", "arbitrary")),
    )(order, h, dpa, dpb, dpc, *smalls)


def _inproj_dgrad(dpa, dpb, dpc, wg, x2d, dx2, g_in):
    t = x2d.shape[0]
    tm = 512

    def body(da_ref, db_ref, dc_ref, w_ref, x_ref, dx2_ref, g_ref, gx_ref, gg_ref, acc_s):
        i, j = pl.program_id(0), pl.program_id(1)

        @pl.when(jnp.logical_and(i == 0, j == 0))
        def _():
            gg_ref[...] = jnp.zeros_like(gg_ref)

        @pl.when(j == 0)
        def _():
            acc_s[...] = jnp.zeros_like(acc_s)

        def use(d):
            acc_s[...] += _dot_nt(d, w_ref[...])

        _dproj_pick(j, da_ref, db_ref, dc_ref, use)

        @pl.when(j == NSEG - 1)
        def _():
            x = x_ref[...]
            r = lax.rsqrt(jnp.mean(x * x, axis=-1, keepdims=True) + EPS)
            xh = x * r
            dh = acc_s[...]
            gg_ref[...] += jnp.sum(dh * xh, axis=0, keepdims=True)
            dxh = dh * g_ref[...]
            gx_ref[...] = dx2_ref[...] + r * (dxh - xh * jnp.mean(dxh * xh, axis=-1, keepdims=True))

    row = lambda: pl.BlockSpec((tm, D), lambda i, j: (i, 0))
    vec = pl.BlockSpec((1, D), lambda i, j: (0, 0))
    return pl.pallas_call(
        body, name="inproj_dgrad", grid=(t // tm, NSEG),
        in_specs=_dproj_specs(tm, lambda i, j: j, lambda i, j: i)
        + [pl.BlockSpec((None, D, D), lambda i, j: (j, 0, 0)), row(), row(), vec],
        out_specs=[row(), vec],
        out_shape=[jax.ShapeDtypeStruct((t, D), F32), jax.ShapeDtypeStruct((1, D), F32)],
        scratch_shapes=[pltpu.VMEM((tm, D), F32)],
        compiler_params=_params(("arbitrary", "arbitrary")),
    )(dpa, dpb, dpc, wg, x2d, dx2, g_in)


def _adam_update(g, w, m, v):
    m_new = ADAM_B1 * m + (1.0 - ADAM_B1) * g
    v_new = ADAM_B2 * v + (1.0 - ADAM_B2) * (g * g)
    m_hat = m_new / (1.0 - ADAM_B1 ** ADAM_STEP)
    v_hat = v_new / (1.0 - ADAM_B2 ** ADAM_STEP)
    return -ADAM_LR * (m_hat / (jnp.sqrt(v_hat) + ADAM_EPS) + ADAM_WD * w), m_new, v_new


def _sum_in_order(ref):
    total = ref[0].astype(F32)
    for k in range(1, ref.shape[0]):
        total = total + ref[k].astype(F32)
    return total


def _adamw_small(me, vec_all, gx_all, ga_all, groups):
    flat = [a for grp in groups for a in grp]
    ng = len(groups)
    nshard = D // NDEV

    def body(me_ref, vec_ref, shard_ref, gx_ref, ga_ref, *refs):
        ins, outs = refs[:3 * ng], refs[3 * ng:]
        vec = _sum_in_order(vec_ref)
        shard = _sum_in_order(shard_ref)
        grads = [vec[r:r + 1, :] for r in range(6)]
        grads += [shard[0:4, :], shard[4:8, 0:DK // NDEV], _sum_in_order(gx_ref), _sum_in_order(ga_ref)]
        for n, g in enumerate(grads):
            delta, m_new, v_new = _adam_update(g, ins[3 * n][...], ins[3 * n + 1][...], ins[3 * n + 2][...])
            outs[4 * n][...] = g
            outs[4 * n + 1][...] = delta
            outs[4 * n + 2][...] = m_new
            outs[4 * n + 3][...] = v_new
        outs[4 * ng][...] = jnp.sum(vec[6:7, :], axis=1, keepdims=True)

    full = lambda a: pl.BlockSpec(a.shape, lambda i, me_ref, nd=len(a.shape): (0,) * nd)
    out_shape = [jax.ShapeDtypeStruct(w.shape, F32) for w, _, _ in groups for _ in range(4)]
    out_shape.append(jax.ShapeDtypeStruct((1, 1), F32))
    outs = pl.pallas_call(
        body, name="adamw_small",
        grid_spec=pltpu.PrefetchScalarGridSpec(
            num_scalar_prefetch=1, grid=(1,),
            in_specs=[full(vec_all),
                      pl.BlockSpec((NDEV, 8, nshard), lambda i, me_ref: (0, 1, me_ref[0])),
                      full(gx_all), full(ga_all)] + [full(a) for a in flat],
            out_specs=[full(s) for s in out_shape]),
        out_shape=out_shape,
        compiler_params=_params(("arbitrary",)),
    )(me, vec_all, vec_all, gx_all, ga_all, *flat)
    return [outs[4 * n:4 * n + 4] for n in range(ng)], outs[4 * ng]


def _adamw(name, parts, w, m, v):
    n, rows, cols = parts.shape
    tr = rows if rows <= 256 else 256

    def body(p_ref, w_ref, m_ref, v_ref, g_ref, d_ref, nm_ref, nv_ref):
        g = _sum_in_order(p_ref)
        delta, m_new, v_new = _adam_update(g, w_ref[...], m_ref[...], v_ref[...])
        g_ref[...] = g
        d_ref[...] = delta
        nm_ref[...] = m_new
        nv_ref[...] = v_new

    blk = lambda: pl.BlockSpec((tr, cols), lambda i: (i, 0))
    return pl.pallas_call(
        body, name=name, grid=(rows // tr,),
        in_specs=[pl.BlockSpec((n, tr, cols), lambda i: (0, i, 0)), blk(), blk(), blk()],
        out_specs=[blk(), blk(), blk(), blk()],
        out_shape=[jax.ShapeDtypeStruct((rows, cols), F32)] * 4,
        compiler_params=_params(("arbitrary",)),
    )(parts, w, m, v)


ANY = pl.BlockSpec(memory_space=pl.ANY)


def _place():
    return lax.axis_index("x"), lax.axis_index("y"), lax.axis_index("c")


def _gather_copies(ins, outs, send_sems, recv_sems, own_sems):
    x, y, c = _place()
    me, sibling = (x, y, c), (x, y, 1 - c)
    chips = [(1 - x, y), (x, 1 - y), (1 - x, 1 - y)]
    n = len(ins)

    def copy(a, k, block, to, src=None):
        px, py, pc = block
        dst = outs[a].at[4 * px + 2 * py + pc]
        return pltpu.make_async_remote_copy(
            src_ref=dst if src is None else src, dst_ref=dst,
            send_sem=send_sems.at[a, k], recv_sem=recv_sems.at[a, k], device_id=to, device_id_type=MESH)

    own = [pltpu.make_async_copy(ins[a], outs[a].at[4 * x + 2 * y + c], own_sems.at[a]) for a in range(n)]
    first = []
    for a in range(n):
        first.append(copy(a, 0, me, sibling, src=ins[a]))
        first += [copy(a, 1 + j, me, (*chip, c), src=ins[a]) for j, chip in enumerate(chips)]
    arrive = [copy(a, 1 + j, (*chip, c), me) for j, chip in enumerate(chips) for a in range(n)]
    forward = [copy(a, 4 + j, (*chip, c), sibling) for j, chip in enumerate(chips) for a in range(n)]
    rest = [copy(a, 0, sibling, me) for a in range(n)]
    rest += [copy(a, 4 + j, (*chip, 1 - c), me) for a in range(n) for j, chip in enumerate(chips)]
    return own, first, arrive, forward, rest


def _sibling_copies(ins, outs, send_sems, recv_sems):
    x, y, c = _place()
    return [pltpu.make_async_remote_copy(
        src_ref=ins[a].at[2 * q + 1 - c], dst_ref=outs[a].at[q],
        send_sem=send_sems.at[a, q], recv_sem=recv_sems.at[a, q],
        device_id=(x, y, 1 - c), device_id_type=MESH) for a in range(len(ins)) for q in range(4)]


def _chip_copies(ins, outs, send_sems, recv_sems, local_sems):
    x, y, c = _place()
    my_chip = 2 * x + y
    chips = [(1 - x, y), (x, 1 - y), (1 - x, 1 - y)]
    n = len(ins)
    mine = [pltpu.make_async_copy(ins[a].at[my_chip], outs[a].at[my_chip], local_sems.at[a]) for a in range(n)]
    sends = [pltpu.make_async_remote_copy(
        src_ref=ins[a].at[2 * px + py], dst_ref=outs[a].at[my_chip],
        send_sem=send_sems.at[a, j], recv_sem=recv_sems.at[a, j],
        device_id=(px, py, c), device_id_type=MESH) for a in range(n) for j, (px, py) in enumerate(chips)]
    recvs = [pltpu.make_async_remote_copy(
        src_ref=ins[a].at[my_chip], dst_ref=outs[a].at[2 * px + py],
        send_sem=send_sems.at[a, j], recv_sem=recv_sems.at[a, j],
        device_id=(px, py, c), device_id_type=MESH) for a in range(n) for j, (px, py) in enumerate(chips)]
    return mine, sends, recvs


def _chip_sum(owns, gots, core):
    n = len(owns)
    _, rows, cols = owns[0].shape

    def body(core_ref, *refs):
        for a in range(n):
            refs[2 * n + a][...] = (refs[a][...] + refs[n + a][...]).astype(BF16)

    own_spec = pl.BlockSpec((None, rows, cols), lambda q, core_ref: (2 * q + core_ref[0], 0, 0))
    slab = pl.BlockSpec((None, rows, cols), lambda q, core_ref: (q, 0, 0))
    return pl.pallas_call(
        body, name="chip_sum",
        grid_spec=pltpu.PrefetchScalarGridSpec(
            num_scalar_prefetch=1, grid=(4,),
            in_specs=[own_spec] * n + [slab] * n, out_specs=[slab] * n),
        out_shape=[jax.ShapeDtypeStruct((4, rows, cols), BF16)] * n,
        compiler_params=_params(("arbitrary",)),
    )(core, *owns, *gots)


def _block_diag(w):
    w4 = w.reshape(NCB, 4, 64, 64)
    eye = jnp.eye(4, dtype=w.dtype)
    return (w4[:, :, :, None, :] * eye[None, :, None, :, None]).reshape(NCB, CB, CB)


def _block_diag_back(g):
    g5 = g.reshape(NCB, 4, 64, 4, 64)
    return jnp.stack([g5[:, m, :, m, :] for m in range(4)], axis=1).reshape(16, 64, 64)


def kernel(x, norm_in, w_in, conv_w, conv_b, gate_x_w, gate_x_b, gate_a_w, gate_a_b, lru_lambda, gn_gain, w_proj_a, w_proj_b, w_out, norm_final, loss_target, m_norm_in, m_w_in, m_conv_w, m_conv_b, m_gate_x_w, m_gate_x_b, m_gate_a_w, m_gate_a_b, m_lru_lambda, m_gn_gain, m_w_proj_a, m_w_proj_b, m_w_out, m_norm_final, v_norm_in, v_w_in, v_conv_w, v_conv_b, v_gate_x_w, v_gate_x_b, v_gate_a_w, v_gate_a_b, v_lru_lambda, v_gn_gain, v_w_proj_a, v_w_proj_b, v_w_out, v_norm_final):
    xi, yi, ci = _place()
    me = 4 * xi + 2 * yi + ci
    core = ci.astype(jnp.int32).reshape(1)
    nshard = D // NDEV
    nb = x.shape[0]
    t = nb * S
    x2d = x.reshape(t, D)
    tgt2d = loss_target.reshape(t, D)
    g_final = norm_final.reshape(1, D)
    wbd = jnp.concatenate([_block_diag(gate_x_w[0]), _block_diag(gate_a_w[0])], axis=-1).astype(BF16)
    tables = _retention_tables()

    wp_own = jnp.concatenate([w_proj_a[0], w_proj_b[0], w_out[0]], axis=0).astype(BF16)
    tiny = jnp.concatenate([conv_w[0], jnp.pad(gn_gain[0], ((0, 0), (0, nshard - DK // NDEV)))], axis=0)
    proj, h, wg, wpg, tiny_g = _inproj_gather(x2d, norm_in, w_in[0].astype(BF16), wp_own, tiny,
                                              _gather_order(xi, yi, ci))
    conv_w_full = tiny_g[:, 0:4, :].transpose(1, 0, 2).reshape(4, D)
    gain3 = tiny_g[:, 4:8, :DK // NDEV].transpose(1, 0, 2).reshape(HEADS, 1, DK)

    ya, hs = _lru_fwd(proj, conv_w_full, conv_b, wbd, gate_x_b, gate_a_b, lru_lambda, nb)
    yb, qr, kr, o, rs = _ret_fwd(proj, gain3, tables, nb)
    dx2, dya, dyb, dpc, merged, doa, dob, g_fin, loss_vec = _tail(ya, yb, proj, x2d, tgt2d, wpg, g_final)
    g_pa, g_pb, g_out = _tail_wgrad(ya, yb, merged, doa, dob, dx2)

    own = [g.reshape(NDEV, nshard, D) for g in (g_pa, g_pb, g_out)]
    dpa, g_wbd, g_vec, *got = _lru_bwd(proj, hs, dya, conv_w_full, conv_b, wbd, gate_x_b, gate_a_b, lru_lambda, nb, own)
    sums = _chip_sum(own, got, core)
    dpb, g_gain, *parts = _ret_bwd(proj, qr, kr, o, rs, dyb, gain3, tables, nb, sums)

    grad_x, g_norm_in = _inproj_dgrad(dpa, dpb, dpc, wg, x2d, dx2, norm_in)
    grad_x = grad_x.reshape(nb, S, D)

    gain_rows = jnp.pad(g_gain.reshape(HEADS, NDEV, DK // NDEV), ((0, 0), (0, 0), (0, nshard - DK // NDEV)))
    vec = jnp.concatenate([g_norm_in, g_vec[0:4], g_fin, loss_vec, jnp.zeros((1, D), F32), g_vec[4:8],
                           gain_rows.reshape(HEADS, D)], axis=0)
    g_gx = _block_diag_back(g_wbd[:, :, :CB]).reshape(D, 64)
    g_ga = _block_diag_back(g_wbd[:, :, CB:]).reshape(D, 64)
    parts_in, vec_all, gx_all, ga_all = _inproj_wgrad_rs(h, dpa, dpb, dpc, _rs_order(2 * xi + yi, ci),
                                                         [vec, g_gx, g_ga])
    parts = [parts_in] + list(parts)

    big = [("w_in", w_in, m_w_in, v_w_in), ("w_proj_a", w_proj_a, m_w_proj_a, v_w_proj_a),
           ("w_proj_b", w_proj_b, m_w_proj_b, v_w_proj_b), ("w_out", w_out, m_w_out, v_w_out)]
    res = {}
    for k, (nm, w, m, v) in enumerate(big):
        out = _adamw("adamw_" + nm, parts[k], w[0], m[0], v[0])
        res[nm] = [o[None] for o in out]

    row = lambda a: a.reshape(1, D)
    gate = lambda a: a.reshape(D, 64)
    groups = [("norm_in", norm_in, m_norm_in, v_norm_in, row), ("conv_b", conv_b, m_conv_b, v_conv_b, row),
              ("gate_x_b", gate_x_b, m_gate_x_b, v_gate_x_b, row), ("gate_a_b", gate_a_b, m_gate_a_b, v_gate_a_b, row),
              ("lru_lambda", lru_lambda, m_lru_lambda, v_lru_lambda, row),
              ("norm_final", norm_final, m_norm_final, v_norm_final, row),
              ("conv_w", conv_w, m_conv_w, v_conv_w, lambda a: a[0]), ("gn_gain", gn_gain, m_gn_gain, v_gn_gain, lambda a: a[0]),
              ("gate_x_w", gate_x_w, m_gate_x_w, v_gate_x_w, gate), ("gate_a_w", gate_a_w, m_gate_a_w, v_gate_a_w, gate)]
    small_out, loss = _adamw_small(me.astype(jnp.int32).reshape(1), vec_all, gx_all, ga_all,
                                   [tuple(view(a) for a in (w, m, v)) for _, w, m, v, view in groups])
    for (nm, w, _, _, _), out in zip(groups, small_out):
        res[nm] = [o.reshape(w.shape) for o in out]
    loss = loss.reshape(())

    order = ["norm_in", "w_in", "conv_w", "conv_b", "gate_x_w", "gate_x_b", "gate_a_w", "gate_a_b", "lru_lambda",
             "gn_gain", "w_proj_a", "w_proj_b", "w_out", "norm_final"]
    outs = [loss, grad_x]
    for k in range(4):
        outs += [res[nm][k] for nm in order]
    return tuple(outs)
```

```python
import jax
import jax.numpy as jnp
from jax import lax
from jax.experimental import pallas as pl
from jax.experimental.pallas import tpu as pltpu

F32 = jnp.float32
BF16 = jnp.bfloat16
MESH = pl.DeviceIdType.MESH

D = 1024
S = 2048
NSEG = 8
NDEV = 8
HEADS = 4
DK = 256
CH = 256
NCH = S // CH
CB = 256
NCB = D // CB
RC = 128
SCAN_GROUP = 8
EPS = 1e-6
LRU_C = 8.0
VMEM_LIMIT = 56 * 1024 * 1024

ADAM_LR = 0.001
ADAM_B1 = 0.9
ADAM_B2 = 0.999
ADAM_EPS = 1e-08
ADAM_WD = 0.01
ADAM_STEP = 10


def _params(sem=None):
    return pltpu.CompilerParams(dimension_semantics=sem, vmem_limit_bytes=VMEM_LIMIT)


def _dot(a, b):
    return jnp.dot(a, b, preferred_element_type=F32)


def _dot_nt(a, b):
    return lax.dot_general(a, b, (((1,), (1,)), ((), ())), preferred_element_type=F32)


def _dot_tn(a, b):
    return lax.dot_general(a, b, (((0,), (0,)), ((), ())), preferred_element_type=F32)


def _sigmoid(x):
    return jax.nn.sigmoid(x)


def _expm1_nonpos(x):
    poly = x * (1.0 + x * (0.5 + x * (1.0 / 6.0 + x * (1.0 / 24.0))))
    return jnp.where(x > -0.05, poly, jnp.exp(x) - 1.0)


def _softplus(x):
    return jnp.maximum(x, 0.0) + jnp.log(1.0 + jnp.exp(-jnp.abs(x)))


def _rows(c, n):
    return pl.ds(pl.multiple_of(c * n, n), n)


def _window_before(ref, c, n):
    r0 = c * n
    prev = ref[pl.ds(pl.multiple_of(jnp.maximum(r0 - 8, 0), 8), 8), :]
    prev = jnp.where(c > 0, prev, 0.0)
    return jnp.concatenate([prev, ref[_rows(c, n), :]], axis=0)


def _shift_down(win, s, n):
    if s == 0:
        return win[8:, :]
    return pltpu.roll(win, s, 0)[8:, :]


def _shift_up(win, s, n):
    if s == 0:
        return win[:n, :]
    return pltpu.roll(win, n + 8 - s, 0)[:n, :]


def _gather_order(x, y, c):
    chips = [(1 - x, y), (x, 1 - y), (1 - x, 1 - y)]
    order = [4 * x + 2 * y + c, 4 * x + 2 * y + 1 - c]
    for px, py in chips:
        order += [4 * px + 2 * py + c, 4 * px + 2 * py + 1 - c]
    return jnp.stack(order).astype(jnp.int32)


def _inproj_gather(x2d, g_in, w_own, wp_own, tiny_own, order):
    t = x2d.shape[0]
    tm = 1024
    nt = t // tm

    def body(order_ref, x_ref, g_ref, w_own_ref, wp_own_ref, tiny_own_ref,
             proj_ref, h_ref, wg_ref, wpg_ref, tinyg_ref,
             w_all, h_all, send_sems, recv_sems, own_sems, out_sems):
        k, i = pl.program_id(0), pl.program_id(1)
        x, y, c = _place()
        me, sibling = (x, y, c), (x, y, 1 - c)
        chips = [(1 - x, y), (x, 1 - y), (1 - x, 1 - y)]
        srcs = [w_own_ref, wp_own_ref, tiny_own_ref]
        dsts = [w_all, wpg_ref, tinyg_ref]

        def copy(a, n, block, to, own_src=False):
            px, py, pc = block
            dst = dsts[a].at[4 * px + 2 * py + pc]
            return pltpu.make_async_remote_copy(
                src_ref=srcs[a] if own_src else dst, dst_ref=dst,
                send_sem=send_sems.at[a, n], recv_sem=recv_sems.at[a, n], device_id=to, device_id_type=MESH)

        def own_copy(a):
            return pltpu.make_async_copy(srcs[a], dsts[a].at[4 * x + 2 * y + c], own_sems.at[a])

        def keep_copy(n):
            return pltpu.make_async_copy(w_all.at[order_ref[n]], wg_ref.at[order_ref[n]], out_sems.at[n])

        def first_copies(a):
            return [copy(a, 0, me, sibling, True)] + [copy(a, 1 + j, me, (*chip, c), True) for j, chip in enumerate(chips)]

        def at_slot(n):
            return jnp.logical_and(k == n, i == 0)

        @pl.when(at_slot(0))
        def _():
            for a in range(3):
                own_copy(a).start()
            for a in range(3):
                for cp in first_copies(a):
                    cp.start()
            own_copy(0).wait()
            keep_copy(0).start()

        @pl.when(at_slot(1))
        def _():
            copy(0, 0, sibling, me).wait_recv()
            keep_copy(1).start()

        for j, chip in enumerate(chips):
            @pl.when(at_slot(2 + 2 * j))
            def _():
                copy(0, 1 + j, (*chip, c), me).wait_recv()
                copy(0, 4 + j, (*chip, c), sibling).start()
                keep_copy(2 + 2 * j).start()

            @pl.when(at_slot(3 + 2 * j))
            def _():
                copy(0, 4 + j, (*chip, 1 - c), me).wait_recv()
                keep_copy(3 + 2 * j).start()

        rows = pl.ds(pl.multiple_of(i * tm, tm), tm)

        @pl.when(k == 0)
        def _():
            xv = x_ref[...]
            r = lax.rsqrt(jnp.mean(xv * xv, axis=-1, keepdims=True) + EPS)
            hv = (xv * r * g_ref[...]).astype(BF16)
            h_ref[...] = hv
            h_all[rows, :] = hv

        proj_ref[...] = _dot(h_all[rows, :], w_all[order_ref[k]])

        @pl.when(jnp.logical_and(k == NSEG - 1, i == nt - 1))
        def _():
            for a in (1, 2):
                for j, chip in enumerate(chips):
                    copy(a, 1 + j, (*chip, c), me).wait_recv()
                    copy(a, 4 + j, (*chip, c), sibling).start()
            for a in (1, 2):
                copy(a, 0, sibling, me).wait_recv()
                for j, chip in enumerate(chips):
                    copy(a, 4 + j, (*chip, 1 - c), me).wait_recv()
            for a in range(3):
                for cp in first_copies(a):
                    cp.wait_send()
                for j, chip in enumerate(chips):
                    copy(a, 4 + j, (*chip, c), sibling).wait_send()
            for a in (1, 2):
                own_copy(a).wait()
            for n in range(NSEG):
                keep_copy(n).wait()

    hold = lambda k, i, order_ref: (jnp.where(k == 0, i, nt - 1), 0)
    return pl.pallas_call(
        body, name="inproj_gather",
        grid_spec=pltpu.PrefetchScalarGridSpec(
            num_scalar_prefetch=1, grid=(NSEG, nt),
            in_specs=[pl.BlockSpec((tm, D), hold),
                      pl.BlockSpec((1, D), lambda k, i, order_ref: (0, 0)),
                      ANY, ANY, ANY],
            out_specs=[pl.BlockSpec((None, tm, D), lambda k, i, order_ref: (order_ref[k], i, 0)),
                       pl.BlockSpec((tm, D), hold),
                       ANY, ANY, ANY],
            scratch_shapes=[pltpu.VMEM((NDEV, D, D), BF16), pltpu.VMEM((t, D), BF16),
                            pltpu.SemaphoreType.DMA((3, 7)), pltpu.SemaphoreType.DMA((3, 7)),
                            pltpu.SemaphoreType.DMA((3,)), pltpu.SemaphoreType.DMA((NSEG,))]),
        out_shape=[jax.ShapeDtypeStruct((NSEG, t, D), F32), jax.ShapeDtypeStruct((t, D), BF16),
                   jax.ShapeDtypeStruct((NDEV,) + w_own.shape, BF16),
                   jax.ShapeDtypeStruct((NDEV,) + wp_own.shape, BF16),
                   jax.ShapeDtypeStruct((NDEV,) + tiny_own.shape, F32)],
        compiler_params=_params(("arbitrary", "arbitrary")),
    )(order, x2d, g_in, w_own, wp_own, tiny_own)


def _tile_scan(a, u):
    row = lax.broadcasted_iota(jnp.int32, a.shape, 0)
    for d in (1, 2, 4):
        m = row >= d
        a_sh = pltpu.roll(a, d, 0)
        u_sh = pltpu.roll(u, d, 0)
        u = jnp.where(m, a * u_sh + u, u)
        a = jnp.where(m, a * a_sh, a)
    return a, u


def _tile_scan_rev(a, w):
    row = lax.broadcasted_iota(jnp.int32, a.shape, 0)
    for d in (1, 2, 4):
        m = row < 8 - d
        a_sh = pltpu.roll(a, 8 - d, 0)
        w_sh = pltpu.roll(w, 8 - d, 0)
        w = jnp.where(m, a * w_sh + w, w)
        a = jnp.where(m, a * a_sh, a)
    return a, w


def _lru_gates(xa_ref, c, cw_ref, cb_ref, wbd_ref, bx_ref, ba_ref, sp):
    win = _window_before(xa_ref, c, RC)
    xc = cb_ref[...] + cw_ref[3:4, :] * _shift_down(win, 0, RC)
    for s in (1, 2, 3):
        xc = xc + cw_ref[3 - s:4 - s, :] * _shift_down(win, s, RC)
    z = _dot(xc.astype(BF16), wbd_ref[...])
    gi = _sigmoid(z[:, :CB] + bx_ref[...])
    gr = _sigmoid(z[:, CB:] + ba_ref[...])
    log_a = -LRU_C * gr * sp
    return win, xc, gi, gr, log_a


def _lru_fwd(proj, conv_w, conv_b, wbd, bx, ba, lam, nb):
    t = nb * S

    def body(xa_ref, ga_ref, cw_ref, cb_ref, wbd_ref, bx_ref, ba_ref, lam_ref, ya_ref, hs_ref, a_s, u_s):
        sp = _softplus(-lam_ref[...])

        def gates(c, carry):
            _, xc, gi, _, log_a = _lru_gates(xa_ref, c, cw_ref, cb_ref, wbd_ref, bx_ref, ba_ref, sp)
            a_s[_rows(c, RC), :] = jnp.exp(log_a)
            u_s[_rows(c, RC), :] = jnp.sqrt(-_expm1_nonpos(2.0 * log_a)) * (gi * xc)
            return carry

        lax.fori_loop(0, S // RC, gates, 0)

        def scan(g, h):
            for k in range(SCAN_GROUP):
                rows = pl.ds(pl.multiple_of(g * (8 * SCAN_GROUP), 8 * SCAN_GROUP) + 8 * k, 8)
                a_cum, u_cum = _tile_scan(a_s[rows, :], u_s[rows, :])
                hs_ref[rows, :] = u_cum + a_cum * h
                h = u_cum[7:8, :] + a_cum[7:8, :] * h
            return h

        lax.fori_loop(0, S // (8 * SCAN_GROUP), scan, jnp.zeros((1, CB), F32))

        def gate_out(c, carry):
            ga = ga_ref[_rows(c, RC), :]
            ya_ref[_rows(c, RC), :] = (ga * _sigmoid(ga) * hs_ref[_rows(c, RC), :]).astype(BF16)
            return carry

        lax.fori_loop(0, S // RC, gate_out, 0)

    vec = pl.BlockSpec((1, CB), lambda b, cb: (0, cb))
    return pl.pallas_call(
        body, name="lru_fwd", grid=(nb, NCB),
        in_specs=[pl.BlockSpec((None, S, CB), lambda b, cb: (0, b, cb)),
                  pl.BlockSpec((None, S, CB), lambda b, cb: (1, b, cb)),
                  pl.BlockSpec((4, CB), lambda b, cb: (0, cb)),
                  vec,
                  pl.BlockSpec((None, CB, 2 * CB), lambda b, cb: (cb, 0, 0)),
                  vec, vec, vec],
        out_specs=[pl.BlockSpec((S, CB), lambda b, cb: (b, cb)),
                   pl.BlockSpec((S, CB), lambda b, cb: (b, cb))],
        out_shape=[jax.ShapeDtypeStruct((t, D), BF16), jax.ShapeDtypeStruct((t, D), F32)],
        scratch_shapes=[pltpu.VMEM((S, CB), F32), pltpu.VMEM((S, CB), F32)],
        compiler_params=_params(("arbitrary", "arbitrary")),
    )(proj, proj, conv_w, conv_b, wbd, bx, ba, lam)


def _lru_bwd(proj, hs, dya, conv_w, conv_b, wbd, bx, ba, lam, nb, give):
    t = nb * S
    ng = len(give)

    def body(xa_ref, ga_ref, hs_ref, dya_ref, cw_ref, cb_ref, wbd_ref, bx_ref, ba_ref, lam_ref, *rest):
        give_refs, rest = rest[:ng], rest[ng:]
        dp_ref, dwbd_ref, vec_ref = rest[:3]
        got_refs, rest = rest[3:3 + ng], rest[3 + ng:]
        a_s, xc_s, gi_s, gr_s, dl_s, dh_s, dxc_s, acc_s, send_sems, recv_sems = rest
        b = pl.program_id(1)
        exchange = _sibling_copies(give_refs, got_refs, send_sems, recv_sems)

        @pl.when(jnp.logical_and(pl.program_id(0) == 0, b == 0))
        def _():
            for cp in exchange:
                cp.start()

        lam_v = lam_ref[...]
        sp = _softplus(-lam_v)
        acc_s[...] = jnp.zeros_like(acc_s)

        @pl.when(b == 0)
        def _():
            dwbd_ref[...] = jnp.zeros_like(dwbd_ref)
            vec_ref[...] = jnp.zeros_like(vec_ref)

        def gates(c, carry):
            _, xc, gi, gr, log_a = _lru_gates(xa_ref, c, cw_ref, cb_ref, wbd_ref, bx_ref, ba_ref, sp)
            rows = _rows(c, RC)
            a_s[rows, :] = jnp.exp(log_a)
            xc_s[rows, :] = xc
            gi_s[rows, :] = gi
            gr_s[rows, :] = gr
            ga = ga_ref[rows, :]
            sg = _sigmoid(ga)
            dya_c = dya_ref[rows, :]
            dl_s[rows, :] = dya_c * (ga * sg)
            dp_ref[1, rows, :] = (dya_c * hs_ref[rows, :] * (sg * (1.0 + ga * (1.0 - sg)))).astype(BF16)
            return carry

        lax.fori_loop(0, S // RC, gates, 0)

        def scan(i, g_in):
            base = pl.multiple_of((S // (8 * SCAN_GROUP) - 1 - i) * (8 * SCAN_GROUP), 8 * SCAN_GROUP)
            row = lax.broadcasted_iota(jnp.int32, (8, CB), 0)
            for k in reversed(range(SCAN_GROUP)):
                rows = pl.ds(base + 8 * k, 8)
                a = a_s[rows, :]
                dl = dl_s[rows, :]
                a_cum, g_loc = _tile_scan_rev(a, a * dl)
                g = g_loc + a_cum * g_in
                dh_s[rows, :] = dl + jnp.where(row < 7, pltpu.roll(g, 7, 0), g_in)
                g_in = g_loc[0:1, :] + a_cum[0:1, :] * g_in
            return g_in

        lax.fori_loop(0, S // (8 * SCAN_GROUP), scan, jnp.zeros((1, CB), F32))

        dxc_s[pl.ds(S, 8), :] = jnp.zeros((8, CB), F32)

        def grads(c, carry):
            rows = _rows(c, RC)
            dh = dh_s[rows, :]
            h_prev = _shift_down(_window_before(hs_ref, c, RC), 1, RC)
            xc, gi, gr, a = xc_s[rows, :], gi_s[rows, :], gr_s[rows, :], a_s[rows, :]
            mult = jnp.sqrt(-_expm1_nonpos(-2.0 * LRU_C * gr * sp))
            dmult = dh * gi * xc
            d_log_a = dh * h_prev * a - dmult * (a * a) / mult
            dzi = dh * mult * xc * gi * (1.0 - gi)
            dzr = d_log_a * (-LRU_C * sp) * gr * (1.0 - gr)
            dz = jnp.concatenate([dzi, dzr], axis=1).astype(BF16)
            dxc_s[rows, :] = dh * mult * gi + _dot_nt(dz, wbd_ref[...])
            dwbd_ref[...] += _dot_tn(xc.astype(BF16), dz)
            acc_s[1:2, :] += jnp.sum(dzi, axis=0, keepdims=True)
            acc_s[2:3, :] += jnp.sum(dzr, axis=0, keepdims=True)
            acc_s[3:4, :] += jnp.sum(d_log_a * (-LRU_C * gr), axis=0, keepdims=True)
            return carry

        lax.fori_loop(0, S // RC, grads, 0)

        def conv_bwd(c, carry):
            rows = _rows(c, RC)
            dwin = dxc_s[pl.ds(pl.multiple_of(c * RC, RC), RC + 8), :]
            dxc = dwin[:RC, :]
            xwin = _window_before(xa_ref, c, RC)
            dxa = cw_ref[3:4, :] * dxc
            acc_s[0:1, :] += jnp.sum(dxc, axis=0, keepdims=True)
            acc_s[7:8, :] += jnp.sum(dxc * _shift_down(xwin, 0, RC), axis=0, keepdims=True)
            for s in (1, 2, 3):
                dxa = dxa + cw_ref[3 - s:4 - s, :] * _shift_up(dwin, s, RC)
                acc_s[7 - s:8 - s, :] += jnp.sum(dxc * _shift_down(xwin, s, RC), axis=0, keepdims=True)
            dp_ref[0, rows, :] = dxa.astype(BF16)
            return carry

        lax.fori_loop(0, S // RC, conv_bwd, 0)

        row = lax.broadcasted_iota(jnp.int32, acc_s.shape, 0)
        vec_ref[...] += jnp.where(row == 3, acc_s[...] * (-_sigmoid(-lam_v)), acc_s[...])

        @pl.when(jnp.logical_and(pl.program_id(0) == NCB - 1, b == nb - 1))
        def _():
            for cp in exchange:
                cp.wait()

    vec = pl.BlockSpec((1, CB), lambda cb, b: (0, cb))
    blk = pl.BlockSpec((S, CB), lambda cb, b: (b, cb))
    return pl.pallas_call(
        body, name="lru_bwd", grid=(NCB, nb),
        in_specs=[pl.BlockSpec((None, S, CB), lambda cb, b: (0, b, cb)),
                  pl.BlockSpec((None, S, CB), lambda cb, b: (1, b, cb)),
                  blk, blk,
                  pl.BlockSpec((4, CB), lambda cb, b: (0, cb)),
                  vec,
                  pl.BlockSpec((None, CB, 2 * CB), lambda cb, b: (cb, 0, 0)),
                  vec, vec, vec] + [ANY] * ng,
        out_specs=[pl.BlockSpec((2, S, CB), lambda cb, b: (0, b, cb)),
                   pl.BlockSpec((None, CB, 2 * CB), lambda cb, b: (cb, 0, 0)),
                   pl.BlockSpec((8, CB), lambda cb, b: (0, cb))] + [ANY] * ng,
        out_shape=[jax.ShapeDtypeStruct((2, t, D), BF16),
                   jax.ShapeDtypeStruct((NCB, CB, 2 * CB), F32),
                   jax.ShapeDtypeStruct((8, D), F32)]
        + [jax.ShapeDtypeStruct((4,) + g.shape[1:], g.dtype) for g in give],
        scratch_shapes=[pltpu.VMEM((S, CB), F32), pltpu.VMEM((S, CB), F32), pltpu.VMEM((S, CB), F32),
                        pltpu.VMEM((S, CB), F32), pltpu.VMEM((S, CB), F32), pltpu.VMEM((S, CB), F32),
                        pltpu.VMEM((S + 8, CB), F32), pltpu.VMEM((8, CB), F32),
                        pltpu.SemaphoreType.DMA((ng, 4)), pltpu.SemaphoreType.DMA((ng, 4))],
        compiler_params=_params(("arbitrary", "arbitrary")),
    )(proj, proj, hs, dya, conv_w, conv_b, wbd, bx, ba, lam, *give)


def _retention_tables():
    log_g = jnp.log1p(-(2.0 ** (-5.0 - jnp.arange(HEADS, dtype=F32))))
    idx = jnp.arange(CH, dtype=F32)
    diff = idx[:, None] - idx[None, :]
    inner = jnp.where(diff >= 0, jnp.exp(jnp.maximum(diff, 0.0)[None] * log_g[:, None, None]), 0.0)
    cross = jnp.exp((idx[None, :] + 1.0) * log_g[:, None])
    state = jnp.exp((CH - 1.0 - idx[None, :]) * log_g[:, None])
    cross = jnp.broadcast_to(cross[:, :, None], (HEADS, CH, DK))
    state = jnp.broadcast_to(state[:, :, None], (HEADS, CH, DK))
    half = DK // 2
    freqs = 10000.0 ** (-jnp.arange(half, dtype=F32) / half)
    ang = jnp.arange(S, dtype=F32)[:, None] * freqs[None, :]
    return inner, cross, state, jnp.cos(ang), jnp.sin(ang)


def _rotate(x, cos, sin):
    half = DK // 2
    x1, x2 = x[:, :half], x[:, half:]
    return jnp.concatenate([x1 * cos - x2 * sin, x1 * sin + x2 * cos], axis=1)


def _rotate_back(d, cos, sin):
    half = DK // 2
    d1, d2 = d[:, :half], d[:, half:]
    return jnp.concatenate([d1 * cos + d2 * sin, d2 * cos - d1 * sin], axis=1)


def _ret_fwd(proj, gain, tables, nb):
    t = nb * S
    inner_t, cross_t, state_t, cos_t, sin_t = tables

    def body(q_ref, k_ref, v_ref, gb_ref, gain_ref, dm_ref, cd_ref, sd_ref, cos_ref, sin_ref,
             yb_ref, qr_ref, kr_ref, o_ref, rs_ref, r_s):
        r_s[...] = jnp.zeros_like(r_s)
        chunk_decay = cd_ref[CH - 1:CH, :]

        def chunk(c, carry):
            rows = _rows(c, CH)
            cos, sin = cos_ref[rows, :], sin_ref[rows, :]
            qr = _rotate(q_ref[rows, :], cos, sin).astype(BF16)
            kr = (_rotate(k_ref[rows, :], cos, sin) * (DK ** -0.5)).astype(BF16)
            v = v_ref[rows, :]
            qr_ref[rows, :] = qr
            kr_ref[rows, :] = kr
            r = r_s[...]
            rb = r.astype(BF16)
            rs_ref[c] = rb
            p = (_dot_nt(qr, kr) * dm_ref[...]).astype(BF16)
            o = _dot(p, v.astype(BF16)) + _dot(qr, rb) * cd_ref[...]
            r_s[...] = chunk_decay * r + _dot_tn(kr, (v * sd_ref[...]).astype(BF16))
            o_ref[rows, :] = o
            oc = o - jnp.mean(o, axis=-1, keepdims=True)
            rstd = lax.rsqrt(jnp.mean(oc * oc, axis=-1, keepdims=True) + EPS)
            gb = gb_ref[rows, :]
            yb_ref[rows, :] = (gb * _sigmoid(gb) * (oc * rstd * gain_ref[...])).astype(BF16)
            return carry

        lax.fori_loop(0, NCH, chunk, 0)

    seg = lambda s: pl.BlockSpec((None, S, DK), lambda b, h: (s, b, h))
    tab = pl.BlockSpec((None, CH, DK), lambda b, h: (h, 0, 0))
    rot = pl.BlockSpec((S, DK // 2), lambda b, h: (0, 0))
    blk = pl.BlockSpec((S, DK), lambda b, h: (b, h))
    return pl.pallas_call(
        body, name="ret_fwd", grid=(nb, HEADS),
        in_specs=[seg(2), seg(3), seg(4), seg(5),
                  pl.BlockSpec((None, 1, DK), lambda b, h: (h, 0, 0)),
                  tab, tab, tab, rot, rot],
        out_specs=[blk, blk, blk, blk,
                   pl.BlockSpec((None, None, NCH, DK, DK), lambda b, h: (b, h, 0, 0, 0))],
        out_shape=[jax.ShapeDtypeStruct((t, D), BF16), jax.ShapeDtypeStruct((t, D), BF16),
                   jax.ShapeDtypeStruct((t, D), BF16), jax.ShapeDtypeStruct((t, D), F32),
                   jax.ShapeDtypeStruct((nb, HEADS, NCH, DK, DK), BF16)],
        scratch_shapes=[pltpu.VMEM((DK, DK), F32)],
        compiler_params=_params(("arbitrary", "arbitrary")),
    )(proj, proj, proj, proj, gain, inner_t, cross_t, state_t, cos_t, sin_t)


def _ret_bwd(proj, qr, kr, o, rs, dyb, gain, tables, nb, sums):
    t = nb * S
    ns = len(sums)
    inner_t, cross_t, state_t, cos_t, sin_t = tables

    def body(qr_ref, kr_ref, v_ref, gb_ref, o_ref, dyb_ref, rs_ref, gain_ref, dm_ref, cd_ref, sd_ref,
             cos_ref, sin_ref, *rest):
        sum_refs, rest = rest[:ns], rest[ns:]
        dp_ref, dgain_ref = rest[:2]
        part_refs, rest = rest[2:2 + ns], rest[2 + ns:]
        dr_s, send_sems, recv_sems, local_sems = rest
        mine, sends, recvs = _chip_copies(sum_refs, part_refs, send_sems, recv_sems, local_sems)

        @pl.when(jnp.logical_and(pl.program_id(0) == 0, pl.program_id(1) == 0))
        def _():
            for cp in mine + sends:
                cp.start()

        dr_s[...] = jnp.zeros_like(dr_s)
        chunk_decay = cd_ref[CH - 1:CH, :]

        @pl.when(pl.program_id(1) == 0)
        def _():
            dgain_ref[...] = jnp.zeros_like(dgain_ref)

        def chunk(i, carry):
            c = NCH - 1 - i
            rows = _rows(c, CH)
            gain_v = gain_ref[...]
            o_c = o_ref[rows, :]
            oc = o_c - jnp.mean(o_c, axis=-1, keepdims=True)
            rstd = lax.rsqrt(jnp.mean(oc * oc, axis=-1, keepdims=True) + EPS)
            yn = oc * rstd
            gb = gb_ref[rows, :]
            sg = _sigmoid(gb)
            dyb_c = dyb_ref[rows, :]
            dgn = dyb_c * (gb * sg)
            dp_ref[3, rows, :] = (dyb_c * (yn * gain_v) * (sg * (1.0 + gb * (1.0 - sg)))).astype(BF16)
            dgain_ref[...] += jnp.sum(dgn * yn, axis=0, keepdims=True)
            dyn = dgn * gain_v
            do = rstd * (dyn - jnp.mean(dyn, axis=-1, keepdims=True)
                         - yn * jnp.mean(dyn * yn, axis=-1, keepdims=True))
            dob = do.astype(BF16)
            dox = (do * cd_ref[...]).astype(BF16)

            q_c, k_c = qr_ref[rows, :], kr_ref[rows, :]
            v = v_ref[rows, :]
            vb = v.astype(BF16)
            vs = (v * sd_ref[...]).astype(BF16)
            rb = rs_ref[c]
            d_r = dr_s[...]
            drb = d_r.astype(BF16)
            dm = dm_ref[...]
            p = (_dot_nt(q_c, k_c) * dm).astype(BF16)
            dpm = (_dot_nt(dob, vb) * dm).astype(BF16)
            dq = _dot(dpm, k_c) + _dot_nt(dox, rb)
            dk = _dot_tn(dpm, q_c) + _dot_nt(vs, drb)
            dv = _dot_tn(p, dob) + _dot(k_c, drb) * sd_ref[...]
            dr_s[...] = chunk_decay * d_r + _dot_tn(q_c, dox)

            cos, sin = cos_ref[rows, :], sin_ref[rows, :]
            dp_ref[0, rows, :] = _rotate_back(dq, cos, sin).astype(BF16)
            dp_ref[1, rows, :] = (_rotate_back(dk, cos, sin) * (DK ** -0.5)).astype(BF16)
            dp_ref[2, rows, :] = dv.astype(BF16)
            return carry

        lax.fori_loop(0, NCH, chunk, 0)

        @pl.when(jnp.logical_and(pl.program_id(0) == HEADS - 1, pl.program_id(1) == nb - 1))
        def _():
            for cp in recvs:
                cp.wait_recv()
            for cp in sends:
                cp.wait_send()
            for cp in mine:
                cp.wait()

    seg = lambda s: pl.BlockSpec((None, S, DK), lambda h, b: (s, b, h))
    tab = pl.BlockSpec((None, CH, DK), lambda h, b: (h, 0, 0))
    rot = pl.BlockSpec((S, DK // 2), lambda h, b: (0, 0))
    blk = pl.BlockSpec((S, DK), lambda h, b: (b, h))
    one = pl.BlockSpec((None, 1, DK), lambda h, b: (h, 0, 0))
    return pl.pallas_call(
        body, name="ret_bwd", grid=(HEADS, nb),
        in_specs=[blk, blk, seg(4), seg(5), blk, blk,
                  pl.BlockSpec((None, None, NCH, DK, DK), lambda h, b: (b, h, 0, 0, 0)),
                  one, tab, tab, tab, rot, rot] + [ANY] * ns,
        out_specs=[pl.BlockSpec((4, S, DK), lambda h, b: (0, b, h)), one] + [ANY] * ns,
        out_shape=[jax.ShapeDtypeStruct((4, t, D), BF16), jax.ShapeDtypeStruct((HEADS, 1, DK), F32)]
        + [jax.ShapeDtypeStruct(a.shape, a.dtype) for a in sums],
        scratch_shapes=[pltpu.VMEM((DK, DK), F32), pltpu.SemaphoreType.DMA((ns, 3)), pltpu.SemaphoreType.DMA((ns, 3)),
                        pltpu.SemaphoreType.DMA((ns,))],
        compiler_params=_params(("arbitrary", "arbitrary")),
    )(qr, kr, proj, proj, o, dyb, rs, gain, inner_t, cross_t, state_t, cos_t, sin_t, *sums)


def _wblock(k):
    return pl.BlockSpec((NDEV, D // NDEV, D), lambda i: (0, k, 0))


def _tail(ya, yb, proj, x2d, tgt, wg, g_fin):
    t = x2d.shape[0]
    tm = 256

    def body(ya_ref, yb_ref, ma_ref, mb_ref, x_ref, t_ref, wa_ref, wb_ref, wo_ref, g_ref,
             dx2_ref, dya_ref, dyb_ref, dm_ref, mg_ref, doa_ref, dob_ref, gfin_ref, loss_ref):
        i = pl.program_id(0)

        @pl.when(i == 0)
        def _():
            gfin_ref[...] = jnp.zeros_like(gfin_ref)
            loss_ref[...] = jnp.zeros_like(loss_ref)

        wa = wa_ref[...].reshape(D, D)
        wb = wb_ref[...].reshape(D, D)
        wo = wo_ref[...].reshape(D, D)
        out_a = _dot(ya_ref[...], wa)
        out_b = _dot(yb_ref[...], wb)
        sa = _sigmoid(ma_ref[...])
        sb = _sigmoid(mb_ref[...])
        merged = (sa * out_a + sb * out_b).astype(BF16)
        mg_ref[...] = merged
        x2 = x_ref[...] + _dot(merged, wo)
        r2 = lax.rsqrt(jnp.mean(x2 * x2, axis=-1, keepdims=True) + EPS)
        xh = x2 * r2
        g = g_ref[...]
        err = xh * g - t_ref[...]
        loss_ref[...] += jnp.sum(err * err, axis=0, keepdims=True) * (0.5 / D)
        dy = err * (1.0 / D)
        gfin_ref[...] += jnp.sum(dy * xh, axis=0, keepdims=True)
        dxh = dy * g
        dx2 = r2 * (dxh - xh * jnp.mean(dxh * xh, axis=-1, keepdims=True))
        dx2_ref[...] = dx2
        dmerged = _dot_nt(dx2.astype(BF16), wo)
        doa = (sa * dmerged).astype(BF16)
        dob = (sb * dmerged).astype(BF16)
        doa_ref[...] = doa
        dob_ref[...] = dob
        dm_ref[0] = (dmerged * out_a * sa * (1.0 - sa)).astype(BF16)
        dm_ref[1] = (dmerged * out_b * sb * (1.0 - sb)).astype(BF16)
        dya_ref[...] = _dot_nt(doa, wa)
        dyb_ref[...] = _dot_nt(dob, wb)

    row = lambda: pl.BlockSpec((tm, D), lambda i: (i, 0))
    seg = lambda s: pl.BlockSpec((None, tm, D), lambda i: (s, i, 0))
    vec = pl.BlockSpec((1, D), lambda i: (0, 0))
    return pl.pallas_call(
        body, name="tail", grid=(t // tm,),
        in_specs=[row(), row(), seg(6), seg(7), row(), row(), _wblock(0), _wblock(1), _wblock(2), vec],
        out_specs=[row(), row(), row(), pl.BlockSpec((2, tm, D), lambda i: (0, i, 0)),
                   row(), row(), row(), vec, vec],
        out_shape=[jax.ShapeDtypeStruct((t, D), F32), jax.ShapeDtypeStruct((t, D), F32),
                   jax.ShapeDtypeStruct((t, D), F32), jax.ShapeDtypeStruct((2, t, D), BF16),
                   jax.ShapeDtypeStruct((t, D), BF16), jax.ShapeDtypeStruct((t, D), BF16),
                   jax.ShapeDtypeStruct((t, D), BF16), jax.ShapeDtypeStruct((1, D), F32),
                   jax.ShapeDtypeStruct((1, D), F32)],
        compiler_params=_params(("arbitrary",)),
    )(ya, yb, proj, proj, x2d, tgt, wg, wg, wg, g_fin)


def _tail_wgrad(ya, yb, merged, doa, dob, dx2):
    t = ya.shape[0]
    tm = 512

    def body(ya_ref, yb_ref, mg_ref, doa_ref, dob_ref, dx2_ref, ga_ref, gb_ref, go_ref):
        @pl.when(pl.program_id(0) == 0)
        def _():
            ga_ref[...] = jnp.zeros_like(ga_ref)
            gb_ref[...] = jnp.zeros_like(gb_ref)
            go_ref[...] = jnp.zeros_like(go_ref)

        ga_ref[...] += _dot_tn(ya_ref[...], doa_ref[...])
        gb_ref[...] += _dot_tn(yb_ref[...], dob_ref[...])
        go_ref[...] += _dot_tn(mg_ref[...], dx2_ref[...].astype(BF16))

    row = lambda: pl.BlockSpec((tm, D), lambda i: (i, 0))
    full = lambda: pl.BlockSpec((D, D), lambda i: (0, 0))
    return pl.pallas_call(
        body, name="tail_wgrad", grid=(t // tm,),
        in_specs=[row() for _ in range(6)], out_specs=[full(), full(), full()],
        out_shape=[jax.ShapeDtypeStruct((D, D), F32)] * 3,
        compiler_params=_params(("arbitrary",)),
    )(ya, yb, merged, doa, dob, dx2)


def _dproj_specs(tm, j_of, i_of):
    last = lambda j, i, lo, n: (jnp.clip(j - lo, 0, n - 1), i, 0)
    return [pl.BlockSpec((None, tm, D), lambda a, b: last(j_of(a, b), i_of(a, b), 0, 2)),
            pl.BlockSpec((None, tm, D), lambda a, b: last(j_of(a, b), i_of(a, b), 2, 4)),
            pl.BlockSpec((None, tm, D), lambda a, b: last(j_of(a, b), i_of(a, b), 6, 2))]


def _dproj_specs_ordered(tm):
    def spec(lo, n):
        def index(k, i, order_ref):
            seg = order_ref[k]
            mine = jnp.logical_and(seg >= lo, seg < lo + n)
            return jnp.where(mine, seg - lo, 0), jnp.where(mine, i, 0), 0
        return pl.BlockSpec((None, tm, D), index)
    return [spec(0, 2), spec(2, 4), spec(6, 2)]


def _dproj_pick(j, da_ref, db_ref, dc_ref, use):
    @pl.when(j < 2)
    def _():
        use(da_ref[...])

    @pl.when(jnp.logical_and(j >= 2, j < 6))
    def _():
        use(db_ref[...])

    @pl.when(j >= 6)
    def _():
        use(dc_ref[...])


def _rs_schedule(q, c):
    steps = []
    for s in range(3):
        d_a = lax.rem(q + 1 + s, 4)
        d_b = lax.rem(q + 1 + (s + 1) % 3, 4)
        steps.append((jnp.where(c == 0, d_a, d_b), jnp.where(c == 0, d_b, d_a)))
    steps.append((q, q))
    return steps


def _rs_order(q, c):
    order = []
    for keep, give in _rs_schedule(q, c):
        order += [2 * give + 1 - c, 2 * keep + c]
    return jnp.stack(order).astype(jnp.int32)


def _inproj_wgrad_rs(h, dpa, dpb, dpc, order, smalls):
    t = h.shape[0]
    tm = 1024
    nt = t // tm
    nsm = len(smalls)

    def body(order_ref, h_ref, da_ref, db_ref, dc_ref, *rest):
        small_refs, parts_ref, rest = rest[:nsm], rest[nsm], rest[nsm + 1:]
        all_refs, rest = rest[:nsm], rest[nsm:]
        (acc, sib, outb, give_send, give_recv, sum_send, sum_recv, own_sem,
         small_send, small_recv, small_own) = rest
        k, i = pl.program_id(0), pl.program_id(1)
        x, y, c = _place()
        schedule = _rs_schedule(2 * x + y, c)
        own, first, arrive, forward, others = _gather_copies(small_refs, all_refs, small_send, small_recv, small_own)

        @pl.when(jnp.logical_and(k == 0, i == 0))
        def _():
            for cp in own + first:
                cp.start()

        @pl.when(jnp.logical_and(k == 2, i == 0))
        def _():
            for came, on in zip(arrive, forward):
                came.wait_recv()
                on.start()

        def use(d):
            @pl.when(i == 0)
            def _():
                acc[k % 2] = _dot_tn(h_ref[...], d)

            @pl.when(i > 0)
            def _():
                acc[k % 2] += _dot_tn(h_ref[...], d)

        _dproj_pick(order_ref[k], da_ref, db_ref, dc_ref, use)

        def give_copy(s):
            return pltpu.make_async_remote_copy(
                src_ref=acc.at[0], dst_ref=sib.at[s % 2], send_sem=give_send.at[s], recv_sem=give_recv.at[s],
                device_id=(x, y, 1 - c), device_id_type=MESH)

        def sum_copy(s):
            keep = schedule[s][0]
            return pltpu.make_async_remote_copy(
                src_ref=outb.at[s], dst_ref=parts_ref.at[s], send_sem=sum_send.at[s], recv_sem=sum_recv.at[s],
                device_id=(keep // 2, lax.rem(keep, 2), c), device_id_type=MESH)

        own_copy = pltpu.make_async_copy(outb.at[3], parts_ref.at[3], own_sem)

        for s in range(4):
            @pl.when(jnp.logical_and(k == 2 * s, i == nt - 1))
            def _():
                give_copy(s).start()

            @pl.when(jnp.logical_and(k == 2 * s + 1, i == nt - 1))
            def _():
                give_copy(s).wait_recv()
                outb[s] = (acc[1] + sib[s % 2]).astype(BF16)
                give_copy(s).wait_send()
                if s < 3:
                    sum_copy(s).start()
                else:
                    own_copy.start()

        @pl.when(jnp.logical_and(k == NSEG - 1, i == nt - 1))
        def _():
            for s in range(3):
                sum_copy(s).wait_recv()
            for s in range(3):
                sum_copy(s).wait_send()
            own_copy.wait()
            for cp in others:
                cp.wait_recv()
            for cp in first + forward:
                cp.wait_send()
            for cp in own:
                cp.wait()

    return pl.pallas_call(
        body, name="inproj_wgrad_rs",
        grid_spec=pltpu.PrefetchScalarGridSpec(
            num_scalar_prefetch=1, grid=(NSEG, nt),
            in_specs=[pl.BlockSpec((tm, D), lambda k, i, order_ref: (i, 0))] + _dproj_specs_ordered(tm) + [ANY] * nsm,
            out_specs=[ANY] * (1 + nsm),
            scratch_shapes=[pltpu.VMEM((2, D, D), F32), pltpu.VMEM((2, D, D), F32), pltpu.VMEM((4, D, D), BF16),
                            pltpu.SemaphoreType.DMA((4,)), pltpu.SemaphoreType.DMA((4,)),
                            pltpu.SemaphoreType.DMA((3,)), pltpu.SemaphoreType.DMA((3,)),
                            pltpu.SemaphoreType.DMA,
                            pltpu.SemaphoreType.DMA((nsm, 7)), pltpu.SemaphoreType.DMA((nsm, 7)),
                            pltpu.SemaphoreType.DMA((nsm,))]),
        out_shape=[jax.ShapeDtypeStruct((4, D, D), BF16)]
        + [jax.ShapeDtypeStruct((NDEV,) + a.shape, a.dtype) for a in smalls],
        compiler_params=_params(("arbitrary", "arbitrary")),
    )(order, h, dpa, dpb, dpc, *smalls)


def _inproj_dgrad(dpa, dpb, dpc, wg, x2d, dx2, g_in):
    t = x2d.shape[0]
    tm = 1024

    def body(da_ref, db_ref, dc_ref, w_ref, x_ref, dx2_ref, g_ref, gx_ref, gg_ref, acc_s):
        i, j = pl.program_id(0), pl.program_id(1)

        @pl.when(jnp.logical_and(i == 0, j == 0))
        def _():
            gg_ref[...] = jnp.zeros_like(gg_ref)

        @pl.when(j == 0)
        def _():
            acc_s[...] = jnp.zeros_like(acc_s)

        def use(d):
            acc_s[...] += _dot_nt(d, w_ref[...])

        _dproj_pick(j, da_ref, db_ref, dc_ref, use)

        @pl.when(j == NSEG - 1)
        def _():
            x = x_ref[...]
            r = lax.rsqrt(jnp.mean(x * x, axis=-1, keepdims=True) + EPS)
            xh = x * r
            dh = acc_s[...]
            gg_ref[...] += jnp.sum(dh * xh, axis=0, keepdims=True)
            dxh = dh * g_ref[...]
            gx_ref[...] = dx2_ref[...] + r * (dxh - xh * jnp.mean(dxh * xh, axis=-1, keepdims=True))

    row = lambda: pl.BlockSpec((tm, D), lambda i, j: (i, 0))
    vec = pl.BlockSpec((1, D), lambda i, j: (0, 0))
    return pl.pallas_call(
        body, name="inproj_dgrad", grid=(t // tm, NSEG),
        in_specs=_dproj_specs(tm, lambda i, j: j, lambda i, j: i)
        + [pl.BlockSpec((None, D, D), lambda i, j: (j, 0, 0)), row(), row(), vec],
        out_specs=[row(), vec],
        out_shape=[jax.ShapeDtypeStruct((t, D), F32), jax.ShapeDtypeStruct((1, D), F32)],
        scratch_shapes=[pltpu.VMEM((tm, D), F32)],
        compiler_params=_params(("arbitrary", "arbitrary")),
    )(dpa, dpb, dpc, wg, x2d, dx2, g_in)


def _adam_update(g, w, m, v):
    m_new = ADAM_B1 * m + (1.0 - ADAM_B1) * g
    v_new = ADAM_B2 * v + (1.0 - ADAM_B2) * (g * g)
    m_hat = m_new / (1.0 - ADAM_B1 ** ADAM_STEP)
    v_hat = v_new / (1.0 - ADAM_B2 ** ADAM_STEP)
    return -ADAM_LR * (m_hat / (jnp.sqrt(v_hat) + ADAM_EPS) + ADAM_WD * w), m_new, v_new


def _sum_in_order(ref):
    total = ref[0].astype(F32)
    for k in range(1, ref.shape[0]):
        total = total + ref[k].astype(F32)
    return total


def _adamw_small(me, vec_all, gx_all, ga_all, groups):
    flat = [a for grp in groups for a in grp]
    ng = len(groups)
    nshard = D // NDEV

    def body(me_ref, vec_ref, shard_ref, gx_ref, ga_ref, *refs):
        ins, outs = refs[:3 * ng], refs[3 * ng:]
        vec = _sum_in_order(vec_ref)
        shard = _sum_in_order(shard_ref)
        grads = [vec[r:r + 1, :] for r in range(6)]
        grads += [shard[0:4, :], shard[4:8, 0:DK // NDEV], _sum_in_order(gx_ref), _sum_in_order(ga_ref)]
        for n, g in enumerate(grads):
            delta, m_new, v_new = _adam_update(g, ins[3 * n][...], ins[3 * n + 1][...], ins[3 * n + 2][...])
            outs[4 * n][...] = g
            outs[4 * n + 1][...] = delta
            outs[4 * n + 2][...] = m_new
            outs[4 * n + 3][...] = v_new
        outs[4 * ng][...] = jnp.sum(vec[6:7, :], axis=1, keepdims=True)

    full = lambda a: pl.BlockSpec(a.shape, lambda i, me_ref, nd=len(a.shape): (0,) * nd)
    out_shape = [jax.ShapeDtypeStruct(w.shape, F32) for w, _, _ in groups for _ in range(4)]
    out_shape.append(jax.ShapeDtypeStruct((1, 1), F32))
    outs = pl.pallas_call(
        body, name="adamw_small",
        grid_spec=pltpu.PrefetchScalarGridSpec(
            num_scalar_prefetch=1, grid=(1,),
            in_specs=[full(vec_all),
                      pl.BlockSpec((NDEV, 8, nshard), lambda i, me_ref: (0, 1, me_ref[0])),
                      full(gx_all), full(ga_all)] + [full(a) for a in flat],
            out_specs=[full(s) for s in out_shape]),
        out_shape=out_shape,
        compiler_params=_params(("arbitrary",)),
    )(me, vec_all, vec_all, gx_all, ga_all, *flat)
    return [outs[4 * n:4 * n + 4] for n in range(ng)], outs[4 * ng]


def _adamw(name, parts, w, m, v):
    n, rows, cols = parts.shape
    tr = rows if rows <= 256 else 256

    def body(p_ref, w_ref, m_ref, v_ref, g_ref, d_ref, nm_ref, nv_ref):
        g = _sum_in_order(p_ref)
        delta, m_new, v_new = _adam_update(g, w_ref[...], m_ref[...], v_ref[...])
        g_ref[...] = g
        d_ref[...] = delta
        nm_ref[...] = m_new
        nv_ref[...] = v_new

    blk = lambda: pl.BlockSpec((tr, cols), lambda i: (i, 0))
    return pl.pallas_call(
        body, name=name, grid=(rows // tr,),
        in_specs=[pl.BlockSpec((n, tr, cols), lambda i: (0, i, 0)), blk(), blk(), blk()],
        out_specs=[blk(), blk(), blk(), blk()],
        out_shape=[jax.ShapeDtypeStruct((rows, cols), F32)] * 4,
        compiler_params=_params(("arbitrary",)),
    )(parts, w, m, v)


ANY = pl.BlockSpec(memory_space=pl.ANY)


def _place():
    return lax.axis_index("x"), lax.axis_index("y"), lax.axis_index("c")


def _gather_copies(ins, outs, send_sems, recv_sems, own_sems):
    x, y, c = _place()
    me, sibling = (x, y, c), (x, y, 1 - c)
    chips = [(1 - x, y), (x, 1 - y), (1 - x, 1 - y)]
    n = len(ins)

    def copy(a, k, block, to, src=None):
        px, py, pc = block
        dst = outs[a].at[4 * px + 2 * py + pc]
        return pltpu.make_async_remote_copy(
            src_ref=dst if src is None else src, dst_ref=dst,
            send_sem=send_sems.at[a, k], recv_sem=recv_sems.at[a, k], device_id=to, device_id_type=MESH)

    own = [pltpu.make_async_copy(ins[a], outs[a].at[4 * x + 2 * y + c], own_sems.at[a]) for a in range(n)]
    first = []
    for a in range(n):
        first.append(copy(a, 0, me, sibling, src=ins[a]))
        first += [copy(a, 1 + j, me, (*chip, c), src=ins[a]) for j, chip in enumerate(chips)]
    arrive = [copy(a, 1 + j, (*chip, c), me) for j, chip in enumerate(chips) for a in range(n)]
    forward = [copy(a, 4 + j, (*chip, c), sibling) for j, chip in enumerate(chips) for a in range(n)]
    rest = [copy(a, 0, sibling, me) for a in range(n)]
    rest += [copy(a, 4 + j, (*chip, 1 - c), me) for a in range(n) for j, chip in enumerate(chips)]
    return own, first, arrive, forward, rest


def _sibling_copies(ins, outs, send_sems, recv_sems):
    x, y, c = _place()
    return [pltpu.make_async_remote_copy(
        src_ref=ins[a].at[2 * q + 1 - c], dst_ref=outs[a].at[q],
        send_sem=send_sems.at[a, q], recv_sem=recv_sems.at[a, q],
        device_id=(x, y, 1 - c), device_id_type=MESH) for a in range(len(ins)) for q in range(4)]


def _chip_copies(ins, outs, send_sems, recv_sems, local_sems):
    x, y, c = _place()
    my_chip = 2 * x + y
    chips = [(1 - x, y), (x, 1 - y), (1 - x, 1 - y)]
    n = len(ins)
    mine = [pltpu.make_async_copy(ins[a].at[my_chip], outs[a].at[my_chip], local_sems.at[a]) for a in range(n)]
    sends = [pltpu.make_async_remote_copy(
        src_ref=ins[a].at[2 * px + py], dst_ref=outs[a].at[my_chip],
        send_sem=send_sems.at[a, j], recv_sem=recv_sems.at[a, j],
        device_id=(px, py, c), device_id_type=MESH) for a in range(n) for j, (px, py) in enumerate(chips)]
    recvs = [pltpu.make_async_remote_copy(
        src_ref=ins[a].at[my_chip], dst_ref=outs[a].at[2 * px + py],
        send_sem=send_sems.at[a, j], recv_sem=recv_sems.at[a, j],
        device_id=(px, py, c), device_id_type=MESH) for a in range(n) for j, (px, py) in enumerate(chips)]
    return mine, sends, recvs


def _chip_sum(owns, gots, core):
    n = len(owns)
    _, rows, cols = owns[0].shape

    def body(core_ref, *refs):
        for a in range(n):
            refs[2 * n + a][...] = (refs[a][...] + refs[n + a][...]).astype(BF16)

    own_spec = pl.BlockSpec((None, rows, cols), lambda q, core_ref: (2 * q + core_ref[0], 0, 0))
    slab = pl.BlockSpec((None, rows, cols), lambda q, core_ref: (q, 0, 0))
    return pl.pallas_call(
        body, name="chip_sum",
        grid_spec=pltpu.PrefetchScalarGridSpec(
            num_scalar_prefetch=1, grid=(4,),
            in_specs=[own_spec] * n + [slab] * n, out_specs=[slab] * n),
        out_shape=[jax.ShapeDtypeStruct((4, rows, cols), BF16)] * n,
        compiler_params=_params(("arbitrary",)),
    )(core, *owns, *gots)


def _block_diag(w):
    w4 = w.reshape(NCB, 4, 64, 64)
    eye = jnp.eye(4, dtype=w.dtype)
    return (w4[:, :, :, None, :] * eye[None, :, None, :, None]).reshape(NCB, CB, CB)


def _block_diag_back(g):
    g5 = g.reshape(NCB, 4, 64, 4, 64)
    return jnp.stack([g5[:, m, :, m, :] for m in range(4)], axis=1).reshape(16, 64, 64)


def kernel(x, norm_in, w_in, conv_w, conv_b, gate_x_w, gate_x_b, gate_a_w, gate_a_b, lru_lambda, gn_gain, w_proj_a, w_proj_b, w_out, norm_final, loss_target, m_norm_in, m_w_in, m_conv_w, m_conv_b, m_gate_x_w, m_gate_x_b, m_gate_a_w, m_gate_a_b, m_lru_lambda, m_gn_gain, m_w_proj_a, m_w_proj_b, m_w_out, m_norm_final, v_norm_in, v_w_in, v_conv_w, v_conv_b, v_gate_x_w, v_gate_x_b, v_gate_a_w, v_gate_a_b, v_lru_lambda, v_gn_gain, v_w_proj_a, v_w_proj_b, v_w_out, v_norm_final):
    xi, yi, ci = _place()
    me = 4 * xi + 2 * yi + ci
    core = ci.astype(jnp.int32).reshape(1)
    nshard = D // NDEV
    nb = x.shape[0]
    t = nb * S
    x2d = x.reshape(t, D)
    tgt2d = loss_target.reshape(t, D)
    g_final = norm_final.reshape(1, D)
    wbd = jnp.concatenate([_block_diag(gate_x_w[0]), _block_diag(gate_a_w[0])], axis=-1).astype(BF16)
    tables = _retention_tables()

    wp_own = jnp.concatenate([w_proj_a[0], w_proj_b[0], w_out[0]], axis=0).astype(BF16)
    tiny = jnp.concatenate([conv_w[0], jnp.pad(gn_gain[0], ((0, 0), (0, nshard - DK // NDEV)))], axis=0)
    proj, h, wg, wpg, tiny_g = _inproj_gather(x2d, norm_in, w_in[0].astype(BF16), wp_own, tiny,
                                              _gather_order(xi, yi, ci))
    conv_w_full = tiny_g[:, 0:4, :].transpose(1, 0, 2).reshape(4, D)
    gain3 = tiny_g[:, 4:8, :DK // NDEV].transpose(1, 0, 2).reshape(HEADS, 1, DK)

    ya, hs = _lru_fwd(proj, conv_w_full, conv_b, wbd, gate_x_b, gate_a_b, lru_lambda, nb)
    yb, qr, kr, o, rs = _ret_fwd(proj, gain3, tables, nb)
    dx2, dya, dyb, dpc, merged, doa, dob, g_fin, loss_vec = _tail(ya, yb, proj, x2d, tgt2d, wpg, g_final)
    g_pa, g_pb, g_out = _tail_wgrad(ya, yb, merged, doa, dob, dx2)

    own = [g.reshape(NDEV, nshard, D) for g in (g_pa, g_pb, g_out)]
    dpa, g_wbd, g_vec, *got = _lru_bwd(proj, hs, dya, conv_w_full, conv_b, wbd, gate_x_b, gate_a_b, lru_lambda, nb, own)
    sums = _chip_sum(own, got, core)
    dpb, g_gain, *parts = _ret_bwd(proj, qr, kr, o, rs, dyb, gain3, tables, nb, sums)

    grad_x, g_norm_in = _inproj_dgrad(dpa, dpb, dpc, wg, x2d, dx2, norm_in)
    grad_x = grad_x.reshape(nb, S, D)

    gain_rows = jnp.pad(g_gain.reshape(HEADS, NDEV, DK // NDEV), ((0, 0), (0, 0), (0, nshard - DK // NDEV)))
    vec = jnp.concatenate([g_norm_in, g_vec[0:4], g_fin, loss_vec, jnp.zeros((1, D), F32), g_vec[4:8],
                           gain_rows.reshape(HEADS, D)], axis=0)
    g_gx = _block_diag_back(g_wbd[:, :, :CB]).reshape(D // 2, 128)
    g_ga = _block_diag_back(g_wbd[:, :, CB:]).reshape(D // 2, 128)
    parts_in, vec_all, gx_all, ga_all = _inproj_wgrad_rs(h, dpa, dpb, dpc, _rs_order(2 * xi + yi, ci),
                                                         [vec, g_gx, g_ga])
    gx_all = gx_all.reshape(NDEV, D, 64)
    ga_all = ga_all.reshape(NDEV, D, 64)
    parts = [parts_in] + list(parts)

    big = [("w_in", w_in, m_w_in, v_w_in), ("w_proj_a", w_proj_a, m_w_proj_a, v_w_proj_a),
           ("w_proj_b", w_proj_b, m_w_proj_b, v_w_proj_b), ("w_out", w_out, m_w_out, v_w_out)]
    res = {}
    for k, (nm, w, m, v) in enumerate(big):
        out = _adamw("adamw_" + nm, parts[k], w[0], m[0], v[0])
        res[nm] = [o[None] for o in out]

    row = lambda a: a.reshape(1, D)
    gate = lambda a: a.reshape(D, 64)
    groups = [("norm_in", norm_in, m_norm_in, v_norm_in, row), ("conv_b", conv_b, m_conv_b, v_conv_b, row),
              ("gate_x_b", gate_x_b, m_gate_x_b, v_gate_x_b, row), ("gate_a_b", gate_a_b, m_gate_a_b, v_gate_a_b, row),
              ("lru_lambda", lru_lambda, m_lru_lambda, v_lru_lambda, row),
              ("norm_final", norm_final, m_norm_final, v_norm_final, row),
              ("conv_w", conv_w, m_conv_w, v_conv_w, lambda a: a[0]), ("gn_gain", gn_gain, m_gn_gain, v_gn_gain, lambda a: a[0]),
              ("gate_x_w", gate_x_w, m_gate_x_w, v_gate_x_w, gate), ("gate_a_w", gate_a_w, m_gate_a_w, v_gate_a_w, gate)]
    small_out, loss = _adamw_small(me.astype(jnp.int32).reshape(1), vec_all, gx_all, ga_all,
                                   [tuple(view(a) for a in (w, m, v)) for _, w, m, v, view in groups])
    for (nm, w, _, _, _), out in zip(groups, small_out):
        res[nm] = [o.reshape(w.shape) for o in out]
    loss = loss.reshape(())

    order = ["norm_in", "w_in", "conv_w", "conv_b", "gate_x_w", "gate_x_b", "gate_a_w", "gate_a_b", "lru_lambda",
             "gn_gain", "w_proj_a", "w_proj_b", "w_out", "norm_final"]
    outs = [loss, grad_x]
    for k in range(4):
        outs += [res[nm][k] for nm in order]
    return tuple(outs)
```

```python
import jax
import jax.numpy as jnp
from jax import lax
from jax.experimental import pallas as pl
from jax.experimental.pallas import tpu as pltpu

F32 = jnp.float32
BF16 = jnp.bfloat16
MESH = pl.DeviceIdType.MESH

D = 1024
S = 2048
NSEG = 8
NDEV = 8
HEADS = 4
DK = 256
CH = 256
NCH = S // CH
CB = 256
NCB = D // CB
RC = 128
SCAN_GROUP = 8
EPS = 1e-6
LRU_C = 8.0
VMEM_LIMIT = 56 * 1024 * 1024

ADAM_LR = 0.001
ADAM_B1 = 0.9
ADAM_B2 = 0.999
ADAM_EPS = 1e-08
ADAM_WD = 0.01
ADAM_STEP = 10


def _params(sem=None):
    return pltpu.CompilerParams(dimension_semantics=sem, vmem_limit_bytes=VMEM_LIMIT)


def _dot(a, b):
    return jnp.dot(a, b, preferred_element_type=F32)


def _dot_nt(a, b):
    return lax.dot_general(a, b, (((1,), (1,)), ((), ())), preferred_element_type=F32)


def _dot_tn(a, b):
    return lax.dot_general(a, b, (((0,), (0,)), ((), ())), preferred_element_type=F32)


def _sigmoid(x):
    return jax.nn.sigmoid(x)


def _expm1_nonpos(x):
    poly = x * (1.0 + x * (0.5 + x * (1.0 / 6.0 + x * (1.0 / 24.0))))
    return jnp.where(x > -0.05, poly, jnp.exp(x) - 1.0)


def _softplus(x):
    return jnp.maximum(x, 0.0) + jnp.log(1.0 + jnp.exp(-jnp.abs(x)))


def _rows(c, n):
    return pl.ds(pl.multiple_of(c * n, n), n)


def _window_before(ref, c, n):
    r0 = c * n
    prev = ref[pl.ds(pl.multiple_of(jnp.maximum(r0 - 8, 0), 8), 8), :]
    prev = jnp.where(c > 0, prev, 0.0)
    return jnp.concatenate([prev, ref[_rows(c, n), :]], axis=0)


def _shift_down(win, s, n):
    if s == 0:
        return win[8:, :]
    return pltpu.roll(win, s, 0)[8:, :]


def _shift_up(win, s, n):
    if s == 0:
        return win[:n, :]
    return pltpu.roll(win, n + 8 - s, 0)[:n, :]


def _gather_order(x, y, c):
    chips = [(1 - x, y), (x, 1 - y), (1 - x, 1 - y)]
    order = [4 * x + 2 * y + c, 4 * x + 2 * y + 1 - c]
    for px, py in chips:
        order += [4 * px + 2 * py + c, 4 * px + 2 * py + 1 - c]
    return jnp.stack(order).astype(jnp.int32)


def _inproj_gather(x2d, g_in, w_own, tiny_own, order):
    t = x2d.shape[0]
    tm = 1024
    nt = t // tm

    def body(order_ref, x_ref, g_ref, w_own_ref, tiny_own_ref,
             proj_ref, h_ref, wg_ref, tinyg_ref,
             w_all, h_all, send_sems, recv_sems, own_sems, out_sems):
        k, i = pl.program_id(0), pl.program_id(1)
        x, y, c = _place()
        me, sibling = (x, y, c), (x, y, 1 - c)
        chips = [(1 - x, y), (x, 1 - y), (1 - x, 1 - y)]
        srcs = [w_own_ref, tiny_own_ref]
        dsts = [w_all, tinyg_ref]

        def copy(a, n, block, to, own_src=False):
            px, py, pc = block
            dst = dsts[a].at[4 * px + 2 * py + pc]
            return pltpu.make_async_remote_copy(
                src_ref=srcs[a] if own_src else dst, dst_ref=dst,
                send_sem=send_sems.at[a, n], recv_sem=recv_sems.at[a, n], device_id=to, device_id_type=MESH)

        def own_copy(a):
            return pltpu.make_async_copy(srcs[a], dsts[a].at[4 * x + 2 * y + c], own_sems.at[a])

        def keep_copy(n):
            return pltpu.make_async_copy(w_all.at[order_ref[n]], wg_ref.at[order_ref[n]], out_sems.at[n])

        def first_copies(a):
            return [copy(a, 0, me, sibling, True)] + [copy(a, 1 + j, me, (*chip, c), True) for j, chip in enumerate(chips)]

        def at_slot(n):
            return jnp.logical_and(k == n, i == 0)

        @pl.when(at_slot(0))
        def _():
            for a in range(2):
                own_copy(a).start()
            for a in range(2):
                for cp in first_copies(a):
                    cp.start()
            own_copy(0).wait()
            keep_copy(0).start()

        @pl.when(at_slot(1))
        def _():
            copy(0, 0, sibling, me).wait_recv()
            keep_copy(1).start()

        for j, chip in enumerate(chips):
            @pl.when(at_slot(2 + 2 * j))
            def _():
                copy(0, 1 + j, (*chip, c), me).wait_recv()
                copy(0, 4 + j, (*chip, c), sibling).start()
                keep_copy(2 + 2 * j).start()

            @pl.when(at_slot(3 + 2 * j))
            def _():
                copy(0, 4 + j, (*chip, 1 - c), me).wait_recv()
                keep_copy(3 + 2 * j).start()

        rows = pl.ds(pl.multiple_of(i * tm, tm), tm)

        @pl.when(k == 0)
        def _():
            xv = x_ref[...]
            r = lax.rsqrt(jnp.mean(xv * xv, axis=-1, keepdims=True) + EPS)
            hv = (xv * r * g_ref[...]).astype(BF16)
            h_ref[...] = hv
            h_all[rows, :] = hv

        proj_ref[...] = _dot(h_all[rows, :], w_all[order_ref[k]])

        @pl.when(jnp.logical_and(k == NSEG - 1, i == nt - 1))
        def _():
            for j, chip in enumerate(chips):
                copy(1, 1 + j, (*chip, c), me).wait_recv()
                copy(1, 4 + j, (*chip, c), sibling).start()
            copy(1, 0, sibling, me).wait_recv()
            for j, chip in enumerate(chips):
                copy(1, 4 + j, (*chip, 1 - c), me).wait_recv()
            for a in range(2):
                for cp in first_copies(a):
                    cp.wait_send()
                for j, chip in enumerate(chips):
                    copy(a, 4 + j, (*chip, c), sibling).wait_send()
            own_copy(1).wait()
            for n in range(NSEG):
                keep_copy(n).wait()

    hold = lambda k, i, order_ref: (jnp.where(k == 0, i, nt - 1), 0)
    return pl.pallas_call(
        body, name="inproj_gather",
        grid_spec=pltpu.PrefetchScalarGridSpec(
            num_scalar_prefetch=1, grid=(NSEG, nt),
            in_specs=[pl.BlockSpec((tm, D), hold),
                      pl.BlockSpec((1, D), lambda k, i, order_ref: (0, 0)),
                      ANY, ANY],
            out_specs=[pl.BlockSpec((None, tm, D), lambda k, i, order_ref: (order_ref[k], i, 0)),
                       pl.BlockSpec((tm, D), hold),
                       ANY, ANY],
            scratch_shapes=[pltpu.VMEM((NDEV, D, D), BF16), pltpu.VMEM((t, D), BF16),
                            pltpu.SemaphoreType.DMA((2, 7)), pltpu.SemaphoreType.DMA((2, 7)),
                            pltpu.SemaphoreType.DMA((2,)), pltpu.SemaphoreType.DMA((NSEG,))]),
        out_shape=[jax.ShapeDtypeStruct((NSEG, t, D), F32), jax.ShapeDtypeStruct((t, D), BF16),
                   jax.ShapeDtypeStruct((NDEV,) + w_own.shape, BF16),
                   jax.ShapeDtypeStruct((NDEV,) + tiny_own.shape, F32)],
        compiler_params=_params(("arbitrary", "arbitrary")),
    )(order, x2d, g_in, w_own, tiny_own)


def _tile_scan(a, u):
    row = lax.broadcasted_iota(jnp.int32, a.shape, 0)
    for d in (1, 2, 4):
        m = row >= d
        a_sh = pltpu.roll(a, d, 0)
        u_sh = pltpu.roll(u, d, 0)
        u = jnp.where(m, a * u_sh + u, u)
        a = jnp.where(m, a * a_sh, a)
    return a, u


def _tile_scan_rev(a, w):
    row = lax.broadcasted_iota(jnp.int32, a.shape, 0)
    for d in (1, 2, 4):
        m = row < 8 - d
        a_sh = pltpu.roll(a, 8 - d, 0)
        w_sh = pltpu.roll(w, 8 - d, 0)
        w = jnp.where(m, a * w_sh + w, w)
        a = jnp.where(m, a * a_sh, a)
    return a, w


def _lru_gates(xa_ref, c, cw_ref, cb_ref, wbd_ref, bx_ref, ba_ref, sp):
    win = _window_before(xa_ref, c, RC)
    xc = cb_ref[...] + cw_ref[3:4, :] * _shift_down(win, 0, RC)
    for s in (1, 2, 3):
        xc = xc + cw_ref[3 - s:4 - s, :] * _shift_down(win, s, RC)
    z = _dot(xc.astype(BF16), wbd_ref[...])
    gi = _sigmoid(z[:, :CB] + bx_ref[...])
    gr = _sigmoid(z[:, CB:] + ba_ref[...])
    log_a = -LRU_C * gr * sp
    return win, xc, gi, gr, log_a


def _lru_fwd(proj, conv_w, conv_b, wbd, bx, ba, lam, nb, wp_own):
    t = nb * S

    def body(xa_ref, ga_ref, cw_ref, cb_ref, wbd_ref, bx_ref, ba_ref, lam_ref, wp_ref,
             ya_ref, hs_ref, xc_ref, gi_ref, gr_ref, wpg_ref, a_s, u_s, send_sems, recv_sems, own_sems):
        sp = _softplus(-lam_ref[...])
        b, cb = pl.program_id(0), pl.program_id(1)
        own, first, arrive, forward, others = _gather_copies([wp_ref], [wpg_ref], send_sems, recv_sems, own_sems)

        @pl.when(jnp.logical_and(b == 0, cb == 0))
        def _():
            for cp in own + first:
                cp.start()

        @pl.when(jnp.logical_and(b == nb - 1, cb == 0))
        def _():
            for came, on in zip(arrive, forward):
                came.wait_recv()
                on.start()

        def gates(c, carry):
            _, xc, gi, gr, log_a = _lru_gates(xa_ref, c, cw_ref, cb_ref, wbd_ref, bx_ref, ba_ref, sp)
            rows = _rows(c, RC)
            a_s[rows, :] = jnp.exp(log_a)
            u_s[rows, :] = jnp.sqrt(-_expm1_nonpos(2.0 * log_a)) * (gi * xc)
            xc_ref[rows, :] = xc
            gi_ref[rows, :] = gi
            gr_ref[rows, :] = gr
            return carry

        lax.fori_loop(0, S // RC, gates, 0)

        def scan(g, h):
            for k in range(SCAN_GROUP):
                rows = pl.ds(pl.multiple_of(g * (8 * SCAN_GROUP), 8 * SCAN_GROUP) + 8 * k, 8)
                a_cum, u_cum = _tile_scan(a_s[rows, :], u_s[rows, :])
                hs_ref[rows, :] = u_cum + a_cum * h
                h = u_cum[7:8, :] + a_cum[7:8, :] * h
            return h

        lax.fori_loop(0, S // (8 * SCAN_GROUP), scan, jnp.zeros((1, CB), F32))

        def gate_out(c, carry):
            ga = ga_ref[_rows(c, RC), :]
            ya_ref[_rows(c, RC), :] = (ga * _sigmoid(ga) * hs_ref[_rows(c, RC), :]).astype(BF16)
            return carry

        lax.fori_loop(0, S // RC, gate_out, 0)

        @pl.when(jnp.logical_and(b == nb - 1, cb == NCB - 1))
        def _():
            for cp in others:
                cp.wait_recv()
            for cp in first + forward:
                cp.wait_send()
            for cp in own:
                cp.wait()

    vec = pl.BlockSpec((1, CB), lambda b, cb: (0, cb))
    blk = pl.BlockSpec((S, CB), lambda b, cb: (b, cb))
    return pl.pallas_call(
        body, name="lru_fwd", grid=(nb, NCB),
        in_specs=[pl.BlockSpec((None, S, CB), lambda b, cb: (0, b, cb)),
                  pl.BlockSpec((None, S, CB), lambda b, cb: (1, b, cb)),
                  pl.BlockSpec((4, CB), lambda b, cb: (0, cb)),
                  vec,
                  pl.BlockSpec((None, CB, 2 * CB), lambda b, cb: (cb, 0, 0)),
                  vec, vec, vec, ANY],
        out_specs=[blk, blk, blk, blk, blk, ANY],
        out_shape=[jax.ShapeDtypeStruct((t, D), BF16)] + [jax.ShapeDtypeStruct((t, D), F32)] * 4
        + [jax.ShapeDtypeStruct((NDEV,) + wp_own.shape, wp_own.dtype)],
        scratch_shapes=[pltpu.VMEM((S, CB), F32), pltpu.VMEM((S, CB), F32),
                        pltpu.SemaphoreType.DMA((1, 7)), pltpu.SemaphoreType.DMA((1, 7)), pltpu.SemaphoreType.DMA((1,))],
        compiler_params=_params(("arbitrary", "arbitrary")),
    )(proj, proj, conv_w, conv_b, wbd, bx, ba, lam, wp_own)


def _lru_bwd(proj, hs, xc_f, gi_f, gr_f, dya, conv_w, wbd, lam, nb, give):
    t = nb * S
    ng = len(give)

    def body(xa_ref, ga_ref, hs_ref, xc_s, gi_s, gr_s, dya_ref, cw_ref, wbd_ref, lam_ref, *rest):
        give_refs, rest = rest[:ng], rest[ng:]
        dp_ref, dwbd_ref, vec_ref = rest[:3]
        got_refs, rest = rest[3:3 + ng], rest[3 + ng:]
        a_s, dl_s, dh_s, dxc_s, acc_s, send_sems, recv_sems = rest
        b = pl.program_id(1)
        exchange = _sibling_copies(give_refs, got_refs, send_sems, recv_sems)

        @pl.when(jnp.logical_and(pl.program_id(0) == 0, b == 0))
        def _():
            for cp in exchange:
                cp.start()

        lam_v = lam_ref[...]
        sp = _softplus(-lam_v)
        acc_s[...] = jnp.zeros_like(acc_s)

        @pl.when(b == 0)
        def _():
            dwbd_ref[...] = jnp.zeros_like(dwbd_ref)
            vec_ref[...] = jnp.zeros_like(vec_ref)

        def gates(c, carry):
            rows = _rows(c, RC)
            a_s[rows, :] = jnp.exp(-LRU_C * gr_s[rows, :] * sp)
            ga = ga_ref[rows, :]
            sg = _sigmoid(ga)
            dya_c = dya_ref[rows, :]
            dl_s[rows, :] = dya_c * (ga * sg)
            dp_ref[1, rows, :] = (dya_c * hs_ref[rows, :] * (sg * (1.0 + ga * (1.0 - sg)))).astype(BF16)
            return carry

        lax.fori_loop(0, S // RC, gates, 0)

        def scan(i, g_in):
            base = pl.multiple_of((S // (8 * SCAN_GROUP) - 1 - i) * (8 * SCAN_GROUP), 8 * SCAN_GROUP)
            row = lax.broadcasted_iota(jnp.int32, (8, CB), 0)
            for k in reversed(range(SCAN_GROUP)):
                rows = pl.ds(base + 8 * k, 8)
                a = a_s[rows, :]
                dl = dl_s[rows, :]
                a_cum, g_loc = _tile_scan_rev(a, a * dl)
                g = g_loc + a_cum * g_in
                dh_s[rows, :] = dl + jnp.where(row < 7, pltpu.roll(g, 7, 0), g_in)
                g_in = g_loc[0:1, :] + a_cum[0:1, :] * g_in
            return g_in

        lax.fori_loop(0, S // (8 * SCAN_GROUP), scan, jnp.zeros((1, CB), F32))

        dxc_s[pl.ds(S, 8), :] = jnp.zeros((8, CB), F32)

        def grads(c, carry):
            rows = _rows(c, RC)
            dh = dh_s[rows, :]
            h_prev = _shift_down(_window_before(hs_ref, c, RC), 1, RC)
            xc, gi, gr, a = xc_s[rows, :], gi_s[rows, :], gr_s[rows, :], a_s[rows, :]
            mult = jnp.sqrt(-_expm1_nonpos(-2.0 * LRU_C * gr * sp))
            dmult = dh * gi * xc
            d_log_a = dh * h_prev * a - dmult * (a * a) / mult
            dzi = dh * mult * xc * gi * (1.0 - gi)
            dzr = d_log_a * (-LRU_C * sp) * gr * (1.0 - gr)
            dz = jnp.concatenate([dzi, dzr], axis=1).astype(BF16)
            dxc_s[rows, :] = dh * mult * gi + _dot_nt(dz, wbd_ref[...])
            dwbd_ref[...] += _dot_tn(xc.astype(BF16), dz)
            acc_s[1:2, :] += jnp.sum(dzi, axis=0, keepdims=True)
            acc_s[2:3, :] += jnp.sum(dzr, axis=0, keepdims=True)
            acc_s[3:4, :] += jnp.sum(d_log_a * (-LRU_C * gr), axis=0, keepdims=True)
            return carry

        lax.fori_loop(0, S // RC, grads, 0)

        def conv_bwd(c, carry):
            rows = _rows(c, RC)
            dwin = dxc_s[pl.ds(pl.multiple_of(c * RC, RC), RC + 8), :]
            dxc = dwin[:RC, :]
            xwin = _window_before(xa_ref, c, RC)
            dxa = cw_ref[3:4, :] * dxc
            acc_s[0:1, :] += jnp.sum(dxc, axis=0, keepdims=True)
            acc_s[7:8, :] += jnp.sum(dxc * _shift_down(xwin, 0, RC), axis=0, keepdims=True)
            for s in (1, 2, 3):
                dxa = dxa + cw_ref[3 - s:4 - s, :] * _shift_up(dwin, s, RC)
                acc_s[7 - s:8 - s, :] += jnp.sum(dxc * _shift_down(xwin, s, RC), axis=0, keepdims=True)
            dp_ref[0, rows, :] = dxa.astype(BF16)
            return carry

        lax.fori_loop(0, S // RC, conv_bwd, 0)

        row = lax.broadcasted_iota(jnp.int32, acc_s.shape, 0)
        vec_ref[...] += jnp.where(row == 3, acc_s[...] * (-_sigmoid(-lam_v)), acc_s[...])

        @pl.when(jnp.logical_and(pl.program_id(0) == NCB - 1, b == nb - 1))
        def _():
            for cp in exchange:
                cp.wait()

    vec = pl.BlockSpec((1, CB), lambda cb, b: (0, cb))
    blk = pl.BlockSpec((S, CB), lambda cb, b: (b, cb))
    return pl.pallas_call(
        body, name="lru_bwd", grid=(NCB, nb),
        in_specs=[pl.BlockSpec((None, S, CB), lambda cb, b: (0, b, cb)),
                  pl.BlockSpec((None, S, CB), lambda cb, b: (1, b, cb)),
                  blk, blk, blk, blk, blk,
                  pl.BlockSpec((4, CB), lambda cb, b: (0, cb)),
                  pl.BlockSpec((None, CB, 2 * CB), lambda cb, b: (cb, 0, 0)),
                  vec] + [ANY] * ng,
        out_specs=[pl.BlockSpec((2, S, CB), lambda cb, b: (0, b, cb)),
                   pl.BlockSpec((None, CB, 2 * CB), lambda cb, b: (cb, 0, 0)),
                   pl.BlockSpec((8, CB), lambda cb, b: (0, cb))] + [ANY] * ng,
        out_shape=[jax.ShapeDtypeStruct((2, t, D), BF16),
                   jax.ShapeDtypeStruct((NCB, CB, 2 * CB), F32),
                   jax.ShapeDtypeStruct((8, D), F32)]
        + [jax.ShapeDtypeStruct((4,) + g.shape[1:], g.dtype) for g in give],
        scratch_shapes=[pltpu.VMEM((S, CB), F32), pltpu.VMEM((S, CB), F32), pltpu.VMEM((S, CB), F32),
                        pltpu.VMEM((S + 8, CB), F32), pltpu.VMEM((8, CB), F32),
                        pltpu.SemaphoreType.DMA((ng, 4)), pltpu.SemaphoreType.DMA((ng, 4))],
        compiler_params=_params(("arbitrary", "arbitrary")),
    )(proj, proj, hs, xc_f, gi_f, gr_f, dya, conv_w, wbd, lam, *give)


def _retention_tables():
    log_g = jnp.log1p(-(2.0 ** (-5.0 - jnp.arange(HEADS, dtype=F32))))
    idx = jnp.arange(CH, dtype=F32)
    diff = idx[:, None] - idx[None, :]
    inner = jnp.where(diff >= 0, jnp.exp(jnp.maximum(diff, 0.0)[None] * log_g[:, None, None]), 0.0)
    cross = jnp.exp((idx[None, :] + 1.0) * log_g[:, None])
    state = jnp.exp((CH - 1.0 - idx[None, :]) * log_g[:, None])
    cross = jnp.broadcast_to(cross[:, :, None], (HEADS, CH, DK))
    state = jnp.broadcast_to(state[:, :, None], (HEADS, CH, DK))
    half = DK // 2
    freqs = 10000.0 ** (-jnp.arange(half, dtype=F32) / half)
    ang = jnp.arange(S, dtype=F32)[:, None] * freqs[None, :]
    return inner, cross, state, jnp.cos(ang), jnp.sin(ang)


def _rotate(x, cos, sin):
    half = DK // 2
    x1, x2 = x[:, :half], x[:, half:]
    return jnp.concatenate([x1 * cos - x2 * sin, x1 * sin + x2 * cos], axis=1)


def _rotate_back(d, cos, sin):
    half = DK // 2
    d1, d2 = d[:, :half], d[:, half:]
    return jnp.concatenate([d1 * cos + d2 * sin, d2 * cos - d1 * sin], axis=1)


def _ret_fwd(proj, gain, tables, nb):
    t = nb * S
    inner_t, cross_t, state_t, cos_t, sin_t = tables

    def body(q_ref, k_ref, v_ref, gb_ref, gain_ref, dm_ref, cd_ref, sd_ref, cos_ref, sin_ref,
             yb_ref, qr_ref, kr_ref, o_ref, rs_ref, r_s):
        r_s[...] = jnp.zeros_like(r_s)
        chunk_decay = cd_ref[CH - 1:CH, :]

        def chunk(c, carry):
            rows = _rows(c, CH)
            cos, sin = cos_ref[rows, :], sin_ref[rows, :]
            qr = _rotate(q_ref[rows, :], cos, sin).astype(BF16)
            kr = (_rotate(k_ref[rows, :], cos, sin) * (DK ** -0.5)).astype(BF16)
            v = v_ref[rows, :]
            qr_ref[rows, :] = qr
            kr_ref[rows, :] = kr
            r = r_s[...]
            rb = r.astype(BF16)
            rs_ref[c] = rb
            p = (_dot_nt(qr, kr) * dm_ref[...]).astype(BF16)
            o = _dot(p, v.astype(BF16)) + _dot(qr, rb) * cd_ref[...]
            r_s[...] = chunk_decay * r + _dot_tn(kr, (v * sd_ref[...]).astype(BF16))
            o_ref[rows, :] = o
            oc = o - jnp.mean(o, axis=-1, keepdims=True)
            rstd = lax.rsqrt(jnp.mean(oc * oc, axis=-1, keepdims=True) + EPS)
            gb = gb_ref[rows, :]
            yb_ref[rows, :] = (gb * _sigmoid(gb) * (oc * rstd * gain_ref[...])).astype(BF16)
            return carry

        lax.fori_loop(0, NCH, chunk, 0)

    seg = lambda s: pl.BlockSpec((None, S, DK), lambda b, h: (s, b, h))
    tab = pl.BlockSpec((None, CH, DK), lambda b, h: (h, 0, 0))
    rot = pl.BlockSpec((S, DK // 2), lambda b, h: (0, 0))
    blk = pl.BlockSpec((S, DK), lambda b, h: (b, h))
    return pl.pallas_call(
        body, name="ret_fwd", grid=(nb, HEADS),
        in_specs=[seg(2), seg(3), seg(4), seg(5),
                  pl.BlockSpec((None, 1, DK), lambda b, h: (h, 0, 0)),
                  tab, tab, tab, rot, rot],
        out_specs=[blk, blk, blk, blk,
                   pl.BlockSpec((None, None, NCH, DK, DK), lambda b, h: (b, h, 0, 0, 0))],
        out_shape=[jax.ShapeDtypeStruct((t, D), BF16), jax.ShapeDtypeStruct((t, D), BF16),
                   jax.ShapeDtypeStruct((t, D), BF16), jax.ShapeDtypeStruct((t, D), F32),
                   jax.ShapeDtypeStruct((nb, HEADS, NCH, DK, DK), BF16)],
        scratch_shapes=[pltpu.VMEM((DK, DK), F32)],
        compiler_params=_params(("arbitrary", "arbitrary")),
    )(proj, proj, proj, proj, gain, inner_t, cross_t, state_t, cos_t, sin_t)


def _ret_bwd(proj, qr, kr, o, rs, dyb, gain, tables, nb, sums):
    t = nb * S
    ns = len(sums)
    inner_t, cross_t, state_t, cos_t, sin_t = tables

    def body(qr_ref, kr_ref, v_ref, gb_ref, o_ref, dyb_ref, rs_ref, gain_ref, dm_ref, cd_ref, sd_ref,
             cos_ref, sin_ref, *rest):
        sum_refs, rest = rest[:ns], rest[ns:]
        dp_ref, dgain_ref = rest[:2]
        part_refs, rest = rest[2:2 + ns], rest[2 + ns:]
        dr_s, send_sems, recv_sems, local_sems = rest
        mine, sends, recvs = _chip_copies(sum_refs, part_refs, send_sems, recv_sems, local_sems)

        @pl.when(jnp.logical_and(pl.program_id(0) == 0, pl.program_id(1) == 0))
        def _():
            for cp in mine + sends:
                cp.start()

        dr_s[...] = jnp.zeros_like(dr_s)
        chunk_decay = cd_ref[CH - 1:CH, :]

        @pl.when(pl.program_id(1) == 0)
        def _():
            dgain_ref[...] = jnp.zeros_like(dgain_ref)

        def chunk(i, carry):
            c = NCH - 1 - i
            rows = _rows(c, CH)
            gain_v = gain_ref[...]
            o_c = o_ref[rows, :]
            oc = o_c - jnp.mean(o_c, axis=-1, keepdims=True)
            rstd = lax.rsqrt(jnp.mean(oc * oc, axis=-1, keepdims=True) + EPS)
            yn = oc * rstd
            gb = gb_ref[rows, :]
            sg = _sigmoid(gb)
            dyb_c = dyb_ref[rows, :]
            dgn = dyb_c * (gb * sg)
            dp_ref[3, rows, :] = (dyb_c * (yn * gain_v) * (sg * (1.0 + gb * (1.0 - sg)))).astype(BF16)
            dgain_ref[...] += jnp.sum(dgn * yn, axis=0, keepdims=True)
            dyn = dgn * gain_v
            do = rstd * (dyn - jnp.mean(dyn, axis=-1, keepdims=True)
                         - yn * jnp.mean(dyn * yn, axis=-1, keepdims=True))
            dob = do.astype(BF16)
            dox = (do * cd_ref[...]).astype(BF16)

            q_c, k_c = qr_ref[rows, :], kr_ref[rows, :]
            v = v_ref[rows, :]
            vb = v.astype(BF16)
            vs = (v * sd_ref[...]).astype(BF16)
            rb = rs_ref[c]
            d_r = dr_s[...]
            drb = d_r.astype(BF16)
            dm = dm_ref[...]
            p = (_dot_nt(q_c, k_c) * dm).astype(BF16)
            dpm = (_dot_nt(dob, vb) * dm).astype(BF16)
            dq = _dot(dpm, k_c) + _dot_nt(dox, rb)
            dk = _dot_tn(dpm, q_c) + _dot_nt(vs, drb)
            dv = _dot_tn(p, dob) + _dot(k_c, drb) * sd_ref[...]
            dr_s[...] = chunk_decay * d_r + _dot_tn(q_c, dox)

            cos, sin = cos_ref[rows, :], sin_ref[rows, :]
            dp_ref[0, rows, :] = _rotate_back(dq, cos, sin).astype(BF16)
            dp_ref[1, rows, :] = (_rotate_back(dk, cos, sin) * (DK ** -0.5)).astype(BF16)
            dp_ref[2, rows, :] = dv.astype(BF16)
            return carry

        lax.fori_loop(0, NCH, chunk, 0)

        @pl.when(jnp.logical_and(pl.program_id(0) == HEADS - 1, pl.program_id(1) == nb - 1))
        def _():
            for cp in recvs:
                cp.wait_recv()
            for cp in sends:
                cp.wait_send()
            for cp in mine:
                cp.wait()

    seg = lambda s: pl.BlockSpec((None, S, DK), lambda h, b: (s, b, h))
    tab = pl.BlockSpec((None, CH, DK), lambda h, b: (h, 0, 0))
    rot = pl.BlockSpec((S, DK // 2), lambda h, b: (0, 0))
    blk = pl.BlockSpec((S, DK), lambda h, b: (b, h))
    one = pl.BlockSpec((None, 1, DK), lambda h, b: (h, 0, 0))
    return pl.pallas_call(
        body, name="ret_bwd", grid=(HEADS, nb),
        in_specs=[blk, blk, seg(4), seg(5), blk, blk,
                  pl.BlockSpec((None, None, NCH, DK, DK), lambda h, b: (b, h, 0, 0, 0)),
                  one, tab, tab, tab, rot, rot] + [ANY] * ns,
        out_specs=[pl.BlockSpec((4, S, DK), lambda h, b: (0, b, h)), one] + [ANY] * ns,
        out_shape=[jax.ShapeDtypeStruct((4, t, D), BF16), jax.ShapeDtypeStruct((HEADS, 1, DK), F32)]
        + [jax.ShapeDtypeStruct(a.shape, a.dtype) for a in sums],
        scratch_shapes=[pltpu.VMEM((DK, DK), F32), pltpu.SemaphoreType.DMA((ns, 3)), pltpu.SemaphoreType.DMA((ns, 3)),
                        pltpu.SemaphoreType.DMA((ns,))],
        compiler_params=_params(("arbitrary", "arbitrary")),
    )(qr, kr, proj, proj, o, dyb, rs, gain, inner_t, cross_t, state_t, cos_t, sin_t, *sums)


def _wblock(k):
    return pl.BlockSpec((NDEV, D // NDEV, D), lambda i: (0, k, 0))


def _tail(ya, yb, proj, x2d, tgt, wg, g_fin):
    t = x2d.shape[0]
    tm = 256

    def body(ya_ref, yb_ref, ma_ref, mb_ref, x_ref, t_ref, wa_ref, wb_ref, wo_ref, g_ref,
             dx2_ref, dya_ref, dyb_ref, dm_ref, mg_ref, doa_ref, dob_ref, gfin_ref, loss_ref):
        i = pl.program_id(0)

        @pl.when(i == 0)
        def _():
            gfin_ref[...] = jnp.zeros_like(gfin_ref)
            loss_ref[...] = jnp.zeros_like(loss_ref)

        wa = wa_ref[...].reshape(D, D)
        wb = wb_ref[...].reshape(D, D)
        wo = wo_ref[...].reshape(D, D)
        out_a = _dot(ya_ref[...], wa)
        out_b = _dot(yb_ref[...], wb)
        sa = _sigmoid(ma_ref[...])
        sb = _sigmoid(mb_ref[...])
        merged = (sa * out_a + sb * out_b).astype(BF16)
        mg_ref[...] = merged
        x2 = x_ref[...] + _dot(merged, wo)
        r2 = lax.rsqrt(jnp.mean(x2 * x2, axis=-1, keepdims=True) + EPS)
        xh = x2 * r2
        g = g_ref[...]
        err = xh * g - t_ref[...]
        loss_ref[...] += jnp.sum(err * err, axis=0, keepdims=True) * (0.5 / D)
        dy = err * (1.0 / D)
        gfin_ref[...] += jnp.sum(dy * xh, axis=0, keepdims=True)
        dxh = dy * g
        dx2 = r2 * (dxh - xh * jnp.mean(dxh * xh, axis=-1, keepdims=True))
        dx2_ref[...] = dx2
        dmerged = _dot_nt(dx2.astype(BF16), wo)
        doa = (sa * dmerged).astype(BF16)
        dob = (sb * dmerged).astype(BF16)
        doa_ref[...] = doa
        dob_ref[...] = dob
        dm_ref[0] = (dmerged * out_a * sa * (1.0 - sa)).astype(BF16)
        dm_ref[1] = (dmerged * out_b * sb * (1.0 - sb)).astype(BF16)
        dya_ref[...] = _dot_nt(doa, wa)
        dyb_ref[...] = _dot_nt(dob, wb)

    row = lambda: pl.BlockSpec((tm, D), lambda i: (i, 0))
    seg = lambda s: pl.BlockSpec((None, tm, D), lambda i: (s, i, 0))
    vec = pl.BlockSpec((1, D), lambda i: (0, 0))
    return pl.pallas_call(
        body, name="tail", grid=(t // tm,),
        in_specs=[row(), row(), seg(6), seg(7), row(), row(), _wblock(0), _wblock(1), _wblock(2), vec],
        out_specs=[row(), row(), row(), pl.BlockSpec((2, tm, D), lambda i: (0, i, 0)),
                   row(), row(), row(), vec, vec],
        out_shape=[jax.ShapeDtypeStruct((t, D), F32), jax.ShapeDtypeStruct((t, D), F32),
                   jax.ShapeDtypeStruct((t, D), F32), jax.ShapeDtypeStruct((2, t, D), BF16),
                   jax.ShapeDtypeStruct((t, D), BF16), jax.ShapeDtypeStruct((t, D), BF16),
                   jax.ShapeDtypeStruct((t, D), BF16), jax.ShapeDtypeStruct((1, D), F32),
                   jax.ShapeDtypeStruct((1, D), F32)],
        compiler_params=_params(("arbitrary",)),
    )(ya, yb, proj, proj, x2d, tgt, wg, wg, wg, g_fin)


def _tail_wgrad(ya, yb, merged, doa, dob, dx2):
    t = ya.shape[0]
    tm = 512

    def body(ya_ref, yb_ref, mg_ref, doa_ref, dob_ref, dx2_ref, ga_ref, gb_ref, go_ref):
        @pl.when(pl.program_id(0) == 0)
        def _():
            ga_ref[...] = jnp.zeros_like(ga_ref)
            gb_ref[...] = jnp.zeros_like(gb_ref)
            go_ref[...] = jnp.zeros_like(go_ref)

        ga_ref[...] += _dot_tn(ya_ref[...], doa_ref[...])
        gb_ref[...] += _dot_tn(yb_ref[...], dob_ref[...])
        go_ref[...] += _dot_tn(mg_ref[...], dx2_ref[...].astype(BF16))

    row = lambda: pl.BlockSpec((tm, D), lambda i: (i, 0))
    full = lambda: pl.BlockSpec((D, D), lambda i: (0, 0))
    return pl.pallas_call(
        body, name="tail_wgrad", grid=(t // tm,),
        in_specs=[row() for _ in range(6)], out_specs=[full(), full(), full()],
        out_shape=[jax.ShapeDtypeStruct((D, D), F32)] * 3,
        compiler_params=_params(("arbitrary",)),
    )(ya, yb, merged, doa, dob, dx2)


def _dproj_specs(tm, j_of, i_of):
    last = lambda j, i, lo, n: (jnp.clip(j - lo, 0, n - 1), i, 0)
    return [pl.BlockSpec((None, tm, D), lambda a, b: last(j_of(a, b), i_of(a, b), 0, 2)),
            pl.BlockSpec((None, tm, D), lambda a, b: last(j_of(a, b), i_of(a, b), 2, 4)),
            pl.BlockSpec((None, tm, D), lambda a, b: last(j_of(a, b), i_of(a, b), 6, 2))]


def _dproj_specs_ordered(tm):
    def spec(lo, n):
        def index(k, i, order_ref):
            seg = order_ref[k]
            mine = jnp.logical_and(seg >= lo, seg < lo + n)
            return jnp.where(mine, seg - lo, 0), jnp.where(mine, i, 0), 0
        return pl.BlockSpec((None, tm, D), index)
    return [spec(0, 2), spec(2, 4), spec(6, 2)]


def _dproj_pick(j, da_ref, db_ref, dc_ref, use):
    @pl.when(j < 2)
    def _():
        use(da_ref[...])

    @pl.when(jnp.logical_and(j >= 2, j < 6))
    def _():
        use(db_ref[...])

    @pl.when(j >= 6)
    def _():
        use(dc_ref[...])


def _rs_schedule(q, c):
    steps = []
    for s in range(3):
        d_a = lax.rem(q + 1 + s, 4)
        d_b = lax.rem(q + 1 + (s + 1) % 3, 4)
        steps.append((jnp.where(c == 0, d_a, d_b), jnp.where(c == 0, d_b, d_a)))
    steps.append((q, q))
    return steps


def _rs_order(q, c):
    order = []
    for keep, give in _rs_schedule(q, c):
        order += [2 * give + 1 - c, 2 * keep + c]
    return jnp.stack(order).astype(jnp.int32)


def _inproj_wgrad_rs(h, dpa, dpb, dpc, order, smalls):
    t = h.shape[0]
    tm = 1024
    nt = t // tm
    nsm = len(smalls)

    def body(order_ref, h_ref, da_ref, db_ref, dc_ref, *rest):
        small_refs, parts_ref, rest = rest[:nsm], rest[nsm], rest[nsm + 1:]
        all_refs, rest = rest[:nsm], rest[nsm:]
        (acc, sib, outb, give_send, give_recv, sum_send, sum_recv, own_sem,
         small_send, small_recv, small_own) = rest
        k, i = pl.program_id(0), pl.program_id(1)
        x, y, c = _place()
        schedule = _rs_schedule(2 * x + y, c)
        own, first, arrive, forward, others = _gather_copies(small_refs, all_refs, small_send, small_recv, small_own)

        @pl.when(jnp.logical_and(k == 0, i == 0))
        def _():
            for cp in own + first:
                cp.start()

        @pl.when(jnp.logical_and(k == 2, i == 0))
        def _():
            for came, on in zip(arrive, forward):
                came.wait_recv()
                on.start()

        def use(d):
            @pl.when(i == 0)
            def _():
                acc[k % 2] = _dot_tn(h_ref[...], d)

            @pl.when(i > 0)
            def _():
                acc[k % 2] += _dot_tn(h_ref[...], d)

        _dproj_pick(order_ref[k], da_ref, db_ref, dc_ref, use)

        def give_copy(s):
            return pltpu.make_async_remote_copy(
                src_ref=acc.at[0], dst_ref=sib.at[s % 2], send_sem=give_send.at[s], recv_sem=give_recv.at[s],
                device_id=(x, y, 1 - c), device_id_type=MESH)

        def sum_copy(s):
            keep = schedule[s][0]
            return pltpu.make_async_remote_copy(
                src_ref=outb.at[s], dst_ref=parts_ref.at[s], send_sem=sum_send.at[s], recv_sem=sum_recv.at[s],
                device_id=(keep // 2, lax.rem(keep, 2), c), device_id_type=MESH)

        own_copy = pltpu.make_async_copy(outb.at[3], parts_ref.at[3], own_sem)

        for s in range(4):
            @pl.when(jnp.logical_and(k == 2 * s, i == nt - 1))
            def _():
                give_copy(s).start()

            @pl.when(jnp.logical_and(k == 2 * s + 1, i == nt - 1))
            def _():
                give_copy(s).wait_recv()
                outb[s] = (acc[1] + sib[s % 2]).astype(BF16)
                give_copy(s).wait_send()
                if s < 3:
                    sum_copy(s).start()
                else:
                    own_copy.start()

        @pl.when(jnp.logical_and(k == NSEG - 1, i == nt - 1))
        def _():
            for s in range(3):
                sum_copy(s).wait_recv()
            for s in range(3):
                sum_copy(s).wait_send()
            own_copy.wait()
            for cp in others:
                cp.wait_recv()
            for cp in first + forward:
                cp.wait_send()
            for cp in own:
                cp.wait()

    return pl.pallas_call(
        body, name="inproj_wgrad_rs",
        grid_spec=pltpu.PrefetchScalarGridSpec(
            num_scalar_prefetch=1, grid=(NSEG, nt),
            in_specs=[pl.BlockSpec((tm, D), lambda k, i, order_ref: (i, 0))] + _dproj_specs_ordered(tm) + [ANY] * nsm,
            out_specs=[ANY] * (1 + nsm),
            scratch_shapes=[pltpu.VMEM((2, D, D), F32), pltpu.VMEM((2, D, D), F32), pltpu.VMEM((4, D, D), BF16),
                            pltpu.SemaphoreType.DMA((4,)), pltpu.SemaphoreType.DMA((4,)),
                            pltpu.SemaphoreType.DMA((3,)), pltpu.SemaphoreType.DMA((3,)),
                            pltpu.SemaphoreType.DMA,
                            pltpu.SemaphoreType.DMA((nsm, 7)), pltpu.SemaphoreType.DMA((nsm, 7)),
                            pltpu.SemaphoreType.DMA((nsm,))]),
        out_shape=[jax.ShapeDtypeStruct((4, D, D), BF16)]
        + [jax.ShapeDtypeStruct((NDEV,) + a.shape, a.dtype) for a in smalls],
        compiler_params=_params(("arbitrary", "arbitrary")),
    )(order, h, dpa, dpb, dpc, *smalls)


def _inproj_dgrad(dpa, dpb, dpc, wg, x2d, dx2, g_in):
    t = x2d.shape[0]
    tm = 1024

    def body(da_ref, db_ref, dc_ref, w_ref, x_ref, dx2_ref, g_ref, gx_ref, gg_ref, acc_s):
        i, j = pl.program_id(0), pl.program_id(1)

        @pl.when(jnp.logical_and(i == 0, j == 0))
        def _():
            gg_ref[...] = jnp.zeros_like(gg_ref)

        @pl.when(j == 0)
        def _():
            acc_s[...] = jnp.zeros_like(acc_s)

        def use(d):
            acc_s[...] += _dot_nt(d, w_ref[...])

        _dproj_pick(j, da_ref, db_ref, dc_ref, use)

        @pl.when(j == NSEG - 1)
        def _():
            x = x_ref[...]
            r = lax.rsqrt(jnp.mean(x * x, axis=-1, keepdims=True) + EPS)
            xh = x * r
            dh = acc_s[...]
            gg_ref[...] += jnp.sum(dh * xh, axis=0, keepdims=True)
            dxh = dh * g_ref[...]
            gx_ref[...] = dx2_ref[...] + r * (dxh - xh * jnp.mean(dxh * xh, axis=-1, keepdims=True))

    row = lambda: pl.BlockSpec((tm, D), lambda i, j: (i, 0))
    vec = pl.BlockSpec((1, D), lambda i, j: (0, 0))
    return pl.pallas_call(
        body, name="inproj_dgrad", grid=(t // tm, NSEG),
        in_specs=_dproj_specs(tm, lambda i, j: j, lambda i, j: i)
        + [pl.BlockSpec((None, D, D), lambda i, j: (j, 0, 0)), row(), row(), vec],
        out_specs=[row(), vec],
        out_shape=[jax.ShapeDtypeStruct((t, D), F32), jax.ShapeDtypeStruct((1, D), F32)],
        scratch_shapes=[pltpu.VMEM((tm, D), F32)],
        compiler_params=_params(("arbitrary", "arbitrary")),
    )(dpa, dpb, dpc, wg, x2d, dx2, g_in)


def _adam_update(g, w, m, v):
    m_new = ADAM_B1 * m + (1.0 - ADAM_B1) * g
    v_new = ADAM_B2 * v + (1.0 - ADAM_B2) * (g * g)
    m_hat = m_new / (1.0 - ADAM_B1 ** ADAM_STEP)
    v_hat = v_new / (1.0 - ADAM_B2 ** ADAM_STEP)
    return -ADAM_LR * (m_hat / (jnp.sqrt(v_hat) + ADAM_EPS) + ADAM_WD * w), m_new, v_new


def _sum_in_order(ref):
    total = ref[0].astype(F32)
    for k in range(1, ref.shape[0]):
        total = total + ref[k].astype(F32)
    return total


def _adamw_small(me, vec_all, gx_all, ga_all, groups):
    flat = [a for grp in groups for a in grp]
    ng = len(groups)
    nshard = D // NDEV

    def body(me_ref, vec_ref, shard_ref, gx_ref, ga_ref, *refs):
        ins, outs = refs[:3 * ng], refs[3 * ng:]
        vec = _sum_in_order(vec_ref)
        shard = _sum_in_order(shard_ref)
        grads = [vec[r:r + 1, :] for r in range(6)]
        grads += [shard[0:4, :], shard[4:8, 0:DK // NDEV], _sum_in_order(gx_ref), _sum_in_order(ga_ref)]
        for n, g in enumerate(grads):
            delta, m_new, v_new = _adam_update(g, ins[3 * n][...], ins[3 * n + 1][...], ins[3 * n + 2][...])
            outs[4 * n][...] = g
            outs[4 * n + 1][...] = delta
            outs[4 * n + 2][...] = m_new
            outs[4 * n + 3][...] = v_new
        outs[4 * ng][...] = jnp.sum(vec[6:7, :], axis=1, keepdims=True)

    full = lambda a: pl.BlockSpec(a.shape, lambda i, me_ref, nd=len(a.shape): (0,) * nd)
    out_shape = [jax.ShapeDtypeStruct(w.shape, F32) for w, _, _ in groups for _ in range(4)]
    out_shape.append(jax.ShapeDtypeStruct((1, 1), F32))
    outs = pl.pallas_call(
        body, name="adamw_small",
        grid_spec=pltpu.PrefetchScalarGridSpec(
            num_scalar_prefetch=1, grid=(1,),
            in_specs=[full(vec_all),
                      pl.BlockSpec((NDEV, 8, nshard), lambda i, me_ref: (0, 1, me_ref[0])),
                      full(gx_all), full(ga_all)] + [full(a) for a in flat],
            out_specs=[full(s) for s in out_shape]),
        out_shape=out_shape,
        compiler_params=_params(("arbitrary",)),
    )(me, vec_all, vec_all, gx_all, ga_all, *flat)
    return [outs[4 * n:4 * n + 4] for n in range(ng)], outs[4 * ng]


def _adamw(name, parts, w, m, v):
    n, rows, cols = parts.shape
    tr = rows if rows <= 256 else 256

    def body(p_ref, w_ref, m_ref, v_ref, g_ref, d_ref, nm_ref, nv_ref):
        g = _sum_in_order(p_ref)
        delta, m_new, v_new = _adam_update(g, w_ref[...], m_ref[...], v_ref[...])
        g_ref[...] = g
        d_ref[...] = delta
        nm_ref[...] = m_new
        nv_ref[...] = v_new

    blk = lambda: pl.BlockSpec((tr, cols), lambda i: (i, 0))
    return pl.pallas_call(
        body, name=name, grid=(rows // tr,),
        in_specs=[pl.BlockSpec((n, tr, cols), lambda i: (0, i, 0)), blk(), blk(), blk()],
        out_specs=[blk(), blk(), blk(), blk()],
        out_shape=[jax.ShapeDtypeStruct((rows, cols), F32)] * 4,
        compiler_params=_params(("arbitrary",)),
    )(parts, w, m, v)


ANY = pl.BlockSpec(memory_space=pl.ANY)


def _place():
    return lax.axis_index("x"), lax.axis_index("y"), lax.axis_index("c")


def _gather_copies(ins, outs, send_sems, recv_sems, own_sems):
    x, y, c = _place()
    me, sibling = (x, y, c), (x, y, 1 - c)
    chips = [(1 - x, y), (x, 1 - y), (1 - x, 1 - y)]
    n = len(ins)

    def copy(a, k, block, to, src=None):
        px, py, pc = block
        dst = outs[a].at[4 * px + 2 * py + pc]
        return pltpu.make_async_remote_copy(
            src_ref=dst if src is None else src, dst_ref=dst,
            send_sem=send_sems.at[a, k], recv_sem=recv_sems.at[a, k], device_id=to, device_id_type=MESH)

    own = [pltpu.make_async_copy(ins[a], outs[a].at[4 * x + 2 * y + c], own_sems.at[a]) for a in range(n)]
    first = []
    for a in range(n):
        first.append(copy(a, 0, me, sibling, src=ins[a]))
        first += [copy(a, 1 + j, me, (*chip, c), src=ins[a]) for j, chip in enumerate(chips)]
    arrive = [copy(a, 1 + j, (*chip, c), me) for j, chip in enumerate(chips) for a in range(n)]
    forward = [copy(a, 4 + j, (*chip, c), sibling) for j, chip in enumerate(chips) for a in range(n)]
    rest = [copy(a, 0, sibling, me) for a in range(n)]
    rest += [copy(a, 4 + j, (*chip, 1 - c), me) for a in range(n) for j, chip in enumerate(chips)]
    return own, first, arrive, forward, rest


def _sibling_copies(ins, outs, send_sems, recv_sems):
    x, y, c = _place()
    return [pltpu.make_async_remote_copy(
        src_ref=ins[a].at[2 * q + 1 - c], dst_ref=outs[a].at[q],
        send_sem=send_sems.at[a, q], recv_sem=recv_sems.at[a, q],
        device_id=(x, y, 1 - c), device_id_type=MESH) for a in range(len(ins)) for q in range(4)]


def _chip_copies(ins, outs, send_sems, recv_sems, local_sems):
    x, y, c = _place()
    my_chip = 2 * x + y
    chips = [(1 - x, y), (x, 1 - y), (1 - x, 1 - y)]
    n = len(ins)
    mine = [pltpu.make_async_copy(ins[a].at[my_chip], outs[a].at[my_chip], local_sems.at[a]) for a in range(n)]
    sends = [pltpu.make_async_remote_copy(
        src_ref=ins[a].at[2 * px + py], dst_ref=outs[a].at[my_chip],
        send_sem=send_sems.at[a, j], recv_sem=recv_sems.at[a, j],
        device_id=(px, py, c), device_id_type=MESH) for a in range(n) for j, (px, py) in enumerate(chips)]
    recvs = [pltpu.make_async_remote_copy(
        src_ref=ins[a].at[my_chip], dst_ref=outs[a].at[2 * px + py],
        send_sem=send_sems.at[a, j], recv_sem=recv_sems.at[a, j],
        device_id=(px, py, c), device_id_type=MESH) for a in range(n) for j, (px, py) in enumerate(chips)]
    return mine, sends, recvs


def _chip_sum(owns, gots, core):
    n = len(owns)
    _, rows, cols = owns[0].shape

    def body(core_ref, *refs):
        for a in range(n):
            refs[2 * n + a][...] = (refs[a][...] + refs[n + a][...]).astype(BF16)

    own_spec = pl.BlockSpec((None, rows, cols), lambda q, core_ref: (2 * q + core_ref[0], 0, 0))
    slab = pl.BlockSpec((None, rows, cols), lambda q, core_ref: (q, 0, 0))
    return pl.pallas_call(
        body, name="chip_sum",
        grid_spec=pltpu.PrefetchScalarGridSpec(
            num_scalar_prefetch=1, grid=(4,),
            in_specs=[own_spec] * n + [slab] * n, out_specs=[slab] * n),
        out_shape=[jax.ShapeDtypeStruct((4, rows, cols), BF16)] * n,
        compiler_params=_params(("arbitrary",)),
    )(core, *owns, *gots)


def _block_diag(w):
    w4 = w.reshape(NCB, 4, 64, 64)
    eye = jnp.eye(4, dtype=w.dtype)
    return (w4[:, :, :, None, :] * eye[None, :, None, :, None]).reshape(NCB, CB, CB)


def _block_diag_back(g):
    g5 = g.reshape(NCB, 4, 64, 4, 64)
    return jnp.stack([g5[:, m, :, m, :] for m in range(4)], axis=1).reshape(16, 64, 64)


def kernel(x, norm_in, w_in, conv_w, conv_b, gate_x_w, gate_x_b, gate_a_w, gate_a_b, lru_lambda, gn_gain, w_proj_a, w_proj_b, w_out, norm_final, loss_target, m_norm_in, m_w_in, m_conv_w, m_conv_b, m_gate_x_w, m_gate_x_b, m_gate_a_w, m_gate_a_b, m_lru_lambda, m_gn_gain, m_w_proj_a, m_w_proj_b, m_w_out, m_norm_final, v_norm_in, v_w_in, v_conv_w, v_conv_b, v_gate_x_w, v_gate_x_b, v_gate_a_w, v_gate_a_b, v_lru_lambda, v_gn_gain, v_w_proj_a, v_w_proj_b, v_w_out, v_norm_final):
    xi, yi, ci = _place()
    me = 4 * xi + 2 * yi + ci
    core = ci.astype(jnp.int32).reshape(1)
    nshard = D // NDEV
    nb = x.shape[0]
    t = nb * S
    x2d = x.reshape(t, D)
    tgt2d = loss_target.reshape(t, D)
    g_final = norm_final.reshape(1, D)
    wbd = jnp.concatenate([_block_diag(gate_x_w[0]), _block_diag(gate_a_w[0])], axis=-1).astype(BF16)
    tables = _retention_tables()

    wp_own = jnp.concatenate([w_proj_a[0], w_proj_b[0], w_out[0]], axis=0).astype(BF16)
    tiny = jnp.concatenate([conv_w[0], jnp.pad(gn_gain[0], ((0, 0), (0, nshard - DK // NDEV)))], axis=0)
    proj, h, wg, tiny_g = _inproj_gather(x2d, norm_in, w_in[0].astype(BF16), tiny, _gather_order(xi, yi, ci))
    conv_w_full = tiny_g[:, 0:4, :].transpose(1, 0, 2).reshape(4, D)
    gain3 = tiny_g[:, 4:8, :DK // NDEV].transpose(1, 0, 2).reshape(HEADS, 1, DK)

    ya, hs, xc, gi, gr, wpg = _lru_fwd(proj, conv_w_full, conv_b, wbd, gate_x_b, gate_a_b, lru_lambda, nb, wp_own)
    yb, qr, kr, o, rs = _ret_fwd(proj, gain3, tables, nb)
    dx2, dya, dyb, dpc, merged, doa, dob, g_fin, loss_vec = _tail(ya, yb, proj, x2d, tgt2d, wpg, g_final)
    g_pa, g_pb, g_out = _tail_wgrad(ya, yb, merged, doa, dob, dx2)

    own = [g.reshape(NDEV, nshard, D) for g in (g_pa, g_pb, g_out)]
    dpa, g_wbd, g_vec, *got = _lru_bwd(proj, hs, xc, gi, gr, dya, conv_w_full, wbd, lru_lambda, nb, own)
    sums = _chip_sum(own, got, core)
    dpb, g_gain, *parts = _ret_bwd(proj, qr, kr, o, rs, dyb, gain3, tables, nb, sums)

    grad_x, g_norm_in = _inproj_dgrad(dpa, dpb, dpc, wg, x2d, dx2, norm_in)
    grad_x = grad_x.reshape(nb, S, D)

    gain_rows = jnp.pad(g_gain.reshape(HEADS, NDEV, DK // NDEV), ((0, 0), (0, 0), (0, nshard - DK // NDEV)))
    vec = jnp.concatenate([g_norm_in, g_vec[0:4], g_fin, loss_vec, jnp.zeros((1, D), F32), g_vec[4:8],
                           gain_rows.reshape(HEADS, D)], axis=0)
    g_gx = _block_diag_back(g_wbd[:, :, :CB]).reshape(D // 2, 128)
    g_ga = _block_diag_back(g_wbd[:, :, CB:]).reshape(D // 2, 128)
    parts_in, vec_all, gx_all, ga_all = _inproj_wgrad_rs(h, dpa, dpb, dpc, _rs_order(2 * xi + yi, ci),
                                                         [vec, g_gx, g_ga])
    gx_all = gx_all.reshape(NDEV, D, 64)
    ga_all = ga_all.reshape(NDEV, D, 64)
    parts = [parts_in] + list(parts)

    big = [("w_in", w_in, m_w_in, v_w_in), ("w_proj_a", w_proj_a, m_w_proj_a, v_w_proj_a),
           ("w_proj_b", w_proj_b, m_w_proj_b, v_w_proj_b), ("w_out", w_out, m_w_out, v_w_out)]
    res = {}
    for k, (nm, w, m, v) in enumerate(big):
        out = _adamw("adamw_" + nm, parts[k], w[0], m[0], v[0])
        res[nm] = [o[None] for o in out]

    row = lambda a: a.reshape(1, D)
    gate = lambda a: a.reshape(D, 64)
    groups = [("norm_in", norm_in, m_norm_in, v_norm_in, row), ("conv_b", conv_b, m_conv_b, v_conv_b, row),
              ("gate_x_b", gate_x_b, m_gate_x_b, v_gate_x_b, row), ("gate_a_b", gate_a_b, m_gate_a_b, v_gate_a_b, row),
              ("lru_lambda", lru_lambda, m_lru_lambda, v_lru_lambda, row),
              ("norm_final", norm_final, m_norm_final, v_norm_final, row),
              ("conv_w", conv_w, m_conv_w, v_conv_w, lambda a: a[0]), ("gn_gain", gn_gain, m_gn_gain, v_gn_gain, lambda a: a[0]),
              ("gate_x_w", gate_x_w, m_gate_x_w, v_gate_x_w, gate), ("gate_a_w", gate_a_w, m_gate_a_w, v_gate_a_w, gate)]
    small_out, loss = _adamw_small(me.astype(jnp.int32).reshape(1), vec_all, gx_all, ga_all,
                                   [tuple(view(a) for a in (w, m, v)) for _, w, m, v, view in groups])
    for (nm, w, _, _, _), out in zip(groups, small_out):
        res[nm] = [o.reshape(w.shape) for o in out]
    loss = loss.reshape(())

    order = ["norm_in", "w_in", "conv_w", "conv_b", "gate_x_w", "gate_x_b", "gate_a_w", "gate_a_b", "lru_lambda",
             "gn_gain", "w_proj_a", "w_proj_b", "w_out", "norm_final"]
    outs = [loss, grad_x]
    for k in range(4):
        outs += [res[nm][k] for nm in order]
    return tuple(outs)
```

```python
import jax
import jax.numpy as jnp
from jax import lax
from jax.experimental import pallas as pl
from jax.experimental.pallas import tpu as pltpu

F32 = jnp.float32
BF16 = jnp.bfloat16
MESH = pl.DeviceIdType.MESH

D = 1024
S = 2048
NSEG = 8
NDEV = 8
HEADS = 4
DK = 256
CH = 256
NCH = S // CH
CB = 256
NCB = D // CB
RC = 128
SCAN_GROUP = 8
EPS = 1e-6
LRU_C = 8.0
VMEM_LIMIT = 56 * 1024 * 1024

ADAM_LR = 0.001
ADAM_B1 = 0.9
ADAM_B2 = 0.999
ADAM_EPS = 1e-08
ADAM_WD = 0.01
ADAM_STEP = 10


def _params(sem=None):
    return pltpu.CompilerParams(dimension_semantics=sem, vmem_limit_bytes=VMEM_LIMIT)


def _dot(a, b):
    return jnp.dot(a, b, preferred_element_type=F32)


def _dot_nt(a, b):
    return lax.dot_general(a, b, (((1,), (1,)), ((), ())), preferred_element_type=F32)


def _dot_tn(a, b):
    return lax.dot_general(a, b, (((0,), (0,)), ((), ())), preferred_element_type=F32)


def _sigmoid(x):
    return jax.nn.sigmoid(x)


def _expm1_nonpos(x):
    poly = x * (1.0 + x * (0.5 + x * (1.0 / 6.0 + x * (1.0 / 24.0))))
    return jnp.where(x > -0.05, poly, jnp.exp(x) - 1.0)


def _softplus(x):
    return jnp.maximum(x, 0.0) + jnp.log(1.0 + jnp.exp(-jnp.abs(x)))


def _rows(c, n):
    return pl.ds(pl.multiple_of(c * n, n), n)


def _window_before(ref, c, n):
    r0 = c * n
    prev = ref[pl.ds(pl.multiple_of(jnp.maximum(r0 - 8, 0), 8), 8), :]
    prev = jnp.where(c > 0, prev, 0.0)
    return jnp.concatenate([prev, ref[_rows(c, n), :]], axis=0)


def _shift_down(win, s, n):
    if s == 0:
        return win[8:, :]
    return pltpu.roll(win, s, 0)[8:, :]


def _shift_up(win, s, n):
    if s == 0:
        return win[:n, :]
    return pltpu.roll(win, n + 8 - s, 0)[:n, :]


def _gather_order(x, y, c):
    chips = [(1 - x, y), (x, 1 - y), (1 - x, 1 - y)]
    order = [4 * x + 2 * y + c, 4 * x + 2 * y + 1 - c]
    for px, py in chips:
        order += [4 * px + 2 * py + c, 4 * px + 2 * py + 1 - c]
    return jnp.stack(order).astype(jnp.int32)


def _inproj_gather(x2d, g_in, w_own, tiny_own, order):
    t = x2d.shape[0]
    tm = 1024
    nt = t // tm

    def body(order_ref, x_ref, g_ref, w_own_ref, tiny_own_ref,
             proj_ref, h_ref, wg_ref, tinyg_ref,
             w_all, h_all, send_sems, recv_sems, own_sems, out_sems):
        k, i = pl.program_id(0), pl.program_id(1)
        x, y, c = _place()
        me, sibling = (x, y, c), (x, y, 1 - c)
        chips = [(1 - x, y), (x, 1 - y), (1 - x, 1 - y)]
        srcs = [w_own_ref, tiny_own_ref]
        dsts = [w_all, tinyg_ref]

        def copy(a, n, block, to, own_src=False):
            px, py, pc = block
            dst = dsts[a].at[4 * px + 2 * py + pc]
            return pltpu.make_async_remote_copy(
                src_ref=srcs[a] if own_src else dst, dst_ref=dst,
                send_sem=send_sems.at[a, n], recv_sem=recv_sems.at[a, n], device_id=to, device_id_type=MESH)

        def own_copy(a):
            return pltpu.make_async_copy(srcs[a], dsts[a].at[4 * x + 2 * y + c], own_sems.at[a])

        def keep_copy(n):
            return pltpu.make_async_copy(w_all.at[order_ref[n]], wg_ref.at[order_ref[n]], out_sems.at[n])

        def first_copies(a):
            return [copy(a, 0, me, sibling, True)] + [copy(a, 1 + j, me, (*chip, c), True) for j, chip in enumerate(chips)]

        def at_slot(n):
            return jnp.logical_and(k == n, i == 0)

        @pl.when(at_slot(0))
        def _():
            for a in range(2):
                own_copy(a).start()
            for a in range(2):
                for cp in first_copies(a):
                    cp.start()
            own_copy(0).wait()
            keep_copy(0).start()

        @pl.when(at_slot(1))
        def _():
            copy(0, 0, sibling, me).wait_recv()
            keep_copy(1).start()

        for j, chip in enumerate(chips):
            @pl.when(at_slot(2 + 2 * j))
            def _():
                copy(0, 1 + j, (*chip, c), me).wait_recv()
                copy(0, 4 + j, (*chip, c), sibling).start()
                keep_copy(2 + 2 * j).start()

            @pl.when(at_slot(3 + 2 * j))
            def _():
                copy(0, 4 + j, (*chip, 1 - c), me).wait_recv()
                keep_copy(3 + 2 * j).start()

        rows = pl.ds(pl.multiple_of(i * tm, tm), tm)

        @pl.when(k == 0)
        def _():
            xv = x_ref[...]
            r = lax.rsqrt(jnp.mean(xv * xv, axis=-1, keepdims=True) + EPS)
            hv = (xv * r * g_ref[...]).astype(BF16)
            h_ref[...] = hv
            h_all[rows, :] = hv

        proj_ref[...] = _dot(h_all[rows, :], w_all[order_ref[k]])

        @pl.when(jnp.logical_and(k == NSEG - 1, i == nt - 1))
        def _():
            for j, chip in enumerate(chips):
                copy(1, 1 + j, (*chip, c), me).wait_recv()
                copy(1, 4 + j, (*chip, c), sibling).start()
            copy(1, 0, sibling, me).wait_recv()
            for j, chip in enumerate(chips):
                copy(1, 4 + j, (*chip, 1 - c), me).wait_recv()
            for a in range(2):
                for cp in first_copies(a):
                    cp.wait_send()
                for j, chip in enumerate(chips):
                    copy(a, 4 + j, (*chip, c), sibling).wait_send()
            own_copy(1).wait()
            for n in range(NSEG):
                keep_copy(n).wait()

    hold = lambda k, i, order_ref: (jnp.where(k == 0, i, nt - 1), 0)
    return pl.pallas_call(
        body, name="inproj_gather",
        grid_spec=pltpu.PrefetchScalarGridSpec(
            num_scalar_prefetch=1, grid=(NSEG, nt),
            in_specs=[pl.BlockSpec((tm, D), hold),
                      pl.BlockSpec((1, D), lambda k, i, order_ref: (0, 0)),
                      ANY, ANY],
            out_specs=[pl.BlockSpec((None, tm, D), lambda k, i, order_ref: (order_ref[k], i, 0)),
                       pl.BlockSpec((tm, D), hold),
                       ANY, ANY],
            scratch_shapes=[pltpu.VMEM((NDEV, D, D), BF16), pltpu.VMEM((t, D), BF16),
                            pltpu.SemaphoreType.DMA((2, 7)), pltpu.SemaphoreType.DMA((2, 7)),
                            pltpu.SemaphoreType.DMA((2,)), pltpu.SemaphoreType.DMA((NSEG,))]),
        out_shape=[jax.ShapeDtypeStruct((NSEG, t, D), F32), jax.ShapeDtypeStruct((t, D), BF16),
                   jax.ShapeDtypeStruct((NDEV,) + w_own.shape, BF16),
                   jax.ShapeDtypeStruct((NDEV,) + tiny_own.shape, F32)],
        compiler_params=_params(("arbitrary", "arbitrary")),
    )(order, x2d, g_in, w_own, tiny_own)


def _tile_scan(a, u):
    row = lax.broadcasted_iota(jnp.int32, a.shape, 0)
    for d in (1, 2, 4):
        m = row >= d
        a_sh = pltpu.roll(a, d, 0)
        u_sh = pltpu.roll(u, d, 0)
        u = jnp.where(m, a * u_sh + u, u)
        a = jnp.where(m, a * a_sh, a)
    return a, u


def _tile_scan_rev(a, w):
    row = lax.broadcasted_iota(jnp.int32, a.shape, 0)
    for d in (1, 2, 4):
        m = row < 8 - d
        a_sh = pltpu.roll(a, 8 - d, 0)
        w_sh = pltpu.roll(w, 8 - d, 0)
        w = jnp.where(m, a * w_sh + w, w)
        a = jnp.where(m, a * a_sh, a)
    return a, w


def _lru_gates(xa_ref, c, cw_ref, cb_ref, wbd_ref, bx_ref, ba_ref, sp):
    win = _window_before(xa_ref, c, RC)
    xc = cb_ref[...] + cw_ref[3:4, :] * _shift_down(win, 0, RC)
    for s in (1, 2, 3):
        xc = xc + cw_ref[3 - s:4 - s, :] * _shift_down(win, s, RC)
    z = _dot(xc.astype(BF16), wbd_ref[...])
    gi = _sigmoid(z[:, :CB] + bx_ref[...])
    gr = _sigmoid(z[:, CB:] + ba_ref[...])
    log_a = -LRU_C * gr * sp
    return win, xc, gi, gr, log_a


def _lru_fwd(proj, conv_w, conv_b, wbd, bx, ba, lam, nb, wp_own):
    t = nb * S

    def body(xa_ref, ga_ref, cw_ref, cb_ref, wbd_ref, bx_ref, ba_ref, lam_ref, wp_ref,
             ya_ref, hs_ref, xc_ref, gi_ref, gr_ref, wpg_ref, a_s, u_s, send_sems, recv_sems, own_sems):
        sp = _softplus(-lam_ref[...])
        b, cb = pl.program_id(0), pl.program_id(1)
        own, first, arrive, forward, others = _gather_copies([wp_ref], [wpg_ref], send_sems, recv_sems, own_sems)

        @pl.when(jnp.logical_and(b == 0, cb == 0))
        def _():
            for cp in own + first:
                cp.start()

        @pl.when(jnp.logical_and(b == nb - 1, cb == 0))
        def _():
            for came, on in zip(arrive, forward):
                came.wait_recv()
                on.start()

        def gates(c, carry):
            _, xc, gi, gr, log_a = _lru_gates(xa_ref, c, cw_ref, cb_ref, wbd_ref, bx_ref, ba_ref, sp)
            rows = _rows(c, RC)
            a_s[rows, :] = jnp.exp(log_a)
            u_s[rows, :] = jnp.sqrt(-_expm1_nonpos(2.0 * log_a)) * (gi * xc)
            xc_ref[rows, :] = xc
            gi_ref[rows, :] = gi
            gr_ref[rows, :] = gr
            return carry

        lax.fori_loop(0, S // RC, gates, 0)

        def scan(g, h):
            for k in range(SCAN_GROUP):
                rows = pl.ds(pl.multiple_of(g * (8 * SCAN_GROUP), 8 * SCAN_GROUP) + 8 * k, 8)
                a_cum, u_cum = _tile_scan(a_s[rows, :], u_s[rows, :])
                hs_ref[rows, :] = u_cum + a_cum * h
                h = u_cum[7:8, :] + a_cum[7:8, :] * h
            return h

        lax.fori_loop(0, S // (8 * SCAN_GROUP), scan, jnp.zeros((1, CB), F32))

        def gate_out(c, carry):
            ga = ga_ref[_rows(c, RC), :]
            ya_ref[_rows(c, RC), :] = (ga * _sigmoid(ga) * hs_ref[_rows(c, RC), :]).astype(BF16)
            return carry

        lax.fori_loop(0, S // RC, gate_out, 0)

        @pl.when(jnp.logical_and(b == nb - 1, cb == NCB - 1))
        def _():
            for cp in others:
                cp.wait_recv()
            for cp in first + forward:
                cp.wait_send()
            for cp in own:
                cp.wait()

    vec = pl.BlockSpec((1, CB), lambda b, cb: (0, cb))
    blk = pl.BlockSpec((S, CB), lambda b, cb: (b, cb))
    return pl.pallas_call(
        body, name="lru_fwd", grid=(nb, NCB),
        in_specs=[pl.BlockSpec((None, S, CB), lambda b, cb: (0, b, cb)),
                  pl.BlockSpec((None, S, CB), lambda b, cb: (1, b, cb)),
                  pl.BlockSpec((4, CB), lambda b, cb: (0, cb)),
                  vec,
                  pl.BlockSpec((None, CB, 2 * CB), lambda b, cb: (cb, 0, 0)),
                  vec, vec, vec, ANY],
        out_specs=[blk] + [pl.BlockSpec((None, None, S, CB), lambda b, cb: (b, cb, 0, 0))] * 4 + [ANY],
        out_shape=[jax.ShapeDtypeStruct((t, D), BF16)] + [jax.ShapeDtypeStruct((nb, NCB, S, CB), F32)] * 4
        + [jax.ShapeDtypeStruct((NDEV,) + wp_own.shape, wp_own.dtype)],
        scratch_shapes=[pltpu.VMEM((S, CB), F32), pltpu.VMEM((S, CB), F32),
                        pltpu.SemaphoreType.DMA((1, 7)), pltpu.SemaphoreType.DMA((1, 7)), pltpu.SemaphoreType.DMA((1,))],
        compiler_params=_params(("arbitrary", "arbitrary")),
    )(proj, proj, conv_w, conv_b, wbd, bx, ba, lam, wp_own)


def _lru_bwd(proj, hs, xc_f, gi_f, gr_f, dya, conv_w, wbd, lam, nb, give):
    t = nb * S
    ng = len(give)

    def body(xa_ref, ga_ref, hs_ref, xc_s, gi_s, gr_s, dya_ref, cw_ref, wbd_ref, lam_ref, *rest):
        give_refs, rest = rest[:ng], rest[ng:]
        dp_ref, dwbd_ref, vec_ref = rest[:3]
        got_refs, rest = rest[3:3 + ng], rest[3 + ng:]
        a_s, dl_s, dh_s, dxc_s, acc_s, send_sems, recv_sems = rest
        b = pl.program_id(1)
        exchange = _sibling_copies(give_refs, got_refs, send_sems, recv_sems)

        @pl.when(jnp.logical_and(pl.program_id(0) == 0, b == 0))
        def _():
            for cp in exchange:
                cp.start()

        lam_v = lam_ref[...]
        sp = _softplus(-lam_v)
        acc_s[...] = jnp.zeros_like(acc_s)

        @pl.when(b == 0)
        def _():
            dwbd_ref[...] = jnp.zeros_like(dwbd_ref)
            vec_ref[...] = jnp.zeros_like(vec_ref)

        def gates(c, carry):
            rows = _rows(c, RC)
            a_s[rows, :] = jnp.exp(-LRU_C * gr_s[rows, :] * sp)
            ga = ga_ref[rows, :]
            sg = _sigmoid(ga)
            dya_c = dya_ref[rows, :]
            dl_s[rows, :] = dya_c * (ga * sg)
            dp_ref[1, rows, :] = (dya_c * hs_ref[rows, :] * (sg * (1.0 + ga * (1.0 - sg)))).astype(BF16)
            return carry

        lax.fori_loop(0, S // RC, gates, 0)

        def scan(i, g_in):
            base = pl.multiple_of((S // (8 * SCAN_GROUP) - 1 - i) * (8 * SCAN_GROUP), 8 * SCAN_GROUP)
            row = lax.broadcasted_iota(jnp.int32, (8, CB), 0)
            for k in reversed(range(SCAN_GROUP)):
                rows = pl.ds(base + 8 * k, 8)
                a = a_s[rows, :]
                dl = dl_s[rows, :]
                a_cum, g_loc = _tile_scan_rev(a, a * dl)
                g = g_loc + a_cum * g_in
                dh_s[rows, :] = dl + jnp.where(row < 7, pltpu.roll(g, 7, 0), g_in)
                g_in = g_loc[0:1, :] + a_cum[0:1, :] * g_in
            return g_in

        lax.fori_loop(0, S // (8 * SCAN_GROUP), scan, jnp.zeros((1, CB), F32))

        dxc_s[pl.ds(S, 8), :] = jnp.zeros((8, CB), F32)

        def grads(c, carry):
            rows = _rows(c, RC)
            dh = dh_s[rows, :]
            h_prev = _shift_down(_window_before(hs_ref, c, RC), 1, RC)
            xc, gi, gr, a = xc_s[rows, :], gi_s[rows, :], gr_s[rows, :], a_s[rows, :]
            mult = jnp.sqrt(-_expm1_nonpos(-2.0 * LRU_C * gr * sp))
            dmult = dh * gi * xc
            d_log_a = dh * h_prev * a - dmult * (a * a) / mult
            dzi = dh * mult * xc * gi * (1.0 - gi)
            dzr = d_log_a * (-LRU_C * sp) * gr * (1.0 - gr)
            dz = jnp.concatenate([dzi, dzr], axis=1).astype(BF16)
            dxc_s[rows, :] = dh * mult * gi + _dot_nt(dz, wbd_ref[...])
            dwbd_ref[...] += _dot_tn(xc.astype(BF16), dz)
            acc_s[1:2, :] += jnp.sum(dzi, axis=0, keepdims=True)
            acc_s[2:3, :] += jnp.sum(dzr, axis=0, keepdims=True)
            acc_s[3:4, :] += jnp.sum(d_log_a * (-LRU_C * gr), axis=0, keepdims=True)
            return carry

        lax.fori_loop(0, S // RC, grads, 0)

        def conv_bwd(c, carry):
            rows = _rows(c, RC)
            dwin = dxc_s[pl.ds(pl.multiple_of(c * RC, RC), RC + 8), :]
            dxc = dwin[:RC, :]
            xwin = _window_before(xa_ref, c, RC)
            dxa = cw_ref[3:4, :] * dxc
            acc_s[0:1, :] += jnp.sum(dxc, axis=0, keepdims=True)
            acc_s[7:8, :] += jnp.sum(dxc * _shift_down(xwin, 0, RC), axis=0, keepdims=True)
            for s in (1, 2, 3):
                dxa = dxa + cw_ref[3 - s:4 - s, :] * _shift_up(dwin, s, RC)
                acc_s[7 - s:8 - s, :] += jnp.sum(dxc * _shift_down(xwin, s, RC), axis=0, keepdims=True)
            dp_ref[0, rows, :] = dxa.astype(BF16)
            return carry

        lax.fori_loop(0, S // RC, conv_bwd, 0)

        row = lax.broadcasted_iota(jnp.int32, acc_s.shape, 0)
        vec_ref[...] += jnp.where(row == 3, acc_s[...] * (-_sigmoid(-lam_v)), acc_s[...])

        @pl.when(jnp.logical_and(pl.program_id(0) == NCB - 1, b == nb - 1))
        def _():
            for cp in exchange:
                cp.wait()

    vec = pl.BlockSpec((1, CB), lambda cb, b: (0, cb))
    blk = pl.BlockSpec((S, CB), lambda cb, b: (b, cb))
    own = pl.BlockSpec((None, None, S, CB), lambda cb, b: (b, cb, 0, 0))
    return pl.pallas_call(
        body, name="lru_bwd", grid=(NCB, nb),
        in_specs=[pl.BlockSpec((None, S, CB), lambda cb, b: (0, b, cb)),
                  pl.BlockSpec((None, S, CB), lambda cb, b: (1, b, cb)),
                  own, own, own, own, blk,
                  pl.BlockSpec((4, CB), lambda cb, b: (0, cb)),
                  pl.BlockSpec((None, CB, 2 * CB), lambda cb, b: (cb, 0, 0)),
                  vec] + [ANY] * ng,
        out_specs=[pl.BlockSpec((2, S, CB), lambda cb, b: (0, b, cb)),
                   pl.BlockSpec((None, CB, 2 * CB), lambda cb, b: (cb, 0, 0)),
                   pl.BlockSpec((8, CB), lambda cb, b: (0, cb))] + [ANY] * ng,
        out_shape=[jax.ShapeDtypeStruct((2, t, D), BF16),
                   jax.ShapeDtypeStruct((NCB, CB, 2 * CB), F32),
                   jax.ShapeDtypeStruct((8, D), F32)]
        + [jax.ShapeDtypeStruct((4,) + g.shape[1:], g.dtype) for g in give],
        scratch_shapes=[pltpu.VMEM((S, CB), F32), pltpu.VMEM((S, CB), F32), pltpu.VMEM((S, CB), F32),
                        pltpu.VMEM((S + 8, CB), F32), pltpu.VMEM((8, CB), F32),
                        pltpu.SemaphoreType.DMA((ng, 4)), pltpu.SemaphoreType.DMA((ng, 4))],
        compiler_params=_params(("arbitrary", "arbitrary")),
    )(proj, proj, hs, xc_f, gi_f, gr_f, dya, conv_w, wbd, lam, *give)


def _retention_tables():
    log_g = jnp.log1p(-(2.0 ** (-5.0 - jnp.arange(HEADS, dtype=F32))))
    idx = jnp.arange(CH, dtype=F32)
    diff = idx[:, None] - idx[None, :]
    inner = jnp.where(diff >= 0, jnp.exp(jnp.maximum(diff, 0.0)[None] * log_g[:, None, None]), 0.0)
    cross = jnp.exp((idx[None, :] + 1.0) * log_g[:, None])
    state = jnp.exp((CH - 1.0 - idx[None, :]) * log_g[:, None])
    cross = jnp.broadcast_to(cross[:, :, None], (HEADS, CH, DK))
    state = jnp.broadcast_to(state[:, :, None], (HEADS, CH, DK))
    half = DK // 2
    freqs = 10000.0 ** (-jnp.arange(half, dtype=F32) / half)
    ang = jnp.arange(S, dtype=F32)[:, None] * freqs[None, :]
    return inner, cross, state, jnp.cos(ang), jnp.sin(ang)


def _rotate(x, cos, sin):
    half = DK // 2
    x1, x2 = x[:, :half], x[:, half:]
    return jnp.concatenate([x1 * cos - x2 * sin, x1 * sin + x2 * cos], axis=1)


def _rotate_back(d, cos, sin):
    half = DK // 2
    d1, d2 = d[:, :half], d[:, half:]
    return jnp.concatenate([d1 * cos + d2 * sin, d2 * cos - d1 * sin], axis=1)


def _ret_fwd(proj, gain, tables, nb):
    t = nb * S
    inner_t, cross_t, state_t, cos_t, sin_t = tables

    def body(q_ref, k_ref, v_ref, gb_ref, gain_ref, dm_ref, cd_ref, sd_ref, cos_ref, sin_ref,
             yb_ref, qr_ref, kr_ref, o_ref, rs_ref, r_s):
        r_s[...] = jnp.zeros_like(r_s)
        chunk_decay = cd_ref[CH - 1:CH, :]

        def chunk(c, carry):
            rows = _rows(c, CH)
            cos, sin = cos_ref[rows, :], sin_ref[rows, :]
            qr = _rotate(q_ref[rows, :], cos, sin).astype(BF16)
            kr = (_rotate(k_ref[rows, :], cos, sin) * (DK ** -0.5)).astype(BF16)
            v = v_ref[rows, :]
            qr_ref[rows, :] = qr
            kr_ref[rows, :] = kr
            r = r_s[...]
            rb = r.astype(BF16)
            rs_ref[c] = rb
            p = (_dot_nt(qr, kr) * dm_ref[...]).astype(BF16)
            o = _dot(p, v.astype(BF16)) + _dot(qr, rb) * cd_ref[...]
            r_s[...] = chunk_decay * r + _dot_tn(kr, (v * sd_ref[...]).astype(BF16))
            o_ref[rows, :] = o
            oc = o - jnp.mean(o, axis=-1, keepdims=True)
            rstd = lax.rsqrt(jnp.mean(oc * oc, axis=-1, keepdims=True) + EPS)
            gb = gb_ref[rows, :]
            yb_ref[rows, :] = (gb * _sigmoid(gb) * (oc * rstd * gain_ref[...])).astype(BF16)
            return carry

        lax.fori_loop(0, NCH, chunk, 0)

    seg = lambda s: pl.BlockSpec((None, S, DK), lambda b, h: (s, b, h))
    tab = pl.BlockSpec((None, CH, DK), lambda b, h: (h, 0, 0))
    rot = pl.BlockSpec((S, DK // 2), lambda b, h: (0, 0))
    blk = pl.BlockSpec((S, DK), lambda b, h: (b, h))
    return pl.pallas_call(
        body, name="ret_fwd", grid=(nb, HEADS),
        in_specs=[seg(2), seg(3), seg(4), seg(5),
                  pl.BlockSpec((None, 1, DK), lambda b, h: (h, 0, 0)),
                  tab, tab, tab, rot, rot],
        out_specs=[blk, blk, blk, blk,
                   pl.BlockSpec((None, None, NCH, DK, DK), lambda b, h: (b, h, 0, 0, 0))],
        out_shape=[jax.ShapeDtypeStruct((t, D), BF16), jax.ShapeDtypeStruct((t, D), BF16),
                   jax.ShapeDtypeStruct((t, D), BF16), jax.ShapeDtypeStruct((t, D), F32),
                   jax.ShapeDtypeStruct((nb, HEADS, NCH, DK, DK), BF16)],
        scratch_shapes=[pltpu.VMEM((DK, DK), F32)],
        compiler_params=_params(("arbitrary", "arbitrary")),
    )(proj, proj, proj, proj, gain, inner_t, cross_t, state_t, cos_t, sin_t)


def _ret_bwd(proj, qr, kr, o, rs, dyb, gain, tables, nb, sums):
    t = nb * S
    ns = len(sums)
    inner_t, cross_t, state_t, cos_t, sin_t = tables

    def body(qr_ref, kr_ref, v_ref, gb_ref, o_ref, dyb_ref, rs_ref, gain_ref, dm_ref, cd_ref, sd_ref,
             cos_ref, sin_ref, *rest):
        sum_refs, rest = rest[:ns], rest[ns:]
        dp_ref, dgain_ref = rest[:2]
        part_refs, rest = rest[2:2 + ns], rest[2 + ns:]
        dr_s, send_sems, recv_sems, local_sems = rest
        mine, sends, recvs = _chip_copies(sum_refs, part_refs, send_sems, recv_sems, local_sems)

        @pl.when(jnp.logical_and(pl.program_id(0) == 0, pl.program_id(1) == 0))
        def _():
            for cp in mine + sends:
                cp.start()

        dr_s[...] = jnp.zeros_like(dr_s)
        chunk_decay = cd_ref[CH - 1:CH, :]

        @pl.when(pl.program_id(1) == 0)
        def _():
            dgain_ref[...] = jnp.zeros_like(dgain_ref)

        def chunk(i, carry):
            c = NCH - 1 - i
            rows = _rows(c, CH)
            gain_v = gain_ref[...]
            o_c = o_ref[rows, :]
            oc = o_c - jnp.mean(o_c, axis=-1, keepdims=True)
            rstd = lax.rsqrt(jnp.mean(oc * oc, axis=-1, keepdims=True) + EPS)
            yn = oc * rstd
            gb = gb_ref[rows, :]
            sg = _sigmoid(gb)
            dyb_c = dyb_ref[rows, :]
            dgn = dyb_c * (gb * sg)
            dp_ref[3, rows, :] = (dyb_c * (yn * gain_v) * (sg * (1.0 + gb * (1.0 - sg)))).astype(BF16)
            dgain_ref[...] += jnp.sum(dgn * yn, axis=0, keepdims=True)
            dyn = dgn * gain_v
            do = rstd * (dyn - jnp.mean(dyn, axis=-1, keepdims=True)
                         - yn * jnp.mean(dyn * yn, axis=-1, keepdims=True))
            dob = do.astype(BF16)
            dox = (do * cd_ref[...]).astype(BF16)

            q_c, k_c = qr_ref[rows, :], kr_ref[rows, :]
            v = v_ref[rows, :]
            vb = v.astype(BF16)
            vs = (v * sd_ref[...]).astype(BF16)
            rb = rs_ref[c]
            d_r = dr_s[...]
            drb = d_r.astype(BF16)
            dm = dm_ref[...]
            p = (_dot_nt(q_c, k_c) * dm).astype(BF16)
            dpm = (_dot_nt(dob, vb) * dm).astype(BF16)
            dq = _dot(dpm, k_c) + _dot_nt(dox, rb)
            dk = _dot_tn(dpm, q_c) + _dot_nt(vs, drb)
            dv = _dot_tn(p, dob) + _dot(k_c, drb) * sd_ref[...]
            dr_s[...] = chunk_decay * d_r + _dot_tn(q_c, dox)

            cos, sin = cos_ref[rows, :], sin_ref[rows, :]
            dp_ref[0, rows, :] = _rotate_back(dq, cos, sin).astype(BF16)
            dp_ref[1, rows, :] = (_rotate_back(dk, cos, sin) * (DK ** -0.5)).astype(BF16)
            dp_ref[2, rows, :] = dv.astype(BF16)
            return carry

        lax.fori_loop(0, NCH, chunk, 0)

        @pl.when(jnp.logical_and(pl.program_id(0) == HEADS - 1, pl.program_id(1) == nb - 1))
        def _():
            for cp in recvs:
                cp.wait_recv()
            for cp in sends:
                cp.wait_send()
            for cp in mine:
                cp.wait()

    seg = lambda s: pl.BlockSpec((None, S, DK), lambda h, b: (s, b, h))
    tab = pl.BlockSpec((None, CH, DK), lambda h, b: (h, 0, 0))
    rot = pl.BlockSpec((S, DK // 2), lambda h, b: (0, 0))
    blk = pl.BlockSpec((S, DK), lambda h, b: (b, h))
    one = pl.BlockSpec((None, 1, DK), lambda h, b: (h, 0, 0))
    return pl.pallas_call(
        body, name="ret_bwd", grid=(HEADS, nb),
        in_specs=[blk, blk, seg(4), seg(5), blk, blk,
                  pl.BlockSpec((None, None, NCH, DK, DK), lambda h, b: (b, h, 0, 0, 0)),
                  one, tab, tab, tab, rot, rot] + [ANY] * ns,
        out_specs=[pl.BlockSpec((4, S, DK), lambda h, b: (0, b, h)), one] + [ANY] * ns,
        out_shape=[jax.ShapeDtypeStruct((4, t, D), BF16), jax.ShapeDtypeStruct((HEADS, 1, DK), F32)]
        + [jax.ShapeDtypeStruct(a.shape, a.dtype) for a in sums],
        scratch_shapes=[pltpu.VMEM((DK, DK), F32), pltpu.SemaphoreType.DMA((ns, 3)), pltpu.SemaphoreType.DMA((ns, 3)),
                        pltpu.SemaphoreType.DMA((ns,))],
        compiler_params=_params(("arbitrary", "arbitrary")),
    )(qr, kr, proj, proj, o, dyb, rs, gain, inner_t, cross_t, state_t, cos_t, sin_t, *sums)


def _wblock(k):
    return pl.BlockSpec((NDEV, D // NDEV, D), lambda i: (0, k, 0))


def _tail(ya, yb, proj, x2d, tgt, wg, g_fin):
    t = x2d.shape[0]
    tm = 256

    def body(ya_ref, yb_ref, ma_ref, mb_ref, x_ref, t_ref, wa_ref, wb_ref, wo_ref, g_ref,
             dx2_ref, dya_ref, dyb_ref, dm_ref, mg_ref, doa_ref, dob_ref, gfin_ref, loss_ref):
        i = pl.program_id(0)

        @pl.when(i == 0)
        def _():
            gfin_ref[...] = jnp.zeros_like(gfin_ref)
            loss_ref[...] = jnp.zeros_like(loss_ref)

        wa = wa_ref[...].reshape(D, D)
        wb = wb_ref[...].reshape(D, D)
        wo = wo_ref[...].reshape(D, D)
        out_a = _dot(ya_ref[...], wa)
        out_b = _dot(yb_ref[...], wb)
        sa = _sigmoid(ma_ref[...])
        sb = _sigmoid(mb_ref[...])
        merged = (sa * out_a + sb * out_b).astype(BF16)
        mg_ref[...] = merged
        x2 = x_ref[...] + _dot(merged, wo)
        r2 = lax.rsqrt(jnp.mean(x2 * x2, axis=-1, keepdims=True) + EPS)
        xh = x2 * r2
        g = g_ref[...]
        err = xh * g - t_ref[...]
        loss_ref[...] += jnp.sum(err * err, axis=0, keepdims=True) * (0.5 / D)
        dy = err * (1.0 / D)
        gfin_ref[...] += jnp.sum(dy * xh, axis=0, keepdims=True)
        dxh = dy * g
        dx2 = r2 * (dxh - xh * jnp.mean(dxh * xh, axis=-1, keepdims=True))
        dx2_ref[...] = dx2
        dmerged = _dot_nt(dx2.astype(BF16), wo)
        doa = (sa * dmerged).astype(BF16)
        dob = (sb * dmerged).astype(BF16)
        doa_ref[...] = doa
        dob_ref[...] = dob
        dm_ref[0] = (dmerged * out_a * sa * (1.0 - sa)).astype(BF16)
        dm_ref[1] = (dmerged * out_b * sb * (1.0 - sb)).astype(BF16)
        dya_ref[...] = _dot_nt(doa, wa)
        dyb_ref[...] = _dot_nt(dob, wb)

    row = lambda: pl.BlockSpec((tm, D), lambda i: (i, 0))
    seg = lambda s: pl.BlockSpec((None, tm, D), lambda i: (s, i, 0))
    vec = pl.BlockSpec((1, D), lambda i: (0, 0))
    return pl.pallas_call(
        body, name="tail", grid=(t // tm,),
        in_specs=[row(), row(), seg(6), seg(7), row(), row(), _wblock(0), _wblock(1), _wblock(2), vec],
        out_specs=[row(), row(), row(), pl.BlockSpec((2, tm, D), lambda i: (0, i, 0)),
                   row(), row(), row(), vec, vec],
        out_shape=[jax.ShapeDtypeStruct((t, D), F32), jax.ShapeDtypeStruct((t, D), F32),
                   jax.ShapeDtypeStruct((t, D), F32), jax.ShapeDtypeStruct((2, t, D), BF16),
                   jax.ShapeDtypeStruct((t, D), BF16), jax.ShapeDtypeStruct((t, D), BF16),
                   jax.ShapeDtypeStruct((t, D), BF16), jax.ShapeDtypeStruct((1, D), F32),
                   jax.ShapeDtypeStruct((1, D), F32)],
        compiler_params=_params(("arbitrary",)),
    )(ya, yb, proj, proj, x2d, tgt, wg, wg, wg, g_fin)


def _tail_wgrad(ya, yb, merged, doa, dob, dx2):
    t = ya.shape[0]
    tm = 512

    def body(ya_ref, yb_ref, mg_ref, doa_ref, dob_ref, dx2_ref, ga_ref, gb_ref, go_ref):
        @pl.when(pl.program_id(0) == 0)
        def _():
            ga_ref[...] = jnp.zeros_like(ga_ref)
            gb_ref[...] = jnp.zeros_like(gb_ref)
            go_ref[...] = jnp.zeros_like(go_ref)

        ga_ref[...] += _dot_tn(ya_ref[...], doa_ref[...])
        gb_ref[...] += _dot_tn(yb_ref[...], dob_ref[...])
        go_ref[...] += _dot_tn(mg_ref[...], dx2_ref[...].astype(BF16))

    row = lambda: pl.BlockSpec((tm, D), lambda i: (i, 0))
    full = lambda: pl.BlockSpec((D, D), lambda i: (0, 0))
    return pl.pallas_call(
        body, name="tail_wgrad", grid=(t // tm,),
        in_specs=[row() for _ in range(6)], out_specs=[full(), full(), full()],
        out_shape=[jax.ShapeDtypeStruct((D, D), F32)] * 3,
        compiler_params=_params(("arbitrary",)),
    )(ya, yb, merged, doa, dob, dx2)


def _dproj_specs(tm, j_of, i_of):
    last = lambda j, i, lo, n: (jnp.clip(j - lo, 0, n - 1), i, 0)
    return [pl.BlockSpec((None, tm, D), lambda a, b: last(j_of(a, b), i_of(a, b), 0, 2)),
            pl.BlockSpec((None, tm, D), lambda a, b: last(j_of(a, b), i_of(a, b), 2, 4)),
            pl.BlockSpec((None, tm, D), lambda a, b: last(j_of(a, b), i_of(a, b), 6, 2))]


def _dproj_specs_ordered(tm):
    def spec(lo, n):
        def index(k, i, order_ref):
            seg = order_ref[k]
            mine = jnp.logical_and(seg >= lo, seg < lo + n)
            return jnp.where(mine, seg - lo, 0), jnp.where(mine, i, 0), 0
        return pl.BlockSpec((None, tm, D), index)
    return [spec(0, 2), spec(2, 4), spec(6, 2)]


def _dproj_pick(j, da_ref, db_ref, dc_ref, use):
    @pl.when(j < 2)
    def _():
        use(da_ref[...])

    @pl.when(jnp.logical_and(j >= 2, j < 6))
    def _():
        use(db_ref[...])

    @pl.when(j >= 6)
    def _():
        use(dc_ref[...])


def _rs_schedule(q, c):
    steps = []
    for s in range(3):
        d_a = lax.rem(q + 1 + s, 4)
        d_b = lax.rem(q + 1 + (s + 1) % 3, 4)
        steps.append((jnp.where(c == 0, d_a, d_b), jnp.where(c == 0, d_b, d_a)))
    steps.append((q, q))
    return steps


def _rs_order(q, c):
    order = []
    for keep, give in _rs_schedule(q, c):
        order += [2 * give + 1 - c, 2 * keep + c]
    return jnp.stack(order).astype(jnp.int32)


def _inproj_wgrad_rs(h, dpa, dpb, dpc, order, smalls):
    t = h.shape[0]
    tm = 1024
    nt = t // tm
    nsm = len(smalls)

    def body(order_ref, h_ref, da_ref, db_ref, dc_ref, *rest):
        small_refs, parts_ref, rest = rest[:nsm], rest[nsm], rest[nsm + 1:]
        all_refs, rest = rest[:nsm], rest[nsm:]
        (acc, sib, outb, give_send, give_recv, sum_send, sum_recv, own_sem,
         small_send, small_recv, small_own) = rest
        k, i = pl.program_id(0), pl.program_id(1)
        x, y, c = _place()
        schedule = _rs_schedule(2 * x + y, c)
        own, first, arrive, forward, others = _gather_copies(small_refs, all_refs, small_send, small_recv, small_own)

        @pl.when(jnp.logical_and(k == 0, i == 0))
        def _():
            for cp in own + first:
                cp.start()

        @pl.when(jnp.logical_and(k == 2, i == 0))
        def _():
            for came, on in zip(arrive, forward):
                came.wait_recv()
                on.start()

        def use(d):
            @pl.when(i == 0)
            def _():
                acc[k % 2] = _dot_tn(h_ref[...], d)

            @pl.when(i > 0)
            def _():
                acc[k % 2] += _dot_tn(h_ref[...], d)

        _dproj_pick(order_ref[k], da_ref, db_ref, dc_ref, use)

        def give_copy(s):
            return pltpu.make_async_remote_copy(
                src_ref=acc.at[0], dst_ref=sib.at[s % 2], send_sem=give_send.at[s], recv_sem=give_recv.at[s],
                device_id=(x, y, 1 - c), device_id_type=MESH)

        def sum_copy(s):
            keep = schedule[s][0]
            return pltpu.make_async_remote_copy(
                src_ref=outb.at[s], dst_ref=parts_ref.at[s], send_sem=sum_send.at[s], recv_sem=sum_recv.at[s],
                device_id=(keep // 2, lax.rem(keep, 2), c), device_id_type=MESH)

        own_copy = pltpu.make_async_copy(outb.at[3], parts_ref.at[3], own_sem)

        for s in range(4):
            @pl.when(jnp.logical_and(k == 2 * s, i == nt - 1))
            def _():
                give_copy(s).start()

            @pl.when(jnp.logical_and(k == 2 * s + 1, i == nt - 1))
            def _():
                give_copy(s).wait_recv()
                outb[s] = (acc[1] + sib[s % 2]).astype(BF16)
                give_copy(s).wait_send()
                if s < 3:
                    sum_copy(s).start()
                else:
                    own_copy.start()

        @pl.when(jnp.logical_and(k == NSEG - 1, i == nt - 1))
        def _():
            for s in range(3):
                sum_copy(s).wait_recv()
            for s in range(3):
                sum_copy(s).wait_send()
            own_copy.wait()
            for cp in others:
                cp.wait_recv()
            for cp in first + forward:
                cp.wait_send()
            for cp in own:
                cp.wait()

    return pl.pallas_call(
        body, name="inproj_wgrad_rs",
        grid_spec=pltpu.PrefetchScalarGridSpec(
            num_scalar_prefetch=1, grid=(NSEG, nt),
            in_specs=[pl.BlockSpec((tm, D), lambda k, i, order_ref: (i, 0))] + _dproj_specs_ordered(tm) + [ANY] * nsm,
            out_specs=[ANY] * (1 + nsm),
            scratch_shapes=[pltpu.VMEM((2, D, D), F32), pltpu.VMEM((2, D, D), F32), pltpu.VMEM((4, D, D), BF16),
                            pltpu.SemaphoreType.DMA((4,)), pltpu.SemaphoreType.DMA((4,)),
                            pltpu.SemaphoreType.DMA((3,)), pltpu.SemaphoreType.DMA((3,)),
                            pltpu.SemaphoreType.DMA,
                            pltpu.SemaphoreType.DMA((nsm, 7)), pltpu.SemaphoreType.DMA((nsm, 7)),
                            pltpu.SemaphoreType.DMA((nsm,))]),
        out_shape=[jax.ShapeDtypeStruct((4, D, D), BF16)]
        + [jax.ShapeDtypeStruct((NDEV,) + a.shape, a.dtype) for a in smalls],
        compiler_params=_params(("arbitrary", "arbitrary")),
    )(order, h, dpa, dpb, dpc, *smalls)


def _inproj_dgrad(dpa, dpb, dpc, wg, x2d, dx2, g_in):
    t = x2d.shape[0]
    tm = 1024

    def body(da_ref, db_ref, dc_ref, w_ref, x_ref, dx2_ref, g_ref, gx_ref, gg_ref, acc_s):
        i, j = pl.program_id(0), pl.program_id(1)

        @pl.when(jnp.logical_and(i == 0, j == 0))
        def _():
            gg_ref[...] = jnp.zeros_like(gg_ref)

        @pl.when(j == 0)
        def _():
            acc_s[...] = jnp.zeros_like(acc_s)

        def use(d):
            acc_s[...] += _dot_nt(d, w_ref[...])

        _dproj_pick(j, da_ref, db_ref, dc_ref, use)

        @pl.when(j == NSEG - 1)
        def _():
            x = x_ref[...]
            r = lax.rsqrt(jnp.mean(x * x, axis=-1, keepdims=True) + EPS)
            xh = x * r
            dh = acc_s[...]
            gg_ref[...] += jnp.sum(dh * xh, axis=0, keepdims=True)
            dxh = dh * g_ref[...]
            gx_ref[...] = dx2_ref[...] + r * (dxh - xh * jnp.mean(dxh * xh, axis=-1, keepdims=True))

    row = lambda: pl.BlockSpec((tm, D), lambda i, j: (i, 0))
    vec = pl.BlockSpec((1, D), lambda i, j: (0, 0))
    return pl.pallas_call(
        body, name="inproj_dgrad", grid=(t // tm, NSEG),
        in_specs=_dproj_specs(tm, lambda i, j: j, lambda i, j: i)
        + [pl.BlockSpec((None, D, D), lambda i, j: (j, 0, 0)), row(), row(), vec],
        out_specs=[row(), vec],
        out_shape=[jax.ShapeDtypeStruct((t, D), F32), jax.ShapeDtypeStruct((1, D), F32)],
        scratch_shapes=[pltpu.VMEM((tm, D), F32)],
        compiler_params=_params(("arbitrary", "arbitrary")),
    )(dpa, dpb, dpc, wg, x2d, dx2, g_in)


def _adam_update(g, w, m, v):
    m_new = ADAM_B1 * m + (1.0 - ADAM_B1) * g
    v_new = ADAM_B2 * v + (1.0 - ADAM_B2) * (g * g)
    m_hat = m_new / (1.0 - ADAM_B1 ** ADAM_STEP)
    v_hat = v_new / (1.0 - ADAM_B2 ** ADAM_STEP)
    return -ADAM_LR * (m_hat / (jnp.sqrt(v_hat) + ADAM_EPS) + ADAM_WD * w), m_new, v_new


def _sum_in_order(ref):
    total = ref[0].astype(F32)
    for k in range(1, ref.shape[0]):
        total = total + ref[k].astype(F32)
    return total


def _adamw_small(me, vec_all, gx_all, ga_all, groups):
    flat = [a for grp in groups for a in grp]
    ng = len(groups)
    nshard = D // NDEV

    def body(me_ref, vec_ref, shard_ref, gx_ref, ga_ref, *refs):
        ins, outs = refs[:3 * ng], refs[3 * ng:]
        vec = _sum_in_order(vec_ref)
        shard = _sum_in_order(shard_ref)
        grads = [vec[r:r + 1, :] for r in range(6)]
        grads += [shard[0:4, :], shard[4:8, 0:DK // NDEV], _sum_in_order(gx_ref), _sum_in_order(ga_ref)]
        for n, g in enumerate(grads):
            delta, m_new, v_new = _adam_update(g, ins[3 * n][...], ins[3 * n + 1][...], ins[3 * n + 2][...])
            outs[4 * n][...] = g
            outs[4 * n + 1][...] = delta
            outs[4 * n + 2][...] = m_new
            outs[4 * n + 3][...] = v_new
        outs[4 * ng][...] = jnp.sum(vec[6:7, :], axis=1, keepdims=True)

    full = lambda a: pl.BlockSpec(a.shape, lambda i, me_ref, nd=len(a.shape): (0,) * nd)
    out_shape = [jax.ShapeDtypeStruct(w.shape, F32) for w, _, _ in groups for _ in range(4)]
    out_shape.append(jax.ShapeDtypeStruct((1, 1), F32))
    outs = pl.pallas_call(
        body, name="adamw_small",
        grid_spec=pltpu.PrefetchScalarGridSpec(
            num_scalar_prefetch=1, grid=(1,),
            in_specs=[full(vec_all),
                      pl.BlockSpec((NDEV, 8, nshard), lambda i, me_ref: (0, 1, me_ref[0])),
                      full(gx_all), full(ga_all)] + [full(a) for a in flat],
            out_specs=[full(s) for s in out_shape]),
        out_shape=out_shape,
        compiler_params=_params(("arbitrary",)),
    )(me, vec_all, vec_all, gx_all, ga_all, *flat)
    return [outs[4 * n:4 * n + 4] for n in range(ng)], outs[4 * ng]


def _adamw(name, parts, w, m, v):
    n, rows, cols = parts.shape
    tr = rows if rows <= 256 else 256

    def body(p_ref, w_ref, m_ref, v_ref, g_ref, d_ref, nm_ref, nv_ref):
        g = _sum_in_order(p_ref)
        delta, m_new, v_new = _adam_update(g, w_ref[...], m_ref[...], v_ref[...])
        g_ref[...] = g
        d_ref[...] = delta
        nm_ref[...] = m_new
        nv_ref[...] = v_new

    blk = lambda: pl.BlockSpec((tr, cols), lambda i: (i, 0))
    return pl.pallas_call(
        body, name=name, grid=(rows // tr,),
        in_specs=[pl.BlockSpec((n, tr, cols), lambda i: (0, i, 0)), blk(), blk(), blk()],
        out_specs=[blk(), blk(), blk(), blk()],
        out_shape=[jax.ShapeDtypeStruct((rows, cols), F32)] * 4,
        compiler_params=_params(("arbitrary",)),
    )(parts, w, m, v)


ANY = pl.BlockSpec(memory_space=pl.ANY)


def _place():
    return lax.axis_index("x"), lax.axis_index("y"), lax.axis_index("c")


def _gather_copies(ins, outs, send_sems, recv_sems, own_sems):
    x, y, c = _place()
    me, sibling = (x, y, c), (x, y, 1 - c)
    chips = [(1 - x, y), (x, 1 - y), (1 - x, 1 - y)]
    n = len(ins)

    def copy(a, k, block, to, src=None):
        px, py, pc = block
        dst = outs[a].at[4 * px + 2 * py + pc]
        return pltpu.make_async_remote_copy(
            src_ref=dst if src is None else src, dst_ref=dst,
            send_sem=send_sems.at[a, k], recv_sem=recv_sems.at[a, k], device_id=to, device_id_type=MESH)

    own = [pltpu.make_async_copy(ins[a], outs[a].at[4 * x + 2 * y + c], own_sems.at[a]) for a in range(n)]
    first = []
    for a in range(n):
        first.append(copy(a, 0, me, sibling, src=ins[a]))
        first += [copy(a, 1 + j, me, (*chip, c), src=ins[a]) for j, chip in enumerate(chips)]
    arrive = [copy(a, 1 + j, (*chip, c), me) for j, chip in enumerate(chips) for a in range(n)]
    forward = [copy(a, 4 + j, (*chip, c), sibling) for j, chip in enumerate(chips) for a in range(n)]
    rest = [copy(a, 0, sibling, me) for a in range(n)]
    rest += [copy(a, 4 + j, (*chip, 1 - c), me) for a in range(n) for j, chip in enumerate(chips)]
    return own, first, arrive, forward, rest


def _sibling_copies(ins, outs, send_sems, recv_sems):
    x, y, c = _place()
    return [pltpu.make_async_remote_copy(
        src_ref=ins[a].at[2 * q + 1 - c], dst_ref=outs[a].at[q],
        send_sem=send_sems.at[a, q], recv_sem=recv_sems.at[a, q],
        device_id=(x, y, 1 - c), device_id_type=MESH) for a in range(len(ins)) for q in range(4)]


def _chip_copies(ins, outs, send_sems, recv_sems, local_sems):
    x, y, c = _place()
    my_chip = 2 * x + y
    chips = [(1 - x, y), (x, 1 - y), (1 - x, 1 - y)]
    n = len(ins)
    mine = [pltpu.make_async_copy(ins[a].at[my_chip], outs[a].at[my_chip], local_sems.at[a]) for a in range(n)]
    sends = [pltpu.make_async_remote_copy(
        src_ref=ins[a].at[2 * px + py], dst_ref=outs[a].at[my_chip],
        send_sem=send_sems.at[a, j], recv_sem=recv_sems.at[a, j],
        device_id=(px, py, c), device_id_type=MESH) for a in range(n) for j, (px, py) in enumerate(chips)]
    recvs = [pltpu.make_async_remote_copy(
        src_ref=ins[a].at[my_chip], dst_ref=outs[a].at[2 * px + py],
        send_sem=send_sems.at[a, j], recv_sem=recv_sems.at[a, j],
        device_id=(px, py, c), device_id_type=MESH) for a in range(n) for j, (px, py) in enumerate(chips)]
    return mine, sends, recvs


def _chip_sum(owns, gots, core):
    n = len(owns)
    _, rows, cols = owns[0].shape

    def body(core_ref, *refs):
        for a in range(n):
            refs[2 * n + a][...] = (refs[a][...] + refs[n + a][...]).astype(BF16)

    own_spec = pl.BlockSpec((None, rows, cols), lambda q, core_ref: (2 * q + core_ref[0], 0, 0))
    slab = pl.BlockSpec((None, rows, cols), lambda q, core_ref: (q, 0, 0))
    return pl.pallas_call(
        body, name="chip_sum",
        grid_spec=pltpu.PrefetchScalarGridSpec(
            num_scalar_prefetch=1, grid=(4,),
            in_specs=[own_spec] * n + [slab] * n, out_specs=[slab] * n),
        out_shape=[jax.ShapeDtypeStruct((4, rows, cols), BF16)] * n,
        compiler_params=_params(("arbitrary",)),
    )(core, *owns, *gots)


def _block_diag(w):
    w4 = w.reshape(NCB, 4, 64, 64)
    eye = jnp.eye(4, dtype=w.dtype)
    return (w4[:, :, :, None, :] * eye[None, :, None, :, None]).reshape(NCB, CB, CB)


def _block_diag_back(g):
    g5 = g.reshape(NCB, 4, 64, 4, 64)
    return jnp.stack([g5[:, m, :, m, :] for m in range(4)], axis=1).reshape(16, 64, 64)


def kernel(x, norm_in, w_in, conv_w, conv_b, gate_x_w, gate_x_b, gate_a_w, gate_a_b, lru_lambda, gn_gain, w_proj_a, w_proj_b, w_out, norm_final, loss_target, m_norm_in, m_w_in, m_conv_w, m_conv_b, m_gate_x_w, m_gate_x_b, m_gate_a_w, m_gate_a_b, m_lru_lambda, m_gn_gain, m_w_proj_a, m_w_proj_b, m_w_out, m_norm_final, v_norm_in, v_w_in, v_conv_w, v_conv_b, v_gate_x_w, v_gate_x_b, v_gate_a_w, v_gate_a_b, v_lru_lambda, v_gn_gain, v_w_proj_a, v_w_proj_b, v_w_out, v_norm_final):
    xi, yi, ci = _place()
    me = 4 * xi + 2 * yi + ci
    core = ci.astype(jnp.int32).reshape(1)
    nshard = D // NDEV
    nb = x.shape[0]
    t = nb * S
    x2d = x.reshape(t, D)
    tgt2d = loss_target.reshape(t, D)
    g_final = norm_final.reshape(1, D)
    wbd = jnp.concatenate([_block_diag(gate_x_w[0]), _block_diag(gate_a_w[0])], axis=-1).astype(BF16)
    tables = _retention_tables()

    wp_own = jnp.concatenate([w_proj_a[0], w_proj_b[0], w_out[0]], axis=0).astype(BF16)
    tiny = jnp.concatenate([conv_w[0], jnp.pad(gn_gain[0], ((0, 0), (0, nshard - DK // NDEV)))], axis=0)
    proj, h, wg, tiny_g = _inproj_gather(x2d, norm_in, w_in[0].astype(BF16), tiny, _gather_order(xi, yi, ci))
    conv_w_full = tiny_g[:, 0:4, :].transpose(1, 0, 2).reshape(4, D)
    gain3 = tiny_g[:, 4:8, :DK // NDEV].transpose(1, 0, 2).reshape(HEADS, 1, DK)

    ya, hs, xc, gi, gr, wpg = _lru_fwd(proj, conv_w_full, conv_b, wbd, gate_x_b, gate_a_b, lru_lambda, nb, wp_own)
    yb, qr, kr, o, rs = _ret_fwd(proj, gain3, tables, nb)
    dx2, dya, dyb, dpc, merged, doa, dob, g_fin, loss_vec = _tail(ya, yb, proj, x2d, tgt2d, wpg, g_final)
    g_pa, g_pb, g_out = _tail_wgrad(ya, yb, merged, doa, dob, dx2)

    own = [g.reshape(NDEV, nshard, D) for g in (g_pa, g_pb, g_out)]
    dpa, g_wbd, g_vec, *got = _lru_bwd(proj, hs, xc, gi, gr, dya, conv_w_full, wbd, lru_lambda, nb, own)
    sums = _chip_sum(own, got, core)
    dpb, g_gain, *parts = _ret_bwd(proj, qr, kr, o, rs, dyb, gain3, tables, nb, sums)

    grad_x, g_norm_in = _inproj_dgrad(dpa, dpb, dpc, wg, x2d, dx2, norm_in)
    grad_x = grad_x.reshape(nb, S, D)

    gain_rows = jnp.pad(g_gain.reshape(HEADS, NDEV, DK // NDEV), ((0, 0), (0, 0), (0, nshard - DK // NDEV)))
    vec = jnp.concatenate([g_norm_in, g_vec[0:4], g_fin, loss_vec, jnp.zeros((1, D), F32), g_vec[4:8],
                           gain_rows.reshape(HEADS, D)], axis=0)
    g_gx = _block_diag_back(g_wbd[:, :, :CB]).reshape(D // 2, 128)
    g_ga = _block_diag_back(g_wbd[:, :, CB:]).reshape(D // 2, 128)
    parts_in, vec_all, gx_all, ga_all = _inproj_wgrad_rs(h, dpa, dpb, dpc, _rs_order(2 * xi + yi, ci),
                                                         [vec, g_gx, g_ga])
    gx_all = gx_all.reshape(NDEV, D, 64)
    ga_all = ga_all.reshape(NDEV, D, 64)
    parts = [parts_in] + list(parts)

    big = [("w_in", w_in, m_w_in, v_w_in), ("w_proj_a", w_proj_a, m_w_proj_a, v_w_proj_a),
           ("w_proj_b", w_proj_b, m_w_proj_b, v_w_proj_b), ("w_out", w_out, m_w_out, v_w_out)]
    res = {}
    for k, (nm, w, m, v) in enumerate(big):
        out = _adamw("adamw_" + nm, parts[k], w[0], m[0], v[0])
        res[nm] = [o[None] for o in out]

    row = lambda a: a.reshape(1, D)
    gate = lambda a: a.reshape(D, 64)
    groups = [("norm_in", norm_in, m_norm_in, v_norm_in, row), ("conv_b", conv_b, m_conv_b, v_conv_b, row),
              ("gate_x_b", gate_x_b, m_gate_x_b, v_gate_x_b, row), ("gate_a_b", gate_a_b, m_gate_a_b, v_gate_a_b, row),
              ("lru_lambda", lru_lambda, m_lru_lambda, v_lru_lambda, row),
              ("norm_final", norm_final, m_norm_final, v_norm_final, row),
              ("conv_w", conv_w, m_conv_w, v_conv_w, lambda a: a[0]), ("gn_gain", gn_gain, m_gn_gain, v_gn_gain, lambda a: a[0]),
              ("gate_x_w", gate_x_w, m_gate_x_w, v_gate_x_w, gate), ("gate_a_w", gate_a_w, m_gate_a_w, v_gate_a_w, gate)]
    small_out, loss = _adamw_small(me.astype(jnp.int32).reshape(1), vec_all, gx_all, ga_all,
                                   [tuple(view(a) for a in (w, m, v)) for _, w, m, v, view in groups])
    for (nm, w, _, _, _), out in zip(groups, small_out):
        res[nm] = [o.reshape(w.shape) for o in out]
    loss = loss.reshape(())

    order = ["norm_in", "w_in", "conv_w", "conv_b", "gate_x_w", "gate_x_b", "gate_a_w", "gate_a_b", "lru_lambda",
             "gn_gain", "w_proj_a", "w_proj_b", "w_out", "norm_final"]
    outs = [loss, grad_x]
    for k in range(4):
        outs += [res[nm][k] for nm in order]
    return tuple(outs)
```

```python
import jax
import jax.numpy as jnp
from jax import lax
from jax.experimental import pallas as pl
from jax.experimental.pallas import tpu as pltpu

F32 = jnp.float32
BF16 = jnp.bfloat16
MESH = pl.DeviceIdType.MESH

D = 1024
S = 2048
NSEG = 8
NDEV = 8
HEADS = 4
DK = 256
CH = 256
NCH = S // CH
CB = 256
NCB = D // CB
RC = 128
SCAN_GROUP = 8
EPS = 1e-6
LRU_C = 8.0
VMEM_LIMIT = 56 * 1024 * 1024

ADAM_LR = 0.001
ADAM_B1 = 0.9
ADAM_B2 = 0.999
ADAM_EPS = 1e-08
ADAM_WD = 0.01
ADAM_STEP = 10


def _params(sem=None):
    return pltpu.CompilerParams(dimension_semantics=sem, vmem_limit_bytes=VMEM_LIMIT)


def _dot(a, b):
    return jnp.dot(a, b, preferred_element_type=F32)


def _dot_nt(a, b):
    return lax.dot_general(a, b, (((1,), (1,)), ((), ())), preferred_element_type=F32)


def _dot_tn(a, b):
    return lax.dot_general(a, b, (((0,), (0,)), ((), ())), preferred_element_type=F32)


def _sigmoid(x):
    return jax.nn.sigmoid(x)


def _expm1_nonpos(x):
    poly = x * (1.0 + x * (0.5 + x * (1.0 / 6.0 + x * (1.0 / 24.0))))
    return jnp.where(x > -0.05, poly, jnp.exp(x) - 1.0)


def _softplus(x):
    return jnp.maximum(x, 0.0) + jnp.log(1.0 + jnp.exp(-jnp.abs(x)))


def _rows(c, n):
    return pl.ds(pl.multiple_of(c * n, n), n)


def _window_before(ref, c, n):
    r0 = c * n
    prev = ref[pl.ds(pl.multiple_of(jnp.maximum(r0 - 8, 0), 8), 8), :]
    prev = jnp.where(c > 0, prev, 0.0)
    return jnp.concatenate([prev, ref[_rows(c, n), :]], axis=0)


def _shift_down(win, s, n):
    if s == 0:
        return win[8:, :]
    return pltpu.roll(win, s, 0)[8:, :]


def _shift_up(win, s, n):
    if s == 0:
        return win[:n, :]
    return pltpu.roll(win, n + 8 - s, 0)[:n, :]


def _gather_order(x, y, c):
    chips = [(1 - x, y), (x, 1 - y), (1 - x, 1 - y)]
    order = [4 * x + 2 * y + c, 4 * x + 2 * y + 1 - c]
    for px, py in chips:
        order += [4 * px + 2 * py + c, 4 * px + 2 * py + 1 - c]
    return jnp.stack(order).astype(jnp.int32)


def _inproj_gather(x2d, g_in, w_own, tiny_own, order):
    t = x2d.shape[0]
    tm = 1024
    nt = t // tm

    def body(order_ref, x_ref, g_ref, w_own_ref, tiny_own_ref,
             proj_ref, h_ref, wg_ref, tinyg_ref,
             w_all, h_all, send_sems, recv_sems, own_sems, out_sems):
        k, i = pl.program_id(0), pl.program_id(1)
        x, y, c = _place()
        me, sibling = (x, y, c), (x, y, 1 - c)
        chips = [(1 - x, y), (x, 1 - y), (1 - x, 1 - y)]
        srcs = [w_own_ref, tiny_own_ref]
        dsts = [w_all, tinyg_ref]

        def copy(a, n, block, to, own_src=False):
            px, py, pc = block
            dst = dsts[a].at[4 * px + 2 * py + pc]
            return pltpu.make_async_remote_copy(
                src_ref=srcs[a] if own_src else dst, dst_ref=dst,
                send_sem=send_sems.at[a, n], recv_sem=recv_sems.at[a, n], device_id=to, device_id_type=MESH)

        def own_copy(a):
            return pltpu.make_async_copy(srcs[a], dsts[a].at[4 * x + 2 * y + c], own_sems.at[a])

        def keep_copy(n):
            return pltpu.make_async_copy(w_all.at[order_ref[n]], wg_ref.at[order_ref[n]], out_sems.at[n])

        def first_copies(a):
            return [copy(a, 0, me, sibling, True)] + [copy(a, 1 + j, me, (*chip, c), True) for j, chip in enumerate(chips)]

        def at_slot(n):
            return jnp.logical_and(k == n, i == 0)

        @pl.when(at_slot(0))
        def _():
            for a in range(2):
                own_copy(a).start()
            for a in range(2):
                for cp in first_copies(a):
                    cp.start()
            own_copy(0).wait()
            keep_copy(0).start()

        @pl.when(at_slot(1))
        def _():
            copy(0, 0, sibling, me).wait_recv()
            keep_copy(1).start()

        for j, chip in enumerate(chips):
            @pl.when(at_slot(2 + 2 * j))
            def _():
                copy(0, 1 + j, (*chip, c), me).wait_recv()
                copy(0, 4 + j, (*chip, c), sibling).start()
                keep_copy(2 + 2 * j).start()

            @pl.when(at_slot(3 + 2 * j))
            def _():
                copy(0, 4 + j, (*chip, 1 - c), me).wait_recv()
                keep_copy(3 + 2 * j).start()

        rows = pl.ds(pl.multiple_of(i * tm, tm), tm)

        @pl.when(k == 0)
        def _():
            xv = x_ref[...]
            r = lax.rsqrt(jnp.mean(xv * xv, axis=-1, keepdims=True) + EPS)
            hv = (xv * r * g_ref[...]).astype(BF16)
            h_ref[...] = hv
            h_all[rows, :] = hv

        proj_ref[...] = _dot(h_all[rows, :], w_all[order_ref[k]])

        @pl.when(jnp.logical_and(k == NSEG - 1, i == nt - 1))
        def _():
            for j, chip in enumerate(chips):
                copy(1, 1 + j, (*chip, c), me).wait_recv()
                copy(1, 4 + j, (*chip, c), sibling).start()
            copy(1, 0, sibling, me).wait_recv()
            for j, chip in enumerate(chips):
                copy(1, 4 + j, (*chip, 1 - c), me).wait_recv()
            for a in range(2):
                for cp in first_copies(a):
                    cp.wait_send()
                for j, chip in enumerate(chips):
                    copy(a, 4 + j, (*chip, c), sibling).wait_send()
            own_copy(1).wait()
            for n in range(NSEG):
                keep_copy(n).wait()

    hold = lambda k, i, order_ref: (jnp.where(k == 0, i, nt - 1), 0)
    return pl.pallas_call(
        body, name="inproj_gather",
        grid_spec=pltpu.PrefetchScalarGridSpec(
            num_scalar_prefetch=1, grid=(NSEG, nt),
            in_specs=[pl.BlockSpec((tm, D), hold),
                      pl.BlockSpec((1, D), lambda k, i, order_ref: (0, 0)),
                      ANY, ANY],
            out_specs=[pl.BlockSpec((None, tm, D), lambda k, i, order_ref: (order_ref[k], i, 0)),
                       pl.BlockSpec((tm, D), hold),
                       ANY, ANY],
            scratch_shapes=[pltpu.VMEM((NDEV, D, D), BF16), pltpu.VMEM((t, D), BF16),
                            pltpu.SemaphoreType.DMA((2, 7)), pltpu.SemaphoreType.DMA((2, 7)),
                            pltpu.SemaphoreType.DMA((2,)), pltpu.SemaphoreType.DMA((NSEG,))]),
        out_shape=[jax.ShapeDtypeStruct((NSEG, t, D), F32), jax.ShapeDtypeStruct((t, D), BF16),
                   jax.ShapeDtypeStruct((NDEV,) + w_own.shape, BF16),
                   jax.ShapeDtypeStruct((NDEV,) + tiny_own.shape, F32)],
        compiler_params=_params(("arbitrary", "arbitrary")),
    )(order, x2d, g_in, w_own, tiny_own)


def _tile_scan(a, u):
    row = lax.broadcasted_iota(jnp.int32, a.shape, 0)
    for d in (1, 2, 4):
        m = row >= d
        a_sh = pltpu.roll(a, d, 0)
        u_sh = pltpu.roll(u, d, 0)
        u = jnp.where(m, a * u_sh + u, u)
        a = jnp.where(m, a * a_sh, a)
    return a, u


def _tile_scan_rev(a, w):
    row = lax.broadcasted_iota(jnp.int32, a.shape, 0)
    for d in (1, 2, 4):
        m = row < 8 - d
        a_sh = pltpu.roll(a, 8 - d, 0)
        w_sh = pltpu.roll(w, 8 - d, 0)
        w = jnp.where(m, a * w_sh + w, w)
        a = jnp.where(m, a * a_sh, a)
    return a, w


def _lru_gates(xa_ref, c, cw_ref, cb_ref, wbd_ref, bx_ref, ba_ref, sp):
    win = _window_before(xa_ref, c, RC)
    xc = cb_ref[...] + cw_ref[3:4, :] * _shift_down(win, 0, RC)
    for s in (1, 2, 3):
        xc = xc + cw_ref[3 - s:4 - s, :] * _shift_down(win, s, RC)
    z = _dot(xc.astype(BF16), wbd_ref[...])
    gi = _sigmoid(z[:, :CB] + bx_ref[...])
    gr = _sigmoid(z[:, CB:] + ba_ref[...])
    log_a = -LRU_C * gr * sp
    return win, xc, gi, gr, log_a


def _lru_fwd(proj, conv_w, conv_b, wbd, bx, ba, lam, nb):
    t = nb * S

    def body(xa_ref, ga_ref, cw_ref, cb_ref, wbd_ref, bx_ref, ba_ref, lam_ref,
             ya_ref, hs_ref, xc_ref, gi_ref, gr_ref, a_s, u_s):
        sp = _softplus(-lam_ref[...])

        def gates(c, carry):
            _, xc, gi, gr, log_a = _lru_gates(xa_ref, c, cw_ref, cb_ref, wbd_ref, bx_ref, ba_ref, sp)
            rows = _rows(c, RC)
            a_s[rows, :] = jnp.exp(log_a)
            u_s[rows, :] = jnp.sqrt(-_expm1_nonpos(2.0 * log_a)) * (gi * xc)
            xc_ref[rows, :] = xc
            gi_ref[rows, :] = gi
            gr_ref[rows, :] = gr
            return carry

        lax.fori_loop(0, S // RC, gates, 0)

        def scan(g, h):
            for k in range(SCAN_GROUP):
                rows = pl.ds(pl.multiple_of(g * (8 * SCAN_GROUP), 8 * SCAN_GROUP) + 8 * k, 8)
                a_cum, u_cum = _tile_scan(a_s[rows, :], u_s[rows, :])
                hs_ref[rows, :] = u_cum + a_cum * h
                h = u_cum[7:8, :] + a_cum[7:8, :] * h
            return h

        lax.fori_loop(0, S // (8 * SCAN_GROUP), scan, jnp.zeros((1, CB), F32))

        def gate_out(c, carry):
            ga = ga_ref[_rows(c, RC), :]
            ya_ref[_rows(c, RC), :] = (ga * _sigmoid(ga) * hs_ref[_rows(c, RC), :]).astype(BF16)
            return carry

        lax.fori_loop(0, S // RC, gate_out, 0)

    vec = pl.BlockSpec((1, CB), lambda b, cb: (0, cb))
    blk = pl.BlockSpec((S, CB), lambda b, cb: (b, cb))
    return pl.pallas_call(
        body, name="lru_fwd", grid=(nb, NCB),
        in_specs=[pl.BlockSpec((None, S, CB), lambda b, cb: (0, b, cb)),
                  pl.BlockSpec((None, S, CB), lambda b, cb: (1, b, cb)),
                  pl.BlockSpec((4, CB), lambda b, cb: (0, cb)),
                  vec,
                  pl.BlockSpec((None, CB, 2 * CB), lambda b, cb: (cb, 0, 0)),
                  vec, vec, vec],
        out_specs=[blk] + [pl.BlockSpec((None, None, S, CB), lambda b, cb: (b, cb, 0, 0))] * 4,
        out_shape=[jax.ShapeDtypeStruct((t, D), BF16)] + [jax.ShapeDtypeStruct((nb, NCB, S, CB), F32)] * 4,
        scratch_shapes=[pltpu.VMEM((S, CB), F32), pltpu.VMEM((S, CB), F32)],
        compiler_params=_params(("arbitrary", "arbitrary")),
    )(proj, proj, conv_w, conv_b, wbd, bx, ba, lam)


def _lru_bwd(proj, hs, xc_f, gi_f, gr_f, dya, conv_w, wbd, lam, nb, give):
    t = nb * S
    ng = len(give)

    def body(xa_ref, ga_ref, hs_ref, xc_s, gi_s, gr_s, dya_ref, cw_ref, wbd_ref, lam_ref, *rest):
        give_refs, rest = rest[:ng], rest[ng:]
        dp_ref, dwbd_ref, vec_ref = rest[:3]
        got_refs, rest = rest[3:3 + ng], rest[3 + ng:]
        a_s, dl_s, dh_s, dxc_s, acc_s, send_sems, recv_sems = rest
        b = pl.program_id(1)
        exchange = _sibling_copies(give_refs, got_refs, send_sems, recv_sems)

        @pl.when(jnp.logical_and(pl.program_id(0) == 0, b == 0))
        def _():
            for cp in exchange:
                cp.start()

        lam_v = lam_ref[...]
        sp = _softplus(-lam_v)
        acc_s[...] = jnp.zeros_like(acc_s)

        @pl.when(b == 0)
        def _():
            dwbd_ref[...] = jnp.zeros_like(dwbd_ref)
            vec_ref[...] = jnp.zeros_like(vec_ref)

        def gates(c, carry):
            rows = _rows(c, RC)
            a_s[rows, :] = jnp.exp(-LRU_C * gr_s[rows, :] * sp)
            ga = ga_ref[rows, :]
            sg = _sigmoid(ga)
            dya_c = dya_ref[rows, :]
            dl_s[rows, :] = dya_c * (ga * sg)
            dp_ref[1, rows, :] = (dya_c * hs_ref[rows, :] * (sg * (1.0 + ga * (1.0 - sg)))).astype(BF16)
            return carry

        lax.fori_loop(0, S // RC, gates, 0)

        def scan(i, g_in):
            base = pl.multiple_of((S // (8 * SCAN_GROUP) - 1 - i) * (8 * SCAN_GROUP), 8 * SCAN_GROUP)
            row = lax.broadcasted_iota(jnp.int32, (8, CB), 0)
            for k in reversed(range(SCAN_GROUP)):
                rows = pl.ds(base + 8 * k, 8)
                a = a_s[rows, :]
                dl = dl_s[rows, :]
                a_cum, g_loc = _tile_scan_rev(a, a * dl)
                g = g_loc + a_cum * g_in
                dh_s[rows, :] = dl + jnp.where(row < 7, pltpu.roll(g, 7, 0), g_in)
                g_in = g_loc[0:1, :] + a_cum[0:1, :] * g_in
            return g_in

        lax.fori_loop(0, S // (8 * SCAN_GROUP), scan, jnp.zeros((1, CB), F32))

        dxc_s[pl.ds(S, 8), :] = jnp.zeros((8, CB), F32)

        def grads(c, carry):
            rows = _rows(c, RC)
            dh = dh_s[rows, :]
            h_prev = _shift_down(_window_before(hs_ref, c, RC), 1, RC)
            xc, gi, gr, a = xc_s[rows, :], gi_s[rows, :], gr_s[rows, :], a_s[rows, :]
            mult = jnp.sqrt(-_expm1_nonpos(-2.0 * LRU_C * gr * sp))
            dmult = dh * gi * xc
            d_log_a = dh * h_prev * a - dmult * (a * a) / mult
            dzi = dh * mult * xc * gi * (1.0 - gi)
            dzr = d_log_a * (-LRU_C * sp) * gr * (1.0 - gr)
            dz = jnp.concatenate([dzi, dzr], axis=1).astype(BF16)
            dxc_s[rows, :] = dh * mult * gi + _dot_nt(dz, wbd_ref[...])
            dwbd_ref[...] += _dot_tn(xc.astype(BF16), dz)
            acc_s[1:2, :] += jnp.sum(dzi, axis=0, keepdims=True)
            acc_s[2:3, :] += jnp.sum(dzr, axis=0, keepdims=True)
            acc_s[3:4, :] += jnp.sum(d_log_a * (-LRU_C * gr), axis=0, keepdims=True)
            return carry

        lax.fori_loop(0, S // RC, grads, 0)

        def conv_bwd(c, carry):
            rows = _rows(c, RC)
            dwin = dxc_s[pl.ds(pl.multiple_of(c * RC, RC), RC + 8), :]
            dxc = dwin[:RC, :]
            xwin = _window_before(xa_ref, c, RC)
            dxa = cw_ref[3:4, :] * dxc
            acc_s[0:1, :] += jnp.sum(dxc, axis=0, keepdims=True)
            acc_s[7:8, :] += jnp.sum(dxc * _shift_down(xwin, 0, RC), axis=0, keepdims=True)
            for s in (1, 2, 3):
                dxa = dxa + cw_ref[3 - s:4 - s, :] * _shift_up(dwin, s, RC)
                acc_s[7 - s:8 - s, :] += jnp.sum(dxc * _shift_down(xwin, s, RC), axis=0, keepdims=True)
            dp_ref[0, rows, :] = dxa.astype(BF16)
            return carry

        lax.fori_loop(0, S // RC, conv_bwd, 0)

        row = lax.broadcasted_iota(jnp.int32, acc_s.shape, 0)
        vec_ref[...] += jnp.where(row == 3, acc_s[...] * (-_sigmoid(-lam_v)), acc_s[...])

        @pl.when(jnp.logical_and(pl.program_id(0) == NCB - 1, b == nb - 1))
        def _():
            for cp in exchange:
                cp.wait()

    vec = pl.BlockSpec((1, CB), lambda cb, b: (0, cb))
    blk = pl.BlockSpec((S, CB), lambda cb, b: (b, cb))
    own = pl.BlockSpec((None, None, S, CB), lambda cb, b: (b, cb, 0, 0))
    return pl.pallas_call(
        body, name="lru_bwd", grid=(NCB, nb),
        in_specs=[pl.BlockSpec((None, S, CB), lambda cb, b: (0, b, cb)),
                  pl.BlockSpec((None, S, CB), lambda cb, b: (1, b, cb)),
                  own, own, own, own, blk,
                  pl.BlockSpec((4, CB), lambda cb, b: (0, cb)),
                  pl.BlockSpec((None, CB, 2 * CB), lambda cb, b: (cb, 0, 0)),
                  vec] + [ANY] * ng,
        out_specs=[pl.BlockSpec((2, S, CB), lambda cb, b: (0, b, cb)),
                   pl.BlockSpec((None, CB, 2 * CB), lambda cb, b: (cb, 0, 0)),
                   pl.BlockSpec((8, CB), lambda cb, b: (0, cb))] + [ANY] * ng,
        out_shape=[jax.ShapeDtypeStruct((2, t, D), BF16),
                   jax.ShapeDtypeStruct((NCB, CB, 2 * CB), F32),
                   jax.ShapeDtypeStruct((8, D), F32)]
        + [jax.ShapeDtypeStruct((4,) + g.shape[1:], g.dtype) for g in give],
        scratch_shapes=[pltpu.VMEM((S, CB), F32), pltpu.VMEM((S, CB), F32), pltpu.VMEM((S, CB), F32),
                        pltpu.VMEM((S + 8, CB), F32), pltpu.VMEM((8, CB), F32),
                        pltpu.SemaphoreType.DMA((ng, 4)), pltpu.SemaphoreType.DMA((ng, 4))],
        compiler_params=_params(("arbitrary", "arbitrary")),
    )(proj, proj, hs, xc_f, gi_f, gr_f, dya, conv_w, wbd, lam, *give)


def _retention_tables():
    log_g = jnp.log1p(-(2.0 ** (-5.0 - jnp.arange(HEADS, dtype=F32))))
    idx = jnp.arange(CH, dtype=F32)
    diff = idx[:, None] - idx[None, :]
    inner = jnp.where(diff >= 0, jnp.exp(jnp.maximum(diff, 0.0)[None] * log_g[:, None, None]), 0.0)
    cross = jnp.exp((idx[None, :] + 1.0) * log_g[:, None])
    state = jnp.exp((CH - 1.0 - idx[None, :]) * log_g[:, None])
    cross = jnp.broadcast_to(cross[:, :, None], (HEADS, CH, DK))
    state = jnp.broadcast_to(state[:, :, None], (HEADS, CH, DK))
    half = DK // 2
    freqs = 10000.0 ** (-jnp.arange(half, dtype=F32) / half)
    ang = jnp.arange(S, dtype=F32)[:, None] * freqs[None, :]
    return inner, cross, state, jnp.cos(ang), jnp.sin(ang)


def _rotate(x, cos, sin):
    half = DK // 2
    x1, x2 = x[:, :half], x[:, half:]
    return jnp.concatenate([x1 * cos - x2 * sin, x1 * sin + x2 * cos], axis=1)


def _rotate_back(d, cos, sin):
    half = DK // 2
    d1, d2 = d[:, :half], d[:, half:]
    return jnp.concatenate([d1 * cos + d2 * sin, d2 * cos - d1 * sin], axis=1)


def _ret_fwd(proj, gain, tables, nb, wp_own):
    t = nb * S
    inner_t, cross_t, state_t, cos_t, sin_t = tables

    def body(q_ref, k_ref, v_ref, gb_ref, gain_ref, dm_ref, cd_ref, sd_ref, cos_ref, sin_ref, wp_ref,
             yb_ref, qr_ref, kr_ref, o_ref, rs_ref, wpg_ref, r_s, send_sems, recv_sems, own_sems):
        b, hd = pl.program_id(0), pl.program_id(1)
        own, first, arrive, forward, others = _gather_copies([wp_ref], [wpg_ref], send_sems, recv_sems, own_sems)

        @pl.when(jnp.logical_and(b == 0, hd == 0))
        def _():
            for cp in own + first:
                cp.start()

        @pl.when(jnp.logical_and(b == nb - 1, hd == 1))
        def _():
            for came, on in zip(arrive, forward):
                came.wait_recv()
                on.start()

        r_s[...] = jnp.zeros_like(r_s)
        chunk_decay = cd_ref[CH - 1:CH, :]

        def chunk(c, carry):
            rows = _rows(c, CH)
            cos, sin = cos_ref[rows, :], sin_ref[rows, :]
            qr = _rotate(q_ref[rows, :], cos, sin).astype(BF16)
            kr = (_rotate(k_ref[rows, :], cos, sin) * (DK ** -0.5)).astype(BF16)
            v = v_ref[rows, :]
            qr_ref[rows, :] = qr
            kr_ref[rows, :] = kr
            r = r_s[...]
            rb = r.astype(BF16)
            rs_ref[c] = rb
            p = (_dot_nt(qr, kr) * dm_ref[...]).astype(BF16)
            o = _dot(p, v.astype(BF16)) + _dot(qr, rb) * cd_ref[...]
            r_s[...] = chunk_decay * r + _dot_tn(kr, (v * sd_ref[...]).astype(BF16))
            o_ref[rows, :] = o
            oc = o - jnp.mean(o, axis=-1, keepdims=True)
            rstd = lax.rsqrt(jnp.mean(oc * oc, axis=-1, keepdims=True) + EPS)
            gb = gb_ref[rows, :]
            yb_ref[rows, :] = (gb * _sigmoid(gb) * (oc * rstd * gain_ref[...])).astype(BF16)
            return carry

        lax.fori_loop(0, NCH, chunk, 0)

        @pl.when(jnp.logical_and(b == nb - 1, hd == HEADS - 1))
        def _():
            for cp in others:
                cp.wait_recv()
            for cp in first + forward:
                cp.wait_send()
            for cp in own:
                cp.wait()

    seg = lambda s: pl.BlockSpec((None, S, DK), lambda b, h: (s, b, h))
    tab = pl.BlockSpec((None, CH, DK), lambda b, h: (h, 0, 0))
    rot = pl.BlockSpec((S, DK // 2), lambda b, h: (0, 0))
    blk = pl.BlockSpec((S, DK), lambda b, h: (b, h))
    return pl.pallas_call(
        body, name="ret_fwd", grid=(nb, HEADS),
        in_specs=[seg(2), seg(3), seg(4), seg(5),
                  pl.BlockSpec((None, 1, DK), lambda b, h: (h, 0, 0)),
                  tab, tab, tab, rot, rot, ANY],
        out_specs=[blk, blk, blk, blk,
                   pl.BlockSpec((None, None, NCH, DK, DK), lambda b, h: (b, h, 0, 0, 0)), ANY],
        out_shape=[jax.ShapeDtypeStruct((t, D), BF16), jax.ShapeDtypeStruct((t, D), BF16),
                   jax.ShapeDtypeStruct((t, D), BF16), jax.ShapeDtypeStruct((t, D), F32),
                   jax.ShapeDtypeStruct((nb, HEADS, NCH, DK, DK), BF16),
                   jax.ShapeDtypeStruct((NDEV,) + wp_own.shape, wp_own.dtype)],
        scratch_shapes=[pltpu.VMEM((DK, DK), F32),
                        pltpu.SemaphoreType.DMA((1, 7)), pltpu.SemaphoreType.DMA((1, 7)), pltpu.SemaphoreType.DMA((1,))],
        compiler_params=_params(("arbitrary", "arbitrary")),
    )(proj, proj, proj, proj, gain, inner_t, cross_t, state_t, cos_t, sin_t, wp_own)


def _ret_bwd(proj, qr, kr, o, rs, dyb, gain, tables, nb, sums):
    t = nb * S
    ns = len(sums)
    inner_t, cross_t, state_t, cos_t, sin_t = tables

    def body(qr_ref, kr_ref, v_ref, gb_ref, o_ref, dyb_ref, rs_ref, gain_ref, dm_ref, cd_ref, sd_ref,
             cos_ref, sin_ref, *rest):
        sum_refs, rest = rest[:ns], rest[ns:]
        dp_ref, dgain_ref = rest[:2]
        part_refs, rest = rest[2:2 + ns], rest[2 + ns:]
        dr_s, send_sems, recv_sems, local_sems = rest
        mine, sends, recvs = _chip_copies(sum_refs, part_refs, send_sems, recv_sems, local_sems)

        @pl.when(jnp.logical_and(pl.program_id(0) == 0, pl.program_id(1) == 0))
        def _():
            for cp in mine + sends:
                cp.start()

        dr_s[...] = jnp.zeros_like(dr_s)
        chunk_decay = cd_ref[CH - 1:CH, :]

        @pl.when(pl.program_id(1) == 0)
        def _():
            dgain_ref[...] = jnp.zeros_like(dgain_ref)

        def chunk(i, carry):
            c = NCH - 1 - i
            rows = _rows(c, CH)
            gain_v = gain_ref[...]
            o_c = o_ref[rows, :]
            oc = o_c - jnp.mean(o_c, axis=-1, keepdims=True)
            rstd = lax.rsqrt(jnp.mean(oc * oc, axis=-1, keepdims=True) + EPS)
            yn = oc * rstd
            gb = gb_ref[rows, :]
            sg = _sigmoid(gb)
            dyb_c = dyb_ref[rows, :]
            dgn = dyb_c * (gb * sg)
            dp_ref[3, rows, :] = (dyb_c * (yn * gain_v) * (sg * (1.0 + gb * (1.0 - sg)))).astype(BF16)
            dgain_ref[...] += jnp.sum(dgn * yn, axis=0, keepdims=True)
            dyn = dgn * gain_v
            do = rstd * (dyn - jnp.mean(dyn, axis=-1, keepdims=True)
                         - yn * jnp.mean(dyn * yn, axis=-1, keepdims=True))
            dob = do.astype(BF16)
            dox = (do * cd_ref[...]).astype(BF16)

            q_c, k_c = qr_ref[rows, :], kr_ref[rows, :]
            v = v_ref[rows, :]
            vb = v.astype(BF16)
            vs = (v * sd_ref[...]).astype(BF16)
            rb = rs_ref[c]
            d_r = dr_s[...]
            drb = d_r.astype(BF16)
            dm = dm_ref[...]
            p = (_dot_nt(q_c, k_c) * dm).astype(BF16)
            dpm = (_dot_nt(dob, vb) * dm).astype(BF16)
            dq = _dot(dpm, k_c) + _dot_nt(dox, rb)
            dk = _dot_tn(dpm, q_c) + _dot_nt(vs, drb)
            dv = _dot_tn(p, dob) + _dot(k_c, drb) * sd_ref[...]
            dr_s[...] = chunk_decay * d_r + _dot_tn(q_c, dox)

            cos, sin = cos_ref[rows, :], sin_ref[rows, :]
            dp_ref[0, rows, :] = _rotate_back(dq, cos, sin).astype(BF16)
            dp_ref[1, rows, :] = (_rotate_back(dk, cos, sin) * (DK ** -0.5)).astype(BF16)
            dp_ref[2, rows, :] = dv.astype(BF16)
            return carry

        lax.fori_loop(0, NCH, chunk, 0)

        @pl.when(jnp.logical_and(pl.program_id(0) == HEADS - 1, pl.program_id(1) == nb - 1))
        def _():
            for cp in recvs:
                cp.wait_recv()
            for cp in sends:
                cp.wait_send()
            for cp in mine:
                cp.wait()

    seg = lambda s: pl.BlockSpec((None, S, DK), lambda h, b: (s, b, h))
    tab = pl.BlockSpec((None, CH, DK), lambda h, b: (h, 0, 0))
    rot = pl.BlockSpec((S, DK // 2), lambda h, b: (0, 0))
    blk = pl.BlockSpec((S, DK), lambda h, b: (b, h))
    one = pl.BlockSpec((None, 1, DK), lambda h, b: (h, 0, 0))
    return pl.pallas_call(
        body, name="ret_bwd", grid=(HEADS, nb),
        in_specs=[blk, blk, seg(4), seg(5), blk, blk,
                  pl.BlockSpec((None, None, NCH, DK, DK), lambda h, b: (b, h, 0, 0, 0)),
                  one, tab, tab, tab, rot, rot] + [ANY] * ns,
        out_specs=[pl.BlockSpec((4, S, DK), lambda h, b: (0, b, h)), one] + [ANY] * ns,
        out_shape=[jax.ShapeDtypeStruct((4, t, D), BF16), jax.ShapeDtypeStruct((HEADS, 1, DK), F32)]
        + [jax.ShapeDtypeStruct(a.shape, a.dtype) for a in sums],
        scratch_shapes=[pltpu.VMEM((DK, DK), F32), pltpu.SemaphoreType.DMA((ns, 3)), pltpu.SemaphoreType.DMA((ns, 3)),
                        pltpu.SemaphoreType.DMA((ns,))],
        compiler_params=_params(("arbitrary", "arbitrary")),
    )(qr, kr, proj, proj, o, dyb, rs, gain, inner_t, cross_t, state_t, cos_t, sin_t, *sums)


def _wblock(k):
    return pl.BlockSpec((NDEV, D // NDEV, D), lambda i: (0, k, 0))


def _tail(ya, yb, proj, x2d, tgt, wg, g_fin):
    t = x2d.shape[0]
    tm = 256

    def body(ya_ref, yb_ref, ma_ref, mb_ref, x_ref, t_ref, wa_ref, wb_ref, wo_ref, g_ref,
             dx2_ref, dya_ref, dyb_ref, dm_ref, mg_ref, doa_ref, dob_ref, gfin_ref, loss_ref):
        i = pl.program_id(0)

        @pl.when(i == 0)
        def _():
            gfin_ref[...] = jnp.zeros_like(gfin_ref)
            loss_ref[...] = jnp.zeros_like(loss_ref)

        wa = wa_ref[...].reshape(D, D)
        wb = wb_ref[...].reshape(D, D)
        wo = wo_ref[...].reshape(D, D)
        out_a = _dot(ya_ref[...], wa)
        out_b = _dot(yb_ref[...], wb)
        sa = _sigmoid(ma_ref[...])
        sb = _sigmoid(mb_ref[...])
        merged = (sa * out_a + sb * out_b).astype(BF16)
        mg_ref[...] = merged
        x2 = x_ref[...] + _dot(merged, wo)
        r2 = lax.rsqrt(jnp.mean(x2 * x2, axis=-1, keepdims=True) + EPS)
        xh = x2 * r2
        g = g_ref[...]
        err = xh * g - t_ref[...]
        loss_ref[...] += jnp.sum(err * err, axis=0, keepdims=True) * (0.5 / D)
        dy = err * (1.0 / D)
        gfin_ref[...] += jnp.sum(dy * xh, axis=0, keepdims=True)
        dxh = dy * g
        dx2 = r2 * (dxh - xh * jnp.mean(dxh * xh, axis=-1, keepdims=True))
        dx2_ref[...] = dx2
        dmerged = _dot_nt(dx2.astype(BF16), wo)
        doa = (sa * dmerged).astype(BF16)
        dob = (sb * dmerged).astype(BF16)
        doa_ref[...] = doa
        dob_ref[...] = dob
        dm_ref[0] = (dmerged * out_a * sa * (1.0 - sa)).astype(BF16)
        dm_ref[1] = (dmerged * out_b * sb * (1.0 - sb)).astype(BF16)
        dya_ref[...] = _dot_nt(doa, wa)
        dyb_ref[...] = _dot_nt(dob, wb)

    row = lambda: pl.BlockSpec((tm, D), lambda i: (i, 0))
    seg = lambda s: pl.BlockSpec((None, tm, D), lambda i: (s, i, 0))
    vec = pl.BlockSpec((1, D), lambda i: (0, 0))
    return pl.pallas_call(
        body, name="tail", grid=(t // tm,),
        in_specs=[row(), row(), seg(6), seg(7), row(), row(), _wblock(0), _wblock(1), _wblock(2), vec],
        out_specs=[row(), row(), row(), pl.BlockSpec((2, tm, D), lambda i: (0, i, 0)),
                   row(), row(), row(), vec, vec],
        out_shape=[jax.ShapeDtypeStruct((t, D), F32), jax.ShapeDtypeStruct((t, D), F32),
                   jax.ShapeDtypeStruct((t, D), F32), jax.ShapeDtypeStruct((2, t, D), BF16),
                   jax.ShapeDtypeStruct((t, D), BF16), jax.ShapeDtypeStruct((t, D), BF16),
                   jax.ShapeDtypeStruct((t, D), BF16), jax.ShapeDtypeStruct((1, D), F32),
                   jax.ShapeDtypeStruct((1, D), F32)],
        compiler_params=_params(("arbitrary",)),
    )(ya, yb, proj, proj, x2d, tgt, wg, wg, wg, g_fin)


def _tail_wgrad(ya, yb, merged, doa, dob, dx2):
    t = ya.shape[0]
    tm = 512

    def body(ya_ref, yb_ref, mg_ref, doa_ref, dob_ref, dx2_ref, ga_ref, gb_ref, go_ref):
        @pl.when(pl.program_id(0) == 0)
        def _():
            ga_ref[...] = jnp.zeros_like(ga_ref)
            gb_ref[...] = jnp.zeros_like(gb_ref)
            go_ref[...] = jnp.zeros_like(go_ref)

        ga_ref[...] += _dot_tn(ya_ref[...], doa_ref[...])
        gb_ref[...] += _dot_tn(yb_ref[...], dob_ref[...])
        go_ref[...] += _dot_tn(mg_ref[...], dx2_ref[...].astype(BF16))

    row = lambda: pl.BlockSpec((tm, D), lambda i: (i, 0))
    full = lambda: pl.BlockSpec((D, D), lambda i: (0, 0))
    return pl.pallas_call(
        body, name="tail_wgrad", grid=(t // tm,),
        in_specs=[row() for _ in range(6)], out_specs=[full(), full(), full()],
        out_shape=[jax.ShapeDtypeStruct((D, D), F32)] * 3,
        compiler_params=_params(("arbitrary",)),
    )(ya, yb, merged, doa, dob, dx2)


def _dproj_specs(tm, j_of, i_of):
    last = lambda j, i, lo, n: (jnp.clip(j - lo, 0, n - 1), i, 0)
    return [pl.BlockSpec((None, tm, D), lambda a, b: last(j_of(a, b), i_of(a, b), 0, 2)),
            pl.BlockSpec((None, tm, D), lambda a, b: last(j_of(a, b), i_of(a, b), 2, 4)),
            pl.BlockSpec((None, tm, D), lambda a, b: last(j_of(a, b), i_of(a, b), 6, 2))]


def _dproj_specs_ordered(tm):
    def spec(lo, n):
        def index(k, i, order_ref):
            seg = order_ref[k]
            mine = jnp.logical_and(seg >= lo, seg < lo + n)
            return jnp.where(mine, seg - lo, 0), jnp.where(mine, i, 0), 0
        return pl.BlockSpec((None, tm, D), index)
    return [spec(0, 2), spec(2, 4), spec(6, 2)]


def _dproj_pick(j, da_ref, db_ref, dc_ref, use):
    @pl.when(j < 2)
    def _():
        use(da_ref[...])

    @pl.when(jnp.logical_and(j >= 2, j < 6))
    def _():
        use(db_ref[...])

    @pl.when(j >= 6)
    def _():
        use(dc_ref[...])


def _rs_schedule(q, c):
    steps = []
    for s in range(3):
        d_a = lax.rem(q + 1 + s, 4)
        d_b = lax.rem(q + 1 + (s + 1) % 3, 4)
        steps.append((jnp.where(c == 0, d_a, d_b), jnp.where(c == 0, d_b, d_a)))
    steps.append((q, q))
    return steps


def _rs_order(q, c):
    order = []
    for keep, give in _rs_schedule(q, c):
        order += [2 * give + 1 - c, 2 * keep + c]
    return jnp.stack(order).astype(jnp.int32)


def _inproj_wgrad_rs(h, dpa, dpb, dpc, order, smalls):
    t = h.shape[0]
    tm = 1024
    nt = t // tm
    nsm = len(smalls)

    def body(order_ref, h_ref, da_ref, db_ref, dc_ref, *rest):
        small_refs, parts_ref, rest = rest[:nsm], rest[nsm], rest[nsm + 1:]
        all_refs, rest = rest[:nsm], rest[nsm:]
        (acc, sib, outb, give_send, give_recv, sum_send, sum_recv, own_sem,
         small_send, small_recv, small_own) = rest
        k, i = pl.program_id(0), pl.program_id(1)
        x, y, c = _place()
        schedule = _rs_schedule(2 * x + y, c)
        own, first, arrive, forward, others = _gather_copies(small_refs, all_refs, small_send, small_recv, small_own)

        @pl.when(jnp.logical_and(k == 0, i == 0))
        def _():
            for cp in own + first:
                cp.start()

        @pl.when(jnp.logical_and(k == 2, i == 0))
        def _():
            for came, on in zip(arrive, forward):
                came.wait_recv()
                on.start()

        def use(d):
            @pl.when(i == 0)
            def _():
                acc[k % 2] = _dot_tn(h_ref[...], d)

            @pl.when(i > 0)
            def _():
                acc[k % 2] += _dot_tn(h_ref[...], d)

        _dproj_pick(order_ref[k], da_ref, db_ref, dc_ref, use)

        def give_copy(s):
            return pltpu.make_async_remote_copy(
                src_ref=acc.at[0], dst_ref=sib.at[s % 2], send_sem=give_send.at[s], recv_sem=give_recv.at[s],
                device_id=(x, y, 1 - c), device_id_type=MESH)

        def sum_copy(s):
            keep = schedule[s][0]
            return pltpu.make_async_remote_copy(
                src_ref=outb.at[s], dst_ref=parts_ref.at[s], send_sem=sum_send.at[s], recv_sem=sum_recv.at[s],
                device_id=(keep // 2, lax.rem(keep, 2), c), device_id_type=MESH)

        own_copy = pltpu.make_async_copy(outb.at[3], parts_ref.at[3], own_sem)

        for s in range(4):
            @pl.when(jnp.logical_and(k == 2 * s, i == nt - 1))
            def _():
                give_copy(s).start()

            @pl.when(jnp.logical_and(k == 2 * s + 1, i == nt - 1))
            def _():
                give_copy(s).wait_recv()
                outb[s] = (acc[1] + sib[s % 2]).astype(BF16)
                give_copy(s).wait_send()
                if s < 3:
                    sum_copy(s).start()
                else:
                    own_copy.start()

        @pl.when(jnp.logical_and(k == NSEG - 1, i == nt - 1))
        def _():
            for s in range(3):
                sum_copy(s).wait_recv()
            for s in range(3):
                sum_copy(s).wait_send()
            own_copy.wait()
            for cp in others:
                cp.wait_recv()
            for cp in first + forward:
                cp.wait_send()
            for cp in own:
                cp.wait()

    return pl.pallas_call(
        body, name="inproj_wgrad_rs",
        grid_spec=pltpu.PrefetchScalarGridSpec(
            num_scalar_prefetch=1, grid=(NSEG, nt),
            in_specs=[pl.BlockSpec((tm, D), lambda k, i, order_ref: (i, 0))] + _dproj_specs_ordered(tm) + [ANY] * nsm,
            out_specs=[ANY] * (1 + nsm),
            scratch_shapes=[pltpu.VMEM((2, D, D), F32), pltpu.VMEM((2, D, D), F32), pltpu.VMEM((4, D, D), BF16),
                            pltpu.SemaphoreType.DMA((4,)), pltpu.SemaphoreType.DMA((4,)),
                            pltpu.SemaphoreType.DMA((3,)), pltpu.SemaphoreType.DMA((3,)),
                            pltpu.SemaphoreType.DMA,
                            pltpu.SemaphoreType.DMA((nsm, 7)), pltpu.SemaphoreType.DMA((nsm, 7)),
                            pltpu.SemaphoreType.DMA((nsm,))]),
        out_shape=[jax.ShapeDtypeStruct((4, D, D), BF16)]
        + [jax.ShapeDtypeStruct((NDEV,) + a.shape, a.dtype) for a in smalls],
        compiler_params=_params(("arbitrary", "arbitrary")),
    )(order, h, dpa, dpb, dpc, *smalls)


def _inproj_dgrad(dpa, dpb, dpc, wg, x2d, dx2, g_in):
    t = x2d.shape[0]
    tm = 1024

    def body(da_ref, db_ref, dc_ref, w_ref, x_ref, dx2_ref, g_ref, gx_ref, gg_ref, acc_s):
        i, j = pl.program_id(0), pl.program_id(1)

        @pl.when(jnp.logical_and(i == 0, j == 0))
        def _():
            gg_ref[...] = jnp.zeros_like(gg_ref)

        @pl.when(j == 0)
        def _():
            acc_s[...] = jnp.zeros_like(acc_s)

        def use(d):
            acc_s[...] += _dot_nt(d, w_ref[...])

        _dproj_pick(j, da_ref, db_ref, dc_ref, use)

        @pl.when(j == NSEG - 1)
        def _():
            x = x_ref[...]
            r = lax.rsqrt(jnp.mean(x * x, axis=-1, keepdims=True) + EPS)
            xh = x * r
            dh = acc_s[...]
            gg_ref[...] += jnp.sum(dh * xh, axis=0, keepdims=True)
            dxh = dh * g_ref[...]
            gx_ref[...] = dx2_ref[...] + r * (dxh - xh * jnp.mean(dxh * xh, axis=-1, keepdims=True))

    row = lambda: pl.BlockSpec((tm, D), lambda i, j: (i, 0))
    vec = pl.BlockSpec((1, D), lambda i, j: (0, 0))
    return pl.pallas_call(
        body, name="inproj_dgrad", grid=(t // tm, NSEG),
        in_specs=_dproj_specs(tm, lambda i, j: j, lambda i, j: i)
        + [pl.BlockSpec((None, D, D), lambda i, j: (j, 0, 0)), row(), row(), vec],
        out_specs=[row(), vec],
        out_shape=[jax.ShapeDtypeStruct((t, D), F32), jax.ShapeDtypeStruct((1, D), F32)],
        scratch_shapes=[pltpu.VMEM((tm, D), F32)],
        compiler_params=_params(("arbitrary", "arbitrary")),
    )(dpa, dpb, dpc, wg, x2d, dx2, g_in)


def _adam_update(g, w, m, v):
    m_new = ADAM_B1 * m + (1.0 - ADAM_B1) * g
    v_new = ADAM_B2 * v + (1.0 - ADAM_B2) * (g * g)
    m_hat = m_new / (1.0 - ADAM_B1 ** ADAM_STEP)
    v_hat = v_new / (1.0 - ADAM_B2 ** ADAM_STEP)
    return -ADAM_LR * (m_hat / (jnp.sqrt(v_hat) + ADAM_EPS) + ADAM_WD * w), m_new, v_new


def _sum_in_order(ref):
    total = ref[0].astype(F32)
    for k in range(1, ref.shape[0]):
        total = total + ref[k].astype(F32)
    return total


def _adamw_small(me, vec_all, gx_all, ga_all, groups):
    flat = [a for grp in groups for a in grp]
    ng = len(groups)
    nshard = D // NDEV

    def body(me_ref, vec_ref, shard_ref, gx_ref, ga_ref, *refs):
        ins, outs = refs[:3 * ng], refs[3 * ng:]
        vec = _sum_in_order(vec_ref)
        shard = _sum_in_order(shard_ref)
        grads = [vec[r:r + 1, :] for r in range(6)]
        grads += [shard[0:4, :], shard[4:8, 0:DK // NDEV], _sum_in_order(gx_ref), _sum_in_order(ga_ref)]
        for n, g in enumerate(grads):
            delta, m_new, v_new = _adam_update(g, ins[3 * n][...], ins[3 * n + 1][...], ins[3 * n + 2][...])
            outs[4 * n][...] = g
            outs[4 * n + 1][...] = delta
            outs[4 * n + 2][...] = m_new
            outs[4 * n + 3][...] = v_new
        outs[4 * ng][...] = jnp.sum(vec[6:7, :], axis=1, keepdims=True)

    full = lambda a: pl.BlockSpec(a.shape, lambda i, me_ref, nd=len(a.shape): (0,) * nd)
    out_shape = [jax.ShapeDtypeStruct(w.shape, F32) for w, _, _ in groups for _ in range(4)]
    out_shape.append(jax.ShapeDtypeStruct((1, 1), F32))
    outs = pl.pallas_call(
        body, name="adamw_small",
        grid_spec=pltpu.PrefetchScalarGridSpec(
            num_scalar_prefetch=1, grid=(1,),
            in_specs=[full(vec_all),
                      pl.BlockSpec((NDEV, 8, nshard), lambda i, me_ref: (0, 1, me_ref[0])),
                      full(gx_all), full(ga_all)] + [full(a) for a in flat],
            out_specs=[full(s) for s in out_shape]),
        out_shape=out_shape,
        compiler_params=_params(("arbitrary",)),
    )(me, vec_all, vec_all, gx_all, ga_all, *flat)
    return [outs[4 * n:4 * n + 4] for n in range(ng)], outs[4 * ng]


def _adamw(name, parts, w, m, v):
    n, rows, cols = parts.shape
    tr = rows if rows <= 256 else 256

    def body(p_ref, w_ref, m_ref, v_ref, g_ref, d_ref, nm_ref, nv_ref):
        g = _sum_in_order(p_ref)
        delta, m_new, v_new = _adam_update(g, w_ref[...], m_ref[...], v_ref[...])
        g_ref[...] = g
        d_ref[...] = delta
        nm_ref[...] = m_new
        nv_ref[...] = v_new

    blk = lambda: pl.BlockSpec((tr, cols), lambda i: (i, 0))
    return pl.pallas_call(
        body, name=name, grid=(rows // tr,),
        in_specs=[pl.BlockSpec((n, tr, cols), lambda i: (0, i, 0)), blk(), blk(), blk()],
        out_specs=[blk(), blk(), blk(), blk()],
        out_shape=[jax.ShapeDtypeStruct((rows, cols), F32)] * 4,
        compiler_params=_params(("arbitrary",)),
    )(parts, w, m, v)


ANY = pl.BlockSpec(memory_space=pl.ANY)


def _place():
    return lax.axis_index("x"), lax.axis_index("y"), lax.axis_index("c")


def _gather_copies(ins, outs, send_sems, recv_sems, own_sems):
    x, y, c = _place()
    me, sibling = (x, y, c), (x, y, 1 - c)
    chips = [(1 - x, y), (x, 1 - y), (1 - x, 1 - y)]
    n = len(ins)

    def copy(a, k, block, to, src=None):
        px, py, pc = block
        dst = outs[a].at[4 * px + 2 * py + pc]
        return pltpu.make_async_remote_copy(
            src_ref=dst if src is None else src, dst_ref=dst,
            send_sem=send_sems.at[a, k], recv_sem=recv_sems.at[a, k], device_id=to, device_id_type=MESH)

    own = [pltpu.make_async_copy(ins[a], outs[a].at[4 * x + 2 * y + c], own_sems.at[a]) for a in range(n)]
    first = []
    for a in range(n):
        first.append(copy(a, 0, me, sibling, src=ins[a]))
        first += [copy(a, 1 + j, me, (*chip, c), src=ins[a]) for j, chip in enumerate(chips)]
    arrive = [copy(a, 1 + j, (*chip, c), me) for j, chip in enumerate(chips) for a in range(n)]
    forward = [copy(a, 4 + j, (*chip, c), sibling) for j, chip in enumerate(chips) for a in range(n)]
    rest = [copy(a, 0, sibling, me) for a in range(n)]
    rest += [copy(a, 4 + j, (*chip, 1 - c), me) for a in range(n) for j, chip in enumerate(chips)]
    return own, first, arrive, forward, rest


def _sibling_copies(ins, outs, send_sems, recv_sems):
    x, y, c = _place()
    return [pltpu.make_async_remote_copy(
        src_ref=ins[a].at[2 * q + 1 - c], dst_ref=outs[a].at[q],
        send_sem=send_sems.at[a, q], recv_sem=recv_sems.at[a, q],
        device_id=(x, y, 1 - c), device_id_type=MESH) for a in range(len(ins)) for q in range(4)]


def _chip_copies(ins, outs, send_sems, recv_sems, local_sems):
    x, y, c = _place()
    my_chip = 2 * x + y
    chips = [(1 - x, y), (x, 1 - y), (1 - x, 1 - y)]
    n = len(ins)
    mine = [pltpu.make_async_copy(ins[a].at[my_chip], outs[a].at[my_chip], local_sems.at[a]) for a in range(n)]
    sends = [pltpu.make_async_remote_copy(
        src_ref=ins[a].at[2 * px + py], dst_ref=outs[a].at[my_chip],
        send_sem=send_sems.at[a, j], recv_sem=recv_sems.at[a, j],
        device_id=(px, py, c), device_id_type=MESH) for a in range(n) for j, (px, py) in enumerate(chips)]
    recvs = [pltpu.make_async_remote_copy(
        src_ref=ins[a].at[my_chip], dst_ref=outs[a].at[2 * px + py],
        send_sem=send_sems.at[a, j], recv_sem=recv_sems.at[a, j],
        device_id=(px, py, c), device_id_type=MESH) for a in range(n) for j, (px, py) in enumerate(chips)]
    return mine, sends, recvs


def _chip_sum(owns, gots, core):
    n = len(owns)
    _, rows, cols = owns[0].shape

    def body(core_ref, *refs):
        for a in range(n):
            refs[2 * n + a][...] = (refs[a][...] + refs[n + a][...]).astype(BF16)

    own_spec = pl.BlockSpec((None, rows, cols), lambda q, core_ref: (2 * q + core_ref[0], 0, 0))
    slab = pl.BlockSpec((None, rows, cols), lambda q, core_ref: (q, 0, 0))
    return pl.pallas_call(
        body, name="chip_sum",
        grid_spec=pltpu.PrefetchScalarGridSpec(
            num_scalar_prefetch=1, grid=(4,),
            in_specs=[own_spec] * n + [slab] * n, out_specs=[slab] * n),
        out_shape=[jax.ShapeDtypeStruct((4, rows, cols), BF16)] * n,
        compiler_params=_params(("arbitrary",)),
    )(core, *owns, *gots)


def _block_diag(w):
    w4 = w.reshape(NCB, 4, 64, 64)
    eye = jnp.eye(4, dtype=w.dtype)
    return (w4[:, :, :, None, :] * eye[None, :, None, :, None]).reshape(NCB, CB, CB)


def _block_diag_back(g):
    g5 = g.reshape(NCB, 4, 64, 4, 64)
    return jnp.stack([g5[:, m, :, m, :] for m in range(4)], axis=1).reshape(16, 64, 64)


def kernel(x, norm_in, w_in, conv_w, conv_b, gate_x_w, gate_x_b, gate_a_w, gate_a_b, lru_lambda, gn_gain, w_proj_a, w_proj_b, w_out, norm_final, loss_target, m_norm_in, m_w_in, m_conv_w, m_conv_b, m_gate_x_w, m_gate_x_b, m_gate_a_w, m_gate_a_b, m_lru_lambda, m_gn_gain, m_w_proj_a, m_w_proj_b, m_w_out, m_norm_final, v_norm_in, v_w_in, v_conv_w, v_conv_b, v_gate_x_w, v_gate_x_b, v_gate_a_w, v_gate_a_b, v_lru_lambda, v_gn_gain, v_w_proj_a, v_w_proj_b, v_w_out, v_norm_final):
    xi, yi, ci = _place()
    me = 4 * xi + 2 * yi + ci
    core = ci.astype(jnp.int32).reshape(1)
    nshard = D // NDEV
    nb = x.shape[0]
    t = nb * S
    x2d = x.reshape(t, D)
    tgt2d = loss_target.reshape(t, D)
    g_final = norm_final.reshape(1, D)
    wbd = jnp.concatenate([_block_diag(gate_x_w[0]), _block_diag(gate_a_w[0])], axis=-1).astype(BF16)
    tables = _retention_tables()

    wp_own = jnp.concatenate([w_proj_a[0], w_proj_b[0], w_out[0]], axis=0).astype(BF16)
    tiny = jnp.concatenate([conv_w[0], jnp.pad(gn_gain[0], ((0, 0), (0, nshard - DK // NDEV)))], axis=0)
    proj, h, wg, tiny_g = _inproj_gather(x2d, norm_in, w_in[0].astype(BF16), tiny, _gather_order(xi, yi, ci))
    conv_w_full = tiny_g[:, 0:4, :].transpose(1, 0, 2).reshape(4, D)
    gain3 = tiny_g[:, 4:8, :DK // NDEV].transpose(1, 0, 2).reshape(HEADS, 1, DK)

    ya, hs, xc, gi, gr = _lru_fwd(proj, conv_w_full, conv_b, wbd, gate_x_b, gate_a_b, lru_lambda, nb)
    yb, qr, kr, o, rs, wpg = _ret_fwd(proj, gain3, tables, nb, wp_own)
    dx2, dya, dyb, dpc, merged, doa, dob, g_fin, loss_vec = _tail(ya, yb, proj, x2d, tgt2d, wpg, g_final)
    g_pa, g_pb, g_out = _tail_wgrad(ya, yb, merged, doa, dob, dx2)

    own = [g.reshape(NDEV, nshard, D) for g in (g_pa, g_pb, g_out)]
    dpa, g_wbd, g_vec, *got = _lru_bwd(proj, hs, xc, gi, gr, dya, conv_w_full, wbd, lru_lambda, nb, own)
    sums = _chip_sum(own, got, core)
    dpb, g_gain, *parts = _ret_bwd(proj, qr, kr, o, rs, dyb, gain3, tables, nb, sums)

    grad_x, g_norm_in = _inproj_dgrad(dpa, dpb, dpc, wg, x2d, dx2, norm_in)
    grad_x = grad_x.reshape(nb, S, D)

    gain_rows = jnp.pad(g_gain.reshape(HEADS, NDEV, DK // NDEV), ((0, 0), (0, 0), (0, nshard - DK // NDEV)))
    vec = jnp.concatenate([g_norm_in, g_vec[0:4], g_fin, loss_vec, jnp.zeros((1, D), F32), g_vec[4:8],
                           gain_rows.reshape(HEADS, D)], axis=0)
    g_gx = _block_diag_back(g_wbd[:, :, :CB]).reshape(D // 2, 128)
    g_ga = _block_diag_back(g_wbd[:, :, CB:]).reshape(D // 2, 128)
    parts_in, vec_all, gx_all, ga_all = _inproj_wgrad_rs(h, dpa, dpb, dpc, _rs_order(2 * xi + yi, ci),
                                                         [vec, g_gx, g_ga])
    gx_all = gx_all.reshape(NDEV, D, 64)
    ga_all = ga_all.reshape(NDEV, D, 64)
    parts = [parts_in] + list(parts)

    big = [("w_in", w_in, m_w_in, v_w_in), ("w_proj_a", w_proj_a, m_w_proj_a, v_w_proj_a),
           ("w_proj_b", w_proj_b, m_w_proj_b, v_w_proj_b), ("w_out", w_out, m_w_out, v_w_out)]
    res = {}
    for k, (nm, w, m, v) in enumerate(big):
        out = _adamw("adamw_" + nm, parts[k], w[0], m[0], v[0])
        res[nm] = [o[None] for o in out]

    row = lambda a: a.reshape(1, D)
    gate = lambda a: a.reshape(D, 64)
    groups = [("norm_in", norm_in, m_norm_in, v_norm_in, row), ("conv_b", conv_b, m_conv_b, v_conv_b, row),
              ("gate_x_b", gate_x_b, m_gate_x_b, v_gate_x_b, row), ("gate_a_b", gate_a_b, m_gate_a_b, v_gate_a_b, row),
              ("lru_lambda", lru_lambda, m_lru_lambda, v_lru_lambda, row),
              ("norm_final", norm_final, m_norm_final, v_norm_final, row),
              ("conv_w", conv_w, m_conv_w, v_conv_w, lambda a: a[0]), ("gn_gain", gn_gain, m_gn_gain, v_gn_gain, lambda a: a[0]),
              ("gate_x_w", gate_x_w, m_gate_x_w, v_gate_x_w, gate), ("gate_a_w", gate_a_w, m_gate_a_w, v_gate_a_w, gate)]
    small_out, loss = _adamw_small(me.astype(jnp.int32).reshape(1), vec_all, gx_all, ga_all,
                                   [tuple(view(a) for a in (w, m, v)) for _, w, m, v, view in groups])
    for (nm, w, _, _, _), out in zip(groups, small_out):
        res[nm] = [o.reshape(w.shape) for o in out]
    loss = loss.reshape(())

    order = ["norm_in", "w_in", "conv_w", "conv_b", "gate_x_w", "gate_x_b", "gate_a_w", "gate_a_b", "lru_lambda",
             "gn_gain", "w_proj_a", "w_proj_b", "w_out", "norm_final"]
    outs = [loss, grad_x]
    for k in range(4):
        outs += [res[nm][k] for nm in order]
    return tuple(outs)
```

```python
import numpy as np

import jax
import jax.numpy as jnp
from jax import lax
from jax.experimental import pallas as pl
from jax.experimental.pallas import tpu as pltpu

F32 = jnp.float32
BF16 = jnp.bfloat16
MESH = pl.DeviceIdType.MESH

D = 1024
S = 2048
NSEG = 8
NDEV = 8
HEADS = 4
DK = 256
CH = 256
NCH = S // CH
CB = 256
NCB = D // CB
RC = 128
SCAN_GROUP = 8
EPS = 1e-6
LRU_C = 8.0
VMEM_LIMIT = 56 * 1024 * 1024

ADAM_LR = 0.001
ADAM_B1 = 0.9
ADAM_B2 = 0.999
ADAM_EPS = 1e-08
ADAM_WD = 0.01
ADAM_STEP = 10


def _params(sem=None):
    return pltpu.CompilerParams(dimension_semantics=sem, vmem_limit_bytes=VMEM_LIMIT)


def _dot(a, b):
    return jnp.dot(a, b, preferred_element_type=F32)


def _dot_nt(a, b):
    return lax.dot_general(a, b, (((1,), (1,)), ((), ())), preferred_element_type=F32)


def _dot_tn(a, b):
    return lax.dot_general(a, b, (((0,), (0,)), ((), ())), preferred_element_type=F32)


def _sigmoid(x):
    return jax.nn.sigmoid(x)


def _expm1_nonpos(x):
    poly = x * (1.0 + x * (0.5 + x * (1.0 / 6.0 + x * (1.0 / 24.0))))
    return jnp.where(x > -0.05, poly, jnp.exp(x) - 1.0)


def _softplus(x):
    return jnp.maximum(x, 0.0) + jnp.log(1.0 + jnp.exp(-jnp.abs(x)))


def _rows(c, n):
    return pl.ds(pl.multiple_of(c * n, n), n)


def _window_before(ref, c, n):
    r0 = c * n
    if ref.dtype == BF16:
        prev = ref[pl.ds(pl.multiple_of(jnp.maximum(r0 - 16, 0), 16), 16), :].astype(F32)[8:, :]
    else:
        prev = ref[pl.ds(pl.multiple_of(jnp.maximum(r0 - 8, 0), 8), 8), :]
    prev = jnp.where(c > 0, prev, 0.0)
    return jnp.concatenate([prev, ref[_rows(c, n), :].astype(F32)], axis=0)


def _shift_down(win, s, n):
    if s == 0:
        return win[8:, :]
    return pltpu.roll(win, s, 0)[8:, :]


def _shift_up(win, s, n):
    if s == 0:
        return win[:n, :]
    return pltpu.roll(win, n + 8 - s, 0)[:n, :]


def _gather_order(x, y, c):
    chips = [(1 - x, y), (x, 1 - y), (1 - x, 1 - y)]
    order = [4 * x + 2 * y + c, 4 * x + 2 * y + 1 - c]
    for px, py in chips:
        order += [4 * px + 2 * py + c, 4 * px + 2 * py + 1 - c]
    return jnp.stack(order).astype(jnp.int32)


def _inproj_gather(x2d, g_in, w_own, tiny_own, order):
    t = x2d.shape[0]
    tm = 1024
    nt = t // tm

    def body(order_ref, x_ref, g_ref, w_own_ref, tiny_own_ref,
             proj_ref, h_ref, wg_ref, tinyg_ref,
             w_all, h_all, send_sems, recv_sems, own_sems, out_sems):
        k, i = pl.program_id(0), pl.program_id(1)
        x, y, c = _place()
        me, sibling = (x, y, c), (x, y, 1 - c)
        chips = [(1 - x, y), (x, 1 - y), (1 - x, 1 - y)]
        srcs = [w_own_ref, tiny_own_ref]
        dsts = [w_all, tinyg_ref]

        def copy(a, n, block, to, own_src=False):
            px, py, pc = block
            dst = dsts[a].at[4 * px + 2 * py + pc]
            return pltpu.make_async_remote_copy(
                src_ref=srcs[a] if own_src else dst, dst_ref=dst,
                send_sem=send_sems.at[a, n], recv_sem=recv_sems.at[a, n], device_id=to, device_id_type=MESH)

        def own_copy(a):
            return pltpu.make_async_copy(srcs[a], dsts[a].at[4 * x + 2 * y + c], own_sems.at[a])

        def keep_copy(n):
            return pltpu.make_async_copy(w_all.at[order_ref[n]], wg_ref.at[order_ref[n]], out_sems.at[n])

        def first_copies(a):
            return [copy(a, 0, me, sibling, True)] + [copy(a, 1 + j, me, (*chip, c), True) for j, chip in enumerate(chips)]

        def at_slot(n):
            return jnp.logical_and(k == n, i == 0)

        @pl.when(at_slot(0))
        def _():
            for a in range(2):
                own_copy(a).start()
            for a in range(2):
                for cp in first_copies(a):
                    cp.start()
            own_copy(0).wait()
            keep_copy(0).start()

        @pl.when(at_slot(1))
        def _():
            copy(0, 0, sibling, me).wait_recv()
            keep_copy(1).start()

        for j, chip in enumerate(chips):
            @pl.when(at_slot(2 + 2 * j))
            def _():
                copy(0, 1 + j, (*chip, c), me).wait_recv()
                copy(0, 4 + j, (*chip, c), sibling).start()
                keep_copy(2 + 2 * j).start()

            @pl.when(at_slot(3 + 2 * j))
            def _():
                copy(0, 4 + j, (*chip, 1 - c), me).wait_recv()
                keep_copy(3 + 2 * j).start()

        rows = pl.ds(pl.multiple_of(i * tm, tm), tm)

        @pl.when(k == 0)
        def _():
            xv = x_ref[...]
            r = lax.rsqrt(jnp.mean(xv * xv, axis=-1, keepdims=True) + EPS)
            hv = (xv * r * g_ref[...]).astype(BF16)
            h_ref[...] = hv
            h_all[rows, :] = hv

        proj_ref[...] = _dot(h_all[rows, :], w_all[order_ref[k]]).astype(BF16)

        @pl.when(jnp.logical_and(k == NSEG - 1, i == nt - 1))
        def _():
            for j, chip in enumerate(chips):
                copy(1, 1 + j, (*chip, c), me).wait_recv()
                copy(1, 4 + j, (*chip, c), sibling).start()
            copy(1, 0, sibling, me).wait_recv()
            for j, chip in enumerate(chips):
                copy(1, 4 + j, (*chip, 1 - c), me).wait_recv()
            for a in range(2):
                for cp in first_copies(a):
                    cp.wait_send()
                for j, chip in enumerate(chips):
                    copy(a, 4 + j, (*chip, c), sibling).wait_send()
            own_copy(1).wait()
            for n in range(NSEG):
                keep_copy(n).wait()

    hold = lambda k, i, order_ref: (jnp.where(k == 0, i, nt - 1), 0)
    return pl.pallas_call(
        body, name="inproj_gather",
        grid_spec=pltpu.PrefetchScalarGridSpec(
            num_scalar_prefetch=1, grid=(NSEG, nt),
            in_specs=[pl.BlockSpec((tm, D), hold),
                      pl.BlockSpec((1, D), lambda k, i, order_ref: (0, 0)),
                      ANY, ANY],
            out_specs=[pl.BlockSpec((None, tm, D), lambda k, i, order_ref: (order_ref[k], i, 0)),
                       pl.BlockSpec((tm, D), hold),
                       ANY, ANY],
            scratch_shapes=[pltpu.VMEM((NDEV, D, D), BF16), pltpu.VMEM((t, D), BF16),
                            pltpu.SemaphoreType.DMA((2, 7)), pltpu.SemaphoreType.DMA((2, 7)),
                            pltpu.SemaphoreType.DMA((2,)), pltpu.SemaphoreType.DMA((NSEG,))]),
        out_shape=[jax.ShapeDtypeStruct((NSEG, t, D), BF16), jax.ShapeDtypeStruct((t, D), BF16),
                   jax.ShapeDtypeStruct((NDEV,) + w_own.shape, BF16),
                   jax.ShapeDtypeStruct((NDEV,) + tiny_own.shape, F32)],
        compiler_params=_params(("arbitrary", "arbitrary")),
    )(order, x2d, g_in, w_own, tiny_own)


def _tile_scan(a, u):
    row = lax.broadcasted_iota(jnp.int32, a.shape, 0)
    for d in (1, 2, 4):
        m = row >= d
        a_sh = pltpu.roll(a, d, 0)
        u_sh = pltpu.roll(u, d, 0)
        u = jnp.where(m, a * u_sh + u, u)
        a = jnp.where(m, a * a_sh, a)
    return a, u


def _tile_scan_rev(a, w):
    row = lax.broadcasted_iota(jnp.int32, a.shape, 0)
    for d in (1, 2, 4):
        m = row < 8 - d
        a_sh = pltpu.roll(a, 8 - d, 0)
        w_sh = pltpu.roll(w, 8 - d, 0)
        w = jnp.where(m, a * w_sh + w, w)
        a = jnp.where(m, a * a_sh, a)
    return a, w


def _lru_gates(xa_ref, c, cw_ref, cb_ref, wbd_ref, bx_ref, ba_ref, sp):
    win = _window_before(xa_ref, c, RC)
    xc = cb_ref[...] + cw_ref[3:4, :] * _shift_down(win, 0, RC)
    for s in (1, 2, 3):
        xc = xc + cw_ref[3 - s:4 - s, :] * _shift_down(win, s, RC)
    z = _dot(xc.astype(BF16), wbd_ref[...])
    gi = _sigmoid(z[:, :CB] + bx_ref[...])
    gr = _sigmoid(z[:, CB:] + ba_ref[...])
    log_a = -LRU_C * gr * sp
    return win, xc, gi, gr, log_a


def _lru_fwd(proj, conv_w, conv_b, wbd, bx, ba, lam, nb):
    t = nb * S

    def body(xa_ref, ga_ref, cw_ref, cb_ref, wbd_ref, bx_ref, ba_ref, lam_ref,
             ya_ref, hs_ref, xc_ref, gi_ref, gr_ref, a_s, u_s):
        sp = _softplus(-lam_ref[...])

        def gates(c, carry):
            _, xc, gi, gr, log_a = _lru_gates(xa_ref, c, cw_ref, cb_ref, wbd_ref, bx_ref, ba_ref, sp)
            rows = _rows(c, RC)
            a_s[rows, :] = jnp.exp(log_a)
            u_s[rows, :] = jnp.sqrt(-_expm1_nonpos(2.0 * log_a)) * (gi * xc)
            xc_ref[rows, :] = xc
            gi_ref[rows, :] = gi
            gr_ref[rows, :] = gr
            return carry

        lax.fori_loop(0, S // RC, gates, 0)

        def scan(g, h):
            for k in range(SCAN_GROUP):
                rows = pl.ds(pl.multiple_of(g * (8 * SCAN_GROUP), 8 * SCAN_GROUP) + 8 * k, 8)
                a_cum, u_cum = _tile_scan(a_s[rows, :], u_s[rows, :])
                hs_ref[rows, :] = u_cum + a_cum * h
                h = u_cum[7:8, :] + a_cum[7:8, :] * h
            return h

        lax.fori_loop(0, S // (8 * SCAN_GROUP), scan, jnp.zeros((1, CB), F32))

        def gate_out(c, carry):
            ga = ga_ref[_rows(c, RC), :].astype(F32)
            ya_ref[_rows(c, RC), :] = (ga * _sigmoid(ga) * hs_ref[_rows(c, RC), :]).astype(BF16)
            return carry

        lax.fori_loop(0, S // RC, gate_out, 0)

    vec = pl.BlockSpec((1, CB), lambda b, cb: (0, cb))
    blk = pl.BlockSpec((S, CB), lambda b, cb: (b, cb))
    return pl.pallas_call(
        body, name="lru_fwd", grid=(nb, NCB),
        in_specs=[pl.BlockSpec((None, S, CB), lambda b, cb: (0, b, cb)),
                  pl.BlockSpec((None, S, CB), lambda b, cb: (1, b, cb)),
                  pl.BlockSpec((4, CB), lambda b, cb: (0, cb)),
                  vec,
                  pl.BlockSpec((None, CB, 2 * CB), lambda b, cb: (cb, 0, 0)),
                  vec, vec, vec],
        out_specs=[blk] + [pl.BlockSpec((None, None, S, CB), lambda b, cb: (b, cb, 0, 0))] * 4,
        out_shape=[jax.ShapeDtypeStruct((t, D), BF16)] + [jax.ShapeDtypeStruct((nb, NCB, S, CB), F32)] * 4,
        scratch_shapes=[pltpu.VMEM((S, CB), F32), pltpu.VMEM((S, CB), F32)],
        compiler_params=_params(("arbitrary", "arbitrary")),
    )(proj, proj, conv_w, conv_b, wbd, bx, ba, lam)


def _lru_bwd(proj, hs, xc_f, gi_f, gr_f, dya, conv_w, wbd, lam, nb, give):
    t = nb * S
    ng = len(give)

    def body(xa_ref, ga_ref, hs_ref, xc_s, gi_s, gr_s, dya_ref, cw_ref, wbd_ref, lam_ref, *rest):
        give_refs, rest = rest[:ng], rest[ng:]
        dp_ref, dwbd_ref, vec_ref = rest[:3]
        got_refs, rest = rest[3:3 + ng], rest[3 + ng:]
        a_s, dl_s, dh_s, dxc_s, acc_s, send_sems, recv_sems = rest
        b = pl.program_id(1)
        exchange = _sibling_copies(give_refs, got_refs, send_sems, recv_sems)

        @pl.when(jnp.logical_and(pl.program_id(0) == 0, b == 0))
        def _():
            for cp in exchange:
                cp.start()

        lam_v = lam_ref[...]
        sp = _softplus(-lam_v)
        acc_s[...] = jnp.zeros_like(acc_s)

        @pl.when(b == 0)
        def _():
            dwbd_ref[...] = jnp.zeros_like(dwbd_ref)
            vec_ref[...] = jnp.zeros_like(vec_ref)

        def gates(c, carry):
            rows = _rows(c, RC)
            a_s[rows, :] = jnp.exp(-LRU_C * gr_s[rows, :] * sp)
            ga = ga_ref[rows, :].astype(F32)
            sg = _sigmoid(ga)
            dya_c = dya_ref[rows, :]
            dl_s[rows, :] = dya_c * (ga * sg)
            dp_ref[1, rows, :] = (dya_c * hs_ref[rows, :] * (sg * (1.0 + ga * (1.0 - sg)))).astype(BF16)
            return carry

        lax.fori_loop(0, S // RC, gates, 0)

        def scan(i, g_in):
            base = pl.multiple_of((S // (8 * SCAN_GROUP) - 1 - i) * (8 * SCAN_GROUP), 8 * SCAN_GROUP)
            row = lax.broadcasted_iota(jnp.int32, (8, CB), 0)
            for k in reversed(range(SCAN_GROUP)):
                rows = pl.ds(base + 8 * k, 8)
                a = a_s[rows, :]
                dl = dl_s[rows, :]
                a_cum, g_loc = _tile_scan_rev(a, a * dl)
                g = g_loc + a_cum * g_in
                dh_s[rows, :] = dl + jnp.where(row < 7, pltpu.roll(g, 7, 0), g_in)
                g_in = g_loc[0:1, :] + a_cum[0:1, :] * g_in
            return g_in

        lax.fori_loop(0, S // (8 * SCAN_GROUP), scan, jnp.zeros((1, CB), F32))

        dxc_s[pl.ds(S, 8), :] = jnp.zeros((8, CB), F32)

        def grads(c, carry):
            rows = _rows(c, RC)
            dh = dh_s[rows, :]
            h_prev = _shift_down(_window_before(hs_ref, c, RC), 1, RC)
            xc, gi, gr, a = xc_s[rows, :], gi_s[rows, :], gr_s[rows, :], a_s[rows, :]
            mult = jnp.sqrt(-_expm1_nonpos(-2.0 * LRU_C * gr * sp))
            dmult = dh * gi * xc
            d_log_a = dh * h_prev * a - dmult * (a * a) / mult
            dzi = dh * mult * xc * gi * (1.0 - gi)
            dzr = d_log_a * (-LRU_C * sp) * gr * (1.0 - gr)
            dz = jnp.concatenate([dzi, dzr], axis=1).astype(BF16)
            dxc_s[rows, :] = dh * mult * gi + _dot_nt(dz, wbd_ref[...])
            dwbd_ref[...] += _dot_tn(xc.astype(BF16), dz)
            acc_s[1:2, :] += jnp.sum(dzi, axis=0, keepdims=True)
            acc_s[2:3, :] += jnp.sum(dzr, axis=0, keepdims=True)
            acc_s[3:4, :] += jnp.sum(d_log_a * (-LRU_C * gr), axis=0, keepdims=True)
            return carry

        lax.fori_loop(0, S // RC, grads, 0)

        def conv_bwd(c, carry):
            rows = _rows(c, RC)
            dwin = dxc_s[pl.ds(pl.multiple_of(c * RC, RC), RC + 8), :]
            dxc = dwin[:RC, :]
            xwin = _window_before(xa_ref, c, RC)
            dxa = cw_ref[3:4, :] * dxc
            acc_s[0:1, :] += jnp.sum(dxc, axis=0, keepdims=True)
            acc_s[7:8, :] += jnp.sum(dxc * _shift_down(xwin, 0, RC), axis=0, keepdims=True)
            for s in (1, 2, 3):
                dxa = dxa + cw_ref[3 - s:4 - s, :] * _shift_up(dwin, s, RC)
                acc_s[7 - s:8 - s, :] += jnp.sum(dxc * _shift_down(xwin, s, RC), axis=0, keepdims=True)
            dp_ref[0, rows, :] = dxa.astype(BF16)
            return carry

        lax.fori_loop(0, S // RC, conv_bwd, 0)

        row = lax.broadcasted_iota(jnp.int32, acc_s.shape, 0)
        vec_ref[...] += jnp.where(row == 3, acc_s[...] * (-_sigmoid(-lam_v)), acc_s[...])

        @pl.when(jnp.logical_and(pl.program_id(0) == NCB - 1, b == nb - 1))
        def _():
            for cp in exchange:
                cp.wait()

    vec = pl.BlockSpec((1, CB), lambda cb, b: (0, cb))
    blk = pl.BlockSpec((S, CB), lambda cb, b: (b, cb))
    own = pl.BlockSpec((None, None, S, CB), lambda cb, b: (b, cb, 0, 0))
    return pl.pallas_call(
        body, name="lru_bwd", grid=(NCB, nb),
        in_specs=[pl.BlockSpec((None, S, CB), lambda cb, b: (0, b, cb)),
                  pl.BlockSpec((None, S, CB), lambda cb, b: (1, b, cb)),
                  own, own, own, own, blk,
                  pl.BlockSpec((4, CB), lambda cb, b: (0, cb)),
                  pl.BlockSpec((None, CB, 2 * CB), lambda cb, b: (cb, 0, 0)),
                  vec] + [ANY] * ng,
        out_specs=[pl.BlockSpec((2, S, CB), lambda cb, b: (0, b, cb)),
                   pl.BlockSpec((None, CB, 2 * CB), lambda cb, b: (cb, 0, 0)),
                   pl.BlockSpec((8, CB), lambda cb, b: (0, cb))] + [ANY] * ng,
        out_shape=[jax.ShapeDtypeStruct((2, t, D), BF16),
                   jax.ShapeDtypeStruct((NCB, CB, 2 * CB), F32),
                   jax.ShapeDtypeStruct((8, D), F32)]
        + [jax.ShapeDtypeStruct((4,) + g.shape[1:], g.dtype) for g in give],
        scratch_shapes=[pltpu.VMEM((S, CB), F32), pltpu.VMEM((S, CB), F32), pltpu.VMEM((S, CB), F32),
                        pltpu.VMEM((S + 8, CB), F32), pltpu.VMEM((8, CB), F32),
                        pltpu.SemaphoreType.DMA((ng, 4)), pltpu.SemaphoreType.DMA((ng, 4))],
        compiler_params=_params(("arbitrary", "arbitrary")),
    )(proj, proj, hs, xc_f, gi_f, gr_f, dya, conv_w, wbd, lam, *give)


def _retention_tables():
    f32 = np.float32
    log_g = np.log1p(-(f32(2.0) ** (f32(-5.0) - np.arange(HEADS, dtype=f32)))).astype(f32)
    idx = np.arange(CH, dtype=f32)
    diff = idx[:, None] - idx[None, :]
    inner = np.where(diff >= 0, np.exp(np.maximum(diff, f32(0.0))[None] * log_g[:, None, None]), f32(0.0)).astype(f32)
    cross = np.exp((idx[None, :] + f32(1.0)) * log_g[:, None]).astype(f32)
    state = np.exp((f32(CH - 1.0) - idx[None, :]) * log_g[:, None]).astype(f32)
    cross = np.ascontiguousarray(np.broadcast_to(cross[:, :, None], (HEADS, CH, DK)))
    state = np.ascontiguousarray(np.broadcast_to(state[:, :, None], (HEADS, CH, DK)))
    half = DK // 2
    freqs = (f32(10000.0) ** (-np.arange(half, dtype=f32) / f32(half))).astype(f32)
    ang = (np.arange(S, dtype=f32)[:, None] * freqs[None, :]).astype(f32)
    return tuple(jnp.asarray(a) for a in (inner, cross, state, np.cos(ang).astype(f32), np.sin(ang).astype(f32)))


def _rotate(x, cos, sin):
    half = DK // 2
    x1, x2 = x[:, :half], x[:, half:]
    return jnp.concatenate([x1 * cos - x2 * sin, x1 * sin + x2 * cos], axis=1)


def _rotate_back(d, cos, sin):
    half = DK // 2
    d1, d2 = d[:, :half], d[:, half:]
    return jnp.concatenate([d1 * cos + d2 * sin, d2 * cos - d1 * sin], axis=1)


def _ret_fwd(proj, gain, tables, nb, wp_own):
    t = nb * S
    inner_t, cross_t, state_t, cos_t, sin_t = tables

    def body(q_ref, k_ref, v_ref, gb_ref, gain_ref, dm_ref, cd_ref, sd_ref, cos_ref, sin_ref, wp_ref,
             yb_ref, qr_ref, kr_ref, o_ref, rs_ref, wpg_ref, r_s, send_sems, recv_sems, own_sems):
        b, hd = pl.program_id(0), pl.program_id(1)
        own, first, arrive, forward, others = _gather_copies([wp_ref], [wpg_ref], send_sems, recv_sems, own_sems)

        @pl.when(jnp.logical_and(b == 0, hd == 0))
        def _():
            for cp in own + first:
                cp.start()

        @pl.when(jnp.logical_and(b == nb - 1, hd == HEADS - 1))
        def _():
            for came, on in zip(arrive, forward):
                came.wait_recv()
                on.start()

        r_s[...] = jnp.zeros_like(r_s)
        chunk_decay = cd_ref[CH - 1:CH, :]

        def chunk(c, carry):
            rows = _rows(c, CH)
            cos, sin = cos_ref[rows, :], sin_ref[rows, :]
            qr = _rotate(q_ref[rows, :].astype(F32), cos, sin).astype(BF16)
            kr = (_rotate(k_ref[rows, :].astype(F32), cos, sin) * (DK ** -0.5)).astype(BF16)
            vb = v_ref[rows, :]
            v = vb.astype(F32)
            qr_ref[rows, :] = qr
            kr_ref[rows, :] = kr
            r = r_s[...]
            rb = r.astype(BF16)
            rs_ref[c] = rb
            p = (_dot_nt(qr, kr) * dm_ref[...]).astype(BF16)
            o = _dot(p, vb) + _dot(qr, rb) * cd_ref[...]
            r_s[...] = chunk_decay * r + _dot_tn(kr, (v * sd_ref[...]).astype(BF16))
            o_ref[rows, :] = o
            oc = o - jnp.mean(o, axis=-1, keepdims=True)
            rstd = lax.rsqrt(jnp.mean(oc * oc, axis=-1, keepdims=True) + EPS)
            gb = gb_ref[rows, :].astype(F32)
            yb_ref[rows, :] = (gb * _sigmoid(gb) * (oc * rstd * gain_ref[...])).astype(BF16)
            return carry

        lax.fori_loop(0, NCH, chunk, 0)

        @pl.when(jnp.logical_and(b == nb - 1, hd == HEADS - 1))
        def _():
            for cp in others:
                cp.wait_recv()
            for cp in first + forward:
                cp.wait_send()
            for cp in own:
                cp.wait()

    seg = lambda s: pl.BlockSpec((None, S, DK), lambda b, h: (s, b, h))
    tab = pl.BlockSpec((None, CH, DK), lambda b, h: (h, 0, 0))
    rot = pl.BlockSpec((S, DK // 2), lambda b, h: (0, 0))
    blk = pl.BlockSpec((S, DK), lambda b, h: (b, h))
    return pl.pallas_call(
        body, name="ret_fwd", grid=(nb, HEADS),
        in_specs=[seg(2), seg(3), seg(4), seg(5),
                  pl.BlockSpec((None, 1, DK), lambda b, h: (h, 0, 0)),
                  tab, tab, tab, rot, rot, ANY],
        out_specs=[blk, blk, blk, blk,
                   pl.BlockSpec((None, None, NCH, DK, DK), lambda b, h: (b, h, 0, 0, 0)), ANY],
        out_shape=[jax.ShapeDtypeStruct((t, D), BF16), jax.ShapeDtypeStruct((t, D), BF16),
                   jax.ShapeDtypeStruct((t, D), BF16), jax.ShapeDtypeStruct((t, D), F32),
                   jax.ShapeDtypeStruct((nb, HEADS, NCH, DK, DK), BF16),
                   jax.ShapeDtypeStruct((NDEV,) + wp_own.shape, wp_own.dtype)],
        scratch_shapes=[pltpu.VMEM((DK, DK), F32),
                        pltpu.SemaphoreType.DMA((1, 7)), pltpu.SemaphoreType.DMA((1, 7)), pltpu.SemaphoreType.DMA((1,))],
        compiler_params=_params(("arbitrary", "arbitrary")),
    )(proj, proj, proj, proj, gain, inner_t, cross_t, state_t, cos_t, sin_t, wp_own)


def _ret_bwd(proj, qr, kr, o, rs, dyb, gain, tables, nb, sums):
    t = nb * S
    ns = len(sums)
    inner_t, cross_t, state_t, cos_t, sin_t = tables

    def body(qr_ref, kr_ref, v_ref, gb_ref, o_ref, dyb_ref, rs_ref, gain_ref, dm_ref, cd_ref, sd_ref,
             cos_ref, sin_ref, *rest):
        sum_refs, rest = rest[:ns], rest[ns:]
        dp_ref, dgain_ref = rest[:2]
        part_refs, rest = rest[2:2 + ns], rest[2 + ns:]
        dr_s, send_sems, recv_sems, local_sems = rest
        mine, sends, recvs = _chip_copies(sum_refs, part_refs, send_sems, recv_sems, local_sems)

        @pl.when(jnp.logical_and(pl.program_id(0) == 0, pl.program_id(1) == 0))
        def _():
            for cp in mine + sends:
                cp.start()

        dr_s[...] = jnp.zeros_like(dr_s)
        chunk_decay = cd_ref[CH - 1:CH, :]

        @pl.when(pl.program_id(1) == 0)
        def _():
            dgain_ref[...] = jnp.zeros_like(dgain_ref)

        def chunk(i, carry):
            c = NCH - 1 - i
            rows = _rows(c, CH)
            gain_v = gain_ref[...]
            o_c = o_ref[rows, :]
            oc = o_c - jnp.mean(o_c, axis=-1, keepdims=True)
            rstd = lax.rsqrt(jnp.mean(oc * oc, axis=-1, keepdims=True) + EPS)
            yn = oc * rstd
            gb = gb_ref[rows, :].astype(F32)
            sg = _sigmoid(gb)
            dyb_c = dyb_ref[rows, :]
            dgn = dyb_c * (gb * sg)
            dp_ref[3, rows, :] = (dyb_c * (yn * gain_v) * (sg * (1.0 + gb * (1.0 - sg)))).astype(BF16)
            dgain_ref[...] += jnp.sum(dgn * yn, axis=0, keepdims=True)
            dyn = dgn * gain_v
            do = rstd * (dyn - jnp.mean(dyn, axis=-1, keepdims=True)
                         - yn * jnp.mean(dyn * yn, axis=-1, keepdims=True))
            dob = do.astype(BF16)
            dox = (do * cd_ref[...]).astype(BF16)

            q_c, k_c = qr_ref[rows, :], kr_ref[rows, :]
            vb = v_ref[rows, :]
            v = vb.astype(F32)
            vs = (v * sd_ref[...]).astype(BF16)
            rb = rs_ref[c]
            d_r = dr_s[...]
            drb = d_r.astype(BF16)
            dm = dm_ref[...]
            p = (_dot_nt(q_c, k_c) * dm).astype(BF16)
            dpm = (_dot_nt(dob, vb) * dm).astype(BF16)
            dq = _dot(dpm, k_c) + _dot_nt(dox, rb)
            dk = _dot_tn(dpm, q_c) + _dot_nt(vs, drb)
            dv = _dot_tn(p, dob) + _dot(k_c, drb) * sd_ref[...]
            dr_s[...] = chunk_decay * d_r + _dot_tn(q_c, dox)

            cos, sin = cos_ref[rows, :], sin_ref[rows, :]
            dp_ref[0, rows, :] = _rotate_back(dq, cos, sin).astype(BF16)
            dp_ref[1, rows, :] = (_rotate_back(dk, cos, sin) * (DK ** -0.5)).astype(BF16)
            dp_ref[2, rows, :] = dv.astype(BF16)
            return carry

        lax.fori_loop(0, NCH, chunk, 0)

        @pl.when(jnp.logical_and(pl.program_id(0) == HEADS - 1, pl.program_id(1) == nb - 1))
        def _():
            for cp in recvs:
                cp.wait_recv()
            for cp in sends:
                cp.wait_send()
            for cp in mine:
                cp.wait()

    seg = lambda s: pl.BlockSpec((None, S, DK), lambda h, b: (s, b, h))
    tab = pl.BlockSpec((None, CH, DK), lambda h, b: (h, 0, 0))
    rot = pl.BlockSpec((S, DK // 2), lambda h, b: (0, 0))
    blk = pl.BlockSpec((S, DK), lambda h, b: (b, h))
    one = pl.BlockSpec((None, 1, DK), lambda h, b: (h, 0, 0))
    return pl.pallas_call(
        body, name="ret_bwd", grid=(HEADS, nb),
        in_specs=[blk, blk, seg(4), seg(5), blk, blk,
                  pl.BlockSpec((None, None, NCH, DK, DK), lambda h, b: (b, h, 0, 0, 0)),
                  one, tab, tab, tab, rot, rot] + [ANY] * ns,
        out_specs=[pl.BlockSpec((4, S, DK), lambda h, b: (0, b, h)), one] + [ANY] * ns,
        out_shape=[jax.ShapeDtypeStruct((4, t, D), BF16), jax.ShapeDtypeStruct((HEADS, 1, DK), F32)]
        + [jax.ShapeDtypeStruct(a.shape, a.dtype) for a in sums],
        scratch_shapes=[pltpu.VMEM((DK, DK), F32), pltpu.SemaphoreType.DMA((ns, 3)), pltpu.SemaphoreType.DMA((ns, 3)),
                        pltpu.SemaphoreType.DMA((ns,))],
        compiler_params=_params(("arbitrary", "arbitrary")),
    )(qr, kr, proj, proj, o, dyb, rs, gain, inner_t, cross_t, state_t, cos_t, sin_t, *sums)


def _wblock(k):
    return pl.BlockSpec((NDEV, D // NDEV, D), lambda i: (0, k, 0))


def _tail(ya, yb, proj, x2d, tgt, wg, g_fin):
    t = x2d.shape[0]
    tm = 256

    def body(ya_ref, yb_ref, ma_ref, mb_ref, x_ref, t_ref, wa_ref, wb_ref, wo_ref, g_ref,
             dx2_ref, dya_ref, dyb_ref, dm_ref, mg_ref, doa_ref, dob_ref, gfin_ref, loss_ref):
        i = pl.program_id(0)

        @pl.when(i == 0)
        def _():
            gfin_ref[...] = jnp.zeros_like(gfin_ref)
            loss_ref[...] = jnp.zeros_like(loss_ref)

        wa = wa_ref[...].reshape(D, D)
        wb = wb_ref[...].reshape(D, D)
        wo = wo_ref[...].reshape(D, D)
        out_a = _dot(ya_ref[...], wa)
        out_b = _dot(yb_ref[...], wb)
        sa = _sigmoid(ma_ref[...].astype(F32))
        sb = _sigmoid(mb_ref[...].astype(F32))
        merged = (sa * out_a + sb * out_b).astype(BF16)
        mg_ref[...] = merged
        x2 = x_ref[...] + _dot(merged, wo)
        r2 = lax.rsqrt(jnp.mean(x2 * x2, axis=-1, keepdims=True) + EPS)
        xh = x2 * r2
        g = g_ref[...]
        err = xh * g - t_ref[...]
        loss_ref[...] += jnp.sum(err * err, axis=0, keepdims=True) * (0.5 / D)
        dy = err * (1.0 / D)
        gfin_ref[...] += jnp.sum(dy * xh, axis=0, keepdims=True)
        dxh = dy * g
        dx2 = r2 * (dxh - xh * jnp.mean(dxh * xh, axis=-1, keepdims=True))
        dx2_ref[...] = dx2
        dmerged = _dot_nt(dx2.astype(BF16), wo)
        doa = (sa * dmerged).astype(BF16)
        dob = (sb * dmerged).astype(BF16)
        doa_ref[...] = doa
        dob_ref[...] = dob
        dm_ref[0] = (dmerged * out_a * sa * (1.0 - sa)).astype(BF16)
        dm_ref[1] = (dmerged * out_b * sb * (1.0 - sb)).astype(BF16)
        dya_ref[...] = _dot_nt(doa, wa)
        dyb_ref[...] = _dot_nt(dob, wb)

    row = lambda: pl.BlockSpec((tm, D), lambda i: (i, 0))
    seg = lambda s: pl.BlockSpec((None, tm, D), lambda i: (s, i, 0))
    vec = pl.BlockSpec((1, D), lambda i: (0, 0))
    return pl.pallas_call(
        body, name="tail", grid=(t // tm,),
        in_specs=[row(), row(), seg(6), seg(7), row(), row(), _wblock(0), _wblock(1), _wblock(2), vec],
        out_specs=[row(), row(), row(), pl.BlockSpec((2, tm, D), lambda i: (0, i, 0)),
                   row(), row(), row(), vec, vec],
        out_shape=[jax.ShapeDtypeStruct((t, D), F32), jax.ShapeDtypeStruct((t, D), F32),
                   jax.ShapeDtypeStruct((t, D), F32), jax.ShapeDtypeStruct((2, t, D), BF16),
                   jax.ShapeDtypeStruct((t, D), BF16), jax.ShapeDtypeStruct((t, D), BF16),
                   jax.ShapeDtypeStruct((t, D), BF16), jax.ShapeDtypeStruct((1, D), F32),
                   jax.ShapeDtypeStruct((1, D), F32)],
        compiler_params=_params(("arbitrary",)),
    )(ya, yb, proj, proj, x2d, tgt, wg, wg, wg, g_fin)


def _tail_wgrad(ya, yb, merged, doa, dob, dx2):
    t = ya.shape[0]
    tm = 512

    def body(ya_ref, yb_ref, mg_ref, doa_ref, dob_ref, dx2_ref, ga_ref, gb_ref, go_ref):
        @pl.when(pl.program_id(0) == 0)
        def _():
            ga_ref[...] = jnp.zeros_like(ga_ref)
            gb_ref[...] = jnp.zeros_like(gb_ref)
            go_ref[...] = jnp.zeros_like(go_ref)

        ga_ref[...] += _dot_tn(ya_ref[...], doa_ref[...])
        gb_ref[...] += _dot_tn(yb_ref[...], dob_ref[...])
        go_ref[...] += _dot_tn(mg_ref[...], dx2_ref[...].astype(BF16))

    row = lambda: pl.BlockSpec((tm, D), lambda i: (i, 0))
    full = lambda: pl.BlockSpec((D, D), lambda i: (0, 0))
    return pl.pallas_call(
        body, name="tail_wgrad", grid=(t // tm,),
        in_specs=[row() for _ in range(6)], out_specs=[full(), full(), full()],
        out_shape=[jax.ShapeDtypeStruct((D, D), F32)] * 3,
        compiler_params=_params(("arbitrary",)),
    )(ya, yb, merged, doa, dob, dx2)


def _dproj_specs(tm, j_of, i_of):
    last = lambda j, i, lo, n: (jnp.clip(j - lo, 0, n - 1), i, 0)
    return [pl.BlockSpec((None, tm, D), lambda a, b: last(j_of(a, b), i_of(a, b), 0, 2)),
            pl.BlockSpec((None, tm, D), lambda a, b: last(j_of(a, b), i_of(a, b), 2, 4)),
            pl.BlockSpec((None, tm, D), lambda a, b: last(j_of(a, b), i_of(a, b), 6, 2))]


def _dproj_specs_ordered(tm):
    def spec(lo, n):
        def index(k, i, order_ref):
            seg = order_ref[k]
            mine = jnp.logical_and(seg >= lo, seg < lo + n)
            return jnp.where(mine, seg - lo, 0), jnp.where(mine, i, 0), 0
        return pl.BlockSpec((None, tm, D), index)
    return [spec(0, 2), spec(2, 4), spec(6, 2)]


def _dproj_pick(j, da_ref, db_ref, dc_ref, use):
    @pl.when(j < 2)
    def _():
        use(da_ref[...])

    @pl.when(jnp.logical_and(j >= 2, j < 6))
    def _():
        use(db_ref[...])

    @pl.when(j >= 6)
    def _():
        use(dc_ref[...])


def _rs_schedule(q, c):
    steps = []
    for s in range(3):
        d_a = lax.rem(q + 1 + s, 4)
        d_b = lax.rem(q + 1 + (s + 1) % 3, 4)
        steps.append((jnp.where(c == 0, d_a, d_b), jnp.where(c == 0, d_b, d_a)))
    steps.append((q, q))
    return steps


def _rs_order(q, c):
    order = []
    for keep, give in _rs_schedule(q, c):
        order += [2 * give + 1 - c, 2 * keep + c]
    return jnp.stack(order).astype(jnp.int32)


def _inproj_wgrad_rs(h, dpa, dpb, dpc, order, smalls):
    t = h.shape[0]
    tm = 1024
    nt = t // tm
    nsm = len(smalls)

    def body(order_ref, h_ref, da_ref, db_ref, dc_ref, *rest):
        small_refs, parts_ref, rest = rest[:nsm], rest[nsm], rest[nsm + 1:]
        all_refs, rest = rest[:nsm], rest[nsm:]
        (acc, sib, outb, give_send, give_recv, sum_send, sum_recv, own_sem,
         small_send, small_recv, small_own) = rest
        k, i = pl.program_id(0), pl.program_id(1)
        x, y, c = _place()
        schedule = _rs_schedule(2 * x + y, c)
        own, first, arrive, forward, others = _gather_copies(small_refs, all_refs, small_send, small_recv, small_own)

        @pl.when(jnp.logical_and(k == 0, i == 0))
        def _():
            for cp in own + first:
                cp.start()

        @pl.when(jnp.logical_and(k == 2, i == 0))
        def _():
            for came, on in zip(arrive, forward):
                came.wait_recv()
                on.start()

        def use(d):
            @pl.when(i == 0)
            def _():
                acc[k % 2] = _dot_tn(h_ref[...], d)

            @pl.when(i > 0)
            def _():
                acc[k % 2] += _dot_tn(h_ref[...], d)

        _dproj_pick(order_ref[k], da_ref, db_ref, dc_ref, use)

        def give_copy(s):
            return pltpu.make_async_remote_copy(
                src_ref=acc.at[0], dst_ref=sib.at[s % 2], send_sem=give_send.at[s], recv_sem=give_recv.at[s],
                device_id=(x, y, 1 - c), device_id_type=MESH)

        def sum_copy(s):
            keep = schedule[s][0]
            return pltpu.make_async_remote_copy(
                src_ref=outb.at[s], dst_ref=parts_ref.at[s], send_sem=sum_send.at[s], recv_sem=sum_recv.at[s],
                device_id=(keep // 2, lax.rem(keep, 2), c), device_id_type=MESH)

        own_copy = pltpu.make_async_copy(outb.at[3], parts_ref.at[3], own_sem)

        for s in range(4):
            @pl.when(jnp.logical_and(k == 2 * s, i == nt - 1))
            def _():
                give_copy(s).start()

            @pl.when(jnp.logical_and(k == 2 * s + 1, i == nt - 1))
            def _():
                give_copy(s).wait_recv()
                outb[s] = (acc[1] + sib[s % 2]).astype(BF16)
                give_copy(s).wait_send()
                if s < 3:
                    sum_copy(s).start()
                else:
                    own_copy.start()

        @pl.when(jnp.logical_and(k == NSEG - 1, i == nt - 1))
        def _():
            for s in range(3):
                sum_copy(s).wait_recv()
            for s in range(3):
                sum_copy(s).wait_send()
            own_copy.wait()
            for cp in others:
                cp.wait_recv()
            for cp in first + forward:
                cp.wait_send()
            for cp in own:
                cp.wait()

    return pl.pallas_call(
        body, name="inproj_wgrad_rs",
        grid_spec=pltpu.PrefetchScalarGridSpec(
            num_scalar_prefetch=1, grid=(NSEG, nt),
            in_specs=[pl.BlockSpec((tm, D), lambda k, i, order_ref: (i, 0))] + _dproj_specs_ordered(tm) + [ANY] * nsm,
            out_specs=[ANY] * (1 + nsm),
            scratch_shapes=[pltpu.VMEM((2, D, D), F32), pltpu.VMEM((2, D, D), F32), pltpu.VMEM((4, D, D), BF16),
                            pltpu.SemaphoreType.DMA((4,)), pltpu.SemaphoreType.DMA((4,)),
                            pltpu.SemaphoreType.DMA((3,)), pltpu.SemaphoreType.DMA((3,)),
                            pltpu.SemaphoreType.DMA,
                            pltpu.SemaphoreType.DMA((nsm, 7)), pltpu.SemaphoreType.DMA((nsm, 7)),
                            pltpu.SemaphoreType.DMA((nsm,))]),
        out_shape=[jax.ShapeDtypeStruct((4, D, D), BF16)]
        + [jax.ShapeDtypeStruct((NDEV,) + a.shape, a.dtype) for a in smalls],
        compiler_params=_params(("arbitrary", "arbitrary")),
    )(order, h, dpa, dpb, dpc, *smalls)


def _inproj_dgrad(dpa, dpb, dpc, wg, x2d, dx2, g_in):
    t = x2d.shape[0]
    tm = 1024

    def body(da_ref, db_ref, dc_ref, w_ref, x_ref, dx2_ref, g_ref, gx_ref, gg_ref, acc_s):
        i, j = pl.program_id(0), pl.program_id(1)

        @pl.when(jnp.logical_and(i == 0, j == 0))
        def _():
            gg_ref[...] = jnp.zeros_like(gg_ref)

        @pl.when(j == 0)
        def _():
            acc_s[...] = jnp.zeros_like(acc_s)

        def use(d):
            acc_s[...] += _dot_nt(d, w_ref[...])

        _dproj_pick(j, da_ref, db_ref, dc_ref, use)

        @pl.when(j == NSEG - 1)
        def _():
            x = x_ref[...]
            r = lax.rsqrt(jnp.mean(x * x, axis=-1, keepdims=True) + EPS)
            xh = x * r
            dh = acc_s[...]
            gg_ref[...] += jnp.sum(dh * xh, axis=0, keepdims=True)
            dxh = dh * g_ref[...]
            gx_ref[...] = dx2_ref[...] + r * (dxh - xh * jnp.mean(dxh * xh, axis=-1, keepdims=True))

    row = lambda: pl.BlockSpec((tm, D), lambda i, j: (i, 0))
    vec = pl.BlockSpec((1, D), lambda i, j: (0, 0))
    return pl.pallas_call(
        body, name="inproj_dgrad", grid=(t // tm, NSEG),
        in_specs=_dproj_specs(tm, lambda i, j: j, lambda i, j: i)
        + [pl.BlockSpec((None, D, D), lambda i, j: (j, 0, 0)), row(), row(), vec],
        out_specs=[row(), vec],
        out_shape=[jax.ShapeDtypeStruct((t, D), F32), jax.ShapeDtypeStruct((1, D), F32)],
        scratch_shapes=[pltpu.VMEM((tm, D), F32)],
        compiler_params=_params(("arbitrary", "arbitrary")),
    )(dpa, dpb, dpc, wg, x2d, dx2, g_in)


def _adam_update(g, w, m, v):
    m_new = ADAM_B1 * m + (1.0 - ADAM_B1) * g
    v_new = ADAM_B2 * v + (1.0 - ADAM_B2) * (g * g)
    m_hat = m_new / (1.0 - ADAM_B1 ** ADAM_STEP)
    v_hat = v_new / (1.0 - ADAM_B2 ** ADAM_STEP)
    return -ADAM_LR * (m_hat / (jnp.sqrt(v_hat) + ADAM_EPS) + ADAM_WD * w), m_new, v_new


def _sum_in_order(ref):
    total = ref[0].astype(F32)
    for k in range(1, ref.shape[0]):
        total = total + ref[k].astype(F32)
    return total


def _adamw_small(me, vec_all, gx_all, ga_all, groups):
    flat = [a for grp in groups for a in grp]
    ng = len(groups)
    nshard = D // NDEV

    def body(me_ref, vec_ref, shard_ref, gx_ref, ga_ref, *refs):
        ins, outs = refs[:3 * ng], refs[3 * ng:]
        vec = _sum_in_order(vec_ref)
        shard = _sum_in_order(shard_ref)
        grads = [vec[r:r + 1, :] for r in range(6)]
        grads += [shard[0:4, :], shard[4:8, 0:DK // NDEV], _sum_in_order(gx_ref), _sum_in_order(ga_ref)]
        for n, g in enumerate(grads):
            delta, m_new, v_new = _adam_update(g, ins[3 * n][...], ins[3 * n + 1][...], ins[3 * n + 2][...])
            outs[4 * n][...] = g
            outs[4 * n + 1][...] = delta
            outs[4 * n + 2][...] = m_new
            outs[4 * n + 3][...] = v_new
        outs[4 * ng][...] = jnp.sum(vec[6:7, :], axis=1, keepdims=True)

    full = lambda a: pl.BlockSpec(a.shape, lambda i, me_ref, nd=len(a.shape): (0,) * nd)
    out_shape = [jax.ShapeDtypeStruct(w.shape, F32) for w, _, _ in groups for _ in range(4)]
    out_shape.append(jax.ShapeDtypeStruct((1, 1), F32))
    outs = pl.pallas_call(
        body, name="adamw_small",
        grid_spec=pltpu.PrefetchScalarGridSpec(
            num_scalar_prefetch=1, grid=(1,),
            in_specs=[full(vec_all),
                      pl.BlockSpec((NDEV, 8, nshard), lambda i, me_ref: (0, 1, me_ref[0])),
                      full(gx_all), full(ga_all)] + [full(a) for a in flat],
            out_specs=[full(s) for s in out_shape]),
        out_shape=out_shape,
        compiler_params=_params(("arbitrary",)),
    )(me, vec_all, vec_all, gx_all, ga_all, *flat)
    return [outs[4 * n:4 * n + 4] for n in range(ng)], outs[4 * ng]


def _adamw(name, parts, w, m, v):
    n, rows, cols = parts.shape
    tr = rows if rows <= 256 else 256

    def body(p_ref, w_ref, m_ref, v_ref, g_ref, d_ref, nm_ref, nv_ref):
        g = _sum_in_order(p_ref)
        delta, m_new, v_new = _adam_update(g, w_ref[...], m_ref[...], v_ref[...])
        g_ref[...] = g
        d_ref[...] = delta
        nm_ref[...] = m_new
        nv_ref[...] = v_new

    blk = lambda: pl.BlockSpec((tr, cols), lambda i: (i, 0))
    return pl.pallas_call(
        body, name=name, grid=(rows // tr,),
        in_specs=[pl.BlockSpec((n, tr, cols), lambda i: (0, i, 0)), blk(), blk(), blk()],
        out_specs=[blk(), blk(), blk(), blk()],
        out_shape=[jax.ShapeDtypeStruct((rows, cols), F32)] * 4,
        compiler_params=_params(("arbitrary",)),
    )(parts, w, m, v)


ANY = pl.BlockSpec(memory_space=pl.ANY)


def _place():
    return lax.axis_index("x"), lax.axis_index("y"), lax.axis_index("c")


def _gather_copies(ins, outs, send_sems, recv_sems, own_sems):
    x, y, c = _place()
    me, sibling = (x, y, c), (x, y, 1 - c)
    chips = [(1 - x, y), (x, 1 - y), (1 - x, 1 - y)]
    n = len(ins)

    def copy(a, k, block, to, src=None):
        px, py, pc = block
        dst = outs[a].at[4 * px + 2 * py + pc]
        return pltpu.make_async_remote_copy(
            src_ref=dst if src is None else src, dst_ref=dst,
            send_sem=send_sems.at[a, k], recv_sem=recv_sems.at[a, k], device_id=to, device_id_type=MESH)

    own = [pltpu.make_async_copy(ins[a], outs[a].at[4 * x + 2 * y + c], own_sems.at[a]) for a in range(n)]
    first = []
    for a in range(n):
        first.append(copy(a, 0, me, sibling, src=ins[a]))
        first += [copy(a, 1 + j, me, (*chip, c), src=ins[a]) for j, chip in enumerate(chips)]
    arrive = [copy(a, 1 + j, (*chip, c), me) for j, chip in enumerate(chips) for a in range(n)]
    forward = [copy(a, 4 + j, (*chip, c), sibling) for j, chip in enumerate(chips) for a in range(n)]
    rest = [copy(a, 0, sibling, me) for a in range(n)]
    rest += [copy(a, 4 + j, (*chip, 1 - c), me) for a in range(n) for j, chip in enumerate(chips)]
    return own, first, arrive, forward, rest


def _sibling_copies(ins, outs, send_sems, recv_sems):
    x, y, c = _place()
    return [pltpu.make_async_remote_copy(
        src_ref=ins[a].at[2 * q + 1 - c], dst_ref=outs[a].at[q],
        send_sem=send_sems.at[a, q], recv_sem=recv_sems.at[a, q],
        device_id=(x, y, 1 - c), device_id_type=MESH) for a in range(len(ins)) for q in range(4)]


def _chip_copies(ins, outs, send_sems, recv_sems, local_sems):
    x, y, c = _place()
    my_chip = 2 * x + y
    chips = [(1 - x, y), (x, 1 - y), (1 - x, 1 - y)]
    n = len(ins)
    mine = [pltpu.make_async_copy(ins[a].at[my_chip], outs[a].at[my_chip], local_sems.at[a]) for a in range(n)]
    sends = [pltpu.make_async_remote_copy(
        src_ref=ins[a].at[2 * px + py], dst_ref=outs[a].at[my_chip],
        send_sem=send_sems.at[a, j], recv_sem=recv_sems.at[a, j],
        device_id=(px, py, c), device_id_type=MESH) for a in range(n) for j, (px, py) in enumerate(chips)]
    recvs = [pltpu.make_async_remote_copy(
        src_ref=ins[a].at[my_chip], dst_ref=outs[a].at[2 * px + py],
        send_sem=send_sems.at[a, j], recv_sem=recv_sems.at[a, j],
        device_id=(px, py, c), device_id_type=MESH) for a in range(n) for j, (px, py) in enumerate(chips)]
    return mine, sends, recvs


def _chip_sum(owns, gots, core):
    n = len(owns)
    _, rows, cols = owns[0].shape

    def body(core_ref, *refs):
        for a in range(n):
            refs[2 * n + a][...] = (refs[a][...] + refs[n + a][...]).astype(BF16)

    own_spec = pl.BlockSpec((None, rows, cols), lambda q, core_ref: (2 * q + core_ref[0], 0, 0))
    slab = pl.BlockSpec((None, rows, cols), lambda q, core_ref: (q, 0, 0))
    return pl.pallas_call(
        body, name="chip_sum",
        grid_spec=pltpu.PrefetchScalarGridSpec(
            num_scalar_prefetch=1, grid=(4,),
            in_specs=[own_spec] * n + [slab] * n, out_specs=[slab] * n),
        out_shape=[jax.ShapeDtypeStruct((4, rows, cols), BF16)] * n,
        compiler_params=_params(("arbitrary",)),
    )(core, *owns, *gots)


def _block_diag(w):
    w4 = w.reshape(NCB, 4, 64, 64)
    eye = jnp.eye(4, dtype=w.dtype)
    return (w4[:, :, :, None, :] * eye[None, :, None, :, None]).reshape(NCB, CB, CB)


def _block_diag_back(g):
    g5 = g.reshape(NCB, 4, 64, 4, 64)
    return jnp.stack([g5[:, m, :, m, :] for m in range(4)], axis=1).reshape(16, 64, 64)


def kernel(x, norm_in, w_in, conv_w, conv_b, gate_x_w, gate_x_b, gate_a_w, gate_a_b, lru_lambda, gn_gain, w_proj_a, w_proj_b, w_out, norm_final, loss_target, m_norm_in, m_w_in, m_conv_w, m_conv_b, m_gate_x_w, m_gate_x_b, m_gate_a_w, m_gate_a_b, m_lru_lambda, m_gn_gain, m_w_proj_a, m_w_proj_b, m_w_out, m_norm_final, v_norm_in, v_w_in, v_conv_w, v_conv_b, v_gate_x_w, v_gate_x_b, v_gate_a_w, v_gate_a_b, v_lru_lambda, v_gn_gain, v_w_proj_a, v_w_proj_b, v_w_out, v_norm_final):
    xi, yi, ci = _place()
    me = 4 * xi + 2 * yi + ci
    core = ci.astype(jnp.int32).reshape(1)
    nshard = D // NDEV
    nb = x.shape[0]
    t = nb * S
    x2d = x.reshape(t, D)
    tgt2d = loss_target.reshape(t, D)
    g_final = norm_final.reshape(1, D)
    wbd = jnp.concatenate([_block_diag(gate_x_w[0]), _block_diag(gate_a_w[0])], axis=-1).astype(BF16)
    tables = _retention_tables()

    wp_own = jnp.concatenate([w_proj_a[0], w_proj_b[0], w_out[0]], axis=0).astype(BF16)
    tiny = jnp.concatenate([conv_w[0], jnp.pad(gn_gain[0], ((0, 0), (0, nshard - DK // NDEV)))], axis=0)
    proj, h, wg, tiny_g = _inproj_gather(x2d, norm_in, w_in[0].astype(BF16), tiny, _gather_order(xi, yi, ci))
    conv_w_full = tiny_g[:, 0:4, :].transpose(1, 0, 2).reshape(4, D)
    gain3 = tiny_g[:, 4:8, :DK // NDEV].transpose(1, 0, 2).reshape(HEADS, 1, DK)

    ya, hs, xc, gi, gr = _lru_fwd(proj, conv_w_full, conv_b, wbd, gate_x_b, gate_a_b, lru_lambda, nb)
    yb, qr, kr, o, rs, wpg = _ret_fwd(proj, gain3, tables, nb, wp_own)
    dx2, dya, dyb, dpc, merged, doa, dob, g_fin, loss_vec = _tail(ya, yb, proj, x2d, tgt2d, wpg, g_final)
    g_pa, g_pb, g_out = _tail_wgrad(ya, yb, merged, doa, dob, dx2)

    own = [g.reshape(NDEV, nshard, D) for g in (g_pa, g_pb, g_out)]
    dpa, g_wbd, g_vec, *got = _lru_bwd(proj, hs, xc, gi, gr, dya, conv_w_full, wbd, lru_lambda, nb, own)
    sums = _chip_sum(own, got, core)
    dpb, g_gain, *parts = _ret_bwd(proj, qr, kr, o, rs, dyb, gain3, tables, nb, sums)

    grad_x, g_norm_in = _inproj_dgrad(dpa, dpb, dpc, wg, x2d, dx2, norm_in)
    grad_x = grad_x.reshape(nb, S, D)

    gain_rows = jnp.pad(g_gain.reshape(HEADS, NDEV, DK // NDEV), ((0, 0), (0, 0), (0, nshard - DK // NDEV)))
    vec = jnp.concatenate([g_norm_in, g_vec[0:4], g_fin, loss_vec, jnp.zeros((1, D), F32), g_vec[4:8],
                           gain_rows.reshape(HEADS, D)], axis=0)
    g_gx = _block_diag_back(g_wbd[:, :, :CB]).reshape(D // 2, 128)
    g_ga = _block_diag_back(g_wbd[:, :, CB:]).reshape(D // 2, 128)
    parts_in, vec_all, gx_all, ga_all = _inproj_wgrad_rs(h, dpa, dpb, dpc, _rs_order(2 * xi + yi, ci),
                                                         [vec, g_gx, g_ga])
    gx_all = gx_all.reshape(NDEV, D, 64)
    ga_all = ga_all.reshape(NDEV, D, 64)
    parts = [parts_in] + list(parts)

    big = [("w_in", w_in, m_w_in, v_w_in), ("w_proj_a", w_proj_a, m_w_proj_a, v_w_proj_a),
           ("w_proj_b", w_proj_b, m_w_proj_b, v_w_proj_b), ("w_out", w_out, m_w_out, v_w_out)]
    res = {}
    for k, (nm, w, m, v) in enumerate(big):
        out = _adamw("adamw_" + nm, parts[k], w[0], m[0], v[0])
        res[nm] = [o[None] for o in out]

    row = lambda a: a.reshape(1, D)
    gate = lambda a: a.reshape(D, 64)
    groups = [("norm_in", norm_in, m_norm_in, v_norm_in, row), ("conv_b", conv_b, m_conv_b, v_conv_b, row),
              ("gate_x_b", gate_x_b, m_gate_x_b, v_gate_x_b, row), ("gate_a_b", gate_a_b, m_gate_a_b, v_gate_a_b, row),
              ("lru_lambda", lru_lambda, m_lru_lambda, v_lru_lambda, row),
              ("norm_final", norm_final, m_norm_final, v_norm_final, row),
              ("conv_w", conv_w, m_conv_w, v_conv_w, lambda a: a[0]), ("gn_gain", gn_gain, m_gn_gain, v_gn_gain, lambda a: a[0]),
              ("gate_x_w", gate_x_w, m_gate_x_w, v_gate_x_w, gate), ("gate_a_w", gate_a_w, m_gate_a_w, v_gate_a_w, gate)]
    small_out, loss = _adamw_small(me.astype(jnp.int32).reshape(1), vec_all, gx_all, ga_all,
                                   [tuple(view(a) for a in (w, m, v)) for _, w, m, v, view in groups])
    for (nm, w, _, _, _), out in zip(groups, small_out):
        res[nm] = [o.reshape(w.shape) for o in out]
    loss = loss.reshape(())

    order = ["norm_in", "w_in", "conv_w", "conv_b", "gate_x_w", "gate_x_b", "gate_a_w", "gate_a_b", "lru_lambda",
             "gn_gain", "w_proj_a", "w_proj_b", "w_out", "norm_final"]
    outs = [loss, grad_x]
    for k in range(4):
        outs += [res[nm][k] for nm in order]
    return tuple(outs)
```

```python
import numpy as np

import jax
import jax.numpy as jnp
from jax import lax
from jax.experimental import pallas as pl
from jax.experimental.pallas import tpu as pltpu

F32 = jnp.float32
BF16 = jnp.bfloat16
MESH = pl.DeviceIdType.MESH

D = 1024
S = 2048
NSEG = 8
NDEV = 8
HEADS = 4
DK = 256
CH = 256
NCH = S // CH
CB = 256
NCB = D // CB
RC = 128
SCAN_GROUP = 8
EPS = 1e-6
LRU_C = 8.0
VMEM_LIMIT = 56 * 1024 * 1024

ADAM_LR = 0.001
ADAM_B1 = 0.9
ADAM_B2 = 0.999
ADAM_EPS = 1e-08
ADAM_WD = 0.01
ADAM_STEP = 10


def _params(sem=None):
    return pltpu.CompilerParams(dimension_semantics=sem, vmem_limit_bytes=VMEM_LIMIT)


def _dot(a, b):
    return jnp.dot(a, b, preferred_element_type=F32)


def _dot_nt(a, b):
    return lax.dot_general(a, b, (((1,), (1,)), ((), ())), preferred_element_type=F32)


def _dot_tn(a, b):
    return lax.dot_general(a, b, (((0,), (0,)), ((), ())), preferred_element_type=F32)


def _sigmoid(x):
    return jax.nn.sigmoid(x)


def _expm1_nonpos(x):
    poly = x * (1.0 + x * (0.5 + x * (1.0 / 6.0 + x * (1.0 / 24.0))))
    return jnp.where(x > -0.05, poly, jnp.exp(x) - 1.0)


def _softplus(x):
    return jnp.maximum(x, 0.0) + jnp.log(1.0 + jnp.exp(-jnp.abs(x)))


def _rows(c, n):
    return pl.ds(pl.multiple_of(c * n, n), n)


def _window_before(ref, c, n):
    r0 = c * n
    if ref.dtype == BF16:
        prev = ref[pl.ds(pl.multiple_of(jnp.maximum(r0 - 16, 0), 16), 16), :].astype(F32)[8:, :]
    else:
        prev = ref[pl.ds(pl.multiple_of(jnp.maximum(r0 - 8, 0), 8), 8), :]
    prev = jnp.where(c > 0, prev, 0.0)
    return jnp.concatenate([prev, ref[_rows(c, n), :].astype(F32)], axis=0)


def _shift_down(win, s, n):
    if s == 0:
        return win[8:, :]
    return pltpu.roll(win, s, 0)[8:, :]


def _shift_up(win, s, n):
    if s == 0:
        return win[:n, :]
    return pltpu.roll(win, n + 8 - s, 0)[:n, :]


def _gather_order(x, y, c):
    chips = [(1 - x, y), (x, 1 - y), (1 - x, 1 - y)]
    order = [4 * x + 2 * y + c, 4 * x + 2 * y + 1 - c]
    for px, py in chips:
        order += [4 * px + 2 * py + c, 4 * px + 2 * py + 1 - c]
    return jnp.stack(order).astype(jnp.int32)


def _inproj_gather(x2d, g_in, w_own, tiny_own, order):
    t = x2d.shape[0]
    tm = 1024
    nt = t // tm

    def body(order_ref, x_ref, g_ref, w_own_ref, tiny_own_ref,
             proj_ref, h_ref, wg_ref, tinyg_ref,
             w_all, h_all, send_sems, recv_sems, own_sems, out_sems):
        k, i = pl.program_id(0), pl.program_id(1)
        x, y, c = _place()
        me, sibling = (x, y, c), (x, y, 1 - c)
        chips = [(1 - x, y), (x, 1 - y), (1 - x, 1 - y)]
        srcs = [w_own_ref, tiny_own_ref]
        dsts = [w_all, tinyg_ref]

        def copy(a, n, block, to, own_src=False):
            px, py, pc = block
            dst = dsts[a].at[4 * px + 2 * py + pc]
            return pltpu.make_async_remote_copy(
                src_ref=srcs[a] if own_src else dst, dst_ref=dst,
                send_sem=send_sems.at[a, n], recv_sem=recv_sems.at[a, n], device_id=to, device_id_type=MESH)

        def own_copy(a):
            return pltpu.make_async_copy(srcs[a], dsts[a].at[4 * x + 2 * y + c], own_sems.at[a])

        def keep_copy(n):
            return pltpu.make_async_copy(w_all.at[order_ref[n]], wg_ref.at[order_ref[n]], out_sems.at[n])

        def first_copies(a):
            return [copy(a, 0, me, sibling, True)] + [copy(a, 1 + j, me, (*chips[j], c), True) for j in (2, 0, 1)]

        def at_slot(n):
            return jnp.logical_and(k == n, i == 0)

        @pl.when(at_slot(0))
        def _():
            for a in range(2):
                own_copy(a).start()
            for a in range(2):
                for cp in first_copies(a):
                    cp.start()
            own_copy(0).wait()
            keep_copy(0).start()

        @pl.when(at_slot(1))
        def _():
            copy(0, 0, sibling, me).wait_recv()
            keep_copy(1).start()

        for j, chip in enumerate(chips):
            @pl.when(at_slot(2 + 2 * j))
            def _():
                copy(0, 1 + j, (*chip, c), me).wait_recv()
                copy(0, 4 + j, (*chip, c), sibling).start()
                keep_copy(2 + 2 * j).start()

            @pl.when(at_slot(3 + 2 * j))
            def _():
                copy(0, 4 + j, (*chip, 1 - c), me).wait_recv()
                keep_copy(3 + 2 * j).start()

        rows = pl.ds(pl.multiple_of(i * tm, tm), tm)

        @pl.when(k == 0)
        def _():
            xv = x_ref[...]
            r = lax.rsqrt(jnp.mean(xv * xv, axis=-1, keepdims=True) + EPS)
            hv = (xv * r * g_ref[...]).astype(BF16)
            h_ref[...] = hv
            h_all[rows, :] = hv

        proj_ref[...] = _dot(h_all[rows, :], w_all[order_ref[k]]).astype(BF16)

        @pl.when(jnp.logical_and(k == NSEG - 1, i == nt - 1))
        def _():
            for j, chip in enumerate(chips):
                copy(1, 1 + j, (*chip, c), me).wait_recv()
                copy(1, 4 + j, (*chip, c), sibling).start()
            copy(1, 0, sibling, me).wait_recv()
            for j, chip in enumerate(chips):
                copy(1, 4 + j, (*chip, 1 - c), me).wait_recv()
            for a in range(2):
                for cp in first_copies(a):
                    cp.wait_send()
                for j, chip in enumerate(chips):
                    copy(a, 4 + j, (*chip, c), sibling).wait_send()
            own_copy(1).wait()
            for n in range(NSEG):
                keep_copy(n).wait()

    hold = lambda k, i, order_ref: (jnp.where(k == 0, i, nt - 1), 0)
    return pl.pallas_call(
        body, name="inproj_gather",
        grid_spec=pltpu.PrefetchScalarGridSpec(
            num_scalar_prefetch=1, grid=(NSEG, nt),
            in_specs=[pl.BlockSpec((tm, D), hold),
                      pl.BlockSpec((1, D), lambda k, i, order_ref: (0, 0)),
                      ANY, ANY],
            out_specs=[pl.BlockSpec((None, tm, D), lambda k, i, order_ref: (order_ref[k], i, 0)),
                       pl.BlockSpec((tm, D), hold),
                       ANY, ANY],
            scratch_shapes=[pltpu.VMEM((NDEV, D, D), BF16), pltpu.VMEM((t, D), BF16),
                            pltpu.SemaphoreType.DMA((2, 7)), pltpu.SemaphoreType.DMA((2, 7)),
                            pltpu.SemaphoreType.DMA((2,)), pltpu.SemaphoreType.DMA((NSEG,))]),
        out_shape=[jax.ShapeDtypeStruct((NSEG, t, D), BF16), jax.ShapeDtypeStruct((t, D), BF16),
                   jax.ShapeDtypeStruct((NDEV,) + w_own.shape, BF16),
                   jax.ShapeDtypeStruct((NDEV,) + tiny_own.shape, F32)],
        compiler_params=_params(("arbitrary", "arbitrary")),
    )(order, x2d, g_in, w_own, tiny_own)


def _tile_scan(a, u):
    row = lax.broadcasted_iota(jnp.int32, a.shape, 0)
    for d in (1, 2, 4):
        m = row >= d
        a_sh = pltpu.roll(a, d, 0)
        u_sh = pltpu.roll(u, d, 0)
        u = jnp.where(m, a * u_sh + u, u)
        a = jnp.where(m, a * a_sh, a)
    return a, u


def _tile_scan_rev(a, w):
    row = lax.broadcasted_iota(jnp.int32, a.shape, 0)
    for d in (1, 2, 4):
        m = row < 8 - d
        a_sh = pltpu.roll(a, 8 - d, 0)
        w_sh = pltpu.roll(w, 8 - d, 0)
        w = jnp.where(m, a * w_sh + w, w)
        a = jnp.where(m, a * a_sh, a)
    return a, w


def _lru_gates(xa_ref, c, cw_ref, cb_ref, wbd_ref, bx_ref, ba_ref, sp):
    win = _window_before(xa_ref, c, RC)
    xc = cb_ref[...] + cw_ref[3:4, :] * _shift_down(win, 0, RC)
    for s in (1, 2, 3):
        xc = xc + cw_ref[3 - s:4 - s, :] * _shift_down(win, s, RC)
    z = _dot(xc.astype(BF16), wbd_ref[...])
    gi = _sigmoid(z[:, :CB] + bx_ref[...])
    gr = _sigmoid(z[:, CB:] + ba_ref[...])
    log_a = -LRU_C * gr * sp
    return win, xc, gi, gr, log_a


def _lru_fwd(proj, conv_w, conv_b, wbd, bx, ba, lam, nb):
    t = nb * S

    def body(xa_ref, ga_ref, cw_ref, cb_ref, wbd_ref, bx_ref, ba_ref, lam_ref,
             ya_ref, hs_ref, xc_ref, gi_ref, gr_ref, a_s, u_s):
        sp = _softplus(-lam_ref[...])

        def gates(c, carry):
            _, xc, gi, gr, log_a = _lru_gates(xa_ref, c, cw_ref, cb_ref, wbd_ref, bx_ref, ba_ref, sp)
            rows = _rows(c, RC)
            a_s[rows, :] = jnp.exp(log_a)
            u_s[rows, :] = jnp.sqrt(-_expm1_nonpos(2.0 * log_a)) * (gi * xc)
            xc_ref[rows, :] = xc
            gi_ref[rows, :] = gi
            gr_ref[rows, :] = gr
            return carry

        lax.fori_loop(0, S // RC, gates, 0)

        def scan(g, h):
            for k in range(SCAN_GROUP):
                rows = pl.ds(pl.multiple_of(g * (8 * SCAN_GROUP), 8 * SCAN_GROUP) + 8 * k, 8)
                a_cum, u_cum = _tile_scan(a_s[rows, :], u_s[rows, :])
                hs_ref[rows, :] = u_cum + a_cum * h
                h = u_cum[7:8, :] + a_cum[7:8, :] * h
            return h

        lax.fori_loop(0, S // (8 * SCAN_GROUP), scan, jnp.zeros((1, CB), F32))

        def gate_out(c, carry):
            ga = ga_ref[_rows(c, RC), :].astype(F32)
            ya_ref[_rows(c, RC), :] = (ga * _sigmoid(ga) * hs_ref[_rows(c, RC), :]).astype(BF16)
            return carry

        lax.fori_loop(0, S // RC, gate_out, 0)

    vec = pl.BlockSpec((1, CB), lambda b, cb: (0, cb))
    blk = pl.BlockSpec((S, CB), lambda b, cb: (b, cb))
    return pl.pallas_call(
        body, name="lru_fwd", grid=(nb, NCB),
        in_specs=[pl.BlockSpec((None, S, CB), lambda b, cb: (0, b, cb)),
                  pl.BlockSpec((None, S, CB), lambda b, cb: (1, b, cb)),
                  pl.BlockSpec((4, CB), lambda b, cb: (0, cb)),
                  vec,
                  pl.BlockSpec((None, CB, 2 * CB), lambda b, cb: (cb, 0, 0)),
                  vec, vec, vec],
        out_specs=[blk] + [pl.BlockSpec((None, None, S, CB), lambda b, cb: (b, cb, 0, 0))] * 4,
        out_shape=[jax.ShapeDtypeStruct((t, D), BF16)] + [jax.ShapeDtypeStruct((nb, NCB, S, CB), F32)] * 4,
        scratch_shapes=[pltpu.VMEM((S, CB), F32), pltpu.VMEM((S, CB), F32)],
        compiler_params=_params(("arbitrary", "arbitrary")),
    )(proj, proj, conv_w, conv_b, wbd, bx, ba, lam)


def _lru_bwd(proj, hs, xc_f, gi_f, gr_f, dya, conv_w, wbd, lam, nb, give):
    t = nb * S
    ng = len(give)

    def body(xa_ref, ga_ref, hs_ref, xc_s, gi_s, gr_s, dya_ref, cw_ref, wbd_ref, lam_ref, *rest):
        give_refs, rest = rest[:ng], rest[ng:]
        dp_ref, dwbd_ref, vec_ref = rest[:3]
        got_refs, rest = rest[3:3 + ng], rest[3 + ng:]
        a_s, dl_s, dh_s, dxc_s, acc_s, send_sems, recv_sems = rest
        b = pl.program_id(1)
        exchange = _sibling_copies(give_refs, got_refs, send_sems, recv_sems)

        @pl.when(jnp.logical_and(pl.program_id(0) == 0, b == 0))
        def _():
            for cp in exchange:
                cp.start()

        lam_v = lam_ref[...]
        sp = _softplus(-lam_v)
        acc_s[...] = jnp.zeros_like(acc_s)

        @pl.when(b == 0)
        def _():
            dwbd_ref[...] = jnp.zeros_like(dwbd_ref)
            vec_ref[...] = jnp.zeros_like(vec_ref)

        def gates(c, carry):
            rows = _rows(c, RC)
            a_s[rows, :] = jnp.exp(-LRU_C * gr_s[rows, :] * sp)
            ga = ga_ref[rows, :].astype(F32)
            sg = _sigmoid(ga)
            dya_c = dya_ref[rows, :]
            dl_s[rows, :] = dya_c * (ga * sg)
            dp_ref[1, rows, :] = (dya_c * hs_ref[rows, :] * (sg * (1.0 + ga * (1.0 - sg)))).astype(BF16)
            return carry

        lax.fori_loop(0, S // RC, gates, 0)

        def scan(i, g_in):
            base = pl.multiple_of((S // (8 * SCAN_GROUP) - 1 - i) * (8 * SCAN_GROUP), 8 * SCAN_GROUP)
            row = lax.broadcasted_iota(jnp.int32, (8, CB), 0)
            for k in reversed(range(SCAN_GROUP)):
                rows = pl.ds(base + 8 * k, 8)
                a = a_s[rows, :]
                dl = dl_s[rows, :]
                a_cum, g_loc = _tile_scan_rev(a, a * dl)
                g = g_loc + a_cum * g_in
                dh_s[rows, :] = dl + jnp.where(row < 7, pltpu.roll(g, 7, 0), g_in)
                g_in = g_loc[0:1, :] + a_cum[0:1, :] * g_in
            return g_in

        lax.fori_loop(0, S // (8 * SCAN_GROUP), scan, jnp.zeros((1, CB), F32))

        dxc_s[pl.ds(S, 8), :] = jnp.zeros((8, CB), F32)

        def grads(c, carry):
            rows = _rows(c, RC)
            dh = dh_s[rows, :]
            h_prev = _shift_down(_window_before(hs_ref, c, RC), 1, RC)
            xc, gi, gr, a = xc_s[rows, :], gi_s[rows, :], gr_s[rows, :], a_s[rows, :]
            mult = jnp.sqrt(-_expm1_nonpos(-2.0 * LRU_C * gr * sp))
            dmult = dh * gi * xc
            d_log_a = dh * h_prev * a - dmult * (a * a) / mult
            dzi = dh * mult * xc * gi * (1.0 - gi)
            dzr = d_log_a * (-LRU_C * sp) * gr * (1.0 - gr)
            dz = jnp.concatenate([dzi, dzr], axis=1).astype(BF16)
            dxc_s[rows, :] = dh * mult * gi + _dot_nt(dz, wbd_ref[...])
            dwbd_ref[...] += _dot_tn(xc.astype(BF16), dz)
            acc_s[1:2, :] += jnp.sum(dzi, axis=0, keepdims=True)
            acc_s[2:3, :] += jnp.sum(dzr, axis=0, keepdims=True)
            acc_s[3:4, :] += jnp.sum(d_log_a * (-LRU_C * gr), axis=0, keepdims=True)
            return carry

        lax.fori_loop(0, S // RC, grads, 0)

        def conv_bwd(c, carry):
            rows = _rows(c, RC)
            dwin = dxc_s[pl.ds(pl.multiple_of(c * RC, RC), RC + 8), :]
            dxc = dwin[:RC, :]
            xwin = _window_before(xa_ref, c, RC)
            dxa = cw_ref[3:4, :] * dxc
            acc_s[0:1, :] += jnp.sum(dxc, axis=0, keepdims=True)
            acc_s[7:8, :] += jnp.sum(dxc * _shift_down(xwin, 0, RC), axis=0, keepdims=True)
            for s in (1, 2, 3):
                dxa = dxa + cw_ref[3 - s:4 - s, :] * _shift_up(dwin, s, RC)
                acc_s[7 - s:8 - s, :] += jnp.sum(dxc * _shift_down(xwin, s, RC), axis=0, keepdims=True)
            dp_ref[0, rows, :] = dxa.astype(BF16)
            return carry

        lax.fori_loop(0, S // RC, conv_bwd, 0)

        row = lax.broadcasted_iota(jnp.int32, acc_s.shape, 0)
        vec_ref[...] += jnp.where(row == 3, acc_s[...] * (-_sigmoid(-lam_v)), acc_s[...])

        @pl.when(jnp.logical_and(pl.program_id(0) == NCB - 1, b == nb - 1))
        def _():
            for cp in exchange:
                cp.wait()

    vec = pl.BlockSpec((1, CB), lambda cb, b: (0, cb))
    blk = pl.BlockSpec((S, CB), lambda cb, b: (b, cb))
    own = pl.BlockSpec((None, None, S, CB), lambda cb, b: (b, cb, 0, 0))
    return pl.pallas_call(
        body, name="lru_bwd", grid=(NCB, nb),
        in_specs=[pl.BlockSpec((None, S, CB), lambda cb, b: (0, b, cb)),
                  pl.BlockSpec((None, S, CB), lambda cb, b: (1, b, cb)),
                  own, own, own, own, blk,
                  pl.BlockSpec((4, CB), lambda cb, b: (0, cb)),
                  pl.BlockSpec((None, CB, 2 * CB), lambda cb, b: (cb, 0, 0)),
                  vec] + [ANY] * ng,
        out_specs=[pl.BlockSpec((2, S, CB), lambda cb, b: (0, b, cb)),
                   pl.BlockSpec((None, CB, 2 * CB), lambda cb, b: (cb, 0, 0)),
                   pl.BlockSpec((8, CB), lambda cb, b: (0, cb))] + [ANY] * ng,
        out_shape=[jax.ShapeDtypeStruct((2, t, D), BF16),
                   jax.ShapeDtypeStruct((NCB, CB, 2 * CB), F32),
                   jax.ShapeDtypeStruct((8, D), F32)]
        + [jax.ShapeDtypeStruct((4,) + g.shape[1:], g.dtype) for g in give],
        scratch_shapes=[pltpu.VMEM((S, CB), F32), pltpu.VMEM((S, CB), F32), pltpu.VMEM((S, CB), F32),
                        pltpu.VMEM((S + 8, CB), F32), pltpu.VMEM((8, CB), F32),
                        pltpu.SemaphoreType.DMA((ng, 4)), pltpu.SemaphoreType.DMA((ng, 4))],
        compiler_params=_params(("arbitrary", "arbitrary")),
    )(proj, proj, hs, xc_f, gi_f, gr_f, dya, conv_w, wbd, lam, *give)


def _retention_tables():
    f32 = np.float32
    log_g = np.log1p(-(f32(2.0) ** (f32(-5.0) - np.arange(HEADS, dtype=f32)))).astype(f32)
    idx = np.arange(CH, dtype=f32)
    diff = idx[:, None] - idx[None, :]
    inner = np.where(diff >= 0, np.exp(np.maximum(diff, f32(0.0))[None] * log_g[:, None, None]), f32(0.0)).astype(f32)
    cross = np.exp((idx[None, :] + f32(1.0)) * log_g[:, None]).astype(f32)
    state = np.exp((f32(CH - 1.0) - idx[None, :]) * log_g[:, None]).astype(f32)
    cross = np.ascontiguousarray(np.broadcast_to(cross[:, :, None], (HEADS, CH, DK)))
    state = np.ascontiguousarray(np.broadcast_to(state[:, :, None], (HEADS, CH, DK)))
    half = DK // 2
    freqs = (f32(10000.0) ** (-np.arange(half, dtype=f32) / f32(half))).astype(f32)
    ang = (np.arange(S, dtype=f32)[:, None] * freqs[None, :]).astype(f32)
    return tuple(jnp.asarray(a) for a in (inner, cross, state, np.cos(ang).astype(f32), np.sin(ang).astype(f32)))


def _rotate(x, cos, sin):
    half = DK // 2
    x1, x2 = x[:, :half], x[:, half:]
    return jnp.concatenate([x1 * cos - x2 * sin, x1 * sin + x2 * cos], axis=1)


def _rotate_back(d, cos, sin):
    half = DK // 2
    d1, d2 = d[:, :half], d[:, half:]
    return jnp.concatenate([d1 * cos + d2 * sin, d2 * cos - d1 * sin], axis=1)


def _ret_fwd(proj, gain, tables, nb, wp_own):
    t = nb * S
    inner_t, cross_t, state_t, cos_t, sin_t = tables

    def body(q_ref, k_ref, v_ref, gb_ref, gain_ref, dm_ref, cd_ref, sd_ref, cos_ref, sin_ref, wp_ref,
             yb_ref, qr_ref, kr_ref, o_ref, rs_ref, wpg_ref, r_s, send_sems, recv_sems, own_sems):
        b, hd = pl.program_id(0), pl.program_id(1)
        own, first, arrive, forward, others = _gather_copies([wp_ref], [wpg_ref], send_sems, recv_sems, own_sems)

        @pl.when(jnp.logical_and(b == 0, hd == 0))
        def _():
            for cp in own + first:
                cp.start()

        @pl.when(jnp.logical_and(b == nb - 1, hd == HEADS - 1))
        def _():
            for came, on in zip(arrive, forward):
                came.wait_recv()
                on.start()

        r_s[...] = jnp.zeros_like(r_s)
        chunk_decay = cd_ref[CH - 1:CH, :]

        def chunk(c, carry):
            rows = _rows(c, CH)
            cos, sin = cos_ref[rows, :], sin_ref[rows, :]
            qr = _rotate(q_ref[rows, :].astype(F32), cos, sin).astype(BF16)
            kr = (_rotate(k_ref[rows, :].astype(F32), cos, sin) * (DK ** -0.5)).astype(BF16)
            vb = v_ref[rows, :]
            v = vb.astype(F32)
            qr_ref[rows, :] = qr
            kr_ref[rows, :] = kr
            r = r_s[...]
            rb = r.astype(BF16)
            rs_ref[c] = rb
            p = (_dot_nt(qr, kr) * dm_ref[...]).astype(BF16)
            o = _dot(p, vb) + _dot(qr, rb) * cd_ref[...]
            r_s[...] = chunk_decay * r + _dot_tn(kr, (v * sd_ref[...]).astype(BF16))
            o_ref[rows, :] = o
            oc = o - jnp.mean(o, axis=-1, keepdims=True)
            rstd = lax.rsqrt(jnp.mean(oc * oc, axis=-1, keepdims=True) + EPS)
            gb = gb_ref[rows, :].astype(F32)
            yb_ref[rows, :] = (gb * _sigmoid(gb) * (oc * rstd * gain_ref[...])).astype(BF16)
            return carry

        lax.fori_loop(0, NCH, chunk, 0)

        @pl.when(jnp.logical_and(b == nb - 1, hd == HEADS - 1))
        def _():
            for cp in others:
                cp.wait_recv()
            for cp in first + forward:
                cp.wait_send()
            for cp in own:
                cp.wait()

    seg = lambda s: pl.BlockSpec((None, S, DK), lambda b, h: (s, b, h))
    tab = pl.BlockSpec((None, CH, DK), lambda b, h: (h, 0, 0))
    rot = pl.BlockSpec((S, DK // 2), lambda b, h: (0, 0))
    blk = pl.BlockSpec((S, DK), lambda b, h: (b, h))
    return pl.pallas_call(
        body, name="ret_fwd", grid=(nb, HEADS),
        in_specs=[seg(2), seg(3), seg(4), seg(5),
                  pl.BlockSpec((None, 1, DK), lambda b, h: (h, 0, 0)),
                  tab, tab, tab, rot, rot, ANY],
        out_specs=[blk, blk, blk, blk,
                   pl.BlockSpec((None, None, NCH, DK, DK), lambda b, h: (b, h, 0, 0, 0)), ANY],
        out_shape=[jax.ShapeDtypeStruct((t, D), BF16), jax.ShapeDtypeStruct((t, D), BF16),
                   jax.ShapeDtypeStruct((t, D), BF16), jax.ShapeDtypeStruct((t, D), F32),
                   jax.ShapeDtypeStruct((nb, HEADS, NCH, DK, DK), BF16),
                   jax.ShapeDtypeStruct((NDEV,) + wp_own.shape, wp_own.dtype)],
        scratch_shapes=[pltpu.VMEM((DK, DK), F32),
                        pltpu.SemaphoreType.DMA((1, 7)), pltpu.SemaphoreType.DMA((1, 7)), pltpu.SemaphoreType.DMA((1,))],
        compiler_params=_params(("arbitrary", "arbitrary")),
    )(proj, proj, proj, proj, gain, inner_t, cross_t, state_t, cos_t, sin_t, wp_own)


def _ret_bwd(proj, qr, kr, o, rs, dyb, gain, tables, nb, sums):
    t = nb * S
    ns = len(sums)
    inner_t, cross_t, state_t, cos_t, sin_t = tables

    def body(qr_ref, kr_ref, v_ref, gb_ref, o_ref, dyb_ref, rs_ref, gain_ref, dm_ref, cd_ref, sd_ref,
             cos_ref, sin_ref, *rest):
        sum_refs, rest = rest[:ns], rest[ns:]
        dp_ref, dgain_ref = rest[:2]
        part_refs, rest = rest[2:2 + ns], rest[2 + ns:]
        dr_s, send_sems, recv_sems, local_sems = rest
        mine, sends, recvs = _chip_copies(sum_refs, part_refs, send_sems, recv_sems, local_sems)

        @pl.when(jnp.logical_and(pl.program_id(0) == 0, pl.program_id(1) == 0))
        def _():
            for cp in mine + sends:
                cp.start()

        dr_s[...] = jnp.zeros_like(dr_s)
        chunk_decay = cd_ref[CH - 1:CH, :]

        @pl.when(pl.program_id(1) == 0)
        def _():
            dgain_ref[...] = jnp.zeros_like(dgain_ref)

        def chunk(i, carry):
            c = NCH - 1 - i
            rows = _rows(c, CH)
            gain_v = gain_ref[...]
            o_c = o_ref[rows, :]
            oc = o_c - jnp.mean(o_c, axis=-1, keepdims=True)
            rstd = lax.rsqrt(jnp.mean(oc * oc, axis=-1, keepdims=True) + EPS)
            yn = oc * rstd
            gb = gb_ref[rows, :].astype(F32)
            sg = _sigmoid(gb)
            dyb_c = dyb_ref[rows, :]
            dgn = dyb_c * (gb * sg)
            dp_ref[3, rows, :] = (dyb_c * (yn * gain_v) * (sg * (1.0 + gb * (1.0 - sg)))).astype(BF16)
            dgain_ref[...] += jnp.sum(dgn * yn, axis=0, keepdims=True)
            dyn = dgn * gain_v
            do = rstd * (dyn - jnp.mean(dyn, axis=-1, keepdims=True)
                         - yn * jnp.mean(dyn * yn, axis=-1, keepdims=True))
            dob = do.astype(BF16)
            dox = (do * cd_ref[...]).astype(BF16)

            q_c, k_c = qr_ref[rows, :], kr_ref[rows, :]
            vb = v_ref[rows, :]
            v = vb.astype(F32)
            vs = (v * sd_ref[...]).astype(BF16)
            rb = rs_ref[c]
            d_r = dr_s[...]
            drb = d_r.astype(BF16)
            dm = dm_ref[...]
            p = (_dot_nt(q_c, k_c) * dm).astype(BF16)
            dpm = (_dot_nt(dob, vb) * dm).astype(BF16)
            dq = _dot(dpm, k_c) + _dot_nt(dox, rb)
            dk = _dot_tn(dpm, q_c) + _dot_nt(vs, drb)
            dv = _dot_tn(p, dob) + _dot(k_c, drb) * sd_ref[...]
            dr_s[...] = chunk_decay * d_r + _dot_tn(q_c, dox)

            cos, sin = cos_ref[rows, :], sin_ref[rows, :]
            dp_ref[0, rows, :] = _rotate_back(dq, cos, sin).astype(BF16)
            dp_ref[1, rows, :] = (_rotate_back(dk, cos, sin) * (DK ** -0.5)).astype(BF16)
            dp_ref[2, rows, :] = dv.astype(BF16)
            return carry

        lax.fori_loop(0, NCH, chunk, 0)

        @pl.when(jnp.logical_and(pl.program_id(0) == HEADS - 1, pl.program_id(1) == nb - 1))
        def _():
            for cp in recvs:
                cp.wait_recv()
            for cp in sends:
                cp.wait_send()
            for cp in mine:
                cp.wait()

    seg = lambda s: pl.BlockSpec((None, S, DK), lambda h, b: (s, b, h))
    tab = pl.BlockSpec((None, CH, DK), lambda h, b: (h, 0, 0))
    rot = pl.BlockSpec((S, DK // 2), lambda h, b: (0, 0))
    blk = pl.BlockSpec((S, DK), lambda h, b: (b, h))
    one = pl.BlockSpec((None, 1, DK), lambda h, b: (h, 0, 0))
    return pl.pallas_call(
        body, name="ret_bwd", grid=(HEADS, nb),
        in_specs=[blk, blk, seg(4), seg(5), blk, blk,
                  pl.BlockSpec((None, None, NCH, DK, DK), lambda h, b: (b, h, 0, 0, 0)),
                  one, tab, tab, tab, rot, rot] + [ANY] * ns,
        out_specs=[pl.BlockSpec((4, S, DK), lambda h, b: (0, b, h)), one] + [ANY] * ns,
        out_shape=[jax.ShapeDtypeStruct((4, t, D), BF16), jax.ShapeDtypeStruct((HEADS, 1, DK), F32)]
        + [jax.ShapeDtypeStruct(a.shape, a.dtype) for a in sums],
        scratch_shapes=[pltpu.VMEM((DK, DK), F32), pltpu.SemaphoreType.DMA((ns, 3)), pltpu.SemaphoreType.DMA((ns, 3)),
                        pltpu.SemaphoreType.DMA((ns,))],
        compiler_params=_params(("arbitrary", "arbitrary")),
    )(qr, kr, proj, proj, o, dyb, rs, gain, inner_t, cross_t, state_t, cos_t, sin_t, *sums)


def _wblock(k):
    return pl.BlockSpec((NDEV, D // NDEV, D), lambda i: (0, k, 0))


def _tail(ya, yb, proj, x2d, tgt, wg, g_fin):
    t = x2d.shape[0]
    tm = 256

    def body(ya_ref, yb_ref, ma_ref, mb_ref, x_ref, t_ref, wa_ref, wb_ref, wo_ref, g_ref,
             dx2_ref, dya_ref, dyb_ref, dm_ref, mg_ref, doa_ref, dob_ref, gfin_ref, loss_ref):
        i = pl.program_id(0)

        @pl.when(i == 0)
        def _():
            gfin_ref[...] = jnp.zeros_like(gfin_ref)
            loss_ref[...] = jnp.zeros_like(loss_ref)

        wa = wa_ref[...].reshape(D, D)
        wb = wb_ref[...].reshape(D, D)
        wo = wo_ref[...].reshape(D, D)
        out_a = _dot(ya_ref[...], wa)
        out_b = _dot(yb_ref[...], wb)
        sa = _sigmoid(ma_ref[...].astype(F32))
        sb = _sigmoid(mb_ref[...].astype(F32))
        merged = (sa * out_a + sb * out_b).astype(BF16)
        mg_ref[...] = merged
        x2 = x_ref[...] + _dot(merged, wo)
        r2 = lax.rsqrt(jnp.mean(x2 * x2, axis=-1, keepdims=True) + EPS)
        xh = x2 * r2
        g = g_ref[...]
        err = xh * g - t_ref[...]
        loss_ref[...] += jnp.sum(err * err, axis=0, keepdims=True) * (0.5 / D)
        dy = err * (1.0 / D)
        gfin_ref[...] += jnp.sum(dy * xh, axis=0, keepdims=True)
        dxh = dy * g
        dx2 = r2 * (dxh - xh * jnp.mean(dxh * xh, axis=-1, keepdims=True))
        dx2_ref[...] = dx2
        dmerged = _dot_nt(dx2.astype(BF16), wo)
        doa = (sa * dmerged).astype(BF16)
        dob = (sb * dmerged).astype(BF16)
        doa_ref[...] = doa
        dob_ref[...] = dob
        dm_ref[0] = (dmerged * out_a * sa * (1.0 - sa)).astype(BF16)
        dm_ref[1] = (dmerged * out_b * sb * (1.0 - sb)).astype(BF16)
        dya_ref[...] = _dot_nt(doa, wa)
        dyb_ref[...] = _dot_nt(dob, wb)

    row = lambda: pl.BlockSpec((tm, D), lambda i: (i, 0))
    seg = lambda s: pl.BlockSpec((None, tm, D), lambda i: (s, i, 0))
    vec = pl.BlockSpec((1, D), lambda i: (0, 0))
    return pl.pallas_call(
        body, name="tail", grid=(t // tm,),
        in_specs=[row(), row(), seg(6), seg(7), row(), row(), _wblock(0), _wblock(1), _wblock(2), vec],
        out_specs=[row(), row(), row(), pl.BlockSpec((2, tm, D), lambda i: (0, i, 0)),
                   row(), row(), row(), vec, vec],
        out_shape=[jax.ShapeDtypeStruct((t, D), F32), jax.ShapeDtypeStruct((t, D), F32),
                   jax.ShapeDtypeStruct((t, D), F32), jax.ShapeDtypeStruct((2, t, D), BF16),
                   jax.ShapeDtypeStruct((t, D), BF16), jax.ShapeDtypeStruct((t, D), BF16),
                   jax.ShapeDtypeStruct((t, D), BF16), jax.ShapeDtypeStruct((1, D), F32),
                   jax.ShapeDtypeStruct((1, D), F32)],
        compiler_params=_params(("arbitrary",)),
    )(ya, yb, proj, proj, x2d, tgt, wg, wg, wg, g_fin)


def _tail_wgrad(ya, yb, merged, doa, dob, dx2):
    t = ya.shape[0]
    tm = 512

    def body(ya_ref, yb_ref, mg_ref, doa_ref, dob_ref, dx2_ref, ga_ref, gb_ref, go_ref):
        @pl.when(pl.program_id(0) == 0)
        def _():
            ga_ref[...] = jnp.zeros_like(ga_ref)
            gb_ref[...] = jnp.zeros_like(gb_ref)
            go_ref[...] = jnp.zeros_like(go_ref)

        ga_ref[...] += _dot_tn(ya_ref[...], doa_ref[...])
        gb_ref[...] += _dot_tn(yb_ref[...], dob_ref[...])
        go_ref[...] += _dot_tn(mg_ref[...], dx2_ref[...].astype(BF16))

    row = lambda: pl.BlockSpec((tm, D), lambda i: (i, 0))
    full = lambda: pl.BlockSpec((D, D), lambda i: (0, 0))
    return pl.pallas_call(
        body, name="tail_wgrad", grid=(t // tm,),
        in_specs=[row() for _ in range(6)], out_specs=[full(), full(), full()],
        out_shape=[jax.ShapeDtypeStruct((D, D), F32)] * 3,
        compiler_params=_params(("arbitrary",)),
    )(ya, yb, merged, doa, dob, dx2)


def _dproj_specs(tm, j_of, i_of):
    last = lambda j, i, lo, n: (jnp.clip(j - lo, 0, n - 1), i, 0)
    return [pl.BlockSpec((None, tm, D), lambda a, b: last(j_of(a, b), i_of(a, b), 0, 2)),
            pl.BlockSpec((None, tm, D), lambda a, b: last(j_of(a, b), i_of(a, b), 2, 4)),
            pl.BlockSpec((None, tm, D), lambda a, b: last(j_of(a, b), i_of(a, b), 6, 2))]


def _dproj_specs_ordered(tm):
    def spec(lo, n):
        def index(k, i, order_ref):
            seg = order_ref[k]
            mine = jnp.logical_and(seg >= lo, seg < lo + n)
            return jnp.where(mine, seg - lo, 0), jnp.where(mine, i, 0), 0
        return pl.BlockSpec((None, tm, D), index)
    return [spec(0, 2), spec(2, 4), spec(6, 2)]


def _dproj_pick(j, da_ref, db_ref, dc_ref, use):
    @pl.when(j < 2)
    def _():
        use(da_ref[...])

    @pl.when(jnp.logical_and(j >= 2, j < 6))
    def _():
        use(db_ref[...])

    @pl.when(j >= 6)
    def _():
        use(dc_ref[...])


def _rs_schedule(q, c):
    steps = []
    for s in range(3):
        d_a = lax.rem(q + 1 + s, 4)
        d_b = lax.rem(q + 1 + (s + 1) % 3, 4)
        steps.append((jnp.where(c == 0, d_a, d_b), jnp.where(c == 0, d_b, d_a)))
    steps.append((q, q))
    return steps


def _rs_order(q, c):
    order = []
    for keep, give in _rs_schedule(q, c):
        order += [2 * give + 1 - c, 2 * keep + c]
    return jnp.stack(order).astype(jnp.int32)


def _inproj_wgrad_rs(h, dpa, dpb, dpc, order, smalls):
    t = h.shape[0]
    tm = 1024
    nt = t // tm
    nsm = len(smalls)

    def body(order_ref, h_ref, da_ref, db_ref, dc_ref, *rest):
        small_refs, parts_ref, rest = rest[:nsm], rest[nsm], rest[nsm + 1:]
        all_refs, rest = rest[:nsm], rest[nsm:]
        (acc, sib, outb, give_send, give_recv, sum_send, sum_recv, own_sem,
         small_send, small_recv, small_own) = rest
        k, i = pl.program_id(0), pl.program_id(1)
        x, y, c = _place()
        schedule = _rs_schedule(2 * x + y, c)
        own, first, arrive, forward, others = _gather_copies(small_refs, all_refs, small_send, small_recv, small_own)

        @pl.when(jnp.logical_and(k == 0, i == 0))
        def _():
            for cp in own + first:
                cp.start()

        @pl.when(jnp.logical_and(k == 2, i == 0))
        def _():
            for came, on in zip(arrive, forward):
                came.wait_recv()
                on.start()

        def use(d):
            @pl.when(i == 0)
            def _():
                acc[k % 2] = _dot_tn(h_ref[...], d)

            @pl.when(i > 0)
            def _():
                acc[k % 2] += _dot_tn(h_ref[...], d)

        _dproj_pick(order_ref[k], da_ref, db_ref, dc_ref, use)

        def give_copy(s):
            return pltpu.make_async_remote_copy(
                src_ref=acc.at[0], dst_ref=sib.at[s % 2], send_sem=give_send.at[s], recv_sem=give_recv.at[s],
                device_id=(x, y, 1 - c), device_id_type=MESH)

        def sum_copy(s):
            keep = schedule[s][0]
            return pltpu.make_async_remote_copy(
                src_ref=outb.at[s], dst_ref=parts_ref.at[s], send_sem=sum_send.at[s], recv_sem=sum_recv.at[s],
                device_id=(keep // 2, lax.rem(keep, 2), c), device_id_type=MESH)

        own_copy = pltpu.make_async_copy(outb.at[3], parts_ref.at[3], own_sem)

        for s in range(4):
            @pl.when(jnp.logical_and(k == 2 * s, i == nt - 1))
            def _():
                give_copy(s).start()

            @pl.when(jnp.logical_and(k == 2 * s + 1, i == nt - 1))
            def _():
                give_copy(s).wait_recv()
                outb[s] = (acc[1] + sib[s % 2]).astype(BF16)
                give_copy(s).wait_send()
                if s < 3:
                    sum_copy(s).start()
                else:
                    own_copy.start()

        @pl.when(jnp.logical_and(k == NSEG - 1, i == nt - 1))
        def _():
            for s in range(3):
                sum_copy(s).wait_recv()
            for s in range(3):
                sum_copy(s).wait_send()
            own_copy.wait()
            for cp in others:
                cp.wait_recv()
            for cp in first + forward:
                cp.wait_send()
            for cp in own:
                cp.wait()

    return pl.pallas_call(
        body, name="inproj_wgrad_rs",
        grid_spec=pltpu.PrefetchScalarGridSpec(
            num_scalar_prefetch=1, grid=(NSEG, nt),
            in_specs=[pl.BlockSpec((tm, D), lambda k, i, order_ref: (i, 0))] + _dproj_specs_ordered(tm) + [ANY] * nsm,
            out_specs=[ANY] * (1 + nsm),
            scratch_shapes=[pltpu.VMEM((2, D, D), F32), pltpu.VMEM((2, D, D), F32), pltpu.VMEM((4, D, D), BF16),
                            pltpu.SemaphoreType.DMA((4,)), pltpu.SemaphoreType.DMA((4,)),
                            pltpu.SemaphoreType.DMA((3,)), pltpu.SemaphoreType.DMA((3,)),
                            pltpu.SemaphoreType.DMA,
                            pltpu.SemaphoreType.DMA((nsm, 7)), pltpu.SemaphoreType.DMA((nsm, 7)),
                            pltpu.SemaphoreType.DMA((nsm,))]),
        out_shape=[jax.ShapeDtypeStruct((4, D, D), BF16)]
        + [jax.ShapeDtypeStruct((NDEV,) + a.shape, a.dtype) for a in smalls],
        compiler_params=_params(("arbitrary", "arbitrary")),
    )(order, h, dpa, dpb, dpc, *smalls)


def _inproj_dgrad(dpa, dpb, dpc, wg, x2d, dx2, g_in):
    t = x2d.shape[0]
    tm = 1024

    def body(da_ref, db_ref, dc_ref, w_ref, x_ref, dx2_ref, g_ref, gx_ref, gg_ref, acc_s):
        i, j = pl.program_id(0), pl.program_id(1)

        @pl.when(jnp.logical_and(i == 0, j == 0))
        def _():
            gg_ref[...] = jnp.zeros_like(gg_ref)

        @pl.when(j == 0)
        def _():
            acc_s[...] = jnp.zeros_like(acc_s)

        def use(d):
            acc_s[...] += _dot_nt(d, w_ref[...])

        _dproj_pick(j, da_ref, db_ref, dc_ref, use)

        @pl.when(j == NSEG - 1)
        def _():
            x = x_ref[...]
            r = lax.rsqrt(jnp.mean(x * x, axis=-1, keepdims=True) + EPS)
            xh = x * r
            dh = acc_s[...]
            gg_ref[...] += jnp.sum(dh * xh, axis=0, keepdims=True)
            dxh = dh * g_ref[...]
            gx_ref[...] = dx2_ref[...] + r * (dxh - xh * jnp.mean(dxh * xh, axis=-1, keepdims=True))

    row = lambda: pl.BlockSpec((tm, D), lambda i, j: (i, 0))
    vec = pl.BlockSpec((1, D), lambda i, j: (0, 0))
    return pl.pallas_call(
        body, name="inproj_dgrad", grid=(t // tm, NSEG),
        in_specs=_dproj_specs(tm, lambda i, j: j, lambda i, j: i)
        + [pl.BlockSpec((None, D, D), lambda i, j: (j, 0, 0)), row(), row(), vec],
        out_specs=[row(), vec],
        out_shape=[jax.ShapeDtypeStruct((t, D), F32), jax.ShapeDtypeStruct((1, D), F32)],
        scratch_shapes=[pltpu.VMEM((tm, D), F32)],
        compiler_params=_params(("arbitrary", "arbitrary")),
    )(dpa, dpb, dpc, wg, x2d, dx2, g_in)


def _adam_update(g, w, m, v):
    m_new = ADAM_B1 * m + (1.0 - ADAM_B1) * g
    v_new = ADAM_B2 * v + (1.0 - ADAM_B2) * (g * g)
    m_hat = m_new / (1.0 - ADAM_B1 ** ADAM_STEP)
    v_hat = v_new / (1.0 - ADAM_B2 ** ADAM_STEP)
    return -ADAM_LR * (m_hat / (jnp.sqrt(v_hat) + ADAM_EPS) + ADAM_WD * w), m_new, v_new


def _sum_in_order(ref):
    total = ref[0].astype(F32)
    for k in range(1, ref.shape[0]):
        total = total + ref[k].astype(F32)
    return total


def _adamw_small(me, vec_all, gx_all, ga_all, groups):
    flat = [a for grp in groups for a in grp]
    ng = len(groups)
    nshard = D // NDEV

    def body(me_ref, vec_ref, shard_ref, gx_ref, ga_ref, *refs):
        ins, outs = refs[:3 * ng], refs[3 * ng:]
        vec = _sum_in_order(vec_ref)
        shard = _sum_in_order(shard_ref)
        grads = [vec[r:r + 1, :] for r in range(6)]
        grads += [shard[0:4, :], shard[4:8, 0:DK // NDEV], _sum_in_order(gx_ref), _sum_in_order(ga_ref)]
        for n, g in enumerate(grads):
            delta, m_new, v_new = _adam_update(g, ins[3 * n][...], ins[3 * n + 1][...], ins[3 * n + 2][...])
            outs[4 * n][...] = g
            outs[4 * n + 1][...] = delta
            outs[4 * n + 2][...] = m_new
            outs[4 * n + 3][...] = v_new
        outs[4 * ng][...] = jnp.sum(vec[6:7, :], axis=1, keepdims=True)

    full = lambda a: pl.BlockSpec(a.shape, lambda i, me_ref, nd=len(a.shape): (0,) * nd)
    out_shape = [jax.ShapeDtypeStruct(w.shape, F32) for w, _, _ in groups for _ in range(4)]
    out_shape.append(jax.ShapeDtypeStruct((1, 1), F32))
    outs = pl.pallas_call(
        body, name="adamw_small",
        grid_spec=pltpu.PrefetchScalarGridSpec(
            num_scalar_prefetch=1, grid=(1,),
            in_specs=[full(vec_all),
                      pl.BlockSpec((NDEV, 8, nshard), lambda i, me_ref: (0, 1, me_ref[0])),
                      full(gx_all), full(ga_all)] + [full(a) for a in flat],
            out_specs=[full(s) for s in out_shape]),
        out_shape=out_shape,
        compiler_params=_params(("arbitrary",)),
    )(me, vec_all, vec_all, gx_all, ga_all, *flat)
    return [outs[4 * n:4 * n + 4] for n in range(ng)], outs[4 * ng]


def _adamw(name, items):
    n, rows, cols = items[0][0].shape
    tr = rows if rows <= 256 else 256
    k = len(items)

    def body(*refs):
        for a in range(k):
            p_ref, w_ref, m_ref, v_ref = refs[4 * a:4 * a + 4]
            g = _sum_in_order(p_ref)
            delta, m_new, v_new = _adam_update(g, w_ref[...], m_ref[...], v_ref[...])
            for o, val in zip(refs[4 * k + 4 * a:4 * k + 4 * a + 4], (g, delta, m_new, v_new)):
                o[...] = val

    blk = lambda: pl.BlockSpec((tr, cols), lambda i: (i, 0))
    outs = pl.pallas_call(
        body, name=name, grid=(rows // tr,),
        in_specs=[pl.BlockSpec((n, tr, cols), lambda i: (0, i, 0)), blk(), blk(), blk()] * k,
        out_specs=[blk() for _ in range(4 * k)],
        out_shape=[jax.ShapeDtypeStruct((rows, cols), F32)] * (4 * k),
        compiler_params=_params(("arbitrary",)),
    )(*[a for item in items for a in item])
    return [outs[4 * a:4 * a + 4] for a in range(k)]


ANY = pl.BlockSpec(memory_space=pl.ANY)


def _place():
    return lax.axis_index("x"), lax.axis_index("y"), lax.axis_index("c")


def _gather_copies(ins, outs, send_sems, recv_sems, own_sems):
    x, y, c = _place()
    me, sibling = (x, y, c), (x, y, 1 - c)
    chips = [(1 - x, y), (x, 1 - y), (1 - x, 1 - y)]
    n = len(ins)

    def copy(a, k, block, to, src=None):
        px, py, pc = block
        dst = outs[a].at[4 * px + 2 * py + pc]
        return pltpu.make_async_remote_copy(
            src_ref=dst if src is None else src, dst_ref=dst,
            send_sem=send_sems.at[a, k], recv_sem=recv_sems.at[a, k], device_id=to, device_id_type=MESH)

    own = [pltpu.make_async_copy(ins[a], outs[a].at[4 * x + 2 * y + c], own_sems.at[a]) for a in range(n)]
    first = []
    for a in range(n):
        first.append(copy(a, 0, me, sibling, src=ins[a]))
        first += [copy(a, 1 + j, me, (*chip, c), src=ins[a]) for j, chip in enumerate(chips)]
    arrive = [copy(a, 1 + j, (*chip, c), me) for j, chip in enumerate(chips) for a in range(n)]
    forward = [copy(a, 4 + j, (*chip, c), sibling) for j, chip in enumerate(chips) for a in range(n)]
    rest = [copy(a, 0, sibling, me) for a in range(n)]
    rest += [copy(a, 4 + j, (*chip, 1 - c), me) for a in range(n) for j, chip in enumerate(chips)]
    return own, first, arrive, forward, rest


def _sibling_copies(ins, outs, send_sems, recv_sems):
    x, y, c = _place()
    return [pltpu.make_async_remote_copy(
        src_ref=ins[a].at[2 * q + 1 - c], dst_ref=outs[a].at[q],
        send_sem=send_sems.at[a, q], recv_sem=recv_sems.at[a, q],
        device_id=(x, y, 1 - c), device_id_type=MESH) for a in range(len(ins)) for q in range(4)]


def _chip_copies(ins, outs, send_sems, recv_sems, local_sems):
    x, y, c = _place()
    my_chip = 2 * x + y
    chips = [(1 - x, y), (x, 1 - y), (1 - x, 1 - y)]
    n = len(ins)
    mine = [pltpu.make_async_copy(ins[a].at[my_chip], outs[a].at[my_chip], local_sems.at[a]) for a in range(n)]
    sends = [pltpu.make_async_remote_copy(
        src_ref=ins[a].at[2 * px + py], dst_ref=outs[a].at[my_chip],
        send_sem=send_sems.at[a, j], recv_sem=recv_sems.at[a, j],
        device_id=(px, py, c), device_id_type=MESH) for a in range(n) for j, (px, py) in enumerate(chips)]
    recvs = [pltpu.make_async_remote_copy(
        src_ref=ins[a].at[my_chip], dst_ref=outs[a].at[2 * px + py],
        send_sem=send_sems.at[a, j], recv_sem=recv_sems.at[a, j],
        device_id=(px, py, c), device_id_type=MESH) for a in range(n) for j, (px, py) in enumerate(chips)]
    return mine, sends, recvs


def _chip_sum(owns, gots, core):
    n = len(owns)
    _, rows, cols = owns[0].shape

    def body(core_ref, *refs):
        for a in range(n):
            refs[2 * n + a][...] = (refs[a][...] + refs[n + a][...]).astype(BF16)

    own_spec = pl.BlockSpec((None, rows, cols), lambda q, core_ref: (2 * q + core_ref[0], 0, 0))
    slab = pl.BlockSpec((None, rows, cols), lambda q, core_ref: (q, 0, 0))
    return pl.pallas_call(
        body, name="chip_sum",
        grid_spec=pltpu.PrefetchScalarGridSpec(
            num_scalar_prefetch=1, grid=(4,),
            in_specs=[own_spec] * n + [slab] * n, out_specs=[slab] * n),
        out_shape=[jax.ShapeDtypeStruct((4, rows, cols), BF16)] * n,
        compiler_params=_params(("arbitrary",)),
    )(core, *owns, *gots)


def _block_diag(w):
    w4 = w.reshape(NCB, 4, 64, 64)
    eye = jnp.eye(4, dtype=w.dtype)
    return (w4[:, :, :, None, :] * eye[None, :, None, :, None]).reshape(NCB, CB, CB)


def _block_diag_back(g):
    g5 = g.reshape(NCB, 4, 64, 4, 64)
    return jnp.stack([g5[:, m, :, m, :] for m in range(4)], axis=1).reshape(16, 64, 64)


def kernel(x, norm_in, w_in, conv_w, conv_b, gate_x_w, gate_x_b, gate_a_w, gate_a_b, lru_lambda, gn_gain, w_proj_a, w_proj_b, w_out, norm_final, loss_target, m_norm_in, m_w_in, m_conv_w, m_conv_b, m_gate_x_w, m_gate_x_b, m_gate_a_w, m_gate_a_b, m_lru_lambda, m_gn_gain, m_w_proj_a, m_w_proj_b, m_w_out, m_norm_final, v_norm_in, v_w_in, v_conv_w, v_conv_b, v_gate_x_w, v_gate_x_b, v_gate_a_w, v_gate_a_b, v_lru_lambda, v_gn_gain, v_w_proj_a, v_w_proj_b, v_w_out, v_norm_final):
    xi, yi, ci = _place()
    me = 4 * xi + 2 * yi + ci
    core = ci.astype(jnp.int32).reshape(1)
    nshard = D // NDEV
    nb = x.shape[0]
    t = nb * S
    x2d = x.reshape(t, D)
    tgt2d = loss_target.reshape(t, D)
    g_final = norm_final.reshape(1, D)
    wbd = jnp.concatenate([_block_diag(gate_x_w[0]), _block_diag(gate_a_w[0])], axis=-1).astype(BF16)
    tables = _retention_tables()

    wp_own = jnp.concatenate([w_proj_a[0], w_proj_b[0], w_out[0]], axis=0).astype(BF16)
    tiny = jnp.concatenate([conv_w[0], jnp.pad(gn_gain[0], ((0, 0), (0, nshard - DK // NDEV)))], axis=0)
    proj, h, wg, tiny_g = _inproj_gather(x2d, norm_in, w_in[0].astype(BF16), tiny, _gather_order(xi, yi, ci))
    conv_w_full = tiny_g[:, 0:4, :].transpose(1, 0, 2).reshape(4, D)
    gain3 = tiny_g[:, 4:8, :DK // NDEV].transpose(1, 0, 2).reshape(HEADS, 1, DK)

    ya, hs, xc, gi, gr = _lru_fwd(proj, conv_w_full, conv_b, wbd, gate_x_b, gate_a_b, lru_lambda, nb)
    yb, qr, kr, o, rs, wpg = _ret_fwd(proj, gain3, tables, nb, wp_own)
    dx2, dya, dyb, dpc, merged, doa, dob, g_fin, loss_vec = _tail(ya, yb, proj, x2d, tgt2d, wpg, g_final)
    g_pa, g_pb, g_out = _tail_wgrad(ya, yb, merged, doa, dob, dx2)

    own = [g.reshape(NDEV, nshard, D) for g in (g_pa, g_pb, g_out)]
    dpa, g_wbd, g_vec, *got = _lru_bwd(proj, hs, xc, gi, gr, dya, conv_w_full, wbd, lru_lambda, nb, own)
    sums = _chip_sum(own, got, core)
    dpb, g_gain, *parts = _ret_bwd(proj, qr, kr, o, rs, dyb, gain3, tables, nb, sums)

    grad_x, g_norm_in = _inproj_dgrad(dpa, dpb, dpc, wg, x2d, dx2, norm_in)
    grad_x = grad_x.reshape(nb, S, D)

    gain_rows = jnp.pad(g_gain.reshape(HEADS, NDEV, DK // NDEV), ((0, 0), (0, 0), (0, nshard - DK // NDEV)))
    vec = jnp.concatenate([g_norm_in, g_vec[0:4], g_fin, loss_vec, jnp.zeros((1, D), F32), g_vec[4:8],
                           gain_rows.reshape(HEADS, D)], axis=0)
    g_gx = _block_diag_back(g_wbd[:, :, :CB]).reshape(D // 2, 128)
    g_ga = _block_diag_back(g_wbd[:, :, CB:]).reshape(D // 2, 128)
    parts_in, vec_all, gx_all, ga_all = _inproj_wgrad_rs(h, dpa, dpb, dpc, _rs_order(2 * xi + yi, ci),
                                                         [vec, g_gx, g_ga])
    gx_all = gx_all.reshape(NDEV, D, 64)
    ga_all = ga_all.reshape(NDEV, D, 64)
    parts = [parts_in] + list(parts)

    res = {}
    (out,) = _adamw("adamw_w_in", [(parts[0], w_in[0], m_w_in[0], v_w_in[0])])
    res["w_in"] = [o[None] for o in out]
    square = [("w_proj_a", w_proj_a, m_w_proj_a, v_w_proj_a), ("w_proj_b", w_proj_b, m_w_proj_b, v_w_proj_b),
              ("w_out", w_out, m_w_out, v_w_out)]
    outs = _adamw("adamw_square", [(parts[1 + k], w[0], m[0], v[0]) for k, (_, w, m, v) in enumerate(square)])
    for (nm, _, _, _), out in zip(square, outs):
        res[nm] = [o[None] for o in out]

    row = lambda a: a.reshape(1, D)
    gate = lambda a: a.reshape(D, 64)
    groups = [("norm_in", norm_in, m_norm_in, v_norm_in, row), ("conv_b", conv_b, m_conv_b, v_conv_b, row),
              ("gate_x_b", gate_x_b, m_gate_x_b, v_gate_x_b, row), ("gate_a_b", gate_a_b, m_gate_a_b, v_gate_a_b, row),
              ("lru_lambda", lru_lambda, m_lru_lambda, v_lru_lambda, row),
              ("norm_final", norm_final, m_norm_final, v_norm_final, row),
              ("conv_w", conv_w, m_conv_w, v_conv_w, lambda a: a[0]), ("gn_gain", gn_gain, m_gn_gain, v_gn_gain, lambda a: a[0]),
              ("gate_x_w", gate_x_w, m_gate_x_w, v_gate_x_w, gate), ("gate_a_w", gate_a_w, m_gate_a_w, v_gate_a_w, gate)]
    small_out, loss = _adamw_small(me.astype(jnp.int32).reshape(1), vec_all, gx_all, ga_all,
                                   [tuple(view(a) for a in (w, m, v)) for _, w, m, v, view in groups])
    for (nm, w, _, _, _), out in zip(groups, small_out):
        res[nm] = [o.reshape(w.shape) for o in out]
    loss = loss.reshape(())

    order = ["norm_in", "w_in", "conv_w", "conv_b", "gate_x_w", "gate_x_b", "gate_a_w", "gate_a_b", "lru_lambda",
             "gn_gain", "w_proj_a", "w_proj_b", "w_out", "norm_final"]
    outs = [loss, grad_x]
    for k in range(4):
        outs += [res[nm][k] for nm in order]
    return tuple(outs)
```

```python
import numpy as np

import jax
import jax.numpy as jnp
from jax import lax
from jax.experimental import pallas as pl
from jax.experimental.pallas import tpu as pltpu

F32 = jnp.float32
BF16 = jnp.bfloat16
MESH = pl.DeviceIdType.MESH

D = 1024
S = 2048
NSEG = 8
NDEV = 8
HEADS = 4
DK = 256
CH = 256
NCH = S // CH
CB = 256
NCB = D // CB
RC = 128
SCAN_GROUP = 8
EPS = 1e-6
LRU_C = 8.0
VMEM_LIMIT = 56 * 1024 * 1024

ADAM_LR = 0.001
ADAM_B1 = 0.9
ADAM_B2 = 0.999
ADAM_EPS = 1e-08
ADAM_WD = 0.01
ADAM_STEP = 10


def _params(sem=None):
    return pltpu.CompilerParams(dimension_semantics=sem, vmem_limit_bytes=VMEM_LIMIT)


def _dot(a, b):
    return jnp.dot(a, b, preferred_element_type=F32)


def _dot_nt(a, b):
    return lax.dot_general(a, b, (((1,), (1,)), ((), ())), preferred_element_type=F32)


def _dot_tn(a, b):
    return lax.dot_general(a, b, (((0,), (0,)), ((), ())), preferred_element_type=F32)


def _sigmoid(x):
    return jax.nn.sigmoid(x)


def _expm1_nonpos(x):
    poly = x * (1.0 + x * (0.5 + x * (1.0 / 6.0 + x * (1.0 / 24.0))))
    return jnp.where(x > -0.05, poly, jnp.exp(x) - 1.0)


def _softplus(x):
    return jnp.maximum(x, 0.0) + jnp.log(1.0 + jnp.exp(-jnp.abs(x)))


def _rows(c, n):
    return pl.ds(pl.multiple_of(c * n, n), n)


def _window_before(ref, c, n):
    r0 = c * n
    if ref.dtype == BF16:
        prev = ref[pl.ds(pl.multiple_of(jnp.maximum(r0 - 16, 0), 16), 16), :].astype(F32)[8:, :]
    else:
        prev = ref[pl.ds(pl.multiple_of(jnp.maximum(r0 - 8, 0), 8), 8), :]
    prev = jnp.where(c > 0, prev, 0.0)
    return jnp.concatenate([prev, ref[_rows(c, n), :].astype(F32)], axis=0)


def _shift_down(win, s, n):
    if s == 0:
        return win[8:, :]
    return pltpu.roll(win, s, 0)[8:, :]


def _shift_up(win, s, n):
    if s == 0:
        return win[:n, :]
    return pltpu.roll(win, n + 8 - s, 0)[:n, :]


HALF = D // 2
GATHER_SLOTS = [("own", None, 0), ("own", None, 1), ("sib", None, 0), ("sib", None, 1)]
for _j, _h in ((0, 0), (1, 0), (0, 1), (1, 1), (2, 0), (2, 1)):
    GATHER_SLOTS += [("ici", _j, _h), ("fwd", _j, _h)]
NSLOT = len(GATHER_SLOTS)


def _gather_order(x, y, c):
    chips = [(1 - x, y), (x, 1 - y), (1 - x, 1 - y)]
    segs, halves = [], []
    for kind, j, h in GATHER_SLOTS:
        if kind == "own":
            seg = 4 * x + 2 * y + c
        elif kind == "sib":
            seg = 4 * x + 2 * y + 1 - c
        else:
            px, py = chips[j]
            seg = 4 * px + 2 * py + (c if kind == "ici" else 1 - c)
        segs.append(seg)
        halves.append(h)
    return jnp.stack(segs).astype(jnp.int32), jnp.asarray(halves, jnp.int32)


def _inproj_gather(x2d, g_in, w_own, tiny_own, order, halves):
    t = x2d.shape[0]
    tm = 1024
    nt = t // tm

    def body(order_ref, half_ref, x_ref, g_ref, w_own_ref, tiny_own_ref,
             proj_ref, h_ref, wg_ref, tinyg_ref,
             w_all, h_all, send_sems, recv_sems, own_sems, out_sems, tiny_send, tiny_recv, tiny_own_sem):
        k, i = pl.program_id(0), pl.program_id(1)
        x, y, c = _place()
        me, sibling = (x, y, c), (x, y, 1 - c)
        mine = 4 * x + 2 * y + c
        chips = [(1 - x, y), (x, 1 - y), (1 - x, 1 - y)]

        def copy(h, n, block, to, own_src=False):
            px, py, pc = block
            dst = w_all.at[4 * px + 2 * py + pc, h]
            return pltpu.make_async_remote_copy(
                src_ref=w_own_ref.at[:, pl.ds(h * HALF, HALF)] if own_src else dst, dst_ref=dst,
                send_sem=send_sems.at[h, n], recv_sem=recv_sems.at[h, n], device_id=to, device_id_type=MESH)

        def tiny_copy(n, block, to, own_src=False):
            px, py, pc = block
            dst = tinyg_ref.at[4 * px + 2 * py + pc]
            return pltpu.make_async_remote_copy(
                src_ref=tiny_own_ref if own_src else dst, dst_ref=dst,
                send_sem=tiny_send.at[n], recv_sem=tiny_recv.at[n], device_id=to, device_id_type=MESH)

        def own_copy(h):
            return pltpu.make_async_copy(w_own_ref.at[:, pl.ds(h * HALF, HALF)], w_all.at[mine, h], own_sems.at[h])

        tiny_mine = pltpu.make_async_copy(tiny_own_ref, tinyg_ref.at[mine], tiny_own_sem)

        def keep_copy(n):
            h = GATHER_SLOTS[n][2]
            return pltpu.make_async_copy(w_all.at[order_ref[n], h], wg_ref.at[order_ref[n], :, pl.ds(h * HALF, HALF)],
                                         out_sems.at[n])

        near = [(0, sibling), (1, (*chips[0], c)), (2, (*chips[1], c))]
        first = [copy(h, n, me, to, True) for h in (0, 1) for n, to in near]
        first += [copy(h, 3, me, (*chips[2], c), True) for h in (0, 1)]
        tiny_first = [tiny_copy(0, me, sibling, True)] + [tiny_copy(1 + j, me, (*chip, c), True) for j, chip in enumerate(chips)]

        for n, (kind, j, h) in enumerate(GATHER_SLOTS):
            @pl.when(jnp.logical_and(k == n, i == 0))
            def _():
                if n == 0:
                    own_copy(0).start()
                    own_copy(1).start()
                    tiny_mine.start()
                    for cp in first + tiny_first:
                        cp.start()
                if kind == "own":
                    own_copy(h).wait()
                elif kind == "sib":
                    copy(h, 0, sibling, me).wait_recv()
                elif kind == "ici":
                    copy(h, 1 + j, (*chips[j], c), me).wait_recv()
                    copy(h, 4 + j, (*chips[j], c), sibling).start()
                else:
                    copy(h, 4 + j, (*chips[j], 1 - c), me).wait_recv()
                keep_copy(n).start()

        rows = pl.ds(pl.multiple_of(i * tm, tm), tm)

        @pl.when(k == 0)
        def _():
            xv = x_ref[...]
            r = lax.rsqrt(jnp.mean(xv * xv, axis=-1, keepdims=True) + EPS)
            hv = (xv * r * g_ref[...]).astype(BF16)
            h_ref[...] = hv
            h_all[rows, :] = hv

        proj_ref[...] = _dot(h_all[rows, :], w_all[order_ref[k], half_ref[k]]).astype(BF16)

        @pl.when(jnp.logical_and(k == NSLOT - 1, i == nt - 1))
        def _():
            for j, chip in enumerate(chips):
                tiny_copy(1 + j, (*chip, c), me).wait_recv()
                tiny_copy(4 + j, (*chip, c), sibling).start()
            tiny_copy(0, sibling, me).wait_recv()
            for j, chip in enumerate(chips):
                tiny_copy(4 + j, (*chip, 1 - c), me).wait_recv()
            for cp in first + tiny_first:
                cp.wait_send()
            for j, chip in enumerate(chips):
                tiny_copy(4 + j, (*chip, c), sibling).wait_send()
                for h in (0, 1):
                    copy(h, 4 + j, (*chip, c), sibling).wait_send()
            tiny_mine.wait()
            for n in range(NSLOT):
                keep_copy(n).wait()

    hold = lambda k, i, order_ref, half_ref: (jnp.where(k == 0, i, nt - 1), 0)
    return pl.pallas_call(
        body, name="inproj_gather",
        grid_spec=pltpu.PrefetchScalarGridSpec(
            num_scalar_prefetch=2, grid=(NSLOT, nt),
            in_specs=[pl.BlockSpec((tm, D), hold),
                      pl.BlockSpec((1, D), lambda k, i, order_ref, half_ref: (0, 0)),
                      ANY, ANY],
            out_specs=[pl.BlockSpec((None, tm, HALF), lambda k, i, order_ref, half_ref: (order_ref[k], i, half_ref[k])),
                       pl.BlockSpec((tm, D), hold),
                       ANY, ANY],
            scratch_shapes=[pltpu.VMEM((NDEV, 2, D, HALF), BF16), pltpu.VMEM((t, D), BF16),
                            pltpu.SemaphoreType.DMA((2, 7)), pltpu.SemaphoreType.DMA((2, 7)),
                            pltpu.SemaphoreType.DMA((2,)), pltpu.SemaphoreType.DMA((NSLOT,)),
                            pltpu.SemaphoreType.DMA((7,)), pltpu.SemaphoreType.DMA((7,)), pltpu.SemaphoreType.DMA]),
        out_shape=[jax.ShapeDtypeStruct((NSEG, t, D), BF16), jax.ShapeDtypeStruct((t, D), BF16),
                   jax.ShapeDtypeStruct((NDEV,) + w_own.shape, BF16),
                   jax.ShapeDtypeStruct((NDEV,) + tiny_own.shape, F32)],
        compiler_params=_params(("arbitrary", "arbitrary")),
    )(order, halves, x2d, g_in, w_own, tiny_own)


def _tile_scan(a, u):
    row = lax.broadcasted_iota(jnp.int32, a.shape, 0)
    for d in (1, 2, 4):
        m = row >= d
        a_sh = pltpu.roll(a, d, 0)
        u_sh = pltpu.roll(u, d, 0)
        u = jnp.where(m, a * u_sh + u, u)
        a = jnp.where(m, a * a_sh, a)
    return a, u


def _tile_scan_rev(a, w):
    row = lax.broadcasted_iota(jnp.int32, a.shape, 0)
    for d in (1, 2, 4):
        m = row < 8 - d
        a_sh = pltpu.roll(a, 8 - d, 0)
        w_sh = pltpu.roll(w, 8 - d, 0)
        w = jnp.where(m, a * w_sh + w, w)
        a = jnp.where(m, a * a_sh, a)
    return a, w


def _lru_gates(xa_ref, c, cw_ref, cb_ref, wbd_ref, bx_ref, ba_ref, sp):
    win = _window_before(xa_ref, c, RC)
    xc = cb_ref[...] + cw_ref[3:4, :] * _shift_down(win, 0, RC)
    for s in (1, 2, 3):
        xc = xc + cw_ref[3 - s:4 - s, :] * _shift_down(win, s, RC)
    z = _dot(xc.astype(BF16), wbd_ref[...])
    gi = _sigmoid(z[:, :CB] + bx_ref[...])
    gr = _sigmoid(z[:, CB:] + ba_ref[...])
    log_a = -LRU_C * gr * sp
    return win, xc, gi, gr, log_a


def _lru_fwd(proj, conv_w, conv_b, wbd, bx, ba, lam, nb):
    t = nb * S

    def body(xa_ref, ga_ref, cw_ref, cb_ref, wbd_ref, bx_ref, ba_ref, lam_ref,
             ya_ref, hs_ref, xc_ref, gi_ref, gr_ref, a_s, u_s):
        sp = _softplus(-lam_ref[...])

        def gates(c, carry):
            _, xc, gi, gr, log_a = _lru_gates(xa_ref, c, cw_ref, cb_ref, wbd_ref, bx_ref, ba_ref, sp)
            rows = _rows(c, RC)
            a_s[rows, :] = jnp.exp(log_a)
            u_s[rows, :] = jnp.sqrt(-_expm1_nonpos(2.0 * log_a)) * (gi * xc)
            xc_ref[rows, :] = xc
            gi_ref[rows, :] = gi
            gr_ref[rows, :] = gr
            return carry

        lax.fori_loop(0, S // RC, gates, 0)

        def scan(g, h):
            for k in range(SCAN_GROUP):
                rows = pl.ds(pl.multiple_of(g * (8 * SCAN_GROUP), 8 * SCAN_GROUP) + 8 * k, 8)
                a_cum, u_cum = _tile_scan(a_s[rows, :], u_s[rows, :])
                hs_ref[rows, :] = u_cum + a_cum * h
                h = u_cum[7:8, :] + a_cum[7:8, :] * h
            return h

        lax.fori_loop(0, S // (8 * SCAN_GROUP), scan, jnp.zeros((1, CB), F32))

        def gate_out(c, carry):
            ga = ga_ref[_rows(c, RC), :].astype(F32)
            ya_ref[_rows(c, RC), :] = (ga * _sigmoid(ga) * hs_ref[_rows(c, RC), :]).astype(BF16)
            return carry

        lax.fori_loop(0, S // RC, gate_out, 0)

    vec = pl.BlockSpec((1, CB), lambda b, cb: (0, cb))
    blk = pl.BlockSpec((S, CB), lambda b, cb: (b, cb))
    return pl.pallas_call(
        body, name="lru_fwd", grid=(nb, NCB),
        in_specs=[pl.BlockSpec((None, S, CB), lambda b, cb: (0, b, cb)),
                  pl.BlockSpec((None, S, CB), lambda b, cb: (1, b, cb)),
                  pl.BlockSpec((4, CB), lambda b, cb: (0, cb)),
                  vec,
                  pl.BlockSpec((None, CB, 2 * CB), lambda b, cb: (cb, 0, 0)),
                  vec, vec, vec],
        out_specs=[blk] + [pl.BlockSpec((None, None, S, CB), lambda b, cb: (b, cb, 0, 0))] * 4,
        out_shape=[jax.ShapeDtypeStruct((t, D), BF16)] + [jax.ShapeDtypeStruct((nb, NCB, S, CB), F32)] * 4,
        scratch_shapes=[pltpu.VMEM((S, CB), F32), pltpu.VMEM((S, CB), F32)],
        compiler_params=_params(("arbitrary", "arbitrary")),
    )(proj, proj, conv_w, conv_b, wbd, bx, ba, lam)


def _lru_bwd(proj, hs, xc_f, gi_f, gr_f, dya, conv_w, wbd, lam, nb, give):
    t = nb * S
    ng = len(give)

    def body(xa_ref, ga_ref, hs_ref, xc_s, gi_s, gr_s, dya_ref, cw_ref, wbd_ref, lam_ref, *rest):
        give_refs, rest = rest[:ng], rest[ng:]
        dp_ref, dwbd_ref, vec_ref = rest[:3]
        got_refs, rest = rest[3:3 + ng], rest[3 + ng:]
        a_s, dl_s, dh_s, dxc_s, acc_s, send_sems, recv_sems = rest
        b = pl.program_id(1)
        exchange = _sibling_copies(give_refs, got_refs, send_sems, recv_sems)

        @pl.when(jnp.logical_and(pl.program_id(0) == 0, b == 0))
        def _():
            for cp in exchange:
                cp.start()

        lam_v = lam_ref[...]
        sp = _softplus(-lam_v)
        acc_s[...] = jnp.zeros_like(acc_s)

        @pl.when(b == 0)
        def _():
            dwbd_ref[...] = jnp.zeros_like(dwbd_ref)
            vec_ref[...] = jnp.zeros_like(vec_ref)

        def gates(c, carry):
            rows = _rows(c, RC)
            a_s[rows, :] = jnp.exp(-LRU_C * gr_s[rows, :] * sp)
            ga = ga_ref[rows, :].astype(F32)
            sg = _sigmoid(ga)
            dya_c = dya_ref[rows, :]
            dl_s[rows, :] = dya_c * (ga * sg)
            dp_ref[1, rows, :] = (dya_c * hs_ref[rows, :] * (sg * (1.0 + ga * (1.0 - sg)))).astype(BF16)
            return carry

        lax.fori_loop(0, S // RC, gates, 0)

        def scan(i, g_in):
            base = pl.multiple_of((S // (8 * SCAN_GROUP) - 1 - i) * (8 * SCAN_GROUP), 8 * SCAN_GROUP)
            row = lax.broadcasted_iota(jnp.int32, (8, CB), 0)
            for k in reversed(range(SCAN_GROUP)):
                rows = pl.ds(base + 8 * k, 8)
                a = a_s[rows, :]
                dl = dl_s[rows, :]
                a_cum, g_loc = _tile_scan_rev(a, a * dl)
                g = g_loc + a_cum * g_in
                dh_s[rows, :] = dl + jnp.where(row < 7, pltpu.roll(g, 7, 0), g_in)
                g_in = g_loc[0:1, :] + a_cum[0:1, :] * g_in
            return g_in

        lax.fori_loop(0, S // (8 * SCAN_GROUP), scan, jnp.zeros((1, CB), F32))

        dxc_s[pl.ds(S, 8), :] = jnp.zeros((8, CB), F32)

        def grads(c, carry):
            rows = _rows(c, RC)
            dh = dh_s[rows, :]
            h_prev = _shift_down(_window_before(hs_ref, c, RC), 1, RC)
            xc, gi, gr, a = xc_s[rows, :], gi_s[rows, :], gr_s[rows, :], a_s[rows, :]
            mult = jnp.sqrt(-_expm1_nonpos(-2.0 * LRU_C * gr * sp))
            dmult = dh * gi * xc
            d_log_a = dh * h_prev * a - dmult * (a * a) / mult
            dzi = dh * mult * xc * gi * (1.0 - gi)
            dzr = d_log_a * (-LRU_C * sp) * gr * (1.0 - gr)
            dz = jnp.concatenate([dzi, dzr], axis=1).astype(BF16)
            dxc_s[rows, :] = dh * mult * gi + _dot_nt(dz, wbd_ref[...])
            dwbd_ref[...] += _dot_tn(xc.astype(BF16), dz)
            acc_s[1:2, :] += jnp.sum(dzi, axis=0, keepdims=True)
            acc_s[2:3, :] += jnp.sum(dzr, axis=0, keepdims=True)
            acc_s[3:4, :] += jnp.sum(d_log_a * (-LRU_C * gr), axis=0, keepdims=True)
            return carry

        lax.fori_loop(0, S // RC, grads, 0)

        def conv_bwd(c, carry):
            rows = _rows(c, RC)
            dwin = dxc_s[pl.ds(pl.multiple_of(c * RC, RC), RC + 8), :]
            dxc = dwin[:RC, :]
            xwin = _window_before(xa_ref, c, RC)
            dxa = cw_ref[3:4, :] * dxc
            acc_s[0:1, :] += jnp.sum(dxc, axis=0, keepdims=True)
            acc_s[7:8, :] += jnp.sum(dxc * _shift_down(xwin, 0, RC), axis=0, keepdims=True)
            for s in (1, 2, 3):
                dxa = dxa + cw_ref[3 - s:4 - s, :] * _shift_up(dwin, s, RC)
                acc_s[7 - s:8 - s, :] += jnp.sum(dxc * _shift_down(xwin, s, RC), axis=0, keepdims=True)
            dp_ref[0, rows, :] = dxa.astype(BF16)
            return carry

        lax.fori_loop(0, S // RC, conv_bwd, 0)

        row = lax.broadcasted_iota(jnp.int32, acc_s.shape, 0)
        vec_ref[...] += jnp.where(row == 3, acc_s[...] * (-_sigmoid(-lam_v)), acc_s[...])

        @pl.when(jnp.logical_and(pl.program_id(0) == NCB - 1, b == nb - 1))
        def _():
            for cp in exchange:
                cp.wait()

    vec = pl.BlockSpec((1, CB), lambda cb, b: (0, cb))
    blk = pl.BlockSpec((S, CB), lambda cb, b: (b, cb))
    own = pl.BlockSpec((None, None, S, CB), lambda cb, b: (b, cb, 0, 0))
    return pl.pallas_call(
        body, name="lru_bwd", grid=(NCB, nb),
        in_specs=[pl.BlockSpec((None, S, CB), lambda cb, b: (0, b, cb)),
                  pl.BlockSpec((None, S, CB), lambda cb, b: (1, b, cb)),
                  own, own, own, own, blk,
                  pl.BlockSpec((4, CB), lambda cb, b: (0, cb)),
                  pl.BlockSpec((None, CB, 2 * CB), lambda cb, b: (cb, 0, 0)),
                  vec] + [ANY] * ng,
        out_specs=[pl.BlockSpec((2, S, CB), lambda cb, b: (0, b, cb)),
                   pl.BlockSpec((None, CB, 2 * CB), lambda cb, b: (cb, 0, 0)),
                   pl.BlockSpec((8, CB), lambda cb, b: (0, cb))] + [ANY] * ng,
        out_shape=[jax.ShapeDtypeStruct((2, t, D), BF16),
                   jax.ShapeDtypeStruct((NCB, CB, 2 * CB), F32),
                   jax.ShapeDtypeStruct((8, D), F32)]
        + [jax.ShapeDtypeStruct((4,) + g.shape[1:], g.dtype) for g in give],
        scratch_shapes=[pltpu.VMEM((S, CB), F32), pltpu.VMEM((S, CB), F32), pltpu.VMEM((S, CB), F32),
                        pltpu.VMEM((S + 8, CB), F32), pltpu.VMEM((8, CB), F32),
                        pltpu.SemaphoreType.DMA((ng, 4)), pltpu.SemaphoreType.DMA((ng, 4))],
        compiler_params=_params(("arbitrary", "arbitrary")),
    )(proj, proj, hs, xc_f, gi_f, gr_f, dya, conv_w, wbd, lam, *give)


def _retention_tables():
    f32 = np.float32
    log_g = np.log1p(-(f32(2.0) ** (f32(-5.0) - np.arange(HEADS, dtype=f32)))).astype(f32)
    idx = np.arange(CH, dtype=f32)
    diff = idx[:, None] - idx[None, :]
    inner = np.where(diff >= 0, np.exp(np.maximum(diff, f32(0.0))[None] * log_g[:, None, None]), f32(0.0)).astype(f32)
    cross = np.exp((idx[None, :] + f32(1.0)) * log_g[:, None]).astype(f32)
    state = np.exp((f32(CH - 1.0) - idx[None, :]) * log_g[:, None]).astype(f32)
    cross = np.ascontiguousarray(np.broadcast_to(cross[:, :, None], (HEADS, CH, DK)))
    state = np.ascontiguousarray(np.broadcast_to(state[:, :, None], (HEADS, CH, DK)))
    half = DK // 2
    freqs = (f32(10000.0) ** (-np.arange(half, dtype=f32) / f32(half))).astype(f32)
    ang = (np.arange(S, dtype=f32)[:, None] * freqs[None, :]).astype(f32)
    return tuple(jnp.asarray(a) for a in (inner, cross, state, np.cos(ang).astype(f32), np.sin(ang).astype(f32)))


def _rotate(x, cos, sin):
    half = DK // 2
    x1, x2 = x[:, :half], x[:, half:]
    return jnp.concatenate([x1 * cos - x2 * sin, x1 * sin + x2 * cos], axis=1)


def _rotate_back(d, cos, sin):
    half = DK // 2
    d1, d2 = d[:, :half], d[:, half:]
    return jnp.concatenate([d1 * cos + d2 * sin, d2 * cos - d1 * sin], axis=1)


def _ret_fwd(proj, gain, tables, nb, wp_own):
    t = nb * S
    inner_t, cross_t, state_t, cos_t, sin_t = tables

    def body(q_ref, k_ref, v_ref, gb_ref, gain_ref, dm_ref, cd_ref, sd_ref, cos_ref, sin_ref, wp_ref,
             yb_ref, qr_ref, kr_ref, o_ref, rs_ref, wpg_ref, r_s, send_sems, recv_sems, own_sems):
        b, hd = pl.program_id(0), pl.program_id(1)
        own, first, arrive, forward, others = _gather_copies([wp_ref], [wpg_ref], send_sems, recv_sems, own_sems)

        @pl.when(jnp.logical_and(b == 0, hd == 0))
        def _():
            for cp in own + first:
                cp.start()

        @pl.when(jnp.logical_and(b == nb - 1, hd == HEADS - 1))
        def _():
            for came, on in zip(arrive, forward):
                came.wait_recv()
                on.start()

        r_s[...] = jnp.zeros_like(r_s)
        chunk_decay = cd_ref[CH - 1:CH, :]

        def chunk(c, carry):
            rows = _rows(c, CH)
            cos, sin = cos_ref[rows, :], sin_ref[rows, :]
            qr = _rotate(q_ref[rows, :].astype(F32), cos, sin).astype(BF16)
            kr = (_rotate(k_ref[rows, :].astype(F32), cos, sin) * (DK ** -0.5)).astype(BF16)
            vb = v_ref[rows, :]
            v = vb.astype(F32)
            qr_ref[rows, :] = qr
            kr_ref[rows, :] = kr
            r = r_s[...]
            rb = r.astype(BF16)
            rs_ref[c] = rb
            p = (_dot_nt(qr, kr) * dm_ref[...]).astype(BF16)
            o = _dot(p, vb) + _dot(qr, rb) * cd_ref[...]
            r_s[...] = chunk_decay * r + _dot_tn(kr, (v * sd_ref[...]).astype(BF16))
            o_ref[rows, :] = o
            oc = o - jnp.mean(o, axis=-1, keepdims=True)
            rstd = lax.rsqrt(jnp.mean(oc * oc, axis=-1, keepdims=True) + EPS)
            gb = gb_ref[rows, :].astype(F32)
            yb_ref[rows, :] = (gb * _sigmoid(gb) * (oc * rstd * gain_ref[...])).astype(BF16)
            return carry

        lax.fori_loop(0, NCH, chunk, 0)

        @pl.when(jnp.logical_and(b == nb - 1, hd == HEADS - 1))
        def _():
            for cp in others:
                cp.wait_recv()
            for cp in first + forward:
                cp.wait_send()
            for cp in own:
                cp.wait()

    seg = lambda s: pl.BlockSpec((None, S, DK), lambda b, h: (s, b, h))
    tab = pl.BlockSpec((None, CH, DK), lambda b, h: (h, 0, 0))
    rot = pl.BlockSpec((S, DK // 2), lambda b, h: (0, 0))
    blk = pl.BlockSpec((S, DK), lambda b, h: (b, h))
    return pl.pallas_call(
        body, name="ret_fwd", grid=(nb, HEADS),
        in_specs=[seg(2), seg(3), seg(4), seg(5),
                  pl.BlockSpec((None, 1, DK), lambda b, h: (h, 0, 0)),
                  tab, tab, tab, rot, rot, ANY],
        out_specs=[blk, blk, blk, blk,
                   pl.BlockSpec((None, None, NCH, DK, DK), lambda b, h: (b, h, 0, 0, 0)), ANY],
        out_shape=[jax.ShapeDtypeStruct((t, D), BF16), jax.ShapeDtypeStruct((t, D), BF16),
                   jax.ShapeDtypeStruct((t, D), BF16), jax.ShapeDtypeStruct((t, D), F32),
                   jax.ShapeDtypeStruct((nb, HEADS, NCH, DK, DK), BF16),
                   jax.ShapeDtypeStruct((NDEV,) + wp_own.shape, wp_own.dtype)],
        scratch_shapes=[pltpu.VMEM((DK, DK), F32),
                        pltpu.SemaphoreType.DMA((1, 7)), pltpu.SemaphoreType.DMA((1, 7)), pltpu.SemaphoreType.DMA((1,))],
        compiler_params=_params(("arbitrary", "arbitrary")),
    )(proj, proj, proj, proj, gain, inner_t, cross_t, state_t, cos_t, sin_t, wp_own)


def _ret_bwd(proj, qr, kr, o, rs, dyb, gain, tables, nb, sums):
    t = nb * S
    ns = len(sums)
    inner_t, cross_t, state_t, cos_t, sin_t = tables

    def body(qr_ref, kr_ref, v_ref, gb_ref, o_ref, dyb_ref, rs_ref, gain_ref, dm_ref, cd_ref, sd_ref,
             cos_ref, sin_ref, *rest):
        sum_refs, rest = rest[:ns], rest[ns:]
        dp_ref, dgain_ref = rest[:2]
        part_refs, rest = rest[2:2 + ns], rest[2 + ns:]
        dr_s, send_sems, recv_sems, local_sems = rest
        mine, sends, recvs = _chip_copies(sum_refs, part_refs, send_sems, recv_sems, local_sems)

        @pl.when(jnp.logical_and(pl.program_id(0) == 0, pl.program_id(1) == 0))
        def _():
            for cp in mine + sends:
                cp.start()

        dr_s[...] = jnp.zeros_like(dr_s)
        chunk_decay = cd_ref[CH - 1:CH, :]

        @pl.when(pl.program_id(1) == 0)
        def _():
            dgain_ref[...] = jnp.zeros_like(dgain_ref)

        def chunk(i, carry):
            c = NCH - 1 - i
            rows = _rows(c, CH)
            gain_v = gain_ref[...]
            o_c = o_ref[rows, :]
            oc = o_c - jnp.mean(o_c, axis=-1, keepdims=True)
            rstd = lax.rsqrt(jnp.mean(oc * oc, axis=-1, keepdims=True) + EPS)
            yn = oc * rstd
            gb = gb_ref[rows, :].astype(F32)
            sg = _sigmoid(gb)
            dyb_c = dyb_ref[rows, :]
            dgn = dyb_c * (gb * sg)
            dp_ref[3, rows, :] = (dyb_c * (yn * gain_v) * (sg * (1.0 + gb * (1.0 - sg)))).astype(BF16)
            dgain_ref[...] += jnp.sum(dgn * yn, axis=0, keepdims=True)
            dyn = dgn * gain_v
            do = rstd * (dyn - jnp.mean(dyn, axis=-1, keepdims=True)
                         - yn * jnp.mean(dyn * yn, axis=-1, keepdims=True))
            dob = do.astype(BF16)
            dox = (do * cd_ref[...]).astype(BF16)

            q_c, k_c = qr_ref[rows, :], kr_ref[rows, :]
            vb = v_ref[rows, :]
            v = vb.astype(F32)
            vs = (v * sd_ref[...]).astype(BF16)
            rb = rs_ref[c]
            d_r = dr_s[...]
            drb = d_r.astype(BF16)
            dm = dm_ref[...]
            p = (_dot_nt(q_c, k_c) * dm).astype(BF16)
            dpm = (_dot_nt(dob, vb) * dm).astype(BF16)
            dq = _dot(dpm, k_c) + _dot_nt(dox, rb)
            dk = _dot_tn(dpm, q_c) + _dot_nt(vs, drb)
            dv = _dot_tn(p, dob) + _dot(k_c, drb) * sd_ref[...]
            dr_s[...] = chunk_decay * d_r + _dot_tn(q_c, dox)

            cos, sin = cos_ref[rows, :], sin_ref[rows, :]
            dp_ref[0, rows, :] = _rotate_back(dq, cos, sin).astype(BF16)
            dp_ref[1, rows, :] = (_rotate_back(dk, cos, sin) * (DK ** -0.5)).astype(BF16)
            dp_ref[2, rows, :] = dv.astype(BF16)
            return carry

        lax.fori_loop(0, NCH, chunk, 0)

        @pl.when(jnp.logical_and(pl.program_id(0) == HEADS - 1, pl.program_id(1) == nb - 1))
        def _():
            for cp in recvs:
                cp.wait_recv()
            for cp in sends:
                cp.wait_send()
            for cp in mine:
                cp.wait()

    seg = lambda s: pl.BlockSpec((None, S, DK), lambda h, b: (s, b, h))
    tab = pl.BlockSpec((None, CH, DK), lambda h, b: (h, 0, 0))
    rot = pl.BlockSpec((S, DK // 2), lambda h, b: (0, 0))
    blk = pl.BlockSpec((S, DK), lambda h, b: (b, h))
    one = pl.BlockSpec((None, 1, DK), lambda h, b: (h, 0, 0))
    return pl.pallas_call(
        body, name="ret_bwd", grid=(HEADS, nb),
        in_specs=[blk, blk, seg(4), seg(5), blk, blk,
                  pl.BlockSpec((None, None, NCH, DK, DK), lambda h, b: (b, h, 0, 0, 0)),
                  one, tab, tab, tab, rot, rot] + [ANY] * ns,
        out_specs=[pl.BlockSpec((4, S, DK), lambda h, b: (0, b, h)), one] + [ANY] * ns,
        out_shape=[jax.ShapeDtypeStruct((4, t, D), BF16), jax.ShapeDtypeStruct((HEADS, 1, DK), F32)]
        + [jax.ShapeDtypeStruct(a.shape, a.dtype) for a in sums],
        scratch_shapes=[pltpu.VMEM((DK, DK), F32), pltpu.SemaphoreType.DMA((ns, 3)), pltpu.SemaphoreType.DMA((ns, 3)),
                        pltpu.SemaphoreType.DMA((ns,))],
        compiler_params=_params(("arbitrary", "arbitrary")),
    )(qr, kr, proj, proj, o, dyb, rs, gain, inner_t, cross_t, state_t, cos_t, sin_t, *sums)


def _wblock(k):
    return pl.BlockSpec((NDEV, D // NDEV, D), lambda i: (0, k, 0))


def _tail(ya, yb, proj, x2d, tgt, wg, g_fin):
    t = x2d.shape[0]
    tm = 256

    def body(ya_ref, yb_ref, ma_ref, mb_ref, x_ref, t_ref, wa_ref, wb_ref, wo_ref, g_ref,
             dx2_ref, dya_ref, dyb_ref, dm_ref, mg_ref, doa_ref, dob_ref, gfin_ref, loss_ref):
        i = pl.program_id(0)

        @pl.when(i == 0)
        def _():
            gfin_ref[...] = jnp.zeros_like(gfin_ref)
            loss_ref[...] = jnp.zeros_like(loss_ref)

        wa = wa_ref[...].reshape(D, D)
        wb = wb_ref[...].reshape(D, D)
        wo = wo_ref[...].reshape(D, D)
        out_a = _dot(ya_ref[...], wa)
        out_b = _dot(yb_ref[...], wb)
        sa = _sigmoid(ma_ref[...].astype(F32))
        sb = _sigmoid(mb_ref[...].astype(F32))
        merged = (sa * out_a + sb * out_b).astype(BF16)
        mg_ref[...] = merged
        x2 = x_ref[...] + _dot(merged, wo)
        r2 = lax.rsqrt(jnp.mean(x2 * x2, axis=-1, keepdims=True) + EPS)
        xh = x2 * r2
        g = g_ref[...]
        err = xh * g - t_ref[...]
        loss_ref[...] += jnp.sum(err * err, axis=0, keepdims=True) * (0.5 / D)
        dy = err * (1.0 / D)
        gfin_ref[...] += jnp.sum(dy * xh, axis=0, keepdims=True)
        dxh = dy * g
        dx2 = r2 * (dxh - xh * jnp.mean(dxh * xh, axis=-1, keepdims=True))
        dx2_ref[...] = dx2
        dmerged = _dot_nt(dx2.astype(BF16), wo)
        doa = (sa * dmerged).astype(BF16)
        dob = (sb * dmerged).astype(BF16)
        doa_ref[...] = doa
        dob_ref[...] = dob
        dm_ref[0] = (dmerged * out_a * sa * (1.0 - sa)).astype(BF16)
        dm_ref[1] = (dmerged * out_b * sb * (1.0 - sb)).astype(BF16)
        dya_ref[...] = _dot_nt(doa, wa)
        dyb_ref[...] = _dot_nt(dob, wb)

    row = lambda: pl.BlockSpec((tm, D), lambda i: (i, 0))
    seg = lambda s: pl.BlockSpec((None, tm, D), lambda i: (s, i, 0))
    vec = pl.BlockSpec((1, D), lambda i: (0, 0))
    return pl.pallas_call(
        body, name="tail", grid=(t // tm,),
        in_specs=[row(), row(), seg(6), seg(7), row(), row(), _wblock(0), _wblock(1), _wblock(2), vec],
        out_specs=[row(), row(), row(), pl.BlockSpec((2, tm, D), lambda i: (0, i, 0)),
                   row(), row(), row(), vec, vec],
        out_shape=[jax.ShapeDtypeStruct((t, D), F32), jax.ShapeDtypeStruct((t, D), F32),
                   jax.ShapeDtypeStruct((t, D), F32), jax.ShapeDtypeStruct((2, t, D), BF16),
                   jax.ShapeDtypeStruct((t, D), BF16), jax.ShapeDtypeStruct((t, D), BF16),
                   jax.ShapeDtypeStruct((t, D), BF16), jax.ShapeDtypeStruct((1, D), F32),
                   jax.ShapeDtypeStruct((1, D), F32)],
        compiler_params=_params(("arbitrary",)),
    )(ya, yb, proj, proj, x2d, tgt, wg, wg, wg, g_fin)


def _tail_wgrad(ya, yb, merged, doa, dob, dx2):
    t = ya.shape[0]
    tm = 512

    def body(ya_ref, yb_ref, mg_ref, doa_ref, dob_ref, dx2_ref, ga_ref, gb_ref, go_ref):
        @pl.when(pl.program_id(0) == 0)
        def _():
            ga_ref[...] = jnp.zeros_like(ga_ref)
            gb_ref[...] = jnp.zeros_like(gb_ref)
            go_ref[...] = jnp.zeros_like(go_ref)

        ga_ref[...] += _dot_tn(ya_ref[...], doa_ref[...])
        gb_ref[...] += _dot_tn(yb_ref[...], dob_ref[...])
        go_ref[...] += _dot_tn(mg_ref[...], dx2_ref[...].astype(BF16))

    row = lambda: pl.BlockSpec((tm, D), lambda i: (i, 0))
    full = lambda: pl.BlockSpec((D, D), lambda i: (0, 0))
    return pl.pallas_call(
        body, name="tail_wgrad", grid=(t // tm,),
        in_specs=[row() for _ in range(6)], out_specs=[full(), full(), full()],
        out_shape=[jax.ShapeDtypeStruct((D, D), F32)] * 3,
        compiler_params=_params(("arbitrary",)),
    )(ya, yb, merged, doa, dob, dx2)


def _dproj_specs(tm, j_of, i_of):
    last = lambda j, i, lo, n: (jnp.clip(j - lo, 0, n - 1), i, 0)
    return [pl.BlockSpec((None, tm, D), lambda a, b: last(j_of(a, b), i_of(a, b), 0, 2)),
            pl.BlockSpec((None, tm, D), lambda a, b: last(j_of(a, b), i_of(a, b), 2, 4)),
            pl.BlockSpec((None, tm, D), lambda a, b: last(j_of(a, b), i_of(a, b), 6, 2))]


def _dproj_specs_ordered(tm):
    def spec(lo, n):
        def index(k, i, order_ref):
            seg = order_ref[k]
            mine = jnp.logical_and(seg >= lo, seg < lo + n)
            return jnp.where(mine, seg - lo, 0), jnp.where(mine, i, 0), 0
        return pl.BlockSpec((None, tm, D), index)
    return [spec(0, 2), spec(2, 4), spec(6, 2)]


def _dproj_pick(j, da_ref, db_ref, dc_ref, use):
    @pl.when(j < 2)
    def _():
        use(da_ref[...])

    @pl.when(jnp.logical_and(j >= 2, j < 6))
    def _():
        use(db_ref[...])

    @pl.when(j >= 6)
    def _():
        use(dc_ref[...])


def _rs_schedule(q, c):
    steps = []
    for s in range(3):
        d_a = lax.rem(q + 1 + s, 4)
        d_b = lax.rem(q + 1 + (s + 1) % 3, 4)
        steps.append((jnp.where(c == 0, d_a, d_b), jnp.where(c == 0, d_b, d_a)))
    steps.append((q, q))
    return steps


def _rs_order(q, c):
    order = []
    for keep, give in _rs_schedule(q, c):
        order += [2 * give + 1 - c, 2 * keep + c]
    return jnp.stack(order).astype(jnp.int32)


def _inproj_wgrad_rs(h, dpa, dpb, dpc, order, smalls):
    t = h.shape[0]
    tm = 1024
    nt = t // tm
    nsm = len(smalls)

    def body(order_ref, h_ref, da_ref, db_ref, dc_ref, *rest):
        small_refs, parts_ref, rest = rest[:nsm], rest[nsm], rest[nsm + 1:]
        all_refs, rest = rest[:nsm], rest[nsm:]
        (acc, sib, outb, give_send, give_recv, sum_send, sum_recv, own_sem,
         small_send, small_recv, small_own) = rest
        k, i = pl.program_id(0), pl.program_id(1)
        x, y, c = _place()
        schedule = _rs_schedule(2 * x + y, c)
        own, first, arrive, forward, others = _gather_copies(small_refs, all_refs, small_send, small_recv, small_own)

        @pl.when(jnp.logical_and(k == 0, i == 0))
        def _():
            for cp in own + first:
                cp.start()

        @pl.when(jnp.logical_and(k == 2, i == 0))
        def _():
            for came, on in zip(arrive, forward):
                came.wait_recv()
                on.start()

        def use(d):
            @pl.when(i == 0)
            def _():
                acc[k % 2] = _dot_tn(h_ref[...], d)

            @pl.when(i > 0)
            def _():
                acc[k % 2] += _dot_tn(h_ref[...], d)

        _dproj_pick(order_ref[k], da_ref, db_ref, dc_ref, use)

        def give_copy(s):
            return pltpu.make_async_remote_copy(
                src_ref=acc.at[0], dst_ref=sib.at[s % 2], send_sem=give_send.at[s], recv_sem=give_recv.at[s],
                device_id=(x, y, 1 - c), device_id_type=MESH)

        def sum_copy(s):
            keep = schedule[s][0]
            return pltpu.make_async_remote_copy(
                src_ref=outb.at[s], dst_ref=parts_ref.at[s], send_sem=sum_send.at[s], recv_sem=sum_recv.at[s],
                device_id=(keep // 2, lax.rem(keep, 2), c), device_id_type=MESH)

        own_copy = pltpu.make_async_copy(outb.at[3], parts_ref.at[3], own_sem)

        for s in range(4):
            @pl.when(jnp.logical_and(k == 2 * s, i == nt - 1))
            def _():
                give_copy(s).start()

            @pl.when(jnp.logical_and(k == 2 * s + 1, i == nt - 1))
            def _():
                give_copy(s).wait_recv()
                outb[s] = (acc[1] + sib[s % 2]).astype(BF16)
                give_copy(s).wait_send()
                if s < 3:
                    sum_copy(s).start()
                else:
                    own_copy.start()

        @pl.when(jnp.logical_and(k == NSEG - 1, i == nt - 1))
        def _():
            for s in range(3):
                sum_copy(s).wait_recv()
            for s in range(3):
                sum_copy(s).wait_send()
            own_copy.wait()
            for cp in others:
                cp.wait_recv()
            for cp in first + forward:
                cp.wait_send()
            for cp in own:
                cp.wait()

    return pl.pallas_call(
        body, name="inproj_wgrad_rs",
        grid_spec=pltpu.PrefetchScalarGridSpec(
            num_scalar_prefetch=1, grid=(NSEG, nt),
            in_specs=[pl.BlockSpec((tm, D), lambda k, i, order_ref: (i, 0))] + _dproj_specs_ordered(tm) + [ANY] * nsm,
            out_specs=[ANY] * (1 + nsm),
            scratch_shapes=[pltpu.VMEM((2, D, D), F32), pltpu.VMEM((2, D, D), F32), pltpu.VMEM((4, D, D), BF16),
                            pltpu.SemaphoreType.DMA((4,)), pltpu.SemaphoreType.DMA((4,)),
                            pltpu.SemaphoreType.DMA((3,)), pltpu.SemaphoreType.DMA((3,)),
                            pltpu.SemaphoreType.DMA,
                            pltpu.SemaphoreType.DMA((nsm, 7)), pltpu.SemaphoreType.DMA((nsm, 7)),
                            pltpu.SemaphoreType.DMA((nsm,))]),
        out_shape=[jax.ShapeDtypeStruct((4, D, D), BF16)]
        + [jax.ShapeDtypeStruct((NDEV,) + a.shape, a.dtype) for a in smalls],
        compiler_params=_params(("arbitrary", "arbitrary")),
    )(order, h, dpa, dpb, dpc, *smalls)


def _inproj_dgrad(dpa, dpb, dpc, wg, x2d, dx2, g_in):
    t = x2d.shape[0]
    tm = 1024

    def body(da_ref, db_ref, dc_ref, w_ref, x_ref, dx2_ref, g_ref, gx_ref, gg_ref, acc_s):
        i, j = pl.program_id(0), pl.program_id(1)

        @pl.when(jnp.logical_and(i == 0, j == 0))
        def _():
            gg_ref[...] = jnp.zeros_like(gg_ref)

        @pl.when(j == 0)
        def _():
            acc_s[...] = jnp.zeros_like(acc_s)

        def use(d):
            acc_s[...] += _dot_nt(d, w_ref[...])

        _dproj_pick(j, da_ref, db_ref, dc_ref, use)

        @pl.when(j == NSEG - 1)
        def _():
            x = x_ref[...]
            r = lax.rsqrt(jnp.mean(x * x, axis=-1, keepdims=True) + EPS)
            xh = x * r
            dh = acc_s[...]
            gg_ref[...] += jnp.sum(dh * xh, axis=0, keepdims=True)
            dxh = dh * g_ref[...]
            gx_ref[...] = dx2_ref[...] + r * (dxh - xh * jnp.mean(dxh * xh, axis=-1, keepdims=True))

    row = lambda: pl.BlockSpec((tm, D), lambda i, j: (i, 0))
    vec = pl.BlockSpec((1, D), lambda i, j: (0, 0))
    return pl.pallas_call(
        body, name="inproj_dgrad", grid=(t // tm, NSEG),
        in_specs=_dproj_specs(tm, lambda i, j: j, lambda i, j: i)
        + [pl.BlockSpec((None, D, D), lambda i, j: (j, 0, 0)), row(), row(), vec],
        out_specs=[row(), vec],
        out_shape=[jax.ShapeDtypeStruct((t, D), F32), jax.ShapeDtypeStruct((1, D), F32)],
        scratch_shapes=[pltpu.VMEM((tm, D), F32)],
        compiler_params=_params(("arbitrary", "arbitrary")),
    )(dpa, dpb, dpc, wg, x2d, dx2, g_in)


def _adam_update(g, w, m, v):
    m_new = ADAM_B1 * m + (1.0 - ADAM_B1) * g
    v_new = ADAM_B2 * v + (1.0 - ADAM_B2) * (g * g)
    m_hat = m_new / (1.0 - ADAM_B1 ** ADAM_STEP)
    v_hat = v_new / (1.0 - ADAM_B2 ** ADAM_STEP)
    return -ADAM_LR * (m_hat / (jnp.sqrt(v_hat) + ADAM_EPS) + ADAM_WD * w), m_new, v_new


def _sum_in_order(ref):
    total = ref[0].astype(F32)
    for k in range(1, ref.shape[0]):
        total = total + ref[k].astype(F32)
    return total


def _adamw_small(me, vec_all, gx_all, ga_all, groups):
    flat = [a for grp in groups for a in grp]
    ng = len(groups)
    nshard = D // NDEV

    def body(me_ref, vec_ref, shard_ref, gx_ref, ga_ref, *refs):
        ins, outs = refs[:3 * ng], refs[3 * ng:]
        vec = _sum_in_order(vec_ref)
        shard = _sum_in_order(shard_ref)
        grads = [vec[r:r + 1, :] for r in range(6)]
        grads += [shard[0:4, :], shard[4:8, 0:DK // NDEV], _sum_in_order(gx_ref), _sum_in_order(ga_ref)]
        for n, g in enumerate(grads):
            delta, m_new, v_new = _adam_update(g, ins[3 * n][...], ins[3 * n + 1][...], ins[3 * n + 2][...])
            outs[4 * n][...] = g
            outs[4 * n + 1][...] = delta
            outs[4 * n + 2][...] = m_new
            outs[4 * n + 3][...] = v_new
        outs[4 * ng][...] = jnp.sum(vec[6:7, :], axis=1, keepdims=True)

    full = lambda a: pl.BlockSpec(a.shape, lambda i, me_ref, nd=len(a.shape): (0,) * nd)
    out_shape = [jax.ShapeDtypeStruct(w.shape, F32) for w, _, _ in groups for _ in range(4)]
    out_shape.append(jax.ShapeDtypeStruct((1, 1), F32))
    outs = pl.pallas_call(
        body, name="adamw_small",
        grid_spec=pltpu.PrefetchScalarGridSpec(
            num_scalar_prefetch=1, grid=(1,),
            in_specs=[full(vec_all),
                      pl.BlockSpec((NDEV, 8, nshard), lambda i, me_ref: (0, 1, me_ref[0])),
                      full(gx_all), full(ga_all)] + [full(a) for a in flat],
            out_specs=[full(s) for s in out_shape]),
        out_shape=out_shape,
        compiler_params=_params(("arbitrary",)),
    )(me, vec_all, vec_all, gx_all, ga_all, *flat)
    return [outs[4 * n:4 * n + 4] for n in range(ng)], outs[4 * ng]


def _adamw(name, items):
    n, rows, cols = items[0][0].shape
    tr = rows if rows <= 256 else 256
    k = len(items)

    def body(*refs):
        for a in range(k):
            p_ref, w_ref, m_ref, v_ref = refs[4 * a:4 * a + 4]
            g = _sum_in_order(p_ref)
            delta, m_new, v_new = _adam_update(g, w_ref[...], m_ref[...], v_ref[...])
            for o, val in zip(refs[4 * k + 4 * a:4 * k + 4 * a + 4], (g, delta, m_new, v_new)):
                o[...] = val

    blk = lambda: pl.BlockSpec((tr, cols), lambda i: (i, 0))
    outs = pl.pallas_call(
        body, name=name, grid=(rows // tr,),
        in_specs=[pl.BlockSpec((n, tr, cols), lambda i: (0, i, 0)), blk(), blk(), blk()] * k,
        out_specs=[blk() for _ in range(4 * k)],
        out_shape=[jax.ShapeDtypeStruct((rows, cols), F32)] * (4 * k),
        compiler_params=_params(("arbitrary",)),
    )(*[a for item in items for a in item])
    return [outs[4 * a:4 * a + 4] for a in range(k)]


ANY = pl.BlockSpec(memory_space=pl.ANY)


def _place():
    return lax.axis_index("x"), lax.axis_index("y"), lax.axis_index("c")


def _gather_copies(ins, outs, send_sems, recv_sems, own_sems):
    x, y, c = _place()
    me, sibling = (x, y, c), (x, y, 1 - c)
    chips = [(1 - x, y), (x, 1 - y), (1 - x, 1 - y)]
    n = len(ins)

    def copy(a, k, block, to, src=None):
        px, py, pc = block
        dst = outs[a].at[4 * px + 2 * py + pc]
        return pltpu.make_async_remote_copy(
            src_ref=dst if src is None else src, dst_ref=dst,
            send_sem=send_sems.at[a, k], recv_sem=recv_sems.at[a, k], device_id=to, device_id_type=MESH)

    own = [pltpu.make_async_copy(ins[a], outs[a].at[4 * x + 2 * y + c], own_sems.at[a]) for a in range(n)]
    first = []
    for a in range(n):
        first.append(copy(a, 0, me, sibling, src=ins[a]))
        first += [copy(a, 1 + j, me, (*chip, c), src=ins[a]) for j, chip in enumerate(chips)]
    arrive = [copy(a, 1 + j, (*chip, c), me) for j, chip in enumerate(chips) for a in range(n)]
    forward = [copy(a, 4 + j, (*chip, c), sibling) for j, chip in enumerate(chips) for a in range(n)]
    rest = [copy(a, 0, sibling, me) for a in range(n)]
    rest += [copy(a, 4 + j, (*chip, 1 - c), me) for a in range(n) for j, chip in enumerate(chips)]
    return own, first, arrive, forward, rest


def _sibling_copies(ins, outs, send_sems, recv_sems):
    x, y, c = _place()
    return [pltpu.make_async_remote_copy(
        src_ref=ins[a].at[2 * q + 1 - c], dst_ref=outs[a].at[q],
        send_sem=send_sems.at[a, q], recv_sem=recv_sems.at[a, q],
        device_id=(x, y, 1 - c), device_id_type=MESH) for a in range(len(ins)) for q in range(4)]


def _chip_copies(ins, outs, send_sems, recv_sems, local_sems):
    x, y, c = _place()
    my_chip = 2 * x + y
    chips = [(1 - x, y), (x, 1 - y), (1 - x, 1 - y)]
    n = len(ins)
    mine = [pltpu.make_async_copy(ins[a].at[my_chip], outs[a].at[my_chip], local_sems.at[a]) for a in range(n)]
    sends = [pltpu.make_async_remote_copy(
        src_ref=ins[a].at[2 * px + py], dst_ref=outs[a].at[my_chip],
        send_sem=send_sems.at[a, j], recv_sem=recv_sems.at[a, j],
        device_id=(px, py, c), device_id_type=MESH) for a in range(n) for j, (px, py) in enumerate(chips)]
    recvs = [pltpu.make_async_remote_copy(
        src_ref=ins[a].at[my_chip], dst_ref=outs[a].at[2 * px + py],
        send_sem=send_sems.at[a, j], recv_sem=recv_sems.at[a, j],
        device_id=(px, py, c), device_id_type=MESH) for a in range(n) for j, (px, py) in enumerate(chips)]
    return mine, sends, recvs


def _chip_sum(owns, gots, core):
    n = len(owns)
    _, rows, cols = owns[0].shape

    def body(core_ref, *refs):
        for a in range(n):
            refs[2 * n + a][...] = (refs[a][...] + refs[n + a][...]).astype(BF16)

    own_spec = pl.BlockSpec((None, rows, cols), lambda q, core_ref: (2 * q + core_ref[0], 0, 0))
    slab = pl.BlockSpec((None, rows, cols), lambda q, core_ref: (q, 0, 0))
    return pl.pallas_call(
        body, name="chip_sum",
        grid_spec=pltpu.PrefetchScalarGridSpec(
            num_scalar_prefetch=1, grid=(4,),
            in_specs=[own_spec] * n + [slab] * n, out_specs=[slab] * n),
        out_shape=[jax.ShapeDtypeStruct((4, rows, cols), BF16)] * n,
        compiler_params=_params(("arbitrary",)),
    )(core, *owns, *gots)


def _block_diag(w):
    w4 = w.reshape(NCB, 4, 64, 64)
    eye = jnp.eye(4, dtype=w.dtype)
    return (w4[:, :, :, None, :] * eye[None, :, None, :, None]).reshape(NCB, CB, CB)


def _block_diag_back(g):
    g5 = g.reshape(NCB, 4, 64, 4, 64)
    return jnp.stack([g5[:, m, :, m, :] for m in range(4)], axis=1).reshape(16, 64, 64)


def kernel(x, norm_in, w_in, conv_w, conv_b, gate_x_w, gate_x_b, gate_a_w, gate_a_b, lru_lambda, gn_gain, w_proj_a, w_proj_b, w_out, norm_final, loss_target, m_norm_in, m_w_in, m_conv_w, m_conv_b, m_gate_x_w, m_gate_x_b, m_gate_a_w, m_gate_a_b, m_lru_lambda, m_gn_gain, m_w_proj_a, m_w_proj_b, m_w_out, m_norm_final, v_norm_in, v_w_in, v_conv_w, v_conv_b, v_gate_x_w, v_gate_x_b, v_gate_a_w, v_gate_a_b, v_lru_lambda, v_gn_gain, v_w_proj_a, v_w_proj_b, v_w_out, v_norm_final):
    xi, yi, ci = _place()
    me = 4 * xi + 2 * yi + ci
    core = ci.astype(jnp.int32).reshape(1)
    nshard = D // NDEV
    nb = x.shape[0]
    t = nb * S
    x2d = x.reshape(t, D)
    tgt2d = loss_target.reshape(t, D)
    g_final = norm_final.reshape(1, D)
    wbd = jnp.concatenate([_block_diag(gate_x_w[0]), _block_diag(gate_a_w[0])], axis=-1).astype(BF16)
    tables = _retention_tables()

    wp_own = jnp.concatenate([w_proj_a[0], w_proj_b[0], w_out[0]], axis=0).astype(BF16)
    tiny = jnp.concatenate([conv_w[0], jnp.pad(gn_gain[0], ((0, 0), (0, nshard - DK // NDEV)))], axis=0)
    proj, h, wg, tiny_g = _inproj_gather(x2d, norm_in, w_in[0].astype(BF16), tiny, *_gather_order(xi, yi, ci))
    conv_w_full = tiny_g[:, 0:4, :].transpose(1, 0, 2).reshape(4, D)
    gain3 = tiny_g[:, 4:8, :DK // NDEV].transpose(1, 0, 2).reshape(HEADS, 1, DK)

    ya, hs, xc, gi, gr = _lru_fwd(proj, conv_w_full, conv_b, wbd, gate_x_b, gate_a_b, lru_lambda, nb)
    yb, qr, kr, o, rs, wpg = _ret_fwd(proj, gain3, tables, nb, wp_own)
    dx2, dya, dyb, dpc, merged, doa, dob, g_fin, loss_vec = _tail(ya, yb, proj, x2d, tgt2d, wpg, g_final)
    g_pa, g_pb, g_out = _tail_wgrad(ya, yb, merged, doa, dob, dx2)

    own = [g.reshape(NDEV, nshard, D) for g in (g_pa, g_pb, g_out)]
    dpa, g_wbd, g_vec, *got = _lru_bwd(proj, hs, xc, gi, gr, dya, conv_w_full, wbd, lru_lambda, nb, own)
    sums = _chip_sum(own, got, core)
    dpb, g_gain, *parts = _ret_bwd(proj, qr, kr, o, rs, dyb, gain3, tables, nb, sums)

    grad_x, g_norm_in = _inproj_dgrad(dpa, dpb, dpc, wg, x2d, dx2, norm_in)
    grad_x = grad_x.reshape(nb, S, D)

    gain_rows = jnp.pad(g_gain.reshape(HEADS, NDEV, DK // NDEV), ((0, 0), (0, 0), (0, nshard - DK // NDEV)))
    vec = jnp.concatenate([g_norm_in, g_vec[0:4], g_fin, loss_vec, jnp.zeros((1, D), F32), g_vec[4:8],
                           gain_rows.reshape(HEADS, D)], axis=0)
    g_gx = _block_diag_back(g_wbd[:, :, :CB]).reshape(D // 2, 128)
    g_ga = _block_diag_back(g_wbd[:, :, CB:]).reshape(D // 2, 128)
    parts_in, vec_all, gx_all, ga_all = _inproj_wgrad_rs(h, dpa, dpb, dpc, _rs_order(2 * xi + yi, ci),
                                                         [vec, g_gx, g_ga])
    gx_all = gx_all.reshape(NDEV, D, 64)
    ga_all = ga_all.reshape(NDEV, D, 64)
    parts = [parts_in] + list(parts)

    res = {}
    (out,) = _adamw("adamw_w_in", [(parts[0], w_in[0], m_w_in[0], v_w_in[0])])
    res["w_in"] = [o[None] for o in out]
    square = [("w_proj_a", w_proj_a, m_w_proj_a, v_w_proj_a), ("w_proj_b", w_proj_b, m_w_proj_b, v_w_proj_b),
              ("w_out", w_out, m_w_out, v_w_out)]
    outs = _adamw("adamw_square", [(parts[1 + k], w[0], m[0], v[0]) for k, (_, w, m, v) in enumerate(square)])
    for (nm, _, _, _), out in zip(square, outs):
        res[nm] = [o[None] for o in out]

    row = lambda a: a.reshape(1, D)
    gate = lambda a: a.reshape(D, 64)
    groups = [("norm_in", norm_in, m_norm_in, v_norm_in, row), ("conv_b", conv_b, m_conv_b, v_conv_b, row),
              ("gate_x_b", gate_x_b, m_gate_x_b, v_gate_x_b, row), ("gate_a_b", gate_a_b, m_gate_a_b, v_gate_a_b, row),
              ("lru_lambda", lru_lambda, m_lru_lambda, v_lru_lambda, row),
              ("norm_final", norm_final, m_norm_final, v_norm_final, row),
              ("conv_w", conv_w, m_conv_w, v_conv_w, lambda a: a[0]), ("gn_gain", gn_gain, m_gn_gain, v_gn_gain, lambda a: a[0]),
              ("gate_x_w", gate_x_w, m_gate_x_w, v_gate_x_w, gate), ("gate_a_w", gate_a_w, m_gate_a_w, v_gate_a_w, gate)]
    small_out, loss = _adamw_small(me.astype(jnp.int32).reshape(1), vec_all, gx_all, ga_all,
                                   [tuple(view(a) for a in (w, m, v)) for _, w, m, v, view in groups])
    for (nm, w, _, _, _), out in zip(groups, small_out):
        res[nm] = [o.reshape(w.shape) for o in out]
    loss = loss.reshape(())

    order = ["norm_in", "w_in", "conv_w", "conv_b", "gate_x_w", "gate_x_b", "gate_a_w", "gate_a_b", "lru_lambda",
             "gn_gain", "w_proj_a", "w_proj_b", "w_out", "norm_final"]
    outs = [loss, grad_x]
    for k in range(4):
        outs += [res[nm][k] for nm in order]
    return tuple(outs)
```

```python
import numpy as np

import jax
import jax.numpy as jnp
from jax import lax
from jax.experimental import pallas as pl
from jax.experimental.pallas import tpu as pltpu

F32 = jnp.float32
BF16 = jnp.bfloat16
MESH = pl.DeviceIdType.MESH

D = 1024
S = 2048
NSEG = 8
NDEV = 8
HEADS = 4
DK = 256
CH = 256
NCH = S // CH
CB = 256
NCB = D // CB
RC = 128
SCAN_GROUP = 8
EPS = 1e-6
LRU_C = 8.0
VMEM_LIMIT = 56 * 1024 * 1024

ADAM_LR = 0.001
ADAM_B1 = 0.9
ADAM_B2 = 0.999
ADAM_EPS = 1e-08
ADAM_WD = 0.01
ADAM_STEP = 10


def _params(sem=None):
    return pltpu.CompilerParams(dimension_semantics=sem, vmem_limit_bytes=VMEM_LIMIT)


def _dot(a, b):
    return jnp.dot(a, b, preferred_element_type=F32)


def _dot_nt(a, b):
    return lax.dot_general(a, b, (((1,), (1,)), ((), ())), preferred_element_type=F32)


def _dot_tn(a, b):
    return lax.dot_general(a, b, (((0,), (0,)), ((), ())), preferred_element_type=F32)


def _sigmoid(x):
    return jax.nn.sigmoid(x)


def _expm1_nonpos(x):
    poly = x * (1.0 + x * (0.5 + x * (1.0 / 6.0 + x * (1.0 / 24.0))))
    return jnp.where(x > -0.05, poly, jnp.exp(x) - 1.0)


def _softplus(x):
    return jnp.maximum(x, 0.0) + jnp.log(1.0 + jnp.exp(-jnp.abs(x)))


def _rows(c, n):
    return pl.ds(pl.multiple_of(c * n, n), n)


def _window_before(ref, c, n):
    r0 = c * n
    if ref.dtype == BF16:
        prev = ref[pl.ds(pl.multiple_of(jnp.maximum(r0 - 16, 0), 16), 16), :].astype(F32)[8:, :]
    else:
        prev = ref[pl.ds(pl.multiple_of(jnp.maximum(r0 - 8, 0), 8), 8), :]
    prev = jnp.where(c > 0, prev, 0.0)
    return jnp.concatenate([prev, ref[_rows(c, n), :].astype(F32)], axis=0)


def _shift_down(win, s, n):
    if s == 0:
        return win[8:, :]
    return pltpu.roll(win, s, 0)[8:, :]


def _shift_up(win, s, n):
    if s == 0:
        return win[:n, :]
    return pltpu.roll(win, n + 8 - s, 0)[:n, :]


HALF = D // 2
GATHER_SLOTS = [("own", None, 0), ("own", None, 1), ("sib", None, 0), ("sib", None, 1)]
for _j, _h in ((0, 0), (1, 0), (0, 1), (1, 1), (2, 0), (2, 1)):
    GATHER_SLOTS += [("ici", _j, _h), ("fwd", _j, _h)]
NSLOT = len(GATHER_SLOTS)


def _gather_order(x, y, c):
    chips = [(1 - x, y), (x, 1 - y), (1 - x, 1 - y)]
    segs, halves = [], []
    for kind, j, h in GATHER_SLOTS:
        if kind == "own":
            seg = 4 * x + 2 * y + c
        elif kind == "sib":
            seg = 4 * x + 2 * y + 1 - c
        else:
            px, py = chips[j]
            seg = 4 * px + 2 * py + (c if kind == "ici" else 1 - c)
        segs.append(seg)
        halves.append(h)
    return jnp.stack(segs).astype(jnp.int32), jnp.asarray(halves, jnp.int32)


def _inproj_gather(x2d, g_in, w_own, tiny_own, order, halves):
    t = x2d.shape[0]
    tm = 1024
    nt = t // tm

    def body(order_ref, half_ref, x_ref, g_ref, w_own_ref, tiny_own_ref,
             proj_ref, h_ref, wg_ref, tinyg_ref,
             w_all, h_all, send_sems, recv_sems, own_sems, out_sems, tiny_send, tiny_recv, tiny_own_sem):
        k, i = pl.program_id(0), pl.program_id(1)
        x, y, c = _place()
        me, sibling = (x, y, c), (x, y, 1 - c)
        mine = 4 * x + 2 * y + c
        chips = [(1 - x, y), (x, 1 - y), (1 - x, 1 - y)]

        def copy(h, n, block, to, own_src=False):
            px, py, pc = block
            dst = w_all.at[4 * px + 2 * py + pc, h]
            return pltpu.make_async_remote_copy(
                src_ref=w_own_ref.at[:, pl.ds(h * HALF, HALF)] if own_src else dst, dst_ref=dst,
                send_sem=send_sems.at[h, n], recv_sem=recv_sems.at[h, n], device_id=to, device_id_type=MESH)

        def tiny_copy(n, block, to, own_src=False):
            px, py, pc = block
            dst = tinyg_ref.at[4 * px + 2 * py + pc]
            return pltpu.make_async_remote_copy(
                src_ref=tiny_own_ref if own_src else dst, dst_ref=dst,
                send_sem=tiny_send.at[n], recv_sem=tiny_recv.at[n], device_id=to, device_id_type=MESH)

        def own_copy(h):
            return pltpu.make_async_copy(w_own_ref.at[:, pl.ds(h * HALF, HALF)], w_all.at[mine, h], own_sems.at[h])

        tiny_mine = pltpu.make_async_copy(tiny_own_ref, tinyg_ref.at[mine], tiny_own_sem)

        def keep_copy(n):
            h = GATHER_SLOTS[n][2]
            return pltpu.make_async_copy(w_all.at[order_ref[n], h], wg_ref.at[order_ref[n], :, pl.ds(h * HALF, HALF)],
                                         out_sems.at[n])

        near = [(0, sibling), (1, (*chips[0], c)), (2, (*chips[1], c))]
        first = [copy(h, n, me, to, True) for h in (0, 1) for n, to in near]
        tiny_first = [tiny_copy(0, me, sibling, True)] + [tiny_copy(1 + j, me, (*chip, c), True) for j, chip in enumerate(chips)]

        def relay(h, j):
            seg = w_all.at[4 * chips[j][0] + 2 * chips[j][1] + c, h]
            return pltpu.make_async_remote_copy(
                src_ref=seg, dst_ref=seg, send_sem=send_sems.at[h, 3], recv_sem=recv_sems.at[h, 3],
                device_id=(*chips[1 - j], c), device_id_type=MESH)

        for n, (kind, j, h) in enumerate(GATHER_SLOTS):
            @pl.when(jnp.logical_and(k == n, i == 0))
            def _():
                if n == 0:
                    own_copy(0).start()
                    own_copy(1).start()
                    tiny_mine.start()
                    for cp in first + tiny_first:
                        cp.start()
                if kind == "own":
                    own_copy(h).wait()
                elif kind == "sib":
                    copy(h, 0, sibling, me).wait_recv()
                elif kind == "ici":
                    copy(h, 1 + j, (*chips[j], c), me).wait_recv()
                    copy(h, 4 + j, (*chips[j], c), sibling).start()
                    if j < 2:
                        @pl.when(c == j)
                        def _():
                            relay(h, j).start()
                else:
                    copy(h, 4 + j, (*chips[j], 1 - c), me).wait_recv()
                keep_copy(n).start()

        rows = pl.ds(pl.multiple_of(i * tm, tm), tm)

        @pl.when(k == 0)
        def _():
            xv = x_ref[...]
            r = lax.rsqrt(jnp.mean(xv * xv, axis=-1, keepdims=True) + EPS)
            hv = (xv * r * g_ref[...]).astype(BF16)
            h_ref[...] = hv
            h_all[rows, :] = hv

        proj_ref[...] = _dot(h_all[rows, :], w_all[order_ref[k], half_ref[k]]).astype(BF16)

        @pl.when(jnp.logical_and(k == NSLOT - 1, i == nt - 1))
        def _():
            for j, chip in enumerate(chips):
                tiny_copy(1 + j, (*chip, c), me).wait_recv()
                tiny_copy(4 + j, (*chip, c), sibling).start()
            tiny_copy(0, sibling, me).wait_recv()
            for j, chip in enumerate(chips):
                tiny_copy(4 + j, (*chip, 1 - c), me).wait_recv()
            for cp in first + tiny_first:
                cp.wait_send()
            for j, chip in enumerate(chips):
                tiny_copy(4 + j, (*chip, c), sibling).wait_send()
                for h in (0, 1):
                    copy(h, 4 + j, (*chip, c), sibling).wait_send()
            for h in (0, 1):
                relay(h, 0).wait_send()
            tiny_mine.wait()
            for n in range(NSLOT):
                keep_copy(n).wait()

    hold = lambda k, i, order_ref, half_ref: (jnp.where(k == 0, i, nt - 1), 0)
    return pl.pallas_call(
        body, name="inproj_gather",
        grid_spec=pltpu.PrefetchScalarGridSpec(
            num_scalar_prefetch=2, grid=(NSLOT, nt),
            in_specs=[pl.BlockSpec((tm, D), hold),
                      pl.BlockSpec((1, D), lambda k, i, order_ref, half_ref: (0, 0)),
                      ANY, ANY],
            out_specs=[pl.BlockSpec((None, tm, HALF), lambda k, i, order_ref, half_ref: (order_ref[k], i, half_ref[k])),
                       pl.BlockSpec((tm, D), hold),
                       ANY, ANY],
            scratch_shapes=[pltpu.VMEM((NDEV, 2, D, HALF), BF16), pltpu.VMEM((t, D), BF16),
                            pltpu.SemaphoreType.DMA((2, 7)), pltpu.SemaphoreType.DMA((2, 7)),
                            pltpu.SemaphoreType.DMA((2,)), pltpu.SemaphoreType.DMA((NSLOT,)),
                            pltpu.SemaphoreType.DMA((7,)), pltpu.SemaphoreType.DMA((7,)), pltpu.SemaphoreType.DMA]),
        out_shape=[jax.ShapeDtypeStruct((NSEG, t, D), BF16), jax.ShapeDtypeStruct((t, D), BF16),
                   jax.ShapeDtypeStruct((NDEV,) + w_own.shape, BF16),
                   jax.ShapeDtypeStruct((NDEV,) + tiny_own.shape, F32)],
        compiler_params=_params(("arbitrary", "arbitrary")),
    )(order, halves, x2d, g_in, w_own, tiny_own)


def _tile_scan(a, u):
    row = lax.broadcasted_iota(jnp.int32, a.shape, 0)
    for d in (1, 2, 4):
        m = row >= d
        a_sh = pltpu.roll(a, d, 0)
        u_sh = pltpu.roll(u, d, 0)
        u = jnp.where(m, a * u_sh + u, u)
        a = jnp.where(m, a * a_sh, a)
    return a, u


def _tile_scan_rev(a, w):
    row = lax.broadcasted_iota(jnp.int32, a.shape, 0)
    for d in (1, 2, 4):
        m = row < 8 - d
        a_sh = pltpu.roll(a, 8 - d, 0)
        w_sh = pltpu.roll(w, 8 - d, 0)
        w = jnp.where(m, a * w_sh + w, w)
        a = jnp.where(m, a * a_sh, a)
    return a, w


def _lru_gates(xa_ref, c, cw_ref, cb_ref, wbd_ref, bx_ref, ba_ref, sp):
    win = _window_before(xa_ref, c, RC)
    xc = cb_ref[...] + cw_ref[3:4, :] * _shift_down(win, 0, RC)
    for s in (1, 2, 3):
        xc = xc + cw_ref[3 - s:4 - s, :] * _shift_down(win, s, RC)
    z = _dot(xc.astype(BF16), wbd_ref[...])
    gi = _sigmoid(z[:, :CB] + bx_ref[...])
    gr = _sigmoid(z[:, CB:] + ba_ref[...])
    log_a = -LRU_C * gr * sp
    return win, xc, gi, gr, log_a


def _lru_fwd(proj, conv_w, conv_b, wbd, bx, ba, lam, nb):
    t = nb * S

    def body(xa_ref, ga_ref, cw_ref, cb_ref, wbd_ref, bx_ref, ba_ref, lam_ref,
             ya_ref, hs_ref, xc_ref, gi_ref, gr_ref, a_s, u_s):
        sp = _softplus(-lam_ref[...])

        def gates(c, carry):
            _, xc, gi, gr, log_a = _lru_gates(xa_ref, c, cw_ref, cb_ref, wbd_ref, bx_ref, ba_ref, sp)
            rows = _rows(c, RC)
            a_s[rows, :] = jnp.exp(log_a)
            u_s[rows, :] = jnp.sqrt(-_expm1_nonpos(2.0 * log_a)) * (gi * xc)
            xc_ref[rows, :] = xc
            gi_ref[rows, :] = gi
            gr_ref[rows, :] = gr
            return carry

        lax.fori_loop(0, S // RC, gates, 0)

        def scan(g, h):
            for k in range(SCAN_GROUP):
                rows = pl.ds(pl.multiple_of(g * (8 * SCAN_GROUP), 8 * SCAN_GROUP) + 8 * k, 8)
                a_cum, u_cum = _tile_scan(a_s[rows, :], u_s[rows, :])
                hs_ref[rows, :] = u_cum + a_cum * h
                h = u_cum[7:8, :] + a_cum[7:8, :] * h
            return h

        lax.fori_loop(0, S // (8 * SCAN_GROUP), scan, jnp.zeros((1, CB), F32))

        def gate_out(c, carry):
            ga = ga_ref[_rows(c, RC), :].astype(F32)
            ya_ref[_rows(c, RC), :] = (ga * _sigmoid(ga) * hs_ref[_rows(c, RC), :]).astype(BF16)
            return carry

        lax.fori_loop(0, S // RC, gate_out, 0)

    vec = pl.BlockSpec((1, CB), lambda b, cb: (0, cb))
    blk = pl.BlockSpec((S, CB), lambda b, cb: (b, cb))
    return pl.pallas_call(
        body, name="lru_fwd", grid=(nb, NCB),
        in_specs=[pl.BlockSpec((None, S, CB), lambda b, cb: (0, b, cb)),
                  pl.BlockSpec((None, S, CB), lambda b, cb: (1, b, cb)),
                  pl.BlockSpec((4, CB), lambda b, cb: (0, cb)),
                  vec,
                  pl.BlockSpec((None, CB, 2 * CB), lambda b, cb: (cb, 0, 0)),
                  vec, vec, vec],
        out_specs=[blk] + [pl.BlockSpec((None, None, S, CB), lambda b, cb: (b, cb, 0, 0))] * 4,
        out_shape=[jax.ShapeDtypeStruct((t, D), BF16)] + [jax.ShapeDtypeStruct((nb, NCB, S, CB), F32)] * 4,
        scratch_shapes=[pltpu.VMEM((S, CB), F32), pltpu.VMEM((S, CB), F32)],
        compiler_params=_params(("arbitrary", "arbitrary")),
    )(proj, proj, conv_w, conv_b, wbd, bx, ba, lam)


def _lru_bwd(proj, hs, xc_f, gi_f, gr_f, dya, conv_w, wbd, lam, nb, give):
    t = nb * S
    ng = len(give)

    def body(xa_ref, ga_ref, hs_ref, xc_s, gi_s, gr_s, dya_ref, cw_ref, wbd_ref, lam_ref, *rest):
        give_refs, rest = rest[:ng], rest[ng:]
        dp_ref, dwbd_ref, vec_ref = rest[:3]
        got_refs, rest = rest[3:3 + ng], rest[3 + ng:]
        a_s, dl_s, dh_s, dxc_s, acc_s, send_sems, recv_sems = rest
        b = pl.program_id(1)
        exchange = _sibling_copies(give_refs, got_refs, send_sems, recv_sems)

        @pl.when(jnp.logical_and(pl.program_id(0) == 0, b == 0))
        def _():
            for cp in exchange:
                cp.start()

        lam_v = lam_ref[...]
        sp = _softplus(-lam_v)
        acc_s[...] = jnp.zeros_like(acc_s)

        @pl.when(b == 0)
        def _():
            dwbd_ref[...] = jnp.zeros_like(dwbd_ref)
            vec_ref[...] = jnp.zeros_like(vec_ref)

        def gates(c, carry):
            rows = _rows(c, RC)
            a_s[rows, :] = jnp.exp(-LRU_C * gr_s[rows, :] * sp)
            ga = ga_ref[rows, :].astype(F32)
            sg = _sigmoid(ga)
            dya_c = dya_ref[rows, :]
            dl_s[rows, :] = dya_c * (ga * sg)
            dp_ref[1, rows, :] = (dya_c * hs_ref[rows, :] * (sg * (1.0 + ga * (1.0 - sg)))).astype(BF16)
            return carry

        lax.fori_loop(0, S // RC, gates, 0)

        def scan(i, g_in):
            base = pl.multiple_of((S // (8 * SCAN_GROUP) - 1 - i) * (8 * SCAN_GROUP), 8 * SCAN_GROUP)
            row = lax.broadcasted_iota(jnp.int32, (8, CB), 0)
            for k in reversed(range(SCAN_GROUP)):
                rows = pl.ds(base + 8 * k, 8)
                a = a_s[rows, :]
                dl = dl_s[rows, :]
                a_cum, g_loc = _tile_scan_rev(a, a * dl)
                g = g_loc + a_cum * g_in
                dh_s[rows, :] = dl + jnp.where(row < 7, pltpu.roll(g, 7, 0), g_in)
                g_in = g_loc[0:1, :] + a_cum[0:1, :] * g_in
            return g_in

        lax.fori_loop(0, S // (8 * SCAN_GROUP), scan, jnp.zeros((1, CB), F32))

        dxc_s[pl.ds(S, 8), :] = jnp.zeros((8, CB), F32)

        def grads(c, carry):
            rows = _rows(c, RC)
            dh = dh_s[rows, :]
            h_prev = _shift_down(_window_before(hs_ref, c, RC), 1, RC)
            xc, gi, gr, a = xc_s[rows, :], gi_s[rows, :], gr_s[rows, :], a_s[rows, :]
            mult = jnp.sqrt(-_expm1_nonpos(-2.0 * LRU_C * gr * sp))
            dmult = dh * gi * xc
            d_log_a = dh * h_prev * a - dmult * (a * a) / mult
            dzi = dh * mult * xc * gi * (1.0 - gi)
            dzr = d_log_a * (-LRU_C * sp) * gr * (1.0 - gr)
            dz = jnp.concatenate([dzi, dzr], axis=1).astype(BF16)
            dxc_s[rows, :] = dh * mult * gi + _dot_nt(dz, wbd_ref[...])
            dwbd_ref[...] += _dot_tn(xc.astype(BF16), dz)
            acc_s[1:2, :] += jnp.sum(dzi, axis=0, keepdims=True)
            acc_s[2:3, :] += jnp.sum(dzr, axis=0, keepdims=True)
            acc_s[3:4, :] += jnp.sum(d_log_a * (-LRU_C * gr), axis=0, keepdims=True)
            return carry

        lax.fori_loop(0, S // RC, grads, 0)

        def conv_bwd(c, carry):
            rows = _rows(c, RC)
            dwin = dxc_s[pl.ds(pl.multiple_of(c * RC, RC), RC + 8), :]
            dxc = dwin[:RC, :]
            xwin = _window_before(xa_ref, c, RC)
            dxa = cw_ref[3:4, :] * dxc
            acc_s[0:1, :] += jnp.sum(dxc, axis=0, keepdims=True)
            acc_s[7:8, :] += jnp.sum(dxc * _shift_down(xwin, 0, RC), axis=0, keepdims=True)
            for s in (1, 2, 3):
                dxa = dxa + cw_ref[3 - s:4 - s, :] * _shift_up(dwin, s, RC)
                acc_s[7 - s:8 - s, :] += jnp.sum(dxc * _shift_down(xwin, s, RC), axis=0, keepdims=True)
            dp_ref[0, rows, :] = dxa.astype(BF16)
            return carry

        lax.fori_loop(0, S // RC, conv_bwd, 0)

        row = lax.broadcasted_iota(jnp.int32, acc_s.shape, 0)
        vec_ref[...] += jnp.where(row == 3, acc_s[...] * (-_sigmoid(-lam_v)), acc_s[...])

        @pl.when(jnp.logical_and(pl.program_id(0) == NCB - 1, b == nb - 1))
        def _():
            for cp in exchange:
                cp.wait()

    vec = pl.BlockSpec((1, CB), lambda cb, b: (0, cb))
    blk = pl.BlockSpec((S, CB), lambda cb, b: (b, cb))
    own = pl.BlockSpec((None, None, S, CB), lambda cb, b: (b, cb, 0, 0))
    return pl.pallas_call(
        body, name="lru_bwd", grid=(NCB, nb),
        in_specs=[pl.BlockSpec((None, S, CB), lambda cb, b: (0, b, cb)),
                  pl.BlockSpec((None, S, CB), lambda cb, b: (1, b, cb)),
                  own, own, own, own, blk,
                  pl.BlockSpec((4, CB), lambda cb, b: (0, cb)),
                  pl.BlockSpec((None, CB, 2 * CB), lambda cb, b: (cb, 0, 0)),
                  vec] + [ANY] * ng,
        out_specs=[pl.BlockSpec((2, S, CB), lambda cb, b: (0, b, cb)),
                   pl.BlockSpec((None, CB, 2 * CB), lambda cb, b: (cb, 0, 0)),
                   pl.BlockSpec((8, CB), lambda cb, b: (0, cb))] + [ANY] * ng,
        out_shape=[jax.ShapeDtypeStruct((2, t, D), BF16),
                   jax.ShapeDtypeStruct((NCB, CB, 2 * CB), F32),
                   jax.ShapeDtypeStruct((8, D), F32)]
        + [jax.ShapeDtypeStruct((4,) + g.shape[1:], g.dtype) for g in give],
        scratch_shapes=[pltpu.VMEM((S, CB), F32), pltpu.VMEM((S, CB), F32), pltpu.VMEM((S, CB), F32),
                        pltpu.VMEM((S + 8, CB), F32), pltpu.VMEM((8, CB), F32),
                        pltpu.SemaphoreType.DMA((ng, 4)), pltpu.SemaphoreType.DMA((ng, 4))],
        compiler_params=_params(("arbitrary", "arbitrary")),
    )(proj, proj, hs, xc_f, gi_f, gr_f, dya, conv_w, wbd, lam, *give)


def _retention_tables():
    f32 = np.float32
    log_g = np.log1p(-(f32(2.0) ** (f32(-5.0) - np.arange(HEADS, dtype=f32)))).astype(f32)
    idx = np.arange(CH, dtype=f32)
    diff = idx[:, None] - idx[None, :]
    inner = np.where(diff >= 0, np.exp(np.maximum(diff, f32(0.0))[None] * log_g[:, None, None]), f32(0.0)).astype(f32)
    cross = np.exp((idx[None, :] + f32(1.0)) * log_g[:, None]).astype(f32)
    state = np.exp((f32(CH - 1.0) - idx[None, :]) * log_g[:, None]).astype(f32)
    cross = np.ascontiguousarray(np.broadcast_to(cross[:, :, None], (HEADS, CH, DK)))
    state = np.ascontiguousarray(np.broadcast_to(state[:, :, None], (HEADS, CH, DK)))
    half = DK // 2
    freqs = (f32(10000.0) ** (-np.arange(half, dtype=f32) / f32(half))).astype(f32)
    ang = (np.arange(S, dtype=f32)[:, None] * freqs[None, :]).astype(f32)
    return tuple(jnp.asarray(a) for a in (inner, cross, state, np.cos(ang).astype(f32), np.sin(ang).astype(f32)))


def _rotate(x, cos, sin):
    half = DK // 2
    x1, x2 = x[:, :half], x[:, half:]
    return jnp.concatenate([x1 * cos - x2 * sin, x1 * sin + x2 * cos], axis=1)


def _rotate_back(d, cos, sin):
    half = DK // 2
    d1, d2 = d[:, :half], d[:, half:]
    return jnp.concatenate([d1 * cos + d2 * sin, d2 * cos - d1 * sin], axis=1)


def _ret_fwd(proj, gain, tables, nb, wp_own):
    t = nb * S
    inner_t, cross_t, state_t, cos_t, sin_t = tables

    def body(q_ref, k_ref, v_ref, gb_ref, gain_ref, dm_ref, cd_ref, sd_ref, cos_ref, sin_ref, wp_ref,
             yb_ref, qr_ref, kr_ref, o_ref, rs_ref, wpg_ref, r_s, send_sems, recv_sems, own_sems):
        b, hd = pl.program_id(0), pl.program_id(1)
        own, first, arrive, forward, others = _gather_copies([wp_ref], [wpg_ref], send_sems, recv_sems, own_sems)

        @pl.when(jnp.logical_and(b == 0, hd == 0))
        def _():
            for cp in own + first:
                cp.start()

        @pl.when(jnp.logical_and(b == nb - 1, hd == HEADS - 1))
        def _():
            for came, on in zip(arrive, forward):
                came.wait_recv()
                on.start()

        r_s[...] = jnp.zeros_like(r_s)
        chunk_decay = cd_ref[CH - 1:CH, :]

        def chunk(c, carry):
            rows = _rows(c, CH)
            cos, sin = cos_ref[rows, :], sin_ref[rows, :]
            qr = _rotate(q_ref[rows, :].astype(F32), cos, sin).astype(BF16)
            kr = (_rotate(k_ref[rows, :].astype(F32), cos, sin) * (DK ** -0.5)).astype(BF16)
            vb = v_ref[rows, :]
            v = vb.astype(F32)
            qr_ref[rows, :] = qr
            kr_ref[rows, :] = kr
            r = r_s[...]
            rb = r.astype(BF16)
            rs_ref[c] = rb
            p = (_dot_nt(qr, kr) * dm_ref[...]).astype(BF16)
            o = _dot(p, vb) + _dot(qr, rb) * cd_ref[...]
            r_s[...] = chunk_decay * r + _dot_tn(kr, (v * sd_ref[...]).astype(BF16))
            o_ref[rows, :] = o
            oc = o - jnp.mean(o, axis=-1, keepdims=True)
            rstd = lax.rsqrt(jnp.mean(oc * oc, axis=-1, keepdims=True) + EPS)
            gb = gb_ref[rows, :].astype(F32)
            yb_ref[rows, :] = (gb * _sigmoid(gb) * (oc * rstd * gain_ref[...])).astype(BF16)
            return carry

        lax.fori_loop(0, NCH, chunk, 0)

        @pl.when(jnp.logical_and(b == nb - 1, hd == HEADS - 1))
        def _():
            for cp in others:
                cp.wait_recv()
            for cp in first + forward:
                cp.wait_send()
            for cp in own:
                cp.wait()

    seg = lambda s: pl.BlockSpec((None, S, DK), lambda b, h: (s, b, h))
    tab = pl.BlockSpec((None, CH, DK), lambda b, h: (h, 0, 0))
    rot = pl.BlockSpec((S, DK // 2), lambda b, h: (0, 0))
    blk = pl.BlockSpec((S, DK), lambda b, h: (b, h))
    return pl.pallas_call(
        body, name="ret_fwd", grid=(nb, HEADS),
        in_specs=[seg(2), seg(3), seg(4), seg(5),
                  pl.BlockSpec((None, 1, DK), lambda b, h: (h, 0, 0)),
                  tab, tab, tab, rot, rot, ANY],
        out_specs=[blk, blk, blk, blk,
                   pl.BlockSpec((None, None, NCH, DK, DK), lambda b, h: (b, h, 0, 0, 0)), ANY],
        out_shape=[jax.ShapeDtypeStruct((t, D), BF16), jax.ShapeDtypeStruct((t, D), BF16),
                   jax.ShapeDtypeStruct((t, D), BF16), jax.ShapeDtypeStruct((t, D), F32),
                   jax.ShapeDtypeStruct((nb, HEADS, NCH, DK, DK), BF16),
                   jax.ShapeDtypeStruct((NDEV,) + wp_own.shape, wp_own.dtype)],
        scratch_shapes=[pltpu.VMEM((DK, DK), F32),
                        pltpu.SemaphoreType.DMA((1, 7)), pltpu.SemaphoreType.DMA((1, 7)), pltpu.SemaphoreType.DMA((1,))],
        compiler_params=_params(("arbitrary", "arbitrary")),
    )(proj, proj, proj, proj, gain, inner_t, cross_t, state_t, cos_t, sin_t, wp_own)


def _ret_bwd(proj, qr, kr, o, rs, dyb, gain, tables, nb, sums):
    t = nb * S
    ns = len(sums)
    inner_t, cross_t, state_t, cos_t, sin_t = tables

    def body(qr_ref, kr_ref, v_ref, gb_ref, o_ref, dyb_ref, rs_ref, gain_ref, dm_ref, cd_ref, sd_ref,
             cos_ref, sin_ref, *rest):
        sum_refs, rest = rest[:ns], rest[ns:]
        dp_ref, dgain_ref = rest[:2]
        part_refs, rest = rest[2:2 + ns], rest[2 + ns:]
        dr_s, send_sems, recv_sems, local_sems = rest
        mine, sends, recvs = _chip_copies(sum_refs, part_refs, send_sems, recv_sems, local_sems)

        @pl.when(jnp.logical_and(pl.program_id(0) == 0, pl.program_id(1) == 0))
        def _():
            for cp in mine + sends:
                cp.start()

        dr_s[...] = jnp.zeros_like(dr_s)
        chunk_decay = cd_ref[CH - 1:CH, :]

        @pl.when(pl.program_id(1) == 0)
        def _():
            dgain_ref[...] = jnp.zeros_like(dgain_ref)

        def chunk(i, carry):
            c = NCH - 1 - i
            rows = _rows(c, CH)
            gain_v = gain_ref[...]
            o_c = o_ref[rows, :]
            oc = o_c - jnp.mean(o_c, axis=-1, keepdims=True)
            rstd = lax.rsqrt(jnp.mean(oc * oc, axis=-1, keepdims=True) + EPS)
            yn = oc * rstd
            gb = gb_ref[rows, :].astype(F32)
            sg = _sigmoid(gb)
            dyb_c = dyb_ref[rows, :]
            dgn = dyb_c * (gb * sg)
            dp_ref[3, rows, :] = (dyb_c * (yn * gain_v) * (sg * (1.0 + gb * (1.0 - sg)))).astype(BF16)
            dgain_ref[...] += jnp.sum(dgn * yn, axis=0, keepdims=True)
            dyn = dgn * gain_v
            do = rstd * (dyn - jnp.mean(dyn, axis=-1, keepdims=True)
                         - yn * jnp.mean(dyn * yn, axis=-1, keepdims=True))
            dob = do.astype(BF16)
            dox = (do * cd_ref[...]).astype(BF16)

            q_c, k_c = qr_ref[rows, :], kr_ref[rows, :]
            vb = v_ref[rows, :]
            v = vb.astype(F32)
            vs = (v * sd_ref[...]).astype(BF16)
            rb = rs_ref[c]
            d_r = dr_s[...]
            drb = d_r.astype(BF16)
            dm = dm_ref[...]
            p = (_dot_nt(q_c, k_c) * dm).astype(BF16)
            dpm = (_dot_nt(dob, vb) * dm).astype(BF16)
            dq = _dot(dpm, k_c) + _dot_nt(dox, rb)
            dk = _dot_tn(dpm, q_c) + _dot_nt(vs, drb)
            dv = _dot_tn(p, dob) + _dot(k_c, drb) * sd_ref[...]
            dr_s[...] = chunk_decay * d_r + _dot_tn(q_c, dox)

            cos, sin = cos_ref[rows, :], sin_ref[rows, :]
            dp_ref[0, rows, :] = _rotate_back(dq, cos, sin).astype(BF16)
            dp_ref[1, rows, :] = (_rotate_back(dk, cos, sin) * (DK ** -0.5)).astype(BF16)
            dp_ref[2, rows, :] = dv.astype(BF16)
            return carry

        lax.fori_loop(0, NCH, chunk, 0)

        @pl.when(jnp.logical_and(pl.program_id(0) == HEADS - 1, pl.program_id(1) == nb - 1))
        def _():
            for cp in recvs:
                cp.wait_recv()
            for cp in sends:
                cp.wait_send()
            for cp in mine:
                cp.wait()

    seg = lambda s: pl.BlockSpec((None, S, DK), lambda h, b: (s, b, h))
    tab = pl.BlockSpec((None, CH, DK), lambda h, b: (h, 0, 0))
    rot = pl.BlockSpec((S, DK // 2), lambda h, b: (0, 0))
    blk = pl.BlockSpec((S, DK), lambda h, b: (b, h))
    one = pl.BlockSpec((None, 1, DK), lambda h, b: (h, 0, 0))
    return pl.pallas_call(
        body, name="ret_bwd", grid=(HEADS, nb),
        in_specs=[blk, blk, seg(4), seg(5), blk, blk,
                  pl.BlockSpec((None, None, NCH, DK, DK), lambda h, b: (b, h, 0, 0, 0)),
                  one, tab, tab, tab, rot, rot] + [ANY] * ns,
        out_specs=[pl.BlockSpec((4, S, DK), lambda h, b: (0, b, h)), one] + [ANY] * ns,
        out_shape=[jax.ShapeDtypeStruct((4, t, D), BF16), jax.ShapeDtypeStruct((HEADS, 1, DK), F32)]
        + [jax.ShapeDtypeStruct(a.shape, a.dtype) for a in sums],
        scratch_shapes=[pltpu.VMEM((DK, DK), F32), pltpu.SemaphoreType.DMA((ns, 3)), pltpu.SemaphoreType.DMA((ns, 3)),
                        pltpu.SemaphoreType.DMA((ns,))],
        compiler_params=_params(("arbitrary", "arbitrary")),
    )(qr, kr, proj, proj, o, dyb, rs, gain, inner_t, cross_t, state_t, cos_t, sin_t, *sums)


def _wblock(k):
    return pl.BlockSpec((NDEV, D // NDEV, D), lambda i: (0, k, 0))


def _tail(ya, yb, proj, x2d, tgt, wg, g_fin):
    t = x2d.shape[0]
    tm = 256

    def body(ya_ref, yb_ref, ma_ref, mb_ref, x_ref, t_ref, wa_ref, wb_ref, wo_ref, g_ref,
             dx2_ref, dya_ref, dyb_ref, dm_ref, mg_ref, doa_ref, dob_ref, gfin_ref, loss_ref):
        i = pl.program_id(0)

        @pl.when(i == 0)
        def _():
            gfin_ref[...] = jnp.zeros_like(gfin_ref)
            loss_ref[...] = jnp.zeros_like(loss_ref)

        wa = wa_ref[...].reshape(D, D)
        wb = wb_ref[...].reshape(D, D)
        wo = wo_ref[...].reshape(D, D)
        out_a = _dot(ya_ref[...], wa)
        out_b = _dot(yb_ref[...], wb)
        sa = _sigmoid(ma_ref[...].astype(F32))
        sb = _sigmoid(mb_ref[...].astype(F32))
        merged = (sa * out_a + sb * out_b).astype(BF16)
        mg_ref[...] = merged
        x2 = x_ref[...] + _dot(merged, wo)
        r2 = lax.rsqrt(jnp.mean(x2 * x2, axis=-1, keepdims=True) + EPS)
        xh = x2 * r2
        g = g_ref[...]
        err = xh * g - t_ref[...]
        loss_ref[...] += jnp.sum(err * err, axis=0, keepdims=True) * (0.5 / D)
        dy = err * (1.0 / D)
        gfin_ref[...] += jnp.sum(dy * xh, axis=0, keepdims=True)
        dxh = dy * g
        dx2 = r2 * (dxh - xh * jnp.mean(dxh * xh, axis=-1, keepdims=True))
        dx2_ref[...] = dx2
        dmerged = _dot_nt(dx2.astype(BF16), wo)
        doa = (sa * dmerged).astype(BF16)
        dob = (sb * dmerged).astype(BF16)
        doa_ref[...] = doa
        dob_ref[...] = dob
        dm_ref[0] = (dmerged * out_a * sa * (1.0 - sa)).astype(BF16)
        dm_ref[1] = (dmerged * out_b * sb * (1.0 - sb)).astype(BF16)
        dya_ref[...] = _dot_nt(doa, wa)
        dyb_ref[...] = _dot_nt(dob, wb)

    row = lambda: pl.BlockSpec((tm, D), lambda i: (i, 0))
    seg = lambda s: pl.BlockSpec((None, tm, D), lambda i: (s, i, 0))
    vec = pl.BlockSpec((1, D), lambda i: (0, 0))
    return pl.pallas_call(
        body, name="tail", grid=(t // tm,),
        in_specs=[row(), row(), seg(6), seg(7), row(), row(), _wblock(0), _wblock(1), _wblock(2), vec],
        out_specs=[row(), row(), row(), pl.BlockSpec((2, tm, D), lambda i: (0, i, 0)),
                   row(), row(), row(), vec, vec],
        out_shape=[jax.ShapeDtypeStruct((t, D), F32), jax.ShapeDtypeStruct((t, D), F32),
                   jax.ShapeDtypeStruct((t, D), F32), jax.ShapeDtypeStruct((2, t, D), BF16),
                   jax.ShapeDtypeStruct((t, D), BF16), jax.ShapeDtypeStruct((t, D), BF16),
                   jax.ShapeDtypeStruct((t, D), BF16), jax.ShapeDtypeStruct((1, D), F32),
                   jax.ShapeDtypeStruct((1, D), F32)],
        compiler_params=_params(("arbitrary",)),
    )(ya, yb, proj, proj, x2d, tgt, wg, wg, wg, g_fin)


def _tail_wgrad(ya, yb, merged, doa, dob, dx2):
    t = ya.shape[0]
    tm = 512

    def body(ya_ref, yb_ref, mg_ref, doa_ref, dob_ref, dx2_ref, ga_ref, gb_ref, go_ref):
        @pl.when(pl.program_id(0) == 0)
        def _():
            ga_ref[...] = jnp.zeros_like(ga_ref)
            gb_ref[...] = jnp.zeros_like(gb_ref)
            go_ref[...] = jnp.zeros_like(go_ref)

        ga_ref[...] += _dot_tn(ya_ref[...], doa_ref[...])
        gb_ref[...] += _dot_tn(yb_ref[...], dob_ref[...])
        go_ref[...] += _dot_tn(mg_ref[...], dx2_ref[...].astype(BF16))

    row = lambda: pl.BlockSpec((tm, D), lambda i: (i, 0))
    full = lambda: pl.BlockSpec((D, D), lambda i: (0, 0))
    return pl.pallas_call(
        body, name="tail_wgrad", grid=(t // tm,),
        in_specs=[row() for _ in range(6)], out_specs=[full(), full(), full()],
        out_shape=[jax.ShapeDtypeStruct((D, D), F32)] * 3,
        compiler_params=_params(("arbitrary",)),
    )(ya, yb, merged, doa, dob, dx2)


def _dproj_specs(tm, j_of, i_of):
    last = lambda j, i, lo, n: (jnp.clip(j - lo, 0, n - 1), i, 0)
    return [pl.BlockSpec((None, tm, D), lambda a, b: last(j_of(a, b), i_of(a, b), 0, 2)),
            pl.BlockSpec((None, tm, D), lambda a, b: last(j_of(a, b), i_of(a, b), 2, 4)),
            pl.BlockSpec((None, tm, D), lambda a, b: last(j_of(a, b), i_of(a, b), 6, 2))]


def _dproj_specs_ordered(tm):
    def spec(lo, n):
        def index(k, i, order_ref):
            seg = order_ref[k]
            mine = jnp.logical_and(seg >= lo, seg < lo + n)
            return jnp.where(mine, seg - lo, 0), jnp.where(mine, i, 0), 0
        return pl.BlockSpec((None, tm, D), index)
    return [spec(0, 2), spec(2, 4), spec(6, 2)]


def _dproj_pick(j, da_ref, db_ref, dc_ref, use):
    @pl.when(j < 2)
    def _():
        use(da_ref[...])

    @pl.when(jnp.logical_and(j >= 2, j < 6))
    def _():
        use(db_ref[...])

    @pl.when(j >= 6)
    def _():
        use(dc_ref[...])


def _rs_schedule(q, c):
    steps = []
    for s in range(3):
        d_a = lax.rem(q + 1 + s, 4)
        d_b = lax.rem(q + 1 + (s + 1) % 3, 4)
        steps.append((jnp.where(c == 0, d_a, d_b), jnp.where(c == 0, d_b, d_a)))
    steps.append((q, q))
    return steps


def _rs_order(q, c):
    order = []
    for keep, give in _rs_schedule(q, c):
        order += [2 * give + 1 - c, 2 * keep + c]
    return jnp.stack(order).astype(jnp.int32)


def _inproj_wgrad_rs(h, dpa, dpb, dpc, order, smalls):
    t = h.shape[0]
    tm = 1024
    nt = t // tm
    nsm = len(smalls)

    def body(order_ref, h_ref, da_ref, db_ref, dc_ref, *rest):
        small_refs, parts_ref, rest = rest[:nsm], rest[nsm], rest[nsm + 1:]
        all_refs, rest = rest[:nsm], rest[nsm:]
        (acc, sib, outb, give_send, give_recv, sum_send, sum_recv, own_sem,
         small_send, small_recv, small_own) = rest
        k, i = pl.program_id(0), pl.program_id(1)
        x, y, c = _place()
        schedule = _rs_schedule(2 * x + y, c)
        own, first, arrive, forward, others = _gather_copies(small_refs, all_refs, small_send, small_recv, small_own)

        @pl.when(jnp.logical_and(k == 0, i == 0))
        def _():
            for cp in own + first:
                cp.start()

        @pl.when(jnp.logical_and(k == 2, i == 0))
        def _():
            for came, on in zip(arrive, forward):
                came.wait_recv()
                on.start()

        def use(d):
            @pl.when(i == 0)
            def _():
                acc[k % 2] = _dot_tn(h_ref[...], d)

            @pl.when(i > 0)
            def _():
                acc[k % 2] += _dot_tn(h_ref[...], d)

        _dproj_pick(order_ref[k], da_ref, db_ref, dc_ref, use)

        def give_copy(s):
            return pltpu.make_async_remote_copy(
                src_ref=acc.at[0], dst_ref=sib.at[s % 2], send_sem=give_send.at[s], recv_sem=give_recv.at[s],
                device_id=(x, y, 1 - c), device_id_type=MESH)

        def sum_copy(s):
            keep = schedule[s][0]
            return pltpu.make_async_remote_copy(
                src_ref=outb.at[s], dst_ref=parts_ref.at[s], send_sem=sum_send.at[s], recv_sem=sum_recv.at[s],
                device_id=(keep // 2, lax.rem(keep, 2), c), device_id_type=MESH)

        own_copy = pltpu.make_async_copy(outb.at[3], parts_ref.at[3], own_sem)

        for s in range(4):
            @pl.when(jnp.logical_and(k == 2 * s, i == nt - 1))
            def _():
                give_copy(s).start()

            @pl.when(jnp.logical_and(k == 2 * s + 1, i == nt - 1))
            def _():
                give_copy(s).wait_recv()
                outb[s] = (acc[1] + sib[s % 2]).astype(BF16)
                give_copy(s).wait_send()
                if s < 3:
                    sum_copy(s).start()
                else:
                    own_copy.start()

        @pl.when(jnp.logical_and(k == NSEG - 1, i == nt - 1))
        def _():
            for s in range(3):
                sum_copy(s).wait_recv()
            for s in range(3):
                sum_copy(s).wait_send()
            own_copy.wait()
            for cp in others:
                cp.wait_recv()
            for cp in first + forward:
                cp.wait_send()
            for cp in own:
                cp.wait()

    return pl.pallas_call(
        body, name="inproj_wgrad_rs",
        grid_spec=pltpu.PrefetchScalarGridSpec(
            num_scalar_prefetch=1, grid=(NSEG, nt),
            in_specs=[pl.BlockSpec((tm, D), lambda k, i, order_ref: (i, 0))] + _dproj_specs_ordered(tm) + [ANY] * nsm,
            out_specs=[ANY] * (1 + nsm),
            scratch_shapes=[pltpu.VMEM((2, D, D), F32), pltpu.VMEM((2, D, D), F32), pltpu.VMEM((4, D, D), BF16),
                            pltpu.SemaphoreType.DMA((4,)), pltpu.SemaphoreType.DMA((4,)),
                            pltpu.SemaphoreType.DMA((3,)), pltpu.SemaphoreType.DMA((3,)),
                            pltpu.SemaphoreType.DMA,
                            pltpu.SemaphoreType.DMA((nsm, 7)), pltpu.SemaphoreType.DMA((nsm, 7)),
                            pltpu.SemaphoreType.DMA((nsm,))]),
        out_shape=[jax.ShapeDtypeStruct((4, D, D), BF16)]
        + [jax.ShapeDtypeStruct((NDEV,) + a.shape, a.dtype) for a in smalls],
        compiler_params=_params(("arbitrary", "arbitrary")),
    )(order, h, dpa, dpb, dpc, *smalls)


def _inproj_dgrad(dpa, dpb, dpc, wg, x2d, dx2, g_in):
    t = x2d.shape[0]
    tm = 1024

    def body(da_ref, db_ref, dc_ref, w_ref, x_ref, dx2_ref, g_ref, gx_ref, gg_ref, acc_s):
        i, j = pl.program_id(0), pl.program_id(1)

        @pl.when(jnp.logical_and(i == 0, j == 0))
        def _():
            gg_ref[...] = jnp.zeros_like(gg_ref)

        @pl.when(j == 0)
        def _():
            acc_s[...] = jnp.zeros_like(acc_s)

        def use(d):
            acc_s[...] += _dot_nt(d, w_ref[...])

        _dproj_pick(j, da_ref, db_ref, dc_ref, use)

        @pl.when(j == NSEG - 1)
        def _():
            x = x_ref[...]
            r = lax.rsqrt(jnp.mean(x * x, axis=-1, keepdims=True) + EPS)
            xh = x * r
            dh = acc_s[...]
            gg_ref[...] += jnp.sum(dh * xh, axis=0, keepdims=True)
            dxh = dh * g_ref[...]
            gx_ref[...] = dx2_ref[...] + r * (dxh - xh * jnp.mean(dxh * xh, axis=-1, keepdims=True))

    row = lambda: pl.BlockSpec((tm, D), lambda i, j: (i, 0))
    vec = pl.BlockSpec((1, D), lambda i, j: (0, 0))
    return pl.pallas_call(
        body, name="inproj_dgrad", grid=(t // tm, NSEG),
        in_specs=_dproj_specs(tm, lambda i, j: j, lambda i, j: i)
        + [pl.BlockSpec((None, D, D), lambda i, j: (j, 0, 0)), row(), row(), vec],
        out_specs=[row(), vec],
        out_shape=[jax.ShapeDtypeStruct((t, D), F32), jax.ShapeDtypeStruct((1, D), F32)],
        scratch_shapes=[pltpu.VMEM((tm, D), F32)],
        compiler_params=_params(("arbitrary", "arbitrary")),
    )(dpa, dpb, dpc, wg, x2d, dx2, g_in)


def _adam_update(g, w, m, v):
    m_new = ADAM_B1 * m + (1.0 - ADAM_B1) * g
    v_new = ADAM_B2 * v + (1.0 - ADAM_B2) * (g * g)
    m_hat = m_new / (1.0 - ADAM_B1 ** ADAM_STEP)
    v_hat = v_new / (1.0 - ADAM_B2 ** ADAM_STEP)
    return -ADAM_LR * (m_hat / (jnp.sqrt(v_hat) + ADAM_EPS) + ADAM_WD * w), m_new, v_new


def _sum_in_order(ref):
    total = ref[0].astype(F32)
    for k in range(1, ref.shape[0]):
        total = total + ref[k].astype(F32)
    return total


def _adamw_small(me, vec_all, gx_all, ga_all, groups):
    flat = [a for grp in groups for a in grp]
    ng = len(groups)
    nshard = D // NDEV

    def body(me_ref, vec_ref, shard_ref, gx_ref, ga_ref, *refs):
        ins, outs = refs[:3 * ng], refs[3 * ng:]
        vec = _sum_in_order(vec_ref)
        shard = _sum_in_order(shard_ref)
        grads = [vec[r:r + 1, :] for r in range(6)]
        grads += [shard[0:4, :], shard[4:8, 0:DK // NDEV], _sum_in_order(gx_ref), _sum_in_order(ga_ref)]
        for n, g in enumerate(grads):
            delta, m_new, v_new = _adam_update(g, ins[3 * n][...], ins[3 * n + 1][...], ins[3 * n + 2][...])
            outs[4 * n][...] = g
            outs[4 * n + 1][...] = delta
            outs[4 * n + 2][...] = m_new
            outs[4 * n + 3][...] = v_new
        outs[4 * ng][...] = jnp.sum(vec[6:7, :], axis=1, keepdims=True)

    full = lambda a: pl.BlockSpec(a.shape, lambda i, me_ref, nd=len(a.shape): (0,) * nd)
    out_shape = [jax.ShapeDtypeStruct(w.shape, F32) for w, _, _ in groups for _ in range(4)]
    out_shape.append(jax.ShapeDtypeStruct((1, 1), F32))
    outs = pl.pallas_call(
        body, name="adamw_small",
        grid_spec=pltpu.PrefetchScalarGridSpec(
            num_scalar_prefetch=1, grid=(1,),
            in_specs=[full(vec_all),
                      pl.BlockSpec((NDEV, 8, nshard), lambda i, me_ref: (0, 1, me_ref[0])),
                      full(gx_all), full(ga_all)] + [full(a) for a in flat],
            out_specs=[full(s) for s in out_shape]),
        out_shape=out_shape,
        compiler_params=_params(("arbitrary",)),
    )(me, vec_all, vec_all, gx_all, ga_all, *flat)
    return [outs[4 * n:4 * n + 4] for n in range(ng)], outs[4 * ng]


def _adamw(name, items):
    n, rows, cols = items[0][0].shape
    tr = rows if rows <= 256 else 256
    k = len(items)

    def body(*refs):
        for a in range(k):
            p_ref, w_ref, m_ref, v_ref = refs[4 * a:4 * a + 4]
            g = _sum_in_order(p_ref)
            delta, m_new, v_new = _adam_update(g, w_ref[...], m_ref[...], v_ref[...])
            for o, val in zip(refs[4 * k + 4 * a:4 * k + 4 * a + 4], (g, delta, m_new, v_new)):
                o[...] = val

    blk = lambda: pl.BlockSpec((tr, cols), lambda i: (i, 0))
    outs = pl.pallas_call(
        body, name=name, grid=(rows // tr,),
        in_specs=[pl.BlockSpec((n, tr, cols), lambda i: (0, i, 0)), blk(), blk(), blk()] * k,
        out_specs=[blk() for _ in range(4 * k)],
        out_shape=[jax.ShapeDtypeStruct((rows, cols), F32)] * (4 * k),
        compiler_params=_params(("arbitrary",)),
    )(*[a for item in items for a in item])
    return [outs[4 * a:4 * a + 4] for a in range(k)]


ANY = pl.BlockSpec(memory_space=pl.ANY)


def _place():
    return lax.axis_index("x"), lax.axis_index("y"), lax.axis_index("c")


def _gather_copies(ins, outs, send_sems, recv_sems, own_sems):
    x, y, c = _place()
    me, sibling = (x, y, c), (x, y, 1 - c)
    chips = [(1 - x, y), (x, 1 - y), (1 - x, 1 - y)]
    n = len(ins)

    def copy(a, k, block, to, src=None):
        px, py, pc = block
        dst = outs[a].at[4 * px + 2 * py + pc]
        return pltpu.make_async_remote_copy(
            src_ref=dst if src is None else src, dst_ref=dst,
            send_sem=send_sems.at[a, k], recv_sem=recv_sems.at[a, k], device_id=to, device_id_type=MESH)

    own = [pltpu.make_async_copy(ins[a], outs[a].at[4 * x + 2 * y + c], own_sems.at[a]) for a in range(n)]
    first = []
    for a in range(n):
        first.append(copy(a, 0, me, sibling, src=ins[a]))
        first += [copy(a, 1 + j, me, (*chip, c), src=ins[a]) for j, chip in enumerate(chips)]
    arrive = [copy(a, 1 + j, (*chip, c), me) for j, chip in enumerate(chips) for a in range(n)]
    forward = [copy(a, 4 + j, (*chip, c), sibling) for j, chip in enumerate(chips) for a in range(n)]
    rest = [copy(a, 0, sibling, me) for a in range(n)]
    rest += [copy(a, 4 + j, (*chip, 1 - c), me) for a in range(n) for j, chip in enumerate(chips)]
    return own, first, arrive, forward, rest


def _sibling_copies(ins, outs, send_sems, recv_sems):
    x, y, c = _place()
    return [pltpu.make_async_remote_copy(
        src_ref=ins[a].at[2 * q + 1 - c], dst_ref=outs[a].at[q],
        send_sem=send_sems.at[a, q], recv_sem=recv_sems.at[a, q],
        device_id=(x, y, 1 - c), device_id_type=MESH) for a in range(len(ins)) for q in range(4)]


def _chip_copies(ins, outs, send_sems, recv_sems, local_sems):
    x, y, c = _place()
    my_chip = 2 * x + y
    chips = [(1 - x, y), (x, 1 - y), (1 - x, 1 - y)]
    n = len(ins)
    mine = [pltpu.make_async_copy(ins[a].at[my_chip], outs[a].at[my_chip], local_sems.at[a]) for a in range(n)]
    sends = [pltpu.make_async_remote_copy(
        src_ref=ins[a].at[2 * px + py], dst_ref=outs[a].at[my_chip],
        send_sem=send_sems.at[a, j], recv_sem=recv_sems.at[a, j],
        device_id=(px, py, c), device_id_type=MESH) for a in range(n) for j, (px, py) in enumerate(chips)]
    recvs = [pltpu.make_async_remote_copy(
        src_ref=ins[a].at[my_chip], dst_ref=outs[a].at[2 * px + py],
        send_sem=send_sems.at[a, j], recv_sem=recv_sems.at[a, j],
        device_id=(px, py, c), device_id_type=MESH) for a in range(n) for j, (px, py) in enumerate(chips)]
    return mine, sends, recvs


def _chip_sum(owns, gots, core):
    n = len(owns)
    _, rows, cols = owns[0].shape

    def body(core_ref, *refs):
        for a in range(n):
            refs[2 * n + a][...] = (refs[a][...] + refs[n + a][...]).astype(BF16)

    own_spec = pl.BlockSpec((None, rows, cols), lambda q, core_ref: (2 * q + core_ref[0], 0, 0))
    slab = pl.BlockSpec((None, rows, cols), lambda q, core_ref: (q, 0, 0))
    return pl.pallas_call(
        body, name="chip_sum",
        grid_spec=pltpu.PrefetchScalarGridSpec(
            num_scalar_prefetch=1, grid=(4,),
            in_specs=[own_spec] * n + [slab] * n, out_specs=[slab] * n),
        out_shape=[jax.ShapeDtypeStruct((4, rows, cols), BF16)] * n,
        compiler_params=_params(("arbitrary",)),
    )(core, *owns, *gots)


def _block_diag(w):
    w4 = w.reshape(NCB, 4, 64, 64)
    eye = jnp.eye(4, dtype=w.dtype)
    return (w4[:, :, :, None, :] * eye[None, :, None, :, None]).reshape(NCB, CB, CB)


def _block_diag_back(g):
    g5 = g.reshape(NCB, 4, 64, 4, 64)
    return jnp.stack([g5[:, m, :, m, :] for m in range(4)], axis=1).reshape(16, 64, 64)


def kernel(x, norm_in, w_in, conv_w, conv_b, gate_x_w, gate_x_b, gate_a_w, gate_a_b, lru_lambda, gn_gain, w_proj_a, w_proj_b, w_out, norm_final, loss_target, m_norm_in, m_w_in, m_conv_w, m_conv_b, m_gate_x_w, m_gate_x_b, m_gate_a_w, m_gate_a_b, m_lru_lambda, m_gn_gain, m_w_proj_a, m_w_proj_b, m_w_out, m_norm_final, v_norm_in, v_w_in, v_conv_w, v_conv_b, v_gate_x_w, v_gate_x_b, v_gate_a_w, v_gate_a_b, v_lru_lambda, v_gn_gain, v_w_proj_a, v_w_proj_b, v_w_out, v_norm_final):
    xi, yi, ci = _place()
    me = 4 * xi + 2 * yi + ci
    core = ci.astype(jnp.int32).reshape(1)
    nshard = D // NDEV
    nb = x.shape[0]
    t = nb * S
    x2d = x.reshape(t, D)
    tgt2d = loss_target.reshape(t, D)
    g_final = norm_final.reshape(1, D)
    wbd = jnp.concatenate([_block_diag(gate_x_w[0]), _block_diag(gate_a_w[0])], axis=-1).astype(BF16)
    tables = _retention_tables()

    wp_own = jnp.concatenate([w_proj_a[0], w_proj_b[0], w_out[0]], axis=0).astype(BF16)
    tiny = jnp.concatenate([conv_w[0], jnp.pad(gn_gain[0], ((0, 0), (0, nshard - DK // NDEV)))], axis=0)
    proj, h, wg, tiny_g = _inproj_gather(x2d, norm_in, w_in[0].astype(BF16), tiny, *_gather_order(xi, yi, ci))
    conv_w_full = tiny_g[:, 0:4, :].transpose(1, 0, 2).reshape(4, D)
    gain3 = tiny_g[:, 4:8, :DK // NDEV].transpose(1, 0, 2).reshape(HEADS, 1, DK)

    ya, hs, xc, gi, gr = _lru_fwd(proj, conv_w_full, conv_b, wbd, gate_x_b, gate_a_b, lru_lambda, nb)
    yb, qr, kr, o, rs, wpg = _ret_fwd(proj, gain3, tables, nb, wp_own)
    dx2, dya, dyb, dpc, merged, doa, dob, g_fin, loss_vec = _tail(ya, yb, proj, x2d, tgt2d, wpg, g_final)
    g_pa, g_pb, g_out = _tail_wgrad(ya, yb, merged, doa, dob, dx2)

    own = [g.reshape(NDEV, nshard, D) for g in (g_pa, g_pb, g_out)]
    dpa, g_wbd, g_vec, *got = _lru_bwd(proj, hs, xc, gi, gr, dya, conv_w_full, wbd, lru_lambda, nb, own)
    sums = _chip_sum(own, got, core)
    dpb, g_gain, *parts = _ret_bwd(proj, qr, kr, o, rs, dyb, gain3, tables, nb, sums)

    grad_x, g_norm_in = _inproj_dgrad(dpa, dpb, dpc, wg, x2d, dx2, norm_in)
    grad_x = grad_x.reshape(nb, S, D)

    gain_rows = jnp.pad(g_gain.reshape(HEADS, NDEV, DK // NDEV), ((0, 0), (0, 0), (0, nshard - DK // NDEV)))
    vec = jnp.concatenate([g_norm_in, g_vec[0:4], g_fin, loss_vec, jnp.zeros((1, D), F32), g_vec[4:8],
                           gain_rows.reshape(HEADS, D)], axis=0)
    g_gx = _block_diag_back(g_wbd[:, :, :CB]).reshape(D // 2, 128)
    g_ga = _block_diag_back(g_wbd[:, :, CB:]).reshape(D // 2, 128)
    parts_in, vec_all, gx_all, ga_all = _inproj_wgrad_rs(h, dpa, dpb, dpc, _rs_order(2 * xi + yi, ci),
                                                         [vec, g_gx, g_ga])
    gx_all = gx_all.reshape(NDEV, D, 64)
    ga_all = ga_all.reshape(NDEV, D, 64)
    parts = [parts_in] + list(parts)

    res = {}
    (out,) = _adamw("adamw_w_in", [(parts[0], w_in[0], m_w_in[0], v_w_in[0])])
    res["w_in"] = [o[None] for o in out]
    square = [("w_proj_a", w_proj_a, m_w_proj_a, v_w_proj_a), ("w_proj_b", w_proj_b, m_w_proj_b, v_w_proj_b),
              ("w_out", w_out, m_w_out, v_w_out)]
    outs = _adamw("adamw_square", [(parts[1 + k], w[0], m[0], v[0]) for k, (_, w, m, v) in enumerate(square)])
    for (nm, _, _, _), out in zip(square, outs):
        res[nm] = [o[None] for o in out]

    row = lambda a: a.reshape(1, D)
    gate = lambda a: a.reshape(D, 64)
    groups = [("norm_in", norm_in, m_norm_in, v_norm_in, row), ("conv_b", conv_b, m_conv_b, v_conv_b, row),
              ("gate_x_b", gate_x_b, m_gate_x_b, v_gate_x_b, row), ("gate_a_b", gate_a_b, m_gate_a_b, v_gate_a_b, row),
              ("lru_lambda", lru_lambda, m_lru_lambda, v_lru_lambda, row),
              ("norm_final", norm_final, m_norm_final, v_norm_final, row),
              ("conv_w", conv_w, m_conv_w, v_conv_w, lambda a: a[0]), ("gn_gain", gn_gain, m_gn_gain, v_gn_gain, lambda a: a[0]),
              ("gate_x_w", gate_x_w, m_gate_x_w, v_gate_x_w, gate), ("gate_a_w", gate_a_w, m_gate_a_w, v_gate_a_w, gate)]
    small_out, loss = _adamw_small(me.astype(jnp.int32).reshape(1), vec_all, gx_all, ga_all,
                                   [tuple(view(a) for a in (w, m, v)) for _, w, m, v, view in groups])
    for (nm, w, _, _, _), out in zip(groups, small_out):
        res[nm] = [o.reshape(w.shape) for o in out]
    loss = loss.reshape(())

    order = ["norm_in", "w_in", "conv_w", "conv_b", "gate_x_w", "gate_x_b", "gate_a_w", "gate_a_b", "lru_lambda",
             "gn_gain", "w_proj_a", "w_proj_b", "w_out", "norm_final"]
    outs = [loss, grad_x]
    for k in range(4):
        outs += [res[nm][k] for nm in order]
    return tuple(outs)
```

```python
import numpy as np

import jax
import jax.numpy as jnp
from jax import lax
from jax.experimental import pallas as pl
from jax.experimental.pallas import tpu as pltpu

F32 = jnp.float32
BF16 = jnp.bfloat16
MESH = pl.DeviceIdType.MESH

D = 1024
S = 2048
NSEG = 8
NDEV = 8
HEADS = 4
DK = 256
CH = 256
NCH = S // CH
CB = 256
NCB = D // CB
RC = 128
SCAN_GROUP = 8
EPS = 1e-6
LRU_C = 8.0
VMEM_LIMIT = 56 * 1024 * 1024

ADAM_LR = 0.001
ADAM_B1 = 0.9
ADAM_B2 = 0.999
ADAM_EPS = 1e-08
ADAM_WD = 0.01
ADAM_STEP = 10


def _params(sem=None):
    return pltpu.CompilerParams(dimension_semantics=sem, vmem_limit_bytes=VMEM_LIMIT)


def _dot(a, b):
    return jnp.dot(a, b, preferred_element_type=F32)


def _dot_nt(a, b):
    return lax.dot_general(a, b, (((1,), (1,)), ((), ())), preferred_element_type=F32)


def _dot_tn(a, b):
    return lax.dot_general(a, b, (((0,), (0,)), ((), ())), preferred_element_type=F32)


def _sigmoid(x):
    return jax.nn.sigmoid(x)


def _expm1_nonpos(x):
    poly = x * (1.0 + x * (0.5 + x * (1.0 / 6.0 + x * (1.0 / 24.0))))
    return jnp.where(x > -0.05, poly, jnp.exp(x) - 1.0)


def _softplus(x):
    return jnp.maximum(x, 0.0) + jnp.log(1.0 + jnp.exp(-jnp.abs(x)))


def _rows(c, n):
    return pl.ds(pl.multiple_of(c * n, n), n)


def _window_before(ref, c, n):
    r0 = c * n
    if ref.dtype == BF16:
        prev = ref[pl.ds(pl.multiple_of(jnp.maximum(r0 - 16, 0), 16), 16), :].astype(F32)[8:, :]
    else:
        prev = ref[pl.ds(pl.multiple_of(jnp.maximum(r0 - 8, 0), 8), 8), :]
    prev = jnp.where(c > 0, prev, 0.0)
    return jnp.concatenate([prev, ref[_rows(c, n), :].astype(F32)], axis=0)


def _shift_down(win, s, n):
    if s == 0:
        return win[8:, :]
    return pltpu.roll(win, s, 0)[8:, :]


def _shift_up(win, s, n):
    if s == 0:
        return win[:n, :]
    return pltpu.roll(win, n + 8 - s, 0)[:n, :]


HALF = D // 2
GATHER_SLOTS = [("own", None, 0), ("own", None, 1), ("sib", None, 0), ("sib", None, 1)]
for _j, _h in ((0, 0), (1, 0), (0, 1), (1, 1), (2, 0), (2, 1)):
    GATHER_SLOTS += [("ici", _j, _h), ("fwd", _j, _h)]
NSLOT = len(GATHER_SLOTS)


def _gather_order(x, y, c):
    chips = [(1 - x, y), (x, 1 - y), (1 - x, 1 - y)]
    segs, halves = [], []
    for kind, j, h in GATHER_SLOTS:
        if kind == "own":
            seg = 4 * x + 2 * y + c
        elif kind == "sib":
            seg = 4 * x + 2 * y + 1 - c
        else:
            px, py = chips[j]
            seg = 4 * px + 2 * py + (c if kind == "ici" else 1 - c)
        segs.append(seg)
        halves.append(h)
    return jnp.stack(segs).astype(jnp.int32), jnp.asarray(halves, jnp.int32)


def _inproj_gather(x2d, g_in, w_own, tiny_own, order, halves):
    t = x2d.shape[0]
    tm = 1024
    nt = t // tm

    def body(order_ref, half_ref, x_ref, g_ref, w_own_ref, tiny_own_ref,
             proj_ref, h_ref, wg_ref, tinyg_ref,
             w_all, h_all, send_sems, recv_sems, own_sems, out_sems, tiny_send, tiny_recv, tiny_own_sem):
        k, i = pl.program_id(0), pl.program_id(1)
        x, y, c = _place()
        me, sibling = (x, y, c), (x, y, 1 - c)
        mine = 4 * x + 2 * y + c
        chips = [(1 - x, y), (x, 1 - y), (1 - x, 1 - y)]

        def copy(h, n, block, to, own_src=False):
            px, py, pc = block
            dst = w_all.at[4 * px + 2 * py + pc, h]
            return pltpu.make_async_remote_copy(
                src_ref=w_own_ref.at[:, pl.ds(h * HALF, HALF)] if own_src else dst, dst_ref=dst,
                send_sem=send_sems.at[h, n], recv_sem=recv_sems.at[h, n], device_id=to, device_id_type=MESH)

        def tiny_copy(n, block, to, own_src=False):
            px, py, pc = block
            dst = tinyg_ref.at[4 * px + 2 * py + pc]
            return pltpu.make_async_remote_copy(
                src_ref=tiny_own_ref if own_src else dst, dst_ref=dst,
                send_sem=tiny_send.at[n], recv_sem=tiny_recv.at[n], device_id=to, device_id_type=MESH)

        def own_copy(h):
            return pltpu.make_async_copy(w_own_ref.at[:, pl.ds(h * HALF, HALF)], w_all.at[mine, h], own_sems.at[h])

        tiny_mine = pltpu.make_async_copy(tiny_own_ref, tinyg_ref.at[mine], tiny_own_sem)

        def keep_copy(n):
            h = GATHER_SLOTS[n][2]
            return pltpu.make_async_copy(w_all.at[order_ref[n], h], wg_ref.at[order_ref[n], :, pl.ds(h * HALF, HALF)],
                                         out_sems.at[n])

        near = [(0, sibling), (1, (*chips[0], c)), (2, (*chips[1], c))]
        first = [copy(h, n, me, to, True) for h in (0, 1) for n, to in near]
        tiny_first = [tiny_copy(0, me, sibling, True)] + [tiny_copy(1 + j, me, (*chip, c), True) for j, chip in enumerate(chips)]

        def relay(h, j):
            seg = w_all.at[4 * chips[j][0] + 2 * chips[j][1] + c, h]
            return pltpu.make_async_remote_copy(
                src_ref=seg, dst_ref=seg, send_sem=send_sems.at[h, 3], recv_sem=recv_sems.at[h, 3],
                device_id=(*chips[1 - j], c), device_id_type=MESH)

        for n, (kind, j, h) in enumerate(GATHER_SLOTS):
            @pl.when(jnp.logical_and(k == n, i == 0))
            def _():
                if n == 0:
                    own_copy(0).start()
                    own_copy(1).start()
                    tiny_mine.start()
                    for cp in first + tiny_first:
                        cp.start()
                if kind == "own":
                    own_copy(h).wait()
                elif kind == "sib":
                    copy(h, 0, sibling, me).wait_recv()
                elif kind == "ici":
                    copy(h, 1 + j, (*chips[j], c), me).wait_recv()
                    copy(h, 4 + j, (*chips[j], c), sibling).start()
                    if j < 2:
                        @pl.when(c == j)
                        def _():
                            relay(h, j).start()
                else:
                    copy(h, 4 + j, (*chips[j], 1 - c), me).wait_recv()
                keep_copy(n).start()

        rows = pl.ds(pl.multiple_of(i * tm, tm), tm)

        @pl.when(k == 0)
        def _():
            xv = x_ref[...]
            r = lax.rsqrt(jnp.mean(xv * xv, axis=-1, keepdims=True) + EPS)
            hv = (xv * r * g_ref[...]).astype(BF16)
            h_ref[...] = hv
            h_all[rows, :] = hv

        proj_ref[...] = _dot(h_all[rows, :], w_all[order_ref[k], half_ref[k]]).astype(BF16)

        @pl.when(jnp.logical_and(k == NSLOT - 1, i == nt - 1))
        def _():
            for j, chip in enumerate(chips):
                tiny_copy(1 + j, (*chip, c), me).wait_recv()
                tiny_copy(4 + j, (*chip, c), sibling).start()
            tiny_copy(0, sibling, me).wait_recv()
            for j, chip in enumerate(chips):
                tiny_copy(4 + j, (*chip, 1 - c), me).wait_recv()
            for cp in first + tiny_first:
                cp.wait_send()
            for j, chip in enumerate(chips):
                tiny_copy(4 + j, (*chip, c), sibling).wait_send()
                for h in (0, 1):
                    copy(h, 4 + j, (*chip, c), sibling).wait_send()
            for h in (0, 1):
                relay(h, 0).wait_send()
            tiny_mine.wait()
            for n in range(NSLOT):
                keep_copy(n).wait()

    hold = lambda k, i, order_ref, half_ref: (jnp.where(k == 0, i, nt - 1), 0)
    return pl.pallas_call(
        body, name="inproj_gather",
        grid_spec=pltpu.PrefetchScalarGridSpec(
            num_scalar_prefetch=2, grid=(NSLOT, nt),
            in_specs=[pl.BlockSpec((tm, D), hold),
                      pl.BlockSpec((1, D), lambda k, i, order_ref, half_ref: (0, 0)),
                      ANY, ANY],
            out_specs=[pl.BlockSpec((None, tm, HALF), lambda k, i, order_ref, half_ref: (order_ref[k], i, half_ref[k])),
                       pl.BlockSpec((tm, D), hold),
                       ANY, ANY],
            scratch_shapes=[pltpu.VMEM((NDEV, 2, D, HALF), BF16), pltpu.VMEM((t, D), BF16),
                            pltpu.SemaphoreType.DMA((2, 7)), pltpu.SemaphoreType.DMA((2, 7)),
                            pltpu.SemaphoreType.DMA((2,)), pltpu.SemaphoreType.DMA((NSLOT,)),
                            pltpu.SemaphoreType.DMA((7,)), pltpu.SemaphoreType.DMA((7,)), pltpu.SemaphoreType.DMA]),
        out_shape=[jax.ShapeDtypeStruct((NSEG, t, D), BF16), jax.ShapeDtypeStruct((t, D), BF16),
                   jax.ShapeDtypeStruct((NDEV,) + w_own.shape, BF16),
                   jax.ShapeDtypeStruct((NDEV,) + tiny_own.shape, F32)],
        compiler_params=_params(("arbitrary", "arbitrary")),
    )(order, halves, x2d, g_in, w_own, tiny_own)


def _tile_scan(a, u):
    row = lax.broadcasted_iota(jnp.int32, a.shape, 0)
    for d in (1, 2, 4):
        m = row >= d
        a_sh = pltpu.roll(a, d, 0)
        u_sh = pltpu.roll(u, d, 0)
        u = jnp.where(m, a * u_sh + u, u)
        a = jnp.where(m, a * a_sh, a)
    return a, u


def _tile_scan_rev(a, w):
    row = lax.broadcasted_iota(jnp.int32, a.shape, 0)
    for d in (1, 2, 4):
        m = row < 8 - d
        a_sh = pltpu.roll(a, 8 - d, 0)
        w_sh = pltpu.roll(w, 8 - d, 0)
        w = jnp.where(m, a * w_sh + w, w)
        a = jnp.where(m, a * a_sh, a)
    return a, w


def _lru_gates(xa_ref, c, cw_ref, cb_ref, wbd_ref, bx_ref, ba_ref, sp):
    win = _window_before(xa_ref, c, RC)
    xc = cb_ref[...] + cw_ref[3:4, :] * _shift_down(win, 0, RC)
    for s in (1, 2, 3):
        xc = xc + cw_ref[3 - s:4 - s, :] * _shift_down(win, s, RC)
    z = _dot(xc.astype(BF16), wbd_ref[...])
    gi = _sigmoid(z[:, :CB] + bx_ref[...])
    gr = _sigmoid(z[:, CB:] + ba_ref[...])
    log_a = -LRU_C * gr * sp
    return win, xc, gi, gr, log_a


def _lru_fwd(proj, conv_w, conv_b, wbd, bx, ba, lam, nb):
    t = nb * S

    def body(xa_ref, ga_ref, cw_ref, cb_ref, wbd_ref, bx_ref, ba_ref, lam_ref,
             ya_ref, hs_ref, xc_ref, gi_ref, gr_ref, a_s, u_s):
        sp = _softplus(-lam_ref[...])

        def gates(c, carry):
            _, xc, gi, gr, log_a = _lru_gates(xa_ref, c, cw_ref, cb_ref, wbd_ref, bx_ref, ba_ref, sp)
            rows = _rows(c, RC)
            a_s[rows, :] = jnp.exp(log_a)
            u_s[rows, :] = jnp.sqrt(-_expm1_nonpos(2.0 * log_a)) * (gi * xc)
            xc_ref[rows, :] = xc
            gi_ref[rows, :] = gi
            gr_ref[rows, :] = gr
            return carry

        lax.fori_loop(0, S // RC, gates, 0)

        def scan(g, h):
            for k in range(SCAN_GROUP):
                rows = pl.ds(pl.multiple_of(g * (8 * SCAN_GROUP), 8 * SCAN_GROUP) + 8 * k, 8)
                a_cum, u_cum = _tile_scan(a_s[rows, :], u_s[rows, :])
                hs_ref[rows, :] = u_cum + a_cum * h
                h = u_cum[7:8, :] + a_cum[7:8, :] * h
            return h

        lax.fori_loop(0, S // (8 * SCAN_GROUP), scan, jnp.zeros((1, CB), F32))

        def gate_out(c, carry):
            ga = ga_ref[_rows(c, RC), :].astype(F32)
            ya_ref[_rows(c, RC), :] = (ga * _sigmoid(ga) * hs_ref[_rows(c, RC), :]).astype(BF16)
            return carry

        lax.fori_loop(0, S // RC, gate_out, 0)

    vec = pl.BlockSpec((1, CB), lambda b, cb: (0, cb))
    blk = pl.BlockSpec((S, CB), lambda b, cb: (b, cb))
    return pl.pallas_call(
        body, name="lru_fwd", grid=(nb, NCB),
        in_specs=[pl.BlockSpec((None, S, CB), lambda b, cb: (0, b, cb)),
                  pl.BlockSpec((None, S, CB), lambda b, cb: (1, b, cb)),
                  pl.BlockSpec((4, CB), lambda b, cb: (0, cb)),
                  vec,
                  pl.BlockSpec((None, CB, 2 * CB), lambda b, cb: (cb, 0, 0)),
                  vec, vec, vec],
        out_specs=[blk] + [pl.BlockSpec((None, None, S, CB), lambda b, cb: (b, cb, 0, 0))] * 4,
        out_shape=[jax.ShapeDtypeStruct((t, D), BF16)] + [jax.ShapeDtypeStruct((nb, NCB, S, CB), F32)] * 4,
        scratch_shapes=[pltpu.VMEM((S, CB), F32), pltpu.VMEM((S, CB), F32)],
        compiler_params=_params(("arbitrary", "arbitrary")),
    )(proj, proj, conv_w, conv_b, wbd, bx, ba, lam)


def _lru_bwd(proj, hs, xc_f, gi_f, gr_f, dya, conv_w, wbd, lam, nb, give):
    t = nb * S
    ng = len(give)

    def body(xa_ref, ga_ref, hs_ref, xc_s, gi_s, gr_s, dya_ref, cw_ref, wbd_ref, lam_ref, *rest):
        give_refs, rest = rest[:ng], rest[ng:]
        dp_ref, dwbd_ref, vec_ref = rest[:3]
        got_refs, rest = rest[3:3 + ng], rest[3 + ng:]
        a_s, dl_s, dh_s, dxc_s, acc_s, send_sems, recv_sems = rest
        b = pl.program_id(1)
        exchange = _sibling_copies(give_refs, got_refs, send_sems, recv_sems)

        @pl.when(jnp.logical_and(pl.program_id(0) == 0, b == 0))
        def _():
            for cp in exchange:
                cp.start()

        lam_v = lam_ref[...]
        sp = _softplus(-lam_v)
        acc_s[...] = jnp.zeros_like(acc_s)

        @pl.when(b == 0)
        def _():
            dwbd_ref[...] = jnp.zeros_like(dwbd_ref)
            vec_ref[...] = jnp.zeros_like(vec_ref)

        def gates(c, carry):
            rows = _rows(c, RC)
            a_s[rows, :] = jnp.exp(-LRU_C * gr_s[rows, :] * sp)
            ga = ga_ref[rows, :].astype(F32)
            sg = _sigmoid(ga)
            dya_c = dya_ref[rows, :]
            dl_s[rows, :] = dya_c * (ga * sg)
            dp_ref[1, rows, :] = (dya_c * hs_ref[rows, :] * (sg * (1.0 + ga * (1.0 - sg)))).astype(BF16)
            return carry

        lax.fori_loop(0, S // RC, gates, 0)

        def scan(i, g_in):
            base = pl.multiple_of((S // (8 * SCAN_GROUP) - 1 - i) * (8 * SCAN_GROUP), 8 * SCAN_GROUP)
            row = lax.broadcasted_iota(jnp.int32, (8, CB), 0)
            for k in reversed(range(SCAN_GROUP)):
                rows = pl.ds(base + 8 * k, 8)
                a = a_s[rows, :]
                dl = dl_s[rows, :]
                a_cum, g_loc = _tile_scan_rev(a, a * dl)
                g = g_loc + a_cum * g_in
                dh_s[rows, :] = dl + jnp.where(row < 7, pltpu.roll(g, 7, 0), g_in)
                g_in = g_loc[0:1, :] + a_cum[0:1, :] * g_in
            return g_in

        lax.fori_loop(0, S // (8 * SCAN_GROUP), scan, jnp.zeros((1, CB), F32))

        dxc_s[pl.ds(S, 8), :] = jnp.zeros((8, CB), F32)

        def grads(c, carry):
            rows = _rows(c, RC)
            dh = dh_s[rows, :]
            h_prev = _shift_down(_window_before(hs_ref, c, RC), 1, RC)
            xc, gi, gr, a = xc_s[rows, :], gi_s[rows, :], gr_s[rows, :], a_s[rows, :]
            mult = jnp.sqrt(-_expm1_nonpos(-2.0 * LRU_C * gr * sp))
            dmult = dh * gi * xc
            d_log_a = dh * h_prev * a - dmult * (a * a) / mult
            dzi = dh * mult * xc * gi * (1.0 - gi)
            dzr = d_log_a * (-LRU_C * sp) * gr * (1.0 - gr)
            dz = jnp.concatenate([dzi, dzr], axis=1).astype(BF16)
            dxc_s[rows, :] = dh * mult * gi + _dot_nt(dz, wbd_ref[...])
            dwbd_ref[...] += _dot_tn(xc.astype(BF16), dz)
            acc_s[1:2, :] += jnp.sum(dzi, axis=0, keepdims=True)
            acc_s[2:3, :] += jnp.sum(dzr, axis=0, keepdims=True)
            acc_s[3:4, :] += jnp.sum(d_log_a * (-LRU_C * gr), axis=0, keepdims=True)
            return carry

        lax.fori_loop(0, S // RC, grads, 0)

        def conv_bwd(c, carry):
            rows = _rows(c, RC)
            dwin = dxc_s[pl.ds(pl.multiple_of(c * RC, RC), RC + 8), :]
            dxc = dwin[:RC, :]
            xwin = _window_before(xa_ref, c, RC)
            dxa = cw_ref[3:4, :] * dxc
            acc_s[0:1, :] += jnp.sum(dxc, axis=0, keepdims=True)
            acc_s[7:8, :] += jnp.sum(dxc * _shift_down(xwin, 0, RC), axis=0, keepdims=True)
            for s in (1, 2, 3):
                dxa = dxa + cw_ref[3 - s:4 - s, :] * _shift_up(dwin, s, RC)
                acc_s[7 - s:8 - s, :] += jnp.sum(dxc * _shift_down(xwin, s, RC), axis=0, keepdims=True)
            dp_ref[0, rows, :] = dxa.astype(BF16)
            return carry

        lax.fori_loop(0, S // RC, conv_bwd, 0)

        row = lax.broadcasted_iota(jnp.int32, acc_s.shape, 0)
        vec_ref[...] += jnp.where(row == 3, acc_s[...] * (-_sigmoid(-lam_v)), acc_s[...])

        @pl.when(jnp.logical_and(pl.program_id(0) == NCB - 1, b == nb - 1))
        def _():
            for cp in exchange:
                cp.wait()

    vec = pl.BlockSpec((1, CB), lambda cb, b: (0, cb))
    blk = pl.BlockSpec((S, CB), lambda cb, b: (b, cb))
    own = pl.BlockSpec((None, None, S, CB), lambda cb, b: (b, cb, 0, 0))
    return pl.pallas_call(
        body, name="lru_bwd", grid=(NCB, nb),
        in_specs=[pl.BlockSpec((None, S, CB), lambda cb, b: (0, b, cb)),
                  pl.BlockSpec((None, S, CB), lambda cb, b: (1, b, cb)),
                  own, own, own, own, blk,
                  pl.BlockSpec((4, CB), lambda cb, b: (0, cb)),
                  pl.BlockSpec((None, CB, 2 * CB), lambda cb, b: (cb, 0, 0)),
                  vec] + [ANY] * ng,
        out_specs=[pl.BlockSpec((2, S, CB), lambda cb, b: (0, b, cb)),
                   pl.BlockSpec((None, CB, 2 * CB), lambda cb, b: (cb, 0, 0)),
                   pl.BlockSpec((8, CB), lambda cb, b: (0, cb))] + [ANY] * ng,
        out_shape=[jax.ShapeDtypeStruct((2, t, D), BF16),
                   jax.ShapeDtypeStruct((NCB, CB, 2 * CB), F32),
                   jax.ShapeDtypeStruct((8, D), F32)]
        + [jax.ShapeDtypeStruct((4,) + g.shape[1:], g.dtype) for g in give],
        scratch_shapes=[pltpu.VMEM((S, CB), F32), pltpu.VMEM((S, CB), F32), pltpu.VMEM((S, CB), F32),
                        pltpu.VMEM((S + 8, CB), F32), pltpu.VMEM((8, CB), F32),
                        pltpu.SemaphoreType.DMA((ng, 4)), pltpu.SemaphoreType.DMA((ng, 4))],
        compiler_params=_params(("arbitrary", "arbitrary")),
    )(proj, proj, hs, xc_f, gi_f, gr_f, dya, conv_w, wbd, lam, *give)


def _retention_tables():
    f32 = np.float32
    log_g = np.log1p(-(f32(2.0) ** (f32(-5.0) - np.arange(HEADS, dtype=f32)))).astype(f32)
    idx = np.arange(CH, dtype=f32)
    diff = idx[:, None] - idx[None, :]
    inner = np.where(diff >= 0, np.exp(np.maximum(diff, f32(0.0))[None] * log_g[:, None, None]), f32(0.0)).astype(f32)
    cross = np.exp((idx[None, :] + f32(1.0)) * log_g[:, None]).astype(f32)
    state = np.exp((f32(CH - 1.0) - idx[None, :]) * log_g[:, None]).astype(f32)
    cross = np.ascontiguousarray(np.broadcast_to(cross[:, :, None], (HEADS, CH, DK)))
    state = np.ascontiguousarray(np.broadcast_to(state[:, :, None], (HEADS, CH, DK)))
    half = DK // 2
    freqs = (f32(10000.0) ** (-np.arange(half, dtype=f32) / f32(half))).astype(f32)
    ang = (np.arange(S, dtype=f32)[:, None] * freqs[None, :]).astype(f32)
    return tuple(jnp.asarray(a) for a in (inner, cross, state, np.cos(ang).astype(f32), np.sin(ang).astype(f32)))


def _rotate(x, cos, sin):
    half = DK // 2
    x1, x2 = x[:, :half], x[:, half:]
    return jnp.concatenate([x1 * cos - x2 * sin, x1 * sin + x2 * cos], axis=1)


def _rotate_back(d, cos, sin):
    half = DK // 2
    d1, d2 = d[:, :half], d[:, half:]
    return jnp.concatenate([d1 * cos + d2 * sin, d2 * cos - d1 * sin], axis=1)


def _ret_fwd(proj, gain, tables, nb, wp_own):
    t = nb * S
    inner_t, cross_t, state_t, cos_t, sin_t = tables

    def body(q_ref, k_ref, v_ref, gb_ref, gain_ref, dm_ref, cd_ref, sd_ref, cos_ref, sin_ref, wp_ref,
             yb_ref, qr_ref, kr_ref, o_ref, rs_ref, wpg_ref, r_s, send_sems, recv_sems, own_sems):
        b, hd = pl.program_id(0), pl.program_id(1)
        own, first, arrive, forward, others = _gather_copies([wp_ref], [wpg_ref], send_sems, recv_sems, own_sems)

        @pl.when(jnp.logical_and(b == 0, hd == 0))
        def _():
            for cp in own + first:
                cp.start()

        @pl.when(jnp.logical_and(b == nb - 1, hd == HEADS - 1))
        def _():
            for came, on in zip(arrive, forward):
                came.wait_recv()
                on.start()

        r_s[...] = jnp.zeros_like(r_s)
        chunk_decay = cd_ref[CH - 1:CH, :]

        def chunk(c, carry):
            rows = _rows(c, CH)
            cos, sin = cos_ref[rows, :], sin_ref[rows, :]
            qr = _rotate(q_ref[rows, :].astype(F32), cos, sin).astype(BF16)
            kr = (_rotate(k_ref[rows, :].astype(F32), cos, sin) * (DK ** -0.5)).astype(BF16)
            vb = v_ref[rows, :]
            v = vb.astype(F32)
            qr_ref[rows, :] = qr
            kr_ref[rows, :] = kr
            r = r_s[...]
            rb = r.astype(BF16)
            rs_ref[c] = rb
            p = (_dot_nt(qr, kr) * dm_ref[...]).astype(BF16)
            o = _dot(p, vb) + _dot(qr, rb) * cd_ref[...]
            r_s[...] = chunk_decay * r + _dot_tn(kr, (v * sd_ref[...]).astype(BF16))
            o_ref[rows, :] = o
            oc = o - jnp.mean(o, axis=-1, keepdims=True)
            rstd = lax.rsqrt(jnp.mean(oc * oc, axis=-1, keepdims=True) + EPS)
            gb = gb_ref[rows, :].astype(F32)
            yb_ref[rows, :] = (gb * _sigmoid(gb) * (oc * rstd * gain_ref[...])).astype(BF16)
            return carry

        lax.fori_loop(0, NCH, chunk, 0)

        @pl.when(jnp.logical_and(b == nb - 1, hd == HEADS - 1))
        def _():
            for cp in others:
                cp.wait_recv()
            for cp in first + forward:
                cp.wait_send()
            for cp in own:
                cp.wait()

    seg = lambda s: pl.BlockSpec((None, S, DK), lambda b, h: (s, b, h))
    tab = pl.BlockSpec((None, CH, DK), lambda b, h: (h, 0, 0))
    rot = pl.BlockSpec((S, DK // 2), lambda b, h: (0, 0))
    blk = pl.BlockSpec((S, DK), lambda b, h: (b, h))
    return pl.pallas_call(
        body, name="ret_fwd", grid=(nb, HEADS),
        in_specs=[seg(2), seg(3), seg(4), seg(5),
                  pl.BlockSpec((None, 1, DK), lambda b, h: (h, 0, 0)),
                  tab, tab, tab, rot, rot, ANY],
        out_specs=[blk, blk, blk, blk,
                   pl.BlockSpec((None, None, NCH, DK, DK), lambda b, h: (b, h, 0, 0, 0)), ANY],
        out_shape=[jax.ShapeDtypeStruct((t, D), BF16), jax.ShapeDtypeStruct((t, D), BF16),
                   jax.ShapeDtypeStruct((t, D), BF16), jax.ShapeDtypeStruct((t, D), F32),
                   jax.ShapeDtypeStruct((nb, HEADS, NCH, DK, DK), BF16),
                   jax.ShapeDtypeStruct((NDEV,) + wp_own.shape, wp_own.dtype)],
        scratch_shapes=[pltpu.VMEM((DK, DK), F32),
                        pltpu.SemaphoreType.DMA((1, 7)), pltpu.SemaphoreType.DMA((1, 7)), pltpu.SemaphoreType.DMA((1,))],
        compiler_params=_params(("arbitrary", "arbitrary")),
    )(proj, proj, proj, proj, gain, inner_t, cross_t, state_t, cos_t, sin_t, wp_own)


def _ret_bwd(proj, qr, kr, o, rs, dyb, gain, tables, nb, sums):
    t = nb * S
    ns = len(sums)
    inner_t, cross_t, state_t, cos_t, sin_t = tables

    def body(qr_ref, kr_ref, v_ref, gb_ref, o_ref, dyb_ref, rs_ref, gain_ref, dm_ref, cd_ref, sd_ref,
             cos_ref, sin_ref, *rest):
        sum_refs, rest = rest[:ns], rest[ns:]
        dp_ref, dgain_ref = rest[:2]
        part_refs, rest = rest[2:2 + ns], rest[2 + ns:]
        dr_s, send_sems, recv_sems, local_sems = rest
        mine, sends, recvs = _chip_copies(sum_refs, part_refs, send_sems, recv_sems, local_sems)

        @pl.when(jnp.logical_and(pl.program_id(0) == 0, pl.program_id(1) == 0))
        def _():
            for cp in mine + sends:
                cp.start()

        dr_s[...] = jnp.zeros_like(dr_s)
        chunk_decay = cd_ref[CH - 1:CH, :]

        @pl.when(pl.program_id(1) == 0)
        def _():
            dgain_ref[...] = jnp.zeros_like(dgain_ref)

        def chunk(i, carry):
            c = NCH - 1 - i
            rows = _rows(c, CH)
            gain_v = gain_ref[...]
            o_c = o_ref[rows, :]
            oc = o_c - jnp.mean(o_c, axis=-1, keepdims=True)
            rstd = lax.rsqrt(jnp.mean(oc * oc, axis=-1, keepdims=True) + EPS)
            yn = oc * rstd
            gb = gb_ref[rows, :].astype(F32)
            sg = _sigmoid(gb)
            dyb_c = dyb_ref[rows, :]
            dgn = dyb_c * (gb * sg)
            dp_ref[3, rows, :] = (dyb_c * (yn * gain_v) * (sg * (1.0 + gb * (1.0 - sg)))).astype(BF16)
            dgain_ref[...] += jnp.sum(dgn * yn, axis=0, keepdims=True)
            dyn = dgn * gain_v
            do = rstd * (dyn - jnp.mean(dyn, axis=-1, keepdims=True)
                         - yn * jnp.mean(dyn * yn, axis=-1, keepdims=True))
            dob = do.astype(BF16)
            dox = (do * cd_ref[...]).astype(BF16)

            q_c, k_c = qr_ref[rows, :], kr_ref[rows, :]
            vb = v_ref[rows, :]
            v = vb.astype(F32)
            vs = (v * sd_ref[...]).astype(BF16)
            rb = rs_ref[c]
            d_r = dr_s[...]
            drb = d_r.astype(BF16)
            dm = dm_ref[...]
            p = (_dot_nt(q_c, k_c) * dm).astype(BF16)
            dpm = (_dot_nt(dob, vb) * dm).astype(BF16)
            dq = _dot(dpm, k_c) + _dot_nt(dox, rb)
            dk = _dot_tn(dpm, q_c) + _dot_nt(vs, drb)
            dv = _dot_tn(p, dob) + _dot(k_c, drb) * sd_ref[...]
            dr_s[...] = chunk_decay * d_r + _dot_tn(q_c, dox)

            cos, sin = cos_ref[rows, :], sin_ref[rows, :]
            dp_ref[0, rows, :] = _rotate_back(dq, cos, sin).astype(BF16)
            dp_ref[1, rows, :] = (_rotate_back(dk, cos, sin) * (DK ** -0.5)).astype(BF16)
            dp_ref[2, rows, :] = dv.astype(BF16)
            return carry

        lax.fori_loop(0, NCH, chunk, 0)

        @pl.when(jnp.logical_and(pl.program_id(0) == HEADS - 1, pl.program_id(1) == nb - 1))
        def _():
            for cp in recvs:
                cp.wait_recv()
            for cp in sends:
                cp.wait_send()
            for cp in mine:
                cp.wait()

    seg = lambda s: pl.BlockSpec((None, S, DK), lambda h, b: (s, b, h))
    tab = pl.BlockSpec((None, CH, DK), lambda h, b: (h, 0, 0))
    rot = pl.BlockSpec((S, DK // 2), lambda h, b: (0, 0))
    blk = pl.BlockSpec((S, DK), lambda h, b: (b, h))
    one = pl.BlockSpec((None, 1, DK), lambda h, b: (h, 0, 0))
    return pl.pallas_call(
        body, name="ret_bwd", grid=(HEADS, nb),
        in_specs=[blk, blk, seg(4), seg(5), blk, blk,
                  pl.BlockSpec((None, None, NCH, DK, DK), lambda h, b: (b, h, 0, 0, 0)),
                  one, tab, tab, tab, rot, rot] + [ANY] * ns,
        out_specs=[pl.BlockSpec((4, S, DK), lambda h, b: (0, b, h)), one] + [ANY] * ns,
        out_shape=[jax.ShapeDtypeStruct((4, t, D), BF16), jax.ShapeDtypeStruct((HEADS, 1, DK), F32)]
        + [jax.ShapeDtypeStruct(a.shape, a.dtype) for a in sums],
        scratch_shapes=[pltpu.VMEM((DK, DK), F32), pltpu.SemaphoreType.DMA((ns, 3)), pltpu.SemaphoreType.DMA((ns, 3)),
                        pltpu.SemaphoreType.DMA((ns,))],
        compiler_params=_params(("arbitrary", "arbitrary")),
    )(qr, kr, proj, proj, o, dyb, rs, gain, inner_t, cross_t, state_t, cos_t, sin_t, *sums)


def _wblock(k):
    return pl.BlockSpec((NDEV, D // NDEV, D), lambda i: (0, k, 0))


def _tail(ya, yb, proj, x2d, tgt, wg, g_fin):
    t = x2d.shape[0]
    tm = 256

    def body(ya_ref, yb_ref, ma_ref, mb_ref, x_ref, t_ref, wa_ref, wb_ref, wo_ref, g_ref,
             dx2_ref, dya_ref, dyb_ref, dm_ref, mg_ref, doa_ref, dob_ref, gfin_ref, loss_ref):
        i = pl.program_id(0)

        @pl.when(i == 0)
        def _():
            gfin_ref[...] = jnp.zeros_like(gfin_ref)
            loss_ref[...] = jnp.zeros_like(loss_ref)

        wa = wa_ref[...].reshape(D, D)
        wb = wb_ref[...].reshape(D, D)
        wo = wo_ref[...].reshape(D, D)
        out_a = _dot(ya_ref[...], wa)
        out_b = _dot(yb_ref[...], wb)
        sa = _sigmoid(ma_ref[...].astype(F32))
        sb = _sigmoid(mb_ref[...].astype(F32))
        merged = (sa * out_a + sb * out_b).astype(BF16)
        mg_ref[...] = merged
        x2 = x_ref[...] + _dot(merged, wo)
        r2 = lax.rsqrt(jnp.mean(x2 * x2, axis=-1, keepdims=True) + EPS)
        xh = x2 * r2
        g = g_ref[...]
        err = xh * g - t_ref[...]
        loss_ref[...] += jnp.sum(err * err, axis=0, keepdims=True) * (0.5 / D)
        dy = err * (1.0 / D)
        gfin_ref[...] += jnp.sum(dy * xh, axis=0, keepdims=True)
        dxh = dy * g
        dx2 = r2 * (dxh - xh * jnp.mean(dxh * xh, axis=-1, keepdims=True))
        dx2_ref[...] = dx2
        dmerged = _dot_nt(dx2.astype(BF16), wo)
        doa = (sa * dmerged).astype(BF16)
        dob = (sb * dmerged).astype(BF16)
        doa_ref[...] = doa
        dob_ref[...] = dob
        dm_ref[0] = (dmerged * out_a * sa * (1.0 - sa)).astype(BF16)
        dm_ref[1] = (dmerged * out_b * sb * (1.0 - sb)).astype(BF16)
        dya_ref[...] = _dot_nt(doa, wa)
        dyb_ref[...] = _dot_nt(dob, wb)

    row = lambda: pl.BlockSpec((tm, D), lambda i: (i, 0))
    seg = lambda s: pl.BlockSpec((None, tm, D), lambda i: (s, i, 0))
    vec = pl.BlockSpec((1, D), lambda i: (0, 0))
    return pl.pallas_call(
        body, name="tail", grid=(t // tm,),
        in_specs=[row(), row(), seg(6), seg(7), row(), row(), _wblock(0), _wblock(1), _wblock(2), vec],
        out_specs=[row(), row(), row(), pl.BlockSpec((2, tm, D), lambda i: (0, i, 0)),
                   row(), row(), row(), vec, vec],
        out_shape=[jax.ShapeDtypeStruct((t, D), F32), jax.ShapeDtypeStruct((t, D), F32),
                   jax.ShapeDtypeStruct((t, D), F32), jax.ShapeDtypeStruct((2, t, D), BF16),
                   jax.ShapeDtypeStruct((t, D), BF16), jax.ShapeDtypeStruct((t, D), BF16),
                   jax.ShapeDtypeStruct((t, D), BF16), jax.ShapeDtypeStruct((1, D), F32),
                   jax.ShapeDtypeStruct((1, D), F32)],
        compiler_params=_params(("arbitrary",)),
    )(ya, yb, proj, proj, x2d, tgt, wg, wg, wg, g_fin)


def _tail_wgrad(ya, yb, merged, doa, dob, dx2):
    t = ya.shape[0]
    tm = 512

    def body(ya_ref, yb_ref, mg_ref, doa_ref, dob_ref, dx2_ref, ga_ref, gb_ref, go_ref):
        @pl.when(pl.program_id(0) == 0)
        def _():
            ga_ref[...] = jnp.zeros_like(ga_ref)
            gb_ref[...] = jnp.zeros_like(gb_ref)
            go_ref[...] = jnp.zeros_like(go_ref)

        ga_ref[...] += _dot_tn(ya_ref[...], doa_ref[...])
        gb_ref[...] += _dot_tn(yb_ref[...], dob_ref[...])
        go_ref[...] += _dot_tn(mg_ref[...], dx2_ref[...].astype(BF16))

    row = lambda: pl.BlockSpec((tm, D), lambda i: (i, 0))
    full = lambda: pl.BlockSpec((D, D), lambda i: (0, 0))
    return pl.pallas_call(
        body, name="tail_wgrad", grid=(t // tm,),
        in_specs=[row() for _ in range(6)], out_specs=[full(), full(), full()],
        out_shape=[jax.ShapeDtypeStruct((D, D), F32)] * 3,
        compiler_params=_params(("arbitrary",)),
    )(ya, yb, merged, doa, dob, dx2)


def _dproj_specs(tm, j_of, i_of):
    last = lambda j, i, lo, n: (jnp.clip(j - lo, 0, n - 1), i, 0)
    return [pl.BlockSpec((None, tm, D), lambda a, b: last(j_of(a, b), i_of(a, b), 0, 2)),
            pl.BlockSpec((None, tm, D), lambda a, b: last(j_of(a, b), i_of(a, b), 2, 4)),
            pl.BlockSpec((None, tm, D), lambda a, b: last(j_of(a, b), i_of(a, b), 6, 2))]


def _dproj_specs_ordered(tm):
    def spec(lo, n):
        def index(k, i, order_ref):
            seg = order_ref[k]
            mine = jnp.logical_and(seg >= lo, seg < lo + n)
            return jnp.where(mine, seg - lo, 0), jnp.where(mine, i, 0), 0
        return pl.BlockSpec((None, tm, D), index)
    return [spec(0, 2), spec(2, 4), spec(6, 2)]


def _dproj_pick(j, da_ref, db_ref, dc_ref, use):
    @pl.when(j < 2)
    def _():
        use(da_ref[...])

    @pl.when(jnp.logical_and(j >= 2, j < 6))
    def _():
        use(db_ref[...])

    @pl.when(j >= 6)
    def _():
        use(dc_ref[...])


RS_X, RS_Y, RS_XY = 0, 1, 2
RS_ROLES = ((RS_XY, RS_X, RS_Y), (RS_Y, RS_XY, RS_X))


def _rs_flip(rel, x, y):
    return ((1 - x, y), (x, 1 - y), (1 - x, 1 - y))[rel]


def _rs_order(x, y, c):
    order = []
    for s in range(4):
        chip = []
        for core in (0, 1):
            px, py = _rs_flip(RS_ROLES[core][s], x, y) if s < 3 else (x, y)
            chip.append(2 * px + py)
        keep = jnp.where(c == 0, chip[0], chip[1])
        give = jnp.where(c == 0, chip[1], chip[0])
        order += [2 * give + 1 - c, 2 * keep + c]
    return jnp.stack(order).astype(jnp.int32)


def _inproj_wgrad_rs(h, dpa, dpb, dpc, order, smalls):
    t = h.shape[0]
    tm = 1024
    nt = t // tm
    nsm = len(smalls)

    def body(order_ref, h_ref, da_ref, db_ref, dc_ref, *rest):
        small_refs, parts_ref, rest = rest[:nsm], rest[nsm], rest[nsm + 1:]
        all_refs, rest = rest[:nsm], rest[nsm:]
        (acc, sib, outb, far, give_send, give_recv, sum_send, sum_recv, far_send, far_recv, own_sem,
         small_send, small_recv, small_own) = rest
        k, i = pl.program_id(0), pl.program_id(1)
        x, y, c = _place()
        own, first, arrive, forward, others = _gather_copies(small_refs, all_refs, small_send, small_recv, small_own)

        @pl.when(jnp.logical_and(k == 0, i == 0))
        def _():
            for cp in own + first:
                cp.start()

        @pl.when(jnp.logical_and(k == 2, i == 0))
        def _():
            for came, on in zip(arrive, forward):
                came.wait_recv()
                on.start()

        def use(d):
            @pl.when(i == 0)
            def _():
                acc[k % 2] = _dot_tn(h_ref[...], d)

            @pl.when(i > 0)
            def _():
                acc[k % 2] += _dot_tn(h_ref[...], d)

        _dproj_pick(order_ref[k], da_ref, db_ref, dc_ref, use)

        def give_copy(s):
            return pltpu.make_async_remote_copy(
                src_ref=acc.at[0], dst_ref=sib.at[s % 2], send_sem=give_send.at[s], recv_sem=give_recv.at[s],
                device_id=(x, y, 1 - c), device_id_type=MESH)

        def sum_copy(s, core):
            slot = 0 if s < 2 else 1
            return pltpu.make_async_remote_copy(
                src_ref=outb.at[s], dst_ref=parts_ref.at[slot], send_sem=sum_send.at[slot], recv_sem=sum_recv.at[slot],
                device_id=(*_rs_flip(RS_ROLES[core][s], x, y), core), device_id_type=MESH)

        def far_copy(s, core):
            return pltpu.make_async_remote_copy(
                src_ref=outb.at[s], dst_ref=far, send_sem=far_send, recv_sem=far_recv,
                device_id=(*_rs_flip(RS_X if core == 0 else RS_Y, x, y), core), device_id_type=MESH)

        own_copy = pltpu.make_async_copy(outb.at[3], parts_ref.at[2], own_sem)

        def send_of(core, s):
            return far_copy(s, core) if RS_ROLES[core][s] == RS_XY else sum_copy(s, core)

        for s in range(4):
            @pl.when(jnp.logical_and(k == 2 * s, i == nt - 1))
            def _():
                give_copy(s).start()

            @pl.when(jnp.logical_and(k == 2 * s + 1, i == nt - 1))
            def _():
                give_copy(s).wait_recv()
                if s == 2:
                    far_copy(s, 0).wait_recv()
                    outb[s] = (acc[1] + sib[s % 2] + far[...].astype(F32)).astype(BF16)
                else:
                    outb[s] = (acc[1] + sib[s % 2]).astype(BF16)
                give_copy(s).wait_send()
                if s < 3:
                    for core in (0, 1):
                        @pl.when(c == core)
                        def _():
                            send_of(core, s).start()
                else:
                    own_copy.start()

        @pl.when(jnp.logical_and(k == NSEG - 1, i == nt - 1))
        def _():
            for slot in (0, 1):
                sum_copy(2 * slot, 0).wait_recv()
            for s in range(3):
                send_of(0, s).wait_send()
            own_copy.wait()
            for cp in others:
                cp.wait_recv()
            for cp in first + forward:
                cp.wait_send()
            for cp in own:
                cp.wait()

    return pl.pallas_call(
        body, name="inproj_wgrad_rs",
        grid_spec=pltpu.PrefetchScalarGridSpec(
            num_scalar_prefetch=1, grid=(NSEG, nt),
            in_specs=[pl.BlockSpec((tm, D), lambda k, i, order_ref: (i, 0))] + _dproj_specs_ordered(tm) + [ANY] * nsm,
            out_specs=[ANY] * (1 + nsm),
            scratch_shapes=[pltpu.VMEM((2, D, D), F32), pltpu.VMEM((2, D, D), F32), pltpu.VMEM((4, D, D), BF16),
                            pltpu.VMEM((D, D), BF16),
                            pltpu.SemaphoreType.DMA((4,)), pltpu.SemaphoreType.DMA((4,)),
                            pltpu.SemaphoreType.DMA((2,)), pltpu.SemaphoreType.DMA((2,)),
                            pltpu.SemaphoreType.DMA, pltpu.SemaphoreType.DMA, pltpu.SemaphoreType.DMA,
                            pltpu.SemaphoreType.DMA((nsm, 7)), pltpu.SemaphoreType.DMA((nsm, 7)),
                            pltpu.SemaphoreType.DMA((nsm,))]),
        out_shape=[jax.ShapeDtypeStruct((3, D, D), BF16)]
        + [jax.ShapeDtypeStruct((NDEV,) + a.shape, a.dtype) for a in smalls],
        compiler_params=_params(("arbitrary", "arbitrary")),
    )(order, h, dpa, dpb, dpc, *smalls)


def _inproj_dgrad(dpa, dpb, dpc, wg, x2d, dx2, g_in):
    t = x2d.shape[0]
    tm = 1024

    def body(da_ref, db_ref, dc_ref, w_ref, x_ref, dx2_ref, g_ref, gx_ref, gg_ref, acc_s):
        i, j = pl.program_id(0), pl.program_id(1)

        @pl.when(jnp.logical_and(i == 0, j == 0))
        def _():
            gg_ref[...] = jnp.zeros_like(gg_ref)

        @pl.when(j == 0)
        def _():
            acc_s[...] = jnp.zeros_like(acc_s)

        def use(d):
            acc_s[...] += _dot_nt(d, w_ref[...])

        _dproj_pick(j, da_ref, db_ref, dc_ref, use)

        @pl.when(j == NSEG - 1)
        def _():
            x = x_ref[...]
            r = lax.rsqrt(jnp.mean(x * x, axis=-1, keepdims=True) + EPS)
            xh = x * r
            dh = acc_s[...]
            gg_ref[...] += jnp.sum(dh * xh, axis=0, keepdims=True)
            dxh = dh * g_ref[...]
            gx_ref[...] = dx2_ref[...] + r * (dxh - xh * jnp.mean(dxh * xh, axis=-1, keepdims=True))

    row = lambda: pl.BlockSpec((tm, D), lambda i, j: (i, 0))
    vec = pl.BlockSpec((1, D), lambda i, j: (0, 0))
    return pl.pallas_call(
        body, name="inproj_dgrad", grid=(t // tm, NSEG),
        in_specs=_dproj_specs(tm, lambda i, j: j, lambda i, j: i)
        + [pl.BlockSpec((None, D, D), lambda i, j: (j, 0, 0)), row(), row(), vec],
        out_specs=[row(), vec],
        out_shape=[jax.ShapeDtypeStruct((t, D), F32), jax.ShapeDtypeStruct((1, D), F32)],
        scratch_shapes=[pltpu.VMEM((tm, D), F32)],
        compiler_params=_params(("arbitrary", "arbitrary")),
    )(dpa, dpb, dpc, wg, x2d, dx2, g_in)


def _adam_update(g, w, m, v):
    m_new = ADAM_B1 * m + (1.0 - ADAM_B1) * g
    v_new = ADAM_B2 * v + (1.0 - ADAM_B2) * (g * g)
    m_hat = m_new / (1.0 - ADAM_B1 ** ADAM_STEP)
    v_hat = v_new / (1.0 - ADAM_B2 ** ADAM_STEP)
    return -ADAM_LR * (m_hat / (jnp.sqrt(v_hat) + ADAM_EPS) + ADAM_WD * w), m_new, v_new


def _sum_in_order(ref):
    total = ref[0].astype(F32)
    for k in range(1, ref.shape[0]):
        total = total + ref[k].astype(F32)
    return total


def _adamw_small(me, vec_all, gx_all, ga_all, groups):
    flat = [a for grp in groups for a in grp]
    ng = len(groups)
    nshard = D // NDEV

    def body(me_ref, vec_ref, shard_ref, gx_ref, ga_ref, *refs):
        ins, outs = refs[:3 * ng], refs[3 * ng:]
        vec = _sum_in_order(vec_ref)
        shard = _sum_in_order(shard_ref)
        grads = [vec[r:r + 1, :] for r in range(6)]
        grads += [shard[0:4, :], shard[4:8, 0:DK // NDEV], _sum_in_order(gx_ref), _sum_in_order(ga_ref)]
        for n, g in enumerate(grads):
            delta, m_new, v_new = _adam_update(g, ins[3 * n][...], ins[3 * n + 1][...], ins[3 * n + 2][...])
            outs[4 * n][...] = g
            outs[4 * n + 1][...] = delta
            outs[4 * n + 2][...] = m_new
            outs[4 * n + 3][...] = v_new
        outs[4 * ng][...] = jnp.sum(vec[6:7, :], axis=1, keepdims=True)

    full = lambda a: pl.BlockSpec(a.shape, lambda i, me_ref, nd=len(a.shape): (0,) * nd)
    out_shape = [jax.ShapeDtypeStruct(w.shape, F32) for w, _, _ in groups for _ in range(4)]
    out_shape.append(jax.ShapeDtypeStruct((1, 1), F32))
    outs = pl.pallas_call(
        body, name="adamw_small",
        grid_spec=pltpu.PrefetchScalarGridSpec(
            num_scalar_prefetch=1, grid=(1,),
            in_specs=[full(vec_all),
                      pl.BlockSpec((NDEV, 8, nshard), lambda i, me_ref: (0, 1, me_ref[0])),
                      full(gx_all), full(ga_all)] + [full(a) for a in flat],
            out_specs=[full(s) for s in out_shape]),
        out_shape=out_shape,
        compiler_params=_params(("arbitrary",)),
    )(me, vec_all, vec_all, gx_all, ga_all, *flat)
    return [outs[4 * n:4 * n + 4] for n in range(ng)], outs[4 * ng]


def _adamw(name, items):
    n, rows, cols = items[0][0].shape
    tr = rows if rows <= 256 else 256
    k = len(items)

    def body(*refs):
        for a in range(k):
            p_ref, w_ref, m_ref, v_ref = refs[4 * a:4 * a + 4]
            g = _sum_in_order(p_ref)
            delta, m_new, v_new = _adam_update(g, w_ref[...], m_ref[...], v_ref[...])
            for o, val in zip(refs[4 * k + 4 * a:4 * k + 4 * a + 4], (g, delta, m_new, v_new)):
                o[...] = val

    blk = lambda: pl.BlockSpec((tr, cols), lambda i: (i, 0))
    outs = pl.pallas_call(
        body, name=name, grid=(rows // tr,),
        in_specs=[pl.BlockSpec((n, tr, cols), lambda i: (0, i, 0)), blk(), blk(), blk()] * k,
        out_specs=[blk() for _ in range(4 * k)],
        out_shape=[jax.ShapeDtypeStruct((rows, cols), F32)] * (4 * k),
        compiler_params=_params(("arbitrary",)),
    )(*[a for item in items for a in item])
    return [outs[4 * a:4 * a + 4] for a in range(k)]


ANY = pl.BlockSpec(memory_space=pl.ANY)


def _place():
    return lax.axis_index("x"), lax.axis_index("y"), lax.axis_index("c")


def _gather_copies(ins, outs, send_sems, recv_sems, own_sems):
    x, y, c = _place()
    me, sibling = (x, y, c), (x, y, 1 - c)
    chips = [(1 - x, y), (x, 1 - y), (1 - x, 1 - y)]
    n = len(ins)

    def copy(a, k, block, to, src=None):
        px, py, pc = block
        dst = outs[a].at[4 * px + 2 * py + pc]
        return pltpu.make_async_remote_copy(
            src_ref=dst if src is None else src, dst_ref=dst,
            send_sem=send_sems.at[a, k], recv_sem=recv_sems.at[a, k], device_id=to, device_id_type=MESH)

    own = [pltpu.make_async_copy(ins[a], outs[a].at[4 * x + 2 * y + c], own_sems.at[a]) for a in range(n)]
    first = []
    for a in range(n):
        first.append(copy(a, 0, me, sibling, src=ins[a]))
        first += [copy(a, 1 + j, me, (*chip, c), src=ins[a]) for j, chip in enumerate(chips)]
    arrive = [copy(a, 1 + j, (*chip, c), me) for j, chip in enumerate(chips) for a in range(n)]
    forward = [copy(a, 4 + j, (*chip, c), sibling) for j, chip in enumerate(chips) for a in range(n)]
    rest = [copy(a, 0, sibling, me) for a in range(n)]
    rest += [copy(a, 4 + j, (*chip, 1 - c), me) for a in range(n) for j, chip in enumerate(chips)]
    return own, first, arrive, forward, rest


def _sibling_copies(ins, outs, send_sems, recv_sems):
    x, y, c = _place()
    return [pltpu.make_async_remote_copy(
        src_ref=ins[a].at[2 * q + 1 - c], dst_ref=outs[a].at[q],
        send_sem=send_sems.at[a, q], recv_sem=recv_sems.at[a, q],
        device_id=(x, y, 1 - c), device_id_type=MESH) for a in range(len(ins)) for q in range(4)]


def _chip_copies(ins, outs, send_sems, recv_sems, local_sems):
    x, y, c = _place()
    my_chip = 2 * x + y
    chips = [(1 - x, y), (x, 1 - y), (1 - x, 1 - y)]
    n = len(ins)
    mine = [pltpu.make_async_copy(ins[a].at[my_chip], outs[a].at[my_chip], local_sems.at[a]) for a in range(n)]
    sends = [pltpu.make_async_remote_copy(
        src_ref=ins[a].at[2 * px + py], dst_ref=outs[a].at[my_chip],
        send_sem=send_sems.at[a, j], recv_sem=recv_sems.at[a, j],
        device_id=(px, py, c), device_id_type=MESH) for a in range(n) for j, (px, py) in enumerate(chips)]
    recvs = [pltpu.make_async_remote_copy(
        src_ref=ins[a].at[my_chip], dst_ref=outs[a].at[2 * px + py],
        send_sem=send_sems.at[a, j], recv_sem=recv_sems.at[a, j],
        device_id=(px, py, c), device_id_type=MESH) for a in range(n) for j, (px, py) in enumerate(chips)]
    return mine, sends, recvs


def _chip_sum(owns, gots, core):
    n = len(owns)
    _, rows, cols = owns[0].shape

    def body(core_ref, *refs):
        for a in range(n):
            refs[2 * n + a][...] = (refs[a][...] + refs[n + a][...]).astype(BF16)

    own_spec = pl.BlockSpec((None, rows, cols), lambda q, core_ref: (2 * q + core_ref[0], 0, 0))
    slab = pl.BlockSpec((None, rows, cols), lambda q, core_ref: (q, 0, 0))
    return pl.pallas_call(
        body, name="chip_sum",
        grid_spec=pltpu.PrefetchScalarGridSpec(
            num_scalar_prefetch=1, grid=(4,),
            in_specs=[own_spec] * n + [slab] * n, out_specs=[slab] * n),
        out_shape=[jax.ShapeDtypeStruct((4, rows, cols), BF16)] * n,
        compiler_params=_params(("arbitrary",)),
    )(core, *owns, *gots)


def _block_diag(w):
    w4 = w.reshape(NCB, 4, 64, 64)
    eye = jnp.eye(4, dtype=w.dtype)
    return (w4[:, :, :, None, :] * eye[None, :, None, :, None]).reshape(NCB, CB, CB)


def _block_diag_back(g):
    g5 = g.reshape(NCB, 4, 64, 4, 64)
    return jnp.stack([g5[:, m, :, m, :] for m in range(4)], axis=1).reshape(16, 64, 64)


def kernel(x, norm_in, w_in, conv_w, conv_b, gate_x_w, gate_x_b, gate_a_w, gate_a_b, lru_lambda, gn_gain, w_proj_a, w_proj_b, w_out, norm_final, loss_target, m_norm_in, m_w_in, m_conv_w, m_conv_b, m_gate_x_w, m_gate_x_b, m_gate_a_w, m_gate_a_b, m_lru_lambda, m_gn_gain, m_w_proj_a, m_w_proj_b, m_w_out, m_norm_final, v_norm_in, v_w_in, v_conv_w, v_conv_b, v_gate_x_w, v_gate_x_b, v_gate_a_w, v_gate_a_b, v_lru_lambda, v_gn_gain, v_w_proj_a, v_w_proj_b, v_w_out, v_norm_final):
    xi, yi, ci = _place()
    me = 4 * xi + 2 * yi + ci
    core = ci.astype(jnp.int32).reshape(1)
    nshard = D // NDEV
    nb = x.shape[0]
    t = nb * S
    x2d = x.reshape(t, D)
    tgt2d = loss_target.reshape(t, D)
    g_final = norm_final.reshape(1, D)
    wbd = jnp.concatenate([_block_diag(gate_x_w[0]), _block_diag(gate_a_w[0])], axis=-1).astype(BF16)
    tables = _retention_tables()

    wp_own = jnp.concatenate([w_proj_a[0], w_proj_b[0], w_out[0]], axis=0).astype(BF16)
    tiny = jnp.concatenate([conv_w[0], jnp.pad(gn_gain[0], ((0, 0), (0, nshard - DK // NDEV)))], axis=0)
    proj, h, wg, tiny_g = _inproj_gather(x2d, norm_in, w_in[0].astype(BF16), tiny, *_gather_order(xi, yi, ci))
    conv_w_full = tiny_g[:, 0:4, :].transpose(1, 0, 2).reshape(4, D)
    gain3 = tiny_g[:, 4:8, :DK // NDEV].transpose(1, 0, 2).reshape(HEADS, 1, DK)

    ya, hs, xc, gi, gr = _lru_fwd(proj, conv_w_full, conv_b, wbd, gate_x_b, gate_a_b, lru_lambda, nb)
    yb, qr, kr, o, rs, wpg = _ret_fwd(proj, gain3, tables, nb, wp_own)
    dx2, dya, dyb, dpc, merged, doa, dob, g_fin, loss_vec = _tail(ya, yb, proj, x2d, tgt2d, wpg, g_final)
    g_pa, g_pb, g_out = _tail_wgrad(ya, yb, merged, doa, dob, dx2)

    own = [g.reshape(NDEV, nshard, D) for g in (g_pa, g_pb, g_out)]
    dpa, g_wbd, g_vec, *got = _lru_bwd(proj, hs, xc, gi, gr, dya, conv_w_full, wbd, lru_lambda, nb, own)
    sums = _chip_sum(own, got, core)
    dpb, g_gain, *parts = _ret_bwd(proj, qr, kr, o, rs, dyb, gain3, tables, nb, sums)

    grad_x, g_norm_in = _inproj_dgrad(dpa, dpb, dpc, wg, x2d, dx2, norm_in)
    grad_x = grad_x.reshape(nb, S, D)

    gain_rows = jnp.pad(g_gain.reshape(HEADS, NDEV, DK // NDEV), ((0, 0), (0, 0), (0, nshard - DK // NDEV)))
    vec = jnp.concatenate([g_norm_in, g_vec[0:4], g_fin, loss_vec, jnp.zeros((1, D), F32), g_vec[4:8],
                           gain_rows.reshape(HEADS, D)], axis=0)
    g_gx = _block_diag_back(g_wbd[:, :, :CB]).reshape(D // 2, 128)
    g_ga = _block_diag_back(g_wbd[:, :, CB:]).reshape(D // 2, 128)
    parts_in, vec_all, gx_all, ga_all = _inproj_wgrad_rs(h, dpa, dpb, dpc, _rs_order(xi, yi, ci),
                                                         [vec, g_gx, g_ga])
    gx_all = gx_all.reshape(NDEV, D, 64)
    ga_all = ga_all.reshape(NDEV, D, 64)
    parts = [parts_in] + list(parts)

    res = {}
    (out,) = _adamw("adamw_w_in", [(parts[0], w_in[0], m_w_in[0], v_w_in[0])])
    res["w_in"] = [o[None] for o in out]
    square = [("w_proj_a", w_proj_a, m_w_proj_a, v_w_proj_a), ("w_proj_b", w_proj_b, m_w_proj_b, v_w_proj_b),
              ("w_out", w_out, m_w_out, v_w_out)]
    outs = _adamw("adamw_square", [(parts[1 + k], w[0], m[0], v[0]) for k, (_, w, m, v) in enumerate(square)])
    for (nm, _, _, _), out in zip(square, outs):
        res[nm] = [o[None] for o in out]

    row = lambda a: a.reshape(1, D)
    gate = lambda a: a.reshape(D, 64)
    groups = [("norm_in", norm_in, m_norm_in, v_norm_in, row), ("conv_b", conv_b, m_conv_b, v_conv_b, row),
              ("gate_x_b", gate_x_b, m_gate_x_b, v_gate_x_b, row), ("gate_a_b", gate_a_b, m_gate_a_b, v_gate_a_b, row),
              ("lru_lambda", lru_lambda, m_lru_lambda, v_lru_lambda, row),
              ("norm_final", norm_final, m_norm_final, v_norm_final, row),
              ("conv_w", conv_w, m_conv_w, v_conv_w, lambda a: a[0]), ("gn_gain", gn_gain, m_gn_gain, v_gn_gain, lambda a: a[0]),
              ("gate_x_w", gate_x_w, m_gate_x_w, v_gate_x_w, gate), ("gate_a_w", gate_a_w, m_gate_a_w, v_gate_a_w, gate)]
    small_out, loss = _adamw_small(me.astype(jnp.int32).reshape(1), vec_all, gx_all, ga_all,
                                   [tuple(view(a) for a in (w, m, v)) for _, w, m, v, view in groups])
    for (nm, w, _, _, _), out in zip(groups, small_out):
        res[nm] = [o.reshape(w.shape) for o in out]
    loss = loss.reshape(())

    order = ["norm_in", "w_in", "conv_w", "conv_b", "gate_x_w", "gate_x_b", "gate_a_w", "gate_a_b", "lru_lambda",
             "gn_gain", "w_proj_a", "w_proj_b", "w_out", "norm_final"]
    outs = [loss, grad_x]
    for k in range(4):
        outs += [res[nm][k] for nm in order]
    return tuple(outs)
```

```python
import numpy as np

import jax
import jax.numpy as jnp
from jax import lax
from jax.experimental import pallas as pl
from jax.experimental.pallas import tpu as pltpu

F32 = jnp.float32
BF16 = jnp.bfloat16
MESH = pl.DeviceIdType.MESH

D = 1024
S = 2048
NSEG = 8
NDEV = 8
HEADS = 4
DK = 256
CH = 256
NCH = S // CH
CB = 256
NCB = D // CB
RC = 256
SCAN_GROUP = 8
EPS = 1e-6
LRU_C = 8.0
VMEM_LIMIT = 56 * 1024 * 1024

ADAM_LR = 0.001
ADAM_B1 = 0.9
ADAM_B2 = 0.999
ADAM_EPS = 1e-08
ADAM_WD = 0.01
ADAM_STEP = 10


def _params(sem=None):
    return pltpu.CompilerParams(dimension_semantics=sem, vmem_limit_bytes=VMEM_LIMIT)


def _dot(a, b):
    return jnp.dot(a, b, preferred_element_type=F32)


def _dot_nt(a, b):
    return lax.dot_general(a, b, (((1,), (1,)), ((), ())), preferred_element_type=F32)


def _dot_tn(a, b):
    return lax.dot_general(a, b, (((0,), (0,)), ((), ())), preferred_element_type=F32)


def _sigmoid(x):
    return jax.nn.sigmoid(x)


def _expm1_nonpos(x):
    poly = x * (1.0 + x * (0.5 + x * (1.0 / 6.0 + x * (1.0 / 24.0))))
    return jnp.where(x > -0.05, poly, jnp.exp(x) - 1.0)


def _softplus(x):
    return jnp.maximum(x, 0.0) + jnp.log(1.0 + jnp.exp(-jnp.abs(x)))


def _rows(c, n):
    return pl.ds(pl.multiple_of(c * n, n), n)


def _window_before(ref, c, n):
    r0 = c * n
    if ref.dtype == BF16:
        prev = ref[pl.ds(pl.multiple_of(jnp.maximum(r0 - 16, 0), 16), 16), :].astype(F32)[8:, :]
    else:
        prev = ref[pl.ds(pl.multiple_of(jnp.maximum(r0 - 8, 0), 8), 8), :]
    prev = jnp.where(c > 0, prev, 0.0)
    return jnp.concatenate([prev, ref[_rows(c, n), :].astype(F32)], axis=0)


def _shift_down(win, s, n):
    if s == 0:
        return win[8:, :]
    return pltpu.roll(win, s, 0)[8:, :]


def _shift_up(win, s, n):
    if s == 0:
        return win[:n, :]
    return pltpu.roll(win, n + 8 - s, 0)[:n, :]


HALF = D // 2
GATHER_SLOTS = [("own", None, 0), ("own", None, 1), ("sib", None, 0), ("sib", None, 1)]
for _j, _h in ((0, 0), (1, 0), (0, 1), (1, 1), (2, 0), (2, 1)):
    GATHER_SLOTS += [("ici", _j, _h), ("fwd", _j, _h)]
NSLOT = len(GATHER_SLOTS)


def _gather_order(x, y, c):
    chips = [(1 - x, y), (x, 1 - y), (1 - x, 1 - y)]
    segs, halves = [], []
    for kind, j, h in GATHER_SLOTS:
        if kind == "own":
            seg = 4 * x + 2 * y + c
        elif kind == "sib":
            seg = 4 * x + 2 * y + 1 - c
        else:
            px, py = chips[j]
            seg = 4 * px + 2 * py + (c if kind == "ici" else 1 - c)
        segs.append(seg)
        halves.append(h)
    return jnp.stack(segs).astype(jnp.int32), jnp.asarray(halves, jnp.int32)


def _inproj_gather(x2d, g_in, w_own, tiny_own, order, halves):
    t = x2d.shape[0]
    tm = 1024
    nt = t // tm

    def body(order_ref, half_ref, x_ref, g_ref, w_own_ref, tiny_own_ref,
             proj_ref, h_ref, wg_ref, tinyg_ref,
             w_all, h_all, send_sems, recv_sems, own_sems, out_sems, tiny_send, tiny_recv, tiny_own_sem):
        k, i = pl.program_id(0), pl.program_id(1)
        x, y, c = _place()
        me, sibling = (x, y, c), (x, y, 1 - c)
        mine = 4 * x + 2 * y + c
        chips = [(1 - x, y), (x, 1 - y), (1 - x, 1 - y)]

        def copy(h, n, block, to, own_src=False):
            px, py, pc = block
            dst = w_all.at[4 * px + 2 * py + pc, h]
            return pltpu.make_async_remote_copy(
                src_ref=w_own_ref.at[:, pl.ds(h * HALF, HALF)] if own_src else dst, dst_ref=dst,
                send_sem=send_sems.at[h, n], recv_sem=recv_sems.at[h, n], device_id=to, device_id_type=MESH)

        def tiny_copy(n, block, to, own_src=False):
            px, py, pc = block
            dst = tinyg_ref.at[4 * px + 2 * py + pc]
            return pltpu.make_async_remote_copy(
                src_ref=tiny_own_ref if own_src else dst, dst_ref=dst,
                send_sem=tiny_send.at[n], recv_sem=tiny_recv.at[n], device_id=to, device_id_type=MESH)

        def own_copy(h):
            return pltpu.make_async_copy(w_own_ref.at[:, pl.ds(h * HALF, HALF)], w_all.at[mine, h], own_sems.at[h])

        tiny_mine = pltpu.make_async_copy(tiny_own_ref, tinyg_ref.at[mine], tiny_own_sem)

        def keep_copy(n):
            h = GATHER_SLOTS[n][2]
            return pltpu.make_async_copy(w_all.at[order_ref[n], h], wg_ref.at[order_ref[n], :, pl.ds(h * HALF, HALF)],
                                         out_sems.at[n])

        near = [(0, sibling), (1, (*chips[0], c)), (2, (*chips[1], c))]
        first = [copy(h, n, me, to, True) for h in (0, 1) for n, to in near]
        tiny_first = [tiny_copy(0, me, sibling, True)] + [tiny_copy(1 + j, me, (*chip, c), True) for j, chip in enumerate(chips)]

        def relay(h, j):
            seg = w_all.at[4 * chips[j][0] + 2 * chips[j][1] + c, h]
            return pltpu.make_async_remote_copy(
                src_ref=seg, dst_ref=seg, send_sem=send_sems.at[h, 3], recv_sem=recv_sems.at[h, 3],
                device_id=(*chips[1 - j], c), device_id_type=MESH)

        for n, (kind, j, h) in enumerate(GATHER_SLOTS):
            @pl.when(jnp.logical_and(k == n, i == 0))
            def _():
                if n == 0:
                    own_copy(0).start()
                    own_copy(1).start()
                    tiny_mine.start()
                    for cp in first + tiny_first:
                        cp.start()
                if kind == "own":
                    own_copy(h).wait()
                elif kind == "sib":
                    copy(h, 0, sibling, me).wait_recv()
                elif kind == "ici":
                    copy(h, 1 + j, (*chips[j], c), me).wait_recv()
                    copy(h, 4 + j, (*chips[j], c), sibling).start()
                    if j < 2:
                        @pl.when(c == j)
                        def _():
                            relay(h, j).start()
                else:
                    copy(h, 4 + j, (*chips[j], 1 - c), me).wait_recv()
                keep_copy(n).start()

        rows = pl.ds(pl.multiple_of(i * tm, tm), tm)

        @pl.when(k == 0)
        def _():
            xv = x_ref[...]
            r = lax.rsqrt(jnp.mean(xv * xv, axis=-1, keepdims=True) + EPS)
            hv = (xv * r * g_ref[...]).astype(BF16)
            h_ref[...] = hv
            h_all[rows, :] = hv

        proj_ref[...] = _dot(h_all[rows, :], w_all[order_ref[k], half_ref[k]]).astype(BF16)

        @pl.when(jnp.logical_and(k == NSLOT - 1, i == nt - 1))
        def _():
            for j, chip in enumerate(chips):
                tiny_copy(1 + j, (*chip, c), me).wait_recv()
                tiny_copy(4 + j, (*chip, c), sibling).start()
            tiny_copy(0, sibling, me).wait_recv()
            for j, chip in enumerate(chips):
                tiny_copy(4 + j, (*chip, 1 - c), me).wait_recv()
            for cp in first + tiny_first:
                cp.wait_send()
            for j, chip in enumerate(chips):
                tiny_copy(4 + j, (*chip, c), sibling).wait_send()
                for h in (0, 1):
                    copy(h, 4 + j, (*chip, c), sibling).wait_send()
            for h in (0, 1):
                relay(h, 0).wait_send()
            tiny_mine.wait()
            for n in range(NSLOT):
                keep_copy(n).wait()

    hold = lambda k, i, order_ref, half_ref: (jnp.where(k == 0, i, nt - 1), 0)
    return pl.pallas_call(
        body, name="inproj_gather",
        grid_spec=pltpu.PrefetchScalarGridSpec(
            num_scalar_prefetch=2, grid=(NSLOT, nt),
            in_specs=[pl.BlockSpec((tm, D), hold),
                      pl.BlockSpec((1, D), lambda k, i, order_ref, half_ref: (0, 0)),
                      ANY, ANY],
            out_specs=[pl.BlockSpec((None, tm, HALF), lambda k, i, order_ref, half_ref: (order_ref[k], i, half_ref[k])),
                       pl.BlockSpec((tm, D), hold),
                       ANY, ANY],
            scratch_shapes=[pltpu.VMEM((NDEV, 2, D, HALF), BF16), pltpu.VMEM((t, D), BF16),
                            pltpu.SemaphoreType.DMA((2, 7)), pltpu.SemaphoreType.DMA((2, 7)),
                            pltpu.SemaphoreType.DMA((2,)), pltpu.SemaphoreType.DMA((NSLOT,)),
                            pltpu.SemaphoreType.DMA((7,)), pltpu.SemaphoreType.DMA((7,)), pltpu.SemaphoreType.DMA]),
        out_shape=[jax.ShapeDtypeStruct((NSEG, t, D), BF16), jax.ShapeDtypeStruct((t, D), BF16),
                   jax.ShapeDtypeStruct((NDEV,) + w_own.shape, BF16),
                   jax.ShapeDtypeStruct((NDEV,) + tiny_own.shape, F32)],
        compiler_params=_params(("arbitrary", "arbitrary")),
    )(order, halves, x2d, g_in, w_own, tiny_own)


def _tile_scan(a, u):
    row = lax.broadcasted_iota(jnp.int32, a.shape, 0)
    for d in (1, 2, 4):
        m = row >= d
        a_sh = pltpu.roll(a, d, 0)
        u_sh = pltpu.roll(u, d, 0)
        u = jnp.where(m, a * u_sh + u, u)
        a = jnp.where(m, a * a_sh, a)
    return a, u


def _tile_scan_rev(a, w):
    row = lax.broadcasted_iota(jnp.int32, a.shape, 0)
    for d in (1, 2, 4):
        m = row < 8 - d
        a_sh = pltpu.roll(a, 8 - d, 0)
        w_sh = pltpu.roll(w, 8 - d, 0)
        w = jnp.where(m, a * w_sh + w, w)
        a = jnp.where(m, a * a_sh, a)
    return a, w


def _lru_gates(xa_ref, c, cw_ref, cb_ref, wbd_ref, bx_ref, ba_ref, sp):
    win = _window_before(xa_ref, c, RC)
    xc = cb_ref[...] + cw_ref[3:4, :] * _shift_down(win, 0, RC)
    for s in (1, 2, 3):
        xc = xc + cw_ref[3 - s:4 - s, :] * _shift_down(win, s, RC)
    z = _dot(xc.astype(BF16), wbd_ref[...])
    gi = _sigmoid(z[:, :CB] + bx_ref[...])
    gr = _sigmoid(z[:, CB:] + ba_ref[...])
    log_a = -LRU_C * gr * sp
    return win, xc, gi, gr, log_a


def _lru_fwd(proj, conv_w, conv_b, wbd, bx, ba, lam, nb):
    t = nb * S

    def body(xa_ref, ga_ref, cw_ref, cb_ref, wbd_ref, bx_ref, ba_ref, lam_ref,
             ya_ref, hs_ref, xc_ref, gi_ref, gr_ref, a_s, u_s):
        sp = _softplus(-lam_ref[...])

        def gates(c, carry):
            _, xc, gi, gr, log_a = _lru_gates(xa_ref, c, cw_ref, cb_ref, wbd_ref, bx_ref, ba_ref, sp)
            rows = _rows(c, RC)
            a_s[rows, :] = jnp.exp(log_a)
            u_s[rows, :] = jnp.sqrt(-_expm1_nonpos(2.0 * log_a)) * (gi * xc)
            xc_ref[rows, :] = xc
            gi_ref[rows, :] = gi
            gr_ref[rows, :] = gr
            return carry

        lax.fori_loop(0, S // RC, gates, 0)

        def scan(g, h):
            for k in range(SCAN_GROUP):
                rows = pl.ds(pl.multiple_of(g * (8 * SCAN_GROUP), 8 * SCAN_GROUP) + 8 * k, 8)
                a_cum, u_cum = _tile_scan(a_s[rows, :], u_s[rows, :])
                hs_ref[rows, :] = u_cum + a_cum * h
                h = u_cum[7:8, :] + a_cum[7:8, :] * h
            return h

        lax.fori_loop(0, S // (8 * SCAN_GROUP), scan, jnp.zeros((1, CB), F32))

        def gate_out(c, carry):
            ga = ga_ref[_rows(c, RC), :].astype(F32)
            ya_ref[_rows(c, RC), :] = (ga * _sigmoid(ga) * hs_ref[_rows(c, RC), :]).astype(BF16)
            return carry

        lax.fori_loop(0, S // RC, gate_out, 0)

    vec = pl.BlockSpec((1, CB), lambda b, cb: (0, cb))
    blk = pl.BlockSpec((S, CB), lambda b, cb: (b, cb))
    return pl.pallas_call(
        body, name="lru_fwd", grid=(nb, NCB),
        in_specs=[pl.BlockSpec((None, S, CB), lambda b, cb: (0, b, cb)),
                  pl.BlockSpec((None, S, CB), lambda b, cb: (1, b, cb)),
                  pl.BlockSpec((4, CB), lambda b, cb: (0, cb)),
                  vec,
                  pl.BlockSpec((None, CB, 2 * CB), lambda b, cb: (cb, 0, 0)),
                  vec, vec, vec],
        out_specs=[blk] + [pl.BlockSpec((None, None, S, CB), lambda b, cb: (b, cb, 0, 0))] * 4,
        out_shape=[jax.ShapeDtypeStruct((t, D), BF16)] + [jax.ShapeDtypeStruct((nb, NCB, S, CB), F32)] * 4,
        scratch_shapes=[pltpu.VMEM((S, CB), F32), pltpu.VMEM((S, CB), F32)],
        compiler_params=_params(("arbitrary", "arbitrary")),
    )(proj, proj, conv_w, conv_b, wbd, bx, ba, lam)


def _lru_bwd(proj, hs, xc_f, gi_f, gr_f, dya, conv_w, wbd, lam, nb, give):
    t = nb * S
    ng = len(give)

    def body(xa_ref, ga_ref, hs_ref, xc_s, gi_s, gr_s, dya_ref, cw_ref, wbd_ref, lam_ref, *rest):
        give_refs, rest = rest[:ng], rest[ng:]
        dp_ref, dwbd_ref, vec_ref = rest[:3]
        got_refs, rest = rest[3:3 + ng], rest[3 + ng:]
        a_s, dl_s, dh_s, dxc_s, acc_s, send_sems, recv_sems = rest
        b = pl.program_id(1)
        exchange = _sibling_copies(give_refs, got_refs, send_sems, recv_sems)

        @pl.when(jnp.logical_and(pl.program_id(0) == 0, b == 0))
        def _():
            for cp in exchange:
                cp.start()

        lam_v = lam_ref[...]
        sp = _softplus(-lam_v)
        acc_s[...] = jnp.zeros_like(acc_s)

        @pl.when(b == 0)
        def _():
            dwbd_ref[...] = jnp.zeros_like(dwbd_ref)
            vec_ref[...] = jnp.zeros_like(vec_ref)

        def gates(c, carry):
            rows = _rows(c, RC)
            a_s[rows, :] = jnp.exp(-LRU_C * gr_s[rows, :] * sp)
            ga = ga_ref[rows, :].astype(F32)
            sg = _sigmoid(ga)
            dya_c = dya_ref[rows, :]
            dl_s[rows, :] = dya_c * (ga * sg)
            dp_ref[1, rows, :] = (dya_c * hs_ref[rows, :] * (sg * (1.0 + ga * (1.0 - sg)))).astype(BF16)
            return carry

        lax.fori_loop(0, S // RC, gates, 0)

        def scan(i, g_in):
            base = pl.multiple_of((S // (8 * SCAN_GROUP) - 1 - i) * (8 * SCAN_GROUP), 8 * SCAN_GROUP)
            row = lax.broadcasted_iota(jnp.int32, (8, CB), 0)
            for k in reversed(range(SCAN_GROUP)):
                rows = pl.ds(base + 8 * k, 8)
                a = a_s[rows, :]
                dl = dl_s[rows, :]
                a_cum, g_loc = _tile_scan_rev(a, a * dl)
                g = g_loc + a_cum * g_in
                dh_s[rows, :] = dl + jnp.where(row < 7, pltpu.roll(g, 7, 0), g_in)
                g_in = g_loc[0:1, :] + a_cum[0:1, :] * g_in
            return g_in

        lax.fori_loop(0, S // (8 * SCAN_GROUP), scan, jnp.zeros((1, CB), F32))

        dxc_s[pl.ds(S, 8), :] = jnp.zeros((8, CB), F32)

        def grads(c, carry):
            rows = _rows(c, RC)
            dh = dh_s[rows, :]
            h_prev = _shift_down(_window_before(hs_ref, c, RC), 1, RC)
            xc, gi, gr, a = xc_s[rows, :], gi_s[rows, :], gr_s[rows, :], a_s[rows, :]
            mult = jnp.sqrt(-_expm1_nonpos(-2.0 * LRU_C * gr * sp))
            dmult = dh * gi * xc
            d_log_a = dh * h_prev * a - dmult * (a * a) / mult
            dzi = dh * mult * xc * gi * (1.0 - gi)
            dzr = d_log_a * (-LRU_C * sp) * gr * (1.0 - gr)
            dz = jnp.concatenate([dzi, dzr], axis=1).astype(BF16)
            dxc_s[rows, :] = dh * mult * gi + _dot_nt(dz, wbd_ref[...])
            dwbd_ref[...] += _dot_tn(xc.astype(BF16), dz)
            acc_s[1:2, :] += jnp.sum(dzi, axis=0, keepdims=True)
            acc_s[2:3, :] += jnp.sum(dzr, axis=0, keepdims=True)
            acc_s[3:4, :] += jnp.sum(d_log_a * (-LRU_C * gr), axis=0, keepdims=True)
            return carry

        lax.fori_loop(0, S // RC, grads, 0)

        def conv_bwd(c, carry):
            rows = _rows(c, RC)
            dwin = dxc_s[pl.ds(pl.multiple_of(c * RC, RC), RC + 8), :]
            dxc = dwin[:RC, :]
            xwin = _window_before(xa_ref, c, RC)
            dxa = cw_ref[3:4, :] * dxc
            acc_s[0:1, :] += jnp.sum(dxc, axis=0, keepdims=True)
            acc_s[7:8, :] += jnp.sum(dxc * _shift_down(xwin, 0, RC), axis=0, keepdims=True)
            for s in (1, 2, 3):
                dxa = dxa + cw_ref[3 - s:4 - s, :] * _shift_up(dwin, s, RC)
                acc_s[7 - s:8 - s, :] += jnp.sum(dxc * _shift_down(xwin, s, RC), axis=0, keepdims=True)
            dp_ref[0, rows, :] = dxa.astype(BF16)
            return carry

        lax.fori_loop(0, S // RC, conv_bwd, 0)

        row = lax.broadcasted_iota(jnp.int32, acc_s.shape, 0)
        vec_ref[...] += jnp.where(row == 3, acc_s[...] * (-_sigmoid(-lam_v)), acc_s[...])

        @pl.when(jnp.logical_and(pl.program_id(0) == NCB - 1, b == nb - 1))
        def _():
            for cp in exchange:
                cp.wait()

    vec = pl.BlockSpec((1, CB), lambda cb, b: (0, cb))
    blk = pl.BlockSpec((S, CB), lambda cb, b: (b, cb))
    own = pl.BlockSpec((None, None, S, CB), lambda cb, b: (b, cb, 0, 0))
    return pl.pallas_call(
        body, name="lru_bwd", grid=(NCB, nb),
        in_specs=[pl.BlockSpec((None, S, CB), lambda cb, b: (0, b, cb)),
                  pl.BlockSpec((None, S, CB), lambda cb, b: (1, b, cb)),
                  own, own, own, own, blk,
                  pl.BlockSpec((4, CB), lambda cb, b: (0, cb)),
                  pl.BlockSpec((None, CB, 2 * CB), lambda cb, b: (cb, 0, 0)),
                  vec] + [ANY] * ng,
        out_specs=[pl.BlockSpec((2, S, CB), lambda cb, b: (0, b, cb)),
                   pl.BlockSpec((None, CB, 2 * CB), lambda cb, b: (cb, 0, 0)),
                   pl.BlockSpec((8, CB), lambda cb, b: (0, cb))] + [ANY] * ng,
        out_shape=[jax.ShapeDtypeStruct((2, t, D), BF16),
                   jax.ShapeDtypeStruct((NCB, CB, 2 * CB), F32),
                   jax.ShapeDtypeStruct((8, D), F32)]
        + [jax.ShapeDtypeStruct((4,) + g.shape[1:], g.dtype) for g in give],
        scratch_shapes=[pltpu.VMEM((S, CB), F32), pltpu.VMEM((S, CB), F32), pltpu.VMEM((S, CB), F32),
                        pltpu.VMEM((S + 8, CB), F32), pltpu.VMEM((8, CB), F32),
                        pltpu.SemaphoreType.DMA((ng, 4)), pltpu.SemaphoreType.DMA((ng, 4))],
        compiler_params=_params(("arbitrary", "arbitrary")),
    )(proj, proj, hs, xc_f, gi_f, gr_f, dya, conv_w, wbd, lam, *give)


def _retention_tables():
    f32 = np.float32
    log_g = np.log1p(-(f32(2.0) ** (f32(-5.0) - np.arange(HEADS, dtype=f32)))).astype(f32)
    idx = np.arange(CH, dtype=f32)
    diff = idx[:, None] - idx[None, :]
    inner = np.where(diff >= 0, np.exp(np.maximum(diff, f32(0.0))[None] * log_g[:, None, None]), f32(0.0)).astype(f32)
    cross = np.exp((idx[None, :] + f32(1.0)) * log_g[:, None]).astype(f32)
    state = np.exp((f32(CH - 1.0) - idx[None, :]) * log_g[:, None]).astype(f32)
    cross = np.ascontiguousarray(np.broadcast_to(cross[:, :, None], (HEADS, CH, DK)))
    state = np.ascontiguousarray(np.broadcast_to(state[:, :, None], (HEADS, CH, DK)))
    half = DK // 2
    freqs = (f32(10000.0) ** (-np.arange(half, dtype=f32) / f32(half))).astype(f32)
    ang = (np.arange(S, dtype=f32)[:, None] * freqs[None, :]).astype(f32)
    return tuple(jnp.asarray(a) for a in (inner, cross, state, np.cos(ang).astype(f32), np.sin(ang).astype(f32)))


def _rotate(x, cos, sin):
    half = DK // 2
    x1, x2 = x[:, :half], x[:, half:]
    return jnp.concatenate([x1 * cos - x2 * sin, x1 * sin + x2 * cos], axis=1)


def _rotate_back(d, cos, sin):
    half = DK // 2
    d1, d2 = d[:, :half], d[:, half:]
    return jnp.concatenate([d1 * cos + d2 * sin, d2 * cos - d1 * sin], axis=1)


def _ret_fwd(proj, gain, tables, nb, wp_own):
    t = nb * S
    inner_t, cross_t, state_t, cos_t, sin_t = tables

    def body(q_ref, k_ref, v_ref, gb_ref, gain_ref, dm_ref, cd_ref, sd_ref, cos_ref, sin_ref, wp_ref,
             yb_ref, qr_ref, kr_ref, o_ref, rs_ref, wpg_ref, r_s, send_sems, recv_sems, own_sems):
        b, hd = pl.program_id(0), pl.program_id(1)
        own, first, arrive, forward, others = _gather_copies([wp_ref], [wpg_ref], send_sems, recv_sems, own_sems)

        @pl.when(jnp.logical_and(b == 0, hd == 0))
        def _():
            for cp in own + first:
                cp.start()

        @pl.when(jnp.logical_and(b == nb - 1, hd == HEADS - 1))
        def _():
            for came, on in zip(arrive, forward):
                came.wait_recv()
                on.start()

        r_s[...] = jnp.zeros_like(r_s)
        chunk_decay = cd_ref[CH - 1:CH, :]

        def chunk(c, carry):
            rows = _rows(c, CH)
            cos, sin = cos_ref[rows, :], sin_ref[rows, :]
            qr = _rotate(q_ref[rows, :].astype(F32), cos, sin).astype(BF16)
            kr = (_rotate(k_ref[rows, :].astype(F32), cos, sin) * (DK ** -0.5)).astype(BF16)
            vb = v_ref[rows, :]
            v = vb.astype(F32)
            qr_ref[rows, :] = qr
            kr_ref[rows, :] = kr
            r = r_s[...]
            rb = r.astype(BF16)
            rs_ref[c] = rb
            p = (_dot_nt(qr, kr) * dm_ref[...]).astype(BF16)
            o = _dot(p, vb) + _dot(qr, rb) * cd_ref[...]
            r_s[...] = chunk_decay * r + _dot_tn(kr, (v * sd_ref[...]).astype(BF16))
            o_ref[rows, :] = o
            oc = o - jnp.mean(o, axis=-1, keepdims=True)
            rstd = lax.rsqrt(jnp.mean(oc * oc, axis=-1, keepdims=True) + EPS)
            gb = gb_ref[rows, :].astype(F32)
            yb_ref[rows, :] = (gb * _sigmoid(gb) * (oc * rstd * gain_ref[...])).astype(BF16)
            return carry

        lax.fori_loop(0, NCH, chunk, 0)

        @pl.when(jnp.logical_and(b == nb - 1, hd == HEADS - 1))
        def _():
            for cp in others:
                cp.wait_recv()
            for cp in first + forward:
                cp.wait_send()
            for cp in own:
                cp.wait()

    seg = lambda s: pl.BlockSpec((None, S, DK), lambda b, h: (s, b, h))
    tab = pl.BlockSpec((None, CH, DK), lambda b, h: (h, 0, 0))
    rot = pl.BlockSpec((S, DK // 2), lambda b, h: (0, 0))
    blk = pl.BlockSpec((S, DK), lambda b, h: (b, h))
    return pl.pallas_call(
        body, name="ret_fwd", grid=(nb, HEADS),
        in_specs=[seg(2), seg(3), seg(4), seg(5),
                  pl.BlockSpec((None, 1, DK), lambda b, h: (h, 0, 0)),
                  tab, tab, tab, rot, rot, ANY],
        out_specs=[blk, blk, blk, blk,
                   pl.BlockSpec((None, None, NCH, DK, DK), lambda b, h: (b, h, 0, 0, 0)), ANY],
        out_shape=[jax.ShapeDtypeStruct((t, D), BF16), jax.ShapeDtypeStruct((t, D), BF16),
                   jax.ShapeDtypeStruct((t, D), BF16), jax.ShapeDtypeStruct((t, D), F32),
                   jax.ShapeDtypeStruct((nb, HEADS, NCH, DK, DK), BF16),
                   jax.ShapeDtypeStruct((NDEV,) + wp_own.shape, wp_own.dtype)],
        scratch_shapes=[pltpu.VMEM((DK, DK), F32),
                        pltpu.SemaphoreType.DMA((1, 7)), pltpu.SemaphoreType.DMA((1, 7)), pltpu.SemaphoreType.DMA((1,))],
        compiler_params=_params(("arbitrary", "arbitrary")),
    )(proj, proj, proj, proj, gain, inner_t, cross_t, state_t, cos_t, sin_t, wp_own)


def _ret_bwd(proj, qr, kr, o, rs, dyb, gain, tables, nb, sums):
    t = nb * S
    ns = len(sums)
    inner_t, cross_t, state_t, cos_t, sin_t = tables

    def body(qr_ref, kr_ref, v_ref, gb_ref, o_ref, dyb_ref, rs_ref, gain_ref, dm_ref, cd_ref, sd_ref,
             cos_ref, sin_ref, *rest):
        sum_refs, rest = rest[:ns], rest[ns:]
        dp_ref, dgain_ref = rest[:2]
        part_refs, rest = rest[2:2 + ns], rest[2 + ns:]
        dr_s, send_sems, recv_sems, local_sems = rest
        mine, sends, recvs = _chip_copies(sum_refs, part_refs, send_sems, recv_sems, local_sems)

        @pl.when(jnp.logical_and(pl.program_id(0) == 0, pl.program_id(1) == 0))
        def _():
            for cp in mine + sends:
                cp.start()

        dr_s[...] = jnp.zeros_like(dr_s)
        chunk_decay = cd_ref[CH - 1:CH, :]

        @pl.when(pl.program_id(1) == 0)
        def _():
            dgain_ref[...] = jnp.zeros_like(dgain_ref)

        def chunk(i, carry):
            c = NCH - 1 - i
            rows = _rows(c, CH)
            gain_v = gain_ref[...]
            o_c = o_ref[rows, :]
            oc = o_c - jnp.mean(o_c, axis=-1, keepdims=True)
            rstd = lax.rsqrt(jnp.mean(oc * oc, axis=-1, keepdims=True) + EPS)
            yn = oc * rstd
            gb = gb_ref[rows, :].astype(F32)
            sg = _sigmoid(gb)
            dyb_c = dyb_ref[rows, :]
            dgn = dyb_c * (gb * sg)
            dp_ref[3, rows, :] = (dyb_c * (yn * gain_v) * (sg * (1.0 + gb * (1.0 - sg)))).astype(BF16)
            dgain_ref[...] += jnp.sum(dgn * yn, axis=0, keepdims=True)
            dyn = dgn * gain_v
            do = rstd * (dyn - jnp.mean(dyn, axis=-1, keepdims=True)
                         - yn * jnp.mean(dyn * yn, axis=-1, keepdims=True))
            dob = do.astype(BF16)
            dox = (do * cd_ref[...]).astype(BF16)

            q_c, k_c = qr_ref[rows, :], kr_ref[rows, :]
            vb = v_ref[rows, :]
            v = vb.astype(F32)
            vs = (v * sd_ref[...]).astype(BF16)
            rb = rs_ref[c]
            d_r = dr_s[...]
            drb = d_r.astype(BF16)
            dm = dm_ref[...]
            p = (_dot_nt(q_c, k_c) * dm).astype(BF16)
            dpm = (_dot_nt(dob, vb) * dm).astype(BF16)
            dq = _dot(dpm, k_c) + _dot_nt(dox, rb)
            dk = _dot_tn(dpm, q_c) + _dot_nt(vs, drb)
            dv = _dot_tn(p, dob) + _dot(k_c, drb) * sd_ref[...]
            dr_s[...] = chunk_decay * d_r + _dot_tn(q_c, dox)

            cos, sin = cos_ref[rows, :], sin_ref[rows, :]
            dp_ref[0, rows, :] = _rotate_back(dq, cos, sin).astype(BF16)
            dp_ref[1, rows, :] = (_rotate_back(dk, cos, sin) * (DK ** -0.5)).astype(BF16)
            dp_ref[2, rows, :] = dv.astype(BF16)
            return carry

        lax.fori_loop(0, NCH, chunk, 0)

        @pl.when(jnp.logical_and(pl.program_id(0) == HEADS - 1, pl.program_id(1) == nb - 1))
        def _():
            for cp in recvs:
                cp.wait_recv()
            for cp in sends:
                cp.wait_send()
            for cp in mine:
                cp.wait()

    seg = lambda s: pl.BlockSpec((None, S, DK), lambda h, b: (s, b, h))
    tab = pl.BlockSpec((None, CH, DK), lambda h, b: (h, 0, 0))
    rot = pl.BlockSpec((S, DK // 2), lambda h, b: (0, 0))
    blk = pl.BlockSpec((S, DK), lambda h, b: (b, h))
    one = pl.BlockSpec((None, 1, DK), lambda h, b: (h, 0, 0))
    return pl.pallas_call(
        body, name="ret_bwd", grid=(HEADS, nb),
        in_specs=[blk, blk, seg(4), seg(5), blk, blk,
                  pl.BlockSpec((None, None, NCH, DK, DK), lambda h, b: (b, h, 0, 0, 0)),
                  one, tab, tab, tab, rot, rot] + [ANY] * ns,
        out_specs=[pl.BlockSpec((4, S, DK), lambda h, b: (0, b, h)), one] + [ANY] * ns,
        out_shape=[jax.ShapeDtypeStruct((4, t, D), BF16), jax.ShapeDtypeStruct((HEADS, 1, DK), F32)]
        + [jax.ShapeDtypeStruct(a.shape, a.dtype) for a in sums],
        scratch_shapes=[pltpu.VMEM((DK, DK), F32), pltpu.SemaphoreType.DMA((ns, 3)), pltpu.SemaphoreType.DMA((ns, 3)),
                        pltpu.SemaphoreType.DMA((ns,))],
        compiler_params=_params(("arbitrary", "arbitrary")),
    )(qr, kr, proj, proj, o, dyb, rs, gain, inner_t, cross_t, state_t, cos_t, sin_t, *sums)


def _wblock(k):
    return pl.BlockSpec((NDEV, D // NDEV, D), lambda i: (0, k, 0))


def _tail(ya, yb, proj, x2d, tgt, wg, g_fin):
    t = x2d.shape[0]
    tm = 256

    def body(ya_ref, yb_ref, ma_ref, mb_ref, x_ref, t_ref, wa_ref, wb_ref, wo_ref, g_ref,
             dx2_ref, dya_ref, dyb_ref, dm_ref, mg_ref, doa_ref, dob_ref, gfin_ref, loss_ref):
        i = pl.program_id(0)

        @pl.when(i == 0)
        def _():
            gfin_ref[...] = jnp.zeros_like(gfin_ref)
            loss_ref[...] = jnp.zeros_like(loss_ref)

        wa = wa_ref[...].reshape(D, D)
        wb = wb_ref[...].reshape(D, D)
        wo = wo_ref[...].reshape(D, D)
        out_a = _dot(ya_ref[...], wa)
        out_b = _dot(yb_ref[...], wb)
        sa = _sigmoid(ma_ref[...].astype(F32))
        sb = _sigmoid(mb_ref[...].astype(F32))
        merged = (sa * out_a + sb * out_b).astype(BF16)
        mg_ref[...] = merged
        x2 = x_ref[...] + _dot(merged, wo)
        r2 = lax.rsqrt(jnp.mean(x2 * x2, axis=-1, keepdims=True) + EPS)
        xh = x2 * r2
        g = g_ref[...]
        err = xh * g - t_ref[...]
        loss_ref[...] += jnp.sum(err * err, axis=0, keepdims=True) * (0.5 / D)
        dy = err * (1.0 / D)
        gfin_ref[...] += jnp.sum(dy * xh, axis=0, keepdims=True)
        dxh = dy * g
        dx2 = r2 * (dxh - xh * jnp.mean(dxh * xh, axis=-1, keepdims=True))
        dx2_ref[...] = dx2
        dmerged = _dot_nt(dx2.astype(BF16), wo)
        doa = (sa * dmerged).astype(BF16)
        dob = (sb * dmerged).astype(BF16)
        doa_ref[...] = doa
        dob_ref[...] = dob
        dm_ref[0] = (dmerged * out_a * sa * (1.0 - sa)).astype(BF16)
        dm_ref[1] = (dmerged * out_b * sb * (1.0 - sb)).astype(BF16)
        dya_ref[...] = _dot_nt(doa, wa)
        dyb_ref[...] = _dot_nt(dob, wb)

    row = lambda: pl.BlockSpec((tm, D), lambda i: (i, 0))
    seg = lambda s: pl.BlockSpec((None, tm, D), lambda i: (s, i, 0))
    vec = pl.BlockSpec((1, D), lambda i: (0, 0))
    return pl.pallas_call(
        body, name="tail", grid=(t // tm,),
        in_specs=[row(), row(), seg(6), seg(7), row(), row(), _wblock(0), _wblock(1), _wblock(2), vec],
        out_specs=[row(), row(), row(), pl.BlockSpec((2, tm, D), lambda i: (0, i, 0)),
                   row(), row(), row(), vec, vec],
        out_shape=[jax.ShapeDtypeStruct((t, D), F32), jax.ShapeDtypeStruct((t, D), F32),
                   jax.ShapeDtypeStruct((t, D), F32), jax.ShapeDtypeStruct((2, t, D), BF16),
                   jax.ShapeDtypeStruct((t, D), BF16), jax.ShapeDtypeStruct((t, D), BF16),
                   jax.ShapeDtypeStruct((t, D), BF16), jax.ShapeDtypeStruct((1, D), F32),
                   jax.ShapeDtypeStruct((1, D), F32)],
        compiler_params=_params(("arbitrary",)),
    )(ya, yb, proj, proj, x2d, tgt, wg, wg, wg, g_fin)


def _tail_wgrad(ya, yb, merged, doa, dob, dx2):
    t = ya.shape[0]
    tm = 512

    def body(ya_ref, yb_ref, mg_ref, doa_ref, dob_ref, dx2_ref, ga_ref, gb_ref, go_ref):
        @pl.when(pl.program_id(0) == 0)
        def _():
            ga_ref[...] = jnp.zeros_like(ga_ref)
            gb_ref[...] = jnp.zeros_like(gb_ref)
            go_ref[...] = jnp.zeros_like(go_ref)

        ga_ref[...] += _dot_tn(ya_ref[...], doa_ref[...])
        gb_ref[...] += _dot_tn(yb_ref[...], dob_ref[...])
        go_ref[...] += _dot_tn(mg_ref[...], dx2_ref[...].astype(BF16))

    row = lambda: pl.BlockSpec((tm, D), lambda i: (i, 0))
    full = lambda: pl.BlockSpec((D, D), lambda i: (0, 0))
    return pl.pallas_call(
        body, name="tail_wgrad", grid=(t // tm,),
        in_specs=[row() for _ in range(6)], out_specs=[full(), full(), full()],
        out_shape=[jax.ShapeDtypeStruct((D, D), F32)] * 3,
        compiler_params=_params(("arbitrary",)),
    )(ya, yb, merged, doa, dob, dx2)


def _dproj_specs(tm, j_of, i_of):
    last = lambda j, i, lo, n: (jnp.clip(j - lo, 0, n - 1), i, 0)
    return [pl.BlockSpec((None, tm, D), lambda a, b: last(j_of(a, b), i_of(a, b), 0, 2)),
            pl.BlockSpec((None, tm, D), lambda a, b: last(j_of(a, b), i_of(a, b), 2, 4)),
            pl.BlockSpec((None, tm, D), lambda a, b: last(j_of(a, b), i_of(a, b), 6, 2))]


def _dproj_specs_ordered(tm):
    def spec(lo, n):
        def index(k, i, order_ref):
            seg = order_ref[k]
            mine = jnp.logical_and(seg >= lo, seg < lo + n)
            return jnp.where(mine, seg - lo, 0), jnp.where(mine, i, 0), 0
        return pl.BlockSpec((None, tm, D), index)
    return [spec(0, 2), spec(2, 4), spec(6, 2)]


def _dproj_pick(j, da_ref, db_ref, dc_ref, use):
    @pl.when(j < 2)
    def _():
        use(da_ref[...])

    @pl.when(jnp.logical_and(j >= 2, j < 6))
    def _():
        use(db_ref[...])

    @pl.when(j >= 6)
    def _():
        use(dc_ref[...])


RS_X, RS_Y, RS_XY = 0, 1, 2
RS_ROLES = ((RS_XY, RS_X, RS_Y), (RS_Y, RS_XY, RS_X))


def _rs_flip(rel, x, y):
    return ((1 - x, y), (x, 1 - y), (1 - x, 1 - y))[rel]


def _rs_order(x, y, c):
    order = []
    for s in range(4):
        chip = []
        for core in (0, 1):
            px, py = _rs_flip(RS_ROLES[core][s], x, y) if s < 3 else (x, y)
            chip.append(2 * px + py)
        keep = jnp.where(c == 0, chip[0], chip[1])
        give = jnp.where(c == 0, chip[1], chip[0])
        order += [2 * give + 1 - c, 2 * keep + c]
    return jnp.stack(order).astype(jnp.int32)


def _inproj_wgrad_rs(h, dpa, dpb, dpc, order, smalls):
    t = h.shape[0]
    tm = 1024
    nt = t // tm
    nsm = len(smalls)

    def body(order_ref, h_ref, da_ref, db_ref, dc_ref, *rest):
        small_refs, parts_ref, rest = rest[:nsm], rest[nsm], rest[nsm + 1:]
        all_refs, rest = rest[:nsm], rest[nsm:]
        (acc, sib, outb, far, give_send, give_recv, sum_send, sum_recv, far_send, far_recv, own_sem,
         small_send, small_recv, small_own) = rest
        k, i = pl.program_id(0), pl.program_id(1)
        x, y, c = _place()
        own, first, arrive, forward, others = _gather_copies(small_refs, all_refs, small_send, small_recv, small_own)

        @pl.when(jnp.logical_and(k == 0, i == 0))
        def _():
            for cp in own + first:
                cp.start()

        @pl.when(jnp.logical_and(k == 2, i == 0))
        def _():
            for came, on in zip(arrive, forward):
                came.wait_recv()
                on.start()

        def use(d):
            @pl.when(i == 0)
            def _():
                acc[k % 2] = _dot_tn(h_ref[...], d)

            @pl.when(i > 0)
            def _():
                acc[k % 2] += _dot_tn(h_ref[...], d)

        _dproj_pick(order_ref[k], da_ref, db_ref, dc_ref, use)

        def give_copy(s):
            return pltpu.make_async_remote_copy(
                src_ref=acc.at[0], dst_ref=sib.at[s % 2], send_sem=give_send.at[s], recv_sem=give_recv.at[s],
                device_id=(x, y, 1 - c), device_id_type=MESH)

        def sum_copy(s, core):
            slot = 0 if s < 2 else 1
            return pltpu.make_async_remote_copy(
                src_ref=outb.at[s], dst_ref=parts_ref.at[slot], send_sem=sum_send.at[slot], recv_sem=sum_recv.at[slot],
                device_id=(*_rs_flip(RS_ROLES[core][s], x, y), core), device_id_type=MESH)

        def far_copy(s, core):
            return pltpu.make_async_remote_copy(
                src_ref=outb.at[s], dst_ref=far, send_sem=far_send, recv_sem=far_recv,
                device_id=(*_rs_flip(RS_X if core == 0 else RS_Y, x, y), core), device_id_type=MESH)

        own_copy = pltpu.make_async_copy(outb.at[3], parts_ref.at[2], own_sem)

        def send_of(core, s):
            return far_copy(s, core) if RS_ROLES[core][s] == RS_XY else sum_copy(s, core)

        for s in range(4):
            @pl.when(jnp.logical_and(k == 2 * s, i == nt - 1))
            def _():
                give_copy(s).start()

            @pl.when(jnp.logical_and(k == 2 * s + 1, i == nt - 1))
            def _():
                give_copy(s).wait_recv()
                if s == 2:
                    far_copy(s, 0).wait_recv()
                    outb[s] = (acc[1] + sib[s % 2] + far[...].astype(F32)).astype(BF16)
                else:
                    outb[s] = (acc[1] + sib[s % 2]).astype(BF16)
                give_copy(s).wait_send()
                if s < 3:
                    for core in (0, 1):
                        @pl.when(c == core)
                        def _():
                            send_of(core, s).start()
                else:
                    own_copy.start()

        @pl.when(jnp.logical_and(k == NSEG - 1, i == nt - 1))
        def _():
            for slot in (0, 1):
                sum_copy(2 * slot, 0).wait_recv()
            for s in range(3):
                send_of(0, s).wait_send()
            own_copy.wait()
            for cp in others:
                cp.wait_recv()
            for cp in first + forward:
                cp.wait_send()
            for cp in own:
                cp.wait()

    return pl.pallas_call(
        body, name="inproj_wgrad_rs",
        grid_spec=pltpu.PrefetchScalarGridSpec(
            num_scalar_prefetch=1, grid=(NSEG, nt),
            in_specs=[pl.BlockSpec((tm, D), lambda k, i, order_ref: (i, 0))] + _dproj_specs_ordered(tm) + [ANY] * nsm,
            out_specs=[ANY] * (1 + nsm),
            scratch_shapes=[pltpu.VMEM((2, D, D), F32), pltpu.VMEM((2, D, D), F32), pltpu.VMEM((4, D, D), BF16),
                            pltpu.VMEM((D, D), BF16),
                            pltpu.SemaphoreType.DMA((4,)), pltpu.SemaphoreType.DMA((4,)),
                            pltpu.SemaphoreType.DMA((2,)), pltpu.SemaphoreType.DMA((2,)),
                            pltpu.SemaphoreType.DMA, pltpu.SemaphoreType.DMA, pltpu.SemaphoreType.DMA,
                            pltpu.SemaphoreType.DMA((nsm, 7)), pltpu.SemaphoreType.DMA((nsm, 7)),
                            pltpu.SemaphoreType.DMA((nsm,))]),
        out_shape=[jax.ShapeDtypeStruct((3, D, D), BF16)]
        + [jax.ShapeDtypeStruct((NDEV,) + a.shape, a.dtype) for a in smalls],
        compiler_params=_params(("arbitrary", "arbitrary")),
    )(order, h, dpa, dpb, dpc, *smalls)


def _inproj_dgrad(dpa, dpb, dpc, wg, x2d, dx2, g_in):
    t = x2d.shape[0]
    tm = 1024

    def body(da_ref, db_ref, dc_ref, w_ref, x_ref, dx2_ref, g_ref, gx_ref, gg_ref, acc_s):
        i, j = pl.program_id(0), pl.program_id(1)

        @pl.when(jnp.logical_and(i == 0, j == 0))
        def _():
            gg_ref[...] = jnp.zeros_like(gg_ref)

        @pl.when(j == 0)
        def _():
            acc_s[...] = jnp.zeros_like(acc_s)

        def use(d):
            acc_s[...] += _dot_nt(d, w_ref[...])

        _dproj_pick(j, da_ref, db_ref, dc_ref, use)

        @pl.when(j == NSEG - 1)
        def _():
            x = x_ref[...]
            r = lax.rsqrt(jnp.mean(x * x, axis=-1, keepdims=True) + EPS)
            xh = x * r
            dh = acc_s[...]
            gg_ref[...] += jnp.sum(dh * xh, axis=0, keepdims=True)
            dxh = dh * g_ref[...]
            gx_ref[...] = dx2_ref[...] + r * (dxh - xh * jnp.mean(dxh * xh, axis=-1, keepdims=True))

    row = lambda: pl.BlockSpec((tm, D), lambda i, j: (i, 0))
    vec = pl.BlockSpec((1, D), lambda i, j: (0, 0))
    return pl.pallas_call(
        body, name="inproj_dgrad", grid=(t // tm, NSEG),
        in_specs=_dproj_specs(tm, lambda i, j: j, lambda i, j: i)
        + [pl.BlockSpec((None, D, D), lambda i, j: (j, 0, 0)), row(), row(), vec],
        out_specs=[row(), vec],
        out_shape=[jax.ShapeDtypeStruct((t, D), F32), jax.ShapeDtypeStruct((1, D), F32)],
        scratch_shapes=[pltpu.VMEM((tm, D), F32)],
        compiler_params=_params(("arbitrary", "arbitrary")),
    )(dpa, dpb, dpc, wg, x2d, dx2, g_in)


def _adam_update(g, w, m, v):
    m_new = ADAM_B1 * m + (1.0 - ADAM_B1) * g
    v_new = ADAM_B2 * v + (1.0 - ADAM_B2) * (g * g)
    m_hat = m_new / (1.0 - ADAM_B1 ** ADAM_STEP)
    v_hat = v_new / (1.0 - ADAM_B2 ** ADAM_STEP)
    return -ADAM_LR * (m_hat / (jnp.sqrt(v_hat) + ADAM_EPS) + ADAM_WD * w), m_new, v_new


def _sum_in_order(ref):
    total = ref[0].astype(F32)
    for k in range(1, ref.shape[0]):
        total = total + ref[k].astype(F32)
    return total


def _adamw_small(me, vec_all, gx_all, ga_all, groups):
    flat = [a for grp in groups for a in grp]
    ng = len(groups)
    nshard = D // NDEV

    def body(me_ref, vec_ref, shard_ref, gx_ref, ga_ref, *refs):
        ins, outs = refs[:3 * ng], refs[3 * ng:]
        vec = _sum_in_order(vec_ref)
        shard = _sum_in_order(shard_ref)
        grads = [vec[r:r + 1, :] for r in range(6)]
        grads += [shard[0:4, :], shard[4:8, 0:DK // NDEV], _sum_in_order(gx_ref), _sum_in_order(ga_ref)]
        for n, g in enumerate(grads):
            delta, m_new, v_new = _adam_update(g, ins[3 * n][...], ins[3 * n + 1][...], ins[3 * n + 2][...])
            outs[4 * n][...] = g
            outs[4 * n + 1][...] = delta
            outs[4 * n + 2][...] = m_new
            outs[4 * n + 3][...] = v_new
        outs[4 * ng][...] = jnp.sum(vec[6:7, :], axis=1, keepdims=True)

    full = lambda a: pl.BlockSpec(a.shape, lambda i, me_ref, nd=len(a.shape): (0,) * nd)
    out_shape = [jax.ShapeDtypeStruct(w.shape, F32) for w, _, _ in groups for _ in range(4)]
    out_shape.append(jax.ShapeDtypeStruct((1, 1), F32))
    outs = pl.pallas_call(
        body, name="adamw_small",
        grid_spec=pltpu.PrefetchScalarGridSpec(
            num_scalar_prefetch=1, grid=(1,),
            in_specs=[full(vec_all),
                      pl.BlockSpec((NDEV, 8, nshard), lambda i, me_ref: (0, 1, me_ref[0])),
                      full(gx_all), full(ga_all)] + [full(a) for a in flat],
            out_specs=[full(s) for s in out_shape]),
        out_shape=out_shape,
        compiler_params=_params(("arbitrary",)),
    )(me, vec_all, vec_all, gx_all, ga_all, *flat)
    return [outs[4 * n:4 * n + 4] for n in range(ng)], outs[4 * ng]


def _adamw(name, items):
    n, rows, cols = items[0][0].shape
    tr = rows if rows <= 256 else 256
    k = len(items)

    def body(*refs):
        for a in range(k):
            p_ref, w_ref, m_ref, v_ref = refs[4 * a:4 * a + 4]
            g = _sum_in_order(p_ref)
            delta, m_new, v_new = _adam_update(g, w_ref[...], m_ref[...], v_ref[...])
            for o, val in zip(refs[4 * k + 4 * a:4 * k + 4 * a + 4], (g, delta, m_new, v_new)):
                o[...] = val

    blk = lambda: pl.BlockSpec((tr, cols), lambda i: (i, 0))
    outs = pl.pallas_call(
        body, name=name, grid=(rows // tr,),
        in_specs=[pl.BlockSpec((n, tr, cols), lambda i: (0, i, 0)), blk(), blk(), blk()] * k,
        out_specs=[blk() for _ in range(4 * k)],
        out_shape=[jax.ShapeDtypeStruct((rows, cols), F32)] * (4 * k),
        compiler_params=_params(("arbitrary",)),
    )(*[a for item in items for a in item])
    return [outs[4 * a:4 * a + 4] for a in range(k)]


ANY = pl.BlockSpec(memory_space=pl.ANY)


def _place():
    return lax.axis_index("x"), lax.axis_index("y"), lax.axis_index("c")


def _gather_copies(ins, outs, send_sems, recv_sems, own_sems):
    x, y, c = _place()
    me, sibling = (x, y, c), (x, y, 1 - c)
    chips = [(1 - x, y), (x, 1 - y), (1 - x, 1 - y)]
    n = len(ins)

    def copy(a, k, block, to, src=None):
        px, py, pc = block
        dst = outs[a].at[4 * px + 2 * py + pc]
        return pltpu.make_async_remote_copy(
            src_ref=dst if src is None else src, dst_ref=dst,
            send_sem=send_sems.at[a, k], recv_sem=recv_sems.at[a, k], device_id=to, device_id_type=MESH)

    own = [pltpu.make_async_copy(ins[a], outs[a].at[4 * x + 2 * y + c], own_sems.at[a]) for a in range(n)]
    first = []
    for a in range(n):
        first.append(copy(a, 0, me, sibling, src=ins[a]))
        first += [copy(a, 1 + j, me, (*chip, c), src=ins[a]) for j, chip in enumerate(chips)]
    arrive = [copy(a, 1 + j, (*chip, c), me) for j, chip in enumerate(chips) for a in range(n)]
    forward = [copy(a, 4 + j, (*chip, c), sibling) for j, chip in enumerate(chips) for a in range(n)]
    rest = [copy(a, 0, sibling, me) for a in range(n)]
    rest += [copy(a, 4 + j, (*chip, 1 - c), me) for a in range(n) for j, chip in enumerate(chips)]
    return own, first, arrive, forward, rest


def _sibling_copies(ins, outs, send_sems, recv_sems):
    x, y, c = _place()
    return [pltpu.make_async_remote_copy(
        src_ref=ins[a].at[2 * q + 1 - c], dst_ref=outs[a].at[q],
        send_sem=send_sems.at[a, q], recv_sem=recv_sems.at[a, q],
        device_id=(x, y, 1 - c), device_id_type=MESH) for a in range(len(ins)) for q in range(4)]


def _chip_copies(ins, outs, send_sems, recv_sems, local_sems):
    x, y, c = _place()
    my_chip = 2 * x + y
    chips = [(1 - x, y), (x, 1 - y), (1 - x, 1 - y)]
    n = len(ins)
    mine = [pltpu.make_async_copy(ins[a].at[my_chip], outs[a].at[my_chip], local_sems.at[a]) for a in range(n)]
    sends = [pltpu.make_async_remote_copy(
        src_ref=ins[a].at[2 * px + py], dst_ref=outs[a].at[my_chip],
        send_sem=send_sems.at[a, j], recv_sem=recv_sems.at[a, j],
        device_id=(px, py, c), device_id_type=MESH) for a in range(n) for j, (px, py) in enumerate(chips)]
    recvs = [pltpu.make_async_remote_copy(
        src_ref=ins[a].at[my_chip], dst_ref=outs[a].at[2 * px + py],
        send_sem=send_sems.at[a, j], recv_sem=recv_sems.at[a, j],
        device_id=(px, py, c), device_id_type=MESH) for a in range(n) for j, (px, py) in enumerate(chips)]
    return mine, sends, recvs


def _chip_sum(owns, gots, core):
    n = len(owns)
    _, rows, cols = owns[0].shape

    def body(core_ref, *refs):
        for a in range(n):
            refs[2 * n + a][...] = (refs[a][...] + refs[n + a][...]).astype(BF16)

    own_spec = pl.BlockSpec((None, rows, cols), lambda q, core_ref: (2 * q + core_ref[0], 0, 0))
    slab = pl.BlockSpec((None, rows, cols), lambda q, core_ref: (q, 0, 0))
    return pl.pallas_call(
        body, name="chip_sum",
        grid_spec=pltpu.PrefetchScalarGridSpec(
            num_scalar_prefetch=1, grid=(4,),
            in_specs=[own_spec] * n + [slab] * n, out_specs=[slab] * n),
        out_shape=[jax.ShapeDtypeStruct((4, rows, cols), BF16)] * n,
        compiler_params=_params(("arbitrary",)),
    )(core, *owns, *gots)


def _block_diag(w):
    w4 = w.reshape(NCB, 4, 64, 64)
    eye = jnp.eye(4, dtype=w.dtype)
    return (w4[:, :, :, None, :] * eye[None, :, None, :, None]).reshape(NCB, CB, CB)


def _block_diag_back(g):
    g5 = g.reshape(NCB, 4, 64, 4, 64)
    return jnp.stack([g5[:, m, :, m, :] for m in range(4)], axis=1).reshape(16, 64, 64)


def kernel(x, norm_in, w_in, conv_w, conv_b, gate_x_w, gate_x_b, gate_a_w, gate_a_b, lru_lambda, gn_gain, w_proj_a, w_proj_b, w_out, norm_final, loss_target, m_norm_in, m_w_in, m_conv_w, m_conv_b, m_gate_x_w, m_gate_x_b, m_gate_a_w, m_gate_a_b, m_lru_lambda, m_gn_gain, m_w_proj_a, m_w_proj_b, m_w_out, m_norm_final, v_norm_in, v_w_in, v_conv_w, v_conv_b, v_gate_x_w, v_gate_x_b, v_gate_a_w, v_gate_a_b, v_lru_lambda, v_gn_gain, v_w_proj_a, v_w_proj_b, v_w_out, v_norm_final):
    xi, yi, ci = _place()
    me = 4 * xi + 2 * yi + ci
    core = ci.astype(jnp.int32).reshape(1)
    nshard = D // NDEV
    nb = x.shape[0]
    t = nb * S
    x2d = x.reshape(t, D)
    tgt2d = loss_target.reshape(t, D)
    g_final = norm_final.reshape(1, D)
    wbd = jnp.concatenate([_block_diag(gate_x_w[0]), _block_diag(gate_a_w[0])], axis=-1).astype(BF16)
    tables = _retention_tables()

    wp_own = jnp.concatenate([w_proj_a[0], w_proj_b[0], w_out[0]], axis=0).astype(BF16)
    tiny = jnp.concatenate([conv_w[0], jnp.pad(gn_gain[0], ((0, 0), (0, nshard - DK // NDEV)))], axis=0)
    proj, h, wg, tiny_g = _inproj_gather(x2d, norm_in, w_in[0].astype(BF16), tiny, *_gather_order(xi, yi, ci))
    conv_w_full = tiny_g[:, 0:4, :].transpose(1, 0, 2).reshape(4, D)
    gain3 = tiny_g[:, 4:8, :DK // NDEV].transpose(1, 0, 2).reshape(HEADS, 1, DK)

    ya, hs, xc, gi, gr = _lru_fwd(proj, conv_w_full, conv_b, wbd, gate_x_b, gate_a_b, lru_lambda, nb)
    yb, qr, kr, o, rs, wpg = _ret_fwd(proj, gain3, tables, nb, wp_own)
    dx2, dya, dyb, dpc, merged, doa, dob, g_fin, loss_vec = _tail(ya, yb, proj, x2d, tgt2d, wpg, g_final)
    g_pa, g_pb, g_out = _tail_wgrad(ya, yb, merged, doa, dob, dx2)

    own = [g.reshape(NDEV, nshard, D) for g in (g_pa, g_pb, g_out)]
    dpa, g_wbd, g_vec, *got = _lru_bwd(proj, hs, xc, gi, gr, dya, conv_w_full, wbd, lru_lambda, nb, own)
    sums = _chip_sum(own, got, core)
    dpb, g_gain, *parts = _ret_bwd(proj, qr, kr, o, rs, dyb, gain3, tables, nb, sums)

    grad_x, g_norm_in = _inproj_dgrad(dpa, dpb, dpc, wg, x2d, dx2, norm_in)
    grad_x = grad_x.reshape(nb, S, D)

    gain_rows = jnp.pad(g_gain.reshape(HEADS, NDEV, DK // NDEV), ((0, 0), (0, 0), (0, nshard - DK // NDEV)))
    vec = jnp.concatenate([g_norm_in, g_vec[0:4], g_fin, loss_vec, jnp.zeros((1, D), F32), g_vec[4:8],
                           gain_rows.reshape(HEADS, D)], axis=0)
    g_gx = _block_diag_back(g_wbd[:, :, :CB]).reshape(D // 2, 128)
    g_ga = _block_diag_back(g_wbd[:, :, CB:]).reshape(D // 2, 128)
    parts_in, vec_all, gx_all, ga_all = _inproj_wgrad_rs(h, dpa, dpb, dpc, _rs_order(xi, yi, ci),
                                                         [vec, g_gx, g_ga])
    gx_all = gx_all.reshape(NDEV, D, 64)
    ga_all = ga_all.reshape(NDEV, D, 64)
    parts = [parts_in] + list(parts)

    res = {}
    (out,) = _adamw("adamw_w_in", [(parts[0], w_in[0], m_w_in[0], v_w_in[0])])
    res["w_in"] = [o[None] for o in out]
    square = [("w_proj_a", w_proj_a, m_w_proj_a, v_w_proj_a), ("w_proj_b", w_proj_b, m_w_proj_b, v_w_proj_b),
              ("w_out", w_out, m_w_out, v_w_out)]
    outs = _adamw("adamw_square", [(parts[1 + k], w[0], m[0], v[0]) for k, (_, w, m, v) in enumerate(square)])
    for (nm, _, _, _), out in zip(square, outs):
        res[nm] = [o[None] for o in out]

    row = lambda a: a.reshape(1, D)
    gate = lambda a: a.reshape(D, 64)
    groups = [("norm_in", norm_in, m_norm_in, v_norm_in, row), ("conv_b", conv_b, m_conv_b, v_conv_b, row),
              ("gate_x_b", gate_x_b, m_gate_x_b, v_gate_x_b, row), ("gate_a_b", gate_a_b, m_gate_a_b, v_gate_a_b, row),
              ("lru_lambda", lru_lambda, m_lru_lambda, v_lru_lambda, row),
              ("norm_final", norm_final, m_norm_final, v_norm_final, row),
              ("conv_w", conv_w, m_conv_w, v_conv_w, lambda a: a[0]), ("gn_gain", gn_gain, m_gn_gain, v_gn_gain, lambda a: a[0]),
              ("gate_x_w", gate_x_w, m_gate_x_w, v_gate_x_w, gate), ("gate_a_w", gate_a_w, m_gate_a_w, v_gate_a_w, gate)]
    small_out, loss = _adamw_small(me.astype(jnp.int32).reshape(1), vec_all, gx_all, ga_all,
                                   [tuple(view(a) for a in (w, m, v)) for _, w, m, v, view in groups])
    for (nm, w, _, _, _), out in zip(groups, small_out):
        res[nm] = [o.reshape(w.shape) for o in out]
    loss = loss.reshape(())

    order = ["norm_in", "w_in", "conv_w", "conv_b", "gate_x_w", "gate_x_b", "gate_a_w", "gate_a_b", "lru_lambda",
             "gn_gain", "w_proj_a", "w_proj_b", "w_out", "norm_final"]
    outs = [loss, grad_x]
    for k in range(4):
        outs += [res[nm][k] for nm in order]
    return tuple(outs)
```

```python
import numpy as np

import jax
import jax.numpy as jnp
from jax import lax
from jax.experimental import pallas as pl
from jax.experimental.pallas import tpu as pltpu

F32 = jnp.float32
BF16 = jnp.bfloat16
MESH = pl.DeviceIdType.MESH

D = 1024
S = 2048
NSEG = 8
NDEV = 8
HEADS = 4
DK = 256
CH = 256
NCH = S // CH
CB = 256
NCB = D // CB
RC = 512
RC_CONV = 128
SCAN_GROUP = 8
EPS = 1e-6
LRU_C = 8.0
VMEM_LIMIT = 56 * 1024 * 1024

ADAM_LR = 0.001
ADAM_B1 = 0.9
ADAM_B2 = 0.999
ADAM_EPS = 1e-08
ADAM_WD = 0.01
ADAM_STEP = 10


def _params(sem=None):
    return pltpu.CompilerParams(dimension_semantics=sem, vmem_limit_bytes=VMEM_LIMIT)


def _dot(a, b):
    return jnp.dot(a, b, preferred_element_type=F32)


def _dot_nt(a, b):
    return lax.dot_general(a, b, (((1,), (1,)), ((), ())), preferred_element_type=F32)


def _dot_tn(a, b):
    return lax.dot_general(a, b, (((0,), (0,)), ((), ())), preferred_element_type=F32)


def _sigmoid(x):
    return jax.nn.sigmoid(x)


def _expm1_nonpos(x):
    poly = x * (1.0 + x * (0.5 + x * (1.0 / 6.0 + x * (1.0 / 24.0))))
    return jnp.where(x > -0.05, poly, jnp.exp(x) - 1.0)


def _softplus(x):
    return jnp.maximum(x, 0.0) + jnp.log(1.0 + jnp.exp(-jnp.abs(x)))


def _rows(c, n):
    return pl.ds(pl.multiple_of(c * n, n), n)


def _window_before(ref, c, n):
    r0 = c * n
    if ref.dtype == BF16:
        prev = ref[pl.ds(pl.multiple_of(jnp.maximum(r0 - 16, 0), 16), 16), :].astype(F32)[8:, :]
    else:
        prev = ref[pl.ds(pl.multiple_of(jnp.maximum(r0 - 8, 0), 8), 8), :]
    prev = jnp.where(c > 0, prev, 0.0)
    return jnp.concatenate([prev, ref[_rows(c, n), :].astype(F32)], axis=0)


def _shift_down(win, s, n):
    if s == 0:
        return win[8:, :]
    return pltpu.roll(win, s, 0)[8:, :]


def _shift_up(win, s, n):
    if s == 0:
        return win[:n, :]
    return pltpu.roll(win, n + 8 - s, 0)[:n, :]


HALF = D // 2
GATHER_SLOTS = [("own", None, 0), ("own", None, 1), ("sib", None, 0), ("sib", None, 1)]
for _j, _h in ((0, 0), (1, 0), (0, 1), (1, 1), (2, 0), (2, 1)):
    GATHER_SLOTS += [("ici", _j, _h), ("fwd", _j, _h)]
NSLOT = len(GATHER_SLOTS)


def _gather_order(x, y, c):
    chips = [(1 - x, y), (x, 1 - y), (1 - x, 1 - y)]
    segs, halves = [], []
    for kind, j, h in GATHER_SLOTS:
        if kind == "own":
            seg = 4 * x + 2 * y + c
        elif kind == "sib":
            seg = 4 * x + 2 * y + 1 - c
        else:
            px, py = chips[j]
            seg = 4 * px + 2 * py + (c if kind == "ici" else 1 - c)
        segs.append(seg)
        halves.append(h)
    return jnp.stack(segs).astype(jnp.int32), jnp.asarray(halves, jnp.int32)


def _inproj_gather(x2d, g_in, w_own, tiny_own, order, halves):
    t = x2d.shape[0]
    tm = 1024
    nt = t // tm

    def body(order_ref, half_ref, x_ref, g_ref, w_own_ref, tiny_own_ref,
             proj_ref, h_ref, wg_ref, tinyg_ref,
             w_all, h_all, send_sems, recv_sems, own_sems, out_sems, tiny_send, tiny_recv, tiny_own_sem):
        k, i = pl.program_id(0), pl.program_id(1)
        x, y, c = _place()
        me, sibling = (x, y, c), (x, y, 1 - c)
        mine = 4 * x + 2 * y + c
        chips = [(1 - x, y), (x, 1 - y), (1 - x, 1 - y)]

        def copy(h, n, block, to, own_src=False):
            px, py, pc = block
            dst = w_all.at[4 * px + 2 * py + pc, h]
            return pltpu.make_async_remote_copy(
                src_ref=w_own_ref.at[:, pl.ds(h * HALF, HALF)] if own_src else dst, dst_ref=dst,
                send_sem=send_sems.at[h, n], recv_sem=recv_sems.at[h, n], device_id=to, device_id_type=MESH)

        def tiny_copy(n, block, to, own_src=False):
            px, py, pc = block
            dst = tinyg_ref.at[4 * px + 2 * py + pc]
            return pltpu.make_async_remote_copy(
                src_ref=tiny_own_ref if own_src else dst, dst_ref=dst,
                send_sem=tiny_send.at[n], recv_sem=tiny_recv.at[n], device_id=to, device_id_type=MESH)

        def own_copy(h):
            return pltpu.make_async_copy(w_own_ref.at[:, pl.ds(h * HALF, HALF)], w_all.at[mine, h], own_sems.at[h])

        tiny_mine = pltpu.make_async_copy(tiny_own_ref, tinyg_ref.at[mine], tiny_own_sem)

        def keep_copy(n):
            h = GATHER_SLOTS[n][2]
            return pltpu.make_async_copy(w_all.at[order_ref[n], h], wg_ref.at[order_ref[n], :, pl.ds(h * HALF, HALF)],
                                         out_sems.at[n])

        near = [(0, sibling), (1, (*chips[0], c)), (2, (*chips[1], c))]
        first = [copy(h, n, me, to, True) for h in (0, 1) for n, to in near]
        tiny_first = [tiny_copy(0, me, sibling, True)] + [tiny_copy(1 + j, me, (*chip, c), True) for j, chip in enumerate(chips)]

        def relay(h, j):
            seg = w_all.at[4 * chips[j][0] + 2 * chips[j][1] + c, h]
            return pltpu.make_async_remote_copy(
                src_ref=seg, dst_ref=seg, send_sem=send_sems.at[h, 3], recv_sem=recv_sems.at[h, 3],
                device_id=(*chips[1 - j], c), device_id_type=MESH)

        for n, (kind, j, h) in enumerate(GATHER_SLOTS):
            @pl.when(jnp.logical_and(k == n, i == 0))
            def _():
                if n == 0:
                    own_copy(0).start()
                    own_copy(1).start()
                    tiny_mine.start()
                    for cp in first + tiny_first:
                        cp.start()
                if kind == "own":
                    own_copy(h).wait()
                elif kind == "sib":
                    copy(h, 0, sibling, me).wait_recv()
                elif kind == "ici":
                    copy(h, 1 + j, (*chips[j], c), me).wait_recv()
                    copy(h, 4 + j, (*chips[j], c), sibling).start()
                    if j < 2:
                        @pl.when(c == j)
                        def _():
                            relay(h, j).start()
                else:
                    copy(h, 4 + j, (*chips[j], 1 - c), me).wait_recv()
                keep_copy(n).start()

        rows = pl.ds(pl.multiple_of(i * tm, tm), tm)

        @pl.when(k == 0)
        def _():
            xv = x_ref[...]
            r = lax.rsqrt(jnp.mean(xv * xv, axis=-1, keepdims=True) + EPS)
            hv = (xv * r * g_ref[...]).astype(BF16)
            h_ref[...] = hv
            h_all[rows, :] = hv

        proj_ref[...] = _dot(h_all[rows, :], w_all[order_ref[k], half_ref[k]]).astype(BF16)

        @pl.when(jnp.logical_and(k == NSLOT - 1, i == nt - 1))
        def _():
            for j, chip in enumerate(chips):
                tiny_copy(1 + j, (*chip, c), me).wait_recv()
                tiny_copy(4 + j, (*chip, c), sibling).start()
            tiny_copy(0, sibling, me).wait_recv()
            for j, chip in enumerate(chips):
                tiny_copy(4 + j, (*chip, 1 - c), me).wait_recv()
            for cp in first + tiny_first:
                cp.wait_send()
            for j, chip in enumerate(chips):
                tiny_copy(4 + j, (*chip, c), sibling).wait_send()
                for h in (0, 1):
                    copy(h, 4 + j, (*chip, c), sibling).wait_send()
            for h in (0, 1):
                relay(h, 0).wait_send()
            tiny_mine.wait()
            for n in range(NSLOT):
                keep_copy(n).wait()

    hold = lambda k, i, order_ref, half_ref: (jnp.where(k == 0, i, nt - 1), 0)
    return pl.pallas_call(
        body, name="inproj_gather",
        grid_spec=pltpu.PrefetchScalarGridSpec(
            num_scalar_prefetch=2, grid=(NSLOT, nt),
            in_specs=[pl.BlockSpec((tm, D), hold),
                      pl.BlockSpec((1, D), lambda k, i, order_ref, half_ref: (0, 0)),
                      ANY, ANY],
            out_specs=[pl.BlockSpec((None, tm, HALF), lambda k, i, order_ref, half_ref: (order_ref[k], i, half_ref[k])),
                       pl.BlockSpec((tm, D), hold),
                       ANY, ANY],
            scratch_shapes=[pltpu.VMEM((NDEV, 2, D, HALF), BF16), pltpu.VMEM((t, D), BF16),
                            pltpu.SemaphoreType.DMA((2, 7)), pltpu.SemaphoreType.DMA((2, 7)),
                            pltpu.SemaphoreType.DMA((2,)), pltpu.SemaphoreType.DMA((NSLOT,)),
                            pltpu.SemaphoreType.DMA((7,)), pltpu.SemaphoreType.DMA((7,)), pltpu.SemaphoreType.DMA]),
        out_shape=[jax.ShapeDtypeStruct((NSEG, t, D), BF16), jax.ShapeDtypeStruct((t, D), BF16),
                   jax.ShapeDtypeStruct((NDEV,) + w_own.shape, BF16),
                   jax.ShapeDtypeStruct((NDEV,) + tiny_own.shape, F32)],
        compiler_params=_params(("arbitrary", "arbitrary")),
    )(order, halves, x2d, g_in, w_own, tiny_own)


def _tile_scan(a, u):
    row = lax.broadcasted_iota(jnp.int32, a.shape, 0)
    for d in (1, 2, 4):
        m = row >= d
        a_sh = pltpu.roll(a, d, 0)
        u_sh = pltpu.roll(u, d, 0)
        u = jnp.where(m, a * u_sh + u, u)
        a = jnp.where(m, a * a_sh, a)
    return a, u


def _tile_scan_rev(a, w):
    row = lax.broadcasted_iota(jnp.int32, a.shape, 0)
    for d in (1, 2, 4):
        m = row < 8 - d
        a_sh = pltpu.roll(a, 8 - d, 0)
        w_sh = pltpu.roll(w, 8 - d, 0)
        w = jnp.where(m, a * w_sh + w, w)
        a = jnp.where(m, a * a_sh, a)
    return a, w


def _lru_gates(xa_ref, c, cw_ref, cb_ref, wbd_ref, bx_ref, ba_ref, sp):
    win = _window_before(xa_ref, c, RC)
    xc = cb_ref[...] + cw_ref[3:4, :] * _shift_down(win, 0, RC)
    for s in (1, 2, 3):
        xc = xc + cw_ref[3 - s:4 - s, :] * _shift_down(win, s, RC)
    z = _dot(xc.astype(BF16), wbd_ref[...])
    gi = _sigmoid(z[:, :CB] + bx_ref[...])
    gr = _sigmoid(z[:, CB:] + ba_ref[...])
    log_a = -LRU_C * gr * sp
    return win, xc, gi, gr, log_a


def _lru_fwd(proj, conv_w, conv_b, wbd, bx, ba, lam, nb):
    t = nb * S

    def body(xa_ref, ga_ref, cw_ref, cb_ref, wbd_ref, bx_ref, ba_ref, lam_ref,
             ya_ref, hs_ref, xc_ref, gi_ref, gr_ref, a_s, u_s):
        sp = _softplus(-lam_ref[...])

        def gates(c, carry):
            _, xc, gi, gr, log_a = _lru_gates(xa_ref, c, cw_ref, cb_ref, wbd_ref, bx_ref, ba_ref, sp)
            rows = _rows(c, RC)
            a_s[rows, :] = jnp.exp(log_a)
            u_s[rows, :] = jnp.sqrt(-_expm1_nonpos(2.0 * log_a)) * (gi * xc)
            xc_ref[rows, :] = xc
            gi_ref[rows, :] = gi
            gr_ref[rows, :] = gr
            return carry

        lax.fori_loop(0, S // RC, gates, 0)

        def scan(g, h):
            for k in range(SCAN_GROUP):
                rows = pl.ds(pl.multiple_of(g * (8 * SCAN_GROUP), 8 * SCAN_GROUP) + 8 * k, 8)
                a_cum, u_cum = _tile_scan(a_s[rows, :], u_s[rows, :])
                hs_ref[rows, :] = u_cum + a_cum * h
                h = u_cum[7:8, :] + a_cum[7:8, :] * h
            return h

        lax.fori_loop(0, S // (8 * SCAN_GROUP), scan, jnp.zeros((1, CB), F32))

        def gate_out(c, carry):
            ga = ga_ref[_rows(c, RC), :].astype(F32)
            ya_ref[_rows(c, RC), :] = (ga * _sigmoid(ga) * hs_ref[_rows(c, RC), :]).astype(BF16)
            return carry

        lax.fori_loop(0, S // RC, gate_out, 0)

    vec = pl.BlockSpec((1, CB), lambda b, cb: (0, cb))
    blk = pl.BlockSpec((S, CB), lambda b, cb: (b, cb))
    return pl.pallas_call(
        body, name="lru_fwd", grid=(nb, NCB),
        in_specs=[pl.BlockSpec((None, S, CB), lambda b, cb: (0, b, cb)),
                  pl.BlockSpec((None, S, CB), lambda b, cb: (1, b, cb)),
                  pl.BlockSpec((4, CB), lambda b, cb: (0, cb)),
                  vec,
                  pl.BlockSpec((None, CB, 2 * CB), lambda b, cb: (cb, 0, 0)),
                  vec, vec, vec],
        out_specs=[blk] + [pl.BlockSpec((None, None, S, CB), lambda b, cb: (b, cb, 0, 0))] * 4,
        out_shape=[jax.ShapeDtypeStruct((t, D), BF16)] + [jax.ShapeDtypeStruct((nb, NCB, S, CB), F32)] * 4,
        scratch_shapes=[pltpu.VMEM((S, CB), F32), pltpu.VMEM((S, CB), F32)],
        compiler_params=_params(("arbitrary", "arbitrary")),
    )(proj, proj, conv_w, conv_b, wbd, bx, ba, lam)


def _lru_bwd(proj, hs, xc_f, gi_f, gr_f, dya, conv_w, wbd, lam, nb, give):
    t = nb * S
    ng = len(give)

    def body(xa_ref, ga_ref, hs_ref, xc_s, gi_s, gr_s, dya_ref, cw_ref, wbd_ref, lam_ref, *rest):
        give_refs, rest = rest[:ng], rest[ng:]
        dp_ref, dwbd_ref, vec_ref = rest[:3]
        got_refs, rest = rest[3:3 + ng], rest[3 + ng:]
        a_s, dl_s, dh_s, dxc_s, acc_s, send_sems, recv_sems = rest
        b = pl.program_id(1)
        exchange = _sibling_copies(give_refs, got_refs, send_sems, recv_sems)

        @pl.when(jnp.logical_and(pl.program_id(0) == 0, b == 0))
        def _():
            for cp in exchange:
                cp.start()

        lam_v = lam_ref[...]
        sp = _softplus(-lam_v)
        acc_s[...] = jnp.zeros_like(acc_s)

        @pl.when(b == 0)
        def _():
            dwbd_ref[...] = jnp.zeros_like(dwbd_ref)
            vec_ref[...] = jnp.zeros_like(vec_ref)

        def gates(c, carry):
            rows = _rows(c, RC)
            a_s[rows, :] = jnp.exp(-LRU_C * gr_s[rows, :] * sp)
            ga = ga_ref[rows, :].astype(F32)
            sg = _sigmoid(ga)
            dya_c = dya_ref[rows, :]
            dl_s[rows, :] = dya_c * (ga * sg)
            dp_ref[1, rows, :] = (dya_c * hs_ref[rows, :] * (sg * (1.0 + ga * (1.0 - sg)))).astype(BF16)
            return carry

        lax.fori_loop(0, S // RC, gates, 0)

        def scan(i, g_in):
            base = pl.multiple_of((S // (8 * SCAN_GROUP) - 1 - i) * (8 * SCAN_GROUP), 8 * SCAN_GROUP)
            row = lax.broadcasted_iota(jnp.int32, (8, CB), 0)
            for k in reversed(range(SCAN_GROUP)):
                rows = pl.ds(base + 8 * k, 8)
                a = a_s[rows, :]
                dl = dl_s[rows, :]
                a_cum, g_loc = _tile_scan_rev(a, a * dl)
                g = g_loc + a_cum * g_in
                dh_s[rows, :] = dl + jnp.where(row < 7, pltpu.roll(g, 7, 0), g_in)
                g_in = g_loc[0:1, :] + a_cum[0:1, :] * g_in
            return g_in

        lax.fori_loop(0, S // (8 * SCAN_GROUP), scan, jnp.zeros((1, CB), F32))

        dxc_s[pl.ds(S, 8), :] = jnp.zeros((8, CB), F32)

        def grads(c, carry):
            rows = _rows(c, RC)
            dh = dh_s[rows, :]
            h_prev = _shift_down(_window_before(hs_ref, c, RC), 1, RC)
            xc, gi, gr, a = xc_s[rows, :], gi_s[rows, :], gr_s[rows, :], a_s[rows, :]
            mult = jnp.sqrt(-_expm1_nonpos(-2.0 * LRU_C * gr * sp))
            dmult = dh * gi * xc
            d_log_a = dh * h_prev * a - dmult * (a * a) / mult
            dzi = dh * mult * xc * gi * (1.0 - gi)
            dzr = d_log_a * (-LRU_C * sp) * gr * (1.0 - gr)
            dz = jnp.concatenate([dzi, dzr], axis=1).astype(BF16)
            dxc_s[rows, :] = dh * mult * gi + _dot_nt(dz, wbd_ref[...])
            dwbd_ref[...] += _dot_tn(xc.astype(BF16), dz)
            acc_s[1:2, :] += jnp.sum(dzi, axis=0, keepdims=True)
            acc_s[2:3, :] += jnp.sum(dzr, axis=0, keepdims=True)
            acc_s[3:4, :] += jnp.sum(d_log_a * (-LRU_C * gr), axis=0, keepdims=True)
            return carry

        lax.fori_loop(0, S // RC, grads, 0)

        def conv_bwd(c, carry):
            rows = _rows(c, RC_CONV)
            dwin = dxc_s[pl.ds(pl.multiple_of(c * RC_CONV, RC_CONV), RC_CONV + 8), :]
            dxc = dwin[:RC_CONV, :]
            xwin = _window_before(xa_ref, c, RC_CONV)
            dxa = cw_ref[3:4, :] * dxc
            acc_s[0:1, :] += jnp.sum(dxc, axis=0, keepdims=True)
            acc_s[7:8, :] += jnp.sum(dxc * _shift_down(xwin, 0, RC_CONV), axis=0, keepdims=True)
            for s in (1, 2, 3):
                dxa = dxa + cw_ref[3 - s:4 - s, :] * _shift_up(dwin, s, RC_CONV)
                acc_s[7 - s:8 - s, :] += jnp.sum(dxc * _shift_down(xwin, s, RC_CONV), axis=0, keepdims=True)
            dp_ref[0, rows, :] = dxa.astype(BF16)
            return carry

        lax.fori_loop(0, S // RC_CONV, conv_bwd, 0)

        row = lax.broadcasted_iota(jnp.int32, acc_s.shape, 0)
        vec_ref[...] += jnp.where(row == 3, acc_s[...] * (-_sigmoid(-lam_v)), acc_s[...])

        @pl.when(jnp.logical_and(pl.program_id(0) == NCB - 1, b == nb - 1))
        def _():
            for cp in exchange:
                cp.wait()

    vec = pl.BlockSpec((1, CB), lambda cb, b: (0, cb))
    blk = pl.BlockSpec((S, CB), lambda cb, b: (b, cb))
    own = pl.BlockSpec((None, None, S, CB), lambda cb, b: (b, cb, 0, 0))
    return pl.pallas_call(
        body, name="lru_bwd", grid=(NCB, nb),
        in_specs=[pl.BlockSpec((None, S, CB), lambda cb, b: (0, b, cb)),
                  pl.BlockSpec((None, S, CB), lambda cb, b: (1, b, cb)),
                  own, own, own, own, blk,
                  pl.BlockSpec((4, CB), lambda cb, b: (0, cb)),
                  pl.BlockSpec((None, CB, 2 * CB), lambda cb, b: (cb, 0, 0)),
                  vec] + [ANY] * ng,
        out_specs=[pl.BlockSpec((2, S, CB), lambda cb, b: (0, b, cb)),
                   pl.BlockSpec((None, CB, 2 * CB), lambda cb, b: (cb, 0, 0)),
                   pl.BlockSpec((8, CB), lambda cb, b: (0, cb))] + [ANY] * ng,
        out_shape=[jax.ShapeDtypeStruct((2, t, D), BF16),
                   jax.ShapeDtypeStruct((NCB, CB, 2 * CB), F32),
                   jax.ShapeDtypeStruct((8, D), F32)]
        + [jax.ShapeDtypeStruct((4,) + g.shape[1:], g.dtype) for g in give],
        scratch_shapes=[pltpu.VMEM((S, CB), F32), pltpu.VMEM((S, CB), F32), pltpu.VMEM((S, CB), F32),
                        pltpu.VMEM((S + 8, CB), F32), pltpu.VMEM((8, CB), F32),
                        pltpu.SemaphoreType.DMA((ng, 4)), pltpu.SemaphoreType.DMA((ng, 4))],
        compiler_params=_params(("arbitrary", "arbitrary")),
    )(proj, proj, hs, xc_f, gi_f, gr_f, dya, conv_w, wbd, lam, *give)


def _retention_tables():
    f32 = np.float32
    log_g = np.log1p(-(f32(2.0) ** (f32(-5.0) - np.arange(HEADS, dtype=f32)))).astype(f32)
    idx = np.arange(CH, dtype=f32)
    diff = idx[:, None] - idx[None, :]
    inner = np.where(diff >= 0, np.exp(np.maximum(diff, f32(0.0))[None] * log_g[:, None, None]), f32(0.0)).astype(f32)
    cross = np.exp((idx[None, :] + f32(1.0)) * log_g[:, None]).astype(f32)
    state = np.exp((f32(CH - 1.0) - idx[None, :]) * log_g[:, None]).astype(f32)
    cross = np.ascontiguousarray(np.broadcast_to(cross[:, :, None], (HEADS, CH, DK)))
    state = np.ascontiguousarray(np.broadcast_to(state[:, :, None], (HEADS, CH, DK)))
    half = DK // 2
    freqs = (f32(10000.0) ** (-np.arange(half, dtype=f32) / f32(half))).astype(f32)
    ang = (np.arange(S, dtype=f32)[:, None] * freqs[None, :]).astype(f32)
    return tuple(jnp.asarray(a) for a in (inner, cross, state, np.cos(ang).astype(f32), np.sin(ang).astype(f32)))


def _rotate(x, cos, sin):
    half = DK // 2
    x1, x2 = x[:, :half], x[:, half:]
    return jnp.concatenate([x1 * cos - x2 * sin, x1 * sin + x2 * cos], axis=1)


def _rotate_back(d, cos, sin):
    half = DK // 2
    d1, d2 = d[:, :half], d[:, half:]
    return jnp.concatenate([d1 * cos + d2 * sin, d2 * cos - d1 * sin], axis=1)


def _ret_fwd(proj, gain, tables, nb, wp_own):
    t = nb * S
    inner_t, cross_t, state_t, cos_t, sin_t = tables

    def body(q_ref, k_ref, v_ref, gb_ref, gain_ref, dm_ref, cd_ref, sd_ref, cos_ref, sin_ref, wp_ref,
             yb_ref, qr_ref, kr_ref, o_ref, rs_ref, wpg_ref, r_s, send_sems, recv_sems, own_sems):
        b, hd = pl.program_id(0), pl.program_id(1)
        own, first, arrive, forward, others = _gather_copies([wp_ref], [wpg_ref], send_sems, recv_sems, own_sems)

        @pl.when(jnp.logical_and(b == 0, hd == 0))
        def _():
            for cp in own + first:
                cp.start()

        @pl.when(jnp.logical_and(b == nb - 1, hd == HEADS - 1))
        def _():
            for came, on in zip(arrive, forward):
                came.wait_recv()
                on.start()

        r_s[...] = jnp.zeros_like(r_s)
        chunk_decay = cd_ref[CH - 1:CH, :]

        def chunk(c, carry):
            rows = _rows(c, CH)
            cos, sin = cos_ref[rows, :], sin_ref[rows, :]
            qr = _rotate(q_ref[rows, :].astype(F32), cos, sin).astype(BF16)
            kr = (_rotate(k_ref[rows, :].astype(F32), cos, sin) * (DK ** -0.5)).astype(BF16)
            vb = v_ref[rows, :]
            v = vb.astype(F32)
            qr_ref[rows, :] = qr
            kr_ref[rows, :] = kr
            r = r_s[...]
            rb = r.astype(BF16)
            rs_ref[c] = rb
            p = (_dot_nt(qr, kr) * dm_ref[...]).astype(BF16)
            o = _dot(p, vb) + _dot(qr, rb) * cd_ref[...]
            r_s[...] = chunk_decay * r + _dot_tn(kr, (v * sd_ref[...]).astype(BF16))
            o_ref[rows, :] = o
            oc = o - jnp.mean(o, axis=-1, keepdims=True)
            rstd = lax.rsqrt(jnp.mean(oc * oc, axis=-1, keepdims=True) + EPS)
            gb = gb_ref[rows, :].astype(F32)
            yb_ref[rows, :] = (gb * _sigmoid(gb) * (oc * rstd * gain_ref[...])).astype(BF16)
            return carry

        lax.fori_loop(0, NCH, chunk, 0)

        @pl.when(jnp.logical_and(b == nb - 1, hd == HEADS - 1))
        def _():
            for cp in others:
                cp.wait_recv()
            for cp in first + forward:
                cp.wait_send()
            for cp in own:
                cp.wait()

    seg = lambda s: pl.BlockSpec((None, S, DK), lambda b, h: (s, b, h))
    tab = pl.BlockSpec((None, CH, DK), lambda b, h: (h, 0, 0))
    rot = pl.BlockSpec((S, DK // 2), lambda b, h: (0, 0))
    blk = pl.BlockSpec((S, DK), lambda b, h: (b, h))
    return pl.pallas_call(
        body, name="ret_fwd", grid=(nb, HEADS),
        in_specs=[seg(2), seg(3), seg(4), seg(5),
                  pl.BlockSpec((None, 1, DK), lambda b, h: (h, 0, 0)),
                  tab, tab, tab, rot, rot, ANY],
        out_specs=[blk, blk, blk, blk,
                   pl.BlockSpec((None, None, NCH, DK, DK), lambda b, h: (b, h, 0, 0, 0)), ANY],
        out_shape=[jax.ShapeDtypeStruct((t, D), BF16), jax.ShapeDtypeStruct((t, D), BF16),
                   jax.ShapeDtypeStruct((t, D), BF16), jax.ShapeDtypeStruct((t, D), F32),
                   jax.ShapeDtypeStruct((nb, HEADS, NCH, DK, DK), BF16),
                   jax.ShapeDtypeStruct((NDEV,) + wp_own.shape, wp_own.dtype)],
        scratch_shapes=[pltpu.VMEM((DK, DK), F32),
                        pltpu.SemaphoreType.DMA((1, 7)), pltpu.SemaphoreType.DMA((1, 7)), pltpu.SemaphoreType.DMA((1,))],
        compiler_params=_params(("arbitrary", "arbitrary")),
    )(proj, proj, proj, proj, gain, inner_t, cross_t, state_t, cos_t, sin_t, wp_own)


def _ret_bwd(proj, qr, kr, o, rs, dyb, gain, tables, nb, sums):
    t = nb * S
    ns = len(sums)
    inner_t, cross_t, state_t, cos_t, sin_t = tables

    def body(qr_ref, kr_ref, v_ref, gb_ref, o_ref, dyb_ref, rs_ref, gain_ref, dm_ref, cd_ref, sd_ref,
             cos_ref, sin_ref, *rest):
        sum_refs, rest = rest[:ns], rest[ns:]
        dp_ref, dgain_ref = rest[:2]
        part_refs, rest = rest[2:2 + ns], rest[2 + ns:]
        dr_s, send_sems, recv_sems, local_sems = rest
        mine, sends, recvs = _chip_copies(sum_refs, part_refs, send_sems, recv_sems, local_sems)

        @pl.when(jnp.logical_and(pl.program_id(0) == 0, pl.program_id(1) == 0))
        def _():
            for cp in mine + sends:
                cp.start()

        dr_s[...] = jnp.zeros_like(dr_s)
        chunk_decay = cd_ref[CH - 1:CH, :]

        @pl.when(pl.program_id(1) == 0)
        def _():
            dgain_ref[...] = jnp.zeros_like(dgain_ref)

        def chunk(i, carry):
            c = NCH - 1 - i
            rows = _rows(c, CH)
            gain_v = gain_ref[...]
            o_c = o_ref[rows, :]
            oc = o_c - jnp.mean(o_c, axis=-1, keepdims=True)
            rstd = lax.rsqrt(jnp.mean(oc * oc, axis=-1, keepdims=True) + EPS)
            yn = oc * rstd
            gb = gb_ref[rows, :].astype(F32)
            sg = _sigmoid(gb)
            dyb_c = dyb_ref[rows, :]
            dgn = dyb_c * (gb * sg)
            dp_ref[3, rows, :] = (dyb_c * (yn * gain_v) * (sg * (1.0 + gb * (1.0 - sg)))).astype(BF16)
            dgain_ref[...] += jnp.sum(dgn * yn, axis=0, keepdims=True)
            dyn = dgn * gain_v
            do = rstd * (dyn - jnp.mean(dyn, axis=-1, keepdims=True)
                         - yn * jnp.mean(dyn * yn, axis=-1, keepdims=True))
            dob = do.astype(BF16)
            dox = (do * cd_ref[...]).astype(BF16)

            q_c, k_c = qr_ref[rows, :], kr_ref[rows, :]
            vb = v_ref[rows, :]
            v = vb.astype(F32)
            vs = (v * sd_ref[...]).astype(BF16)
            rb = rs_ref[c]
            d_r = dr_s[...]
            drb = d_r.astype(BF16)
            dm = dm_ref[...]
            p = (_dot_nt(q_c, k_c) * dm).astype(BF16)
            dpm = (_dot_nt(dob, vb) * dm).astype(BF16)
            dq = _dot(dpm, k_c) + _dot_nt(dox, rb)
            dk = _dot_tn(dpm, q_c) + _dot_nt(vs, drb)
            dv = _dot_tn(p, dob) + _dot(k_c, drb) * sd_ref[...]
            dr_s[...] = chunk_decay * d_r + _dot_tn(q_c, dox)

            cos, sin = cos_ref[rows, :], sin_ref[rows, :]
            dp_ref[0, rows, :] = _rotate_back(dq, cos, sin).astype(BF16)
            dp_ref[1, rows, :] = (_rotate_back(dk, cos, sin) * (DK ** -0.5)).astype(BF16)
            dp_ref[2, rows, :] = dv.astype(BF16)
            return carry

        lax.fori_loop(0, NCH, chunk, 0)

        @pl.when(jnp.logical_and(pl.program_id(0) == HEADS - 1, pl.program_id(1) == nb - 1))
        def _():
            for cp in recvs:
                cp.wait_recv()
            for cp in sends:
                cp.wait_send()
            for cp in mine:
                cp.wait()

    seg = lambda s: pl.BlockSpec((None, S, DK), lambda h, b: (s, b, h))
    tab = pl.BlockSpec((None, CH, DK), lambda h, b: (h, 0, 0))
    rot = pl.BlockSpec((S, DK // 2), lambda h, b: (0, 0))
    blk = pl.BlockSpec((S, DK), lambda h, b: (b, h))
    one = pl.BlockSpec((None, 1, DK), lambda h, b: (h, 0, 0))
    return pl.pallas_call(
        body, name="ret_bwd", grid=(HEADS, nb),
        in_specs=[blk, blk, seg(4), seg(5), blk, blk,
                  pl.BlockSpec((None, None, NCH, DK, DK), lambda h, b: (b, h, 0, 0, 0)),
                  one, tab, tab, tab, rot, rot] + [ANY] * ns,
        out_specs=[pl.BlockSpec((4, S, DK), lambda h, b: (0, b, h)), one] + [ANY] * ns,
        out_shape=[jax.ShapeDtypeStruct((4, t, D), BF16), jax.ShapeDtypeStruct((HEADS, 1, DK), F32)]
        + [jax.ShapeDtypeStruct(a.shape, a.dtype) for a in sums],
        scratch_shapes=[pltpu.VMEM((DK, DK), F32), pltpu.SemaphoreType.DMA((ns, 3)), pltpu.SemaphoreType.DMA((ns, 3)),
                        pltpu.SemaphoreType.DMA((ns,))],
        compiler_params=_params(("arbitrary", "arbitrary")),
    )(qr, kr, proj, proj, o, dyb, rs, gain, inner_t, cross_t, state_t, cos_t, sin_t, *sums)


def _wblock(k):
    return pl.BlockSpec((NDEV, D // NDEV, D), lambda i: (0, k, 0))


def _tail(ya, yb, proj, x2d, tgt, wg, g_fin):
    t = x2d.shape[0]
    tm = 256

    def body(ya_ref, yb_ref, ma_ref, mb_ref, x_ref, t_ref, wa_ref, wb_ref, wo_ref, g_ref,
             dx2_ref, dya_ref, dyb_ref, dm_ref, mg_ref, doa_ref, dob_ref, gfin_ref, loss_ref):
        i = pl.program_id(0)

        @pl.when(i == 0)
        def _():
            gfin_ref[...] = jnp.zeros_like(gfin_ref)
            loss_ref[...] = jnp.zeros_like(loss_ref)

        wa = wa_ref[...].reshape(D, D)
        wb = wb_ref[...].reshape(D, D)
        wo = wo_ref[...].reshape(D, D)
        out_a = _dot(ya_ref[...], wa)
        out_b = _dot(yb_ref[...], wb)
        sa = _sigmoid(ma_ref[...].astype(F32))
        sb = _sigmoid(mb_ref[...].astype(F32))
        merged = (sa * out_a + sb * out_b).astype(BF16)
        mg_ref[...] = merged
        x2 = x_ref[...] + _dot(merged, wo)
        r2 = lax.rsqrt(jnp.mean(x2 * x2, axis=-1, keepdims=True) + EPS)
        xh = x2 * r2
        g = g_ref[...]
        err = xh * g - t_ref[...]
        loss_ref[...] += jnp.sum(err * err, axis=0, keepdims=True) * (0.5 / D)
        dy = err * (1.0 / D)
        gfin_ref[...] += jnp.sum(dy * xh, axis=0, keepdims=True)
        dxh = dy * g
        dx2 = r2 * (dxh - xh * jnp.mean(dxh * xh, axis=-1, keepdims=True))
        dx2_ref[...] = dx2
        dmerged = _dot_nt(dx2.astype(BF16), wo)
        doa = (sa * dmerged).astype(BF16)
        dob = (sb * dmerged).astype(BF16)
        doa_ref[...] = doa
        dob_ref[...] = dob
        dm_ref[0] = (dmerged * out_a * sa * (1.0 - sa)).astype(BF16)
        dm_ref[1] = (dmerged * out_b * sb * (1.0 - sb)).astype(BF16)
        dya_ref[...] = _dot_nt(doa, wa)
        dyb_ref[...] = _dot_nt(dob, wb)

    row = lambda: pl.BlockSpec((tm, D), lambda i: (i, 0))
    seg = lambda s: pl.BlockSpec((None, tm, D), lambda i: (s, i, 0))
    vec = pl.BlockSpec((1, D), lambda i: (0, 0))
    return pl.pallas_call(
        body, name="tail", grid=(t // tm,),
        in_specs=[row(), row(), seg(6), seg(7), row(), row(), _wblock(0), _wblock(1), _wblock(2), vec],
        out_specs=[row(), row(), row(), pl.BlockSpec((2, tm, D), lambda i: (0, i, 0)),
                   row(), row(), row(), vec, vec],
        out_shape=[jax.ShapeDtypeStruct((t, D), F32), jax.ShapeDtypeStruct((t, D), F32),
                   jax.ShapeDtypeStruct((t, D), F32), jax.ShapeDtypeStruct((2, t, D), BF16),
                   jax.ShapeDtypeStruct((t, D), BF16), jax.ShapeDtypeStruct((t, D), BF16),
                   jax.ShapeDtypeStruct((t, D), BF16), jax.ShapeDtypeStruct((1, D), F32),
                   jax.ShapeDtypeStruct((1, D), F32)],
        compiler_params=_params(("arbitrary",)),
    )(ya, yb, proj, proj, x2d, tgt, wg, wg, wg, g_fin)


def _tail_wgrad(ya, yb, merged, doa, dob, dx2):
    t = ya.shape[0]
    tm = 512

    def body(ya_ref, yb_ref, mg_ref, doa_ref, dob_ref, dx2_ref, ga_ref, gb_ref, go_ref):
        @pl.when(pl.program_id(0) == 0)
        def _():
            ga_ref[...] = jnp.zeros_like(ga_ref)
            gb_ref[...] = jnp.zeros_like(gb_ref)
            go_ref[...] = jnp.zeros_like(go_ref)

        ga_ref[...] += _dot_tn(ya_ref[...], doa_ref[...])
        gb_ref[...] += _dot_tn(yb_ref[...], dob_ref[...])
        go_ref[...] += _dot_tn(mg_ref[...], dx2_ref[...].astype(BF16))

    row = lambda: pl.BlockSpec((tm, D), lambda i: (i, 0))
    full = lambda: pl.BlockSpec((D, D), lambda i: (0, 0))
    return pl.pallas_call(
        body, name="tail_wgrad", grid=(t // tm,),
        in_specs=[row() for _ in range(6)], out_specs=[full(), full(), full()],
        out_shape=[jax.ShapeDtypeStruct((D, D), F32)] * 3,
        compiler_params=_params(("arbitrary",)),
    )(ya, yb, merged, doa, dob, dx2)


def _dproj_specs(tm, j_of, i_of):
    last = lambda j, i, lo, n: (jnp.clip(j - lo, 0, n - 1), i, 0)
    return [pl.BlockSpec((None, tm, D), lambda a, b: last(j_of(a, b), i_of(a, b), 0, 2)),
            pl.BlockSpec((None, tm, D), lambda a, b: last(j_of(a, b), i_of(a, b), 2, 4)),
            pl.BlockSpec((None, tm, D), lambda a, b: last(j_of(a, b), i_of(a, b), 6, 2))]


def _dproj_specs_ordered(tm):
    def spec(lo, n):
        def index(k, i, order_ref):
            seg = order_ref[k]
            mine = jnp.logical_and(seg >= lo, seg < lo + n)
            return jnp.where(mine, seg - lo, 0), jnp.where(mine, i, 0), 0
        return pl.BlockSpec((None, tm, D), index)
    return [spec(0, 2), spec(2, 4), spec(6, 2)]


def _dproj_pick(j, da_ref, db_ref, dc_ref, use):
    @pl.when(j < 2)
    def _():
        use(da_ref[...])

    @pl.when(jnp.logical_and(j >= 2, j < 6))
    def _():
        use(db_ref[...])

    @pl.when(j >= 6)
    def _():
        use(dc_ref[...])


RS_X, RS_Y, RS_XY = 0, 1, 2
RS_ROLES = ((RS_XY, RS_X, RS_Y), (RS_Y, RS_XY, RS_X))


def _rs_flip(rel, x, y):
    return ((1 - x, y), (x, 1 - y), (1 - x, 1 - y))[rel]


def _rs_order(x, y, c):
    order = []
    for s in range(4):
        chip = []
        for core in (0, 1):
            px, py = _rs_flip(RS_ROLES[core][s], x, y) if s < 3 else (x, y)
            chip.append(2 * px + py)
        keep = jnp.where(c == 0, chip[0], chip[1])
        give = jnp.where(c == 0, chip[1], chip[0])
        order += [2 * give + 1 - c, 2 * keep + c]
    return jnp.stack(order).astype(jnp.int32)


def _inproj_wgrad_rs(h, dpa, dpb, dpc, order, smalls):
    t = h.shape[0]
    tm = 1024
    nt = t // tm
    nsm = len(smalls)

    def body(order_ref, h_ref, da_ref, db_ref, dc_ref, *rest):
        small_refs, parts_ref, rest = rest[:nsm], rest[nsm], rest[nsm + 1:]
        all_refs, rest = rest[:nsm], rest[nsm:]
        (acc, sib, outb, far, give_send, give_recv, sum_send, sum_recv, far_send, far_recv, own_sem,
         small_send, small_recv, small_own) = rest
        k, i = pl.program_id(0), pl.program_id(1)
        x, y, c = _place()
        own, first, arrive, forward, others = _gather_copies(small_refs, all_refs, small_send, small_recv, small_own)

        @pl.when(jnp.logical_and(k == 0, i == 0))
        def _():
            for cp in own + first:
                cp.start()

        @pl.when(jnp.logical_and(k == 2, i == 0))
        def _():
            for came, on in zip(arrive, forward):
                came.wait_recv()
                on.start()

        def use(d):
            @pl.when(i == 0)
            def _():
                acc[k % 2] = _dot_tn(h_ref[...], d)

            @pl.when(i > 0)
            def _():
                acc[k % 2] += _dot_tn(h_ref[...], d)

        _dproj_pick(order_ref[k], da_ref, db_ref, dc_ref, use)

        def give_copy(s):
            return pltpu.make_async_remote_copy(
                src_ref=acc.at[0], dst_ref=sib.at[s % 2], send_sem=give_send.at[s], recv_sem=give_recv.at[s],
                device_id=(x, y, 1 - c), device_id_type=MESH)

        def sum_copy(s, core):
            slot = 0 if s < 2 else 1
            return pltpu.make_async_remote_copy(
                src_ref=outb.at[s], dst_ref=parts_ref.at[slot], send_sem=sum_send.at[slot], recv_sem=sum_recv.at[slot],
                device_id=(*_rs_flip(RS_ROLES[core][s], x, y), core), device_id_type=MESH)

        def far_copy(s, core):
            return pltpu.make_async_remote_copy(
                src_ref=outb.at[s], dst_ref=far, send_sem=far_send, recv_sem=far_recv,
                device_id=(*_rs_flip(RS_X if core == 0 else RS_Y, x, y), core), device_id_type=MESH)

        own_copy = pltpu.make_async_copy(outb.at[3], parts_ref.at[2], own_sem)

        def send_of(core, s):
            return far_copy(s, core) if RS_ROLES[core][s] == RS_XY else sum_copy(s, core)

        for s in range(4):
            @pl.when(jnp.logical_and(k == 2 * s, i == nt - 1))
            def _():
                give_copy(s).start()

            @pl.when(jnp.logical_and(k == 2 * s + 1, i == nt - 1))
            def _():
                give_copy(s).wait_recv()
                if s == 2:
                    far_copy(s, 0).wait_recv()
                    outb[s] = (acc[1] + sib[s % 2] + far[...].astype(F32)).astype(BF16)
                else:
                    outb[s] = (acc[1] + sib[s % 2]).astype(BF16)
                give_copy(s).wait_send()
                if s < 3:
                    for core in (0, 1):
                        @pl.when(c == core)
                        def _():
                            send_of(core, s).start()
                else:
                    own_copy.start()

        @pl.when(jnp.logical_and(k == NSEG - 1, i == nt - 1))
        def _():
            for slot in (0, 1):
                sum_copy(2 * slot, 0).wait_recv()
            for s in range(3):
                send_of(0, s).wait_send()
            own_copy.wait()
            for cp in others:
                cp.wait_recv()
            for cp in first + forward:
                cp.wait_send()
            for cp in own:
                cp.wait()

    return pl.pallas_call(
        body, name="inproj_wgrad_rs",
        grid_spec=pltpu.PrefetchScalarGridSpec(
            num_scalar_prefetch=1, grid=(NSEG, nt),
            in_specs=[pl.BlockSpec((tm, D), lambda k, i, order_ref: (i, 0))] + _dproj_specs_ordered(tm) + [ANY] * nsm,
            out_specs=[ANY] * (1 + nsm),
            scratch_shapes=[pltpu.VMEM((2, D, D), F32), pltpu.VMEM((2, D, D), F32), pltpu.VMEM((4, D, D), BF16),
                            pltpu.VMEM((D, D), BF16),
                            pltpu.SemaphoreType.DMA((4,)), pltpu.SemaphoreType.DMA((4,)),
                            pltpu.SemaphoreType.DMA((2,)), pltpu.SemaphoreType.DMA((2,)),
                            pltpu.SemaphoreType.DMA, pltpu.SemaphoreType.DMA, pltpu.SemaphoreType.DMA,
                            pltpu.SemaphoreType.DMA((nsm, 7)), pltpu.SemaphoreType.DMA((nsm, 7)),
                            pltpu.SemaphoreType.DMA((nsm,))]),
        out_shape=[jax.ShapeDtypeStruct((3, D, D), BF16)]
        + [jax.ShapeDtypeStruct((NDEV,) + a.shape, a.dtype) for a in smalls],
        compiler_params=_params(("arbitrary", "arbitrary")),
    )(order, h, dpa, dpb, dpc, *smalls)


def _inproj_dgrad(dpa, dpb, dpc, wg, x2d, dx2, g_in):
    t = x2d.shape[0]
    tm = 1024

    def body(da_ref, db_ref, dc_ref, w_ref, x_ref, dx2_ref, g_ref, gx_ref, gg_ref, acc_s):
        i, j = pl.program_id(0), pl.program_id(1)

        @pl.when(jnp.logical_and(i == 0, j == 0))
        def _():
            gg_ref[...] = jnp.zeros_like(gg_ref)

        @pl.when(j == 0)
        def _():
            acc_s[...] = jnp.zeros_like(acc_s)

        def use(d):
            acc_s[...] += _dot_nt(d, w_ref[...])

        _dproj_pick(j, da_ref, db_ref, dc_ref, use)

        @pl.when(j == NSEG - 1)
        def _():
            x = x_ref[...]
            r = lax.rsqrt(jnp.mean(x * x, axis=-1, keepdims=True) + EPS)
            xh = x * r
            dh = acc_s[...]
            gg_ref[...] += jnp.sum(dh * xh, axis=0, keepdims=True)
            dxh = dh * g_ref[...]
            gx_ref[...] = dx2_ref[...] + r * (dxh - xh * jnp.mean(dxh * xh, axis=-1, keepdims=True))

    row = lambda: pl.BlockSpec((tm, D), lambda i, j: (i, 0))
    vec = pl.BlockSpec((1, D), lambda i, j: (0, 0))
    return pl.pallas_call(
        body, name="inproj_dgrad", grid=(t // tm, NSEG),
        in_specs=_dproj_specs(tm, lambda i, j: j, lambda i, j: i)
        + [pl.BlockSpec((None, D, D), lambda i, j: (j, 0, 0)), row(), row(), vec],
        out_specs=[row(), vec],
        out_shape=[jax.ShapeDtypeStruct((t, D), F32), jax.ShapeDtypeStruct((1, D), F32)],
        scratch_shapes=[pltpu.VMEM((tm, D), F32)],
        compiler_params=_params(("arbitrary", "arbitrary")),
    )(dpa, dpb, dpc, wg, x2d, dx2, g_in)


def _adam_update(g, w, m, v):
    m_new = ADAM_B1 * m + (1.0 - ADAM_B1) * g
    v_new = ADAM_B2 * v + (1.0 - ADAM_B2) * (g * g)
    m_hat = m_new / (1.0 - ADAM_B1 ** ADAM_STEP)
    v_hat = v_new / (1.0 - ADAM_B2 ** ADAM_STEP)
    return -ADAM_LR * (m_hat / (jnp.sqrt(v_hat) + ADAM_EPS) + ADAM_WD * w), m_new, v_new


def _sum_in_order(ref):
    total = ref[0].astype(F32)
    for k in range(1, ref.shape[0]):
        total = total + ref[k].astype(F32)
    return total


def _adamw_small(me, vec_all, gx_all, ga_all, groups):
    flat = [a for grp in groups for a in grp]
    ng = len(groups)
    nshard = D // NDEV

    def body(me_ref, vec_ref, shard_ref, gx_ref, ga_ref, *refs):
        ins, outs = refs[:3 * ng], refs[3 * ng:]
        vec = _sum_in_order(vec_ref)
        shard = _sum_in_order(shard_ref)
        grads = [vec[r:r + 1, :] for r in range(6)]
        grads += [shard[0:4, :], shard[4:8, 0:DK // NDEV], _sum_in_order(gx_ref), _sum_in_order(ga_ref)]
        for n, g in enumerate(grads):
            delta, m_new, v_new = _adam_update(g, ins[3 * n][...], ins[3 * n + 1][...], ins[3 * n + 2][...])
            outs[4 * n][...] = g
            outs[4 * n + 1][...] = delta
            outs[4 * n + 2][...] = m_new
            outs[4 * n + 3][...] = v_new
        outs[4 * ng][...] = jnp.sum(vec[6:7, :], axis=1, keepdims=True)

    full = lambda a: pl.BlockSpec(a.shape, lambda i, me_ref, nd=len(a.shape): (0,) * nd)
    out_shape = [jax.ShapeDtypeStruct(w.shape, F32) for w, _, _ in groups for _ in range(4)]
    out_shape.append(jax.ShapeDtypeStruct((1, 1), F32))
    outs = pl.pallas_call(
        body, name="adamw_small",
        grid_spec=pltpu.PrefetchScalarGridSpec(
            num_scalar_prefetch=1, grid=(1,),
            in_specs=[full(vec_all),
                      pl.BlockSpec((NDEV, 8, nshard), lambda i, me_ref: (0, 1, me_ref[0])),
                      full(gx_all), full(ga_all)] + [full(a) for a in flat],
            out_specs=[full(s) for s in out_shape]),
        out_shape=out_shape,
        compiler_params=_params(("arbitrary",)),
    )(me, vec_all, vec_all, gx_all, ga_all, *flat)
    return [outs[4 * n:4 * n + 4] for n in range(ng)], outs[4 * ng]


def _adamw(name, items):
    n, rows, cols = items[0][0].shape
    tr = rows if rows <= 256 else 256
    k = len(items)

    def body(*refs):
        for a in range(k):
            p_ref, w_ref, m_ref, v_ref = refs[4 * a:4 * a + 4]
            g = _sum_in_order(p_ref)
            delta, m_new, v_new = _adam_update(g, w_ref[...], m_ref[...], v_ref[...])
            for o, val in zip(refs[4 * k + 4 * a:4 * k + 4 * a + 4], (g, delta, m_new, v_new)):
                o[...] = val

    blk = lambda: pl.BlockSpec((tr, cols), lambda i: (i, 0))
    outs = pl.pallas_call(
        body, name=name, grid=(rows // tr,),
        in_specs=[pl.BlockSpec((n, tr, cols), lambda i: (0, i, 0)), blk(), blk(), blk()] * k,
        out_specs=[blk() for _ in range(4 * k)],
        out_shape=[jax.ShapeDtypeStruct((rows, cols), F32)] * (4 * k),
        compiler_params=_params(("arbitrary",)),
    )(*[a for item in items for a in item])
    return [outs[4 * a:4 * a + 4] for a in range(k)]


ANY = pl.BlockSpec(memory_space=pl.ANY)


def _place():
    return lax.axis_index("x"), lax.axis_index("y"), lax.axis_index("c")


def _gather_copies(ins, outs, send_sems, recv_sems, own_sems):
    x, y, c = _place()
    me, sibling = (x, y, c), (x, y, 1 - c)
    chips = [(1 - x, y), (x, 1 - y), (1 - x, 1 - y)]
    n = len(ins)

    def copy(a, k, block, to, src=None):
        px, py, pc = block
        dst = outs[a].at[4 * px + 2 * py + pc]
        return pltpu.make_async_remote_copy(
            src_ref=dst if src is None else src, dst_ref=dst,
            send_sem=send_sems.at[a, k], recv_sem=recv_sems.at[a, k], device_id=to, device_id_type=MESH)

    own = [pltpu.make_async_copy(ins[a], outs[a].at[4 * x + 2 * y + c], own_sems.at[a]) for a in range(n)]
    first = []
    for a in range(n):
        first.append(copy(a, 0, me, sibling, src=ins[a]))
        first += [copy(a, 1 + j, me, (*chip, c), src=ins[a]) for j, chip in enumerate(chips)]
    arrive = [copy(a, 1 + j, (*chip, c), me) for j, chip in enumerate(chips) for a in range(n)]
    forward = [copy(a, 4 + j, (*chip, c), sibling) for j, chip in enumerate(chips) for a in range(n)]
    rest = [copy(a, 0, sibling, me) for a in range(n)]
    rest += [copy(a, 4 + j, (*chip, 1 - c), me) for a in range(n) for j, chip in enumerate(chips)]
    return own, first, arrive, forward, rest


def _sibling_copies(ins, outs, send_sems, recv_sems):
    x, y, c = _place()
    return [pltpu.make_async_remote_copy(
        src_ref=ins[a].at[2 * q + 1 - c], dst_ref=outs[a].at[q],
        send_sem=send_sems.at[a, q], recv_sem=recv_sems.at[a, q],
        device_id=(x, y, 1 - c), device_id_type=MESH) for a in range(len(ins)) for q in range(4)]


def _chip_copies(ins, outs, send_sems, recv_sems, local_sems):
    x, y, c = _place()
    my_chip = 2 * x + y
    chips = [(1 - x, y), (x, 1 - y), (1 - x, 1 - y)]
    n = len(ins)
    mine = [pltpu.make_async_copy(ins[a].at[my_chip], outs[a].at[my_chip], local_sems.at[a]) for a in range(n)]
    sends = [pltpu.make_async_remote_copy(
        src_ref=ins[a].at[2 * px + py], dst_ref=outs[a].at[my_chip],
        send_sem=send_sems.at[a, j], recv_sem=recv_sems.at[a, j],
        device_id=(px, py, c), device_id_type=MESH) for a in range(n) for j, (px, py) in enumerate(chips)]
    recvs = [pltpu.make_async_remote_copy(
        src_ref=ins[a].at[my_chip], dst_ref=outs[a].at[2 * px + py],
        send_sem=send_sems.at[a, j], recv_sem=recv_sems.at[a, j],
        device_id=(px, py, c), device_id_type=MESH) for a in range(n) for j, (px, py) in enumerate(chips)]
    return mine, sends, recvs


def _chip_sum(owns, gots, core):
    n = len(owns)
    _, rows, cols = owns[0].shape

    def body(core_ref, *refs):
        for a in range(n):
            refs[2 * n + a][...] = (refs[a][...] + refs[n + a][...]).astype(BF16)

    own_spec = pl.BlockSpec((None, rows, cols), lambda q, core_ref: (2 * q + core_ref[0], 0, 0))
    slab = pl.BlockSpec((None, rows, cols), lambda q, core_ref: (q, 0, 0))
    return pl.pallas_call(
        body, name="chip_sum",
        grid_spec=pltpu.PrefetchScalarGridSpec(
            num_scalar_prefetch=1, grid=(4,),
            in_specs=[own_spec] * n + [slab] * n, out_specs=[slab] * n),
        out_shape=[jax.ShapeDtypeStruct((4, rows, cols), BF16)] * n,
        compiler_params=_params(("arbitrary",)),
    )(core, *owns, *gots)


def _block_diag(w):
    w4 = w.reshape(NCB, 4, 64, 64)
    eye = jnp.eye(4, dtype=w.dtype)
    return (w4[:, :, :, None, :] * eye[None, :, None, :, None]).reshape(NCB, CB, CB)


def _block_diag_back(g):
    g5 = g.reshape(NCB, 4, 64, 4, 64)
    return jnp.stack([g5[:, m, :, m, :] for m in range(4)], axis=1).reshape(16, 64, 64)


def kernel(x, norm_in, w_in, conv_w, conv_b, gate_x_w, gate_x_b, gate_a_w, gate_a_b, lru_lambda, gn_gain, w_proj_a, w_proj_b, w_out, norm_final, loss_target, m_norm_in, m_w_in, m_conv_w, m_conv_b, m_gate_x_w, m_gate_x_b, m_gate_a_w, m_gate_a_b, m_lru_lambda, m_gn_gain, m_w_proj_a, m_w_proj_b, m_w_out, m_norm_final, v_norm_in, v_w_in, v_conv_w, v_conv_b, v_gate_x_w, v_gate_x_b, v_gate_a_w, v_gate_a_b, v_lru_lambda, v_gn_gain, v_w_proj_a, v_w_proj_b, v_w_out, v_norm_final):
    xi, yi, ci = _place()
    me = 4 * xi + 2 * yi + ci
    core = ci.astype(jnp.int32).reshape(1)
    nshard = D // NDEV
    nb = x.shape[0]
    t = nb * S
    x2d = x.reshape(t, D)
    tgt2d = loss_target.reshape(t, D)
    g_final = norm_final.reshape(1, D)
    wbd = jnp.concatenate([_block_diag(gate_x_w[0]), _block_diag(gate_a_w[0])], axis=-1).astype(BF16)
    tables = _retention_tables()

    wp_own = jnp.concatenate([w_proj_a[0], w_proj_b[0], w_out[0]], axis=0).astype(BF16)
    tiny = jnp.concatenate([conv_w[0], jnp.pad(gn_gain[0], ((0, 0), (0, nshard - DK // NDEV)))], axis=0)
    proj, h, wg, tiny_g = _inproj_gather(x2d, norm_in, w_in[0].astype(BF16), tiny, *_gather_order(xi, yi, ci))
    conv_w_full = tiny_g[:, 0:4, :].transpose(1, 0, 2).reshape(4, D)
    gain3 = tiny_g[:, 4:8, :DK // NDEV].transpose(1, 0, 2).reshape(HEADS, 1, DK)

    ya, hs, xc, gi, gr = _lru_fwd(proj, conv_w_full, conv_b, wbd, gate_x_b, gate_a_b, lru_lambda, nb)
    yb, qr, kr, o, rs, wpg = _ret_fwd(proj, gain3, tables, nb, wp_own)
    dx2, dya, dyb, dpc, merged, doa, dob, g_fin, loss_vec = _tail(ya, yb, proj, x2d, tgt2d, wpg, g_final)
    g_pa, g_pb, g_out = _tail_wgrad(ya, yb, merged, doa, dob, dx2)

    own = [g.reshape(NDEV, nshard, D) for g in (g_pa, g_pb, g_out)]
    dpa, g_wbd, g_vec, *got = _lru_bwd(proj, hs, xc, gi, gr, dya, conv_w_full, wbd, lru_lambda, nb, own)
    sums = _chip_sum(own, got, core)
    dpb, g_gain, *parts = _ret_bwd(proj, qr, kr, o, rs, dyb, gain3, tables, nb, sums)

    grad_x, g_norm_in = _inproj_dgrad(dpa, dpb, dpc, wg, x2d, dx2, norm_in)
    grad_x = grad_x.reshape(nb, S, D)

    gain_rows = jnp.pad(g_gain.reshape(HEADS, NDEV, DK // NDEV), ((0, 0), (0, 0), (0, nshard - DK // NDEV)))
    vec = jnp.concatenate([g_norm_in, g_vec[0:4], g_fin, loss_vec, jnp.zeros((1, D), F32), g_vec[4:8],
                           gain_rows.reshape(HEADS, D)], axis=0)
    g_gx = _block_diag_back(g_wbd[:, :, :CB]).reshape(D // 2, 128)
    g_ga = _block_diag_back(g_wbd[:, :, CB:]).reshape(D // 2, 128)
    parts_in, vec_all, gx_all, ga_all = _inproj_wgrad_rs(h, dpa, dpb, dpc, _rs_order(xi, yi, ci),
                                                         [vec, g_gx, g_ga])
    gx_all = gx_all.reshape(NDEV, D, 64)
    ga_all = ga_all.reshape(NDEV, D, 64)
    parts = [parts_in] + list(parts)

    res = {}
    (out,) = _adamw("adamw_w_in", [(parts[0], w_in[0], m_w_in[0], v_w_in[0])])
    res["w_in"] = [o[None] for o in out]
    square = [("w_proj_a", w_proj_a, m_w_proj_a, v_w_proj_a), ("w_proj_b", w_proj_b, m_w_proj_b, v_w_proj_b),
              ("w_out", w_out, m_w_out, v_w_out)]
    outs = _adamw("adamw_square", [(parts[1 + k], w[0], m[0], v[0]) for k, (_, w, m, v) in enumerate(square)])
    for (nm, _, _, _), out in zip(square, outs):
        res[nm] = [o[None] for o in out]

    row = lambda a: a.reshape(1, D)
    gate = lambda a: a.reshape(D, 64)
    groups = [("norm_in", norm_in, m_norm_in, v_norm_in, row), ("conv_b", conv_b, m_conv_b, v_conv_b, row),
              ("gate_x_b", gate_x_b, m_gate_x_b, v_gate_x_b, row), ("gate_a_b", gate_a_b, m_gate_a_b, v_gate_a_b, row),
              ("lru_lambda", lru_lambda, m_lru_lambda, v_lru_lambda, row),
              ("norm_final", norm_final, m_norm_final, v_norm_final, row),
              ("conv_w", conv_w, m_conv_w, v_conv_w, lambda a: a[0]), ("gn_gain", gn_gain, m_gn_gain, v_gn_gain, lambda a: a[0]),
              ("gate_x_w", gate_x_w, m_gate_x_w, v_gate_x_w, gate), ("gate_a_w", gate_a_w, m_gate_a_w, v_gate_a_w, gate)]
    small_out, loss = _adamw_small(me.astype(jnp.int32).reshape(1), vec_all, gx_all, ga_all,
                                   [tuple(view(a) for a in (w, m, v)) for _, w, m, v, view in groups])
    for (nm, w, _, _, _), out in zip(groups, small_out):
        res[nm] = [o.reshape(w.shape) for o in out]
    loss = loss.reshape(())

    order = ["norm_in", "w_in", "conv_w", "conv_b", "gate_x_w", "gate_x_b", "gate_a_w", "gate_a_b", "lru_lambda",
             "gn_gain", "w_proj_a", "w_proj_b", "w_out", "norm_final"]
    outs = [loss, grad_x]
    for k in range(4):
        outs += [res[nm][k] for nm in order]
    return tuple(outs)
```

```python
import numpy as np

import jax
import jax.numpy as jnp
from jax import lax
from jax.experimental import pallas as pl
from jax.experimental.pallas import tpu as pltpu

F32 = jnp.float32
BF16 = jnp.bfloat16
MESH = pl.DeviceIdType.MESH

D = 1024
S = 2048
NSEG = 8
NDEV = 8
HEADS = 4
DK = 256
CH = 256
NCH = S // CH
CB = 256
NCB = D // CB
RC = 512
RC_CONV = 128
SCAN_GROUP = 8
EPS = 1e-6
LRU_C = 8.0
VMEM_LIMIT = 56 * 1024 * 1024

ADAM_LR = 0.001
ADAM_B1 = 0.9
ADAM_B2 = 0.999
ADAM_EPS = 1e-08
ADAM_WD = 0.01
ADAM_STEP = 10


def _params(sem=None):
    return pltpu.CompilerParams(dimension_semantics=sem, vmem_limit_bytes=VMEM_LIMIT)


def _dot(a, b):
    return jnp.dot(a, b, preferred_element_type=F32)


def _dot_nt(a, b):
    return lax.dot_general(a, b, (((1,), (1,)), ((), ())), preferred_element_type=F32)


def _dot_tn(a, b):
    return lax.dot_general(a, b, (((0,), (0,)), ((), ())), preferred_element_type=F32)


def _sigmoid(x):
    return jax.nn.sigmoid(x)


def _expm1_nonpos(x):
    poly = x * (1.0 + x * (0.5 + x * (1.0 / 6.0 + x * (1.0 / 24.0))))
    return jnp.where(x > -0.05, poly, jnp.exp(x) - 1.0)


def _softplus(x):
    return jnp.maximum(x, 0.0) + jnp.log(1.0 + jnp.exp(-jnp.abs(x)))


def _rows(c, n):
    return pl.ds(pl.multiple_of(c * n, n), n)


def _window_before(ref, c, n):
    r0 = c * n
    if ref.dtype == BF16:
        prev = ref[pl.ds(pl.multiple_of(jnp.maximum(r0 - 16, 0), 16), 16), :].astype(F32)[8:, :]
    else:
        prev = ref[pl.ds(pl.multiple_of(jnp.maximum(r0 - 8, 0), 8), 8), :]
    prev = jnp.where(c > 0, prev, 0.0)
    return jnp.concatenate([prev, ref[_rows(c, n), :].astype(F32)], axis=0)


def _shift_down(win, s, n):
    if s == 0:
        return win[8:, :]
    return pltpu.roll(win, s, 0)[8:, :]


def _shift_up(win, s, n):
    if s == 0:
        return win[:n, :]
    return pltpu.roll(win, n + 8 - s, 0)[:n, :]


HALF = D // 2
GATHER_SLOTS = [("own", None, 0), ("own", None, 1), ("sib", None, 0), ("sib", None, 1)]
for _j, _h in ((0, 0), (1, 0), (0, 1), (1, 1), (2, 0), (2, 1)):
    GATHER_SLOTS += [("ici", _j, _h), ("fwd", _j, _h)]
NSLOT = len(GATHER_SLOTS)


def _gather_order(x, y, c):
    chips = [(1 - x, y), (x, 1 - y), (1 - x, 1 - y)]
    segs, halves = [], []
    for kind, j, h in GATHER_SLOTS:
        if kind == "own":
            seg = 4 * x + 2 * y + c
        elif kind == "sib":
            seg = 4 * x + 2 * y + 1 - c
        else:
            px, py = chips[j]
            seg = 4 * px + 2 * py + (c if kind == "ici" else 1 - c)
        segs.append(seg)
        halves.append(h)
    return jnp.stack(segs).astype(jnp.int32), jnp.asarray(halves, jnp.int32)


def _inproj_gather(x2d, g_in, w_own, tiny_own, order, halves):
    t = x2d.shape[0]
    tm = 1024
    nt = t // tm

    def body(order_ref, half_ref, x_ref, g_ref, w_own_ref, tiny_own_ref,
             proj_ref, h_ref, wg_ref, tinyg_ref,
             w_all, h_all, send_sems, recv_sems, own_sems, out_sems, tiny_send, tiny_recv, tiny_own_sem):
        k, i = pl.program_id(0), pl.program_id(1)
        x, y, c = _place()
        me, sibling = (x, y, c), (x, y, 1 - c)
        mine = 4 * x + 2 * y + c
        chips = [(1 - x, y), (x, 1 - y), (1 - x, 1 - y)]

        def copy(h, n, block, to, own_src=False):
            px, py, pc = block
            dst = w_all.at[4 * px + 2 * py + pc, h]
            return pltpu.make_async_remote_copy(
                src_ref=w_own_ref.at[:, pl.ds(h * HALF, HALF)] if own_src else dst, dst_ref=dst,
                send_sem=send_sems.at[h, n], recv_sem=recv_sems.at[h, n], device_id=to, device_id_type=MESH)

        def tiny_copy(n, block, to, own_src=False):
            px, py, pc = block
            dst = tinyg_ref.at[4 * px + 2 * py + pc]
            return pltpu.make_async_remote_copy(
                src_ref=tiny_own_ref if own_src else dst, dst_ref=dst,
                send_sem=tiny_send.at[n], recv_sem=tiny_recv.at[n], device_id=to, device_id_type=MESH)

        def own_copy(h):
            return pltpu.make_async_copy(w_own_ref.at[:, pl.ds(h * HALF, HALF)], w_all.at[mine, h], own_sems.at[h])

        tiny_mine = pltpu.make_async_copy(tiny_own_ref, tinyg_ref.at[mine], tiny_own_sem)

        def keep_copy(n):
            h = GATHER_SLOTS[n][2]
            return pltpu.make_async_copy(w_all.at[order_ref[n], h], wg_ref.at[order_ref[n], :, pl.ds(h * HALF, HALF)],
                                         out_sems.at[n])

        near = [(0, sibling), (1, (*chips[0], c)), (2, (*chips[1], c))]
        first = [copy(h, n, me, to, True) for h in (0, 1) for n, to in near]
        tiny_first = [tiny_copy(0, me, sibling, True)] + [tiny_copy(1 + j, me, (*chip, c), True) for j, chip in enumerate(chips)]

        def relay(h, j):
            seg = w_all.at[4 * chips[j][0] + 2 * chips[j][1] + c, h]
            return pltpu.make_async_remote_copy(
                src_ref=seg, dst_ref=seg, send_sem=send_sems.at[h, 3], recv_sem=recv_sems.at[h, 3],
                device_id=(*chips[1 - j], c), device_id_type=MESH)

        for n, (kind, j, h) in enumerate(GATHER_SLOTS):
            @pl.when(jnp.logical_and(k == n, i == 0))
            def _():
                if n == 0:
                    own_copy(0).start()
                    own_copy(1).start()
                    tiny_mine.start()
                    for cp in first + tiny_first:
                        cp.start()
                if kind == "own":
                    own_copy(h).wait()
                elif kind == "sib":
                    copy(h, 0, sibling, me).wait_recv()
                elif kind == "ici":
                    copy(h, 1 + j, (*chips[j], c), me).wait_recv()
                    copy(h, 4 + j, (*chips[j], c), sibling).start()
                    if j < 2:
                        @pl.when(c == j)
                        def _():
                            relay(h, j).start()
                else:
                    copy(h, 4 + j, (*chips[j], 1 - c), me).wait_recv()
                keep_copy(n).start()

        rows = pl.ds(pl.multiple_of(i * tm, tm), tm)

        @pl.when(k == 0)
        def _():
            xv = x_ref[...]
            r = lax.rsqrt(jnp.mean(xv * xv, axis=-1, keepdims=True) + EPS)
            hv = (xv * r * g_ref[...]).astype(BF16)
            h_ref[...] = hv
            h_all[rows, :] = hv

        proj_ref[...] = _dot(h_all[rows, :], w_all[order_ref[k], half_ref[k]]).astype(BF16)

        @pl.when(jnp.logical_and(k == NSLOT - 1, i == nt - 1))
        def _():
            for j, chip in enumerate(chips):
                tiny_copy(1 + j, (*chip, c), me).wait_recv()
                tiny_copy(4 + j, (*chip, c), sibling).start()
            tiny_copy(0, sibling, me).wait_recv()
            for j, chip in enumerate(chips):
                tiny_copy(4 + j, (*chip, 1 - c), me).wait_recv()
            for cp in first + tiny_first:
                cp.wait_send()
            for j, chip in enumerate(chips):
                tiny_copy(4 + j, (*chip, c), sibling).wait_send()
                for h in (0, 1):
                    copy(h, 4 + j, (*chip, c), sibling).wait_send()
            for h in (0, 1):
                relay(h, 0).wait_send()
            tiny_mine.wait()
            for n in range(NSLOT):
                keep_copy(n).wait()

    hold = lambda k, i, order_ref, half_ref: (jnp.where(k == 0, i, nt - 1), 0)
    return pl.pallas_call(
        body, name="inproj_gather",
        grid_spec=pltpu.PrefetchScalarGridSpec(
            num_scalar_prefetch=2, grid=(NSLOT, nt),
            in_specs=[pl.BlockSpec((tm, D), hold),
                      pl.BlockSpec((1, D), lambda k, i, order_ref, half_ref: (0, 0)),
                      ANY, ANY],
            out_specs=[pl.BlockSpec((None, tm, HALF), lambda k, i, order_ref, half_ref: (order_ref[k], i, half_ref[k])),
                       pl.BlockSpec((tm, D), hold),
                       ANY, ANY],
            scratch_shapes=[pltpu.VMEM((NDEV, 2, D, HALF), BF16), pltpu.VMEM((t, D), BF16),
                            pltpu.SemaphoreType.DMA((2, 7)), pltpu.SemaphoreType.DMA((2, 7)),
                            pltpu.SemaphoreType.DMA((2,)), pltpu.SemaphoreType.DMA((NSLOT,)),
                            pltpu.SemaphoreType.DMA((7,)), pltpu.SemaphoreType.DMA((7,)), pltpu.SemaphoreType.DMA]),
        out_shape=[jax.ShapeDtypeStruct((NSEG, t, D), BF16), jax.ShapeDtypeStruct((t, D), BF16),
                   jax.ShapeDtypeStruct((NDEV,) + w_own.shape, BF16),
                   jax.ShapeDtypeStruct((NDEV,) + tiny_own.shape, F32)],
        compiler_params=_params(("arbitrary", "arbitrary")),
    )(order, halves, x2d, g_in, w_own, tiny_own)


def _tile_scan(a, u):
    row = lax.broadcasted_iota(jnp.int32, a.shape, 0)
    for d in (1, 2, 4):
        m = row >= d
        a_sh = pltpu.roll(a, d, 0)
        u_sh = pltpu.roll(u, d, 0)
        u = jnp.where(m, a * u_sh + u, u)
        a = jnp.where(m, a * a_sh, a)
    return a, u


def _tile_scan_rev(a, w):
    row = lax.broadcasted_iota(jnp.int32, a.shape, 0)
    for d in (1, 2, 4):
        m = row < 8 - d
        a_sh = pltpu.roll(a, 8 - d, 0)
        w_sh = pltpu.roll(w, 8 - d, 0)
        w = jnp.where(m, a * w_sh + w, w)
        a = jnp.where(m, a * a_sh, a)
    return a, w


def _lru_gates(xa_ref, c, cw_ref, cb_ref, wbd_ref, bx_ref, ba_ref, sp):
    win = _window_before(xa_ref, c, RC)
    xc = cb_ref[...] + cw_ref[3:4, :] * _shift_down(win, 0, RC)
    for s in (1, 2, 3):
        xc = xc + cw_ref[3 - s:4 - s, :] * _shift_down(win, s, RC)
    z = _dot(xc.astype(BF16), wbd_ref[...])
    gi = _sigmoid(z[:, :CB] + bx_ref[...])
    gr = _sigmoid(z[:, CB:] + ba_ref[...])
    log_a = -LRU_C * gr * sp
    return win, xc, gi, gr, log_a


def _lru_fwd(proj, conv_w, conv_b, wbd, bx, ba, lam, nb):
    t = nb * S

    def body(xa_ref, ga_ref, cw_ref, cb_ref, wbd_ref, bx_ref, ba_ref, lam_ref,
             ya_ref, hs_ref, xc_ref, gi_ref, gr_ref, a_s, u_s):
        sp = _softplus(-lam_ref[...])

        def gates(c, carry):
            _, xc, gi, gr, log_a = _lru_gates(xa_ref, c, cw_ref, cb_ref, wbd_ref, bx_ref, ba_ref, sp)
            rows = _rows(c, RC)
            a_s[rows, :] = jnp.exp(log_a)
            u_s[rows, :] = jnp.sqrt(-_expm1_nonpos(2.0 * log_a)) * (gi * xc)
            xc_ref[rows, :] = xc
            gi_ref[rows, :] = gi
            gr_ref[rows, :] = gr
            return carry

        lax.fori_loop(0, S // RC, gates, 0)

        def scan(g, h):
            for k in range(SCAN_GROUP):
                rows = pl.ds(pl.multiple_of(g * (8 * SCAN_GROUP), 8 * SCAN_GROUP) + 8 * k, 8)
                a_cum, u_cum = _tile_scan(a_s[rows, :], u_s[rows, :])
                hs_ref[rows, :] = u_cum + a_cum * h
                h = u_cum[7:8, :] + a_cum[7:8, :] * h
            return h

        lax.fori_loop(0, S // (8 * SCAN_GROUP), scan, jnp.zeros((1, CB), F32))

        def gate_out(c, carry):
            ga = ga_ref[_rows(c, RC), :].astype(F32)
            ya_ref[_rows(c, RC), :] = (ga * _sigmoid(ga) * hs_ref[_rows(c, RC), :]).astype(BF16)
            return carry

        lax.fori_loop(0, S // RC, gate_out, 0)

    vec = pl.BlockSpec((1, CB), lambda b, cb: (0, cb))
    blk = pl.BlockSpec((S, CB), lambda b, cb: (b, cb))
    return pl.pallas_call(
        body, name="lru_fwd", grid=(nb, NCB),
        in_specs=[pl.BlockSpec((None, S, CB), lambda b, cb: (0, b, cb)),
                  pl.BlockSpec((None, S, CB), lambda b, cb: (1, b, cb)),
                  pl.BlockSpec((4, CB), lambda b, cb: (0, cb)),
                  vec,
                  pl.BlockSpec((None, CB, 2 * CB), lambda b, cb: (cb, 0, 0)),
                  vec, vec, vec],
        out_specs=[blk] + [pl.BlockSpec((None, None, S, CB), lambda b, cb: (b, cb, 0, 0))] * 4,
        out_shape=[jax.ShapeDtypeStruct((t, D), BF16)] + [jax.ShapeDtypeStruct((nb, NCB, S, CB), F32)] * 4,
        scratch_shapes=[pltpu.VMEM((S, CB), F32), pltpu.VMEM((S, CB), F32)],
        compiler_params=_params(("arbitrary", "arbitrary")),
    )(proj, proj, conv_w, conv_b, wbd, bx, ba, lam)


def _lru_bwd(proj, hs, xc_f, gi_f, gr_f, dya, conv_w, wbd, lam, nb, give):
    t = nb * S
    ng = len(give)

    def body(xa_ref, ga_ref, hs_ref, xc_s, gi_s, gr_s, dya_ref, cw_ref, wbd_ref, lam_ref, *rest):
        give_refs, rest = rest[:ng], rest[ng:]
        dp_ref, dwbd_ref, vec_ref = rest[:3]
        got_refs, rest = rest[3:3 + ng], rest[3 + ng:]
        a_s, dl_s, dh_s, dxc_s, acc_s, send_sems, recv_sems = rest
        b = pl.program_id(1)
        exchange = _sibling_copies(give_refs, got_refs, send_sems, recv_sems)

        @pl.when(jnp.logical_and(pl.program_id(0) == 0, b == 0))
        def _():
            for cp in exchange:
                cp.start()

        lam_v = lam_ref[...]
        sp = _softplus(-lam_v)
        acc_s[...] = jnp.zeros_like(acc_s)

        @pl.when(b == 0)
        def _():
            dwbd_ref[...] = jnp.zeros_like(dwbd_ref)
            vec_ref[...] = jnp.zeros_like(vec_ref)

        def gates(c, carry):
            rows = _rows(c, RC)
            a_s[rows, :] = jnp.exp(-LRU_C * gr_s[rows, :] * sp)
            ga = ga_ref[rows, :].astype(F32)
            sg = _sigmoid(ga)
            dya_c = dya_ref[rows, :]
            dl_s[rows, :] = dya_c * (ga * sg)
            dp_ref[1, rows, :] = (dya_c * hs_ref[rows, :] * (sg * (1.0 + ga * (1.0 - sg)))).astype(BF16)
            return carry

        lax.fori_loop(0, S // RC, gates, 0)

        def scan(i, g_in):
            base = pl.multiple_of((S // (8 * SCAN_GROUP) - 1 - i) * (8 * SCAN_GROUP), 8 * SCAN_GROUP)
            row = lax.broadcasted_iota(jnp.int32, (8, CB), 0)
            for k in reversed(range(SCAN_GROUP)):
                rows = pl.ds(base + 8 * k, 8)
                a = a_s[rows, :]
                dl = dl_s[rows, :]
                a_cum, g_loc = _tile_scan_rev(a, a * dl)
                g = g_loc + a_cum * g_in
                dh_s[rows, :] = dl + jnp.where(row < 7, pltpu.roll(g, 7, 0), g_in)
                g_in = g_loc[0:1, :] + a_cum[0:1, :] * g_in
            return g_in

        lax.fori_loop(0, S // (8 * SCAN_GROUP), scan, jnp.zeros((1, CB), F32))

        dxc_s[pl.ds(S, 8), :] = jnp.zeros((8, CB), F32)

        def grads(c, carry):
            rows = _rows(c, RC)
            dh = dh_s[rows, :]
            h_prev = _shift_down(_window_before(hs_ref, c, RC), 1, RC)
            xc, gi, gr, a = xc_s[rows, :], gi_s[rows, :], gr_s[rows, :], a_s[rows, :]
            mult = jnp.sqrt(-_expm1_nonpos(-2.0 * LRU_C * gr * sp))
            dmult = dh * gi * xc
            d_log_a = dh * h_prev * a - dmult * (a * a) / mult
            dzi = dh * mult * xc * gi * (1.0 - gi)
            dzr = d_log_a * (-LRU_C * sp) * gr * (1.0 - gr)
            dz = jnp.concatenate([dzi, dzr], axis=1).astype(BF16)
            dxc_s[rows, :] = dh * mult * gi + _dot_nt(dz, wbd_ref[...])
            dwbd_ref[...] += _dot_tn(xc.astype(BF16), dz)
            acc_s[1:2, :] += jnp.sum(dzi, axis=0, keepdims=True)
            acc_s[2:3, :] += jnp.sum(dzr, axis=0, keepdims=True)
            acc_s[3:4, :] += jnp.sum(d_log_a * (-LRU_C * gr), axis=0, keepdims=True)
            return carry

        lax.fori_loop(0, S // RC, grads, 0)

        def conv_bwd(c, carry):
            rows = _rows(c, RC_CONV)
            dwin = dxc_s[pl.ds(pl.multiple_of(c * RC_CONV, RC_CONV), RC_CONV + 8), :]
            dxc = dwin[:RC_CONV, :]
            xwin = _window_before(xa_ref, c, RC_CONV)
            dxa = cw_ref[3:4, :] * dxc
            acc_s[0:1, :] += jnp.sum(dxc, axis=0, keepdims=True)
            acc_s[7:8, :] += jnp.sum(dxc * _shift_down(xwin, 0, RC_CONV), axis=0, keepdims=True)
            for s in (1, 2, 3):
                dxa = dxa + cw_ref[3 - s:4 - s, :] * _shift_up(dwin, s, RC_CONV)
                acc_s[7 - s:8 - s, :] += jnp.sum(dxc * _shift_down(xwin, s, RC_CONV), axis=0, keepdims=True)
            dp_ref[0, rows, :] = dxa.astype(BF16)
            return carry

        lax.fori_loop(0, S // RC_CONV, conv_bwd, 0)

        row = lax.broadcasted_iota(jnp.int32, acc_s.shape, 0)
        vec_ref[...] += jnp.where(row == 3, acc_s[...] * (-_sigmoid(-lam_v)), acc_s[...])

        @pl.when(jnp.logical_and(pl.program_id(0) == NCB - 1, b == nb - 1))
        def _():
            for cp in exchange:
                cp.wait()

    vec = pl.BlockSpec((1, CB), lambda cb, b: (0, cb))
    blk = pl.BlockSpec((S, CB), lambda cb, b: (b, cb))
    own = pl.BlockSpec((None, None, S, CB), lambda cb, b: (b, cb, 0, 0))
    return pl.pallas_call(
        body, name="lru_bwd", grid=(NCB, nb),
        in_specs=[pl.BlockSpec((None, S, CB), lambda cb, b: (0, b, cb)),
                  pl.BlockSpec((None, S, CB), lambda cb, b: (1, b, cb)),
                  own, own, own, own, blk,
                  pl.BlockSpec((4, CB), lambda cb, b: (0, cb)),
                  pl.BlockSpec((None, CB, 2 * CB), lambda cb, b: (cb, 0, 0)),
                  vec] + [ANY] * ng,
        out_specs=[pl.BlockSpec((2, S, CB), lambda cb, b: (0, b, cb)),
                   pl.BlockSpec((None, CB, 2 * CB), lambda cb, b: (cb, 0, 0)),
                   pl.BlockSpec((8, CB), lambda cb, b: (0, cb))] + [ANY] * ng,
        out_shape=[jax.ShapeDtypeStruct((2, t, D), BF16),
                   jax.ShapeDtypeStruct((NCB, CB, 2 * CB), F32),
                   jax.ShapeDtypeStruct((8, D), F32)]
        + [jax.ShapeDtypeStruct((4,) + g.shape[1:], g.dtype) for g in give],
        scratch_shapes=[pltpu.VMEM((S, CB), F32), pltpu.VMEM((S, CB), F32), pltpu.VMEM((S, CB), F32),
                        pltpu.VMEM((S + 8, CB), F32), pltpu.VMEM((8, CB), F32),
                        pltpu.SemaphoreType.DMA((ng, 4)), pltpu.SemaphoreType.DMA((ng, 4))],
        compiler_params=_params(("arbitrary", "arbitrary")),
    )(proj, proj, hs, xc_f, gi_f, gr_f, dya, conv_w, wbd, lam, *give)


def _retention_tables():
    f32 = np.float32
    log_g = np.log1p(-(f32(2.0) ** (f32(-5.0) - np.arange(HEADS, dtype=f32)))).astype(f32)
    idx = np.arange(CH, dtype=f32)
    diff = idx[:, None] - idx[None, :]
    inner = np.where(diff >= 0, np.exp(np.maximum(diff, f32(0.0))[None] * log_g[:, None, None]), f32(0.0)).astype(f32)
    cross = np.exp((idx[None, :] + f32(1.0)) * log_g[:, None]).astype(f32)
    state = np.exp((f32(CH - 1.0) - idx[None, :]) * log_g[:, None]).astype(f32)
    cross = np.ascontiguousarray(np.broadcast_to(cross[:, :, None], (HEADS, CH, DK)))
    state = np.ascontiguousarray(np.broadcast_to(state[:, :, None], (HEADS, CH, DK)))
    half = DK // 2
    freqs = (f32(10000.0) ** (-np.arange(half, dtype=f32) / f32(half))).astype(f32)
    ang = (np.arange(S, dtype=f32)[:, None] * freqs[None, :]).astype(f32)
    return tuple(jnp.asarray(a) for a in (inner, cross, state, np.cos(ang).astype(f32), np.sin(ang).astype(f32)))


def _rotate(x, cos, sin):
    half = DK // 2
    x1, x2 = x[:, :half], x[:, half:]
    return jnp.concatenate([x1 * cos - x2 * sin, x1 * sin + x2 * cos], axis=1)


def _rotate_back(d, cos, sin):
    half = DK // 2
    d1, d2 = d[:, :half], d[:, half:]
    return jnp.concatenate([d1 * cos + d2 * sin, d2 * cos - d1 * sin], axis=1)


def _ret_fwd(proj, gain, tables, nb, wp_own):
    t = nb * S
    inner_t, cross_t, state_t, cos_t, sin_t = tables

    def body(q_ref, k_ref, v_ref, gb_ref, gain_ref, dm_ref, cd_ref, sd_ref, cos_ref, sin_ref, wp_ref,
             yb_ref, qr_ref, kr_ref, o_ref, rs_ref, wpg_ref, r_s, send_sems, recv_sems, own_sems):
        b, hd = pl.program_id(0), pl.program_id(1)
        own, first, arrive, forward, others = _gather_copies([wp_ref], [wpg_ref], send_sems, recv_sems, own_sems)

        @pl.when(jnp.logical_and(b == 0, hd == 0))
        def _():
            for cp in own + first:
                cp.start()

        @pl.when(jnp.logical_and(b == nb - 1, hd == HEADS - 1))
        def _():
            for came, on in zip(arrive, forward):
                came.wait_recv()
                on.start()

        r_s[...] = jnp.zeros_like(r_s)
        chunk_decay = cd_ref[CH - 1:CH, :]

        def chunk(c, carry):
            rows = _rows(c, CH)
            cos, sin = cos_ref[rows, :], sin_ref[rows, :]
            qr = _rotate(q_ref[rows, :].astype(F32), cos, sin).astype(BF16)
            kr = (_rotate(k_ref[rows, :].astype(F32), cos, sin) * (DK ** -0.5)).astype(BF16)
            vb = v_ref[rows, :]
            v = vb.astype(F32)
            qr_ref[rows, :] = qr
            kr_ref[rows, :] = kr
            r = r_s[...]
            rb = r.astype(BF16)
            rs_ref[c] = rb
            p = (_dot_nt(qr, kr) * dm_ref[...]).astype(BF16)
            o = _dot(p, vb) + _dot(qr, rb) * cd_ref[...]
            r_s[...] = chunk_decay * r + _dot_tn(kr, (v * sd_ref[...]).astype(BF16))
            o_ref[rows, :] = o
            oc = o - jnp.mean(o, axis=-1, keepdims=True)
            rstd = lax.rsqrt(jnp.mean(oc * oc, axis=-1, keepdims=True) + EPS)
            gb = gb_ref[rows, :].astype(F32)
            yb_ref[rows, :] = (gb * _sigmoid(gb) * (oc * rstd * gain_ref[...])).astype(BF16)
            return carry

        lax.fori_loop(0, NCH, chunk, 0, unroll=2)

        @pl.when(jnp.logical_and(b == nb - 1, hd == HEADS - 1))
        def _():
            for cp in others:
                cp.wait_recv()
            for cp in first + forward:
                cp.wait_send()
            for cp in own:
                cp.wait()

    seg = lambda s: pl.BlockSpec((None, S, DK), lambda b, h: (s, b, h))
    tab = pl.BlockSpec((None, CH, DK), lambda b, h: (h, 0, 0))
    rot = pl.BlockSpec((S, DK // 2), lambda b, h: (0, 0))
    blk = pl.BlockSpec((S, DK), lambda b, h: (b, h))
    return pl.pallas_call(
        body, name="ret_fwd", grid=(nb, HEADS),
        in_specs=[seg(2), seg(3), seg(4), seg(5),
                  pl.BlockSpec((None, 1, DK), lambda b, h: (h, 0, 0)),
                  tab, tab, tab, rot, rot, ANY],
        out_specs=[blk, blk, blk, blk,
                   pl.BlockSpec((None, None, NCH, DK, DK), lambda b, h: (b, h, 0, 0, 0)), ANY],
        out_shape=[jax.ShapeDtypeStruct((t, D), BF16), jax.ShapeDtypeStruct((t, D), BF16),
                   jax.ShapeDtypeStruct((t, D), BF16), jax.ShapeDtypeStruct((t, D), F32),
                   jax.ShapeDtypeStruct((nb, HEADS, NCH, DK, DK), BF16),
                   jax.ShapeDtypeStruct((NDEV,) + wp_own.shape, wp_own.dtype)],
        scratch_shapes=[pltpu.VMEM((DK, DK), F32),
                        pltpu.SemaphoreType.DMA((1, 7)), pltpu.SemaphoreType.DMA((1, 7)), pltpu.SemaphoreType.DMA((1,))],
        compiler_params=_params(("arbitrary", "arbitrary")),
    )(proj, proj, proj, proj, gain, inner_t, cross_t, state_t, cos_t, sin_t, wp_own)


def _ret_bwd(proj, qr, kr, o, rs, dyb, gain, tables, nb, sums):
    t = nb * S
    ns = len(sums)
    inner_t, cross_t, state_t, cos_t, sin_t = tables

    def body(qr_ref, kr_ref, v_ref, gb_ref, o_ref, dyb_ref, rs_ref, gain_ref, dm_ref, cd_ref, sd_ref,
             cos_ref, sin_ref, *rest):
        sum_refs, rest = rest[:ns], rest[ns:]
        dp_ref, dgain_ref = rest[:2]
        part_refs, rest = rest[2:2 + ns], rest[2 + ns:]
        dr_s, send_sems, recv_sems, local_sems = rest
        mine, sends, recvs = _chip_copies(sum_refs, part_refs, send_sems, recv_sems, local_sems)

        @pl.when(jnp.logical_and(pl.program_id(0) == 0, pl.program_id(1) == 0))
        def _():
            for cp in mine + sends:
                cp.start()

        dr_s[...] = jnp.zeros_like(dr_s)
        chunk_decay = cd_ref[CH - 1:CH, :]

        @pl.when(pl.program_id(1) == 0)
        def _():
            dgain_ref[...] = jnp.zeros_like(dgain_ref)

        def chunk(i, carry):
            c = NCH - 1 - i
            rows = _rows(c, CH)
            gain_v = gain_ref[...]
            o_c = o_ref[rows, :]
            oc = o_c - jnp.mean(o_c, axis=-1, keepdims=True)
            rstd = lax.rsqrt(jnp.mean(oc * oc, axis=-1, keepdims=True) + EPS)
            yn = oc * rstd
            gb = gb_ref[rows, :].astype(F32)
            sg = _sigmoid(gb)
            dyb_c = dyb_ref[rows, :]
            dgn = dyb_c * (gb * sg)
            dp_ref[3, rows, :] = (dyb_c * (yn * gain_v) * (sg * (1.0 + gb * (1.0 - sg)))).astype(BF16)
            dgain_ref[...] += jnp.sum(dgn * yn, axis=0, keepdims=True)
            dyn = dgn * gain_v
            do = rstd * (dyn - jnp.mean(dyn, axis=-1, keepdims=True)
                         - yn * jnp.mean(dyn * yn, axis=-1, keepdims=True))
            dob = do.astype(BF16)
            dox = (do * cd_ref[...]).astype(BF16)

            q_c, k_c = qr_ref[rows, :], kr_ref[rows, :]
            vb = v_ref[rows, :]
            v = vb.astype(F32)
            vs = (v * sd_ref[...]).astype(BF16)
            rb = rs_ref[c]
            d_r = dr_s[...]
            drb = d_r.astype(BF16)
            dm = dm_ref[...]
            p = (_dot_nt(q_c, k_c) * dm).astype(BF16)
            dpm = (_dot_nt(dob, vb) * dm).astype(BF16)
            dq = _dot(dpm, k_c) + _dot_nt(dox, rb)
            dk = _dot_tn(dpm, q_c) + _dot_nt(vs, drb)
            dv = _dot_tn(p, dob) + _dot(k_c, drb) * sd_ref[...]
            dr_s[...] = chunk_decay * d_r + _dot_tn(q_c, dox)

            cos, sin = cos_ref[rows, :], sin_ref[rows, :]
            dp_ref[0, rows, :] = _rotate_back(dq, cos, sin).astype(BF16)
            dp_ref[1, rows, :] = (_rotate_back(dk, cos, sin) * (DK ** -0.5)).astype(BF16)
            dp_ref[2, rows, :] = dv.astype(BF16)
            return carry

        lax.fori_loop(0, NCH, chunk, 0, unroll=2)

        @pl.when(jnp.logical_and(pl.program_id(0) == HEADS - 1, pl.program_id(1) == nb - 1))
        def _():
            for cp in recvs:
                cp.wait_recv()
            for cp in sends:
                cp.wait_send()
            for cp in mine:
                cp.wait()

    seg = lambda s: pl.BlockSpec((None, S, DK), lambda h, b: (s, b, h))
    tab = pl.BlockSpec((None, CH, DK), lambda h, b: (h, 0, 0))
    rot = pl.BlockSpec((S, DK // 2), lambda h, b: (0, 0))
    blk = pl.BlockSpec((S, DK), lambda h, b: (b, h))
    one = pl.BlockSpec((None, 1, DK), lambda h, b: (h, 0, 0))
    return pl.pallas_call(
        body, name="ret_bwd", grid=(HEADS, nb),
        in_specs=[blk, blk, seg(4), seg(5), blk, blk,
                  pl.BlockSpec((None, None, NCH, DK, DK), lambda h, b: (b, h, 0, 0, 0)),
                  one, tab, tab, tab, rot, rot] + [ANY] * ns,
        out_specs=[pl.BlockSpec((4, S, DK), lambda h, b: (0, b, h)), one] + [ANY] * ns,
        out_shape=[jax.ShapeDtypeStruct((4, t, D), BF16), jax.ShapeDtypeStruct((HEADS, 1, DK), F32)]
        + [jax.ShapeDtypeStruct(a.shape, a.dtype) for a in sums],
        scratch_shapes=[pltpu.VMEM((DK, DK), F32), pltpu.SemaphoreType.DMA((ns, 3)), pltpu.SemaphoreType.DMA((ns, 3)),
                        pltpu.SemaphoreType.DMA((ns,))],
        compiler_params=_params(("arbitrary", "arbitrary")),
    )(qr, kr, proj, proj, o, dyb, rs, gain, inner_t, cross_t, state_t, cos_t, sin_t, *sums)


def _wblock(k):
    return pl.BlockSpec((NDEV, D // NDEV, D), lambda i: (0, k, 0))


def _tail(ya, yb, proj, x2d, tgt, wg, g_fin):
    t = x2d.shape[0]
    tm = 256

    def body(ya_ref, yb_ref, ma_ref, mb_ref, x_ref, t_ref, wa_ref, wb_ref, wo_ref, g_ref,
             dx2_ref, dya_ref, dyb_ref, dm_ref, mg_ref, doa_ref, dob_ref, gfin_ref, loss_ref):
        i = pl.program_id(0)

        @pl.when(i == 0)
        def _():
            gfin_ref[...] = jnp.zeros_like(gfin_ref)
            loss_ref[...] = jnp.zeros_like(loss_ref)

        wa = wa_ref[...].reshape(D, D)
        wb = wb_ref[...].reshape(D, D)
        wo = wo_ref[...].reshape(D, D)
        out_a = _dot(ya_ref[...], wa)
        out_b = _dot(yb_ref[...], wb)
        sa = _sigmoid(ma_ref[...].astype(F32))
        sb = _sigmoid(mb_ref[...].astype(F32))
        merged = (sa * out_a + sb * out_b).astype(BF16)
        mg_ref[...] = merged
        x2 = x_ref[...] + _dot(merged, wo)
        r2 = lax.rsqrt(jnp.mean(x2 * x2, axis=-1, keepdims=True) + EPS)
        xh = x2 * r2
        g = g_ref[...]
        err = xh * g - t_ref[...]
        loss_ref[...] += jnp.sum(err * err, axis=0, keepdims=True) * (0.5 / D)
        dy = err * (1.0 / D)
        gfin_ref[...] += jnp.sum(dy * xh, axis=0, keepdims=True)
        dxh = dy * g
        dx2 = r2 * (dxh - xh * jnp.mean(dxh * xh, axis=-1, keepdims=True))
        dx2_ref[...] = dx2
        dmerged = _dot_nt(dx2.astype(BF16), wo)
        doa = (sa * dmerged).astype(BF16)
        dob = (sb * dmerged).astype(BF16)
        doa_ref[...] = doa
        dob_ref[...] = dob
        dm_ref[0] = (dmerged * out_a * sa * (1.0 - sa)).astype(BF16)
        dm_ref[1] = (dmerged * out_b * sb * (1.0 - sb)).astype(BF16)
        dya_ref[...] = _dot_nt(doa, wa)
        dyb_ref[...] = _dot_nt(dob, wb)

    row = lambda: pl.BlockSpec((tm, D), lambda i: (i, 0))
    seg = lambda s: pl.BlockSpec((None, tm, D), lambda i: (s, i, 0))
    vec = pl.BlockSpec((1, D), lambda i: (0, 0))
    return pl.pallas_call(
        body, name="tail", grid=(t // tm,),
        in_specs=[row(), row(), seg(6), seg(7), row(), row(), _wblock(0), _wblock(1), _wblock(2), vec],
        out_specs=[row(), row(), row(), pl.BlockSpec((2, tm, D), lambda i: (0, i, 0)),
                   row(), row(), row(), vec, vec],
        out_shape=[jax.ShapeDtypeStruct((t, D), F32), jax.ShapeDtypeStruct((t, D), F32),
                   jax.ShapeDtypeStruct((t, D), F32), jax.ShapeDtypeStruct((2, t, D), BF16),
                   jax.ShapeDtypeStruct((t, D), BF16), jax.ShapeDtypeStruct((t, D), BF16),
                   jax.ShapeDtypeStruct((t, D), BF16), jax.ShapeDtypeStruct((1, D), F32),
                   jax.ShapeDtypeStruct((1, D), F32)],
        compiler_params=_params(("arbitrary",)),
    )(ya, yb, proj, proj, x2d, tgt, wg, wg, wg, g_fin)


def _tail_wgrad(ya, yb, merged, doa, dob, dx2):
    t = ya.shape[0]
    tm = 512

    def body(ya_ref, yb_ref, mg_ref, doa_ref, dob_ref, dx2_ref, ga_ref, gb_ref, go_ref):
        @pl.when(pl.program_id(0) == 0)
        def _():
            ga_ref[...] = jnp.zeros_like(ga_ref)
            gb_ref[...] = jnp.zeros_like(gb_ref)
            go_ref[...] = jnp.zeros_like(go_ref)

        ga_ref[...] += _dot_tn(ya_ref[...], doa_ref[...])
        gb_ref[...] += _dot_tn(yb_ref[...], dob_ref[...])
        go_ref[...] += _dot_tn(mg_ref[...], dx2_ref[...].astype(BF16))

    row = lambda: pl.BlockSpec((tm, D), lambda i: (i, 0))
    full = lambda: pl.BlockSpec((D, D), lambda i: (0, 0))
    return pl.pallas_call(
        body, name="tail_wgrad", grid=(t // tm,),
        in_specs=[row() for _ in range(6)], out_specs=[full(), full(), full()],
        out_shape=[jax.ShapeDtypeStruct((D, D), F32)] * 3,
        compiler_params=_params(("arbitrary",)),
    )(ya, yb, merged, doa, dob, dx2)


def _dproj_specs(tm, j_of, i_of):
    last = lambda j, i, lo, n: (jnp.clip(j - lo, 0, n - 1), i, 0)
    return [pl.BlockSpec((None, tm, D), lambda a, b: last(j_of(a, b), i_of(a, b), 0, 2)),
            pl.BlockSpec((None, tm, D), lambda a, b: last(j_of(a, b), i_of(a, b), 2, 4)),
            pl.BlockSpec((None, tm, D), lambda a, b: last(j_of(a, b), i_of(a, b), 6, 2))]


def _dproj_specs_ordered(tm):
    def spec(lo, n):
        def index(k, i, order_ref):
            seg = order_ref[k]
            mine = jnp.logical_and(seg >= lo, seg < lo + n)
            return jnp.where(mine, seg - lo, 0), jnp.where(mine, i, 0), 0
        return pl.BlockSpec((None, tm, D), index)
    return [spec(0, 2), spec(2, 4), spec(6, 2)]


def _dproj_pick(j, da_ref, db_ref, dc_ref, use):
    @pl.when(j < 2)
    def _():
        use(da_ref[...])

    @pl.when(jnp.logical_and(j >= 2, j < 6))
    def _():
        use(db_ref[...])

    @pl.when(j >= 6)
    def _():
        use(dc_ref[...])


RS_X, RS_Y, RS_XY = 0, 1, 2
RS_ROLES = ((RS_XY, RS_X, RS_Y), (RS_Y, RS_XY, RS_X))


def _rs_flip(rel, x, y):
    return ((1 - x, y), (x, 1 - y), (1 - x, 1 - y))[rel]


def _rs_order(x, y, c):
    order = []
    for s in range(4):
        chip = []
        for core in (0, 1):
            px, py = _rs_flip(RS_ROLES[core][s], x, y) if s < 3 else (x, y)
            chip.append(2 * px + py)
        keep = jnp.where(c == 0, chip[0], chip[1])
        give = jnp.where(c == 0, chip[1], chip[0])
        order += [2 * give + 1 - c, 2 * keep + c]
    return jnp.stack(order).astype(jnp.int32)


def _inproj_wgrad_rs(h, dpa, dpb, dpc, order, smalls):
    t = h.shape[0]
    tm = 1024
    nt = t // tm
    nsm = len(smalls)

    def body(order_ref, h_ref, da_ref, db_ref, dc_ref, *rest):
        small_refs, parts_ref, rest = rest[:nsm], rest[nsm], rest[nsm + 1:]
        all_refs, rest = rest[:nsm], rest[nsm:]
        (acc, sib, outb, far, give_send, give_recv, sum_send, sum_recv, far_send, far_recv, own_sem,
         small_send, small_recv, small_own) = rest
        k, i = pl.program_id(0), pl.program_id(1)
        x, y, c = _place()
        own, first, arrive, forward, others = _gather_copies(small_refs, all_refs, small_send, small_recv, small_own)

        @pl.when(jnp.logical_and(k == 0, i == 0))
        def _():
            for cp in own + first:
                cp.start()

        @pl.when(jnp.logical_and(k == 2, i == 0))
        def _():
            for came, on in zip(arrive, forward):
                came.wait_recv()
                on.start()

        def use(d):
            @pl.when(i == 0)
            def _():
                acc[k % 2] = _dot_tn(h_ref[...], d)

            @pl.when(i > 0)
            def _():
                acc[k % 2] += _dot_tn(h_ref[...], d)

        _dproj_pick(order_ref[k], da_ref, db_ref, dc_ref, use)

        def give_copy(s):
            return pltpu.make_async_remote_copy(
                src_ref=acc.at[0], dst_ref=sib.at[s % 2], send_sem=give_send.at[s], recv_sem=give_recv.at[s],
                device_id=(x, y, 1 - c), device_id_type=MESH)

        def sum_copy(s, core):
            slot = 0 if s < 2 else 1
            return pltpu.make_async_remote_copy(
                src_ref=outb.at[s], dst_ref=parts_ref.at[slot], send_sem=sum_send.at[slot], recv_sem=sum_recv.at[slot],
                device_id=(*_rs_flip(RS_ROLES[core][s], x, y), core), device_id_type=MESH)

        def far_copy(s, core):
            return pltpu.make_async_remote_copy(
                src_ref=outb.at[s], dst_ref=far, send_sem=far_send, recv_sem=far_recv,
                device_id=(*_rs_flip(RS_X if core == 0 else RS_Y, x, y), core), device_id_type=MESH)

        own_copy = pltpu.make_async_copy(outb.at[3], parts_ref.at[2], own_sem)

        def send_of(core, s):
            return far_copy(s, core) if RS_ROLES[core][s] == RS_XY else sum_copy(s, core)

        for s in range(4):
            @pl.when(jnp.logical_and(k == 2 * s, i == nt - 1))
            def _():
                give_copy(s).start()

            @pl.when(jnp.logical_and(k == 2 * s + 1, i == nt - 1))
            def _():
                give_copy(s).wait_recv()
                if s == 2:
                    far_copy(s, 0).wait_recv()
                    outb[s] = (acc[1] + sib[s % 2] + far[...].astype(F32)).astype(BF16)
                else:
                    outb[s] = (acc[1] + sib[s % 2]).astype(BF16)
                give_copy(s).wait_send()
                if s < 3:
                    for core in (0, 1):
                        @pl.when(c == core)
                        def _():
                            send_of(core, s).start()
                else:
                    own_copy.start()

        @pl.when(jnp.logical_and(k == NSEG - 1, i == nt - 1))
        def _():
            for slot in (0, 1):
                sum_copy(2 * slot, 0).wait_recv()
            for s in range(3):
                send_of(0, s).wait_send()
            own_copy.wait()
            for cp in others:
                cp.wait_recv()
            for cp in first + forward:
                cp.wait_send()
            for cp in own:
                cp.wait()

    return pl.pallas_call(
        body, name="inproj_wgrad_rs",
        grid_spec=pltpu.PrefetchScalarGridSpec(
            num_scalar_prefetch=1, grid=(NSEG, nt),
            in_specs=[pl.BlockSpec((tm, D), lambda k, i, order_ref: (i, 0))] + _dproj_specs_ordered(tm) + [ANY] * nsm,
            out_specs=[ANY] * (1 + nsm),
            scratch_shapes=[pltpu.VMEM((2, D, D), F32), pltpu.VMEM((2, D, D), F32), pltpu.VMEM((4, D, D), BF16),
                            pltpu.VMEM((D, D), BF16),
                            pltpu.SemaphoreType.DMA((4,)), pltpu.SemaphoreType.DMA((4,)),
                            pltpu.SemaphoreType.DMA((2,)), pltpu.SemaphoreType.DMA((2,)),
                            pltpu.SemaphoreType.DMA, pltpu.SemaphoreType.DMA, pltpu.SemaphoreType.DMA,
                            pltpu.SemaphoreType.DMA((nsm, 7)), pltpu.SemaphoreType.DMA((nsm, 7)),
                            pltpu.SemaphoreType.DMA((nsm,))]),
        out_shape=[jax.ShapeDtypeStruct((3, D, D), BF16)]
        + [jax.ShapeDtypeStruct((NDEV,) + a.shape, a.dtype) for a in smalls],
        compiler_params=_params(("arbitrary", "arbitrary")),
    )(order, h, dpa, dpb, dpc, *smalls)


def _inproj_dgrad(dpa, dpb, dpc, wg, x2d, dx2, g_in):
    t = x2d.shape[0]
    tm = 1024

    def body(da_ref, db_ref, dc_ref, w_ref, x_ref, dx2_ref, g_ref, gx_ref, gg_ref, acc_s):
        i, j = pl.program_id(0), pl.program_id(1)

        @pl.when(jnp.logical_and(i == 0, j == 0))
        def _():
            gg_ref[...] = jnp.zeros_like(gg_ref)

        @pl.when(j == 0)
        def _():
            acc_s[...] = jnp.zeros_like(acc_s)

        def use(d):
            acc_s[...] += _dot_nt(d, w_ref[...])

        _dproj_pick(j, da_ref, db_ref, dc_ref, use)

        @pl.when(j == NSEG - 1)
        def _():
            x = x_ref[...]
            r = lax.rsqrt(jnp.mean(x * x, axis=-1, keepdims=True) + EPS)
            xh = x * r
            dh = acc_s[...]
            gg_ref[...] += jnp.sum(dh * xh, axis=0, keepdims=True)
            dxh = dh * g_ref[...]
            gx_ref[...] = dx2_ref[...] + r * (dxh - xh * jnp.mean(dxh * xh, axis=-1, keepdims=True))

    row = lambda: pl.BlockSpec((tm, D), lambda i, j: (i, 0))
    vec = pl.BlockSpec((1, D), lambda i, j: (0, 0))
    return pl.pallas_call(
        body, name="inproj_dgrad", grid=(t // tm, NSEG),
        in_specs=_dproj_specs(tm, lambda i, j: j, lambda i, j: i)
        + [pl.BlockSpec((None, D, D), lambda i, j: (j, 0, 0)), row(), row(), vec],
        out_specs=[row(), vec],
        out_shape=[jax.ShapeDtypeStruct((t, D), F32), jax.ShapeDtypeStruct((1, D), F32)],
        scratch_shapes=[pltpu.VMEM((tm, D), F32)],
        compiler_params=_params(("arbitrary", "arbitrary")),
    )(dpa, dpb, dpc, wg, x2d, dx2, g_in)


def _adam_update(g, w, m, v):
    m_new = ADAM_B1 * m + (1.0 - ADAM_B1) * g
    v_new = ADAM_B2 * v + (1.0 - ADAM_B2) * (g * g)
    m_hat = m_new / (1.0 - ADAM_B1 ** ADAM_STEP)
    v_hat = v_new / (1.0 - ADAM_B2 ** ADAM_STEP)
    return -ADAM_LR * (m_hat / (jnp.sqrt(v_hat) + ADAM_EPS) + ADAM_WD * w), m_new, v_new


def _sum_in_order(ref):
    total = ref[0].astype(F32)
    for k in range(1, ref.shape[0]):
        total = total + ref[k].astype(F32)
    return total


def _adamw_small(me, vec_all, gx_all, ga_all, groups):
    flat = [a for grp in groups for a in grp]
    ng = len(groups)
    nshard = D // NDEV

    def body(me_ref, vec_ref, shard_ref, gx_ref, ga_ref, *refs):
        ins, outs = refs[:3 * ng], refs[3 * ng:]
        vec = _sum_in_order(vec_ref)
        shard = _sum_in_order(shard_ref)
        grads = [vec[r:r + 1, :] for r in range(6)]
        grads += [shard[0:4, :], shard[4:8, 0:DK // NDEV], _sum_in_order(gx_ref), _sum_in_order(ga_ref)]
        for n, g in enumerate(grads):
            delta, m_new, v_new = _adam_update(g, ins[3 * n][...], ins[3 * n + 1][...], ins[3 * n + 2][...])
            outs[4 * n][...] = g
            outs[4 * n + 1][...] = delta
            outs[4 * n + 2][...] = m_new
            outs[4 * n + 3][...] = v_new
        outs[4 * ng][...] = jnp.sum(vec[6:7, :], axis=1, keepdims=True)

    full = lambda a: pl.BlockSpec(a.shape, lambda i, me_ref, nd=len(a.shape): (0,) * nd)
    out_shape = [jax.ShapeDtypeStruct(w.shape, F32) for w, _, _ in groups for _ in range(4)]
    out_shape.append(jax.ShapeDtypeStruct((1, 1), F32))
    outs = pl.pallas_call(
        body, name="adamw_small",
        grid_spec=pltpu.PrefetchScalarGridSpec(
            num_scalar_prefetch=1, grid=(1,),
            in_specs=[full(vec_all),
                      pl.BlockSpec((NDEV, 8, nshard), lambda i, me_ref: (0, 1, me_ref[0])),
                      full(gx_all), full(ga_all)] + [full(a) for a in flat],
            out_specs=[full(s) for s in out_shape]),
        out_shape=out_shape,
        compiler_params=_params(("arbitrary",)),
    )(me, vec_all, vec_all, gx_all, ga_all, *flat)
    return [outs[4 * n:4 * n + 4] for n in range(ng)], outs[4 * ng]


def _adamw(name, items):
    n, rows, cols = items[0][0].shape
    tr = rows if rows <= 256 else 256
    k = len(items)

    def body(*refs):
        for a in range(k):
            p_ref, w_ref, m_ref, v_ref = refs[4 * a:4 * a + 4]
            g = _sum_in_order(p_ref)
            delta, m_new, v_new = _adam_update(g, w_ref[...], m_ref[...], v_ref[...])
            for o, val in zip(refs[4 * k + 4 * a:4 * k + 4 * a + 4], (g, delta, m_new, v_new)):
                o[...] = val

    blk = lambda: pl.BlockSpec((tr, cols), lambda i: (i, 0))
    outs = pl.pallas_call(
        body, name=name, grid=(rows // tr,),
        in_specs=[pl.BlockSpec((n, tr, cols), lambda i: (0, i, 0)), blk(), blk(), blk()] * k,
        out_specs=[blk() for _ in range(4 * k)],
        out_shape=[jax.ShapeDtypeStruct((rows, cols), F32)] * (4 * k),
        compiler_params=_params(("arbitrary",)),
    )(*[a for item in items for a in item])
    return [outs[4 * a:4 * a + 4] for a in range(k)]


ANY = pl.BlockSpec(memory_space=pl.ANY)


def _place():
    return lax.axis_index("x"), lax.axis_index("y"), lax.axis_index("c")


def _gather_copies(ins, outs, send_sems, recv_sems, own_sems):
    x, y, c = _place()
    me, sibling = (x, y, c), (x, y, 1 - c)
    chips = [(1 - x, y), (x, 1 - y), (1 - x, 1 - y)]
    n = len(ins)

    def copy(a, k, block, to, src=None):
        px, py, pc = block
        dst = outs[a].at[4 * px + 2 * py + pc]
        return pltpu.make_async_remote_copy(
            src_ref=dst if src is None else src, dst_ref=dst,
            send_sem=send_sems.at[a, k], recv_sem=recv_sems.at[a, k], device_id=to, device_id_type=MESH)

    own = [pltpu.make_async_copy(ins[a], outs[a].at[4 * x + 2 * y + c], own_sems.at[a]) for a in range(n)]
    first = []
    for a in range(n):
        first.append(copy(a, 0, me, sibling, src=ins[a]))
        first += [copy(a, 1 + j, me, (*chip, c), src=ins[a]) for j, chip in enumerate(chips)]
    arrive = [copy(a, 1 + j, (*chip, c), me) for j, chip in enumerate(chips) for a in range(n)]
    forward = [copy(a, 4 + j, (*chip, c), sibling) for j, chip in enumerate(chips) for a in range(n)]
    rest = [copy(a, 0, sibling, me) for a in range(n)]
    rest += [copy(a, 4 + j, (*chip, 1 - c), me) for a in range(n) for j, chip in enumerate(chips)]
    return own, first, arrive, forward, rest


def _sibling_copies(ins, outs, send_sems, recv_sems):
    x, y, c = _place()
    return [pltpu.make_async_remote_copy(
        src_ref=ins[a].at[2 * q + 1 - c], dst_ref=outs[a].at[q],
        send_sem=send_sems.at[a, q], recv_sem=recv_sems.at[a, q],
        device_id=(x, y, 1 - c), device_id_type=MESH) for a in range(len(ins)) for q in range(4)]


def _chip_copies(ins, outs, send_sems, recv_sems, local_sems):
    x, y, c = _place()
    my_chip = 2 * x + y
    chips = [(1 - x, y), (x, 1 - y), (1 - x, 1 - y)]
    n = len(ins)
    mine = [pltpu.make_async_copy(ins[a].at[my_chip], outs[a].at[my_chip], local_sems.at[a]) for a in range(n)]
    sends = [pltpu.make_async_remote_copy(
        src_ref=ins[a].at[2 * px + py], dst_ref=outs[a].at[my_chip],
        send_sem=send_sems.at[a, j], recv_sem=recv_sems.at[a, j],
        device_id=(px, py, c), device_id_type=MESH) for a in range(n) for j, (px, py) in enumerate(chips)]
    recvs = [pltpu.make_async_remote_copy(
        src_ref=ins[a].at[my_chip], dst_ref=outs[a].at[2 * px + py],
        send_sem=send_sems.at[a, j], recv_sem=recv_sems.at[a, j],
        device_id=(px, py, c), device_id_type=MESH) for a in range(n) for j, (px, py) in enumerate(chips)]
    return mine, sends, recvs


def _chip_sum(owns, gots, core):
    n = len(owns)
    _, rows, cols = owns[0].shape

    def body(core_ref, *refs):
        for a in range(n):
            refs[2 * n + a][...] = (refs[a][...] + refs[n + a][...]).astype(BF16)

    own_spec = pl.BlockSpec((None, rows, cols), lambda q, core_ref: (2 * q + core_ref[0], 0, 0))
    slab = pl.BlockSpec((None, rows, cols), lambda q, core_ref: (q, 0, 0))
    return pl.pallas_call(
        body, name="chip_sum",
        grid_spec=pltpu.PrefetchScalarGridSpec(
            num_scalar_prefetch=1, grid=(4,),
            in_specs=[own_spec] * n + [slab] * n, out_specs=[slab] * n),
        out_shape=[jax.ShapeDtypeStruct((4, rows, cols), BF16)] * n,
        compiler_params=_params(("arbitrary",)),
    )(core, *owns, *gots)


def _block_diag(w):
    w4 = w.reshape(NCB, 4, 64, 64)
    eye = jnp.eye(4, dtype=w.dtype)
    return (w4[:, :, :, None, :] * eye[None, :, None, :, None]).reshape(NCB, CB, CB)


def _block_diag_back(g):
    g5 = g.reshape(NCB, 4, 64, 4, 64)
    return jnp.stack([g5[:, m, :, m, :] for m in range(4)], axis=1).reshape(16, 64, 64)


def kernel(x, norm_in, w_in, conv_w, conv_b, gate_x_w, gate_x_b, gate_a_w, gate_a_b, lru_lambda, gn_gain, w_proj_a, w_proj_b, w_out, norm_final, loss_target, m_norm_in, m_w_in, m_conv_w, m_conv_b, m_gate_x_w, m_gate_x_b, m_gate_a_w, m_gate_a_b, m_lru_lambda, m_gn_gain, m_w_proj_a, m_w_proj_b, m_w_out, m_norm_final, v_norm_in, v_w_in, v_conv_w, v_conv_b, v_gate_x_w, v_gate_x_b, v_gate_a_w, v_gate_a_b, v_lru_lambda, v_gn_gain, v_w_proj_a, v_w_proj_b, v_w_out, v_norm_final):
    xi, yi, ci = _place()
    me = 4 * xi + 2 * yi + ci
    core = ci.astype(jnp.int32).reshape(1)
    nshard = D // NDEV
    nb = x.shape[0]
    t = nb * S
    x2d = x.reshape(t, D)
    tgt2d = loss_target.reshape(t, D)
    g_final = norm_final.reshape(1, D)
    wbd = jnp.concatenate([_block_diag(gate_x_w[0]), _block_diag(gate_a_w[0])], axis=-1).astype(BF16)
    tables = _retention_tables()

    wp_own = jnp.concatenate([w_proj_a[0], w_proj_b[0], w_out[0]], axis=0).astype(BF16)
    tiny = jnp.concatenate([conv_w[0], jnp.pad(gn_gain[0], ((0, 0), (0, nshard - DK // NDEV)))], axis=0)
    proj, h, wg, tiny_g = _inproj_gather(x2d, norm_in, w_in[0].astype(BF16), tiny, *_gather_order(xi, yi, ci))
    conv_w_full = tiny_g[:, 0:4, :].transpose(1, 0, 2).reshape(4, D)
    gain3 = tiny_g[:, 4:8, :DK // NDEV].transpose(1, 0, 2).reshape(HEADS, 1, DK)

    ya, hs, xc, gi, gr = _lru_fwd(proj, conv_w_full, conv_b, wbd, gate_x_b, gate_a_b, lru_lambda, nb)
    yb, qr, kr, o, rs, wpg = _ret_fwd(proj, gain3, tables, nb, wp_own)
    dx2, dya, dyb, dpc, merged, doa, dob, g_fin, loss_vec = _tail(ya, yb, proj, x2d, tgt2d, wpg, g_final)
    g_pa, g_pb, g_out = _tail_wgrad(ya, yb, merged, doa, dob, dx2)

    own = [g.reshape(NDEV, nshard, D) for g in (g_pa, g_pb, g_out)]
    dpa, g_wbd, g_vec, *got = _lru_bwd(proj, hs, xc, gi, gr, dya, conv_w_full, wbd, lru_lambda, nb, own)
    sums = _chip_sum(own, got, core)
    dpb, g_gain, *parts = _ret_bwd(proj, qr, kr, o, rs, dyb, gain3, tables, nb, sums)

    grad_x, g_norm_in = _inproj_dgrad(dpa, dpb, dpc, wg, x2d, dx2, norm_in)
    grad_x = grad_x.reshape(nb, S, D)

    gain_rows = jnp.pad(g_gain.reshape(HEADS, NDEV, DK // NDEV), ((0, 0), (0, 0), (0, nshard - DK // NDEV)))
    vec = jnp.concatenate([g_norm_in, g_vec[0:4], g_fin, loss_vec, jnp.zeros((1, D), F32), g_vec[4:8],
                           gain_rows.reshape(HEADS, D)], axis=0)
    g_gx = _block_diag_back(g_wbd[:, :, :CB]).reshape(D // 2, 128)
    g_ga = _block_diag_back(g_wbd[:, :, CB:]).reshape(D // 2, 128)
    parts_in, vec_all, gx_all, ga_all = _inproj_wgrad_rs(h, dpa, dpb, dpc, _rs_order(xi, yi, ci),
                                                         [vec, g_gx, g_ga])
    gx_all = gx_all.reshape(NDEV, D, 64)
    ga_all = ga_all.reshape(NDEV, D, 64)
    parts = [parts_in] + list(parts)

    res = {}
    (out,) = _adamw("adamw_w_in", [(parts[0], w_in[0], m_w_in[0], v_w_in[0])])
    res["w_in"] = [o[None] for o in out]
    square = [("w_proj_a", w_proj_a, m_w_proj_a, v_w_proj_a), ("w_proj_b", w_proj_b, m_w_proj_b, v_w_proj_b),
              ("w_out", w_out, m_w_out, v_w_out)]
    outs = _adamw("adamw_square", [(parts[1 + k], w[0], m[0], v[0]) for k, (_, w, m, v) in enumerate(square)])
    for (nm, _, _, _), out in zip(square, outs):
        res[nm] = [o[None] for o in out]

    row = lambda a: a.reshape(1, D)
    gate = lambda a: a.reshape(D, 64)
    groups = [("norm_in", norm_in, m_norm_in, v_norm_in, row), ("conv_b", conv_b, m_conv_b, v_conv_b, row),
              ("gate_x_b", gate_x_b, m_gate_x_b, v_gate_x_b, row), ("gate_a_b", gate_a_b, m_gate_a_b, v_gate_a_b, row),
              ("lru_lambda", lru_lambda, m_lru_lambda, v_lru_lambda, row),
              ("norm_final", norm_final, m_norm_final, v_norm_final, row),
              ("conv_w", conv_w, m_conv_w, v_conv_w, lambda a: a[0]), ("gn_gain", gn_gain, m_gn_gain, v_gn_gain, lambda a: a[0]),
              ("gate_x_w", gate_x_w, m_gate_x_w, v_gate_x_w, gate), ("gate_a_w", gate_a_w, m_gate_a_w, v_gate_a_w, gate)]
    small_out, loss = _adamw_small(me.astype(jnp.int32).reshape(1), vec_all, gx_all, ga_all,
                                   [tuple(view(a) for a in (w, m, v)) for _, w, m, v, view in groups])
    for (nm, w, _, _, _), out in zip(groups, small_out):
        res[nm] = [o.reshape(w.shape) for o in out]
    loss = loss.reshape(())

    order = ["norm_in", "w_in", "conv_w", "conv_b", "gate_x_w", "gate_x_b", "gate_a_w", "gate_a_b", "lru_lambda",
             "gn_gain", "w_proj_a", "w_proj_b", "w_out", "norm_final"]
    outs = [loss, grad_x]
    for k in range(4):
        outs += [res[nm][k] for nm in order]
    return tuple(outs)
```

```python
import numpy as np

import jax
import jax.numpy as jnp
from jax import lax
from jax.experimental import pallas as pl
from jax.experimental.pallas import tpu as pltpu

F32 = jnp.float32
BF16 = jnp.bfloat16
MESH = pl.DeviceIdType.MESH

D = 1024
S = 2048
NSEG = 8
NDEV = 8
HEADS = 4
DK = 256
CH = 256
NCH = S // CH
CB = 256
NCB = D // CB
RC = 512
RC_CONV = 128
SCAN_GROUP = 8
EPS = 1e-6
LRU_C = 8.0
VMEM_LIMIT = 56 * 1024 * 1024

ADAM_LR = 0.001
ADAM_B1 = 0.9
ADAM_B2 = 0.999
ADAM_EPS = 1e-08
ADAM_WD = 0.01
ADAM_STEP = 10


def _params(sem=None):
    return pltpu.CompilerParams(dimension_semantics=sem, vmem_limit_bytes=VMEM_LIMIT)


def _dot(a, b):
    return jnp.dot(a, b, preferred_element_type=F32)


def _dot_nt(a, b):
    return lax.dot_general(a, b, (((1,), (1,)), ((), ())), preferred_element_type=F32)


def _dot_tn(a, b):
    return lax.dot_general(a, b, (((0,), (0,)), ((), ())), preferred_element_type=F32)


def _sigmoid(x):
    return jax.nn.sigmoid(x)


def _expm1_nonpos(x):
    poly = x * (1.0 + x * (0.5 + x * (1.0 / 6.0 + x * (1.0 / 24.0))))
    return jnp.where(x > -0.05, poly, jnp.exp(x) - 1.0)


def _softplus(x):
    return jnp.maximum(x, 0.0) + jnp.log(1.0 + jnp.exp(-jnp.abs(x)))


def _rows(c, n):
    return pl.ds(pl.multiple_of(c * n, n), n)


def _window_before(ref, c, n):
    r0 = c * n
    if ref.dtype == BF16:
        prev = ref[pl.ds(pl.multiple_of(jnp.maximum(r0 - 16, 0), 16), 16), :].astype(F32)[8:, :]
    else:
        prev = ref[pl.ds(pl.multiple_of(jnp.maximum(r0 - 8, 0), 8), 8), :]
    prev = jnp.where(c > 0, prev, 0.0)
    return jnp.concatenate([prev, ref[_rows(c, n), :].astype(F32)], axis=0)


def _shift_down(win, s, n):
    if s == 0:
        return win[8:, :]
    return pltpu.roll(win, s, 0)[8:, :]


def _shift_up(win, s, n):
    if s == 0:
        return win[:n, :]
    return pltpu.roll(win, n + 8 - s, 0)[:n, :]


HALF = D // 2
GATHER_SLOTS = [("own", None, 0), ("own", None, 1), ("sib", None, 0), ("sib", None, 1)]
for _j, _h in ((0, 0), (1, 0), (0, 1), (1, 1), (2, 0), (2, 1)):
    GATHER_SLOTS += [("ici", _j, _h), ("fwd", _j, _h)]
NSLOT = len(GATHER_SLOTS)


def _gather_order(x, y, c):
    chips = [(1 - x, y), (x, 1 - y), (1 - x, 1 - y)]
    segs, halves = [], []
    for kind, j, h in GATHER_SLOTS:
        if kind == "own":
            seg = 4 * x + 2 * y + c
        elif kind == "sib":
            seg = 4 * x + 2 * y + 1 - c
        else:
            px, py = chips[j]
            seg = 4 * px + 2 * py + (c if kind == "ici" else 1 - c)
        segs.append(seg)
        halves.append(h)
    return jnp.stack(segs).astype(jnp.int32), jnp.asarray(halves, jnp.int32)


def _inproj_gather(x2d, g_in, w_own, sides, order, halves):
    t = x2d.shape[0]
    tm = 1024
    nt = t // tm
    ns = len(sides)

    def body(order_ref, half_ref, x_ref, g_ref, w_own_ref, *rest):
        side_refs, (proj_ref, h_ref, wg_ref), rest = rest[:ns], rest[ns:ns + 3], rest[ns + 3:]
        sideg_refs, rest = rest[:ns], rest[ns:]
        w_all, h_all, send_sems, recv_sems, own_sems, out_sems, side_send, side_recv, side_own = rest
        k, i = pl.program_id(0), pl.program_id(1)
        x, y, c = _place()
        me, sibling = (x, y, c), (x, y, 1 - c)
        mine = 4 * x + 2 * y + c
        chips = [(1 - x, y), (x, 1 - y), (1 - x, 1 - y)]

        def copy(h, n, block, to, own_src=False):
            px, py, pc = block
            dst = w_all.at[4 * px + 2 * py + pc, h]
            return pltpu.make_async_remote_copy(
                src_ref=w_own_ref.at[:, pl.ds(h * HALF, HALF)] if own_src else dst, dst_ref=dst,
                send_sem=send_sems.at[h, n], recv_sem=recv_sems.at[h, n], device_id=to, device_id_type=MESH)

        def side_copy(a, n, block, to, own_src=False):
            px, py, pc = block
            dst = sideg_refs[a].at[4 * px + 2 * py + pc]
            return pltpu.make_async_remote_copy(
                src_ref=side_refs[a] if own_src else dst, dst_ref=dst,
                send_sem=side_send.at[a, n], recv_sem=side_recv.at[a, n], device_id=to, device_id_type=MESH)

        def own_copy(h):
            return pltpu.make_async_copy(w_own_ref.at[:, pl.ds(h * HALF, HALF)], w_all.at[mine, h], own_sems.at[h])

        side_mine = [pltpu.make_async_copy(side_refs[a], sideg_refs[a].at[mine], side_own.at[a]) for a in range(ns)]

        def keep_copy(n):
            h = GATHER_SLOTS[n][2]
            return pltpu.make_async_copy(w_all.at[order_ref[n], h], wg_ref.at[order_ref[n], :, pl.ds(h * HALF, HALF)],
                                         out_sems.at[n])

        near = [(0, sibling), (1, (*chips[0], c)), (2, (*chips[1], c))]
        first = [copy(h, n, me, to, True) for h in (0, 1) for n, to in near]
        side_first = [side_copy(a, n, me, to, True) for a in range(ns)
                      for n, to in [(0, sibling)] + [(1 + j, (*chip, c)) for j, chip in enumerate(chips)]]

        def relay(h, j):
            seg = w_all.at[4 * chips[j][0] + 2 * chips[j][1] + c, h]
            return pltpu.make_async_remote_copy(
                src_ref=seg, dst_ref=seg, send_sem=send_sems.at[h, 3], recv_sem=recv_sems.at[h, 3],
                device_id=(*chips[1 - j], c), device_id_type=MESH)

        for n, (kind, j, h) in enumerate(GATHER_SLOTS):
            @pl.when(jnp.logical_and(k == n, i == 0))
            def _():
                if n == 0:
                    own_copy(0).start()
                    own_copy(1).start()
                    for cp in side_mine + first + side_first:
                        cp.start()
                if kind == "own":
                    own_copy(h).wait()
                elif kind == "sib":
                    copy(h, 0, sibling, me).wait_recv()
                elif kind == "ici":
                    copy(h, 1 + j, (*chips[j], c), me).wait_recv()
                    copy(h, 4 + j, (*chips[j], c), sibling).start()
                    if j < 2:
                        @pl.when(c == j)
                        def _():
                            relay(h, j).start()
                else:
                    copy(h, 4 + j, (*chips[j], 1 - c), me).wait_recv()
                keep_copy(n).start()

        rows = pl.ds(pl.multiple_of(i * tm, tm), tm)

        @pl.when(k == 0)
        def _():
            xv = x_ref[...]
            r = lax.rsqrt(jnp.mean(xv * xv, axis=-1, keepdims=True) + EPS)
            hv = (xv * r * g_ref[...]).astype(BF16)
            h_ref[...] = hv
            h_all[rows, :] = hv

        proj_ref[...] = _dot(h_all[rows, :], w_all[order_ref[k], half_ref[k]]).astype(BF16)

        @pl.when(jnp.logical_and(k == NSLOT - 2, i == 0))
        def _():
            for a in range(ns):
                for j, chip in enumerate(chips):
                    side_copy(a, 1 + j, (*chip, c), me).wait_recv()
                    side_copy(a, 4 + j, (*chip, c), sibling).start()

        @pl.when(jnp.logical_and(k == NSLOT - 1, i == nt - 1))
        def _():
            for a in range(ns):
                side_copy(a, 0, sibling, me).wait_recv()
                for j, chip in enumerate(chips):
                    side_copy(a, 4 + j, (*chip, 1 - c), me).wait_recv()
            for cp in first + side_first:
                cp.wait_send()
            for j, chip in enumerate(chips):
                for a in range(ns):
                    side_copy(a, 4 + j, (*chip, c), sibling).wait_send()
                for h in (0, 1):
                    copy(h, 4 + j, (*chip, c), sibling).wait_send()
            for h in (0, 1):
                relay(h, 0).wait_send()
            for cp in side_mine:
                cp.wait()
            for n in range(NSLOT):
                keep_copy(n).wait()

    hold = lambda k, i, order_ref, half_ref: (jnp.where(k == 0, i, nt - 1), 0)
    return pl.pallas_call(
        body, name="inproj_gather",
        grid_spec=pltpu.PrefetchScalarGridSpec(
            num_scalar_prefetch=2, grid=(NSLOT, nt),
            in_specs=[pl.BlockSpec((tm, D), hold),
                      pl.BlockSpec((1, D), lambda k, i, order_ref, half_ref: (0, 0)),
                      ANY] + [ANY] * ns,
            out_specs=[pl.BlockSpec((None, tm, HALF), lambda k, i, order_ref, half_ref: (order_ref[k], i, half_ref[k])),
                       pl.BlockSpec((tm, D), hold),
                       ANY] + [ANY] * ns,
            scratch_shapes=[pltpu.VMEM((NDEV, 2, D, HALF), BF16), pltpu.VMEM((t, D), BF16),
                            pltpu.SemaphoreType.DMA((2, 7)), pltpu.SemaphoreType.DMA((2, 7)),
                            pltpu.SemaphoreType.DMA((2,)), pltpu.SemaphoreType.DMA((NSLOT,)),
                            pltpu.SemaphoreType.DMA((ns, 7)), pltpu.SemaphoreType.DMA((ns, 7)),
                            pltpu.SemaphoreType.DMA((ns,))]),
        out_shape=[jax.ShapeDtypeStruct((NSEG, t, D), BF16), jax.ShapeDtypeStruct((t, D), BF16),
                   jax.ShapeDtypeStruct((NDEV,) + w_own.shape, BF16)]
        + [jax.ShapeDtypeStruct((NDEV,) + a.shape, a.dtype) for a in sides],
        compiler_params=_params(("arbitrary", "arbitrary")),
    )(order, halves, x2d, g_in, w_own, *sides)


def _tile_scan(a, u):
    row = lax.broadcasted_iota(jnp.int32, a.shape, 0)
    for d in (1, 2, 4):
        m = row >= d
        a_sh = pltpu.roll(a, d, 0)
        u_sh = pltpu.roll(u, d, 0)
        u = jnp.where(m, a * u_sh + u, u)
        a = jnp.where(m, a * a_sh, a)
    return a, u


def _tile_scan_rev(a, w):
    row = lax.broadcasted_iota(jnp.int32, a.shape, 0)
    for d in (1, 2, 4):
        m = row < 8 - d
        a_sh = pltpu.roll(a, 8 - d, 0)
        w_sh = pltpu.roll(w, 8 - d, 0)
        w = jnp.where(m, a * w_sh + w, w)
        a = jnp.where(m, a * a_sh, a)
    return a, w


def _lru_gates(xa_ref, c, cw_ref, cb_ref, wbd_ref, bx_ref, ba_ref, sp):
    win = _window_before(xa_ref, c, RC)
    xc = cb_ref[...] + cw_ref[3:4, :] * _shift_down(win, 0, RC)
    for s in (1, 2, 3):
        xc = xc + cw_ref[3 - s:4 - s, :] * _shift_down(win, s, RC)
    z = _dot(xc.astype(BF16), wbd_ref[...])
    gi = _sigmoid(z[:, :CB] + bx_ref[...])
    gr = _sigmoid(z[:, CB:] + ba_ref[...])
    log_a = -LRU_C * gr * sp
    return win, xc, gi, gr, log_a


def _lru_fwd(proj, conv_w, conv_b, wbd, bx, ba, lam, nb):
    t = nb * S

    def body(xa_ref, ga_ref, cw_ref, cb_ref, wbd_ref, bx_ref, ba_ref, lam_ref,
             ya_ref, hs_ref, xc_ref, gi_ref, gr_ref, a_s, u_s):
        sp = _softplus(-lam_ref[...])

        def gates(c, carry):
            _, xc, gi, gr, log_a = _lru_gates(xa_ref, c, cw_ref, cb_ref, wbd_ref, bx_ref, ba_ref, sp)
            rows = _rows(c, RC)
            a_s[rows, :] = jnp.exp(log_a)
            u_s[rows, :] = jnp.sqrt(-_expm1_nonpos(2.0 * log_a)) * (gi * xc)
            xc_ref[rows, :] = xc
            gi_ref[rows, :] = gi
            gr_ref[rows, :] = gr
            return carry

        lax.fori_loop(0, S // RC, gates, 0)

        def scan(g, h):
            for k in range(SCAN_GROUP):
                rows = pl.ds(pl.multiple_of(g * (8 * SCAN_GROUP), 8 * SCAN_GROUP) + 8 * k, 8)
                a_cum, u_cum = _tile_scan(a_s[rows, :], u_s[rows, :])
                hs_ref[rows, :] = u_cum + a_cum * h
                h = u_cum[7:8, :] + a_cum[7:8, :] * h
            return h

        lax.fori_loop(0, S // (8 * SCAN_GROUP), scan, jnp.zeros((1, CB), F32))

        def gate_out(c, carry):
            ga = ga_ref[_rows(c, RC), :].astype(F32)
            ya_ref[_rows(c, RC), :] = (ga * _sigmoid(ga) * hs_ref[_rows(c, RC), :]).astype(BF16)
            return carry

        lax.fori_loop(0, S // RC, gate_out, 0)

    vec = pl.BlockSpec((1, CB), lambda b, cb: (0, cb))
    blk = pl.BlockSpec((S, CB), lambda b, cb: (b, cb))
    return pl.pallas_call(
        body, name="lru_fwd", grid=(nb, NCB),
        in_specs=[pl.BlockSpec((None, S, CB), lambda b, cb: (0, b, cb)),
                  pl.BlockSpec((None, S, CB), lambda b, cb: (1, b, cb)),
                  pl.BlockSpec((4, CB), lambda b, cb: (0, cb)),
                  vec,
                  pl.BlockSpec((None, CB, 2 * CB), lambda b, cb: (cb, 0, 0)),
                  vec, vec, vec],
        out_specs=[blk] + [pl.BlockSpec((None, None, S, CB), lambda b, cb: (b, cb, 0, 0))] * 4,
        out_shape=[jax.ShapeDtypeStruct((t, D), BF16)] + [jax.ShapeDtypeStruct((nb, NCB, S, CB), F32)] * 4,
        scratch_shapes=[pltpu.VMEM((S, CB), F32), pltpu.VMEM((S, CB), F32)],
        compiler_params=_params(("arbitrary", "arbitrary")),
    )(proj, proj, conv_w, conv_b, wbd, bx, ba, lam)


def _lru_bwd(proj, hs, xc_f, gi_f, gr_f, dya, conv_w, wbd, lam, nb, give):
    t = nb * S
    ng = len(give)

    def body(xa_ref, ga_ref, hs_ref, xc_s, gi_s, gr_s, dya_ref, cw_ref, wbd_ref, lam_ref, *rest):
        give_refs, rest = rest[:ng], rest[ng:]
        dp_ref, dwbd_ref, vec_ref = rest[:3]
        got_refs, rest = rest[3:3 + ng], rest[3 + ng:]
        a_s, dl_s, dh_s, dxc_s, acc_s, send_sems, recv_sems = rest
        b = pl.program_id(1)
        exchange = _sibling_copies(give_refs, got_refs, send_sems, recv_sems)

        @pl.when(jnp.logical_and(pl.program_id(0) == 0, b == 0))
        def _():
            for cp in exchange:
                cp.start()

        lam_v = lam_ref[...]
        sp = _softplus(-lam_v)
        acc_s[...] = jnp.zeros_like(acc_s)

        @pl.when(b == 0)
        def _():
            dwbd_ref[...] = jnp.zeros_like(dwbd_ref)
            vec_ref[...] = jnp.zeros_like(vec_ref)

        def gates(c, carry):
            rows = _rows(c, RC)
            a_s[rows, :] = jnp.exp(-LRU_C * gr_s[rows, :] * sp)
            ga = ga_ref[rows, :].astype(F32)
            sg = _sigmoid(ga)
            dya_c = dya_ref[rows, :]
            dl_s[rows, :] = dya_c * (ga * sg)
            dp_ref[1, rows, :] = (dya_c * hs_ref[rows, :] * (sg * (1.0 + ga * (1.0 - sg)))).astype(BF16)
            return carry

        lax.fori_loop(0, S // RC, gates, 0)

        def scan(i, g_in):
            base = pl.multiple_of((S // (8 * SCAN_GROUP) - 1 - i) * (8 * SCAN_GROUP), 8 * SCAN_GROUP)
            row = lax.broadcasted_iota(jnp.int32, (8, CB), 0)
            for k in reversed(range(SCAN_GROUP)):
                rows = pl.ds(base + 8 * k, 8)
                a = a_s[rows, :]
                dl = dl_s[rows, :]
                a_cum, g_loc = _tile_scan_rev(a, a * dl)
                g = g_loc + a_cum * g_in
                dh_s[rows, :] = dl + jnp.where(row < 7, pltpu.roll(g, 7, 0), g_in)
                g_in = g_loc[0:1, :] + a_cum[0:1, :] * g_in
            return g_in

        lax.fori_loop(0, S // (8 * SCAN_GROUP), scan, jnp.zeros((1, CB), F32))

        dxc_s[pl.ds(S, 8), :] = jnp.zeros((8, CB), F32)

        def grads(c, carry):
            rows = _rows(c, RC)
            dh = dh_s[rows, :]
            h_prev = _shift_down(_window_before(hs_ref, c, RC), 1, RC)
            xc, gi, gr, a = xc_s[rows, :], gi_s[rows, :], gr_s[rows, :], a_s[rows, :]
            mult = jnp.sqrt(-_expm1_nonpos(-2.0 * LRU_C * gr * sp))
            dmult = dh * gi * xc
            d_log_a = dh * h_prev * a - dmult * (a * a) / mult
            dzi = dh * mult * xc * gi * (1.0 - gi)
            dzr = d_log_a * (-LRU_C * sp) * gr * (1.0 - gr)
            dz = jnp.concatenate([dzi, dzr], axis=1).astype(BF16)
            dxc_s[rows, :] = dh * mult * gi + _dot_nt(dz, wbd_ref[...])
            dwbd_ref[...] += _dot_tn(xc.astype(BF16), dz)
            acc_s[1:2, :] += jnp.sum(dzi, axis=0, keepdims=True)
            acc_s[2:3, :] += jnp.sum(dzr, axis=0, keepdims=True)
            acc_s[3:4, :] += jnp.sum(d_log_a * (-LRU_C * gr), axis=0, keepdims=True)
            return carry

        lax.fori_loop(0, S // RC, grads, 0)

        def conv_bwd(c, carry):
            rows = _rows(c, RC_CONV)
            dwin = dxc_s[pl.ds(pl.multiple_of(c * RC_CONV, RC_CONV), RC_CONV + 8), :]
            dxc = dwin[:RC_CONV, :]
            xwin = _window_before(xa_ref, c, RC_CONV)
            dxa = cw_ref[3:4, :] * dxc
            acc_s[0:1, :] += jnp.sum(dxc, axis=0, keepdims=True)
            acc_s[7:8, :] += jnp.sum(dxc * _shift_down(xwin, 0, RC_CONV), axis=0, keepdims=True)
            for s in (1, 2, 3):
                dxa = dxa + cw_ref[3 - s:4 - s, :] * _shift_up(dwin, s, RC_CONV)
                acc_s[7 - s:8 - s, :] += jnp.sum(dxc * _shift_down(xwin, s, RC_CONV), axis=0, keepdims=True)
            dp_ref[0, rows, :] = dxa.astype(BF16)
            return carry

        lax.fori_loop(0, S // RC_CONV, conv_bwd, 0)

        row = lax.broadcasted_iota(jnp.int32, acc_s.shape, 0)
        vec_ref[...] += jnp.where(row == 3, acc_s[...] * (-_sigmoid(-lam_v)), acc_s[...])

        @pl.when(jnp.logical_and(pl.program_id(0) == NCB - 1, b == nb - 1))
        def _():
            for cp in exchange:
                cp.wait()

    vec = pl.BlockSpec((1, CB), lambda cb, b: (0, cb))
    blk = pl.BlockSpec((S, CB), lambda cb, b: (b, cb))
    own = pl.BlockSpec((None, None, S, CB), lambda cb, b: (b, cb, 0, 0))
    return pl.pallas_call(
        body, name="lru_bwd", grid=(NCB, nb),
        in_specs=[pl.BlockSpec((None, S, CB), lambda cb, b: (0, b, cb)),
                  pl.BlockSpec((None, S, CB), lambda cb, b: (1, b, cb)),
                  own, own, own, own, blk,
                  pl.BlockSpec((4, CB), lambda cb, b: (0, cb)),
                  pl.BlockSpec((None, CB, 2 * CB), lambda cb, b: (cb, 0, 0)),
                  vec] + [ANY] * ng,
        out_specs=[pl.BlockSpec((2, S, CB), lambda cb, b: (0, b, cb)),
                   pl.BlockSpec((None, CB, 2 * CB), lambda cb, b: (cb, 0, 0)),
                   pl.BlockSpec((8, CB), lambda cb, b: (0, cb))] + [ANY] * ng,
        out_shape=[jax.ShapeDtypeStruct((2, t, D), BF16),
                   jax.ShapeDtypeStruct((NCB, CB, 2 * CB), F32),
                   jax.ShapeDtypeStruct((8, D), F32)]
        + [jax.ShapeDtypeStruct((4,) + g.shape[1:], g.dtype) for g in give],
        scratch_shapes=[pltpu.VMEM((S, CB), F32), pltpu.VMEM((S, CB), F32), pltpu.VMEM((S, CB), F32),
                        pltpu.VMEM((S + 8, CB), F32), pltpu.VMEM((8, CB), F32),
                        pltpu.SemaphoreType.DMA((ng, 4)), pltpu.SemaphoreType.DMA((ng, 4))],
        compiler_params=_params(("arbitrary", "arbitrary")),
    )(proj, proj, hs, xc_f, gi_f, gr_f, dya, conv_w, wbd, lam, *give)


def _retention_tables():
    f32 = np.float32
    log_g = np.log1p(-(f32(2.0) ** (f32(-5.0) - np.arange(HEADS, dtype=f32)))).astype(f32)
    idx = np.arange(CH, dtype=f32)
    diff = idx[:, None] - idx[None, :]
    inner = np.where(diff >= 0, np.exp(np.maximum(diff, f32(0.0))[None] * log_g[:, None, None]), f32(0.0)).astype(f32)
    cross = np.exp((idx[None, :] + f32(1.0)) * log_g[:, None]).astype(f32)
    state = np.exp((f32(CH - 1.0) - idx[None, :]) * log_g[:, None]).astype(f32)
    cross = np.ascontiguousarray(np.broadcast_to(cross[:, :, None], (HEADS, CH, DK)))
    state = np.ascontiguousarray(np.broadcast_to(state[:, :, None], (HEADS, CH, DK)))
    half = DK // 2
    freqs = (f32(10000.0) ** (-np.arange(half, dtype=f32) / f32(half))).astype(f32)
    ang = (np.arange(S, dtype=f32)[:, None] * freqs[None, :]).astype(f32)
    return tuple(jnp.asarray(a) for a in (inner, cross, state, np.cos(ang).astype(f32), np.sin(ang).astype(f32)))


def _rotate(x, cos, sin):
    half = DK // 2
    x1, x2 = x[:, :half], x[:, half:]
    return jnp.concatenate([x1 * cos - x2 * sin, x1 * sin + x2 * cos], axis=1)


def _rotate_back(d, cos, sin):
    half = DK // 2
    d1, d2 = d[:, :half], d[:, half:]
    return jnp.concatenate([d1 * cos + d2 * sin, d2 * cos - d1 * sin], axis=1)


def _ret_fwd(proj, gain, tables, nb):
    t = nb * S
    inner_t, cross_t, state_t, cos_t, sin_t = tables

    def body(q_ref, k_ref, v_ref, gb_ref, gain_ref, dm_ref, cd_ref, sd_ref, cos_ref, sin_ref,
             yb_ref, qr_ref, kr_ref, o_ref, rs_ref, r_s):
        r_s[...] = jnp.zeros_like(r_s)
        chunk_decay = cd_ref[CH - 1:CH, :]

        def chunk(c, carry):
            rows = _rows(c, CH)
            cos, sin = cos_ref[rows, :], sin_ref[rows, :]
            qr = _rotate(q_ref[rows, :].astype(F32), cos, sin).astype(BF16)
            kr = (_rotate(k_ref[rows, :].astype(F32), cos, sin) * (DK ** -0.5)).astype(BF16)
            vb = v_ref[rows, :]
            v = vb.astype(F32)
            qr_ref[rows, :] = qr
            kr_ref[rows, :] = kr
            r = r_s[...]
            rb = r.astype(BF16)
            rs_ref[c] = rb
            p = (_dot_nt(qr, kr) * dm_ref[...]).astype(BF16)
            o = _dot(p, vb) + _dot(qr, rb) * cd_ref[...]
            r_s[...] = chunk_decay * r + _dot_tn(kr, (v * sd_ref[...]).astype(BF16))
            o_ref[rows, :] = o
            oc = o - jnp.mean(o, axis=-1, keepdims=True)
            rstd = lax.rsqrt(jnp.mean(oc * oc, axis=-1, keepdims=True) + EPS)
            gb = gb_ref[rows, :].astype(F32)
            yb_ref[rows, :] = (gb * _sigmoid(gb) * (oc * rstd * gain_ref[...])).astype(BF16)
            return carry

        lax.fori_loop(0, NCH, chunk, 0, unroll=2)

    seg = lambda s: pl.BlockSpec((None, S, DK), lambda b, h: (s, b, h))
    tab = pl.BlockSpec((None, CH, DK), lambda b, h: (h, 0, 0))
    rot = pl.BlockSpec((S, DK // 2), lambda b, h: (0, 0))
    blk = pl.BlockSpec((S, DK), lambda b, h: (b, h))
    return pl.pallas_call(
        body, name="ret_fwd", grid=(nb, HEADS),
        in_specs=[seg(2), seg(3), seg(4), seg(5),
                  pl.BlockSpec((None, 1, DK), lambda b, h: (h, 0, 0)),
                  tab, tab, tab, rot, rot],
        out_specs=[blk, blk, blk, blk,
                   pl.BlockSpec((None, None, NCH, DK, DK), lambda b, h: (b, h, 0, 0, 0))],
        out_shape=[jax.ShapeDtypeStruct((t, D), BF16), jax.ShapeDtypeStruct((t, D), BF16),
                   jax.ShapeDtypeStruct((t, D), BF16), jax.ShapeDtypeStruct((t, D), F32),
                   jax.ShapeDtypeStruct((nb, HEADS, NCH, DK, DK), BF16)],
        scratch_shapes=[pltpu.VMEM((DK, DK), F32)],
        compiler_params=_params(("arbitrary", "arbitrary")),
    )(proj, proj, proj, proj, gain, inner_t, cross_t, state_t, cos_t, sin_t)


def _ret_bwd(proj, qr, kr, o, rs, dyb, gain, tables, nb, sums):
    t = nb * S
    ns = len(sums)
    inner_t, cross_t, state_t, cos_t, sin_t = tables

    def body(qr_ref, kr_ref, v_ref, gb_ref, o_ref, dyb_ref, rs_ref, gain_ref, dm_ref, cd_ref, sd_ref,
             cos_ref, sin_ref, *rest):
        sum_refs, rest = rest[:ns], rest[ns:]
        dp_ref, dgain_ref = rest[:2]
        part_refs, rest = rest[2:2 + ns], rest[2 + ns:]
        dr_s, send_sems, recv_sems, local_sems = rest
        mine, sends, recvs = _chip_copies(sum_refs, part_refs, send_sems, recv_sems, local_sems)

        @pl.when(jnp.logical_and(pl.program_id(0) == 0, pl.program_id(1) == 0))
        def _():
            for cp in mine + sends:
                cp.start()

        dr_s[...] = jnp.zeros_like(dr_s)
        chunk_decay = cd_ref[CH - 1:CH, :]

        @pl.when(pl.program_id(1) == 0)
        def _():
            dgain_ref[...] = jnp.zeros_like(dgain_ref)

        def chunk(i, carry):
            c = NCH - 1 - i
            rows = _rows(c, CH)
            gain_v = gain_ref[...]
            o_c = o_ref[rows, :]
            oc = o_c - jnp.mean(o_c, axis=-1, keepdims=True)
            rstd = lax.rsqrt(jnp.mean(oc * oc, axis=-1, keepdims=True) + EPS)
            yn = oc * rstd
            gb = gb_ref[rows, :].astype(F32)
            sg = _sigmoid(gb)
            dyb_c = dyb_ref[rows, :]
            dgn = dyb_c * (gb * sg)
            dp_ref[3, rows, :] = (dyb_c * (yn * gain_v) * (sg * (1.0 + gb * (1.0 - sg)))).astype(BF16)
            dgain_ref[...] += jnp.sum(dgn * yn, axis=0, keepdims=True)
            dyn = dgn * gain_v
            do = rstd * (dyn - jnp.mean(dyn, axis=-1, keepdims=True)
                         - yn * jnp.mean(dyn * yn, axis=-1, keepdims=True))
            dob = do.astype(BF16)
            dox = (do * cd_ref[...]).astype(BF16)

            q_c, k_c = qr_ref[rows, :], kr_ref[rows, :]
            vb = v_ref[rows, :]
            v = vb.astype(F32)
            vs = (v * sd_ref[...]).astype(BF16)
            rb = rs_ref[c]
            d_r = dr_s[...]
            drb = d_r.astype(BF16)
            dm = dm_ref[...]
            p = (_dot_nt(q_c, k_c) * dm).astype(BF16)
            dpm = (_dot_nt(dob, vb) * dm).astype(BF16)
            dq = _dot(dpm, k_c) + _dot_nt(dox, rb)
            dk = _dot_tn(dpm, q_c) + _dot_nt(vs, drb)
            dv = _dot_tn(p, dob) + _dot(k_c, drb) * sd_ref[...]
            dr_s[...] = chunk_decay * d_r + _dot_tn(q_c, dox)

            cos, sin = cos_ref[rows, :], sin_ref[rows, :]
            dp_ref[0, rows, :] = _rotate_back(dq, cos, sin).astype(BF16)
            dp_ref[1, rows, :] = (_rotate_back(dk, cos, sin) * (DK ** -0.5)).astype(BF16)
            dp_ref[2, rows, :] = dv.astype(BF16)
            return carry

        lax.fori_loop(0, NCH, chunk, 0, unroll=2)

        @pl.when(jnp.logical_and(pl.program_id(0) == HEADS - 1, pl.program_id(1) == nb - 1))
        def _():
            for cp in recvs:
                cp.wait_recv()
            for cp in sends:
                cp.wait_send()
            for cp in mine:
                cp.wait()

    seg = lambda s: pl.BlockSpec((None, S, DK), lambda h, b: (s, b, h))
    tab = pl.BlockSpec((None, CH, DK), lambda h, b: (h, 0, 0))
    rot = pl.BlockSpec((S, DK // 2), lambda h, b: (0, 0))
    blk = pl.BlockSpec((S, DK), lambda h, b: (b, h))
    one = pl.BlockSpec((None, 1, DK), lambda h, b: (h, 0, 0))
    return pl.pallas_call(
        body, name="ret_bwd", grid=(HEADS, nb),
        in_specs=[blk, blk, seg(4), seg(5), blk, blk,
                  pl.BlockSpec((None, None, NCH, DK, DK), lambda h, b: (b, h, 0, 0, 0)),
                  one, tab, tab, tab, rot, rot] + [ANY] * ns,
        out_specs=[pl.BlockSpec((4, S, DK), lambda h, b: (0, b, h)), one] + [ANY] * ns,
        out_shape=[jax.ShapeDtypeStruct((4, t, D), BF16), jax.ShapeDtypeStruct((HEADS, 1, DK), F32)]
        + [jax.ShapeDtypeStruct(a.shape, a.dtype) for a in sums],
        scratch_shapes=[pltpu.VMEM((DK, DK), F32), pltpu.SemaphoreType.DMA((ns, 3)), pltpu.SemaphoreType.DMA((ns, 3)),
                        pltpu.SemaphoreType.DMA((ns,))],
        compiler_params=_params(("arbitrary", "arbitrary")),
    )(qr, kr, proj, proj, o, dyb, rs, gain, inner_t, cross_t, state_t, cos_t, sin_t, *sums)


def _wblock(k):
    return pl.BlockSpec((NDEV, D // NDEV, D), lambda i: (0, k, 0))


def _tail(ya, yb, proj, x2d, tgt, wg, g_fin):
    t = x2d.shape[0]
    tm = 256

    def body(ya_ref, yb_ref, ma_ref, mb_ref, x_ref, t_ref, wa_ref, wb_ref, wo_ref, g_ref,
             dx2_ref, dya_ref, dyb_ref, dm_ref, mg_ref, doa_ref, dob_ref, gfin_ref, loss_ref):
        i = pl.program_id(0)

        @pl.when(i == 0)
        def _():
            gfin_ref[...] = jnp.zeros_like(gfin_ref)
            loss_ref[...] = jnp.zeros_like(loss_ref)

        wa = wa_ref[...].reshape(D, D)
        wb = wb_ref[...].reshape(D, D)
        wo = wo_ref[...].reshape(D, D)
        out_a = _dot(ya_ref[...], wa)
        out_b = _dot(yb_ref[...], wb)
        sa = _sigmoid(ma_ref[...].astype(F32))
        sb = _sigmoid(mb_ref[...].astype(F32))
        merged = (sa * out_a + sb * out_b).astype(BF16)
        mg_ref[...] = merged
        x2 = x_ref[...] + _dot(merged, wo)
        r2 = lax.rsqrt(jnp.mean(x2 * x2, axis=-1, keepdims=True) + EPS)
        xh = x2 * r2
        g = g_ref[...]
        err = xh * g - t_ref[...]
        loss_ref[...] += jnp.sum(err * err, axis=0, keepdims=True) * (0.5 / D)
        dy = err * (1.0 / D)
        gfin_ref[...] += jnp.sum(dy * xh, axis=0, keepdims=True)
        dxh = dy * g
        dx2 = r2 * (dxh - xh * jnp.mean(dxh * xh, axis=-1, keepdims=True))
        dx2_ref[...] = dx2
        dmerged = _dot_nt(dx2.astype(BF16), wo)
        doa = (sa * dmerged).astype(BF16)
        dob = (sb * dmerged).astype(BF16)
        doa_ref[...] = doa
        dob_ref[...] = dob
        dm_ref[0] = (dmerged * out_a * sa * (1.0 - sa)).astype(BF16)
        dm_ref[1] = (dmerged * out_b * sb * (1.0 - sb)).astype(BF16)
        dya_ref[...] = _dot_nt(doa, wa)
        dyb_ref[...] = _dot_nt(dob, wb)

    row = lambda: pl.BlockSpec((tm, D), lambda i: (i, 0))
    seg = lambda s: pl.BlockSpec((None, tm, D), lambda i: (s, i, 0))
    vec = pl.BlockSpec((1, D), lambda i: (0, 0))
    return pl.pallas_call(
        body, name="tail", grid=(t // tm,),
        in_specs=[row(), row(), seg(6), seg(7), row(), row(), _wblock(0), _wblock(1), _wblock(2), vec],
        out_specs=[row(), row(), row(), pl.BlockSpec((2, tm, D), lambda i: (0, i, 0)),
                   row(), row(), row(), vec, vec],
        out_shape=[jax.ShapeDtypeStruct((t, D), F32), jax.ShapeDtypeStruct((t, D), F32),
                   jax.ShapeDtypeStruct((t, D), F32), jax.ShapeDtypeStruct((2, t, D), BF16),
                   jax.ShapeDtypeStruct((t, D), BF16), jax.ShapeDtypeStruct((t, D), BF16),
                   jax.ShapeDtypeStruct((t, D), BF16), jax.ShapeDtypeStruct((1, D), F32),
                   jax.ShapeDtypeStruct((1, D), F32)],
        compiler_params=_params(("arbitrary",)),
    )(ya, yb, proj, proj, x2d, tgt, wg, wg, wg, g_fin)


def _tail_wgrad(ya, yb, merged, doa, dob, dx2):
    t = ya.shape[0]
    tm = 512

    def body(ya_ref, yb_ref, mg_ref, doa_ref, dob_ref, dx2_ref, ga_ref, gb_ref, go_ref):
        @pl.when(pl.program_id(0) == 0)
        def _():
            ga_ref[...] = jnp.zeros_like(ga_ref)
            gb_ref[...] = jnp.zeros_like(gb_ref)
            go_ref[...] = jnp.zeros_like(go_ref)

        ga_ref[...] += _dot_tn(ya_ref[...], doa_ref[...])
        gb_ref[...] += _dot_tn(yb_ref[...], dob_ref[...])
        go_ref[...] += _dot_tn(mg_ref[...], dx2_ref[...].astype(BF16))

    row = lambda: pl.BlockSpec((tm, D), lambda i: (i, 0))
    full = lambda: pl.BlockSpec((D, D), lambda i: (0, 0))
    return pl.pallas_call(
        body, name="tail_wgrad", grid=(t // tm,),
        in_specs=[row() for _ in range(6)], out_specs=[full(), full(), full()],
        out_shape=[jax.ShapeDtypeStruct((D, D), F32)] * 3,
        compiler_params=_params(("arbitrary",)),
    )(ya, yb, merged, doa, dob, dx2)


def _dproj_specs(tm, j_of, i_of):
    last = lambda j, i, lo, n: (jnp.clip(j - lo, 0, n - 1), i, 0)
    return [pl.BlockSpec((None, tm, D), lambda a, b: last(j_of(a, b), i_of(a, b), 0, 2)),
            pl.BlockSpec((None, tm, D), lambda a, b: last(j_of(a, b), i_of(a, b), 2, 4)),
            pl.BlockSpec((None, tm, D), lambda a, b: last(j_of(a, b), i_of(a, b), 6, 2))]


def _dproj_specs_ordered(tm):
    def spec(lo, n):
        def index(k, i, order_ref):
            seg = order_ref[k]
            mine = jnp.logical_and(seg >= lo, seg < lo + n)
            return jnp.where(mine, seg - lo, 0), jnp.where(mine, i, 0), 0
        return pl.BlockSpec((None, tm, D), index)
    return [spec(0, 2), spec(2, 4), spec(6, 2)]


def _dproj_pick(j, da_ref, db_ref, dc_ref, use):
    @pl.when(j < 2)
    def _():
        use(da_ref[...])

    @pl.when(jnp.logical_and(j >= 2, j < 6))
    def _():
        use(db_ref[...])

    @pl.when(j >= 6)
    def _():
        use(dc_ref[...])


RS_X, RS_Y, RS_XY = 0, 1, 2
RS_ROLES = ((RS_XY, RS_X, RS_Y), (RS_Y, RS_XY, RS_X))


def _rs_flip(rel, x, y):
    return ((1 - x, y), (x, 1 - y), (1 - x, 1 - y))[rel]


def _rs_order(x, y, c):
    order = []
    for s in range(4):
        chip = []
        for core in (0, 1):
            px, py = _rs_flip(RS_ROLES[core][s], x, y) if s < 3 else (x, y)
            chip.append(2 * px + py)
        keep = jnp.where(c == 0, chip[0], chip[1])
        give = jnp.where(c == 0, chip[1], chip[0])
        order += [2 * give + 1 - c, 2 * keep + c]
    return jnp.stack(order).astype(jnp.int32)


def _inproj_wgrad_rs(h, dpa, dpb, dpc, order, smalls):
    t = h.shape[0]
    tm = 1024
    nt = t // tm
    nsm = len(smalls)

    def body(order_ref, h_ref, da_ref, db_ref, dc_ref, *rest):
        small_refs, parts_ref, rest = rest[:nsm], rest[nsm], rest[nsm + 1:]
        all_refs, rest = rest[:nsm], rest[nsm:]
        (acc, sib, outb, far, give_send, give_recv, sum_send, sum_recv, far_send, far_recv, own_sem,
         small_send, small_recv, small_own) = rest
        k, i = pl.program_id(0), pl.program_id(1)
        x, y, c = _place()
        own, first, arrive, forward, others = _gather_copies(small_refs, all_refs, small_send, small_recv, small_own)

        @pl.when(jnp.logical_and(k == 0, i == 0))
        def _():
            for cp in own + first:
                cp.start()

        @pl.when(jnp.logical_and(k == 2, i == 0))
        def _():
            for came, on in zip(arrive, forward):
                came.wait_recv()
                on.start()

        def use(d):
            @pl.when(i == 0)
            def _():
                acc[k % 2] = _dot_tn(h_ref[...], d)

            @pl.when(i > 0)
            def _():
                acc[k % 2] += _dot_tn(h_ref[...], d)

        _dproj_pick(order_ref[k], da_ref, db_ref, dc_ref, use)

        def give_copy(s):
            return pltpu.make_async_remote_copy(
                src_ref=acc.at[0], dst_ref=sib.at[s % 2], send_sem=give_send.at[s], recv_sem=give_recv.at[s],
                device_id=(x, y, 1 - c), device_id_type=MESH)

        def sum_copy(s, core):
            slot = 0 if s < 2 else 1
            return pltpu.make_async_remote_copy(
                src_ref=outb.at[s], dst_ref=parts_ref.at[slot], send_sem=sum_send.at[slot], recv_sem=sum_recv.at[slot],
                device_id=(*_rs_flip(RS_ROLES[core][s], x, y), core), device_id_type=MESH)

        def far_copy(s, core):
            return pltpu.make_async_remote_copy(
                src_ref=outb.at[s], dst_ref=far, send_sem=far_send, recv_sem=far_recv,
                device_id=(*_rs_flip(RS_X if core == 0 else RS_Y, x, y), core), device_id_type=MESH)

        own_copy = pltpu.make_async_copy(outb.at[3], parts_ref.at[2], own_sem)

        def send_of(core, s):
            return far_copy(s, core) if RS_ROLES[core][s] == RS_XY else sum_copy(s, core)

        for s in range(4):
            @pl.when(jnp.logical_and(k == 2 * s, i == nt - 1))
            def _():
                give_copy(s).start()

            @pl.when(jnp.logical_and(k == 2 * s + 1, i == nt - 1))
            def _():
                give_copy(s).wait_recv()
                if s == 2:
                    far_copy(s, 0).wait_recv()
                    outb[s] = (acc[1] + sib[s % 2] + far[...].astype(F32)).astype(BF16)
                else:
                    outb[s] = (acc[1] + sib[s % 2]).astype(BF16)
                give_copy(s).wait_send()
                if s < 3:
                    for core in (0, 1):
                        @pl.when(c == core)
                        def _():
                            send_of(core, s).start()
                else:
                    own_copy.start()

        @pl.when(jnp.logical_and(k == NSEG - 1, i == nt - 1))
        def _():
            for slot in (0, 1):
                sum_copy(2 * slot, 0).wait_recv()
            for s in range(3):
                send_of(0, s).wait_send()
            own_copy.wait()
            for cp in others:
                cp.wait_recv()
            for cp in first + forward:
                cp.wait_send()
            for cp in own:
                cp.wait()

    return pl.pallas_call(
        body, name="inproj_wgrad_rs",
        grid_spec=pltpu.PrefetchScalarGridSpec(
            num_scalar_prefetch=1, grid=(NSEG, nt),
            in_specs=[pl.BlockSpec((tm, D), lambda k, i, order_ref: (i, 0))] + _dproj_specs_ordered(tm) + [ANY] * nsm,
            out_specs=[ANY] * (1 + nsm),
            scratch_shapes=[pltpu.VMEM((2, D, D), F32), pltpu.VMEM((2, D, D), F32), pltpu.VMEM((4, D, D), BF16),
                            pltpu.VMEM((D, D), BF16),
                            pltpu.SemaphoreType.DMA((4,)), pltpu.SemaphoreType.DMA((4,)),
                            pltpu.SemaphoreType.DMA((2,)), pltpu.SemaphoreType.DMA((2,)),
                            pltpu.SemaphoreType.DMA, pltpu.SemaphoreType.DMA, pltpu.SemaphoreType.DMA,
                            pltpu.SemaphoreType.DMA((nsm, 7)), pltpu.SemaphoreType.DMA((nsm, 7)),
                            pltpu.SemaphoreType.DMA((nsm,))]),
        out_shape=[jax.ShapeDtypeStruct((3, D, D), BF16)]
        + [jax.ShapeDtypeStruct((NDEV,) + a.shape, a.dtype) for a in smalls],
        compiler_params=_params(("arbitrary", "arbitrary")),
    )(order, h, dpa, dpb, dpc, *smalls)


def _inproj_dgrad(dpa, dpb, dpc, wg, x2d, dx2, g_in):
    t = x2d.shape[0]
    tm = 1024

    def body(da_ref, db_ref, dc_ref, w_ref, x_ref, dx2_ref, g_ref, gx_ref, gg_ref, acc_s):
        i, j = pl.program_id(0), pl.program_id(1)

        @pl.when(jnp.logical_and(i == 0, j == 0))
        def _():
            gg_ref[...] = jnp.zeros_like(gg_ref)

        @pl.when(j == 0)
        def _():
            acc_s[...] = jnp.zeros_like(acc_s)

        def use(d):
            acc_s[...] += _dot_nt(d, w_ref[...])

        _dproj_pick(j, da_ref, db_ref, dc_ref, use)

        @pl.when(j == NSEG - 1)
        def _():
            x = x_ref[...]
            r = lax.rsqrt(jnp.mean(x * x, axis=-1, keepdims=True) + EPS)
            xh = x * r
            dh = acc_s[...]
            gg_ref[...] += jnp.sum(dh * xh, axis=0, keepdims=True)
            dxh = dh * g_ref[...]
            gx_ref[...] = dx2_ref[...] + r * (dxh - xh * jnp.mean(dxh * xh, axis=-1, keepdims=True))

    row = lambda: pl.BlockSpec((tm, D), lambda i, j: (i, 0))
    vec = pl.BlockSpec((1, D), lambda i, j: (0, 0))
    return pl.pallas_call(
        body, name="inproj_dgrad", grid=(t // tm, NSEG),
        in_specs=_dproj_specs(tm, lambda i, j: j, lambda i, j: i)
        + [pl.BlockSpec((None, D, D), lambda i, j: (j, 0, 0)), row(), row(), vec],
        out_specs=[row(), vec],
        out_shape=[jax.ShapeDtypeStruct((t, D), F32), jax.ShapeDtypeStruct((1, D), F32)],
        scratch_shapes=[pltpu.VMEM((tm, D), F32)],
        compiler_params=_params(("arbitrary", "arbitrary")),
    )(dpa, dpb, dpc, wg, x2d, dx2, g_in)


def _adam_update(g, w, m, v):
    m_new = ADAM_B1 * m + (1.0 - ADAM_B1) * g
    v_new = ADAM_B2 * v + (1.0 - ADAM_B2) * (g * g)
    m_hat = m_new / (1.0 - ADAM_B1 ** ADAM_STEP)
    v_hat = v_new / (1.0 - ADAM_B2 ** ADAM_STEP)
    return -ADAM_LR * (m_hat / (jnp.sqrt(v_hat) + ADAM_EPS) + ADAM_WD * w), m_new, v_new


def _sum_in_order(ref):
    total = ref[0].astype(F32)
    for k in range(1, ref.shape[0]):
        total = total + ref[k].astype(F32)
    return total


def _adamw_small(me, vec_all, gx_all, ga_all, groups):
    flat = [a for grp in groups for a in grp]
    ng = len(groups)
    nshard = D // NDEV

    def body(me_ref, vec_ref, shard_ref, gx_ref, ga_ref, *refs):
        ins, outs = refs[:3 * ng], refs[3 * ng:]
        vec = _sum_in_order(vec_ref)
        shard = _sum_in_order(shard_ref)
        grads = [vec[r:r + 1, :] for r in range(6)]
        grads += [shard[0:4, :], shard[4:8, 0:DK // NDEV], _sum_in_order(gx_ref), _sum_in_order(ga_ref)]
        for n, g in enumerate(grads):
            delta, m_new, v_new = _adam_update(g, ins[3 * n][...], ins[3 * n + 1][...], ins[3 * n + 2][...])
            outs[4 * n][...] = g
            outs[4 * n + 1][...] = delta
            outs[4 * n + 2][...] = m_new
            outs[4 * n + 3][...] = v_new
        outs[4 * ng][...] = jnp.sum(vec[6:7, :], axis=1, keepdims=True)

    full = lambda a: pl.BlockSpec(a.shape, lambda i, me_ref, nd=len(a.shape): (0,) * nd)
    out_shape = [jax.ShapeDtypeStruct(w.shape, F32) for w, _, _ in groups for _ in range(4)]
    out_shape.append(jax.ShapeDtypeStruct((1, 1), F32))
    outs = pl.pallas_call(
        body, name="adamw_small",
        grid_spec=pltpu.PrefetchScalarGridSpec(
            num_scalar_prefetch=1, grid=(1,),
            in_specs=[full(vec_all),
                      pl.BlockSpec((NDEV, 8, nshard), lambda i, me_ref: (0, 1, me_ref[0])),
                      full(gx_all), full(ga_all)] + [full(a) for a in flat],
            out_specs=[full(s) for s in out_shape]),
        out_shape=out_shape,
        compiler_params=_params(("arbitrary",)),
    )(me, vec_all, vec_all, gx_all, ga_all, *flat)
    return [outs[4 * n:4 * n + 4] for n in range(ng)], outs[4 * ng]


def _adamw(name, items):
    n, rows, cols = items[0][0].shape
    tr = rows if rows <= 256 else 256
    k = len(items)

    def body(*refs):
        for a in range(k):
            p_ref, w_ref, m_ref, v_ref = refs[4 * a:4 * a + 4]
            g = _sum_in_order(p_ref)
            delta, m_new, v_new = _adam_update(g, w_ref[...], m_ref[...], v_ref[...])
            for o, val in zip(refs[4 * k + 4 * a:4 * k + 4 * a + 4], (g, delta, m_new, v_new)):
                o[...] = val

    blk = lambda: pl.BlockSpec((tr, cols), lambda i: (i, 0))
    outs = pl.pallas_call(
        body, name=name, grid=(rows // tr,),
        in_specs=[pl.BlockSpec((n, tr, cols), lambda i: (0, i, 0)), blk(), blk(), blk()] * k,
        out_specs=[blk() for _ in range(4 * k)],
        out_shape=[jax.ShapeDtypeStruct((rows, cols), F32)] * (4 * k),
        compiler_params=_params(("arbitrary",)),
    )(*[a for item in items for a in item])
    return [outs[4 * a:4 * a + 4] for a in range(k)]


ANY = pl.BlockSpec(memory_space=pl.ANY)


def _place():
    return lax.axis_index("x"), lax.axis_index("y"), lax.axis_index("c")


def _gather_copies(ins, outs, send_sems, recv_sems, own_sems):
    x, y, c = _place()
    me, sibling = (x, y, c), (x, y, 1 - c)
    chips = [(1 - x, y), (x, 1 - y), (1 - x, 1 - y)]
    n = len(ins)

    def copy(a, k, block, to, src=None):
        px, py, pc = block
        dst = outs[a].at[4 * px + 2 * py + pc]
        return pltpu.make_async_remote_copy(
            src_ref=dst if src is None else src, dst_ref=dst,
            send_sem=send_sems.at[a, k], recv_sem=recv_sems.at[a, k], device_id=to, device_id_type=MESH)

    own = [pltpu.make_async_copy(ins[a], outs[a].at[4 * x + 2 * y + c], own_sems.at[a]) for a in range(n)]
    first = []
    for a in range(n):
        first.append(copy(a, 0, me, sibling, src=ins[a]))
        first += [copy(a, 1 + j, me, (*chip, c), src=ins[a]) for j, chip in enumerate(chips)]
    arrive = [copy(a, 1 + j, (*chip, c), me) for j, chip in enumerate(chips) for a in range(n)]
    forward = [copy(a, 4 + j, (*chip, c), sibling) for j, chip in enumerate(chips) for a in range(n)]
    rest = [copy(a, 0, sibling, me) for a in range(n)]
    rest += [copy(a, 4 + j, (*chip, 1 - c), me) for a in range(n) for j, chip in enumerate(chips)]
    return own, first, arrive, forward, rest


def _sibling_copies(ins, outs, send_sems, recv_sems):
    x, y, c = _place()
    return [pltpu.make_async_remote_copy(
        src_ref=ins[a].at[2 * q + 1 - c], dst_ref=outs[a].at[q],
        send_sem=send_sems.at[a, q], recv_sem=recv_sems.at[a, q],
        device_id=(x, y, 1 - c), device_id_type=MESH) for a in range(len(ins)) for q in range(4)]


def _chip_copies(ins, outs, send_sems, recv_sems, local_sems):
    x, y, c = _place()
    my_chip = 2 * x + y
    chips = [(1 - x, y), (x, 1 - y), (1 - x, 1 - y)]
    n = len(ins)
    mine = [pltpu.make_async_copy(ins[a].at[my_chip], outs[a].at[my_chip], local_sems.at[a]) for a in range(n)]
    sends = [pltpu.make_async_remote_copy(
        src_ref=ins[a].at[2 * px + py], dst_ref=outs[a].at[my_chip],
        send_sem=send_sems.at[a, j], recv_sem=recv_sems.at[a, j],
        device_id=(px, py, c), device_id_type=MESH) for a in range(n) for j, (px, py) in enumerate(chips)]
    recvs = [pltpu.make_async_remote_copy(
        src_ref=ins[a].at[my_chip], dst_ref=outs[a].at[2 * px + py],
        send_sem=send_sems.at[a, j], recv_sem=recv_sems.at[a, j],
        device_id=(px, py, c), device_id_type=MESH) for a in range(n) for j, (px, py) in enumerate(chips)]
    return mine, sends, recvs


def _chip_sum(owns, gots, core):
    n = len(owns)
    _, rows, cols = owns[0].shape

    def body(core_ref, *refs):
        for a in range(n):
            refs[2 * n + a][...] = (refs[a][...] + refs[n + a][...]).astype(BF16)

    own_spec = pl.BlockSpec((None, rows, cols), lambda q, core_ref: (2 * q + core_ref[0], 0, 0))
    slab = pl.BlockSpec((None, rows, cols), lambda q, core_ref: (q, 0, 0))
    return pl.pallas_call(
        body, name="chip_sum",
        grid_spec=pltpu.PrefetchScalarGridSpec(
            num_scalar_prefetch=1, grid=(4,),
            in_specs=[own_spec] * n + [slab] * n, out_specs=[slab] * n),
        out_shape=[jax.ShapeDtypeStruct((4, rows, cols), BF16)] * n,
        compiler_params=_params(("arbitrary",)),
    )(core, *owns, *gots)


def _block_diag(w):
    w4 = w.reshape(NCB, 4, 64, 64)
    eye = jnp.eye(4, dtype=w.dtype)
    return (w4[:, :, :, None, :] * eye[None, :, None, :, None]).reshape(NCB, CB, CB)


def _block_diag_back(g):
    g5 = g.reshape(NCB, 4, 64, 4, 64)
    return jnp.stack([g5[:, m, :, m, :] for m in range(4)], axis=1).reshape(16, 64, 64)


def kernel(x, norm_in, w_in, conv_w, conv_b, gate_x_w, gate_x_b, gate_a_w, gate_a_b, lru_lambda, gn_gain, w_proj_a, w_proj_b, w_out, norm_final, loss_target, m_norm_in, m_w_in, m_conv_w, m_conv_b, m_gate_x_w, m_gate_x_b, m_gate_a_w, m_gate_a_b, m_lru_lambda, m_gn_gain, m_w_proj_a, m_w_proj_b, m_w_out, m_norm_final, v_norm_in, v_w_in, v_conv_w, v_conv_b, v_gate_x_w, v_gate_x_b, v_gate_a_w, v_gate_a_b, v_lru_lambda, v_gn_gain, v_w_proj_a, v_w_proj_b, v_w_out, v_norm_final):
    xi, yi, ci = _place()
    me = 4 * xi + 2 * yi + ci
    core = ci.astype(jnp.int32).reshape(1)
    nshard = D // NDEV
    nb = x.shape[0]
    t = nb * S
    x2d = x.reshape(t, D)
    tgt2d = loss_target.reshape(t, D)
    g_final = norm_final.reshape(1, D)
    wbd = jnp.concatenate([_block_diag(gate_x_w[0]), _block_diag(gate_a_w[0])], axis=-1).astype(BF16)
    tables = _retention_tables()

    wp_own = jnp.concatenate([w_proj_a[0], w_proj_b[0], w_out[0]], axis=0).astype(BF16)
    tiny = jnp.concatenate([conv_w[0], jnp.pad(gn_gain[0], ((0, 0), (0, nshard - DK // NDEV)))], axis=0)
    proj, h, wg, tiny_g, wpg = _inproj_gather(x2d, norm_in, w_in[0].astype(BF16), [tiny, wp_own],
                                              *_gather_order(xi, yi, ci))
    conv_w_full = tiny_g[:, 0:4, :].transpose(1, 0, 2).reshape(4, D)
    gain3 = tiny_g[:, 4:8, :DK // NDEV].transpose(1, 0, 2).reshape(HEADS, 1, DK)

    ya, hs, xc, gi, gr = _lru_fwd(proj, conv_w_full, conv_b, wbd, gate_x_b, gate_a_b, lru_lambda, nb)
    yb, qr, kr, o, rs = _ret_fwd(proj, gain3, tables, nb)
    dx2, dya, dyb, dpc, merged, doa, dob, g_fin, loss_vec = _tail(ya, yb, proj, x2d, tgt2d, wpg, g_final)
    g_pa, g_pb, g_out = _tail_wgrad(ya, yb, merged, doa, dob, dx2)

    own = [g.reshape(NDEV, nshard, D) for g in (g_pa, g_pb, g_out)]
    dpa, g_wbd, g_vec, *got = _lru_bwd(proj, hs, xc, gi, gr, dya, conv_w_full, wbd, lru_lambda, nb, own)
    sums = _chip_sum(own, got, core)
    dpb, g_gain, *parts = _ret_bwd(proj, qr, kr, o, rs, dyb, gain3, tables, nb, sums)

    grad_x, g_norm_in = _inproj_dgrad(dpa, dpb, dpc, wg, x2d, dx2, norm_in)
    grad_x = grad_x.reshape(nb, S, D)

    gain_rows = jnp.pad(g_gain.reshape(HEADS, NDEV, DK // NDEV), ((0, 0), (0, 0), (0, nshard - DK // NDEV)))
    vec = jnp.concatenate([g_norm_in, g_vec[0:4], g_fin, loss_vec, jnp.zeros((1, D), F32), g_vec[4:8],
                           gain_rows.reshape(HEADS, D)], axis=0)
    g_gx = _block_diag_back(g_wbd[:, :, :CB]).reshape(D // 2, 128)
    g_ga = _block_diag_back(g_wbd[:, :, CB:]).reshape(D // 2, 128)
    parts_in, vec_all, gx_all, ga_all = _inproj_wgrad_rs(h, dpa, dpb, dpc, _rs_order(xi, yi, ci),
                                                         [vec, g_gx, g_ga])
    gx_all = gx_all.reshape(NDEV, D, 64)
    ga_all = ga_all.reshape(NDEV, D, 64)
    parts = [parts_in] + list(parts)

    res = {}
    (out,) = _adamw("adamw_w_in", [(parts[0], w_in[0], m_w_in[0], v_w_in[0])])
    res["w_in"] = [o[None] for o in out]
    square = [("w_proj_a", w_proj_a, m_w_proj_a, v_w_proj_a), ("w_proj_b", w_proj_b, m_w_proj_b, v_w_proj_b),
              ("w_out", w_out, m_w_out, v_w_out)]
    outs = _adamw("adamw_square", [(parts[1 + k], w[0], m[0], v[0]) for k, (_, w, m, v) in enumerate(square)])
    for (nm, _, _, _), out in zip(square, outs):
        res[nm] = [o[None] for o in out]

    row = lambda a: a.reshape(1, D)
    gate = lambda a: a.reshape(D, 64)
    groups = [("norm_in", norm_in, m_norm_in, v_norm_in, row), ("conv_b", conv_b, m_conv_b, v_conv_b, row),
              ("gate_x_b", gate_x_b, m_gate_x_b, v_gate_x_b, row), ("gate_a_b", gate_a_b, m_gate_a_b, v_gate_a_b, row),
              ("lru_lambda", lru_lambda, m_lru_lambda, v_lru_lambda, row),
              ("norm_final", norm_final, m_norm_final, v_norm_final, row),
              ("conv_w", conv_w, m_conv_w, v_conv_w, lambda a: a[0]), ("gn_gain", gn_gain, m_gn_gain, v_gn_gain, lambda a: a[0]),
              ("gate_x_w", gate_x_w, m_gate_x_w, v_gate_x_w, gate), ("gate_a_w", gate_a_w, m_gate_a_w, v_gate_a_w, gate)]
    small_out, loss = _adamw_small(me.astype(jnp.int32).reshape(1), vec_all, gx_all, ga_all,
                                   [tuple(view(a) for a in (w, m, v)) for _, w, m, v, view in groups])
    for (nm, w, _, _, _), out in zip(groups, small_out):
        res[nm] = [o.reshape(w.shape) for o in out]
    loss = loss.reshape(())

    order = ["norm_in", "w_in", "conv_w", "conv_b", "gate_x_w", "gate_x_b", "gate_a_w", "gate_a_b", "lru_lambda",
             "gn_gain", "w_proj_a", "w_proj_b", "w_out", "norm_final"]
    outs = [loss, grad_x]
    for k in range(4):
        outs += [res[nm][k] for nm in order]
    return tuple(outs)
```

```python
import numpy as np

import jax
import jax.numpy as jnp
from jax import lax
from jax.experimental import pallas as pl
from jax.experimental.pallas import tpu as pltpu

F32 = jnp.float32
BF16 = jnp.bfloat16
MESH = pl.DeviceIdType.MESH

D = 1024
S = 2048
NSEG = 8
NDEV = 8
HEADS = 4
DK = 256
CH = 256
NCH = S // CH
CB = 256
NCB = D // CB
RC = 512
RC_CONV = 128
SCAN_GROUP = 8
EPS = 1e-6
LRU_C = 8.0
VMEM_LIMIT = 56 * 1024 * 1024

ADAM_LR = 0.001
ADAM_B1 = 0.9
ADAM_B2 = 0.999
ADAM_EPS = 1e-08
ADAM_WD = 0.01
ADAM_STEP = 10


def _params(sem=None):
    return pltpu.CompilerParams(dimension_semantics=sem, vmem_limit_bytes=VMEM_LIMIT)


def _dot(a, b):
    return jnp.dot(a, b, preferred_element_type=F32)


def _dot_nt(a, b):
    return lax.dot_general(a, b, (((1,), (1,)), ((), ())), preferred_element_type=F32)


def _dot_tn(a, b):
    return lax.dot_general(a, b, (((0,), (0,)), ((), ())), preferred_element_type=F32)


def _sigmoid(x):
    return jax.nn.sigmoid(x)


def _expm1_nonpos(x):
    poly = x * (1.0 + x * (0.5 + x * (1.0 / 6.0 + x * (1.0 / 24.0))))
    return jnp.where(x > -0.05, poly, jnp.exp(x) - 1.0)


def _softplus(x):
    return jnp.maximum(x, 0.0) + jnp.log(1.0 + jnp.exp(-jnp.abs(x)))


def _rows(c, n):
    return pl.ds(pl.multiple_of(c * n, n), n)


def _window_before(ref, c, n):
    r0 = c * n
    if ref.dtype == BF16:
        prev = ref[pl.ds(pl.multiple_of(jnp.maximum(r0 - 16, 0), 16), 16), :].astype(F32)[8:, :]
    else:
        prev = ref[pl.ds(pl.multiple_of(jnp.maximum(r0 - 8, 0), 8), 8), :]
    prev = jnp.where(c > 0, prev, 0.0)
    return jnp.concatenate([prev, ref[_rows(c, n), :].astype(F32)], axis=0)


def _shift_down(win, s, n):
    if s == 0:
        return win[8:, :]
    return pltpu.roll(win, s, 0)[8:, :]


def _shift_up(win, s, n):
    if s == 0:
        return win[:n, :]
    return pltpu.roll(win, n + 8 - s, 0)[:n, :]


HALF = D // 2
GATHER_SLOTS = [("own", None, 0), ("own", None, 1), ("sib", None, 0), ("sib", None, 1)]
for _j, _h in ((0, 0), (1, 0), (0, 1), (1, 1), (2, 0), (2, 1)):
    GATHER_SLOTS += [("ici", _j, _h), ("fwd", _j, _h)]
NSLOT = len(GATHER_SLOTS)


def _gather_order(x, y, c):
    chips = [(1 - x, y), (x, 1 - y), (1 - x, 1 - y)]
    segs, halves = [], []
    for kind, j, h in GATHER_SLOTS:
        if kind == "own":
            seg = 4 * x + 2 * y + c
        elif kind == "sib":
            seg = 4 * x + 2 * y + 1 - c
        else:
            px, py = chips[j]
            seg = 4 * px + 2 * py + (c if kind == "ici" else 1 - c)
        segs.append(seg)
        halves.append(h)
    return jnp.stack(segs).astype(jnp.int32), jnp.asarray(halves, jnp.int32)


def _inproj_gather(x2d, g_in, w_own, tiny_own, order, halves):
    t = x2d.shape[0]
    tm = 1024
    nt = t // tm

    def body(order_ref, half_ref, x_ref, g_ref, w_own_ref, tiny_own_ref,
             proj_ref, h_ref, wg_ref, tinyg_ref,
             w_all, h_all, send_sems, recv_sems, own_sems, out_sems, tiny_send, tiny_recv, tiny_own_sem):
        k, i = pl.program_id(0), pl.program_id(1)
        x, y, c = _place()
        me, sibling = (x, y, c), (x, y, 1 - c)
        mine = 4 * x + 2 * y + c
        chips = [(1 - x, y), (x, 1 - y), (1 - x, 1 - y)]

        def copy(h, n, block, to, own_src=False):
            px, py, pc = block
            dst = w_all.at[4 * px + 2 * py + pc, h]
            return pltpu.make_async_remote_copy(
                src_ref=w_own_ref.at[:, pl.ds(h * HALF, HALF)] if own_src else dst, dst_ref=dst,
                send_sem=send_sems.at[h, n], recv_sem=recv_sems.at[h, n], device_id=to, device_id_type=MESH)

        def tiny_copy(n, block, to, own_src=False):
            px, py, pc = block
            dst = tinyg_ref.at[4 * px + 2 * py + pc]
            return pltpu.make_async_remote_copy(
                src_ref=tiny_own_ref if own_src else dst, dst_ref=dst,
                send_sem=tiny_send.at[n], recv_sem=tiny_recv.at[n], device_id=to, device_id_type=MESH)

        def own_copy(h):
            return pltpu.make_async_copy(w_own_ref.at[:, pl.ds(h * HALF, HALF)], w_all.at[mine, h], own_sems.at[h])

        tiny_mine = pltpu.make_async_copy(tiny_own_ref, tinyg_ref.at[mine], tiny_own_sem)

        def keep_copy(n):
            h = GATHER_SLOTS[n][2]
            return pltpu.make_async_copy(w_all.at[order_ref[n], h], wg_ref.at[order_ref[n], :, pl.ds(h * HALF, HALF)],
                                         out_sems.at[n])

        near = [(0, sibling), (1, (*chips[0], c)), (2, (*chips[1], c))]
        first = [copy(h, n, me, to, True) for h in (0, 1) for n, to in near]
        tiny_first = [tiny_copy(0, me, sibling, True)] + [tiny_copy(1 + j, me, (*chip, c), True) for j, chip in enumerate(chips)]

        def relay(h, j):
            seg = w_all.at[4 * chips[j][0] + 2 * chips[j][1] + c, h]
            return pltpu.make_async_remote_copy(
                src_ref=seg, dst_ref=seg, send_sem=send_sems.at[h, 3], recv_sem=recv_sems.at[h, 3],
                device_id=(*chips[1 - j], c), device_id_type=MESH)

        for n, (kind, j, h) in enumerate(GATHER_SLOTS):
            @pl.when(jnp.logical_and(k == n, i == 0))
            def _():
                if n == 0:
                    own_copy(0).start()
                    own_copy(1).start()
                    tiny_mine.start()
                    for cp in first + tiny_first:
                        cp.start()
                if kind == "own":
                    own_copy(h).wait()
                elif kind == "sib":
                    copy(h, 0, sibling, me).wait_recv()
                elif kind == "ici":
                    copy(h, 1 + j, (*chips[j], c), me).wait_recv()
                    copy(h, 4 + j, (*chips[j], c), sibling).start()
                    if j < 2:
                        @pl.when(c == j)
                        def _():
                            relay(h, j).start()
                else:
                    copy(h, 4 + j, (*chips[j], 1 - c), me).wait_recv()
                keep_copy(n).start()

        rows = pl.ds(pl.multiple_of(i * tm, tm), tm)

        @pl.when(k == 0)
        def _():
            xv = x_ref[...]
            r = lax.rsqrt(jnp.mean(xv * xv, axis=-1, keepdims=True) + EPS)
            hv = (xv * r * g_ref[...]).astype(BF16)
            h_ref[...] = hv
            h_all[rows, :] = hv

        proj_ref[...] = _dot(h_all[rows, :], w_all[order_ref[k], half_ref[k]]).astype(BF16)

        @pl.when(jnp.logical_and(k == NSLOT - 1, i == nt - 1))
        def _():
            for j, chip in enumerate(chips):
                tiny_copy(1 + j, (*chip, c), me).wait_recv()
                tiny_copy(4 + j, (*chip, c), sibling).start()
            tiny_copy(0, sibling, me).wait_recv()
            for j, chip in enumerate(chips):
                tiny_copy(4 + j, (*chip, 1 - c), me).wait_recv()
            for cp in first + tiny_first:
                cp.wait_send()
            for j, chip in enumerate(chips):
                tiny_copy(4 + j, (*chip, c), sibling).wait_send()
                for h in (0, 1):
                    copy(h, 4 + j, (*chip, c), sibling).wait_send()
            for h in (0, 1):
                relay(h, 0).wait_send()
            tiny_mine.wait()
            for n in range(NSLOT):
                keep_copy(n).wait()

    hold = lambda k, i, order_ref, half_ref: (jnp.where(k == 0, i, nt - 1), 0)
    return pl.pallas_call(
        body, name="inproj_gather",
        grid_spec=pltpu.PrefetchScalarGridSpec(
            num_scalar_prefetch=2, grid=(NSLOT, nt),
            in_specs=[pl.BlockSpec((tm, D), hold),
                      pl.BlockSpec((1, D), lambda k, i, order_ref, half_ref: (0, 0)),
                      ANY, ANY],
            out_specs=[pl.BlockSpec((None, tm, HALF), lambda k, i, order_ref, half_ref: (order_ref[k], i, half_ref[k])),
                       pl.BlockSpec((tm, D), hold),
                       ANY, ANY],
            scratch_shapes=[pltpu.VMEM((NDEV, 2, D, HALF), BF16), pltpu.VMEM((t, D), BF16),
                            pltpu.SemaphoreType.DMA((2, 7)), pltpu.SemaphoreType.DMA((2, 7)),
                            pltpu.SemaphoreType.DMA((2,)), pltpu.SemaphoreType.DMA((NSLOT,)),
                            pltpu.SemaphoreType.DMA((7,)), pltpu.SemaphoreType.DMA((7,)), pltpu.SemaphoreType.DMA]),
        out_shape=[jax.ShapeDtypeStruct((NSEG, t, D), BF16), jax.ShapeDtypeStruct((t, D), BF16),
                   jax.ShapeDtypeStruct((NDEV,) + w_own.shape, BF16),
                   jax.ShapeDtypeStruct((NDEV,) + tiny_own.shape, F32)],
        compiler_params=_params(("arbitrary", "arbitrary")),
    )(order, halves, x2d, g_in, w_own, tiny_own)


def _tile_scan(a, u):
    row = lax.broadcasted_iota(jnp.int32, a.shape, 0)
    for d in (1, 2, 4):
        m = row >= d
        a_sh = pltpu.roll(a, d, 0)
        u_sh = pltpu.roll(u, d, 0)
        u = jnp.where(m, a * u_sh + u, u)
        a = jnp.where(m, a * a_sh, a)
    return a, u


def _tile_scan_rev(a, w):
    row = lax.broadcasted_iota(jnp.int32, a.shape, 0)
    for d in (1, 2, 4):
        m = row < 8 - d
        a_sh = pltpu.roll(a, 8 - d, 0)
        w_sh = pltpu.roll(w, 8 - d, 0)
        w = jnp.where(m, a * w_sh + w, w)
        a = jnp.where(m, a * a_sh, a)
    return a, w


def _lru_gates(xa_ref, c, cw_ref, cb_ref, wbd_ref, bx_ref, ba_ref, sp):
    win = _window_before(xa_ref, c, RC)
    xc = cb_ref[...] + cw_ref[3:4, :] * _shift_down(win, 0, RC)
    for s in (1, 2, 3):
        xc = xc + cw_ref[3 - s:4 - s, :] * _shift_down(win, s, RC)
    z = _dot(xc.astype(BF16), wbd_ref[...])
    gi = _sigmoid(z[:, :CB] + bx_ref[...])
    gr = _sigmoid(z[:, CB:] + ba_ref[...])
    log_a = -LRU_C * gr * sp
    return win, xc, gi, gr, log_a


def _lru_fwd(proj, conv_w, conv_b, wbd, bx, ba, lam, nb):
    t = nb * S

    def body(xa_ref, ga_ref, cw_ref, cb_ref, wbd_ref, bx_ref, ba_ref, lam_ref,
             ya_ref, hs_ref, xc_ref, gi_ref, gr_ref, a_s, u_s):
        sp = _softplus(-lam_ref[...])

        def gates(c, carry):
            _, xc, gi, gr, log_a = _lru_gates(xa_ref, c, cw_ref, cb_ref, wbd_ref, bx_ref, ba_ref, sp)
            rows = _rows(c, RC)
            a_s[rows, :] = jnp.exp(log_a)
            u_s[rows, :] = jnp.sqrt(-_expm1_nonpos(2.0 * log_a)) * (gi * xc)
            xc_ref[rows, :] = xc
            gi_ref[rows, :] = gi
            gr_ref[rows, :] = gr
            return carry

        lax.fori_loop(0, S // RC, gates, 0)

        def scan(g, h):
            for k in range(SCAN_GROUP):
                rows = pl.ds(pl.multiple_of(g * (8 * SCAN_GROUP), 8 * SCAN_GROUP) + 8 * k, 8)
                a_cum, u_cum = _tile_scan(a_s[rows, :], u_s[rows, :])
                hs_ref[rows, :] = u_cum + a_cum * h
                h = u_cum[7:8, :] + a_cum[7:8, :] * h
            return h

        lax.fori_loop(0, S // (8 * SCAN_GROUP), scan, jnp.zeros((1, CB), F32))

        def gate_out(c, carry):
            ga = ga_ref[_rows(c, RC), :].astype(F32)
            ya_ref[_rows(c, RC), :] = (ga * _sigmoid(ga) * hs_ref[_rows(c, RC), :]).astype(BF16)
            return carry

        lax.fori_loop(0, S // RC, gate_out, 0)

    vec = pl.BlockSpec((1, CB), lambda b, cb: (0, cb))
    blk = pl.BlockSpec((S, CB), lambda b, cb: (b, cb))
    return pl.pallas_call(
        body, name="lru_fwd", grid=(nb, NCB),
        in_specs=[pl.BlockSpec((None, S, CB), lambda b, cb: (0, b, cb)),
                  pl.BlockSpec((None, S, CB), lambda b, cb: (1, b, cb)),
                  pl.BlockSpec((4, CB), lambda b, cb: (0, cb)),
                  vec,
                  pl.BlockSpec((None, CB, 2 * CB), lambda b, cb: (cb, 0, 0)),
                  vec, vec, vec],
        out_specs=[blk] + [pl.BlockSpec((None, None, S, CB), lambda b, cb: (b, cb, 0, 0))] * 4,
        out_shape=[jax.ShapeDtypeStruct((t, D), BF16)] + [jax.ShapeDtypeStruct((nb, NCB, S, CB), F32)] * 4,
        scratch_shapes=[pltpu.VMEM((S, CB), F32), pltpu.VMEM((S, CB), F32)],
        compiler_params=_params(("arbitrary", "arbitrary")),
    )(proj, proj, conv_w, conv_b, wbd, bx, ba, lam)


def _lru_bwd(proj, hs, xc_f, gi_f, gr_f, dya, conv_w, wbd, lam, nb, give):
    t = nb * S
    ng = len(give)

    def body(xa_ref, ga_ref, hs_ref, xc_s, gi_s, gr_s, dya_ref, cw_ref, wbd_ref, lam_ref, *rest):
        give_refs, rest = rest[:ng], rest[ng:]
        dp_ref, dwbd_ref, vec_ref = rest[:3]
        got_refs, rest = rest[3:3 + ng], rest[3 + ng:]
        a_s, dl_s, dh_s, dxc_s, acc_s, send_sems, recv_sems = rest
        b = pl.program_id(1)
        exchange = _sibling_copies(give_refs, got_refs, send_sems, recv_sems)

        @pl.when(jnp.logical_and(pl.program_id(0) == 0, b == 0))
        def _():
            for cp in exchange:
                cp.start()

        lam_v = lam_ref[...]
        sp = _softplus(-lam_v)
        acc_s[...] = jnp.zeros_like(acc_s)

        @pl.when(b == 0)
        def _():
            dwbd_ref[...] = jnp.zeros_like(dwbd_ref)
            vec_ref[...] = jnp.zeros_like(vec_ref)

        def gates(c, carry):
            rows = _rows(c, RC)
            a_s[rows, :] = jnp.exp(-LRU_C * gr_s[rows, :] * sp)
            ga = ga_ref[rows, :].astype(F32)
            sg = _sigmoid(ga)
            dya_c = dya_ref[rows, :]
            dl_s[rows, :] = dya_c * (ga * sg)
            dp_ref[1, rows, :] = (dya_c * hs_ref[rows, :] * (sg * (1.0 + ga * (1.0 - sg)))).astype(BF16)
            return carry

        lax.fori_loop(0, S // RC, gates, 0)

        def scan(i, g_in):
            base = pl.multiple_of((S // (8 * SCAN_GROUP) - 1 - i) * (8 * SCAN_GROUP), 8 * SCAN_GROUP)
            row = lax.broadcasted_iota(jnp.int32, (8, CB), 0)
            for k in reversed(range(SCAN_GROUP)):
                rows = pl.ds(base + 8 * k, 8)
                a = a_s[rows, :]
                dl = dl_s[rows, :]
                a_cum, g_loc = _tile_scan_rev(a, a * dl)
                g = g_loc + a_cum * g_in
                dh_s[rows, :] = dl + jnp.where(row < 7, pltpu.roll(g, 7, 0), g_in)
                g_in = g_loc[0:1, :] + a_cum[0:1, :] * g_in
            return g_in

        lax.fori_loop(0, S // (8 * SCAN_GROUP), scan, jnp.zeros((1, CB), F32))

        dxc_s[pl.ds(S, 8), :] = jnp.zeros((8, CB), F32)

        def grads(c, carry):
            rows = _rows(c, RC)
            dh = dh_s[rows, :]
            h_prev = _shift_down(_window_before(hs_ref, c, RC), 1, RC)
            xc, gi, gr, a = xc_s[rows, :], gi_s[rows, :], gr_s[rows, :], a_s[rows, :]
            mult = jnp.sqrt(-_expm1_nonpos(-2.0 * LRU_C * gr * sp))
            dmult = dh * gi * xc
            d_log_a = dh * h_prev * a - dmult * (a * a) / mult
            dzi = dh * mult * xc * gi * (1.0 - gi)
            dzr = d_log_a * (-LRU_C * sp) * gr * (1.0 - gr)
            dz = jnp.concatenate([dzi, dzr], axis=1).astype(BF16)
            dxc_s[rows, :] = dh * mult * gi + _dot_nt(dz, wbd_ref[...])
            dwbd_ref[...] += _dot_tn(xc.astype(BF16), dz)
            acc_s[1:2, :] += jnp.sum(dzi, axis=0, keepdims=True)
            acc_s[2:3, :] += jnp.sum(dzr, axis=0, keepdims=True)
            acc_s[3:4, :] += jnp.sum(d_log_a * (-LRU_C * gr), axis=0, keepdims=True)
            return carry

        lax.fori_loop(0, S // RC, grads, 0)

        def conv_bwd(c, carry):
            rows = _rows(c, RC_CONV)
            dwin = dxc_s[pl.ds(pl.multiple_of(c * RC_CONV, RC_CONV), RC_CONV + 8), :]
            dxc = dwin[:RC_CONV, :]
            xwin = _window_before(xa_ref, c, RC_CONV)
            dxa = cw_ref[3:4, :] * dxc
            acc_s[0:1, :] += jnp.sum(dxc, axis=0, keepdims=True)
            acc_s[7:8, :] += jnp.sum(dxc * _shift_down(xwin, 0, RC_CONV), axis=0, keepdims=True)
            for s in (1, 2, 3):
                dxa = dxa + cw_ref[3 - s:4 - s, :] * _shift_up(dwin, s, RC_CONV)
                acc_s[7 - s:8 - s, :] += jnp.sum(dxc * _shift_down(xwin, s, RC_CONV), axis=0, keepdims=True)
            dp_ref[0, rows, :] = dxa.astype(BF16)
            return carry

        lax.fori_loop(0, S // RC_CONV, conv_bwd, 0)

        row = lax.broadcasted_iota(jnp.int32, acc_s.shape, 0)
        vec_ref[...] += jnp.where(row == 3, acc_s[...] * (-_sigmoid(-lam_v)), acc_s[...])

        @pl.when(jnp.logical_and(pl.program_id(0) == NCB - 1, b == nb - 1))
        def _():
            for cp in exchange:
                cp.wait()

    vec = pl.BlockSpec((1, CB), lambda cb, b: (0, cb))
    blk = pl.BlockSpec((S, CB), lambda cb, b: (b, cb))
    own = pl.BlockSpec((None, None, S, CB), lambda cb, b: (b, cb, 0, 0))
    return pl.pallas_call(
        body, name="lru_bwd", grid=(NCB, nb),
        in_specs=[pl.BlockSpec((None, S, CB), lambda cb, b: (0, b, cb)),
                  pl.BlockSpec((None, S, CB), lambda cb, b: (1, b, cb)),
                  own, own, own, own, blk,
                  pl.BlockSpec((4, CB), lambda cb, b: (0, cb)),
                  pl.BlockSpec((None, CB, 2 * CB), lambda cb, b: (cb, 0, 0)),
                  vec] + [ANY] * ng,
        out_specs=[pl.BlockSpec((2, S, CB), lambda cb, b: (0, b, cb)),
                   pl.BlockSpec((None, CB, 2 * CB), lambda cb, b: (cb, 0, 0)),
                   pl.BlockSpec((8, CB), lambda cb, b: (0, cb))] + [ANY] * ng,
        out_shape=[jax.ShapeDtypeStruct((2, t, D), BF16),
                   jax.ShapeDtypeStruct((NCB, CB, 2 * CB), F32),
                   jax.ShapeDtypeStruct((8, D), F32)]
        + [jax.ShapeDtypeStruct((4,) + g.shape[1:], g.dtype) for g in give],
        scratch_shapes=[pltpu.VMEM((S, CB), F32), pltpu.VMEM((S, CB), F32), pltpu.VMEM((S, CB), F32),
                        pltpu.VMEM((S + 8, CB), F32), pltpu.VMEM((8, CB), F32),
                        pltpu.SemaphoreType.DMA((ng, 4)), pltpu.SemaphoreType.DMA((ng, 4))],
        compiler_params=_params(("arbitrary", "arbitrary")),
    )(proj, proj, hs, xc_f, gi_f, gr_f, dya, conv_w, wbd, lam, *give)


def _retention_tables():
    f32 = np.float32
    log_g = np.log1p(-(f32(2.0) ** (f32(-5.0) - np.arange(HEADS, dtype=f32)))).astype(f32)
    idx = np.arange(CH, dtype=f32)
    diff = idx[:, None] - idx[None, :]
    inner = np.where(diff >= 0, np.exp(np.maximum(diff, f32(0.0))[None] * log_g[:, None, None]), f32(0.0)).astype(f32)
    cross = np.exp((idx[None, :] + f32(1.0)) * log_g[:, None]).astype(f32)
    state = np.exp((f32(CH - 1.0) - idx[None, :]) * log_g[:, None]).astype(f32)
    cross = np.ascontiguousarray(np.broadcast_to(cross[:, :, None], (HEADS, CH, DK)))
    state = np.ascontiguousarray(np.broadcast_to(state[:, :, None], (HEADS, CH, DK)))
    half = DK // 2
    freqs = (f32(10000.0) ** (-np.arange(half, dtype=f32) / f32(half))).astype(f32)
    ang = (np.arange(S, dtype=f32)[:, None] * freqs[None, :]).astype(f32)
    return tuple(jnp.asarray(a) for a in (inner, cross, state, np.cos(ang).astype(f32), np.sin(ang).astype(f32)))


def _rotate(x, cos, sin):
    half = DK // 2
    x1, x2 = x[:, :half], x[:, half:]
    return jnp.concatenate([x1 * cos - x2 * sin, x1 * sin + x2 * cos], axis=1)


def _rotate_back(d, cos, sin):
    half = DK // 2
    d1, d2 = d[:, :half], d[:, half:]
    return jnp.concatenate([d1 * cos + d2 * sin, d2 * cos - d1 * sin], axis=1)


def _ret_fwd(proj, gain, tables, nb, wp_own):
    t = nb * S
    inner_t, cross_t, state_t, cos_t, sin_t = tables

    def body(q_ref, k_ref, v_ref, gb_ref, gain_ref, dm_ref, cd_ref, sd_ref, cos_ref, sin_ref, wp_ref,
             yb_ref, qr_ref, kr_ref, o_ref, rs_ref, wpg_ref, r_s, send_sems, recv_sems, own_sems):
        b, hd = pl.program_id(0), pl.program_id(1)
        own, first, arrive, forward, others = _gather_copies([wp_ref], [wpg_ref], send_sems, recv_sems, own_sems)

        @pl.when(jnp.logical_and(b == 0, hd == 0))
        def _():
            for cp in own + first:
                cp.start()

        @pl.when(jnp.logical_and(b == nb - 1, hd == HEADS - 1))
        def _():
            for came, on in zip(arrive, forward):
                came.wait_recv()
                on.start()

        r_s[...] = jnp.zeros_like(r_s)
        chunk_decay = cd_ref[CH - 1:CH, :]

        def chunk(c, carry):
            rows = _rows(c, CH)
            cos, sin = cos_ref[rows, :], sin_ref[rows, :]
            qr = _rotate(q_ref[rows, :].astype(F32), cos, sin).astype(BF16)
            kr = (_rotate(k_ref[rows, :].astype(F32), cos, sin) * (DK ** -0.5)).astype(BF16)
            vb = v_ref[rows, :]
            v = vb.astype(F32)
            qr_ref[rows, :] = qr
            kr_ref[rows, :] = kr
            r = r_s[...]
            rb = r.astype(BF16)
            rs_ref[c] = rb
            p = (_dot_nt(qr, kr) * dm_ref[...]).astype(BF16)
            o = _dot(p, vb) + _dot(qr, rb) * cd_ref[...]
            r_s[...] = chunk_decay * r + _dot_tn(kr, (v * sd_ref[...]).astype(BF16))
            o_ref[rows, :] = o
            oc = o - jnp.mean(o, axis=-1, keepdims=True)
            rstd = lax.rsqrt(jnp.mean(oc * oc, axis=-1, keepdims=True) + EPS)
            gb = gb_ref[rows, :].astype(F32)
            yb_ref[rows, :] = (gb * _sigmoid(gb) * (oc * rstd * gain_ref[...])).astype(BF16)
            return carry

        lax.fori_loop(0, NCH, chunk, 0, unroll=2)

        @pl.when(jnp.logical_and(b == nb - 1, hd == HEADS - 1))
        def _():
            for cp in others:
                cp.wait_recv()
            for cp in first + forward:
                cp.wait_send()
            for cp in own:
                cp.wait()

    seg = lambda s: pl.BlockSpec((None, S, DK), lambda b, h: (s, b, h))
    tab = pl.BlockSpec((None, CH, DK), lambda b, h: (h, 0, 0))
    rot = pl.BlockSpec((S, DK // 2), lambda b, h: (0, 0))
    blk = pl.BlockSpec((S, DK), lambda b, h: (b, h))
    return pl.pallas_call(
        body, name="ret_fwd", grid=(nb, HEADS),
        in_specs=[seg(2), seg(3), seg(4), seg(5),
                  pl.BlockSpec((None, 1, DK), lambda b, h: (h, 0, 0)),
                  tab, tab, tab, rot, rot, ANY],
        out_specs=[blk, blk, blk, blk,
                   pl.BlockSpec((None, None, NCH, DK, DK), lambda b, h: (b, h, 0, 0, 0)), ANY],
        out_shape=[jax.ShapeDtypeStruct((t, D), BF16), jax.ShapeDtypeStruct((t, D), BF16),
                   jax.ShapeDtypeStruct((t, D), BF16), jax.ShapeDtypeStruct((t, D), F32),
                   jax.ShapeDtypeStruct((nb, HEADS, NCH, DK, DK), BF16),
                   jax.ShapeDtypeStruct((NDEV,) + wp_own.shape, wp_own.dtype)],
        scratch_shapes=[pltpu.VMEM((DK, DK), F32),
                        pltpu.SemaphoreType.DMA((1, 7)), pltpu.SemaphoreType.DMA((1, 7)), pltpu.SemaphoreType.DMA((1,))],
        compiler_params=_params(("arbitrary", "arbitrary")),
    )(proj, proj, proj, proj, gain, inner_t, cross_t, state_t, cos_t, sin_t, wp_own)


def _ret_bwd(proj, qr, kr, o, rs, dyb, gain, tables, nb, sums):
    t = nb * S
    ns = len(sums)
    inner_t, cross_t, state_t, cos_t, sin_t = tables

    def body(qr_ref, kr_ref, v_ref, gb_ref, o_ref, dyb_ref, rs_ref, gain_ref, dm_ref, cd_ref, sd_ref,
             cos_ref, sin_ref, *rest):
        sum_refs, rest = rest[:ns], rest[ns:]
        dp_ref, dgain_ref = rest[:2]
        part_refs, rest = rest[2:2 + ns], rest[2 + ns:]
        dr_s, send_sems, recv_sems, local_sems = rest
        mine, sends, recvs = _chip_copies(sum_refs, part_refs, send_sems, recv_sems, local_sems)

        @pl.when(jnp.logical_and(pl.program_id(0) == 0, pl.program_id(1) == 0))
        def _():
            for cp in mine + sends:
                cp.start()

        dr_s[...] = jnp.zeros_like(dr_s)
        chunk_decay = cd_ref[CH - 1:CH, :]

        @pl.when(pl.program_id(1) == 0)
        def _():
            dgain_ref[...] = jnp.zeros_like(dgain_ref)

        def chunk(i, carry):
            c = NCH - 1 - i
            rows = _rows(c, CH)
            gain_v = gain_ref[...]
            o_c = o_ref[rows, :]
            oc = o_c - jnp.mean(o_c, axis=-1, keepdims=True)
            rstd = lax.rsqrt(jnp.mean(oc * oc, axis=-1, keepdims=True) + EPS)
            yn = oc * rstd
            gb = gb_ref[rows, :].astype(F32)
            sg = _sigmoid(gb)
            dyb_c = dyb_ref[rows, :]
            dgn = dyb_c * (gb * sg)
            dp_ref[3, rows, :] = (dyb_c * (yn * gain_v) * (sg * (1.0 + gb * (1.0 - sg)))).astype(BF16)
            dgain_ref[...] += jnp.sum(dgn * yn, axis=0, keepdims=True)
            dyn = dgn * gain_v
            do = rstd * (dyn - jnp.mean(dyn, axis=-1, keepdims=True)
                         - yn * jnp.mean(dyn * yn, axis=-1, keepdims=True))
            dob = do.astype(BF16)
            dox = (do * cd_ref[...]).astype(BF16)

            q_c, k_c = qr_ref[rows, :], kr_ref[rows, :]
            vb = v_ref[rows, :]
            v = vb.astype(F32)
            vs = (v * sd_ref[...]).astype(BF16)
            rb = rs_ref[c]
            d_r = dr_s[...]
            drb = d_r.astype(BF16)
            dm = dm_ref[...]
            p = (_dot_nt(q_c, k_c) * dm).astype(BF16)
            dpm = (_dot_nt(dob, vb) * dm).astype(BF16)
            dq = _dot(dpm, k_c) + _dot_nt(dox, rb)
            dk = _dot_tn(dpm, q_c) + _dot_nt(vs, drb)
            dv = _dot_tn(p, dob) + _dot(k_c, drb) * sd_ref[...]
            dr_s[...] = chunk_decay * d_r + _dot_tn(q_c, dox)

            cos, sin = cos_ref[rows, :], sin_ref[rows, :]
            dp_ref[0, rows, :] = _rotate_back(dq, cos, sin).astype(BF16)
            dp_ref[1, rows, :] = (_rotate_back(dk, cos, sin) * (DK ** -0.5)).astype(BF16)
            dp_ref[2, rows, :] = dv.astype(BF16)
            return carry

        lax.fori_loop(0, NCH, chunk, 0, unroll=2)

        @pl.when(jnp.logical_and(pl.program_id(0) == HEADS - 1, pl.program_id(1) == nb - 1))
        def _():
            for cp in recvs:
                cp.wait_recv()
            for cp in sends:
                cp.wait_send()
            for cp in mine:
                cp.wait()

    seg = lambda s: pl.BlockSpec((None, S, DK), lambda h, b: (s, b, h))
    tab = pl.BlockSpec((None, CH, DK), lambda h, b: (h, 0, 0))
    rot = pl.BlockSpec((S, DK // 2), lambda h, b: (0, 0))
    blk = pl.BlockSpec((S, DK), lambda h, b: (b, h))
    one = pl.BlockSpec((None, 1, DK), lambda h, b: (h, 0, 0))
    return pl.pallas_call(
        body, name="ret_bwd", grid=(HEADS, nb),
        in_specs=[blk, blk, seg(4), seg(5), blk, blk,
                  pl.BlockSpec((None, None, NCH, DK, DK), lambda h, b: (b, h, 0, 0, 0)),
                  one, tab, tab, tab, rot, rot] + [ANY] * ns,
        out_specs=[pl.BlockSpec((4, S, DK), lambda h, b: (0, b, h)), one] + [ANY] * ns,
        out_shape=[jax.ShapeDtypeStruct((4, t, D), BF16), jax.ShapeDtypeStruct((HEADS, 1, DK), F32)]
        + [jax.ShapeDtypeStruct(a.shape, a.dtype) for a in sums],
        scratch_shapes=[pltpu.VMEM((DK, DK), F32), pltpu.SemaphoreType.DMA((ns, 3)), pltpu.SemaphoreType.DMA((ns, 3)),
                        pltpu.SemaphoreType.DMA((ns,))],
        compiler_params=_params(("arbitrary", "arbitrary")),
    )(qr, kr, proj, proj, o, dyb, rs, gain, inner_t, cross_t, state_t, cos_t, sin_t, *sums)


def _wblock(k):
    return pl.BlockSpec((NDEV, D // NDEV, D), lambda i: (0, k, 0))


def _tail(ya, yb, proj, x2d, tgt, wg, g_fin):
    t = x2d.shape[0]
    tm = 256

    def body(ya_ref, yb_ref, ma_ref, mb_ref, x_ref, t_ref, wa_ref, wb_ref, wo_ref, g_ref,
             dx2_ref, dya_ref, dyb_ref, dm_ref, mg_ref, doa_ref, dob_ref, gfin_ref, loss_ref):
        i = pl.program_id(0)

        @pl.when(i == 0)
        def _():
            gfin_ref[...] = jnp.zeros_like(gfin_ref)
            loss_ref[...] = jnp.zeros_like(loss_ref)

        wa = wa_ref[...].reshape(D, D)
        wb = wb_ref[...].reshape(D, D)
        wo = wo_ref[...].reshape(D, D)
        out_a = _dot(ya_ref[...], wa)
        out_b = _dot(yb_ref[...], wb)
        sa = _sigmoid(ma_ref[...].astype(F32))
        sb = _sigmoid(mb_ref[...].astype(F32))
        merged = (sa * out_a + sb * out_b).astype(BF16)
        mg_ref[...] = merged
        x2 = x_ref[...] + _dot(merged, wo)
        r2 = lax.rsqrt(jnp.mean(x2 * x2, axis=-1, keepdims=True) + EPS)
        xh = x2 * r2
        g = g_ref[...]
        err = xh * g - t_ref[...]
        loss_ref[...] += jnp.sum(err * err, axis=0, keepdims=True) * (0.5 / D)
        dy = err * (1.0 / D)
        gfin_ref[...] += jnp.sum(dy * xh, axis=0, keepdims=True)
        dxh = dy * g
        dx2 = r2 * (dxh - xh * jnp.mean(dxh * xh, axis=-1, keepdims=True))
        dx2_ref[...] = dx2
        dmerged = _dot_nt(dx2.astype(BF16), wo)
        doa = (sa * dmerged).astype(BF16)
        dob = (sb * dmerged).astype(BF16)
        doa_ref[...] = doa
        dob_ref[...] = dob
        dm_ref[0] = (dmerged * out_a * sa * (1.0 - sa)).astype(BF16)
        dm_ref[1] = (dmerged * out_b * sb * (1.0 - sb)).astype(BF16)
        dya_ref[...] = _dot_nt(doa, wa)
        dyb_ref[...] = _dot_nt(dob, wb)

    row = lambda: pl.BlockSpec((tm, D), lambda i: (i, 0))
    seg = lambda s: pl.BlockSpec((None, tm, D), lambda i: (s, i, 0))
    vec = pl.BlockSpec((1, D), lambda i: (0, 0))
    return pl.pallas_call(
        body, name="tail", grid=(t // tm,),
        in_specs=[row(), row(), seg(6), seg(7), row(), row(), _wblock(0), _wblock(1), _wblock(2), vec],
        out_specs=[row(), row(), row(), pl.BlockSpec((2, tm, D), lambda i: (0, i, 0)),
                   row(), row(), row(), vec, vec],
        out_shape=[jax.ShapeDtypeStruct((t, D), F32), jax.ShapeDtypeStruct((t, D), F32),
                   jax.ShapeDtypeStruct((t, D), F32), jax.ShapeDtypeStruct((2, t, D), BF16),
                   jax.ShapeDtypeStruct((t, D), BF16), jax.ShapeDtypeStruct((t, D), BF16),
                   jax.ShapeDtypeStruct((t, D), BF16), jax.ShapeDtypeStruct((1, D), F32),
                   jax.ShapeDtypeStruct((1, D), F32)],
        compiler_params=_params(("arbitrary",)),
    )(ya, yb, proj, proj, x2d, tgt, wg, wg, wg, g_fin)


def _tail_wgrad(ya, yb, merged, doa, dob, dx2):
    t = ya.shape[0]
    tm = 512

    def body(ya_ref, yb_ref, mg_ref, doa_ref, dob_ref, dx2_ref, ga_ref, gb_ref, go_ref):
        @pl.when(pl.program_id(0) == 0)
        def _():
            ga_ref[...] = jnp.zeros_like(ga_ref)
            gb_ref[...] = jnp.zeros_like(gb_ref)
            go_ref[...] = jnp.zeros_like(go_ref)

        ga_ref[...] += _dot_tn(ya_ref[...], doa_ref[...])
        gb_ref[...] += _dot_tn(yb_ref[...], dob_ref[...])
        go_ref[...] += _dot_tn(mg_ref[...], dx2_ref[...].astype(BF16))

    row = lambda: pl.BlockSpec((tm, D), lambda i: (i, 0))
    full = lambda: pl.BlockSpec((D, D), lambda i: (0, 0))
    return pl.pallas_call(
        body, name="tail_wgrad", grid=(t // tm,),
        in_specs=[row() for _ in range(6)], out_specs=[full(), full(), full()],
        out_shape=[jax.ShapeDtypeStruct((D, D), F32)] * 3,
        compiler_params=_params(("arbitrary",)),
    )(ya, yb, merged, doa, dob, dx2)


def _dproj_specs_ordered(tm):
    def spec(lo, n):
        def index(k, i, order_ref):
            seg = order_ref[k]
            mine = jnp.logical_and(seg >= lo, seg < lo + n)
            return jnp.where(mine, seg - lo, 0), jnp.where(mine, i, 0), 0
        return pl.BlockSpec((None, tm, D), index)
    return [spec(0, 2), spec(2, 4), spec(6, 2)]


def _dproj_pick(j, da_ref, db_ref, dc_ref, use):
    @pl.when(j < 2)
    def _():
        use(da_ref[...])

    @pl.when(jnp.logical_and(j >= 2, j < 6))
    def _():
        use(db_ref[...])

    @pl.when(j >= 6)
    def _():
        use(dc_ref[...])


RS_X, RS_Y, RS_XY = 0, 1, 2
RS_ROLES = ((RS_XY, RS_X, RS_Y), (RS_Y, RS_XY, RS_X))


def _rs_flip(rel, x, y):
    return ((1 - x, y), (x, 1 - y), (1 - x, 1 - y))[rel]


def _rs_order(x, y, c):
    order = []
    for s in range(4):
        chip = []
        for core in (0, 1):
            px, py = _rs_flip(RS_ROLES[core][s], x, y) if s < 3 else (x, y)
            chip.append(2 * px + py)
        keep = jnp.where(c == 0, chip[0], chip[1])
        give = jnp.where(c == 0, chip[1], chip[0])
        order += [2 * give + 1 - c, 2 * keep + c]
    return jnp.stack(order).astype(jnp.int32)


def _inproj_wgrad_rs(h, dpa, dpb, dpc, order, smalls):
    t = h.shape[0]
    tm = 1024
    nt = t // tm
    nsm = len(smalls)

    def body(order_ref, h_ref, da_ref, db_ref, dc_ref, *rest):
        small_refs, parts_ref, rest = rest[:nsm], rest[nsm], rest[nsm + 1:]
        all_refs, rest = rest[:nsm], rest[nsm:]
        (acc, sib, outb, far, give_send, give_recv, sum_send, sum_recv, far_send, far_recv, own_sem,
         small_send, small_recv, small_own) = rest
        k, i = pl.program_id(0), pl.program_id(1)
        x, y, c = _place()
        own, first, arrive, forward, others = _gather_copies(small_refs, all_refs, small_send, small_recv, small_own)

        @pl.when(jnp.logical_and(k == 0, i == 0))
        def _():
            for cp in own + first:
                cp.start()

        @pl.when(jnp.logical_and(k == 2, i == 0))
        def _():
            for came, on in zip(arrive, forward):
                came.wait_recv()
                on.start()

        def use(d):
            @pl.when(i == 0)
            def _():
                acc[k % 2] = _dot_tn(h_ref[...], d)

            @pl.when(i > 0)
            def _():
                acc[k % 2] += _dot_tn(h_ref[...], d)

        _dproj_pick(order_ref[k], da_ref, db_ref, dc_ref, use)

        def give_copy(s):
            return pltpu.make_async_remote_copy(
                src_ref=acc.at[0], dst_ref=sib.at[s % 2], send_sem=give_send.at[s], recv_sem=give_recv.at[s],
                device_id=(x, y, 1 - c), device_id_type=MESH)

        def sum_copy(s, core):
            slot = 0 if s < 2 else 1
            return pltpu.make_async_remote_copy(
                src_ref=outb.at[s], dst_ref=parts_ref.at[slot], send_sem=sum_send.at[slot], recv_sem=sum_recv.at[slot],
                device_id=(*_rs_flip(RS_ROLES[core][s], x, y), core), device_id_type=MESH)

        def far_copy(s, core):
            return pltpu.make_async_remote_copy(
                src_ref=outb.at[s], dst_ref=far, send_sem=far_send, recv_sem=far_recv,
                device_id=(*_rs_flip(RS_X if core == 0 else RS_Y, x, y), core), device_id_type=MESH)

        own_copy = pltpu.make_async_copy(outb.at[3], parts_ref.at[2], own_sem)

        def send_of(core, s):
            return far_copy(s, core) if RS_ROLES[core][s] == RS_XY else sum_copy(s, core)

        for s in range(4):
            @pl.when(jnp.logical_and(k == 2 * s, i == nt - 1))
            def _():
                give_copy(s).start()

            @pl.when(jnp.logical_and(k == 2 * s + 1, i == nt - 1))
            def _():
                give_copy(s).wait_recv()
                if s == 2:
                    far_copy(s, 0).wait_recv()
                    outb[s] = (acc[1] + sib[s % 2] + far[...].astype(F32)).astype(BF16)
                else:
                    outb[s] = (acc[1] + sib[s % 2]).astype(BF16)
                give_copy(s).wait_send()
                if s < 3:
                    for core in (0, 1):
                        @pl.when(c == core)
                        def _():
                            send_of(core, s).start()
                else:
                    own_copy.start()

        @pl.when(jnp.logical_and(k == NSEG - 1, i == nt - 1))
        def _():
            for slot in (0, 1):
                sum_copy(2 * slot, 0).wait_recv()
            for s in range(3):
                send_of(0, s).wait_send()
            own_copy.wait()
            for cp in others:
                cp.wait_recv()
            for cp in first + forward:
                cp.wait_send()
            for cp in own:
                cp.wait()

    return pl.pallas_call(
        body, name="inproj_wgrad_rs",
        grid_spec=pltpu.PrefetchScalarGridSpec(
            num_scalar_prefetch=1, grid=(NSEG, nt),
            in_specs=[pl.BlockSpec((tm, D), lambda k, i, order_ref: (i, 0))] + _dproj_specs_ordered(tm) + [ANY] * nsm,
            out_specs=[ANY] * (1 + nsm),
            scratch_shapes=[pltpu.VMEM((2, D, D), F32), pltpu.VMEM((2, D, D), F32), pltpu.VMEM((4, D, D), BF16),
                            pltpu.VMEM((D, D), BF16),
                            pltpu.SemaphoreType.DMA((4,)), pltpu.SemaphoreType.DMA((4,)),
                            pltpu.SemaphoreType.DMA((2,)), pltpu.SemaphoreType.DMA((2,)),
                            pltpu.SemaphoreType.DMA, pltpu.SemaphoreType.DMA, pltpu.SemaphoreType.DMA,
                            pltpu.SemaphoreType.DMA((nsm, 7)), pltpu.SemaphoreType.DMA((nsm, 7)),
                            pltpu.SemaphoreType.DMA((nsm,))]),
        out_shape=[jax.ShapeDtypeStruct((3, D, D), BF16)]
        + [jax.ShapeDtypeStruct((NDEV,) + a.shape, a.dtype) for a in smalls],
        compiler_params=_params(("arbitrary", "arbitrary")),
    )(order, h, dpa, dpb, dpc, *smalls)


def _inproj_dgrad(dpa, dpb, dpc, wg, x2d, dx2, g_in):
    t = x2d.shape[0]
    tm = 512

    def body(da_ref, db_ref, dc_ref, w_hbm, x_ref, dx2_ref, g_ref, gx_ref, gg_ref, w_s, w_sem):
        i = pl.program_id(0)

        @pl.when(i == 0)
        def _():
            gg_ref[...] = jnp.zeros_like(gg_ref)
            load = pltpu.make_async_copy(w_hbm, w_s, w_sem)
            load.start()
            load.wait()

        dh = None
        for ref, lo in ((da_ref, 0), (db_ref, 2), (dc_ref, 6)):
            for k in range(ref.shape[0]):
                part = _dot_nt(ref[k], w_s[lo + k])
                dh = part if dh is None else dh + part
        x = x_ref[...]
        r = lax.rsqrt(jnp.mean(x * x, axis=-1, keepdims=True) + EPS)
        xh = x * r
        gg_ref[...] += jnp.sum(dh * xh, axis=0, keepdims=True)
        dxh = dh * g_ref[...]
        gx_ref[...] = dx2_ref[...] + r * (dxh - xh * jnp.mean(dxh * xh, axis=-1, keepdims=True))

    row = lambda: pl.BlockSpec((tm, D), lambda i: (i, 0))
    seg = lambda n: pl.BlockSpec((n, tm, D), lambda i: (0, i, 0))
    vec = pl.BlockSpec((1, D), lambda i: (0, 0))
    return pl.pallas_call(
        body, name="inproj_dgrad", grid=(t // tm,),
        in_specs=[seg(2), seg(4), seg(2), ANY, row(), row(), vec],
        out_specs=[row(), vec],
        out_shape=[jax.ShapeDtypeStruct((t, D), F32), jax.ShapeDtypeStruct((1, D), F32)],
        scratch_shapes=[pltpu.VMEM((NSEG, D, D), BF16), pltpu.SemaphoreType.DMA],
        compiler_params=_params(("arbitrary",)),
    )(dpa, dpb, dpc, wg, x2d, dx2, g_in)


def _adam_update(g, w, m, v):
    m_new = ADAM_B1 * m + (1.0 - ADAM_B1) * g
    v_new = ADAM_B2 * v + (1.0 - ADAM_B2) * (g * g)
    m_hat = m_new / (1.0 - ADAM_B1 ** ADAM_STEP)
    v_hat = v_new / (1.0 - ADAM_B2 ** ADAM_STEP)
    return -ADAM_LR * (m_hat / (jnp.sqrt(v_hat) + ADAM_EPS) + ADAM_WD * w), m_new, v_new


def _sum_in_order(ref):
    total = ref[0].astype(F32)
    for k in range(1, ref.shape[0]):
        total = total + ref[k].astype(F32)
    return total


def _adamw_small(me, vec_all, gx_all, ga_all, groups):
    flat = [a for grp in groups for a in grp]
    ng = len(groups)
    nshard = D // NDEV

    def body(me_ref, vec_ref, shard_ref, gx_ref, ga_ref, *refs):
        ins, outs = refs[:3 * ng], refs[3 * ng:]
        vec = _sum_in_order(vec_ref)
        shard = _sum_in_order(shard_ref)
        grads = [vec[r:r + 1, :] for r in range(6)]
        grads += [shard[0:4, :], shard[4:8, 0:DK // NDEV], _sum_in_order(gx_ref), _sum_in_order(ga_ref)]
        for n, g in enumerate(grads):
            delta, m_new, v_new = _adam_update(g, ins[3 * n][...], ins[3 * n + 1][...], ins[3 * n + 2][...])
            outs[4 * n][...] = g
            outs[4 * n + 1][...] = delta
            outs[4 * n + 2][...] = m_new
            outs[4 * n + 3][...] = v_new
        outs[4 * ng][...] = jnp.sum(vec[6:7, :], axis=1, keepdims=True)

    full = lambda a: pl.BlockSpec(a.shape, lambda i, me_ref, nd=len(a.shape): (0,) * nd)
    out_shape = [jax.ShapeDtypeStruct(w.shape, F32) for w, _, _ in groups for _ in range(4)]
    out_shape.append(jax.ShapeDtypeStruct((1, 1), F32))
    outs = pl.pallas_call(
        body, name="adamw_small",
        grid_spec=pltpu.PrefetchScalarGridSpec(
            num_scalar_prefetch=1, grid=(1,),
            in_specs=[full(vec_all),
                      pl.BlockSpec((NDEV, 8, nshard), lambda i, me_ref: (0, 1, me_ref[0])),
                      full(gx_all), full(ga_all)] + [full(a) for a in flat],
            out_specs=[full(s) for s in out_shape]),
        out_shape=out_shape,
        compiler_params=_params(("arbitrary",)),
    )(me, vec_all, vec_all, gx_all, ga_all, *flat)
    return [outs[4 * n:4 * n + 4] for n in range(ng)], outs[4 * ng]


def _adamw(name, items):
    n, rows, cols = items[0][0].shape
    tr = rows if rows <= 256 else 256
    k = len(items)

    def body(*refs):
        for a in range(k):
            p_ref, w_ref, m_ref, v_ref = refs[4 * a:4 * a + 4]
            g = _sum_in_order(p_ref)
            delta, m_new, v_new = _adam_update(g, w_ref[...], m_ref[...], v_ref[...])
            for o, val in zip(refs[4 * k + 4 * a:4 * k + 4 * a + 4], (g, delta, m_new, v_new)):
                o[...] = val

    blk = lambda: pl.BlockSpec((tr, cols), lambda i: (i, 0))
    outs = pl.pallas_call(
        body, name=name, grid=(rows // tr,),
        in_specs=[pl.BlockSpec((n, tr, cols), lambda i: (0, i, 0)), blk(), blk(), blk()] * k,
        out_specs=[blk() for _ in range(4 * k)],
        out_shape=[jax.ShapeDtypeStruct((rows, cols), F32)] * (4 * k),
        compiler_params=_params(("arbitrary",)),
    )(*[a for item in items for a in item])
    return [outs[4 * a:4 * a + 4] for a in range(k)]


ANY = pl.BlockSpec(memory_space=pl.ANY)


def _place():
    return lax.axis_index("x"), lax.axis_index("y"), lax.axis_index("c")


def _gather_copies(ins, outs, send_sems, recv_sems, own_sems):
    x, y, c = _place()
    me, sibling = (x, y, c), (x, y, 1 - c)
    chips = [(1 - x, y), (x, 1 - y), (1 - x, 1 - y)]
    n = len(ins)

    def copy(a, k, block, to, src=None):
        px, py, pc = block
        dst = outs[a].at[4 * px + 2 * py + pc]
        return pltpu.make_async_remote_copy(
            src_ref=dst if src is None else src, dst_ref=dst,
            send_sem=send_sems.at[a, k], recv_sem=recv_sems.at[a, k], device_id=to, device_id_type=MESH)

    own = [pltpu.make_async_copy(ins[a], outs[a].at[4 * x + 2 * y + c], own_sems.at[a]) for a in range(n)]
    first = []
    for a in range(n):
        first.append(copy(a, 0, me, sibling, src=ins[a]))
        first += [copy(a, 1 + j, me, (*chip, c), src=ins[a]) for j, chip in enumerate(chips)]
    arrive = [copy(a, 1 + j, (*chip, c), me) for j, chip in enumerate(chips) for a in range(n)]
    forward = [copy(a, 4 + j, (*chip, c), sibling) for j, chip in enumerate(chips) for a in range(n)]
    rest = [copy(a, 0, sibling, me) for a in range(n)]
    rest += [copy(a, 4 + j, (*chip, 1 - c), me) for a in range(n) for j, chip in enumerate(chips)]
    return own, first, arrive, forward, rest


def _sibling_copies(ins, outs, send_sems, recv_sems):
    x, y, c = _place()
    return [pltpu.make_async_remote_copy(
        src_ref=ins[a].at[2 * q + 1 - c], dst_ref=outs[a].at[q],
        send_sem=send_sems.at[a, q], recv_sem=recv_sems.at[a, q],
        device_id=(x, y, 1 - c), device_id_type=MESH) for a in range(len(ins)) for q in range(4)]


def _chip_copies(ins, outs, send_sems, recv_sems, local_sems):
    x, y, c = _place()
    my_chip = 2 * x + y
    chips = [(1 - x, y), (x, 1 - y), (1 - x, 1 - y)]
    n = len(ins)
    mine = [pltpu.make_async_copy(ins[a].at[my_chip], outs[a].at[my_chip], local_sems.at[a]) for a in range(n)]
    sends = [pltpu.make_async_remote_copy(
        src_ref=ins[a].at[2 * px + py], dst_ref=outs[a].at[my_chip],
        send_sem=send_sems.at[a, j], recv_sem=recv_sems.at[a, j],
        device_id=(px, py, c), device_id_type=MESH) for a in range(n) for j, (px, py) in enumerate(chips)]
    recvs = [pltpu.make_async_remote_copy(
        src_ref=ins[a].at[my_chip], dst_ref=outs[a].at[2 * px + py],
        send_sem=send_sems.at[a, j], recv_sem=recv_sems.at[a, j],
        device_id=(px, py, c), device_id_type=MESH) for a in range(n) for j, (px, py) in enumerate(chips)]
    return mine, sends, recvs


def _chip_sum(owns, gots, core):
    n = len(owns)
    _, rows, cols = owns[0].shape

    def body(core_ref, *refs):
        for a in range(n):
            refs[2 * n + a][...] = (refs[a][...] + refs[n + a][...]).astype(BF16)

    own_spec = pl.BlockSpec((None, rows, cols), lambda q, core_ref: (2 * q + core_ref[0], 0, 0))
    slab = pl.BlockSpec((None, rows, cols), lambda q, core_ref: (q, 0, 0))
    return pl.pallas_call(
        body, name="chip_sum",
        grid_spec=pltpu.PrefetchScalarGridSpec(
            num_scalar_prefetch=1, grid=(4,),
            in_specs=[own_spec] * n + [slab] * n, out_specs=[slab] * n),
        out_shape=[jax.ShapeDtypeStruct((4, rows, cols), BF16)] * n,
        compiler_params=_params(("arbitrary",)),
    )(core, *owns, *gots)


def _block_diag(w):
    w4 = w.reshape(NCB, 4, 64, 64)
    eye = jnp.eye(4, dtype=w.dtype)
    return (w4[:, :, :, None, :] * eye[None, :, None, :, None]).reshape(NCB, CB, CB)


def _block_diag_back(g):
    g5 = g.reshape(NCB, 4, 64, 4, 64)
    return jnp.stack([g5[:, m, :, m, :] for m in range(4)], axis=1).reshape(16, 64, 64)


def kernel(x, norm_in, w_in, conv_w, conv_b, gate_x_w, gate_x_b, gate_a_w, gate_a_b, lru_lambda, gn_gain, w_proj_a, w_proj_b, w_out, norm_final, loss_target, m_norm_in, m_w_in, m_conv_w, m_conv_b, m_gate_x_w, m_gate_x_b, m_gate_a_w, m_gate_a_b, m_lru_lambda, m_gn_gain, m_w_proj_a, m_w_proj_b, m_w_out, m_norm_final, v_norm_in, v_w_in, v_conv_w, v_conv_b, v_gate_x_w, v_gate_x_b, v_gate_a_w, v_gate_a_b, v_lru_lambda, v_gn_gain, v_w_proj_a, v_w_proj_b, v_w_out, v_norm_final):
    xi, yi, ci = _place()
    me = 4 * xi + 2 * yi + ci
    core = ci.astype(jnp.int32).reshape(1)
    nshard = D // NDEV
    nb = x.shape[0]
    t = nb * S
    x2d = x.reshape(t, D)
    tgt2d = loss_target.reshape(t, D)
    g_final = norm_final.reshape(1, D)
    wbd = jnp.concatenate([_block_diag(gate_x_w[0]), _block_diag(gate_a_w[0])], axis=-1).astype(BF16)
    tables = _retention_tables()

    wp_own = jnp.concatenate([w_proj_a[0], w_proj_b[0], w_out[0]], axis=0).astype(BF16)
    tiny = jnp.concatenate([conv_w[0], jnp.pad(gn_gain[0], ((0, 0), (0, nshard - DK // NDEV)))], axis=0)
    proj, h, wg, tiny_g = _inproj_gather(x2d, norm_in, w_in[0].astype(BF16), tiny, *_gather_order(xi, yi, ci))
    conv_w_full = tiny_g[:, 0:4, :].transpose(1, 0, 2).reshape(4, D)
    gain3 = tiny_g[:, 4:8, :DK // NDEV].transpose(1, 0, 2).reshape(HEADS, 1, DK)

    ya, hs, xc, gi, gr = _lru_fwd(proj, conv_w_full, conv_b, wbd, gate_x_b, gate_a_b, lru_lambda, nb)
    yb, qr, kr, o, rs, wpg = _ret_fwd(proj, gain3, tables, nb, wp_own)
    dx2, dya, dyb, dpc, merged, doa, dob, g_fin, loss_vec = _tail(ya, yb, proj, x2d, tgt2d, wpg, g_final)
    g_pa, g_pb, g_out = _tail_wgrad(ya, yb, merged, doa, dob, dx2)

    own = [g.reshape(NDEV, nshard, D) for g in (g_pa, g_pb, g_out)]
    dpa, g_wbd, g_vec, *got = _lru_bwd(proj, hs, xc, gi, gr, dya, conv_w_full, wbd, lru_lambda, nb, own)
    sums = _chip_sum(own, got, core)
    dpb, g_gain, *parts = _ret_bwd(proj, qr, kr, o, rs, dyb, gain3, tables, nb, sums)

    grad_x, g_norm_in = _inproj_dgrad(dpa, dpb, dpc, wg, x2d, dx2, norm_in)
    grad_x = grad_x.reshape(nb, S, D)

    gain_rows = jnp.pad(g_gain.reshape(HEADS, NDEV, DK // NDEV), ((0, 0), (0, 0), (0, nshard - DK // NDEV)))
    vec = jnp.concatenate([g_norm_in, g_vec[0:4], g_fin, loss_vec, jnp.zeros((1, D), F32), g_vec[4:8],
                           gain_rows.reshape(HEADS, D)], axis=0)
    g_gx = _block_diag_back(g_wbd[:, :, :CB]).reshape(D // 2, 128)
    g_ga = _block_diag_back(g_wbd[:, :, CB:]).reshape(D // 2, 128)
    parts_in, vec_all, gx_all, ga_all = _inproj_wgrad_rs(h, dpa, dpb, dpc, _rs_order(xi, yi, ci),
                                                         [vec, g_gx, g_ga])
    gx_all = gx_all.reshape(NDEV, D, 64)
    ga_all = ga_all.reshape(NDEV, D, 64)
    parts = [parts_in] + list(parts)

    res = {}
    (out,) = _adamw("adamw_w_in", [(parts[0], w_in[0], m_w_in[0], v_w_in[0])])
    res["w_in"] = [o[None] for o in out]
    square = [("w_proj_a", w_proj_a, m_w_proj_a, v_w_proj_a), ("w_proj_b", w_proj_b, m_w_proj_b, v_w_proj_b),
              ("w_out", w_out, m_w_out, v_w_out)]
    outs = _adamw("adamw_square", [(parts[1 + k], w[0], m[0], v[0]) for k, (_, w, m, v) in enumerate(square)])
    for (nm, _, _, _), out in zip(square, outs):
        res[nm] = [o[None] for o in out]

    row = lambda a: a.reshape(1, D)
    gate = lambda a: a.reshape(D, 64)
    groups = [("norm_in", norm_in, m_norm_in, v_norm_in, row), ("conv_b", conv_b, m_conv_b, v_conv_b, row),
              ("gate_x_b", gate_x_b, m_gate_x_b, v_gate_x_b, row), ("gate_a_b", gate_a_b, m_gate_a_b, v_gate_a_b, row),
              ("lru_lambda", lru_lambda, m_lru_lambda, v_lru_lambda, row),
              ("norm_final", norm_final, m_norm_final, v_norm_final, row),
              ("conv_w", conv_w, m_conv_w, v_conv_w, lambda a: a[0]), ("gn_gain", gn_gain, m_gn_gain, v_gn_gain, lambda a: a[0]),
              ("gate_x_w", gate_x_w, m_gate_x_w, v_gate_x_w, gate), ("gate_a_w", gate_a_w, m_gate_a_w, v_gate_a_w, gate)]
    small_out, loss = _adamw_small(me.astype(jnp.int32).reshape(1), vec_all, gx_all, ga_all,
                                   [tuple(view(a) for a in (w, m, v)) for _, w, m, v, view in groups])
    for (nm, w, _, _, _), out in zip(groups, small_out):
        res[nm] = [o.reshape(w.shape) for o in out]
    loss = loss.reshape(())

    order = ["norm_in", "w_in", "conv_w", "conv_b", "gate_x_w", "gate_x_b", "gate_a_w", "gate_a_b", "lru_lambda",
             "gn_gain", "w_proj_a", "w_proj_b", "w_out", "norm_final"]
    outs = [loss, grad_x]
    for k in range(4):
        outs += [res[nm][k] for nm in order]
    return tuple(outs)
```

```python
import numpy as np

import jax
import jax.numpy as jnp
from jax import lax
from jax.experimental import pallas as pl
from jax.experimental.pallas import tpu as pltpu

F32 = jnp.float32
BF16 = jnp.bfloat16
MESH = pl.DeviceIdType.MESH

D = 1024
S = 2048
NSEG = 8
NDEV = 8
HEADS = 4
DK = 256
CH = 256
NCH = S // CH
CB = 256
NCB = D // CB
RC = 512
RC_CONV = 128
SCAN_GROUP = 8
EPS = 1e-6
LRU_C = 8.0
VMEM_LIMIT = 56 * 1024 * 1024

ADAM_LR = 0.001
ADAM_B1 = 0.9
ADAM_B2 = 0.999
ADAM_EPS = 1e-08
ADAM_WD = 0.01
ADAM_STEP = 10


def _params(sem=None):
    return pltpu.CompilerParams(dimension_semantics=sem, vmem_limit_bytes=VMEM_LIMIT)


def _dot(a, b):
    return jnp.dot(a, b, preferred_element_type=F32)


def _dot_nt(a, b):
    return lax.dot_general(a, b, (((1,), (1,)), ((), ())), preferred_element_type=F32)


def _dot_tn(a, b):
    return lax.dot_general(a, b, (((0,), (0,)), ((), ())), preferred_element_type=F32)


def _sigmoid(x):
    return jax.nn.sigmoid(x)


def _expm1_nonpos(x):
    poly = x * (1.0 + x * (0.5 + x * (1.0 / 6.0 + x * (1.0 / 24.0))))
    return jnp.where(x > -0.05, poly, jnp.exp(x) - 1.0)


def _softplus(x):
    return jnp.maximum(x, 0.0) + jnp.log(1.0 + jnp.exp(-jnp.abs(x)))


def _rows(c, n):
    return pl.ds(pl.multiple_of(c * n, n), n)


def _window_before(ref, c, n):
    r0 = c * n
    if ref.dtype == BF16:
        prev = ref[pl.ds(pl.multiple_of(jnp.maximum(r0 - 16, 0), 16), 16), :].astype(F32)[8:, :]
    else:
        prev = ref[pl.ds(pl.multiple_of(jnp.maximum(r0 - 8, 0), 8), 8), :]
    prev = jnp.where(c > 0, prev, 0.0)
    return jnp.concatenate([prev, ref[_rows(c, n), :].astype(F32)], axis=0)


def _shift_down(win, s, n):
    if s == 0:
        return win[8:, :]
    return pltpu.roll(win, s, 0)[8:, :]


def _shift_up(win, s, n):
    if s == 0:
        return win[:n, :]
    return pltpu.roll(win, n + 8 - s, 0)[:n, :]


HALF = D // 2
GATHER_SLOTS = [("own", None, 0), ("own", None, 1), ("sib", None, 0), ("sib", None, 1)]
for _j, _h in ((0, 0), (1, 0), (0, 1), (1, 1), (2, 0), (2, 1)):
    GATHER_SLOTS += [("ici", _j, _h), ("fwd", _j, _h)]
NSLOT = len(GATHER_SLOTS)


def _gather_order(x, y, c):
    chips = [(1 - x, y), (x, 1 - y), (1 - x, 1 - y)]
    segs, halves = [], []
    for kind, j, h in GATHER_SLOTS:
        if kind == "own":
            seg = 4 * x + 2 * y + c
        elif kind == "sib":
            seg = 4 * x + 2 * y + 1 - c
        else:
            px, py = chips[j]
            seg = 4 * px + 2 * py + (c if kind == "ici" else 1 - c)
        segs.append(seg)
        halves.append(h)
    return jnp.stack(segs).astype(jnp.int32), jnp.asarray(halves, jnp.int32)


def _inproj_gather(x2d, g_in, w_own, tiny_own, order, halves):
    t = x2d.shape[0]
    tm = 2048
    nt = t // tm
    tx = 1024
    nx = tm // tx

    def body(order_ref, half_ref, g_ref, x_hbm, w_own_ref, tiny_own_ref,
             proj_ref, h_hbm, wg_ref, tinyg_ref,
             w_all, h_all, x_s, send_sems, recv_sems, own_sems, out_sems, tiny_send, tiny_recv, tiny_own_sem,
             x_sems, h_sem):
        k, i = pl.program_id(0), pl.program_id(1)
        x, y, c = _place()
        me, sibling = (x, y, c), (x, y, 1 - c)
        mine = 4 * x + 2 * y + c
        chips = [(1 - x, y), (x, 1 - y), (1 - x, 1 - y)]

        def copy(h, n, block, to, own_src=False):
            px, py, pc = block
            dst = w_all.at[4 * px + 2 * py + pc, h]
            return pltpu.make_async_remote_copy(
                src_ref=w_own_ref.at[:, pl.ds(h * HALF, HALF)] if own_src else dst, dst_ref=dst,
                send_sem=send_sems.at[h, n], recv_sem=recv_sems.at[h, n], device_id=to, device_id_type=MESH)

        def tiny_copy(n, block, to, own_src=False):
            px, py, pc = block
            dst = tinyg_ref.at[4 * px + 2 * py + pc]
            return pltpu.make_async_remote_copy(
                src_ref=tiny_own_ref if own_src else dst, dst_ref=dst,
                send_sem=tiny_send.at[n], recv_sem=tiny_recv.at[n], device_id=to, device_id_type=MESH)

        def own_copy(h):
            return pltpu.make_async_copy(w_own_ref.at[:, pl.ds(h * HALF, HALF)], w_all.at[mine, h], own_sems.at[h])

        tiny_mine = pltpu.make_async_copy(tiny_own_ref, tinyg_ref.at[mine], tiny_own_sem)

        def keep_copy(n):
            h = GATHER_SLOTS[n][2]
            return pltpu.make_async_copy(w_all.at[order_ref[n], h], wg_ref.at[order_ref[n], :, pl.ds(h * HALF, HALF)],
                                         out_sems.at[n])

        near = [(0, sibling), (1, (*chips[0], c)), (2, (*chips[1], c))]
        first = [copy(h, n, me, to, True) for h in (0, 1) for n, to in near]
        tiny_first = [tiny_copy(0, me, sibling, True)] + [tiny_copy(1 + j, me, (*chip, c), True) for j, chip in enumerate(chips)]

        def relay(h, j):
            seg = w_all.at[4 * chips[j][0] + 2 * chips[j][1] + c, h]
            return pltpu.make_async_remote_copy(
                src_ref=seg, dst_ref=seg, send_sem=send_sems.at[h, 3], recv_sem=recv_sems.at[h, 3],
                device_id=(*chips[1 - j], c), device_id_type=MESH)

        for n, (kind, j, h) in enumerate(GATHER_SLOTS):
            @pl.when(jnp.logical_and(k == n, i == 0))
            def _():
                if n == 0:
                    own_copy(0).start()
                    own_copy(1).start()
                    tiny_mine.start()
                    for cp in first + tiny_first:
                        cp.start()
                if kind == "own":
                    own_copy(h).wait()
                elif kind == "sib":
                    copy(h, 0, sibling, me).wait_recv()
                elif kind == "ici":
                    copy(h, 1 + j, (*chips[j], c), me).wait_recv()
                    copy(h, 4 + j, (*chips[j], c), sibling).start()
                    if j < 2:
                        @pl.when(c == j)
                        def _():
                            relay(h, j).start()
                else:
                    copy(h, 4 + j, (*chips[j], 1 - c), me).wait_recv()
                keep_copy(n).start()

        rows = pl.ds(pl.multiple_of(i * tm, tm), tm)

        def x_copy(n):
            return pltpu.make_async_copy(x_hbm.at[pl.ds(n * tx, tx), :], x_s.at[n % 2], x_sems.at[n % 2])

        keep_h = pltpu.make_async_copy(h_all, h_hbm, h_sem)

        for step in range(nt):
            @pl.when(jnp.logical_and(k == 0, i == step))
            def _():
                if step == 0:
                    x_copy(0).start()
                for n in range(step * nx, (step + 1) * nx):
                    x_copy(n).wait()
                    if n + 1 < nt * nx:
                        x_copy(n + 1).start()
                    xv = x_s[n % 2]
                    r = lax.rsqrt(jnp.mean(xv * xv, axis=-1, keepdims=True) + EPS)
                    h_all[pl.ds(n * tx, tx), :] = (xv * r * g_ref[...]).astype(BF16)
                if step == nt - 1:
                    keep_h.start()

        proj_ref[...] = _dot(h_all[rows, :], w_all[order_ref[k], half_ref[k]]).astype(BF16)

        @pl.when(jnp.logical_and(k == NSLOT - 1, i == nt - 1))
        def _():
            for j, chip in enumerate(chips):
                tiny_copy(1 + j, (*chip, c), me).wait_recv()
                tiny_copy(4 + j, (*chip, c), sibling).start()
            tiny_copy(0, sibling, me).wait_recv()
            for j, chip in enumerate(chips):
                tiny_copy(4 + j, (*chip, 1 - c), me).wait_recv()
            for cp in first + tiny_first:
                cp.wait_send()
            for j, chip in enumerate(chips):
                tiny_copy(4 + j, (*chip, c), sibling).wait_send()
                for h in (0, 1):
                    copy(h, 4 + j, (*chip, c), sibling).wait_send()
            for h in (0, 1):
                relay(h, 0).wait_send()
            tiny_mine.wait()
            keep_h.wait()
            for n in range(NSLOT):
                keep_copy(n).wait()

    return pl.pallas_call(
        body, name="inproj_gather",
        grid_spec=pltpu.PrefetchScalarGridSpec(
            num_scalar_prefetch=2, grid=(NSLOT, nt),
            in_specs=[pl.BlockSpec((1, D), lambda k, i, order_ref, half_ref: (0, 0)),
                      ANY, ANY, ANY],
            out_specs=[pl.BlockSpec((None, tm, HALF), lambda k, i, order_ref, half_ref: (order_ref[k], i, half_ref[k])),
                       ANY, ANY, ANY],
            scratch_shapes=[pltpu.VMEM((NDEV, 2, D, HALF), BF16), pltpu.VMEM((t, D), BF16), pltpu.VMEM((2, tx, D), F32),
                            pltpu.SemaphoreType.DMA((2, 7)), pltpu.SemaphoreType.DMA((2, 7)),
                            pltpu.SemaphoreType.DMA((2,)), pltpu.SemaphoreType.DMA((NSLOT,)),
                            pltpu.SemaphoreType.DMA((7,)), pltpu.SemaphoreType.DMA((7,)), pltpu.SemaphoreType.DMA,
                            pltpu.SemaphoreType.DMA((2,)), pltpu.SemaphoreType.DMA]),
        out_shape=[jax.ShapeDtypeStruct((NSEG, t, D), BF16), jax.ShapeDtypeStruct((t, D), BF16),
                   jax.ShapeDtypeStruct((NDEV,) + w_own.shape, BF16),
                   jax.ShapeDtypeStruct((NDEV,) + tiny_own.shape, F32)],
        compiler_params=_params(("arbitrary", "arbitrary")),
    )(order, halves, g_in, x2d, w_own, tiny_own)


def _tile_scan(a, u):
    row = lax.broadcasted_iota(jnp.int32, a.shape, 0)
    for d in (1, 2, 4):
        m = row >= d
        a_sh = pltpu.roll(a, d, 0)
        u_sh = pltpu.roll(u, d, 0)
        u = jnp.where(m, a * u_sh + u, u)
        a = jnp.where(m, a * a_sh, a)
    return a, u


def _tile_scan_rev(a, w):
    row = lax.broadcasted_iota(jnp.int32, a.shape, 0)
    for d in (1, 2, 4):
        m = row < 8 - d
        a_sh = pltpu.roll(a, 8 - d, 0)
        w_sh = pltpu.roll(w, 8 - d, 0)
        w = jnp.where(m, a * w_sh + w, w)
        a = jnp.where(m, a * a_sh, a)
    return a, w


def _lru_gates(xa_ref, c, cw_ref, cb_ref, wbd_ref, bx_ref, ba_ref, sp):
    win = _window_before(xa_ref, c, RC)
    xc = cb_ref[...] + cw_ref[3:4, :] * _shift_down(win, 0, RC)
    for s in (1, 2, 3):
        xc = xc + cw_ref[3 - s:4 - s, :] * _shift_down(win, s, RC)
    z = _dot(xc.astype(BF16), wbd_ref[...])
    gi = _sigmoid(z[:, :CB] + bx_ref[...])
    gr = _sigmoid(z[:, CB:] + ba_ref[...])
    log_a = -LRU_C * gr * sp
    return win, xc, gi, gr, log_a


def _lru_fwd(proj, conv_w, conv_b, wbd, bx, ba, lam, nb):
    t = nb * S

    def body(xa_ref, ga_ref, cw_ref, cb_ref, wbd_ref, bx_ref, ba_ref, lam_ref,
             ya_ref, hs_ref, xc_ref, gi_ref, gr_ref, a_s, u_s):
        sp = _softplus(-lam_ref[...])

        def gates(c, carry):
            _, xc, gi, gr, log_a = _lru_gates(xa_ref, c, cw_ref, cb_ref, wbd_ref, bx_ref, ba_ref, sp)
            rows = _rows(c, RC)
            a_s[rows, :] = jnp.exp(log_a)
            u_s[rows, :] = jnp.sqrt(-_expm1_nonpos(2.0 * log_a)) * (gi * xc)
            xc_ref[rows, :] = xc
            gi_ref[rows, :] = gi
            gr_ref[rows, :] = gr
            return carry

        lax.fori_loop(0, S // RC, gates, 0)

        def scan(g, h):
            for k in range(SCAN_GROUP):
                rows = pl.ds(pl.multiple_of(g * (8 * SCAN_GROUP), 8 * SCAN_GROUP) + 8 * k, 8)
                a_cum, u_cum = _tile_scan(a_s[rows, :], u_s[rows, :])
                hs_ref[rows, :] = u_cum + a_cum * h
                h = u_cum[7:8, :] + a_cum[7:8, :] * h
            return h

        lax.fori_loop(0, S // (8 * SCAN_GROUP), scan, jnp.zeros((1, CB), F32))

        def gate_out(c, carry):
            ga = ga_ref[_rows(c, RC), :].astype(F32)
            ya_ref[_rows(c, RC), :] = (ga * _sigmoid(ga) * hs_ref[_rows(c, RC), :]).astype(BF16)
            return carry

        lax.fori_loop(0, S // RC, gate_out, 0)

    vec = pl.BlockSpec((1, CB), lambda b, cb: (0, cb))
    blk = pl.BlockSpec((S, CB), lambda b, cb: (b, cb))
    return pl.pallas_call(
        body, name="lru_fwd", grid=(nb, NCB),
        in_specs=[pl.BlockSpec((None, S, CB), lambda b, cb: (0, b, cb)),
                  pl.BlockSpec((None, S, CB), lambda b, cb: (1, b, cb)),
                  pl.BlockSpec((4, CB), lambda b, cb: (0, cb)),
                  vec,
                  pl.BlockSpec((None, CB, 2 * CB), lambda b, cb: (cb, 0, 0)),
                  vec, vec, vec],
        out_specs=[blk] + [pl.BlockSpec((None, None, S, CB), lambda b, cb: (b, cb, 0, 0))] * 4,
        out_shape=[jax.ShapeDtypeStruct((t, D), BF16)] + [jax.ShapeDtypeStruct((nb, NCB, S, CB), F32)] * 4,
        scratch_shapes=[pltpu.VMEM((S, CB), F32), pltpu.VMEM((S, CB), F32)],
        compiler_params=_params(("arbitrary", "arbitrary")),
    )(proj, proj, conv_w, conv_b, wbd, bx, ba, lam)


def _lru_bwd(proj, hs, xc_f, gi_f, gr_f, dya, conv_w, wbd, lam, nb, give):
    t = nb * S
    ng = len(give)

    def body(xa_ref, ga_ref, hs_ref, xc_s, gi_s, gr_s, dya_ref, cw_ref, wbd_ref, lam_ref, *rest):
        give_refs, rest = rest[:ng], rest[ng:]
        dp_ref, dwbd_ref, vec_ref = rest[:3]
        got_refs, rest = rest[3:3 + ng], rest[3 + ng:]
        a_s, dl_s, dh_s, dxc_s, acc_s, send_sems, recv_sems = rest
        b = pl.program_id(1)
        exchange = _sibling_copies(give_refs, got_refs, send_sems, recv_sems)

        @pl.when(jnp.logical_and(pl.program_id(0) == 0, b == 0))
        def _():
            for cp in exchange:
                cp.start()

        lam_v = lam_ref[...]
        sp = _softplus(-lam_v)
        acc_s[...] = jnp.zeros_like(acc_s)

        @pl.when(b == 0)
        def _():
            dwbd_ref[...] = jnp.zeros_like(dwbd_ref)
            vec_ref[...] = jnp.zeros_like(vec_ref)

        def gates(c, carry):
            rows = _rows(c, RC)
            a_s[rows, :] = jnp.exp(-LRU_C * gr_s[rows, :] * sp)
            ga = ga_ref[rows, :].astype(F32)
            sg = _sigmoid(ga)
            dya_c = dya_ref[rows, :]
            dl_s[rows, :] = dya_c * (ga * sg)
            dp_ref[1, rows, :] = (dya_c * hs_ref[rows, :] * (sg * (1.0 + ga * (1.0 - sg)))).astype(BF16)
            return carry

        lax.fori_loop(0, S // RC, gates, 0)

        def scan(i, g_in):
            base = pl.multiple_of((S // (8 * SCAN_GROUP) - 1 - i) * (8 * SCAN_GROUP), 8 * SCAN_GROUP)
            row = lax.broadcasted_iota(jnp.int32, (8, CB), 0)
            for k in reversed(range(SCAN_GROUP)):
                rows = pl.ds(base + 8 * k, 8)
                a = a_s[rows, :]
                dl = dl_s[rows, :]
                a_cum, g_loc = _tile_scan_rev(a, a * dl)
                g = g_loc + a_cum * g_in
                dh_s[rows, :] = dl + jnp.where(row < 7, pltpu.roll(g, 7, 0), g_in)
                g_in = g_loc[0:1, :] + a_cum[0:1, :] * g_in
            return g_in

        lax.fori_loop(0, S // (8 * SCAN_GROUP), scan, jnp.zeros((1, CB), F32))

        dxc_s[pl.ds(S, 8), :] = jnp.zeros((8, CB), F32)

        def grads(c, carry):
            rows = _rows(c, RC)
            dh = dh_s[rows, :]
            h_prev = _shift_down(_window_before(hs_ref, c, RC), 1, RC)
            xc, gi, gr, a = xc_s[rows, :], gi_s[rows, :], gr_s[rows, :], a_s[rows, :]
            mult = jnp.sqrt(-_expm1_nonpos(-2.0 * LRU_C * gr * sp))
            dmult = dh * gi * xc
            d_log_a = dh * h_prev * a - dmult * (a * a) / mult
            dzi = dh * mult * xc * gi * (1.0 - gi)
            dzr = d_log_a * (-LRU_C * sp) * gr * (1.0 - gr)
            dz = jnp.concatenate([dzi, dzr], axis=1).astype(BF16)
            dxc_s[rows, :] = dh * mult * gi + _dot_nt(dz, wbd_ref[...])
            dwbd_ref[...] += _dot_tn(xc.astype(BF16), dz)
            acc_s[1:2, :] += jnp.sum(dzi, axis=0, keepdims=True)
            acc_s[2:3, :] += jnp.sum(dzr, axis=0, keepdims=True)
            acc_s[3:4, :] += jnp.sum(d_log_a * (-LRU_C * gr), axis=0, keepdims=True)
            return carry

        lax.fori_loop(0, S // RC, grads, 0)

        def conv_bwd(c, carry):
            rows = _rows(c, RC_CONV)
            dwin = dxc_s[pl.ds(pl.multiple_of(c * RC_CONV, RC_CONV), RC_CONV + 8), :]
            dxc = dwin[:RC_CONV, :]
            xwin = _window_before(xa_ref, c, RC_CONV)
            dxa = cw_ref[3:4, :] * dxc
            acc_s[0:1, :] += jnp.sum(dxc, axis=0, keepdims=True)
            acc_s[7:8, :] += jnp.sum(dxc * _shift_down(xwin, 0, RC_CONV), axis=0, keepdims=True)
            for s in (1, 2, 3):
                dxa = dxa + cw_ref[3 - s:4 - s, :] * _shift_up(dwin, s, RC_CONV)
                acc_s[7 - s:8 - s, :] += jnp.sum(dxc * _shift_down(xwin, s, RC_CONV), axis=0, keepdims=True)
            dp_ref[0, rows, :] = dxa.astype(BF16)
            return carry

        lax.fori_loop(0, S // RC_CONV, conv_bwd, 0)

        row = lax.broadcasted_iota(jnp.int32, acc_s.shape, 0)
        vec_ref[...] += jnp.where(row == 3, acc_s[...] * (-_sigmoid(-lam_v)), acc_s[...])

        @pl.when(jnp.logical_and(pl.program_id(0) == NCB - 1, b == nb - 1))
        def _():
            for cp in exchange:
                cp.wait()

    vec = pl.BlockSpec((1, CB), lambda cb, b: (0, cb))
    blk = pl.BlockSpec((S, CB), lambda cb, b: (b, cb))
    own = pl.BlockSpec((None, None, S, CB), lambda cb, b: (b, cb, 0, 0))
    return pl.pallas_call(
        body, name="lru_bwd", grid=(NCB, nb),
        in_specs=[pl.BlockSpec((None, S, CB), lambda cb, b: (0, b, cb)),
                  pl.BlockSpec((None, S, CB), lambda cb, b: (1, b, cb)),
                  own, own, own, own, blk,
                  pl.BlockSpec((4, CB), lambda cb, b: (0, cb)),
                  pl.BlockSpec((None, CB, 2 * CB), lambda cb, b: (cb, 0, 0)),
                  vec] + [ANY] * ng,
        out_specs=[pl.BlockSpec((2, S, CB), lambda cb, b: (0, b, cb)),
                   pl.BlockSpec((None, CB, 2 * CB), lambda cb, b: (cb, 0, 0)),
                   pl.BlockSpec((8, CB), lambda cb, b: (0, cb))] + [ANY] * ng,
        out_shape=[jax.ShapeDtypeStruct((2, t, D), BF16),
                   jax.ShapeDtypeStruct((NCB, CB, 2 * CB), F32),
                   jax.ShapeDtypeStruct((8, D), F32)]
        + [jax.ShapeDtypeStruct((4,) + g.shape[1:], g.dtype) for g in give],
        scratch_shapes=[pltpu.VMEM((S, CB), F32), pltpu.VMEM((S, CB), F32), pltpu.VMEM((S, CB), F32),
                        pltpu.VMEM((S + 8, CB), F32), pltpu.VMEM((8, CB), F32),
                        pltpu.SemaphoreType.DMA((ng, 4)), pltpu.SemaphoreType.DMA((ng, 4))],
        compiler_params=_params(("arbitrary", "arbitrary")),
    )(proj, proj, hs, xc_f, gi_f, gr_f, dya, conv_w, wbd, lam, *give)


def _retention_tables():
    f32 = np.float32
    log_g = np.log1p(-(f32(2.0) ** (f32(-5.0) - np.arange(HEADS, dtype=f32)))).astype(f32)
    idx = np.arange(CH, dtype=f32)
    diff = idx[:, None] - idx[None, :]
    inner = np.where(diff >= 0, np.exp(np.maximum(diff, f32(0.0))[None] * log_g[:, None, None]), f32(0.0)).astype(f32)
    cross = np.exp((idx[None, :] + f32(1.0)) * log_g[:, None]).astype(f32)
    state = np.exp((f32(CH - 1.0) - idx[None, :]) * log_g[:, None]).astype(f32)
    cross = np.ascontiguousarray(np.broadcast_to(cross[:, :, None], (HEADS, CH, DK)))
    state = np.ascontiguousarray(np.broadcast_to(state[:, :, None], (HEADS, CH, DK)))
    half = DK // 2
    freqs = (f32(10000.0) ** (-np.arange(half, dtype=f32) / f32(half))).astype(f32)
    ang = (np.arange(S, dtype=f32)[:, None] * freqs[None, :]).astype(f32)
    return tuple(jnp.asarray(a) for a in (inner, cross, state, np.cos(ang).astype(f32), np.sin(ang).astype(f32)))


def _rotate(x, cos, sin):
    half = DK // 2
    x1, x2 = x[:, :half], x[:, half:]
    return jnp.concatenate([x1 * cos - x2 * sin, x1 * sin + x2 * cos], axis=1)


def _rotate_back(d, cos, sin):
    half = DK // 2
    d1, d2 = d[:, :half], d[:, half:]
    return jnp.concatenate([d1 * cos + d2 * sin, d2 * cos - d1 * sin], axis=1)


def _ret_fwd(proj, gain, tables, nb, wp_own):
    t = nb * S
    inner_t, cross_t, state_t, cos_t, sin_t = tables

    def body(q_ref, k_ref, v_ref, gb_ref, gain_ref, dm_ref, cd_ref, sd_ref, cos_ref, sin_ref, wp_ref,
             yb_ref, qr_ref, kr_ref, o_ref, rs_ref, wpg_ref, r_s, send_sems, recv_sems, own_sems):
        b, hd = pl.program_id(0), pl.program_id(1)
        own, first, arrive, forward, others = _gather_copies([wp_ref], [wpg_ref], send_sems, recv_sems, own_sems)

        @pl.when(jnp.logical_and(b == 0, hd == 0))
        def _():
            for cp in own + first:
                cp.start()

        @pl.when(jnp.logical_and(b == nb - 1, hd == HEADS - 1))
        def _():
            for came, on in zip(arrive, forward):
                came.wait_recv()
                on.start()

        r_s[...] = jnp.zeros_like(r_s)
        chunk_decay = cd_ref[CH - 1:CH, :]

        def chunk(c, carry):
            rows = _rows(c, CH)
            cos, sin = cos_ref[rows, :], sin_ref[rows, :]
            qr = _rotate(q_ref[rows, :].astype(F32), cos, sin).astype(BF16)
            kr = (_rotate(k_ref[rows, :].astype(F32), cos, sin) * (DK ** -0.5)).astype(BF16)
            vb = v_ref[rows, :]
            v = vb.astype(F32)
            qr_ref[rows, :] = qr
            kr_ref[rows, :] = kr
            r = r_s[...]
            rb = r.astype(BF16)
            rs_ref[c] = rb
            p = (_dot_nt(qr, kr) * dm_ref[...]).astype(BF16)
            o = _dot(p, vb) + _dot(qr, rb) * cd_ref[...]
            r_s[...] = chunk_decay * r + _dot_tn(kr, (v * sd_ref[...]).astype(BF16))
            o_ref[rows, :] = o
            oc = o - jnp.mean(o, axis=-1, keepdims=True)
            rstd = lax.rsqrt(jnp.mean(oc * oc, axis=-1, keepdims=True) + EPS)
            gb = gb_ref[rows, :].astype(F32)
            yb_ref[rows, :] = (gb * _sigmoid(gb) * (oc * rstd * gain_ref[...])).astype(BF16)
            return carry

        lax.fori_loop(0, NCH, chunk, 0, unroll=2)

        @pl.when(jnp.logical_and(b == nb - 1, hd == HEADS - 1))
        def _():
            for cp in others:
                cp.wait_recv()
            for cp in first + forward:
                cp.wait_send()
            for cp in own:
                cp.wait()

    seg = lambda s: pl.BlockSpec((None, S, DK), lambda b, h: (s, b, h))
    tab = pl.BlockSpec((None, CH, DK), lambda b, h: (h, 0, 0))
    rot = pl.BlockSpec((S, DK // 2), lambda b, h: (0, 0))
    blk = pl.BlockSpec((S, DK), lambda b, h: (b, h))
    return pl.pallas_call(
        body, name="ret_fwd", grid=(nb, HEADS),
        in_specs=[seg(2), seg(3), seg(4), seg(5),
                  pl.BlockSpec((None, 1, DK), lambda b, h: (h, 0, 0)),
                  tab, tab, tab, rot, rot, ANY],
        out_specs=[blk, blk, blk, blk,
                   pl.BlockSpec((None, None, NCH, DK, DK), lambda b, h: (b, h, 0, 0, 0)), ANY],
        out_shape=[jax.ShapeDtypeStruct((t, D), BF16), jax.ShapeDtypeStruct((t, D), BF16),
                   jax.ShapeDtypeStruct((t, D), BF16), jax.ShapeDtypeStruct((t, D), F32),
                   jax.ShapeDtypeStruct((nb, HEADS, NCH, DK, DK), BF16),
                   jax.ShapeDtypeStruct((NDEV,) + wp_own.shape, wp_own.dtype)],
        scratch_shapes=[pltpu.VMEM((DK, DK), F32),
                        pltpu.SemaphoreType.DMA((1, 7)), pltpu.SemaphoreType.DMA((1, 7)), pltpu.SemaphoreType.DMA((1,))],
        compiler_params=_params(("arbitrary", "arbitrary")),
    )(proj, proj, proj, proj, gain, inner_t, cross_t, state_t, cos_t, sin_t, wp_own)


def _ret_bwd(proj, qr, kr, o, rs, dyb, gain, tables, nb, sums):
    t = nb * S
    ns = len(sums)
    inner_t, cross_t, state_t, cos_t, sin_t = tables

    def body(qr_ref, kr_ref, v_ref, gb_ref, o_ref, dyb_ref, rs_ref, gain_ref, dm_ref, cd_ref, sd_ref,
             cos_ref, sin_ref, *rest):
        sum_refs, rest = rest[:ns], rest[ns:]
        dp_ref, dgain_ref = rest[:2]
        part_refs, rest = rest[2:2 + ns], rest[2 + ns:]
        dr_s, send_sems, recv_sems, local_sems = rest
        mine, sends, recvs = _chip_copies(sum_refs, part_refs, send_sems, recv_sems, local_sems)

        @pl.when(jnp.logical_and(pl.program_id(0) == 0, pl.program_id(1) == 0))
        def _():
            for cp in mine + sends:
                cp.start()

        dr_s[...] = jnp.zeros_like(dr_s)
        chunk_decay = cd_ref[CH - 1:CH, :]

        @pl.when(pl.program_id(1) == 0)
        def _():
            dgain_ref[...] = jnp.zeros_like(dgain_ref)

        def chunk(i, carry):
            c = NCH - 1 - i
            rows = _rows(c, CH)
            gain_v = gain_ref[...]
            o_c = o_ref[rows, :]
            oc = o_c - jnp.mean(o_c, axis=-1, keepdims=True)
            rstd = lax.rsqrt(jnp.mean(oc * oc, axis=-1, keepdims=True) + EPS)
            yn = oc * rstd
            gb = gb_ref[rows, :].astype(F32)
            sg = _sigmoid(gb)
            dyb_c = dyb_ref[rows, :]
            dgn = dyb_c * (gb * sg)
            dp_ref[3, rows, :] = (dyb_c * (yn * gain_v) * (sg * (1.0 + gb * (1.0 - sg)))).astype(BF16)
            dgain_ref[...] += jnp.sum(dgn * yn, axis=0, keepdims=True)
            dyn = dgn * gain_v
            do = rstd * (dyn - jnp.mean(dyn, axis=-1, keepdims=True)
                         - yn * jnp.mean(dyn * yn, axis=-1, keepdims=True))
            dob = do.astype(BF16)
            dox = (do * cd_ref[...]).astype(BF16)

            q_c, k_c = qr_ref[rows, :], kr_ref[rows, :]
            vb = v_ref[rows, :]
            v = vb.astype(F32)
            vs = (v * sd_ref[...]).astype(BF16)
            rb = rs_ref[c]
            d_r = dr_s[...]
            drb = d_r.astype(BF16)
            dm = dm_ref[...]
            p = (_dot_nt(q_c, k_c) * dm).astype(BF16)
            dpm = (_dot_nt(dob, vb) * dm).astype(BF16)
            dq = _dot(dpm, k_c) + _dot_nt(dox, rb)
            dk = _dot_tn(dpm, q_c) + _dot_nt(vs, drb)
            dv = _dot_tn(p, dob) + _dot(k_c, drb) * sd_ref[...]
            dr_s[...] = chunk_decay * d_r + _dot_tn(q_c, dox)

            cos, sin = cos_ref[rows, :], sin_ref[rows, :]
            dp_ref[0, rows, :] = _rotate_back(dq, cos, sin).astype(BF16)
            dp_ref[1, rows, :] = (_rotate_back(dk, cos, sin) * (DK ** -0.5)).astype(BF16)
            dp_ref[2, rows, :] = dv.astype(BF16)
            return carry

        lax.fori_loop(0, NCH, chunk, 0, unroll=2)

        @pl.when(jnp.logical_and(pl.program_id(0) == HEADS - 1, pl.program_id(1) == nb - 1))
        def _():
            for cp in recvs:
                cp.wait_recv()
            for cp in sends:
                cp.wait_send()
            for cp in mine:
                cp.wait()

    seg = lambda s: pl.BlockSpec((None, S, DK), lambda h, b: (s, b, h))
    tab = pl.BlockSpec((None, CH, DK), lambda h, b: (h, 0, 0))
    rot = pl.BlockSpec((S, DK // 2), lambda h, b: (0, 0))
    blk = pl.BlockSpec((S, DK), lambda h, b: (b, h))
    one = pl.BlockSpec((None, 1, DK), lambda h, b: (h, 0, 0))
    return pl.pallas_call(
        body, name="ret_bwd", grid=(HEADS, nb),
        in_specs=[blk, blk, seg(4), seg(5), blk, blk,
                  pl.BlockSpec((None, None, NCH, DK, DK), lambda h, b: (b, h, 0, 0, 0)),
                  one, tab, tab, tab, rot, rot] + [ANY] * ns,
        out_specs=[pl.BlockSpec((4, S, DK), lambda h, b: (0, b, h)), one] + [ANY] * ns,
        out_shape=[jax.ShapeDtypeStruct((4, t, D), BF16), jax.ShapeDtypeStruct((HEADS, 1, DK), F32)]
        + [jax.ShapeDtypeStruct(a.shape, a.dtype) for a in sums],
        scratch_shapes=[pltpu.VMEM((DK, DK), F32), pltpu.SemaphoreType.DMA((ns, 3)), pltpu.SemaphoreType.DMA((ns, 3)),
                        pltpu.SemaphoreType.DMA((ns,))],
        compiler_params=_params(("arbitrary", "arbitrary")),
    )(qr, kr, proj, proj, o, dyb, rs, gain, inner_t, cross_t, state_t, cos_t, sin_t, *sums)


def _wblock(k):
    return pl.BlockSpec((NDEV, D // NDEV, D), lambda i: (0, k, 0))


def _tail(ya, yb, proj, x2d, tgt, wg, g_fin):
    t = x2d.shape[0]
    tm = 256

    def body(ya_ref, yb_ref, ma_ref, mb_ref, x_ref, t_ref, wa_ref, wb_ref, wo_ref, g_ref,
             dx2_ref, dya_ref, dyb_ref, dm_ref, mg_ref, doa_ref, dob_ref, gfin_ref, loss_ref):
        i = pl.program_id(0)

        @pl.when(i == 0)
        def _():
            gfin_ref[...] = jnp.zeros_like(gfin_ref)
            loss_ref[...] = jnp.zeros_like(loss_ref)

        wa = wa_ref[...].reshape(D, D)
        wb = wb_ref[...].reshape(D, D)
        wo = wo_ref[...].reshape(D, D)
        out_a = _dot(ya_ref[...], wa)
        out_b = _dot(yb_ref[...], wb)
        sa = _sigmoid(ma_ref[...].astype(F32))
        sb = _sigmoid(mb_ref[...].astype(F32))
        merged = (sa * out_a + sb * out_b).astype(BF16)
        mg_ref[...] = merged
        x2 = x_ref[...] + _dot(merged, wo)
        r2 = lax.rsqrt(jnp.mean(x2 * x2, axis=-1, keepdims=True) + EPS)
        xh = x2 * r2
        g = g_ref[...]
        err = xh * g - t_ref[...]
        loss_ref[...] += jnp.sum(err * err, axis=0, keepdims=True) * (0.5 / D)
        dy = err * (1.0 / D)
        gfin_ref[...] += jnp.sum(dy * xh, axis=0, keepdims=True)
        dxh = dy * g
        dx2 = r2 * (dxh - xh * jnp.mean(dxh * xh, axis=-1, keepdims=True))
        dx2_ref[...] = dx2
        dmerged = _dot_nt(dx2.astype(BF16), wo)
        doa = (sa * dmerged).astype(BF16)
        dob = (sb * dmerged).astype(BF16)
        doa_ref[...] = doa
        dob_ref[...] = dob
        dm_ref[0] = (dmerged * out_a * sa * (1.0 - sa)).astype(BF16)
        dm_ref[1] = (dmerged * out_b * sb * (1.0 - sb)).astype(BF16)
        dya_ref[...] = _dot_nt(doa, wa)
        dyb_ref[...] = _dot_nt(dob, wb)

    row = lambda: pl.BlockSpec((tm, D), lambda i: (i, 0))
    seg = lambda s: pl.BlockSpec((None, tm, D), lambda i: (s, i, 0))
    vec = pl.BlockSpec((1, D), lambda i: (0, 0))
    return pl.pallas_call(
        body, name="tail", grid=(t // tm,),
        in_specs=[row(), row(), seg(6), seg(7), row(), row(), _wblock(0), _wblock(1), _wblock(2), vec],
        out_specs=[row(), row(), row(), pl.BlockSpec((2, tm, D), lambda i: (0, i, 0)),
                   row(), row(), row(), vec, vec],
        out_shape=[jax.ShapeDtypeStruct((t, D), F32), jax.ShapeDtypeStruct((t, D), F32),
                   jax.ShapeDtypeStruct((t, D), F32), jax.ShapeDtypeStruct((2, t, D), BF16),
                   jax.ShapeDtypeStruct((t, D), BF16), jax.ShapeDtypeStruct((t, D), BF16),
                   jax.ShapeDtypeStruct((t, D), BF16), jax.ShapeDtypeStruct((1, D), F32),
                   jax.ShapeDtypeStruct((1, D), F32)],
        compiler_params=_params(("arbitrary",)),
    )(ya, yb, proj, proj, x2d, tgt, wg, wg, wg, g_fin)


def _tail_wgrad(ya, yb, merged, doa, dob, dx2):
    t = ya.shape[0]
    tm = 512

    def body(ya_ref, yb_ref, mg_ref, doa_ref, dob_ref, dx2_ref, ga_ref, gb_ref, go_ref):
        @pl.when(pl.program_id(0) == 0)
        def _():
            ga_ref[...] = jnp.zeros_like(ga_ref)
            gb_ref[...] = jnp.zeros_like(gb_ref)
            go_ref[...] = jnp.zeros_like(go_ref)

        ga_ref[...] += _dot_tn(ya_ref[...], doa_ref[...])
        gb_ref[...] += _dot_tn(yb_ref[...], dob_ref[...])
        go_ref[...] += _dot_tn(mg_ref[...], dx2_ref[...].astype(BF16))

    row = lambda: pl.BlockSpec((tm, D), lambda i: (i, 0))
    full = lambda: pl.BlockSpec((D, D), lambda i: (0, 0))
    return pl.pallas_call(
        body, name="tail_wgrad", grid=(t // tm,),
        in_specs=[row() for _ in range(6)], out_specs=[full(), full(), full()],
        out_shape=[jax.ShapeDtypeStruct((D, D), F32)] * 3,
        compiler_params=_params(("arbitrary",)),
    )(ya, yb, merged, doa, dob, dx2)


def _dproj_specs_ordered(tm):
    def spec(lo, n):
        def index(k, i, order_ref):
            seg = order_ref[k]
            mine = jnp.logical_and(seg >= lo, seg < lo + n)
            return jnp.where(mine, seg - lo, 0), jnp.where(mine, i, 0), 0
        return pl.BlockSpec((None, tm, D), index)
    return [spec(0, 2), spec(2, 4), spec(6, 2)]


def _dproj_pick(j, da_ref, db_ref, dc_ref, use):
    @pl.when(j < 2)
    def _():
        use(da_ref[...])

    @pl.when(jnp.logical_and(j >= 2, j < 6))
    def _():
        use(db_ref[...])

    @pl.when(j >= 6)
    def _():
        use(dc_ref[...])


RS_X, RS_Y, RS_XY = 0, 1, 2
RS_ROLES = ((RS_XY, RS_X, RS_Y), (RS_Y, RS_XY, RS_X))


def _rs_flip(rel, x, y):
    return ((1 - x, y), (x, 1 - y), (1 - x, 1 - y))[rel]


def _rs_order(x, y, c):
    order = []
    for s in range(4):
        chip = []
        for core in (0, 1):
            px, py = _rs_flip(RS_ROLES[core][s], x, y) if s < 3 else (x, y)
            chip.append(2 * px + py)
        keep = jnp.where(c == 0, chip[0], chip[1])
        give = jnp.where(c == 0, chip[1], chip[0])
        order += [2 * give + 1 - c, 2 * keep + c]
    return jnp.stack(order).astype(jnp.int32)


def _inproj_wgrad_rs(h, dpa, dpb, dpc, order, smalls):
    t = h.shape[0]
    tm = 1024
    nt = t // tm
    nsm = len(smalls)

    def body(order_ref, h_ref, da_ref, db_ref, dc_ref, *rest):
        small_refs, parts_ref, rest = rest[:nsm], rest[nsm], rest[nsm + 1:]
        all_refs, rest = rest[:nsm], rest[nsm:]
        (acc, sib, outb, far, give_send, give_recv, sum_send, sum_recv, far_send, far_recv, own_sem,
         small_send, small_recv, small_own) = rest
        k, i = pl.program_id(0), pl.program_id(1)
        x, y, c = _place()
        own, first, arrive, forward, others = _gather_copies(small_refs, all_refs, small_send, small_recv, small_own)

        @pl.when(jnp.logical_and(k == 0, i == 0))
        def _():
            for cp in own + first:
                cp.start()

        @pl.when(jnp.logical_and(k == 2, i == 0))
        def _():
            for came, on in zip(arrive, forward):
                came.wait_recv()
                on.start()

        def use(d):
            @pl.when(i == 0)
            def _():
                acc[k % 2] = _dot_tn(h_ref[...], d)

            @pl.when(i > 0)
            def _():
                acc[k % 2] += _dot_tn(h_ref[...], d)

        _dproj_pick(order_ref[k], da_ref, db_ref, dc_ref, use)

        def give_copy(s):
            return pltpu.make_async_remote_copy(
                src_ref=acc.at[0], dst_ref=sib.at[s % 2], send_sem=give_send.at[s], recv_sem=give_recv.at[s],
                device_id=(x, y, 1 - c), device_id_type=MESH)

        def sum_copy(s, core):
            slot = 0 if s < 2 else 1
            return pltpu.make_async_remote_copy(
                src_ref=outb.at[s], dst_ref=parts_ref.at[slot], send_sem=sum_send.at[slot], recv_sem=sum_recv.at[slot],
                device_id=(*_rs_flip(RS_ROLES[core][s], x, y), core), device_id_type=MESH)

        def far_copy(s, core):
            return pltpu.make_async_remote_copy(
                src_ref=outb.at[s], dst_ref=far, send_sem=far_send, recv_sem=far_recv,
                device_id=(*_rs_flip(RS_X if core == 0 else RS_Y, x, y), core), device_id_type=MESH)

        own_copy = pltpu.make_async_copy(outb.at[3], parts_ref.at[2], own_sem)

        def send_of(core, s):
            return far_copy(s, core) if RS_ROLES[core][s] == RS_XY else sum_copy(s, core)

        for s in range(4):
            @pl.when(jnp.logical_and(k == 2 * s, i == nt - 1))
            def _():
                give_copy(s).start()

            @pl.when(jnp.logical_and(k == 2 * s + 1, i == nt - 1))
            def _():
                give_copy(s).wait_recv()
                if s == 2:
                    far_copy(s, 0).wait_recv()
                    outb[s] = (acc[1] + sib[s % 2] + far[...].astype(F32)).astype(BF16)
                else:
                    outb[s] = (acc[1] + sib[s % 2]).astype(BF16)
                give_copy(s).wait_send()
                if s < 3:
                    for core in (0, 1):
                        @pl.when(c == core)
                        def _():
                            send_of(core, s).start()
                else:
                    own_copy.start()

        @pl.when(jnp.logical_and(k == NSEG - 1, i == nt - 1))
        def _():
            for slot in (0, 1):
                sum_copy(2 * slot, 0).wait_recv()
            for s in range(3):
                send_of(0, s).wait_send()
            own_copy.wait()
            for cp in others:
                cp.wait_recv()
            for cp in first + forward:
                cp.wait_send()
            for cp in own:
                cp.wait()

    return pl.pallas_call(
        body, name="inproj_wgrad_rs",
        grid_spec=pltpu.PrefetchScalarGridSpec(
            num_scalar_prefetch=1, grid=(NSEG, nt),
            in_specs=[pl.BlockSpec((tm, D), lambda k, i, order_ref: (i, 0))] + _dproj_specs_ordered(tm) + [ANY] * nsm,
            out_specs=[ANY] * (1 + nsm),
            scratch_shapes=[pltpu.VMEM((2, D, D), F32), pltpu.VMEM((2, D, D), F32), pltpu.VMEM((4, D, D), BF16),
                            pltpu.VMEM((D, D), BF16),
                            pltpu.SemaphoreType.DMA((4,)), pltpu.SemaphoreType.DMA((4,)),
                            pltpu.SemaphoreType.DMA((2,)), pltpu.SemaphoreType.DMA((2,)),
                            pltpu.SemaphoreType.DMA, pltpu.SemaphoreType.DMA, pltpu.SemaphoreType.DMA,
                            pltpu.SemaphoreType.DMA((nsm, 7)), pltpu.SemaphoreType.DMA((nsm, 7)),
                            pltpu.SemaphoreType.DMA((nsm,))]),
        out_shape=[jax.ShapeDtypeStruct((3, D, D), BF16)]
        + [jax.ShapeDtypeStruct((NDEV,) + a.shape, a.dtype) for a in smalls],
        compiler_params=_params(("arbitrary", "arbitrary")),
    )(order, h, dpa, dpb, dpc, *smalls)


def _inproj_dgrad(dpa, dpb, dpc, wg, x2d, dx2, g_in):
    t = x2d.shape[0]
    tm = 512

    def body(da_ref, db_ref, dc_ref, w_hbm, x_ref, dx2_ref, g_ref, gx_ref, gg_ref, w_s, w_sem):
        i = pl.program_id(0)

        @pl.when(i == 0)
        def _():
            gg_ref[...] = jnp.zeros_like(gg_ref)
            load = pltpu.make_async_copy(w_hbm, w_s, w_sem)
            load.start()
            load.wait()

        dh = None
        for ref, lo in ((da_ref, 0), (db_ref, 2), (dc_ref, 6)):
            for k in range(ref.shape[0]):
                part = _dot_nt(ref[k], w_s[lo + k])
                dh = part if dh is None else dh + part
        x = x_ref[...]
        r = lax.rsqrt(jnp.mean(x * x, axis=-1, keepdims=True) + EPS)
        xh = x * r
        gg_ref[...] += jnp.sum(dh * xh, axis=0, keepdims=True)
        dxh = dh * g_ref[...]
        gx_ref[...] = dx2_ref[...] + r * (dxh - xh * jnp.mean(dxh * xh, axis=-1, keepdims=True))

    row = lambda: pl.BlockSpec((tm, D), lambda i: (i, 0))
    seg = lambda n: pl.BlockSpec((n, tm, D), lambda i: (0, i, 0))
    vec = pl.BlockSpec((1, D), lambda i: (0, 0))
    return pl.pallas_call(
        body, name="inproj_dgrad", grid=(t // tm,),
        in_specs=[seg(2), seg(4), seg(2), ANY, row(), row(), vec],
        out_specs=[row(), vec],
        out_shape=[jax.ShapeDtypeStruct((t, D), F32), jax.ShapeDtypeStruct((1, D), F32)],
        scratch_shapes=[pltpu.VMEM((NSEG, D, D), BF16), pltpu.SemaphoreType.DMA],
        compiler_params=_params(("arbitrary",)),
    )(dpa, dpb, dpc, wg, x2d, dx2, g_in)


def _adam_update(g, w, m, v):
    m_new = ADAM_B1 * m + (1.0 - ADAM_B1) * g
    v_new = ADAM_B2 * v + (1.0 - ADAM_B2) * (g * g)
    m_hat = m_new / (1.0 - ADAM_B1 ** ADAM_STEP)
    v_hat = v_new / (1.0 - ADAM_B2 ** ADAM_STEP)
    return -ADAM_LR * (m_hat / (jnp.sqrt(v_hat) + ADAM_EPS) + ADAM_WD * w), m_new, v_new


def _sum_in_order(ref):
    total = ref[0].astype(F32)
    for k in range(1, ref.shape[0]):
        total = total + ref[k].astype(F32)
    return total


def _adamw_small(me, vec_all, gx_all, ga_all, groups):
    flat = [a for grp in groups for a in grp]
    ng = len(groups)
    nshard = D // NDEV

    def body(me_ref, vec_ref, shard_ref, gx_ref, ga_ref, *refs):
        ins, outs = refs[:3 * ng], refs[3 * ng:]
        vec = _sum_in_order(vec_ref)
        shard = _sum_in_order(shard_ref)
        grads = [vec[r:r + 1, :] for r in range(6)]
        grads += [shard[0:4, :], shard[4:8, 0:DK // NDEV], _sum_in_order(gx_ref), _sum_in_order(ga_ref)]
        for n, g in enumerate(grads):
            delta, m_new, v_new = _adam_update(g, ins[3 * n][...], ins[3 * n + 1][...], ins[3 * n + 2][...])
            outs[4 * n][...] = g
            outs[4 * n + 1][...] = delta
            outs[4 * n + 2][...] = m_new
            outs[4 * n + 3][...] = v_new
        outs[4 * ng][...] = jnp.sum(vec[6:7, :], axis=1, keepdims=True)

    full = lambda a: pl.BlockSpec(a.shape, lambda i, me_ref, nd=len(a.shape): (0,) * nd)
    out_shape = [jax.ShapeDtypeStruct(w.shape, F32) for w, _, _ in groups for _ in range(4)]
    out_shape.append(jax.ShapeDtypeStruct((1, 1), F32))
    outs = pl.pallas_call(
        body, name="adamw_small",
        grid_spec=pltpu.PrefetchScalarGridSpec(
            num_scalar_prefetch=1, grid=(1,),
            in_specs=[full(vec_all),
                      pl.BlockSpec((NDEV, 8, nshard), lambda i, me_ref: (0, 1, me_ref[0])),
                      full(gx_all), full(ga_all)] + [full(a) for a in flat],
            out_specs=[full(s) for s in out_shape]),
        out_shape=out_shape,
        compiler_params=_params(("arbitrary",)),
    )(me, vec_all, vec_all, gx_all, ga_all, *flat)
    return [outs[4 * n:4 * n + 4] for n in range(ng)], outs[4 * ng]


def _adamw(name, items):
    n, rows, cols = items[0][0].shape
    tr = rows if rows <= 256 else 256
    k = len(items)

    def body(*refs):
        for a in range(k):
            p_ref, w_ref, m_ref, v_ref = refs[4 * a:4 * a + 4]
            g = _sum_in_order(p_ref)
            delta, m_new, v_new = _adam_update(g, w_ref[...], m_ref[...], v_ref[...])
            for o, val in zip(refs[4 * k + 4 * a:4 * k + 4 * a + 4], (g, delta, m_new, v_new)):
                o[...] = val

    blk = lambda: pl.BlockSpec((tr, cols), lambda i: (i, 0))
    outs = pl.pallas_call(
        body, name=name, grid=(rows // tr,),
        in_specs=[pl.BlockSpec((n, tr, cols), lambda i: (0, i, 0)), blk(), blk(), blk()] * k,
        out_specs=[blk() for _ in range(4 * k)],
        out_shape=[jax.ShapeDtypeStruct((rows, cols), F32)] * (4 * k),
        compiler_params=_params(("arbitrary",)),
    )(*[a for item in items for a in item])
    return [outs[4 * a:4 * a + 4] for a in range(k)]


ANY = pl.BlockSpec(memory_space=pl.ANY)


def _place():
    return lax.axis_index("x"), lax.axis_index("y"), lax.axis_index("c")


def _gather_copies(ins, outs, send_sems, recv_sems, own_sems):
    x, y, c = _place()
    me, sibling = (x, y, c), (x, y, 1 - c)
    chips = [(1 - x, y), (x, 1 - y), (1 - x, 1 - y)]
    n = len(ins)

    def copy(a, k, block, to, src=None):
        px, py, pc = block
        dst = outs[a].at[4 * px + 2 * py + pc]
        return pltpu.make_async_remote_copy(
            src_ref=dst if src is None else src, dst_ref=dst,
            send_sem=send_sems.at[a, k], recv_sem=recv_sems.at[a, k], device_id=to, device_id_type=MESH)

    own = [pltpu.make_async_copy(ins[a], outs[a].at[4 * x + 2 * y + c], own_sems.at[a]) for a in range(n)]
    first = []
    for a in range(n):
        first.append(copy(a, 0, me, sibling, src=ins[a]))
        first += [copy(a, 1 + j, me, (*chip, c), src=ins[a]) for j, chip in enumerate(chips)]
    arrive = [copy(a, 1 + j, (*chip, c), me) for j, chip in enumerate(chips) for a in range(n)]
    forward = [copy(a, 4 + j, (*chip, c), sibling) for j, chip in enumerate(chips) for a in range(n)]
    rest = [copy(a, 0, sibling, me) for a in range(n)]
    rest += [copy(a, 4 + j, (*chip, 1 - c), me) for a in range(n) for j, chip in enumerate(chips)]
    return own, first, arrive, forward, rest


def _sibling_copies(ins, outs, send_sems, recv_sems):
    x, y, c = _place()
    return [pltpu.make_async_remote_copy(
        src_ref=ins[a].at[2 * q + 1 - c], dst_ref=outs[a].at[q],
        send_sem=send_sems.at[a, q], recv_sem=recv_sems.at[a, q],
        device_id=(x, y, 1 - c), device_id_type=MESH) for a in range(len(ins)) for q in range(4)]


def _chip_copies(ins, outs, send_sems, recv_sems, local_sems):
    x, y, c = _place()
    my_chip = 2 * x + y
    chips = [(1 - x, y), (x, 1 - y), (1 - x, 1 - y)]
    n = len(ins)
    mine = [pltpu.make_async_copy(ins[a].at[my_chip], outs[a].at[my_chip], local_sems.at[a]) for a in range(n)]
    sends = [pltpu.make_async_remote_copy(
        src_ref=ins[a].at[2 * px + py], dst_ref=outs[a].at[my_chip],
        send_sem=send_sems.at[a, j], recv_sem=recv_sems.at[a, j],
        device_id=(px, py, c), device_id_type=MESH) for a in range(n) for j, (px, py) in enumerate(chips)]
    recvs = [pltpu.make_async_remote_copy(
        src_ref=ins[a].at[my_chip], dst_ref=outs[a].at[2 * px + py],
        send_sem=send_sems.at[a, j], recv_sem=recv_sems.at[a, j],
        device_id=(px, py, c), device_id_type=MESH) for a in range(n) for j, (px, py) in enumerate(chips)]
    return mine, sends, recvs


def _chip_sum(owns, gots, core):
    n = len(owns)
    _, rows, cols = owns[0].shape

    def body(core_ref, *refs):
        for a in range(n):
            refs[2 * n + a][...] = (refs[a][...] + refs[n + a][...]).astype(BF16)

    own_spec = pl.BlockSpec((None, rows, cols), lambda q, core_ref: (2 * q + core_ref[0], 0, 0))
    slab = pl.BlockSpec((None, rows, cols), lambda q, core_ref: (q, 0, 0))
    return pl.pallas_call(
        body, name="chip_sum",
        grid_spec=pltpu.PrefetchScalarGridSpec(
            num_scalar_prefetch=1, grid=(4,),
            in_specs=[own_spec] * n + [slab] * n, out_specs=[slab] * n),
        out_shape=[jax.ShapeDtypeStruct((4, rows, cols), BF16)] * n,
        compiler_params=_params(("arbitrary",)),
    )(core, *owns, *gots)


def _block_diag(w):
    w4 = w.reshape(NCB, 4, 64, 64)
    eye = jnp.eye(4, dtype=w.dtype)
    return (w4[:, :, :, None, :] * eye[None, :, None, :, None]).reshape(NCB, CB, CB)


def _block_diag_back(g):
    g5 = g.reshape(NCB, 4, 64, 4, 64)
    return jnp.stack([g5[:, m, :, m, :] for m in range(4)], axis=1).reshape(16, 64, 64)


def kernel(x, norm_in, w_in, conv_w, conv_b, gate_x_w, gate_x_b, gate_a_w, gate_a_b, lru_lambda, gn_gain, w_proj_a, w_proj_b, w_out, norm_final, loss_target, m_norm_in, m_w_in, m_conv_w, m_conv_b, m_gate_x_w, m_gate_x_b, m_gate_a_w, m_gate_a_b, m_lru_lambda, m_gn_gain, m_w_proj_a, m_w_proj_b, m_w_out, m_norm_final, v_norm_in, v_w_in, v_conv_w, v_conv_b, v_gate_x_w, v_gate_x_b, v_gate_a_w, v_gate_a_b, v_lru_lambda, v_gn_gain, v_w_proj_a, v_w_proj_b, v_w_out, v_norm_final):
    xi, yi, ci = _place()
    me = 4 * xi + 2 * yi + ci
    core = ci.astype(jnp.int32).reshape(1)
    nshard = D // NDEV
    nb = x.shape[0]
    t = nb * S
    x2d = x.reshape(t, D)
    tgt2d = loss_target.reshape(t, D)
    g_final = norm_final.reshape(1, D)
    wbd = jnp.concatenate([_block_diag(gate_x_w[0]), _block_diag(gate_a_w[0])], axis=-1).astype(BF16)
    tables = _retention_tables()

    wp_own = jnp.concatenate([w_proj_a[0], w_proj_b[0], w_out[0]], axis=0).astype(BF16)
    tiny = jnp.concatenate([conv_w[0], jnp.pad(gn_gain[0], ((0, 0), (0, nshard - DK // NDEV)))], axis=0)
    proj, h, wg, tiny_g = _inproj_gather(x2d, norm_in, w_in[0].astype(BF16), tiny, *_gather_order(xi, yi, ci))
    conv_w_full = tiny_g[:, 0:4, :].transpose(1, 0, 2).reshape(4, D)
    gain3 = tiny_g[:, 4:8, :DK // NDEV].transpose(1, 0, 2).reshape(HEADS, 1, DK)

    ya, hs, xc, gi, gr = _lru_fwd(proj, conv_w_full, conv_b, wbd, gate_x_b, gate_a_b, lru_lambda, nb)
    yb, qr, kr, o, rs, wpg = _ret_fwd(proj, gain3, tables, nb, wp_own)
    dx2, dya, dyb, dpc, merged, doa, dob, g_fin, loss_vec = _tail(ya, yb, proj, x2d, tgt2d, wpg, g_final)
    g_pa, g_pb, g_out = _tail_wgrad(ya, yb, merged, doa, dob, dx2)

    own = [g.reshape(NDEV, nshard, D) for g in (g_pa, g_pb, g_out)]
    dpa, g_wbd, g_vec, *got = _lru_bwd(proj, hs, xc, gi, gr, dya, conv_w_full, wbd, lru_lambda, nb, own)
    sums = _chip_sum(own, got, core)
    dpb, g_gain, *parts = _ret_bwd(proj, qr, kr, o, rs, dyb, gain3, tables, nb, sums)

    grad_x, g_norm_in = _inproj_dgrad(dpa, dpb, dpc, wg, x2d, dx2, norm_in)
    grad_x = grad_x.reshape(nb, S, D)

    gain_rows = jnp.pad(g_gain.reshape(HEADS, NDEV, DK // NDEV), ((0, 0), (0, 0), (0, nshard - DK // NDEV)))
    vec = jnp.concatenate([g_norm_in, g_vec[0:4], g_fin, loss_vec, jnp.zeros((1, D), F32), g_vec[4:8],
                           gain_rows.reshape(HEADS, D)], axis=0)
    g_gx = _block_diag_back(g_wbd[:, :, :CB]).reshape(D // 2, 128)
    g_ga = _block_diag_back(g_wbd[:, :, CB:]).reshape(D // 2, 128)
    parts_in, vec_all, gx_all, ga_all = _inproj_wgrad_rs(h, dpa, dpb, dpc, _rs_order(xi, yi, ci),
                                                         [vec, g_gx, g_ga])
    gx_all = gx_all.reshape(NDEV, D, 64)
    ga_all = ga_all.reshape(NDEV, D, 64)
    parts = [parts_in] + list(parts)

    res = {}
    (out,) = _adamw("adamw_w_in", [(parts[0], w_in[0], m_w_in[0], v_w_in[0])])
    res["w_in"] = [o[None] for o in out]
    square = [("w_proj_a", w_proj_a, m_w_proj_a, v_w_proj_a), ("w_proj_b", w_proj_b, m_w_proj_b, v_w_proj_b),
              ("w_out", w_out, m_w_out, v_w_out)]
    outs = _adamw("adamw_square", [(parts[1 + k], w[0], m[0], v[0]) for k, (_, w, m, v) in enumerate(square)])
    for (nm, _, _, _), out in zip(square, outs):
        res[nm] = [o[None] for o in out]

    row = lambda a: a.reshape(1, D)
    gate = lambda a: a.reshape(D, 64)
    groups = [("norm_in", norm_in, m_norm_in, v_norm_in, row), ("conv_b", conv_b, m_conv_b, v_conv_b, row),
              ("gate_x_b", gate_x_b, m_gate_x_b, v_gate_x_b, row), ("gate_a_b", gate_a_b, m_gate_a_b, v_gate_a_b, row),
              ("lru_lambda", lru_lambda, m_lru_lambda, v_lru_lambda, row),
              ("norm_final", norm_final, m_norm_final, v_norm_final, row),
              ("conv_w", conv_w, m_conv_w, v_conv_w, lambda a: a[0]), ("gn_gain", gn_gain, m_gn_gain, v_gn_gain, lambda a: a[0]),
              ("gate_x_w", gate_x_w, m_gate_x_w, v_gate_x_w, gate), ("gate_a_w", gate_a_w, m_gate_a_w, v_gate_a_w, gate)]
    small_out, loss = _adamw_small(me.astype(jnp.int32).reshape(1), vec_all, gx_all, ga_all,
                                   [tuple(view(a) for a in (w, m, v)) for _, w, m, v, view in groups])
    for (nm, w, _, _, _), out in zip(groups, small_out):
        res[nm] = [o.reshape(w.shape) for o in out]
    loss = loss.reshape(())

    order = ["norm_in", "w_in", "conv_w", "conv_b", "gate_x_w", "gate_x_b", "gate_a_w", "gate_a_b", "lru_lambda",
             "gn_gain", "w_proj_a", "w_proj_b", "w_out", "norm_final"]
    outs = [loss, grad_x]
    for k in range(4):
        outs += [res[nm][k] for nm in order]
    return tuple(outs)
```

```python
import numpy as np

import jax
import jax.numpy as jnp
from jax import lax
from jax.experimental import pallas as pl
from jax.experimental.pallas import tpu as pltpu

F32 = jnp.float32
BF16 = jnp.bfloat16
MESH = pl.DeviceIdType.MESH

D = 1024
S = 2048
NSEG = 8
NDEV = 8
HEADS = 4
DK = 256
CH = 256
NCH = S // CH
CB = 256
NCB = D // CB
RC = 512
RC_CONV = 128
SCAN_GROUP = 16
EPS = 1e-6
LRU_C = 8.0
VMEM_LIMIT = 56 * 1024 * 1024

ADAM_LR = 0.001
ADAM_B1 = 0.9
ADAM_B2 = 0.999
ADAM_EPS = 1e-08
ADAM_WD = 0.01
ADAM_STEP = 10


def _params(sem=None):
    return pltpu.CompilerParams(dimension_semantics=sem, vmem_limit_bytes=VMEM_LIMIT)


def _dot(a, b):
    return jnp.dot(a, b, preferred_element_type=F32)


def _dot_nt(a, b):
    return lax.dot_general(a, b, (((1,), (1,)), ((), ())), preferred_element_type=F32)


def _dot_tn(a, b):
    return lax.dot_general(a, b, (((0,), (0,)), ((), ())), preferred_element_type=F32)


def _sigmoid(x):
    return jax.nn.sigmoid(x)


def _expm1_nonpos(x):
    poly = x * (1.0 + x * (0.5 + x * (1.0 / 6.0 + x * (1.0 / 24.0))))
    return jnp.where(x > -0.05, poly, jnp.exp(x) - 1.0)


def _softplus(x):
    return jnp.maximum(x, 0.0) + jnp.log(1.0 + jnp.exp(-jnp.abs(x)))


def _rows(c, n):
    return pl.ds(pl.multiple_of(c * n, n), n)


def _window_before(ref, c, n):
    r0 = c * n
    if ref.dtype == BF16:
        prev = ref[pl.ds(pl.multiple_of(jnp.maximum(r0 - 16, 0), 16), 16), :].astype(F32)[8:, :]
    else:
        prev = ref[pl.ds(pl.multiple_of(jnp.maximum(r0 - 8, 0), 8), 8), :]
    prev = jnp.where(c > 0, prev, 0.0)
    return jnp.concatenate([prev, ref[_rows(c, n), :].astype(F32)], axis=0)


def _shift_down(win, s, n):
    if s == 0:
        return win[8:, :]
    return pltpu.roll(win, s, 0)[8:, :]


def _shift_up(win, s, n):
    if s == 0:
        return win[:n, :]
    return pltpu.roll(win, n + 8 - s, 0)[:n, :]


HALF = D // 2
GATHER_SLOTS = [("own", None, 0), ("own", None, 1), ("sib", None, 0), ("sib", None, 1)]
for _j, _h in ((0, 0), (1, 0), (0, 1), (1, 1), (2, 0), (2, 1)):
    GATHER_SLOTS += [("ici", _j, _h), ("fwd", _j, _h)]
NSLOT = len(GATHER_SLOTS)


def _gather_order(x, y, c):
    chips = [(1 - x, y), (x, 1 - y), (1 - x, 1 - y)]
    segs, halves = [], []
    for kind, j, h in GATHER_SLOTS:
        if kind == "own":
            seg = 4 * x + 2 * y + c
        elif kind == "sib":
            seg = 4 * x + 2 * y + 1 - c
        else:
            px, py = chips[j]
            seg = 4 * px + 2 * py + (c if kind == "ici" else 1 - c)
        segs.append(seg)
        halves.append(h)
    return jnp.stack(segs).astype(jnp.int32), jnp.asarray(halves, jnp.int32)


def _inproj_gather(x2d, g_in, w_own, tiny_own, order, halves):
    t = x2d.shape[0]
    tm = 2048
    nt = t // tm
    tx = 1024
    nx = tm // tx

    def body(order_ref, half_ref, g_ref, x_hbm, w_own_ref, tiny_own_ref,
             proj_ref, h_hbm, wg_ref, tinyg_ref,
             w_all, h_all, x_s, send_sems, recv_sems, own_sems, out_sems, tiny_send, tiny_recv, tiny_own_sem,
             x_sems, h_sem):
        k, i = pl.program_id(0), pl.program_id(1)
        x, y, c = _place()
        me, sibling = (x, y, c), (x, y, 1 - c)
        mine = 4 * x + 2 * y + c
        chips = [(1 - x, y), (x, 1 - y), (1 - x, 1 - y)]

        def copy(h, n, block, to, own_src=False):
            px, py, pc = block
            dst = w_all.at[4 * px + 2 * py + pc, h]
            return pltpu.make_async_remote_copy(
                src_ref=w_own_ref.at[:, pl.ds(h * HALF, HALF)] if own_src else dst, dst_ref=dst,
                send_sem=send_sems.at[h, n], recv_sem=recv_sems.at[h, n], device_id=to, device_id_type=MESH)

        def tiny_copy(n, block, to, own_src=False):
            px, py, pc = block
            dst = tinyg_ref.at[4 * px + 2 * py + pc]
            return pltpu.make_async_remote_copy(
                src_ref=tiny_own_ref if own_src else dst, dst_ref=dst,
                send_sem=tiny_send.at[n], recv_sem=tiny_recv.at[n], device_id=to, device_id_type=MESH)

        def own_copy(h):
            return pltpu.make_async_copy(w_own_ref.at[:, pl.ds(h * HALF, HALF)], w_all.at[mine, h], own_sems.at[h])

        tiny_mine = pltpu.make_async_copy(tiny_own_ref, tinyg_ref.at[mine], tiny_own_sem)

        def keep_copy(n):
            h = GATHER_SLOTS[n][2]
            return pltpu.make_async_copy(w_all.at[order_ref[n], h], wg_ref.at[order_ref[n], :, pl.ds(h * HALF, HALF)],
                                         out_sems.at[n])

        near = [(0, sibling), (1, (*chips[0], c)), (2, (*chips[1], c))]
        first = [copy(h, n, me, to, True) for h in (0, 1) for n, to in near]
        tiny_first = [tiny_copy(0, me, sibling, True)] + [tiny_copy(1 + j, me, (*chip, c), True) for j, chip in enumerate(chips)]

        def relay(h, j):
            seg = w_all.at[4 * chips[j][0] + 2 * chips[j][1] + c, h]
            return pltpu.make_async_remote_copy(
                src_ref=seg, dst_ref=seg, send_sem=send_sems.at[h, 3], recv_sem=recv_sems.at[h, 3],
                device_id=(*chips[1 - j], c), device_id_type=MESH)

        for n, (kind, j, h) in enumerate(GATHER_SLOTS):
            @pl.when(jnp.logical_and(k == n, i == 0))
            def _():
                if n == 0:
                    own_copy(0).start()
                    own_copy(1).start()
                    tiny_mine.start()
                    for cp in first + tiny_first:
                        cp.start()
                if kind == "own":
                    own_copy(h).wait()
                elif kind == "sib":
                    copy(h, 0, sibling, me).wait_recv()
                elif kind == "ici":
                    copy(h, 1 + j, (*chips[j], c), me).wait_recv()
                    copy(h, 4 + j, (*chips[j], c), sibling).start()
                    if j < 2:
                        @pl.when(c == j)
                        def _():
                            relay(h, j).start()
                else:
                    copy(h, 4 + j, (*chips[j], 1 - c), me).wait_recv()
                keep_copy(n).start()

        rows = pl.ds(pl.multiple_of(i * tm, tm), tm)

        def x_copy(n):
            return pltpu.make_async_copy(x_hbm.at[pl.ds(n * tx, tx), :], x_s.at[n % 2], x_sems.at[n % 2])

        keep_h = pltpu.make_async_copy(h_all, h_hbm, h_sem)

        for step in range(nt):
            @pl.when(jnp.logical_and(k == 0, i == step))
            def _():
                if step == 0:
                    x_copy(0).start()
                for n in range(step * nx, (step + 1) * nx):
                    x_copy(n).wait()
                    if n + 1 < nt * nx:
                        x_copy(n + 1).start()
                    xv = x_s[n % 2]
                    r = lax.rsqrt(jnp.mean(xv * xv, axis=-1, keepdims=True) + EPS)
                    h_all[pl.ds(n * tx, tx), :] = (xv * r * g_ref[...]).astype(BF16)
                if step == nt - 1:
                    keep_h.start()

        proj_ref[...] = _dot(h_all[rows, :], w_all[order_ref[k], half_ref[k]]).astype(BF16)

        @pl.when(jnp.logical_and(k == NSLOT - 1, i == nt - 1))
        def _():
            for j, chip in enumerate(chips):
                tiny_copy(1 + j, (*chip, c), me).wait_recv()
                tiny_copy(4 + j, (*chip, c), sibling).start()
            tiny_copy(0, sibling, me).wait_recv()
            for j, chip in enumerate(chips):
                tiny_copy(4 + j, (*chip, 1 - c), me).wait_recv()
            for cp in first + tiny_first:
                cp.wait_send()
            for j, chip in enumerate(chips):
                tiny_copy(4 + j, (*chip, c), sibling).wait_send()
                for h in (0, 1):
                    copy(h, 4 + j, (*chip, c), sibling).wait_send()
            for h in (0, 1):
                relay(h, 0).wait_send()
            tiny_mine.wait()
            keep_h.wait()
            for n in range(NSLOT):
                keep_copy(n).wait()

    return pl.pallas_call(
        body, name="inproj_gather",
        grid_spec=pltpu.PrefetchScalarGridSpec(
            num_scalar_prefetch=2, grid=(NSLOT, nt),
            in_specs=[pl.BlockSpec((1, D), lambda k, i, order_ref, half_ref: (0, 0)),
                      ANY, ANY, ANY],
            out_specs=[pl.BlockSpec((None, tm, HALF), lambda k, i, order_ref, half_ref: (order_ref[k], i, half_ref[k])),
                       ANY, ANY, ANY],
            scratch_shapes=[pltpu.VMEM((NDEV, 2, D, HALF), BF16), pltpu.VMEM((t, D), BF16), pltpu.VMEM((2, tx, D), F32),
                            pltpu.SemaphoreType.DMA((2, 7)), pltpu.SemaphoreType.DMA((2, 7)),
                            pltpu.SemaphoreType.DMA((2,)), pltpu.SemaphoreType.DMA((NSLOT,)),
                            pltpu.SemaphoreType.DMA((7,)), pltpu.SemaphoreType.DMA((7,)), pltpu.SemaphoreType.DMA,
                            pltpu.SemaphoreType.DMA((2,)), pltpu.SemaphoreType.DMA]),
        out_shape=[jax.ShapeDtypeStruct((NSEG, t, D), BF16), jax.ShapeDtypeStruct((t, D), BF16),
                   jax.ShapeDtypeStruct((NDEV,) + w_own.shape, BF16),
                   jax.ShapeDtypeStruct((NDEV,) + tiny_own.shape, F32)],
        compiler_params=_params(("arbitrary", "arbitrary")),
    )(order, halves, g_in, x2d, w_own, tiny_own)


def _tile_scan(a, u):
    row = lax.broadcasted_iota(jnp.int32, a.shape, 0)
    for d in (1, 2, 4):
        m = row >= d
        a_sh = pltpu.roll(a, d, 0)
        u_sh = pltpu.roll(u, d, 0)
        u = jnp.where(m, a * u_sh + u, u)
        a = jnp.where(m, a * a_sh, a)
    return a, u


def _tile_scan_rev(a, w):
    row = lax.broadcasted_iota(jnp.int32, a.shape, 0)
    for d in (1, 2, 4):
        m = row < 8 - d
        a_sh = pltpu.roll(a, 8 - d, 0)
        w_sh = pltpu.roll(w, 8 - d, 0)
        w = jnp.where(m, a * w_sh + w, w)
        a = jnp.where(m, a * a_sh, a)
    return a, w


def _lru_gates(xa_ref, c, cw_ref, cb_ref, wbd_ref, bx_ref, ba_ref, sp):
    win = _window_before(xa_ref, c, RC)
    xc = cb_ref[...] + cw_ref[3:4, :] * _shift_down(win, 0, RC)
    for s in (1, 2, 3):
        xc = xc + cw_ref[3 - s:4 - s, :] * _shift_down(win, s, RC)
    z = _dot(xc.astype(BF16), wbd_ref[...])
    gi = _sigmoid(z[:, :CB] + bx_ref[...])
    gr = _sigmoid(z[:, CB:] + ba_ref[...])
    log_a = -LRU_C * gr * sp
    return win, xc, gi, gr, log_a


def _lru_fwd(proj, conv_w, conv_b, wbd, bx, ba, lam, nb):
    t = nb * S

    def body(xa_ref, ga_ref, cw_ref, cb_ref, wbd_ref, bx_ref, ba_ref, lam_ref,
             ya_ref, hs_ref, xc_ref, gi_ref, gr_ref, a_s, u_s):
        sp = _softplus(-lam_ref[...])

        def gates(c, carry):
            _, xc, gi, gr, log_a = _lru_gates(xa_ref, c, cw_ref, cb_ref, wbd_ref, bx_ref, ba_ref, sp)
            rows = _rows(c, RC)
            a_s[rows, :] = jnp.exp(log_a)
            u_s[rows, :] = jnp.sqrt(-_expm1_nonpos(2.0 * log_a)) * (gi * xc)
            xc_ref[rows, :] = xc
            gi_ref[rows, :] = gi
            gr_ref[rows, :] = gr
            return carry

        lax.fori_loop(0, S // RC, gates, 0)

        def scan(g, h):
            for k in range(SCAN_GROUP):
                rows = pl.ds(pl.multiple_of(g * (8 * SCAN_GROUP), 8 * SCAN_GROUP) + 8 * k, 8)
                a_cum, u_cum = _tile_scan(a_s[rows, :], u_s[rows, :])
                hs_ref[rows, :] = u_cum + a_cum * h
                h = u_cum[7:8, :] + a_cum[7:8, :] * h
            return h

        lax.fori_loop(0, S // (8 * SCAN_GROUP), scan, jnp.zeros((1, CB), F32))

        def gate_out(c, carry):
            ga = ga_ref[_rows(c, RC), :].astype(F32)
            ya_ref[_rows(c, RC), :] = (ga * _sigmoid(ga) * hs_ref[_rows(c, RC), :]).astype(BF16)
            return carry

        lax.fori_loop(0, S // RC, gate_out, 0)

    vec = pl.BlockSpec((1, CB), lambda b, cb: (0, cb))
    blk = pl.BlockSpec((S, CB), lambda b, cb: (b, cb))
    return pl.pallas_call(
        body, name="lru_fwd", grid=(nb, NCB),
        in_specs=[pl.BlockSpec((None, S, CB), lambda b, cb: (0, b, cb)),
                  pl.BlockSpec((None, S, CB), lambda b, cb: (1, b, cb)),
                  pl.BlockSpec((4, CB), lambda b, cb: (0, cb)),
                  vec,
                  pl.BlockSpec((None, CB, 2 * CB), lambda b, cb: (cb, 0, 0)),
                  vec, vec, vec],
        out_specs=[blk] + [pl.BlockSpec((None, None, S, CB), lambda b, cb: (b, cb, 0, 0))] * 4,
        out_shape=[jax.ShapeDtypeStruct((t, D), BF16)] + [jax.ShapeDtypeStruct((nb, NCB, S, CB), F32)] * 4,
        scratch_shapes=[pltpu.VMEM((S, CB), F32), pltpu.VMEM((S, CB), F32)],
        compiler_params=_params(("arbitrary", "arbitrary")),
    )(proj, proj, conv_w, conv_b, wbd, bx, ba, lam)


def _lru_bwd(proj, hs, xc_f, gi_f, gr_f, dya, conv_w, wbd, lam, nb, give):
    t = nb * S
    ng = len(give)

    def body(xa_ref, ga_ref, hs_ref, xc_s, gi_s, gr_s, dya_ref, cw_ref, wbd_ref, lam_ref, *rest):
        give_refs, rest = rest[:ng], rest[ng:]
        dp_ref, dwbd_ref, vec_ref = rest[:3]
        got_refs, rest = rest[3:3 + ng], rest[3 + ng:]
        a_s, dl_s, dh_s, dxc_s, acc_s, send_sems, recv_sems = rest
        b = pl.program_id(1)
        exchange = _sibling_copies(give_refs, got_refs, send_sems, recv_sems)

        @pl.when(jnp.logical_and(pl.program_id(0) == 0, b == 0))
        def _():
            for cp in exchange:
                cp.start()

        lam_v = lam_ref[...]
        sp = _softplus(-lam_v)
        acc_s[...] = jnp.zeros_like(acc_s)

        @pl.when(b == 0)
        def _():
            dwbd_ref[...] = jnp.zeros_like(dwbd_ref)
            vec_ref[...] = jnp.zeros_like(vec_ref)

        def gates(c, carry):
            rows = _rows(c, RC)
            a_s[rows, :] = jnp.exp(-LRU_C * gr_s[rows, :] * sp)
            ga = ga_ref[rows, :].astype(F32)
            sg = _sigmoid(ga)
            dya_c = dya_ref[rows, :]
            dl_s[rows, :] = dya_c * (ga * sg)
            dp_ref[1, rows, :] = (dya_c * hs_ref[rows, :] * (sg * (1.0 + ga * (1.0 - sg)))).astype(BF16)
            return carry

        lax.fori_loop(0, S // RC, gates, 0)

        def scan(i, g_in):
            base = pl.multiple_of((S // (8 * SCAN_GROUP) - 1 - i) * (8 * SCAN_GROUP), 8 * SCAN_GROUP)
            row = lax.broadcasted_iota(jnp.int32, (8, CB), 0)
            for k in reversed(range(SCAN_GROUP)):
                rows = pl.ds(base + 8 * k, 8)
                a = a_s[rows, :]
                dl = dl_s[rows, :]
                a_cum, g_loc = _tile_scan_rev(a, a * dl)
                g = g_loc + a_cum * g_in
                dh_s[rows, :] = dl + jnp.where(row < 7, pltpu.roll(g, 7, 0), g_in)
                g_in = g_loc[0:1, :] + a_cum[0:1, :] * g_in
            return g_in

        lax.fori_loop(0, S // (8 * SCAN_GROUP), scan, jnp.zeros((1, CB), F32))

        dxc_s[pl.ds(S, 8), :] = jnp.zeros((8, CB), F32)

        def grads(c, carry):
            rows = _rows(c, RC)
            dh = dh_s[rows, :]
            h_prev = _shift_down(_window_before(hs_ref, c, RC), 1, RC)
            xc, gi, gr, a = xc_s[rows, :], gi_s[rows, :], gr_s[rows, :], a_s[rows, :]
            mult = jnp.sqrt(-_expm1_nonpos(-2.0 * LRU_C * gr * sp))
            dmult = dh * gi * xc
            d_log_a = dh * h_prev * a - dmult * (a * a) / mult
            dzi = dh * mult * xc * gi * (1.0 - gi)
            dzr = d_log_a * (-LRU_C * sp) * gr * (1.0 - gr)
            dz = jnp.concatenate([dzi, dzr], axis=1).astype(BF16)
            dxc_s[rows, :] = dh * mult * gi + _dot_nt(dz, wbd_ref[...])
            dwbd_ref[...] += _dot_tn(xc.astype(BF16), dz)
            acc_s[1:2, :] += jnp.sum(dzi, axis=0, keepdims=True)
            acc_s[2:3, :] += jnp.sum(dzr, axis=0, keepdims=True)
            acc_s[3:4, :] += jnp.sum(d_log_a * (-LRU_C * gr), axis=0, keepdims=True)
            return carry

        lax.fori_loop(0, S // RC, grads, 0, unroll=2)

        def conv_bwd(c, carry):
            rows = _rows(c, RC_CONV)
            dwin = dxc_s[pl.ds(pl.multiple_of(c * RC_CONV, RC_CONV), RC_CONV + 8), :]
            dxc = dwin[:RC_CONV, :]
            xwin = _window_before(xa_ref, c, RC_CONV)
            dxa = cw_ref[3:4, :] * dxc
            acc_s[0:1, :] += jnp.sum(dxc, axis=0, keepdims=True)
            acc_s[7:8, :] += jnp.sum(dxc * _shift_down(xwin, 0, RC_CONV), axis=0, keepdims=True)
            for s in (1, 2, 3):
                dxa = dxa + cw_ref[3 - s:4 - s, :] * _shift_up(dwin, s, RC_CONV)
                acc_s[7 - s:8 - s, :] += jnp.sum(dxc * _shift_down(xwin, s, RC_CONV), axis=0, keepdims=True)
            dp_ref[0, rows, :] = dxa.astype(BF16)
            return carry

        lax.fori_loop(0, S // RC_CONV, conv_bwd, 0)

        row = lax.broadcasted_iota(jnp.int32, acc_s.shape, 0)
        vec_ref[...] += jnp.where(row == 3, acc_s[...] * (-_sigmoid(-lam_v)), acc_s[...])

        @pl.when(jnp.logical_and(pl.program_id(0) == NCB - 1, b == nb - 1))
        def _():
            for cp in exchange:
                cp.wait()

    vec = pl.BlockSpec((1, CB), lambda cb, b: (0, cb))
    blk = pl.BlockSpec((S, CB), lambda cb, b: (b, cb))
    own = pl.BlockSpec((None, None, S, CB), lambda cb, b: (b, cb, 0, 0))
    return pl.pallas_call(
        body, name="lru_bwd", grid=(NCB, nb),
        in_specs=[pl.BlockSpec((None, S, CB), lambda cb, b: (0, b, cb)),
                  pl.BlockSpec((None, S, CB), lambda cb, b: (1, b, cb)),
                  own, own, own, own, blk,
                  pl.BlockSpec((4, CB), lambda cb, b: (0, cb)),
                  pl.BlockSpec((None, CB, 2 * CB), lambda cb, b: (cb, 0, 0)),
                  vec] + [ANY] * ng,
        out_specs=[pl.BlockSpec((2, S, CB), lambda cb, b: (0, b, cb)),
                   pl.BlockSpec((None, CB, 2 * CB), lambda cb, b: (cb, 0, 0)),
                   pl.BlockSpec((8, CB), lambda cb, b: (0, cb))] + [ANY] * ng,
        out_shape=[jax.ShapeDtypeStruct((2, t, D), BF16),
                   jax.ShapeDtypeStruct((NCB, CB, 2 * CB), F32),
                   jax.ShapeDtypeStruct((8, D), F32)]
        + [jax.ShapeDtypeStruct((4,) + g.shape[1:], g.dtype) for g in give],
        scratch_shapes=[pltpu.VMEM((S, CB), F32), pltpu.VMEM((S, CB), F32), pltpu.VMEM((S, CB), F32),
                        pltpu.VMEM((S + 8, CB), F32), pltpu.VMEM((8, CB), F32),
                        pltpu.SemaphoreType.DMA((ng, 4)), pltpu.SemaphoreType.DMA((ng, 4))],
        compiler_params=_params(("arbitrary", "arbitrary")),
    )(proj, proj, hs, xc_f, gi_f, gr_f, dya, conv_w, wbd, lam, *give)


def _retention_tables():
    f32 = np.float32
    log_g = np.log1p(-(f32(2.0) ** (f32(-5.0) - np.arange(HEADS, dtype=f32)))).astype(f32)
    idx = np.arange(CH, dtype=f32)
    diff = idx[:, None] - idx[None, :]
    inner = np.where(diff >= 0, np.exp(np.maximum(diff, f32(0.0))[None] * log_g[:, None, None]), f32(0.0)).astype(f32)
    cross = np.exp((idx[None, :] + f32(1.0)) * log_g[:, None]).astype(f32)
    state = np.exp((f32(CH - 1.0) - idx[None, :]) * log_g[:, None]).astype(f32)
    cross = np.ascontiguousarray(np.broadcast_to(cross[:, :, None], (HEADS, CH, DK)))
    state = np.ascontiguousarray(np.broadcast_to(state[:, :, None], (HEADS, CH, DK)))
    half = DK // 2
    freqs = (f32(10000.0) ** (-np.arange(half, dtype=f32) / f32(half))).astype(f32)
    ang = (np.arange(S, dtype=f32)[:, None] * freqs[None, :]).astype(f32)
    return tuple(jnp.asarray(a) for a in (inner, cross, state, np.cos(ang).astype(f32), np.sin(ang).astype(f32)))


def _rotate(x, cos, sin):
    half = DK // 2
    x1, x2 = x[:, :half], x[:, half:]
    return jnp.concatenate([x1 * cos - x2 * sin, x1 * sin + x2 * cos], axis=1)


def _rotate_back(d, cos, sin):
    half = DK // 2
    d1, d2 = d[:, :half], d[:, half:]
    return jnp.concatenate([d1 * cos + d2 * sin, d2 * cos - d1 * sin], axis=1)


def _ret_fwd(proj, gain, tables, nb, wp_own):
    t = nb * S
    inner_t, cross_t, state_t, cos_t, sin_t = tables

    def body(q_ref, k_ref, v_ref, gb_ref, gain_ref, dm_ref, cd_ref, sd_ref, cos_ref, sin_ref, wp_ref,
             yb_ref, qr_ref, kr_ref, o_ref, rs_ref, wpg_ref, r_s, send_sems, recv_sems, own_sems):
        b, hd = pl.program_id(0), pl.program_id(1)
        own, first, arrive, forward, others = _gather_copies([wp_ref], [wpg_ref], send_sems, recv_sems, own_sems)

        @pl.when(jnp.logical_and(b == 0, hd == 0))
        def _():
            for cp in own + first:
                cp.start()

        @pl.when(jnp.logical_and(b == nb - 1, hd == HEADS - 1))
        def _():
            for came, on in zip(arrive, forward):
                came.wait_recv()
                on.start()

        r_s[...] = jnp.zeros_like(r_s)
        chunk_decay = cd_ref[CH - 1:CH, :]

        def chunk(c, carry):
            rows = _rows(c, CH)
            cos, sin = cos_ref[rows, :], sin_ref[rows, :]
            qr = _rotate(q_ref[rows, :].astype(F32), cos, sin).astype(BF16)
            kr = (_rotate(k_ref[rows, :].astype(F32), cos, sin) * (DK ** -0.5)).astype(BF16)
            vb = v_ref[rows, :]
            v = vb.astype(F32)
            qr_ref[rows, :] = qr
            kr_ref[rows, :] = kr
            r = r_s[...]
            rb = r.astype(BF16)
            rs_ref[c] = rb
            p = (_dot_nt(qr, kr) * dm_ref[...]).astype(BF16)
            o = _dot(p, vb) + _dot(qr, rb) * cd_ref[...]
            r_s[...] = chunk_decay * r + _dot_tn(kr, (v * sd_ref[...]).astype(BF16))
            o_ref[rows, :] = o
            oc = o - jnp.mean(o, axis=-1, keepdims=True)
            rstd = lax.rsqrt(jnp.mean(oc * oc, axis=-1, keepdims=True) + EPS)
            gb = gb_ref[rows, :].astype(F32)
            yb_ref[rows, :] = (gb * _sigmoid(gb) * (oc * rstd * gain_ref[...])).astype(BF16)
            return carry

        lax.fori_loop(0, NCH, chunk, 0, unroll=2)

        @pl.when(jnp.logical_and(b == nb - 1, hd == HEADS - 1))
        def _():
            for cp in others:
                cp.wait_recv()
            for cp in first + forward:
                cp.wait_send()
            for cp in own:
                cp.wait()

    seg = lambda s: pl.BlockSpec((None, S, DK), lambda b, h: (s, b, h))
    tab = pl.BlockSpec((None, CH, DK), lambda b, h: (h, 0, 0))
    rot = pl.BlockSpec((S, DK // 2), lambda b, h: (0, 0))
    blk = pl.BlockSpec((S, DK), lambda b, h: (b, h))
    return pl.pallas_call(
        body, name="ret_fwd", grid=(nb, HEADS),
        in_specs=[seg(2), seg(3), seg(4), seg(5),
                  pl.BlockSpec((None, 1, DK), lambda b, h: (h, 0, 0)),
                  tab, tab, tab, rot, rot, ANY],
        out_specs=[blk, blk, blk, blk,
                   pl.BlockSpec((None, None, NCH, DK, DK), lambda b, h: (b, h, 0, 0, 0)), ANY],
        out_shape=[jax.ShapeDtypeStruct((t, D), BF16), jax.ShapeDtypeStruct((t, D), BF16),
                   jax.ShapeDtypeStruct((t, D), BF16), jax.ShapeDtypeStruct((t, D), F32),
                   jax.ShapeDtypeStruct((nb, HEADS, NCH, DK, DK), BF16),
                   jax.ShapeDtypeStruct((NDEV,) + wp_own.shape, wp_own.dtype)],
        scratch_shapes=[pltpu.VMEM((DK, DK), F32),
                        pltpu.SemaphoreType.DMA((1, 7)), pltpu.SemaphoreType.DMA((1, 7)), pltpu.SemaphoreType.DMA((1,))],
        compiler_params=_params(("arbitrary", "arbitrary")),
    )(proj, proj, proj, proj, gain, inner_t, cross_t, state_t, cos_t, sin_t, wp_own)


def _ret_bwd(proj, qr, kr, o, rs, dyb, gain, tables, nb, sums):
    t = nb * S
    ns = len(sums)
    inner_t, cross_t, state_t, cos_t, sin_t = tables

    def body(qr_ref, kr_ref, v_ref, gb_ref, o_ref, dyb_ref, rs_ref, gain_ref, dm_ref, cd_ref, sd_ref,
             cos_ref, sin_ref, *rest):
        sum_refs, rest = rest[:ns], rest[ns:]
        dp_ref, dgain_ref = rest[:2]
        part_refs, rest = rest[2:2 + ns], rest[2 + ns:]
        dr_s, send_sems, recv_sems, local_sems = rest
        mine, sends, recvs = _chip_copies(sum_refs, part_refs, send_sems, recv_sems, local_sems)

        @pl.when(jnp.logical_and(pl.program_id(0) == 0, pl.program_id(1) == 0))
        def _():
            for cp in mine + sends:
                cp.start()

        dr_s[...] = jnp.zeros_like(dr_s)
        chunk_decay = cd_ref[CH - 1:CH, :]

        @pl.when(pl.program_id(1) == 0)
        def _():
            dgain_ref[...] = jnp.zeros_like(dgain_ref)

        def chunk(i, carry):
            c = NCH - 1 - i
            rows = _rows(c, CH)
            gain_v = gain_ref[...]
            o_c = o_ref[rows, :]
            oc = o_c - jnp.mean(o_c, axis=-1, keepdims=True)
            rstd = lax.rsqrt(jnp.mean(oc * oc, axis=-1, keepdims=True) + EPS)
            yn = oc * rstd
            gb = gb_ref[rows, :].astype(F32)
            sg = _sigmoid(gb)
            dyb_c = dyb_ref[rows, :]
            dgn = dyb_c * (gb * sg)
            dp_ref[3, rows, :] = (dyb_c * (yn * gain_v) * (sg * (1.0 + gb * (1.0 - sg)))).astype(BF16)
            dgain_ref[...] += jnp.sum(dgn * yn, axis=0, keepdims=True)
            dyn = dgn * gain_v
            do = rstd * (dyn - jnp.mean(dyn, axis=-1, keepdims=True)
                         - yn * jnp.mean(dyn * yn, axis=-1, keepdims=True))
            dob = do.astype(BF16)
            dox = (do * cd_ref[...]).astype(BF16)

            q_c, k_c = qr_ref[rows, :], kr_ref[rows, :]
            vb = v_ref[rows, :]
            v = vb.astype(F32)
            vs = (v * sd_ref[...]).astype(BF16)
            rb = rs_ref[c]
            d_r = dr_s[...]
            drb = d_r.astype(BF16)
            dm = dm_ref[...]
            p = (_dot_nt(q_c, k_c) * dm).astype(BF16)
            dpm = (_dot_nt(dob, vb) * dm).astype(BF16)
            dq = _dot(dpm, k_c) + _dot_nt(dox, rb)
            dk = _dot_tn(dpm, q_c) + _dot_nt(vs, drb)
            dv = _dot_tn(p, dob) + _dot(k_c, drb) * sd_ref[...]
            dr_s[...] = chunk_decay * d_r + _dot_tn(q_c, dox)

            cos, sin = cos_ref[rows, :], sin_ref[rows, :]
            dp_ref[0, rows, :] = _rotate_back(dq, cos, sin).astype(BF16)
            dp_ref[1, rows, :] = (_rotate_back(dk, cos, sin) * (DK ** -0.5)).astype(BF16)
            dp_ref[2, rows, :] = dv.astype(BF16)
            return carry

        lax.fori_loop(0, NCH, chunk, 0, unroll=2)

        @pl.when(jnp.logical_and(pl.program_id(0) == HEADS - 1, pl.program_id(1) == nb - 1))
        def _():
            for cp in recvs:
                cp.wait_recv()
            for cp in sends:
                cp.wait_send()
            for cp in mine:
                cp.wait()

    seg = lambda s: pl.BlockSpec((None, S, DK), lambda h, b: (s, b, h))
    tab = pl.BlockSpec((None, CH, DK), lambda h, b: (h, 0, 0))
    rot = pl.BlockSpec((S, DK // 2), lambda h, b: (0, 0))
    blk = pl.BlockSpec((S, DK), lambda h, b: (b, h))
    one = pl.BlockSpec((None, 1, DK), lambda h, b: (h, 0, 0))
    return pl.pallas_call(
        body, name="ret_bwd", grid=(HEADS, nb),
        in_specs=[blk, blk, seg(4), seg(5), blk, blk,
                  pl.BlockSpec((None, None, NCH, DK, DK), lambda h, b: (b, h, 0, 0, 0)),
                  one, tab, tab, tab, rot, rot] + [ANY] * ns,
        out_specs=[pl.BlockSpec((4, S, DK), lambda h, b: (0, b, h)), one] + [ANY] * ns,
        out_shape=[jax.ShapeDtypeStruct((4, t, D), BF16), jax.ShapeDtypeStruct((HEADS, 1, DK), F32)]
        + [jax.ShapeDtypeStruct(a.shape, a.dtype) for a in sums],
        scratch_shapes=[pltpu.VMEM((DK, DK), F32), pltpu.SemaphoreType.DMA((ns, 3)), pltpu.SemaphoreType.DMA((ns, 3)),
                        pltpu.SemaphoreType.DMA((ns,))],
        compiler_params=_params(("arbitrary", "arbitrary")),
    )(qr, kr, proj, proj, o, dyb, rs, gain, inner_t, cross_t, state_t, cos_t, sin_t, *sums)


def _wblock(k):
    return pl.BlockSpec((NDEV, D // NDEV, D), lambda i: (0, k, 0))


def _tail(ya, yb, proj, x2d, tgt, wg, g_fin):
    t = x2d.shape[0]
    tm = 256

    def body(ya_ref, yb_ref, ma_ref, mb_ref, x_ref, t_ref, wa_ref, wb_ref, wo_ref, g_ref,
             dx2_ref, dya_ref, dyb_ref, dm_ref, mg_ref, doa_ref, dob_ref, gfin_ref, loss_ref):
        i = pl.program_id(0)

        @pl.when(i == 0)
        def _():
            gfin_ref[...] = jnp.zeros_like(gfin_ref)
            loss_ref[...] = jnp.zeros_like(loss_ref)

        wa = wa_ref[...].reshape(D, D)
        wb = wb_ref[...].reshape(D, D)
        wo = wo_ref[...].reshape(D, D)
        out_a = _dot(ya_ref[...], wa)
        out_b = _dot(yb_ref[...], wb)
        sa = _sigmoid(ma_ref[...].astype(F32))
        sb = _sigmoid(mb_ref[...].astype(F32))
        merged = (sa * out_a + sb * out_b).astype(BF16)
        mg_ref[...] = merged
        x2 = x_ref[...] + _dot(merged, wo)
        r2 = lax.rsqrt(jnp.mean(x2 * x2, axis=-1, keepdims=True) + EPS)
        xh = x2 * r2
        g = g_ref[...]
        err = xh * g - t_ref[...]
        loss_ref[...] += jnp.sum(err * err, axis=0, keepdims=True) * (0.5 / D)
        dy = err * (1.0 / D)
        gfin_ref[...] += jnp.sum(dy * xh, axis=0, keepdims=True)
        dxh = dy * g
        dx2 = r2 * (dxh - xh * jnp.mean(dxh * xh, axis=-1, keepdims=True))
        dx2_ref[...] = dx2
        dmerged = _dot_nt(dx2.astype(BF16), wo)
        doa = (sa * dmerged).astype(BF16)
        dob = (sb * dmerged).astype(BF16)
        doa_ref[...] = doa
        dob_ref[...] = dob
        dm_ref[0] = (dmerged * out_a * sa * (1.0 - sa)).astype(BF16)
        dm_ref[1] = (dmerged * out_b * sb * (1.0 - sb)).astype(BF16)
        dya_ref[...] = _dot_nt(doa, wa)
        dyb_ref[...] = _dot_nt(dob, wb)

    row = lambda: pl.BlockSpec((tm, D), lambda i: (i, 0))
    seg = lambda s: pl.BlockSpec((None, tm, D), lambda i: (s, i, 0))
    vec = pl.BlockSpec((1, D), lambda i: (0, 0))
    return pl.pallas_call(
        body, name="tail", grid=(t // tm,),
        in_specs=[row(), row(), seg(6), seg(7), row(), row(), _wblock(0), _wblock(1), _wblock(2), vec],
        out_specs=[row(), row(), row(), pl.BlockSpec((2, tm, D), lambda i: (0, i, 0)),
                   row(), row(), row(), vec, vec],
        out_shape=[jax.ShapeDtypeStruct((t, D), F32), jax.ShapeDtypeStruct((t, D), F32),
                   jax.ShapeDtypeStruct((t, D), F32), jax.ShapeDtypeStruct((2, t, D), BF16),
                   jax.ShapeDtypeStruct((t, D), BF16), jax.ShapeDtypeStruct((t, D), BF16),
                   jax.ShapeDtypeStruct((t, D), BF16), jax.ShapeDtypeStruct((1, D), F32),
                   jax.ShapeDtypeStruct((1, D), F32)],
        compiler_params=_params(("arbitrary",)),
    )(ya, yb, proj, proj, x2d, tgt, wg, wg, wg, g_fin)


def _tail_wgrad(ya, yb, merged, doa, dob, dx2):
    t = ya.shape[0]
    tm = 512

    def body(ya_ref, yb_ref, mg_ref, doa_ref, dob_ref, dx2_ref, ga_ref, gb_ref, go_ref):
        @pl.when(pl.program_id(0) == 0)
        def _():
            ga_ref[...] = jnp.zeros_like(ga_ref)
            gb_ref[...] = jnp.zeros_like(gb_ref)
            go_ref[...] = jnp.zeros_like(go_ref)

        ga_ref[...] += _dot_tn(ya_ref[...], doa_ref[...])
        gb_ref[...] += _dot_tn(yb_ref[...], dob_ref[...])
        go_ref[...] += _dot_tn(mg_ref[...], dx2_ref[...].astype(BF16))

    row = lambda: pl.BlockSpec((tm, D), lambda i: (i, 0))
    full = lambda: pl.BlockSpec((D, D), lambda i: (0, 0))
    return pl.pallas_call(
        body, name="tail_wgrad", grid=(t // tm,),
        in_specs=[row() for _ in range(6)], out_specs=[full(), full(), full()],
        out_shape=[jax.ShapeDtypeStruct((D, D), F32)] * 3,
        compiler_params=_params(("arbitrary",)),
    )(ya, yb, merged, doa, dob, dx2)


def _dproj_specs_ordered(tm):
    def spec(lo, n):
        def index(k, i, order_ref):
            seg = order_ref[k]
            mine = jnp.logical_and(seg >= lo, seg < lo + n)
            return jnp.where(mine, seg - lo, 0), jnp.where(mine, i, 0), 0
        return pl.BlockSpec((None, tm, D), index)
    return [spec(0, 2), spec(2, 4), spec(6, 2)]


def _dproj_pick(j, da_ref, db_ref, dc_ref, use):
    @pl.when(j < 2)
    def _():
        use(da_ref[...])

    @pl.when(jnp.logical_and(j >= 2, j < 6))
    def _():
        use(db_ref[...])

    @pl.when(j >= 6)
    def _():
        use(dc_ref[...])


RS_X, RS_Y, RS_XY = 0, 1, 2
RS_ROLES = ((RS_XY, RS_X, RS_Y), (RS_Y, RS_XY, RS_X))


def _rs_flip(rel, x, y):
    return ((1 - x, y), (x, 1 - y), (1 - x, 1 - y))[rel]


def _rs_order(x, y, c):
    order = []
    for s in range(4):
        chip = []
        for core in (0, 1):
            px, py = _rs_flip(RS_ROLES[core][s], x, y) if s < 3 else (x, y)
            chip.append(2 * px + py)
        keep = jnp.where(c == 0, chip[0], chip[1])
        give = jnp.where(c == 0, chip[1], chip[0])
        order += [2 * give + 1 - c, 2 * keep + c]
    return jnp.stack(order).astype(jnp.int32)


def _inproj_wgrad_rs(h, dpa, dpb, dpc, order, smalls):
    t = h.shape[0]
    tm = 1024
    nt = t // tm
    nsm = len(smalls)

    def body(order_ref, h_ref, da_ref, db_ref, dc_ref, *rest):
        small_refs, parts_ref, rest = rest[:nsm], rest[nsm], rest[nsm + 1:]
        all_refs, rest = rest[:nsm], rest[nsm:]
        (acc, sib, outb, far, give_send, give_recv, sum_send, sum_recv, far_send, far_recv, own_sem,
         small_send, small_recv, small_own) = rest
        k, i = pl.program_id(0), pl.program_id(1)
        x, y, c = _place()
        own, first, arrive, forward, others = _gather_copies(small_refs, all_refs, small_send, small_recv, small_own)

        @pl.when(jnp.logical_and(k == 0, i == 0))
        def _():
            for cp in own + first:
                cp.start()

        @pl.when(jnp.logical_and(k == 2, i == 0))
        def _():
            for came, on in zip(arrive, forward):
                came.wait_recv()
                on.start()

        def use(d):
            @pl.when(i == 0)
            def _():
                acc[k % 2] = _dot_tn(h_ref[...], d)

            @pl.when(i > 0)
            def _():
                acc[k % 2] += _dot_tn(h_ref[...], d)

        _dproj_pick(order_ref[k], da_ref, db_ref, dc_ref, use)

        def give_copy(s):
            return pltpu.make_async_remote_copy(
                src_ref=acc.at[0], dst_ref=sib.at[s % 2], send_sem=give_send.at[s], recv_sem=give_recv.at[s],
                device_id=(x, y, 1 - c), device_id_type=MESH)

        def sum_copy(s, core):
            slot = 0 if s < 2 else 1
            return pltpu.make_async_remote_copy(
                src_ref=outb.at[s], dst_ref=parts_ref.at[slot], send_sem=sum_send.at[slot], recv_sem=sum_recv.at[slot],
                device_id=(*_rs_flip(RS_ROLES[core][s], x, y), core), device_id_type=MESH)

        def far_copy(s, core):
            return pltpu.make_async_remote_copy(
                src_ref=outb.at[s], dst_ref=far, send_sem=far_send, recv_sem=far_recv,
                device_id=(*_rs_flip(RS_X if core == 0 else RS_Y, x, y), core), device_id_type=MESH)

        own_copy = pltpu.make_async_copy(outb.at[3], parts_ref.at[2], own_sem)

        def send_of(core, s):
            return far_copy(s, core) if RS_ROLES[core][s] == RS_XY else sum_copy(s, core)

        for s in range(4):
            @pl.when(jnp.logical_and(k == 2 * s, i == nt - 1))
            def _():
                give_copy(s).start()

            @pl.when(jnp.logical_and(k == 2 * s + 1, i == nt - 1))
            def _():
                give_copy(s).wait_recv()
                if s == 2:
                    far_copy(s, 0).wait_recv()
                    outb[s] = (acc[1] + sib[s % 2] + far[...].astype(F32)).astype(BF16)
                else:
                    outb[s] = (acc[1] + sib[s % 2]).astype(BF16)
                give_copy(s).wait_send()
                if s < 3:
                    for core in (0, 1):
                        @pl.when(c == core)
                        def _():
                            send_of(core, s).start()
                else:
                    own_copy.start()

        @pl.when(jnp.logical_and(k == NSEG - 1, i == nt - 1))
        def _():
            for slot in (0, 1):
                sum_copy(2 * slot, 0).wait_recv()
            for s in range(3):
                send_of(0, s).wait_send()
            own_copy.wait()
            for cp in others:
                cp.wait_recv()
            for cp in first + forward:
                cp.wait_send()
            for cp in own:
                cp.wait()

    return pl.pallas_call(
        body, name="inproj_wgrad_rs",
        grid_spec=pltpu.PrefetchScalarGridSpec(
            num_scalar_prefetch=1, grid=(NSEG, nt),
            in_specs=[pl.BlockSpec((tm, D), lambda k, i, order_ref: (i, 0))] + _dproj_specs_ordered(tm) + [ANY] * nsm,
            out_specs=[ANY] * (1 + nsm),
            scratch_shapes=[pltpu.VMEM((2, D, D), F32), pltpu.VMEM((2, D, D), F32), pltpu.VMEM((4, D, D), BF16),
                            pltpu.VMEM((D, D), BF16),
                            pltpu.SemaphoreType.DMA((4,)), pltpu.SemaphoreType.DMA((4,)),
                            pltpu.SemaphoreType.DMA((2,)), pltpu.SemaphoreType.DMA((2,)),
                            pltpu.SemaphoreType.DMA, pltpu.SemaphoreType.DMA, pltpu.SemaphoreType.DMA,
                            pltpu.SemaphoreType.DMA((nsm, 7)), pltpu.SemaphoreType.DMA((nsm, 7)),
                            pltpu.SemaphoreType.DMA((nsm,))]),
        out_shape=[jax.ShapeDtypeStruct((3, D, D), BF16)]
        + [jax.ShapeDtypeStruct((NDEV,) + a.shape, a.dtype) for a in smalls],
        compiler_params=_params(("arbitrary", "arbitrary")),
    )(order, h, dpa, dpb, dpc, *smalls)


def _inproj_dgrad(dpa, dpb, dpc, wg, x2d, dx2, g_in):
    t = x2d.shape[0]
    tm = 512

    def body(da_ref, db_ref, dc_ref, w_hbm, x_ref, dx2_ref, g_ref, gx_ref, gg_ref, w_s, w_sem):
        i = pl.program_id(0)

        @pl.when(i == 0)
        def _():
            gg_ref[...] = jnp.zeros_like(gg_ref)
            load = pltpu.make_async_copy(w_hbm, w_s, w_sem)
            load.start()
            load.wait()

        dh = None
        for ref, lo in ((da_ref, 0), (db_ref, 2), (dc_ref, 6)):
            for k in range(ref.shape[0]):
                part = _dot_nt(ref[k], w_s[lo + k])
                dh = part if dh is None else dh + part
        x = x_ref[...]
        r = lax.rsqrt(jnp.mean(x * x, axis=-1, keepdims=True) + EPS)
        xh = x * r
        gg_ref[...] += jnp.sum(dh * xh, axis=0, keepdims=True)
        dxh = dh * g_ref[...]
        gx_ref[...] = dx2_ref[...] + r * (dxh - xh * jnp.mean(dxh * xh, axis=-1, keepdims=True))

    row = lambda: pl.BlockSpec((tm, D), lambda i: (i, 0))
    seg = lambda n: pl.BlockSpec((n, tm, D), lambda i: (0, i, 0))
    vec = pl.BlockSpec((1, D), lambda i: (0, 0))
    return pl.pallas_call(
        body, name="inproj_dgrad", grid=(t // tm,),
        in_specs=[seg(2), seg(4), seg(2), ANY, row(), row(), vec],
        out_specs=[row(), vec],
        out_shape=[jax.ShapeDtypeStruct((t, D), F32), jax.ShapeDtypeStruct((1, D), F32)],
        scratch_shapes=[pltpu.VMEM((NSEG, D, D), BF16), pltpu.SemaphoreType.DMA],
        compiler_params=_params(("arbitrary",)),
    )(dpa, dpb, dpc, wg, x2d, dx2, g_in)


def _adam_update(g, w, m, v):
    m_new = ADAM_B1 * m + (1.0 - ADAM_B1) * g
    v_new = ADAM_B2 * v + (1.0 - ADAM_B2) * (g * g)
    m_hat = m_new / (1.0 - ADAM_B1 ** ADAM_STEP)
    v_hat = v_new / (1.0 - ADAM_B2 ** ADAM_STEP)
    return -ADAM_LR * (m_hat / (jnp.sqrt(v_hat) + ADAM_EPS) + ADAM_WD * w), m_new, v_new


def _sum_in_order(ref):
    total = ref[0].astype(F32)
    for k in range(1, ref.shape[0]):
        total = total + ref[k].astype(F32)
    return total


def _adamw_small(me, vec_all, gx_all, ga_all, groups):
    flat = [a for grp in groups for a in grp]
    ng = len(groups)
    nshard = D // NDEV

    def body(me_ref, vec_ref, shard_ref, gx_ref, ga_ref, *refs):
        ins, outs = refs[:3 * ng], refs[3 * ng:]
        vec = _sum_in_order(vec_ref)
        shard = _sum_in_order(shard_ref)
        grads = [vec[r:r + 1, :] for r in range(6)]
        grads += [shard[0:4, :], shard[4:8, 0:DK // NDEV], _sum_in_order(gx_ref), _sum_in_order(ga_ref)]
        for n, g in enumerate(grads):
            delta, m_new, v_new = _adam_update(g, ins[3 * n][...], ins[3 * n + 1][...], ins[3 * n + 2][...])
            outs[4 * n][...] = g
            outs[4 * n + 1][...] = delta
            outs[4 * n + 2][...] = m_new
            outs[4 * n + 3][...] = v_new
        outs[4 * ng][...] = jnp.sum(vec[6:7, :], axis=1, keepdims=True)

    full = lambda a: pl.BlockSpec(a.shape, lambda i, me_ref, nd=len(a.shape): (0,) * nd)
    out_shape = [jax.ShapeDtypeStruct(w.shape, F32) for w, _, _ in groups for _ in range(4)]
    out_shape.append(jax.ShapeDtypeStruct((1, 1), F32))
    outs = pl.pallas_call(
        body, name="adamw_small",
        grid_spec=pltpu.PrefetchScalarGridSpec(
            num_scalar_prefetch=1, grid=(1,),
            in_specs=[full(vec_all),
                      pl.BlockSpec((NDEV, 8, nshard), lambda i, me_ref: (0, 1, me_ref[0])),
                      full(gx_all), full(ga_all)] + [full(a) for a in flat],
            out_specs=[full(s) for s in out_shape]),
        out_shape=out_shape,
        compiler_params=_params(("arbitrary",)),
    )(me, vec_all, vec_all, gx_all, ga_all, *flat)
    return [outs[4 * n:4 * n + 4] for n in range(ng)], outs[4 * ng]


def _adamw(name, items):
    n, rows, cols = items[0][0].shape
    tr = rows if rows <= 256 else 256
    k = len(items)

    def body(*refs):
        for a in range(k):
            p_ref, w_ref, m_ref, v_ref = refs[4 * a:4 * a + 4]
            g = _sum_in_order(p_ref)
            delta, m_new, v_new = _adam_update(g, w_ref[...], m_ref[...], v_ref[...])
            for o, val in zip(refs[4 * k + 4 * a:4 * k + 4 * a + 4], (g, delta, m_new, v_new)):
                o[...] = val

    blk = lambda: pl.BlockSpec((tr, cols), lambda i: (i, 0))
    outs = pl.pallas_call(
        body, name=name, grid=(rows // tr,),
        in_specs=[pl.BlockSpec((n, tr, cols), lambda i: (0, i, 0)), blk(), blk(), blk()] * k,
        out_specs=[blk() for _ in range(4 * k)],
        out_shape=[jax.ShapeDtypeStruct((rows, cols), F32)] * (4 * k),
        compiler_params=_params(("arbitrary",)),
    )(*[a for item in items for a in item])
    return [outs[4 * a:4 * a + 4] for a in range(k)]


ANY = pl.BlockSpec(memory_space=pl.ANY)


def _place():
    return lax.axis_index("x"), lax.axis_index("y"), lax.axis_index("c")


def _gather_copies(ins, outs, send_sems, recv_sems, own_sems):
    x, y, c = _place()
    me, sibling = (x, y, c), (x, y, 1 - c)
    chips = [(1 - x, y), (x, 1 - y), (1 - x, 1 - y)]
    n = len(ins)

    def copy(a, k, block, to, src=None):
        px, py, pc = block
        dst = outs[a].at[4 * px + 2 * py + pc]
        return pltpu.make_async_remote_copy(
            src_ref=dst if src is None else src, dst_ref=dst,
            send_sem=send_sems.at[a, k], recv_sem=recv_sems.at[a, k], device_id=to, device_id_type=MESH)

    own = [pltpu.make_async_copy(ins[a], outs[a].at[4 * x + 2 * y + c], own_sems.at[a]) for a in range(n)]
    first = []
    for a in range(n):
        first.append(copy(a, 0, me, sibling, src=ins[a]))
        first += [copy(a, 1 + j, me, (*chip, c), src=ins[a]) for j, chip in enumerate(chips)]
    arrive = [copy(a, 1 + j, (*chip, c), me) for j, chip in enumerate(chips) for a in range(n)]
    forward = [copy(a, 4 + j, (*chip, c), sibling) for j, chip in enumerate(chips) for a in range(n)]
    rest = [copy(a, 0, sibling, me) for a in range(n)]
    rest += [copy(a, 4 + j, (*chip, 1 - c), me) for a in range(n) for j, chip in enumerate(chips)]
    return own, first, arrive, forward, rest


def _sibling_copies(ins, outs, send_sems, recv_sems):
    x, y, c = _place()
    return [pltpu.make_async_remote_copy(
        src_ref=ins[a].at[2 * q + 1 - c], dst_ref=outs[a].at[q],
        send_sem=send_sems.at[a, q], recv_sem=recv_sems.at[a, q],
        device_id=(x, y, 1 - c), device_id_type=MESH) for a in range(len(ins)) for q in range(4)]


def _chip_copies(ins, outs, send_sems, recv_sems, local_sems):
    x, y, c = _place()
    my_chip = 2 * x + y
    chips = [(1 - x, y), (x, 1 - y), (1 - x, 1 - y)]
    n = len(ins)
    mine = [pltpu.make_async_copy(ins[a].at[my_chip], outs[a].at[my_chip], local_sems.at[a]) for a in range(n)]
    sends = [pltpu.make_async_remote_copy(
        src_ref=ins[a].at[2 * px + py], dst_ref=outs[a].at[my_chip],
        send_sem=send_sems.at[a, j], recv_sem=recv_sems.at[a, j],
        device_id=(px, py, c), device_id_type=MESH) for a in range(n) for j, (px, py) in enumerate(chips)]
    recvs = [pltpu.make_async_remote_copy(
        src_ref=ins[a].at[my_chip], dst_ref=outs[a].at[2 * px + py],
        send_sem=send_sems.at[a, j], recv_sem=recv_sems.at[a, j],
        device_id=(px, py, c), device_id_type=MESH) for a in range(n) for j, (px, py) in enumerate(chips)]
    return mine, sends, recvs


def _chip_sum(owns, gots, core):
    n = len(owns)
    _, rows, cols = owns[0].shape

    def body(core_ref, *refs):
        for a in range(n):
            refs[2 * n + a][...] = (refs[a][...] + refs[n + a][...]).astype(BF16)

    own_spec = pl.BlockSpec((None, rows, cols), lambda q, core_ref: (2 * q + core_ref[0], 0, 0))
    slab = pl.BlockSpec((None, rows, cols), lambda q, core_ref: (q, 0, 0))
    return pl.pallas_call(
        body, name="chip_sum",
        grid_spec=pltpu.PrefetchScalarGridSpec(
            num_scalar_prefetch=1, grid=(4,),
            in_specs=[own_spec] * n + [slab] * n, out_specs=[slab] * n),
        out_shape=[jax.ShapeDtypeStruct((4, rows, cols), BF16)] * n,
        compiler_params=_params(("arbitrary",)),
    )(core, *owns, *gots)


def _block_diag(w):
    w4 = w.reshape(NCB, 4, 64, 64)
    eye = jnp.eye(4, dtype=w.dtype)
    return (w4[:, :, :, None, :] * eye[None, :, None, :, None]).reshape(NCB, CB, CB)


def _block_diag_back(g):
    g5 = g.reshape(NCB, 4, 64, 4, 64)
    return jnp.stack([g5[:, m, :, m, :] for m in range(4)], axis=1).reshape(16, 64, 64)


def kernel(x, norm_in, w_in, conv_w, conv_b, gate_x_w, gate_x_b, gate_a_w, gate_a_b, lru_lambda, gn_gain, w_proj_a, w_proj_b, w_out, norm_final, loss_target, m_norm_in, m_w_in, m_conv_w, m_conv_b, m_gate_x_w, m_gate_x_b, m_gate_a_w, m_gate_a_b, m_lru_lambda, m_gn_gain, m_w_proj_a, m_w_proj_b, m_w_out, m_norm_final, v_norm_in, v_w_in, v_conv_w, v_conv_b, v_gate_x_w, v_gate_x_b, v_gate_a_w, v_gate_a_b, v_lru_lambda, v_gn_gain, v_w_proj_a, v_w_proj_b, v_w_out, v_norm_final):
    xi, yi, ci = _place()
    me = 4 * xi + 2 * yi + ci
    core = ci.astype(jnp.int32).reshape(1)
    nshard = D // NDEV
    nb = x.shape[0]
    t = nb * S
    x2d = x.reshape(t, D)
    tgt2d = loss_target.reshape(t, D)
    g_final = norm_final.reshape(1, D)
    wbd = jnp.concatenate([_block_diag(gate_x_w[0]), _block_diag(gate_a_w[0])], axis=-1).astype(BF16)
    tables = _retention_tables()

    wp_own = jnp.concatenate([w_proj_a[0], w_proj_b[0], w_out[0]], axis=0).astype(BF16)
    tiny = jnp.concatenate([conv_w[0], jnp.pad(gn_gain[0], ((0, 0), (0, nshard - DK // NDEV)))], axis=0)
    proj, h, wg, tiny_g = _inproj_gather(x2d, norm_in, w_in[0].astype(BF16), tiny, *_gather_order(xi, yi, ci))
    conv_w_full = tiny_g[:, 0:4, :].transpose(1, 0, 2).reshape(4, D)
    gain3 = tiny_g[:, 4:8, :DK // NDEV].transpose(1, 0, 2).reshape(HEADS, 1, DK)

    ya, hs, xc, gi, gr = _lru_fwd(proj, conv_w_full, conv_b, wbd, gate_x_b, gate_a_b, lru_lambda, nb)
    yb, qr, kr, o, rs, wpg = _ret_fwd(proj, gain3, tables, nb, wp_own)
    dx2, dya, dyb, dpc, merged, doa, dob, g_fin, loss_vec = _tail(ya, yb, proj, x2d, tgt2d, wpg, g_final)
    g_pa, g_pb, g_out = _tail_wgrad(ya, yb, merged, doa, dob, dx2)

    own = [g.reshape(NDEV, nshard, D) for g in (g_pa, g_pb, g_out)]
    dpa, g_wbd, g_vec, *got = _lru_bwd(proj, hs, xc, gi, gr, dya, conv_w_full, wbd, lru_lambda, nb, own)
    sums = _chip_sum(own, got, core)
    dpb, g_gain, *parts = _ret_bwd(proj, qr, kr, o, rs, dyb, gain3, tables, nb, sums)

    grad_x, g_norm_in = _inproj_dgrad(dpa, dpb, dpc, wg, x2d, dx2, norm_in)
    grad_x = grad_x.reshape(nb, S, D)

    gain_rows = jnp.pad(g_gain.reshape(HEADS, NDEV, DK // NDEV), ((0, 0), (0, 0), (0, nshard - DK // NDEV)))
    vec = jnp.concatenate([g_norm_in, g_vec[0:4], g_fin, loss_vec, jnp.zeros((1, D), F32), g_vec[4:8],
                           gain_rows.reshape(HEADS, D)], axis=0)
    g_gx = _block_diag_back(g_wbd[:, :, :CB]).reshape(D // 2, 128)
    g_ga = _block_diag_back(g_wbd[:, :, CB:]).reshape(D // 2, 128)
    parts_in, vec_all, gx_all, ga_all = _inproj_wgrad_rs(h, dpa, dpb, dpc, _rs_order(xi, yi, ci),
                                                         [vec, g_gx, g_ga])
    gx_all = gx_all.reshape(NDEV, D, 64)
    ga_all = ga_all.reshape(NDEV, D, 64)
    parts = [parts_in] + list(parts)

    res = {}
    (out,) = _adamw("adamw_w_in", [(parts[0], w_in[0], m_w_in[0], v_w_in[0])])
    res["w_in"] = [o[None] for o in out]
    square = [("w_proj_a", w_proj_a, m_w_proj_a, v_w_proj_a), ("w_proj_b", w_proj_b, m_w_proj_b, v_w_proj_b),
              ("w_out", w_out, m_w_out, v_w_out)]
    outs = _adamw("adamw_square", [(parts[1 + k], w[0], m[0], v[0]) for k, (_, w, m, v) in enumerate(square)])
    for (nm, _, _, _), out in zip(square, outs):
        res[nm] = [o[None] for o in out]

    row = lambda a: a.reshape(1, D)
    gate = lambda a: a.reshape(D, 64)
    groups = [("norm_in", norm_in, m_norm_in, v_norm_in, row), ("conv_b", conv_b, m_conv_b, v_conv_b, row),
              ("gate_x_b", gate_x_b, m_gate_x_b, v_gate_x_b, row), ("gate_a_b", gate_a_b, m_gate_a_b, v_gate_a_b, row),
              ("lru_lambda", lru_lambda, m_lru_lambda, v_lru_lambda, row),
              ("norm_final", norm_final, m_norm_final, v_norm_final, row),
              ("conv_w", conv_w, m_conv_w, v_conv_w, lambda a: a[0]), ("gn_gain", gn_gain, m_gn_gain, v_gn_gain, lambda a: a[0]),
              ("gate_x_w", gate_x_w, m_gate_x_w, v_gate_x_w, gate), ("gate_a_w", gate_a_w, m_gate_a_w, v_gate_a_w, gate)]
    small_out, loss = _adamw_small(me.astype(jnp.int32).reshape(1), vec_all, gx_all, ga_all,
                                   [tuple(view(a) for a in (w, m, v)) for _, w, m, v, view in groups])
    for (nm, w, _, _, _), out in zip(groups, small_out):
        res[nm] = [o.reshape(w.shape) for o in out]
    loss = loss.reshape(())

    order = ["norm_in", "w_in", "conv_w", "conv_b", "gate_x_w", "gate_x_b", "gate_a_w", "gate_a_b", "lru_lambda",
             "gn_gain", "w_proj_a", "w_proj_b", "w_out", "norm_final"]
    outs = [loss, grad_x]
    for k in range(4):
        outs += [res[nm][k] for nm in order]
    return tuple(outs)
```

```python
import numpy as np

import jax
import jax.numpy as jnp
from jax import lax
from jax.experimental import pallas as pl
from jax.experimental.pallas import tpu as pltpu

F32 = jnp.float32
BF16 = jnp.bfloat16
MESH = pl.DeviceIdType.MESH

D = 1024
S = 2048
NSEG = 8
NDEV = 8
HEADS = 4
DK = 256
CH = 256
NCH = S // CH
CB = 256
NCB = D // CB
RC = 512
RC_CONV = 128
SCAN_GROUP = 16
EPS = 1e-6
LRU_C = 8.0
VMEM_LIMIT = 56 * 1024 * 1024

ADAM_LR = 0.001
ADAM_B1 = 0.9
ADAM_B2 = 0.999
ADAM_EPS = 1e-08
ADAM_WD = 0.01
ADAM_STEP = 10


def _params(sem=None):
    return pltpu.CompilerParams(dimension_semantics=sem, vmem_limit_bytes=VMEM_LIMIT)


def _dot(a, b):
    return jnp.dot(a, b, preferred_element_type=F32)


def _dot_nt(a, b):
    return lax.dot_general(a, b, (((1,), (1,)), ((), ())), preferred_element_type=F32)


def _dot_tn(a, b):
    return lax.dot_general(a, b, (((0,), (0,)), ((), ())), preferred_element_type=F32)


def _sigmoid(x):
    return jax.nn.sigmoid(x)


def _expm1_nonpos(x):
    poly = x * (1.0 + x * (0.5 + x * (1.0 / 6.0 + x * (1.0 / 24.0))))
    return jnp.where(x > -0.05, poly, jnp.exp(x) - 1.0)


def _softplus(x):
    return jnp.maximum(x, 0.0) + jnp.log(1.0 + jnp.exp(-jnp.abs(x)))


def _rows(c, n):
    return pl.ds(pl.multiple_of(c * n, n), n)


def _window_before(ref, c, n):
    r0 = c * n
    if ref.dtype == BF16:
        prev = ref[pl.ds(pl.multiple_of(jnp.maximum(r0 - 16, 0), 16), 16), :].astype(F32)[8:, :]
    else:
        prev = ref[pl.ds(pl.multiple_of(jnp.maximum(r0 - 8, 0), 8), 8), :]
    prev = jnp.where(c > 0, prev, 0.0)
    return jnp.concatenate([prev, ref[_rows(c, n), :].astype(F32)], axis=0)


def _shift_down(win, s, n):
    if s == 0:
        return win[8:, :]
    return pltpu.roll(win, s, 0)[8:, :]


def _shift_up(win, s, n):
    if s == 0:
        return win[:n, :]
    return pltpu.roll(win, n + 8 - s, 0)[:n, :]


HALF = D // 2
GATHER_SLOTS = [("own", None, 0), ("own", None, 1), ("sib", None, 0), ("sib", None, 1)]
for _j, _h in ((0, 0), (1, 0), (0, 1), (1, 1), (2, 0), (2, 1)):
    GATHER_SLOTS += [("ici", _j, _h), ("fwd", _j, _h)]
NSLOT = len(GATHER_SLOTS)


def _gather_order(x, y, c):
    chips = [(1 - x, y), (x, 1 - y), (1 - x, 1 - y)]
    segs, halves = [], []
    for kind, j, h in GATHER_SLOTS:
        if kind == "own":
            seg = 4 * x + 2 * y + c
        elif kind == "sib":
            seg = 4 * x + 2 * y + 1 - c
        else:
            px, py = chips[j]
            seg = 4 * px + 2 * py + (c if kind == "ici" else 1 - c)
        segs.append(seg)
        halves.append(h)
    return jnp.stack(segs).astype(jnp.int32), jnp.asarray(halves, jnp.int32)


def _inproj_gather(x2d, g_in, w_own, tiny_own, order, halves):
    t = x2d.shape[0]
    tm = 2048
    nt = t // tm
    tx = 1024
    nx = tm // tx

    def body(order_ref, half_ref, g_ref, x_hbm, w_own_ref, tiny_own_ref,
             proj_ref, h_hbm, wg_ref, tinyg_ref,
             w_all, h_all, x_s, send_sems, recv_sems, own_sems, out_sems, tiny_send, tiny_recv, tiny_own_sem,
             x_sems, h_sem):
        k, i = pl.program_id(0), pl.program_id(1)
        x, y, c = _place()
        me, sibling = (x, y, c), (x, y, 1 - c)
        mine = 4 * x + 2 * y + c
        chips = [(1 - x, y), (x, 1 - y), (1 - x, 1 - y)]

        def copy(h, n, block, to, own_src=False):
            px, py, pc = block
            dst = w_all.at[4 * px + 2 * py + pc, h]
            return pltpu.make_async_remote_copy(
                src_ref=w_own_ref.at[:, pl.ds(h * HALF, HALF)] if own_src else dst, dst_ref=dst,
                send_sem=send_sems.at[h, n], recv_sem=recv_sems.at[h, n], device_id=to, device_id_type=MESH)

        def tiny_copy(n, block, to, own_src=False):
            px, py, pc = block
            dst = tinyg_ref.at[4 * px + 2 * py + pc]
            return pltpu.make_async_remote_copy(
                src_ref=tiny_own_ref if own_src else dst, dst_ref=dst,
                send_sem=tiny_send.at[n], recv_sem=tiny_recv.at[n], device_id=to, device_id_type=MESH)

        def own_copy(h):
            return pltpu.make_async_copy(w_own_ref.at[:, pl.ds(h * HALF, HALF)], w_all.at[mine, h], own_sems.at[h])

        tiny_mine = pltpu.make_async_copy(tiny_own_ref, tinyg_ref.at[mine], tiny_own_sem)

        def keep_copy(n):
            h = GATHER_SLOTS[n][2]
            return pltpu.make_async_copy(w_all.at[order_ref[n], h], wg_ref.at[order_ref[n], :, pl.ds(h * HALF, HALF)],
                                         out_sems.at[n])

        near = [(0, sibling), (1, (*chips[0], c)), (2, (*chips[1], c))]
        first = [copy(h, n, me, to, True) for h in (0, 1) for n, to in near]
        tiny_first = [tiny_copy(0, me, sibling, True)] + [tiny_copy(1 + j, me, (*chip, c), True) for j, chip in enumerate(chips)]

        def relay(h, j):
            seg = w_all.at[4 * chips[j][0] + 2 * chips[j][1] + c, h]
            return pltpu.make_async_remote_copy(
                src_ref=seg, dst_ref=seg, send_sem=send_sems.at[h, 3], recv_sem=recv_sems.at[h, 3],
                device_id=(*chips[1 - j], c), device_id_type=MESH)

        for n, (kind, j, h) in enumerate(GATHER_SLOTS):
            @pl.when(jnp.logical_and(k == n, i == 0))
            def _():
                if n == 0:
                    own_copy(0).start()
                    own_copy(1).start()
                    tiny_mine.start()
                    for cp in first + tiny_first:
                        cp.start()
                if kind == "own":
                    own_copy(h).wait()
                elif kind == "sib":
                    copy(h, 0, sibling, me).wait_recv()
                elif kind == "ici":
                    copy(h, 1 + j, (*chips[j], c), me).wait_recv()
                    copy(h, 4 + j, (*chips[j], c), sibling).start()
                    if j < 2:
                        @pl.when(c == j)
                        def _():
                            relay(h, j).start()
                else:
                    copy(h, 4 + j, (*chips[j], 1 - c), me).wait_recv()
                keep_copy(n).start()

        rows = pl.ds(pl.multiple_of(i * tm, tm), tm)

        def x_copy(n):
            return pltpu.make_async_copy(x_hbm.at[pl.ds(n * tx, tx), :], x_s.at[n % 2], x_sems.at[n % 2])

        keep_h = pltpu.make_async_copy(h_all, h_hbm, h_sem)

        for step in range(nt):
            @pl.when(jnp.logical_and(k == 0, i == step))
            def _():
                if step == 0:
                    x_copy(0).start()
                for n in range(step * nx, (step + 1) * nx):
                    x_copy(n).wait()
                    if n + 1 < nt * nx:
                        x_copy(n + 1).start()
                    xv = x_s[n % 2]
                    r = lax.rsqrt(jnp.mean(xv * xv, axis=-1, keepdims=True) + EPS)
                    h_all[pl.ds(n * tx, tx), :] = (xv * r * g_ref[...]).astype(BF16)
                if step == nt - 1:
                    keep_h.start()

        proj_ref[...] = _dot(h_all[rows, :], w_all[order_ref[k], half_ref[k]]).astype(BF16)

        @pl.when(jnp.logical_and(k == NSLOT - 1, i == nt - 1))
        def _():
            for j, chip in enumerate(chips):
                tiny_copy(1 + j, (*chip, c), me).wait_recv()
                tiny_copy(4 + j, (*chip, c), sibling).start()
            tiny_copy(0, sibling, me).wait_recv()
            for j, chip in enumerate(chips):
                tiny_copy(4 + j, (*chip, 1 - c), me).wait_recv()
            for cp in first + tiny_first:
                cp.wait_send()
            for j, chip in enumerate(chips):
                tiny_copy(4 + j, (*chip, c), sibling).wait_send()
                for h in (0, 1):
                    copy(h, 4 + j, (*chip, c), sibling).wait_send()
            for h in (0, 1):
                relay(h, 0).wait_send()
            tiny_mine.wait()
            keep_h.wait()
            for n in range(NSLOT):
                keep_copy(n).wait()

    return pl.pallas_call(
        body, name="inproj_gather",
        grid_spec=pltpu.PrefetchScalarGridSpec(
            num_scalar_prefetch=2, grid=(NSLOT, nt),
            in_specs=[pl.BlockSpec((1, D), lambda k, i, order_ref, half_ref: (0, 0)),
                      ANY, ANY, ANY],
            out_specs=[pl.BlockSpec((None, tm, HALF), lambda k, i, order_ref, half_ref: (order_ref[k], i, half_ref[k])),
                       ANY, ANY, ANY],
            scratch_shapes=[pltpu.VMEM((NDEV, 2, D, HALF), BF16), pltpu.VMEM((t, D), BF16), pltpu.VMEM((2, tx, D), F32),
                            pltpu.SemaphoreType.DMA((2, 7)), pltpu.SemaphoreType.DMA((2, 7)),
                            pltpu.SemaphoreType.DMA((2,)), pltpu.SemaphoreType.DMA((NSLOT,)),
                            pltpu.SemaphoreType.DMA((7,)), pltpu.SemaphoreType.DMA((7,)), pltpu.SemaphoreType.DMA,
                            pltpu.SemaphoreType.DMA((2,)), pltpu.SemaphoreType.DMA]),
        out_shape=[jax.ShapeDtypeStruct((NSEG, t, D), BF16), jax.ShapeDtypeStruct((t, D), BF16),
                   jax.ShapeDtypeStruct((NDEV,) + w_own.shape, BF16),
                   jax.ShapeDtypeStruct((NDEV,) + tiny_own.shape, F32)],
        compiler_params=_params(("arbitrary", "arbitrary")),
    )(order, halves, g_in, x2d, w_own, tiny_own)


def _tile_scan(a, u):
    row = lax.broadcasted_iota(jnp.int32, a.shape, 0)
    for d in (1, 2, 4):
        m = row >= d
        a_sh = pltpu.roll(a, d, 0)
        u_sh = pltpu.roll(u, d, 0)
        u = jnp.where(m, a * u_sh + u, u)
        a = jnp.where(m, a * a_sh, a)
    return a, u


def _tile_scan_rev(a, w):
    row = lax.broadcasted_iota(jnp.int32, a.shape, 0)
    for d in (1, 2, 4):
        m = row < 8 - d
        a_sh = pltpu.roll(a, 8 - d, 0)
        w_sh = pltpu.roll(w, 8 - d, 0)
        w = jnp.where(m, a * w_sh + w, w)
        a = jnp.where(m, a * a_sh, a)
    return a, w


def _lru_gates(xa_ref, c, cw_ref, cb_ref, wbd_ref, bx_ref, ba_ref, sp):
    win = _window_before(xa_ref, c, RC)
    xc = cb_ref[...] + cw_ref[3:4, :] * _shift_down(win, 0, RC)
    for s in (1, 2, 3):
        xc = xc + cw_ref[3 - s:4 - s, :] * _shift_down(win, s, RC)
    z = _dot(xc.astype(BF16), wbd_ref[...])
    gi = _sigmoid(z[:, :CB] + bx_ref[...])
    gr = _sigmoid(z[:, CB:] + ba_ref[...])
    log_a = -LRU_C * gr * sp
    return win, xc, gi, gr, log_a


def _lru_fwd(proj, conv_w, conv_b, wbd, bx, ba, lam, nb):
    t = nb * S

    def body(xa_ref, ga_ref, cw_ref, cb_ref, wbd_ref, bx_ref, ba_ref, lam_ref,
             ya_ref, hs_ref, xc_ref, gi_ref, gr_ref, a_s, u_s):
        sp = _softplus(-lam_ref[...])

        def gates(c, carry):
            _, xc, gi, gr, log_a = _lru_gates(xa_ref, c, cw_ref, cb_ref, wbd_ref, bx_ref, ba_ref, sp)
            rows = _rows(c, RC)
            a_s[rows, :] = jnp.exp(log_a)
            u_s[rows, :] = jnp.sqrt(-_expm1_nonpos(2.0 * log_a)) * (gi * xc)
            xc_ref[rows, :] = xc
            gi_ref[rows, :] = gi
            gr_ref[rows, :] = gr
            return carry

        lax.fori_loop(0, S // RC, gates, 0)

        def scan(g, h):
            for k in range(SCAN_GROUP):
                rows = pl.ds(pl.multiple_of(g * (8 * SCAN_GROUP), 8 * SCAN_GROUP) + 8 * k, 8)
                a_cum, u_cum = _tile_scan(a_s[rows, :], u_s[rows, :])
                hs_ref[rows, :] = u_cum + a_cum * h
                h = u_cum[7:8, :] + a_cum[7:8, :] * h
            return h

        lax.fori_loop(0, S // (8 * SCAN_GROUP), scan, jnp.zeros((1, CB), F32))

        def gate_out(c, carry):
            ga = ga_ref[_rows(c, RC), :].astype(F32)
            ya_ref[_rows(c, RC), :] = (ga * _sigmoid(ga) * hs_ref[_rows(c, RC), :]).astype(BF16)
            return carry

        lax.fori_loop(0, S // RC, gate_out, 0)

    vec = pl.BlockSpec((1, CB), lambda b, cb: (0, cb))
    blk = pl.BlockSpec((S, CB), lambda b, cb: (b, cb))
    return pl.pallas_call(
        body, name="lru_fwd", grid=(nb, NCB),
        in_specs=[pl.BlockSpec((None, S, CB), lambda b, cb: (0, b, cb)),
                  pl.BlockSpec((None, S, CB), lambda b, cb: (1, b, cb)),
                  pl.BlockSpec((4, CB), lambda b, cb: (0, cb)),
                  vec,
                  pl.BlockSpec((None, CB, 2 * CB), lambda b, cb: (cb, 0, 0)),
                  vec, vec, vec],
        out_specs=[blk] + [pl.BlockSpec((None, None, S, CB), lambda b, cb: (b, cb, 0, 0))] * 4,
        out_shape=[jax.ShapeDtypeStruct((t, D), BF16)] + [jax.ShapeDtypeStruct((nb, NCB, S, CB), F32)] * 4,
        scratch_shapes=[pltpu.VMEM((S, CB), F32), pltpu.VMEM((S, CB), F32)],
        compiler_params=_params(("arbitrary", "arbitrary")),
    )(proj, proj, conv_w, conv_b, wbd, bx, ba, lam)


def _lru_bwd(proj, hs, xc_f, gi_f, gr_f, dya, conv_w, wbd, lam, nb, give):
    t = nb * S
    ng = len(give)

    def body(xa_ref, ga_ref, hs_ref, xc_s, gi_s, gr_s, dya_ref, cw_ref, wbd_ref, lam_ref, *rest):
        give_refs, rest = rest[:ng], rest[ng:]
        dp_ref, dwbd_ref, vec_ref = rest[:3]
        got_refs, rest = rest[3:3 + ng], rest[3 + ng:]
        a_s, dl_s, dh_s, dxc_s, acc_s, send_sems, recv_sems = rest
        b = pl.program_id(1)
        exchange = _sibling_copies(give_refs, got_refs, send_sems, recv_sems)

        @pl.when(jnp.logical_and(pl.program_id(0) == 0, b == 0))
        def _():
            for cp in exchange:
                cp.start()

        lam_v = lam_ref[...]
        sp = _softplus(-lam_v)
        acc_s[...] = jnp.zeros_like(acc_s)

        @pl.when(b == 0)
        def _():
            dwbd_ref[...] = jnp.zeros_like(dwbd_ref)
            vec_ref[...] = jnp.zeros_like(vec_ref)

        def gates(c, carry):
            rows = _rows(c, RC)
            a_s[rows, :] = jnp.exp(-LRU_C * gr_s[rows, :] * sp)
            ga = ga_ref[rows, :].astype(F32)
            sg = _sigmoid(ga)
            dya_c = dya_ref[rows, :]
            dl_s[rows, :] = dya_c * (ga * sg)
            dp_ref[1, rows, :] = (dya_c * hs_ref[rows, :] * (sg * (1.0 + ga * (1.0 - sg)))).astype(BF16)
            return carry

        lax.fori_loop(0, S // RC, gates, 0)

        def scan(i, g_in):
            base = pl.multiple_of((S // (8 * SCAN_GROUP) - 1 - i) * (8 * SCAN_GROUP), 8 * SCAN_GROUP)
            row = lax.broadcasted_iota(jnp.int32, (8, CB), 0)
            for k in reversed(range(SCAN_GROUP)):
                rows = pl.ds(base + 8 * k, 8)
                a = a_s[rows, :]
                dl = dl_s[rows, :]
                a_cum, g_loc = _tile_scan_rev(a, a * dl)
                g = g_loc + a_cum * g_in
                dh_s[rows, :] = dl + jnp.where(row < 7, pltpu.roll(g, 7, 0), g_in)
                g_in = g_loc[0:1, :] + a_cum[0:1, :] * g_in
            return g_in

        lax.fori_loop(0, S // (8 * SCAN_GROUP), scan, jnp.zeros((1, CB), F32))

        dxc_s[pl.ds(S, 8), :] = jnp.zeros((8, CB), F32)

        def grads(c, carry):
            rows = _rows(c, RC)
            dh = dh_s[rows, :]
            h_prev = _shift_down(_window_before(hs_ref, c, RC), 1, RC)
            xc, gi, gr, a = xc_s[rows, :], gi_s[rows, :], gr_s[rows, :], a_s[rows, :]
            mult = jnp.sqrt(-_expm1_nonpos(-2.0 * LRU_C * gr * sp))
            dmult = dh * gi * xc
            d_log_a = dh * h_prev * a - dmult * (a * a) / mult
            dzi = dh * mult * xc * gi * (1.0 - gi)
            dzr = d_log_a * (-LRU_C * sp) * gr * (1.0 - gr)
            dz = jnp.concatenate([dzi, dzr], axis=1).astype(BF16)
            dxc_s[rows, :] = dh * mult * gi + _dot_nt(dz, wbd_ref[...])
            dwbd_ref[...] += _dot_tn(xc.astype(BF16), dz)
            acc_s[1:2, :] += jnp.sum(dzi, axis=0, keepdims=True)
            acc_s[2:3, :] += jnp.sum(dzr, axis=0, keepdims=True)
            acc_s[3:4, :] += jnp.sum(d_log_a * (-LRU_C * gr), axis=0, keepdims=True)
            return carry

        lax.fori_loop(0, S // RC, grads, 0, unroll=2)

        def conv_bwd(c, carry):
            rows = _rows(c, RC_CONV)
            dwin = dxc_s[pl.ds(pl.multiple_of(c * RC_CONV, RC_CONV), RC_CONV + 8), :]
            dxc = dwin[:RC_CONV, :]
            xwin = _window_before(xa_ref, c, RC_CONV)
            dxa = cw_ref[3:4, :] * dxc
            acc_s[0:1, :] += jnp.sum(dxc, axis=0, keepdims=True)
            acc_s[7:8, :] += jnp.sum(dxc * _shift_down(xwin, 0, RC_CONV), axis=0, keepdims=True)
            for s in (1, 2, 3):
                dxa = dxa + cw_ref[3 - s:4 - s, :] * _shift_up(dwin, s, RC_CONV)
                acc_s[7 - s:8 - s, :] += jnp.sum(dxc * _shift_down(xwin, s, RC_CONV), axis=0, keepdims=True)
            dp_ref[0, rows, :] = dxa.astype(BF16)
            return carry

        lax.fori_loop(0, S // RC_CONV, conv_bwd, 0)

        row = lax.broadcasted_iota(jnp.int32, acc_s.shape, 0)
        vec_ref[...] += jnp.where(row == 3, acc_s[...] * (-_sigmoid(-lam_v)), acc_s[...])

        @pl.when(jnp.logical_and(pl.program_id(0) == NCB - 1, b == nb - 1))
        def _():
            for cp in exchange:
                cp.wait()

    vec = pl.BlockSpec((1, CB), lambda cb, b: (0, cb))
    blk = pl.BlockSpec((S, CB), lambda cb, b: (b, cb))
    own = pl.BlockSpec((None, None, S, CB), lambda cb, b: (b, cb, 0, 0))
    return pl.pallas_call(
        body, name="lru_bwd", grid=(NCB, nb),
        in_specs=[pl.BlockSpec((None, S, CB), lambda cb, b: (0, b, cb)),
                  pl.BlockSpec((None, S, CB), lambda cb, b: (1, b, cb)),
                  own, own, own, own, blk,
                  pl.BlockSpec((4, CB), lambda cb, b: (0, cb)),
                  pl.BlockSpec((None, CB, 2 * CB), lambda cb, b: (cb, 0, 0)),
                  vec] + [ANY] * ng,
        out_specs=[pl.BlockSpec((2, S, CB), lambda cb, b: (0, b, cb)),
                   pl.BlockSpec((None, CB, 2 * CB), lambda cb, b: (cb, 0, 0)),
                   pl.BlockSpec((8, CB), lambda cb, b: (0, cb))] + [ANY] * ng,
        out_shape=[jax.ShapeDtypeStruct((2, t, D), BF16),
                   jax.ShapeDtypeStruct((NCB, CB, 2 * CB), F32),
                   jax.ShapeDtypeStruct((8, D), F32)]
        + [jax.ShapeDtypeStruct((4,) + g.shape[1:], g.dtype) for g in give],
        scratch_shapes=[pltpu.VMEM((S, CB), F32), pltpu.VMEM((S, CB), F32), pltpu.VMEM((S, CB), F32),
                        pltpu.VMEM((S + 8, CB), F32), pltpu.VMEM((8, CB), F32),
                        pltpu.SemaphoreType.DMA((ng, 4)), pltpu.SemaphoreType.DMA((ng, 4))],
        compiler_params=_params(("arbitrary", "arbitrary")),
    )(proj, proj, hs, xc_f, gi_f, gr_f, dya, conv_w, wbd, lam, *give)


def _retention_tables():
    f32 = np.float32
    log_g = np.log1p(-(f32(2.0) ** (f32(-5.0) - np.arange(HEADS, dtype=f32)))).astype(f32)
    idx = np.arange(CH, dtype=f32)
    diff = idx[:, None] - idx[None, :]
    inner = np.where(diff >= 0, np.exp(np.maximum(diff, f32(0.0))[None] * log_g[:, None, None]), f32(0.0)).astype(f32)
    cross = np.exp((idx[None, :] + f32(1.0)) * log_g[:, None]).astype(f32)
    state = np.exp((f32(CH - 1.0) - idx[None, :]) * log_g[:, None]).astype(f32)
    cross = np.ascontiguousarray(np.broadcast_to(cross[:, :, None], (HEADS, CH, DK)))
    state = np.ascontiguousarray(np.broadcast_to(state[:, :, None], (HEADS, CH, DK)))
    half = DK // 2
    freqs = (f32(10000.0) ** (-np.arange(half, dtype=f32) / f32(half))).astype(f32)
    ang = (np.arange(S, dtype=f32)[:, None] * freqs[None, :]).astype(f32)
    return tuple(jnp.asarray(a) for a in (inner, cross, state, np.cos(ang).astype(f32), np.sin(ang).astype(f32)))


def _rotate(x, cos, sin):
    half = DK // 2
    x1, x2 = x[:, :half], x[:, half:]
    return jnp.concatenate([x1 * cos - x2 * sin, x1 * sin + x2 * cos], axis=1)


def _rotate_back(d, cos, sin):
    half = DK // 2
    d1, d2 = d[:, :half], d[:, half:]
    return jnp.concatenate([d1 * cos + d2 * sin, d2 * cos - d1 * sin], axis=1)


def _ret_fwd(proj, gain, tables, nb, wp_own):
    t = nb * S
    inner_t, cross_t, state_t, cos_t, sin_t = tables

    def body(q_ref, k_ref, v_ref, gb_ref, gain_ref, dm_ref, cd_ref, sd_ref, cos_ref, sin_ref, wp_ref,
             yb_ref, qr_ref, kr_ref, o_ref, rs_ref, wpg_ref, r_s, send_sems, recv_sems, own_sems):
        b, hd = pl.program_id(0), pl.program_id(1)
        own, first, arrive, forward, others = _gather_copies([wp_ref], [wpg_ref], send_sems, recv_sems, own_sems)

        @pl.when(jnp.logical_and(b == 0, hd == 0))
        def _():
            for cp in own + first:
                cp.start()

        @pl.when(jnp.logical_and(b == nb - 1, hd == HEADS - 1))
        def _():
            for came, on in zip(arrive, forward):
                came.wait_recv()
                on.start()

        r_s[...] = jnp.zeros_like(r_s)
        chunk_decay = cd_ref[CH - 1:CH, :]

        def chunk(c, carry):
            rows = _rows(c, CH)
            cos, sin = cos_ref[rows, :], sin_ref[rows, :]
            qr = _rotate(q_ref[rows, :].astype(F32), cos, sin).astype(BF16)
            kr = (_rotate(k_ref[rows, :].astype(F32), cos, sin) * (DK ** -0.5)).astype(BF16)
            vb = v_ref[rows, :]
            v = vb.astype(F32)
            qr_ref[rows, :] = qr
            kr_ref[rows, :] = kr
            r = r_s[...]
            rb = r.astype(BF16)
            rs_ref[c] = rb
            p = (_dot_nt(qr, kr) * dm_ref[...]).astype(BF16)
            o = _dot(p, vb) + _dot(qr, rb) * cd_ref[...]
            r_s[...] = chunk_decay * r + _dot_tn(kr, (v * sd_ref[...]).astype(BF16))
            o_ref[rows, :] = o
            oc = o - jnp.mean(o, axis=-1, keepdims=True)
            rstd = lax.rsqrt(jnp.mean(oc * oc, axis=-1, keepdims=True) + EPS)
            gb = gb_ref[rows, :].astype(F32)
            yb_ref[rows, :] = (gb * _sigmoid(gb) * (oc * rstd * gain_ref[...])).astype(BF16)
            return carry

        lax.fori_loop(0, NCH, chunk, 0, unroll=2)

        @pl.when(jnp.logical_and(b == nb - 1, hd == HEADS - 1))
        def _():
            for cp in others:
                cp.wait_recv()
            for cp in first + forward:
                cp.wait_send()
            for cp in own:
                cp.wait()

    seg = lambda s: pl.BlockSpec((None, S, DK), lambda b, h: (s, b, h))
    tab = pl.BlockSpec((None, CH, DK), lambda b, h: (h, 0, 0))
    rot = pl.BlockSpec((S, DK // 2), lambda b, h: (0, 0))
    blk = pl.BlockSpec((S, DK), lambda b, h: (b, h))
    return pl.pallas_call(
        body, name="ret_fwd", grid=(nb, HEADS),
        in_specs=[seg(2), seg(3), seg(4), seg(5),
                  pl.BlockSpec((None, 1, DK), lambda b, h: (h, 0, 0)),
                  tab, tab, tab, rot, rot, ANY],
        out_specs=[blk, blk, blk, blk,
                   pl.BlockSpec((None, None, NCH, DK, DK), lambda b, h: (b, h, 0, 0, 0)), ANY],
        out_shape=[jax.ShapeDtypeStruct((t, D), BF16), jax.ShapeDtypeStruct((t, D), BF16),
                   jax.ShapeDtypeStruct((t, D), BF16), jax.ShapeDtypeStruct((t, D), F32),
                   jax.ShapeDtypeStruct((nb, HEADS, NCH, DK, DK), BF16),
                   jax.ShapeDtypeStruct((NDEV,) + wp_own.shape, wp_own.dtype)],
        scratch_shapes=[pltpu.VMEM((DK, DK), F32),
                        pltpu.SemaphoreType.DMA((1, 7)), pltpu.SemaphoreType.DMA((1, 7)), pltpu.SemaphoreType.DMA((1,))],
        compiler_params=_params(("arbitrary", "arbitrary")),
    )(proj, proj, proj, proj, gain, inner_t, cross_t, state_t, cos_t, sin_t, wp_own)


def _ret_bwd(proj, qr, kr, o, rs, dyb, gain, tables, nb, sums):
    t = nb * S
    ns = len(sums)
    inner_t, cross_t, state_t, cos_t, sin_t = tables

    def body(qr_ref, kr_ref, v_ref, gb_ref, o_ref, dyb_ref, rs_ref, gain_ref, dm_ref, cd_ref, sd_ref,
             cos_ref, sin_ref, *rest):
        sum_refs, rest = rest[:ns], rest[ns:]
        dp_ref, dgain_ref = rest[:2]
        part_refs, rest = rest[2:2 + ns], rest[2 + ns:]
        dr_s, send_sems, recv_sems, local_sems = rest
        mine, sends, recvs = _chip_copies(sum_refs, part_refs, send_sems, recv_sems, local_sems)

        @pl.when(jnp.logical_and(pl.program_id(0) == 0, pl.program_id(1) == 0))
        def _():
            for cp in mine + sends:
                cp.start()

        dr_s[...] = jnp.zeros_like(dr_s)
        chunk_decay = cd_ref[CH - 1:CH, :]

        @pl.when(pl.program_id(1) == 0)
        def _():
            dgain_ref[...] = jnp.zeros_like(dgain_ref)

        def chunk(i, carry):
            c = NCH - 1 - i
            rows = _rows(c, CH)
            gain_v = gain_ref[...]
            o_c = o_ref[rows, :]
            oc = o_c - jnp.mean(o_c, axis=-1, keepdims=True)
            rstd = lax.rsqrt(jnp.mean(oc * oc, axis=-1, keepdims=True) + EPS)
            yn = oc * rstd
            gb = gb_ref[rows, :].astype(F32)
            sg = _sigmoid(gb)
            dyb_c = dyb_ref[rows, :]
            dgn = dyb_c * (gb * sg)
            dp_ref[3, rows, :] = (dyb_c * (yn * gain_v) * (sg * (1.0 + gb * (1.0 - sg)))).astype(BF16)
            dgain_ref[...] += jnp.sum(dgn * yn, axis=0, keepdims=True)
            dyn = dgn * gain_v
            do = rstd * (dyn - jnp.mean(dyn, axis=-1, keepdims=True)
                         - yn * jnp.mean(dyn * yn, axis=-1, keepdims=True))
            dob = do.astype(BF16)
            dox = (do * cd_ref[...]).astype(BF16)

            q_c, k_c = qr_ref[rows, :], kr_ref[rows, :]
            vb = v_ref[rows, :]
            v = vb.astype(F32)
            vs = (v * sd_ref[...]).astype(BF16)
            rb = rs_ref[c]
            d_r = dr_s[...]
            drb = d_r.astype(BF16)
            dm = dm_ref[...]
            p = (_dot_nt(q_c, k_c) * dm).astype(BF16)
            dpm = (_dot_nt(dob, vb) * dm).astype(BF16)
            dq = _dot(dpm, k_c) + _dot_nt(dox, rb)
            dk = _dot_tn(dpm, q_c) + _dot_nt(vs, drb)
            dv = _dot_tn(p, dob) + _dot(k_c, drb) * sd_ref[...]
            dr_s[...] = chunk_decay * d_r + _dot_tn(q_c, dox)

            cos, sin = cos_ref[rows, :], sin_ref[rows, :]
            dp_ref[0, rows, :] = _rotate_back(dq, cos, sin).astype(BF16)
            dp_ref[1, rows, :] = (_rotate_back(dk, cos, sin) * (DK ** -0.5)).astype(BF16)
            dp_ref[2, rows, :] = dv.astype(BF16)
            return carry

        lax.fori_loop(0, NCH, chunk, 0, unroll=2)

        @pl.when(jnp.logical_and(pl.program_id(0) == HEADS - 1, pl.program_id(1) == nb - 1))
        def _():
            for cp in recvs:
                cp.wait_recv()
            for cp in sends:
                cp.wait_send()
            for cp in mine:
                cp.wait()

    seg = lambda s: pl.BlockSpec((None, S, DK), lambda h, b: (s, b, h))
    tab = pl.BlockSpec((None, CH, DK), lambda h, b: (h, 0, 0))
    rot = pl.BlockSpec((S, DK // 2), lambda h, b: (0, 0))
    blk = pl.BlockSpec((S, DK), lambda h, b: (b, h))
    one = pl.BlockSpec((None, 1, DK), lambda h, b: (h, 0, 0))
    return pl.pallas_call(
        body, name="ret_bwd", grid=(HEADS, nb),
        in_specs=[blk, blk, seg(4), seg(5), blk, blk,
                  pl.BlockSpec((None, None, NCH, DK, DK), lambda h, b: (b, h, 0, 0, 0)),
                  one, tab, tab, tab, rot, rot] + [ANY] * ns,
        out_specs=[pl.BlockSpec((4, S, DK), lambda h, b: (0, b, h)), one] + [ANY] * ns,
        out_shape=[jax.ShapeDtypeStruct((4, t, D), BF16), jax.ShapeDtypeStruct((HEADS, 1, DK), F32)]
        + [jax.ShapeDtypeStruct(a.shape, a.dtype) for a in sums],
        scratch_shapes=[pltpu.VMEM((DK, DK), F32), pltpu.SemaphoreType.DMA((ns, 3)), pltpu.SemaphoreType.DMA((ns, 3)),
                        pltpu.SemaphoreType.DMA((ns,))],
        compiler_params=_params(("arbitrary", "arbitrary")),
    )(qr, kr, proj, proj, o, dyb, rs, gain, inner_t, cross_t, state_t, cos_t, sin_t, *sums)


def _wblock(k):
    return pl.BlockSpec((NDEV, D // NDEV, D), lambda i: (0, k, 0))


def _tail(ya, yb, proj, x2d, tgt, wg, g_fin):
    t = x2d.shape[0]
    tm = 256

    def body(ya_ref, yb_ref, ma_ref, mb_ref, x_ref, t_ref, wa_ref, wb_ref, wo_ref, g_ref,
             dx2_ref, dya_ref, dyb_ref, dm_ref, mg_ref, doa_ref, dob_ref, gfin_ref, loss_ref):
        i = pl.program_id(0)

        @pl.when(i == 0)
        def _():
            gfin_ref[...] = jnp.zeros_like(gfin_ref)
            loss_ref[...] = jnp.zeros_like(loss_ref)

        wa = wa_ref[...].reshape(D, D)
        wb = wb_ref[...].reshape(D, D)
        wo = wo_ref[...].reshape(D, D)
        out_a = _dot(ya_ref[...], wa)
        out_b = _dot(yb_ref[...], wb)
        sa = _sigmoid(ma_ref[...].astype(F32))
        sb = _sigmoid(mb_ref[...].astype(F32))
        merged = (sa * out_a + sb * out_b).astype(BF16)
        mg_ref[...] = merged
        x2 = x_ref[...] + _dot(merged, wo)
        r2 = lax.rsqrt(jnp.mean(x2 * x2, axis=-1, keepdims=True) + EPS)
        xh = x2 * r2
        g = g_ref[...]
        err = xh * g - t_ref[...]
        loss_ref[...] += jnp.sum(err * err, axis=0, keepdims=True) * (0.5 / D)
        dy = err * (1.0 / D)
        gfin_ref[...] += jnp.sum(dy * xh, axis=0, keepdims=True)
        dxh = dy * g
        dx2 = r2 * (dxh - xh * jnp.mean(dxh * xh, axis=-1, keepdims=True))
        dx2_ref[...] = dx2
        dmerged = _dot_nt(dx2.astype(BF16), wo)
        doa = (sa * dmerged).astype(BF16)
        dob = (sb * dmerged).astype(BF16)
        doa_ref[...] = doa
        dob_ref[...] = dob
        dm_ref[0] = (dmerged * out_a * sa * (1.0 - sa)).astype(BF16)
        dm_ref[1] = (dmerged * out_b * sb * (1.0 - sb)).astype(BF16)
        dya_ref[...] = _dot_nt(doa, wa)
        dyb_ref[...] = _dot_nt(dob, wb)

    row = lambda: pl.BlockSpec((tm, D), lambda i: (i, 0))
    seg = lambda s: pl.BlockSpec((None, tm, D), lambda i: (s, i, 0))
    vec = pl.BlockSpec((1, D), lambda i: (0, 0))
    return pl.pallas_call(
        body, name="tail", grid=(t // tm,),
        in_specs=[row(), row(), seg(6), seg(7), row(), row(), _wblock(0), _wblock(1), _wblock(2), vec],
        out_specs=[row(), row(), row(), pl.BlockSpec((2, tm, D), lambda i: (0, i, 0)),
                   row(), row(), row(), vec, vec],
        out_shape=[jax.ShapeDtypeStruct((t, D), F32), jax.ShapeDtypeStruct((t, D), F32),
                   jax.ShapeDtypeStruct((t, D), F32), jax.ShapeDtypeStruct((2, t, D), BF16),
                   jax.ShapeDtypeStruct((t, D), BF16), jax.ShapeDtypeStruct((t, D), BF16),
                   jax.ShapeDtypeStruct((t, D), BF16), jax.ShapeDtypeStruct((1, D), F32),
                   jax.ShapeDtypeStruct((1, D), F32)],
        compiler_params=_params(("arbitrary",)),
    )(ya, yb, proj, proj, x2d, tgt, wg, wg, wg, g_fin)


def _tail_wgrad(ya, yb, merged, doa, dob, dx2):
    t = ya.shape[0]
    tm = 512

    def body(ya_ref, yb_ref, mg_ref, doa_ref, dob_ref, dx2_ref, ga_ref, gb_ref, go_ref):
        @pl.when(pl.program_id(0) == 0)
        def _():
            ga_ref[...] = jnp.zeros_like(ga_ref)
            gb_ref[...] = jnp.zeros_like(gb_ref)
            go_ref[...] = jnp.zeros_like(go_ref)

        ga_ref[...] += _dot_tn(ya_ref[...], doa_ref[...])
        gb_ref[...] += _dot_tn(yb_ref[...], dob_ref[...])
        go_ref[...] += _dot_tn(mg_ref[...], dx2_ref[...].astype(BF16))

    row = lambda: pl.BlockSpec((tm, D), lambda i: (i, 0))
    full = lambda: pl.BlockSpec((D, D), lambda i: (0, 0))
    return pl.pallas_call(
        body, name="tail_wgrad", grid=(t // tm,),
        in_specs=[row() for _ in range(6)], out_specs=[full(), full(), full()],
        out_shape=[jax.ShapeDtypeStruct((D, D), F32)] * 3,
        compiler_params=_params(("arbitrary",)),
    )(ya, yb, merged, doa, dob, dx2)


def _dproj_specs_ordered(tm):
    def spec(lo, n):
        def index(k, i, order_ref):
            seg = order_ref[k]
            mine = jnp.logical_and(seg >= lo, seg < lo + n)
            return jnp.where(mine, seg - lo, 0), jnp.where(mine, i, 0), 0
        return pl.BlockSpec((None, tm, D), index)
    return [spec(0, 2), spec(2, 4), spec(6, 2)]


def _dproj_pick(j, da_ref, db_ref, dc_ref, use):
    @pl.when(j < 2)
    def _():
        use(da_ref[...])

    @pl.when(jnp.logical_and(j >= 2, j < 6))
    def _():
        use(db_ref[...])

    @pl.when(j >= 6)
    def _():
        use(dc_ref[...])


RS_X, RS_Y, RS_XY = 0, 1, 2
RS_ROLES = ((RS_XY, RS_X, RS_Y), (RS_Y, RS_XY, RS_X))


def _rs_flip(rel, x, y):
    return ((1 - x, y), (x, 1 - y), (1 - x, 1 - y))[rel]


def _rs_order(x, y, c):
    order = []
    for s in range(4):
        chip = []
        for core in (0, 1):
            px, py = _rs_flip(RS_ROLES[core][s], x, y) if s < 3 else (x, y)
            chip.append(2 * px + py)
        keep = jnp.where(c == 0, chip[0], chip[1])
        give = jnp.where(c == 0, chip[1], chip[0])
        order += [2 * give + 1 - c, 2 * keep + c]
    return jnp.stack(order).astype(jnp.int32)


def _inproj_wgrad_rs(h, dpa, dpb, dpc, order, smalls):
    t = h.shape[0]
    tm = 1024
    nt = t // tm
    nsm = len(smalls)

    def body(order_ref, h_ref, da_ref, db_ref, dc_ref, *rest):
        small_refs, parts_ref, rest = rest[:nsm], rest[nsm], rest[nsm + 1:]
        all_refs, rest = rest[:nsm], rest[nsm:]
        (acc, sib, outb, far, give_send, give_recv, sum_send, sum_recv, far_send, far_recv, own_sem,
         small_send, small_recv, small_own) = rest
        k, i = pl.program_id(0), pl.program_id(1)
        x, y, c = _place()
        own, first, arrive, forward, others = _gather_copies(small_refs, all_refs, small_send, small_recv, small_own)

        @pl.when(jnp.logical_and(k == 0, i == 0))
        def _():
            for cp in own + first:
                cp.start()

        @pl.when(jnp.logical_and(k == 2, i == 0))
        def _():
            for came, on in zip(arrive, forward):
                came.wait_recv()
                on.start()

        def use(d):
            @pl.when(i == 0)
            def _():
                acc[k % 2] = _dot_tn(h_ref[...], d)

            @pl.when(i > 0)
            def _():
                acc[k % 2] += _dot_tn(h_ref[...], d)

        _dproj_pick(order_ref[k], da_ref, db_ref, dc_ref, use)

        def give_copy(s):
            return pltpu.make_async_remote_copy(
                src_ref=acc.at[0], dst_ref=sib.at[s % 2], send_sem=give_send.at[s], recv_sem=give_recv.at[s],
                device_id=(x, y, 1 - c), device_id_type=MESH)

        def sum_copy(s, core):
            slot = 0 if s < 2 else 1
            return pltpu.make_async_remote_copy(
                src_ref=outb.at[s], dst_ref=parts_ref.at[slot], send_sem=sum_send.at[slot], recv_sem=sum_recv.at[slot],
                device_id=(*_rs_flip(RS_ROLES[core][s], x, y), core), device_id_type=MESH)

        def far_copy(s, core):
            return pltpu.make_async_remote_copy(
                src_ref=outb.at[s], dst_ref=far, send_sem=far_send, recv_sem=far_recv,
                device_id=(*_rs_flip(RS_X if core == 0 else RS_Y, x, y), core), device_id_type=MESH)

        own_copy = pltpu.make_async_copy(outb.at[3], parts_ref.at[2], own_sem)

        def send_of(core, s):
            return far_copy(s, core) if RS_ROLES[core][s] == RS_XY else sum_copy(s, core)

        for s in range(4):
            @pl.when(jnp.logical_and(k == 2 * s, i == nt - 1))
            def _():
                give_copy(s).start()

            @pl.when(jnp.logical_and(k == 2 * s + 1, i == nt - 1))
            def _():
                give_copy(s).wait_recv()
                if s == 2:
                    far_copy(s, 0).wait_recv()
                    outb[s] = (acc[1] + sib[s % 2] + far[...].astype(F32)).astype(BF16)
                else:
                    outb[s] = (acc[1] + sib[s % 2]).astype(BF16)
                give_copy(s).wait_send()
                if s < 3:
                    for core in (0, 1):
                        @pl.when(c == core)
                        def _():
                            send_of(core, s).start()
                else:
                    own_copy.start()

        @pl.when(jnp.logical_and(k == NSEG - 1, i == nt - 1))
        def _():
            for slot in (0, 1):
                sum_copy(2 * slot, 0).wait_recv()
            for s in range(3):
                send_of(0, s).wait_send()
            own_copy.wait()
            for cp in others:
                cp.wait_recv()
            for cp in first + forward:
                cp.wait_send()
            for cp in own:
                cp.wait()

    return pl.pallas_call(
        body, name="inproj_wgrad_rs",
        grid_spec=pltpu.PrefetchScalarGridSpec(
            num_scalar_prefetch=1, grid=(NSEG, nt),
            in_specs=[pl.BlockSpec((tm, D), lambda k, i, order_ref: (i, 0))] + _dproj_specs_ordered(tm) + [ANY] * nsm,
            out_specs=[ANY] * (1 + nsm),
            scratch_shapes=[pltpu.VMEM((2, D, D), F32), pltpu.VMEM((2, D, D), F32), pltpu.VMEM((4, D, D), BF16),
                            pltpu.VMEM((D, D), BF16),
                            pltpu.SemaphoreType.DMA((4,)), pltpu.SemaphoreType.DMA((4,)),
                            pltpu.SemaphoreType.DMA((2,)), pltpu.SemaphoreType.DMA((2,)),
                            pltpu.SemaphoreType.DMA, pltpu.SemaphoreType.DMA, pltpu.SemaphoreType.DMA,
                            pltpu.SemaphoreType.DMA((nsm, 7)), pltpu.SemaphoreType.DMA((nsm, 7)),
                            pltpu.SemaphoreType.DMA((nsm,))]),
        out_shape=[jax.ShapeDtypeStruct((3, D, D), BF16)]
        + [jax.ShapeDtypeStruct((NDEV,) + a.shape, a.dtype) for a in smalls],
        compiler_params=_params(("arbitrary", "arbitrary")),
    )(order, h, dpa, dpb, dpc, *smalls)


def _inproj_dgrad(dpa, dpb, dpc, wg, x2d, dx2, g_in):
    t = x2d.shape[0]
    tm = 512

    def body(da_ref, db_ref, dc_ref, w_hbm, x_ref, dx2_ref, g_ref, gx_ref, gg_ref, w_s, w_sem):
        i = pl.program_id(0)

        @pl.when(i == 0)
        def _():
            gg_ref[...] = jnp.zeros_like(gg_ref)
            load = pltpu.make_async_copy(w_hbm, w_s, w_sem)
            load.start()
            load.wait()

        dh = None
        for ref, lo in ((da_ref, 0), (db_ref, 2), (dc_ref, 6)):
            for k in range(ref.shape[0]):
                part = _dot_nt(ref[k], w_s[lo + k])
                dh = part if dh is None else dh + part
        x = x_ref[...]
        r = lax.rsqrt(jnp.mean(x * x, axis=-1, keepdims=True) + EPS)
        xh = x * r
        gg_ref[...] += jnp.sum(dh * xh, axis=0, keepdims=True)
        dxh = dh * g_ref[...]
        gx_ref[...] = dx2_ref[...] + r * (dxh - xh * jnp.mean(dxh * xh, axis=-1, keepdims=True))

    row = lambda: pl.BlockSpec((tm, D), lambda i: (i, 0))
    seg = lambda n: pl.BlockSpec((n, tm, D), lambda i: (0, i, 0))
    vec = pl.BlockSpec((1, D), lambda i: (0, 0))
    return pl.pallas_call(
        body, name="inproj_dgrad", grid=(t // tm,),
        in_specs=[seg(2), seg(4), seg(2), ANY, row(), row(), vec],
        out_specs=[row(), vec],
        out_shape=[jax.ShapeDtypeStruct((t, D), F32), jax.ShapeDtypeStruct((1, D), F32)],
        scratch_shapes=[pltpu.VMEM((NSEG, D, D), BF16), pltpu.SemaphoreType.DMA],
        compiler_params=_params(("arbitrary",)),
    )(dpa, dpb, dpc, wg, x2d, dx2, g_in)


def _adam_update(g, w, m, v):
    m_new = ADAM_B1 * m + (1.0 - ADAM_B1) * g
    v_new = ADAM_B2 * v + (1.0 - ADAM_B2) * (g * g)
    m_hat = m_new / (1.0 - ADAM_B1 ** ADAM_STEP)
    v_hat = v_new / (1.0 - ADAM_B2 ** ADAM_STEP)
    return -ADAM_LR * (m_hat / (jnp.sqrt(v_hat) + ADAM_EPS) + ADAM_WD * w), m_new, v_new


def _sum_in_order(ref):
    total = ref[0].astype(F32)
    for k in range(1, ref.shape[0]):
        total = total + ref[k].astype(F32)
    return total


def _sum_devices(arrs):
    def body(*refs):
        for a in range(len(arrs)):
            refs[len(arrs) + a][...] = _sum_in_order(refs[a])

    return pl.pallas_call(
        body, name="sum_devices",
        out_shape=[jax.ShapeDtypeStruct(a.shape[1:], F32) for a in arrs],
        compiler_params=_params(),
    )(*arrs)


def _adamw_small(me, vec_all, gx_all, ga_all, groups):
    flat = [a for grp in groups for a in grp]
    ng = len(groups)
    nshard = D // NDEV

    def body(me_ref, vec_ref, shard_ref, gx_ref, ga_ref, *refs):
        ins, outs = refs[:3 * ng], refs[3 * ng:]
        vec = _sum_in_order(vec_ref)
        shard = _sum_in_order(shard_ref)
        grads = [vec[r:r + 1, :] for r in range(6)]
        grads += [shard[0:4, :], shard[4:8, 0:DK // NDEV], _sum_in_order(gx_ref), _sum_in_order(ga_ref)]
        for n, g in enumerate(grads):
            delta, m_new, v_new = _adam_update(g, ins[3 * n][...], ins[3 * n + 1][...], ins[3 * n + 2][...])
            outs[4 * n][...] = g
            outs[4 * n + 1][...] = delta
            outs[4 * n + 2][...] = m_new
            outs[4 * n + 3][...] = v_new
        outs[4 * ng][...] = jnp.sum(vec[6:7, :], axis=1, keepdims=True)

    full = lambda a: pl.BlockSpec(a.shape, lambda i, me_ref, nd=len(a.shape): (0,) * nd)
    out_shape = [jax.ShapeDtypeStruct(w.shape, F32) for w, _, _ in groups for _ in range(4)]
    out_shape.append(jax.ShapeDtypeStruct((1, 1), F32))
    outs = pl.pallas_call(
        body, name="adamw_small",
        grid_spec=pltpu.PrefetchScalarGridSpec(
            num_scalar_prefetch=1, grid=(1,),
            in_specs=[full(vec_all),
                      pl.BlockSpec((NDEV, 8, nshard), lambda i, me_ref: (0, 1, me_ref[0])),
                      full(gx_all), full(ga_all)] + [full(a) for a in flat],
            out_specs=[full(s) for s in out_shape]),
        out_shape=out_shape,
        compiler_params=_params(("arbitrary",)),
    )(me, vec_all, vec_all, gx_all, ga_all, *flat)
    return [outs[4 * n:4 * n + 4] for n in range(ng)], outs[4 * ng]


def _adamw(name, items):
    n, rows, cols = items[0][0].shape
    tr = rows if rows <= 256 else 256
    k = len(items)

    def body(*refs):
        for a in range(k):
            p_ref, w_ref, m_ref, v_ref = refs[4 * a:4 * a + 4]
            g = _sum_in_order(p_ref)
            delta, m_new, v_new = _adam_update(g, w_ref[...], m_ref[...], v_ref[...])
            for o, val in zip(refs[4 * k + 4 * a:4 * k + 4 * a + 4], (g, delta, m_new, v_new)):
                o[...] = val

    blk = lambda: pl.BlockSpec((tr, cols), lambda i: (i, 0))
    outs = pl.pallas_call(
        body, name=name, grid=(rows // tr,),
        in_specs=[pl.BlockSpec((n, tr, cols), lambda i: (0, i, 0)), blk(), blk(), blk()] * k,
        out_specs=[blk() for _ in range(4 * k)],
        out_shape=[jax.ShapeDtypeStruct((rows, cols), F32)] * (4 * k),
        compiler_params=_params(("arbitrary",)),
    )(*[a for item in items for a in item])
    return [outs[4 * a:4 * a + 4] for a in range(k)]


ANY = pl.BlockSpec(memory_space=pl.ANY)


def _place():
    return lax.axis_index("x"), lax.axis_index("y"), lax.axis_index("c")


def _gather_copies(ins, outs, send_sems, recv_sems, own_sems):
    x, y, c = _place()
    me, sibling = (x, y, c), (x, y, 1 - c)
    chips = [(1 - x, y), (x, 1 - y), (1 - x, 1 - y)]
    n = len(ins)

    def copy(a, k, block, to, src=None):
        px, py, pc = block
        dst = outs[a].at[4 * px + 2 * py + pc]
        return pltpu.make_async_remote_copy(
            src_ref=dst if src is None else src, dst_ref=dst,
            send_sem=send_sems.at[a, k], recv_sem=recv_sems.at[a, k], device_id=to, device_id_type=MESH)

    own = [pltpu.make_async_copy(ins[a], outs[a].at[4 * x + 2 * y + c], own_sems.at[a]) for a in range(n)]
    first = []
    for a in range(n):
        first.append(copy(a, 0, me, sibling, src=ins[a]))
        first += [copy(a, 1 + j, me, (*chip, c), src=ins[a]) for j, chip in enumerate(chips)]
    arrive = [copy(a, 1 + j, (*chip, c), me) for j, chip in enumerate(chips) for a in range(n)]
    forward = [copy(a, 4 + j, (*chip, c), sibling) for j, chip in enumerate(chips) for a in range(n)]
    rest = [copy(a, 0, sibling, me) for a in range(n)]
    rest += [copy(a, 4 + j, (*chip, 1 - c), me) for a in range(n) for j, chip in enumerate(chips)]
    return own, first, arrive, forward, rest


def _sibling_copies(ins, outs, send_sems, recv_sems):
    x, y, c = _place()
    return [pltpu.make_async_remote_copy(
        src_ref=ins[a].at[2 * q + 1 - c], dst_ref=outs[a].at[q],
        send_sem=send_sems.at[a, q], recv_sem=recv_sems.at[a, q],
        device_id=(x, y, 1 - c), device_id_type=MESH) for a in range(len(ins)) for q in range(4)]


def _chip_copies(ins, outs, send_sems, recv_sems, local_sems):
    x, y, c = _place()
    my_chip = 2 * x + y
    chips = [(1 - x, y), (x, 1 - y), (1 - x, 1 - y)]
    n = len(ins)
    mine = [pltpu.make_async_copy(ins[a].at[my_chip], outs[a].at[my_chip], local_sems.at[a]) for a in range(n)]
    sends = [pltpu.make_async_remote_copy(
        src_ref=ins[a].at[2 * px + py], dst_ref=outs[a].at[my_chip],
        send_sem=send_sems.at[a, j], recv_sem=recv_sems.at[a, j],
        device_id=(px, py, c), device_id_type=MESH) for a in range(n) for j, (px, py) in enumerate(chips)]
    recvs = [pltpu.make_async_remote_copy(
        src_ref=ins[a].at[my_chip], dst_ref=outs[a].at[2 * px + py],
        send_sem=send_sems.at[a, j], recv_sem=recv_sems.at[a, j],
        device_id=(px, py, c), device_id_type=MESH) for a in range(n) for j, (px, py) in enumerate(chips)]
    return mine, sends, recvs


def _chip_sum(owns, gots, core):
    n = len(owns)
    _, rows, cols = owns[0].shape

    def body(core_ref, *refs):
        for a in range(n):
            refs[2 * n + a][...] = (refs[a][...] + refs[n + a][...]).astype(BF16)

    own_spec = pl.BlockSpec((None, rows, cols), lambda q, core_ref: (2 * q + core_ref[0], 0, 0))
    slab = pl.BlockSpec((None, rows, cols), lambda q, core_ref: (q, 0, 0))
    return pl.pallas_call(
        body, name="chip_sum",
        grid_spec=pltpu.PrefetchScalarGridSpec(
            num_scalar_prefetch=1, grid=(4,),
            in_specs=[own_spec] * n + [slab] * n, out_specs=[slab] * n),
        out_shape=[jax.ShapeDtypeStruct((4, rows, cols), BF16)] * n,
        compiler_params=_params(("arbitrary",)),
    )(core, *owns, *gots)


def _block_diag(w):
    w4 = w.reshape(NCB, 4, 64, 64)
    eye = jnp.eye(4, dtype=w.dtype)
    return (w4[:, :, :, None, :] * eye[None, :, None, :, None]).reshape(NCB, CB, CB)


def _block_diag_back(g):
    g5 = g.reshape(NCB, 4, 64, 4, 64)
    return jnp.stack([g5[:, m, :, m, :] for m in range(4)], axis=1).reshape(16, 64, 64)


def kernel(x, norm_in, w_in, conv_w, conv_b, gate_x_w, gate_x_b, gate_a_w, gate_a_b, lru_lambda, gn_gain, w_proj_a, w_proj_b, w_out, norm_final, loss_target, m_norm_in, m_w_in, m_conv_w, m_conv_b, m_gate_x_w, m_gate_x_b, m_gate_a_w, m_gate_a_b, m_lru_lambda, m_gn_gain, m_w_proj_a, m_w_proj_b, m_w_out, m_norm_final, v_norm_in, v_w_in, v_conv_w, v_conv_b, v_gate_x_w, v_gate_x_b, v_gate_a_w, v_gate_a_b, v_lru_lambda, v_gn_gain, v_w_proj_a, v_w_proj_b, v_w_out, v_norm_final):
    xi, yi, ci = _place()
    me = 4 * xi + 2 * yi + ci
    core = ci.astype(jnp.int32).reshape(1)
    nshard = D // NDEV
    nb = x.shape[0]
    t = nb * S
    x2d = x.reshape(t, D)
    tgt2d = loss_target.reshape(t, D)
    g_final = norm_final.reshape(1, D)
    wbd = jnp.concatenate([_block_diag(gate_x_w[0]), _block_diag(gate_a_w[0])], axis=-1).astype(BF16)
    tables = _retention_tables()

    wp_own = jnp.concatenate([w_proj_a[0], w_proj_b[0], w_out[0]], axis=0).astype(BF16)
    tiny = jnp.concatenate([conv_w[0], jnp.pad(gn_gain[0], ((0, 0), (0, nshard - DK // NDEV)))], axis=0)
    proj, h, wg, tiny_g = _inproj_gather(x2d, norm_in, w_in[0].astype(BF16), tiny, *_gather_order(xi, yi, ci))
    conv_w_full = tiny_g[:, 0:4, :].transpose(1, 0, 2).reshape(4, D)
    gain3 = tiny_g[:, 4:8, :DK // NDEV].transpose(1, 0, 2).reshape(HEADS, 1, DK)

    ya, hs, xc, gi, gr = _lru_fwd(proj, conv_w_full, conv_b, wbd, gate_x_b, gate_a_b, lru_lambda, nb)
    yb, qr, kr, o, rs, wpg = _ret_fwd(proj, gain3, tables, nb, wp_own)
    dx2, dya, dyb, dpc, merged, doa, dob, g_fin, loss_vec = _tail(ya, yb, proj, x2d, tgt2d, wpg, g_final)
    g_pa, g_pb, g_out = _tail_wgrad(ya, yb, merged, doa, dob, dx2)

    own = [g.reshape(NDEV, nshard, D) for g in (g_pa, g_pb, g_out)]
    dpa, g_wbd, g_vec, *got = _lru_bwd(proj, hs, xc, gi, gr, dya, conv_w_full, wbd, lru_lambda, nb, own)
    sums = _chip_sum(own, got, core)
    dpb, g_gain, *parts = _ret_bwd(proj, qr, kr, o, rs, dyb, gain3, tables, nb, sums)

    grad_x, g_norm_in = _inproj_dgrad(dpa, dpb, dpc, wg, x2d, dx2, norm_in)
    grad_x = grad_x.reshape(nb, S, D)

    gain_rows = jnp.pad(g_gain.reshape(HEADS, NDEV, DK // NDEV), ((0, 0), (0, 0), (0, nshard - DK // NDEV)))
    vec = jnp.concatenate([g_norm_in, g_vec[0:4], g_fin, loss_vec, jnp.zeros((1, D), F32), g_vec[4:8],
                           gain_rows.reshape(HEADS, D)], axis=0)
    g_gx = _block_diag_back(g_wbd[:, :, :CB]).reshape(D // 2, 128)
    g_ga = _block_diag_back(g_wbd[:, :, CB:]).reshape(D // 2, 128)
    parts_in, vec_all, gx_all, ga_all = _inproj_wgrad_rs(h, dpa, dpb, dpc, _rs_order(xi, yi, ci),
                                                         [vec, g_gx, g_ga])
    gx_all, ga_all = [g.reshape(1, D, 64) for g in _sum_devices([gx_all, ga_all])]
    parts = [parts_in] + list(parts)

    res = {}
    (out,) = _adamw("adamw_w_in", [(parts[0], w_in[0], m_w_in[0], v_w_in[0])])
    res["w_in"] = [o[None] for o in out]
    square = [("w_proj_a", w_proj_a, m_w_proj_a, v_w_proj_a), ("w_proj_b", w_proj_b, m_w_proj_b, v_w_proj_b),
              ("w_out", w_out, m_w_out, v_w_out)]
    outs = _adamw("adamw_square", [(parts[1 + k], w[0], m[0], v[0]) for k, (_, w, m, v) in enumerate(square)])
    for (nm, _, _, _), out in zip(square, outs):
        res[nm] = [o[None] for o in out]

    row = lambda a: a.reshape(1, D)
    gate = lambda a: a.reshape(D, 64)
    groups = [("norm_in", norm_in, m_norm_in, v_norm_in, row), ("conv_b", conv_b, m_conv_b, v_conv_b, row),
              ("gate_x_b", gate_x_b, m_gate_x_b, v_gate_x_b, row), ("gate_a_b", gate_a_b, m_gate_a_b, v_gate_a_b, row),
              ("lru_lambda", lru_lambda, m_lru_lambda, v_lru_lambda, row),
              ("norm_final", norm_final, m_norm_final, v_norm_final, row),
              ("conv_w", conv_w, m_conv_w, v_conv_w, lambda a: a[0]), ("gn_gain", gn_gain, m_gn_gain, v_gn_gain, lambda a: a[0]),
              ("gate_x_w", gate_x_w, m_gate_x_w, v_gate_x_w, gate), ("gate_a_w", gate_a_w, m_gate_a_w, v_gate_a_w, gate)]
    small_out, loss = _adamw_small(me.astype(jnp.int32).reshape(1), vec_all, gx_all, ga_all,
                                   [tuple(view(a) for a in (w, m, v)) for _, w, m, v, view in groups])
    for (nm, w, _, _, _), out in zip(groups, small_out):
        res[nm] = [o.reshape(w.shape) for o in out]
    loss = loss.reshape(())

    order = ["norm_in", "w_in", "conv_w", "conv_b", "gate_x_w", "gate_x_b", "gate_a_w", "gate_a_b", "lru_lambda",
             "gn_gain", "w_proj_a", "w_proj_b", "w_out", "norm_final"]
    outs = [loss, grad_x]
    for k in range(4):
        outs += [res[nm][k] for nm in order]
    return tuple(outs)
```

```python
import numpy as np

import jax
import jax.numpy as jnp
from jax import lax
from jax.experimental import pallas as pl
from jax.experimental.pallas import tpu as pltpu

F32 = jnp.float32
BF16 = jnp.bfloat16
MESH = pl.DeviceIdType.MESH

D = 1024
S = 2048
NSEG = 8
NDEV = 8
HEADS = 4
DK = 256
CH = 256
NCH = S // CH
CB = 256
NCB = D // CB
RC = 512
RC_CONV = 128
SCAN_GROUP = 16
EPS = 1e-6
LRU_C = 8.0
VMEM_LIMIT = 56 * 1024 * 1024

ADAM_LR = 0.001
ADAM_B1 = 0.9
ADAM_B2 = 0.999
ADAM_EPS = 1e-08
ADAM_WD = 0.01
ADAM_STEP = 10


def _params(sem=None):
    return pltpu.CompilerParams(dimension_semantics=sem, vmem_limit_bytes=VMEM_LIMIT)


def _dot(a, b):
    return jnp.dot(a, b, preferred_element_type=F32)


def _dot_nt(a, b):
    return lax.dot_general(a, b, (((1,), (1,)), ((), ())), preferred_element_type=F32)


def _dot_tn(a, b):
    return lax.dot_general(a, b, (((0,), (0,)), ((), ())), preferred_element_type=F32)


def _sigmoid(x):
    return jax.nn.sigmoid(x)


def _expm1_nonpos(x):
    poly = x * (1.0 + x * (0.5 + x * (1.0 / 6.0 + x * (1.0 / 24.0))))
    return jnp.where(x > -0.05, poly, jnp.exp(x) - 1.0)


def _softplus(x):
    return jnp.maximum(x, 0.0) + jnp.log(1.0 + jnp.exp(-jnp.abs(x)))


def _rows(c, n):
    return pl.ds(pl.multiple_of(c * n, n), n)


def _window_before(ref, c, n):
    r0 = c * n
    if ref.dtype == BF16:
        prev = ref[pl.ds(pl.multiple_of(jnp.maximum(r0 - 16, 0), 16), 16), :].astype(F32)[8:, :]
    else:
        prev = ref[pl.ds(pl.multiple_of(jnp.maximum(r0 - 8, 0), 8), 8), :]
    prev = jnp.where(c > 0, prev, 0.0)
    return jnp.concatenate([prev, ref[_rows(c, n), :].astype(F32)], axis=0)


def _shift_down(win, s, n):
    if s == 0:
        return win[8:, :]
    return pltpu.roll(win, s, 0)[8:, :]


def _shift_up(win, s, n):
    if s == 0:
        return win[:n, :]
    return pltpu.roll(win, n + 8 - s, 0)[:n, :]


HALF = D // 2
GATHER_SLOTS = [("own", None, 0), ("own", None, 1), ("sib", None, 0), ("sib", None, 1)]
for _j, _h in ((0, 0), (1, 0), (0, 1), (1, 1), (2, 0), (2, 1)):
    GATHER_SLOTS += [("ici", _j, _h), ("fwd", _j, _h)]
NSLOT = len(GATHER_SLOTS)


def _gather_order(x, y, c):
    chips = [(1 - x, y), (x, 1 - y), (1 - x, 1 - y)]
    segs, halves = [], []
    for kind, j, h in GATHER_SLOTS:
        if kind == "own":
            seg = 4 * x + 2 * y + c
        elif kind == "sib":
            seg = 4 * x + 2 * y + 1 - c
        else:
            px, py = chips[j]
            seg = 4 * px + 2 * py + (c if kind == "ici" else 1 - c)
        segs.append(seg)
        halves.append(h)
    return jnp.stack(segs).astype(jnp.int32), jnp.asarray(halves, jnp.int32)


def _inproj_gather(x2d, g_in, w_own, tiny_own, order, halves):
    t = x2d.shape[0]
    tm = 2048
    nt = t // tm
    tx = 1024
    nx = tm // tx

    def body(order_ref, half_ref, g_ref, x_hbm, w_own_ref, tiny_own_ref,
             proj_ref, h_hbm, wg_ref, tinyg_ref,
             w_all, h_all, x_s, send_sems, recv_sems, own_sems, out_sems, tiny_send, tiny_recv, tiny_own_sem,
             x_sems, h_sem):
        k, i = pl.program_id(0), pl.program_id(1)
        x, y, c = _place()
        me, sibling = (x, y, c), (x, y, 1 - c)
        mine = 4 * x + 2 * y + c
        chips = [(1 - x, y), (x, 1 - y), (1 - x, 1 - y)]

        def copy(h, n, block, to, own_src=False):
            px, py, pc = block
            dst = w_all.at[4 * px + 2 * py + pc, h]
            return pltpu.make_async_remote_copy(
                src_ref=w_own_ref.at[:, pl.ds(h * HALF, HALF)] if own_src else dst, dst_ref=dst,
                send_sem=send_sems.at[h, n], recv_sem=recv_sems.at[h, n], device_id=to, device_id_type=MESH)

        def tiny_copy(n, block, to, own_src=False):
            px, py, pc = block
            dst = tinyg_ref.at[4 * px + 2 * py + pc]
            return pltpu.make_async_remote_copy(
                src_ref=tiny_own_ref if own_src else dst, dst_ref=dst,
                send_sem=tiny_send.at[n], recv_sem=tiny_recv.at[n], device_id=to, device_id_type=MESH)

        def own_copy(h):
            return pltpu.make_async_copy(w_own_ref.at[:, pl.ds(h * HALF, HALF)], w_all.at[mine, h], own_sems.at[h])

        tiny_mine = pltpu.make_async_copy(tiny_own_ref, tinyg_ref.at[mine], tiny_own_sem)

        def keep_copy(n):
            h = GATHER_SLOTS[n][2]
            return pltpu.make_async_copy(w_all.at[order_ref[n], h], wg_ref.at[order_ref[n], :, pl.ds(h * HALF, HALF)],
                                         out_sems.at[n])

        near = [(0, sibling), (1, (*chips[0], c)), (2, (*chips[1], c))]
        first = [copy(h, n, me, to, True) for h in (0, 1) for n, to in near]
        tiny_first = [tiny_copy(0, me, sibling, True)] + [tiny_copy(1 + j, me, (*chip, c), True) for j, chip in enumerate(chips)]

        def relay(h, j):
            seg = w_all.at[4 * chips[j][0] + 2 * chips[j][1] + c, h]
            return pltpu.make_async_remote_copy(
                src_ref=seg, dst_ref=seg, send_sem=send_sems.at[h, 3], recv_sem=recv_sems.at[h, 3],
                device_id=(*chips[1 - j], c), device_id_type=MESH)

        for n, (kind, j, h) in enumerate(GATHER_SLOTS):
            @pl.when(jnp.logical_and(k == n, i == 0))
            def _():
                if n == 0:
                    own_copy(0).start()
                    own_copy(1).start()
                    tiny_mine.start()
                    for cp in first + tiny_first:
                        cp.start()
                if kind == "own":
                    own_copy(h).wait()
                elif kind == "sib":
                    copy(h, 0, sibling, me).wait_recv()
                elif kind == "ici":
                    copy(h, 1 + j, (*chips[j], c), me).wait_recv()
                    copy(h, 4 + j, (*chips[j], c), sibling).start()
                    if j < 2:
                        @pl.when(c == j)
                        def _():
                            relay(h, j).start()
                else:
                    copy(h, 4 + j, (*chips[j], 1 - c), me).wait_recv()
                keep_copy(n).start()

        rows = pl.ds(pl.multiple_of(i * tm, tm), tm)

        def x_copy(n):
            return pltpu.make_async_copy(x_hbm.at[pl.ds(n * tx, tx), :], x_s.at[n % 2], x_sems.at[n % 2])

        keep_h = pltpu.make_async_copy(h_all, h_hbm, h_sem)

        for step in range(nt):
            @pl.when(jnp.logical_and(k == 0, i == step))
            def _():
                if step == 0:
                    x_copy(0).start()
                for n in range(step * nx, (step + 1) * nx):
                    x_copy(n).wait()
                    if n + 1 < nt * nx:
                        x_copy(n + 1).start()
                    xv = x_s[n % 2]
                    r = lax.rsqrt(jnp.mean(xv * xv, axis=-1, keepdims=True) + EPS)
                    h_all[pl.ds(n * tx, tx), :] = (xv * r * g_ref[...]).astype(BF16)
                if step == nt - 1:
                    keep_h.start()

        proj_ref[...] = _dot(h_all[rows, :], w_all[order_ref[k], half_ref[k]]).astype(BF16)

        @pl.when(jnp.logical_and(k == NSLOT - 1, i == nt - 1))
        def _():
            for j, chip in enumerate(chips):
                tiny_copy(1 + j, (*chip, c), me).wait_recv()
                tiny_copy(4 + j, (*chip, c), sibling).start()
            tiny_copy(0, sibling, me).wait_recv()
            for j, chip in enumerate(chips):
                tiny_copy(4 + j, (*chip, 1 - c), me).wait_recv()
            for cp in first + tiny_first:
                cp.wait_send()
            for j, chip in enumerate(chips):
                tiny_copy(4 + j, (*chip, c), sibling).wait_send()
                for h in (0, 1):
                    copy(h, 4 + j, (*chip, c), sibling).wait_send()
            for h in (0, 1):
                relay(h, 0).wait_send()
            tiny_mine.wait()
            keep_h.wait()
            for n in range(NSLOT):
                keep_copy(n).wait()

    return pl.pallas_call(
        body, name="inproj_gather",
        grid_spec=pltpu.PrefetchScalarGridSpec(
            num_scalar_prefetch=2, grid=(NSLOT, nt),
            in_specs=[pl.BlockSpec((1, D), lambda k, i, order_ref, half_ref: (0, 0)),
                      ANY, ANY, ANY],
            out_specs=[pl.BlockSpec((None, tm, HALF), lambda k, i, order_ref, half_ref: (order_ref[k], i, half_ref[k])),
                       ANY, ANY, ANY],
            scratch_shapes=[pltpu.VMEM((NDEV, 2, D, HALF), BF16), pltpu.VMEM((t, D), BF16), pltpu.VMEM((2, tx, D), F32),
                            pltpu.SemaphoreType.DMA((2, 7)), pltpu.SemaphoreType.DMA((2, 7)),
                            pltpu.SemaphoreType.DMA((2,)), pltpu.SemaphoreType.DMA((NSLOT,)),
                            pltpu.SemaphoreType.DMA((7,)), pltpu.SemaphoreType.DMA((7,)), pltpu.SemaphoreType.DMA,
                            pltpu.SemaphoreType.DMA((2,)), pltpu.SemaphoreType.DMA]),
        out_shape=[jax.ShapeDtypeStruct((NSEG, t, D), BF16), jax.ShapeDtypeStruct((t, D), BF16),
                   jax.ShapeDtypeStruct((NDEV,) + w_own.shape, BF16),
                   jax.ShapeDtypeStruct((NDEV,) + tiny_own.shape, F32)],
        compiler_params=_params(("arbitrary", "arbitrary")),
    )(order, halves, g_in, x2d, w_own, tiny_own)


def _tile_scan(a, u):
    row = lax.broadcasted_iota(jnp.int32, a.shape, 0)
    for d in (1, 2, 4):
        m = row >= d
        a_sh = pltpu.roll(a, d, 0)
        u_sh = pltpu.roll(u, d, 0)
        u = jnp.where(m, a * u_sh + u, u)
        a = jnp.where(m, a * a_sh, a)
    return a, u


def _tile_scan_rev(a, w):
    row = lax.broadcasted_iota(jnp.int32, a.shape, 0)
    for d in (1, 2, 4):
        m = row < 8 - d
        a_sh = pltpu.roll(a, 8 - d, 0)
        w_sh = pltpu.roll(w, 8 - d, 0)
        w = jnp.where(m, a * w_sh + w, w)
        a = jnp.where(m, a * a_sh, a)
    return a, w


def _lru_gates(xa_ref, c, cw_ref, cb_ref, wbd_ref, bx_ref, ba_ref, sp):
    win = _window_before(xa_ref, c, RC)
    xc = cb_ref[...] + cw_ref[3:4, :] * _shift_down(win, 0, RC)
    for s in (1, 2, 3):
        xc = xc + cw_ref[3 - s:4 - s, :] * _shift_down(win, s, RC)
    z = _dot(xc.astype(BF16), wbd_ref[...])
    gi = _sigmoid(z[:, :CB] + bx_ref[...])
    gr = _sigmoid(z[:, CB:] + ba_ref[...])
    log_a = -LRU_C * gr * sp
    return win, xc, gi, gr, log_a


def _lru_fwd(proj, conv_w, conv_b, wbd, bx, ba, lam, nb):
    t = nb * S

    def body(xa_ref, ga_ref, cw_ref, cb_ref, wbd_ref, bx_ref, ba_ref, lam_ref,
             ya_ref, hs_ref, xc_ref, gi_ref, gr_ref, a_s, u_s):
        sp = _softplus(-lam_ref[...])

        def gates(c, carry):
            _, xc, gi, gr, log_a = _lru_gates(xa_ref, c, cw_ref, cb_ref, wbd_ref, bx_ref, ba_ref, sp)
            rows = _rows(c, RC)
            a_s[rows, :] = jnp.exp(log_a)
            u_s[rows, :] = jnp.sqrt(-_expm1_nonpos(2.0 * log_a)) * (gi * xc)
            xc_ref[rows, :] = xc
            gi_ref[rows, :] = gi
            gr_ref[rows, :] = gr
            return carry

        lax.fori_loop(0, S // RC, gates, 0)

        def scan(g, h):
            for k in range(SCAN_GROUP):
                rows = pl.ds(pl.multiple_of(g * (8 * SCAN_GROUP), 8 * SCAN_GROUP) + 8 * k, 8)
                a_cum, u_cum = _tile_scan(a_s[rows, :], u_s[rows, :])
                hs_ref[rows, :] = u_cum + a_cum * h
                h = u_cum[7:8, :] + a_cum[7:8, :] * h
            return h

        lax.fori_loop(0, S // (8 * SCAN_GROUP), scan, jnp.zeros((1, CB), F32))

        def gate_out(c, carry):
            ga = ga_ref[_rows(c, RC), :].astype(F32)
            ya_ref[_rows(c, RC), :] = (ga * _sigmoid(ga) * hs_ref[_rows(c, RC), :]).astype(BF16)
            return carry

        lax.fori_loop(0, S // RC, gate_out, 0)

    vec = pl.BlockSpec((1, CB), lambda b, cb: (0, cb))
    blk = pl.BlockSpec((S, CB), lambda b, cb: (b, cb))
    return pl.pallas_call(
        body, name="lru_fwd", grid=(nb, NCB),
        in_specs=[pl.BlockSpec((None, S, CB), lambda b, cb: (0, b, cb)),
                  pl.BlockSpec((None, S, CB), lambda b, cb: (1, b, cb)),
                  pl.BlockSpec((4, CB), lambda b, cb: (0, cb)),
                  vec,
                  pl.BlockSpec((None, CB, 2 * CB), lambda b, cb: (cb, 0, 0)),
                  vec, vec, vec],
        out_specs=[blk] + [pl.BlockSpec((None, None, S, CB), lambda b, cb: (b, cb, 0, 0))] * 4,
        out_shape=[jax.ShapeDtypeStruct((t, D), BF16)] + [jax.ShapeDtypeStruct((nb, NCB, S, CB), F32)] * 4,
        scratch_shapes=[pltpu.VMEM((S, CB), F32), pltpu.VMEM((S, CB), F32)],
        compiler_params=_params(("arbitrary", "arbitrary")),
    )(proj, proj, conv_w, conv_b, wbd, bx, ba, lam)


def _lru_bwd(proj, hs, xc_f, gi_f, gr_f, dya, conv_w, wbd, lam, nb, give):
    t = nb * S
    ng = len(give)

    def body(xa_ref, ga_ref, hs_ref, xc_s, gi_s, gr_s, dya_ref, cw_ref, wbd_ref, lam_ref, *rest):
        give_refs, rest = rest[:ng], rest[ng:]
        dp_ref, dwbd_ref, vec_ref = rest[:3]
        got_refs, rest = rest[3:3 + ng], rest[3 + ng:]
        a_s, dl_s, dh_s, dxc_s, acc_s, send_sems, recv_sems = rest
        b = pl.program_id(1)
        exchange = _sibling_copies(give_refs, got_refs, send_sems, recv_sems)

        @pl.when(jnp.logical_and(pl.program_id(0) == 0, b == 0))
        def _():
            for cp in exchange:
                cp.start()

        lam_v = lam_ref[...]
        sp = _softplus(-lam_v)
        acc_s[...] = jnp.zeros_like(acc_s)

        @pl.when(b == 0)
        def _():
            dwbd_ref[...] = jnp.zeros_like(dwbd_ref)
            vec_ref[...] = jnp.zeros_like(vec_ref)

        def gates(c, carry):
            rows = _rows(c, RC)
            a_s[rows, :] = jnp.exp(-LRU_C * gr_s[rows, :] * sp)
            ga = ga_ref[rows, :].astype(F32)
            sg = _sigmoid(ga)
            dya_c = dya_ref[rows, :].astype(F32)
            dl_s[rows, :] = dya_c * (ga * sg)
            dp_ref[1, rows, :] = (dya_c * hs_ref[rows, :] * (sg * (1.0 + ga * (1.0 - sg)))).astype(BF16)
            return carry

        lax.fori_loop(0, S // RC, gates, 0)

        def scan(i, g_in):
            base = pl.multiple_of((S // (8 * SCAN_GROUP) - 1 - i) * (8 * SCAN_GROUP), 8 * SCAN_GROUP)
            row = lax.broadcasted_iota(jnp.int32, (8, CB), 0)
            for k in reversed(range(SCAN_GROUP)):
                rows = pl.ds(base + 8 * k, 8)
                a = a_s[rows, :]
                dl = dl_s[rows, :]
                a_cum, g_loc = _tile_scan_rev(a, a * dl)
                g = g_loc + a_cum * g_in
                dh_s[rows, :] = dl + jnp.where(row < 7, pltpu.roll(g, 7, 0), g_in)
                g_in = g_loc[0:1, :] + a_cum[0:1, :] * g_in
            return g_in

        lax.fori_loop(0, S // (8 * SCAN_GROUP), scan, jnp.zeros((1, CB), F32))

        dxc_s[pl.ds(S, 8), :] = jnp.zeros((8, CB), F32)

        def grads(c, carry):
            rows = _rows(c, RC)
            dh = dh_s[rows, :]
            h_prev = _shift_down(_window_before(hs_ref, c, RC), 1, RC)
            xc, gi, gr, a = xc_s[rows, :], gi_s[rows, :], gr_s[rows, :], a_s[rows, :]
            mult = jnp.sqrt(-_expm1_nonpos(-2.0 * LRU_C * gr * sp))
            dmult = dh * gi * xc
            d_log_a = dh * h_prev * a - dmult * (a * a) / mult
            dzi = dh * mult * xc * gi * (1.0 - gi)
            dzr = d_log_a * (-LRU_C * sp) * gr * (1.0 - gr)
            dz = jnp.concatenate([dzi, dzr], axis=1).astype(BF16)
            dxc_s[rows, :] = dh * mult * gi + _dot_nt(dz, wbd_ref[...])
            dwbd_ref[...] += _dot_tn(xc.astype(BF16), dz)
            acc_s[1:2, :] += jnp.sum(dzi, axis=0, keepdims=True)
            acc_s[2:3, :] += jnp.sum(dzr, axis=0, keepdims=True)
            acc_s[3:4, :] += jnp.sum(d_log_a * (-LRU_C * gr), axis=0, keepdims=True)
            return carry

        lax.fori_loop(0, S // RC, grads, 0, unroll=2)

        def conv_bwd(c, carry):
            rows = _rows(c, RC_CONV)
            dwin = dxc_s[pl.ds(pl.multiple_of(c * RC_CONV, RC_CONV), RC_CONV + 8), :]
            dxc = dwin[:RC_CONV, :]
            xwin = _window_before(xa_ref, c, RC_CONV)
            dxa = cw_ref[3:4, :] * dxc
            acc_s[0:1, :] += jnp.sum(dxc, axis=0, keepdims=True)
            acc_s[7:8, :] += jnp.sum(dxc * _shift_down(xwin, 0, RC_CONV), axis=0, keepdims=True)
            for s in (1, 2, 3):
                dxa = dxa + cw_ref[3 - s:4 - s, :] * _shift_up(dwin, s, RC_CONV)
                acc_s[7 - s:8 - s, :] += jnp.sum(dxc * _shift_down(xwin, s, RC_CONV), axis=0, keepdims=True)
            dp_ref[0, rows, :] = dxa.astype(BF16)
            return carry

        lax.fori_loop(0, S // RC_CONV, conv_bwd, 0)

        row = lax.broadcasted_iota(jnp.int32, acc_s.shape, 0)
        vec_ref[...] += jnp.where(row == 3, acc_s[...] * (-_sigmoid(-lam_v)), acc_s[...])

        @pl.when(jnp.logical_and(pl.program_id(0) == NCB - 1, b == nb - 1))
        def _():
            for cp in exchange:
                cp.wait()

    vec = pl.BlockSpec((1, CB), lambda cb, b: (0, cb))
    blk = pl.BlockSpec((S, CB), lambda cb, b: (b, cb))
    own = pl.BlockSpec((None, None, S, CB), lambda cb, b: (b, cb, 0, 0))
    return pl.pallas_call(
        body, name="lru_bwd", grid=(NCB, nb),
        in_specs=[pl.BlockSpec((None, S, CB), lambda cb, b: (0, b, cb)),
                  pl.BlockSpec((None, S, CB), lambda cb, b: (1, b, cb)),
                  own, own, own, own, blk,
                  pl.BlockSpec((4, CB), lambda cb, b: (0, cb)),
                  pl.BlockSpec((None, CB, 2 * CB), lambda cb, b: (cb, 0, 0)),
                  vec] + [ANY] * ng,
        out_specs=[pl.BlockSpec((2, S, CB), lambda cb, b: (0, b, cb)),
                   pl.BlockSpec((None, CB, 2 * CB), lambda cb, b: (cb, 0, 0)),
                   pl.BlockSpec((8, CB), lambda cb, b: (0, cb))] + [ANY] * ng,
        out_shape=[jax.ShapeDtypeStruct((2, t, D), BF16),
                   jax.ShapeDtypeStruct((NCB, CB, 2 * CB), F32),
                   jax.ShapeDtypeStruct((8, D), F32)]
        + [jax.ShapeDtypeStruct((4,) + g.shape[1:], g.dtype) for g in give],
        scratch_shapes=[pltpu.VMEM((S, CB), F32), pltpu.VMEM((S, CB), F32), pltpu.VMEM((S, CB), F32),
                        pltpu.VMEM((S + 8, CB), F32), pltpu.VMEM((8, CB), F32),
                        pltpu.SemaphoreType.DMA((ng, 4)), pltpu.SemaphoreType.DMA((ng, 4))],
        compiler_params=_params(("arbitrary", "arbitrary")),
    )(proj, proj, hs, xc_f, gi_f, gr_f, dya, conv_w, wbd, lam, *give)


def _retention_tables():
    f32 = np.float32
    log_g = np.log1p(-(f32(2.0) ** (f32(-5.0) - np.arange(HEADS, dtype=f32)))).astype(f32)
    idx = np.arange(CH, dtype=f32)
    diff = idx[:, None] - idx[None, :]
    inner = np.where(diff >= 0, np.exp(np.maximum(diff, f32(0.0))[None] * log_g[:, None, None]), f32(0.0)).astype(f32)
    cross = np.exp((idx[None, :] + f32(1.0)) * log_g[:, None]).astype(f32)
    state = np.exp((f32(CH - 1.0) - idx[None, :]) * log_g[:, None]).astype(f32)
    cross = np.ascontiguousarray(np.broadcast_to(cross[:, :, None], (HEADS, CH, DK)))
    state = np.ascontiguousarray(np.broadcast_to(state[:, :, None], (HEADS, CH, DK)))
    half = DK // 2
    freqs = (f32(10000.0) ** (-np.arange(half, dtype=f32) / f32(half))).astype(f32)
    ang = (np.arange(S, dtype=f32)[:, None] * freqs[None, :]).astype(f32)
    return tuple(jnp.asarray(a) for a in (inner, cross, state, np.cos(ang).astype(f32), np.sin(ang).astype(f32)))


def _rotate(x, cos, sin):
    half = DK // 2
    x1, x2 = x[:, :half], x[:, half:]
    return jnp.concatenate([x1 * cos - x2 * sin, x1 * sin + x2 * cos], axis=1)


def _rotate_back(d, cos, sin):
    half = DK // 2
    d1, d2 = d[:, :half], d[:, half:]
    return jnp.concatenate([d1 * cos + d2 * sin, d2 * cos - d1 * sin], axis=1)


def _ret_fwd(proj, gain, tables, nb, wp_own):
    t = nb * S
    inner_t, cross_t, state_t, cos_t, sin_t = tables

    def body(q_ref, k_ref, v_ref, gb_ref, gain_ref, dm_ref, cd_ref, sd_ref, cos_ref, sin_ref, wp_ref,
             yb_ref, qr_ref, kr_ref, o_ref, rs_ref, wpg_ref, r_s, send_sems, recv_sems, own_sems):
        b, hd = pl.program_id(0), pl.program_id(1)
        own, first, arrive, forward, others = _gather_copies([wp_ref], [wpg_ref], send_sems, recv_sems, own_sems)

        @pl.when(jnp.logical_and(b == 0, hd == 0))
        def _():
            for cp in own + first:
                cp.start()

        @pl.when(jnp.logical_and(b == nb - 1, hd == HEADS - 1))
        def _():
            for came, on in zip(arrive, forward):
                came.wait_recv()
                on.start()

        r_s[...] = jnp.zeros_like(r_s)
        chunk_decay = cd_ref[CH - 1:CH, :]

        def chunk(c, carry):
            rows = _rows(c, CH)
            cos, sin = cos_ref[rows, :], sin_ref[rows, :]
            qr = _rotate(q_ref[rows, :].astype(F32), cos, sin).astype(BF16)
            kr = (_rotate(k_ref[rows, :].astype(F32), cos, sin) * (DK ** -0.5)).astype(BF16)
            vb = v_ref[rows, :]
            v = vb.astype(F32)
            qr_ref[rows, :] = qr
            kr_ref[rows, :] = kr
            r = r_s[...]
            rb = r.astype(BF16)
            rs_ref[c] = rb
            p = (_dot_nt(qr, kr) * dm_ref[...]).astype(BF16)
            o = _dot(p, vb) + _dot(qr, rb) * cd_ref[...]
            r_s[...] = chunk_decay * r + _dot_tn(kr, (v * sd_ref[...]).astype(BF16))
            o_ref[rows, :] = o.astype(BF16)
            oc = o - jnp.mean(o, axis=-1, keepdims=True)
            rstd = lax.rsqrt(jnp.mean(oc * oc, axis=-1, keepdims=True) + EPS)
            gb = gb_ref[rows, :].astype(F32)
            yb_ref[rows, :] = (gb * _sigmoid(gb) * (oc * rstd * gain_ref[...])).astype(BF16)
            return carry

        lax.fori_loop(0, NCH, chunk, 0, unroll=2)

        @pl.when(jnp.logical_and(b == nb - 1, hd == HEADS - 1))
        def _():
            for cp in others:
                cp.wait_recv()
            for cp in first + forward:
                cp.wait_send()
            for cp in own:
                cp.wait()

    seg = lambda s: pl.BlockSpec((None, S, DK), lambda b, h: (s, b, h))
    tab = pl.BlockSpec((None, CH, DK), lambda b, h: (h, 0, 0))
    rot = pl.BlockSpec((S, DK // 2), lambda b, h: (0, 0))
    blk = pl.BlockSpec((S, DK), lambda b, h: (b, h))
    return pl.pallas_call(
        body, name="ret_fwd", grid=(nb, HEADS),
        in_specs=[seg(2), seg(3), seg(4), seg(5),
                  pl.BlockSpec((None, 1, DK), lambda b, h: (h, 0, 0)),
                  tab, tab, tab, rot, rot, ANY],
        out_specs=[blk, blk, blk, blk,
                   pl.BlockSpec((None, None, NCH, DK, DK), lambda b, h: (b, h, 0, 0, 0)), ANY],
        out_shape=[jax.ShapeDtypeStruct((t, D), BF16), jax.ShapeDtypeStruct((t, D), BF16),
                   jax.ShapeDtypeStruct((t, D), BF16), jax.ShapeDtypeStruct((t, D), BF16),
                   jax.ShapeDtypeStruct((nb, HEADS, NCH, DK, DK), BF16),
                   jax.ShapeDtypeStruct((NDEV,) + wp_own.shape, wp_own.dtype)],
        scratch_shapes=[pltpu.VMEM((DK, DK), F32),
                        pltpu.SemaphoreType.DMA((1, 7)), pltpu.SemaphoreType.DMA((1, 7)), pltpu.SemaphoreType.DMA((1,))],
        compiler_params=_params(("arbitrary", "arbitrary")),
    )(proj, proj, proj, proj, gain, inner_t, cross_t, state_t, cos_t, sin_t, wp_own)


def _ret_bwd(proj, qr, kr, o, rs, dyb, gain, tables, nb, sums):
    t = nb * S
    ns = len(sums)
    inner_t, cross_t, state_t, cos_t, sin_t = tables

    def body(qr_ref, kr_ref, v_ref, gb_ref, o_ref, dyb_ref, rs_ref, gain_ref, dm_ref, cd_ref, sd_ref,
             cos_ref, sin_ref, *rest):
        sum_refs, rest = rest[:ns], rest[ns:]
        dp_ref, dgain_ref = rest[:2]
        part_refs, rest = rest[2:2 + ns], rest[2 + ns:]
        dr_s, send_sems, recv_sems, local_sems = rest
        mine, sends, recvs = _chip_copies(sum_refs, part_refs, send_sems, recv_sems, local_sems)

        @pl.when(jnp.logical_and(pl.program_id(0) == 0, pl.program_id(1) == 0))
        def _():
            for cp in mine + sends:
                cp.start()

        dr_s[...] = jnp.zeros_like(dr_s)
        chunk_decay = cd_ref[CH - 1:CH, :]

        @pl.when(pl.program_id(1) == 0)
        def _():
            dgain_ref[...] = jnp.zeros_like(dgain_ref)

        def chunk(i, carry):
            c = NCH - 1 - i
            rows = _rows(c, CH)
            gain_v = gain_ref[...]
            o_c = o_ref[rows, :].astype(F32)
            oc = o_c - jnp.mean(o_c, axis=-1, keepdims=True)
            rstd = lax.rsqrt(jnp.mean(oc * oc, axis=-1, keepdims=True) + EPS)
            yn = oc * rstd
            gb = gb_ref[rows, :].astype(F32)
            sg = _sigmoid(gb)
            dyb_c = dyb_ref[rows, :].astype(F32)
            dgn = dyb_c * (gb * sg)
            dp_ref[3, rows, :] = (dyb_c * (yn * gain_v) * (sg * (1.0 + gb * (1.0 - sg)))).astype(BF16)
            dgain_ref[...] += jnp.sum(dgn * yn, axis=0, keepdims=True)
            dyn = dgn * gain_v
            do = rstd * (dyn - jnp.mean(dyn, axis=-1, keepdims=True)
                         - yn * jnp.mean(dyn * yn, axis=-1, keepdims=True))
            dob = do.astype(BF16)
            dox = (do * cd_ref[...]).astype(BF16)

            q_c, k_c = qr_ref[rows, :], kr_ref[rows, :]
            vb = v_ref[rows, :]
            v = vb.astype(F32)
            vs = (v * sd_ref[...]).astype(BF16)
            rb = rs_ref[c]
            d_r = dr_s[...]
            drb = d_r.astype(BF16)
            dm = dm_ref[...]
            p = (_dot_nt(q_c, k_c) * dm).astype(BF16)
            dpm = (_dot_nt(dob, vb) * dm).astype(BF16)
            dq = _dot(dpm, k_c) + _dot_nt(dox, rb)
            dk = _dot_tn(dpm, q_c) + _dot_nt(vs, drb)
            dv = _dot_tn(p, dob) + _dot(k_c, drb) * sd_ref[...]
            dr_s[...] = chunk_decay * d_r + _dot_tn(q_c, dox)

            cos, sin = cos_ref[rows, :], sin_ref[rows, :]
            dp_ref[0, rows, :] = _rotate_back(dq, cos, sin).astype(BF16)
            dp_ref[1, rows, :] = (_rotate_back(dk, cos, sin) * (DK ** -0.5)).astype(BF16)
            dp_ref[2, rows, :] = dv.astype(BF16)
            return carry

        lax.fori_loop(0, NCH, chunk, 0, unroll=2)

        @pl.when(jnp.logical_and(pl.program_id(0) == HEADS - 1, pl.program_id(1) == nb - 1))
        def _():
            for cp in recvs:
                cp.wait_recv()
            for cp in sends:
                cp.wait_send()
            for cp in mine:
                cp.wait()

    seg = lambda s: pl.BlockSpec((None, S, DK), lambda h, b: (s, b, h))
    tab = pl.BlockSpec((None, CH, DK), lambda h, b: (h, 0, 0))
    rot = pl.BlockSpec((S, DK // 2), lambda h, b: (0, 0))
    blk = pl.BlockSpec((S, DK), lambda h, b: (b, h))
    one = pl.BlockSpec((None, 1, DK), lambda h, b: (h, 0, 0))
    return pl.pallas_call(
        body, name="ret_bwd", grid=(HEADS, nb),
        in_specs=[blk, blk, seg(4), seg(5), blk, blk,
                  pl.BlockSpec((None, None, NCH, DK, DK), lambda h, b: (b, h, 0, 0, 0)),
                  one, tab, tab, tab, rot, rot] + [ANY] * ns,
        out_specs=[pl.BlockSpec((4, S, DK), lambda h, b: (0, b, h)), one] + [ANY] * ns,
        out_shape=[jax.ShapeDtypeStruct((4, t, D), BF16), jax.ShapeDtypeStruct((HEADS, 1, DK), F32)]
        + [jax.ShapeDtypeStruct(a.shape, a.dtype) for a in sums],
        scratch_shapes=[pltpu.VMEM((DK, DK), F32), pltpu.SemaphoreType.DMA((ns, 3)), pltpu.SemaphoreType.DMA((ns, 3)),
                        pltpu.SemaphoreType.DMA((ns,))],
        compiler_params=_params(("arbitrary", "arbitrary")),
    )(qr, kr, proj, proj, o, dyb, rs, gain, inner_t, cross_t, state_t, cos_t, sin_t, *sums)


def _wblock(k):
    return pl.BlockSpec((NDEV, D // NDEV, D), lambda i: (0, k, 0))


def _tail(ya, yb, proj, x2d, tgt, wg, g_fin):
    t = x2d.shape[0]
    tm = 256

    def body(ya_ref, yb_ref, ma_ref, mb_ref, x_ref, t_ref, wa_ref, wb_ref, wo_ref, g_ref,
             dx2_ref, dya_ref, dyb_ref, dm_ref, mg_ref, doa_ref, dob_ref, gfin_ref, loss_ref):
        i = pl.program_id(0)

        @pl.when(i == 0)
        def _():
            gfin_ref[...] = jnp.zeros_like(gfin_ref)
            loss_ref[...] = jnp.zeros_like(loss_ref)

        wa = wa_ref[...].reshape(D, D)
        wb = wb_ref[...].reshape(D, D)
        wo = wo_ref[...].reshape(D, D)
        out_a = _dot(ya_ref[...], wa)
        out_b = _dot(yb_ref[...], wb)
        sa = _sigmoid(ma_ref[...].astype(F32))
        sb = _sigmoid(mb_ref[...].astype(F32))
        merged = (sa * out_a + sb * out_b).astype(BF16)
        mg_ref[...] = merged
        x2 = x_ref[...] + _dot(merged, wo)
        r2 = lax.rsqrt(jnp.mean(x2 * x2, axis=-1, keepdims=True) + EPS)
        xh = x2 * r2
        g = g_ref[...]
        err = xh * g - t_ref[...]
        loss_ref[...] += jnp.sum(err * err, axis=0, keepdims=True) * (0.5 / D)
        dy = err * (1.0 / D)
        gfin_ref[...] += jnp.sum(dy * xh, axis=0, keepdims=True)
        dxh = dy * g
        dx2 = r2 * (dxh - xh * jnp.mean(dxh * xh, axis=-1, keepdims=True))
        dx2_ref[...] = dx2
        dmerged = _dot_nt(dx2.astype(BF16), wo)
        doa = (sa * dmerged).astype(BF16)
        dob = (sb * dmerged).astype(BF16)
        doa_ref[...] = doa
        dob_ref[...] = dob
        dm_ref[0] = (dmerged * out_a * sa * (1.0 - sa)).astype(BF16)
        dm_ref[1] = (dmerged * out_b * sb * (1.0 - sb)).astype(BF16)
        dya_ref[...] = _dot_nt(doa, wa).astype(BF16)
        dyb_ref[...] = _dot_nt(dob, wb).astype(BF16)

    row = lambda: pl.BlockSpec((tm, D), lambda i: (i, 0))
    seg = lambda s: pl.BlockSpec((None, tm, D), lambda i: (s, i, 0))
    vec = pl.BlockSpec((1, D), lambda i: (0, 0))
    return pl.pallas_call(
        body, name="tail", grid=(t // tm,),
        in_specs=[row(), row(), seg(6), seg(7), row(), row(), _wblock(0), _wblock(1), _wblock(2), vec],
        out_specs=[row(), row(), row(), pl.BlockSpec((2, tm, D), lambda i: (0, i, 0)),
                   row(), row(), row(), vec, vec],
        out_shape=[jax.ShapeDtypeStruct((t, D), F32), jax.ShapeDtypeStruct((t, D), BF16),
                   jax.ShapeDtypeStruct((t, D), BF16), jax.ShapeDtypeStruct((2, t, D), BF16),
                   jax.ShapeDtypeStruct((t, D), BF16), jax.ShapeDtypeStruct((t, D), BF16),
                   jax.ShapeDtypeStruct((t, D), BF16), jax.ShapeDtypeStruct((1, D), F32),
                   jax.ShapeDtypeStruct((1, D), F32)],
        compiler_params=_params(("arbitrary",)),
    )(ya, yb, proj, proj, x2d, tgt, wg, wg, wg, g_fin)


def _tail_wgrad(ya, yb, merged, doa, dob, dx2):
    t = ya.shape[0]
    tm = 512

    def body(ya_ref, yb_ref, mg_ref, doa_ref, dob_ref, dx2_ref, ga_ref, gb_ref, go_ref):
        @pl.when(pl.program_id(0) == 0)
        def _():
            ga_ref[...] = jnp.zeros_like(ga_ref)
            gb_ref[...] = jnp.zeros_like(gb_ref)
            go_ref[...] = jnp.zeros_like(go_ref)

        ga_ref[...] += _dot_tn(ya_ref[...], doa_ref[...])
        gb_ref[...] += _dot_tn(yb_ref[...], dob_ref[...])
        go_ref[...] += _dot_tn(mg_ref[...], dx2_ref[...].astype(BF16))

    row = lambda: pl.BlockSpec((tm, D), lambda i: (i, 0))
    full = lambda: pl.BlockSpec((D, D), lambda i: (0, 0))
    return pl.pallas_call(
        body, name="tail_wgrad", grid=(t // tm,),
        in_specs=[row() for _ in range(6)], out_specs=[full(), full(), full()],
        out_shape=[jax.ShapeDtypeStruct((D, D), F32)] * 3,
        compiler_params=_params(("arbitrary",)),
    )(ya, yb, merged, doa, dob, dx2)


def _dproj_specs_ordered(tm):
    def spec(lo, n):
        def index(k, i, order_ref):
            seg = order_ref[k]
            mine = jnp.logical_and(seg >= lo, seg < lo + n)
            return jnp.where(mine, seg - lo, 0), jnp.where(mine, i, 0), 0
        return pl.BlockSpec((None, tm, D), index)
    return [spec(0, 2), spec(2, 4), spec(6, 2)]


def _dproj_pick(j, da_ref, db_ref, dc_ref, use):
    @pl.when(j < 2)
    def _():
        use(da_ref[...])

    @pl.when(jnp.logical_and(j >= 2, j < 6))
    def _():
        use(db_ref[...])

    @pl.when(j >= 6)
    def _():
        use(dc_ref[...])


RS_X, RS_Y, RS_XY = 0, 1, 2
RS_ROLES = ((RS_XY, RS_X, RS_Y), (RS_Y, RS_XY, RS_X))


def _rs_flip(rel, x, y):
    return ((1 - x, y), (x, 1 - y), (1 - x, 1 - y))[rel]


def _rs_order(x, y, c):
    order = []
    for s in range(4):
        chip = []
        for core in (0, 1):
            px, py = _rs_flip(RS_ROLES[core][s], x, y) if s < 3 else (x, y)
            chip.append(2 * px + py)
        keep = jnp.where(c == 0, chip[0], chip[1])
        give = jnp.where(c == 0, chip[1], chip[0])
        order += [2 * give + 1 - c, 2 * keep + c]
    return jnp.stack(order).astype(jnp.int32)


def _inproj_wgrad_rs(h, dpa, dpb, dpc, order, smalls):
    t = h.shape[0]
    tm = 1024
    nt = t // tm
    nsm = len(smalls)

    def body(order_ref, h_ref, da_ref, db_ref, dc_ref, *rest):
        small_refs, parts_ref, rest = rest[:nsm], rest[nsm], rest[nsm + 1:]
        all_refs, rest = rest[:nsm], rest[nsm:]
        (acc, sib, outb, far, give_send, give_recv, sum_send, sum_recv, far_send, far_recv, own_sem,
         small_send, small_recv, small_own) = rest
        k, i = pl.program_id(0), pl.program_id(1)
        x, y, c = _place()
        own, first, arrive, forward, others = _gather_copies(small_refs, all_refs, small_send, small_recv, small_own)

        @pl.when(jnp.logical_and(k == 0, i == 0))
        def _():
            for cp in own + first:
                cp.start()

        @pl.when(jnp.logical_and(k == 2, i == 0))
        def _():
            for came, on in zip(arrive, forward):
                came.wait_recv()
                on.start()

        def use(d):
            @pl.when(i == 0)
            def _():
                acc[k % 2] = _dot_tn(h_ref[...], d)

            @pl.when(i > 0)
            def _():
                acc[k % 2] += _dot_tn(h_ref[...], d)

        _dproj_pick(order_ref[k], da_ref, db_ref, dc_ref, use)

        def give_copy(s):
            return pltpu.make_async_remote_copy(
                src_ref=acc.at[0], dst_ref=sib.at[s % 2], send_sem=give_send.at[s], recv_sem=give_recv.at[s],
                device_id=(x, y, 1 - c), device_id_type=MESH)

        def sum_copy(s, core):
            slot = 0 if s < 2 else 1
            return pltpu.make_async_remote_copy(
                src_ref=outb.at[s], dst_ref=parts_ref.at[slot], send_sem=sum_send.at[slot], recv_sem=sum_recv.at[slot],
                device_id=(*_rs_flip(RS_ROLES[core][s], x, y), core), device_id_type=MESH)

        def far_copy(s, core):
            return pltpu.make_async_remote_copy(
                src_ref=outb.at[s], dst_ref=far, send_sem=far_send, recv_sem=far_recv,
                device_id=(*_rs_flip(RS_X if core == 0 else RS_Y, x, y), core), device_id_type=MESH)

        own_copy = pltpu.make_async_copy(outb.at[3], parts_ref.at[2], own_sem)

        def send_of(core, s):
            return far_copy(s, core) if RS_ROLES[core][s] == RS_XY else sum_copy(s, core)

        for s in range(4):
            @pl.when(jnp.logical_and(k == 2 * s, i == nt - 1))
            def _():
                give_copy(s).start()

            @pl.when(jnp.logical_and(k == 2 * s + 1, i == nt - 1))
            def _():
                give_copy(s).wait_recv()
                if s == 2:
                    far_copy(s, 0).wait_recv()
                    outb[s] = (acc[1] + sib[s % 2] + far[...].astype(F32)).astype(BF16)
                else:
                    outb[s] = (acc[1] + sib[s % 2]).astype(BF16)
                give_copy(s).wait_send()
                if s < 3:
                    for core in (0, 1):
                        @pl.when(c == core)
                        def _():
                            send_of(core, s).start()
                else:
                    own_copy.start()

        @pl.when(jnp.logical_and(k == NSEG - 1, i == nt - 1))
        def _():
            for slot in (0, 1):
                sum_copy(2 * slot, 0).wait_recv()
            for s in range(3):
                send_of(0, s).wait_send()
            own_copy.wait()
            for cp in others:
                cp.wait_recv()
            for cp in first + forward:
                cp.wait_send()
            for cp in own:
                cp.wait()

    return pl.pallas_call(
        body, name="inproj_wgrad_rs",
        grid_spec=pltpu.PrefetchScalarGridSpec(
            num_scalar_prefetch=1, grid=(NSEG, nt),
            in_specs=[pl.BlockSpec((tm, D), lambda k, i, order_ref: (i, 0))] + _dproj_specs_ordered(tm) + [ANY] * nsm,
            out_specs=[ANY] * (1 + nsm),
            scratch_shapes=[pltpu.VMEM((2, D, D), F32), pltpu.VMEM((2, D, D), F32), pltpu.VMEM((4, D, D), BF16),
                            pltpu.VMEM((D, D), BF16),
                            pltpu.SemaphoreType.DMA((4,)), pltpu.SemaphoreType.DMA((4,)),
                            pltpu.SemaphoreType.DMA((2,)), pltpu.SemaphoreType.DMA((2,)),
                            pltpu.SemaphoreType.DMA, pltpu.SemaphoreType.DMA, pltpu.SemaphoreType.DMA,
                            pltpu.SemaphoreType.DMA((nsm, 7)), pltpu.SemaphoreType.DMA((nsm, 7)),
                            pltpu.SemaphoreType.DMA((nsm,))]),
        out_shape=[jax.ShapeDtypeStruct((3, D, D), BF16)]
        + [jax.ShapeDtypeStruct((NDEV,) + a.shape, a.dtype) for a in smalls],
        compiler_params=_params(("arbitrary", "arbitrary")),
    )(order, h, dpa, dpb, dpc, *smalls)


def _inproj_dgrad(dpa, dpb, dpc, wg, x2d, dx2, g_in):
    t = x2d.shape[0]
    tm = 512

    def body(da_ref, db_ref, dc_ref, w_hbm, x_ref, dx2_ref, g_ref, gx_ref, gg_ref, w_s, w_sem):
        i = pl.program_id(0)

        @pl.when(i == 0)
        def _():
            gg_ref[...] = jnp.zeros_like(gg_ref)
            load = pltpu.make_async_copy(w_hbm, w_s, w_sem)
            load.start()
            load.wait()

        dh = None
        for ref, lo in ((da_ref, 0), (db_ref, 2), (dc_ref, 6)):
            for k in range(ref.shape[0]):
                part = _dot_nt(ref[k], w_s[lo + k])
                dh = part if dh is None else dh + part
        x = x_ref[...]
        r = lax.rsqrt(jnp.mean(x * x, axis=-1, keepdims=True) + EPS)
        xh = x * r
        gg_ref[...] += jnp.sum(dh * xh, axis=0, keepdims=True)
        dxh = dh * g_ref[...]
        gx_ref[...] = dx2_ref[...] + r * (dxh - xh * jnp.mean(dxh * xh, axis=-1, keepdims=True))

    row = lambda: pl.BlockSpec((tm, D), lambda i: (i, 0))
    seg = lambda n: pl.BlockSpec((n, tm, D), lambda i: (0, i, 0))
    vec = pl.BlockSpec((1, D), lambda i: (0, 0))
    return pl.pallas_call(
        body, name="inproj_dgrad", grid=(t // tm,),
        in_specs=[seg(2), seg(4), seg(2), ANY, row(), row(), vec],
        out_specs=[row(), vec],
        out_shape=[jax.ShapeDtypeStruct((t, D), F32), jax.ShapeDtypeStruct((1, D), F32)],
        scratch_shapes=[pltpu.VMEM((NSEG, D, D), BF16), pltpu.SemaphoreType.DMA],
        compiler_params=_params(("arbitrary",)),
    )(dpa, dpb, dpc, wg, x2d, dx2, g_in)


def _adam_update(g, w, m, v):
    m_new = ADAM_B1 * m + (1.0 - ADAM_B1) * g
    v_new = ADAM_B2 * v + (1.0 - ADAM_B2) * (g * g)
    m_hat = m_new / (1.0 - ADAM_B1 ** ADAM_STEP)
    v_hat = v_new / (1.0 - ADAM_B2 ** ADAM_STEP)
    return -ADAM_LR * (m_hat / (jnp.sqrt(v_hat) + ADAM_EPS) + ADAM_WD * w), m_new, v_new


def _sum_in_order(ref):
    total = ref[0].astype(F32)
    for k in range(1, ref.shape[0]):
        total = total + ref[k].astype(F32)
    return total


def _sum_devices(arrs):
    def body(*refs):
        for a in range(len(arrs)):
            refs[len(arrs) + a][...] = _sum_in_order(refs[a])

    return pl.pallas_call(
        body, name="sum_devices",
        out_shape=[jax.ShapeDtypeStruct(a.shape[1:], F32) for a in arrs],
        compiler_params=_params(),
    )(*arrs)


def _adamw_small(me, vec_all, gx_all, ga_all, groups):
    flat = [a for grp in groups for a in grp]
    ng = len(groups)
    nshard = D // NDEV

    def body(me_ref, vec_ref, shard_ref, gx_ref, ga_ref, *refs):
        ins, outs = refs[:3 * ng], refs[3 * ng:]
        vec = _sum_in_order(vec_ref)
        shard = _sum_in_order(shard_ref)
        grads = [vec[r:r + 1, :] for r in range(6)]
        grads += [shard[0:4, :], shard[4:8, 0:DK // NDEV], _sum_in_order(gx_ref), _sum_in_order(ga_ref)]
        for n, g in enumerate(grads):
            delta, m_new, v_new = _adam_update(g, ins[3 * n][...], ins[3 * n + 1][...], ins[3 * n + 2][...])
            outs[4 * n][...] = g
            outs[4 * n + 1][...] = delta
            outs[4 * n + 2][...] = m_new
            outs[4 * n + 3][...] = v_new
        outs[4 * ng][...] = jnp.sum(vec[6:7, :], axis=1, keepdims=True)

    full = lambda a: pl.BlockSpec(a.shape, lambda i, me_ref, nd=len(a.shape): (0,) * nd)
    out_shape = [jax.ShapeDtypeStruct(w.shape, F32) for w, _, _ in groups for _ in range(4)]
    out_shape.append(jax.ShapeDtypeStruct((1, 1), F32))
    outs = pl.pallas_call(
        body, name="adamw_small",
        grid_spec=pltpu.PrefetchScalarGridSpec(
            num_scalar_prefetch=1, grid=(1,),
            in_specs=[full(vec_all),
                      pl.BlockSpec((NDEV, 8, nshard), lambda i, me_ref: (0, 1, me_ref[0])),
                      full(gx_all), full(ga_all)] + [full(a) for a in flat],
            out_specs=[full(s) for s in out_shape]),
        out_shape=out_shape,
        compiler_params=_params(("arbitrary",)),
    )(me, vec_all, vec_all, gx_all, ga_all, *flat)
    return [outs[4 * n:4 * n + 4] for n in range(ng)], outs[4 * ng]


def _adamw(name, items):
    n, rows, cols = items[0][0].shape
    tr = rows if rows <= 256 else 256
    k = len(items)

    def body(*refs):
        for a in range(k):
            p_ref, w_ref, m_ref, v_ref = refs[4 * a:4 * a + 4]
            g = _sum_in_order(p_ref)
            delta, m_new, v_new = _adam_update(g, w_ref[...], m_ref[...], v_ref[...])
            for o, val in zip(refs[4 * k + 4 * a:4 * k + 4 * a + 4], (g, delta, m_new, v_new)):
                o[...] = val

    blk = lambda: pl.BlockSpec((tr, cols), lambda i: (i, 0))
    outs = pl.pallas_call(
        body, name=name, grid=(rows // tr,),
        in_specs=[pl.BlockSpec((n, tr, cols), lambda i: (0, i, 0)), blk(), blk(), blk()] * k,
        out_specs=[blk() for _ in range(4 * k)],
        out_shape=[jax.ShapeDtypeStruct((rows, cols), F32)] * (4 * k),
        compiler_params=_params(("arbitrary",)),
    )(*[a for item in items for a in item])
    return [outs[4 * a:4 * a + 4] for a in range(k)]


ANY = pl.BlockSpec(memory_space=pl.ANY)


def _place():
    return lax.axis_index("x"), lax.axis_index("y"), lax.axis_index("c")


def _gather_copies(ins, outs, send_sems, recv_sems, own_sems):
    x, y, c = _place()
    me, sibling = (x, y, c), (x, y, 1 - c)
    chips = [(1 - x, y), (x, 1 - y), (1 - x, 1 - y)]
    n = len(ins)

    def copy(a, k, block, to, src=None):
        px, py, pc = block
        dst = outs[a].at[4 * px + 2 * py + pc]
        return pltpu.make_async_remote_copy(
            src_ref=dst if src is None else src, dst_ref=dst,
            send_sem=send_sems.at[a, k], recv_sem=recv_sems.at[a, k], device_id=to, device_id_type=MESH)

    own = [pltpu.make_async_copy(ins[a], outs[a].at[4 * x + 2 * y + c], own_sems.at[a]) for a in range(n)]
    first = []
    for a in range(n):
        first.append(copy(a, 0, me, sibling, src=ins[a]))
        first += [copy(a, 1 + j, me, (*chip, c), src=ins[a]) for j, chip in enumerate(chips)]
    arrive = [copy(a, 1 + j, (*chip, c), me) for j, chip in enumerate(chips) for a in range(n)]
    forward = [copy(a, 4 + j, (*chip, c), sibling) for j, chip in enumerate(chips) for a in range(n)]
    rest = [copy(a, 0, sibling, me) for a in range(n)]
    rest += [copy(a, 4 + j, (*chip, 1 - c), me) for a in range(n) for j, chip in enumerate(chips)]
    return own, first, arrive, forward, rest


def _sibling_copies(ins, outs, send_sems, recv_sems):
    x, y, c = _place()
    return [pltpu.make_async_remote_copy(
        src_ref=ins[a].at[2 * q + 1 - c], dst_ref=outs[a].at[q],
        send_sem=send_sems.at[a, q], recv_sem=recv_sems.at[a, q],
        device_id=(x, y, 1 - c), device_id_type=MESH) for a in range(len(ins)) for q in range(4)]


def _chip_copies(ins, outs, send_sems, recv_sems, local_sems):
    x, y, c = _place()
    my_chip = 2 * x + y
    chips = [(1 - x, y), (x, 1 - y), (1 - x, 1 - y)]
    n = len(ins)
    mine = [pltpu.make_async_copy(ins[a].at[my_chip], outs[a].at[my_chip], local_sems.at[a]) for a in range(n)]
    sends = [pltpu.make_async_remote_copy(
        src_ref=ins[a].at[2 * px + py], dst_ref=outs[a].at[my_chip],
        send_sem=send_sems.at[a, j], recv_sem=recv_sems.at[a, j],
        device_id=(px, py, c), device_id_type=MESH) for a in range(n) for j, (px, py) in enumerate(chips)]
    recvs = [pltpu.make_async_remote_copy(
        src_ref=ins[a].at[my_chip], dst_ref=outs[a].at[2 * px + py],
        send_sem=send_sems.at[a, j], recv_sem=recv_sems.at[a, j],
        device_id=(px, py, c), device_id_type=MESH) for a in range(n) for j, (px, py) in enumerate(chips)]
    return mine, sends, recvs


def _chip_sum(owns, gots, core):
    n = len(owns)
    _, rows, cols = owns[0].shape

    def body(core_ref, *refs):
        for a in range(n):
            refs[2 * n + a][...] = (refs[a][...] + refs[n + a][...]).astype(BF16)

    own_spec = pl.BlockSpec((None, rows, cols), lambda q, core_ref: (2 * q + core_ref[0], 0, 0))
    slab = pl.BlockSpec((None, rows, cols), lambda q, core_ref: (q, 0, 0))
    return pl.pallas_call(
        body, name="chip_sum",
        grid_spec=pltpu.PrefetchScalarGridSpec(
            num_scalar_prefetch=1, grid=(4,),
            in_specs=[own_spec] * n + [slab] * n, out_specs=[slab] * n),
        out_shape=[jax.ShapeDtypeStruct((4, rows, cols), BF16)] * n,
        compiler_params=_params(("arbitrary",)),
    )(core, *owns, *gots)


def _block_diag(w):
    w4 = w.reshape(NCB, 4, 64, 64)
    eye = jnp.eye(4, dtype=w.dtype)
    return (w4[:, :, :, None, :] * eye[None, :, None, :, None]).reshape(NCB, CB, CB)


def _block_diag_back(g):
    g5 = g.reshape(NCB, 4, 64, 4, 64)
    return jnp.stack([g5[:, m, :, m, :] for m in range(4)], axis=1).reshape(16, 64, 64)


def kernel(x, norm_in, w_in, conv_w, conv_b, gate_x_w, gate_x_b, gate_a_w, gate_a_b, lru_lambda, gn_gain, w_proj_a, w_proj_b, w_out, norm_final, loss_target, m_norm_in, m_w_in, m_conv_w, m_conv_b, m_gate_x_w, m_gate_x_b, m_gate_a_w, m_gate_a_b, m_lru_lambda, m_gn_gain, m_w_proj_a, m_w_proj_b, m_w_out, m_norm_final, v_norm_in, v_w_in, v_conv_w, v_conv_b, v_gate_x_w, v_gate_x_b, v_gate_a_w, v_gate_a_b, v_lru_lambda, v_gn_gain, v_w_proj_a, v_w_proj_b, v_w_out, v_norm_final):
    xi, yi, ci = _place()
    me = 4 * xi + 2 * yi + ci
    core = ci.astype(jnp.int32).reshape(1)
    nshard = D // NDEV
    nb = x.shape[0]
    t = nb * S
    x2d = x.reshape(t, D)
    tgt2d = loss_target.reshape(t, D)
    g_final = norm_final.reshape(1, D)
    wbd = jnp.concatenate([_block_diag(gate_x_w[0]), _block_diag(gate_a_w[0])], axis=-1).astype(BF16)
    tables = _retention_tables()

    wp_own = jnp.concatenate([w_proj_a[0], w_proj_b[0], w_out[0]], axis=0).astype(BF16)
    tiny = jnp.concatenate([conv_w[0], jnp.pad(gn_gain[0], ((0, 0), (0, nshard - DK // NDEV)))], axis=0)
    proj, h, wg, tiny_g = _inproj_gather(x2d, norm_in, w_in[0].astype(BF16), tiny, *_gather_order(xi, yi, ci))
    conv_w_full = tiny_g[:, 0:4, :].transpose(1, 0, 2).reshape(4, D)
    gain3 = tiny_g[:, 4:8, :DK // NDEV].transpose(1, 0, 2).reshape(HEADS, 1, DK)

    ya, hs, xc, gi, gr = _lru_fwd(proj, conv_w_full, conv_b, wbd, gate_x_b, gate_a_b, lru_lambda, nb)
    yb, qr, kr, o, rs, wpg = _ret_fwd(proj, gain3, tables, nb, wp_own)
    dx2, dya, dyb, dpc, merged, doa, dob, g_fin, loss_vec = _tail(ya, yb, proj, x2d, tgt2d, wpg, g_final)
    g_pa, g_pb, g_out = _tail_wgrad(ya, yb, merged, doa, dob, dx2)

    own = [g.reshape(NDEV, nshard, D) for g in (g_pa, g_pb, g_out)]
    dpa, g_wbd, g_vec, *got = _lru_bwd(proj, hs, xc, gi, gr, dya, conv_w_full, wbd, lru_lambda, nb, own)
    sums = _chip_sum(own, got, core)
    dpb, g_gain, *parts = _ret_bwd(proj, qr, kr, o, rs, dyb, gain3, tables, nb, sums)

    grad_x, g_norm_in = _inproj_dgrad(dpa, dpb, dpc, wg, x2d, dx2, norm_in)
    grad_x = grad_x.reshape(nb, S, D)

    gain_rows = jnp.pad(g_gain.reshape(HEADS, NDEV, DK // NDEV), ((0, 0), (0, 0), (0, nshard - DK // NDEV)))
    vec = jnp.concatenate([g_norm_in, g_vec[0:4], g_fin, loss_vec, jnp.zeros((1, D), F32), g_vec[4:8],
                           gain_rows.reshape(HEADS, D)], axis=0)
    g_gx = _block_diag_back(g_wbd[:, :, :CB]).reshape(D // 2, 128)
    g_ga = _block_diag_back(g_wbd[:, :, CB:]).reshape(D // 2, 128)
    parts_in, vec_all, gx_all, ga_all = _inproj_wgrad_rs(h, dpa, dpb, dpc, _rs_order(xi, yi, ci),
                                                         [vec, g_gx, g_ga])
    gx_all, ga_all = [g.reshape(1, D, 64) for g in _sum_devices([gx_all, ga_all])]
    parts = [parts_in] + list(parts)

    res = {}
    (out,) = _adamw("adamw_w_in", [(parts[0], w_in[0], m_w_in[0], v_w_in[0])])
    res["w_in"] = [o[None] for o in out]
    square = [("w_proj_a", w_proj_a, m_w_proj_a, v_w_proj_a), ("w_proj_b", w_proj_b, m_w_proj_b, v_w_proj_b),
              ("w_out", w_out, m_w_out, v_w_out)]
    outs = _adamw("adamw_square", [(parts[1 + k], w[0], m[0], v[0]) for k, (_, w, m, v) in enumerate(square)])
    for (nm, _, _, _), out in zip(square, outs):
        res[nm] = [o[None] for o in out]

    row = lambda a: a.reshape(1, D)
    gate = lambda a: a.reshape(D, 64)
    groups = [("norm_in", norm_in, m_norm_in, v_norm_in, row), ("conv_b", conv_b, m_conv_b, v_conv_b, row),
              ("gate_x_b", gate_x_b, m_gate_x_b, v_gate_x_b, row), ("gate_a_b", gate_a_b, m_gate_a_b, v_gate_a_b, row),
              ("lru_lambda", lru_lambda, m_lru_lambda, v_lru_lambda, row),
              ("norm_final", norm_final, m_norm_final, v_norm_final, row),
              ("conv_w", conv_w, m_conv_w, v_conv_w, lambda a: a[0]), ("gn_gain", gn_gain, m_gn_gain, v_gn_gain, lambda a: a[0]),
              ("gate_x_w", gate_x_w, m_gate_x_w, v_gate_x_w, gate), ("gate_a_w", gate_a_w, m_gate_a_w, v_gate_a_w, gate)]
    small_out, loss = _adamw_small(me.astype(jnp.int32).reshape(1), vec_all, gx_all, ga_all,
                                   [tuple(view(a) for a in (w, m, v)) for _, w, m, v, view in groups])
    for (nm, w, _, _, _), out in zip(groups, small_out):
        res[nm] = [o.reshape(w.shape) for o in out]
    loss = loss.reshape(())

    order = ["norm_in", "w_in", "conv_w", "conv_b", "gate_x_w", "gate_x_b", "gate_a_w", "gate_a_b", "lru_lambda",
             "gn_gain", "w_proj_a", "w_proj_b", "w_out", "norm_final"]
    outs = [loss, grad_x]
    for k in range(4):
        outs += [res[nm][k] for nm in order]
    return tuple(outs)
```

```python
import numpy as np

import jax
import jax.numpy as jnp
from jax import lax
from jax.experimental import pallas as pl
from jax.experimental.pallas import tpu as pltpu

F32 = jnp.float32
BF16 = jnp.bfloat16
MESH = pl.DeviceIdType.MESH

D = 1024
S = 2048
NSEG = 8
NDEV = 8
HEADS = 4
DK = 256
CH = 256
NCH = S // CH
CB = 256
NCB = D // CB
RC = 512
RC_CONV = 128
SCAN_GROUP = 16
EPS = 1e-6
LRU_C = 8.0
VMEM_LIMIT = 56 * 1024 * 1024

ADAM_LR = 0.001
ADAM_B1 = 0.9
ADAM_B2 = 0.999
ADAM_EPS = 1e-08
ADAM_WD = 0.01
ADAM_STEP = 10


def _params(sem=None):
    return pltpu.CompilerParams(dimension_semantics=sem, vmem_limit_bytes=VMEM_LIMIT)


def _dot(a, b):
    return jnp.dot(a, b, preferred_element_type=F32)


def _dot_nt(a, b):
    return lax.dot_general(a, b, (((1,), (1,)), ((), ())), preferred_element_type=F32)


def _dot_tn(a, b):
    return lax.dot_general(a, b, (((0,), (0,)), ((), ())), preferred_element_type=F32)


def _sigmoid(x):
    return jax.nn.sigmoid(x)


def _expm1_nonpos(x):
    poly = x * (1.0 + x * (0.5 + x * (1.0 / 6.0 + x * (1.0 / 24.0))))
    return jnp.where(x > -0.05, poly, jnp.exp(x) - 1.0)


def _softplus(x):
    return jnp.maximum(x, 0.0) + jnp.log(1.0 + jnp.exp(-jnp.abs(x)))


def _rows(c, n):
    return pl.ds(pl.multiple_of(c * n, n), n)


def _window_before(ref, c, n):
    r0 = c * n
    if ref.dtype == BF16:
        prev = ref[pl.ds(pl.multiple_of(jnp.maximum(r0 - 16, 0), 16), 16), :].astype(F32)[8:, :]
    else:
        prev = ref[pl.ds(pl.multiple_of(jnp.maximum(r0 - 8, 0), 8), 8), :]
    prev = jnp.where(c > 0, prev, 0.0)
    return jnp.concatenate([prev, ref[_rows(c, n), :].astype(F32)], axis=0)


def _shift_down(win, s, n):
    if s == 0:
        return win[8:, :]
    return pltpu.roll(win, s, 0)[8:, :]


def _shift_up(win, s, n):
    if s == 0:
        return win[:n, :]
    return pltpu.roll(win, n + 8 - s, 0)[:n, :]


HALF = D // 2
GATHER_SLOTS = [("own", None, 0), ("own", None, 1), ("sib", None, 0), ("sib", None, 1)]
for _j, _h in ((0, 0), (1, 0), (0, 1), (1, 1), (2, 0), (2, 1)):
    GATHER_SLOTS += [("ici", _j, _h), ("fwd", _j, _h)]
NSLOT = len(GATHER_SLOTS)


def _gather_order(x, y, c):
    chips = [(1 - x, y), (x, 1 - y), (1 - x, 1 - y)]
    segs, halves = [], []
    for kind, j, h in GATHER_SLOTS:
        if kind == "own":
            seg = 4 * x + 2 * y + c
        elif kind == "sib":
            seg = 4 * x + 2 * y + 1 - c
        else:
            px, py = chips[j]
            seg = 4 * px + 2 * py + (c if kind == "ici" else 1 - c)
        segs.append(seg)
        halves.append(h)
    return jnp.stack(segs).astype(jnp.int32), jnp.asarray(halves, jnp.int32)


def _inproj_gather(x2d, g_in, w_own, tiny_own, order, halves):
    t = x2d.shape[0]
    tm = 2048
    nt = t // tm
    tx = 1024
    nx = tm // tx

    def body(order_ref, half_ref, g_ref, x_hbm, w_own_ref, tiny_own_ref,
             proj_ref, h_hbm, wg_ref, tinyg_ref,
             w_all, h_all, x_s, send_sems, recv_sems, own_sems, out_sems, tiny_send, tiny_recv, tiny_own_sem,
             x_sems, h_sem):
        k, i = pl.program_id(0), pl.program_id(1)
        x, y, c = _place()
        me, sibling = (x, y, c), (x, y, 1 - c)
        mine = 4 * x + 2 * y + c
        chips = [(1 - x, y), (x, 1 - y), (1 - x, 1 - y)]

        def copy(h, n, block, to, own_src=False):
            px, py, pc = block
            dst = w_all.at[4 * px + 2 * py + pc, h]
            return pltpu.make_async_remote_copy(
                src_ref=w_own_ref.at[:, pl.ds(h * HALF, HALF)] if own_src else dst, dst_ref=dst,
                send_sem=send_sems.at[h, n], recv_sem=recv_sems.at[h, n], device_id=to, device_id_type=MESH)

        def tiny_copy(n, block, to, own_src=False):
            px, py, pc = block
            dst = tinyg_ref.at[4 * px + 2 * py + pc]
            return pltpu.make_async_remote_copy(
                src_ref=tiny_own_ref if own_src else dst, dst_ref=dst,
                send_sem=tiny_send.at[n], recv_sem=tiny_recv.at[n], device_id=to, device_id_type=MESH)

        def own_copy(h):
            return pltpu.make_async_copy(w_own_ref.at[:, pl.ds(h * HALF, HALF)], w_all.at[mine, h], own_sems.at[h])

        tiny_mine = pltpu.make_async_copy(tiny_own_ref, tinyg_ref.at[mine], tiny_own_sem)

        def keep_copy(n):
            h = GATHER_SLOTS[n][2]
            return pltpu.make_async_copy(w_all.at[order_ref[n], h], wg_ref.at[order_ref[n], :, pl.ds(h * HALF, HALF)],
                                         out_sems.at[n])

        near = [(0, sibling), (1, (*chips[0], c)), (2, (*chips[1], c))]
        first = [copy(h, n, me, to, True) for h in (0, 1) for n, to in near]
        tiny_first = [tiny_copy(0, me, sibling, True)] + [tiny_copy(1 + j, me, (*chip, c), True) for j, chip in enumerate(chips)]

        def relay(h, j):
            seg = w_all.at[4 * chips[j][0] + 2 * chips[j][1] + c, h]
            return pltpu.make_async_remote_copy(
                src_ref=seg, dst_ref=seg, send_sem=send_sems.at[h, 3], recv_sem=recv_sems.at[h, 3],
                device_id=(*chips[1 - j], c), device_id_type=MESH)

        for n, (kind, j, h) in enumerate(GATHER_SLOTS):
            @pl.when(jnp.logical_and(k == n, i == 0))
            def _():
                if n == 0:
                    own_copy(0).start()
                    own_copy(1).start()
                    tiny_mine.start()
                    for cp in first + tiny_first:
                        cp.start()
                if kind == "own":
                    own_copy(h).wait()
                elif kind == "sib":
                    copy(h, 0, sibling, me).wait_recv()
                elif kind == "ici":
                    copy(h, 1 + j, (*chips[j], c), me).wait_recv()
                    copy(h, 4 + j, (*chips[j], c), sibling).start()
                    if j < 2:
                        @pl.when(c == j)
                        def _():
                            relay(h, j).start()
                else:
                    copy(h, 4 + j, (*chips[j], 1 - c), me).wait_recv()
                keep_copy(n).start()

        rows = pl.ds(pl.multiple_of(i * tm, tm), tm)

        def x_copy(n):
            return pltpu.make_async_copy(x_hbm.at[pl.ds(n * tx, tx), :], x_s.at[n % 2], x_sems.at[n % 2])

        keep_h = pltpu.make_async_copy(h_all, h_hbm, h_sem)

        for step in range(nt):
            @pl.when(jnp.logical_and(k == 0, i == step))
            def _():
                if step == 0:
                    x_copy(0).start()
                for n in range(step * nx, (step + 1) * nx):
                    x_copy(n).wait()
                    if n + 1 < nt * nx:
                        x_copy(n + 1).start()
                    xv = x_s[n % 2]
                    r = lax.rsqrt(jnp.mean(xv * xv, axis=-1, keepdims=True) + EPS)
                    h_all[pl.ds(n * tx, tx), :] = (xv * r * g_ref[...]).astype(BF16)
                if step == nt - 1:
                    keep_h.start()

        proj_ref[...] = _dot(h_all[rows, :], w_all[order_ref[k], half_ref[k]]).astype(BF16)

        @pl.when(jnp.logical_and(k == NSLOT - 1, i == nt - 1))
        def _():
            for j, chip in enumerate(chips):
                tiny_copy(1 + j, (*chip, c), me).wait_recv()
                tiny_copy(4 + j, (*chip, c), sibling).start()
            tiny_copy(0, sibling, me).wait_recv()
            for j, chip in enumerate(chips):
                tiny_copy(4 + j, (*chip, 1 - c), me).wait_recv()
            for cp in first + tiny_first:
                cp.wait_send()
            for j, chip in enumerate(chips):
                tiny_copy(4 + j, (*chip, c), sibling).wait_send()
                for h in (0, 1):
                    copy(h, 4 + j, (*chip, c), sibling).wait_send()
            for h in (0, 1):
                relay(h, 0).wait_send()
            tiny_mine.wait()
            keep_h.wait()
            for n in range(NSLOT):
                keep_copy(n).wait()

    return pl.pallas_call(
        body, name="inproj_gather",
        grid_spec=pltpu.PrefetchScalarGridSpec(
            num_scalar_prefetch=2, grid=(NSLOT, nt),
            in_specs=[pl.BlockSpec((1, D), lambda k, i, order_ref, half_ref: (0, 0)),
                      ANY, ANY, ANY],
            out_specs=[pl.BlockSpec((None, tm, HALF), lambda k, i, order_ref, half_ref: (order_ref[k], i, half_ref[k])),
                       ANY, ANY, ANY],
            scratch_shapes=[pltpu.VMEM((NDEV, 2, D, HALF), BF16), pltpu.VMEM((t, D), BF16), pltpu.VMEM((2, tx, D), F32),
                            pltpu.SemaphoreType.DMA((2, 7)), pltpu.SemaphoreType.DMA((2, 7)),
                            pltpu.SemaphoreType.DMA((2,)), pltpu.SemaphoreType.DMA((NSLOT,)),
                            pltpu.SemaphoreType.DMA((7,)), pltpu.SemaphoreType.DMA((7,)), pltpu.SemaphoreType.DMA,
                            pltpu.SemaphoreType.DMA((2,)), pltpu.SemaphoreType.DMA]),
        out_shape=[jax.ShapeDtypeStruct((NSEG, t, D), BF16), jax.ShapeDtypeStruct((t, D), BF16),
                   jax.ShapeDtypeStruct((NDEV,) + w_own.shape, BF16),
                   jax.ShapeDtypeStruct((NDEV,) + tiny_own.shape, F32)],
        compiler_params=_params(("arbitrary", "arbitrary")),
    )(order, halves, g_in, x2d, w_own, tiny_own)


def _tile_scan(a, u):
    row = lax.broadcasted_iota(jnp.int32, a.shape, 0)
    for d in (1, 2, 4):
        m = row >= d
        a_sh = pltpu.roll(a, d, 0)
        u_sh = pltpu.roll(u, d, 0)
        u = jnp.where(m, a * u_sh + u, u)
        a = jnp.where(m, a * a_sh, a)
    return a, u


def _tile_scan_rev(a, w):
    row = lax.broadcasted_iota(jnp.int32, a.shape, 0)
    for d in (1, 2, 4):
        m = row < 8 - d
        a_sh = pltpu.roll(a, 8 - d, 0)
        w_sh = pltpu.roll(w, 8 - d, 0)
        w = jnp.where(m, a * w_sh + w, w)
        a = jnp.where(m, a * a_sh, a)
    return a, w


def _lru_gates(xa_ref, c, cw_ref, cb_ref, wbd_ref, bx_ref, ba_ref, sp):
    win = _window_before(xa_ref, c, RC)
    xc = cb_ref[...] + cw_ref[3:4, :] * _shift_down(win, 0, RC)
    for s in (1, 2, 3):
        xc = xc + cw_ref[3 - s:4 - s, :] * _shift_down(win, s, RC)
    z = _dot(xc.astype(BF16), wbd_ref[...])
    gi = _sigmoid(z[:, :CB] + bx_ref[...])
    gr = _sigmoid(z[:, CB:] + ba_ref[...])
    log_a = -LRU_C * gr * sp
    return win, xc, gi, gr, log_a


def _lru_fwd(proj, conv_w, conv_b, wbd, bx, ba, lam, nb):
    t = nb * S

    def body(xa_ref, ga_ref, cw_ref, cb_ref, wbd_ref, bx_ref, ba_ref, lam_ref,
             ya_ref, hs_ref, xc_ref, gi_ref, gr_ref, a_s, u_s):
        sp = _softplus(-lam_ref[...])

        def gates(c, carry):
            _, xc, gi, gr, log_a = _lru_gates(xa_ref, c, cw_ref, cb_ref, wbd_ref, bx_ref, ba_ref, sp)
            rows = _rows(c, RC)
            a_s[rows, :] = jnp.exp(log_a)
            u_s[rows, :] = jnp.sqrt(-_expm1_nonpos(2.0 * log_a)) * (gi * xc)
            xc_ref[rows, :] = xc
            gi_ref[rows, :] = gi
            gr_ref[rows, :] = gr
            return carry

        lax.fori_loop(0, S // RC, gates, 0)

        def scan(g, h):
            for k in range(SCAN_GROUP):
                rows = pl.ds(pl.multiple_of(g * (8 * SCAN_GROUP), 8 * SCAN_GROUP) + 8 * k, 8)
                a_cum, u_cum = _tile_scan(a_s[rows, :], u_s[rows, :])
                hs_ref[rows, :] = u_cum + a_cum * h
                h = u_cum[7:8, :] + a_cum[7:8, :] * h
            return h

        lax.fori_loop(0, S // (8 * SCAN_GROUP), scan, jnp.zeros((1, CB), F32))

        def gate_out(c, carry):
            ga = ga_ref[_rows(c, RC), :].astype(F32)
            ya_ref[_rows(c, RC), :] = (ga * _sigmoid(ga) * hs_ref[_rows(c, RC), :]).astype(BF16)
            return carry

        lax.fori_loop(0, S // RC, gate_out, 0)

    vec = pl.BlockSpec((1, CB), lambda b, cb: (0, cb))
    blk = pl.BlockSpec((S, CB), lambda b, cb: (b, cb))
    return pl.pallas_call(
        body, name="lru_fwd", grid=(nb, NCB),
        in_specs=[pl.BlockSpec((None, S, CB), lambda b, cb: (0, b, cb)),
                  pl.BlockSpec((None, S, CB), lambda b, cb: (1, b, cb)),
                  pl.BlockSpec((4, CB), lambda b, cb: (0, cb)),
                  vec,
                  pl.BlockSpec((None, CB, 2 * CB), lambda b, cb: (cb, 0, 0)),
                  vec, vec, vec],
        out_specs=[blk] + [pl.BlockSpec((None, None, S, CB), lambda b, cb: (b, cb, 0, 0))] * 4,
        out_shape=[jax.ShapeDtypeStruct((t, D), BF16)] + [jax.ShapeDtypeStruct((nb, NCB, S, CB), F32)] * 4,
        scratch_shapes=[pltpu.VMEM((S, CB), F32), pltpu.VMEM((S, CB), F32)],
        compiler_params=_params(("arbitrary", "arbitrary")),
    )(proj, proj, conv_w, conv_b, wbd, bx, ba, lam)


def _lru_bwd(proj, hs, xc_f, gi_f, gr_f, dya, conv_w, wbd, lam, nb, give):
    t = nb * S
    ng = len(give)

    def body(xa_ref, ga_ref, hs_ref, xc_s, gi_s, gr_s, dya_ref, cw_ref, wbd_ref, lam_ref, *rest):
        give_refs, rest = rest[:ng], rest[ng:]
        dp_ref, dwbd_ref, vec_ref = rest[:3]
        got_refs, rest = rest[3:3 + ng], rest[3 + ng:]
        a_s, dl_s, dh_s, dxc_s, acc_s, send_sems, recv_sems = rest
        b = pl.program_id(1)
        exchange = _sibling_copies(give_refs, got_refs, send_sems, recv_sems)

        @pl.when(jnp.logical_and(pl.program_id(0) == 0, b == 0))
        def _():
            for cp in exchange:
                cp.start()

        lam_v = lam_ref[...]
        sp = _softplus(-lam_v)
        acc_s[...] = jnp.zeros_like(acc_s)

        @pl.when(b == 0)
        def _():
            dwbd_ref[...] = jnp.zeros_like(dwbd_ref)
            vec_ref[...] = jnp.zeros_like(vec_ref)

        def gates(c, carry):
            rows = _rows(c, RC)
            a_s[rows, :] = jnp.exp(-LRU_C * gr_s[rows, :] * sp)
            ga = ga_ref[rows, :].astype(F32)
            sg = _sigmoid(ga)
            dya_c = dya_ref[rows, :].astype(F32)
            dl_s[rows, :] = dya_c * (ga * sg)
            dp_ref[1, rows, :] = (dya_c * hs_ref[rows, :] * (sg * (1.0 + ga * (1.0 - sg)))).astype(BF16)
            return carry

        lax.fori_loop(0, S // RC, gates, 0)

        def scan(i, g_in):
            base = pl.multiple_of((S // (8 * SCAN_GROUP) - 1 - i) * (8 * SCAN_GROUP), 8 * SCAN_GROUP)
            row = lax.broadcasted_iota(jnp.int32, (8, CB), 0)
            for k in reversed(range(SCAN_GROUP)):
                rows = pl.ds(base + 8 * k, 8)
                a = a_s[rows, :]
                dl = dl_s[rows, :]
                a_cum, g_loc = _tile_scan_rev(a, a * dl)
                g = g_loc + a_cum * g_in
                dh_s[rows, :] = dl + jnp.where(row < 7, pltpu.roll(g, 7, 0), g_in)
                g_in = g_loc[0:1, :] + a_cum[0:1, :] * g_in
            return g_in

        lax.fori_loop(0, S // (8 * SCAN_GROUP), scan, jnp.zeros((1, CB), F32))

        dxc_s[pl.ds(S, 8), :] = jnp.zeros((8, CB), F32)

        def grads(c, carry):
            rows = _rows(c, RC)
            dh = dh_s[rows, :]
            h_prev = _shift_down(_window_before(hs_ref, c, RC), 1, RC)
            xc, gi, gr, a = xc_s[rows, :], gi_s[rows, :], gr_s[rows, :], a_s[rows, :]
            mult = jnp.sqrt(-_expm1_nonpos(-2.0 * LRU_C * gr * sp))
            dmult = dh * gi * xc
            d_log_a = dh * h_prev * a - dmult * (a * a) / mult
            dzi = dh * mult * xc * gi * (1.0 - gi)
            dzr = d_log_a * (-LRU_C * sp) * gr * (1.0 - gr)
            dz = jnp.concatenate([dzi, dzr], axis=1).astype(BF16)
            dxc_s[rows, :] = dh * mult * gi + _dot_nt(dz, wbd_ref[...])
            dwbd_ref[...] += _dot_tn(xc.astype(BF16), dz)
            acc_s[1:2, :] += jnp.sum(dzi, axis=0, keepdims=True)
            acc_s[2:3, :] += jnp.sum(dzr, axis=0, keepdims=True)
            acc_s[3:4, :] += jnp.sum(d_log_a * (-LRU_C * gr), axis=0, keepdims=True)
            return carry

        lax.fori_loop(0, S // RC, grads, 0, unroll=2)

        def conv_bwd(c, carry):
            rows = _rows(c, RC_CONV)
            dwin = dxc_s[pl.ds(pl.multiple_of(c * RC_CONV, RC_CONV), RC_CONV + 8), :]
            dxc = dwin[:RC_CONV, :]
            xwin = _window_before(xa_ref, c, RC_CONV)
            dxa = cw_ref[3:4, :] * dxc
            acc_s[0:1, :] += jnp.sum(dxc, axis=0, keepdims=True)
            acc_s[7:8, :] += jnp.sum(dxc * _shift_down(xwin, 0, RC_CONV), axis=0, keepdims=True)
            for s in (1, 2, 3):
                dxa = dxa + cw_ref[3 - s:4 - s, :] * _shift_up(dwin, s, RC_CONV)
                acc_s[7 - s:8 - s, :] += jnp.sum(dxc * _shift_down(xwin, s, RC_CONV), axis=0, keepdims=True)
            dp_ref[0, rows, :] = dxa.astype(BF16)
            return carry

        lax.fori_loop(0, S // RC_CONV, conv_bwd, 0)

        row = lax.broadcasted_iota(jnp.int32, acc_s.shape, 0)
        vec_ref[...] += jnp.where(row == 3, acc_s[...] * (-_sigmoid(-lam_v)), acc_s[...])

        @pl.when(jnp.logical_and(pl.program_id(0) == NCB - 1, b == nb - 1))
        def _():
            for cp in exchange:
                cp.wait()

    vec = pl.BlockSpec((1, CB), lambda cb, b: (0, cb))
    blk = pl.BlockSpec((S, CB), lambda cb, b: (b, cb))
    own = pl.BlockSpec((None, None, S, CB), lambda cb, b: (b, cb, 0, 0))
    return pl.pallas_call(
        body, name="lru_bwd", grid=(NCB, nb),
        in_specs=[pl.BlockSpec((None, S, CB), lambda cb, b: (0, b, cb)),
                  pl.BlockSpec((None, S, CB), lambda cb, b: (1, b, cb)),
                  own, own, own, own, blk,
                  pl.BlockSpec((4, CB), lambda cb, b: (0, cb)),
                  pl.BlockSpec((None, CB, 2 * CB), lambda cb, b: (cb, 0, 0)),
                  vec] + [ANY] * ng,
        out_specs=[pl.BlockSpec((2, S, CB), lambda cb, b: (0, b, cb)),
                   pl.BlockSpec((None, CB, 2 * CB), lambda cb, b: (cb, 0, 0)),
                   pl.BlockSpec((8, CB), lambda cb, b: (0, cb))] + [ANY] * ng,
        out_shape=[jax.ShapeDtypeStruct((2, t, D), BF16),
                   jax.ShapeDtypeStruct((NCB, CB, 2 * CB), F32),
                   jax.ShapeDtypeStruct((8, D), F32)]
        + [jax.ShapeDtypeStruct((4,) + g.shape[1:], g.dtype) for g in give],
        scratch_shapes=[pltpu.VMEM((S, CB), F32), pltpu.VMEM((S, CB), F32), pltpu.VMEM((S, CB), F32),
                        pltpu.VMEM((S + 8, CB), F32), pltpu.VMEM((8, CB), F32),
                        pltpu.SemaphoreType.DMA((ng, 4)), pltpu.SemaphoreType.DMA((ng, 4))],
        compiler_params=_params(("arbitrary", "arbitrary")),
    )(proj, proj, hs, xc_f, gi_f, gr_f, dya, conv_w, wbd, lam, *give)


def _retention_tables():
    f32 = np.float32
    log_g = np.log1p(-(f32(2.0) ** (f32(-5.0) - np.arange(HEADS, dtype=f32)))).astype(f32)
    idx = np.arange(CH, dtype=f32)
    diff = idx[:, None] - idx[None, :]
    inner = np.where(diff >= 0, np.exp(np.maximum(diff, f32(0.0))[None] * log_g[:, None, None]), f32(0.0)).astype(f32)
    cross = np.exp((idx[None, :] + f32(1.0)) * log_g[:, None]).astype(f32)
    state = np.exp((f32(CH - 1.0) - idx[None, :]) * log_g[:, None]).astype(f32)
    cross = np.ascontiguousarray(np.broadcast_to(cross[:, :, None], (HEADS, CH, DK)))
    state = np.ascontiguousarray(np.broadcast_to(state[:, :, None], (HEADS, CH, DK)))
    half = DK // 2
    freqs = (f32(10000.0) ** (-np.arange(half, dtype=f32) / f32(half))).astype(f32)
    ang = (np.arange(S, dtype=f32)[:, None] * freqs[None, :]).astype(f32)
    return tuple(jnp.asarray(a) for a in (inner, cross, state, np.cos(ang).astype(f32), np.sin(ang).astype(f32)))


def _rotate(x, cos, sin):
    half = DK // 2
    x1, x2 = x[:, :half], x[:, half:]
    return jnp.concatenate([x1 * cos - x2 * sin, x1 * sin + x2 * cos], axis=1)


def _rotate_back(d, cos, sin):
    half = DK // 2
    d1, d2 = d[:, :half], d[:, half:]
    return jnp.concatenate([d1 * cos + d2 * sin, d2 * cos - d1 * sin], axis=1)


def _ret_fwd(proj, gain, tables, nb, wp_own):
    t = nb * S
    inner_t, cross_t, state_t, cos_t, sin_t = tables

    def body(q_ref, k_ref, v_ref, gb_ref, gain_ref, dm_ref, cd_ref, sd_ref, cos_ref, sin_ref, wp_ref,
             yb_ref, qr_ref, kr_ref, o_ref, rs_ref, wpg_ref, r_s, send_sems, recv_sems, own_sems):
        b, hd = pl.program_id(0), pl.program_id(1)
        own, first, arrive, forward, others = _gather_copies([wp_ref], [wpg_ref], send_sems, recv_sems, own_sems)

        @pl.when(jnp.logical_and(b == 0, hd == 0))
        def _():
            for cp in own + first:
                cp.start()

        @pl.when(jnp.logical_and(b == nb - 1, hd == HEADS - 1))
        def _():
            for came, on in zip(arrive, forward):
                came.wait_recv()
                on.start()

        r_s[...] = jnp.zeros_like(r_s)
        chunk_decay = cd_ref[CH - 1:CH, :]

        def chunk(c, carry):
            rows = _rows(c, CH)
            cos, sin = cos_ref[rows, :], sin_ref[rows, :]
            qr = _rotate(q_ref[rows, :].astype(F32), cos, sin).astype(BF16)
            kr = (_rotate(k_ref[rows, :].astype(F32), cos, sin) * (DK ** -0.5)).astype(BF16)
            vb = v_ref[rows, :]
            v = vb.astype(F32)
            qr_ref[rows, :] = qr
            kr_ref[rows, :] = kr
            r = r_s[...]
            rb = r.astype(BF16)
            rs_ref[c] = rb
            p = (_dot_nt(qr, kr) * dm_ref[...]).astype(BF16)
            o = _dot(p, vb) + _dot(qr, rb) * cd_ref[...]
            r_s[...] = chunk_decay * r + _dot_tn(kr, (v * sd_ref[...]).astype(BF16))
            o_ref[rows, :] = o.astype(BF16)
            oc = o - jnp.mean(o, axis=-1, keepdims=True)
            rstd = lax.rsqrt(jnp.mean(oc * oc, axis=-1, keepdims=True) + EPS)
            gb = gb_ref[rows, :].astype(F32)
            yb_ref[rows, :] = (gb * _sigmoid(gb) * (oc * rstd * gain_ref[...])).astype(BF16)
            return carry

        lax.fori_loop(0, NCH, chunk, 0, unroll=2)

        @pl.when(jnp.logical_and(b == nb - 1, hd == HEADS - 1))
        def _():
            for cp in others:
                cp.wait_recv()
            for cp in first + forward:
                cp.wait_send()
            for cp in own:
                cp.wait()

    seg = lambda s: pl.BlockSpec((None, S, DK), lambda b, h: (s, b, h))
    tab = pl.BlockSpec((None, CH, DK), lambda b, h: (h, 0, 0))
    rot = pl.BlockSpec((S, DK // 2), lambda b, h: (0, 0))
    blk = pl.BlockSpec((S, DK), lambda b, h: (b, h))
    return pl.pallas_call(
        body, name="ret_fwd", grid=(nb, HEADS),
        in_specs=[seg(2), seg(3), seg(4), seg(5),
                  pl.BlockSpec((None, 1, DK), lambda b, h: (h, 0, 0)),
                  tab, tab, tab, rot, rot, ANY],
        out_specs=[blk, blk, blk, blk,
                   pl.BlockSpec((None, None, NCH, DK, DK), lambda b, h: (b, h, 0, 0, 0)), ANY],
        out_shape=[jax.ShapeDtypeStruct((t, D), BF16), jax.ShapeDtypeStruct((t, D), BF16),
                   jax.ShapeDtypeStruct((t, D), BF16), jax.ShapeDtypeStruct((t, D), BF16),
                   jax.ShapeDtypeStruct((nb, HEADS, NCH, DK, DK), BF16),
                   jax.ShapeDtypeStruct((NDEV,) + wp_own.shape, wp_own.dtype)],
        scratch_shapes=[pltpu.VMEM((DK, DK), F32),
                        pltpu.SemaphoreType.DMA((1, 7)), pltpu.SemaphoreType.DMA((1, 7)), pltpu.SemaphoreType.DMA((1,))],
        compiler_params=_params(("arbitrary", "arbitrary")),
    )(proj, proj, proj, proj, gain, inner_t, cross_t, state_t, cos_t, sin_t, wp_own)


def _ret_bwd(proj, qr, kr, o, rs, dyb, gain, tables, nb, sums):
    t = nb * S
    ns = len(sums)
    inner_t, cross_t, state_t, cos_t, sin_t = tables

    def body(qr_ref, kr_ref, v_ref, gb_ref, o_ref, dyb_ref, rs_ref, gain_ref, dm_ref, cd_ref, sd_ref,
             cos_ref, sin_ref, *rest):
        sum_refs, rest = rest[:ns], rest[ns:]
        dp_ref, dgain_ref = rest[:2]
        part_refs, rest = rest[2:2 + ns], rest[2 + ns:]
        dr_s, send_sems, recv_sems, local_sems = rest
        mine, sends, recvs = _chip_copies(sum_refs, part_refs, send_sems, recv_sems, local_sems)

        @pl.when(jnp.logical_and(pl.program_id(0) == 0, pl.program_id(1) == 0))
        def _():
            for cp in mine + sends:
                cp.start()

        dr_s[...] = jnp.zeros_like(dr_s)
        chunk_decay = cd_ref[CH - 1:CH, :]

        @pl.when(pl.program_id(1) == 0)
        def _():
            dgain_ref[...] = jnp.zeros_like(dgain_ref)

        def chunk(i, carry):
            c = NCH - 1 - i
            rows = _rows(c, CH)
            gain_v = gain_ref[...]
            o_c = o_ref[rows, :].astype(F32)
            oc = o_c - jnp.mean(o_c, axis=-1, keepdims=True)
            rstd = lax.rsqrt(jnp.mean(oc * oc, axis=-1, keepdims=True) + EPS)
            yn = oc * rstd
            gb = gb_ref[rows, :].astype(F32)
            sg = _sigmoid(gb)
            dyb_c = dyb_ref[rows, :].astype(F32)
            dgn = dyb_c * (gb * sg)
            dp_ref[3, rows, :] = (dyb_c * (yn * gain_v) * (sg * (1.0 + gb * (1.0 - sg)))).astype(BF16)
            dgain_ref[...] += jnp.sum(dgn * yn, axis=0, keepdims=True)
            dyn = dgn * gain_v
            do = rstd * (dyn - jnp.mean(dyn, axis=-1, keepdims=True)
                         - yn * jnp.mean(dyn * yn, axis=-1, keepdims=True))
            dob = do.astype(BF16)
            dox = (do * cd_ref[...]).astype(BF16)

            q_c, k_c = qr_ref[rows, :], kr_ref[rows, :]
            vb = v_ref[rows, :]
            v = vb.astype(F32)
            vs = (v * sd_ref[...]).astype(BF16)
            rb = rs_ref[c]
            d_r = dr_s[...]
            drb = d_r.astype(BF16)
            dm = dm_ref[...]
            p = (_dot_nt(q_c, k_c) * dm).astype(BF16)
            dpm = (_dot_nt(dob, vb) * dm).astype(BF16)
            dq = _dot(dpm, k_c) + _dot_nt(dox, rb)
            dk = _dot_tn(dpm, q_c) + _dot_nt(vs, drb)
            dv = _dot_tn(p, dob) + _dot(k_c, drb) * sd_ref[...]
            dr_s[...] = chunk_decay * d_r + _dot_tn(q_c, dox)

            cos, sin = cos_ref[rows, :], sin_ref[rows, :]
            dp_ref[0, rows, :] = _rotate_back(dq, cos, sin).astype(BF16)
            dp_ref[1, rows, :] = (_rotate_back(dk, cos, sin) * (DK ** -0.5)).astype(BF16)
            dp_ref[2, rows, :] = dv.astype(BF16)
            return carry

        lax.fori_loop(0, NCH, chunk, 0, unroll=2)

        @pl.when(jnp.logical_and(pl.program_id(0) == HEADS - 1, pl.program_id(1) == nb - 1))
        def _():
            for cp in recvs:
                cp.wait_recv()
            for cp in sends:
                cp.wait_send()
            for cp in mine:
                cp.wait()

    seg = lambda s: pl.BlockSpec((None, S, DK), lambda h, b: (s, b, h))
    tab = pl.BlockSpec((None, CH, DK), lambda h, b: (h, 0, 0))
    rot = pl.BlockSpec((S, DK // 2), lambda h, b: (0, 0))
    blk = pl.BlockSpec((S, DK), lambda h, b: (b, h))
    one = pl.BlockSpec((None, 1, DK), lambda h, b: (h, 0, 0))
    return pl.pallas_call(
        body, name="ret_bwd", grid=(HEADS, nb),
        in_specs=[blk, blk, seg(4), seg(5), blk, blk,
                  pl.BlockSpec((None, None, NCH, DK, DK), lambda h, b: (b, h, 0, 0, 0)),
                  one, tab, tab, tab, rot, rot] + [ANY] * ns,
        out_specs=[pl.BlockSpec((4, S, DK), lambda h, b: (0, b, h)), one] + [ANY] * ns,
        out_shape=[jax.ShapeDtypeStruct((4, t, D), BF16), jax.ShapeDtypeStruct((HEADS, 1, DK), F32)]
        + [jax.ShapeDtypeStruct(a.shape, a.dtype) for a in sums],
        scratch_shapes=[pltpu.VMEM((DK, DK), F32), pltpu.SemaphoreType.DMA((ns, 3)), pltpu.SemaphoreType.DMA((ns, 3)),
                        pltpu.SemaphoreType.DMA((ns,))],
        compiler_params=_params(("arbitrary", "arbitrary")),
    )(qr, kr, proj, proj, o, dyb, rs, gain, inner_t, cross_t, state_t, cos_t, sin_t, *sums)


def _wblock(k):
    return pl.BlockSpec((NDEV, D // NDEV, D), lambda i: (0, k, 0))


def _tail(ya, yb, proj, x2d, tgt, wg, g_fin):
    t = x2d.shape[0]
    tm = 256

    def body(ya_ref, yb_ref, ma_ref, mb_ref, x_ref, t_ref, wa_ref, wb_ref, wo_ref, g_ref,
             dx2_ref, dya_ref, dyb_ref, dm_ref, mg_ref, doa_ref, dob_ref, gfin_ref, loss_ref):
        i = pl.program_id(0)

        @pl.when(i == 0)
        def _():
            gfin_ref[...] = jnp.zeros_like(gfin_ref)
            loss_ref[...] = jnp.zeros_like(loss_ref)

        wa = wa_ref[...].reshape(D, D)
        wb = wb_ref[...].reshape(D, D)
        wo = wo_ref[...].reshape(D, D)
        out_a = _dot(ya_ref[...], wa)
        out_b = _dot(yb_ref[...], wb)
        sa = _sigmoid(ma_ref[...].astype(F32))
        sb = _sigmoid(mb_ref[...].astype(F32))
        merged = (sa * out_a + sb * out_b).astype(BF16)
        mg_ref[...] = merged
        x2 = x_ref[...] + _dot(merged, wo)
        r2 = lax.rsqrt(jnp.mean(x2 * x2, axis=-1, keepdims=True) + EPS)
        xh = x2 * r2
        g = g_ref[...]
        err = xh * g - t_ref[...]
        loss_ref[...] += jnp.sum(err * err, axis=0, keepdims=True) * (0.5 / D)
        dy = err * (1.0 / D)
        gfin_ref[...] += jnp.sum(dy * xh, axis=0, keepdims=True)
        dxh = dy * g
        dx2 = r2 * (dxh - xh * jnp.mean(dxh * xh, axis=-1, keepdims=True))
        dx2_ref[...] = dx2
        dmerged = _dot_nt(dx2.astype(BF16), wo)
        doa = (sa * dmerged).astype(BF16)
        dob = (sb * dmerged).astype(BF16)
        doa_ref[...] = doa
        dob_ref[...] = dob
        dm_ref[0] = (dmerged * out_a * sa * (1.0 - sa)).astype(BF16)
        dm_ref[1] = (dmerged * out_b * sb * (1.0 - sb)).astype(BF16)
        dya_ref[...] = _dot_nt(doa, wa).astype(BF16)
        dyb_ref[...] = _dot_nt(dob, wb).astype(BF16)

    row = lambda: pl.BlockSpec((tm, D), lambda i: (i, 0))
    seg = lambda s: pl.BlockSpec((None, tm, D), lambda i: (s, i, 0))
    vec = pl.BlockSpec((1, D), lambda i: (0, 0))
    return pl.pallas_call(
        body, name="tail", grid=(t // tm,),
        in_specs=[row(), row(), seg(6), seg(7), row(), row(), _wblock(0), _wblock(1), _wblock(2), vec],
        out_specs=[row(), row(), row(), pl.BlockSpec((2, tm, D), lambda i: (0, i, 0)),
                   row(), row(), row(), vec, vec],
        out_shape=[jax.ShapeDtypeStruct((t, D), F32), jax.ShapeDtypeStruct((t, D), BF16),
                   jax.ShapeDtypeStruct((t, D), BF16), jax.ShapeDtypeStruct((2, t, D), BF16),
                   jax.ShapeDtypeStruct((t, D), BF16), jax.ShapeDtypeStruct((t, D), BF16),
                   jax.ShapeDtypeStruct((t, D), BF16), jax.ShapeDtypeStruct((1, D), F32),
                   jax.ShapeDtypeStruct((1, D), F32)],
        compiler_params=_params(("arbitrary",)),
    )(ya, yb, proj, proj, x2d, tgt, wg, wg, wg, g_fin)


def _tail_wgrad(ya, yb, merged, doa, dob, dx2):
    t = ya.shape[0]
    tm = 512

    def body(ya_ref, yb_ref, mg_ref, doa_ref, dob_ref, dx2_ref, ga_ref, gb_ref, go_ref):
        @pl.when(pl.program_id(0) == 0)
        def _():
            ga_ref[...] = jnp.zeros_like(ga_ref)
            gb_ref[...] = jnp.zeros_like(gb_ref)
            go_ref[...] = jnp.zeros_like(go_ref)

        ga_ref[...] += _dot_tn(ya_ref[...], doa_ref[...])
        gb_ref[...] += _dot_tn(yb_ref[...], dob_ref[...])
        go_ref[...] += _dot_tn(mg_ref[...], dx2_ref[...].astype(BF16))

    row = lambda: pl.BlockSpec((tm, D), lambda i: (i, 0))
    full = lambda: pl.BlockSpec((D, D), lambda i: (0, 0))
    return pl.pallas_call(
        body, name="tail_wgrad", grid=(t // tm,),
        in_specs=[row() for _ in range(6)], out_specs=[full(), full(), full()],
        out_shape=[jax.ShapeDtypeStruct((D, D), F32)] * 3,
        compiler_params=_params(("arbitrary",)),
    )(ya, yb, merged, doa, dob, dx2)


def _dproj_specs_ordered(tm):
    def spec(lo, n):
        def index(k, i, order_ref):
            seg = order_ref[k]
            mine = jnp.logical_and(seg >= lo, seg < lo + n)
            return jnp.where(mine, seg - lo, 0), jnp.where(mine, i, 0), 0
        return pl.BlockSpec((None, tm, D), index)
    return [spec(0, 2), spec(2, 4), spec(6, 2)]


def _dproj_pick(j, da_ref, db_ref, dc_ref, use):
    @pl.when(j < 2)
    def _():
        use(da_ref[...])

    @pl.when(jnp.logical_and(j >= 2, j < 6))
    def _():
        use(db_ref[...])

    @pl.when(j >= 6)
    def _():
        use(dc_ref[...])


RS_X, RS_Y, RS_XY = 0, 1, 2
RS_ROLES = ((RS_XY, RS_X, RS_Y), (RS_Y, RS_XY, RS_X))


def _rs_flip(rel, x, y):
    return ((1 - x, y), (x, 1 - y), (1 - x, 1 - y))[rel]


def _rs_order(x, y, c):
    order = []
    for s in range(4):
        chip = []
        for core in (0, 1):
            px, py = _rs_flip(RS_ROLES[core][s], x, y) if s < 3 else (x, y)
            chip.append(2 * px + py)
        keep = jnp.where(c == 0, chip[0], chip[1])
        give = jnp.where(c == 0, chip[1], chip[0])
        order += [2 * give + 1 - c, 2 * keep + c]
    return jnp.stack(order).astype(jnp.int32)


def _inproj_wgrad_rs(h, dpa, dpb, dpc, order, smalls):
    t = h.shape[0]
    tm = 1024
    nt = t // tm
    nsm = len(smalls)

    def body(order_ref, h_ref, da_ref, db_ref, dc_ref, *rest):
        small_refs, parts_ref, rest = rest[:nsm], rest[nsm], rest[nsm + 1:]
        all_refs, rest = rest[:nsm], rest[nsm:]
        (acc, sib, outb, far, give_send, give_recv, sum_send, sum_recv, far_send, far_recv, own_sem,
         small_send, small_recv, small_own) = rest
        k, i = pl.program_id(0), pl.program_id(1)
        x, y, c = _place()
        own, first, arrive, forward, others = _gather_copies(small_refs, all_refs, small_send, small_recv, small_own)

        @pl.when(jnp.logical_and(k == 0, i == 0))
        def _():
            for cp in own + first:
                cp.start()

        @pl.when(jnp.logical_and(k == 2, i == 0))
        def _():
            for came, on in zip(arrive, forward):
                came.wait_recv()
                on.start()

        def use(d):
            @pl.when(i == 0)
            def _():
                acc[k % 2] = _dot_tn(h_ref[...], d)

            @pl.when(i > 0)
            def _():
                acc[k % 2] += _dot_tn(h_ref[...], d)

        _dproj_pick(order_ref[k], da_ref, db_ref, dc_ref, use)

        def give_copy(s):
            return pltpu.make_async_remote_copy(
                src_ref=acc.at[0], dst_ref=sib.at[s % 2], send_sem=give_send.at[s], recv_sem=give_recv.at[s],
                device_id=(x, y, 1 - c), device_id_type=MESH)

        def sum_copy(s, core):
            slot = 0 if s < 2 else 1
            return pltpu.make_async_remote_copy(
                src_ref=outb.at[s], dst_ref=parts_ref.at[slot], send_sem=sum_send.at[slot], recv_sem=sum_recv.at[slot],
                device_id=(*_rs_flip(RS_ROLES[core][s], x, y), core), device_id_type=MESH)

        def far_copy(s, core):
            return pltpu.make_async_remote_copy(
                src_ref=outb.at[s], dst_ref=far, send_sem=far_send, recv_sem=far_recv,
                device_id=(*_rs_flip(RS_X if core == 0 else RS_Y, x, y), core), device_id_type=MESH)

        own_copy = pltpu.make_async_copy(outb.at[3], parts_ref.at[2], own_sem)

        def send_of(core, s):
            return far_copy(s, core) if RS_ROLES[core][s] == RS_XY else sum_copy(s, core)

        for s in range(4):
            @pl.when(jnp.logical_and(k == 2 * s, i == nt - 1))
            def _():
                give_copy(s).start()

            @pl.when(jnp.logical_and(k == 2 * s + 1, i == nt - 1))
            def _():
                give_copy(s).wait_recv()
                if s == 2:
                    far_copy(s, 0).wait_recv()
                    outb[s] = (acc[1] + sib[s % 2] + far[...].astype(F32)).astype(BF16)
                else:
                    outb[s] = (acc[1] + sib[s % 2]).astype(BF16)
                give_copy(s).wait_send()
                if s < 3:
                    for core in (0, 1):
                        @pl.when(c == core)
                        def _():
                            send_of(core, s).start()
                else:
                    own_copy.start()

        @pl.when(jnp.logical_and(k == NSEG - 1, i == nt - 1))
        def _():
            for slot in (0, 1):
                sum_copy(2 * slot, 0).wait_recv()
            for s in range(3):
                send_of(0, s).wait_send()
            own_copy.wait()
            for cp in others:
                cp.wait_recv()
            for cp in first + forward:
                cp.wait_send()
            for cp in own:
                cp.wait()

    return pl.pallas_call(
        body, name="inproj_wgrad_rs",
        grid_spec=pltpu.PrefetchScalarGridSpec(
            num_scalar_prefetch=1, grid=(NSEG, nt),
            in_specs=[pl.BlockSpec((tm, D), lambda k, i, order_ref: (i, 0))] + _dproj_specs_ordered(tm) + [ANY] * nsm,
            out_specs=[ANY] * (1 + nsm),
            scratch_shapes=[pltpu.VMEM((2, D, D), F32), pltpu.VMEM((2, D, D), F32), pltpu.VMEM((4, D, D), BF16),
                            pltpu.VMEM((D, D), BF16),
                            pltpu.SemaphoreType.DMA((4,)), pltpu.SemaphoreType.DMA((4,)),
                            pltpu.SemaphoreType.DMA((2,)), pltpu.SemaphoreType.DMA((2,)),
                            pltpu.SemaphoreType.DMA, pltpu.SemaphoreType.DMA, pltpu.SemaphoreType.DMA,
                            pltpu.SemaphoreType.DMA((nsm, 7)), pltpu.SemaphoreType.DMA((nsm, 7)),
                            pltpu.SemaphoreType.DMA((nsm,))]),
        out_shape=[jax.ShapeDtypeStruct((3, D, D), BF16)]
        + [jax.ShapeDtypeStruct((NDEV,) + a.shape, a.dtype) for a in smalls],
        compiler_params=_params(("arbitrary", "arbitrary")),
    )(order, h, dpa, dpb, dpc, *smalls)


def _inproj_dgrad(dpa, dpb, dpc, wg, x2d, dx2, g_in):
    t = x2d.shape[0]
    tm = 512

    def body(da_ref, db_ref, dc_ref, w_hbm, x_ref, dx2_ref, g_ref, gx_ref, gg_ref, w_s, w_sem):
        i = pl.program_id(0)

        @pl.when(i == 0)
        def _():
            gg_ref[...] = jnp.zeros_like(gg_ref)
            load = pltpu.make_async_copy(w_hbm, w_s, w_sem)
            load.start()
            load.wait()

        dh = None
        for ref, lo in ((da_ref, 0), (db_ref, 2), (dc_ref, 6)):
            for k in range(ref.shape[0]):
                part = _dot_nt(ref[k], w_s[lo + k])
                dh = part if dh is None else dh + part
        x = x_ref[...]
        r = lax.rsqrt(jnp.mean(x * x, axis=-1, keepdims=True) + EPS)
        xh = x * r
        gg_ref[...] += jnp.sum(dh * xh, axis=0, keepdims=True)
        dxh = dh * g_ref[...]
        gx_ref[...] = dx2_ref[...] + r * (dxh - xh * jnp.mean(dxh * xh, axis=-1, keepdims=True))

    row = lambda: pl.BlockSpec((tm, D), lambda i: (i, 0))
    seg = lambda n: pl.BlockSpec((n, tm, D), lambda i: (0, i, 0))
    vec = pl.BlockSpec((1, D), lambda i: (0, 0))
    return pl.pallas_call(
        body, name="inproj_dgrad", grid=(t // tm,),
        in_specs=[seg(2), seg(4), seg(2), ANY, row(), row(), vec],
        out_specs=[row(), vec],
        out_shape=[jax.ShapeDtypeStruct((t, D), F32), jax.ShapeDtypeStruct((1, D), F32)],
        scratch_shapes=[pltpu.VMEM((NSEG, D, D), BF16), pltpu.SemaphoreType.DMA],
        compiler_params=_params(("arbitrary",)),
    )(dpa, dpb, dpc, wg, x2d, dx2, g_in)


def _adam_update(g, w, m, v):
    m_new = ADAM_B1 * m + (1.0 - ADAM_B1) * g
    v_new = ADAM_B2 * v + (1.0 - ADAM_B2) * (g * g)
    m_hat = m_new / (1.0 - ADAM_B1 ** ADAM_STEP)
    v_hat = v_new / (1.0 - ADAM_B2 ** ADAM_STEP)
    return -ADAM_LR * (m_hat / (jnp.sqrt(v_hat) + ADAM_EPS) + ADAM_WD * w), m_new, v_new


def _sum_in_order(ref):
    total = ref[0].astype(F32)
    for k in range(1, ref.shape[0]):
        total = total + ref[k].astype(F32)
    return total


def _sum_devices(arrs):
    def body(*refs):
        for a in range(len(arrs)):
            refs[len(arrs) + a][...] = _sum_in_order(refs[a])

    return pl.pallas_call(
        body, name="sum_devices",
        out_shape=[jax.ShapeDtypeStruct(a.shape[1:], F32) for a in arrs],
        compiler_params=_params(),
    )(*arrs)


def _adamw_small(me, vec_all, gx_all, ga_all, groups, items):
    flat = [a for grp in groups for a in grp]
    iflat = [a for item in items for a in item]
    ng, ni = len(groups), len(items)
    nshard = D // NDEV

    def body(me_ref, vec_ref, shard_ref, gx_ref, ga_ref, *refs):
        ins, iins, refs = refs[:3 * ng], refs[3 * ng:3 * ng + 4 * ni], refs[3 * ng + 4 * ni:]
        outs, iouts = refs[:4 * ng + 1], refs[4 * ng + 1:]
        for a in range(ni):
            p_ref, w_ref, m_ref, v_ref = iins[4 * a:4 * a + 4]
            g = _sum_in_order(p_ref)
            delta, m_new, v_new = _adam_update(g, w_ref[...], m_ref[...], v_ref[...])
            for o, val in zip(iouts[4 * a:4 * a + 4], (g, delta, m_new, v_new)):
                o[...] = val
        vec = _sum_in_order(vec_ref)
        shard = _sum_in_order(shard_ref)
        grads = [vec[r:r + 1, :] for r in range(6)]
        grads += [shard[0:4, :], shard[4:8, 0:DK // NDEV], _sum_in_order(gx_ref), _sum_in_order(ga_ref)]
        for n, g in enumerate(grads):
            delta, m_new, v_new = _adam_update(g, ins[3 * n][...], ins[3 * n + 1][...], ins[3 * n + 2][...])
            outs[4 * n][...] = g
            outs[4 * n + 1][...] = delta
            outs[4 * n + 2][...] = m_new
            outs[4 * n + 3][...] = v_new
        outs[4 * ng][...] = jnp.sum(vec[6:7, :], axis=1, keepdims=True)

    full = lambda a: pl.BlockSpec(a.shape, lambda i, me_ref, nd=len(a.shape): (0,) * nd)
    out_shape = [jax.ShapeDtypeStruct(w.shape, F32) for w, _, _ in groups for _ in range(4)]
    out_shape.append(jax.ShapeDtypeStruct((1, 1), F32))
    out_shape += [jax.ShapeDtypeStruct(w.shape, F32) for _, w, _, _ in items for _ in range(4)]
    outs = pl.pallas_call(
        body, name="adamw_small",
        grid_spec=pltpu.PrefetchScalarGridSpec(
            num_scalar_prefetch=1, grid=(1,),
            in_specs=[full(vec_all),
                      pl.BlockSpec((NDEV, 8, nshard), lambda i, me_ref: (0, 1, me_ref[0])),
                      full(gx_all), full(ga_all)] + [full(a) for a in flat + iflat],
            out_specs=[full(s) for s in out_shape]),
        out_shape=out_shape,
        compiler_params=_params(("arbitrary",)),
    )(me, vec_all, vec_all, gx_all, ga_all, *flat, *iflat)
    rest = outs[4 * ng + 1:]
    return [outs[4 * n:4 * n + 4] for n in range(ng)], outs[4 * ng], [rest[4 * a:4 * a + 4] for a in range(ni)]


def _adamw(name, items):
    n, rows, cols = items[0][0].shape
    tr = rows if rows <= 256 else 256
    k = len(items)

    def body(*refs):
        for a in range(k):
            p_ref, w_ref, m_ref, v_ref = refs[4 * a:4 * a + 4]
            g = _sum_in_order(p_ref)
            delta, m_new, v_new = _adam_update(g, w_ref[...], m_ref[...], v_ref[...])
            for o, val in zip(refs[4 * k + 4 * a:4 * k + 4 * a + 4], (g, delta, m_new, v_new)):
                o[...] = val

    blk = lambda: pl.BlockSpec((tr, cols), lambda i: (i, 0))
    outs = pl.pallas_call(
        body, name=name, grid=(rows // tr,),
        in_specs=[pl.BlockSpec((n, tr, cols), lambda i: (0, i, 0)), blk(), blk(), blk()] * k,
        out_specs=[blk() for _ in range(4 * k)],
        out_shape=[jax.ShapeDtypeStruct((rows, cols), F32)] * (4 * k),
        compiler_params=_params(("arbitrary",)),
    )(*[a for item in items for a in item])
    return [outs[4 * a:4 * a + 4] for a in range(k)]


ANY = pl.BlockSpec(memory_space=pl.ANY)


def _place():
    return lax.axis_index("x"), lax.axis_index("y"), lax.axis_index("c")


def _gather_copies(ins, outs, send_sems, recv_sems, own_sems):
    x, y, c = _place()
    me, sibling = (x, y, c), (x, y, 1 - c)
    chips = [(1 - x, y), (x, 1 - y), (1 - x, 1 - y)]
    n = len(ins)

    def copy(a, k, block, to, src=None):
        px, py, pc = block
        dst = outs[a].at[4 * px + 2 * py + pc]
        return pltpu.make_async_remote_copy(
            src_ref=dst if src is None else src, dst_ref=dst,
            send_sem=send_sems.at[a, k], recv_sem=recv_sems.at[a, k], device_id=to, device_id_type=MESH)

    own = [pltpu.make_async_copy(ins[a], outs[a].at[4 * x + 2 * y + c], own_sems.at[a]) for a in range(n)]
    first = []
    for a in range(n):
        first.append(copy(a, 0, me, sibling, src=ins[a]))
        first += [copy(a, 1 + j, me, (*chip, c), src=ins[a]) for j, chip in enumerate(chips)]
    arrive = [copy(a, 1 + j, (*chip, c), me) for j, chip in enumerate(chips) for a in range(n)]
    forward = [copy(a, 4 + j, (*chip, c), sibling) for j, chip in enumerate(chips) for a in range(n)]
    rest = [copy(a, 0, sibling, me) for a in range(n)]
    rest += [copy(a, 4 + j, (*chip, 1 - c), me) for a in range(n) for j, chip in enumerate(chips)]
    return own, first, arrive, forward, rest


def _sibling_copies(ins, outs, send_sems, recv_sems):
    x, y, c = _place()
    return [pltpu.make_async_remote_copy(
        src_ref=ins[a].at[2 * q + 1 - c], dst_ref=outs[a].at[q],
        send_sem=send_sems.at[a, q], recv_sem=recv_sems.at[a, q],
        device_id=(x, y, 1 - c), device_id_type=MESH) for a in range(len(ins)) for q in range(4)]


def _chip_copies(ins, outs, send_sems, recv_sems, local_sems):
    x, y, c = _place()
    my_chip = 2 * x + y
    chips = [(1 - x, y), (x, 1 - y), (1 - x, 1 - y)]
    n = len(ins)
    mine = [pltpu.make_async_copy(ins[a].at[my_chip], outs[a].at[my_chip], local_sems.at[a]) for a in range(n)]
    sends = [pltpu.make_async_remote_copy(
        src_ref=ins[a].at[2 * px + py], dst_ref=outs[a].at[my_chip],
        send_sem=send_sems.at[a, j], recv_sem=recv_sems.at[a, j],
        device_id=(px, py, c), device_id_type=MESH) for a in range(n) for j, (px, py) in enumerate(chips)]
    recvs = [pltpu.make_async_remote_copy(
        src_ref=ins[a].at[my_chip], dst_ref=outs[a].at[2 * px + py],
        send_sem=send_sems.at[a, j], recv_sem=recv_sems.at[a, j],
        device_id=(px, py, c), device_id_type=MESH) for a in range(n) for j, (px, py) in enumerate(chips)]
    return mine, sends, recvs


def _chip_sum(owns, gots, core):
    n = len(owns)
    _, rows, cols = owns[0].shape

    def body(core_ref, *refs):
        for a in range(n):
            refs[2 * n + a][...] = (refs[a][...] + refs[n + a][...]).astype(BF16)

    own_spec = pl.BlockSpec((None, rows, cols), lambda q, core_ref: (2 * q + core_ref[0], 0, 0))
    slab = pl.BlockSpec((None, rows, cols), lambda q, core_ref: (q, 0, 0))
    return pl.pallas_call(
        body, name="chip_sum",
        grid_spec=pltpu.PrefetchScalarGridSpec(
            num_scalar_prefetch=1, grid=(4,),
            in_specs=[own_spec] * n + [slab] * n, out_specs=[slab] * n),
        out_shape=[jax.ShapeDtypeStruct((4, rows, cols), BF16)] * n,
        compiler_params=_params(("arbitrary",)),
    )(core, *owns, *gots)


def _block_diag(w):
    w4 = w.reshape(NCB, 4, 64, 64)
    eye = jnp.eye(4, dtype=w.dtype)
    return (w4[:, :, :, None, :] * eye[None, :, None, :, None]).reshape(NCB, CB, CB)


def _block_diag_back(g):
    g5 = g.reshape(NCB, 4, 64, 4, 64)
    return jnp.stack([g5[:, m, :, m, :] for m in range(4)], axis=1).reshape(16, 64, 64)


def kernel(x, norm_in, w_in, conv_w, conv_b, gate_x_w, gate_x_b, gate_a_w, gate_a_b, lru_lambda, gn_gain, w_proj_a, w_proj_b, w_out, norm_final, loss_target, m_norm_in, m_w_in, m_conv_w, m_conv_b, m_gate_x_w, m_gate_x_b, m_gate_a_w, m_gate_a_b, m_lru_lambda, m_gn_gain, m_w_proj_a, m_w_proj_b, m_w_out, m_norm_final, v_norm_in, v_w_in, v_conv_w, v_conv_b, v_gate_x_w, v_gate_x_b, v_gate_a_w, v_gate_a_b, v_lru_lambda, v_gn_gain, v_w_proj_a, v_w_proj_b, v_w_out, v_norm_final):
    xi, yi, ci = _place()
    me = 4 * xi + 2 * yi + ci
    core = ci.astype(jnp.int32).reshape(1)
    nshard = D // NDEV
    nb = x.shape[0]
    t = nb * S
    x2d = x.reshape(t, D)
    tgt2d = loss_target.reshape(t, D)
    g_final = norm_final.reshape(1, D)
    wbd = jnp.concatenate([_block_diag(gate_x_w[0]), _block_diag(gate_a_w[0])], axis=-1).astype(BF16)
    tables = _retention_tables()

    wp_own = jnp.concatenate([w_proj_a[0], w_proj_b[0], w_out[0]], axis=0).astype(BF16)
    tiny = jnp.concatenate([conv_w[0], jnp.pad(gn_gain[0], ((0, 0), (0, nshard - DK // NDEV)))], axis=0)
    proj, h, wg, tiny_g = _inproj_gather(x2d, norm_in, w_in[0].astype(BF16), tiny, *_gather_order(xi, yi, ci))
    conv_w_full = tiny_g[:, 0:4, :].transpose(1, 0, 2).reshape(4, D)
    gain3 = tiny_g[:, 4:8, :DK // NDEV].transpose(1, 0, 2).reshape(HEADS, 1, DK)

    ya, hs, xc, gi, gr = _lru_fwd(proj, conv_w_full, conv_b, wbd, gate_x_b, gate_a_b, lru_lambda, nb)
    yb, qr, kr, o, rs, wpg = _ret_fwd(proj, gain3, tables, nb, wp_own)
    dx2, dya, dyb, dpc, merged, doa, dob, g_fin, loss_vec = _tail(ya, yb, proj, x2d, tgt2d, wpg, g_final)
    g_pa, g_pb, g_out = _tail_wgrad(ya, yb, merged, doa, dob, dx2)

    own = [g.reshape(NDEV, nshard, D) for g in (g_pa, g_pb, g_out)]
    dpa, g_wbd, g_vec, *got = _lru_bwd(proj, hs, xc, gi, gr, dya, conv_w_full, wbd, lru_lambda, nb, own)
    sums = _chip_sum(own, got, core)
    dpb, g_gain, *parts = _ret_bwd(proj, qr, kr, o, rs, dyb, gain3, tables, nb, sums)

    grad_x, g_norm_in = _inproj_dgrad(dpa, dpb, dpc, wg, x2d, dx2, norm_in)
    grad_x = grad_x.reshape(nb, S, D)

    gain_rows = jnp.pad(g_gain.reshape(HEADS, NDEV, DK // NDEV), ((0, 0), (0, 0), (0, nshard - DK // NDEV)))
    vec = jnp.concatenate([g_norm_in, g_vec[0:4], g_fin, loss_vec, jnp.zeros((1, D), F32), g_vec[4:8],
                           gain_rows.reshape(HEADS, D)], axis=0)
    g_gx = _block_diag_back(g_wbd[:, :, :CB]).reshape(D // 2, 128)
    g_ga = _block_diag_back(g_wbd[:, :, CB:]).reshape(D // 2, 128)
    parts_in, vec_all, gx_all, ga_all = _inproj_wgrad_rs(h, dpa, dpb, dpc, _rs_order(xi, yi, ci),
                                                         [vec, g_gx, g_ga])
    gx_all, ga_all = [g.reshape(1, D, 64) for g in _sum_devices([gx_all, ga_all])]
    parts = [parts_in] + list(parts)

    res = {}
    (out,) = _adamw("adamw_w_in", [(parts[0], w_in[0], m_w_in[0], v_w_in[0])])
    res["w_in"] = [o[None] for o in out]
    square = [("w_proj_a", w_proj_a, m_w_proj_a, v_w_proj_a), ("w_proj_b", w_proj_b, m_w_proj_b, v_w_proj_b),
              ("w_out", w_out, m_w_out, v_w_out)]

    row = lambda a: a.reshape(1, D)
    gate = lambda a: a.reshape(D, 64)
    groups = [("norm_in", norm_in, m_norm_in, v_norm_in, row), ("conv_b", conv_b, m_conv_b, v_conv_b, row),
              ("gate_x_b", gate_x_b, m_gate_x_b, v_gate_x_b, row), ("gate_a_b", gate_a_b, m_gate_a_b, v_gate_a_b, row),
              ("lru_lambda", lru_lambda, m_lru_lambda, v_lru_lambda, row),
              ("norm_final", norm_final, m_norm_final, v_norm_final, row),
              ("conv_w", conv_w, m_conv_w, v_conv_w, lambda a: a[0]), ("gn_gain", gn_gain, m_gn_gain, v_gn_gain, lambda a: a[0]),
              ("gate_x_w", gate_x_w, m_gate_x_w, v_gate_x_w, gate), ("gate_a_w", gate_a_w, m_gate_a_w, v_gate_a_w, gate)]
    small_out, loss, square_out = _adamw_small(
        me.astype(jnp.int32).reshape(1), vec_all, gx_all, ga_all,
        [tuple(view(a) for a in (w, m, v)) for _, w, m, v, view in groups],
        [(parts[1 + k], w[0], m[0], v[0]) for k, (_, w, m, v) in enumerate(square)])
    for (nm, _, _, _), out in zip(square, square_out):
        res[nm] = [o[None] for o in out]
    for (nm, w, _, _, _), out in zip(groups, small_out):
        res[nm] = [o.reshape(w.shape) for o in out]
    loss = loss.reshape(())

    order = ["norm_in", "w_in", "conv_w", "conv_b", "gate_x_w", "gate_x_b", "gate_a_w", "gate_a_b", "lru_lambda",
             "gn_gain", "w_proj_a", "w_proj_b", "w_out", "norm_final"]
    outs = [loss, grad_x]
    for k in range(4):
        outs += [res[nm][k] for nm in order]
    return tuple(outs)
```

```python
import numpy as np

import jax
import jax.numpy as jnp
from jax import lax
from jax.experimental import pallas as pl
from jax.experimental.pallas import tpu as pltpu

F32 = jnp.float32
BF16 = jnp.bfloat16
MESH = pl.DeviceIdType.MESH

D = 1024
S = 2048
NSEG = 8
NDEV = 8
HEADS = 4
DK = 256
CH = 256
NCH = S // CH
CB = 256
NCB = D // CB
RC = 512
RC_CONV = 128
SCAN_GROUP = 16
EPS = 1e-6
LRU_C = 8.0
VMEM_LIMIT = 56 * 1024 * 1024

ADAM_LR = 0.001
ADAM_B1 = 0.9
ADAM_B2 = 0.999
ADAM_EPS = 1e-08
ADAM_WD = 0.01
ADAM_STEP = 10


def _params(sem=None):
    return pltpu.CompilerParams(dimension_semantics=sem, vmem_limit_bytes=VMEM_LIMIT)


def _dot(a, b):
    return jnp.dot(a, b, preferred_element_type=F32)


def _dot_nt(a, b):
    return lax.dot_general(a, b, (((1,), (1,)), ((), ())), preferred_element_type=F32)


def _dot_tn(a, b):
    return lax.dot_general(a, b, (((0,), (0,)), ((), ())), preferred_element_type=F32)


def _sigmoid(x):
    return jax.nn.sigmoid(x)


def _expm1_nonpos(x):
    poly = x * (1.0 + x * (0.5 + x * (1.0 / 6.0 + x * (1.0 / 24.0))))
    return jnp.where(x > -0.05, poly, jnp.exp(x) - 1.0)


def _softplus(x):
    return jnp.maximum(x, 0.0) + jnp.log(1.0 + jnp.exp(-jnp.abs(x)))


def _rows(c, n):
    return pl.ds(pl.multiple_of(c * n, n), n)


def _window_before(ref, c, n):
    r0 = c * n
    if ref.dtype == BF16:
        prev = ref[pl.ds(pl.multiple_of(jnp.maximum(r0 - 16, 0), 16), 16), :].astype(F32)[8:, :]
    else:
        prev = ref[pl.ds(pl.multiple_of(jnp.maximum(r0 - 8, 0), 8), 8), :]
    prev = jnp.where(c > 0, prev, 0.0)
    return jnp.concatenate([prev, ref[_rows(c, n), :].astype(F32)], axis=0)


def _shift_down(win, s, n):
    if s == 0:
        return win[8:, :]
    return pltpu.roll(win, s, 0)[8:, :]


def _shift_up(win, s, n):
    if s == 0:
        return win[:n, :]
    return pltpu.roll(win, n + 8 - s, 0)[:n, :]


HALF = D // 2
GATHER_SLOTS = [("own", None, 0), ("own", None, 1), ("sib", None, 0), ("sib", None, 1)]
for _j, _h in ((0, 0), (1, 0), (0, 1), (1, 1), (2, 0), (2, 1)):
    GATHER_SLOTS += [("ici", _j, _h), ("fwd", _j, _h)]
NSLOT = len(GATHER_SLOTS)


def _gather_order(x, y, c):
    chips = [(1 - x, y), (x, 1 - y), (1 - x, 1 - y)]
    segs, halves = [], []
    for kind, j, h in GATHER_SLOTS:
        if kind == "own":
            seg = 4 * x + 2 * y + c
        elif kind == "sib":
            seg = 4 * x + 2 * y + 1 - c
        else:
            px, py = chips[j]
            seg = 4 * px + 2 * py + (c if kind == "ici" else 1 - c)
        segs.append(seg)
        halves.append(h)
    return jnp.stack(segs).astype(jnp.int32), jnp.asarray(halves, jnp.int32)


def _inproj_gather(x2d, g_in, w_own, tiny_own, order, halves):
    t = x2d.shape[0]
    tm = 2048
    nt = t // tm
    tx = 1024
    nx = tm // tx

    def body(order_ref, half_ref, g_ref, x_hbm, w_own_ref, tiny_own_ref,
             proj_ref, h_hbm, wg_ref, tinyg_ref,
             w_all, h_all, x_s, send_sems, recv_sems, own_sems, out_sems, tiny_send, tiny_recv, tiny_own_sem,
             x_sems, h_sem):
        k, i = pl.program_id(0), pl.program_id(1)
        x, y, c = _place()
        me, sibling = (x, y, c), (x, y, 1 - c)
        mine = 4 * x + 2 * y + c
        chips = [(1 - x, y), (x, 1 - y), (1 - x, 1 - y)]

        def copy(h, n, block, to, own_src=False):
            px, py, pc = block
            dst = w_all.at[4 * px + 2 * py + pc, h]
            return pltpu.make_async_remote_copy(
                src_ref=w_own_ref.at[:, pl.ds(h * HALF, HALF)] if own_src else dst, dst_ref=dst,
                send_sem=send_sems.at[h, n], recv_sem=recv_sems.at[h, n], device_id=to, device_id_type=MESH)

        def tiny_copy(n, block, to, own_src=False):
            px, py, pc = block
            dst = tinyg_ref.at[4 * px + 2 * py + pc]
            return pltpu.make_async_remote_copy(
                src_ref=tiny_own_ref if own_src else dst, dst_ref=dst,
                send_sem=tiny_send.at[n], recv_sem=tiny_recv.at[n], device_id=to, device_id_type=MESH)

        def own_copy(h):
            return pltpu.make_async_copy(w_own_ref.at[:, pl.ds(h * HALF, HALF)], w_all.at[mine, h], own_sems.at[h])

        tiny_mine = pltpu.make_async_copy(tiny_own_ref, tinyg_ref.at[mine], tiny_own_sem)

        def keep_copy(n):
            h = GATHER_SLOTS[n][2]
            return pltpu.make_async_copy(w_all.at[order_ref[n], h], wg_ref.at[order_ref[n], :, pl.ds(h * HALF, HALF)],
                                         out_sems.at[n])

        near = [(0, sibling), (1, (*chips[0], c)), (2, (*chips[1], c))]
        first = [copy(h, n, me, to, True) for h in (0, 1) for n, to in near]
        tiny_first = [tiny_copy(0, me, sibling, True)] + [tiny_copy(1 + j, me, (*chip, c), True) for j, chip in enumerate(chips)]

        def relay(h, j):
            seg = w_all.at[4 * chips[j][0] + 2 * chips[j][1] + c, h]
            return pltpu.make_async_remote_copy(
                src_ref=seg, dst_ref=seg, send_sem=send_sems.at[h, 3], recv_sem=recv_sems.at[h, 3],
                device_id=(*chips[1 - j], c), device_id_type=MESH)

        for n, (kind, j, h) in enumerate(GATHER_SLOTS):
            @pl.when(jnp.logical_and(k == n, i == 0))
            def _():
                if n == 0:
                    own_copy(0).start()
                    own_copy(1).start()
                    tiny_mine.start()
                    for cp in first + tiny_first:
                        cp.start()
                if kind == "own":
                    own_copy(h).wait()
                elif kind == "sib":
                    copy(h, 0, sibling, me).wait_recv()
                elif kind == "ici":
                    copy(h, 1 + j, (*chips[j], c), me).wait_recv()
                    copy(h, 4 + j, (*chips[j], c), sibling).start()
                    if j < 2:
                        @pl.when(c == j)
                        def _():
                            relay(h, j).start()
                else:
                    copy(h, 4 + j, (*chips[j], 1 - c), me).wait_recv()
                keep_copy(n).start()

        rows = pl.ds(pl.multiple_of(i * tm, tm), tm)

        def x_copy(n):
            return pltpu.make_async_copy(x_hbm.at[pl.ds(n * tx, tx), :], x_s.at[n % 2], x_sems.at[n % 2])

        keep_h = pltpu.make_async_copy(h_all, h_hbm, h_sem)

        for step in range(nt):
            @pl.when(jnp.logical_and(k == 0, i == step))
            def _():
                if step == 0:
                    x_copy(0).start()
                for n in range(step * nx, (step + 1) * nx):
                    x_copy(n).wait()
                    if n + 1 < nt * nx:
                        x_copy(n + 1).start()
                    xv = x_s[n % 2]
                    r = lax.rsqrt(jnp.mean(xv * xv, axis=-1, keepdims=True) + EPS)
                    h_all[pl.ds(n * tx, tx), :] = (xv * r * g_ref[...]).astype(BF16)
                if step == nt - 1:
                    keep_h.start()

        proj_ref[...] = _dot(h_all[rows, :], w_all[order_ref[k], half_ref[k]]).astype(BF16)

        @pl.when(jnp.logical_and(k == NSLOT - 1, i == nt - 1))
        def _():
            for j, chip in enumerate(chips):
                tiny_copy(1 + j, (*chip, c), me).wait_recv()
                tiny_copy(4 + j, (*chip, c), sibling).start()
            tiny_copy(0, sibling, me).wait_recv()
            for j, chip in enumerate(chips):
                tiny_copy(4 + j, (*chip, 1 - c), me).wait_recv()
            for cp in first + tiny_first:
                cp.wait_send()
            for j, chip in enumerate(chips):
                tiny_copy(4 + j, (*chip, c), sibling).wait_send()
                for h in (0, 1):
                    copy(h, 4 + j, (*chip, c), sibling).wait_send()
            for h in (0, 1):
                relay(h, 0).wait_send()
            tiny_mine.wait()
            keep_h.wait()
            for n in range(NSLOT):
                keep_copy(n).wait()

    return pl.pallas_call(
        body, name="inproj_gather",
        grid_spec=pltpu.PrefetchScalarGridSpec(
            num_scalar_prefetch=2, grid=(NSLOT, nt),
            in_specs=[pl.BlockSpec((1, D), lambda k, i, order_ref, half_ref: (0, 0)),
                      ANY, ANY, ANY],
            out_specs=[pl.BlockSpec((None, tm, HALF), lambda k, i, order_ref, half_ref: (order_ref[k], i, half_ref[k])),
                       ANY, ANY, ANY],
            scratch_shapes=[pltpu.VMEM((NDEV, 2, D, HALF), BF16), pltpu.VMEM((t, D), BF16), pltpu.VMEM((2, tx, D), F32),
                            pltpu.SemaphoreType.DMA((2, 7)), pltpu.SemaphoreType.DMA((2, 7)),
                            pltpu.SemaphoreType.DMA((2,)), pltpu.SemaphoreType.DMA((NSLOT,)),
                            pltpu.SemaphoreType.DMA((7,)), pltpu.SemaphoreType.DMA((7,)), pltpu.SemaphoreType.DMA,
                            pltpu.SemaphoreType.DMA((2,)), pltpu.SemaphoreType.DMA]),
        out_shape=[jax.ShapeDtypeStruct((NSEG, t, D), BF16), jax.ShapeDtypeStruct((t, D), BF16),
                   jax.ShapeDtypeStruct((NDEV,) + w_own.shape, BF16),
                   jax.ShapeDtypeStruct((NDEV,) + tiny_own.shape, F32)],
        compiler_params=_params(("arbitrary", "arbitrary")),
    )(order, halves, g_in, x2d, w_own, tiny_own)


def _tile_scan(a, u):
    row = lax.broadcasted_iota(jnp.int32, a.shape, 0)
    for d in (1, 2, 4):
        m = row >= d
        a_sh = pltpu.roll(a, d, 0)
        u_sh = pltpu.roll(u, d, 0)
        u = jnp.where(m, a * u_sh + u, u)
        a = jnp.where(m, a * a_sh, a)
    return a, u


def _tile_scan_rev(a, w):
    row = lax.broadcasted_iota(jnp.int32, a.shape, 0)
    for d in (1, 2, 4):
        m = row < 8 - d
        a_sh = pltpu.roll(a, 8 - d, 0)
        w_sh = pltpu.roll(w, 8 - d, 0)
        w = jnp.where(m, a * w_sh + w, w)
        a = jnp.where(m, a * a_sh, a)
    return a, w


def _lru_gates(xa_ref, c, cw_ref, cb_ref, wbd_ref, bx_ref, ba_ref, sp):
    win = _window_before(xa_ref, c, RC)
    xc = cb_ref[...] + cw_ref[3:4, :] * _shift_down(win, 0, RC)
    for s in (1, 2, 3):
        xc = xc + cw_ref[3 - s:4 - s, :] * _shift_down(win, s, RC)
    z = _dot(xc.astype(BF16), wbd_ref[...])
    gi = _sigmoid(z[:, :CB] + bx_ref[...])
    gr = _sigmoid(z[:, CB:] + ba_ref[...])
    log_a = -LRU_C * gr * sp
    return win, xc, gi, gr, log_a


def _lru_fwd(proj, conv_w, conv_b, wbd, bx, ba, lam, nb):
    t = nb * S

    def body(xa_ref, ga_ref, cw_ref, cb_ref, wbd_ref, bx_ref, ba_ref, lam_ref,
             ya_ref, hs_ref, xc_ref, gi_ref, gr_ref, a_s, u_s):
        sp = _softplus(-lam_ref[...])

        def gates(c, carry):
            _, xc, gi, gr, log_a = _lru_gates(xa_ref, c, cw_ref, cb_ref, wbd_ref, bx_ref, ba_ref, sp)
            rows = _rows(c, RC)
            a_s[rows, :] = jnp.exp(log_a)
            u_s[rows, :] = jnp.sqrt(-_expm1_nonpos(2.0 * log_a)) * (gi * xc)
            xc_ref[rows, :] = xc
            gi_ref[rows, :] = gi
            gr_ref[rows, :] = gr
            return carry

        lax.fori_loop(0, S // RC, gates, 0)

        def scan(g, h):
            for k in range(SCAN_GROUP):
                rows = pl.ds(pl.multiple_of(g * (8 * SCAN_GROUP), 8 * SCAN_GROUP) + 8 * k, 8)
                a_cum, u_cum = _tile_scan(a_s[rows, :], u_s[rows, :])
                hs_ref[rows, :] = u_cum + a_cum * h
                h = u_cum[7:8, :] + a_cum[7:8, :] * h
            return h

        lax.fori_loop(0, S // (8 * SCAN_GROUP), scan, jnp.zeros((1, CB), F32))

        def gate_out(c, carry):
            ga = ga_ref[_rows(c, RC), :].astype(F32)
            ya_ref[_rows(c, RC), :] = (ga * _sigmoid(ga) * hs_ref[_rows(c, RC), :]).astype(BF16)
            return carry

        lax.fori_loop(0, S // RC, gate_out, 0)

    vec = pl.BlockSpec((1, CB), lambda b, cb: (0, cb))
    blk = pl.BlockSpec((S, CB), lambda b, cb: (b, cb))
    return pl.pallas_call(
        body, name="lru_fwd", grid=(nb, NCB),
        in_specs=[pl.BlockSpec((None, S, CB), lambda b, cb: (0, b, cb)),
                  pl.BlockSpec((None, S, CB), lambda b, cb: (1, b, cb)),
                  pl.BlockSpec((4, CB), lambda b, cb: (0, cb)),
                  vec,
                  pl.BlockSpec((None, CB, 2 * CB), lambda b, cb: (cb, 0, 0)),
                  vec, vec, vec],
        out_specs=[blk] + [pl.BlockSpec((None, None, S, CB), lambda b, cb: (b, cb, 0, 0))] * 4,
        out_shape=[jax.ShapeDtypeStruct((t, D), BF16)] + [jax.ShapeDtypeStruct((nb, NCB, S, CB), F32)] * 4,
        scratch_shapes=[pltpu.VMEM((S, CB), F32), pltpu.VMEM((S, CB), F32)],
        compiler_params=_params(("arbitrary", "arbitrary")),
    )(proj, proj, conv_w, conv_b, wbd, bx, ba, lam)


def _lru_bwd(proj, hs, xc_f, gi_f, gr_f, dya, conv_w, wbd, lam, nb, give):
    t = nb * S
    ng = len(give)

    def body(xa_ref, ga_ref, hs_ref, xc_s, gi_s, gr_s, dya_ref, cw_ref, wbd_ref, lam_ref, *rest):
        give_refs, rest = rest[:ng], rest[ng:]
        dp_ref, dwbd_ref, vec_ref = rest[:3]
        got_refs, rest = rest[3:3 + ng], rest[3 + ng:]
        a_s, dl_s, dh_s, dxc_s, acc_s, send_sems, recv_sems = rest
        b = pl.program_id(1)
        exchange = _sibling_copies(give_refs, got_refs, send_sems, recv_sems)

        @pl.when(jnp.logical_and(pl.program_id(0) == 0, b == 0))
        def _():
            for cp in exchange:
                cp.start()

        lam_v = lam_ref[...]
        sp = _softplus(-lam_v)
        acc_s[...] = jnp.zeros_like(acc_s)

        @pl.when(b == 0)
        def _():
            dwbd_ref[...] = jnp.zeros_like(dwbd_ref)
            vec_ref[...] = jnp.zeros_like(vec_ref)

        def gates(c, carry):
            rows = _rows(c, RC)
            a_s[rows, :] = jnp.exp(-LRU_C * gr_s[rows, :] * sp)
            ga = ga_ref[rows, :].astype(F32)
            sg = _sigmoid(ga)
            dya_c = dya_ref[rows, :].astype(F32)
            dl_s[rows, :] = dya_c * (ga * sg)
            dp_ref[1, rows, :] = (dya_c * hs_ref[rows, :] * (sg * (1.0 + ga * (1.0 - sg)))).astype(BF16)
            return carry

        lax.fori_loop(0, S // RC, gates, 0)

        def scan(i, g_in):
            base = pl.multiple_of((S // (8 * SCAN_GROUP) - 1 - i) * (8 * SCAN_GROUP), 8 * SCAN_GROUP)
            row = lax.broadcasted_iota(jnp.int32, (8, CB), 0)
            for k in reversed(range(SCAN_GROUP)):
                rows = pl.ds(base + 8 * k, 8)
                a = a_s[rows, :]
                dl = dl_s[rows, :]
                a_cum, g_loc = _tile_scan_rev(a, a * dl)
                g = g_loc + a_cum * g_in
                dh_s[rows, :] = dl + jnp.where(row < 7, pltpu.roll(g, 7, 0), g_in)
                g_in = g_loc[0:1, :] + a_cum[0:1, :] * g_in
            return g_in

        lax.fori_loop(0, S // (8 * SCAN_GROUP), scan, jnp.zeros((1, CB), F32))

        dxc_s[pl.ds(S, 8), :] = jnp.zeros((8, CB), F32)

        def grads(c, carry):
            rows = _rows(c, RC)
            dh = dh_s[rows, :]
            h_prev = _shift_down(_window_before(hs_ref, c, RC), 1, RC)
            xc, gi, gr, a = xc_s[rows, :], gi_s[rows, :], gr_s[rows, :], a_s[rows, :]
            mult = jnp.sqrt(-_expm1_nonpos(-2.0 * LRU_C * gr * sp))
            dmult = dh * gi * xc
            d_log_a = dh * h_prev * a - dmult * (a * a) / mult
            dzi = dh * mult * xc * gi * (1.0 - gi)
            dzr = d_log_a * (-LRU_C * sp) * gr * (1.0 - gr)
            dz = jnp.concatenate([dzi, dzr], axis=1).astype(BF16)
            dxc_s[rows, :] = dh * mult * gi + _dot_nt(dz, wbd_ref[...])
            dwbd_ref[...] += _dot_tn(xc.astype(BF16), dz)
            acc_s[1:2, :] += jnp.sum(dzi, axis=0, keepdims=True)
            acc_s[2:3, :] += jnp.sum(dzr, axis=0, keepdims=True)
            acc_s[3:4, :] += jnp.sum(d_log_a * (-LRU_C * gr), axis=0, keepdims=True)
            return carry

        lax.fori_loop(0, S // RC, grads, 0, unroll=2)

        def conv_bwd(c, carry):
            rows = _rows(c, RC_CONV)
            dwin = dxc_s[pl.ds(pl.multiple_of(c * RC_CONV, RC_CONV), RC_CONV + 8), :]
            dxc = dwin[:RC_CONV, :]
            xwin = _window_before(xa_ref, c, RC_CONV)
            dxa = cw_ref[3:4, :] * dxc
            acc_s[0:1, :] += jnp.sum(dxc, axis=0, keepdims=True)
            acc_s[7:8, :] += jnp.sum(dxc * _shift_down(xwin, 0, RC_CONV), axis=0, keepdims=True)
            for s in (1, 2, 3):
                dxa = dxa + cw_ref[3 - s:4 - s, :] * _shift_up(dwin, s, RC_CONV)
                acc_s[7 - s:8 - s, :] += jnp.sum(dxc * _shift_down(xwin, s, RC_CONV), axis=0, keepdims=True)
            dp_ref[0, rows, :] = dxa.astype(BF16)
            return carry

        lax.fori_loop(0, S // RC_CONV, conv_bwd, 0)

        row = lax.broadcasted_iota(jnp.int32, acc_s.shape, 0)
        vec_ref[...] += jnp.where(row == 3, acc_s[...] * (-_sigmoid(-lam_v)), acc_s[...])

        @pl.when(jnp.logical_and(pl.program_id(0) == NCB - 1, b == nb - 1))
        def _():
            for cp in exchange:
                cp.wait()

    vec = pl.BlockSpec((1, CB), lambda cb, b: (0, cb))
    blk = pl.BlockSpec((S, CB), lambda cb, b: (b, cb))
    own = pl.BlockSpec((None, None, S, CB), lambda cb, b: (b, cb, 0, 0))
    return pl.pallas_call(
        body, name="lru_bwd", grid=(NCB, nb),
        in_specs=[pl.BlockSpec((None, S, CB), lambda cb, b: (0, b, cb)),
                  pl.BlockSpec((None, S, CB), lambda cb, b: (1, b, cb)),
                  own, own, own, own, blk,
                  pl.BlockSpec((4, CB), lambda cb, b: (0, cb)),
                  pl.BlockSpec((None, CB, 2 * CB), lambda cb, b: (cb, 0, 0)),
                  vec] + [ANY] * ng,
        out_specs=[pl.BlockSpec((2, S, CB), lambda cb, b: (0, b, cb)),
                   pl.BlockSpec((None, CB, 2 * CB), lambda cb, b: (cb, 0, 0)),
                   pl.BlockSpec((8, CB), lambda cb, b: (0, cb))] + [ANY] * ng,
        out_shape=[jax.ShapeDtypeStruct((2, t, D), BF16),
                   jax.ShapeDtypeStruct((NCB, CB, 2 * CB), F32),
                   jax.ShapeDtypeStruct((8, D), F32)]
        + [jax.ShapeDtypeStruct((4,) + g.shape[1:], g.dtype) for g in give],
        scratch_shapes=[pltpu.VMEM((S, CB), F32), pltpu.VMEM((S, CB), F32), pltpu.VMEM((S, CB), F32),
                        pltpu.VMEM((S + 8, CB), F32), pltpu.VMEM((8, CB), F32),
                        pltpu.SemaphoreType.DMA((ng, 4)), pltpu.SemaphoreType.DMA((ng, 4))],
        compiler_params=_params(("arbitrary", "arbitrary")),
    )(proj, proj, hs, xc_f, gi_f, gr_f, dya, conv_w, wbd, lam, *give)


def _retention_tables():
    f32 = np.float32
    log_g = np.log1p(-(f32(2.0) ** (f32(-5.0) - np.arange(HEADS, dtype=f32)))).astype(f32)
    idx = np.arange(CH, dtype=f32)
    diff = idx[:, None] - idx[None, :]
    inner = np.where(diff >= 0, np.exp(np.maximum(diff, f32(0.0))[None] * log_g[:, None, None]), f32(0.0)).astype(f32)
    cross = np.exp((idx[None, :] + f32(1.0)) * log_g[:, None]).astype(f32)
    state = np.exp((f32(CH - 1.0) - idx[None, :]) * log_g[:, None]).astype(f32)
    cross = np.ascontiguousarray(np.broadcast_to(cross[:, :, None], (HEADS, CH, DK)))
    state = np.ascontiguousarray(np.broadcast_to(state[:, :, None], (HEADS, CH, DK)))
    half = DK // 2
    freqs = (f32(10000.0) ** (-np.arange(half, dtype=f32) / f32(half))).astype(f32)
    ang = (np.arange(S, dtype=f32)[:, None] * freqs[None, :]).astype(f32)
    return tuple(jnp.asarray(a) for a in (inner, cross, state, np.cos(ang).astype(f32), np.sin(ang).astype(f32)))


def _rotate(x, cos, sin):
    half = DK // 2
    x1, x2 = x[:, :half], x[:, half:]
    return jnp.concatenate([x1 * cos - x2 * sin, x1 * sin + x2 * cos], axis=1)


def _rotate_back(d, cos, sin):
    half = DK // 2
    d1, d2 = d[:, :half], d[:, half:]
    return jnp.concatenate([d1 * cos + d2 * sin, d2 * cos - d1 * sin], axis=1)


def _ret_fwd(proj, gain, tables, nb, wp_own):
    t = nb * S
    inner_t, cross_t, state_t, cos_t, sin_t = tables

    def body(q_ref, k_ref, v_ref, gb_ref, gain_ref, dm_ref, cd_ref, sd_ref, cos_ref, sin_ref, wp_ref,
             yb_ref, qr_ref, kr_ref, o_ref, rs_ref, wpg_ref, r_s, send_sems, recv_sems, own_sems):
        b, hd = pl.program_id(0), pl.program_id(1)
        own, first, arrive, forward, others = _gather_copies([wp_ref], [wpg_ref], send_sems, recv_sems, own_sems)

        @pl.when(jnp.logical_and(b == 0, hd == 0))
        def _():
            for cp in own + first:
                cp.start()

        @pl.when(jnp.logical_and(b == nb - 1, hd == HEADS - 2))
        def _():
            for came, on in zip(arrive, forward):
                came.wait_recv()
                on.start()

        r_s[...] = jnp.zeros_like(r_s)
        chunk_decay = cd_ref[CH - 1:CH, :]

        def chunk(c, carry):
            rows = _rows(c, CH)
            cos, sin = cos_ref[rows, :], sin_ref[rows, :]
            qr = _rotate(q_ref[rows, :].astype(F32), cos, sin).astype(BF16)
            kr = (_rotate(k_ref[rows, :].astype(F32), cos, sin) * (DK ** -0.5)).astype(BF16)
            vb = v_ref[rows, :]
            v = vb.astype(F32)
            qr_ref[rows, :] = qr
            kr_ref[rows, :] = kr
            r = r_s[...]
            rb = r.astype(BF16)
            rs_ref[c] = rb
            p = (_dot_nt(qr, kr) * dm_ref[...]).astype(BF16)
            o = _dot(p, vb) + _dot(qr, rb) * cd_ref[...]
            r_s[...] = chunk_decay * r + _dot_tn(kr, (v * sd_ref[...]).astype(BF16))
            o_ref[rows, :] = o.astype(BF16)
            oc = o - jnp.mean(o, axis=-1, keepdims=True)
            rstd = lax.rsqrt(jnp.mean(oc * oc, axis=-1, keepdims=True) + EPS)
            gb = gb_ref[rows, :].astype(F32)
            yb_ref[rows, :] = (gb * _sigmoid(gb) * (oc * rstd * gain_ref[...])).astype(BF16)
            return carry

        lax.fori_loop(0, NCH, chunk, 0, unroll=2)

        @pl.when(jnp.logical_and(b == nb - 1, hd == HEADS - 1))
        def _():
            for cp in others:
                cp.wait_recv()
            for cp in first + forward:
                cp.wait_send()
            for cp in own:
                cp.wait()

    seg = lambda s: pl.BlockSpec((None, S, DK), lambda b, h: (s, b, h))
    tab = pl.BlockSpec((None, CH, DK), lambda b, h: (h, 0, 0))
    rot = pl.BlockSpec((S, DK // 2), lambda b, h: (0, 0))
    blk = pl.BlockSpec((S, DK), lambda b, h: (b, h))
    return pl.pallas_call(
        body, name="ret_fwd", grid=(nb, HEADS),
        in_specs=[seg(2), seg(3), seg(4), seg(5),
                  pl.BlockSpec((None, 1, DK), lambda b, h: (h, 0, 0)),
                  tab, tab, tab, rot, rot, ANY],
        out_specs=[blk, blk, blk, blk,
                   pl.BlockSpec((None, None, NCH, DK, DK), lambda b, h: (b, h, 0, 0, 0)), ANY],
        out_shape=[jax.ShapeDtypeStruct((t, D), BF16), jax.ShapeDtypeStruct((t, D), BF16),
                   jax.ShapeDtypeStruct((t, D), BF16), jax.ShapeDtypeStruct((t, D), BF16),
                   jax.ShapeDtypeStruct((nb, HEADS, NCH, DK, DK), BF16),
                   jax.ShapeDtypeStruct((NDEV,) + wp_own.shape, wp_own.dtype)],
        scratch_shapes=[pltpu.VMEM((DK, DK), F32),
                        pltpu.SemaphoreType.DMA((1, 7)), pltpu.SemaphoreType.DMA((1, 7)), pltpu.SemaphoreType.DMA((1,))],
        compiler_params=_params(("arbitrary", "arbitrary")),
    )(proj, proj, proj, proj, gain, inner_t, cross_t, state_t, cos_t, sin_t, wp_own)


def _ret_bwd(proj, qr, kr, o, rs, dyb, gain, tables, nb, sums):
    t = nb * S
    ns = len(sums)
    inner_t, cross_t, state_t, cos_t, sin_t = tables

    def body(qr_ref, kr_ref, v_ref, gb_ref, o_ref, dyb_ref, rs_ref, gain_ref, dm_ref, cd_ref, sd_ref,
             cos_ref, sin_ref, *rest):
        sum_refs, rest = rest[:ns], rest[ns:]
        dp_ref, dgain_ref = rest[:2]
        part_refs, rest = rest[2:2 + ns], rest[2 + ns:]
        dr_s, send_sems, recv_sems, local_sems = rest
        mine, sends, recvs = _chip_copies(sum_refs, part_refs, send_sems, recv_sems, local_sems)

        @pl.when(jnp.logical_and(pl.program_id(0) == 0, pl.program_id(1) == 0))
        def _():
            for cp in mine + sends:
                cp.start()

        dr_s[...] = jnp.zeros_like(dr_s)
        chunk_decay = cd_ref[CH - 1:CH, :]

        @pl.when(pl.program_id(1) == 0)
        def _():
            dgain_ref[...] = jnp.zeros_like(dgain_ref)

        def chunk(i, carry):
            c = NCH - 1 - i
            rows = _rows(c, CH)
            gain_v = gain_ref[...]
            o_c = o_ref[rows, :].astype(F32)
            oc = o_c - jnp.mean(o_c, axis=-1, keepdims=True)
            rstd = lax.rsqrt(jnp.mean(oc * oc, axis=-1, keepdims=True) + EPS)
            yn = oc * rstd
            gb = gb_ref[rows, :].astype(F32)
            sg = _sigmoid(gb)
            dyb_c = dyb_ref[rows, :].astype(F32)
            dgn = dyb_c * (gb * sg)
            dp_ref[3, rows, :] = (dyb_c * (yn * gain_v) * (sg * (1.0 + gb * (1.0 - sg)))).astype(BF16)
            dgain_ref[...] += jnp.sum(dgn * yn, axis=0, keepdims=True)
            dyn = dgn * gain_v
            do = rstd * (dyn - jnp.mean(dyn, axis=-1, keepdims=True)
                         - yn * jnp.mean(dyn * yn, axis=-1, keepdims=True))
            dob = do.astype(BF16)
            dox = (do * cd_ref[...]).astype(BF16)

            q_c, k_c = qr_ref[rows, :], kr_ref[rows, :]
            vb = v_ref[rows, :]
            v = vb.astype(F32)
            vs = (v * sd_ref[...]).astype(BF16)
            rb = rs_ref[c]
            d_r = dr_s[...]
            drb = d_r.astype(BF16)
            dm = dm_ref[...]
            p = (_dot_nt(q_c, k_c) * dm).astype(BF16)
            dpm = (_dot_nt(dob, vb) * dm).astype(BF16)
            dq = _dot(dpm, k_c) + _dot_nt(dox, rb)
            dk = _dot_tn(dpm, q_c) + _dot_nt(vs, drb)
            dv = _dot_tn(p, dob) + _dot(k_c, drb) * sd_ref[...]
            dr_s[...] = chunk_decay * d_r + _dot_tn(q_c, dox)

            cos, sin = cos_ref[rows, :], sin_ref[rows, :]
            dp_ref[0, rows, :] = _rotate_back(dq, cos, sin).astype(BF16)
            dp_ref[1, rows, :] = (_rotate_back(dk, cos, sin) * (DK ** -0.5)).astype(BF16)
            dp_ref[2, rows, :] = dv.astype(BF16)
            return carry

        lax.fori_loop(0, NCH, chunk, 0, unroll=2)

        @pl.when(jnp.logical_and(pl.program_id(0) == HEADS - 1, pl.program_id(1) == nb - 1))
        def _():
            for cp in recvs:
                cp.wait_recv()
            for cp in sends:
                cp.wait_send()
            for cp in mine:
                cp.wait()

    seg = lambda s: pl.BlockSpec((None, S, DK), lambda h, b: (s, b, h))
    tab = pl.BlockSpec((None, CH, DK), lambda h, b: (h, 0, 0))
    rot = pl.BlockSpec((S, DK // 2), lambda h, b: (0, 0))
    blk = pl.BlockSpec((S, DK), lambda h, b: (b, h))
    one = pl.BlockSpec((None, 1, DK), lambda h, b: (h, 0, 0))
    return pl.pallas_call(
        body, name="ret_bwd", grid=(HEADS, nb),
        in_specs=[blk, blk, seg(4), seg(5), blk, blk,
                  pl.BlockSpec((None, None, NCH, DK, DK), lambda h, b: (b, h, 0, 0, 0)),
                  one, tab, tab, tab, rot, rot] + [ANY] * ns,
        out_specs=[pl.BlockSpec((4, S, DK), lambda h, b: (0, b, h)), one] + [ANY] * ns,
        out_shape=[jax.ShapeDtypeStruct((4, t, D), BF16), jax.ShapeDtypeStruct((HEADS, 1, DK), F32)]
        + [jax.ShapeDtypeStruct(a.shape, a.dtype) for a in sums],
        scratch_shapes=[pltpu.VMEM((DK, DK), F32), pltpu.SemaphoreType.DMA((ns, 3)), pltpu.SemaphoreType.DMA((ns, 3)),
                        pltpu.SemaphoreType.DMA((ns,))],
        compiler_params=_params(("arbitrary", "arbitrary")),
    )(qr, kr, proj, proj, o, dyb, rs, gain, inner_t, cross_t, state_t, cos_t, sin_t, *sums)


def _wblock(k):
    return pl.BlockSpec((NDEV, D // NDEV, D), lambda i: (0, k, 0))


def _tail(ya, yb, proj, x2d, tgt, wg, g_fin):
    t = x2d.shape[0]
    tm = 256

    def body(ya_ref, yb_ref, ma_ref, mb_ref, x_ref, t_ref, wa_ref, wb_ref, wo_ref, g_ref,
             dx2_ref, dya_ref, dyb_ref, dm_ref, mg_ref, doa_ref, dob_ref, gfin_ref, loss_ref):
        i = pl.program_id(0)

        @pl.when(i == 0)
        def _():
            gfin_ref[...] = jnp.zeros_like(gfin_ref)
            loss_ref[...] = jnp.zeros_like(loss_ref)

        wa = wa_ref[...].reshape(D, D)
        wb = wb_ref[...].reshape(D, D)
        wo = wo_ref[...].reshape(D, D)
        out_a = _dot(ya_ref[...], wa)
        out_b = _dot(yb_ref[...], wb)
        sa = _sigmoid(ma_ref[...].astype(F32))
        sb = _sigmoid(mb_ref[...].astype(F32))
        merged = (sa * out_a + sb * out_b).astype(BF16)
        mg_ref[...] = merged
        x2 = x_ref[...] + _dot(merged, wo)
        r2 = lax.rsqrt(jnp.mean(x2 * x2, axis=-1, keepdims=True) + EPS)
        xh = x2 * r2
        g = g_ref[...]
        err = xh * g - t_ref[...]
        loss_ref[...] += jnp.sum(err * err, axis=0, keepdims=True) * (0.5 / D)
        dy = err * (1.0 / D)
        gfin_ref[...] += jnp.sum(dy * xh, axis=0, keepdims=True)
        dxh = dy * g
        dx2 = r2 * (dxh - xh * jnp.mean(dxh * xh, axis=-1, keepdims=True))
        dx2_ref[...] = dx2
        dmerged = _dot_nt(dx2.astype(BF16), wo)
        doa = (sa * dmerged).astype(BF16)
        dob = (sb * dmerged).astype(BF16)
        doa_ref[...] = doa
        dob_ref[...] = dob
        dm_ref[0] = (dmerged * out_a * sa * (1.0 - sa)).astype(BF16)
        dm_ref[1] = (dmerged * out_b * sb * (1.0 - sb)).astype(BF16)
        dya_ref[...] = _dot_nt(doa, wa).astype(BF16)
        dyb_ref[...] = _dot_nt(dob, wb).astype(BF16)

    row = lambda: pl.BlockSpec((tm, D), lambda i: (i, 0))
    seg = lambda s: pl.BlockSpec((None, tm, D), lambda i: (s, i, 0))
    vec = pl.BlockSpec((1, D), lambda i: (0, 0))
    return pl.pallas_call(
        body, name="tail", grid=(t // tm,),
        in_specs=[row(), row(), seg(6), seg(7), row(), row(), _wblock(0), _wblock(1), _wblock(2), vec],
        out_specs=[row(), row(), row(), pl.BlockSpec((2, tm, D), lambda i: (0, i, 0)),
                   row(), row(), row(), vec, vec],
        out_shape=[jax.ShapeDtypeStruct((t, D), F32), jax.ShapeDtypeStruct((t, D), BF16),
                   jax.ShapeDtypeStruct((t, D), BF16), jax.ShapeDtypeStruct((2, t, D), BF16),
                   jax.ShapeDtypeStruct((t, D), BF16), jax.ShapeDtypeStruct((t, D), BF16),
                   jax.ShapeDtypeStruct((t, D), BF16), jax.ShapeDtypeStruct((1, D), F32),
                   jax.ShapeDtypeStruct((1, D), F32)],
        compiler_params=_params(("arbitrary",)),
    )(ya, yb, proj, proj, x2d, tgt, wg, wg, wg, g_fin)


def _tail_wgrad(ya, yb, merged, doa, dob, dx2):
    t = ya.shape[0]
    tm = 512

    def body(ya_ref, yb_ref, mg_ref, doa_ref, dob_ref, dx2_ref, ga_ref, gb_ref, go_ref):
        @pl.when(pl.program_id(0) == 0)
        def _():
            ga_ref[...] = jnp.zeros_like(ga_ref)
            gb_ref[...] = jnp.zeros_like(gb_ref)
            go_ref[...] = jnp.zeros_like(go_ref)

        ga_ref[...] += _dot_tn(ya_ref[...], doa_ref[...])
        gb_ref[...] += _dot_tn(yb_ref[...], dob_ref[...])
        go_ref[...] += _dot_tn(mg_ref[...], dx2_ref[...].astype(BF16))

    row = lambda: pl.BlockSpec((tm, D), lambda i: (i, 0))
    full = lambda: pl.BlockSpec((D, D), lambda i: (0, 0))
    return pl.pallas_call(
        body, name="tail_wgrad", grid=(t // tm,),
        in_specs=[row() for _ in range(6)], out_specs=[full(), full(), full()],
        out_shape=[jax.ShapeDtypeStruct((D, D), F32)] * 3,
        compiler_params=_params(("arbitrary",)),
    )(ya, yb, merged, doa, dob, dx2)


def _dproj_specs_ordered(tm):
    def spec(lo, n):
        def index(k, i, order_ref):
            seg = order_ref[k]
            mine = jnp.logical_and(seg >= lo, seg < lo + n)
            return jnp.where(mine, seg - lo, 0), jnp.where(mine, i, 0), 0
        return pl.BlockSpec((None, tm, D), index)
    return [spec(0, 2), spec(2, 4), spec(6, 2)]


def _dproj_pick(j, da_ref, db_ref, dc_ref, use):
    @pl.when(j < 2)
    def _():
        use(da_ref[...])

    @pl.when(jnp.logical_and(j >= 2, j < 6))
    def _():
        use(db_ref[...])

    @pl.when(j >= 6)
    def _():
        use(dc_ref[...])


RS_X, RS_Y, RS_XY = 0, 1, 2
RS_ROLES = ((RS_XY, RS_X, RS_Y), (RS_Y, RS_XY, RS_X))


def _rs_flip(rel, x, y):
    return ((1 - x, y), (x, 1 - y), (1 - x, 1 - y))[rel]


def _rs_order(x, y, c):
    order = []
    for s in range(4):
        chip = []
        for core in (0, 1):
            px, py = _rs_flip(RS_ROLES[core][s], x, y) if s < 3 else (x, y)
            chip.append(2 * px + py)
        keep = jnp.where(c == 0, chip[0], chip[1])
        give = jnp.where(c == 0, chip[1], chip[0])
        order += [2 * give + 1 - c, 2 * keep + c]
    return jnp.stack(order).astype(jnp.int32)


def _inproj_wgrad_rs(h, dpa, dpb, dpc, order, smalls):
    t = h.shape[0]
    tm = 1024
    nt = t // tm
    nsm = len(smalls)

    def body(order_ref, h_ref, da_ref, db_ref, dc_ref, *rest):
        small_refs, parts_ref, rest = rest[:nsm], rest[nsm], rest[nsm + 1:]
        all_refs, rest = rest[:nsm], rest[nsm:]
        (acc, sib, outb, far, give_send, give_recv, sum_send, sum_recv, far_send, far_recv, own_sem,
         small_send, small_recv, small_own) = rest
        k, i = pl.program_id(0), pl.program_id(1)
        x, y, c = _place()
        own, first, arrive, forward, others = _gather_copies(small_refs, all_refs, small_send, small_recv, small_own)

        @pl.when(jnp.logical_and(k == 0, i == 0))
        def _():
            for cp in own + first:
                cp.start()

        @pl.when(jnp.logical_and(k == 2, i == 0))
        def _():
            for came, on in zip(arrive, forward):
                came.wait_recv()
                on.start()

        def use(d):
            @pl.when(i == 0)
            def _():
                acc[k % 2] = _dot_tn(h_ref[...], d)

            @pl.when(i > 0)
            def _():
                acc[k % 2] += _dot_tn(h_ref[...], d)

        _dproj_pick(order_ref[k], da_ref, db_ref, dc_ref, use)

        def give_copy(s):
            return pltpu.make_async_remote_copy(
                src_ref=acc.at[0], dst_ref=sib.at[s % 2], send_sem=give_send.at[s], recv_sem=give_recv.at[s],
                device_id=(x, y, 1 - c), device_id_type=MESH)

        def sum_copy(s, core):
            slot = 0 if s < 2 else 1
            return pltpu.make_async_remote_copy(
                src_ref=outb.at[s], dst_ref=parts_ref.at[slot], send_sem=sum_send.at[slot], recv_sem=sum_recv.at[slot],
                device_id=(*_rs_flip(RS_ROLES[core][s], x, y), core), device_id_type=MESH)

        def far_copy(s, core):
            return pltpu.make_async_remote_copy(
                src_ref=outb.at[s], dst_ref=far, send_sem=far_send, recv_sem=far_recv,
                device_id=(*_rs_flip(RS_X if core == 0 else RS_Y, x, y), core), device_id_type=MESH)

        own_copy = pltpu.make_async_copy(outb.at[3], parts_ref.at[2], own_sem)

        def send_of(core, s):
            return far_copy(s, core) if RS_ROLES[core][s] == RS_XY else sum_copy(s, core)

        for s in range(4):
            @pl.when(jnp.logical_and(k == 2 * s, i == nt - 1))
            def _():
                give_copy(s).start()

            @pl.when(jnp.logical_and(k == 2 * s + 1, i == nt - 1))
            def _():
                give_copy(s).wait_recv()
                if s == 2:
                    far_copy(s, 0).wait_recv()
                    outb[s] = (acc[1] + sib[s % 2] + far[...].astype(F32)).astype(BF16)
                else:
                    outb[s] = (acc[1] + sib[s % 2]).astype(BF16)
                give_copy(s).wait_send()
                if s < 3:
                    for core in (0, 1):
                        @pl.when(c == core)
                        def _():
                            send_of(core, s).start()
                else:
                    own_copy.start()

        @pl.when(jnp.logical_and(k == NSEG - 1, i == nt - 1))
        def _():
            for slot in (0, 1):
                sum_copy(2 * slot, 0).wait_recv()
            for s in range(3):
                send_of(0, s).wait_send()
            own_copy.wait()
            for cp in others:
                cp.wait_recv()
            for cp in first + forward:
                cp.wait_send()
            for cp in own:
                cp.wait()

    return pl.pallas_call(
        body, name="inproj_wgrad_rs",
        grid_spec=pltpu.PrefetchScalarGridSpec(
            num_scalar_prefetch=1, grid=(NSEG, nt),
            in_specs=[pl.BlockSpec((tm, D), lambda k, i, order_ref: (i, 0))] + _dproj_specs_ordered(tm) + [ANY] * nsm,
            out_specs=[ANY] * (1 + nsm),
            scratch_shapes=[pltpu.VMEM((2, D, D), F32), pltpu.VMEM((2, D, D), F32), pltpu.VMEM((4, D, D), BF16),
                            pltpu.VMEM((D, D), BF16),
                            pltpu.SemaphoreType.DMA((4,)), pltpu.SemaphoreType.DMA((4,)),
                            pltpu.SemaphoreType.DMA((2,)), pltpu.SemaphoreType.DMA((2,)),
                            pltpu.SemaphoreType.DMA, pltpu.SemaphoreType.DMA, pltpu.SemaphoreType.DMA,
                            pltpu.SemaphoreType.DMA((nsm, 7)), pltpu.SemaphoreType.DMA((nsm, 7)),
                            pltpu.SemaphoreType.DMA((nsm,))]),
        out_shape=[jax.ShapeDtypeStruct((3, D, D), BF16)]
        + [jax.ShapeDtypeStruct((NDEV,) + a.shape, a.dtype) for a in smalls],
        compiler_params=_params(("arbitrary", "arbitrary")),
    )(order, h, dpa, dpb, dpc, *smalls)


def _inproj_dgrad(dpa, dpb, dpc, wg, x2d, dx2, g_in):
    t = x2d.shape[0]
    tm = 512

    def body(da_ref, db_ref, dc_ref, w_hbm, x_ref, dx2_ref, g_ref, gx_ref, gg_ref, w_s, w_sem):
        i = pl.program_id(0)

        @pl.when(i == 0)
        def _():
            gg_ref[...] = jnp.zeros_like(gg_ref)
            load = pltpu.make_async_copy(w_hbm, w_s, w_sem)
            load.start()
            load.wait()

        dh = None
        for ref, lo in ((da_ref, 0), (db_ref, 2), (dc_ref, 6)):
            for k in range(ref.shape[0]):
                part = _dot_nt(ref[k], w_s[lo + k])
                dh = part if dh is None else dh + part
        x = x_ref[...]
        r = lax.rsqrt(jnp.mean(x * x, axis=-1, keepdims=True) + EPS)
        xh = x * r
        gg_ref[...] += jnp.sum(dh * xh, axis=0, keepdims=True)
        dxh = dh * g_ref[...]
        gx_ref[...] = dx2_ref[...] + r * (dxh - xh * jnp.mean(dxh * xh, axis=-1, keepdims=True))

    row = lambda: pl.BlockSpec((tm, D), lambda i: (i, 0))
    seg = lambda n: pl.BlockSpec((n, tm, D), lambda i: (0, i, 0))
    vec = pl.BlockSpec((1, D), lambda i: (0, 0))
    return pl.pallas_call(
        body, name="inproj_dgrad", grid=(t // tm,),
        in_specs=[seg(2), seg(4), seg(2), ANY, row(), row(), vec],
        out_specs=[row(), vec],
        out_shape=[jax.ShapeDtypeStruct((t, D), F32), jax.ShapeDtypeStruct((1, D), F32)],
        scratch_shapes=[pltpu.VMEM((NSEG, D, D), BF16), pltpu.SemaphoreType.DMA],
        compiler_params=_params(("arbitrary",)),
    )(dpa, dpb, dpc, wg, x2d, dx2, g_in)


def _adam_update(g, w, m, v):
    m_new = ADAM_B1 * m + (1.0 - ADAM_B1) * g
    v_new = ADAM_B2 * v + (1.0 - ADAM_B2) * (g * g)
    m_hat = m_new / (1.0 - ADAM_B1 ** ADAM_STEP)
    v_hat = v_new / (1.0 - ADAM_B2 ** ADAM_STEP)
    return -ADAM_LR * (m_hat / (jnp.sqrt(v_hat) + ADAM_EPS) + ADAM_WD * w), m_new, v_new


def _sum_in_order(ref):
    total = ref[0].astype(F32)
    for k in range(1, ref.shape[0]):
        total = total + ref[k].astype(F32)
    return total


def _sum_devices(arrs):
    def body(*refs):
        for a in range(len(arrs)):
            refs[len(arrs) + a][...] = _sum_in_order(refs[a])

    return pl.pallas_call(
        body, name="sum_devices",
        out_shape=[jax.ShapeDtypeStruct(a.shape[1:], F32) for a in arrs],
        compiler_params=_params(),
    )(*arrs)


def _adamw_small(me, vec_all, gx_all, ga_all, groups):
    flat = [a for grp in groups for a in grp]
    ng = len(groups)
    nshard = D // NDEV

    def body(me_ref, vec_ref, shard_ref, gx_ref, ga_ref, *refs):
        ins, outs = refs[:3 * ng], refs[3 * ng:]
        vec = _sum_in_order(vec_ref)
        shard = _sum_in_order(shard_ref)
        grads = [vec[r:r + 1, :] for r in range(6)]
        grads += [shard[0:4, :], shard[4:8, 0:DK // NDEV], _sum_in_order(gx_ref), _sum_in_order(ga_ref)]
        for n, g in enumerate(grads):
            delta, m_new, v_new = _adam_update(g, ins[3 * n][...], ins[3 * n + 1][...], ins[3 * n + 2][...])
            outs[4 * n][...] = g
            outs[4 * n + 1][...] = delta
            outs[4 * n + 2][...] = m_new
            outs[4 * n + 3][...] = v_new
        outs[4 * ng][...] = jnp.sum(vec[6:7, :], axis=1, keepdims=True)

    full = lambda a: pl.BlockSpec(a.shape, lambda i, me_ref, nd=len(a.shape): (0,) * nd)
    out_shape = [jax.ShapeDtypeStruct(w.shape, F32) for w, _, _ in groups for _ in range(4)]
    out_shape.append(jax.ShapeDtypeStruct((1, 1), F32))
    outs = pl.pallas_call(
        body, name="adamw_small",
        grid_spec=pltpu.PrefetchScalarGridSpec(
            num_scalar_prefetch=1, grid=(1,),
            in_specs=[full(vec_all),
                      pl.BlockSpec((NDEV, 8, nshard), lambda i, me_ref: (0, 1, me_ref[0])),
                      full(gx_all), full(ga_all)] + [full(a) for a in flat],
            out_specs=[full(s) for s in out_shape]),
        out_shape=out_shape,
        compiler_params=_params(("arbitrary",)),
    )(me, vec_all, vec_all, gx_all, ga_all, *flat)
    return [outs[4 * n:4 * n + 4] for n in range(ng)], outs[4 * ng]


def _adamw(name, items):
    n, rows, cols = items[0][0].shape
    tr = rows if rows <= 256 else 256
    k = len(items)

    def body(*refs):
        for a in range(k):
            p_ref, w_ref, m_ref, v_ref = refs[4 * a:4 * a + 4]
            g = _sum_in_order(p_ref)
            delta, m_new, v_new = _adam_update(g, w_ref[...], m_ref[...], v_ref[...])
            for o, val in zip(refs[4 * k + 4 * a:4 * k + 4 * a + 4], (g, delta, m_new, v_new)):
                o[...] = val

    blk = lambda: pl.BlockSpec((tr, cols), lambda i: (i, 0))
    outs = pl.pallas_call(
        body, name=name, grid=(rows // tr,),
        in_specs=[pl.BlockSpec((n, tr, cols), lambda i: (0, i, 0)), blk(), blk(), blk()] * k,
        out_specs=[blk() for _ in range(4 * k)],
        out_shape=[jax.ShapeDtypeStruct((rows, cols), F32)] * (4 * k),
        compiler_params=_params(("arbitrary",)),
    )(*[a for item in items for a in item])
    return [outs[4 * a:4 * a + 4] for a in range(k)]


ANY = pl.BlockSpec(memory_space=pl.ANY)


def _place():
    return lax.axis_index("x"), lax.axis_index("y"), lax.axis_index("c")


def _gather_copies(ins, outs, send_sems, recv_sems, own_sems):
    x, y, c = _place()
    me, sibling = (x, y, c), (x, y, 1 - c)
    chips = [(1 - x, y), (x, 1 - y), (1 - x, 1 - y)]
    n = len(ins)

    def copy(a, k, block, to, src=None):
        px, py, pc = block
        dst = outs[a].at[4 * px + 2 * py + pc]
        return pltpu.make_async_remote_copy(
            src_ref=dst if src is None else src, dst_ref=dst,
            send_sem=send_sems.at[a, k], recv_sem=recv_sems.at[a, k], device_id=to, device_id_type=MESH)

    own = [pltpu.make_async_copy(ins[a], outs[a].at[4 * x + 2 * y + c], own_sems.at[a]) for a in range(n)]
    first = []
    for a in range(n):
        first.append(copy(a, 0, me, sibling, src=ins[a]))
        first += [copy(a, 1 + j, me, (*chip, c), src=ins[a]) for j, chip in enumerate(chips)]
    arrive = [copy(a, 1 + j, (*chip, c), me) for j, chip in enumerate(chips) for a in range(n)]
    forward = [copy(a, 4 + j, (*chip, c), sibling) for j, chip in enumerate(chips) for a in range(n)]
    rest = [copy(a, 0, sibling, me) for a in range(n)]
    rest += [copy(a, 4 + j, (*chip, 1 - c), me) for a in range(n) for j, chip in enumerate(chips)]
    return own, first, arrive, forward, rest


def _sibling_copies(ins, outs, send_sems, recv_sems):
    x, y, c = _place()
    return [pltpu.make_async_remote_copy(
        src_ref=ins[a].at[2 * q + 1 - c], dst_ref=outs[a].at[q],
        send_sem=send_sems.at[a, q], recv_sem=recv_sems.at[a, q],
        device_id=(x, y, 1 - c), device_id_type=MESH) for a in range(len(ins)) for q in range(4)]


def _chip_copies(ins, outs, send_sems, recv_sems, local_sems):
    x, y, c = _place()
    my_chip = 2 * x + y
    chips = [(1 - x, y), (x, 1 - y), (1 - x, 1 - y)]
    n = len(ins)
    mine = [pltpu.make_async_copy(ins[a].at[my_chip], outs[a].at[my_chip], local_sems.at[a]) for a in range(n)]
    sends = [pltpu.make_async_remote_copy(
        src_ref=ins[a].at[2 * px + py], dst_ref=outs[a].at[my_chip],
        send_sem=send_sems.at[a, j], recv_sem=recv_sems.at[a, j],
        device_id=(px, py, c), device_id_type=MESH) for a in range(n) for j, (px, py) in enumerate(chips)]
    recvs = [pltpu.make_async_remote_copy(
        src_ref=ins[a].at[my_chip], dst_ref=outs[a].at[2 * px + py],
        send_sem=send_sems.at[a, j], recv_sem=recv_sems.at[a, j],
        device_id=(px, py, c), device_id_type=MESH) for a in range(n) for j, (px, py) in enumerate(chips)]
    return mine, sends, recvs


def _chip_sum(owns, gots, core):
    n = len(owns)
    _, rows, cols = owns[0].shape

    def body(core_ref, *refs):
        for a in range(n):
            refs[2 * n + a][...] = (refs[a][...] + refs[n + a][...]).astype(BF16)

    own_spec = pl.BlockSpec((None, rows, cols), lambda q, core_ref: (2 * q + core_ref[0], 0, 0))
    slab = pl.BlockSpec((None, rows, cols), lambda q, core_ref: (q, 0, 0))
    return pl.pallas_call(
        body, name="chip_sum",
        grid_spec=pltpu.PrefetchScalarGridSpec(
            num_scalar_prefetch=1, grid=(4,),
            in_specs=[own_spec] * n + [slab] * n, out_specs=[slab] * n),
        out_shape=[jax.ShapeDtypeStruct((4, rows, cols), BF16)] * n,
        compiler_params=_params(("arbitrary",)),
    )(core, *owns, *gots)


def _block_diag(w):
    w4 = w.reshape(NCB, 4, 64, 64)
    eye = jnp.eye(4, dtype=w.dtype)
    return (w4[:, :, :, None, :] * eye[None, :, None, :, None]).reshape(NCB, CB, CB)


def _block_diag_back(g):
    g5 = g.reshape(NCB, 4, 64, 4, 64)
    return jnp.stack([g5[:, m, :, m, :] for m in range(4)], axis=1).reshape(16, 64, 64)


def kernel(x, norm_in, w_in, conv_w, conv_b, gate_x_w, gate_x_b, gate_a_w, gate_a_b, lru_lambda, gn_gain, w_proj_a, w_proj_b, w_out, norm_final, loss_target, m_norm_in, m_w_in, m_conv_w, m_conv_b, m_gate_x_w, m_gate_x_b, m_gate_a_w, m_gate_a_b, m_lru_lambda, m_gn_gain, m_w_proj_a, m_w_proj_b, m_w_out, m_norm_final, v_norm_in, v_w_in, v_conv_w, v_conv_b, v_gate_x_w, v_gate_x_b, v_gate_a_w, v_gate_a_b, v_lru_lambda, v_gn_gain, v_w_proj_a, v_w_proj_b, v_w_out, v_norm_final):
    xi, yi, ci = _place()
    me = 4 * xi + 2 * yi + ci
    core = ci.astype(jnp.int32).reshape(1)
    nshard = D // NDEV
    nb = x.shape[0]
    t = nb * S
    x2d = x.reshape(t, D)
    tgt2d = loss_target.reshape(t, D)
    g_final = norm_final.reshape(1, D)
    wbd = jnp.concatenate([_block_diag(gate_x_w[0]), _block_diag(gate_a_w[0])], axis=-1).astype(BF16)
    tables = _retention_tables()

    wp_own = jnp.concatenate([w_proj_a[0], w_proj_b[0], w_out[0]], axis=0).astype(BF16)
    tiny = jnp.concatenate([conv_w[0], jnp.pad(gn_gain[0], ((0, 0), (0, nshard - DK // NDEV)))], axis=0)
    proj, h, wg, tiny_g = _inproj_gather(x2d, norm_in, w_in[0].astype(BF16), tiny, *_gather_order(xi, yi, ci))
    conv_w_full = tiny_g[:, 0:4, :].transpose(1, 0, 2).reshape(4, D)
    gain3 = tiny_g[:, 4:8, :DK // NDEV].transpose(1, 0, 2).reshape(HEADS, 1, DK)

    ya, hs, xc, gi, gr = _lru_fwd(proj, conv_w_full, conv_b, wbd, gate_x_b, gate_a_b, lru_lambda, nb)
    yb, qr, kr, o, rs, wpg = _ret_fwd(proj, gain3, tables, nb, wp_own)
    dx2, dya, dyb, dpc, merged, doa, dob, g_fin, loss_vec = _tail(ya, yb, proj, x2d, tgt2d, wpg, g_final)
    g_pa, g_pb, g_out = _tail_wgrad(ya, yb, merged, doa, dob, dx2)

    own = [g.reshape(NDEV, nshard, D) for g in (g_pa, g_pb, g_out)]
    dpa, g_wbd, g_vec, *got = _lru_bwd(proj, hs, xc, gi, gr, dya, conv_w_full, wbd, lru_lambda, nb, own)
    sums = _chip_sum(own, got, core)
    dpb, g_gain, *parts = _ret_bwd(proj, qr, kr, o, rs, dyb, gain3, tables, nb, sums)

    grad_x, g_norm_in = _inproj_dgrad(dpa, dpb, dpc, wg, x2d, dx2, norm_in)
    grad_x = grad_x.reshape(nb, S, D)

    gain_rows = jnp.pad(g_gain.reshape(HEADS, NDEV, DK // NDEV), ((0, 0), (0, 0), (0, nshard - DK // NDEV)))
    vec = jnp.concatenate([g_norm_in, g_vec[0:4], g_fin, loss_vec, jnp.zeros((1, D), F32), g_vec[4:8],
                           gain_rows.reshape(HEADS, D)], axis=0)
    g_gx = _block_diag_back(g_wbd[:, :, :CB]).reshape(D // 2, 128)
    g_ga = _block_diag_back(g_wbd[:, :, CB:]).reshape(D // 2, 128)
    parts_in, vec_all, gx_all, ga_all = _inproj_wgrad_rs(h, dpa, dpb, dpc, _rs_order(xi, yi, ci),
                                                         [vec, g_gx, g_ga])
    gx_all, ga_all = [g.reshape(1, D, 64) for g in _sum_devices([gx_all, ga_all])]
    parts = [parts_in] + list(parts)

    res = {}
    (out,) = _adamw("adamw_w_in", [(parts[0], w_in[0], m_w_in[0], v_w_in[0])])
    res["w_in"] = [o[None] for o in out]
    square = [("w_proj_a", w_proj_a, m_w_proj_a, v_w_proj_a), ("w_proj_b", w_proj_b, m_w_proj_b, v_w_proj_b),
              ("w_out", w_out, m_w_out, v_w_out)]
    outs = _adamw("adamw_square", [(parts[1 + k], w[0], m[0], v[0]) for k, (_, w, m, v) in enumerate(square)])
    for (nm, _, _, _), out in zip(square, outs):
        res[nm] = [o[None] for o in out]

    row = lambda a: a.reshape(1, D)
    gate = lambda a: a.reshape(D, 64)
    groups = [("norm_in", norm_in, m_norm_in, v_norm_in, row), ("conv_b", conv_b, m_conv_b, v_conv_b, row),
              ("gate_x_b", gate_x_b, m_gate_x_b, v_gate_x_b, row), ("gate_a_b", gate_a_b, m_gate_a_b, v_gate_a_b, row),
              ("lru_lambda", lru_lambda, m_lru_lambda, v_lru_lambda, row),
              ("norm_final", norm_final, m_norm_final, v_norm_final, row),
              ("conv_w", conv_w, m_conv_w, v_conv_w, lambda a: a[0]), ("gn_gain", gn_gain, m_gn_gain, v_gn_gain, lambda a: a[0]),
              ("gate_x_w", gate_x_w, m_gate_x_w, v_gate_x_w, gate), ("gate_a_w", gate_a_w, m_gate_a_w, v_gate_a_w, gate)]
    small_out, loss = _adamw_small(me.astype(jnp.int32).reshape(1), vec_all, gx_all, ga_all,
                                   [tuple(view(a) for a in (w, m, v)) for _, w, m, v, view in groups])
    for (nm, w, _, _, _), out in zip(groups, small_out):
        res[nm] = [o.reshape(w.shape) for o in out]
    loss = loss.reshape(())

    order = ["norm_in", "w_in", "conv_w", "conv_b", "gate_x_w", "gate_x_b", "gate_a_w", "gate_a_b", "lru_lambda",
             "gn_gain", "w_proj_a", "w_proj_b", "w_out", "norm_final"]
    outs = [loss, grad_x]
    for k in range(4):
        outs += [res[nm][k] for nm in order]
    return tuple(outs)
```

```python
import numpy as np

import jax
import jax.numpy as jnp
from jax import lax
from jax.experimental import pallas as pl
from jax.experimental.pallas import tpu as pltpu

F32 = jnp.float32
BF16 = jnp.bfloat16
MESH = pl.DeviceIdType.MESH

D = 1024
S = 2048
NSEG = 8
NDEV = 8
HEADS = 4
DK = 256
CH = 256
NCH = S // CH
CB = 256
NCB = D // CB
RC = 512
RC_CONV = 128
SCAN_GROUP = 16
EPS = 1e-6
LRU_C = 8.0
VMEM_LIMIT = 56 * 1024 * 1024

ADAM_LR = 0.001
ADAM_B1 = 0.9
ADAM_B2 = 0.999
ADAM_EPS = 1e-08
ADAM_WD = 0.01
ADAM_STEP = 10


def _params(sem=None):
    return pltpu.CompilerParams(dimension_semantics=sem, vmem_limit_bytes=VMEM_LIMIT)


def _dot(a, b):
    return jnp.dot(a, b, preferred_element_type=F32)


def _dot_nt(a, b):
    return lax.dot_general(a, b, (((1,), (1,)), ((), ())), preferred_element_type=F32)


def _dot_tn(a, b):
    return lax.dot_general(a, b, (((0,), (0,)), ((), ())), preferred_element_type=F32)


def _sigmoid(x):
    return jax.nn.sigmoid(x)


def _expm1_nonpos(x):
    poly = x * (1.0 + x * (0.5 + x * (1.0 / 6.0 + x * (1.0 / 24.0))))
    return jnp.where(x > -0.05, poly, jnp.exp(x) - 1.0)


def _softplus(x):
    return jnp.maximum(x, 0.0) + jnp.log(1.0 + jnp.exp(-jnp.abs(x)))


def _rows(c, n):
    return pl.ds(pl.multiple_of(c * n, n), n)


def _window_before(ref, c, n):
    r0 = c * n
    if ref.dtype == BF16:
        prev = ref[pl.ds(pl.multiple_of(jnp.maximum(r0 - 16, 0), 16), 16), :].astype(F32)[8:, :]
    else:
        prev = ref[pl.ds(pl.multiple_of(jnp.maximum(r0 - 8, 0), 8), 8), :]
    prev = jnp.where(c > 0, prev, 0.0)
    return jnp.concatenate([prev, ref[_rows(c, n), :].astype(F32)], axis=0)


def _shift_down(win, s, n):
    if s == 0:
        return win[8:, :]
    return pltpu.roll(win, s, 0)[8:, :]


def _shift_up(win, s, n):
    if s == 0:
        return win[:n, :]
    return pltpu.roll(win, n + 8 - s, 0)[:n, :]


HALF = D // 2
GATHER_SLOTS = [("own", None, 0), ("own", None, 1), ("sib", None, 0), ("sib", None, 1)]
for _j, _h in ((0, 0), (1, 0), (0, 1), (1, 1), (2, 0), (2, 1)):
    GATHER_SLOTS += [("ici", _j, _h), ("fwd", _j, _h)]
NSLOT = len(GATHER_SLOTS)


def _gather_order(x, y, c):
    chips = [(1 - x, y), (x, 1 - y), (1 - x, 1 - y)]
    segs, halves = [], []
    for kind, j, h in GATHER_SLOTS:
        if kind == "own":
            seg = 4 * x + 2 * y + c
        elif kind == "sib":
            seg = 4 * x + 2 * y + 1 - c
        else:
            px, py = chips[j]
            seg = 4 * px + 2 * py + (c if kind == "ici" else 1 - c)
        segs.append(seg)
        halves.append(h)
    return jnp.stack(segs).astype(jnp.int32), jnp.asarray(halves, jnp.int32)


def _inproj_gather(x2d, g_in, w_own, tiny_own, order, halves):
    t = x2d.shape[0]
    tm = 2048
    nt = t // tm
    tx = 1024
    nx = tm // tx

    def body(order_ref, half_ref, g_ref, x_hbm, w_own_ref, tiny_own_ref,
             proj_ref, h_hbm, wg_ref, tinyg_ref,
             w_all, h_all, x_s, send_sems, recv_sems, own_sems, out_sems, tiny_send, tiny_recv, tiny_own_sem,
             x_sems, h_sem):
        k, i = pl.program_id(0), pl.program_id(1)
        x, y, c = _place()
        me, sibling = (x, y, c), (x, y, 1 - c)
        mine = 4 * x + 2 * y + c
        chips = [(1 - x, y), (x, 1 - y), (1 - x, 1 - y)]

        def copy(h, n, block, to, own_src=False):
            px, py, pc = block
            dst = w_all.at[4 * px + 2 * py + pc, h]
            return pltpu.make_async_remote_copy(
                src_ref=w_own_ref.at[:, pl.ds(h * HALF, HALF)] if own_src else dst, dst_ref=dst,
                send_sem=send_sems.at[h, n], recv_sem=recv_sems.at[h, n], device_id=to, device_id_type=MESH)

        def tiny_copy(n, block, to, own_src=False):
            px, py, pc = block
            dst = tinyg_ref.at[4 * px + 2 * py + pc]
            return pltpu.make_async_remote_copy(
                src_ref=tiny_own_ref if own_src else dst, dst_ref=dst,
                send_sem=tiny_send.at[n], recv_sem=tiny_recv.at[n], device_id=to, device_id_type=MESH)

        def own_copy(h):
            return pltpu.make_async_copy(w_own_ref.at[:, pl.ds(h * HALF, HALF)], w_all.at[mine, h], own_sems.at[h])

        tiny_mine = pltpu.make_async_copy(tiny_own_ref, tinyg_ref.at[mine], tiny_own_sem)

        def keep_copy(n):
            h = GATHER_SLOTS[n][2]
            return pltpu.make_async_copy(w_all.at[order_ref[n], h], wg_ref.at[order_ref[n], :, pl.ds(h * HALF, HALF)],
                                         out_sems.at[n])

        near = [(0, sibling), (1, (*chips[0], c)), (2, (*chips[1], c))]
        first = [copy(h, n, me, to, True) for h in (0, 1) for n, to in near]
        tiny_first = [tiny_copy(0, me, sibling, True)] + [tiny_copy(1 + j, me, (*chip, c), True) for j, chip in enumerate(chips)]

        def relay(h, j):
            seg = w_all.at[4 * chips[j][0] + 2 * chips[j][1] + c, h]
            return pltpu.make_async_remote_copy(
                src_ref=seg, dst_ref=seg, send_sem=send_sems.at[h, 3], recv_sem=recv_sems.at[h, 3],
                device_id=(*chips[1 - j], c), device_id_type=MESH)

        for n, (kind, j, h) in enumerate(GATHER_SLOTS):
            @pl.when(jnp.logical_and(k == n, i == 0))
            def _():
                if n == 0:
                    own_copy(0).start()
                    own_copy(1).start()
                    tiny_mine.start()
                    for cp in first + tiny_first:
                        cp.start()
                if kind == "own":
                    own_copy(h).wait()
                elif kind == "sib":
                    copy(h, 0, sibling, me).wait_recv()
                elif kind == "ici":
                    copy(h, 1 + j, (*chips[j], c), me).wait_recv()
                    copy(h, 4 + j, (*chips[j], c), sibling).start()
                    if j < 2:
                        @pl.when(c == j)
                        def _():
                            relay(h, j).start()
                else:
                    copy(h, 4 + j, (*chips[j], 1 - c), me).wait_recv()
                keep_copy(n).start()

        rows = pl.ds(pl.multiple_of(i * tm, tm), tm)

        def x_copy(n):
            return pltpu.make_async_copy(x_hbm.at[pl.ds(n * tx, tx), :], x_s.at[n % 2], x_sems.at[n % 2])

        keep_h = pltpu.make_async_copy(h_all, h_hbm, h_sem)

        for step in range(nt):
            @pl.when(jnp.logical_and(k == 0, i == step))
            def _():
                if step == 0:
                    x_copy(0).start()
                for n in range(step * nx, (step + 1) * nx):
                    x_copy(n).wait()
                    if n + 1 < nt * nx:
                        x_copy(n + 1).start()
                    xv = x_s[n % 2]
                    r = lax.rsqrt(jnp.mean(xv * xv, axis=-1, keepdims=True) + EPS)
                    h_all[pl.ds(n * tx, tx), :] = (xv * r * g_ref[...]).astype(BF16)
                if step == nt - 1:
                    keep_h.start()

        proj_ref[...] = _dot(h_all[rows, :], w_all[order_ref[k], half_ref[k]]).astype(BF16)

        @pl.when(jnp.logical_and(k == NSLOT - 1, i == nt - 1))
        def _():
            for j, chip in enumerate(chips):
                tiny_copy(1 + j, (*chip, c), me).wait_recv()
                tiny_copy(4 + j, (*chip, c), sibling).start()
            tiny_copy(0, sibling, me).wait_recv()
            for j, chip in enumerate(chips):
                tiny_copy(4 + j, (*chip, 1 - c), me).wait_recv()
            for cp in first + tiny_first:
                cp.wait_send()
            for j, chip in enumerate(chips):
                tiny_copy(4 + j, (*chip, c), sibling).wait_send()
                for h in (0, 1):
                    copy(h, 4 + j, (*chip, c), sibling).wait_send()
            for h in (0, 1):
                relay(h, 0).wait_send()
            tiny_mine.wait()
            keep_h.wait()
            for n in range(NSLOT):
                keep_copy(n).wait()

    return pl.pallas_call(
        body, name="inproj_gather",
        grid_spec=pltpu.PrefetchScalarGridSpec(
            num_scalar_prefetch=2, grid=(NSLOT, nt),
            in_specs=[pl.BlockSpec((1, D), lambda k, i, order_ref, half_ref: (0, 0)),
                      ANY, ANY, ANY],
            out_specs=[pl.BlockSpec((None, tm, HALF), lambda k, i, order_ref, half_ref: (order_ref[k], i, half_ref[k])),
                       ANY, ANY, ANY],
            scratch_shapes=[pltpu.VMEM((NDEV, 2, D, HALF), BF16), pltpu.VMEM((t, D), BF16), pltpu.VMEM((2, tx, D), F32),
                            pltpu.SemaphoreType.DMA((2, 7)), pltpu.SemaphoreType.DMA((2, 7)),
                            pltpu.SemaphoreType.DMA((2,)), pltpu.SemaphoreType.DMA((NSLOT,)),
                            pltpu.SemaphoreType.DMA((7,)), pltpu.SemaphoreType.DMA((7,)), pltpu.SemaphoreType.DMA,
                            pltpu.SemaphoreType.DMA((2,)), pltpu.SemaphoreType.DMA]),
        out_shape=[jax.ShapeDtypeStruct((NSEG, t, D), BF16), jax.ShapeDtypeStruct((t, D), BF16),
                   jax.ShapeDtypeStruct((NDEV,) + w_own.shape, BF16),
                   jax.ShapeDtypeStruct((NDEV,) + tiny_own.shape, F32)],
        compiler_params=_params(("arbitrary", "arbitrary")),
    )(order, halves, g_in, x2d, w_own, tiny_own)


def _tile_scan(a, u):
    row = lax.broadcasted_iota(jnp.int32, a.shape, 0)
    for d in (1, 2, 4):
        m = row >= d
        a_sh = pltpu.roll(a, d, 0)
        u_sh = pltpu.roll(u, d, 0)
        u = jnp.where(m, a * u_sh + u, u)
        a = jnp.where(m, a * a_sh, a)
    return a, u


def _tile_scan_rev(a, w):
    row = lax.broadcasted_iota(jnp.int32, a.shape, 0)
    for d in (1, 2, 4):
        m = row < 8 - d
        a_sh = pltpu.roll(a, 8 - d, 0)
        w_sh = pltpu.roll(w, 8 - d, 0)
        w = jnp.where(m, a * w_sh + w, w)
        a = jnp.where(m, a * a_sh, a)
    return a, w


def _lru_gates(xa_ref, c, cw_ref, cb_ref, wbd_ref, bx_ref, ba_ref, sp):
    win = _window_before(xa_ref, c, RC)
    xc = cb_ref[...] + cw_ref[3:4, :] * _shift_down(win, 0, RC)
    for s in (1, 2, 3):
        xc = xc + cw_ref[3 - s:4 - s, :] * _shift_down(win, s, RC)
    z = _dot(xc.astype(BF16), wbd_ref[...])
    gi = _sigmoid(z[:, :CB] + bx_ref[...])
    gr = _sigmoid(z[:, CB:] + ba_ref[...])
    log_a = -LRU_C * gr * sp
    return win, xc, gi, gr, log_a


def _lru_fwd(proj, conv_w, conv_b, wbd, bx, ba, lam, nb):
    t = nb * S

    def body(xa_ref, ga_ref, cw_ref, cb_ref, wbd_ref, bx_ref, ba_ref, lam_ref,
             ya_ref, hs_ref, xc_ref, gi_ref, gr_ref, a_s, u_s):
        sp = _softplus(-lam_ref[...])

        def gates(c, carry):
            _, xc, gi, gr, log_a = _lru_gates(xa_ref, c, cw_ref, cb_ref, wbd_ref, bx_ref, ba_ref, sp)
            rows = _rows(c, RC)
            a_s[rows, :] = jnp.exp(log_a)
            u_s[rows, :] = jnp.sqrt(-_expm1_nonpos(2.0 * log_a)) * (gi * xc)
            xc_ref[rows, :] = xc
            gi_ref[rows, :] = gi
            gr_ref[rows, :] = gr
            return carry

        lax.fori_loop(0, S // RC, gates, 0)

        def scan(g, h):
            for k in range(SCAN_GROUP):
                rows = pl.ds(pl.multiple_of(g * (8 * SCAN_GROUP), 8 * SCAN_GROUP) + 8 * k, 8)
                a_cum, u_cum = _tile_scan(a_s[rows, :], u_s[rows, :])
                hs_ref[rows, :] = u_cum + a_cum * h
                h = u_cum[7:8, :] + a_cum[7:8, :] * h
            return h

        lax.fori_loop(0, S // (8 * SCAN_GROUP), scan, jnp.zeros((1, CB), F32))

        def gate_out(c, carry):
            ga = ga_ref[_rows(c, RC), :].astype(F32)
            ya_ref[_rows(c, RC), :] = (ga * _sigmoid(ga) * hs_ref[_rows(c, RC), :]).astype(BF16)
            return carry

        lax.fori_loop(0, S // RC, gate_out, 0)

    vec = pl.BlockSpec((1, CB), lambda b, cb: (0, cb))
    blk = pl.BlockSpec((S, CB), lambda b, cb: (b, cb))
    return pl.pallas_call(
        body, name="lru_fwd", grid=(nb, NCB),
        in_specs=[pl.BlockSpec((None, S, CB), lambda b, cb: (0, b, cb)),
                  pl.BlockSpec((None, S, CB), lambda b, cb: (1, b, cb)),
                  pl.BlockSpec((4, CB), lambda b, cb: (0, cb)),
                  vec,
                  pl.BlockSpec((None, CB, 2 * CB), lambda b, cb: (cb, 0, 0)),
                  vec, vec, vec],
        out_specs=[blk] + [pl.BlockSpec((None, None, S, CB), lambda b, cb: (b, cb, 0, 0))] * 4,
        out_shape=[jax.ShapeDtypeStruct((t, D), BF16)] + [jax.ShapeDtypeStruct((nb, NCB, S, CB), F32)] * 4,
        scratch_shapes=[pltpu.VMEM((S, CB), F32), pltpu.VMEM((S, CB), F32)],
        compiler_params=_params(("arbitrary", "arbitrary")),
    )(proj, proj, conv_w, conv_b, wbd, bx, ba, lam)


def _lru_bwd(proj, hs, xc_f, gi_f, gr_f, dya, conv_w, wbd, lam, nb, give):
    t = nb * S
    ng = len(give)

    def body(xa_ref, ga_ref, hs_ref, xc_s, gi_s, gr_s, dya_ref, cw_ref, wbd_ref, lam_ref, *rest):
        give_refs, rest = rest[:ng], rest[ng:]
        dp_ref, dwbd_ref, vec_ref = rest[:3]
        got_refs, rest = rest[3:3 + ng], rest[3 + ng:]
        a_s, dl_s, dh_s, dxc_s, acc_s, send_sems, recv_sems = rest
        b = pl.program_id(1)
        exchange = _sibling_copies(give_refs, got_refs, send_sems, recv_sems)

        @pl.when(jnp.logical_and(pl.program_id(0) == 0, b == 0))
        def _():
            for cp in exchange:
                cp.start()

        lam_v = lam_ref[...]
        sp = _softplus(-lam_v)
        acc_s[...] = jnp.zeros_like(acc_s)

        @pl.when(b == 0)
        def _():
            dwbd_ref[...] = jnp.zeros_like(dwbd_ref)
            vec_ref[...] = jnp.zeros_like(vec_ref)

        def gates(c, carry):
            rows = _rows(c, RC)
            a_s[rows, :] = jnp.exp(-LRU_C * gr_s[rows, :] * sp)
            ga = ga_ref[rows, :].astype(F32)
            sg = _sigmoid(ga)
            dya_c = dya_ref[rows, :].astype(F32)
            dl_s[rows, :] = dya_c * (ga * sg)
            dp_ref[1, rows, :] = (dya_c * hs_ref[rows, :] * (sg * (1.0 + ga * (1.0 - sg)))).astype(BF16)
            return carry

        lax.fori_loop(0, S // RC, gates, 0)

        def scan(i, g_in):
            base = pl.multiple_of((S // (8 * SCAN_GROUP) - 1 - i) * (8 * SCAN_GROUP), 8 * SCAN_GROUP)
            row = lax.broadcasted_iota(jnp.int32, (8, CB), 0)
            for k in reversed(range(SCAN_GROUP)):
                rows = pl.ds(base + 8 * k, 8)
                a = a_s[rows, :]
                dl = dl_s[rows, :]
                a_cum, g_loc = _tile_scan_rev(a, a * dl)
                g = g_loc + a_cum * g_in
                dh_s[rows, :] = dl + jnp.where(row < 7, pltpu.roll(g, 7, 0), g_in)
                g_in = g_loc[0:1, :] + a_cum[0:1, :] * g_in
            return g_in

        lax.fori_loop(0, S // (8 * SCAN_GROUP), scan, jnp.zeros((1, CB), F32))

        dxc_s[pl.ds(S, 8), :] = jnp.zeros((8, CB), F32)

        def grads(c, carry):
            rows = _rows(c, RC)
            dh = dh_s[rows, :]
            h_prev = _shift_down(_window_before(hs_ref, c, RC), 1, RC)
            xc, gi, gr, a = xc_s[rows, :], gi_s[rows, :], gr_s[rows, :], a_s[rows, :]
            mult = jnp.sqrt(-_expm1_nonpos(-2.0 * LRU_C * gr * sp))
            dmult = dh * gi * xc
            d_log_a = dh * h_prev * a - dmult * (a * a) / mult
            dzi = dh * mult * xc * gi * (1.0 - gi)
            dzr = d_log_a * (-LRU_C * sp) * gr * (1.0 - gr)
            dz = jnp.concatenate([dzi, dzr], axis=1).astype(BF16)
            dxc_s[rows, :] = dh * mult * gi + _dot_nt(dz, wbd_ref[...])
            dwbd_ref[...] += _dot_tn(xc.astype(BF16), dz)
            acc_s[1:2, :] += jnp.sum(dzi, axis=0, keepdims=True)
            acc_s[2:3, :] += jnp.sum(dzr, axis=0, keepdims=True)
            acc_s[3:4, :] += jnp.sum(d_log_a * (-LRU_C * gr), axis=0, keepdims=True)
            return carry

        lax.fori_loop(0, S // RC, grads, 0, unroll=2)

        def conv_bwd(c, carry):
            rows = _rows(c, RC_CONV)
            dwin = dxc_s[pl.ds(pl.multiple_of(c * RC_CONV, RC_CONV), RC_CONV + 8), :]
            dxc = dwin[:RC_CONV, :]
            xwin = _window_before(xa_ref, c, RC_CONV)
            dxa = cw_ref[3:4, :] * dxc
            acc_s[0:1, :] += jnp.sum(dxc, axis=0, keepdims=True)
            acc_s[7:8, :] += jnp.sum(dxc * _shift_down(xwin, 0, RC_CONV), axis=0, keepdims=True)
            for s in (1, 2, 3):
                dxa = dxa + cw_ref[3 - s:4 - s, :] * _shift_up(dwin, s, RC_CONV)
                acc_s[7 - s:8 - s, :] += jnp.sum(dxc * _shift_down(xwin, s, RC_CONV), axis=0, keepdims=True)
            dp_ref[0, rows, :] = dxa.astype(BF16)
            return carry

        lax.fori_loop(0, S // RC_CONV, conv_bwd, 0)

        row = lax.broadcasted_iota(jnp.int32, acc_s.shape, 0)
        vec_ref[...] += jnp.where(row == 3, acc_s[...] * (-_sigmoid(-lam_v)), acc_s[...])

        @pl.when(jnp.logical_and(pl.program_id(0) == NCB - 1, b == nb - 1))
        def _():
            for cp in exchange:
                cp.wait()

    vec = pl.BlockSpec((1, CB), lambda cb, b: (0, cb))
    blk = pl.BlockSpec((S, CB), lambda cb, b: (b, cb))
    own = pl.BlockSpec((None, None, S, CB), lambda cb, b: (b, cb, 0, 0))
    return pl.pallas_call(
        body, name="lru_bwd", grid=(NCB, nb),
        in_specs=[pl.BlockSpec((None, S, CB), lambda cb, b: (0, b, cb)),
                  pl.BlockSpec((None, S, CB), lambda cb, b: (1, b, cb)),
                  own, own, own, own, blk,
                  pl.BlockSpec((4, CB), lambda cb, b: (0, cb)),
                  pl.BlockSpec((None, CB, 2 * CB), lambda cb, b: (cb, 0, 0)),
                  vec] + [ANY] * ng,
        out_specs=[pl.BlockSpec((2, S, CB), lambda cb, b: (0, b, cb)),
                   pl.BlockSpec((None, CB, 2 * CB), lambda cb, b: (cb, 0, 0)),
                   pl.BlockSpec((8, CB), lambda cb, b: (0, cb))] + [ANY] * ng,
        out_shape=[jax.ShapeDtypeStruct((2, t, D), BF16),
                   jax.ShapeDtypeStruct((NCB, CB, 2 * CB), F32),
                   jax.ShapeDtypeStruct((8, D), F32)]
        + [jax.ShapeDtypeStruct((4,) + g.shape[1:], g.dtype) for g in give],
        scratch_shapes=[pltpu.VMEM((S, CB), F32), pltpu.VMEM((S, CB), F32), pltpu.VMEM((S, CB), F32),
                        pltpu.VMEM((S + 8, CB), F32), pltpu.VMEM((8, CB), F32),
                        pltpu.SemaphoreType.DMA((ng, 4)), pltpu.SemaphoreType.DMA((ng, 4))],
        compiler_params=_params(("arbitrary", "arbitrary")),
    )(proj, proj, hs, xc_f, gi_f, gr_f, dya, conv_w, wbd, lam, *give)


def _retention_tables():
    f32 = np.float32
    log_g = np.log1p(-(f32(2.0) ** (f32(-5.0) - np.arange(HEADS, dtype=f32)))).astype(f32)
    idx = np.arange(CH, dtype=f32)
    diff = idx[:, None] - idx[None, :]
    inner = np.where(diff >= 0, np.exp(np.maximum(diff, f32(0.0))[None] * log_g[:, None, None]), f32(0.0)).astype(f32)
    cross = np.exp((idx[None, :] + f32(1.0)) * log_g[:, None]).astype(f32)
    state = np.exp((f32(CH - 1.0) - idx[None, :]) * log_g[:, None]).astype(f32)
    cross = np.ascontiguousarray(np.broadcast_to(cross[:, :, None], (HEADS, CH, DK)))
    state = np.ascontiguousarray(np.broadcast_to(state[:, :, None], (HEADS, CH, DK)))
    half = DK // 2
    freqs = (f32(10000.0) ** (-np.arange(half, dtype=f32) / f32(half))).astype(f32)
    ang = (np.arange(S, dtype=f32)[:, None] * freqs[None, :]).astype(f32)
    return tuple(jnp.asarray(a) for a in (inner, cross, state, np.cos(ang).astype(f32), np.sin(ang).astype(f32)))


def _rotate(x, cos, sin):
    half = DK // 2
    x1, x2 = x[:, :half], x[:, half:]
    return jnp.concatenate([x1 * cos - x2 * sin, x1 * sin + x2 * cos], axis=1)


def _rotate_back(d, cos, sin):
    half = DK // 2
    d1, d2 = d[:, :half], d[:, half:]
    return jnp.concatenate([d1 * cos + d2 * sin, d2 * cos - d1 * sin], axis=1)


def _ret_fwd(proj, gain, tables, nb, wp_own):
    t = nb * S
    inner_t, cross_t, state_t, cos_t, sin_t = tables

    def body(q_ref, k_ref, v_ref, gb_ref, gain_ref, dm_ref, cd_ref, sd_ref, cos_ref, sin_ref, wp_ref,
             yb_ref, qr_ref, kr_ref, o_ref, rs_ref, wpg_ref, r_s, send_sems, recv_sems, own_sems):
        b, hd = pl.program_id(0), pl.program_id(1)
        own, first, arrive, forward, others = _gather_copies([wp_ref], [wpg_ref], send_sems, recv_sems, own_sems)

        @pl.when(jnp.logical_and(b == 0, hd == 0))
        def _():
            for cp in own + first:
                cp.start()

        @pl.when(jnp.logical_and(b == nb - 1, hd == HEADS - 1))
        def _():
            for came, on in zip(arrive, forward):
                came.wait_recv()
                on.start()

        r_s[...] = jnp.zeros_like(r_s)
        chunk_decay = cd_ref[CH - 1:CH, :]

        def chunk(c, carry):
            rows = _rows(c, CH)
            cos, sin = cos_ref[rows, :], sin_ref[rows, :]
            qr = _rotate(q_ref[rows, :].astype(F32), cos, sin).astype(BF16)
            kr = (_rotate(k_ref[rows, :].astype(F32), cos, sin) * (DK ** -0.5)).astype(BF16)
            vb = v_ref[rows, :]
            v = vb.astype(F32)
            qr_ref[rows, :] = qr
            kr_ref[rows, :] = kr
            r = r_s[...]
            rb = r.astype(BF16)
            rs_ref[c] = rb
            p = (_dot_nt(qr, kr) * dm_ref[...]).astype(BF16)
            o = _dot(p, vb) + _dot(qr, rb) * cd_ref[...]
            r_s[...] = chunk_decay * r + _dot_tn(kr, (v * sd_ref[...]).astype(BF16))
            o_ref[rows, :] = o.astype(BF16)
            oc = o - jnp.mean(o, axis=-1, keepdims=True)
            rstd = lax.rsqrt(jnp.mean(oc * oc, axis=-1, keepdims=True) + EPS)
            gb = gb_ref[rows, :].astype(F32)
            yb_ref[rows, :] = (gb * _sigmoid(gb) * (oc * rstd * gain_ref[...])).astype(BF16)
            return carry

        lax.fori_loop(0, NCH, chunk, 0, unroll=2)

        @pl.when(jnp.logical_and(b == nb - 1, hd == HEADS - 1))
        def _():
            for cp in others:
                cp.wait_recv()
            for cp in first + forward:
                cp.wait_send()
            for cp in own:
                cp.wait()

    seg = lambda s: pl.BlockSpec((None, S, DK), lambda b, h: (s, b, h))
    tab = pl.BlockSpec((None, CH, DK), lambda b, h: (h, 0, 0))
    rot = pl.BlockSpec((S, DK // 2), lambda b, h: (0, 0))
    blk = pl.BlockSpec((S, DK), lambda b, h: (b, h))
    return pl.pallas_call(
        body, name="ret_fwd", grid=(nb, HEADS),
        in_specs=[seg(2), seg(3), seg(4), seg(5),
                  pl.BlockSpec((None, 1, DK), lambda b, h: (h, 0, 0)),
                  tab, tab, tab, rot, rot, ANY],
        out_specs=[blk, blk, blk, blk,
                   pl.BlockSpec((None, None, NCH, DK, DK), lambda b, h: (b, h, 0, 0, 0)), ANY],
        out_shape=[jax.ShapeDtypeStruct((t, D), BF16), jax.ShapeDtypeStruct((t, D), BF16),
                   jax.ShapeDtypeStruct((t, D), BF16), jax.ShapeDtypeStruct((t, D), BF16),
                   jax.ShapeDtypeStruct((nb, HEADS, NCH, DK, DK), BF16),
                   jax.ShapeDtypeStruct((NDEV,) + wp_own.shape, wp_own.dtype)],
        scratch_shapes=[pltpu.VMEM((DK, DK), F32),
                        pltpu.SemaphoreType.DMA((1, 7)), pltpu.SemaphoreType.DMA((1, 7)), pltpu.SemaphoreType.DMA((1,))],
        compiler_params=_params(("arbitrary", "arbitrary")),
    )(proj, proj, proj, proj, gain, inner_t, cross_t, state_t, cos_t, sin_t, wp_own)


def _ret_bwd(proj, qr, kr, o, rs, dyb, gain, tables, nb, sums):
    t = nb * S
    ns = len(sums)
    inner_t, cross_t, state_t, cos_t, sin_t = tables

    def body(qr_ref, kr_ref, v_ref, gb_ref, o_ref, dyb_ref, rs_ref, gain_ref, dm_ref, cd_ref, sd_ref,
             cos_ref, sin_ref, *rest):
        sum_refs, rest = rest[:ns], rest[ns:]
        dp_ref, dgain_ref = rest[:2]
        part_refs, rest = rest[2:2 + ns], rest[2 + ns:]
        dr_s, send_sems, recv_sems, local_sems = rest
        mine, sends, recvs = _chip_copies(sum_refs, part_refs, send_sems, recv_sems, local_sems)

        @pl.when(jnp.logical_and(pl.program_id(0) == 0, pl.program_id(1) == 0))
        def _():
            for cp in mine + sends:
                cp.start()

        dr_s[...] = jnp.zeros_like(dr_s)
        chunk_decay = cd_ref[CH - 1:CH, :]

        @pl.when(pl.program_id(1) == 0)
        def _():
            dgain_ref[...] = jnp.zeros_like(dgain_ref)

        def chunk(i, carry):
            c = NCH - 1 - i
            rows = _rows(c, CH)
            gain_v = gain_ref[...]
            o_c = o_ref[rows, :].astype(F32)
            oc = o_c - jnp.mean(o_c, axis=-1, keepdims=True)
            rstd = lax.rsqrt(jnp.mean(oc * oc, axis=-1, keepdims=True) + EPS)
            yn = oc * rstd
            gb = gb_ref[rows, :].astype(F32)
            sg = _sigmoid(gb)
            dyb_c = dyb_ref[rows, :].astype(F32)
            dgn = dyb_c * (gb * sg)
            dp_ref[3, rows, :] = (dyb_c * (yn * gain_v) * (sg * (1.0 + gb * (1.0 - sg)))).astype(BF16)
            dgain_ref[...] += jnp.sum(dgn * yn, axis=0, keepdims=True)
            dyn = dgn * gain_v
            do = rstd * (dyn - jnp.mean(dyn, axis=-1, keepdims=True)
                         - yn * jnp.mean(dyn * yn, axis=-1, keepdims=True))
            dob = do.astype(BF16)
            dox = (do * cd_ref[...]).astype(BF16)

            q_c, k_c = qr_ref[rows, :], kr_ref[rows, :]
            vb = v_ref[rows, :]
            v = vb.astype(F32)
            vs = (v * sd_ref[...]).astype(BF16)
            rb = rs_ref[c]
            d_r = dr_s[...]
            drb = d_r.astype(BF16)
            dm = dm_ref[...]
            p = (_dot_nt(q_c, k_c) * dm).astype(BF16)
            dpm = (_dot_nt(dob, vb) * dm).astype(BF16)
            dq = _dot(dpm, k_c) + _dot_nt(dox, rb)
            dk = _dot_tn(dpm, q_c) + _dot_nt(vs, drb)
            dv = _dot_tn(p, dob) + _dot(k_c, drb) * sd_ref[...]
            dr_s[...] = chunk_decay * d_r + _dot_tn(q_c, dox)

            cos, sin = cos_ref[rows, :], sin_ref[rows, :]
            dp_ref[0, rows, :] = _rotate_back(dq, cos, sin).astype(BF16)
            dp_ref[1, rows, :] = (_rotate_back(dk, cos, sin) * (DK ** -0.5)).astype(BF16)
            dp_ref[2, rows, :] = dv.astype(BF16)
            return carry

        lax.fori_loop(0, NCH, chunk, 0, unroll=2)

        @pl.when(jnp.logical_and(pl.program_id(0) == HEADS - 1, pl.program_id(1) == nb - 1))
        def _():
            for cp in recvs:
                cp.wait_recv()
            for cp in sends:
                cp.wait_send()
            for cp in mine:
                cp.wait()

    seg = lambda s: pl.BlockSpec((None, S, DK), lambda h, b: (s, b, h))
    tab = pl.BlockSpec((None, CH, DK), lambda h, b: (h, 0, 0))
    rot = pl.BlockSpec((S, DK // 2), lambda h, b: (0, 0))
    blk = pl.BlockSpec((S, DK), lambda h, b: (b, h))
    one = pl.BlockSpec((None, 1, DK), lambda h, b: (h, 0, 0))
    return pl.pallas_call(
        body, name="ret_bwd", grid=(HEADS, nb),
        in_specs=[blk, blk, seg(4), seg(5), blk, blk,
                  pl.BlockSpec((None, None, NCH, DK, DK), lambda h, b: (b, h, 0, 0, 0)),
                  one, tab, tab, tab, rot, rot] + [ANY] * ns,
        out_specs=[pl.BlockSpec((4, S, DK), lambda h, b: (0, b, h)), one] + [ANY] * ns,
        out_shape=[jax.ShapeDtypeStruct((4, t, D), BF16), jax.ShapeDtypeStruct((HEADS, 1, DK), F32)]
        + [jax.ShapeDtypeStruct(a.shape, a.dtype) for a in sums],
        scratch_shapes=[pltpu.VMEM((DK, DK), F32), pltpu.SemaphoreType.DMA((ns, 3)), pltpu.SemaphoreType.DMA((ns, 3)),
                        pltpu.SemaphoreType.DMA((ns,))],
        compiler_params=_params(("arbitrary", "arbitrary")),
    )(qr, kr, proj, proj, o, dyb, rs, gain, inner_t, cross_t, state_t, cos_t, sin_t, *sums)


def _wblock(k):
    return pl.BlockSpec((NDEV, D // NDEV, D), lambda i: (0, k, 0))


def _tail(ya, yb, proj, x2d, tgt, wg, g_fin):
    t = x2d.shape[0]
    tm = 256

    def body(ya_ref, yb_ref, ma_ref, mb_ref, x_ref, t_ref, wa_ref, wb_ref, wo_ref, g_ref,
             dx2_ref, dya_ref, dyb_ref, dm_ref, mg_ref, doa_ref, dob_ref, gfin_ref, loss_ref):
        i = pl.program_id(0)

        @pl.when(i == 0)
        def _():
            gfin_ref[...] = jnp.zeros_like(gfin_ref)
            loss_ref[...] = jnp.zeros_like(loss_ref)

        wa = wa_ref[...].reshape(D, D)
        wb = wb_ref[...].reshape(D, D)
        wo = wo_ref[...].reshape(D, D)
        out_a = _dot(ya_ref[...], wa)
        out_b = _dot(yb_ref[...], wb)
        sa = _sigmoid(ma_ref[...].astype(F32))
        sb = _sigmoid(mb_ref[...].astype(F32))
        merged = (sa * out_a + sb * out_b).astype(BF16)
        mg_ref[...] = merged
        x2 = x_ref[...] + _dot(merged, wo)
        r2 = lax.rsqrt(jnp.mean(x2 * x2, axis=-1, keepdims=True) + EPS)
        xh = x2 * r2
        g = g_ref[...]
        err = xh * g - t_ref[...]
        loss_ref[...] += jnp.sum(err * err, axis=0, keepdims=True) * (0.5 / D)
        dy = err * (1.0 / D)
        gfin_ref[...] += jnp.sum(dy * xh, axis=0, keepdims=True)
        dxh = dy * g
        dx2 = r2 * (dxh - xh * jnp.mean(dxh * xh, axis=-1, keepdims=True))
        dx2_ref[...] = dx2
        dmerged = _dot_nt(dx2.astype(BF16), wo)
        doa = (sa * dmerged).astype(BF16)
        dob = (sb * dmerged).astype(BF16)
        doa_ref[...] = doa
        dob_ref[...] = dob
        dm_ref[0] = (dmerged * out_a * sa * (1.0 - sa)).astype(BF16)
        dm_ref[1] = (dmerged * out_b * sb * (1.0 - sb)).astype(BF16)
        dya_ref[...] = _dot_nt(doa, wa).astype(BF16)
        dyb_ref[...] = _dot_nt(dob, wb).astype(BF16)

    row = lambda: pl.BlockSpec((tm, D), lambda i: (i, 0))
    seg = lambda s: pl.BlockSpec((None, tm, D), lambda i: (s, i, 0))
    vec = pl.BlockSpec((1, D), lambda i: (0, 0))
    return pl.pallas_call(
        body, name="tail", grid=(t // tm,),
        in_specs=[row(), row(), seg(6), seg(7), row(), row(), _wblock(0), _wblock(1), _wblock(2), vec],
        out_specs=[row(), row(), row(), pl.BlockSpec((2, tm, D), lambda i: (0, i, 0)),
                   row(), row(), row(), vec, vec],
        out_shape=[jax.ShapeDtypeStruct((t, D), F32), jax.ShapeDtypeStruct((t, D), BF16),
                   jax.ShapeDtypeStruct((t, D), BF16), jax.ShapeDtypeStruct((2, t, D), BF16),
                   jax.ShapeDtypeStruct((t, D), BF16), jax.ShapeDtypeStruct((t, D), BF16),
                   jax.ShapeDtypeStruct((t, D), BF16), jax.ShapeDtypeStruct((1, D), F32),
                   jax.ShapeDtypeStruct((1, D), F32)],
        compiler_params=_params(("arbitrary",)),
    )(ya, yb, proj, proj, x2d, tgt, wg, wg, wg, g_fin)


def _tail_wgrad(ya, yb, merged, doa, dob, dx2):
    t = ya.shape[0]
    tm = 512

    def body(ya_ref, yb_ref, mg_ref, doa_ref, dob_ref, dx2_ref, ga_ref, gb_ref, go_ref):
        @pl.when(pl.program_id(0) == 0)
        def _():
            ga_ref[...] = jnp.zeros_like(ga_ref)
            gb_ref[...] = jnp.zeros_like(gb_ref)
            go_ref[...] = jnp.zeros_like(go_ref)

        ga_ref[...] += _dot_tn(ya_ref[...], doa_ref[...])
        gb_ref[...] += _dot_tn(yb_ref[...], dob_ref[...])
        go_ref[...] += _dot_tn(mg_ref[...], dx2_ref[...].astype(BF16))

    row = lambda: pl.BlockSpec((tm, D), lambda i: (i, 0))
    full = lambda: pl.BlockSpec((D, D), lambda i: (0, 0))
    return pl.pallas_call(
        body, name="tail_wgrad", grid=(t // tm,),
        in_specs=[row() for _ in range(6)], out_specs=[full(), full(), full()],
        out_shape=[jax.ShapeDtypeStruct((D, D), F32)] * 3,
        compiler_params=_params(("arbitrary",)),
    )(ya, yb, merged, doa, dob, dx2)


def _dproj_specs_ordered(tm):
    def spec(lo, n):
        def index(k, i, order_ref):
            seg = order_ref[k]
            mine = jnp.logical_and(seg >= lo, seg < lo + n)
            return jnp.where(mine, seg - lo, 0), jnp.where(mine, i, 0), 0
        return pl.BlockSpec((None, tm, D), index)
    return [spec(0, 2), spec(2, 4), spec(6, 2)]


def _dproj_pick(j, da_ref, db_ref, dc_ref, use):
    @pl.when(j < 2)
    def _():
        use(da_ref[...])

    @pl.when(jnp.logical_and(j >= 2, j < 6))
    def _():
        use(db_ref[...])

    @pl.when(j >= 6)
    def _():
        use(dc_ref[...])


RS_X, RS_Y, RS_XY = 0, 1, 2
RS_ROLES = ((RS_XY, RS_X, RS_Y), (RS_Y, RS_XY, RS_X))


def _rs_flip(rel, x, y):
    return ((1 - x, y), (x, 1 - y), (1 - x, 1 - y))[rel]


def _rs_order(x, y, c):
    order = []
    for s in range(4):
        chip = []
        for core in (0, 1):
            px, py = _rs_flip(RS_ROLES[core][s], x, y) if s < 3 else (x, y)
            chip.append(2 * px + py)
        keep = jnp.where(c == 0, chip[0], chip[1])
        give = jnp.where(c == 0, chip[1], chip[0])
        order += [2 * give + 1 - c, 2 * keep + c]
    return jnp.stack(order).astype(jnp.int32)


def _inproj_wgrad_rs(h, dpa, dpb, dpc, order, smalls):
    t = h.shape[0]
    tm = 1024
    nt = t // tm
    nsm = len(smalls)

    def body(order_ref, h_hbm, da_hbm, db_hbm, dc_hbm, *rest):
        small_refs, parts_ref, rest = rest[:nsm], rest[nsm], rest[nsm + 1:]
        all_refs, rest = rest[:nsm], rest[nsm:]
        (acc, sib, outb, far, h_s, stage, give_send, give_recv, sum_send, sum_recv, far_send, far_recv, own_sem,
         small_send, small_recv, small_own, h_sem, d_sems) = rest
        k = pl.program_id(0)
        i = nt - 1
        x, y, c = _place()
        own, first, arrive, forward, others = _gather_copies(small_refs, all_refs, small_send, small_recv, small_own)

        def d_start(seg, n, buf):
            for ref, lo, cnt in ((da_hbm, 0, 2), (db_hbm, 2, 4), (dc_hbm, 6, 2)):
                @pl.when(jnp.logical_and(seg >= lo, seg < lo + cnt))
                def _():
                    pltpu.make_async_copy(ref.at[seg - lo, pl.ds(n * tm, tm), :], stage.at[buf], d_sems.at[buf]).start()

        def d_wait(buf):
            pltpu.make_async_copy(da_hbm.at[0, pl.ds(0, tm), :], stage.at[buf], d_sems.at[buf]).wait()

        @pl.when(k == 0)
        def _():
            for cp in own + first:
                cp.start()
            load = pltpu.make_async_copy(h_hbm, h_s, h_sem)
            load.start()
            d_start(order_ref[0], 0, 0)
            load.wait()

        @pl.when(k == 2)
        def _():
            for came, on in zip(arrive, forward):
                came.wait_recv()
                on.start()

        total = None
        for n in range(nt):
            d_wait(n % 2)
            if n + 1 < nt:
                d_start(order_ref[k], n + 1, (n + 1) % 2)
            part = _dot_tn(h_s[pl.ds(n * tm, tm), :], stage[n % 2])
            total = part if total is None else total + part
            if n == nt - 1:
                @pl.when(k < NSEG - 1)
                def _():
                    d_start(order_ref[jnp.minimum(k + 1, NSEG - 1)], 0, 0)
        acc[k % 2] = total

        def give_copy(s):
            return pltpu.make_async_remote_copy(
                src_ref=acc.at[0], dst_ref=sib.at[s % 2], send_sem=give_send.at[s], recv_sem=give_recv.at[s],
                device_id=(x, y, 1 - c), device_id_type=MESH)

        def sum_copy(s, core):
            slot = 0 if s < 2 else 1
            return pltpu.make_async_remote_copy(
                src_ref=outb.at[s], dst_ref=parts_ref.at[slot], send_sem=sum_send.at[slot], recv_sem=sum_recv.at[slot],
                device_id=(*_rs_flip(RS_ROLES[core][s], x, y), core), device_id_type=MESH)

        def far_copy(s, core):
            return pltpu.make_async_remote_copy(
                src_ref=outb.at[s], dst_ref=far, send_sem=far_send, recv_sem=far_recv,
                device_id=(*_rs_flip(RS_X if core == 0 else RS_Y, x, y), core), device_id_type=MESH)

        own_copy = pltpu.make_async_copy(outb.at[3], parts_ref.at[2], own_sem)

        def send_of(core, s):
            return far_copy(s, core) if RS_ROLES[core][s] == RS_XY else sum_copy(s, core)

        for s in range(4):
            @pl.when(jnp.logical_and(k == 2 * s, i == nt - 1))
            def _():
                give_copy(s).start()

            @pl.when(jnp.logical_and(k == 2 * s + 1, i == nt - 1))
            def _():
                give_copy(s).wait_recv()
                if s == 2:
                    far_copy(s, 0).wait_recv()
                    outb[s] = (acc[1] + sib[s % 2] + far[...].astype(F32)).astype(BF16)
                else:
                    outb[s] = (acc[1] + sib[s % 2]).astype(BF16)
                give_copy(s).wait_send()
                if s < 3:
                    for core in (0, 1):
                        @pl.when(c == core)
                        def _():
                            send_of(core, s).start()
                else:
                    own_copy.start()

        @pl.when(jnp.logical_and(k == NSEG - 1, i == nt - 1))
        def _():
            for slot in (0, 1):
                sum_copy(2 * slot, 0).wait_recv()
            for s in range(3):
                send_of(0, s).wait_send()
            own_copy.wait()
            for cp in others:
                cp.wait_recv()
            for cp in first + forward:
                cp.wait_send()
            for cp in own:
                cp.wait()

    return pl.pallas_call(
        body, name="inproj_wgrad_rs",
        grid_spec=pltpu.PrefetchScalarGridSpec(
            num_scalar_prefetch=1, grid=(NSEG,),
            in_specs=[ANY] * (4 + nsm),
            out_specs=[ANY] * (1 + nsm),
            scratch_shapes=[pltpu.VMEM((2, D, D), F32), pltpu.VMEM((2, D, D), F32), pltpu.VMEM((4, D, D), BF16),
                            pltpu.VMEM((D, D), BF16), pltpu.VMEM((t, D), BF16), pltpu.VMEM((2, tm, D), BF16),
                            pltpu.SemaphoreType.DMA((4,)), pltpu.SemaphoreType.DMA((4,)),
                            pltpu.SemaphoreType.DMA((2,)), pltpu.SemaphoreType.DMA((2,)),
                            pltpu.SemaphoreType.DMA, pltpu.SemaphoreType.DMA, pltpu.SemaphoreType.DMA,
                            pltpu.SemaphoreType.DMA((nsm, 7)), pltpu.SemaphoreType.DMA((nsm, 7)),
                            pltpu.SemaphoreType.DMA((nsm,)),
                            pltpu.SemaphoreType.DMA, pltpu.SemaphoreType.DMA((2,))]),
        out_shape=[jax.ShapeDtypeStruct((3, D, D), BF16)]
        + [jax.ShapeDtypeStruct((NDEV,) + a.shape, a.dtype) for a in smalls],
        compiler_params=_params(("arbitrary",)),
    )(order, h, dpa, dpb, dpc, *smalls)


def _inproj_dgrad(dpa, dpb, dpc, wg, x2d, dx2, g_in):
    t = x2d.shape[0]
    tm = 512

    def body(da_ref, db_ref, dc_ref, w_hbm, x_ref, dx2_ref, g_ref, gx_ref, gg_ref, w_s, w_sem):
        i = pl.program_id(0)

        @pl.when(i == 0)
        def _():
            gg_ref[...] = jnp.zeros_like(gg_ref)
            load = pltpu.make_async_copy(w_hbm, w_s, w_sem)
            load.start()
            load.wait()

        dh = None
        for ref, lo in ((da_ref, 0), (db_ref, 2), (dc_ref, 6)):
            for k in range(ref.shape[0]):
                part = _dot_nt(ref[k], w_s[lo + k])
                dh = part if dh is None else dh + part
        x = x_ref[...]
        r = lax.rsqrt(jnp.mean(x * x, axis=-1, keepdims=True) + EPS)
        xh = x * r
        gg_ref[...] += jnp.sum(dh * xh, axis=0, keepdims=True)
        dxh = dh * g_ref[...]
        gx_ref[...] = dx2_ref[...] + r * (dxh - xh * jnp.mean(dxh * xh, axis=-1, keepdims=True))

    row = lambda: pl.BlockSpec((tm, D), lambda i: (i, 0))
    seg = lambda n: pl.BlockSpec((n, tm, D), lambda i: (0, i, 0))
    vec = pl.BlockSpec((1, D), lambda i: (0, 0))
    return pl.pallas_call(
        body, name="inproj_dgrad", grid=(t // tm,),
        in_specs=[seg(2), seg(4), seg(2), ANY, row(), row(), vec],
        out_specs=[row(), vec],
        out_shape=[jax.ShapeDtypeStruct((t, D), F32), jax.ShapeDtypeStruct((1, D), F32)],
        scratch_shapes=[pltpu.VMEM((NSEG, D, D), BF16), pltpu.SemaphoreType.DMA],
        compiler_params=_params(("arbitrary",)),
    )(dpa, dpb, dpc, wg, x2d, dx2, g_in)


def _adam_update(g, w, m, v):
    m_new = ADAM_B1 * m + (1.0 - ADAM_B1) * g
    v_new = ADAM_B2 * v + (1.0 - ADAM_B2) * (g * g)
    m_hat = m_new / (1.0 - ADAM_B1 ** ADAM_STEP)
    v_hat = v_new / (1.0 - ADAM_B2 ** ADAM_STEP)
    return -ADAM_LR * (m_hat / (jnp.sqrt(v_hat) + ADAM_EPS) + ADAM_WD * w), m_new, v_new


def _sum_in_order(ref):
    total = ref[0].astype(F32)
    for k in range(1, ref.shape[0]):
        total = total + ref[k].astype(F32)
    return total


def _sum_devices(arrs):
    def body(*refs):
        for a in range(len(arrs)):
            refs[len(arrs) + a][...] = _sum_in_order(refs[a])

    return pl.pallas_call(
        body, name="sum_devices",
        out_shape=[jax.ShapeDtypeStruct(a.shape[1:], F32) for a in arrs],
        compiler_params=_params(),
    )(*arrs)


def _adamw_small(me, vec_all, gx_all, ga_all, groups):
    flat = [a for grp in groups for a in grp]
    ng = len(groups)
    nshard = D // NDEV

    def body(me_ref, vec_ref, shard_ref, gx_ref, ga_ref, *refs):
        ins, outs = refs[:3 * ng], refs[3 * ng:]
        vec = _sum_in_order(vec_ref)
        shard = _sum_in_order(shard_ref)
        grads = [vec[r:r + 1, :] for r in range(6)]
        grads += [shard[0:4, :], shard[4:8, 0:DK // NDEV], _sum_in_order(gx_ref), _sum_in_order(ga_ref)]
        for n, g in enumerate(grads):
            delta, m_new, v_new = _adam_update(g, ins[3 * n][...], ins[3 * n + 1][...], ins[3 * n + 2][...])
            outs[4 * n][...] = g
            outs[4 * n + 1][...] = delta
            outs[4 * n + 2][...] = m_new
            outs[4 * n + 3][...] = v_new
        outs[4 * ng][...] = jnp.sum(vec[6:7, :], axis=1, keepdims=True)

    full = lambda a: pl.BlockSpec(a.shape, lambda i, me_ref, nd=len(a.shape): (0,) * nd)
    out_shape = [jax.ShapeDtypeStruct(w.shape, F32) for w, _, _ in groups for _ in range(4)]
    out_shape.append(jax.ShapeDtypeStruct((1, 1), F32))
    outs = pl.pallas_call(
        body, name="adamw_small",
        grid_spec=pltpu.PrefetchScalarGridSpec(
            num_scalar_prefetch=1, grid=(1,),
            in_specs=[full(vec_all),
                      pl.BlockSpec((NDEV, 8, nshard), lambda i, me_ref: (0, 1, me_ref[0])),
                      full(gx_all), full(ga_all)] + [full(a) for a in flat],
            out_specs=[full(s) for s in out_shape]),
        out_shape=out_shape,
        compiler_params=_params(("arbitrary",)),
    )(me, vec_all, vec_all, gx_all, ga_all, *flat)
    return [outs[4 * n:4 * n + 4] for n in range(ng)], outs[4 * ng]


def _adamw(name, items):
    n, rows, cols = items[0][0].shape
    tr = rows if rows <= 256 else 256
    k = len(items)

    def body(*refs):
        for a in range(k):
            p_ref, w_ref, m_ref, v_ref = refs[4 * a:4 * a + 4]
            g = _sum_in_order(p_ref)
            delta, m_new, v_new = _adam_update(g, w_ref[...], m_ref[...], v_ref[...])
            for o, val in zip(refs[4 * k + 4 * a:4 * k + 4 * a + 4], (g, delta, m_new, v_new)):
                o[...] = val

    blk = lambda: pl.BlockSpec((tr, cols), lambda i: (i, 0))
    outs = pl.pallas_call(
        body, name=name, grid=(rows // tr,),
        in_specs=[pl.BlockSpec((n, tr, cols), lambda i: (0, i, 0)), blk(), blk(), blk()] * k,
        out_specs=[blk() for _ in range(4 * k)],
        out_shape=[jax.ShapeDtypeStruct((rows, cols), F32)] * (4 * k),
        compiler_params=_params(("arbitrary",)),
    )(*[a for item in items for a in item])
    return [outs[4 * a:4 * a + 4] for a in range(k)]


ANY = pl.BlockSpec(memory_space=pl.ANY)


def _place():
    return lax.axis_index("x"), lax.axis_index("y"), lax.axis_index("c")


def _gather_copies(ins, outs, send_sems, recv_sems, own_sems):
    x, y, c = _place()
    me, sibling = (x, y, c), (x, y, 1 - c)
    chips = [(1 - x, y), (x, 1 - y), (1 - x, 1 - y)]
    n = len(ins)

    def copy(a, k, block, to, src=None):
        px, py, pc = block
        dst = outs[a].at[4 * px + 2 * py + pc]
        return pltpu.make_async_remote_copy(
            src_ref=dst if src is None else src, dst_ref=dst,
            send_sem=send_sems.at[a, k], recv_sem=recv_sems.at[a, k], device_id=to, device_id_type=MESH)

    own = [pltpu.make_async_copy(ins[a], outs[a].at[4 * x + 2 * y + c], own_sems.at[a]) for a in range(n)]
    first = []
    for a in range(n):
        first.append(copy(a, 0, me, sibling, src=ins[a]))
        first += [copy(a, 1 + j, me, (*chip, c), src=ins[a]) for j, chip in enumerate(chips)]
    arrive = [copy(a, 1 + j, (*chip, c), me) for j, chip in enumerate(chips) for a in range(n)]
    forward = [copy(a, 4 + j, (*chip, c), sibling) for j, chip in enumerate(chips) for a in range(n)]
    rest = [copy(a, 0, sibling, me) for a in range(n)]
    rest += [copy(a, 4 + j, (*chip, 1 - c), me) for a in range(n) for j, chip in enumerate(chips)]
    return own, first, arrive, forward, rest


def _sibling_copies(ins, outs, send_sems, recv_sems):
    x, y, c = _place()
    return [pltpu.make_async_remote_copy(
        src_ref=ins[a].at[2 * q + 1 - c], dst_ref=outs[a].at[q],
        send_sem=send_sems.at[a, q], recv_sem=recv_sems.at[a, q],
        device_id=(x, y, 1 - c), device_id_type=MESH) for a in range(len(ins)) for q in range(4)]


def _chip_copies(ins, outs, send_sems, recv_sems, local_sems):
    x, y, c = _place()
    my_chip = 2 * x + y
    chips = [(1 - x, y), (x, 1 - y), (1 - x, 1 - y)]
    n = len(ins)
    mine = [pltpu.make_async_copy(ins[a].at[my_chip], outs[a].at[my_chip], local_sems.at[a]) for a in range(n)]
    sends = [pltpu.make_async_remote_copy(
        src_ref=ins[a].at[2 * px + py], dst_ref=outs[a].at[my_chip],
        send_sem=send_sems.at[a, j], recv_sem=recv_sems.at[a, j],
        device_id=(px, py, c), device_id_type=MESH) for a in range(n) for j, (px, py) in enumerate(chips)]
    recvs = [pltpu.make_async_remote_copy(
        src_ref=ins[a].at[my_chip], dst_ref=outs[a].at[2 * px + py],
        send_sem=send_sems.at[a, j], recv_sem=recv_sems.at[a, j],
        device_id=(px, py, c), device_id_type=MESH) for a in range(n) for j, (px, py) in enumerate(chips)]
    return mine, sends, recvs


def _chip_sum(owns, gots, core):
    n = len(owns)
    _, rows, cols = owns[0].shape

    def body(core_ref, *refs):
        for a in range(n):
            refs[2 * n + a][...] = (refs[a][...] + refs[n + a][...]).astype(BF16)

    own_spec = pl.BlockSpec((None, rows, cols), lambda q, core_ref: (2 * q + core_ref[0], 0, 0))
    slab = pl.BlockSpec((None, rows, cols), lambda q, core_ref: (q, 0, 0))
    return pl.pallas_call(
        body, name="chip_sum",
        grid_spec=pltpu.PrefetchScalarGridSpec(
            num_scalar_prefetch=1, grid=(4,),
            in_specs=[own_spec] * n + [slab] * n, out_specs=[slab] * n),
        out_shape=[jax.ShapeDtypeStruct((4, rows, cols), BF16)] * n,
        compiler_params=_params(("arbitrary",)),
    )(core, *owns, *gots)


def _block_diag(w):
    w4 = w.reshape(NCB, 4, 64, 64)
    eye = jnp.eye(4, dtype=w.dtype)
    return (w4[:, :, :, None, :] * eye[None, :, None, :, None]).reshape(NCB, CB, CB)


def _block_diag_back(g):
    g5 = g.reshape(NCB, 4, 64, 4, 64)
    return jnp.stack([g5[:, m, :, m, :] for m in range(4)], axis=1).reshape(16, 64, 64)


def kernel(x, norm_in, w_in, conv_w, conv_b, gate_x_w, gate_x_b, gate_a_w, gate_a_b, lru_lambda, gn_gain, w_proj_a, w_proj_b, w_out, norm_final, loss_target, m_norm_in, m_w_in, m_conv_w, m_conv_b, m_gate_x_w, m_gate_x_b, m_gate_a_w, m_gate_a_b, m_lru_lambda, m_gn_gain, m_w_proj_a, m_w_proj_b, m_w_out, m_norm_final, v_norm_in, v_w_in, v_conv_w, v_conv_b, v_gate_x_w, v_gate_x_b, v_gate_a_w, v_gate_a_b, v_lru_lambda, v_gn_gain, v_w_proj_a, v_w_proj_b, v_w_out, v_norm_final):
    xi, yi, ci = _place()
    me = 4 * xi + 2 * yi + ci
    core = ci.astype(jnp.int32).reshape(1)
    nshard = D // NDEV
    nb = x.shape[0]
    t = nb * S
    x2d = x.reshape(t, D)
    tgt2d = loss_target.reshape(t, D)
    g_final = norm_final.reshape(1, D)
    wbd = jnp.concatenate([_block_diag(gate_x_w[0]), _block_diag(gate_a_w[0])], axis=-1).astype(BF16)
    tables = _retention_tables()

    wp_own = jnp.concatenate([w_proj_a[0], w_proj_b[0], w_out[0]], axis=0).astype(BF16)
    tiny = jnp.concatenate([conv_w[0], jnp.pad(gn_gain[0], ((0, 0), (0, nshard - DK // NDEV)))], axis=0)
    proj, h, wg, tiny_g = _inproj_gather(x2d, norm_in, w_in[0].astype(BF16), tiny, *_gather_order(xi, yi, ci))
    conv_w_full = tiny_g[:, 0:4, :].transpose(1, 0, 2).reshape(4, D)
    gain3 = tiny_g[:, 4:8, :DK // NDEV].transpose(1, 0, 2).reshape(HEADS, 1, DK)

    ya, hs, xc, gi, gr = _lru_fwd(proj, conv_w_full, conv_b, wbd, gate_x_b, gate_a_b, lru_lambda, nb)
    yb, qr, kr, o, rs, wpg = _ret_fwd(proj, gain3, tables, nb, wp_own)
    dx2, dya, dyb, dpc, merged, doa, dob, g_fin, loss_vec = _tail(ya, yb, proj, x2d, tgt2d, wpg, g_final)
    g_pa, g_pb, g_out = _tail_wgrad(ya, yb, merged, doa, dob, dx2)

    own = [g.reshape(NDEV, nshard, D) for g in (g_pa, g_pb, g_out)]
    dpa, g_wbd, g_vec, *got = _lru_bwd(proj, hs, xc, gi, gr, dya, conv_w_full, wbd, lru_lambda, nb, own)
    sums = _chip_sum(own, got, core)
    dpb, g_gain, *parts = _ret_bwd(proj, qr, kr, o, rs, dyb, gain3, tables, nb, sums)

    grad_x, g_norm_in = _inproj_dgrad(dpa, dpb, dpc, wg, x2d, dx2, norm_in)
    grad_x = grad_x.reshape(nb, S, D)

    gain_rows = jnp.pad(g_gain.reshape(HEADS, NDEV, DK // NDEV), ((0, 0), (0, 0), (0, nshard - DK // NDEV)))
    vec = jnp.concatenate([g_norm_in, g_vec[0:4], g_fin, loss_vec, jnp.zeros((1, D), F32), g_vec[4:8],
                           gain_rows.reshape(HEADS, D)], axis=0)
    g_gx = _block_diag_back(g_wbd[:, :, :CB]).reshape(D // 2, 128)
    g_ga = _block_diag_back(g_wbd[:, :, CB:]).reshape(D // 2, 128)
    parts_in, vec_all, gx_all, ga_all = _inproj_wgrad_rs(h, dpa, dpb, dpc, _rs_order(xi, yi, ci),
                                                         [vec, g_gx, g_ga])
    gx_all, ga_all = [g.reshape(1, D, 64) for g in _sum_devices([gx_all, ga_all])]
    parts = [parts_in] + list(parts)

    res = {}
    (out,) = _adamw("adamw_w_in", [(parts[0], w_in[0], m_w_in[0], v_w_in[0])])
    res["w_in"] = [o[None] for o in out]
    square = [("w_proj_a", w_proj_a, m_w_proj_a, v_w_proj_a), ("w_proj_b", w_proj_b, m_w_proj_b, v_w_proj_b),
              ("w_out", w_out, m_w_out, v_w_out)]
    outs = _adamw("adamw_square", [(parts[1 + k], w[0], m[0], v[0]) for k, (_, w, m, v) in enumerate(square)])
    for (nm, _, _, _), out in zip(square, outs):
        res[nm] = [o[None] for o in out]

    row = lambda a: a.reshape(1, D)
    gate = lambda a: a.reshape(D, 64)
    groups = [("norm_in", norm_in, m_norm_in, v_norm_in, row), ("conv_b", conv_b, m_conv_b, v_conv_b, row),
              ("gate_x_b", gate_x_b, m_gate_x_b, v_gate_x_b, row), ("gate_a_b", gate_a_b, m_gate_a_b, v_gate_a_b, row),
              ("lru_lambda", lru_lambda, m_lru_lambda, v_lru_lambda, row),
              ("norm_final", norm_final, m_norm_final, v_norm_final, row),
              ("conv_w", conv_w, m_conv_w, v_conv_w, lambda a: a[0]), ("gn_gain", gn_gain, m_gn_gain, v_gn_gain, lambda a: a[0]),
              ("gate_x_w", gate_x_w, m_gate_x_w, v_gate_x_w, gate), ("gate_a_w", gate_a_w, m_gate_a_w, v_gate_a_w, gate)]
    small_out, loss = _adamw_small(me.astype(jnp.int32).reshape(1), vec_all, gx_all, ga_all,
                                   [tuple(view(a) for a in (w, m, v)) for _, w, m, v, view in groups])
    for (nm, w, _, _, _), out in zip(groups, small_out):
        res[nm] = [o.reshape(w.shape) for o in out]
    loss = loss.reshape(())

    order = ["norm_in", "w_in", "conv_w", "conv_b", "gate_x_w", "gate_x_b", "gate_a_w", "gate_a_b", "lru_lambda",
             "gn_gain", "w_proj_a", "w_proj_b", "w_out", "norm_final"]
    outs = [loss, grad_x]
    for k in range(4):
        outs += [res[nm][k] for nm in order]
    return tuple(outs)
```

```python
import numpy as np

import jax
import jax.numpy as jnp
from jax import lax
from jax.experimental import pallas as pl
from jax.experimental.pallas import tpu as pltpu

F32 = jnp.float32
BF16 = jnp.bfloat16
MESH = pl.DeviceIdType.MESH

D = 1024
S = 2048
NSEG = 8
NDEV = 8
HEADS = 4
DK = 256
CH = 256
NCH = S // CH
CB = 256
NCB = D // CB
RC = 512
RC_CONV = 128
SCAN_GROUP = 16
EPS = 1e-6
LRU_C = 8.0
VMEM_LIMIT = 56 * 1024 * 1024

ADAM_LR = 0.001
ADAM_B1 = 0.9
ADAM_B2 = 0.999
ADAM_EPS = 1e-08
ADAM_WD = 0.01
ADAM_STEP = 10


def _params(sem=None):
    return pltpu.CompilerParams(dimension_semantics=sem, vmem_limit_bytes=VMEM_LIMIT)


def _dot(a, b):
    return jnp.dot(a, b, preferred_element_type=F32)


def _dot_nt(a, b):
    return lax.dot_general(a, b, (((1,), (1,)), ((), ())), preferred_element_type=F32)


def _dot_tn(a, b):
    return lax.dot_general(a, b, (((0,), (0,)), ((), ())), preferred_element_type=F32)


def _sigmoid(x):
    return jax.nn.sigmoid(x)


def _expm1_nonpos(x):
    poly = x * (1.0 + x * (0.5 + x * (1.0 / 6.0 + x * (1.0 / 24.0))))
    return jnp.where(x > -0.05, poly, jnp.exp(x) - 1.0)


def _softplus(x):
    return jnp.maximum(x, 0.0) + jnp.log(1.0 + jnp.exp(-jnp.abs(x)))


def _rows(c, n):
    return pl.ds(pl.multiple_of(c * n, n), n)


def _window_before(ref, c, n):
    r0 = c * n
    if ref.dtype == BF16:
        prev = ref[pl.ds(pl.multiple_of(jnp.maximum(r0 - 16, 0), 16), 16), :].astype(F32)[8:, :]
    else:
        prev = ref[pl.ds(pl.multiple_of(jnp.maximum(r0 - 8, 0), 8), 8), :]
    prev = jnp.where(c > 0, prev, 0.0)
    return jnp.concatenate([prev, ref[_rows(c, n), :].astype(F32)], axis=0)


def _shift_down(win, s, n):
    if s == 0:
        return win[8:, :]
    return pltpu.roll(win, s, 0)[8:, :]


def _shift_up(win, s, n):
    if s == 0:
        return win[:n, :]
    return pltpu.roll(win, n + 8 - s, 0)[:n, :]


HALF = D // 2
GATHER_SLOTS = [("own", None, 0), ("own", None, 1), ("sib", None, 0), ("sib", None, 1)]
for _j, _h in ((0, 0), (1, 0), (0, 1), (1, 1), (2, 0), (2, 1)):
    GATHER_SLOTS += [("ici", _j, _h), ("fwd", _j, _h)]
NSLOT = len(GATHER_SLOTS)


def _gather_order(x, y, c):
    chips = [(1 - x, y), (x, 1 - y), (1 - x, 1 - y)]
    segs, halves = [], []
    for kind, j, h in GATHER_SLOTS:
        if kind == "own":
            seg = 4 * x + 2 * y + c
        elif kind == "sib":
            seg = 4 * x + 2 * y + 1 - c
        else:
            px, py = chips[j]
            seg = 4 * px + 2 * py + (c if kind == "ici" else 1 - c)
        segs.append(seg)
        halves.append(h)
    return jnp.stack(segs).astype(jnp.int32), jnp.asarray(halves, jnp.int32)


def _inproj_gather(x2d, g_in, w_own, tiny_own, order, halves):
    t = x2d.shape[0]
    tm = 2048
    nt = t // tm
    tx = 1024
    nx = tm // tx

    def body(order_ref, half_ref, g_ref, x_hbm, w_own_ref, tiny_own_ref,
             proj_ref, h_hbm, wg_ref, tinyg_ref,
             w_all, h_all, x_s, send_sems, recv_sems, own_sems, out_sems, tiny_send, tiny_recv, tiny_own_sem,
             x_sems, h_sem):
        k, i = pl.program_id(0), pl.program_id(1)
        x, y, c = _place()
        me, sibling = (x, y, c), (x, y, 1 - c)
        mine = 4 * x + 2 * y + c
        chips = [(1 - x, y), (x, 1 - y), (1 - x, 1 - y)]

        def copy(h, n, block, to, own_src=False):
            px, py, pc = block
            dst = w_all.at[4 * px + 2 * py + pc, h]
            return pltpu.make_async_remote_copy(
                src_ref=w_own_ref.at[:, pl.ds(h * HALF, HALF)] if own_src else dst, dst_ref=dst,
                send_sem=send_sems.at[h, n], recv_sem=recv_sems.at[h, n], device_id=to, device_id_type=MESH)

        def tiny_copy(n, block, to, own_src=False):
            px, py, pc = block
            dst = tinyg_ref.at[4 * px + 2 * py + pc]
            return pltpu.make_async_remote_copy(
                src_ref=tiny_own_ref if own_src else dst, dst_ref=dst,
                send_sem=tiny_send.at[n], recv_sem=tiny_recv.at[n], device_id=to, device_id_type=MESH)

        def own_copy(h):
            return pltpu.make_async_copy(w_own_ref.at[:, pl.ds(h * HALF, HALF)], w_all.at[mine, h], own_sems.at[h])

        tiny_mine = pltpu.make_async_copy(tiny_own_ref, tinyg_ref.at[mine], tiny_own_sem)

        def keep_copy(n):
            h = GATHER_SLOTS[n][2]
            return pltpu.make_async_copy(w_all.at[order_ref[n], h], wg_ref.at[order_ref[n], :, pl.ds(h * HALF, HALF)],
                                         out_sems.at[n])

        near = [(0, sibling), (1, (*chips[0], c)), (2, (*chips[1], c))]
        first = [copy(h, n, me, to, True) for h in (0, 1) for n, to in near]
        tiny_first = [tiny_copy(0, me, sibling, True)] + [tiny_copy(1 + j, me, (*chip, c), True) for j, chip in enumerate(chips)]

        def relay(h, j):
            seg = w_all.at[4 * chips[j][0] + 2 * chips[j][1] + c, h]
            return pltpu.make_async_remote_copy(
                src_ref=seg, dst_ref=seg, send_sem=send_sems.at[h, 3], recv_sem=recv_sems.at[h, 3],
                device_id=(*chips[1 - j], c), device_id_type=MESH)

        for n, (kind, j, h) in enumerate(GATHER_SLOTS):
            @pl.when(jnp.logical_and(k == n, i == 0))
            def _():
                if n == 0:
                    own_copy(0).start()
                    own_copy(1).start()
                    tiny_mine.start()
                    for cp in first + tiny_first:
                        cp.start()
                if kind == "own":
                    own_copy(h).wait()
                elif kind == "sib":
                    copy(h, 0, sibling, me).wait_recv()
                elif kind == "ici":
                    copy(h, 1 + j, (*chips[j], c), me).wait_recv()
                    copy(h, 4 + j, (*chips[j], c), sibling).start()
                    if j < 2:
                        @pl.when(c == j)
                        def _():
                            relay(h, j).start()
                else:
                    copy(h, 4 + j, (*chips[j], 1 - c), me).wait_recv()
                keep_copy(n).start()

        rows = pl.ds(pl.multiple_of(i * tm, tm), tm)

        def x_copy(n):
            return pltpu.make_async_copy(x_hbm.at[pl.ds(n * tx, tx), :], x_s.at[n % 2], x_sems.at[n % 2])

        keep_h = pltpu.make_async_copy(h_all, h_hbm, h_sem)

        for step in range(nt):
            @pl.when(jnp.logical_and(k == 0, i == step))
            def _():
                if step == 0:
                    x_copy(0).start()
                for n in range(step * nx, (step + 1) * nx):
                    x_copy(n).wait()
                    if n + 1 < nt * nx:
                        x_copy(n + 1).start()
                    xv = x_s[n % 2]
                    r = lax.rsqrt(jnp.mean(xv * xv, axis=-1, keepdims=True) + EPS)
                    h_all[pl.ds(n * tx, tx), :] = (xv * r * g_ref[...]).astype(BF16)
                if step == nt - 1:
                    keep_h.start()

        proj_ref[...] = _dot(h_all[rows, :], w_all[order_ref[k], half_ref[k]]).astype(BF16)

        @pl.when(jnp.logical_and(k == NSLOT - 1, i == nt - 1))
        def _():
            for j, chip in enumerate(chips):
                tiny_copy(1 + j, (*chip, c), me).wait_recv()
                tiny_copy(4 + j, (*chip, c), sibling).start()
            tiny_copy(0, sibling, me).wait_recv()
            for j, chip in enumerate(chips):
                tiny_copy(4 + j, (*chip, 1 - c), me).wait_recv()
            for cp in first + tiny_first:
                cp.wait_send()
            for j, chip in enumerate(chips):
                tiny_copy(4 + j, (*chip, c), sibling).wait_send()
                for h in (0, 1):
                    copy(h, 4 + j, (*chip, c), sibling).wait_send()
            for h in (0, 1):
                relay(h, 0).wait_send()
            tiny_mine.wait()
            keep_h.wait()
            for n in range(NSLOT):
                keep_copy(n).wait()

    return pl.pallas_call(
        body, name="inproj_gather",
        grid_spec=pltpu.PrefetchScalarGridSpec(
            num_scalar_prefetch=2, grid=(NSLOT, nt),
            in_specs=[pl.BlockSpec((1, D), lambda k, i, order_ref, half_ref: (0, 0)),
                      ANY, ANY, ANY],
            out_specs=[pl.BlockSpec((None, tm, HALF), lambda k, i, order_ref, half_ref: (order_ref[k], i, half_ref[k])),
                       ANY, ANY, ANY],
            scratch_shapes=[pltpu.VMEM((NDEV, 2, D, HALF), BF16), pltpu.VMEM((t, D), BF16), pltpu.VMEM((2, tx, D), F32),
                            pltpu.SemaphoreType.DMA((2, 7)), pltpu.SemaphoreType.DMA((2, 7)),
                            pltpu.SemaphoreType.DMA((2,)), pltpu.SemaphoreType.DMA((NSLOT,)),
                            pltpu.SemaphoreType.DMA((7,)), pltpu.SemaphoreType.DMA((7,)), pltpu.SemaphoreType.DMA,
                            pltpu.SemaphoreType.DMA((2,)), pltpu.SemaphoreType.DMA]),
        out_shape=[jax.ShapeDtypeStruct((NSEG, t, D), BF16), jax.ShapeDtypeStruct((t, D), BF16),
                   jax.ShapeDtypeStruct((NDEV,) + w_own.shape, BF16),
                   jax.ShapeDtypeStruct((NDEV,) + tiny_own.shape, F32)],
        compiler_params=_params(("arbitrary", "arbitrary")),
    )(order, halves, g_in, x2d, w_own, tiny_own)


def _tile_scan(a, u):
    row = lax.broadcasted_iota(jnp.int32, a.shape, 0)
    for d in (1, 2, 4):
        m = row >= d
        a_sh = pltpu.roll(a, d, 0)
        u_sh = pltpu.roll(u, d, 0)
        u = jnp.where(m, a * u_sh + u, u)
        a = jnp.where(m, a * a_sh, a)
    return a, u


def _tile_scan_rev(a, w):
    row = lax.broadcasted_iota(jnp.int32, a.shape, 0)
    for d in (1, 2, 4):
        m = row < 8 - d
        a_sh = pltpu.roll(a, 8 - d, 0)
        w_sh = pltpu.roll(w, 8 - d, 0)
        w = jnp.where(m, a * w_sh + w, w)
        a = jnp.where(m, a * a_sh, a)
    return a, w


def _lru_gates(xa_ref, c, cw_ref, cb_ref, wbd_ref, bx_ref, ba_ref, sp):
    win = _window_before(xa_ref, c, RC)
    xc = cb_ref[...] + cw_ref[3:4, :] * _shift_down(win, 0, RC)
    for s in (1, 2, 3):
        xc = xc + cw_ref[3 - s:4 - s, :] * _shift_down(win, s, RC)
    z = _dot(xc.astype(BF16), wbd_ref[...])
    gi = _sigmoid(z[:, :CB] + bx_ref[...])
    gr = _sigmoid(z[:, CB:] + ba_ref[...])
    log_a = -LRU_C * gr * sp
    return win, xc, gi, gr, log_a


def _lru_fwd(proj, conv_w, conv_b, wbd, bx, ba, lam, nb):
    t = nb * S

    def body(xa_ref, ga_ref, cw_ref, cb_ref, wbd_ref, bx_ref, ba_ref, lam_ref,
             ya_ref, hs_ref, xc_ref, gi_ref, gr_ref, a_s, u_s):
        sp = _softplus(-lam_ref[...])

        def gates(c, carry):
            _, xc, gi, gr, log_a = _lru_gates(xa_ref, c, cw_ref, cb_ref, wbd_ref, bx_ref, ba_ref, sp)
            rows = _rows(c, RC)
            a_s[rows, :] = jnp.exp(log_a)
            u_s[rows, :] = jnp.sqrt(-_expm1_nonpos(2.0 * log_a)) * (gi * xc)
            xc_ref[rows, :] = xc
            gi_ref[rows, :] = gi
            gr_ref[rows, :] = gr
            return carry

        lax.fori_loop(0, S // RC, gates, 0)

        def scan(g, h):
            for k in range(SCAN_GROUP):
                rows = pl.ds(pl.multiple_of(g * (8 * SCAN_GROUP), 8 * SCAN_GROUP) + 8 * k, 8)
                a_cum, u_cum = _tile_scan(a_s[rows, :], u_s[rows, :])
                hs_ref[rows, :] = u_cum + a_cum * h
                h = u_cum[7:8, :] + a_cum[7:8, :] * h
            return h

        lax.fori_loop(0, S // (8 * SCAN_GROUP), scan, jnp.zeros((1, CB), F32))

        def gate_out(c, carry):
            ga = ga_ref[_rows(c, RC), :].astype(F32)
            ya_ref[_rows(c, RC), :] = (ga * _sigmoid(ga) * hs_ref[_rows(c, RC), :]).astype(BF16)
            return carry

        lax.fori_loop(0, S // RC, gate_out, 0)

    vec = pl.BlockSpec((1, CB), lambda b, cb: (0, cb))
    blk = pl.BlockSpec((S, CB), lambda b, cb: (b, cb))
    return pl.pallas_call(
        body, name="lru_fwd", grid=(nb, NCB),
        in_specs=[pl.BlockSpec((None, S, CB), lambda b, cb: (0, b, cb)),
                  pl.BlockSpec((None, S, CB), lambda b, cb: (1, b, cb)),
                  pl.BlockSpec((4, CB), lambda b, cb: (0, cb)),
                  vec,
                  pl.BlockSpec((None, CB, 2 * CB), lambda b, cb: (cb, 0, 0)),
                  vec, vec, vec],
        out_specs=[blk] + [pl.BlockSpec((None, None, S, CB), lambda b, cb: (b, cb, 0, 0))] * 4,
        out_shape=[jax.ShapeDtypeStruct((t, D), BF16)] + [jax.ShapeDtypeStruct((nb, NCB, S, CB), F32)] * 4,
        scratch_shapes=[pltpu.VMEM((S, CB), F32), pltpu.VMEM((S, CB), F32)],
        compiler_params=_params(("arbitrary", "arbitrary")),
    )(proj, proj, conv_w, conv_b, wbd, bx, ba, lam)


def _lru_bwd(proj, hs, xc_f, gi_f, gr_f, dya, conv_w, wbd, lam, nb, give):
    t = nb * S
    ng = len(give)

    def body(xa_ref, ga_ref, hs_ref, xc_s, gi_s, gr_s, dya_ref, cw_ref, wbd_ref, lam_ref, *rest):
        give_refs, rest = rest[:ng], rest[ng:]
        dp_ref, dwbd_ref, vec_ref = rest[:3]
        got_refs, rest = rest[3:3 + ng], rest[3 + ng:]
        a_s, dl_s, dh_s, dxc_s, acc_s, send_sems, recv_sems = rest
        b = pl.program_id(1)
        exchange = _sibling_copies(give_refs, got_refs, send_sems, recv_sems)

        @pl.when(jnp.logical_and(pl.program_id(0) == 0, b == 0))
        def _():
            for cp in exchange:
                cp.start()

        lam_v = lam_ref[...]
        sp = _softplus(-lam_v)
        acc_s[...] = jnp.zeros_like(acc_s)

        @pl.when(b == 0)
        def _():
            dwbd_ref[...] = jnp.zeros_like(dwbd_ref)
            vec_ref[...] = jnp.zeros_like(vec_ref)

        def gates(c, carry):
            rows = _rows(c, RC)
            a_s[rows, :] = jnp.exp(-LRU_C * gr_s[rows, :] * sp)
            ga = ga_ref[rows, :].astype(F32)
            sg = _sigmoid(ga)
            dya_c = dya_ref[rows, :].astype(F32)
            dl_s[rows, :] = dya_c * (ga * sg)
            dp_ref[1, rows, :] = (dya_c * hs_ref[rows, :] * (sg * (1.0 + ga * (1.0 - sg)))).astype(BF16)
            return carry

        lax.fori_loop(0, S // RC, gates, 0)

        def scan(i, g_in):
            base = pl.multiple_of((S // (8 * SCAN_GROUP) - 1 - i) * (8 * SCAN_GROUP), 8 * SCAN_GROUP)
            row = lax.broadcasted_iota(jnp.int32, (8, CB), 0)
            for k in reversed(range(SCAN_GROUP)):
                rows = pl.ds(base + 8 * k, 8)
                a = a_s[rows, :]
                dl = dl_s[rows, :]
                a_cum, g_loc = _tile_scan_rev(a, a * dl)
                g = g_loc + a_cum * g_in
                dh_s[rows, :] = dl + jnp.where(row < 7, pltpu.roll(g, 7, 0), g_in)
                g_in = g_loc[0:1, :] + a_cum[0:1, :] * g_in
            return g_in

        lax.fori_loop(0, S // (8 * SCAN_GROUP), scan, jnp.zeros((1, CB), F32))

        dxc_s[pl.ds(S, 8), :] = jnp.zeros((8, CB), F32)

        def grads(c, carry):
            rows = _rows(c, RC)
            dh = dh_s[rows, :]
            h_prev = _shift_down(_window_before(hs_ref, c, RC), 1, RC)
            xc, gi, gr, a = xc_s[rows, :], gi_s[rows, :], gr_s[rows, :], a_s[rows, :]
            mult = jnp.sqrt(-_expm1_nonpos(-2.0 * LRU_C * gr * sp))
            dmult = dh * gi * xc
            d_log_a = dh * h_prev * a - dmult * (a * a) / mult
            dzi = dh * mult * xc * gi * (1.0 - gi)
            dzr = d_log_a * (-LRU_C * sp) * gr * (1.0 - gr)
            dz = jnp.concatenate([dzi, dzr], axis=1).astype(BF16)
            dxc_s[rows, :] = dh * mult * gi + _dot_nt(dz, wbd_ref[...])
            dwbd_ref[...] += _dot_tn(xc.astype(BF16), dz)
            acc_s[1:2, :] += jnp.sum(dzi, axis=0, keepdims=True)
            acc_s[2:3, :] += jnp.sum(dzr, axis=0, keepdims=True)
            acc_s[3:4, :] += jnp.sum(d_log_a * (-LRU_C * gr), axis=0, keepdims=True)
            return carry

        lax.fori_loop(0, S // RC, grads, 0, unroll=2)

        def conv_bwd(c, carry):
            rows = _rows(c, RC_CONV)
            dwin = dxc_s[pl.ds(pl.multiple_of(c * RC_CONV, RC_CONV), RC_CONV + 8), :]
            dxc = dwin[:RC_CONV, :]
            xwin = _window_before(xa_ref, c, RC_CONV)
            dxa = cw_ref[3:4, :] * dxc
            acc_s[0:1, :] += jnp.sum(dxc, axis=0, keepdims=True)
            acc_s[7:8, :] += jnp.sum(dxc * _shift_down(xwin, 0, RC_CONV), axis=0, keepdims=True)
            for s in (1, 2, 3):
                dxa = dxa + cw_ref[3 - s:4 - s, :] * _shift_up(dwin, s, RC_CONV)
                acc_s[7 - s:8 - s, :] += jnp.sum(dxc * _shift_down(xwin, s, RC_CONV), axis=0, keepdims=True)
            dp_ref[0, rows, :] = dxa.astype(BF16)
            return carry

        lax.fori_loop(0, S // RC_CONV, conv_bwd, 0)

        row = lax.broadcasted_iota(jnp.int32, acc_s.shape, 0)
        vec_ref[...] += jnp.where(row == 3, acc_s[...] * (-_sigmoid(-lam_v)), acc_s[...])

        @pl.when(jnp.logical_and(pl.program_id(0) == NCB - 1, b == nb - 1))
        def _():
            for cp in exchange:
                cp.wait()

    vec = pl.BlockSpec((1, CB), lambda cb, b: (0, cb))
    blk = pl.BlockSpec((S, CB), lambda cb, b: (b, cb))
    own = pl.BlockSpec((None, None, S, CB), lambda cb, b: (b, cb, 0, 0))
    return pl.pallas_call(
        body, name="lru_bwd", grid=(NCB, nb),
        in_specs=[pl.BlockSpec((None, S, CB), lambda cb, b: (0, b, cb)),
                  pl.BlockSpec((None, S, CB), lambda cb, b: (1, b, cb)),
                  own, own, own, own, blk,
                  pl.BlockSpec((4, CB), lambda cb, b: (0, cb)),
                  pl.BlockSpec((None, CB, 2 * CB), lambda cb, b: (cb, 0, 0)),
                  vec] + [ANY] * ng,
        out_specs=[pl.BlockSpec((2, S, CB), lambda cb, b: (0, b, cb)),
                   pl.BlockSpec((None, CB, 2 * CB), lambda cb, b: (cb, 0, 0)),
                   pl.BlockSpec((8, CB), lambda cb, b: (0, cb))] + [ANY] * ng,
        out_shape=[jax.ShapeDtypeStruct((2, t, D), BF16),
                   jax.ShapeDtypeStruct((NCB, CB, 2 * CB), F32),
                   jax.ShapeDtypeStruct((8, D), F32)]
        + [jax.ShapeDtypeStruct((4,) + g.shape[1:], g.dtype) for g in give],
        scratch_shapes=[pltpu.VMEM((S, CB), F32), pltpu.VMEM((S, CB), F32), pltpu.VMEM((S, CB), F32),
                        pltpu.VMEM((S + 8, CB), F32), pltpu.VMEM((8, CB), F32),
                        pltpu.SemaphoreType.DMA((ng, 4)), pltpu.SemaphoreType.DMA((ng, 4))],
        compiler_params=_params(("arbitrary", "arbitrary")),
    )(proj, proj, hs, xc_f, gi_f, gr_f, dya, conv_w, wbd, lam, *give)


def _retention_tables():
    f32 = np.float32
    log_g = np.log1p(-(f32(2.0) ** (f32(-5.0) - np.arange(HEADS, dtype=f32)))).astype(f32)
    idx = np.arange(CH, dtype=f32)
    diff = idx[:, None] - idx[None, :]
    inner = np.where(diff >= 0, np.exp(np.maximum(diff, f32(0.0))[None] * log_g[:, None, None]), f32(0.0)).astype(f32)
    cross = np.exp((idx[None, :] + f32(1.0)) * log_g[:, None]).astype(f32)
    state = np.exp((f32(CH - 1.0) - idx[None, :]) * log_g[:, None]).astype(f32)
    cross = np.ascontiguousarray(np.broadcast_to(cross[:, :, None], (HEADS, CH, DK)))
    state = np.ascontiguousarray(np.broadcast_to(state[:, :, None], (HEADS, CH, DK)))
    half = DK // 2
    freqs = (f32(10000.0) ** (-np.arange(half, dtype=f32) / f32(half))).astype(f32)
    ang = (np.arange(S, dtype=f32)[:, None] * freqs[None, :]).astype(f32)
    return tuple(jnp.asarray(a) for a in (inner, cross, state, np.cos(ang).astype(f32), np.sin(ang).astype(f32)))


def _rotate(x, cos, sin):
    half = DK // 2
    x1, x2 = x[:, :half], x[:, half:]
    return jnp.concatenate([x1 * cos - x2 * sin, x1 * sin + x2 * cos], axis=1)


def _rotate_back(d, cos, sin):
    half = DK // 2
    d1, d2 = d[:, :half], d[:, half:]
    return jnp.concatenate([d1 * cos + d2 * sin, d2 * cos - d1 * sin], axis=1)


def _ret_fwd(proj, gain, tables, nb, wp_own):
    t = nb * S
    inner_t, cross_t, state_t, cos_t, sin_t = tables

    def body(q_ref, k_ref, v_ref, gb_ref, gain_ref, dm_ref, cd_ref, sd_ref, cos_ref, sin_ref, wp_ref,
             yb_ref, qr_ref, kr_ref, o_ref, rs_ref, wpg_ref, r_s, send_sems, recv_sems, own_sems):
        b, hd = pl.program_id(0), pl.program_id(1)
        own, first, arrive, forward, others = _gather_copies([wp_ref], [wpg_ref], send_sems, recv_sems, own_sems)

        @pl.when(jnp.logical_and(b == 0, hd == 0))
        def _():
            for cp in own + first:
                cp.start()

        @pl.when(jnp.logical_and(b == nb - 1, hd == HEADS - 1))
        def _():
            for came, on in zip(arrive, forward):
                came.wait_recv()
                on.start()

        r_s[...] = jnp.zeros_like(r_s)
        chunk_decay = cd_ref[CH - 1:CH, :]

        def chunk(c, carry):
            rows = _rows(c, CH)
            cos, sin = cos_ref[rows, :], sin_ref[rows, :]
            qr = _rotate(q_ref[rows, :].astype(F32), cos, sin).astype(BF16)
            kr = (_rotate(k_ref[rows, :].astype(F32), cos, sin) * (DK ** -0.5)).astype(BF16)
            vb = v_ref[rows, :]
            v = vb.astype(F32)
            qr_ref[rows, :] = qr
            kr_ref[rows, :] = kr
            r = r_s[...]
            rb = r.astype(BF16)
            rs_ref[c] = rb
            p = (_dot_nt(qr, kr) * dm_ref[...]).astype(BF16)
            o = _dot(p, vb) + _dot(qr, rb) * cd_ref[...]
            r_s[...] = chunk_decay * r + _dot_tn(kr, (v * sd_ref[...]).astype(BF16))
            o_ref[rows, :] = o.astype(BF16)
            oc = o - jnp.mean(o, axis=-1, keepdims=True)
            rstd = lax.rsqrt(jnp.mean(oc * oc, axis=-1, keepdims=True) + EPS)
            gb = gb_ref[rows, :].astype(F32)
            yb_ref[rows, :] = (gb * _sigmoid(gb) * (oc * rstd * gain_ref[...])).astype(BF16)
            return carry

        lax.fori_loop(0, NCH, chunk, 0, unroll=2)

        @pl.when(jnp.logical_and(b == nb - 1, hd == HEADS - 1))
        def _():
            for cp in others:
                cp.wait_recv()
            for cp in first + forward:
                cp.wait_send()
            for cp in own:
                cp.wait()

    seg = lambda s: pl.BlockSpec((None, S, DK), lambda b, h: (s, b, h))
    tab = pl.BlockSpec((None, CH, DK), lambda b, h: (h, 0, 0))
    rot = pl.BlockSpec((S, DK // 2), lambda b, h: (0, 0))
    blk = pl.BlockSpec((S, DK), lambda b, h: (b, h))
    return pl.pallas_call(
        body, name="ret_fwd", grid=(nb, HEADS),
        in_specs=[seg(2), seg(3), seg(4), seg(5),
                  pl.BlockSpec((None, 1, DK), lambda b, h: (h, 0, 0)),
                  tab, tab, tab, rot, rot, ANY],
        out_specs=[blk, blk, blk, blk,
                   pl.BlockSpec((None, None, NCH, DK, DK), lambda b, h: (b, h, 0, 0, 0)), ANY],
        out_shape=[jax.ShapeDtypeStruct((t, D), BF16), jax.ShapeDtypeStruct((t, D), BF16),
                   jax.ShapeDtypeStruct((t, D), BF16), jax.ShapeDtypeStruct((t, D), BF16),
                   jax.ShapeDtypeStruct((nb, HEADS, NCH, DK, DK), BF16),
                   jax.ShapeDtypeStruct((NDEV,) + wp_own.shape, wp_own.dtype)],
        scratch_shapes=[pltpu.VMEM((DK, DK), F32),
                        pltpu.SemaphoreType.DMA((1, 7)), pltpu.SemaphoreType.DMA((1, 7)), pltpu.SemaphoreType.DMA((1,))],
        compiler_params=_params(("arbitrary", "arbitrary")),
    )(proj, proj, proj, proj, gain, inner_t, cross_t, state_t, cos_t, sin_t, wp_own)


def _ret_bwd(proj, qr, kr, o, rs, dyb, gain, tables, nb, sums):
    t = nb * S
    ns = len(sums)
    inner_t, cross_t, state_t, cos_t, sin_t = tables

    def body(qr_ref, kr_ref, v_ref, gb_ref, o_ref, dyb_ref, rs_ref, gain_ref, dm_ref, cd_ref, sd_ref,
             cos_ref, sin_ref, *rest):
        sum_refs, rest = rest[:ns], rest[ns:]
        dp_ref, dgain_ref = rest[:2]
        part_refs, rest = rest[2:2 + ns], rest[2 + ns:]
        dr_s, send_sems, recv_sems, local_sems = rest
        mine, sends, recvs = _chip_copies(sum_refs, part_refs, send_sems, recv_sems, local_sems)

        @pl.when(jnp.logical_and(pl.program_id(0) == 0, pl.program_id(1) == 0))
        def _():
            for cp in mine + sends:
                cp.start()

        dr_s[...] = jnp.zeros_like(dr_s)
        chunk_decay = cd_ref[CH - 1:CH, :]

        @pl.when(pl.program_id(1) == 0)
        def _():
            dgain_ref[...] = jnp.zeros_like(dgain_ref)

        def chunk(i, carry):
            c = NCH - 1 - i
            rows = _rows(c, CH)
            gain_v = gain_ref[...]
            o_c = o_ref[rows, :].astype(F32)
            oc = o_c - jnp.mean(o_c, axis=-1, keepdims=True)
            rstd = lax.rsqrt(jnp.mean(oc * oc, axis=-1, keepdims=True) + EPS)
            yn = oc * rstd
            gb = gb_ref[rows, :].astype(F32)
            sg = _sigmoid(gb)
            dyb_c = dyb_ref[rows, :].astype(F32)
            dgn = dyb_c * (gb * sg)
            dp_ref[3, rows, :] = (dyb_c * (yn * gain_v) * (sg * (1.0 + gb * (1.0 - sg)))).astype(BF16)
            dgain_ref[...] += jnp.sum(dgn * yn, axis=0, keepdims=True)
            dyn = dgn * gain_v
            do = rstd * (dyn - jnp.mean(dyn, axis=-1, keepdims=True)
                         - yn * jnp.mean(dyn * yn, axis=-1, keepdims=True))
            dob = do.astype(BF16)
            dox = (do * cd_ref[...]).astype(BF16)

            q_c, k_c = qr_ref[rows, :], kr_ref[rows, :]
            vb = v_ref[rows, :]
            v = vb.astype(F32)
            vs = (v * sd_ref[...]).astype(BF16)
            rb = rs_ref[c]
            d_r = dr_s[...]
            drb = d_r.astype(BF16)
            dm = dm_ref[...]
            p = (_dot_nt(q_c, k_c) * dm).astype(BF16)
            dpm = (_dot_nt(dob, vb) * dm).astype(BF16)
            dq = _dot(dpm, k_c) + _dot_nt(dox, rb)
            dk = _dot_tn(dpm, q_c) + _dot_nt(vs, drb)
            dv = _dot_tn(p, dob) + _dot(k_c, drb) * sd_ref[...]
            dr_s[...] = chunk_decay * d_r + _dot_tn(q_c, dox)

            cos, sin = cos_ref[rows, :], sin_ref[rows, :]
            dp_ref[0, rows, :] = _rotate_back(dq, cos, sin).astype(BF16)
            dp_ref[1, rows, :] = (_rotate_back(dk, cos, sin) * (DK ** -0.5)).astype(BF16)
            dp_ref[2, rows, :] = dv.astype(BF16)
            return carry

        lax.fori_loop(0, NCH, chunk, 0, unroll=2)

        @pl.when(jnp.logical_and(pl.program_id(0) == HEADS - 1, pl.program_id(1) == nb - 1))
        def _():
            for cp in recvs:
                cp.wait_recv()
            for cp in sends:
                cp.wait_send()
            for cp in mine:
                cp.wait()

    seg = lambda s: pl.BlockSpec((None, S, DK), lambda h, b: (s, b, h))
    tab = pl.BlockSpec((None, CH, DK), lambda h, b: (h, 0, 0))
    rot = pl.BlockSpec((S, DK // 2), lambda h, b: (0, 0))
    blk = pl.BlockSpec((S, DK), lambda h, b: (b, h))
    one = pl.BlockSpec((None, 1, DK), lambda h, b: (h, 0, 0))
    return pl.pallas_call(
        body, name="ret_bwd", grid=(HEADS, nb),
        in_specs=[blk, blk, seg(4), seg(5), blk, blk,
                  pl.BlockSpec((None, None, NCH, DK, DK), lambda h, b: (b, h, 0, 0, 0)),
                  one, tab, tab, tab, rot, rot] + [ANY] * ns,
        out_specs=[pl.BlockSpec((4, S, DK), lambda h, b: (0, b, h)), one] + [ANY] * ns,
        out_shape=[jax.ShapeDtypeStruct((4, t, D), BF16), jax.ShapeDtypeStruct((HEADS, 1, DK), F32)]
        + [jax.ShapeDtypeStruct(a.shape, a.dtype) for a in sums],
        scratch_shapes=[pltpu.VMEM((DK, DK), F32), pltpu.SemaphoreType.DMA((ns, 3)), pltpu.SemaphoreType.DMA((ns, 3)),
                        pltpu.SemaphoreType.DMA((ns,))],
        compiler_params=_params(("arbitrary", "arbitrary")),
    )(qr, kr, proj, proj, o, dyb, rs, gain, inner_t, cross_t, state_t, cos_t, sin_t, *sums)


def _wblock(k):
    return pl.BlockSpec((NDEV, D // NDEV, D), lambda i: (0, k, 0))


def _tail(ya, yb, proj, x2d, tgt, wg, g_fin):
    t = x2d.shape[0]
    tm = 256

    def body(ya_ref, yb_ref, ma_ref, mb_ref, x_ref, t_ref, wa_ref, wb_ref, wo_ref, g_ref,
             dx2_ref, dya_ref, dyb_ref, dm_ref, mg_ref, doa_ref, dob_ref, gfin_ref, loss_ref):
        i = pl.program_id(0)

        @pl.when(i == 0)
        def _():
            gfin_ref[...] = jnp.zeros_like(gfin_ref)
            loss_ref[...] = jnp.zeros_like(loss_ref)

        wa = wa_ref[...].reshape(D, D)
        wb = wb_ref[...].reshape(D, D)
        wo = wo_ref[...].reshape(D, D)
        out_a = _dot(ya_ref[...], wa)
        out_b = _dot(yb_ref[...], wb)
        sa = _sigmoid(ma_ref[...].astype(F32))
        sb = _sigmoid(mb_ref[...].astype(F32))
        merged = (sa * out_a + sb * out_b).astype(BF16)
        mg_ref[...] = merged
        x2 = x_ref[...] + _dot(merged, wo)
        r2 = lax.rsqrt(jnp.mean(x2 * x2, axis=-1, keepdims=True) + EPS)
        xh = x2 * r2
        g = g_ref[...]
        err = xh * g - t_ref[...]
        loss_ref[...] += jnp.sum(err * err, axis=0, keepdims=True) * (0.5 / D)
        dy = err * (1.0 / D)
        gfin_ref[...] += jnp.sum(dy * xh, axis=0, keepdims=True)
        dxh = dy * g
        dx2 = r2 * (dxh - xh * jnp.mean(dxh * xh, axis=-1, keepdims=True))
        dx2_ref[...] = dx2
        dmerged = _dot_nt(dx2.astype(BF16), wo)
        doa = (sa * dmerged).astype(BF16)
        dob = (sb * dmerged).astype(BF16)
        doa_ref[...] = doa
        dob_ref[...] = dob
        dm_ref[0] = (dmerged * out_a * sa * (1.0 - sa)).astype(BF16)
        dm_ref[1] = (dmerged * out_b * sb * (1.0 - sb)).astype(BF16)
        dya_ref[...] = _dot_nt(doa, wa).astype(BF16)
        dyb_ref[...] = _dot_nt(dob, wb).astype(BF16)

    row = lambda: pl.BlockSpec((tm, D), lambda i: (i, 0))
    seg = lambda s: pl.BlockSpec((None, tm, D), lambda i: (s, i, 0))
    vec = pl.BlockSpec((1, D), lambda i: (0, 0))
    return pl.pallas_call(
        body, name="tail", grid=(t // tm,),
        in_specs=[row(), row(), seg(6), seg(7), row(), row(), _wblock(0), _wblock(1), _wblock(2), vec],
        out_specs=[row(), row(), row(), pl.BlockSpec((2, tm, D), lambda i: (0, i, 0)),
                   row(), row(), row(), vec, vec],
        out_shape=[jax.ShapeDtypeStruct((t, D), F32), jax.ShapeDtypeStruct((t, D), BF16),
                   jax.ShapeDtypeStruct((t, D), BF16), jax.ShapeDtypeStruct((2, t, D), BF16),
                   jax.ShapeDtypeStruct((t, D), BF16), jax.ShapeDtypeStruct((t, D), BF16),
                   jax.ShapeDtypeStruct((t, D), BF16), jax.ShapeDtypeStruct((1, D), F32),
                   jax.ShapeDtypeStruct((1, D), F32)],
        compiler_params=_params(("arbitrary",)),
    )(ya, yb, proj, proj, x2d, tgt, wg, wg, wg, g_fin)


def _tail_wgrad(ya, yb, merged, doa, dob, dx2):
    t = ya.shape[0]
    tm = 512

    def body(ya_ref, yb_ref, mg_ref, doa_ref, dob_ref, dx2_ref, ga_ref, gb_ref, go_ref):
        @pl.when(pl.program_id(0) == 0)
        def _():
            ga_ref[...] = jnp.zeros_like(ga_ref)
            gb_ref[...] = jnp.zeros_like(gb_ref)
            go_ref[...] = jnp.zeros_like(go_ref)

        ga_ref[...] += _dot_tn(ya_ref[...], doa_ref[...])
        gb_ref[...] += _dot_tn(yb_ref[...], dob_ref[...])
        go_ref[...] += _dot_tn(mg_ref[...], dx2_ref[...].astype(BF16))

    row = lambda: pl.BlockSpec((tm, D), lambda i: (i, 0))
    full = lambda: pl.BlockSpec((D, D), lambda i: (0, 0))
    return pl.pallas_call(
        body, name="tail_wgrad", grid=(t // tm,),
        in_specs=[row() for _ in range(6)], out_specs=[full(), full(), full()],
        out_shape=[jax.ShapeDtypeStruct((D, D), F32)] * 3,
        compiler_params=_params(("arbitrary",)),
    )(ya, yb, merged, doa, dob, dx2)


def _dproj_specs_ordered(tm):
    def spec(lo, n):
        def index(k, i, order_ref):
            seg = order_ref[k]
            mine = jnp.logical_and(seg >= lo, seg < lo + n)
            return jnp.where(mine, seg - lo, 0), jnp.where(mine, i, 0), 0
        return pl.BlockSpec((None, tm, D), index)
    return [spec(0, 2), spec(2, 4), spec(6, 2)]


def _dproj_pick(j, da_ref, db_ref, dc_ref, use):
    @pl.when(j < 2)
    def _():
        use(da_ref[...])

    @pl.when(jnp.logical_and(j >= 2, j < 6))
    def _():
        use(db_ref[...])

    @pl.when(j >= 6)
    def _():
        use(dc_ref[...])


RS_X, RS_Y, RS_XY = 0, 1, 2
RS_ROLES = ((RS_XY, RS_X, RS_Y), (RS_Y, RS_XY, RS_X))


def _rs_flip(rel, x, y):
    return ((1 - x, y), (x, 1 - y), (1 - x, 1 - y))[rel]


def _rs_order(x, y, c):
    order = []
    for s in range(4):
        chip = []
        for core in (0, 1):
            px, py = _rs_flip(RS_ROLES[core][s], x, y) if s < 3 else (x, y)
            chip.append(2 * px + py)
        keep = jnp.where(c == 0, chip[0], chip[1])
        give = jnp.where(c == 0, chip[1], chip[0])
        order += [2 * give + 1 - c, 2 * keep + c]
    return jnp.stack(order).astype(jnp.int32)


def _inproj_wgrad_rs(h, dpa, dpb, dpc, order, smalls):
    t = h.shape[0]
    tm = 1024
    nt = t // tm
    nsm = len(smalls)

    def body(order_ref, h_ref, da_ref, db_ref, dc_ref, *rest):
        small_refs, parts_ref, rest = rest[:nsm], rest[nsm], rest[nsm + 1:]
        all_refs, rest = rest[:nsm], rest[nsm:]
        (acc, sib, outb, far, give_send, give_recv, sum_send, sum_recv, far_send, far_recv, own_sem,
         small_send, small_recv, small_own) = rest
        k, i = pl.program_id(0), pl.program_id(1)
        x, y, c = _place()
        own, first, arrive, forward, others = _gather_copies(small_refs, all_refs, small_send, small_recv, small_own)

        @pl.when(jnp.logical_and(k == 0, i == 0))
        def _():
            for cp in own + first:
                cp.start()

        @pl.when(jnp.logical_and(k == 2, i == 0))
        def _():
            for came, on in zip(arrive, forward):
                came.wait_recv()
                on.start()

        def use(d):
            @pl.when(i == 0)
            def _():
                acc[k % 2] = _dot_tn(h_ref[...], d)

            @pl.when(i > 0)
            def _():
                acc[k % 2] += _dot_tn(h_ref[...], d)

        _dproj_pick(order_ref[k], da_ref, db_ref, dc_ref, use)

        def give_copy(s):
            return pltpu.make_async_remote_copy(
                src_ref=acc.at[0], dst_ref=sib.at[s % 2], send_sem=give_send.at[s], recv_sem=give_recv.at[s],
                device_id=(x, y, 1 - c), device_id_type=MESH)

        def sum_copy(s, core):
            slot = 0 if s < 2 else 1
            return pltpu.make_async_remote_copy(
                src_ref=outb.at[s], dst_ref=parts_ref.at[slot], send_sem=sum_send.at[slot], recv_sem=sum_recv.at[slot],
                device_id=(*_rs_flip(RS_ROLES[core][s], x, y), core), device_id_type=MESH)

        def far_copy(s, core):
            return pltpu.make_async_remote_copy(
                src_ref=outb.at[s], dst_ref=far, send_sem=far_send, recv_sem=far_recv,
                device_id=(*_rs_flip(RS_X if core == 0 else RS_Y, x, y), core), device_id_type=MESH)

        own_copy = pltpu.make_async_copy(outb.at[3], parts_ref.at[2], own_sem)

        def send_of(core, s):
            return far_copy(s, core) if RS_ROLES[core][s] == RS_XY else sum_copy(s, core)

        for s in range(4):
            @pl.when(jnp.logical_and(k == 2 * s, i == nt - 1))
            def _():
                give_copy(s).start()

            @pl.when(jnp.logical_and(k == 2 * s + 1, i == nt - 1))
            def _():
                give_copy(s).wait_recv()
                if s == 2:
                    far_copy(s, 0).wait_recv()
                    outb[s] = (acc[1] + sib[s % 2] + far[...].astype(F32)).astype(BF16)
                else:
                    outb[s] = (acc[1] + sib[s % 2]).astype(BF16)
                give_copy(s).wait_send()
                if s < 3:
                    for core in (0, 1):
                        @pl.when(c == core)
                        def _():
                            send_of(core, s).start()
                else:
                    own_copy.start()

        @pl.when(jnp.logical_and(k == NSEG - 1, i == nt - 1))
        def _():
            for slot in (0, 1):
                sum_copy(2 * slot, 0).wait_recv()
            for s in range(3):
                send_of(0, s).wait_send()
            own_copy.wait()
            for cp in others:
                cp.wait_recv()
            for cp in first + forward:
                cp.wait_send()
            for cp in own:
                cp.wait()

    return pl.pallas_call(
        body, name="inproj_wgrad_rs",
        grid_spec=pltpu.PrefetchScalarGridSpec(
            num_scalar_prefetch=1, grid=(NSEG, nt),
            in_specs=[pl.BlockSpec((tm, D), lambda k, i, order_ref: (i, 0))] + _dproj_specs_ordered(tm) + [ANY] * nsm,
            out_specs=[ANY] * (1 + nsm),
            scratch_shapes=[pltpu.VMEM((2, D, D), F32), pltpu.VMEM((2, D, D), F32), pltpu.VMEM((4, D, D), BF16),
                            pltpu.VMEM((D, D), BF16),
                            pltpu.SemaphoreType.DMA((4,)), pltpu.SemaphoreType.DMA((4,)),
                            pltpu.SemaphoreType.DMA((2,)), pltpu.SemaphoreType.DMA((2,)),
                            pltpu.SemaphoreType.DMA, pltpu.SemaphoreType.DMA, pltpu.SemaphoreType.DMA,
                            pltpu.SemaphoreType.DMA((nsm, 7)), pltpu.SemaphoreType.DMA((nsm, 7)),
                            pltpu.SemaphoreType.DMA((nsm,))]),
        out_shape=[jax.ShapeDtypeStruct((3, D, D), BF16)]
        + [jax.ShapeDtypeStruct((NDEV,) + a.shape, a.dtype) for a in smalls],
        compiler_params=_params(("arbitrary", "arbitrary")),
    )(order, h, dpa, dpb, dpc, *smalls)


def _inproj_dgrad(dpa, dpb, dpc, wg, x2d, dx2, g_in):
    t = x2d.shape[0]
    tm = 512

    def body(da_ref, db_ref, dc_ref, w_hbm, x_ref, dx2_ref, g_ref, gx_ref, gg_ref, w_s, w_sem):
        i = pl.program_id(0)

        def load(seg):
            return pltpu.make_async_copy(w_hbm.at[seg], w_s.at[seg], w_sem.at[seg])

        @pl.when(i == 0)
        def _():
            gg_ref[...] = jnp.zeros_like(gg_ref)
            for seg in range(NSEG):
                load(seg).start()

        dh = None
        for ref, lo in ((da_ref, 0), (db_ref, 2), (dc_ref, 6)):
            for k in range(ref.shape[0]):
                @pl.when(i == 0)
                def _():
                    load(lo + k).wait()

                part = _dot_nt(ref[k], w_s[lo + k])
                dh = part if dh is None else dh + part
        x = x_ref[...]
        r = lax.rsqrt(jnp.mean(x * x, axis=-1, keepdims=True) + EPS)
        xh = x * r
        gg_ref[...] += jnp.sum(dh * xh, axis=0, keepdims=True)
        dxh = dh * g_ref[...]
        gx_ref[...] = dx2_ref[...] + r * (dxh - xh * jnp.mean(dxh * xh, axis=-1, keepdims=True))

    row = lambda: pl.BlockSpec((tm, D), lambda i: (i, 0))
    seg = lambda n: pl.BlockSpec((n, tm, D), lambda i: (0, i, 0))
    vec = pl.BlockSpec((1, D), lambda i: (0, 0))
    return pl.pallas_call(
        body, name="inproj_dgrad", grid=(t // tm,),
        in_specs=[seg(2), seg(4), seg(2), ANY, row(), row(), vec],
        out_specs=[row(), vec],
        out_shape=[jax.ShapeDtypeStruct((t, D), F32), jax.ShapeDtypeStruct((1, D), F32)],
        scratch_shapes=[pltpu.VMEM((NSEG, D, D), BF16), pltpu.SemaphoreType.DMA((NSEG,))],
        compiler_params=_params(("arbitrary",)),
    )(dpa, dpb, dpc, wg, x2d, dx2, g_in)


def _adam_update(g, w, m, v):
    m_new = ADAM_B1 * m + (1.0 - ADAM_B1) * g
    v_new = ADAM_B2 * v + (1.0 - ADAM_B2) * (g * g)
    m_hat = m_new / (1.0 - ADAM_B1 ** ADAM_STEP)
    v_hat = v_new / (1.0 - ADAM_B2 ** ADAM_STEP)
    return -ADAM_LR * (m_hat / (jnp.sqrt(v_hat) + ADAM_EPS) + ADAM_WD * w), m_new, v_new


def _sum_in_order(ref):
    total = ref[0].astype(F32)
    for k in range(1, ref.shape[0]):
        total = total + ref[k].astype(F32)
    return total


def _sum_devices(arrs):
    def body(*refs):
        for a in range(len(arrs)):
            refs[len(arrs) + a][...] = _sum_in_order(refs[a])

    return pl.pallas_call(
        body, name="sum_devices",
        out_shape=[jax.ShapeDtypeStruct(a.shape[1:], F32) for a in arrs],
        compiler_params=_params(),
    )(*arrs)


def _adamw_small(me, vec_all, gx_all, ga_all, groups):
    flat = [a for grp in groups for a in grp]
    ng = len(groups)
    nshard = D // NDEV

    def body(me_ref, vec_ref, shard_ref, gx_ref, ga_ref, *refs):
        ins, outs = refs[:3 * ng], refs[3 * ng:]
        vec = _sum_in_order(vec_ref)
        shard = _sum_in_order(shard_ref)
        grads = [vec[r:r + 1, :] for r in range(6)]
        grads += [shard[0:4, :], shard[4:8, 0:DK // NDEV], _sum_in_order(gx_ref), _sum_in_order(ga_ref)]
        for n, g in enumerate(grads):
            delta, m_new, v_new = _adam_update(g, ins[3 * n][...], ins[3 * n + 1][...], ins[3 * n + 2][...])
            outs[4 * n][...] = g
            outs[4 * n + 1][...] = delta
            outs[4 * n + 2][...] = m_new
            outs[4 * n + 3][...] = v_new
        outs[4 * ng][...] = jnp.sum(vec[6:7, :], axis=1, keepdims=True)

    full = lambda a: pl.BlockSpec(a.shape, lambda i, me_ref, nd=len(a.shape): (0,) * nd)
    out_shape = [jax.ShapeDtypeStruct(w.shape, F32) for w, _, _ in groups for _ in range(4)]
    out_shape.append(jax.ShapeDtypeStruct((1, 1), F32))
    outs = pl.pallas_call(
        body, name="adamw_small",
        grid_spec=pltpu.PrefetchScalarGridSpec(
            num_scalar_prefetch=1, grid=(1,),
            in_specs=[full(vec_all),
                      pl.BlockSpec((NDEV, 8, nshard), lambda i, me_ref: (0, 1, me_ref[0])),
                      full(gx_all), full(ga_all)] + [full(a) for a in flat],
            out_specs=[full(s) for s in out_shape]),
        out_shape=out_shape,
        compiler_params=_params(("arbitrary",)),
    )(me, vec_all, vec_all, gx_all, ga_all, *flat)
    return [outs[4 * n:4 * n + 4] for n in range(ng)], outs[4 * ng]


def _adamw(name, items):
    n, rows, cols = items[0][0].shape
    tr = rows if rows <= 256 else 256
    k = len(items)

    def body(*refs):
        for a in range(k):
            p_ref, w_ref, m_ref, v_ref = refs[4 * a:4 * a + 4]
            g = _sum_in_order(p_ref)
            delta, m_new, v_new = _adam_update(g, w_ref[...], m_ref[...], v_ref[...])
            for o, val in zip(refs[4 * k + 4 * a:4 * k + 4 * a + 4], (g, delta, m_new, v_new)):
                o[...] = val

    blk = lambda: pl.BlockSpec((tr, cols), lambda i: (i, 0))
    outs = pl.pallas_call(
        body, name=name, grid=(rows // tr,),
        in_specs=[pl.BlockSpec((n, tr, cols), lambda i: (0, i, 0)), blk(), blk(), blk()] * k,
        out_specs=[blk() for _ in range(4 * k)],
        out_shape=[jax.ShapeDtypeStruct((rows, cols), F32)] * (4 * k),
        compiler_params=_params(("arbitrary",)),
    )(*[a for item in items for a in item])
    return [outs[4 * a:4 * a + 4] for a in range(k)]


ANY = pl.BlockSpec(memory_space=pl.ANY)


def _place():
    return lax.axis_index("x"), lax.axis_index("y"), lax.axis_index("c")


def _gather_copies(ins, outs, send_sems, recv_sems, own_sems):
    x, y, c = _place()
    me, sibling = (x, y, c), (x, y, 1 - c)
    chips = [(1 - x, y), (x, 1 - y), (1 - x, 1 - y)]
    n = len(ins)

    def copy(a, k, block, to, src=None):
        px, py, pc = block
        dst = outs[a].at[4 * px + 2 * py + pc]
        return pltpu.make_async_remote_copy(
            src_ref=dst if src is None else src, dst_ref=dst,
            send_sem=send_sems.at[a, k], recv_sem=recv_sems.at[a, k], device_id=to, device_id_type=MESH)

    own = [pltpu.make_async_copy(ins[a], outs[a].at[4 * x + 2 * y + c], own_sems.at[a]) for a in range(n)]
    first = []
    for a in range(n):
        first.append(copy(a, 0, me, sibling, src=ins[a]))
        first += [copy(a, 1 + j, me, (*chip, c), src=ins[a]) for j, chip in enumerate(chips)]
    arrive = [copy(a, 1 + j, (*chip, c), me) for j, chip in enumerate(chips) for a in range(n)]
    forward = [copy(a, 4 + j, (*chip, c), sibling) for j, chip in enumerate(chips) for a in range(n)]
    rest = [copy(a, 0, sibling, me) for a in range(n)]
    rest += [copy(a, 4 + j, (*chip, 1 - c), me) for a in range(n) for j, chip in enumerate(chips)]
    return own, first, arrive, forward, rest


def _sibling_copies(ins, outs, send_sems, recv_sems):
    x, y, c = _place()
    return [pltpu.make_async_remote_copy(
        src_ref=ins[a].at[2 * q + 1 - c], dst_ref=outs[a].at[q],
        send_sem=send_sems.at[a, q], recv_sem=recv_sems.at[a, q],
        device_id=(x, y, 1 - c), device_id_type=MESH) for a in range(len(ins)) for q in range(4)]


def _chip_copies(ins, outs, send_sems, recv_sems, local_sems):
    x, y, c = _place()
    my_chip = 2 * x + y
    chips = [(1 - x, y), (x, 1 - y), (1 - x, 1 - y)]
    n = len(ins)
    mine = [pltpu.make_async_copy(ins[a].at[my_chip], outs[a].at[my_chip], local_sems.at[a]) for a in range(n)]
    sends = [pltpu.make_async_remote_copy(
        src_ref=ins[a].at[2 * px + py], dst_ref=outs[a].at[my_chip],
        send_sem=send_sems.at[a, j], recv_sem=recv_sems.at[a, j],
        device_id=(px, py, c), device_id_type=MESH) for a in range(n) for j, (px, py) in enumerate(chips)]
    recvs = [pltpu.make_async_remote_copy(
        src_ref=ins[a].at[my_chip], dst_ref=outs[a].at[2 * px + py],
        send_sem=send_sems.at[a, j], recv_sem=recv_sems.at[a, j],
        device_id=(px, py, c), device_id_type=MESH) for a in range(n) for j, (px, py) in enumerate(chips)]
    return mine, sends, recvs


def _chip_sum(owns, gots, core):
    n = len(owns)
    _, rows, cols = owns[0].shape

    def body(core_ref, *refs):
        for a in range(n):
            refs[2 * n + a][...] = (refs[a][...] + refs[n + a][...]).astype(BF16)

    own_spec = pl.BlockSpec((None, rows, cols), lambda q, core_ref: (2 * q + core_ref[0], 0, 0))
    slab = pl.BlockSpec((None, rows, cols), lambda q, core_ref: (q, 0, 0))
    return pl.pallas_call(
        body, name="chip_sum",
        grid_spec=pltpu.PrefetchScalarGridSpec(
            num_scalar_prefetch=1, grid=(4,),
            in_specs=[own_spec] * n + [slab] * n, out_specs=[slab] * n),
        out_shape=[jax.ShapeDtypeStruct((4, rows, cols), BF16)] * n,
        compiler_params=_params(("arbitrary",)),
    )(core, *owns, *gots)


def _block_diag(w):
    w4 = w.reshape(NCB, 4, 64, 64)
    eye = jnp.eye(4, dtype=w.dtype)
    return (w4[:, :, :, None, :] * eye[None, :, None, :, None]).reshape(NCB, CB, CB)


def _block_diag_back(g):
    g5 = g.reshape(NCB, 4, 64, 4, 64)
    return jnp.stack([g5[:, m, :, m, :] for m in range(4)], axis=1).reshape(16, 64, 64)


def kernel(x, norm_in, w_in, conv_w, conv_b, gate_x_w, gate_x_b, gate_a_w, gate_a_b, lru_lambda, gn_gain, w_proj_a, w_proj_b, w_out, norm_final, loss_target, m_norm_in, m_w_in, m_conv_w, m_conv_b, m_gate_x_w, m_gate_x_b, m_gate_a_w, m_gate_a_b, m_lru_lambda, m_gn_gain, m_w_proj_a, m_w_proj_b, m_w_out, m_norm_final, v_norm_in, v_w_in, v_conv_w, v_conv_b, v_gate_x_w, v_gate_x_b, v_gate_a_w, v_gate_a_b, v_lru_lambda, v_gn_gain, v_w_proj_a, v_w_proj_b, v_w_out, v_norm_final):
    xi, yi, ci = _place()
    me = 4 * xi + 2 * yi + ci
    core = ci.astype(jnp.int32).reshape(1)
    nshard = D // NDEV
    nb = x.shape[0]
    t = nb * S
    x2d = x.reshape(t, D)
    tgt2d = loss_target.reshape(t, D)
    g_final = norm_final.reshape(1, D)
    wbd = jnp.concatenate([_block_diag(gate_x_w[0]), _block_diag(gate_a_w[0])], axis=-1).astype(BF16)
    tables = _retention_tables()

    wp_own = jnp.concatenate([w_proj_a[0], w_proj_b[0], w_out[0]], axis=0).astype(BF16)
    tiny = jnp.concatenate([conv_w[0], jnp.pad(gn_gain[0], ((0, 0), (0, nshard - DK // NDEV)))], axis=0)
    proj, h, wg, tiny_g = _inproj_gather(x2d, norm_in, w_in[0].astype(BF16), tiny, *_gather_order(xi, yi, ci))
    conv_w_full = tiny_g[:, 0:4, :].transpose(1, 0, 2).reshape(4, D)
    gain3 = tiny_g[:, 4:8, :DK // NDEV].transpose(1, 0, 2).reshape(HEADS, 1, DK)

    ya, hs, xc, gi, gr = _lru_fwd(proj, conv_w_full, conv_b, wbd, gate_x_b, gate_a_b, lru_lambda, nb)
    yb, qr, kr, o, rs, wpg = _ret_fwd(proj, gain3, tables, nb, wp_own)
    dx2, dya, dyb, dpc, merged, doa, dob, g_fin, loss_vec = _tail(ya, yb, proj, x2d, tgt2d, wpg, g_final)
    g_pa, g_pb, g_out = _tail_wgrad(ya, yb, merged, doa, dob, dx2)

    own = [g.reshape(NDEV, nshard, D) for g in (g_pa, g_pb, g_out)]
    dpa, g_wbd, g_vec, *got = _lru_bwd(proj, hs, xc, gi, gr, dya, conv_w_full, wbd, lru_lambda, nb, own)
    sums = _chip_sum(own, got, core)
    dpb, g_gain, *parts = _ret_bwd(proj, qr, kr, o, rs, dyb, gain3, tables, nb, sums)

    grad_x, g_norm_in = _inproj_dgrad(dpa, dpb, dpc, wg, x2d, dx2, norm_in)
    grad_x = grad_x.reshape(nb, S, D)

    gain_rows = jnp.pad(g_gain.reshape(HEADS, NDEV, DK // NDEV), ((0, 0), (0, 0), (0, nshard - DK // NDEV)))
    vec = jnp.concatenate([g_norm_in, g_vec[0:4], g_fin, loss_vec, jnp.zeros((1, D), F32), g_vec[4:8],
                           gain_rows.reshape(HEADS, D)], axis=0)
    g_gx = _block_diag_back(g_wbd[:, :, :CB]).reshape(D // 2, 128)
    g_ga = _block_diag_back(g_wbd[:, :, CB:]).reshape(D // 2, 128)
    parts_in, vec_all, gx_all, ga_all = _inproj_wgrad_rs(h, dpa, dpb, dpc, _rs_order(xi, yi, ci),
                                                         [vec, g_gx, g_ga])
    gx_all, ga_all = [g.reshape(1, D, 64) for g in _sum_devices([gx_all, ga_all])]
    parts = [parts_in] + list(parts)

    res = {}
    (out,) = _adamw("adamw_w_in", [(parts[0], w_in[0], m_w_in[0], v_w_in[0])])
    res["w_in"] = [o[None] for o in out]
    square = [("w_proj_a", w_proj_a, m_w_proj_a, v_w_proj_a), ("w_proj_b", w_proj_b, m_w_proj_b, v_w_proj_b),
              ("w_out", w_out, m_w_out, v_w_out)]
    outs = _adamw("adamw_square", [(parts[1 + k], w[0], m[0], v[0]) for k, (_, w, m, v) in enumerate(square)])
    for (nm, _, _, _), out in zip(square, outs):
        res[nm] = [o[None] for o in out]

    row = lambda a: a.reshape(1, D)
    gate = lambda a: a.reshape(D, 64)
    groups = [("norm_in", norm_in, m_norm_in, v_norm_in, row), ("conv_b", conv_b, m_conv_b, v_conv_b, row),
              ("gate_x_b", gate_x_b, m_gate_x_b, v_gate_x_b, row), ("gate_a_b", gate_a_b, m_gate_a_b, v_gate_a_b, row),
              ("lru_lambda", lru_lambda, m_lru_lambda, v_lru_lambda, row),
              ("norm_final", norm_final, m_norm_final, v_norm_final, row),
              ("conv_w", conv_w, m_conv_w, v_conv_w, lambda a: a[0]), ("gn_gain", gn_gain, m_gn_gain, v_gn_gain, lambda a: a[0]),
              ("gate_x_w", gate_x_w, m_gate_x_w, v_gate_x_w, gate), ("gate_a_w", gate_a_w, m_gate_a_w, v_gate_a_w, gate)]
    small_out, loss = _adamw_small(me.astype(jnp.int32).reshape(1), vec_all, gx_all, ga_all,
                                   [tuple(view(a) for a in (w, m, v)) for _, w, m, v, view in groups])
    for (nm, w, _, _, _), out in zip(groups, small_out):
        res[nm] = [o.reshape(w.shape) for o in out]
    loss = loss.reshape(())

    order = ["norm_in", "w_in", "conv_w", "conv_b", "gate_x_w", "gate_x_b", "gate_a_w", "gate_a_b", "lru_lambda",
             "gn_gain", "w_proj_a", "w_proj_b", "w_out", "norm_final"]
    outs = [loss, grad_x]
    for k in range(4):
        outs += [res[nm][k] for nm in order]
    return tuple(outs)
```
